```python
import jax, jax.numpy as jnp
from jax import lax
import numpy as np

D_MODEL = 1024
BATCH = 32
SEQ = 2048
DEPTH = 1

CHUNK = 64
LRU_WIDTH = 1280
LRU_HEADS = 10
LRU_HEAD_DIM = LRU_WIDTH // LRU_HEADS
CONV_WIDTH = 4
LRU_C = 8.0
SGU_WIDTH = 768
SGU_GROUPS = 6
SGU_GROUP_DIM = SGU_WIDTH // SGU_GROUPS
SGU_BLOCK = 128
D_FF = 4 * D_MODEL
N_BRANCH = 2
D_IN = 2 * LRU_WIDTH + 2 * SGU_WIDTH + N_BRANCH * D_MODEL
IN_SPLITS = (LRU_WIDTH, 2 * LRU_WIDTH, 2 * LRU_WIDTH + SGU_WIDTH,
             2 * LRU_WIDTH + 2 * SGU_WIDTH, 2 * LRU_WIDTH + 2 * SGU_WIDTH + D_MODEL)
ALPHA = (2.0 * DEPTH) ** 0.25
BETA = (8.0 * DEPTH) ** -0.25
LN_EPS = 1e-5

kernel_name = "hawk_gmlp_hybrid_deepnorm_adaln"


def _layer_norm(x, g, b):
    xf = x.astype(jnp.float32)
    mu = jnp.mean(xf, axis=-1, keepdims=True)
    var = jnp.mean(jnp.square(xf - mu), axis=-1, keepdims=True)
    y = (xf - mu) * lax.rsqrt(var + LN_EPS)
    return (y * g.astype(jnp.float32) + b.astype(jnp.float32)).astype(x.dtype)


def _causal_depthwise_conv(x, w, b):
    y = lax.conv_general_dilated(
        x, w[:, None, :].astype(x.dtype), window_strides=(1,),
        padding=[(CONV_WIDTH - 1, 0)], dimension_numbers=("NWC", "WIO", "NWC"),
        feature_group_count=x.shape[-1])
    return y + b


def _rg_lru(x, w_a, b_a, w_x, b_x, lam):
    B, S, _ = x.shape
    xh = x.reshape(B, S, LRU_HEADS, LRU_HEAD_DIM)
    r = jax.nn.sigmoid(jnp.einsum("bshi,hij->bshj", xh, w_a).reshape(B, S, LRU_WIDTH) + b_a)
    i = jax.nn.sigmoid(jnp.einsum("bshi,hij->bshj", xh, w_x).reshape(B, S, LRU_WIDTH) + b_x)
    log_a = (-LRU_C * jax.nn.softplus(-lam.astype(jnp.float32))) * r.astype(jnp.float32)
    a = jnp.exp(log_a)
    inp = jnp.sqrt(-jnp.expm1(2.0 * log_a)) * (i * x).astype(jnp.float32)

    def step(h, ab):
        a_t, b_t = ab
        h = a_t * h + b_t
        return h, h

    h0 = jnp.zeros((B, LRU_WIDTH), jnp.float32)
    _, hs = lax.scan(step, h0, (jnp.swapaxes(a, 0, 1), jnp.swapaxes(inp, 0, 1)))
    return jnp.swapaxes(hs, 0, 1).astype(x.dtype)


def _spatial_gating(u, v, w_sp, b_sp, ln_g, ln_b):
    B, S, _ = u.shape
    v = _layer_norm(v, ln_g, ln_b)
    nblk = S // SGU_BLOCK
    vb = v.reshape(B, nblk, SGU_BLOCK, SGU_GROUPS, SGU_GROUP_DIM)
    pos = jnp.arange(SGU_BLOCK)
    mask = (pos[None, :] // CHUNK) <= (pos[:, None] // CHUNK)
    w = jnp.where(mask[None], w_sp, 0.0).astype(v.dtype)
    mixed = jnp.einsum("gts,bnsgc->bntgc", w, vb) + jnp.transpose(b_sp)[None, None, :, :, None]
    return u * mixed.reshape(B, S, SGU_WIDTH)


def _fwd_setup_inputs(seed: int = 0) -> dict:
    key = jax.random.key(seed)
    ks = jax.random.split(key, 28)
    L = DEPTH
    nrm = lambda k, shape, s: jax.random.normal(k, shape, jnp.float32) * s
    u = jax.random.uniform(ks[12], (L, LRU_WIDTH), jnp.float32, 0.9, 0.999)
    a0 = u ** (1.0 / LRU_C)
    lru_lambda = jnp.log(a0) - jnp.log1p(-a0)
    return {
        "x": nrm(ks[0], (BATCH, SEQ, D_MODEL), 1.0),
        "c": nrm(ks[1], (BATCH, D_MODEL), 1.0),
        "w_ada": nrm(ks[2], (L, D_MODEL, 6 * D_MODEL), 0.1 * D_MODEL ** -0.5),
        "b_ada": nrm(ks[3], (L, 6 * D_MODEL), 0.01),
        "w_in": nrm(ks[4], (L, D_MODEL, D_IN), D_MODEL ** -0.5),
        "b_in": nrm(ks[5], (L, D_IN), 0.01),
        "w_conv": nrm(ks[6], (L, CONV_WIDTH, LRU_WIDTH), CONV_WIDTH ** -0.5),
        "b_conv": nrm(ks[7], (L, LRU_WIDTH), 0.01),
        "w_rg_a": nrm(ks[8], (L, LRU_HEADS, LRU_HEAD_DIM, LRU_HEAD_DIM), LRU_HEAD_DIM ** -0.5),
        "b_rg_a": nrm(ks[9], (L, LRU_WIDTH), 0.01),
        "w_rg_x": nrm(ks[10], (L, LRU_HEADS, LRU_HEAD_DIM, LRU_HEAD_DIM), LRU_HEAD_DIM ** -0.5),
        "b_rg_x": nrm(ks[11], (L, LRU_WIDTH), 0.01),
        "lru_lambda": lru_lambda,
        "w_sp": nrm(ks[13], (L, SGU_GROUPS, SGU_BLOCK, SGU_BLOCK), SGU_BLOCK ** -0.5),
        "b_sp": 1.0 + nrm(ks[14], (L, SGU_GROUPS, SGU_BLOCK), 0.01),
        "ln_v_g": 1.0 + nrm(ks[15], (L, SGU_WIDTH), 0.01),
        "ln_v_b": nrm(ks[16], (L, SGU_WIDTH), 0.01),
        "w_o_lru": nrm(ks[17], (L, LRU_WIDTH, D_MODEL), BETA * LRU_WIDTH ** -0.5),
        "w_o_sgu": nrm(ks[18], (L, SGU_WIDTH, D_MODEL), BETA * SGU_WIDTH ** -0.5),
        "w_out": nrm(ks[19], (L, D_MODEL, D_MODEL), BETA * D_MODEL ** -0.5),
        "ln1_g": 1.0 + nrm(ks[20], (L, D_MODEL), 0.01),
        "ln1_b": nrm(ks[21], (L, D_MODEL), 0.01),
        "w_up": nrm(ks[22], (L, D_MODEL, D_FF), BETA * D_MODEL ** -0.5),
        "w_down": nrm(ks[23], (L, D_FF, D_MODEL), BETA * D_FF ** -0.5),
        "ln2_g": 1.0 + nrm(ks[24], (L, D_MODEL), 0.01),
        "ln2_b": nrm(ks[25], (L, D_MODEL), 0.01),
    }


def _fwd_reference(x, c, w_ada, b_ada, w_in, b_in, w_conv, b_conv, w_rg_a, b_rg_a, w_rg_x, b_rg_x,
              lru_lambda, w_sp, b_sp, ln_v_g, ln_v_b, w_o_lru, w_o_sgu, w_out, ln1_g, ln1_b,
              w_up, w_down, ln2_g, ln2_b):
    c_act = jax.nn.silu(c)
    for l in range(DEPTH):
        mod = c_act @ w_ada[l] + b_ada[l]
        sh1, sc1, gt1, sh2, sc2, gt2 = jnp.split(mod, 6, axis=-1)

        h = x * (1.0 + sc1[:, None, :]) + sh1[:, None, :]
        proj = h @ w_in[l] + b_in[l]
        x_lru, g_lru, u, v, gate_a, gate_b = jnp.split(proj, IN_SPLITS, axis=-1)

        xc = _causal_depthwise_conv(x_lru, w_conv[l], b_conv[l])
        y_lru = _rg_lru(xc, w_rg_a[l], b_rg_a[l], w_rg_x[l], b_rg_x[l], lru_lambda[l])
        y_a = (y_lru * jax.nn.gelu(g_lru)) @ w_o_lru[l]

        y_sgu = _spatial_gating(jax.nn.gelu(u), jax.nn.gelu(v), w_sp[l], b_sp[l], ln_v_g[l], ln_v_b[l])
        y_b = y_sgu @ w_o_sgu[l]

        merged = jax.nn.sigmoid(gate_a) * y_a + jax.nn.sigmoid(gate_b) * y_b
        mix = merged @ w_out[l]
        x = _layer_norm(ALPHA * x + (1.0 + gt1[:, None, :]) * mix, ln1_g[l], ln1_b[l])

        h2 = x * (1.0 + sc2[:, None, :]) + sh2[:, None, :]
        f = jnp.square(jax.nn.relu(h2 @ w_up[l])) @ w_down[l]
        x = _layer_norm(ALPHA * x + (1.0 + gt2[:, None, :]) * f, ln2_g[l], ln2_b[l])
    return x


import jax as _jax
import jax.numpy as _jnp

TWIN_FORMAT = 'train_step'
FWD_PARAMS = ['x', 'c', 'w_ada', 'b_ada', 'w_in', 'b_in', 'w_conv', 'b_conv', 'w_rg_a', 'b_rg_a', 'w_rg_x', 'b_rg_x', 'lru_lambda', 'w_sp', 'b_sp', 'ln_v_g', 'ln_v_b', 'w_o_lru', 'w_o_sgu', 'w_out', 'ln1_g', 'ln1_b', 'w_up', 'w_down', 'ln2_g', 'ln2_b']
TWIN_WEIGHTS = ['w_ada', 'b_ada', 'w_in', 'b_in', 'w_conv', 'b_conv', 'w_rg_a', 'b_rg_a', 'w_rg_x', 'b_rg_x', 'lru_lambda', 'w_sp', 'b_sp', 'ln_v_g', 'ln_v_b', 'w_o_lru', 'w_o_sgu', 'w_out', 'ln1_g', 'ln1_b', 'w_up', 'w_down', 'ln2_g', 'ln2_b']
TWIN_DIFF_INPUT = 'x'
TWIN_INPUTS = ['x', 'c', 'w_ada', 'b_ada', 'w_in', 'b_in', 'w_conv', 'b_conv', 'w_rg_a', 'b_rg_a', 'w_rg_x', 'b_rg_x', 'lru_lambda', 'w_sp', 'b_sp', 'ln_v_g', 'ln_v_b', 'w_o_lru', 'w_o_sgu', 'w_out', 'ln1_g', 'ln1_b', 'w_up', 'w_down', 'ln2_g', 'ln2_b', 'loss_target', 'm_w_ada', 'm_b_ada', 'm_w_in', 'm_b_in', 'm_w_conv', 'm_b_conv', 'm_w_rg_a', 'm_b_rg_a', 'm_w_rg_x', 'm_b_rg_x', 'm_lru_lambda', 'm_w_sp', 'm_b_sp', 'm_ln_v_g', 'm_ln_v_b', 'm_w_o_lru', 'm_w_o_sgu', 'm_w_out', 'm_ln1_g', 'm_ln1_b', 'm_w_up', 'm_w_down', 'm_ln2_g', 'm_ln2_b', 'v_w_ada', 'v_b_ada', 'v_w_in', 'v_b_in', 'v_w_conv', 'v_b_conv', 'v_w_rg_a', 'v_b_rg_a', 'v_w_rg_x', 'v_b_rg_x', 'v_lru_lambda', 'v_w_sp', 'v_b_sp', 'v_ln_v_g', 'v_ln_v_b', 'v_w_o_lru', 'v_w_o_sgu', 'v_w_out', 'v_ln1_g', 'v_ln1_b', 'v_w_up', 'v_w_down', 'v_ln2_g', 'v_ln2_b']
TWIN_OUTPUTS = ['loss', 'grad_x', 'grad_w_ada', 'grad_b_ada', 'grad_w_in', 'grad_b_in', 'grad_w_conv', 'grad_b_conv', 'grad_w_rg_a', 'grad_b_rg_a', 'grad_w_rg_x', 'grad_b_rg_x', 'grad_lru_lambda', 'grad_w_sp', 'grad_b_sp', 'grad_ln_v_g', 'grad_ln_v_b', 'grad_w_o_lru', 'grad_w_o_sgu', 'grad_w_out', 'grad_ln1_g', 'grad_ln1_b', 'grad_w_up', 'grad_w_down', 'grad_ln2_g', 'grad_ln2_b', 'delta_w_ada', 'delta_b_ada', 'delta_w_in', 'delta_b_in', 'delta_w_conv', 'delta_b_conv', 'delta_w_rg_a', 'delta_b_rg_a', 'delta_w_rg_x', 'delta_b_rg_x', 'delta_lru_lambda', 'delta_w_sp', 'delta_b_sp', 'delta_ln_v_g', 'delta_ln_v_b', 'delta_w_o_lru', 'delta_w_o_sgu', 'delta_w_out', 'delta_ln1_g', 'delta_ln1_b', 'delta_w_up', 'delta_w_down', 'delta_ln2_g', 'delta_ln2_b', 'new_m_w_ada', 'new_m_b_ada', 'new_m_w_in', 'new_m_b_in', 'new_m_w_conv', 'new_m_b_conv', 'new_m_w_rg_a', 'new_m_b_rg_a', 'new_m_w_rg_x', 'new_m_b_rg_x', 'new_m_lru_lambda', 'new_m_w_sp', 'new_m_b_sp', 'new_m_ln_v_g', 'new_m_ln_v_b', 'new_m_w_o_lru', 'new_m_w_o_sgu', 'new_m_w_out', 'new_m_ln1_g', 'new_m_ln1_b', 'new_m_w_up', 'new_m_w_down', 'new_m_ln2_g', 'new_m_ln2_b', 'new_v_w_ada', 'new_v_b_ada', 'new_v_w_in', 'new_v_b_in', 'new_v_w_conv', 'new_v_b_conv', 'new_v_w_rg_a', 'new_v_b_rg_a', 'new_v_w_rg_x', 'new_v_b_rg_x', 'new_v_lru_lambda', 'new_v_w_sp', 'new_v_b_sp', 'new_v_ln_v_g', 'new_v_ln_v_b', 'new_v_w_o_lru', 'new_v_w_o_sgu', 'new_v_w_out', 'new_v_ln1_g', 'new_v_ln1_b', 'new_v_w_up', 'new_v_w_down', 'new_v_ln2_g', 'new_v_ln2_b']
TWIN_LEAF_KINDS = {'loss': 'loss', 'grad_x': 'grad_x', 'grad_w_ada': 'grad_w', 'grad_b_ada': 'grad_w', 'grad_w_in': 'grad_w', 'grad_b_in': 'grad_w', 'grad_w_conv': 'grad_w', 'grad_b_conv': 'grad_w', 'grad_w_rg_a': 'grad_w', 'grad_b_rg_a': 'grad_w', 'grad_w_rg_x': 'grad_w', 'grad_b_rg_x': 'grad_w', 'grad_lru_lambda': 'grad_w', 'grad_w_sp': 'grad_w', 'grad_b_sp': 'grad_w', 'grad_ln_v_g': 'grad_w', 'grad_ln_v_b': 'grad_w', 'grad_w_o_lru': 'grad_w', 'grad_w_o_sgu': 'grad_w', 'grad_w_out': 'grad_w', 'grad_ln1_g': 'grad_w', 'grad_ln1_b': 'grad_w', 'grad_w_up': 'grad_w', 'grad_w_down': 'grad_w', 'grad_ln2_g': 'grad_w', 'grad_ln2_b': 'grad_w', 'delta_w_ada': 'delta_w', 'delta_b_ada': 'delta_w', 'delta_w_in': 'delta_w', 'delta_b_in': 'delta_w', 'delta_w_conv': 'delta_w', 'delta_b_conv': 'delta_w', 'delta_w_rg_a': 'delta_w', 'delta_b_rg_a': 'delta_w', 'delta_w_rg_x': 'delta_w', 'delta_b_rg_x': 'delta_w', 'delta_lru_lambda': 'delta_w', 'delta_w_sp': 'delta_w', 'delta_b_sp': 'delta_w', 'delta_ln_v_g': 'delta_w', 'delta_ln_v_b': 'delta_w', 'delta_w_o_lru': 'delta_w', 'delta_w_o_sgu': 'delta_w', 'delta_w_out': 'delta_w', 'delta_ln1_g': 'delta_w', 'delta_ln1_b': 'delta_w', 'delta_w_up': 'delta_w', 'delta_w_down': 'delta_w', 'delta_ln2_g': 'delta_w', 'delta_ln2_b': 'delta_w', 'new_m_w_ada': 'new_m', 'new_m_b_ada': 'new_m', 'new_m_w_in': 'new_m', 'new_m_b_in': 'new_m', 'new_m_w_conv': 'new_m', 'new_m_b_conv': 'new_m', 'new_m_w_rg_a': 'new_m', 'new_m_b_rg_a': 'new_m', 'new_m_w_rg_x': 'new_m', 'new_m_b_rg_x': 'new_m', 'new_m_lru_lambda': 'new_m', 'new_m_w_sp': 'new_m', 'new_m_b_sp': 'new_m', 'new_m_ln_v_g': 'new_m', 'new_m_ln_v_b': 'new_m', 'new_m_w_o_lru': 'new_m', 'new_m_w_o_sgu': 'new_m', 'new_m_w_out': 'new_m', 'new_m_ln1_g': 'new_m', 'new_m_ln1_b': 'new_m', 'new_m_w_up': 'new_m', 'new_m_w_down': 'new_m', 'new_m_ln2_g': 'new_m', 'new_m_ln2_b': 'new_m', 'new_v_w_ada': 'new_v', 'new_v_b_ada': 'new_v', 'new_v_w_in': 'new_v', 'new_v_b_in': 'new_v', 'new_v_w_conv': 'new_v', 'new_v_b_conv': 'new_v', 'new_v_w_rg_a': 'new_v', 'new_v_b_rg_a': 'new_v', 'new_v_w_rg_x': 'new_v', 'new_v_b_rg_x': 'new_v', 'new_v_lru_lambda': 'new_v', 'new_v_w_sp': 'new_v', 'new_v_b_sp': 'new_v', 'new_v_ln_v_g': 'new_v', 'new_v_ln_v_b': 'new_v', 'new_v_w_o_lru': 'new_v', 'new_v_w_o_sgu': 'new_v', 'new_v_w_out': 'new_v', 'new_v_ln1_g': 'new_v', 'new_v_ln1_b': 'new_v', 'new_v_w_up': 'new_v', 'new_v_w_down': 'new_v', 'new_v_ln2_g': 'new_v', 'new_v_ln2_b': 'new_v'}


def _forward(args):
    return _fwd_reference(*[args[k] for k in FWD_PARAMS])


def _output_shape():
    out = _jax.eval_shape(lambda: _forward(_fwd_setup_inputs(0)))
    return out.shape, out.dtype

N_MICROBATCH = 1
ADAM_LR = 0.001
ADAM_B1 = 0.9
ADAM_B2 = 0.999
ADAM_EPS = 1e-08
ADAM_WD = 0.01
ADAM_STEP = 10
PER_EXAMPLE_BATCH_AXIS = {'x': 0, 'c': 0, 'loss_target': 0}
SHARED_INPUTS = []
_WEIGHT_DTYPES = {'w_ada': _jnp.float32, 'b_ada': _jnp.float32, 'w_in': _jnp.float32, 'b_in': _jnp.float32, 'w_conv': _jnp.float32, 'b_conv': _jnp.float32, 'w_rg_a': _jnp.float32, 'b_rg_a': _jnp.float32, 'w_rg_x': _jnp.float32, 'b_rg_x': _jnp.float32, 'lru_lambda': _jnp.float32, 'w_sp': _jnp.float32, 'b_sp': _jnp.float32, 'ln_v_g': _jnp.float32, 'ln_v_b': _jnp.float32, 'w_o_lru': _jnp.float32, 'w_o_sgu': _jnp.float32, 'w_out': _jnp.float32, 'ln1_g': _jnp.float32, 'ln1_b': _jnp.float32, 'w_up': _jnp.float32, 'w_down': _jnp.float32, 'ln2_g': _jnp.float32, 'ln2_b': _jnp.float32}
MOMENT_SCALE = {'w_ada': 6.768457e-02, 'b_ada': 1.090066e-01, 'w_in': 2.325321e-02, 'b_in': 1.043458e-01, 'w_conv': 2.195885e-02, 'b_conv': 2.589453e-01, 'w_rg_a': 6.858607e-03, 'b_rg_a': 6.373493e-03, 'w_rg_x': 1.247842e-02, 'b_rg_x': 7.575734e-03, 'lru_lambda': 1.279577e-02, 'w_sp': 2.645303e-02, 'b_sp': 2.993023e-02, 'ln_v_g': 2.660803e-02, 'ln_v_b': 2.646429e-02, 'w_o_lru': 4.137531e-02, 'w_o_sgu': 6.205596e-02, 'w_out': 7.255458e-02, 'ln1_g': 7.145836e-01, 'ln1_b': 3.753901e-01, 'w_up': 5.189274e-02, 'w_down': 9.763153e-02, 'ln2_g': 6.392498e+01, 'ln2_b': 5.833491e+00}


def _to_microbatches(a, axis):
    t = _jnp.moveaxis(a, axis, 0)
    t = t.reshape((N_MICROBATCH, t.shape[0] // N_MICROBATCH) + t.shape[1:])
    return _jnp.moveaxis(t, 1, axis + 1)


def setup_inputs(seed: int = 0) -> dict:
    inp = _fwd_setup_inputs(seed)
    key = _jax.random.fold_in(_jax.random.key(seed), 7919)
    shape, _ = _output_shape()
    out = dict(inp)
    out["loss_target"] = _jax.random.normal(_jax.random.fold_in(key, 0), shape, _jnp.float32)
    for i, name in enumerate(TWIN_WEIGHTS):
        w = inp[name].astype(_jnp.float32)
        if MOMENT_SCALE is None:
            s = _jnp.sqrt(_jnp.mean(_jnp.square(w)) + 1e-30)
        else:
            s = MOMENT_SCALE[name]
        km, kv = _jax.random.split(_jax.random.fold_in(key, i + 1))
        out[name] = w
        out["m_" + name] = s * _jax.random.normal(km, w.shape, _jnp.float32)
        out["v_" + name] = (s * s) * _jax.random.uniform(kv, w.shape, _jnp.float32, 0.5, 1.5)
    if N_MICROBATCH > 1:
        for name, axis in PER_EXAMPLE_BATCH_AXIS.items():
            out[name] = _to_microbatches(out[name], axis)
    return {'x': out['x'], 'c': out['c'], 'w_ada': out['w_ada'], 'b_ada': out['b_ada'], 'w_in': out['w_in'], 'b_in': out['b_in'], 'w_conv': out['w_conv'], 'b_conv': out['b_conv'], 'w_rg_a': out['w_rg_a'], 'b_rg_a': out['b_rg_a'], 'w_rg_x': out['w_rg_x'], 'b_rg_x': out['b_rg_x'], 'lru_lambda': out['lru_lambda'], 'w_sp': out['w_sp'], 'b_sp': out['b_sp'], 'ln_v_g': out['ln_v_g'], 'ln_v_b': out['ln_v_b'], 'w_o_lru': out['w_o_lru'], 'w_o_sgu': out['w_o_sgu'], 'w_out': out['w_out'], 'ln1_g': out['ln1_g'], 'ln1_b': out['ln1_b'], 'w_up': out['w_up'], 'w_down': out['w_down'], 'ln2_g': out['ln2_g'], 'ln2_b': out['ln2_b'], 'loss_target': out['loss_target'], 'm_w_ada': out['m_w_ada'], 'm_b_ada': out['m_b_ada'], 'm_w_in': out['m_w_in'], 'm_b_in': out['m_b_in'], 'm_w_conv': out['m_w_conv'], 'm_b_conv': out['m_b_conv'], 'm_w_rg_a': out['m_w_rg_a'], 'm_b_rg_a': out['m_b_rg_a'], 'm_w_rg_x': out['m_w_rg_x'], 'm_b_rg_x': out['m_b_rg_x'], 'm_lru_lambda': out['m_lru_lambda'], 'm_w_sp': out['m_w_sp'], 'm_b_sp': out['m_b_sp'], 'm_ln_v_g': out['m_ln_v_g'], 'm_ln_v_b': out['m_ln_v_b'], 'm_w_o_lru': out['m_w_o_lru'], 'm_w_o_sgu': out['m_w_o_sgu'], 'm_w_out': out['m_w_out'], 'm_ln1_g': out['m_ln1_g'], 'm_ln1_b': out['m_ln1_b'], 'm_w_up': out['m_w_up'], 'm_w_down': out['m_w_down'], 'm_ln2_g': out['m_ln2_g'], 'm_ln2_b': out['m_ln2_b'], 'v_w_ada': out['v_w_ada'], 'v_b_ada': out['v_b_ada'], 'v_w_in': out['v_w_in'], 'v_b_in': out['v_b_in'], 'v_w_conv': out['v_w_conv'], 'v_b_conv': out['v_b_conv'], 'v_w_rg_a': out['v_w_rg_a'], 'v_b_rg_a': out['v_b_rg_a'], 'v_w_rg_x': out['v_w_rg_x'], 'v_b_rg_x': out['v_b_rg_x'], 'v_lru_lambda': out['v_lru_lambda'], 'v_w_sp': out['v_w_sp'], 'v_b_sp': out['v_b_sp'], 'v_ln_v_g': out['v_ln_v_g'], 'v_ln_v_b': out['v_ln_v_b'], 'v_w_o_lru': out['v_w_o_lru'], 'v_w_o_sgu': out['v_w_o_sgu'], 'v_w_out': out['v_w_out'], 'v_ln1_g': out['v_ln1_g'], 'v_ln1_b': out['v_ln1_b'], 'v_w_up': out['v_w_up'], 'v_w_down': out['v_w_down'], 'v_ln2_g': out['v_ln2_g'], 'v_ln2_b': out['v_ln2_b']}


def _loss(weights, diff, rest, loss_target):
    with _jax.named_scope("forward"):
        args = {**rest, TWIN_DIFF_INPUT: diff, **{k: w.astype(_WEIGHT_DTYPES[k]) for k, w in weights.items()}}
        y = _forward(args)
    with _jax.named_scope("loss_head"):
        err = _jnp.square(y.astype(_jnp.float32) - loss_target)
        return 0.5 * _jnp.sum(_jnp.mean(err, axis=-1)) if err.ndim else 0.5 * err


def _adamw(w, g, m, v):
    m = ADAM_B1 * m + (1.0 - ADAM_B1) * g
    v = ADAM_B2 * v + (1.0 - ADAM_B2) * _jnp.square(g)
    m_hat = m / (1.0 - ADAM_B1 ** ADAM_STEP)
    v_hat = v / (1.0 - ADAM_B2 ** ADAM_STEP)
    delta = -ADAM_LR * (m_hat / (_jnp.sqrt(v_hat) + ADAM_EPS) + ADAM_WD * w)
    return delta, m, v


def reference(x, c, w_ada, b_ada, w_in, b_in, w_conv, b_conv, w_rg_a, b_rg_a, w_rg_x, b_rg_x, lru_lambda, w_sp, b_sp, ln_v_g, ln_v_b, w_o_lru, w_o_sgu, w_out, ln1_g, ln1_b, w_up, w_down, ln2_g, ln2_b, loss_target, m_w_ada, m_b_ada, m_w_in, m_b_in, m_w_conv, m_b_conv, m_w_rg_a, m_b_rg_a, m_w_rg_x, m_b_rg_x, m_lru_lambda, m_w_sp, m_b_sp, m_ln_v_g, m_ln_v_b, m_w_o_lru, m_w_o_sgu, m_w_out, m_ln1_g, m_ln1_b, m_w_up, m_w_down, m_ln2_g, m_ln2_b, v_w_ada, v_b_ada, v_w_in, v_b_in, v_w_conv, v_b_conv, v_w_rg_a, v_b_rg_a, v_w_rg_x, v_b_rg_x, v_lru_lambda, v_w_sp, v_b_sp, v_ln_v_g, v_ln_v_b, v_w_o_lru, v_w_o_sgu, v_w_out, v_ln1_g, v_ln1_b, v_w_up, v_w_down, v_ln2_g, v_ln2_b):
    given = dict(x=x, c=c, w_ada=w_ada, b_ada=b_ada, w_in=w_in, b_in=b_in, w_conv=w_conv, b_conv=b_conv, w_rg_a=w_rg_a, b_rg_a=b_rg_a, w_rg_x=w_rg_x, b_rg_x=b_rg_x, lru_lambda=lru_lambda, w_sp=w_sp, b_sp=b_sp, ln_v_g=ln_v_g, ln_v_b=ln_v_b, w_o_lru=w_o_lru, w_o_sgu=w_o_sgu, w_out=w_out, ln1_g=ln1_g, ln1_b=ln1_b, w_up=w_up, w_down=w_down, ln2_g=ln2_g, ln2_b=ln2_b, loss_target=loss_target, m_w_ada=m_w_ada, m_b_ada=m_b_ada, m_w_in=m_w_in, m_b_in=m_b_in, m_w_conv=m_w_conv, m_b_conv=m_b_conv, m_w_rg_a=m_w_rg_a, m_b_rg_a=m_b_rg_a, m_w_rg_x=m_w_rg_x, m_b_rg_x=m_b_rg_x, m_lru_lambda=m_lru_lambda, m_w_sp=m_w_sp, m_b_sp=m_b_sp, m_ln_v_g=m_ln_v_g, m_ln_v_b=m_ln_v_b, m_w_o_lru=m_w_o_lru, m_w_o_sgu=m_w_o_sgu, m_w_out=m_w_out, m_ln1_g=m_ln1_g, m_ln1_b=m_ln1_b, m_w_up=m_w_up, m_w_down=m_w_down, m_ln2_g=m_ln2_g, m_ln2_b=m_ln2_b, v_w_ada=v_w_ada, v_b_ada=v_b_ada, v_w_in=v_w_in, v_b_in=v_b_in, v_w_conv=v_w_conv, v_b_conv=v_b_conv, v_w_rg_a=v_w_rg_a, v_b_rg_a=v_b_rg_a, v_w_rg_x=v_w_rg_x, v_b_rg_x=v_b_rg_x, v_lru_lambda=v_lru_lambda, v_w_sp=v_w_sp, v_b_sp=v_b_sp, v_ln_v_g=v_ln_v_g, v_ln_v_b=v_ln_v_b, v_w_o_lru=v_w_o_lru, v_w_o_sgu=v_w_o_sgu, v_w_out=v_w_out, v_ln1_g=v_ln1_g, v_ln1_b=v_ln1_b, v_w_up=v_w_up, v_w_down=v_w_down, v_ln2_g=v_ln2_g, v_ln2_b=v_ln2_b)
    weights = {n: given[n] for n in TWIN_WEIGHTS}
    shared = {n: given[n] for n in SHARED_INPUTS}
    per_example = {n: given[n] for n in ['x', 'c']}
    grad_fn = _jax.value_and_grad(_loss, argnums=(0, 1))

    def one_microbatch(ex, loss_target):
        ex = dict(ex)
        diff = ex.pop(TWIN_DIFF_INPUT)
        return grad_fn(weights, diff, {**shared, **ex}, loss_target)

    if N_MICROBATCH == 1:
        loss, (grad_w, grad_x) = one_microbatch(per_example, given["loss_target"])
    else:
        def body(carry, xs):
            loss_sum, grad_sum = carry
            l_k, (gw_k, gx_k) = one_microbatch(xs[0], xs[1])
            with _jax.named_scope("update"):
                return (loss_sum + l_k, _jax.tree.map(_jnp.add, grad_sum, gw_k)), gx_k

        init = (_jnp.zeros((), _jnp.float32), _jax.tree.map(_jnp.zeros_like, weights))
        (loss, grad_w), grad_x = _jax.lax.scan(body, init, (per_example, given["loss_target"]))
    with _jax.named_scope("update"):
        delta_w, new_m, new_v = {}, {}, {}
        for n in TWIN_WEIGHTS:
            delta_w[n], new_m[n], new_v[n] = _adamw(weights[n], grad_w[n], given["m_" + n], given["v_" + n])
    return (loss, grad_x, *[grad_w[n] for n in TWIN_WEIGHTS], *[delta_w[n] for n in TWIN_WEIGHTS],
            *[new_m[n] for n in TWIN_WEIGHTS], *[new_v[n] for n in TWIN_WEIGHTS])
```

```python
import functools
import math

import jax
import jax.numpy as jnp
from jax import lax
from jax.experimental import pallas as pl
from jax.experimental.pallas import tpu as pltpu

F32 = jnp.float32
BF16 = jnp.bfloat16
MESH = pl.DeviceIdType.MESH

N_CHIPS = 4
N_DEV = 8
LRU_HEADS = 10
HEAD = 128
SGU_GROUPS = 6
SGU_CHUNK = 64
CONV_WIDTH = 4
LRU_C = 8.0
ALPHA = 2.0 ** 0.25
LN_EPS = 1e-5
ADAM_LR, ADAM_B1, ADAM_B2, ADAM_EPS, ADAM_WD, ADAM_STEP = 0.001, 0.9, 0.999, 1e-08, 0.01, 10

VMEM_LIMIT = 56 * 1024 * 1024
TM_PROJ = 1024
TM_MIX = 256
TM_MLP = 512
TM_SGU = 512
TM_DH = 512
TT_DW = 1024
TC_SCAN = 256
TR_EW = 256


def _cp(sem=None):
    return pltpu.CompilerParams(dimension_semantics=sem, vmem_limit_bytes=VMEM_LIMIT)


def _mm(a, b):
    return jnp.dot(a.astype(BF16), b.astype(BF16), preferred_element_type=F32)


def _mm_nt(a, b):
    return lax.dot_general(a.astype(BF16), b.astype(BF16), (((1,), (1,)), ((), ())), preferred_element_type=F32)


def _mm_tn(a, b):
    return lax.dot_general(a.astype(BF16), b.astype(BF16), (((0,), (0,)), ((), ())), preferred_element_type=F32)


def _sigmoid(x):
    return 1.0 / (1.0 + jnp.exp(-x))


_GELU_K = math.sqrt(2.0 / math.pi)


def _gelu(x):
    t = jnp.tanh(_GELU_K * (x + 0.044715 * (x * x * x)))
    return 0.5 * x * (1.0 + t)


def _gelu_and_grad(x):
    x2 = x * x
    t = jnp.tanh(_GELU_K * (x + 0.044715 * (x2 * x)))
    g = 0.5 * x * (1.0 + t)
    dg = 0.5 * (1.0 + t) + 0.5 * x * (1.0 - t * t) * (_GELU_K * (1.0 + 3.0 * 0.044715 * x2))
    return g, dg


def _ln_stats(z):
    mu = jnp.mean(z, axis=-1, keepdims=True)
    zc = z - mu
    var = jnp.mean(zc * zc, axis=-1, keepdims=True)
    rstd = lax.rsqrt(var + LN_EPS)
    return zc * rstd, rstd


def _ln_bwd(dxh, xhat, rstd):
    m1 = jnp.mean(dxh, axis=-1, keepdims=True)
    m2 = jnp.mean(dxh * xhat, axis=-1, keepdims=True)
    return rstd * (dxh - m1 - xhat * m2)


def _colsum(v):
    return jnp.sum(v, axis=0, keepdims=True)


def _shift_down(v, j):
    if j == 0:
        return v
    rows = lax.broadcasted_iota(jnp.int32, v.shape, 0)
    return jnp.where(rows >= j, pltpu.roll(v, j, 0), 0.0)


def _shift_up(v, j):
    if j == 0:
        return v
    n = v.shape[0]
    rows = lax.broadcasted_iota(jnp.int32, v.shape, 0)
    return jnp.where(rows < n - j, pltpu.roll(v, n - j, 0), 0.0)


def _my_pos():
    return lax.axis_index("x"), lax.axis_index("y"), lax.axis_index("c")


def _all_gather_small(v, name):
    m_per, n = v.shape

    def body(x_ref, out_ref, send_sems, recv_sems, local_sem):
        x, y, c = _my_pos()
        me, sibling = (x, y, c), (x, y, 1 - c)
        chips = [(1 - x, y), (x, 1 - y), (1 - x, 1 - y)]

        def rows(px, py, pc):
            return out_ref.at[pl.ds((4 * px + 2 * py + pc) * m_per, m_per), :]

        def copy(k, block, to, src=None):
            return pltpu.make_async_remote_copy(
                src_ref=rows(*block) if src is None else src, dst_ref=rows(*block),
                send_sem=send_sems.at[k], recv_sem=recv_sems.at[k], device_id=to, device_id_type=MESH)

        mine = pltpu.make_async_copy(x_ref, rows(*me), local_sem)
        mine.start()
        first = [copy(0, me, sibling, src=x_ref)]
        first += [copy(1 + j, me, (*chip, c), src=x_ref) for j, chip in enumerate(chips)]
        for cp in first:
            cp.start()
        passed = [copy(4 + j, (*chip, c), sibling) for j, chip in enumerate(chips)]
        for j, chip in enumerate(chips):
            copy(1 + j, (*chip, c), me).wait_recv()
            passed[j].start()
        copy(0, sibling, me).wait_recv()
        for j, chip in enumerate(chips):
            copy(4 + j, (*chip, 1 - c), me).wait_recv()
        for cp in first + passed:
            cp.wait_send()
        mine.wait()

    return pl.pallas_call(
        body, name=name,
        out_shape=jax.ShapeDtypeStruct((N_DEV * m_per, n), v.dtype),
        in_specs=[pl.BlockSpec(memory_space=pltpu.VMEM)],
        out_specs=pl.BlockSpec(memory_space=pltpu.VMEM),
        scratch_shapes=[pltpu.SemaphoreType.DMA((7,)), pltpu.SemaphoreType.DMA((7,)), pltpu.SemaphoreType.DMA],
        compiler_params=pltpu.CompilerParams(vmem_limit_bytes=VMEM_LIMIT),
    )(v)


def _gather_weights(shards):
    n = len(shards)

    def body(*refs):
        ins, outs = refs[:n], refs[n:2 * n]
        send_sems, recv_sems, local_sems = refs[2 * n:]
        x, y, c = _my_pos()
        p = 2 * x + y
        peers = [(x, 1 - y), (1 - x, y), (1 - x, 1 - y)]
        local = [pltpu.make_async_copy(ins[k], outs[k].at[p], local_sems.at[k]) for k in range(n)]
        for cp in local:
            cp.start()
        remote = []
        for k in range(n):
            for j, (qx, qy) in enumerate(peers):
                remote.append(pltpu.make_async_remote_copy(
                    src_ref=ins[k], dst_ref=outs[k].at[p], send_sem=send_sems.at[k, j], recv_sem=recv_sems.at[k, j],
                    device_id=(qx, qy, c), device_id_type=MESH))
        for cp in remote:
            cp.start()
        for k in range(n):
            for j, (qx, qy) in enumerate(peers):
                pltpu.make_async_remote_copy(
                    src_ref=ins[k], dst_ref=outs[k].at[2 * qx + qy], send_sem=send_sems.at[k, j],
                    recv_sem=recv_sems.at[k, j], device_id=(qx, qy, c), device_id_type=MESH).wait_recv()
        for cp in remote:
            cp.wait_send()
        for cp in local:
            cp.wait()

    any_spec = pl.BlockSpec(memory_space=pl.ANY)
    return pl.pallas_call(
        body, name="gather_weights",
        out_shape=[jax.ShapeDtypeStruct((N_CHIPS,) + s.shape, s.dtype) for s in shards],
        in_specs=[any_spec] * n, out_specs=[any_spec] * n,
        scratch_shapes=[pltpu.SemaphoreType.DMA((n, 3)), pltpu.SemaphoreType.DMA((n, 3)), pltpu.SemaphoreType.DMA((n,))],
    )(*shards)


def _sibling_swap_halves(grads):
    n = len(grads)

    def body(*refs):
        ins, outs = refs[:n], refs[n:2 * n]
        send_sems, recv_sems = refs[2 * n:]
        x, y, c = _my_pos()
        cps = [pltpu.make_async_remote_copy(
            src_ref=ins[k].at[:, 1 - c], dst_ref=outs[k], send_sem=send_sems.at[k], recv_sem=recv_sems.at[k],
            device_id=(x, y, 1 - c), device_id_type=MESH) for k in range(n)]
        for cp in cps:
            cp.start()
        for cp in cps:
            cp.wait()

    any_spec = pl.BlockSpec(memory_space=pl.ANY)
    return pl.pallas_call(
        body, name="grads_to_sibling",
        out_shape=[jax.ShapeDtypeStruct((g.shape[0],) + g.shape[2:], g.dtype) for g in grads],
        in_specs=[any_spec] * n, out_specs=[any_spec] * n,
        scratch_shapes=[pltpu.SemaphoreType.DMA((n,)), pltpu.SemaphoreType.DMA((n,))],
    )(*grads)


def _chip_exchange(parts):
    n = len(parts)

    def body(*refs):
        ins, outs = refs[:n], refs[n:2 * n]
        send_sems, recv_sems, local_sems = refs[2 * n:]
        x, y, c = _my_pos()
        p = 2 * x + y
        peers = [(x, 1 - y), (1 - x, y), (1 - x, 1 - y)]
        local = [pltpu.make_async_copy(ins[k].at[p], outs[k].at[p], local_sems.at[k]) for k in range(n)]
        for cp in local:
            cp.start()
        remote = []
        for k in range(n):
            for j, (qx, qy) in enumerate(peers):
                remote.append(pltpu.make_async_remote_copy(
                    src_ref=ins[k].at[2 * qx + qy], dst_ref=outs[k].at[p], send_sem=send_sems.at[k, j],
                    recv_sem=recv_sems.at[k, j], device_id=(qx, qy, c), device_id_type=MESH))
        for cp in remote:
            cp.start()
        for k in range(n):
            for j, (qx, qy) in enumerate(peers):
                pltpu.make_async_remote_copy(
                    src_ref=ins[k].at[p], dst_ref=outs[k].at[2 * qx + qy], send_sem=send_sems.at[k, j],
                    recv_sem=recv_sems.at[k, j], device_id=(qx, qy, c), device_id_type=MESH).wait_recv()
        for cp in remote:
            cp.wait_send()
        for cp in local:
            cp.wait()

    any_spec = pl.BlockSpec(memory_space=pl.ANY)
    return pl.pallas_call(
        body, name="grads_chip_exchange",
        out_shape=[jax.ShapeDtypeStruct(g.shape, g.dtype) for g in parts],
        in_specs=[any_spec] * n, out_specs=[any_spec] * n,
        scratch_shapes=[pltpu.SemaphoreType.DMA((n, 3)), pltpu.SemaphoreType.DMA((n, 3)), pltpu.SemaphoreType.DMA((n,))],
    )(*parts)


def _sibling_join_halves(halves):
    n = len(halves)

    def body(*refs):
        ins, outs = refs[:n], refs[n:2 * n]
        send_sems, recv_sems, local_sems = refs[2 * n:]
        x, y, c = _my_pos()
        local = [pltpu.make_async_copy(ins[k], outs[k].at[c], local_sems.at[k]) for k in range(n)]
        for cp in local:
            cp.start()
        cps = [pltpu.make_async_remote_copy(
            src_ref=ins[k], dst_ref=outs[k].at[c], send_sem=send_sems.at[k], recv_sem=recv_sems.at[k],
            device_id=(x, y, 1 - c), device_id_type=MESH) for k in range(n)]
        for cp in cps:
            cp.start()
        for k in range(n):
            pltpu.make_async_remote_copy(
                src_ref=ins[k], dst_ref=outs[k].at[1 - c], send_sem=send_sems.at[k], recv_sem=recv_sems.at[k],
                device_id=(x, y, 1 - c), device_id_type=MESH).wait_recv()
        for cp in cps:
            cp.wait_send()
        for cp in local:
            cp.wait()

    any_spec = pl.BlockSpec(memory_space=pl.ANY)
    return pl.pallas_call(
        body, name="grads_join_halves",
        out_shape=[jax.ShapeDtypeStruct((2,) + h.shape, h.dtype) for h in halves],
        in_specs=[any_spec] * n, out_specs=[any_spec] * n,
        scratch_shapes=[pltpu.SemaphoreType.DMA((n,)), pltpu.SemaphoreType.DMA((n,)), pltpu.SemaphoreType.DMA((n,))],
    )(*halves)


def _row_tile(r):
    t = min(TR_EW, r)
    while r % t:
        t //= 2
    return t


def _add_own_half(g4, recv, cidx, name):
    _, _, rh, cdim = g4.shape
    tr = _row_tile(rh)

    def body(c_ref, a_ref, b_ref, o_ref):
        o_ref[...] = a_ref[...] + b_ref[...]

    return pl.pallas_call(
        body, name=name,
        grid_spec=pltpu.PrefetchScalarGridSpec(
            num_scalar_prefetch=1, grid=(N_CHIPS, rh // tr),
            in_specs=[pl.BlockSpec((None, None, tr, cdim), lambda q, i, c: (q, c[0], i, 0)),
                      pl.BlockSpec((None, tr, cdim), lambda q, i, c: (q, i, 0))],
            out_specs=pl.BlockSpec((None, tr, cdim), lambda q, i, c: (q, i, 0))),
        out_shape=jax.ShapeDtypeStruct(recv.shape, F32),
        compiler_params=_cp(("arbitrary", "arbitrary")),
    )(cidx, g4, recv)


def _sum_slots(v, name):
    n, r, cdim = v.shape
    tr = _row_tile(r)

    def body(v_ref, o_ref):
        acc = v_ref[0]
        for k in range(1, n):
            acc = acc + v_ref[k]
        o_ref[...] = acc

    return pl.pallas_call(
        body, name=name, grid=(r // tr,),
        in_specs=[pl.BlockSpec((n, tr, cdim), lambda i: (0, i, 0))],
        out_specs=pl.BlockSpec((tr, cdim), lambda i: (i, 0)),
        out_shape=jax.ShapeDtypeStruct((r, cdim), F32),
        compiler_params=_cp(("arbitrary",)),
    )(v)


def _adamw(w, g, m, v, name):
    r, cdim = w.shape
    tr = _row_tile(r) if r % 8 == 0 else r

    def body(w_ref, g_ref, m_ref, v_ref, d_ref, nm_ref, nv_ref):
        gg = g_ref[...]
        nm = ADAM_B1 * m_ref[...] + (1.0 - ADAM_B1) * gg
        nv = ADAM_B2 * v_ref[...] + (1.0 - ADAM_B2) * (gg * gg)
        m_hat = nm / (1.0 - ADAM_B1 ** ADAM_STEP)
        v_hat = nv / (1.0 - ADAM_B2 ** ADAM_STEP)
        d_ref[...] = -ADAM_LR * (m_hat / (jnp.sqrt(v_hat) + ADAM_EPS) + ADAM_WD * w_ref[...])
        nm_ref[...] = nm
        nv_ref[...] = nv

    spec = pl.BlockSpec((tr, cdim), lambda i: (i, 0))
    return pl.pallas_call(
        body, name=name, grid=(r // tr,), in_specs=[spec] * 4, out_specs=[spec] * 3,
        out_shape=[jax.ShapeDtypeStruct((r, cdim), F32)] * 3,
        compiler_params=_cp(("arbitrary",)),
    )(w, g, m, v)


def _ada_fwd(c_all, w_ada, b_cols):
    nb, _ = c_all.shape
    n = w_ada.shape[1]

    def body(c_ref, w_ref, b_ref, o_ref):
        cv = c_ref[...]
        o_ref[...] = _mm(cv * _sigmoid(cv), w_ref[...]) + b_ref[...]

    return pl.pallas_call(
        body, name="ada_fwd", out_shape=jax.ShapeDtypeStruct((nb, n), F32),
        compiler_params=pltpu.CompilerParams(vmem_limit_bytes=VMEM_LIMIT),
    )(c_all, w_ada, b_cols)


def _ada_bwd(c_all, dmod_all, dmod_cols):
    d = c_all.shape[1]
    n = dmod_cols.shape[1]

    def body(c_ref, da_ref, dc_ref, gw_ref, gb_ref):
        cv = c_ref[...]
        gw_ref[...] = _mm_tn(cv * _sigmoid(cv), dc_ref[...])
        gb_ref[...] = _colsum(da_ref[...])

    return pl.pallas_call(
        body, name="ada_bwd",
        out_shape=[jax.ShapeDtypeStruct((d, n), F32), jax.ShapeDtypeStruct((1, dmod_all.shape[1]), F32)],
        compiler_params=pltpu.CompilerParams(vmem_limit_bytes=VMEM_LIMIT),
    )(c_all, dmod_all, dmod_cols)


def _proj_fwd(x2, modv, w_in_g, b_in, seq):
    t, d = x2.shape
    nq, _, ns = w_in_g.shape
    tm = min(TM_PROJ, seq)
    tpb = seq // tm

    def body(x_ref, mod_ref, w_ref, b_ref, proj_ref, h_ref, h_s):
        @pl.when(pl.program_id(1) == 0)
        def _():
            h = x_ref[...] * (1.0 + mod_ref[1:2, :]) + mod_ref[0:1, :]
            h_s[...] = h.astype(BF16)
            h_ref[...] = h.astype(BF16)

        proj_ref[...] = jnp.dot(h_s[...], w_ref[...], preferred_element_type=F32) + b_ref[...]

    return pl.pallas_call(
        body, name="proj_fwd", grid=(t // tm, nq),
        in_specs=[pl.BlockSpec((tm, d), lambda i, q: (i, 0)),
                  pl.BlockSpec((None, 8, d), lambda i, q: (i // tpb, 0, 0)),
                  pl.BlockSpec((None, d, ns), lambda i, q: (q, 0, 0)),
                  pl.BlockSpec((1, ns), lambda i, q: (0, q))],
        out_specs=[pl.BlockSpec((tm, ns), lambda i, q: (i, q)),
                   pl.BlockSpec((tm, d), lambda i, q: (i, 0))],
        out_shape=[jax.ShapeDtypeStruct((t, nq * ns), F32), jax.ShapeDtypeStruct((t, d), BF16)],
        scratch_shapes=[pltpu.VMEM((tm, d), BF16)],
        compiler_params=_cp(("arbitrary", "arbitrary")),
    )(x2, modv, w_in_g, b_in)


def _lru_gates(xl, wc_ref, bc_ref, wa_ref, ba_ref, wx_ref, bx_ref, lam_ref):
    xc = bc_ref[...] + wc_ref[CONV_WIDTH - 1:CONV_WIDTH, :] * xl
    for k in range(CONV_WIDTH - 1):
        xc = xc + wc_ref[k:k + 1, :] * _shift_down(xl, CONV_WIDTH - 1 - k)
    r = _sigmoid(_mm(xc, wa_ref[...]) + ba_ref[...])
    gi = _sigmoid(_mm(xc, wx_ref[...]) + bx_ref[...])
    nl = -lam_ref[...]
    e = jnp.exp(-jnp.abs(nl))
    u = 1.0 + e
    dlt = u - 1.0
    log1p_e = jnp.where(dlt == 0.0, e, jnp.log(u) * (e / jnp.where(dlt == 0.0, 1.0, dlt)))
    big_l = -LRU_C * (jnp.maximum(nl, 0.0) + log1p_e)
    la = big_l * r
    a = jnp.exp(la)
    mult = jnp.sqrt(jnp.tanh(-la) * (a * a + 1.0))
    return xc, r, gi, big_l, a, mult


def _lru_prep(proj, lru_w, nb, seq):
    t = proj.shape[0]
    w = LRU_HEADS * HEAD
    w_conv, b_conv, w_a, b_a, w_x, b_x, lam = lru_w

    def body(x_ref, wc_ref, bc_ref, wa_ref, ba_ref, wx_ref, bx_ref, lam_ref, a_ref, inp_ref):
        xc, r, gi, big_l, a, mult = _lru_gates(x_ref[...], wc_ref, bc_ref, wa_ref, ba_ref, wx_ref, bx_ref, lam_ref)
        a_ref[...] = a
        inp_ref[...] = mult * (gi * xc)

    col = lambda b, hd: (0, hd)
    head = lambda b, hd: (hd, 0, 0)
    tok = lambda b, hd: (b, hd)
    return pl.pallas_call(
        body, name="lru_prep", grid=(nb, LRU_HEADS),
        in_specs=[pl.BlockSpec((seq, HEAD), tok),
                  pl.BlockSpec((CONV_WIDTH, HEAD), col), pl.BlockSpec((1, HEAD), col),
                  pl.BlockSpec((None, HEAD, HEAD), head), pl.BlockSpec((1, HEAD), col),
                  pl.BlockSpec((None, HEAD, HEAD), head), pl.BlockSpec((1, HEAD), col),
                  pl.BlockSpec((1, HEAD), col)],
        out_specs=[pl.BlockSpec((seq, HEAD), tok)] * 2,
        out_shape=[jax.ShapeDtypeStruct((t, w), F32)] * 2,
        compiler_params=_cp(("arbitrary", "arbitrary")),
    )(proj, w_conv, b_conv, w_a, b_a, w_x, b_x, lam)


def _scan(a3, b3, reverse, name):
    nb, seq, w = a3.shape
    tc = min(TC_SCAN, seq)
    nchunk = seq // tc
    ntile = tc // 8

    def combine(av, bv):
        rows = lax.broadcasted_iota(jnp.int32, av.shape, 0)
        for s in (1, 2, 4):
            if reverse:
                keep = rows < 8 - s
                a_sh, b_sh = pltpu.roll(av, 8 - s, 0), pltpu.roll(bv, 8 - s, 0)
            else:
                keep = rows >= s
                a_sh, b_sh = pltpu.roll(av, s, 0), pltpu.roll(bv, s, 0)
            bv = jnp.where(keep, bv + av * b_sh, bv)
            av = jnp.where(keep, av * a_sh, av)
        return av, bv

    def body(a_ref, b_ref, h_ref, carry):
        @pl.when(pl.program_id(0) == 0)
        def _():
            carry[...] = jnp.zeros_like(carry)

        for b in range(nb):
            def tile(j, hprev):
                jj = ntile - 1 - j if reverse else j
                base = pl.multiple_of(jj * 8, 8)
                av, bv = a_ref[b, pl.ds(base, 8), :], b_ref[b, pl.ds(base, 8), :]
                av, bv = combine(av, av * bv if reverse else bv)
                h = bv + av * hprev
                h_ref[b, pl.ds(base, 8), :] = h
                edge = h[0:1, :] if reverse else h[7:8, :]
                return jnp.broadcast_to(edge, (8, w))

            carry[b] = lax.fori_loop(0, ntile, tile, carry[b])

    imap = (lambda i: (0, nchunk - 1 - i, 0)) if reverse else (lambda i: (0, i, 0))
    spec = pl.BlockSpec((nb, tc, w), imap)
    return pl.pallas_call(
        body, name=name, grid=(nchunk,), in_specs=[spec, spec], out_specs=spec,
        out_shape=jax.ShapeDtypeStruct((nb, seq, w), F32),
        scratch_shapes=[pltpu.VMEM((nb, 8, w), F32)],
        compiler_params=_cp(("arbitrary",)),
    )(a3, b3)


def _sgu_mask():
    ti = lax.broadcasted_iota(jnp.int32, (HEAD, HEAD), 0) // SGU_CHUNK
    si = lax.broadcasted_iota(jnp.int32, (HEAD, HEAD), 1) // SGU_CHUNK
    return si <= ti


def _sgu_specs(tm, d_sgu):
    pw = 256
    first_u = (2 * LRU_HEADS * HEAD) // pw
    n_piece = d_sgu // pw
    specs = [pl.BlockSpec((tm, pw), functools.partial(lambda i, k: (i, k), k=first_u + j)) for j in range(2 * n_piece)]
    return specs, n_piece


def _sgu_fwd(proj, w_sp, b_sp_t, ln_g, ln_b):
    t = proj.shape[0]
    d_sgu = SGU_GROUPS * HEAD
    tm = min(TM_SGU, t)
    nblk = tm // HEAD
    specs, n_piece = _sgu_specs(tm, d_sgu)

    def body(*refs):
        u = jnp.concatenate([r[...] for r in refs[:n_piece]], axis=1)
        v = jnp.concatenate([r[...] for r in refs[n_piece:2 * n_piece]], axis=1)
        w_ref, bt_ref, g_ref, b_ref, y_ref = refs[2 * n_piece:]
        ug = _gelu(u)
        xhat, _ = _ln_stats(_gelu(v))
        vn = (xhat * g_ref[...] + b_ref[...]).astype(BF16)
        mask = _sgu_mask()
        for g in range(SGU_GROUPS):
            wm = jnp.where(mask, w_ref[g], 0.0).astype(BF16)
            cols = slice(g * HEAD, (g + 1) * HEAD)
            for n in range(nblk):
                rows = slice(n * HEAD, (n + 1) * HEAD)
                mixed = jnp.dot(wm, vn[rows, cols], preferred_element_type=F32) + bt_ref[:, g:g + 1]
                y_ref[rows, cols] = (ug[rows, cols] * mixed).astype(BF16)

    full = lambda shape: pl.BlockSpec(shape, lambda i: (0,) * len(shape))
    return pl.pallas_call(
        body, name="sgu_fwd", grid=(t // tm,),
        in_specs=specs + [full(w_sp.shape), full(b_sp_t.shape), full(ln_g.shape), full(ln_b.shape)],
        out_specs=pl.BlockSpec((tm, d_sgu), lambda i: (i, 0)),
        out_shape=jax.ShapeDtypeStruct((t, d_sgu), BF16),
        compiler_params=_cp(("arbitrary",)),
    )(*([proj] * (2 * n_piece)), w_sp, b_sp_t, ln_g, ln_b)


def _mix_fwd(hs, proj, y_sgu, x2, modv, w_o_lru_g, w_o_sgu_g, w_out_g, ln1_g, ln1_b, seq):
    t, d = x2.shape
    w = hs.shape[1]
    d_sgu = y_sgu.shape[1]
    nq, _, ns = w_o_sgu_g.shape
    tm = min(TM_MIX, seq)
    tpb = seq // tm

    def body(hs_ref, gl_ref, ys_ref, ga_ref, gb_ref, x_ref, mod_ref, wl_ref, ws_ref, wo_ref, g1_ref, b1_ref,
             yap_ref, ya_ref, yb_ref, mg_ref, mix_ref, x1_ref):
        yap = (hs_ref[...] * _gelu(gl_ref[...])).astype(BF16)
        yap_ref[...] = yap
        y_a = jnp.dot(yap, wl_ref[...], preferred_element_type=F32)
        ys = ys_ref[...]
        y_b = jnp.concatenate([jnp.dot(ys, ws_ref[q], preferred_element_type=F32) for q in range(nq)], axis=1)
        ya_ref[...] = y_a.astype(BF16)
        yb_ref[...] = y_b.astype(BF16)
        merged = (_sigmoid(ga_ref[...]) * y_a + _sigmoid(gb_ref[...]) * y_b).astype(BF16)
        mg_ref[...] = merged
        mix = jnp.dot(merged, wo_ref[...], preferred_element_type=F32)
        mix_ref[...] = mix
        xhat, _ = _ln_stats(ALPHA * x_ref[...] + (1.0 + mod_ref[2:3, :]) * mix)
        x1_ref[...] = xhat * g1_ref[...] + b1_ref[...]

    row = lambda width, col: pl.BlockSpec((tm, width), functools.partial(lambda i, k: (i, k), k=col))
    full = lambda shape: pl.BlockSpec(shape, lambda i: (0,) * len(shape))
    return pl.pallas_call(
        body, name="mix_fwd", grid=(t // tm,),
        in_specs=[row(w, 0), row(w, 1), row(d_sgu, 0), row(d, 4), row(d, 5), row(d, 0),
                  pl.BlockSpec((None, 8, d), lambda i: (i // tpb, 0, 0)),
                  full(w_o_lru_g.shape), full(w_o_sgu_g.shape), full(w_out_g.shape), full(ln1_g.shape), full(ln1_b.shape)],
        out_specs=[row(w, 0), row(d, 0), row(d, 0), row(d, 0), row(d, 0), row(d, 0)],
        out_shape=[jax.ShapeDtypeStruct((t, w), BF16), jax.ShapeDtypeStruct((t, d), BF16),
                   jax.ShapeDtypeStruct((t, d), BF16), jax.ShapeDtypeStruct((t, d), BF16),
                   jax.ShapeDtypeStruct((t, d), F32), jax.ShapeDtypeStruct((t, d), F32)],
        compiler_params=_cp(("arbitrary",)),
    )(hs, proj, y_sgu, proj, proj, x2, modv, w_o_lru_g, w_o_sgu_g, w_out_g, ln1_g, ln1_b)


def _mlp_fwd(x1, modv, w_up_g, w_down_g, ln2_g, ln2_b, target, nb, seq):
    t, d = x1.shape
    nq, _, ns = w_up_g.shape
    tm = min(TM_MLP, seq)
    tpb = seq // tm
    nt = t // tm

    def body(x1_ref, mod_ref, wu_ref, wd_ref, g2_ref, b2_ref, tg_ref,
             up_ref, h2_ref, dz2_ref, df_ref, st_ref, pb_ref, h2_s, acc):
        i, j = pl.program_id(0), pl.program_id(1)

        @pl.when(j == 0)
        def _():
            h2 = (x1_ref[...] * (1.0 + mod_ref[4:5, :]) + mod_ref[3:4, :]).astype(BF16)
            h2_s[...] = h2
            h2_ref[...] = h2
            acc[...] = jnp.zeros_like(acc)

        @pl.when((i == 0) & (j == 0))
        def _():
            st_ref[...] = jnp.zeros_like(st_ref)

        @pl.when((i % tpb == 0) & (j == 0))
        def _():
            pb_ref[...] = jnp.zeros_like(pb_ref)

        up = jnp.dot(h2_s[...], wu_ref[...], preferred_element_type=F32)
        up_ref[...] = up
        r = jnp.maximum(up, 0.0)
        acc[...] += jnp.dot((r * r).astype(BF16), wd_ref[...], preferred_element_type=F32)

        @pl.when(j == nq - 1)
        def _():
            f = acc[...]
            xhat, rstd = _ln_stats(ALPHA * x1_ref[...] + (1.0 + mod_ref[5:6, :]) * f)
            y = xhat * g2_ref[...] + b2_ref[...]
            err = y - tg_ref[...]
            dy = err * (1.0 / d)
            dz2 = _ln_bwd(dy * g2_ref[...], xhat, rstd)
            dz2_ref[...] = dz2
            df_ref[...] = ((1.0 + mod_ref[5:6, :]) * dz2).astype(BF16)
            st_ref[0:1, :] += _colsum(dy * xhat)
            st_ref[1:2, :] += _colsum(dy)
            st_ref[2:3, :] += (0.5 / d) * jnp.sum(_colsum(err * err), axis=1, keepdims=True)
            pb_ref[0:1, :] += _colsum(dz2 * f)

    tok = lambda i, j: (i, 0)
    return pl.pallas_call(
        body, name="mlp_fwd", grid=(nt, nq),
        in_specs=[pl.BlockSpec((tm, d), tok), pl.BlockSpec((None, 8, d), lambda i, j: (i // tpb, 0, 0)),
                  pl.BlockSpec((None, d, ns), lambda i, j: (j, 0, 0)), pl.BlockSpec((ns, d), lambda i, j: (j, 0)),
                  pl.BlockSpec((1, d), lambda i, j: (0, 0)), pl.BlockSpec((1, d), lambda i, j: (0, 0)),
                  pl.BlockSpec((tm, d), tok)],
        out_specs=[pl.BlockSpec((tm, ns), lambda i, j: (i, j)), pl.BlockSpec((tm, d), tok),
                   pl.BlockSpec((tm, d), tok), pl.BlockSpec((tm, d), tok),
                   pl.BlockSpec((8, d), lambda i, j: (0, 0)), pl.BlockSpec((None, 8, d), lambda i, j: (i // tpb, 0, 0))],
        out_shape=[jax.ShapeDtypeStruct((t, nq * ns), F32), jax.ShapeDtypeStruct((t, d), BF16),
                   jax.ShapeDtypeStruct((t, d), F32), jax.ShapeDtypeStruct((t, d), BF16),
                   jax.ShapeDtypeStruct((8, d), F32), jax.ShapeDtypeStruct((nb, 8, d), F32)],
        scratch_shapes=[pltpu.VMEM((tm, d), BF16), pltpu.VMEM((tm, d), F32)],
        compiler_params=_cp(("arbitrary", "arbitrary")),
    )(x1, modv, w_up_g, w_down_g, ln2_g, ln2_b, target)


def _mlp_bwd(df, up, w_down_g, w_up_g, dz2, x2, mix, modv, ln1_g, ln1_b, nb, seq):
    t, d = x2.shape
    nq, _, ns = w_up_g.shape
    tm = min(TM_MLP, seq)
    tpb = seq // tm

    def body(df_ref, up_ref, wd_ref, wu_ref, dz2_ref, x_ref, mix_ref, mod_ref, g1_ref, b1_ref,
             dup_ref, act_ref, dz1_ref, dmix_ref, st_ref, pb_ref, acc):
        i, j = pl.program_id(0), pl.program_id(1)

        @pl.when(j == 0)
        def _():
            acc[...] = jnp.zeros_like(acc)

        @pl.when((i == 0) & (j == 0))
        def _():
            st_ref[...] = jnp.zeros_like(st_ref)

        @pl.when((i % tpb == 0) & (j == 0))
        def _():
            pb_ref[...] = jnp.zeros_like(pb_ref)

        r = jnp.maximum(up_ref[...], 0.0)
        act_ref[...] = (r * r).astype(BF16)
        dup = (_mm_nt(df_ref[...], wd_ref[...]) * (2.0 * r)).astype(BF16)
        dup_ref[...] = dup
        acc[...] += _mm_nt(dup, wu_ref[...])

        @pl.when(j == nq - 1)
        def _():
            dh2 = acc[...]
            mix = mix_ref[...]
            xhat, rstd = _ln_stats(ALPHA * x_ref[...] + (1.0 + mod_ref[2:3, :]) * mix)
            x1 = xhat * g1_ref[...] + b1_ref[...]
            dx1 = ALPHA * dz2_ref[...] + dh2 * (1.0 + mod_ref[4:5, :])
            dz1 = _ln_bwd(dx1 * g1_ref[...], xhat, rstd)
            dz1_ref[...] = dz1
            dmix_ref[...] = ((1.0 + mod_ref[2:3, :]) * dz1).astype(BF16)
            st_ref[0:1, :] += _colsum(dx1 * xhat)
            st_ref[1:2, :] += _colsum(dx1)
            pb_ref[0:1, :] += _colsum(dh2 * x1)
            pb_ref[1:2, :] += _colsum(dh2)
            pb_ref[2:3, :] += _colsum(dz1 * mix)

    tok = lambda i, j: (i, 0)
    chunk = lambda i, j: (i, j)
    return pl.pallas_call(
        body, name="mlp_bwd", grid=(t // tm, nq),
        in_specs=[pl.BlockSpec((tm, d), tok), pl.BlockSpec((tm, ns), chunk),
                  pl.BlockSpec((ns, d), lambda i, j: (j, 0)), pl.BlockSpec((None, d, ns), lambda i, j: (j, 0, 0)),
                  pl.BlockSpec((tm, d), tok), pl.BlockSpec((tm, d), tok), pl.BlockSpec((tm, d), tok),
                  pl.BlockSpec((None, 8, d), lambda i, j: (i // tpb, 0, 0)),
                  pl.BlockSpec((1, d), lambda i, j: (0, 0)), pl.BlockSpec((1, d), lambda i, j: (0, 0))],
        out_specs=[pl.BlockSpec((tm, ns), chunk), pl.BlockSpec((tm, ns), chunk),
                   pl.BlockSpec((tm, d), tok), pl.BlockSpec((tm, d), tok),
                   pl.BlockSpec((8, d), lambda i, j: (0, 0)), pl.BlockSpec((None, 8, d), lambda i, j: (i // tpb, 0, 0))],
        out_shape=[jax.ShapeDtypeStruct((t, nq * ns), BF16), jax.ShapeDtypeStruct((t, nq * ns), BF16),
                   jax.ShapeDtypeStruct((t, d), F32), jax.ShapeDtypeStruct((t, d), BF16),
                   jax.ShapeDtypeStruct((8, d), F32), jax.ShapeDtypeStruct((nb, 8, d), F32)],
        scratch_shapes=[pltpu.VMEM((tm, d), F32)],
        compiler_params=_cp(("arbitrary", "arbitrary")),
    )(df, up, w_down_g, w_up_g, dz2, x2, mix, modv, ln1_g, ln1_b)


def _mix_bwd(dmix, proj, y_a, y_b, hs, w_out_g, w_o_lru_g, w_o_sgu_g, seq):
    t, d = dmix.shape
    w = hs.shape[1]
    nq, d_sgu, ns = w_o_sgu_g.shape
    tm = min(TM_MIX, seq)

    def body(dmix_ref, ga_ref, gb_ref, ya_ref, yb_ref, gl_ref, hs_ref, wo_ref, wl_ref, ws_ref,
             dya_ref, dyb_ref, dga_ref, dgb_ref, dgl_ref, dyl_ref, dys_ref):
        dmerged = _mm_nt(dmix_ref[...], wo_ref[...])
        sa, sb = _sigmoid(ga_ref[...]), _sigmoid(gb_ref[...])
        dy_a = (dmerged * sa).astype(BF16)
        dy_b = (dmerged * sb).astype(BF16)
        dya_ref[...] = dy_a
        dyb_ref[...] = dy_b
        dga_ref[...] = (dmerged * ya_ref[...].astype(F32) * (sa * (1.0 - sa))).astype(BF16)
        dgb_ref[...] = (dmerged * yb_ref[...].astype(F32) * (sb * (1.0 - sb))).astype(BF16)
        dyap = _mm_nt(dy_a, wl_ref[...])
        gel, dgel = _gelu_and_grad(gl_ref[...])
        dyl_ref[...] = dyap * gel
        dgl_ref[...] = (dyap * hs_ref[...] * dgel).astype(BF16)
        dys = _mm_nt(dy_b[:, 0:ns], ws_ref[0])
        for q in range(1, nq):
            dys = dys + _mm_nt(dy_b[:, q * ns:(q + 1) * ns], ws_ref[q])
        dys_ref[...] = dys

    row = lambda width, col: pl.BlockSpec((tm, width), functools.partial(lambda i, k: (i, k), k=col))
    full = lambda shape: pl.BlockSpec(shape, lambda i: (0,) * len(shape))
    return pl.pallas_call(
        body, name="mix_bwd", grid=(t // tm,),
        in_specs=[row(d, 0), row(d, 4), row(d, 5), row(d, 0), row(d, 0), row(w, 1), row(w, 0),
                  full(w_out_g.shape), full(w_o_lru_g.shape), full(w_o_sgu_g.shape)],
        out_specs=[row(d, 0), row(d, 0), row(d, 0), row(d, 0), row(w, 0), row(w, 0), row(d_sgu, 0)],
        out_shape=[jax.ShapeDtypeStruct((t, d), BF16), jax.ShapeDtypeStruct((t, d), BF16),
                   jax.ShapeDtypeStruct((t, d), BF16), jax.ShapeDtypeStruct((t, d), BF16),
                   jax.ShapeDtypeStruct((t, w), BF16), jax.ShapeDtypeStruct((t, w), F32),
                   jax.ShapeDtypeStruct((t, d_sgu), F32)],
        compiler_params=_cp(("arbitrary",)),
    )(dmix, proj, proj, y_a, y_b, proj, hs, w_out_g, w_o_lru_g, w_o_sgu_g)


def _sgu_bwd(proj, dys, w_sp, b_sp_t, ln_g, ln_b):
    t = proj.shape[0]
    d_sgu = SGU_GROUPS * HEAD
    tm = min(TM_SGU, t)
    nblk = tm // HEAD
    specs, n_piece = _sgu_specs(tm, d_sgu)

    def body(*refs):
        u = jnp.concatenate([r[...] for r in refs[:n_piece]], axis=1)
        v = jnp.concatenate([r[...] for r in refs[n_piece:2 * n_piece]], axis=1)
        dys_ref, w_ref, bt_ref, g_ref, b_ref, du_ref, dv_ref, dw_ref, st_ref, dbt_ref, dvn_s = refs[2 * n_piece:]

        @pl.when(pl.program_id(0) == 0)
        def _():
            dw_ref[...] = jnp.zeros_like(dw_ref)
            st_ref[...] = jnp.zeros_like(st_ref)
            dbt_ref[...] = jnp.zeros_like(dbt_ref)

        ug, dug_du = _gelu_and_grad(u)
        vg, dvg_dv = _gelu_and_grad(v)
        xhat, rstd = _ln_stats(vg)
        vn = (xhat * g_ref[...] + b_ref[...]).astype(BF16)
        dys_v = dys_ref[...]
        mask = _sgu_mask()
        for g in range(SGU_GROUPS):
            wm = jnp.where(mask, w_ref[g], 0.0).astype(BF16)
            cols = slice(g * HEAD, (g + 1) * HEAD)
            dw_g = jnp.zeros((HEAD, HEAD), F32)
            db_g = jnp.zeros((HEAD, 1), F32)
            for n in range(nblk):
                rows = slice(n * HEAD, (n + 1) * HEAD)
                vn_blk = vn[rows, cols]
                mixed = jnp.dot(wm, vn_blk, preferred_element_type=F32) + bt_ref[:, g:g + 1]
                dy_blk = dys_v[rows, cols]
                du_ref[rows, cols] = (dy_blk * mixed * dug_du[rows, cols]).astype(BF16)
                dmx = dy_blk * ug[rows, cols]
                dvn_s[rows, cols] = _mm_tn(wm, dmx)
                dw_g = dw_g + _mm_nt(dmx, vn_blk)
                db_g = db_g + jnp.sum(dmx, axis=1, keepdims=True)
            dw_ref[g] += jnp.where(mask, dw_g, 0.0)
            dbt_ref[:, g:g + 1] += db_g
        dvn = dvn_s[...]
        st_ref[0:1, :] += _colsum(dvn * xhat)
        st_ref[1:2, :] += _colsum(dvn)
        dv_ref[...] = (_ln_bwd(dvn * g_ref[...], xhat, rstd) * dvg_dv).astype(BF16)

    full = lambda shape: pl.BlockSpec(shape, lambda i: (0,) * len(shape))
    tok = pl.BlockSpec((tm, d_sgu), lambda i: (i, 0))
    return pl.pallas_call(
        body, name="sgu_bwd", grid=(t // tm,),
        in_specs=specs + [tok, full(w_sp.shape), full(b_sp_t.shape), full(ln_g.shape), full(ln_b.shape)],
        out_specs=[tok, tok, full(w_sp.shape), full((8, d_sgu)), full((HEAD, HEAD))],
        out_shape=[jax.ShapeDtypeStruct((t, d_sgu), BF16), jax.ShapeDtypeStruct((t, d_sgu), BF16),
                   jax.ShapeDtypeStruct(w_sp.shape, F32), jax.ShapeDtypeStruct((8, d_sgu), F32),
                   jax.ShapeDtypeStruct((HEAD, HEAD), F32)],
        scratch_shapes=[pltpu.VMEM((tm, d_sgu), F32)],
        compiler_params=_cp(("arbitrary",)),
    )(*([proj] * (2 * n_piece)), dys, w_sp, b_sp_t, ln_g, ln_b)


def _lru_bwd(proj, hs, e, dyl, lru_w, nb, seq):
    t = proj.shape[0]
    w = LRU_HEADS * HEAD
    w_conv, b_conv, w_a, b_a, w_x, b_x, lam = lru_w

    def body(x_ref, hs_ref, e_ref, dy_ref, wc_ref, bc_ref, wa_ref, ba_ref, wx_ref, bx_ref, lam_ref,
             dxl_ref, dwa_ref, dwx_ref, st_ref):
        @pl.when(pl.program_id(1) == 0)
        def _():
            dwa_ref[...] = jnp.zeros_like(dwa_ref)
            dwx_ref[...] = jnp.zeros_like(dwx_ref)
            st_ref[...] = jnp.zeros_like(st_ref)

        xl = x_ref[...]
        xc, r, gi, big_l, a, mult = _lru_gates(xl, wc_ref, bc_ref, wa_ref, ba_ref, wx_ref, bx_ref, lam_ref)
        dh = dy_ref[...] + _shift_up(e_ref[...], 1)
        da = dh * _shift_down(hs_ref[...], 1)
        dmult = dh * (gi * xc)
        d_i = dh * (mult * xc)
        dxc = dh * (mult * gi)
        a2 = a * a
        dla = da * a - dmult * (a2 / mult)
        dr = dla * big_l
        d_big_l = _colsum(dla * r)
        dra = dr * (r * (1.0 - r))
        dia = d_i * (gi * (1.0 - gi))
        dwa_ref[...] += _mm_tn(xc, dra)
        dwx_ref[...] += _mm_tn(xc, dia)
        dxc = dxc + _mm_nt(dra, wa_ref[...]) + _mm_nt(dia, wx_ref[...])
        dxl = wc_ref[CONV_WIDTH - 1:CONV_WIDTH, :] * dxc
        for k in range(CONV_WIDTH - 1):
            dxl = dxl + wc_ref[k:k + 1, :] * _shift_up(dxc, CONV_WIDTH - 1 - k)
        dxl_ref[...] = dxl.astype(BF16)
        st_ref[0:1, :] += _colsum(dra)
        st_ref[1:2, :] += _colsum(dia)
        st_ref[2:3, :] += d_big_l * (LRU_C * _sigmoid(-lam_ref[...]))
        st_ref[3:4, :] += _colsum(dxc)
        for k in range(CONV_WIDTH):
            st_ref[4 + k:5 + k, :] += _colsum(dxc * _shift_down(xl, CONV_WIDTH - 1 - k))

    col = lambda hd, b: (0, hd)
    head = lambda hd, b: (hd, 0, 0)
    tok = lambda hd, b: (b, hd)
    seq_blk = pl.BlockSpec((seq, HEAD), tok)
    return pl.pallas_call(
        body, name="lru_bwd", grid=(LRU_HEADS, nb),
        in_specs=[seq_blk, seq_blk, seq_blk, seq_blk,
                  pl.BlockSpec((CONV_WIDTH, HEAD), col), pl.BlockSpec((1, HEAD), col),
                  pl.BlockSpec((None, HEAD, HEAD), head), pl.BlockSpec((1, HEAD), col),
                  pl.BlockSpec((None, HEAD, HEAD), head), pl.BlockSpec((1, HEAD), col),
                  pl.BlockSpec((1, HEAD), col)],
        out_specs=[seq_blk, pl.BlockSpec((None, HEAD, HEAD), head), pl.BlockSpec((None, HEAD, HEAD), head),
                   pl.BlockSpec((8, HEAD), col)],
        out_shape=[jax.ShapeDtypeStruct((t, w), BF16), jax.ShapeDtypeStruct((LRU_HEADS, HEAD, HEAD), F32),
                   jax.ShapeDtypeStruct((LRU_HEADS, HEAD, HEAD), F32), jax.ShapeDtypeStruct((8, w), F32)],
        compiler_params=_cp(("arbitrary", "arbitrary")),
    )(proj, hs, e, dyl, w_conv, b_conv, w_a, b_a, w_x, b_x, lam)


def _weight_grad(a, g, col_shards, name):
    t, k = a.shape
    n = g.shape[1]
    tt = min(TT_DW, t)
    tk = min(k, 1024)
    while k % tk:
        tk //= 2
    ns = n // N_CHIPS if col_shards else n
    tn = min(ns, 768 if ns % 768 == 0 else 1024)
    while ns % tn:
        tn //= 2
    per = ns // tn

    def body(a_ref, g_ref, o_ref):
        @pl.when(pl.program_id(2) == 0)
        def _():
            o_ref[...] = jnp.zeros_like(o_ref)

        o_ref[...] += _mm_tn(a_ref[...], g_ref[...])

    if col_shards:
        out_spec = pl.BlockSpec((None, tk, tn), lambda i, j, s: (j // per, i, j % per))
        out_shape = jax.ShapeDtypeStruct((N_CHIPS, k, ns), F32)
    else:
        out_spec = pl.BlockSpec((tk, tn), lambda i, j, s: (i, j))
        out_shape = jax.ShapeDtypeStruct((k, n), F32)
    return pl.pallas_call(
        body, name=name, grid=(k // tk, n // tn, t // tt),
        in_specs=[pl.BlockSpec((tt, tk), lambda i, j, s: (s, i)), pl.BlockSpec((tt, tn), lambda i, j, s: (s, j))],
        out_specs=out_spec, out_shape=out_shape,
        compiler_params=_cp(("arbitrary", "arbitrary", "arbitrary")),
    )(a, g)


def _input_grad(dproj, w_in_g, dz1, x2, modv, nb, seq):
    t, d = x2.shape
    nq, _, ns = w_in_g.shape
    tm = min(TM_DH, seq)
    tpb = seq // tm

    def body(dp_ref, w_ref, dz1_ref, x_ref, mod_ref, gx_ref, db_ref, pb_ref, acc):
        i, q = pl.program_id(0), pl.program_id(1)

        @pl.when(q == 0)
        def _():
            acc[...] = jnp.zeros_like(acc)

        @pl.when((i == 0) & (q == 0))
        def _():
            db_ref[...] = jnp.zeros_like(db_ref)

        @pl.when((i % tpb == 0) & (q == 0))
        def _():
            pb_ref[...] = jnp.zeros_like(pb_ref)

        dp = dp_ref[...]
        acc[...] += _mm_nt(dp, w_ref[...])
        db_ref[q, 0:1, :] += _colsum(dp.astype(F32))

        @pl.when(q == nq - 1)
        def _():
            dh = acc[...]
            gx_ref[...] = ALPHA * dz1_ref[...] + dh * (1.0 + mod_ref[1:2, :])
            pb_ref[0:1, :] += _colsum(dh * x_ref[...])
            pb_ref[1:2, :] += _colsum(dh)

    tok = lambda i, q: (i, 0)
    return pl.pallas_call(
        body, name="input_grad", grid=(t // tm, nq),
        in_specs=[pl.BlockSpec((tm, ns), lambda i, q: (i, q)), pl.BlockSpec((None, d, ns), lambda i, q: (q, 0, 0)),
                  pl.BlockSpec((tm, d), tok), pl.BlockSpec((tm, d), tok),
                  pl.BlockSpec((None, 8, d), lambda i, q: (i // tpb, 0, 0))],
        out_specs=[pl.BlockSpec((tm, d), tok), pl.BlockSpec((nq, 8, ns), lambda i, q: (0, 0, 0)),
                   pl.BlockSpec((None, 8, d), lambda i, q: (i // tpb, 0, 0))],
        out_shape=[jax.ShapeDtypeStruct((t, d), F32), jax.ShapeDtypeStruct((nq, 8, ns), F32),
                   jax.ShapeDtypeStruct((nb, 8, d), F32)],
        scratch_shapes=[pltpu.VMEM((tm, d), F32)],
        compiler_params=_cp(("arbitrary", "arbitrary")),
    )(dproj, w_in_g, dz1, x2, modv)


def _rows128(v):
    flat = v.reshape(-1, HEAD)
    pad = (-flat.shape[0]) % 8
    return jnp.pad(flat, ((0, pad), (0, 0))) if pad else flat


def kernel(x, c, w_ada, b_ada, w_in, b_in, w_conv, b_conv, w_rg_a, b_rg_a, w_rg_x, b_rg_x, lru_lambda, w_sp, b_sp, ln_v_g, ln_v_b, w_o_lru, w_o_sgu, w_out, ln1_g, ln1_b, w_up, w_down, ln2_g, ln2_b, loss_target, m_w_ada, m_b_ada, m_w_in, m_b_in, m_w_conv, m_b_conv, m_w_rg_a, m_b_rg_a, m_w_rg_x, m_b_rg_x, m_lru_lambda, m_w_sp, m_b_sp, m_ln_v_g, m_ln_v_b, m_w_o_lru, m_w_o_sgu, m_w_out, m_ln1_g, m_ln1_b, m_w_up, m_w_down, m_ln2_g, m_ln2_b, v_w_ada, v_b_ada, v_w_in, v_b_in, v_w_conv, v_b_conv, v_w_rg_a, v_b_rg_a, v_w_rg_x, v_b_rg_x, v_lru_lambda, v_w_sp, v_b_sp, v_ln_v_g, v_ln_v_b, v_w_o_lru, v_w_o_sgu, v_w_out, v_ln1_g, v_ln1_b, v_w_up, v_w_down, v_ln2_g, v_ln2_b):
    given = dict(locals())
    nb, seq, d = x.shape
    t = nb * seq
    w_lru = LRU_HEADS * HEAD
    d_sgu = SGU_GROUPS * HEAD
    xi, yi, ci = lax.axis_index("x"), lax.axis_index("y"), lax.axis_index("c")
    chip = 2 * xi + yi
    dev = 2 * chip + ci
    cidx = jnp.reshape(ci, (1,)).astype(jnp.int32)

    x2 = x.reshape(t, d)
    target = loss_target.reshape(t, d)

    c_rows = _rows128(c)
    wconv_rows = _rows128(w_conv[0])
    slab0 = _all_gather_small(jnp.concatenate([c_rows, wconv_rows], axis=0), "gather_c_wconv")
    slab0 = slab0.reshape(N_DEV, -1, HEAD)
    c_all = slab0[:, :c_rows.shape[0]].reshape(N_DEV * nb, d)
    n_wc = CONV_WIDTH * (w_lru // N_CHIPS) // HEAD
    wc = slab0[0::2, c_rows.shape[0]:c_rows.shape[0] + n_wc].reshape(N_CHIPS, CONV_WIDTH, w_lru // N_CHIPS)
    w_conv_full = jnp.transpose(wc, (1, 0, 2)).reshape(CONV_WIDTH, w_lru)

    n_ada = w_ada.shape[2]
    b_ada_cols = lax.dynamic_slice(b_ada, (0, chip * n_ada), (1, n_ada))
    mod_cols = _ada_fwd(c_all, w_ada[0], b_ada_cols)
    half = (N_DEV * nb) // 2
    mod_half = lax.dynamic_slice(mod_cols, (ci * half, 0), (half, n_ada))
    mod_g = _all_gather_small(mod_half, "gather_mod").reshape(N_CHIPS, 2, half, n_ada)
    mod_all = jnp.transpose(mod_g, (1, 2, 0, 3)).reshape(N_DEV * nb, N_CHIPS * n_ada)
    mod_loc = lax.dynamic_slice(mod_all, (dev * nb, 0), (nb, N_CHIPS * n_ada)).reshape(nb, 6, d)
    modv = jnp.pad(mod_loc, ((0, 0), (0, 2), (0, 0)))

    big = ["w_in", "w_o_lru", "w_o_sgu", "w_out", "w_up", "w_down"]
    w_in_g, w_o_lru_g, w_o_sgu_g, w_out_g, w_up_g, w_down_g = _gather_weights([given[n][0].astype(BF16) for n in big])
    w_o_lru_g = w_o_lru_g.reshape(w_lru, d)
    w_out_g = w_out_g.reshape(d, d)
    w_down_g = w_down_g.reshape(-1, d)

    lru_w = (w_conv_full, b_conv, w_rg_a[0], b_rg_a, w_rg_x[0], b_rg_x, lru_lambda)
    b_sp_t = jnp.transpose(b_sp[0])

    proj, h = _proj_fwd(x2, modv, w_in_g, b_in, seq)
    a, inp = _lru_prep(proj, lru_w, nb, seq)
    a3 = a.reshape(nb, seq, w_lru)
    hs = _scan(a3, inp.reshape(nb, seq, w_lru), False, "lru_scan").reshape(t, w_lru)
    y_sgu = _sgu_fwd(proj, w_sp[0], b_sp_t, ln_v_g, ln_v_b)
    yap, y_a, y_b, merged, mix, x1 = _mix_fwd(hs, proj, y_sgu, x2, modv, w_o_lru_g, w_o_sgu_g, w_out_g, ln1_g, ln1_b, seq)
    up, h2, dz2, df, st2, pb2 = _mlp_fwd(x1, modv, w_up_g, w_down_g, ln2_g, ln2_b, target, nb, seq)
    loss = lax.psum(st2[2, 0], ("x", "y", "c"))

    dup, act, dz1, dmix, st1, pb1 = _mlp_bwd(df, up, w_down_g, w_up_g, dz2, x2, mix, modv, ln1_g, ln1_b, nb, seq)
    dy_a, dy_b, dga, dgb, dgl, dyl, dys = _mix_bwd(dmix, proj, y_a, y_b, hs, w_out_g, w_o_lru_g, w_o_sgu_g, seq)
    du, dv, g_w_sp, st_sgu, g_b_sp_t = _sgu_bwd(proj, dys, w_sp[0], b_sp_t, ln_v_g, ln_v_b)
    dyl3 = dyl.reshape(nb, seq, w_lru)
    e = _scan(a3, dyl3, True, "lru_scan_bwd").reshape(t, w_lru)
    dxl, g_w_rg_a, g_w_rg_x, st_lru = _lru_bwd(proj, hs, e, dyl, lru_w, nb, seq)
    dproj = jnp.concatenate([dxl, dgl, du, dv, dga, dgb], axis=1)
    grad_x2, g_b_in4, pb0 = _input_grad(dproj, w_in_g, dz1, x2, modv, nb, seq)

    part = {
        "w_in": _weight_grad(h, dproj, True, "grad_w_in"),
        "w_o_lru": _weight_grad(yap, dy_a, False, "grad_w_o_lru"),
        "w_o_sgu": _weight_grad(y_sgu, dy_b, True, "grad_w_o_sgu"),
        "w_out": _weight_grad(merged, dmix, False, "grad_w_out"),
        "w_up": _weight_grad(h2, dup, True, "grad_w_up"),
        "w_down": _weight_grad(act, df, False, "grad_w_down"),
    }

    g4 = []
    for n in big:
        shard = given[n].shape[1:]
        g4.append(part[n].reshape(N_CHIPS, 2, shard[0] // 2, shard[1]))
    recv = _sibling_swap_halves(g4)
    chip_part = [_add_own_half(g4[k], recv[k], cidx, "grad_pair_sum_" + big[k]) for k in range(len(big))]
    slots = _chip_exchange(chip_part)
    halves = [_sum_slots(slots[k], "grad_chip_sum_" + big[k]) for k in range(len(big))]
    joined = _sibling_join_halves(halves)
    grads = {n: joined[k].reshape(given[n].shape[1:]) for k, n in enumerate(big)}

    dmod_loc = jnp.stack([pb0[:, 1], pb0[:, 0], pb1[:, 2], pb1[:, 1], pb1[:, 0], pb2[:, 0]], axis=1)
    small = [
        ("dmod", dmod_loc),
        ("b_in", g_b_in4[:, 0]), ("w_conv", st_lru[4:8]), ("b_conv", st_lru[3]),
        ("w_rg_a", g_w_rg_a), ("b_rg_a", st_lru[0]), ("w_rg_x", g_w_rg_x), ("b_rg_x", st_lru[1]),
        ("lru_lambda", st_lru[2]), ("w_sp", g_w_sp), ("b_sp", jnp.transpose(g_b_sp_t[:, :SGU_GROUPS])),
        ("ln_v_g", st_sgu[0]), ("ln_v_b", st_sgu[1]), ("ln1_g", st1[0]), ("ln1_b", st1[1]),
        ("ln2_g", st2[0]), ("ln2_b", st2[1]),
    ]
    pieces = [_rows128(v) for _, v in small]
    slab = jnp.concatenate(pieces, axis=0)
    slab = jnp.pad(slab, ((0, (-slab.shape[0]) % TR_EW), (0, 0)))
    n_rows = slab.shape[0]
    gathered = _all_gather_small(slab, "gather_small_grads").reshape(N_DEV, n_rows, HEAD)
    summed = _sum_slots(gathered, "small_grad_sum")
    off = 0
    for (n, v), piece in zip(small, pieces):
        rows = v.size // HEAD
        if n == "dmod":
            dmod_all = gathered[:, off:off + rows].reshape(N_DEV * nb, 6 * d)
        else:
            grads[n] = summed[off:off + rows].reshape(v.shape)
        off += piece.shape[0]

    dmod_cols = lax.dynamic_slice(dmod_all, (0, chip * n_ada), (N_DEV * nb, n_ada))
    grads["w_ada"], grads["b_ada"] = _ada_bwd(c_all, dmod_all, dmod_cols)
    n_wcs = w_lru // N_CHIPS
    grads["w_conv"] = lax.dynamic_slice(grads["w_conv"], (0, chip * n_wcs), (CONV_WIDTH, n_wcs))

    names = ['w_ada', 'b_ada', 'w_in', 'b_in', 'w_conv', 'b_conv', 'w_rg_a', 'b_rg_a', 'w_rg_x', 'b_rg_x', 'lru_lambda',
             'w_sp', 'b_sp', 'ln_v_g', 'ln_v_b', 'w_o_lru', 'w_o_sgu', 'w_out', 'ln1_g', 'ln1_b', 'w_up', 'w_down',
             'ln2_g', 'ln2_b']
    out_g, out_d, out_m, out_v = [], [], [], []
    for n in names:
        wv = given[n]
        shape2 = (-1, wv.shape[-1])
        g2 = grads[n].reshape(wv.shape).reshape(shape2)
        dlt, nm, nv = _adamw(wv.reshape(shape2), g2, given["m_" + n].reshape(shape2), given["v_" + n].reshape(shape2),
                             "adamw_" + n)
        out_g.append(g2.reshape(wv.shape))
        out_d.append(dlt.reshape(wv.shape))
        out_m.append(nm.reshape(wv.shape))
        out_v.append(nv.reshape(wv.shape))

    return (loss, grad_x2.reshape(nb, seq, d), *out_g, *out_d, *out_m, *out_v)
```

```python
import functools
import math

import jax
import jax.numpy as jnp
from jax import lax
from jax.experimental import pallas as pl
from jax.experimental.pallas import tpu as pltpu

F32 = jnp.float32
BF16 = jnp.bfloat16
MESH = pl.DeviceIdType.MESH

N_CHIPS = 4
N_DEV = 8
LRU_HEADS = 10
HEAD = 128
SGU_GROUPS = 6
SGU_CHUNK = 64
CONV_WIDTH = 4
LRU_C = 8.0
ALPHA = 2.0 ** 0.25
LN_EPS = 1e-5
ADAM_LR, ADAM_B1, ADAM_B2, ADAM_EPS, ADAM_WD, ADAM_STEP = 0.001, 0.9, 0.999, 1e-08, 0.01, 10

VMEM_LIMIT = 56 * 1024 * 1024
TM_PROJ = 1024
TM_MIX = 256
TM_MLP = 512
TM_SGU = 512
TM_DH = 512
TT_DW = 1024
TC_SCAN = 256
TR_EW = 256


def _cp(sem=None):
    return pltpu.CompilerParams(dimension_semantics=sem, vmem_limit_bytes=VMEM_LIMIT)


def _mm(a, b):
    return jnp.dot(a.astype(BF16), b.astype(BF16), preferred_element_type=F32)


def _mm_nt(a, b):
    return lax.dot_general(a.astype(BF16), b.astype(BF16), (((1,), (1,)), ((), ())), preferred_element_type=F32)


def _mm_tn(a, b):
    return lax.dot_general(a.astype(BF16), b.astype(BF16), (((0,), (0,)), ((), ())), preferred_element_type=F32)


def _sigmoid(x):
    return 1.0 / (1.0 + jnp.exp(-x))


_GELU_K = math.sqrt(2.0 / math.pi)


def _gelu(x):
    t = jnp.tanh(_GELU_K * (x + 0.044715 * (x * x * x)))
    return 0.5 * x * (1.0 + t)


def _gelu_and_grad(x):
    x2 = x * x
    t = jnp.tanh(_GELU_K * (x + 0.044715 * (x2 * x)))
    g = 0.5 * x * (1.0 + t)
    dg = 0.5 * (1.0 + t) + 0.5 * x * (1.0 - t * t) * (_GELU_K * (1.0 + 3.0 * 0.044715 * x2))
    return g, dg


def _ln_stats(z):
    mu = jnp.mean(z, axis=-1, keepdims=True)
    zc = z - mu
    var = jnp.mean(zc * zc, axis=-1, keepdims=True)
    rstd = lax.rsqrt(var + LN_EPS)
    return zc * rstd, rstd


def _ln_bwd(dxh, xhat, rstd):
    m1 = jnp.mean(dxh, axis=-1, keepdims=True)
    m2 = jnp.mean(dxh * xhat, axis=-1, keepdims=True)
    return rstd * (dxh - m1 - xhat * m2)


def _colsum(v):
    return jnp.sum(v, axis=0, keepdims=True)


def _shift_down(v, j):
    if j == 0:
        return v
    rows = lax.broadcasted_iota(jnp.int32, v.shape, 0)
    return jnp.where(rows >= j, pltpu.roll(v, j, 0), 0.0)


def _shift_up(v, j):
    if j == 0:
        return v
    n = v.shape[0]
    rows = lax.broadcasted_iota(jnp.int32, v.shape, 0)
    return jnp.where(rows < n - j, pltpu.roll(v, n - j, 0), 0.0)


def _my_pos():
    return lax.axis_index("x"), lax.axis_index("y"), lax.axis_index("c")


def _all_gather_small(v, name):
    m_per, n = v.shape

    def body(x_ref, out_ref, send_sems, recv_sems, local_sem):
        x, y, c = _my_pos()
        me, sibling = (x, y, c), (x, y, 1 - c)
        chips = [(1 - x, y), (x, 1 - y), (1 - x, 1 - y)]

        def rows(px, py, pc):
            return out_ref.at[pl.ds((4 * px + 2 * py + pc) * m_per, m_per), :]

        def copy(k, block, to, src=None):
            return pltpu.make_async_remote_copy(
                src_ref=rows(*block) if src is None else src, dst_ref=rows(*block),
                send_sem=send_sems.at[k], recv_sem=recv_sems.at[k], device_id=to, device_id_type=MESH)

        mine = pltpu.make_async_copy(x_ref, rows(*me), local_sem)
        mine.start()
        first = [copy(0, me, sibling, src=x_ref)]
        first += [copy(1 + j, me, (*chip, c), src=x_ref) for j, chip in enumerate(chips)]
        for cp in first:
            cp.start()
        passed = [copy(4 + j, (*chip, c), sibling) for j, chip in enumerate(chips)]
        for j, chip in enumerate(chips):
            copy(1 + j, (*chip, c), me).wait_recv()
            passed[j].start()
        copy(0, sibling, me).wait_recv()
        for j, chip in enumerate(chips):
            copy(4 + j, (*chip, 1 - c), me).wait_recv()
        for cp in first + passed:
            cp.wait_send()
        mine.wait()

    return pl.pallas_call(
        body, name=name,
        out_shape=jax.ShapeDtypeStruct((N_DEV * m_per, n), v.dtype),
        in_specs=[pl.BlockSpec(memory_space=pltpu.VMEM)],
        out_specs=pl.BlockSpec(memory_space=pltpu.VMEM),
        scratch_shapes=[pltpu.SemaphoreType.DMA((7,)), pltpu.SemaphoreType.DMA((7,)), pltpu.SemaphoreType.DMA],
        compiler_params=pltpu.CompilerParams(vmem_limit_bytes=VMEM_LIMIT),
    )(v)


_HBM = pl.BlockSpec(memory_space=pltpu.HBM)
_SEM = pl.BlockSpec(memory_space=pltpu.SEMAPHORE)
_EFFECT = pltpu.SideEffectType.DATAFLOW_SIDE_EFFECTING


def _gather_copies(ins, lands, send_sems, recv_sems):
    x, y, c = _my_pos()
    p = 2 * x + y
    peers = [(x, 1 - y), (1 - x, y), (1 - x, 1 - y)]
    sends, recvs = [], []
    for k in range(len(ins)):
        for j, (qx, qy) in enumerate(peers):
            sems = dict(send_sem=send_sems.at[3 * k + j], recv_sem=recv_sems.at[3 * k + j],
                        device_id=(qx, qy, c), device_id_type=MESH)
            sends.append(pltpu.make_async_remote_copy(src_ref=ins[k], dst_ref=lands[k].at[p], **sems))
            recvs.append(pltpu.make_async_remote_copy(src_ref=ins[k], dst_ref=lands[k].at[2 * qx + qy], **sems))
    return sends, recvs


def _gather_weights_start(shards, name, after=()):
    n = len(shards)
    first_out = 2 * n + len(after)

    def body(*refs):
        ins, lands = refs[:n], refs[n:2 * n]
        send_sems, recv_sems = refs[first_out:first_out + 2]
        token = refs[-1]
        sends, _ = _gather_copies(ins, lands, send_sems, recv_sems)
        for cp in sends:
            cp.start()
        token[...] = jnp.zeros_like(token)

    lands = [pltpu.with_memory_space_constraint(lax.empty((N_CHIPS,) + s.shape, s.dtype), pltpu.HBM) for s in shards]
    ins = [pltpu.with_memory_space_constraint(s, pltpu.HBM) for s in shards]
    return pl.pallas_call(
        body, name=name,
        out_shape=(pltpu.SemaphoreType.DMA((3 * n,)), pltpu.SemaphoreType.DMA((3 * n,)),
                   *[pltpu.HBM(s.shape, s.dtype) for s in ins], *[pltpu.HBM(s.shape, s.dtype) for s in lands],
                   jax.ShapeDtypeStruct((8, HEAD), F32)),
        in_specs=[_HBM] * (2 * n) + [pl.BlockSpec(memory_space=pl.ANY)] * len(after),
        out_specs=(_SEM, _SEM, *([_HBM] * (2 * n)), pl.BlockSpec(memory_space=pltpu.VMEM)),
        input_output_aliases={k: 2 + k for k in range(2 * n)},
        compiler_params=pltpu.CompilerParams(has_side_effects=_EFFECT),
    )(*ins, *lands, *after)


def _gather_weights_wait(started, shards, after, name):
    n = len(shards)
    send_sems, recv_sems = started[0], started[1]
    ins, lands = started[2:2 + n], started[2 + n:2 + 2 * n]

    def body(*refs):
        in_refs, land_refs = refs[:n], refs[n:2 * n]
        s_sems, r_sems = refs[2 * n], refs[2 * n + 1]
        sends, recvs = _gather_copies(in_refs, land_refs, s_sems, r_sems)
        for cp in sends:
            cp.wait_send()
        for cp in recvs:
            cp.wait_recv()

    outs = pl.pallas_call(
        body, name=name,
        out_shape=tuple(pltpu.HBM(s.shape, s.dtype) for s in (*ins, *lands)),
        in_specs=[_HBM] * (2 * n) + [_SEM, _SEM] + [pl.BlockSpec(memory_space=pl.ANY)] * len(after),
        out_specs=tuple([_HBM] * (2 * n)),
        input_output_aliases={k: k for k in range(2 * n)},
        compiler_params=pltpu.CompilerParams(has_side_effects=_EFFECT),
    )(*ins, *lands, send_sems, recv_sems, *after)
    return outs[n:]


def _fill_own_slot(gathered, shards, name):
    n = len(shards)

    def body(*refs):
        ins, lands, outs, sems = refs[:n], refs[n:2 * n], refs[2 * n:3 * n], refs[3 * n]
        x, y, _ = _my_pos()
        cps = [pltpu.make_async_copy(ins[k], outs[k].at[2 * x + y], sems.at[k]) for k in range(n)]
        for cp in cps:
            cp.start()
        for cp in cps:
            cp.wait()

    any_spec = pl.BlockSpec(memory_space=pl.ANY)
    return pl.pallas_call(
        body, name=name,
        out_shape=[jax.ShapeDtypeStruct(g.shape, g.dtype) for g in gathered],
        in_specs=[any_spec] * (2 * n), out_specs=[any_spec] * n,
        input_output_aliases={n + k: k for k in range(n)},
        scratch_shapes=[pltpu.SemaphoreType.DMA((n,))],
    )(*shards, *gathered)


def _sibling_swap_halves(grads):
    n = len(grads)

    def body(*refs):
        ins, outs = refs[:n], refs[n:2 * n]
        send_sems, recv_sems = refs[2 * n:]
        x, y, c = _my_pos()
        cps = [pltpu.make_async_remote_copy(
            src_ref=ins[k].at[:, 1 - c], dst_ref=outs[k], send_sem=send_sems.at[k], recv_sem=recv_sems.at[k],
            device_id=(x, y, 1 - c), device_id_type=MESH) for k in range(n)]
        for cp in cps:
            cp.start()
        for cp in cps:
            cp.wait()

    any_spec = pl.BlockSpec(memory_space=pl.ANY)
    return pl.pallas_call(
        body, name="grads_to_sibling",
        out_shape=[jax.ShapeDtypeStruct((g.shape[0],) + g.shape[2:], g.dtype) for g in grads],
        in_specs=[any_spec] * n, out_specs=[any_spec] * n,
        scratch_shapes=[pltpu.SemaphoreType.DMA((n,)), pltpu.SemaphoreType.DMA((n,))],
    )(*grads)


def _chip_exchange(parts):
    n = len(parts)

    def body(*refs):
        ins, outs = refs[:n], refs[n:2 * n]
        send_sems, recv_sems, local_sems = refs[2 * n:]
        x, y, c = _my_pos()
        p = 2 * x + y
        peers = [(x, 1 - y), (1 - x, y), (1 - x, 1 - y)]
        local = [pltpu.make_async_copy(ins[k].at[p], outs[k].at[p], local_sems.at[k]) for k in range(n)]
        for cp in local:
            cp.start()
        remote = []
        for k in range(n):
            for j, (qx, qy) in enumerate(peers):
                remote.append(pltpu.make_async_remote_copy(
                    src_ref=ins[k].at[2 * qx + qy], dst_ref=outs[k].at[p], send_sem=send_sems.at[k, j],
                    recv_sem=recv_sems.at[k, j], device_id=(qx, qy, c), device_id_type=MESH))
        for cp in remote:
            cp.start()
        for k in range(n):
            for j, (qx, qy) in enumerate(peers):
                pltpu.make_async_remote_copy(
                    src_ref=ins[k].at[p], dst_ref=outs[k].at[2 * qx + qy], send_sem=send_sems.at[k, j],
                    recv_sem=recv_sems.at[k, j], device_id=(qx, qy, c), device_id_type=MESH).wait_recv()
        for cp in remote:
            cp.wait_send()
        for cp in local:
            cp.wait()

    any_spec = pl.BlockSpec(memory_space=pl.ANY)
    return pl.pallas_call(
        body, name="grads_chip_exchange",
        out_shape=[jax.ShapeDtypeStruct(g.shape, g.dtype) for g in parts],
        in_specs=[any_spec] * n, out_specs=[any_spec] * n,
        scratch_shapes=[pltpu.SemaphoreType.DMA((n, 3)), pltpu.SemaphoreType.DMA((n, 3)), pltpu.SemaphoreType.DMA((n,))],
    )(*parts)


def _sibling_swap(halves):
    n = len(halves)

    def body(*refs):
        ins, outs = refs[:n], refs[n:2 * n]
        send_sems, recv_sems = refs[2 * n:]
        x, y, c = _my_pos()
        cps = [pltpu.make_async_remote_copy(
            src_ref=ins[k], dst_ref=outs[k], send_sem=send_sems.at[k], recv_sem=recv_sems.at[k],
            device_id=(x, y, 1 - c), device_id_type=MESH) for k in range(n)]
        for cp in cps:
            cp.start()
        for cp in cps:
            cp.wait()

    any_spec = pl.BlockSpec(memory_space=pl.ANY)
    return pl.pallas_call(
        body, name="grads_sibling_swap",
        out_shape=[jax.ShapeDtypeStruct(h.shape, h.dtype) for h in halves],
        in_specs=[any_spec] * n, out_specs=[any_spec] * n,
        scratch_shapes=[pltpu.SemaphoreType.DMA((n,)), pltpu.SemaphoreType.DMA((n,))],
    )(*halves)


def _row_tile(r):
    t = min(TR_EW, r)
    while r % t:
        t //= 2
    return t


def _add_own_half(g4, recv, cidx, name):
    _, _, rh, cdim = g4.shape
    tr = _row_tile(rh)

    def body(c_ref, a_ref, b_ref, o_ref):
        o_ref[...] = (a_ref[...] + b_ref[...]).astype(BF16)

    return pl.pallas_call(
        body, name=name,
        grid_spec=pltpu.PrefetchScalarGridSpec(
            num_scalar_prefetch=1, grid=(N_CHIPS, rh // tr),
            in_specs=[pl.BlockSpec((None, None, tr, cdim), lambda q, i, c: (q, c[0], i, 0)),
                      pl.BlockSpec((None, tr, cdim), lambda q, i, c: (q, i, 0))],
            out_specs=pl.BlockSpec((None, tr, cdim), lambda q, i, c: (q, i, 0))),
        out_shape=jax.ShapeDtypeStruct(recv.shape, BF16),
        compiler_params=_cp(("arbitrary", "arbitrary")),
    )(cidx, g4, recv)


def _sum_slots(v, name):
    n, r, cdim = v.shape
    tr = _row_tile(r)

    def body(v_ref, o_ref):
        acc = v_ref[0].astype(F32)
        for k in range(1, n):
            acc = acc + v_ref[k].astype(F32)
        o_ref[...] = acc

    return pl.pallas_call(
        body, name=name, grid=(r // tr,),
        in_specs=[pl.BlockSpec((n, tr, cdim), lambda i: (0, i, 0))],
        out_specs=pl.BlockSpec((tr, cdim), lambda i: (i, 0)),
        out_shape=jax.ShapeDtypeStruct((r, cdim), F32),
        compiler_params=_cp(("arbitrary",)),
    )(v)


def _adamw_math(wv, gg, mv, vv):
    nm = ADAM_B1 * mv + (1.0 - ADAM_B1) * gg
    nv = ADAM_B2 * vv + (1.0 - ADAM_B2) * (gg * gg)
    m_hat = nm / (1.0 - ADAM_B1 ** ADAM_STEP)
    v_hat = nv / (1.0 - ADAM_B2 ** ADAM_STEP)
    return -ADAM_LR * (m_hat / (jnp.sqrt(v_hat) + ADAM_EPS) + ADAM_WD * wv), nm, nv


def _adamw_halves(w, mine, theirs, m, v, cidx, name):
    r, cdim = w.shape
    rh = r // 2
    tr = _row_tile(rh)
    nblk = rh // tr

    def body(c_ref, w_ref, a_ref, b_ref, m_ref, v_ref, g_ref, d_ref, nm_ref, nv_ref):
        gg = jnp.where(pl.program_id(0) == c_ref[0], a_ref[...], b_ref[...])
        g_ref[...] = gg
        d_ref[...], nm_ref[...], nv_ref[...] = _adamw_math(w_ref[...], gg, m_ref[...], v_ref[...])

    full = pl.BlockSpec((tr, cdim), lambda hh, i, c: (hh * nblk + i, 0))
    half = pl.BlockSpec((tr, cdim), lambda hh, i, c: (i, 0))
    return pl.pallas_call(
        body, name=name,
        grid_spec=pltpu.PrefetchScalarGridSpec(
            num_scalar_prefetch=1, grid=(2, nblk),
            in_specs=[full, half, half, full, full], out_specs=[full] * 4),
        out_shape=[jax.ShapeDtypeStruct((r, cdim), F32)] * 4,
        compiler_params=_cp(("arbitrary", "arbitrary")),
    )(cidx, w, mine, theirs, m, v)


def _adamw(w, g, m, v, name):
    r, cdim = w.shape
    tr = _row_tile(r) if r % 8 == 0 else r

    def body(w_ref, g_ref, m_ref, v_ref, d_ref, nm_ref, nv_ref):
        d_ref[...], nm_ref[...], nv_ref[...] = _adamw_math(w_ref[...], g_ref[...], m_ref[...], v_ref[...])

    spec = pl.BlockSpec((tr, cdim), lambda i: (i, 0))
    return pl.pallas_call(
        body, name=name, grid=(r // tr,), in_specs=[spec] * 4, out_specs=[spec] * 3,
        out_shape=[jax.ShapeDtypeStruct((r, cdim), F32)] * 3,
        compiler_params=_cp(("arbitrary",)),
    )(w, g, m, v)


def _ada_fwd(c_all, w_ada, b_cols):
    nb, _ = c_all.shape
    n = w_ada.shape[1]

    def body(c_ref, w_ref, b_ref, o_ref):
        cv = c_ref[...]
        o_ref[...] = _mm(cv * _sigmoid(cv), w_ref[...]) + b_ref[...]

    return pl.pallas_call(
        body, name="ada_fwd", out_shape=jax.ShapeDtypeStruct((nb, n), F32),
        compiler_params=pltpu.CompilerParams(vmem_limit_bytes=VMEM_LIMIT),
    )(c_all, w_ada, b_cols)


def _ada_bwd(c_all, dmod_all, dmod_cols):
    d = c_all.shape[1]
    n = dmod_cols.shape[1]

    def body(c_ref, da_ref, dc_ref, gw_ref, gb_ref):
        cv = c_ref[...]
        gw_ref[...] = _mm_tn(cv * _sigmoid(cv), dc_ref[...])
        gb_ref[...] = _colsum(da_ref[...])

    return pl.pallas_call(
        body, name="ada_bwd",
        out_shape=[jax.ShapeDtypeStruct((d, n), F32), jax.ShapeDtypeStruct((1, dmod_all.shape[1]), F32)],
        compiler_params=pltpu.CompilerParams(vmem_limit_bytes=VMEM_LIMIT),
    )(c_all, dmod_all, dmod_cols)


def _proj_fwd(x2, modv, w_in_g, b_in, seq):
    t, d = x2.shape
    nq, _, ns = w_in_g.shape
    tm = min(TM_PROJ, seq)
    tpb = seq // tm

    def body(x_ref, mod_ref, w_ref, b_ref, proj_ref, h_ref, h_s):
        @pl.when(pl.program_id(1) == 0)
        def _():
            h = x_ref[...] * (1.0 + mod_ref[1:2, :]) + mod_ref[0:1, :]
            h_s[...] = h.astype(BF16)
            h_ref[...] = h.astype(BF16)

        proj_ref[...] = jnp.dot(h_s[...], w_ref[...], preferred_element_type=F32) + b_ref[...]

    return pl.pallas_call(
        body, name="proj_fwd", grid=(t // tm, nq),
        in_specs=[pl.BlockSpec((tm, d), lambda i, q: (i, 0)),
                  pl.BlockSpec((None, 8, d), lambda i, q: (i // tpb, 0, 0)),
                  pl.BlockSpec((None, d, ns), lambda i, q: (q, 0, 0)),
                  pl.BlockSpec((1, ns), lambda i, q: (0, q))],
        out_specs=[pl.BlockSpec((tm, ns), lambda i, q: (i, q)),
                   pl.BlockSpec((tm, d), lambda i, q: (i, 0))],
        out_shape=[jax.ShapeDtypeStruct((t, nq * ns), F32), jax.ShapeDtypeStruct((t, d), BF16)],
        scratch_shapes=[pltpu.VMEM((tm, d), BF16)],
        compiler_params=_cp(("arbitrary", "arbitrary")),
    )(x2, modv, w_in_g, b_in)


def _lru_gates(xl, wc_ref, bc_ref, wa_ref, ba_ref, wx_ref, bx_ref, lam_ref):
    xc = bc_ref[...] + wc_ref[CONV_WIDTH - 1:CONV_WIDTH, :] * xl
    for k in range(CONV_WIDTH - 1):
        xc = xc + wc_ref[k:k + 1, :] * _shift_down(xl, CONV_WIDTH - 1 - k)
    r = _sigmoid(_mm(xc, wa_ref[...]) + ba_ref[...])
    gi = _sigmoid(_mm(xc, wx_ref[...]) + bx_ref[...])
    nl = -lam_ref[...]
    e = jnp.exp(-jnp.abs(nl))
    u = 1.0 + e
    dlt = u - 1.0
    log1p_e = jnp.where(dlt == 0.0, e, jnp.log(u) * (e / jnp.where(dlt == 0.0, 1.0, dlt)))
    big_l = -LRU_C * (jnp.maximum(nl, 0.0) + log1p_e)
    la = big_l * r
    a = jnp.exp(la)
    mult = jnp.sqrt(jnp.tanh(-la) * (a * a + 1.0))
    return xc, r, gi, big_l, a, mult


def _lru_prep(proj, lru_w, nb, seq):
    t = proj.shape[0]
    w = LRU_HEADS * HEAD
    w_conv, b_conv, w_a, b_a, w_x, b_x, lam = lru_w

    def body(x_ref, wc_ref, bc_ref, wa_ref, ba_ref, wx_ref, bx_ref, lam_ref, a_ref, inp_ref):
        xc, r, gi, big_l, a, mult = _lru_gates(x_ref[...], wc_ref, bc_ref, wa_ref, ba_ref, wx_ref, bx_ref, lam_ref)
        a_ref[...] = a
        inp_ref[...] = mult * (gi * xc)

    col = lambda b, hd: (0, hd)
    head = lambda b, hd: (hd, 0, 0)
    tok = lambda b, hd: (b, hd)
    return pl.pallas_call(
        body, name="lru_prep", grid=(nb, LRU_HEADS),
        in_specs=[pl.BlockSpec((seq, HEAD), tok),
                  pl.BlockSpec((CONV_WIDTH, HEAD), col), pl.BlockSpec((1, HEAD), col),
                  pl.BlockSpec((None, HEAD, HEAD), head), pl.BlockSpec((1, HEAD), col),
                  pl.BlockSpec((None, HEAD, HEAD), head), pl.BlockSpec((1, HEAD), col),
                  pl.BlockSpec((1, HEAD), col)],
        out_specs=[pl.BlockSpec((seq, HEAD), tok)] * 2,
        out_shape=[jax.ShapeDtypeStruct((t, w), F32)] * 2,
        compiler_params=_cp(("arbitrary", "arbitrary")),
    )(proj, w_conv, b_conv, w_a, b_a, w_x, b_x, lam)


def _scan(a3, b3, reverse, name):
    nb, seq, w = a3.shape
    tc = min(TC_SCAN, seq)
    nchunk = seq // tc
    ntile = tc // 8

    def combine(av, bv):
        rows = lax.broadcasted_iota(jnp.int32, av.shape, 0)
        for s in (1, 2, 4):
            if reverse:
                keep = rows < 8 - s
                a_sh, b_sh = pltpu.roll(av, 8 - s, 0), pltpu.roll(bv, 8 - s, 0)
            else:
                keep = rows >= s
                a_sh, b_sh = pltpu.roll(av, s, 0), pltpu.roll(bv, s, 0)
            bv = jnp.where(keep, bv + av * b_sh, bv)
            av = jnp.where(keep, av * a_sh, av)
        return av, bv

    def body(a_ref, b_ref, h_ref, carry):
        @pl.when(pl.program_id(0) == 0)
        def _():
            carry[...] = jnp.zeros_like(carry)

        for b in range(nb):
            def tile(j, hprev):
                jj = ntile - 1 - j if reverse else j
                base = pl.multiple_of(jj * 8, 8)
                av, bv = a_ref[b, pl.ds(base, 8), :], b_ref[b, pl.ds(base, 8), :]
                av, bv = combine(av, av * bv if reverse else bv)
                h = bv + av * hprev
                h_ref[b, pl.ds(base, 8), :] = h
                edge = h[0:1, :] if reverse else h[7:8, :]
                return jnp.broadcast_to(edge, (8, w))

            carry[b] = lax.fori_loop(0, ntile, tile, carry[b])

    imap = (lambda i: (0, nchunk - 1 - i, 0)) if reverse else (lambda i: (0, i, 0))
    spec = pl.BlockSpec((nb, tc, w), imap)
    return pl.pallas_call(
        body, name=name, grid=(nchunk,), in_specs=[spec, spec], out_specs=spec,
        out_shape=jax.ShapeDtypeStruct((nb, seq, w), F32),
        scratch_shapes=[pltpu.VMEM((nb, 8, w), F32)],
        compiler_params=_cp(("arbitrary",)),
    )(a3, b3)


def _sgu_mask():
    ti = lax.broadcasted_iota(jnp.int32, (HEAD, HEAD), 0) // SGU_CHUNK
    si = lax.broadcasted_iota(jnp.int32, (HEAD, HEAD), 1) // SGU_CHUNK
    return si <= ti


def _sgu_specs(tm, d_sgu):
    pw = 256
    first_u = (2 * LRU_HEADS * HEAD) // pw
    n_piece = d_sgu // pw
    specs = [pl.BlockSpec((tm, pw), functools.partial(lambda i, k: (i, k), k=first_u + j)) for j in range(2 * n_piece)]
    return specs, n_piece


def _sgu_fwd(proj, w_sp, b_sp_t, ln_g, ln_b):
    t = proj.shape[0]
    d_sgu = SGU_GROUPS * HEAD
    tm = min(TM_SGU, t)
    nblk = tm // HEAD
    specs, n_piece = _sgu_specs(tm, d_sgu)

    def body(*refs):
        u = jnp.concatenate([r[...] for r in refs[:n_piece]], axis=1)
        v = jnp.concatenate([r[...] for r in refs[n_piece:2 * n_piece]], axis=1)
        w_ref, bt_ref, g_ref, b_ref, y_ref = refs[2 * n_piece:]
        ug = _gelu(u)
        xhat, _ = _ln_stats(_gelu(v))
        vn = (xhat * g_ref[...] + b_ref[...]).astype(BF16)
        mask = _sgu_mask()
        for g in range(SGU_GROUPS):
            wm = jnp.where(mask, w_ref[g], 0.0).astype(BF16)
            cols = slice(g * HEAD, (g + 1) * HEAD)
            for n in range(nblk):
                rows = slice(n * HEAD, (n + 1) * HEAD)
                mixed = jnp.dot(wm, vn[rows, cols], preferred_element_type=F32) + bt_ref[:, g:g + 1]
                y_ref[rows, cols] = (ug[rows, cols] * mixed).astype(BF16)

    full = lambda shape: pl.BlockSpec(shape, lambda i: (0,) * len(shape))
    return pl.pallas_call(
        body, name="sgu_fwd", grid=(t // tm,),
        in_specs=specs + [full(w_sp.shape), full(b_sp_t.shape), full(ln_g.shape), full(ln_b.shape)],
        out_specs=pl.BlockSpec((tm, d_sgu), lambda i: (i, 0)),
        out_shape=jax.ShapeDtypeStruct((t, d_sgu), BF16),
        compiler_params=_cp(("arbitrary",)),
    )(*([proj] * (2 * n_piece)), w_sp, b_sp_t, ln_g, ln_b)


def _mix_fwd(hs, proj, y_sgu, x2, modv, w_o_lru_g, w_o_sgu_g, w_out_g, ln1_g, ln1_b, seq):
    t, d = x2.shape
    w = hs.shape[1]
    d_sgu = y_sgu.shape[1]
    nq, _, ns = w_o_sgu_g.shape
    tm = min(TM_MIX, seq)
    tpb = seq // tm

    def body(hs_ref, gl_ref, ys_ref, ga_ref, gb_ref, x_ref, mod_ref, wl_ref, ws_ref, wo_ref, g1_ref, b1_ref,
             yap_ref, ya_ref, yb_ref, mg_ref, mix_ref, x1_ref):
        yap = (hs_ref[...] * _gelu(gl_ref[...])).astype(BF16)
        yap_ref[...] = yap
        y_a = jnp.dot(yap, wl_ref[...], preferred_element_type=F32)
        ys = ys_ref[...]
        y_b = jnp.concatenate([jnp.dot(ys, ws_ref[q], preferred_element_type=F32) for q in range(nq)], axis=1)
        ya_ref[...] = y_a.astype(BF16)
        yb_ref[...] = y_b.astype(BF16)
        merged = (_sigmoid(ga_ref[...]) * y_a + _sigmoid(gb_ref[...]) * y_b).astype(BF16)
        mg_ref[...] = merged
        mix = jnp.dot(merged, wo_ref[...], preferred_element_type=F32)
        mix_ref[...] = mix
        xhat, _ = _ln_stats(ALPHA * x_ref[...] + (1.0 + mod_ref[2:3, :]) * mix)
        x1_ref[...] = xhat * g1_ref[...] + b1_ref[...]

    row = lambda width, col: pl.BlockSpec((tm, width), functools.partial(lambda i, k: (i, k), k=col))
    full = lambda shape: pl.BlockSpec(shape, lambda i: (0,) * len(shape))
    return pl.pallas_call(
        body, name="mix_fwd", grid=(t // tm,),
        in_specs=[row(w, 0), row(w, 1), row(d_sgu, 0), row(d, 4), row(d, 5), row(d, 0),
                  pl.BlockSpec((None, 8, d), lambda i: (i // tpb, 0, 0)),
                  full(w_o_lru_g.shape), full(w_o_sgu_g.shape), full(w_out_g.shape), full(ln1_g.shape), full(ln1_b.shape)],
        out_specs=[row(w, 0), row(d, 0), row(d, 0), row(d, 0), row(d, 0), row(d, 0)],
        out_shape=[jax.ShapeDtypeStruct((t, w), BF16), jax.ShapeDtypeStruct((t, d), BF16),
                   jax.ShapeDtypeStruct((t, d), BF16), jax.ShapeDtypeStruct((t, d), BF16),
                   jax.ShapeDtypeStruct((t, d), F32), jax.ShapeDtypeStruct((t, d), F32)],
        compiler_params=_cp(("arbitrary",)),
    )(hs, proj, y_sgu, proj, proj, x2, modv, w_o_lru_g, w_o_sgu_g, w_out_g, ln1_g, ln1_b)


def _mlp_fwd(x1, modv, w_up_g, w_down_g, ln2_g, ln2_b, target, nb, seq):
    t, d = x1.shape
    nq, _, ns = w_up_g.shape
    tm = min(TM_MLP, seq)
    tpb = seq // tm
    nt = t // tm

    def body(x1_ref, mod_ref, wu_ref, wd_ref, g2_ref, b2_ref, tg_ref,
             up_ref, h2_ref, dz2_ref, df_ref, st_ref, pb_ref, h2_s, acc):
        i, j = pl.program_id(0), pl.program_id(1)

        @pl.when(j == 0)
        def _():
            h2 = (x1_ref[...] * (1.0 + mod_ref[4:5, :]) + mod_ref[3:4, :]).astype(BF16)
            h2_s[...] = h2
            h2_ref[...] = h2
            acc[...] = jnp.zeros_like(acc)

        @pl.when((i == 0) & (j == 0))
        def _():
            st_ref[...] = jnp.zeros_like(st_ref)

        @pl.when((i % tpb == 0) & (j == 0))
        def _():
            pb_ref[...] = jnp.zeros_like(pb_ref)

        up = jnp.dot(h2_s[...], wu_ref[...], preferred_element_type=F32)
        up_ref[...] = up
        r = jnp.maximum(up, 0.0)
        acc[...] += jnp.dot((r * r).astype(BF16), wd_ref[...], preferred_element_type=F32)

        @pl.when(j == nq - 1)
        def _():
            f = acc[...]
            xhat, rstd = _ln_stats(ALPHA * x1_ref[...] + (1.0 + mod_ref[5:6, :]) * f)
            y = xhat * g2_ref[...] + b2_ref[...]
            err = y - tg_ref[...]
            dy = err * (1.0 / d)
            dz2 = _ln_bwd(dy * g2_ref[...], xhat, rstd)
            dz2_ref[...] = dz2
            df_ref[...] = ((1.0 + mod_ref[5:6, :]) * dz2).astype(BF16)
            st_ref[0:1, :] += _colsum(dy * xhat)
            st_ref[1:2, :] += _colsum(dy)
            st_ref[2:3, :] += (0.5 / d) * jnp.sum(_colsum(err * err), axis=1, keepdims=True)
            pb_ref[0:1, :] += _colsum(dz2 * f)

    tok = lambda i, j: (i, 0)
    return pl.pallas_call(
        body, name="mlp_fwd", grid=(nt, nq),
        in_specs=[pl.BlockSpec((tm, d), tok), pl.BlockSpec((None, 8, d), lambda i, j: (i // tpb, 0, 0)),
                  pl.BlockSpec((None, d, ns), lambda i, j: (j, 0, 0)), pl.BlockSpec((ns, d), lambda i, j: (j, 0)),
                  pl.BlockSpec((1, d), lambda i, j: (0, 0)), pl.BlockSpec((1, d), lambda i, j: (0, 0)),
                  pl.BlockSpec((tm, d), tok)],
        out_specs=[pl.BlockSpec((tm, ns), lambda i, j: (i, j)), pl.BlockSpec((tm, d), tok),
                   pl.BlockSpec((tm, d), tok), pl.BlockSpec((tm, d), tok),
                   pl.BlockSpec((8, d), lambda i, j: (0, 0)), pl.BlockSpec((None, 8, d), lambda i, j: (i // tpb, 0, 0))],
        out_shape=[jax.ShapeDtypeStruct((t, nq * ns), F32), jax.ShapeDtypeStruct((t, d), BF16),
                   jax.ShapeDtypeStruct((t, d), F32), jax.ShapeDtypeStruct((t, d), BF16),
                   jax.ShapeDtypeStruct((8, d), F32), jax.ShapeDtypeStruct((nb, 8, d), F32)],
        scratch_shapes=[pltpu.VMEM((tm, d), BF16), pltpu.VMEM((tm, d), F32)],
        compiler_params=_cp(("arbitrary", "arbitrary")),
    )(x1, modv, w_up_g, w_down_g, ln2_g, ln2_b, target)


def _mlp_bwd(df, up, w_down_g, w_up_g, dz2, x2, mix, modv, ln1_g, ln1_b, nb, seq):
    t, d = x2.shape
    nq, _, ns = w_up_g.shape
    tm = min(TM_MLP, seq)
    tpb = seq // tm

    def body(df_ref, up_ref, wd_ref, wu_ref, dz2_ref, x_ref, mix_ref, mod_ref, g1_ref, b1_ref,
             dup_ref, act_ref, dz1_ref, dmix_ref, st_ref, pb_ref, acc):
        i, j = pl.program_id(0), pl.program_id(1)

        @pl.when(j == 0)
        def _():
            acc[...] = jnp.zeros_like(acc)

        @pl.when((i == 0) & (j == 0))
        def _():
            st_ref[...] = jnp.zeros_like(st_ref)

        @pl.when((i % tpb == 0) & (j == 0))
        def _():
            pb_ref[...] = jnp.zeros_like(pb_ref)

        r = jnp.maximum(up_ref[...], 0.0)
        act_ref[...] = (r * r).astype(BF16)
        dup = (_mm_nt(df_ref[...], wd_ref[...]) * (2.0 * r)).astype(BF16)
        dup_ref[...] = dup
        acc[...] += _mm_nt(dup, wu_ref[...])

        @pl.when(j == nq - 1)
        def _():
            dh2 = acc[...]
            mix = mix_ref[...]
            xhat, rstd = _ln_stats(ALPHA * x_ref[...] + (1.0 + mod_ref[2:3, :]) * mix)
            x1 = xhat * g1_ref[...] + b1_ref[...]
            dx1 = ALPHA * dz2_ref[...] + dh2 * (1.0 + mod_ref[4:5, :])
            dz1 = _ln_bwd(dx1 * g1_ref[...], xhat, rstd)
            dz1_ref[...] = dz1
            dmix_ref[...] = ((1.0 + mod_ref[2:3, :]) * dz1).astype(BF16)
            st_ref[0:1, :] += _colsum(dx1 * xhat)
            st_ref[1:2, :] += _colsum(dx1)
            pb_ref[0:1, :] += _colsum(dh2 * x1)
            pb_ref[1:2, :] += _colsum(dh2)
            pb_ref[2:3, :] += _colsum(dz1 * mix)

    tok = lambda i, j: (i, 0)
    chunk = lambda i, j: (i, j)
    return pl.pallas_call(
        body, name="mlp_bwd", grid=(t // tm, nq),
        in_specs=[pl.BlockSpec((tm, d), tok), pl.BlockSpec((tm, ns), chunk),
                  pl.BlockSpec((ns, d), lambda i, j: (j, 0)), pl.BlockSpec((None, d, ns), lambda i, j: (j, 0, 0)),
                  pl.BlockSpec((tm, d), tok), pl.BlockSpec((tm, d), tok), pl.BlockSpec((tm, d), tok),
                  pl.BlockSpec((None, 8, d), lambda i, j: (i // tpb, 0, 0)),
                  pl.BlockSpec((1, d), lambda i, j: (0, 0)), pl.BlockSpec((1, d), lambda i, j: (0, 0))],
        out_specs=[pl.BlockSpec((tm, ns), chunk), pl.BlockSpec((tm, ns), chunk),
                   pl.BlockSpec((tm, d), tok), pl.BlockSpec((tm, d), tok),
                   pl.BlockSpec((8, d), lambda i, j: (0, 0)), pl.BlockSpec((None, 8, d), lambda i, j: (i // tpb, 0, 0))],
        out_shape=[jax.ShapeDtypeStruct((t, nq * ns), BF16), jax.ShapeDtypeStruct((t, nq * ns), BF16),
                   jax.ShapeDtypeStruct((t, d), F32), jax.ShapeDtypeStruct((t, d), BF16),
                   jax.ShapeDtypeStruct((8, d), F32), jax.ShapeDtypeStruct((nb, 8, d), F32)],
        scratch_shapes=[pltpu.VMEM((tm, d), F32)],
        compiler_params=_cp(("arbitrary", "arbitrary")),
    )(df, up, w_down_g, w_up_g, dz2, x2, mix, modv, ln1_g, ln1_b)


def _mix_bwd(dmix, proj, y_a, y_b, hs, w_out_g, w_o_lru_g, w_o_sgu_g, seq):
    t, d = dmix.shape
    w = hs.shape[1]
    nq, d_sgu, ns = w_o_sgu_g.shape
    tm = min(TM_MIX, seq)

    def body(dmix_ref, ga_ref, gb_ref, ya_ref, yb_ref, gl_ref, hs_ref, wo_ref, wl_ref, ws_ref,
             dya_ref, dyb_ref, dga_ref, dgb_ref, dgl_ref, dyl_ref, dys_ref):
        dmerged = _mm_nt(dmix_ref[...], wo_ref[...])
        sa, sb = _sigmoid(ga_ref[...]), _sigmoid(gb_ref[...])
        dy_a = (dmerged * sa).astype(BF16)
        dy_b = (dmerged * sb).astype(BF16)
        dya_ref[...] = dy_a
        dyb_ref[...] = dy_b
        dga_ref[...] = (dmerged * ya_ref[...].astype(F32) * (sa * (1.0 - sa))).astype(BF16)
        dgb_ref[...] = (dmerged * yb_ref[...].astype(F32) * (sb * (1.0 - sb))).astype(BF16)
        dyap = _mm_nt(dy_a, wl_ref[...])
        gel, dgel = _gelu_and_grad(gl_ref[...])
        dyl_ref[...] = dyap * gel
        dgl_ref[...] = (dyap * hs_ref[...] * dgel).astype(BF16)
        dys = _mm_nt(dy_b[:, 0:ns], ws_ref[0])
        for q in range(1, nq):
            dys = dys + _mm_nt(dy_b[:, q * ns:(q + 1) * ns], ws_ref[q])
        dys_ref[...] = dys

    row = lambda width, col: pl.BlockSpec((tm, width), functools.partial(lambda i, k: (i, k), k=col))
    full = lambda shape: pl.BlockSpec(shape, lambda i: (0,) * len(shape))
    return pl.pallas_call(
        body, name="mix_bwd", grid=(t // tm,),
        in_specs=[row(d, 0), row(d, 4), row(d, 5), row(d, 0), row(d, 0), row(w, 1), row(w, 0),
                  full(w_out_g.shape), full(w_o_lru_g.shape), full(w_o_sgu_g.shape)],
        out_specs=[row(d, 0), row(d, 0), row(d, 0), row(d, 0), row(w, 0), row(w, 0), row(d_sgu, 0)],
        out_shape=[jax.ShapeDtypeStruct((t, d), BF16), jax.ShapeDtypeStruct((t, d), BF16),
                   jax.ShapeDtypeStruct((t, d), BF16), jax.ShapeDtypeStruct((t, d), BF16),
                   jax.ShapeDtypeStruct((t, w), BF16), jax.ShapeDtypeStruct((t, w), F32),
                   jax.ShapeDtypeStruct((t, d_sgu), F32)],
        compiler_params=_cp(("arbitrary",)),
    )(dmix, proj, proj, y_a, y_b, proj, hs, w_out_g, w_o_lru_g, w_o_sgu_g)


def _sgu_bwd(proj, dys, w_sp, b_sp_t, ln_g, ln_b):
    t = proj.shape[0]
    d_sgu = SGU_GROUPS * HEAD
    tm = min(TM_SGU, t)
    nblk = tm // HEAD
    specs, n_piece = _sgu_specs(tm, d_sgu)

    def body(*refs):
        u = jnp.concatenate([r[...] for r in refs[:n_piece]], axis=1)
        v = jnp.concatenate([r[...] for r in refs[n_piece:2 * n_piece]], axis=1)
        dys_ref, w_ref, bt_ref, g_ref, b_ref, du_ref, dv_ref, dw_ref, st_ref, dbt_ref, dvn_s = refs[2 * n_piece:]

        @pl.when(pl.program_id(0) == 0)
        def _():
            dw_ref[...] = jnp.zeros_like(dw_ref)
            st_ref[...] = jnp.zeros_like(st_ref)
            dbt_ref[...] = jnp.zeros_like(dbt_ref)

        ug, dug_du = _gelu_and_grad(u)
        vg, dvg_dv = _gelu_and_grad(v)
        xhat, rstd = _ln_stats(vg)
        vn = (xhat * g_ref[...] + b_ref[...]).astype(BF16)
        dys_v = dys_ref[...]
        mask = _sgu_mask()
        for g in range(SGU_GROUPS):
            wm = jnp.where(mask, w_ref[g], 0.0).astype(BF16)
            cols = slice(g * HEAD, (g + 1) * HEAD)
            dw_g = jnp.zeros((HEAD, HEAD), F32)
            db_g = jnp.zeros((HEAD, 1), F32)
            for n in range(nblk):
                rows = slice(n * HEAD, (n + 1) * HEAD)
                vn_blk = vn[rows, cols]
                mixed = jnp.dot(wm, vn_blk, preferred_element_type=F32) + bt_ref[:, g:g + 1]
                dy_blk = dys_v[rows, cols]
                du_ref[rows, cols] = (dy_blk * mixed * dug_du[rows, cols]).astype(BF16)
                dmx = dy_blk * ug[rows, cols]
                dvn_s[rows, cols] = _mm_tn(wm, dmx)
                dw_g = dw_g + _mm_nt(dmx, vn_blk)
                db_g = db_g + jnp.sum(dmx, axis=1, keepdims=True)
            dw_ref[g] += jnp.where(mask, dw_g, 0.0)
            dbt_ref[:, g:g + 1] += db_g
        dvn = dvn_s[...]
        st_ref[0:1, :] += _colsum(dvn * xhat)
        st_ref[1:2, :] += _colsum(dvn)
        dv_ref[...] = (_ln_bwd(dvn * g_ref[...], xhat, rstd) * dvg_dv).astype(BF16)

    full = lambda shape: pl.BlockSpec(shape, lambda i: (0,) * len(shape))
    tok = pl.BlockSpec((tm, d_sgu), lambda i: (i, 0))
    return pl.pallas_call(
        body, name="sgu_bwd", grid=(t // tm,),
        in_specs=specs + [tok, full(w_sp.shape), full(b_sp_t.shape), full(ln_g.shape), full(ln_b.shape)],
        out_specs=[tok, tok, full(w_sp.shape), full((8, d_sgu)), full((HEAD, HEAD))],
        out_shape=[jax.ShapeDtypeStruct((t, d_sgu), BF16), jax.ShapeDtypeStruct((t, d_sgu), BF16),
                   jax.ShapeDtypeStruct(w_sp.shape, F32), jax.ShapeDtypeStruct((8, d_sgu), F32),
                   jax.ShapeDtypeStruct((HEAD, HEAD), F32)],
        scratch_shapes=[pltpu.VMEM((tm, d_sgu), F32)],
        compiler_params=_cp(("arbitrary",)),
    )(*([proj] * (2 * n_piece)), dys, w_sp, b_sp_t, ln_g, ln_b)


def _lru_bwd(proj, hs, e, dyl, lru_w, nb, seq):
    t = proj.shape[0]
    w = LRU_HEADS * HEAD
    w_conv, b_conv, w_a, b_a, w_x, b_x, lam = lru_w

    def body(x_ref, hs_ref, e_ref, dy_ref, wc_ref, bc_ref, wa_ref, ba_ref, wx_ref, bx_ref, lam_ref,
             dxl_ref, dwa_ref, dwx_ref, st_ref):
        @pl.when(pl.program_id(1) == 0)
        def _():
            dwa_ref[...] = jnp.zeros_like(dwa_ref)
            dwx_ref[...] = jnp.zeros_like(dwx_ref)
            st_ref[...] = jnp.zeros_like(st_ref)

        xl = x_ref[...]
        xc, r, gi, big_l, a, mult = _lru_gates(xl, wc_ref, bc_ref, wa_ref, ba_ref, wx_ref, bx_ref, lam_ref)
        dh = dy_ref[...] + _shift_up(e_ref[...], 1)
        da = dh * _shift_down(hs_ref[...], 1)
        dmult = dh * (gi * xc)
        d_i = dh * (mult * xc)
        dxc = dh * (mult * gi)
        a2 = a * a
        dla = da * a - dmult * (a2 / mult)
        dr = dla * big_l
        d_big_l = _colsum(dla * r)
        dra = dr * (r * (1.0 - r))
        dia = d_i * (gi * (1.0 - gi))
        dwa_ref[...] += _mm_tn(xc, dra)
        dwx_ref[...] += _mm_tn(xc, dia)
        dxc = dxc + _mm_nt(dra, wa_ref[...]) + _mm_nt(dia, wx_ref[...])
        dxl = wc_ref[CONV_WIDTH - 1:CONV_WIDTH, :] * dxc
        for k in range(CONV_WIDTH - 1):
            dxl = dxl + wc_ref[k:k + 1, :] * _shift_up(dxc, CONV_WIDTH - 1 - k)
        dxl_ref[...] = dxl.astype(BF16)
        st_ref[0:1, :] += _colsum(dra)
        st_ref[1:2, :] += _colsum(dia)
        st_ref[2:3, :] += d_big_l * (LRU_C * _sigmoid(-lam_ref[...]))
        st_ref[3:4, :] += _colsum(dxc)
        for k in range(CONV_WIDTH):
            st_ref[4 + k:5 + k, :] += _colsum(dxc * _shift_down(xl, CONV_WIDTH - 1 - k))

    col = lambda hd, b: (0, hd)
    head = lambda hd, b: (hd, 0, 0)
    tok = lambda hd, b: (b, hd)
    seq_blk = pl.BlockSpec((seq, HEAD), tok)
    return pl.pallas_call(
        body, name="lru_bwd", grid=(LRU_HEADS, nb),
        in_specs=[seq_blk, seq_blk, seq_blk, seq_blk,
                  pl.BlockSpec((CONV_WIDTH, HEAD), col), pl.BlockSpec((1, HEAD), col),
                  pl.BlockSpec((None, HEAD, HEAD), head), pl.BlockSpec((1, HEAD), col),
                  pl.BlockSpec((None, HEAD, HEAD), head), pl.BlockSpec((1, HEAD), col),
                  pl.BlockSpec((1, HEAD), col)],
        out_specs=[seq_blk, pl.BlockSpec((None, HEAD, HEAD), head), pl.BlockSpec((None, HEAD, HEAD), head),
                   pl.BlockSpec((8, HEAD), col)],
        out_shape=[jax.ShapeDtypeStruct((t, w), BF16), jax.ShapeDtypeStruct((LRU_HEADS, HEAD, HEAD), F32),
                   jax.ShapeDtypeStruct((LRU_HEADS, HEAD, HEAD), F32), jax.ShapeDtypeStruct((8, w), F32)],
        compiler_params=_cp(("arbitrary", "arbitrary")),
    )(proj, hs, e, dyl, w_conv, b_conv, w_a, b_a, w_x, b_x, lam)


def _weight_grad(a, g, col_shards, name):
    t, k = a.shape
    n = g.shape[1]
    tt = min(TT_DW, t)
    tk = min(k, 1024)
    while k % tk:
        tk //= 2
    ns = n // N_CHIPS if col_shards else n
    tn = min(ns, 768 if ns % 768 == 0 else 1024)
    while ns % tn:
        tn //= 2
    per = ns // tn

    def body(a_ref, g_ref, o_ref):
        @pl.when(pl.program_id(2) == 0)
        def _():
            o_ref[...] = jnp.zeros_like(o_ref)

        o_ref[...] += _mm_tn(a_ref[...], g_ref[...])

    if col_shards:
        out_spec = pl.BlockSpec((None, tk, tn), lambda i, j, s: (j // per, i, j % per))
        out_shape = jax.ShapeDtypeStruct((N_CHIPS, k, ns), F32)
    else:
        out_spec = pl.BlockSpec((tk, tn), lambda i, j, s: (i, j))
        out_shape = jax.ShapeDtypeStruct((k, n), F32)
    return pl.pallas_call(
        body, name=name, grid=(k // tk, n // tn, t // tt),
        in_specs=[pl.BlockSpec((tt, tk), lambda i, j, s: (s, i)), pl.BlockSpec((tt, tn), lambda i, j, s: (s, j))],
        out_specs=out_spec, out_shape=out_shape,
        compiler_params=_cp(("arbitrary", "arbitrary", "arbitrary")),
    )(a, g)


def _input_grad(dproj, w_in_g, dz1, x2, modv, nb, seq):
    t, d = x2.shape
    nq, _, ns = w_in_g.shape
    tm = min(TM_DH, seq)
    tpb = seq // tm

    def body(dp_ref, w_ref, dz1_ref, x_ref, mod_ref, gx_ref, db_ref, pb_ref, acc):
        i, q = pl.program_id(0), pl.program_id(1)

        @pl.when(q == 0)
        def _():
            acc[...] = jnp.zeros_like(acc)

        @pl.when((i == 0) & (q == 0))
        def _():
            db_ref[...] = jnp.zeros_like(db_ref)

        @pl.when((i % tpb == 0) & (q == 0))
        def _():
            pb_ref[...] = jnp.zeros_like(pb_ref)

        dp = dp_ref[...]
        acc[...] += _mm_nt(dp, w_ref[...])
        db_ref[q, 0:1, :] += _colsum(dp.astype(F32))

        @pl.when(q == nq - 1)
        def _():
            dh = acc[...]
            gx_ref[...] = ALPHA * dz1_ref[...] + dh * (1.0 + mod_ref[1:2, :])
            pb_ref[0:1, :] += _colsum(dh * x_ref[...])
            pb_ref[1:2, :] += _colsum(dh)

    tok = lambda i, q: (i, 0)
    return pl.pallas_call(
        body, name="input_grad", grid=(t // tm, nq),
        in_specs=[pl.BlockSpec((tm, ns), lambda i, q: (i, q)), pl.BlockSpec((None, d, ns), lambda i, q: (q, 0, 0)),
                  pl.BlockSpec((tm, d), tok), pl.BlockSpec((tm, d), tok),
                  pl.BlockSpec((None, 8, d), lambda i, q: (i // tpb, 0, 0))],
        out_specs=[pl.BlockSpec((tm, d), tok), pl.BlockSpec((nq, 8, ns), lambda i, q: (0, 0, 0)),
                   pl.BlockSpec((None, 8, d), lambda i, q: (i // tpb, 0, 0))],
        out_shape=[jax.ShapeDtypeStruct((t, d), F32), jax.ShapeDtypeStruct((nq, 8, ns), F32),
                   jax.ShapeDtypeStruct((nb, 8, d), F32)],
        scratch_shapes=[pltpu.VMEM((tm, d), F32)],
        compiler_params=_cp(("arbitrary", "arbitrary")),
    )(dproj, w_in_g, dz1, x2, modv)


def _rows128(v):
    flat = v.reshape(-1, HEAD)
    pad = (-flat.shape[0]) % 8
    return jnp.pad(flat, ((0, pad), (0, 0))) if pad else flat


def kernel(x, c, w_ada, b_ada, w_in, b_in, w_conv, b_conv, w_rg_a, b_rg_a, w_rg_x, b_rg_x, lru_lambda, w_sp, b_sp, ln_v_g, ln_v_b, w_o_lru, w_o_sgu, w_out, ln1_g, ln1_b, w_up, w_down, ln2_g, ln2_b, loss_target, m_w_ada, m_b_ada, m_w_in, m_b_in, m_w_conv, m_b_conv, m_w_rg_a, m_b_rg_a, m_w_rg_x, m_b_rg_x, m_lru_lambda, m_w_sp, m_b_sp, m_ln_v_g, m_ln_v_b, m_w_o_lru, m_w_o_sgu, m_w_out, m_ln1_g, m_ln1_b, m_w_up, m_w_down, m_ln2_g, m_ln2_b, v_w_ada, v_b_ada, v_w_in, v_b_in, v_w_conv, v_b_conv, v_w_rg_a, v_b_rg_a, v_w_rg_x, v_b_rg_x, v_lru_lambda, v_w_sp, v_b_sp, v_ln_v_g, v_ln_v_b, v_w_o_lru, v_w_o_sgu, v_w_out, v_ln1_g, v_ln1_b, v_w_up, v_w_down, v_ln2_g, v_ln2_b):
    given = dict(locals())
    nb, seq, d = x.shape
    t = nb * seq
    w_lru = LRU_HEADS * HEAD
    d_sgu = SGU_GROUPS * HEAD
    xi, yi, ci = lax.axis_index("x"), lax.axis_index("y"), lax.axis_index("c")
    chip = 2 * xi + yi
    dev = 2 * chip + ci
    cidx = jnp.reshape(ci, (1,)).astype(jnp.int32)

    x2 = x.reshape(t, d)
    target = loss_target.reshape(t, d)

    big = ["w_in", "w_o_lru", "w_o_sgu", "w_out", "w_up", "w_down"]
    shards_a = [w_in[0].astype(BF16)]
    shards_b = [given[n][0].astype(BF16) for n in big[1:]]
    started_a = _gather_weights_start(shards_a, "gather_w_in_start")
    started_b = _gather_weights_start(shards_b, "gather_w_rest_start", after=(started_a[-1],))
    tok = started_b[-1][0:1, 0:1]

    c_rows = _rows128(c)
    wconv_rows = _rows128(w_conv[0])
    slab0 = _all_gather_small(jnp.concatenate([c_rows, wconv_rows], axis=0) + tok, "gather_c_wconv")
    slab0 = slab0.reshape(N_DEV, -1, HEAD)
    c_all = slab0[:, :c_rows.shape[0]].reshape(N_DEV * nb, d)
    n_wc = CONV_WIDTH * (w_lru // N_CHIPS) // HEAD
    wc = slab0[0::2, c_rows.shape[0]:c_rows.shape[0] + n_wc].reshape(N_CHIPS, CONV_WIDTH, w_lru // N_CHIPS)
    w_conv_full = jnp.transpose(wc, (1, 0, 2)).reshape(CONV_WIDTH, w_lru)

    n_ada = w_ada.shape[2]
    b_ada_cols = lax.dynamic_slice(b_ada, (0, chip * n_ada), (1, n_ada))
    mod_cols = _ada_fwd(c_all, w_ada[0], b_ada_cols)
    half = (N_DEV * nb) // 2
    mod_half = lax.dynamic_slice(mod_cols, (ci * half, 0), (half, n_ada))
    mod_g = _all_gather_small(mod_half, "gather_mod").reshape(N_CHIPS, 2, half, n_ada)
    mod_all = jnp.transpose(mod_g, (1, 2, 0, 3)).reshape(N_DEV * nb, N_CHIPS * n_ada)
    mod_loc = lax.dynamic_slice(mod_all, (dev * nb, 0), (nb, N_CHIPS * n_ada)).reshape(nb, 6, d)
    modv = jnp.pad(mod_loc, ((0, 0), (0, 2), (0, 0)))

    lru_w = (w_conv_full, b_conv, w_rg_a[0], b_rg_a, w_rg_x[0], b_rg_x, lru_lambda)
    b_sp_t = jnp.transpose(b_sp[0])

    (w_in_g,) = _fill_own_slot(_gather_weights_wait(started_a, shards_a, (modv,), "gather_w_in_wait"), shards_a, "own_w_in")
    proj, h = _proj_fwd(x2, modv, w_in_g, b_in, seq)
    a, inp = _lru_prep(proj, lru_w, nb, seq)
    a3 = a.reshape(nb, seq, w_lru)
    hs = _scan(a3, inp.reshape(nb, seq, w_lru), False, "lru_scan").reshape(t, w_lru)
    y_sgu = _sgu_fwd(proj, w_sp[0], b_sp_t, ln_v_g, ln_v_b)
    w_o_lru_g, w_o_sgu_g, w_out_g, w_up_g, w_down_g = _fill_own_slot(
        _gather_weights_wait(started_b, shards_b, (hs, y_sgu), "gather_w_rest_wait"), shards_b, "own_w_rest")
    w_o_lru_g = w_o_lru_g.reshape(w_lru, d)
    w_out_g = w_out_g.reshape(d, d)
    w_down_g = w_down_g.reshape(-1, d)
    yap, y_a, y_b, merged, mix, x1 = _mix_fwd(hs, proj, y_sgu, x2, modv, w_o_lru_g, w_o_sgu_g, w_out_g, ln1_g, ln1_b, seq)
    up, h2, dz2, df, st2, pb2 = _mlp_fwd(x1, modv, w_up_g, w_down_g, ln2_g, ln2_b, target, nb, seq)
    loss = lax.psum(st2[2, 0], ("x", "y", "c"))

    dup, act, dz1, dmix, st1, pb1 = _mlp_bwd(df, up, w_down_g, w_up_g, dz2, x2, mix, modv, ln1_g, ln1_b, nb, seq)
    dy_a, dy_b, dga, dgb, dgl, dyl, dys = _mix_bwd(dmix, proj, y_a, y_b, hs, w_out_g, w_o_lru_g, w_o_sgu_g, seq)
    du, dv, g_w_sp, st_sgu, g_b_sp_t = _sgu_bwd(proj, dys, w_sp[0], b_sp_t, ln_v_g, ln_v_b)
    dyl3 = dyl.reshape(nb, seq, w_lru)
    e = _scan(a3, dyl3, True, "lru_scan_bwd").reshape(t, w_lru)
    dxl, g_w_rg_a, g_w_rg_x, st_lru = _lru_bwd(proj, hs, e, dyl, lru_w, nb, seq)
    dproj = jnp.concatenate([dxl, dgl, du, dv, dga, dgb], axis=1)
    grad_x2, g_b_in4, pb0 = _input_grad(dproj, w_in_g, dz1, x2, modv, nb, seq)

    part = {
        "w_in": _weight_grad(h, dproj, True, "grad_w_in"),
        "w_o_lru": _weight_grad(yap, dy_a, False, "grad_w_o_lru"),
        "w_o_sgu": _weight_grad(y_sgu, dy_b, True, "grad_w_o_sgu"),
        "w_out": _weight_grad(merged, dmix, False, "grad_w_out"),
        "w_up": _weight_grad(h2, dup, True, "grad_w_up"),
        "w_down": _weight_grad(act, df, False, "grad_w_down"),
    }

    g4 = []
    for n in big:
        shard = given[n].shape[1:]
        g4.append(part[n].reshape(N_CHIPS, 2, shard[0] // 2, shard[1]))
    recv = _sibling_swap_halves(g4)
    chip_part = [_add_own_half(g4[k], recv[k], cidx, "grad_pair_sum_" + big[k]) for k in range(len(big))]
    slots = _chip_exchange(chip_part)
    halves = [_sum_slots(slots[k], "grad_chip_sum_" + big[k]) for k in range(len(big))]
    theirs = _sibling_swap(halves)
    grads = {}

    dmod_loc = jnp.stack([pb0[:, 1], pb0[:, 0], pb1[:, 2], pb1[:, 1], pb1[:, 0], pb2[:, 0]], axis=1)
    small = [
        ("dmod", dmod_loc),
        ("b_in", g_b_in4[:, 0]), ("w_conv", st_lru[4:8]), ("b_conv", st_lru[3]),
        ("w_rg_a", g_w_rg_a), ("b_rg_a", st_lru[0]), ("w_rg_x", g_w_rg_x), ("b_rg_x", st_lru[1]),
        ("lru_lambda", st_lru[2]), ("w_sp", g_w_sp), ("b_sp", jnp.transpose(g_b_sp_t[:, :SGU_GROUPS])),
        ("ln_v_g", st_sgu[0]), ("ln_v_b", st_sgu[1]), ("ln1_g", st1[0]), ("ln1_b", st1[1]),
        ("ln2_g", st2[0]), ("ln2_b", st2[1]),
    ]
    pieces = [_rows128(v) for _, v in small]
    slab = jnp.concatenate(pieces, axis=0)
    slab = jnp.pad(slab, ((0, (-slab.shape[0]) % TR_EW), (0, 0)))
    n_rows = slab.shape[0]
    gathered = _all_gather_small(slab, "gather_small_grads").reshape(N_DEV, n_rows, HEAD)
    summed = _sum_slots(gathered, "small_grad_sum")
    off = 0
    for (n, v), piece in zip(small, pieces):
        rows = v.size // HEAD
        if n == "dmod":
            dmod_all = gathered[:, off:off + rows].reshape(N_DEV * nb, 6 * d)
        else:
            grads[n] = summed[off:off + rows].reshape(v.shape)
        off += piece.shape[0]

    dmod_cols = lax.dynamic_slice(dmod_all, (0, chip * n_ada), (N_DEV * nb, n_ada))
    grads["w_ada"], grads["b_ada"] = _ada_bwd(c_all, dmod_all, dmod_cols)
    n_wcs = w_lru // N_CHIPS
    grads["w_conv"] = lax.dynamic_slice(grads["w_conv"], (0, chip * n_wcs), (CONV_WIDTH, n_wcs))

    names = ['w_ada', 'b_ada', 'w_in', 'b_in', 'w_conv', 'b_conv', 'w_rg_a', 'b_rg_a', 'w_rg_x', 'b_rg_x', 'lru_lambda',
             'w_sp', 'b_sp', 'ln_v_g', 'ln_v_b', 'w_o_lru', 'w_o_sgu', 'w_out', 'ln1_g', 'ln1_b', 'w_up', 'w_down',
             'ln2_g', 'ln2_b']
    out_g, out_d, out_m, out_v = [], [], [], []
    for n in names:
        wv = given[n]
        shape2 = (-1, wv.shape[-1])
        w2, m2, v2 = wv.reshape(shape2), given["m_" + n].reshape(shape2), given["v_" + n].reshape(shape2)
        if n in big:
            k = big.index(n)
            g2, dlt, nm, nv = _adamw_halves(w2, halves[k], theirs[k], m2, v2, cidx, "adamw_" + n)
        else:
            g2 = grads[n].reshape(wv.shape).reshape(shape2)
            dlt, nm, nv = _adamw(w2, g2, m2, v2, "adamw_" + n)
        out_g.append(g2.reshape(wv.shape))
        out_d.append(dlt.reshape(wv.shape))
        out_m.append(nm.reshape(wv.shape))
        out_v.append(nv.reshape(wv.shape))

    return (loss, grad_x2.reshape(nb, seq, d), *out_g, *out_d, *out_m, *out_v)
```

```python
import functools
import math

import jax
import jax.numpy as jnp
from jax import lax
from jax.experimental import pallas as pl
from jax.experimental.pallas import tpu as pltpu

F32 = jnp.float32
BF16 = jnp.bfloat16
MESH = pl.DeviceIdType.MESH

N_CHIPS = 4
N_DEV = 8
LRU_HEADS = 10
HEAD = 128
SGU_GROUPS = 6
SGU_CHUNK = 64
CONV_WIDTH = 4
LRU_C = 8.0
ALPHA = 2.0 ** 0.25
LN_EPS = 1e-5
ADAM_LR, ADAM_B1, ADAM_B2, ADAM_EPS, ADAM_WD, ADAM_STEP = 0.001, 0.9, 0.999, 1e-08, 0.01, 10

VMEM_LIMIT = 56 * 1024 * 1024
TM_PROJ = 1024
TM_MIX = 256
TM_MLP = 512
TM_SGU = 512
TM_DH = 512
TT_DW = 1024
TC_SCAN = 256
TR_EW = 256


def _cp(sem=None):
    return pltpu.CompilerParams(dimension_semantics=sem, vmem_limit_bytes=VMEM_LIMIT)


def _mm(a, b):
    return jnp.dot(a.astype(BF16), b.astype(BF16), preferred_element_type=F32)


def _mm_nt(a, b):
    return lax.dot_general(a.astype(BF16), b.astype(BF16), (((1,), (1,)), ((), ())), preferred_element_type=F32)


def _mm_tn(a, b):
    return lax.dot_general(a.astype(BF16), b.astype(BF16), (((0,), (0,)), ((), ())), preferred_element_type=F32)


def _sigmoid(x):
    return 1.0 / (1.0 + jnp.exp(-x))


_GELU_K = math.sqrt(2.0 / math.pi)


def _gelu(x):
    t = jnp.tanh(_GELU_K * (x + 0.044715 * (x * x * x)))
    return 0.5 * x * (1.0 + t)


def _gelu_and_grad(x):
    x2 = x * x
    t = jnp.tanh(_GELU_K * (x + 0.044715 * (x2 * x)))
    g = 0.5 * x * (1.0 + t)
    dg = 0.5 * (1.0 + t) + 0.5 * x * (1.0 - t * t) * (_GELU_K * (1.0 + 3.0 * 0.044715 * x2))
    return g, dg


def _ln_stats(z):
    mu = jnp.mean(z, axis=-1, keepdims=True)
    zc = z - mu
    var = jnp.mean(zc * zc, axis=-1, keepdims=True)
    rstd = lax.rsqrt(var + LN_EPS)
    return zc * rstd, rstd


def _ln_bwd(dxh, xhat, rstd):
    m1 = jnp.mean(dxh, axis=-1, keepdims=True)
    m2 = jnp.mean(dxh * xhat, axis=-1, keepdims=True)
    return rstd * (dxh - m1 - xhat * m2)


def _colsum(v):
    return jnp.sum(v, axis=0, keepdims=True)


def _shift_down(v, j):
    if j == 0:
        return v
    rows = lax.broadcasted_iota(jnp.int32, v.shape, 0)
    return jnp.where(rows >= j, pltpu.roll(v, j, 0), 0.0)


def _shift_up(v, j):
    if j == 0:
        return v
    n = v.shape[0]
    rows = lax.broadcasted_iota(jnp.int32, v.shape, 0)
    return jnp.where(rows < n - j, pltpu.roll(v, n - j, 0), 0.0)


def _my_pos():
    return lax.axis_index("x"), lax.axis_index("y"), lax.axis_index("c")


def _all_gather_small(v, name, after=()):
    m_per, n = v.shape

    def body(x_ref, out_ref, send_sems, recv_sems, local_sem):
        x, y, c = _my_pos()
        me, sibling = (x, y, c), (x, y, 1 - c)
        chips = [(1 - x, y), (x, 1 - y), (1 - x, 1 - y)]

        def rows(px, py, pc):
            return out_ref.at[pl.ds((4 * px + 2 * py + pc) * m_per, m_per), :]

        def copy(k, block, to, src=None):
            return pltpu.make_async_remote_copy(
                src_ref=rows(*block) if src is None else src, dst_ref=rows(*block),
                send_sem=send_sems.at[k], recv_sem=recv_sems.at[k], device_id=to, device_id_type=MESH)

        mine = pltpu.make_async_copy(x_ref, rows(*me), local_sem)
        mine.start()
        first = [copy(0, me, sibling, src=x_ref)]
        first += [copy(1 + j, me, (*chip, c), src=x_ref) for j, chip in enumerate(chips)]
        for cp in first:
            cp.start()
        passed = [copy(4 + j, (*chip, c), sibling) for j, chip in enumerate(chips)]
        for j, chip in enumerate(chips):
            copy(1 + j, (*chip, c), me).wait_recv()
            passed[j].start()
        copy(0, sibling, me).wait_recv()
        for j, chip in enumerate(chips):
            copy(4 + j, (*chip, 1 - c), me).wait_recv()
        for cp in first + passed:
            cp.wait_send()
        mine.wait()

    return pl.pallas_call(
        _ordered(body, 1, after), name=name,
        out_shape=jax.ShapeDtypeStruct((N_DEV * m_per, n), v.dtype),
        in_specs=[pl.BlockSpec(memory_space=pltpu.VMEM)] + [pl.BlockSpec(memory_space=pl.ANY)] * len(after),
        out_specs=pl.BlockSpec(memory_space=pltpu.VMEM),
        scratch_shapes=[pltpu.SemaphoreType.DMA((7,)), pltpu.SemaphoreType.DMA((7,)), pltpu.SemaphoreType.DMA],
        compiler_params=pltpu.CompilerParams(vmem_limit_bytes=VMEM_LIMIT),
    )(v, *after)


_HBM = pl.BlockSpec(memory_space=pltpu.HBM)
_ANY = pl.BlockSpec(memory_space=pl.ANY)
_SEM = pl.BlockSpec(memory_space=pltpu.SEMAPHORE)
_EFFECT = pltpu.SideEffectType.DATAFLOW_SIDE_EFFECTING


def _ordered(body, n_in, after):
    k = len(after)
    if not k:
        return body
    return lambda *refs: body(*refs[:n_in], *refs[n_in + k:])


def _gather_copies(ins, lands, send_sems, recv_sems):
    x, y, c = _my_pos()
    p = 2 * x + y
    peers = [(x, 1 - y), (1 - x, y), (1 - x, 1 - y)]
    sends, recvs = [], []
    for k in range(len(ins)):
        for j, (qx, qy) in enumerate(peers):
            sems = dict(send_sem=send_sems.at[3 * k + j], recv_sem=recv_sems.at[3 * k + j],
                        device_id=(qx, qy, c), device_id_type=MESH)
            sends.append(pltpu.make_async_remote_copy(src_ref=ins[k], dst_ref=lands[k].at[p], **sems))
            recvs.append(pltpu.make_async_remote_copy(src_ref=ins[k], dst_ref=lands[k].at[2 * qx + qy], **sems))
    return sends, recvs


def _to_sibling_copies(ins, lands, send_sems, recv_sems):
    x, y, c = _my_pos()
    cps = [pltpu.make_async_remote_copy(
        src_ref=ins[k].at[:, 1 - c], dst_ref=lands[k], send_sem=send_sems.at[k], recv_sem=recv_sems.at[k],
        device_id=(x, y, 1 - c), device_id_type=MESH) for k in range(len(ins))]
    return cps, cps


def _chip_exchange_copies(ins, lands, send_sems, recv_sems):
    x, y, c = _my_pos()
    peers = [(x, 1 - y), (1 - x, y), (1 - x, 1 - y)]
    cps = []
    for k in range(len(ins)):
        for j, (qx, qy) in enumerate(peers):
            cps.append(pltpu.make_async_remote_copy(
                src_ref=ins[k].at[2 * qx + qy], dst_ref=lands[k].at[j], send_sem=send_sems.at[3 * k + j],
                recv_sem=recv_sems.at[3 * k + j], device_id=(qx, qy, c), device_id_type=MESH))
    return cps, cps


def _swap_copies(ins, lands, send_sems, recv_sems):
    x, y, c = _my_pos()
    cps = [pltpu.make_async_remote_copy(
        src_ref=ins[k], dst_ref=lands[k], send_sem=send_sems.at[k], recv_sem=recv_sems.at[k],
        device_id=(x, y, 1 - c), device_id_type=MESH) for k in range(len(ins))]
    return cps, cps


def _split_start(ins, land_shapes, copies, n_sems, name, after=()):
    n, nl = len(ins), len(land_shapes)
    first_out = n + nl + len(after)

    def body(*refs):
        in_refs, land_refs = refs[:n], refs[n:n + nl]
        send_sems, recv_sems = refs[first_out:first_out + 2]
        token = refs[-1]
        sends, _ = copies(in_refs, land_refs, send_sems, recv_sems)
        for cp in sends:
            cp.start()
        token[...] = jnp.zeros_like(token)

    lands = [pltpu.with_memory_space_constraint(lax.empty(s.shape, s.dtype), pltpu.HBM) for s in land_shapes]
    ins = [pltpu.with_memory_space_constraint(s, pltpu.HBM) for s in ins]
    return pl.pallas_call(
        body, name=name,
        out_shape=(pltpu.SemaphoreType.DMA((n_sems,)), pltpu.SemaphoreType.DMA((n_sems,)),
                   *[pltpu.HBM(s.shape, s.dtype) for s in ins], *[pltpu.HBM(s.shape, s.dtype) for s in lands],
                   jax.ShapeDtypeStruct((8, HEAD), F32)),
        in_specs=[_HBM] * (n + nl) + [pl.BlockSpec(memory_space=pl.ANY)] * len(after),
        out_specs=(_SEM, _SEM, *([_HBM] * (n + nl)), pl.BlockSpec(memory_space=pltpu.VMEM)),
        input_output_aliases={k: 2 + k for k in range(n + nl)},
        compiler_params=pltpu.CompilerParams(has_side_effects=_EFFECT),
    )(*ins, *lands, *after)


def _split_wait(started, n, copies, name, after=()):
    send_sems, recv_sems = started[0], started[1]
    bufs = started[2:-1]
    nb = len(bufs)

    def body(*refs):
        in_refs, land_refs = refs[:n], refs[n:nb]
        sends, recvs = copies(in_refs, land_refs, refs[nb], refs[nb + 1])
        for cp in sends:
            cp.wait_send()
        for cp in recvs:
            cp.wait_recv()

    outs = pl.pallas_call(
        body, name=name,
        out_shape=tuple(pltpu.HBM(s.shape, s.dtype) for s in bufs),
        in_specs=[_HBM] * nb + [_SEM, _SEM] + [pl.BlockSpec(memory_space=pl.ANY)] * len(after),
        out_specs=tuple([_HBM] * nb),
        input_output_aliases={k: k for k in range(nb)},
        compiler_params=pltpu.CompilerParams(has_side_effects=_EFFECT),
    )(*bufs, send_sems, recv_sems, *after)
    return list(outs[:n]), list(outs[n:])


def _fill_own_slot(gathered, shards, pidx, names):
    outs = []
    for g, s, name in zip(gathered, shards, names):
        r, cdim = s.shape
        tr = _row_tile(r)

        def body(p_ref, s_ref, g_ref, o_ref):
            o_ref[...] = s_ref[...]

        outs.append(pl.pallas_call(
            body, name=name,
            grid_spec=pltpu.PrefetchScalarGridSpec(
                num_scalar_prefetch=1, grid=(r // tr,),
                in_specs=[pl.BlockSpec((tr, cdim), lambda i, p: (i, 0)), pl.BlockSpec(memory_space=pl.ANY)],
                out_specs=pl.BlockSpec((None, tr, cdim), lambda i, p: (p[0], i, 0))),
            out_shape=jax.ShapeDtypeStruct(g.shape, g.dtype),
            input_output_aliases={2: 0},
            compiler_params=_cp(("arbitrary",)),
        )(pidx, s, g))
    return outs


def _sum_own_and_peers(own4, slots, pidx, name):
    _, rh, cdim = own4.shape
    tr = _row_tile(rh)

    def body(p_ref, own_ref, s_ref, o_ref):
        acc = own_ref[...].astype(F32)
        for j in range(3):
            acc = acc + s_ref[j].astype(F32)
        o_ref[...] = acc

    return pl.pallas_call(
        body, name=name,
        grid_spec=pltpu.PrefetchScalarGridSpec(
            num_scalar_prefetch=1, grid=(rh // tr,),
            in_specs=[pl.BlockSpec((None, tr, cdim), lambda i, p: (p[0], i, 0)),
                      pl.BlockSpec((3, tr, cdim), lambda i, p: (0, i, 0))],
            out_specs=pl.BlockSpec((tr, cdim), lambda i, p: (i, 0))),
        out_shape=jax.ShapeDtypeStruct((rh, cdim), F32),
        compiler_params=_cp(("arbitrary",)),
    )(pidx, own4, slots)


def _exchange(ins, land_shapes, copies, n_sems, name):
    n, nl = len(ins), len(land_shapes)

    def body(*refs):
        sends, recvs = copies(refs[:n], refs[n:n + nl], refs[n + nl], refs[n + nl + 1])
        for cp in sends:
            cp.start()
        for cp in sends:
            cp.wait_send()
        for cp in recvs:
            cp.wait_recv()

    any_spec = pl.BlockSpec(memory_space=pl.ANY)
    return pl.pallas_call(
        body, name=name,
        out_shape=[jax.ShapeDtypeStruct(s.shape, s.dtype) for s in land_shapes],
        in_specs=[any_spec] * n, out_specs=[any_spec] * nl,
        scratch_shapes=[pltpu.SemaphoreType.DMA((n_sems,)), pltpu.SemaphoreType.DMA((n_sems,))],
    )(*ins)


def _row_tile(r):
    t = min(TR_EW, r)
    while r % t:
        t //= 2
    return t


def _add_own_half(g4, recv, cidx, name):
    _, _, rh, cdim = g4.shape
    tr = _row_tile(rh)

    def body(c_ref, a_ref, b_ref, o_ref):
        o_ref[...] = (a_ref[...] + b_ref[...]).astype(BF16)

    return pl.pallas_call(
        body, name=name,
        grid_spec=pltpu.PrefetchScalarGridSpec(
            num_scalar_prefetch=1, grid=(N_CHIPS, rh // tr),
            in_specs=[pl.BlockSpec((None, None, tr, cdim), lambda q, i, c: (q, c[0], i, 0)),
                      pl.BlockSpec((None, tr, cdim), lambda q, i, c: (q, i, 0))],
            out_specs=pl.BlockSpec((None, tr, cdim), lambda q, i, c: (q, i, 0))),
        out_shape=jax.ShapeDtypeStruct(recv.shape, BF16),
        compiler_params=_cp(("arbitrary", "arbitrary")),
    )(cidx, g4, recv)


def _sum_slots(v, name):
    n, r, cdim = v.shape
    tr = _row_tile(r)

    def body(v_ref, o_ref):
        acc = v_ref[0].astype(F32)
        for k in range(1, n):
            acc = acc + v_ref[k].astype(F32)
        o_ref[...] = acc

    return pl.pallas_call(
        body, name=name, grid=(r // tr,),
        in_specs=[pl.BlockSpec((n, tr, cdim), lambda i: (0, i, 0))],
        out_specs=pl.BlockSpec((tr, cdim), lambda i: (i, 0)),
        out_shape=jax.ShapeDtypeStruct((r, cdim), F32),
        compiler_params=_cp(("arbitrary",)),
    )(v)


def _adamw_math(wv, gg, mv, vv):
    nm = ADAM_B1 * mv + (1.0 - ADAM_B1) * gg
    nv = ADAM_B2 * vv + (1.0 - ADAM_B2) * (gg * gg)
    m_hat = nm / (1.0 - ADAM_B1 ** ADAM_STEP)
    v_hat = nv / (1.0 - ADAM_B2 ** ADAM_STEP)
    return -ADAM_LR * (m_hat / (jnp.sqrt(v_hat) + ADAM_EPS) + ADAM_WD * wv), nm, nv


def _adamw_halves(w, mine, theirs, m, v, cidx, name):
    r, cdim = w.shape
    rh = r // 2
    tr = _row_tile(rh)
    nblk = rh // tr

    def body(c_ref, w_ref, a_ref, b_ref, m_ref, v_ref, g_ref, d_ref, nm_ref, nv_ref):
        gg = jnp.where(pl.program_id(0) == c_ref[0], a_ref[...], b_ref[...])
        g_ref[...] = gg
        d_ref[...], nm_ref[...], nv_ref[...] = _adamw_math(w_ref[...], gg, m_ref[...], v_ref[...])

    full = pl.BlockSpec((tr, cdim), lambda hh, i, c: (hh * nblk + i, 0))
    half = pl.BlockSpec((tr, cdim), lambda hh, i, c: (i, 0))
    return pl.pallas_call(
        body, name=name,
        grid_spec=pltpu.PrefetchScalarGridSpec(
            num_scalar_prefetch=1, grid=(2, nblk),
            in_specs=[full, half, half, full, full], out_specs=[full] * 4),
        out_shape=[jax.ShapeDtypeStruct((r, cdim), F32)] * 4,
        compiler_params=_cp(("arbitrary", "arbitrary")),
    )(cidx, w, mine, theirs, m, v)


def _adamw(w, g, m, v, name):
    r, cdim = w.shape
    tr = _row_tile(r) if r % 8 == 0 else r

    def body(w_ref, g_ref, m_ref, v_ref, d_ref, nm_ref, nv_ref):
        d_ref[...], nm_ref[...], nv_ref[...] = _adamw_math(w_ref[...], g_ref[...], m_ref[...], v_ref[...])

    spec = pl.BlockSpec((tr, cdim), lambda i: (i, 0))
    return pl.pallas_call(
        body, name=name, grid=(r // tr,), in_specs=[spec] * 4, out_specs=[spec] * 3,
        out_shape=[jax.ShapeDtypeStruct((r, cdim), F32)] * 3,
        compiler_params=_cp(("arbitrary",)),
    )(w, g, m, v)


def _ada_fwd(c_all, w_ada, b_cols):
    nb, _ = c_all.shape
    n = w_ada.shape[1]

    def body(c_ref, w_ref, b_ref, o_ref):
        cv = c_ref[...]
        o_ref[...] = _mm(cv * _sigmoid(cv), w_ref[...]) + b_ref[...]

    return pl.pallas_call(
        body, name="ada_fwd", out_shape=jax.ShapeDtypeStruct((nb, n), F32),
        compiler_params=pltpu.CompilerParams(vmem_limit_bytes=VMEM_LIMIT),
    )(c_all, w_ada, b_cols)


def _ada_bwd(c_all, dmod_all, dmod_cols):
    d = c_all.shape[1]
    n = dmod_cols.shape[1]

    def body(c_ref, da_ref, dc_ref, gw_ref, gb_ref):
        cv = c_ref[...]
        gw_ref[...] = _mm_tn(cv * _sigmoid(cv), dc_ref[...])
        gb_ref[...] = _colsum(da_ref[...])

    return pl.pallas_call(
        body, name="ada_bwd",
        out_shape=[jax.ShapeDtypeStruct((d, n), F32), jax.ShapeDtypeStruct((1, dmod_all.shape[1]), F32)],
        compiler_params=pltpu.CompilerParams(vmem_limit_bytes=VMEM_LIMIT),
    )(c_all, dmod_all, dmod_cols)


def _proj_fwd(x2, modv, w_in_g, b_in, seq):
    t, d = x2.shape
    nq, _, ns = w_in_g.shape
    tm = min(TM_PROJ, seq)
    tpb = seq // tm

    def body(x_ref, mod_ref, w_ref, b_ref, proj_ref, h_ref, h_s):
        @pl.when(pl.program_id(1) == 0)
        def _():
            h = x_ref[...] * (1.0 + mod_ref[1:2, :]) + mod_ref[0:1, :]
            h_s[...] = h.astype(BF16)
            h_ref[...] = h.astype(BF16)

        proj_ref[...] = jnp.dot(h_s[...], w_ref[...], preferred_element_type=F32) + b_ref[...]

    return pl.pallas_call(
        body, name="proj_fwd", grid=(t // tm, nq),
        in_specs=[pl.BlockSpec((tm, d), lambda i, q: (i, 0)),
                  pl.BlockSpec((None, 8, d), lambda i, q: (i // tpb, 0, 0)),
                  pl.BlockSpec((None, d, ns), lambda i, q: (q, 0, 0)),
                  pl.BlockSpec((1, ns), lambda i, q: (0, q))],
        out_specs=[pl.BlockSpec((tm, ns), lambda i, q: (i, q)),
                   pl.BlockSpec((tm, d), lambda i, q: (i, 0))],
        out_shape=[jax.ShapeDtypeStruct((t, nq * ns), F32), jax.ShapeDtypeStruct((t, d), BF16)],
        scratch_shapes=[pltpu.VMEM((tm, d), BF16)],
        compiler_params=_cp(("arbitrary", "arbitrary")),
    )(x2, modv, w_in_g, b_in)


def _lru_gates(xl, wc_ref, bc_ref, wa_ref, ba_ref, wx_ref, bx_ref, lam_ref):
    xc = bc_ref[...] + wc_ref[CONV_WIDTH - 1:CONV_WIDTH, :] * xl
    for k in range(CONV_WIDTH - 1):
        xc = xc + wc_ref[k:k + 1, :] * _shift_down(xl, CONV_WIDTH - 1 - k)
    r = _sigmoid(_mm(xc, wa_ref[...]) + ba_ref[...])
    gi = _sigmoid(_mm(xc, wx_ref[...]) + bx_ref[...])
    nl = -lam_ref[...]
    e = jnp.exp(-jnp.abs(nl))
    u = 1.0 + e
    dlt = u - 1.0
    log1p_e = jnp.where(dlt == 0.0, e, jnp.log(u) * (e / jnp.where(dlt == 0.0, 1.0, dlt)))
    big_l = -LRU_C * (jnp.maximum(nl, 0.0) + log1p_e)
    la = big_l * r
    a = jnp.exp(la)
    mult = jnp.sqrt(jnp.tanh(-la) * (a * a + 1.0))
    return xc, r, gi, big_l, a, mult


def _lru_prep(proj, lru_w, nb, seq):
    t = proj.shape[0]
    w = LRU_HEADS * HEAD
    w_conv, b_conv, w_a, b_a, w_x, b_x, lam = lru_w

    def body(x_ref, wc_ref, bc_ref, wa_ref, ba_ref, wx_ref, bx_ref, lam_ref, a_ref, inp_ref):
        xc, r, gi, big_l, a, mult = _lru_gates(x_ref[...], wc_ref, bc_ref, wa_ref, ba_ref, wx_ref, bx_ref, lam_ref)
        a_ref[...] = a
        inp_ref[...] = mult * (gi * xc)

    col = lambda b, hd: (0, hd)
    head = lambda b, hd: (hd, 0, 0)
    tok = lambda b, hd: (b, hd)
    return pl.pallas_call(
        body, name="lru_prep", grid=(nb, LRU_HEADS),
        in_specs=[pl.BlockSpec((seq, HEAD), tok),
                  pl.BlockSpec((CONV_WIDTH, HEAD), col), pl.BlockSpec((1, HEAD), col),
                  pl.BlockSpec((None, HEAD, HEAD), head), pl.BlockSpec((1, HEAD), col),
                  pl.BlockSpec((None, HEAD, HEAD), head), pl.BlockSpec((1, HEAD), col),
                  pl.BlockSpec((1, HEAD), col)],
        out_specs=[pl.BlockSpec((seq, HEAD), tok)] * 2,
        out_shape=[jax.ShapeDtypeStruct((t, w), F32)] * 2,
        compiler_params=_cp(("arbitrary", "arbitrary")),
    )(proj, w_conv, b_conv, w_a, b_a, w_x, b_x, lam)


def _scan(a3, b3, reverse, name):
    nb, seq, w = a3.shape
    tc = min(TC_SCAN, seq)
    nchunk = seq // tc
    ntile = tc // 8

    def combine(av, bv):
        rows = lax.broadcasted_iota(jnp.int32, av.shape, 0)
        for s in (1, 2, 4):
            if reverse:
                keep = rows < 8 - s
                a_sh, b_sh = pltpu.roll(av, 8 - s, 0), pltpu.roll(bv, 8 - s, 0)
            else:
                keep = rows >= s
                a_sh, b_sh = pltpu.roll(av, s, 0), pltpu.roll(bv, s, 0)
            bv = jnp.where(keep, bv + av * b_sh, bv)
            av = jnp.where(keep, av * a_sh, av)
        return av, bv

    def body(a_ref, b_ref, h_ref, carry):
        @pl.when(pl.program_id(0) == 0)
        def _():
            carry[...] = jnp.zeros_like(carry)

        for b in range(nb):
            def tile(j, hprev):
                jj = ntile - 1 - j if reverse else j
                base = pl.multiple_of(jj * 8, 8)
                av, bv = a_ref[b, pl.ds(base, 8), :], b_ref[b, pl.ds(base, 8), :]
                av, bv = combine(av, av * bv if reverse else bv)
                h = bv + av * hprev
                h_ref[b, pl.ds(base, 8), :] = h
                edge = h[0:1, :] if reverse else h[7:8, :]
                return jnp.broadcast_to(edge, (8, w))

            carry[b] = lax.fori_loop(0, ntile, tile, carry[b])

    imap = (lambda i: (0, nchunk - 1 - i, 0)) if reverse else (lambda i: (0, i, 0))
    spec = pl.BlockSpec((nb, tc, w), imap)
    return pl.pallas_call(
        body, name=name, grid=(nchunk,), in_specs=[spec, spec], out_specs=spec,
        out_shape=jax.ShapeDtypeStruct((nb, seq, w), F32),
        scratch_shapes=[pltpu.VMEM((nb, 8, w), F32)],
        compiler_params=_cp(("arbitrary",)),
    )(a3, b3)


def _sgu_mask():
    ti = lax.broadcasted_iota(jnp.int32, (HEAD, HEAD), 0) // SGU_CHUNK
    si = lax.broadcasted_iota(jnp.int32, (HEAD, HEAD), 1) // SGU_CHUNK
    return si <= ti


def _sgu_specs(tm, d_sgu):
    pw = 256
    first_u = (2 * LRU_HEADS * HEAD) // pw
    n_piece = d_sgu // pw
    specs = [pl.BlockSpec((tm, pw), functools.partial(lambda i, k: (i, k), k=first_u + j)) for j in range(2 * n_piece)]
    return specs, n_piece


def _sgu_fwd(proj, w_sp, b_sp_t, ln_g, ln_b):
    t = proj.shape[0]
    d_sgu = SGU_GROUPS * HEAD
    tm = min(TM_SGU, t)
    nblk = tm // HEAD
    specs, n_piece = _sgu_specs(tm, d_sgu)

    def body(*refs):
        u = jnp.concatenate([r[...] for r in refs[:n_piece]], axis=1)
        v = jnp.concatenate([r[...] for r in refs[n_piece:2 * n_piece]], axis=1)
        w_ref, bt_ref, g_ref, b_ref, y_ref = refs[2 * n_piece:]
        ug = _gelu(u)
        xhat, _ = _ln_stats(_gelu(v))
        vn = (xhat * g_ref[...] + b_ref[...]).astype(BF16)
        mask = _sgu_mask()
        for g in range(SGU_GROUPS):
            wm = jnp.where(mask, w_ref[g], 0.0).astype(BF16)
            cols = slice(g * HEAD, (g + 1) * HEAD)
            for n in range(nblk):
                rows = slice(n * HEAD, (n + 1) * HEAD)
                mixed = jnp.dot(wm, vn[rows, cols], preferred_element_type=F32) + bt_ref[:, g:g + 1]
                y_ref[rows, cols] = (ug[rows, cols] * mixed).astype(BF16)

    full = lambda shape: pl.BlockSpec(shape, lambda i: (0,) * len(shape))
    return pl.pallas_call(
        body, name="sgu_fwd", grid=(t // tm,),
        in_specs=specs + [full(w_sp.shape), full(b_sp_t.shape), full(ln_g.shape), full(ln_b.shape)],
        out_specs=pl.BlockSpec((tm, d_sgu), lambda i: (i, 0)),
        out_shape=jax.ShapeDtypeStruct((t, d_sgu), BF16),
        compiler_params=_cp(("arbitrary",)),
    )(*([proj] * (2 * n_piece)), w_sp, b_sp_t, ln_g, ln_b)


def _mix_fwd(hs, proj, y_sgu, x2, modv, w_o_lru_g, w_o_sgu_g, w_out_g, ln1_g, ln1_b, seq):
    t, d = x2.shape
    w = hs.shape[1]
    d_sgu = y_sgu.shape[1]
    nq, _, ns = w_o_sgu_g.shape
    tm = min(TM_MIX, seq)
    tpb = seq // tm

    def body(hs_ref, gl_ref, ys_ref, ga_ref, gb_ref, x_ref, mod_ref, wl_ref, ws_ref, wo_ref, g1_ref, b1_ref,
             yap_ref, ya_ref, yb_ref, mg_ref, mix_ref, x1_ref):
        yap = (hs_ref[...] * _gelu(gl_ref[...])).astype(BF16)
        yap_ref[...] = yap
        y_a = jnp.dot(yap, wl_ref[...], preferred_element_type=F32)
        ys = ys_ref[...]
        y_b = jnp.concatenate([jnp.dot(ys, ws_ref[q], preferred_element_type=F32) for q in range(nq)], axis=1)
        ya_ref[...] = y_a.astype(BF16)
        yb_ref[...] = y_b.astype(BF16)
        merged = (_sigmoid(ga_ref[...]) * y_a + _sigmoid(gb_ref[...]) * y_b).astype(BF16)
        mg_ref[...] = merged
        mix = jnp.dot(merged, wo_ref[...], preferred_element_type=F32)
        mix_ref[...] = mix
        xhat, _ = _ln_stats(ALPHA * x_ref[...] + (1.0 + mod_ref[2:3, :]) * mix)
        x1_ref[...] = xhat * g1_ref[...] + b1_ref[...]

    row = lambda width, col: pl.BlockSpec((tm, width), functools.partial(lambda i, k: (i, k), k=col))
    full = lambda shape: pl.BlockSpec(shape, lambda i: (0,) * len(shape))
    return pl.pallas_call(
        body, name="mix_fwd", grid=(t // tm,),
        in_specs=[row(w, 0), row(w, 1), row(d_sgu, 0), row(d, 4), row(d, 5), row(d, 0),
                  pl.BlockSpec((None, 8, d), lambda i: (i // tpb, 0, 0)),
                  full(w_o_lru_g.shape), full(w_o_sgu_g.shape), full(w_out_g.shape), full(ln1_g.shape), full(ln1_b.shape)],
        out_specs=[row(w, 0), row(d, 0), row(d, 0), row(d, 0), row(d, 0), row(d, 0)],
        out_shape=[jax.ShapeDtypeStruct((t, w), BF16), jax.ShapeDtypeStruct((t, d), BF16),
                   jax.ShapeDtypeStruct((t, d), BF16), jax.ShapeDtypeStruct((t, d), BF16),
                   jax.ShapeDtypeStruct((t, d), F32), jax.ShapeDtypeStruct((t, d), F32)],
        compiler_params=_cp(("arbitrary",)),
    )(hs, proj, y_sgu, proj, proj, x2, modv, w_o_lru_g, w_o_sgu_g, w_out_g, ln1_g, ln1_b)


def _mlp_fwd(x1, modv, w_up_g, w_down_g, ln2_g, ln2_b, target, nb, seq):
    t, d = x1.shape
    nq, _, ns = w_up_g.shape
    tm = min(TM_MLP, seq)
    tpb = seq // tm
    nt = t // tm

    def body(x1_ref, mod_ref, wu_ref, wd_ref, g2_ref, b2_ref, tg_ref,
             up_ref, h2_ref, dz2_ref, df_ref, st_ref, pb_ref, h2_s, acc):
        i, j = pl.program_id(0), pl.program_id(1)

        @pl.when(j == 0)
        def _():
            h2 = (x1_ref[...] * (1.0 + mod_ref[4:5, :]) + mod_ref[3:4, :]).astype(BF16)
            h2_s[...] = h2
            h2_ref[...] = h2
            acc[...] = jnp.zeros_like(acc)

        @pl.when((i == 0) & (j == 0))
        def _():
            st_ref[...] = jnp.zeros_like(st_ref)

        @pl.when((i % tpb == 0) & (j == 0))
        def _():
            pb_ref[...] = jnp.zeros_like(pb_ref)

        up = jnp.dot(h2_s[...], wu_ref[...], preferred_element_type=F32)
        up_ref[...] = up
        r = jnp.maximum(up, 0.0)
        acc[...] += jnp.dot((r * r).astype(BF16), wd_ref[...], preferred_element_type=F32)

        @pl.when(j == nq - 1)
        def _():
            f = acc[...]
            xhat, rstd = _ln_stats(ALPHA * x1_ref[...] + (1.0 + mod_ref[5:6, :]) * f)
            y = xhat * g2_ref[...] + b2_ref[...]
            err = y - tg_ref[...]
            dy = err * (1.0 / d)
            dz2 = _ln_bwd(dy * g2_ref[...], xhat, rstd)
            dz2_ref[...] = dz2
            df_ref[...] = ((1.0 + mod_ref[5:6, :]) * dz2).astype(BF16)
            st_ref[0:1, :] += _colsum(dy * xhat)
            st_ref[1:2, :] += _colsum(dy)
            st_ref[2:3, :] += (0.5 / d) * jnp.sum(_colsum(err * err), axis=1, keepdims=True)
            pb_ref[0:1, :] += _colsum(dz2 * f)

    tok = lambda i, j: (i, 0)
    return pl.pallas_call(
        body, name="mlp_fwd", grid=(nt, nq),
        in_specs=[pl.BlockSpec((tm, d), tok), pl.BlockSpec((None, 8, d), lambda i, j: (i // tpb, 0, 0)),
                  pl.BlockSpec((None, d, ns), lambda i, j: (j, 0, 0)), pl.BlockSpec((ns, d), lambda i, j: (j, 0)),
                  pl.BlockSpec((1, d), lambda i, j: (0, 0)), pl.BlockSpec((1, d), lambda i, j: (0, 0)),
                  pl.BlockSpec((tm, d), tok)],
        out_specs=[pl.BlockSpec((tm, ns), lambda i, j: (i, j)), pl.BlockSpec((tm, d), tok),
                   pl.BlockSpec((tm, d), tok), pl.BlockSpec((tm, d), tok),
                   pl.BlockSpec((8, d), lambda i, j: (0, 0)), pl.BlockSpec((None, 8, d), lambda i, j: (i // tpb, 0, 0))],
        out_shape=[jax.ShapeDtypeStruct((t, nq * ns), F32), jax.ShapeDtypeStruct((t, d), BF16),
                   jax.ShapeDtypeStruct((t, d), F32), jax.ShapeDtypeStruct((t, d), BF16),
                   jax.ShapeDtypeStruct((8, d), F32), jax.ShapeDtypeStruct((nb, 8, d), F32)],
        scratch_shapes=[pltpu.VMEM((tm, d), BF16), pltpu.VMEM((tm, d), F32)],
        compiler_params=_cp(("arbitrary", "arbitrary")),
    )(x1, modv, w_up_g, w_down_g, ln2_g, ln2_b, target)


def _mlp_bwd(df, up, w_down_g, w_up_g, dz2, x2, mix, modv, ln1_g, ln1_b, nb, seq):
    t, d = x2.shape
    nq, _, ns = w_up_g.shape
    tm = min(TM_MLP, seq)
    tpb = seq // tm

    def body(df_ref, up_ref, wd_ref, wu_ref, dz2_ref, x_ref, mix_ref, mod_ref, g1_ref, b1_ref,
             dup_ref, act_ref, dz1_ref, dmix_ref, st_ref, pb_ref, acc):
        i, j = pl.program_id(0), pl.program_id(1)

        @pl.when(j == 0)
        def _():
            acc[...] = jnp.zeros_like(acc)

        @pl.when((i == 0) & (j == 0))
        def _():
            st_ref[...] = jnp.zeros_like(st_ref)

        @pl.when((i % tpb == 0) & (j == 0))
        def _():
            pb_ref[...] = jnp.zeros_like(pb_ref)

        r = jnp.maximum(up_ref[...], 0.0)
        act_ref[...] = (r * r).astype(BF16)
        dup = (_mm_nt(df_ref[...], wd_ref[...]) * (2.0 * r)).astype(BF16)
        dup_ref[...] = dup
        acc[...] += _mm_nt(dup, wu_ref[...])

        @pl.when(j == nq - 1)
        def _():
            dh2 = acc[...]
            mix = mix_ref[...]
            xhat, rstd = _ln_stats(ALPHA * x_ref[...] + (1.0 + mod_ref[2:3, :]) * mix)
            x1 = xhat * g1_ref[...] + b1_ref[...]
            dx1 = ALPHA * dz2_ref[...] + dh2 * (1.0 + mod_ref[4:5, :])
            dz1 = _ln_bwd(dx1 * g1_ref[...], xhat, rstd)
            dz1_ref[...] = dz1
            dmix_ref[...] = ((1.0 + mod_ref[2:3, :]) * dz1).astype(BF16)
            st_ref[0:1, :] += _colsum(dx1 * xhat)
            st_ref[1:2, :] += _colsum(dx1)
            pb_ref[0:1, :] += _colsum(dh2 * x1)
            pb_ref[1:2, :] += _colsum(dh2)
            pb_ref[2:3, :] += _colsum(dz1 * mix)

    tok = lambda i, j: (i, 0)
    chunk = lambda i, j: (i, j)
    return pl.pallas_call(
        body, name="mlp_bwd", grid=(t // tm, nq),
        in_specs=[pl.BlockSpec((tm, d), tok), pl.BlockSpec((tm, ns), chunk),
                  pl.BlockSpec((ns, d), lambda i, j: (j, 0)), pl.BlockSpec((None, d, ns), lambda i, j: (j, 0, 0)),
                  pl.BlockSpec((tm, d), tok), pl.BlockSpec((tm, d), tok), pl.BlockSpec((tm, d), tok),
                  pl.BlockSpec((None, 8, d), lambda i, j: (i // tpb, 0, 0)),
                  pl.BlockSpec((1, d), lambda i, j: (0, 0)), pl.BlockSpec((1, d), lambda i, j: (0, 0))],
        out_specs=[pl.BlockSpec((tm, ns), chunk), pl.BlockSpec((tm, ns), chunk),
                   pl.BlockSpec((tm, d), tok), pl.BlockSpec((tm, d), tok),
                   pl.BlockSpec((8, d), lambda i, j: (0, 0)), pl.BlockSpec((None, 8, d), lambda i, j: (i // tpb, 0, 0))],
        out_shape=[jax.ShapeDtypeStruct((t, nq * ns), BF16), jax.ShapeDtypeStruct((t, nq * ns), BF16),
                   jax.ShapeDtypeStruct((t, d), F32), jax.ShapeDtypeStruct((t, d), BF16),
                   jax.ShapeDtypeStruct((8, d), F32), jax.ShapeDtypeStruct((nb, 8, d), F32)],
        scratch_shapes=[pltpu.VMEM((tm, d), F32)],
        compiler_params=_cp(("arbitrary", "arbitrary")),
    )(df, up, w_down_g, w_up_g, dz2, x2, mix, modv, ln1_g, ln1_b)


def _mix_bwd(dmix, proj, y_a, y_b, hs, w_out_g, w_o_lru_g, w_o_sgu_g, seq, after=()):
    t, d = dmix.shape
    w = hs.shape[1]
    nq, d_sgu, ns = w_o_sgu_g.shape
    tm = min(TM_MIX, seq)

    def body(dmix_ref, ga_ref, gb_ref, ya_ref, yb_ref, gl_ref, hs_ref, wo_ref, wl_ref, ws_ref,
             dya_ref, dyb_ref, dga_ref, dgb_ref, dgl_ref, dyl_ref, dys_ref):
        dmerged = _mm_nt(dmix_ref[...], wo_ref[...])
        sa, sb = _sigmoid(ga_ref[...]), _sigmoid(gb_ref[...])
        dy_a = (dmerged * sa).astype(BF16)
        dy_b = (dmerged * sb).astype(BF16)
        dya_ref[...] = dy_a
        dyb_ref[...] = dy_b
        dga_ref[...] = (dmerged * ya_ref[...].astype(F32) * (sa * (1.0 - sa))).astype(BF16)
        dgb_ref[...] = (dmerged * yb_ref[...].astype(F32) * (sb * (1.0 - sb))).astype(BF16)
        dyap = _mm_nt(dy_a, wl_ref[...])
        gel, dgel = _gelu_and_grad(gl_ref[...])
        dyl_ref[...] = dyap * gel
        dgl_ref[...] = (dyap * hs_ref[...] * dgel).astype(BF16)
        dys = _mm_nt(dy_b[:, 0:ns], ws_ref[0])
        for q in range(1, nq):
            dys = dys + _mm_nt(dy_b[:, q * ns:(q + 1) * ns], ws_ref[q])
        dys_ref[...] = dys

    row = lambda width, col: pl.BlockSpec((tm, width), functools.partial(lambda i, k: (i, k), k=col))
    full = lambda shape: pl.BlockSpec(shape, lambda i: (0,) * len(shape))
    return pl.pallas_call(
        _ordered(body, 10, after), name="mix_bwd", grid=(t // tm,),
        in_specs=[row(d, 0), row(d, 4), row(d, 5), row(d, 0), row(d, 0), row(w, 1), row(w, 0),
                  full(w_out_g.shape), full(w_o_lru_g.shape), full(w_o_sgu_g.shape)] + [_ANY] * len(after),
        out_specs=[row(d, 0), row(d, 0), row(d, 0), row(d, 0), row(w, 0), row(w, 0), row(d_sgu, 0)],
        out_shape=[jax.ShapeDtypeStruct((t, d), BF16), jax.ShapeDtypeStruct((t, d), BF16),
                   jax.ShapeDtypeStruct((t, d), BF16), jax.ShapeDtypeStruct((t, d), BF16),
                   jax.ShapeDtypeStruct((t, w), BF16), jax.ShapeDtypeStruct((t, w), F32),
                   jax.ShapeDtypeStruct((t, d_sgu), F32)],
        compiler_params=_cp(("arbitrary",)),
    )(dmix, proj, proj, y_a, y_b, proj, hs, w_out_g, w_o_lru_g, w_o_sgu_g, *after)


def _sgu_bwd(proj, dys, w_sp, b_sp_t, ln_g, ln_b, after=()):
    t = proj.shape[0]
    d_sgu = SGU_GROUPS * HEAD
    tm = min(TM_SGU, t)
    nblk = tm // HEAD
    specs, n_piece = _sgu_specs(tm, d_sgu)

    def body(*refs):
        u = jnp.concatenate([r[...] for r in refs[:n_piece]], axis=1)
        v = jnp.concatenate([r[...] for r in refs[n_piece:2 * n_piece]], axis=1)
        dys_ref, w_ref, bt_ref, g_ref, b_ref, du_ref, dv_ref, dw_ref, st_ref, dbt_ref, dvn_s = refs[2 * n_piece:]

        @pl.when(pl.program_id(0) == 0)
        def _():
            dw_ref[...] = jnp.zeros_like(dw_ref)
            st_ref[...] = jnp.zeros_like(st_ref)
            dbt_ref[...] = jnp.zeros_like(dbt_ref)

        ug, dug_du = _gelu_and_grad(u)
        vg, dvg_dv = _gelu_and_grad(v)
        xhat, rstd = _ln_stats(vg)
        vn = (xhat * g_ref[...] + b_ref[...]).astype(BF16)
        dys_v = dys_ref[...]
        mask = _sgu_mask()
        for g in range(SGU_GROUPS):
            wm = jnp.where(mask, w_ref[g], 0.0).astype(BF16)
            cols = slice(g * HEAD, (g + 1) * HEAD)
            dw_g = jnp.zeros((HEAD, HEAD), F32)
            db_g = jnp.zeros((HEAD, 1), F32)
            for n in range(nblk):
                rows = slice(n * HEAD, (n + 1) * HEAD)
                vn_blk = vn[rows, cols]
                mixed = jnp.dot(wm, vn_blk, preferred_element_type=F32) + bt_ref[:, g:g + 1]
                dy_blk = dys_v[rows, cols]
                du_ref[rows, cols] = (dy_blk * mixed * dug_du[rows, cols]).astype(BF16)
                dmx = dy_blk * ug[rows, cols]
                dvn_s[rows, cols] = _mm_tn(wm, dmx)
                dw_g = dw_g + _mm_nt(dmx, vn_blk)
                db_g = db_g + jnp.sum(dmx, axis=1, keepdims=True)
            dw_ref[g] += jnp.where(mask, dw_g, 0.0)
            dbt_ref[:, g:g + 1] += db_g
        dvn = dvn_s[...]
        st_ref[0:1, :] += _colsum(dvn * xhat)
        st_ref[1:2, :] += _colsum(dvn)
        dv_ref[...] = (_ln_bwd(dvn * g_ref[...], xhat, rstd) * dvg_dv).astype(BF16)

    full = lambda shape: pl.BlockSpec(shape, lambda i: (0,) * len(shape))
    tok = pl.BlockSpec((tm, d_sgu), lambda i: (i, 0))
    return pl.pallas_call(
        _ordered(body, 2 * n_piece + 5, after), name="sgu_bwd", grid=(t // tm,),
        in_specs=specs + [tok, full(w_sp.shape), full(b_sp_t.shape), full(ln_g.shape), full(ln_b.shape)]
        + [_ANY] * len(after),
        out_specs=[tok, tok, full(w_sp.shape), full((8, d_sgu)), full((HEAD, HEAD))],
        out_shape=[jax.ShapeDtypeStruct((t, d_sgu), BF16), jax.ShapeDtypeStruct((t, d_sgu), BF16),
                   jax.ShapeDtypeStruct(w_sp.shape, F32), jax.ShapeDtypeStruct((8, d_sgu), F32),
                   jax.ShapeDtypeStruct((HEAD, HEAD), F32)],
        scratch_shapes=[pltpu.VMEM((tm, d_sgu), F32)],
        compiler_params=_cp(("arbitrary",)),
    )(*([proj] * (2 * n_piece)), dys, w_sp, b_sp_t, ln_g, ln_b, *after)


def _lru_bwd(proj, hs, e, dyl, lru_w, nb, seq, after=()):
    t = proj.shape[0]
    w = LRU_HEADS * HEAD
    w_conv, b_conv, w_a, b_a, w_x, b_x, lam = lru_w

    def body(x_ref, hs_ref, e_ref, dy_ref, wc_ref, bc_ref, wa_ref, ba_ref, wx_ref, bx_ref, lam_ref,
             dxl_ref, dwa_ref, dwx_ref, st_ref):
        @pl.when(pl.program_id(1) == 0)
        def _():
            dwa_ref[...] = jnp.zeros_like(dwa_ref)
            dwx_ref[...] = jnp.zeros_like(dwx_ref)
            st_ref[...] = jnp.zeros_like(st_ref)

        xl = x_ref[...]
        xc, r, gi, big_l, a, mult = _lru_gates(xl, wc_ref, bc_ref, wa_ref, ba_ref, wx_ref, bx_ref, lam_ref)
        dh = dy_ref[...] + _shift_up(e_ref[...], 1)
        da = dh * _shift_down(hs_ref[...], 1)
        dmult = dh * (gi * xc)
        d_i = dh * (mult * xc)
        dxc = dh * (mult * gi)
        a2 = a * a
        dla = da * a - dmult * (a2 / mult)
        dr = dla * big_l
        d_big_l = _colsum(dla * r)
        dra = dr * (r * (1.0 - r))
        dia = d_i * (gi * (1.0 - gi))
        dwa_ref[...] += _mm_tn(xc, dra)
        dwx_ref[...] += _mm_tn(xc, dia)
        dxc = dxc + _mm_nt(dra, wa_ref[...]) + _mm_nt(dia, wx_ref[...])
        dxl = wc_ref[CONV_WIDTH - 1:CONV_WIDTH, :] * dxc
        for k in range(CONV_WIDTH - 1):
            dxl = dxl + wc_ref[k:k + 1, :] * _shift_up(dxc, CONV_WIDTH - 1 - k)
        dxl_ref[...] = dxl.astype(BF16)
        st_ref[0:1, :] += _colsum(dra)
        st_ref[1:2, :] += _colsum(dia)
        st_ref[2:3, :] += d_big_l * (LRU_C * _sigmoid(-lam_ref[...]))
        st_ref[3:4, :] += _colsum(dxc)
        for k in range(CONV_WIDTH):
            st_ref[4 + k:5 + k, :] += _colsum(dxc * _shift_down(xl, CONV_WIDTH - 1 - k))

    col = lambda hd, b: (0, hd)
    head = lambda hd, b: (hd, 0, 0)
    tok = lambda hd, b: (b, hd)
    seq_blk = pl.BlockSpec((seq, HEAD), tok)
    return pl.pallas_call(
        _ordered(body, 11, after), name="lru_bwd", grid=(LRU_HEADS, nb),
        in_specs=[seq_blk, seq_blk, seq_blk, seq_blk,
                  pl.BlockSpec((CONV_WIDTH, HEAD), col), pl.BlockSpec((1, HEAD), col),
                  pl.BlockSpec((None, HEAD, HEAD), head), pl.BlockSpec((1, HEAD), col),
                  pl.BlockSpec((None, HEAD, HEAD), head), pl.BlockSpec((1, HEAD), col),
                  pl.BlockSpec((1, HEAD), col)] + [_ANY] * len(after),
        out_specs=[seq_blk, pl.BlockSpec((None, HEAD, HEAD), head), pl.BlockSpec((None, HEAD, HEAD), head),
                   pl.BlockSpec((8, HEAD), col)],
        out_shape=[jax.ShapeDtypeStruct((t, w), BF16), jax.ShapeDtypeStruct((LRU_HEADS, HEAD, HEAD), F32),
                   jax.ShapeDtypeStruct((LRU_HEADS, HEAD, HEAD), F32), jax.ShapeDtypeStruct((8, w), F32)],
        compiler_params=_cp(("arbitrary", "arbitrary")),
    )(proj, hs, e, dyl, w_conv, b_conv, w_a, b_a, w_x, b_x, lam, *after)


def _weight_grad(a, g, col_shards, name):
    t, k = a.shape
    n = g.shape[1]
    tt = min(TT_DW, t)
    tk = min(k, 1024)
    while k % tk:
        tk //= 2
    ns = n // N_CHIPS if col_shards else n
    tn = min(ns, 768 if ns % 768 == 0 else 1024)
    while ns % tn:
        tn //= 2
    per = ns // tn

    def body(a_ref, g_ref, o_ref):
        @pl.when(pl.program_id(2) == 0)
        def _():
            o_ref[...] = jnp.zeros_like(o_ref)

        o_ref[...] += _mm_tn(a_ref[...], g_ref[...])

    if col_shards:
        out_spec = pl.BlockSpec((None, tk, tn), lambda i, j, s: (j // per, i, j % per))
        out_shape = jax.ShapeDtypeStruct((N_CHIPS, k, ns), F32)
    else:
        out_spec = pl.BlockSpec((tk, tn), lambda i, j, s: (i, j))
        out_shape = jax.ShapeDtypeStruct((k, n), F32)
    return pl.pallas_call(
        body, name=name, grid=(k // tk, n // tn, t // tt),
        in_specs=[pl.BlockSpec((tt, tk), lambda i, j, s: (s, i)), pl.BlockSpec((tt, tn), lambda i, j, s: (s, j))],
        out_specs=out_spec, out_shape=out_shape,
        compiler_params=_cp(("arbitrary", "arbitrary", "arbitrary")),
    )(a, g)


def _input_grad(dproj, w_in_g, dz1, x2, modv, nb, seq, after=()):
    t, d = x2.shape
    nq, _, ns = w_in_g.shape
    tm = min(TM_DH, seq)
    tpb = seq // tm

    def body(dp_ref, w_ref, dz1_ref, x_ref, mod_ref, gx_ref, db_ref, pb_ref, acc):
        i, q = pl.program_id(0), pl.program_id(1)

        @pl.when(q == 0)
        def _():
            acc[...] = jnp.zeros_like(acc)

        @pl.when((i == 0) & (q == 0))
        def _():
            db_ref[...] = jnp.zeros_like(db_ref)

        @pl.when((i % tpb == 0) & (q == 0))
        def _():
            pb_ref[...] = jnp.zeros_like(pb_ref)

        dp = dp_ref[...]
        acc[...] += _mm_nt(dp, w_ref[...])
        db_ref[q, 0:1, :] += _colsum(dp.astype(F32))

        @pl.when(q == nq - 1)
        def _():
            dh = acc[...]
            gx_ref[...] = ALPHA * dz1_ref[...] + dh * (1.0 + mod_ref[1:2, :])
            pb_ref[0:1, :] += _colsum(dh * x_ref[...])
            pb_ref[1:2, :] += _colsum(dh)

    tok = lambda i, q: (i, 0)
    return pl.pallas_call(
        _ordered(body, 5, after), name="input_grad", grid=(t // tm, nq),
        in_specs=[pl.BlockSpec((tm, ns), lambda i, q: (i, q)), pl.BlockSpec((None, d, ns), lambda i, q: (q, 0, 0)),
                  pl.BlockSpec((tm, d), tok), pl.BlockSpec((tm, d), tok),
                  pl.BlockSpec((None, 8, d), lambda i, q: (i // tpb, 0, 0))] + [_ANY] * len(after),
        out_specs=[pl.BlockSpec((tm, d), tok), pl.BlockSpec((nq, 8, ns), lambda i, q: (0, 0, 0)),
                   pl.BlockSpec((None, 8, d), lambda i, q: (i // tpb, 0, 0))],
        out_shape=[jax.ShapeDtypeStruct((t, d), F32), jax.ShapeDtypeStruct((nq, 8, ns), F32),
                   jax.ShapeDtypeStruct((nb, 8, d), F32)],
        scratch_shapes=[pltpu.VMEM((tm, d), F32)],
        compiler_params=_cp(("arbitrary", "arbitrary")),
    )(dproj, w_in_g, dz1, x2, modv, *after)


def _rows128(v):
    flat = v.reshape(-1, HEAD)
    pad = (-flat.shape[0]) % 8
    return jnp.pad(flat, ((0, pad), (0, 0))) if pad else flat


def kernel(x, c, w_ada, b_ada, w_in, b_in, w_conv, b_conv, w_rg_a, b_rg_a, w_rg_x, b_rg_x, lru_lambda, w_sp, b_sp, ln_v_g, ln_v_b, w_o_lru, w_o_sgu, w_out, ln1_g, ln1_b, w_up, w_down, ln2_g, ln2_b, loss_target, m_w_ada, m_b_ada, m_w_in, m_b_in, m_w_conv, m_b_conv, m_w_rg_a, m_b_rg_a, m_w_rg_x, m_b_rg_x, m_lru_lambda, m_w_sp, m_b_sp, m_ln_v_g, m_ln_v_b, m_w_o_lru, m_w_o_sgu, m_w_out, m_ln1_g, m_ln1_b, m_w_up, m_w_down, m_ln2_g, m_ln2_b, v_w_ada, v_b_ada, v_w_in, v_b_in, v_w_conv, v_b_conv, v_w_rg_a, v_b_rg_a, v_w_rg_x, v_b_rg_x, v_lru_lambda, v_w_sp, v_b_sp, v_ln_v_g, v_ln_v_b, v_w_o_lru, v_w_o_sgu, v_w_out, v_ln1_g, v_ln1_b, v_w_up, v_w_down, v_ln2_g, v_ln2_b):
    given = dict(locals())
    nb, seq, d = x.shape
    t = nb * seq
    w_lru = LRU_HEADS * HEAD
    d_sgu = SGU_GROUPS * HEAD
    xi, yi, ci = lax.axis_index("x"), lax.axis_index("y"), lax.axis_index("c")
    chip = 2 * xi + yi
    dev = 2 * chip + ci
    cidx = jnp.reshape(ci, (1,)).astype(jnp.int32)

    x2 = x.reshape(t, d)
    target = loss_target.reshape(t, d)

    big = ["w_in", "w_o_lru", "w_o_sgu", "w_out", "w_up", "w_down"]
    shards_a = [w_in[0].astype(BF16)]
    shards_b = [given[n][0].astype(BF16) for n in big[1:]]
    pidx = jnp.reshape(chip, (1,)).astype(jnp.int32)

    c_rows = _rows128(c)
    wconv_rows = _rows128(w_conv[0])
    slab0 = _all_gather_small(jnp.concatenate([c_rows, wconv_rows], axis=0), "gather_c_wconv")
    slab0 = slab0.reshape(N_DEV, -1, HEAD)
    c_all = slab0[:, :c_rows.shape[0]].reshape(N_DEV * nb, d)
    n_wc = CONV_WIDTH * (w_lru // N_CHIPS) // HEAD
    wc = slab0[0::2, c_rows.shape[0]:c_rows.shape[0] + n_wc].reshape(N_CHIPS, CONV_WIDTH, w_lru // N_CHIPS)
    w_conv_full = jnp.transpose(wc, (1, 0, 2)).reshape(CONV_WIDTH, w_lru)

    n_ada = w_ada.shape[2]
    b_ada_cols = lax.dynamic_slice(b_ada, (0, chip * n_ada), (1, n_ada))
    mod_cols = _ada_fwd(c_all, w_ada[0], b_ada_cols)
    half = (N_DEV * nb) // 2
    mod_half = lax.dynamic_slice(mod_cols, (ci * half, 0), (half, n_ada))
    mod_g = _all_gather_small(mod_half, "gather_mod").reshape(N_CHIPS, 2, half, n_ada)
    mod_all = jnp.transpose(mod_g, (1, 2, 0, 3)).reshape(N_DEV * nb, N_CHIPS * n_ada)
    mod_loc = lax.dynamic_slice(mod_all, (dev * nb, 0), (nb, N_CHIPS * n_ada)).reshape(nb, 6, d)
    modv = jnp.pad(mod_loc, ((0, 0), (0, 2), (0, 0)))

    lru_w = (w_conv_full, b_conv, w_rg_a[0], b_rg_a, w_rg_x[0], b_rg_x, lru_lambda)
    b_sp_t = jnp.transpose(b_sp[0])

    land = lambda s: jax.ShapeDtypeStruct((N_CHIPS,) + s.shape, s.dtype)
    started_a = _split_start(shards_a, [land(s) for s in shards_a], _gather_copies, 3, "gather_w_in_start", after=(modv,))
    started_b = _split_start(shards_b, [land(s) for s in shards_b], _gather_copies, 3 * len(shards_b),
                             "gather_w_rest_start", after=(started_a[-1],))

    shards_a, lands_a = _split_wait(started_a, 1, _gather_copies, "gather_w_in_wait", after=(started_b[-1],))
    (w_in_g,) = _fill_own_slot(lands_a, shards_a, pidx, ["own_w_in"])
    proj, h = _proj_fwd(x2, modv, w_in_g, b_in, seq)
    a, inp = _lru_prep(proj, lru_w, nb, seq)
    a3 = a.reshape(nb, seq, w_lru)
    hs = _scan(a3, inp.reshape(nb, seq, w_lru), False, "lru_scan").reshape(t, w_lru)
    y_sgu = _sgu_fwd(proj, w_sp[0], b_sp_t, ln_v_g, ln_v_b)
    shards_b, lands_b = _split_wait(started_b, len(shards_b), _gather_copies, "gather_w_rest_wait", after=(hs, y_sgu))
    w_o_lru_g, w_o_sgu_g, w_out_g, w_up_g, w_down_g = _fill_own_slot(lands_b, shards_b, pidx, ["own_" + n for n in big[1:]])
    w_o_lru_g = w_o_lru_g.reshape(w_lru, d)
    w_out_g = w_out_g.reshape(d, d)
    w_down_g = w_down_g.reshape(-1, d)
    yap, y_a, y_b, merged, mix, x1 = _mix_fwd(hs, proj, y_sgu, x2, modv, w_o_lru_g, w_o_sgu_g, w_out_g, ln1_g, ln1_b, seq)
    up, h2, dz2, df, st2, pb2 = _mlp_fwd(x1, modv, w_up_g, w_down_g, ln2_g, ln2_b, target, nb, seq)
    loss = lax.psum(st2[2, 0], ("x", "y", "c"))

    part = {}

    def to_sibling_start(group, tag, after=()):
        g4 = []
        for n in group:
            shard = given[n].shape[1:]
            g4.append(part[n].reshape(N_CHIPS, 2, shard[0] // 2, shard[1]))
        shapes = [jax.ShapeDtypeStruct((N_CHIPS,) + g.shape[2:], F32) for g in g4]
        return _split_start(g4, shapes, _to_sibling_copies, len(g4), "grads_to_sibling_start_" + tag, after)

    def to_chips_start(group, started, tag, after=()):
        g4, recv = _split_wait(started, len(group), _to_sibling_copies, "grads_to_sibling_wait_" + tag, after)
        own4 = [_add_own_half(g4[k], recv[k], cidx, "grad_pair_sum_" + n) for k, n in enumerate(group)]
        shapes = [jax.ShapeDtypeStruct((3,) + o.shape[1:], BF16) for o in own4]
        return _split_start(own4, shapes, _chip_exchange_copies, 3 * len(own4), "grads_chip_exchange_start_" + tag)

    def chips_finish(group, started, tag, after=()):
        own4, slots = _split_wait(started, len(group), _chip_exchange_copies, "grads_chip_exchange_wait_" + tag, after)
        return [_sum_own_and_peers(own4[k], slots[k], pidx, "grad_chip_sum_" + n) for k, n in enumerate(group)]

    dup, act, dz1, dmix, st1, pb1 = _mlp_bwd(df, up, w_down_g, w_up_g, dz2, x2, mix, modv, ln1_g, ln1_b, nb, seq)
    group1 = ["w_up", "w_down"]
    part["w_up"] = _weight_grad(h2, dup, True, "grad_w_up")
    part["w_down"] = _weight_grad(act, df, False, "grad_w_down")
    sib1 = to_sibling_start(group1, "mlp")
    dy_a, dy_b, dga, dgb, dgl, dyl, dys = _mix_bwd(dmix, proj, y_a, y_b, hs, w_out_g, w_o_lru_g, w_o_sgu_g, seq,
                                                   after=(sib1[-1],))
    group2 = ["w_o_lru", "w_o_sgu", "w_out"]
    part["w_o_lru"] = _weight_grad(yap, dy_a, False, "grad_w_o_lru")
    part["w_o_sgu"] = _weight_grad(y_sgu, dy_b, True, "grad_w_o_sgu")
    part["w_out"] = _weight_grad(merged, dmix, False, "grad_w_out")
    chips1 = to_chips_start(group1, sib1, "mlp", after=(dys, part["w_o_lru"], part["w_o_sgu"], part["w_out"]))
    sib2 = to_sibling_start(group2, "mix", after=(chips1[-1],))
    du, dv, g_w_sp, st_sgu, g_b_sp_t = _sgu_bwd(proj, dys, w_sp[0], b_sp_t, ln_v_g, ln_v_b, after=(sib2[-1],))
    dyl3 = dyl.reshape(nb, seq, w_lru)
    e = _scan(a3, dyl3, True, "lru_scan_bwd").reshape(t, w_lru)
    chips2 = to_chips_start(group2, sib2, "mix", after=(e, du))
    dxl, g_w_rg_a, g_w_rg_x, st_lru = _lru_bwd(proj, hs, e, dyl, lru_w, nb, seq, after=(chips2[-1],))
    dproj = jnp.concatenate([dxl, dgl, du, dv, dga, dgb], axis=1)
    group3 = ["w_in"]
    part["w_in"] = _weight_grad(h, dproj, True, "grad_w_in")
    sib3 = to_sibling_start(group3, "in")
    chips3 = to_chips_start(group3, sib3, "in")
    grad_x2, g_b_in4, pb0 = _input_grad(dproj, w_in_g, dz1, x2, modv, nb, seq, after=(chips3[-1],))
    halves12 = chips_finish(group1, chips1, "mlp", after=(grad_x2,)) + chips_finish(group2, chips2, "mix", after=(grad_x2,))
    swap12 = _split_start(halves12, [jax.ShapeDtypeStruct(hv.shape, F32) for hv in halves12], _swap_copies, len(halves12),
                          "grads_swap_start")
    grads = {}

    dmod_loc = jnp.stack([pb0[:, 1], pb0[:, 0], pb1[:, 2], pb1[:, 1], pb1[:, 0], pb2[:, 0]], axis=1)
    small = [
        ("dmod", dmod_loc),
        ("b_in", g_b_in4[:, 0]), ("w_conv", st_lru[4:8]), ("b_conv", st_lru[3]),
        ("w_rg_a", g_w_rg_a), ("b_rg_a", st_lru[0]), ("w_rg_x", g_w_rg_x), ("b_rg_x", st_lru[1]),
        ("lru_lambda", st_lru[2]), ("w_sp", g_w_sp), ("b_sp", jnp.transpose(g_b_sp_t[:, :SGU_GROUPS])),
        ("ln_v_g", st_sgu[0]), ("ln_v_b", st_sgu[1]), ("ln1_g", st1[0]), ("ln1_b", st1[1]),
        ("ln2_g", st2[0]), ("ln2_b", st2[1]),
    ]
    pieces = [_rows128(v) for _, v in small]
    slab = jnp.concatenate(pieces, axis=0)
    slab = jnp.pad(slab, ((0, (-slab.shape[0]) % TR_EW), (0, 0)))
    n_rows = slab.shape[0]
    gathered = _all_gather_small(slab, "gather_small_grads", after=(swap12[-1],)).reshape(N_DEV, n_rows, HEAD)
    summed = _sum_slots(gathered, "small_grad_sum")
    mine12, theirs12 = _split_wait(swap12, len(halves12), _swap_copies, "grads_swap_wait", after=(summed,))
    (mine3,) = chips_finish(group3, chips3, "in", after=(summed,))
    (theirs3,) = _exchange([mine3], [jax.ShapeDtypeStruct(mine3.shape, F32)], _swap_copies, 1, "grads_swap_w_in")
    mine = dict(zip(group1 + group2 + group3, mine12 + [mine3]))
    theirs = dict(zip(group1 + group2 + group3, theirs12 + [theirs3]))
    off = 0
    for (n, v), piece in zip(small, pieces):
        rows = v.size // HEAD
        if n == "dmod":
            dmod_all = gathered[:, off:off + rows].reshape(N_DEV * nb, 6 * d)
        else:
            grads[n] = summed[off:off + rows].reshape(v.shape)
        off += piece.shape[0]

    dmod_cols = lax.dynamic_slice(dmod_all, (0, chip * n_ada), (N_DEV * nb, n_ada))
    grads["w_ada"], grads["b_ada"] = _ada_bwd(c_all, dmod_all, dmod_cols)
    n_wcs = w_lru // N_CHIPS
    grads["w_conv"] = lax.dynamic_slice(grads["w_conv"], (0, chip * n_wcs), (CONV_WIDTH, n_wcs))

    names = ['w_ada', 'b_ada', 'w_in', 'b_in', 'w_conv', 'b_conv', 'w_rg_a', 'b_rg_a', 'w_rg_x', 'b_rg_x', 'lru_lambda',
             'w_sp', 'b_sp', 'ln_v_g', 'ln_v_b', 'w_o_lru', 'w_o_sgu', 'w_out', 'ln1_g', 'ln1_b', 'w_up', 'w_down',
             'ln2_g', 'ln2_b']
    out_g, out_d, out_m, out_v = [], [], [], []
    for n in names:
        wv = given[n]
        shape2 = (-1, wv.shape[-1])
        w2, m2, v2 = wv.reshape(shape2), given["m_" + n].reshape(shape2), given["v_" + n].reshape(shape2)
        if n in big:
            g2, dlt, nm, nv = _adamw_halves(w2, mine[n], theirs[n], m2, v2, cidx, "adamw_" + n)
        else:
            g2 = grads[n].reshape(wv.shape).reshape(shape2)
            dlt, nm, nv = _adamw(w2, g2, m2, v2, "adamw_" + n)
        out_g.append(g2.reshape(wv.shape))
        out_d.append(dlt.reshape(wv.shape))
        out_m.append(nm.reshape(wv.shape))
        out_v.append(nv.reshape(wv.shape))

    return (loss, grad_x2.reshape(nb, seq, d), *out_g, *out_d, *out_m, *out_v)
```

```python
import functools
import math

import jax
import jax.numpy as jnp
from jax import lax
from jax.experimental import pallas as pl
from jax.experimental.pallas import tpu as pltpu

F32 = jnp.float32
BF16 = jnp.bfloat16
MESH = pl.DeviceIdType.MESH

N_CHIPS = 4
N_DEV = 8
LRU_HEADS = 10
HEAD = 128
SGU_GROUPS = 6
SGU_CHUNK = 64
CONV_WIDTH = 4
LRU_C = 8.0
ALPHA = 2.0 ** 0.25
LN_EPS = 1e-5
ADAM_LR, ADAM_B1, ADAM_B2, ADAM_EPS, ADAM_WD, ADAM_STEP = 0.001, 0.9, 0.999, 1e-08, 0.01, 10

VMEM_LIMIT = 56 * 1024 * 1024
TM_PROJ = 1024
TM_MIX = 256
TM_MLP = 512
TM_SGU = 512
TM_DH = 512
TT_DW = 1024
TC_SCAN = 256
TR_EW = 256


def _cp(sem=None):
    return pltpu.CompilerParams(dimension_semantics=sem, vmem_limit_bytes=VMEM_LIMIT)


def _mm(a, b):
    return jnp.dot(a.astype(BF16), b.astype(BF16), preferred_element_type=F32)


def _mm_nt(a, b):
    return lax.dot_general(a.astype(BF16), b.astype(BF16), (((1,), (1,)), ((), ())), preferred_element_type=F32)


def _mm_tn(a, b):
    return lax.dot_general(a.astype(BF16), b.astype(BF16), (((0,), (0,)), ((), ())), preferred_element_type=F32)


def _sigmoid(x):
    return 1.0 / (1.0 + jnp.exp(-x))


def _sigmoid_t(x):
    return 0.5 * jnp.tanh(0.5 * x) + 0.5


_GELU_K = math.sqrt(2.0 / math.pi)


def _gelu(x):
    t = jnp.tanh(_GELU_K * (x + 0.044715 * (x * x * x)))
    return 0.5 * x * (1.0 + t)


def _gelu_and_grad(x):
    x2 = x * x
    t = jnp.tanh(_GELU_K * (x + 0.044715 * (x2 * x)))
    g = 0.5 * x * (1.0 + t)
    dg = 0.5 * (1.0 + t) + 0.5 * x * (1.0 - t * t) * (_GELU_K * (1.0 + 3.0 * 0.044715 * x2))
    return g, dg


def _ln_stats(z):
    mu = jnp.mean(z, axis=-1, keepdims=True)
    zc = z - mu
    var = jnp.mean(zc * zc, axis=-1, keepdims=True)
    rstd = lax.rsqrt(var + LN_EPS)
    return zc * rstd, rstd


def _ln_bwd(dxh, xhat, rstd):
    m1 = jnp.mean(dxh, axis=-1, keepdims=True)
    m2 = jnp.mean(dxh * xhat, axis=-1, keepdims=True)
    return rstd * (dxh - m1 - xhat * m2)


def _colsum(v):
    return jnp.sum(v, axis=0, keepdims=True)


def _shift_down(v, j):
    if j == 0:
        return v
    rows = lax.broadcasted_iota(jnp.int32, v.shape, 0)
    return jnp.where(rows >= j, pltpu.roll(v, j, 0), 0.0)


def _shift_up(v, j):
    if j == 0:
        return v
    n = v.shape[0]
    rows = lax.broadcasted_iota(jnp.int32, v.shape, 0)
    return jnp.where(rows < n - j, pltpu.roll(v, n - j, 0), 0.0)


def _my_pos():
    return lax.axis_index("x"), lax.axis_index("y"), lax.axis_index("c")


def _all_gather_small(v, name, after=()):
    m_per, n = v.shape

    def body(x_ref, out_ref, send_sems, recv_sems, local_sem):
        x, y, c = _my_pos()
        me, sibling = (x, y, c), (x, y, 1 - c)
        chips = [(1 - x, y), (x, 1 - y), (1 - x, 1 - y)]

        def rows(px, py, pc):
            return out_ref.at[pl.ds((4 * px + 2 * py + pc) * m_per, m_per), :]

        def copy(k, block, to, src=None):
            return pltpu.make_async_remote_copy(
                src_ref=rows(*block) if src is None else src, dst_ref=rows(*block),
                send_sem=send_sems.at[k], recv_sem=recv_sems.at[k], device_id=to, device_id_type=MESH)

        mine = pltpu.make_async_copy(x_ref, rows(*me), local_sem)
        mine.start()
        first = [copy(0, me, sibling, src=x_ref)]
        first += [copy(1 + j, me, (*chip, c), src=x_ref) for j, chip in enumerate(chips)]
        for cp in first:
            cp.start()
        passed = [copy(4 + j, (*chip, c), sibling) for j, chip in enumerate(chips)]
        for j, chip in enumerate(chips):
            copy(1 + j, (*chip, c), me).wait_recv()
            passed[j].start()
        copy(0, sibling, me).wait_recv()
        for j, chip in enumerate(chips):
            copy(4 + j, (*chip, 1 - c), me).wait_recv()
        for cp in first + passed:
            cp.wait_send()
        mine.wait()

    return pl.pallas_call(
        _ordered(body, 1, after), name=name,
        out_shape=jax.ShapeDtypeStruct((N_DEV * m_per, n), v.dtype),
        in_specs=[pl.BlockSpec(memory_space=pltpu.VMEM)] + [pl.BlockSpec(memory_space=pl.ANY)] * len(after),
        out_specs=pl.BlockSpec(memory_space=pltpu.VMEM),
        scratch_shapes=[pltpu.SemaphoreType.DMA((7,)), pltpu.SemaphoreType.DMA((7,)), pltpu.SemaphoreType.DMA],
        compiler_params=pltpu.CompilerParams(vmem_limit_bytes=VMEM_LIMIT),
    )(v, *after)


_HBM = pl.BlockSpec(memory_space=pltpu.HBM)
_ANY = pl.BlockSpec(memory_space=pl.ANY)
_SEM = pl.BlockSpec(memory_space=pltpu.SEMAPHORE)
_EFFECT = pltpu.SideEffectType.DATAFLOW_SIDE_EFFECTING


def _ordered(body, n_in, after):
    k = len(after)
    if not k:
        return body
    return lambda *refs: body(*refs[:n_in], *refs[n_in + k:])


def _gather_copies(ins, lands, send_sems, recv_sems):
    x, y, c = _my_pos()
    p = 2 * x + y
    peers = [(x, 1 - y), (1 - x, y), (1 - x, 1 - y)]
    sends, recvs = [], []
    for k in range(len(ins)):
        for j, (qx, qy) in enumerate(peers):
            sems = dict(send_sem=send_sems.at[3 * k + j], recv_sem=recv_sems.at[3 * k + j],
                        device_id=(qx, qy, c), device_id_type=MESH)
            sends.append(pltpu.make_async_remote_copy(src_ref=ins[k], dst_ref=lands[k].at[p], **sems))
            recvs.append(pltpu.make_async_remote_copy(src_ref=ins[k], dst_ref=lands[k].at[2 * qx + qy], **sems))
    return sends, recvs


def _to_sibling_copies(ins, lands, send_sems, recv_sems):
    x, y, c = _my_pos()
    cps = [pltpu.make_async_remote_copy(
        src_ref=ins[k].at[:, 1 - c], dst_ref=lands[k], send_sem=send_sems.at[k], recv_sem=recv_sems.at[k],
        device_id=(x, y, 1 - c), device_id_type=MESH) for k in range(len(ins))]
    return cps, cps


def _chip_exchange_copies(ins, lands, send_sems, recv_sems):
    x, y, c = _my_pos()
    peers = [(x, 1 - y), (1 - x, y), (1 - x, 1 - y)]
    cps = []
    for k in range(len(ins)):
        for j, (qx, qy) in enumerate(peers):
            cps.append(pltpu.make_async_remote_copy(
                src_ref=ins[k].at[2 * qx + qy], dst_ref=lands[k].at[j], send_sem=send_sems.at[3 * k + j],
                recv_sem=recv_sems.at[3 * k + j], device_id=(qx, qy, c), device_id_type=MESH))
    return cps, cps


def _all_devices_copies(ins, lands, send_sems, recv_sems):
    x, y, c = _my_pos()
    me = 4 * x + 2 * y + c
    sends, recvs = [], []
    for r in range(1, N_DEV):
        px = 1 - x if r & 4 else x
        py = 1 - y if r & 2 else y
        pc = 1 - c if r & 1 else c
        sems = dict(send_sem=send_sems.at[r - 1], recv_sem=recv_sems.at[r - 1], device_id=(px, py, pc), device_id_type=MESH)
        sends.append(pltpu.make_async_remote_copy(src_ref=ins[0], dst_ref=lands[0].at[me], **sems))
        recvs.append(pltpu.make_async_remote_copy(src_ref=ins[0], dst_ref=lands[0].at[4 * px + 2 * py + pc], **sems))
    return sends, recvs


def _swap_copies(ins, lands, send_sems, recv_sems):
    x, y, c = _my_pos()
    cps = [pltpu.make_async_remote_copy(
        src_ref=ins[k], dst_ref=lands[k], send_sem=send_sems.at[k], recv_sem=recv_sems.at[k],
        device_id=(x, y, 1 - c), device_id_type=MESH) for k in range(len(ins))]
    return cps, cps


def _split_start(ins, land_shapes, copies, n_sems, name, after=()):
    n, nl = len(ins), len(land_shapes)
    first_out = n + nl + len(after)

    def body(*refs):
        in_refs, land_refs = refs[:n], refs[n:n + nl]
        send_sems, recv_sems = refs[first_out:first_out + 2]
        token = refs[-1]
        sends, _ = copies(in_refs, land_refs, send_sems, recv_sems)
        for cp in sends:
            cp.start()
        token[...] = jnp.zeros_like(token)

    lands = [pltpu.with_memory_space_constraint(lax.empty(s.shape, s.dtype), pltpu.HBM) for s in land_shapes]
    ins = [pltpu.with_memory_space_constraint(s, pltpu.HBM) for s in ins]
    return pl.pallas_call(
        body, name=name,
        out_shape=(pltpu.SemaphoreType.DMA((n_sems,)), pltpu.SemaphoreType.DMA((n_sems,)),
                   *[pltpu.HBM(s.shape, s.dtype) for s in ins], *[pltpu.HBM(s.shape, s.dtype) for s in lands],
                   jax.ShapeDtypeStruct((8, HEAD), F32)),
        in_specs=[_HBM] * (n + nl) + [pl.BlockSpec(memory_space=pl.ANY)] * len(after),
        out_specs=(_SEM, _SEM, *([_HBM] * (n + nl)), pl.BlockSpec(memory_space=pltpu.VMEM)),
        input_output_aliases={k: 2 + k for k in range(n + nl)},
        compiler_params=pltpu.CompilerParams(has_side_effects=_EFFECT),
    )(*ins, *lands, *after)


def _split_wait(started, n, copies, name, after=()):
    send_sems, recv_sems = started[0], started[1]
    bufs = started[2:-1]
    nb = len(bufs)

    def body(*refs):
        in_refs, land_refs = refs[:n], refs[n:nb]
        sends, recvs = copies(in_refs, land_refs, refs[nb], refs[nb + 1])
        for cp in sends:
            cp.wait_send()
        for cp in recvs:
            cp.wait_recv()

    outs = pl.pallas_call(
        body, name=name,
        out_shape=tuple(pltpu.HBM(s.shape, s.dtype) for s in bufs),
        in_specs=[_HBM] * nb + [_SEM, _SEM] + [pl.BlockSpec(memory_space=pl.ANY)] * len(after),
        out_specs=tuple([_HBM] * nb),
        input_output_aliases={k: k for k in range(nb)},
        compiler_params=pltpu.CompilerParams(has_side_effects=_EFFECT),
    )(*bufs, send_sems, recv_sems, *after)
    return list(outs[:n]), list(outs[n:])


def _fill_own_slot(gathered, shards, pidx, names):
    outs = []
    for g, s, name in zip(gathered, shards, names):
        r, cdim = s.shape
        tr = _row_tile(r)

        def body(p_ref, s_ref, g_ref, o_ref):
            o_ref[...] = s_ref[...]

        outs.append(pl.pallas_call(
            body, name=name,
            grid_spec=pltpu.PrefetchScalarGridSpec(
                num_scalar_prefetch=1, grid=(r // tr,),
                in_specs=[pl.BlockSpec((tr, cdim), lambda i, p: (i, 0)), pl.BlockSpec(memory_space=pl.ANY)],
                out_specs=pl.BlockSpec((None, tr, cdim), lambda i, p: (p[0], i, 0))),
            out_shape=jax.ShapeDtypeStruct(g.shape, g.dtype),
            input_output_aliases={2: 0},
            compiler_params=_cp(("arbitrary",)),
        )(pidx, s, g))
    return outs


def _sum_own_and_peers(own4, slots, pidx, name):
    _, rh, cdim = own4.shape
    tr = _row_tile(rh)

    def body(p_ref, own_ref, s_ref, o_ref):
        acc = own_ref[...].astype(F32)
        for j in range(3):
            acc = acc + s_ref[j].astype(F32)
        o_ref[...] = acc

    return pl.pallas_call(
        body, name=name,
        grid_spec=pltpu.PrefetchScalarGridSpec(
            num_scalar_prefetch=1, grid=(rh // tr,),
            in_specs=[pl.BlockSpec((None, tr, cdim), lambda i, p: (p[0], i, 0)),
                      pl.BlockSpec((3, tr, cdim), lambda i, p: (0, i, 0))],
            out_specs=pl.BlockSpec((tr, cdim), lambda i, p: (i, 0))),
        out_shape=jax.ShapeDtypeStruct((rh, cdim), F32),
        compiler_params=_cp(("arbitrary",)),
    )(pidx, own4, slots)


def _exchange(ins, land_shapes, copies, n_sems, name):
    n, nl = len(ins), len(land_shapes)

    def body(*refs):
        sends, recvs = copies(refs[:n], refs[n:n + nl], refs[n + nl], refs[n + nl + 1])
        for cp in sends:
            cp.start()
        for cp in sends:
            cp.wait_send()
        for cp in recvs:
            cp.wait_recv()

    any_spec = pl.BlockSpec(memory_space=pl.ANY)
    return pl.pallas_call(
        body, name=name,
        out_shape=[jax.ShapeDtypeStruct(s.shape, s.dtype) for s in land_shapes],
        in_specs=[any_spec] * n, out_specs=[any_spec] * nl,
        scratch_shapes=[pltpu.SemaphoreType.DMA((n_sems,)), pltpu.SemaphoreType.DMA((n_sems,))],
    )(*ins)


def _row_tile(r):
    t = min(TR_EW, r)
    while r % t:
        t //= 2
    return t


def _add_own_half(g4, recv, cidx, name):
    _, _, rh, cdim = g4.shape
    tr = _row_tile(rh)

    def body(c_ref, a_ref, b_ref, o_ref):
        o_ref[...] = (a_ref[...] + b_ref[...]).astype(BF16)

    return pl.pallas_call(
        body, name=name,
        grid_spec=pltpu.PrefetchScalarGridSpec(
            num_scalar_prefetch=1, grid=(N_CHIPS, rh // tr),
            in_specs=[pl.BlockSpec((None, None, tr, cdim), lambda q, i, c: (q, c[0], i, 0)),
                      pl.BlockSpec((None, tr, cdim), lambda q, i, c: (q, i, 0))],
            out_specs=pl.BlockSpec((None, tr, cdim), lambda q, i, c: (q, i, 0))),
        out_shape=jax.ShapeDtypeStruct(recv.shape, BF16),
        compiler_params=_cp(("arbitrary", "arbitrary")),
    )(cidx, g4, recv)


def _sum_slots(v, name):
    n, r, cdim = v.shape
    tr = _row_tile(r)

    def body(v_ref, o_ref):
        acc = v_ref[0].astype(F32)
        for k in range(1, n):
            acc = acc + v_ref[k].astype(F32)
        o_ref[...] = acc

    return pl.pallas_call(
        body, name=name, grid=(r // tr,),
        in_specs=[pl.BlockSpec((n, tr, cdim), lambda i: (0, i, 0))],
        out_specs=pl.BlockSpec((tr, cdim), lambda i: (i, 0)),
        out_shape=jax.ShapeDtypeStruct((r, cdim), F32),
        compiler_params=_cp(("arbitrary",)),
    )(v)


def _sum_devices(lands, own, didx, name):
    _, r, cdim = lands.shape
    tr = _row_tile(r)

    def body(d_ref, l_ref, own_ref, o_ref):
        acc = jnp.where(d_ref[0] == 0, own_ref[...], l_ref[0])
        for dv in range(1, N_DEV):
            acc = acc + jnp.where(d_ref[0] == dv, own_ref[...], l_ref[dv])
        o_ref[...] = acc

    return pl.pallas_call(
        body, name=name,
        grid_spec=pltpu.PrefetchScalarGridSpec(
            num_scalar_prefetch=1, grid=(r // tr,),
            in_specs=[pl.BlockSpec((N_DEV, tr, cdim), lambda i, dd: (0, i, 0)), pl.BlockSpec((tr, cdim), lambda i, dd: (i, 0))],
            out_specs=pl.BlockSpec((tr, cdim), lambda i, dd: (i, 0))),
        out_shape=jax.ShapeDtypeStruct((r, cdim), F32),
        compiler_params=_cp(("arbitrary",)),
    )(didx, lands, own)


def _adamw_math(wv, gg, mv, vv):
    nm = ADAM_B1 * mv + (1.0 - ADAM_B1) * gg
    nv = ADAM_B2 * vv + (1.0 - ADAM_B2) * (gg * gg)
    m_hat = nm / (1.0 - ADAM_B1 ** ADAM_STEP)
    v_hat = nv / (1.0 - ADAM_B2 ** ADAM_STEP)
    return -ADAM_LR * (m_hat / (jnp.sqrt(v_hat) + ADAM_EPS) + ADAM_WD * wv), nm, nv


def _adamw_halves(w, mine, theirs, m, v, cidx, name):
    r, cdim = w.shape
    rh = r // 2
    tr = _row_tile(rh)
    nblk = rh // tr

    def body(c_ref, w_ref, a_ref, b_ref, m_ref, v_ref, g_ref, d_ref, nm_ref, nv_ref):
        gg = jnp.where(pl.program_id(0) == c_ref[0], a_ref[...], b_ref[...])
        g_ref[...] = gg
        d_ref[...], nm_ref[...], nv_ref[...] = _adamw_math(w_ref[...], gg, m_ref[...], v_ref[...])

    full = pl.BlockSpec((tr, cdim), lambda hh, i, c: (hh * nblk + i, 0))
    half = pl.BlockSpec((tr, cdim), lambda hh, i, c: (i, 0))
    return pl.pallas_call(
        body, name=name,
        grid_spec=pltpu.PrefetchScalarGridSpec(
            num_scalar_prefetch=1, grid=(2, nblk),
            in_specs=[full, half, half, full, full], out_specs=[full] * 4),
        out_shape=[jax.ShapeDtypeStruct((r, cdim), F32)] * 4,
        compiler_params=_cp(("arbitrary", "arbitrary")),
    )(cidx, w, mine, theirs, m, v)


def _adamw(w, g, m, v, name):
    r, cdim = w.shape
    tr = _row_tile(r) if r % 8 == 0 else r

    def body(w_ref, g_ref, m_ref, v_ref, d_ref, nm_ref, nv_ref):
        d_ref[...], nm_ref[...], nv_ref[...] = _adamw_math(w_ref[...], g_ref[...], m_ref[...], v_ref[...])

    spec = pl.BlockSpec((tr, cdim), lambda i: (i, 0))
    return pl.pallas_call(
        body, name=name, grid=(r // tr,), in_specs=[spec] * 4, out_specs=[spec] * 3,
        out_shape=[jax.ShapeDtypeStruct((r, cdim), F32)] * 3,
        compiler_params=_cp(("arbitrary",)),
    )(w, g, m, v)


def _ada_fwd(c_all, w_ada, b_cols):
    nb, _ = c_all.shape
    n = w_ada.shape[1]

    def body(c_ref, w_ref, b_ref, o_ref):
        cv = c_ref[...]
        o_ref[...] = _mm(cv * _sigmoid(cv), w_ref[...]) + b_ref[...]

    return pl.pallas_call(
        body, name="ada_fwd", out_shape=jax.ShapeDtypeStruct((nb, n), F32),
        compiler_params=pltpu.CompilerParams(vmem_limit_bytes=VMEM_LIMIT),
    )(c_all, w_ada, b_cols)


def _ada_bwd(c_all, dmod_all, dmod_cols):
    d = c_all.shape[1]
    n = dmod_cols.shape[1]

    def body(c_ref, da_ref, dc_ref, gw_ref, gb_ref):
        cv = c_ref[...]
        gw_ref[...] = _mm_tn(cv * _sigmoid(cv), dc_ref[...])
        gb_ref[...] = _colsum(da_ref[...])

    return pl.pallas_call(
        body, name="ada_bwd",
        out_shape=[jax.ShapeDtypeStruct((d, n), F32), jax.ShapeDtypeStruct((1, dmod_all.shape[1]), F32)],
        compiler_params=pltpu.CompilerParams(vmem_limit_bytes=VMEM_LIMIT),
    )(c_all, dmod_all, dmod_cols)


def _proj_fwd(x2, modv, w_in_g, b_in, seq):
    t, d = x2.shape
    nq, _, ns = w_in_g.shape
    tm = min(TM_PROJ, seq)
    tpb = seq // tm

    def body(x_ref, mod_ref, w_ref, b_ref, proj_ref, h_ref, h_s):
        @pl.when(pl.program_id(1) == 0)
        def _():
            h = x_ref[...] * (1.0 + mod_ref[1:2, :]) + mod_ref[0:1, :]
            h_s[...] = h.astype(BF16)
            h_ref[...] = h.astype(BF16)

        proj_ref[...] = jnp.dot(h_s[...], w_ref[...], preferred_element_type=F32) + b_ref[...]

    return pl.pallas_call(
        body, name="proj_fwd", grid=(t // tm, nq),
        in_specs=[pl.BlockSpec((tm, d), lambda i, q: (i, 0)),
                  pl.BlockSpec((None, 8, d), lambda i, q: (i // tpb, 0, 0)),
                  pl.BlockSpec((None, d, ns), lambda i, q: (q, 0, 0)),
                  pl.BlockSpec((1, ns), lambda i, q: (0, q))],
        out_specs=[pl.BlockSpec((tm, ns), lambda i, q: (i, q)),
                   pl.BlockSpec((tm, d), lambda i, q: (i, 0))],
        out_shape=[jax.ShapeDtypeStruct((t, nq * ns), F32), jax.ShapeDtypeStruct((t, d), BF16)],
        scratch_shapes=[pltpu.VMEM((tm, d), BF16)],
        compiler_params=_cp(("arbitrary", "arbitrary")),
    )(x2, modv, w_in_g, b_in)


def _lru_gates(xl, wc_ref, bc_ref, wa_ref, ba_ref, wx_ref, bx_ref, lam_ref):
    xc = bc_ref[...] + wc_ref[CONV_WIDTH - 1:CONV_WIDTH, :] * xl
    for k in range(CONV_WIDTH - 1):
        xc = xc + wc_ref[k:k + 1, :] * _shift_down(xl, CONV_WIDTH - 1 - k)
    r = _sigmoid(_mm(xc, wa_ref[...]) + ba_ref[...])
    gi = _sigmoid_t(_mm(xc, wx_ref[...]) + bx_ref[...])
    nl = -lam_ref[...]
    e = jnp.exp(-jnp.abs(nl))
    u = 1.0 + e
    dlt = u - 1.0
    log1p_e = jnp.where(dlt == 0.0, e, jnp.log(u) * (e / jnp.where(dlt == 0.0, 1.0, dlt)))
    big_l = -LRU_C * (jnp.maximum(nl, 0.0) + log1p_e)
    la = big_l * r
    a = jnp.exp(la)
    m2 = jnp.tanh(-la) * (a * a + 1.0)
    return xc, r, gi, big_l, a, m2


def _lru_prep(proj, lru_w, nb, seq):
    t = proj.shape[0]
    w = LRU_HEADS * HEAD
    w_conv, b_conv, w_a, b_a, w_x, b_x, lam = lru_w

    def body(x_ref, wc_ref, bc_ref, wa_ref, ba_ref, wx_ref, bx_ref, lam_ref, a_ref, inp_ref):
        xc, r, gi, big_l, a, m2 = _lru_gates(x_ref[...], wc_ref, bc_ref, wa_ref, ba_ref, wx_ref, bx_ref, lam_ref)
        a_ref[...] = a
        inp_ref[...] = jnp.sqrt(m2) * (gi * xc)

    col = lambda b, hd: (0, hd)
    head = lambda b, hd: (hd, 0, 0)
    tok = lambda b, hd: (b, hd)
    return pl.pallas_call(
        body, name="lru_prep", grid=(nb, LRU_HEADS),
        in_specs=[pl.BlockSpec((seq, HEAD), tok),
                  pl.BlockSpec((CONV_WIDTH, HEAD), col), pl.BlockSpec((1, HEAD), col),
                  pl.BlockSpec((None, HEAD, HEAD), head), pl.BlockSpec((1, HEAD), col),
                  pl.BlockSpec((None, HEAD, HEAD), head), pl.BlockSpec((1, HEAD), col),
                  pl.BlockSpec((1, HEAD), col)],
        out_specs=[pl.BlockSpec((seq, HEAD), tok)] * 2,
        out_shape=[jax.ShapeDtypeStruct((t, w), F32)] * 2,
        compiler_params=_cp(("arbitrary", "arbitrary")),
    )(proj, w_conv, b_conv, w_a, b_a, w_x, b_x, lam)


def _scan(a3, b3, reverse, name):
    nb, seq, w = a3.shape
    tc = min(TC_SCAN, seq)
    nchunk = seq // tc
    ntile = tc // 8

    def combine(av, bv):
        rows = lax.broadcasted_iota(jnp.int32, av.shape, 0)
        for s in (1, 2, 4):
            if reverse:
                keep = rows < 8 - s
                a_sh, b_sh = pltpu.roll(av, 8 - s, 0), pltpu.roll(bv, 8 - s, 0)
            else:
                keep = rows >= s
                a_sh, b_sh = pltpu.roll(av, s, 0), pltpu.roll(bv, s, 0)
            bv = jnp.where(keep, bv + av * b_sh, bv)
            av = jnp.where(keep, av * a_sh, av)
        return av, bv

    def body(a_ref, b_ref, h_ref, carry):
        @pl.when(pl.program_id(0) == 0)
        def _():
            carry[...] = jnp.zeros_like(carry)

        for b in range(nb):
            def tile(j, hprev):
                jj = ntile - 1 - j if reverse else j
                base = pl.multiple_of(jj * 8, 8)
                av, bv = a_ref[b, pl.ds(base, 8), :], b_ref[b, pl.ds(base, 8), :]
                av, bv = combine(av, av * bv if reverse else bv)
                h = bv + av * hprev
                h_ref[b, pl.ds(base, 8), :] = h
                edge = h[0:1, :] if reverse else h[7:8, :]
                return jnp.broadcast_to(edge, (8, w))

            carry[b] = lax.fori_loop(0, ntile, tile, carry[b])

    imap = (lambda i: (0, nchunk - 1 - i, 0)) if reverse else (lambda i: (0, i, 0))
    spec = pl.BlockSpec((nb, tc, w), imap)
    return pl.pallas_call(
        body, name=name, grid=(nchunk,), in_specs=[spec, spec], out_specs=spec,
        out_shape=jax.ShapeDtypeStruct((nb, seq, w), F32),
        scratch_shapes=[pltpu.VMEM((nb, 8, w), F32)],
        compiler_params=_cp(("arbitrary",)),
    )(a3, b3)


def _sgu_mask():
    ti = lax.broadcasted_iota(jnp.int32, (HEAD, HEAD), 0) // SGU_CHUNK
    si = lax.broadcasted_iota(jnp.int32, (HEAD, HEAD), 1) // SGU_CHUNK
    return si <= ti


def _sgu_specs(tm, d_sgu):
    pw = 256
    first_u = (2 * LRU_HEADS * HEAD) // pw
    n_piece = d_sgu // pw
    specs = [pl.BlockSpec((tm, pw), functools.partial(lambda i, k: (i, k), k=first_u + j)) for j in range(2 * n_piece)]
    return specs, n_piece


def _sgu_fwd(proj, w_sp, b_sp_t, ln_g, ln_b):
    t = proj.shape[0]
    d_sgu = SGU_GROUPS * HEAD
    tm = min(TM_SGU, t)
    nblk = tm // HEAD
    specs, n_piece = _sgu_specs(tm, d_sgu)

    def body(*refs):
        u = jnp.concatenate([r[...] for r in refs[:n_piece]], axis=1)
        v = jnp.concatenate([r[...] for r in refs[n_piece:2 * n_piece]], axis=1)
        w_ref, bt_ref, g_ref, b_ref, y_ref = refs[2 * n_piece:]
        ug = _gelu(u)
        xhat, _ = _ln_stats(_gelu(v))
        vn = (xhat * g_ref[...] + b_ref[...]).astype(BF16)
        mask = _sgu_mask()
        for g in range(SGU_GROUPS):
            wm = jnp.where(mask, w_ref[g], 0.0).astype(BF16)
            cols = slice(g * HEAD, (g + 1) * HEAD)
            for n in range(nblk):
                rows = slice(n * HEAD, (n + 1) * HEAD)
                mixed = jnp.dot(wm, vn[rows, cols], preferred_element_type=F32) + bt_ref[:, g:g + 1]
                y_ref[rows, cols] = (ug[rows, cols] * mixed).astype(BF16)

    full = lambda shape: pl.BlockSpec(shape, lambda i: (0,) * len(shape))
    return pl.pallas_call(
        body, name="sgu_fwd", grid=(t // tm,),
        in_specs=specs + [full(w_sp.shape), full(b_sp_t.shape), full(ln_g.shape), full(ln_b.shape)],
        out_specs=pl.BlockSpec((tm, d_sgu), lambda i: (i, 0)),
        out_shape=jax.ShapeDtypeStruct((t, d_sgu), BF16),
        compiler_params=_cp(("arbitrary",)),
    )(*([proj] * (2 * n_piece)), w_sp, b_sp_t, ln_g, ln_b)


def _mix_fwd(hs, proj, y_sgu, x2, modv, w_o_lru_g, w_o_sgu_g, w_out_g, ln1_g, ln1_b, seq):
    t, d = x2.shape
    w = hs.shape[1]
    d_sgu = y_sgu.shape[1]
    nq, _, ns = w_o_sgu_g.shape
    tm = min(TM_MIX, seq)
    tpb = seq // tm

    def body(hs_ref, gl_ref, ys_ref, ga_ref, gb_ref, x_ref, mod_ref, wl_ref, ws_ref, wo_ref, g1_ref, b1_ref,
             yap_ref, ya_ref, yb_ref, mg_ref, mix_ref, x1_ref):
        yap = (hs_ref[...] * _gelu(gl_ref[...])).astype(BF16)
        yap_ref[...] = yap
        y_a = jnp.dot(yap, wl_ref[...], preferred_element_type=F32)
        ys = ys_ref[...]
        y_b = jnp.concatenate([jnp.dot(ys, ws_ref[q], preferred_element_type=F32) for q in range(nq)], axis=1)
        ya_ref[...] = y_a.astype(BF16)
        yb_ref[...] = y_b.astype(BF16)
        merged = (_sigmoid_t(ga_ref[...]) * y_a + _sigmoid_t(gb_ref[...]) * y_b).astype(BF16)
        mg_ref[...] = merged
        mix = jnp.dot(merged, wo_ref[...], preferred_element_type=F32)
        mix_ref[...] = mix
        xhat, _ = _ln_stats(ALPHA * x_ref[...] + (1.0 + mod_ref[2:3, :]) * mix)
        x1_ref[...] = xhat * g1_ref[...] + b1_ref[...]

    row = lambda width, col: pl.BlockSpec((tm, width), functools.partial(lambda i, k: (i, k), k=col))
    full = lambda shape: pl.BlockSpec(shape, lambda i: (0,) * len(shape))
    return pl.pallas_call(
        body, name="mix_fwd", grid=(t // tm,),
        in_specs=[row(w, 0), row(w, 1), row(d_sgu, 0), row(d, 4), row(d, 5), row(d, 0),
                  pl.BlockSpec((None, 8, d), lambda i: (i // tpb, 0, 0)),
                  full(w_o_lru_g.shape), full(w_o_sgu_g.shape), full(w_out_g.shape), full(ln1_g.shape), full(ln1_b.shape)],
        out_specs=[row(w, 0), row(d, 0), row(d, 0), row(d, 0), row(d, 0), row(d, 0)],
        out_shape=[jax.ShapeDtypeStruct((t, w), BF16), jax.ShapeDtypeStruct((t, d), BF16),
                   jax.ShapeDtypeStruct((t, d), BF16), jax.ShapeDtypeStruct((t, d), BF16),
                   jax.ShapeDtypeStruct((t, d), F32), jax.ShapeDtypeStruct((t, d), F32)],
        compiler_params=_cp(("arbitrary",)),
    )(hs, proj, y_sgu, proj, proj, x2, modv, w_o_lru_g, w_o_sgu_g, w_out_g, ln1_g, ln1_b)


def _mlp_fwd(x1, modv, w_up_g, w_down_g, ln2_g, ln2_b, target, nb, seq):
    t, d = x1.shape
    nq, _, ns = w_up_g.shape
    tm = min(TM_MLP, seq)
    tpb = seq // tm
    nt = t // tm

    def body(x1_ref, mod_ref, wu_ref, wd_ref, g2_ref, b2_ref, tg_ref,
             rl_ref, act_ref, h2_ref, dz2_ref, df_ref, st_ref, pb_ref, h2_s, acc):
        i, j = pl.program_id(0), pl.program_id(1)

        @pl.when(j == 0)
        def _():
            h2 =(x1_ref[...] * (1.0 + mod_ref[4:5, :]) + mod_ref[3:4, :]).astype(BF16)
            h2_s[...] = h2
            h2_ref[...] = h2
            acc[...] = jnp.zeros_like(acc)

        @pl.when((i == 0) & (j == 0))
        def _():
            st_ref[...] = jnp.zeros_like(st_ref)

        @pl.when((i % tpb == 0) & (j == 0))
        def _():
            pb_ref[...] = jnp.zeros_like(pb_ref)

        r = jnp.maximum(jnp.dot(h2_s[...], wu_ref[...], preferred_element_type=F32), 0.0)
        act = (r * r).astype(BF16)
        rl_ref[...] = r.astype(BF16)
        act_ref[...] = act
        acc[...] += jnp.dot(act, wd_ref[...], preferred_element_type=F32)

        @pl.when(j == nq - 1)
        def _():
            f = acc[...]
            xhat, rstd = _ln_stats(ALPHA * x1_ref[...] + (1.0 + mod_ref[5:6, :]) * f)
            y = xhat * g2_ref[...] + b2_ref[...]
            err = y - tg_ref[...]
            dy = err * (1.0 / d)
            dz2 = _ln_bwd(dy * g2_ref[...], xhat, rstd)
            dz2_ref[...] = dz2
            df_ref[...] = ((1.0 + mod_ref[5:6, :]) * dz2).astype(BF16)
            st_ref[0:1, :] += _colsum(dy * xhat)
            st_ref[1:2, :] += _colsum(dy)
            st_ref[2:3, :] += (0.5 / d) * jnp.sum(_colsum(err * err), axis=1, keepdims=True)
            pb_ref[0:1, :] += _colsum(dz2 * f)

    tok = lambda i, j: (i, 0)
    return pl.pallas_call(
        body, name="mlp_fwd", grid=(nt, nq),
        in_specs=[pl.BlockSpec((tm, d), tok), pl.BlockSpec((None, 8, d), lambda i, j: (i // tpb, 0, 0)),
                  pl.BlockSpec((None, d, ns), lambda i, j: (j, 0, 0)), pl.BlockSpec((ns, d), lambda i, j: (j, 0)),
                  pl.BlockSpec((1, d), lambda i, j: (0, 0)), pl.BlockSpec((1, d), lambda i, j: (0, 0)),
                  pl.BlockSpec((tm, d), tok)],
        out_specs=[pl.BlockSpec((tm, ns), lambda i, j: (i, j)), pl.BlockSpec((tm, ns), lambda i, j: (i, j)),
                   pl.BlockSpec((tm, d), tok), pl.BlockSpec((tm, d), tok), pl.BlockSpec((tm, d), tok),
                   pl.BlockSpec((8, d), lambda i, j: (0, 0)), pl.BlockSpec((None, 8, d), lambda i, j: (i // tpb, 0, 0))],
        out_shape=[jax.ShapeDtypeStruct((t, nq * ns), BF16), jax.ShapeDtypeStruct((t, nq * ns), BF16),
                   jax.ShapeDtypeStruct((t, d), BF16),
                   jax.ShapeDtypeStruct((t, d), F32), jax.ShapeDtypeStruct((t, d), BF16),
                   jax.ShapeDtypeStruct((8, d), F32), jax.ShapeDtypeStruct((nb, 8, d), F32)],
        scratch_shapes=[pltpu.VMEM((tm, d), BF16), pltpu.VMEM((tm, d), F32)],
        compiler_params=_cp(("arbitrary", "arbitrary")),
    )(x1, modv, w_up_g, w_down_g, ln2_g, ln2_b, target)


def _mlp_bwd(df, up, w_down_g, w_up_g, dz2, x2, mix, modv, ln1_g, ln1_b, nb, seq):
    t, d = x2.shape
    nq, _, ns = w_up_g.shape
    tm = min(TM_MLP, seq)
    tpb = seq // tm

    def body(df_ref, rl_ref, wd_ref, wu_ref, dz2_ref, x_ref, mix_ref, mod_ref, g1_ref, b1_ref,
             dup_ref, dz1_ref, dmix_ref, st_ref, pb_ref, acc):
        i, j = pl.program_id(0), pl.program_id(1)

        @pl.when(j == 0)
        def _():
            acc[...] = jnp.zeros_like(acc)

        @pl.when((i == 0) & (j == 0))
        def _():
            st_ref[...] = jnp.zeros_like(st_ref)

        @pl.when((i % tpb == 0) & (j == 0))
        def _():
            pb_ref[...] = jnp.zeros_like(pb_ref)

        dup = (_mm_nt(df_ref[...], wd_ref[...]) * (2.0 * rl_ref[...].astype(F32))).astype(BF16)
        dup_ref[...] = dup
        acc[...] += _mm_nt(dup, wu_ref[...])

        @pl.when(j == nq - 1)
        def _():
            dh2 = acc[...]
            mix = mix_ref[...]
            xhat, rstd = _ln_stats(ALPHA * x_ref[...] + (1.0 + mod_ref[2:3, :]) * mix)
            x1 = xhat * g1_ref[...] + b1_ref[...]
            dx1 = ALPHA * dz2_ref[...] + dh2 * (1.0 + mod_ref[4:5, :])
            dz1 = _ln_bwd(dx1 * g1_ref[...], xhat, rstd)
            dz1_ref[...] = dz1
            dmix_ref[...] = ((1.0 + mod_ref[2:3, :]) * dz1).astype(BF16)
            st_ref[0:1, :] += _colsum(dx1 * xhat)
            st_ref[1:2, :] += _colsum(dx1)
            pb_ref[0:1, :] += _colsum(dh2 * x1)
            pb_ref[1:2, :] += _colsum(dh2)
            pb_ref[2:3, :] += _colsum(dz1 * mix)

    tok = lambda i, j: (i, 0)
    chunk = lambda i, j: (i, j)
    return pl.pallas_call(
        body, name="mlp_bwd", grid=(t // tm, nq),
        in_specs=[pl.BlockSpec((tm, d), tok), pl.BlockSpec((tm, ns), chunk),
                  pl.BlockSpec((ns, d), lambda i, j: (j, 0)), pl.BlockSpec((None, d, ns), lambda i, j: (j, 0, 0)),
                  pl.BlockSpec((tm, d), tok), pl.BlockSpec((tm, d), tok), pl.BlockSpec((tm, d), tok),
                  pl.BlockSpec((None, 8, d), lambda i, j: (i // tpb, 0, 0)),
                  pl.BlockSpec((1, d), lambda i, j: (0, 0)), pl.BlockSpec((1, d), lambda i, j: (0, 0))],
        out_specs=[pl.BlockSpec((tm, ns), chunk),
                   pl.BlockSpec((tm, d), tok), pl.BlockSpec((tm, d), tok),
                   pl.BlockSpec((8, d), lambda i, j: (0, 0)), pl.BlockSpec((None, 8, d), lambda i, j: (i // tpb, 0, 0))],
        out_shape=[jax.ShapeDtypeStruct((t, nq * ns), BF16),
                   jax.ShapeDtypeStruct((t, d), F32), jax.ShapeDtypeStruct((t, d), BF16),
                   jax.ShapeDtypeStruct((8, d), F32), jax.ShapeDtypeStruct((nb, 8, d), F32)],
        scratch_shapes=[pltpu.VMEM((tm, d), F32)],
        compiler_params=_cp(("arbitrary", "arbitrary")),
    )(df, up, w_down_g, w_up_g, dz2, x2, mix, modv, ln1_g, ln1_b)


def _mix_bwd(dmix, proj, y_a, y_b, hs, w_out_g, w_o_lru_g, w_o_sgu_g, seq, after=()):
    t, d = dmix.shape
    w = hs.shape[1]
    nq, d_sgu, ns = w_o_sgu_g.shape
    tm = min(TM_MIX, seq)

    def body(dmix_ref, ga_ref, gb_ref, ya_ref, yb_ref, gl_ref, hs_ref, wo_ref, wl_ref, ws_ref,
             dya_ref, dyb_ref, dga_ref, dgb_ref, dgl_ref, dyl_ref, dys_ref):
        dmerged = _mm_nt(dmix_ref[...], wo_ref[...])
        sa, sb = _sigmoid_t(ga_ref[...]), _sigmoid_t(gb_ref[...])
        dy_a = (dmerged * sa).astype(BF16)
        dy_b = (dmerged * sb).astype(BF16)
        dya_ref[...] = dy_a
        dyb_ref[...] = dy_b
        dga_ref[...] = (dmerged * ya_ref[...].astype(F32) * (sa * (1.0 - sa))).astype(BF16)
        dgb_ref[...] = (dmerged * yb_ref[...].astype(F32) * (sb * (1.0 - sb))).astype(BF16)
        dyap = _mm_nt(dy_a, wl_ref[...])
        gel, dgel = _gelu_and_grad(gl_ref[...])
        dyl_ref[...] = dyap * gel
        dgl_ref[...] = (dyap * hs_ref[...] * dgel).astype(BF16)
        dys = _mm_nt(dy_b[:, 0:ns], ws_ref[0])
        for q in range(1, nq):
            dys = dys + _mm_nt(dy_b[:, q * ns:(q + 1) * ns], ws_ref[q])
        dys_ref[...] = dys

    row = lambda width, col: pl.BlockSpec((tm, width), functools.partial(lambda i, k: (i, k), k=col))
    full = lambda shape: pl.BlockSpec(shape, lambda i: (0,) * len(shape))
    return pl.pallas_call(
        _ordered(body, 10, after), name="mix_bwd", grid=(t // tm,),
        in_specs=[row(d, 0), row(d, 4), row(d, 5), row(d, 0), row(d, 0), row(w, 1), row(w, 0),
                  full(w_out_g.shape), full(w_o_lru_g.shape), full(w_o_sgu_g.shape)] + [_ANY] * len(after),
        out_specs=[row(d, 0), row(d, 0), row(d, 0), row(d, 0), row(w, 0), row(w, 0), row(d_sgu, 0)],
        out_shape=[jax.ShapeDtypeStruct((t, d), BF16), jax.ShapeDtypeStruct((t, d), BF16),
                   jax.ShapeDtypeStruct((t, d), BF16), jax.ShapeDtypeStruct((t, d), BF16),
                   jax.ShapeDtypeStruct((t, w), BF16), jax.ShapeDtypeStruct((t, w), F32),
                   jax.ShapeDtypeStruct((t, d_sgu), F32)],
        compiler_params=_cp(("arbitrary",)),
    )(dmix, proj, proj, y_a, y_b, proj, hs, w_out_g, w_o_lru_g, w_o_sgu_g, *after)


def _sgu_bwd(proj, dys, w_sp, b_sp_t, ln_g, ln_b, after=()):
    t = proj.shape[0]
    d_sgu = SGU_GROUPS * HEAD
    tm = min(TM_SGU, t)
    nblk = tm // HEAD
    specs, n_piece = _sgu_specs(tm, d_sgu)

    def body(*refs):
        u = jnp.concatenate([r[...] for r in refs[:n_piece]], axis=1)
        v = jnp.concatenate([r[...] for r in refs[n_piece:2 * n_piece]], axis=1)
        dys_ref, w_ref, bt_ref, g_ref, b_ref, du_ref, dv_ref, dw_ref, st_ref, dbt_ref, dvn_s = refs[2 * n_piece:]

        @pl.when(pl.program_id(0) == 0)
        def _():
            dw_ref[...] = jnp.zeros_like(dw_ref)
            st_ref[...] = jnp.zeros_like(st_ref)
            dbt_ref[...] = jnp.zeros_like(dbt_ref)

        ug, dug_du = _gelu_and_grad(u)
        vg, dvg_dv = _gelu_and_grad(v)
        xhat, rstd = _ln_stats(vg)
        vn = (xhat * g_ref[...] + b_ref[...]).astype(BF16)
        dys_v = dys_ref[...]
        mask = _sgu_mask()
        for g in range(SGU_GROUPS):
            wm = jnp.where(mask, w_ref[g], 0.0).astype(BF16)
            cols = slice(g * HEAD, (g + 1) * HEAD)
            dw_g = jnp.zeros((HEAD, HEAD), F32)
            db_g = jnp.zeros((HEAD, 1), F32)
            for n in range(nblk):
                rows = slice(n * HEAD, (n + 1) * HEAD)
                vn_blk = vn[rows, cols]
                mixed = jnp.dot(wm, vn_blk, preferred_element_type=F32) + bt_ref[:, g:g + 1]
                dy_blk = dys_v[rows, cols]
                du_ref[rows, cols] = (dy_blk * mixed * dug_du[rows, cols]).astype(BF16)
                dmx = dy_blk * ug[rows, cols]
                dvn_s[rows, cols] = _mm_tn(wm, dmx)
                dw_g = dw_g + _mm_nt(dmx, vn_blk)
                db_g = db_g + jnp.sum(dmx, axis=1, keepdims=True)
            dw_ref[g] += jnp.where(mask, dw_g, 0.0)
            dbt_ref[:, g:g + 1] += db_g
        dvn = dvn_s[...]
        st_ref[0:1, :] += _colsum(dvn * xhat)
        st_ref[1:2, :] += _colsum(dvn)
        dv_ref[...] = (_ln_bwd(dvn * g_ref[...], xhat, rstd) * dvg_dv).astype(BF16)

    full = lambda shape: pl.BlockSpec(shape, lambda i: (0,) * len(shape))
    tok = pl.BlockSpec((tm, d_sgu), lambda i: (i, 0))
    return pl.pallas_call(
        _ordered(body, 2 * n_piece + 5, after), name="sgu_bwd", grid=(t // tm,),
        in_specs=specs + [tok, full(w_sp.shape), full(b_sp_t.shape), full(ln_g.shape), full(ln_b.shape)]
        + [_ANY] * len(after),
        out_specs=[tok, tok, full(w_sp.shape), full((8, d_sgu)), full((HEAD, HEAD))],
        out_shape=[jax.ShapeDtypeStruct((t, d_sgu), BF16), jax.ShapeDtypeStruct((t, d_sgu), BF16),
                   jax.ShapeDtypeStruct(w_sp.shape, F32), jax.ShapeDtypeStruct((8, d_sgu), F32),
                   jax.ShapeDtypeStruct((HEAD, HEAD), F32)],
        scratch_shapes=[pltpu.VMEM((tm, d_sgu), F32)],
        compiler_params=_cp(("arbitrary",)),
    )(*([proj] * (2 * n_piece)), dys, w_sp, b_sp_t, ln_g, ln_b, *after)


def _lru_bwd(proj, hs, e, dyl, lru_w, nb, seq, after=()):
    t = proj.shape[0]
    w = LRU_HEADS * HEAD
    w_conv, b_conv, w_a, b_a, w_x, b_x, lam = lru_w

    def body(x_ref, hs_ref, e_ref, dy_ref, wc_ref, bc_ref, wa_ref, ba_ref, wx_ref, bx_ref, lam_ref,
             dxl_ref, dwa_ref, dwx_ref, st_ref):
        @pl.when(pl.program_id(1) == 0)
        def _():
            dwa_ref[...] = jnp.zeros_like(dwa_ref)
            dwx_ref[...] = jnp.zeros_like(dwx_ref)
            st_ref[...] = jnp.zeros_like(st_ref)

        xl = x_ref[...]
        xc, r, gi, big_l, a, m2 = _lru_gates(xl, wc_ref, bc_ref, wa_ref, ba_ref, wx_ref, bx_ref, lam_ref)
        inv_mult = lax.rsqrt(m2)
        mult = m2 * inv_mult
        dh = dy_ref[...] + _shift_up(e_ref[...], 1)
        da = dh * _shift_down(hs_ref[...], 1)
        dmult = dh * (gi * xc)
        d_i = dh * (mult * xc)
        dxc = dh * (mult * gi)
        dla = a * (da - dmult * (a * inv_mult))
        dr = dla * big_l
        d_big_l = _colsum(dla * r)
        dra = dr * (r * (1.0 - r))
        dia = d_i * (gi * (1.0 - gi))
        dwa_ref[...] += _mm_tn(xc, dra)
        dwx_ref[...] += _mm_tn(xc, dia)
        dxc = dxc + _mm_nt(dra, wa_ref[...]) + _mm_nt(dia, wx_ref[...])
        dxl = wc_ref[CONV_WIDTH - 1:CONV_WIDTH, :] * dxc
        st_ref[4 + CONV_WIDTH - 1:4 + CONV_WIDTH, :] += _colsum(dxc * xl)
        for k in range(CONV_WIDTH - 1):
            ahead = _shift_up(dxc, CONV_WIDTH - 1 - k)
            dxl = dxl + wc_ref[k:k + 1, :] * ahead
            st_ref[4 + k:5 + k, :] += _colsum(ahead * xl)
        dxl_ref[...] = dxl.astype(BF16)
        st_ref[0:1, :] += _colsum(dra)
        st_ref[1:2, :] += _colsum(dia)
        st_ref[2:3, :] += d_big_l * (LRU_C * _sigmoid(-lam_ref[...]))
        st_ref[3:4, :] += _colsum(dxc)

    col = lambda hd, b: (0, hd)
    head = lambda hd, b: (hd, 0, 0)
    tok = lambda hd, b: (b, hd)
    seq_blk = pl.BlockSpec((seq, HEAD), tok)
    return pl.pallas_call(
        _ordered(body, 11, after), name="lru_bwd", grid=(LRU_HEADS, nb),
        in_specs=[seq_blk, seq_blk, seq_blk, seq_blk,
                  pl.BlockSpec((CONV_WIDTH, HEAD), col), pl.BlockSpec((1, HEAD), col),
                  pl.BlockSpec((None, HEAD, HEAD), head), pl.BlockSpec((1, HEAD), col),
                  pl.BlockSpec((None, HEAD, HEAD), head), pl.BlockSpec((1, HEAD), col),
                  pl.BlockSpec((1, HEAD), col)] + [_ANY] * len(after),
        out_specs=[seq_blk, pl.BlockSpec((None, HEAD, HEAD), head), pl.BlockSpec((None, HEAD, HEAD), head),
                   pl.BlockSpec((8, HEAD), col)],
        out_shape=[jax.ShapeDtypeStruct((t, w), BF16), jax.ShapeDtypeStruct((LRU_HEADS, HEAD, HEAD), F32),
                   jax.ShapeDtypeStruct((LRU_HEADS, HEAD, HEAD), F32), jax.ShapeDtypeStruct((8, w), F32)],
        compiler_params=_cp(("arbitrary", "arbitrary")),
    )(proj, hs, e, dyl, w_conv, b_conv, w_a, b_a, w_x, b_x, lam, *after)


def _weight_grad(a, g, col_shards, name, after=()):
    t, k = a.shape
    n = g.shape[1]
    tt = min(TT_DW, t)
    tk = k if k <= 1536 else 1024
    ns = n // N_CHIPS if col_shards else n
    narrow = col_shards and ns < 512
    tn = n if narrow else min(ns, 768 if ns % 768 == 0 else 1024)
    while ns % tn and not narrow:
        tn //= 2
    per = max(ns // tn, 1)

    def body(a_ref, g_ref, o_ref):
        @pl.when(pl.program_id(2) == 0)
        def _():
            o_ref[...] = jnp.zeros_like(o_ref)

        res = _mm_tn(a_ref[...], g_ref[...])
        if narrow:
            for q in range(N_CHIPS):
                o_ref[q] += res[:, q * ns:(q + 1) * ns]
        else:
            o_ref[...] += res

    if narrow:
        out_spec = pl.BlockSpec((N_CHIPS, tk, ns), lambda i, j, s: (0, i, 0))
        out_shape = jax.ShapeDtypeStruct((N_CHIPS, k, ns), F32)
    elif col_shards:
        out_spec = pl.BlockSpec((None, tk, tn), lambda i, j, s: (j // per, i, j % per))
        out_shape = jax.ShapeDtypeStruct((N_CHIPS, k, ns), F32)
    else:
        out_spec = pl.BlockSpec((tk, tn), lambda i, j, s: (i, j))
        out_shape = jax.ShapeDtypeStruct((k, n), F32)
    return pl.pallas_call(
        _ordered(body, 2, after), name=name, grid=(k // tk, n // tn, t // tt),
        in_specs=[pl.BlockSpec((tt, tk), lambda i, j, s: (s, i)), pl.BlockSpec((tt, tn), lambda i, j, s: (s, j))]
        + [_ANY] * len(after),
        out_specs=out_spec, out_shape=out_shape,
        compiler_params=_cp(("arbitrary", "arbitrary", "arbitrary")),
    )(a, g, *after)


def _input_grad(dproj, w_in_g, dz1, x2, modv, nb, seq, after=()):
    t, d = x2.shape
    nq, _, ns = w_in_g.shape
    tm = min(TM_DH, seq)
    tpb = seq // tm

    def body(dp_ref, w_ref, dz1_ref, x_ref, mod_ref, gx_ref, db_ref, pb_ref, acc):
        i, q = pl.program_id(0), pl.program_id(1)

        @pl.when(q == 0)
        def _():
            acc[...] = jnp.zeros_like(acc)

        @pl.when((i == 0) & (q == 0))
        def _():
            db_ref[...] = jnp.zeros_like(db_ref)

        @pl.when((i % tpb == 0) & (q == 0))
        def _():
            pb_ref[...] = jnp.zeros_like(pb_ref)

        dp = dp_ref[...]
        acc[...] += _mm_nt(dp, w_ref[...])
        db_ref[q, 0:1, :] += _colsum(dp.astype(F32))

        @pl.when(q == nq - 1)
        def _():
            dh = acc[...]
            gx_ref[...] = ALPHA * dz1_ref[...] + dh * (1.0 + mod_ref[1:2, :])
            pb_ref[0:1, :] += _colsum(dh * x_ref[...])
            pb_ref[1:2, :] += _colsum(dh)

    tok = lambda i, q: (i, 0)
    return pl.pallas_call(
        _ordered(body, 5, after), name="input_grad", grid=(t // tm, nq),
        in_specs=[pl.BlockSpec((tm, ns), lambda i, q: (i, q)), pl.BlockSpec((None, d, ns), lambda i, q: (q, 0, 0)),
                  pl.BlockSpec((tm, d), tok), pl.BlockSpec((tm, d), tok),
                  pl.BlockSpec((None, 8, d), lambda i, q: (i // tpb, 0, 0))] + [_ANY] * len(after),
        out_specs=[pl.BlockSpec((tm, d), tok), pl.BlockSpec((nq, 8, ns), lambda i, q: (0, 0, 0)),
                   pl.BlockSpec((None, 8, d), lambda i, q: (i // tpb, 0, 0))],
        out_shape=[jax.ShapeDtypeStruct((t, d), F32), jax.ShapeDtypeStruct((nq, 8, ns), F32),
                   jax.ShapeDtypeStruct((nb, 8, d), F32)],
        scratch_shapes=[pltpu.VMEM((tm, d), F32)],
        compiler_params=_cp(("arbitrary", "arbitrary")),
    )(dproj, w_in_g, dz1, x2, modv, *after)


def _rows128(v):
    flat = v.reshape(-1, HEAD)
    pad = (-flat.shape[0]) % 8
    return jnp.pad(flat, ((0, pad), (0, 0))) if pad else flat


def kernel(x, c, w_ada, b_ada, w_in, b_in, w_conv, b_conv, w_rg_a, b_rg_a, w_rg_x, b_rg_x, lru_lambda, w_sp, b_sp, ln_v_g, ln_v_b, w_o_lru, w_o_sgu, w_out, ln1_g, ln1_b, w_up, w_down, ln2_g, ln2_b, loss_target, m_w_ada, m_b_ada, m_w_in, m_b_in, m_w_conv, m_b_conv, m_w_rg_a, m_b_rg_a, m_w_rg_x, m_b_rg_x, m_lru_lambda, m_w_sp, m_b_sp, m_ln_v_g, m_ln_v_b, m_w_o_lru, m_w_o_sgu, m_w_out, m_ln1_g, m_ln1_b, m_w_up, m_w_down, m_ln2_g, m_ln2_b, v_w_ada, v_b_ada, v_w_in, v_b_in, v_w_conv, v_b_conv, v_w_rg_a, v_b_rg_a, v_w_rg_x, v_b_rg_x, v_lru_lambda, v_w_sp, v_b_sp, v_ln_v_g, v_ln_v_b, v_w_o_lru, v_w_o_sgu, v_w_out, v_ln1_g, v_ln1_b, v_w_up, v_w_down, v_ln2_g, v_ln2_b):
    given = dict(locals())
    nb, seq, d = x.shape
    t = nb * seq
    w_lru = LRU_HEADS * HEAD
    d_sgu = SGU_GROUPS * HEAD
    xi, yi, ci = lax.axis_index("x"), lax.axis_index("y"), lax.axis_index("c")
    chip = 2 * xi + yi
    dev = 2 * chip + ci
    cidx = jnp.reshape(ci, (1,)).astype(jnp.int32)

    x2 = x.reshape(t, d)
    target = loss_target.reshape(t, d)

    big = ["w_in", "w_o_lru", "w_o_sgu", "w_out", "w_up", "w_down"]
    shards_a = [w_in[0].astype(BF16)]
    shards_b = [given[n][0].astype(BF16) for n in big[1:]]
    pidx = jnp.reshape(chip, (1,)).astype(jnp.int32)

    c_rows = _rows128(c)
    wconv_rows = _rows128(w_conv[0])
    slab0 = _all_gather_small(jnp.concatenate([c_rows, wconv_rows], axis=0), "gather_c_wconv")
    slab0 = slab0.reshape(N_DEV, -1, HEAD)
    c_all = slab0[:, :c_rows.shape[0]].reshape(N_DEV * nb, d)
    n_wc = CONV_WIDTH * (w_lru // N_CHIPS) // HEAD
    wc = slab0[0::2, c_rows.shape[0]:c_rows.shape[0] + n_wc].reshape(N_CHIPS, CONV_WIDTH, w_lru // N_CHIPS)
    w_conv_full = jnp.transpose(wc, (1, 0, 2)).reshape(CONV_WIDTH, w_lru)

    n_ada = w_ada.shape[2]
    b_ada_cols = lax.dynamic_slice(b_ada, (0, chip * n_ada), (1, n_ada))
    mod_cols = _ada_fwd(c_all, w_ada[0], b_ada_cols)
    half = (N_DEV * nb) // 2
    mod_half = lax.dynamic_slice(mod_cols, (ci * half, 0), (half, n_ada))
    mod_g = _all_gather_small(mod_half, "gather_mod").reshape(N_CHIPS, 2, half, n_ada)
    mod_all = jnp.transpose(mod_g, (1, 2, 0, 3)).reshape(N_DEV * nb, N_CHIPS * n_ada)
    mod_loc = lax.dynamic_slice(mod_all, (dev * nb, 0), (nb, N_CHIPS * n_ada)).reshape(nb, 6, d)
    modv = jnp.pad(mod_loc, ((0, 0), (0, 2), (0, 0)))

    lru_w = (w_conv_full, b_conv, w_rg_a[0], b_rg_a, w_rg_x[0], b_rg_x, lru_lambda)
    b_sp_t = jnp.transpose(b_sp[0])

    land = lambda s: jax.ShapeDtypeStruct((N_CHIPS,) + s.shape, s.dtype)
    started_a = _split_start(shards_a, [land(s) for s in shards_a], _gather_copies, 3, "gather_w_in_start", after=(modv,))
    started_b = _split_start(shards_b, [land(s) for s in shards_b], _gather_copies, 3 * len(shards_b),
                             "gather_w_rest_start", after=(started_a[-1],))

    shards_a, lands_a = _split_wait(started_a, 1, _gather_copies, "gather_w_in_wait", after=(started_b[-1],))
    (w_in_g,) = _fill_own_slot(lands_a, shards_a, pidx, ["own_w_in"])
    proj, h = _proj_fwd(x2, modv, w_in_g, b_in, seq)
    a, inp = _lru_prep(proj, lru_w, nb, seq)
    a3 = a.reshape(nb, seq, w_lru)
    hs = _scan(a3, inp.reshape(nb, seq, w_lru), False, "lru_scan").reshape(t, w_lru)
    y_sgu = _sgu_fwd(proj, w_sp[0], b_sp_t, ln_v_g, ln_v_b)
    shards_b, lands_b = _split_wait(started_b, len(shards_b), _gather_copies, "gather_w_rest_wait", after=(hs, y_sgu))
    w_o_lru_g, w_o_sgu_g, w_out_g, w_up_g, w_down_g = _fill_own_slot(lands_b, shards_b, pidx, ["own_" + n for n in big[1:]])
    w_o_lru_g = w_o_lru_g.reshape(w_lru, d)
    w_out_g = w_out_g.reshape(d, d)
    w_down_g = w_down_g.reshape(-1, d)
    yap, y_a, y_b, merged, mix, x1 = _mix_fwd(hs, proj, y_sgu, x2, modv, w_o_lru_g, w_o_sgu_g, w_out_g, ln1_g, ln1_b, seq)
    up, act, h2, dz2, df, st2, pb2 = _mlp_fwd(x1, modv, w_up_g, w_down_g, ln2_g, ln2_b, target, nb, seq)
    loss = lax.psum(st2[2, 0], ("x", "y", "c"))

    part = {}

    def to_sibling_start(group, tag, after=()):
        g4 = []
        for n in group:
            shard = given[n].shape[1:]
            g4.append(part[n].reshape(N_CHIPS, 2, shard[0] // 2, shard[1]))
        shapes = [jax.ShapeDtypeStruct((N_CHIPS,) + g.shape[2:], F32) for g in g4]
        return _split_start(g4, shapes, _to_sibling_copies, len(g4), "grads_to_sibling_start_" + tag, after)

    def to_chips_start(group, started, tag, after=()):
        g4, recv = _split_wait(started, len(group), _to_sibling_copies, "grads_to_sibling_wait_" + tag, after)
        own4 = [_add_own_half(g4[k], recv[k], cidx, "grad_pair_sum_" + n) for k, n in enumerate(group)]
        shapes = [jax.ShapeDtypeStruct((3,) + o.shape[1:], BF16) for o in own4]
        return _split_start(own4, shapes, _chip_exchange_copies, 3 * len(own4), "grads_chip_exchange_start_" + tag)

    def chips_finish(group, started, tag, after=()):
        own4, slots = _split_wait(started, len(group), _chip_exchange_copies, "grads_chip_exchange_wait_" + tag, after)
        return [_sum_own_and_peers(own4[k], slots[k], pidx, "grad_chip_sum_" + n) for k, n in enumerate(group)]

    dup, dz1, dmix, st1, pb1 = _mlp_bwd(df, up, w_down_g, w_up_g, dz2, x2, mix, modv, ln1_g, ln1_b, nb, seq)
    group1 = ["w_up", "w_down"]
    part["w_up"] = _weight_grad(h2, dup, True, "grad_w_up")
    part["w_down"] = _weight_grad(act, df, False, "grad_w_down")
    sib1 = to_sibling_start(group1, "mlp")
    dy_a, dy_b, dga, dgb, dgl, dyl, dys = _mix_bwd(dmix, proj, y_a, y_b, hs, w_out_g, w_o_lru_g, w_o_sgu_g, seq,
                                                   after=(sib1[-1],))
    group2 = ["w_o_lru", "w_o_sgu", "w_out"]
    part["w_o_lru"] = _weight_grad(yap, dy_a, False, "grad_w_o_lru")
    part["w_o_sgu"] = _weight_grad(y_sgu, dy_b, True, "grad_w_o_sgu")
    part["w_out"] = _weight_grad(merged, dmix, False, "grad_w_out")
    chips1 = to_chips_start(group1, sib1, "mlp", after=(dys, part["w_o_lru"], part["w_o_sgu"], part["w_out"]))
    sib2 = to_sibling_start(group2, "mix", after=(chips1[-1],))
    du, dv, g_w_sp, st_sgu, g_b_sp_t = _sgu_bwd(proj, dys, w_sp[0], b_sp_t, ln_v_g, ln_v_b, after=(sib2[-1],))
    dyl3 = dyl.reshape(nb, seq, w_lru)
    e = _scan(a3, dyl3, True, "lru_scan_bwd").reshape(t, w_lru)
    chips2 = to_chips_start(group2, sib2, "mix", after=(e, du))
    dxl, g_w_rg_a, g_w_rg_x, st_lru = _lru_bwd(proj, hs, e, dyl, lru_w, nb, seq, after=(chips2[-1],))
    dproj = jnp.concatenate([dxl, dgl, du, dv, dga, dgb], axis=1)

    didx = jnp.reshape(dev, (1,)).astype(jnp.int32)
    early = [
        ("w_conv", st_lru[4:8]), ("b_conv", st_lru[3]), ("w_rg_a", g_w_rg_a), ("b_rg_a", st_lru[0]),
        ("w_rg_x", g_w_rg_x), ("b_rg_x", st_lru[1]), ("lru_lambda", st_lru[2]), ("w_sp", g_w_sp),
        ("b_sp", jnp.transpose(g_b_sp_t[:, :SGU_GROUPS])), ("ln_v_g", st_sgu[0]), ("ln_v_b", st_sgu[1]),
        ("ln1_g", st1[0]), ("ln1_b", st1[1]), ("ln2_g", st2[0]), ("ln2_b", st2[1]),
    ]
    pieces_e = [_rows128(v) for _, v in early]
    slab_e = jnp.concatenate(pieces_e, axis=0)
    slab_e = jnp.pad(slab_e, ((0, (-slab_e.shape[0]) % TR_EW), (0, 0)))
    small_st = _split_start([slab_e], [jax.ShapeDtypeStruct((N_DEV,) + slab_e.shape, F32)], _all_devices_copies, N_DEV - 1,
                            "small_grads_start")

    group3 = ["w_in"]
    part["w_in"] = _weight_grad(h, dproj, True, "grad_w_in", after=(small_st[-1],))
    sib3 = to_sibling_start(group3, "in")
    chips3 = to_chips_start(group3, sib3, "in")
    grad_x2, g_b_in4, pb0 = _input_grad(dproj, w_in_g, dz1, x2, modv, nb, seq, after=(chips3[-1],))
    halves12 = chips_finish(group1, chips1, "mlp", after=(grad_x2,)) + chips_finish(group2, chips2, "mix", after=(grad_x2,))
    swap12 = _split_start(halves12, [jax.ShapeDtypeStruct(hv.shape, F32) for hv in halves12], _swap_copies, len(halves12),
                          "grads_swap_start")
    grads = {}

    dmod_loc = jnp.stack([pb0[:, 1], pb0[:, 0], pb1[:, 2], pb1[:, 1], pb1[:, 0], pb2[:, 0]], axis=1)
    late = [("dmod", dmod_loc), ("b_in", g_b_in4[:, 0])]
    pieces_l = [_rows128(v) for _, v in late]
    slab_l = jnp.concatenate(pieces_l, axis=0)
    gathered = _all_gather_small(slab_l, "gather_small_grads", after=(swap12[-1],)).reshape(N_DEV, slab_l.shape[0], HEAD)
    rows_dmod = dmod_loc.size // HEAD
    dmod_all = gathered[:, :rows_dmod].reshape(N_DEV * nb, 6 * d)
    grads["b_in"] = _sum_slots(gathered[:, rows_dmod:], "grad_b_in_sum").reshape(1, -1)

    (slab_e,), (lands_e,) = _split_wait(small_st, 1, _all_devices_copies, "small_grads_wait", after=(gathered,))
    summed = _sum_devices(lands_e, slab_e, didx, "small_grad_sum")
    off = 0
    for (n, v), piece in zip(early, pieces_e):
        grads[n] = summed[off:off + v.size // HEAD].reshape(v.shape)
        off += piece.shape[0]

    mine12, theirs12 = _split_wait(swap12, len(halves12), _swap_copies, "grads_swap_wait", after=(summed,))
    (mine3,) = chips_finish(group3, chips3, "in", after=(summed,))
    (theirs3,) = _exchange([mine3], [jax.ShapeDtypeStruct(mine3.shape, F32)], _swap_copies, 1, "grads_swap_w_in")
    mine = dict(zip(group1 + group2 + group3, mine12 + [mine3]))
    theirs = dict(zip(group1 + group2 + group3, theirs12 + [theirs3]))

    dmod_cols = lax.dynamic_slice(dmod_all, (0, chip * n_ada), (N_DEV * nb, n_ada))
    grads["w_ada"], grads["b_ada"] = _ada_bwd(c_all, dmod_all, dmod_cols)
    n_wcs = w_lru // N_CHIPS
    grads["w_conv"] = lax.dynamic_slice(grads["w_conv"], (0, chip * n_wcs), (CONV_WIDTH, n_wcs))

    names = ['w_ada', 'b_ada', 'w_in', 'b_in', 'w_conv', 'b_conv', 'w_rg_a', 'b_rg_a', 'w_rg_x', 'b_rg_x', 'lru_lambda',
             'w_sp', 'b_sp', 'ln_v_g', 'ln_v_b', 'w_o_lru', 'w_o_sgu', 'w_out', 'ln1_g', 'ln1_b', 'w_up', 'w_down',
             'ln2_g', 'ln2_b']
    out_g, out_d, out_m, out_v = [], [], [], []
    for n in names:
        wv = given[n]
        shape2 = (-1, wv.shape[-1])
        w2, m2, v2 = wv.reshape(shape2), given["m_" + n].reshape(shape2), given["v_" + n].reshape(shape2)
        if n in big:
            g2, dlt, nm, nv = _adamw_halves(w2, mine[n], theirs[n], m2, v2, cidx, "adamw_" + n)
        else:
            g2 = grads[n].reshape(wv.shape).reshape(shape2)
            dlt, nm, nv = _adamw(w2, g2, m2, v2, "adamw_" + n)
        out_g.append(g2.reshape(wv.shape))
        out_d.append(dlt.reshape(wv.shape))
        out_m.append(nm.reshape(wv.shape))
        out_v.append(nv.reshape(wv.shape))

    return (loss, grad_x2.reshape(nb, seq, d), *out_g, *out_d, *out_m, *out_v)
```

```python
import functools
import math

import jax
import jax.numpy as jnp
from jax import lax
from jax.experimental import pallas as pl
from jax.experimental.pallas import tpu as pltpu

F32 = jnp.float32
BF16 = jnp.bfloat16
MESH = pl.DeviceIdType.MESH

N_CHIPS = 4
N_DEV = 8
LRU_HEADS = 10
HEAD = 128
SGU_GROUPS = 6
SGU_CHUNK = 64
CONV_WIDTH = 4
LRU_C = 8.0
ALPHA = 2.0 ** 0.25
LN_EPS = 1e-5
ADAM_LR, ADAM_B1, ADAM_B2, ADAM_EPS, ADAM_WD, ADAM_STEP = 0.001, 0.9, 0.999, 1e-08, 0.01, 10

VMEM_LIMIT = 56 * 1024 * 1024
TM_PROJ = 1024
TM_MIX = 256
TM_MLP = 512
TM_SGU = 512
TM_DH = 512
TT_DW = 1024
TC_SCAN = 256
TR_EW = 256


def _cp(sem=None):
    return pltpu.CompilerParams(dimension_semantics=sem, vmem_limit_bytes=VMEM_LIMIT)


def _mm(a, b):
    return jnp.dot(a.astype(BF16), b.astype(BF16), preferred_element_type=F32)


def _mm_nt(a, b):
    return lax.dot_general(a.astype(BF16), b.astype(BF16), (((1,), (1,)), ((), ())), preferred_element_type=F32)


def _mm_tn(a, b):
    return lax.dot_general(a.astype(BF16), b.astype(BF16), (((0,), (0,)), ((), ())), preferred_element_type=F32)


def _sigmoid(x):
    return 1.0 / (1.0 + jnp.exp(-x))


def _sigmoid_t(x):
    return 0.5 * jnp.tanh(0.5 * x) + 0.5


_GELU_K = math.sqrt(2.0 / math.pi)


def _gelu(x):
    t = jnp.tanh(_GELU_K * (x + 0.044715 * (x * x * x)))
    return 0.5 * x * (1.0 + t)


def _gelu_and_grad(x):
    x2 = x * x
    t = jnp.tanh(_GELU_K * (x + 0.044715 * (x2 * x)))
    g = 0.5 * x * (1.0 + t)
    dg = 0.5 * (1.0 + t) + 0.5 * x * (1.0 - t * t) * (_GELU_K * (1.0 + 3.0 * 0.044715 * x2))
    return g, dg


def _ln_stats(z):
    mu = jnp.mean(z, axis=-1, keepdims=True)
    zc = z - mu
    var = jnp.mean(zc * zc, axis=-1, keepdims=True)
    rstd = lax.rsqrt(var + LN_EPS)
    return zc * rstd, rstd


def _ln_bwd(dxh, xhat, rstd):
    m1 = jnp.mean(dxh, axis=-1, keepdims=True)
    m2 = jnp.mean(dxh * xhat, axis=-1, keepdims=True)
    return rstd * (dxh - m1 - xhat * m2)


def _colsum(v):
    return jnp.sum(v, axis=0, keepdims=True)


def _shift_down(v, j):
    if j == 0:
        return v
    rows = lax.broadcasted_iota(jnp.int32, v.shape, 0)
    return jnp.where(rows >= j, pltpu.roll(v, j, 0), 0.0)


def _shift_up(v, j):
    if j == 0:
        return v
    n = v.shape[0]
    rows = lax.broadcasted_iota(jnp.int32, v.shape, 0)
    return jnp.where(rows < n - j, pltpu.roll(v, n - j, 0), 0.0)


def _my_pos():
    return lax.axis_index("x"), lax.axis_index("y"), lax.axis_index("c")


def _all_gather_small(v, name, after=()):
    m_per, n = v.shape

    def body(x_ref, out_ref, send_sems, recv_sems, local_sem):
        x, y, c = _my_pos()
        me, sibling = (x, y, c), (x, y, 1 - c)
        chips = [(1 - x, y), (x, 1 - y), (1 - x, 1 - y)]

        def rows(px, py, pc):
            return out_ref.at[pl.ds((4 * px + 2 * py + pc) * m_per, m_per), :]

        def copy(k, block, to, src=None):
            return pltpu.make_async_remote_copy(
                src_ref=rows(*block) if src is None else src, dst_ref=rows(*block),
                send_sem=send_sems.at[k], recv_sem=recv_sems.at[k], device_id=to, device_id_type=MESH)

        mine = pltpu.make_async_copy(x_ref, rows(*me), local_sem)
        mine.start()
        first = [copy(0, me, sibling, src=x_ref)]
        first += [copy(1 + j, me, (*chip, c), src=x_ref) for j, chip in enumerate(chips)]
        for cp in first:
            cp.start()
        passed = [copy(4 + j, (*chip, c), sibling) for j, chip in enumerate(chips)]
        for j, chip in enumerate(chips):
            copy(1 + j, (*chip, c), me).wait_recv()
            passed[j].start()
        copy(0, sibling, me).wait_recv()
        for j, chip in enumerate(chips):
            copy(4 + j, (*chip, 1 - c), me).wait_recv()
        for cp in first + passed:
            cp.wait_send()
        mine.wait()

    return pl.pallas_call(
        _ordered(body, 1, after), name=name,
        out_shape=jax.ShapeDtypeStruct((N_DEV * m_per, n), v.dtype),
        in_specs=[pl.BlockSpec(memory_space=pltpu.VMEM)] + [pl.BlockSpec(memory_space=pl.ANY)] * len(after),
        out_specs=pl.BlockSpec(memory_space=pltpu.VMEM),
        scratch_shapes=[pltpu.SemaphoreType.DMA((7,)), pltpu.SemaphoreType.DMA((7,)), pltpu.SemaphoreType.DMA],
        compiler_params=pltpu.CompilerParams(vmem_limit_bytes=VMEM_LIMIT),
    )(v, *after)


_HBM = pl.BlockSpec(memory_space=pltpu.HBM)
_ANY = pl.BlockSpec(memory_space=pl.ANY)
_SEM = pl.BlockSpec(memory_space=pltpu.SEMAPHORE)
_EFFECT = pltpu.SideEffectType.DATAFLOW_SIDE_EFFECTING


def _ordered(body, n_in, after):
    k = len(after)
    if not k:
        return body
    return lambda *refs: body(*refs[:n_in], *refs[n_in + k:])


def _gather_copies(ins, lands, send_sems, recv_sems):
    x, y, c = _my_pos()
    p = 2 * x + y
    peers = [(x, 1 - y), (1 - x, y), (1 - x, 1 - y)]
    sends, recvs = [], []
    for k in range(len(ins)):
        for j, (qx, qy) in enumerate(peers):
            sems = dict(send_sem=send_sems.at[3 * k + j], recv_sem=recv_sems.at[3 * k + j],
                        device_id=(qx, qy, c), device_id_type=MESH)
            sends.append(pltpu.make_async_remote_copy(src_ref=ins[k], dst_ref=lands[k].at[p], **sems))
            recvs.append(pltpu.make_async_remote_copy(src_ref=ins[k], dst_ref=lands[k].at[2 * qx + qy], **sems))
    return sends, recvs


def _peer_gather_copies(peers):
    def copies(ins, lands, send_sems, recv_sems):
        x, y, c = _my_pos()
        where = [(x, 1 - y), (1 - x, y), (1 - x, 1 - y)]
        cps = [pltpu.make_async_remote_copy(
            src_ref=ins[0], dst_ref=lands[j], send_sem=send_sems.at[j], recv_sem=recv_sems.at[j],
            device_id=(*where[j], c), device_id_type=MESH) for j in peers]
        return cps, cps
    return copies


def _to_sibling_copies(ins, lands, send_sems, recv_sems):
    x, y, c = _my_pos()
    cps = [pltpu.make_async_remote_copy(
        src_ref=ins[k].at[:, 1 - c], dst_ref=lands[k], send_sem=send_sems.at[k], recv_sem=recv_sems.at[k],
        device_id=(x, y, 1 - c), device_id_type=MESH) for k in range(len(ins))]
    return cps, cps


def _chip_exchange_copies(ins, lands, send_sems, recv_sems):
    x, y, c = _my_pos()
    peers = [(x, 1 - y), (1 - x, y), (1 - x, 1 - y)]
    cps = []
    for k in range(len(ins)):
        for j, (qx, qy) in enumerate(peers):
            cps.append(pltpu.make_async_remote_copy(
                src_ref=ins[k].at[2 * qx + qy], dst_ref=lands[k].at[j], send_sem=send_sems.at[3 * k + j],
                recv_sem=recv_sems.at[3 * k + j], device_id=(qx, qy, c), device_id_type=MESH))
    return cps, cps


def _all_devices_copies(ins, lands, send_sems, recv_sems):
    x, y, c = _my_pos()
    me = 4 * x + 2 * y + c
    sends, recvs = [], []
    for r in range(1, N_DEV):
        px = 1 - x if r & 4 else x
        py = 1 - y if r & 2 else y
        pc = 1 - c if r & 1 else c
        sems = dict(send_sem=send_sems.at[r - 1], recv_sem=recv_sems.at[r - 1], device_id=(px, py, pc), device_id_type=MESH)
        sends.append(pltpu.make_async_remote_copy(src_ref=ins[0], dst_ref=lands[0].at[me], **sems))
        recvs.append(pltpu.make_async_remote_copy(src_ref=ins[0], dst_ref=lands[0].at[4 * px + 2 * py + pc], **sems))
    return sends, recvs


def _swap_copies(ins, lands, send_sems, recv_sems):
    x, y, c = _my_pos()
    cps = [pltpu.make_async_remote_copy(
        src_ref=ins[k], dst_ref=lands[k], send_sem=send_sems.at[k], recv_sem=recv_sems.at[k],
        device_id=(x, y, 1 - c), device_id_type=MESH) for k in range(len(ins))]
    return cps, cps


def _split_start(ins, land_shapes, copies, n_sems, name, after=()):
    n, nl = len(ins), len(land_shapes)
    first_out = n + nl + len(after)

    def body(*refs):
        in_refs, land_refs = refs[:n], refs[n:n + nl]
        send_sems, recv_sems = refs[first_out:first_out + 2]
        token = refs[-1]
        sends, _ = copies(in_refs, land_refs, send_sems, recv_sems)
        for cp in sends:
            cp.start()
        token[...] = jnp.zeros_like(token)

    lands = [pltpu.with_memory_space_constraint(lax.empty(s.shape, s.dtype), pltpu.HBM) for s in land_shapes]
    ins = [pltpu.with_memory_space_constraint(s, pltpu.HBM) for s in ins]
    return pl.pallas_call(
        body, name=name,
        out_shape=(pltpu.SemaphoreType.DMA((n_sems,)), pltpu.SemaphoreType.DMA((n_sems,)),
                   *[pltpu.HBM(s.shape, s.dtype) for s in ins], *[pltpu.HBM(s.shape, s.dtype) for s in lands],
                   jax.ShapeDtypeStruct((8, HEAD), F32)),
        in_specs=[_HBM] * (n + nl) + [pl.BlockSpec(memory_space=pl.ANY)] * len(after),
        out_specs=(_SEM, _SEM, *([_HBM] * (n + nl)), pl.BlockSpec(memory_space=pltpu.VMEM)),
        input_output_aliases={k: 2 + k for k in range(n + nl)},
        compiler_params=pltpu.CompilerParams(has_side_effects=_EFFECT),
    )(*ins, *lands, *after)


def _split_wait(started, n, copies, name, after=()):
    send_sems, recv_sems = started[0], started[1]
    bufs = started[2:-1]
    nb = len(bufs)

    def body(*refs):
        in_refs, land_refs = refs[:n], refs[n:nb]
        sends, recvs = copies(in_refs, land_refs, refs[nb], refs[nb + 1])
        for cp in sends:
            cp.wait_send()
        for cp in recvs:
            cp.wait_recv()

    outs = pl.pallas_call(
        body, name=name,
        out_shape=tuple(pltpu.HBM(s.shape, s.dtype) for s in bufs),
        in_specs=[_HBM] * nb + [_SEM, _SEM] + [pl.BlockSpec(memory_space=pl.ANY)] * len(after),
        out_specs=tuple([_HBM] * nb),
        input_output_aliases={k: k for k in range(nb)},
        compiler_params=pltpu.CompilerParams(has_side_effects=_EFFECT),
    )(*bufs, send_sems, recv_sems, *after)
    return list(outs[:n]), list(outs[n:])


def _fill_own_slot(gathered, shards, pidx, names):
    outs = []
    for g, s, name in zip(gathered, shards, names):
        r, cdim = s.shape
        tr = _row_tile(r)

        def body(p_ref, s_ref, g_ref, o_ref):
            o_ref[...] = s_ref[...]

        outs.append(pl.pallas_call(
            body, name=name,
            grid_spec=pltpu.PrefetchScalarGridSpec(
                num_scalar_prefetch=1, grid=(r // tr,),
                in_specs=[pl.BlockSpec((tr, cdim), lambda i, p: (i, 0)), pl.BlockSpec(memory_space=pl.ANY)],
                out_specs=pl.BlockSpec((None, tr, cdim), lambda i, p: (p[0], i, 0))),
            out_shape=jax.ShapeDtypeStruct(g.shape, g.dtype),
            input_output_aliases={2: 0},
            compiler_params=_cp(("arbitrary",)),
        )(pidx, s, g))
    return outs


def _sum_own_and_peers(own4, slots, pidx, name):
    _, rh, cdim = own4.shape
    tr = _row_tile(rh)

    def body(p_ref, own_ref, s_ref, o_ref):
        acc = own_ref[...].astype(F32)
        for j in range(3):
            acc = acc + s_ref[j].astype(F32)
        o_ref[...] = acc

    return pl.pallas_call(
        body, name=name,
        grid_spec=pltpu.PrefetchScalarGridSpec(
            num_scalar_prefetch=1, grid=(rh // tr,),
            in_specs=[pl.BlockSpec((None, tr, cdim), lambda i, p: (p[0], i, 0)),
                      pl.BlockSpec((3, tr, cdim), lambda i, p: (0, i, 0))],
            out_specs=pl.BlockSpec((tr, cdim), lambda i, p: (i, 0))),
        out_shape=jax.ShapeDtypeStruct((rh, cdim), F32),
        compiler_params=_cp(("arbitrary",)),
    )(pidx, own4, slots)


def _exchange(ins, land_shapes, copies, n_sems, name):
    n, nl = len(ins), len(land_shapes)

    def body(*refs):
        sends, recvs = copies(refs[:n], refs[n:n + nl], refs[n + nl], refs[n + nl + 1])
        for cp in sends:
            cp.start()
        for cp in sends:
            cp.wait_send()
        for cp in recvs:
            cp.wait_recv()

    any_spec = pl.BlockSpec(memory_space=pl.ANY)
    return pl.pallas_call(
        body, name=name,
        out_shape=[jax.ShapeDtypeStruct(s.shape, s.dtype) for s in land_shapes],
        in_specs=[any_spec] * n, out_specs=[any_spec] * nl,
        scratch_shapes=[pltpu.SemaphoreType.DMA((n_sems,)), pltpu.SemaphoreType.DMA((n_sems,))],
    )(*ins)


def _row_tile(r):
    t = min(TR_EW, r)
    while r % t:
        t //= 2
    return t


def _add_own_half(g4, recv, cidx, name):
    _, _, rh, cdim = g4.shape
    tr = _row_tile(rh)

    def body(c_ref, a_ref, b_ref, o_ref):
        o_ref[...] = (a_ref[...] + b_ref[...]).astype(BF16)

    return pl.pallas_call(
        body, name=name,
        grid_spec=pltpu.PrefetchScalarGridSpec(
            num_scalar_prefetch=1, grid=(N_CHIPS, rh // tr),
            in_specs=[pl.BlockSpec((None, None, tr, cdim), lambda q, i, c: (q, c[0], i, 0)),
                      pl.BlockSpec((None, tr, cdim), lambda q, i, c: (q, i, 0))],
            out_specs=pl.BlockSpec((None, tr, cdim), lambda q, i, c: (q, i, 0))),
        out_shape=jax.ShapeDtypeStruct(recv.shape, BF16),
        compiler_params=_cp(("arbitrary", "arbitrary")),
    )(cidx, g4, recv)


def _sum_slots(v, name):
    n, r, cdim = v.shape
    tr = _row_tile(r)

    def body(v_ref, o_ref):
        acc = v_ref[0].astype(F32)
        for k in range(1, n):
            acc = acc + v_ref[k].astype(F32)
        o_ref[...] = acc

    return pl.pallas_call(
        body, name=name, grid=(r // tr,),
        in_specs=[pl.BlockSpec((n, tr, cdim), lambda i: (0, i, 0))],
        out_specs=pl.BlockSpec((tr, cdim), lambda i: (i, 0)),
        out_shape=jax.ShapeDtypeStruct((r, cdim), F32),
        compiler_params=_cp(("arbitrary",)),
    )(v)


def _sum_devices(lands, own, didx, name):
    _, r, cdim = lands.shape
    tr = _row_tile(r)

    def body(d_ref, l_ref, own_ref, o_ref):
        acc = jnp.where(d_ref[0] == 0, own_ref[...], l_ref[0])
        for dv in range(1, N_DEV):
            acc = acc + jnp.where(d_ref[0] == dv, own_ref[...], l_ref[dv])
        o_ref[...] = acc

    return pl.pallas_call(
        body, name=name,
        grid_spec=pltpu.PrefetchScalarGridSpec(
            num_scalar_prefetch=1, grid=(r // tr,),
            in_specs=[pl.BlockSpec((N_DEV, tr, cdim), lambda i, dd: (0, i, 0)), pl.BlockSpec((tr, cdim), lambda i, dd: (i, 0))],
            out_specs=pl.BlockSpec((tr, cdim), lambda i, dd: (i, 0))),
        out_shape=jax.ShapeDtypeStruct((r, cdim), F32),
        compiler_params=_cp(("arbitrary",)),
    )(didx, lands, own)


def _adamw_math(wv, gg, mv, vv):
    nm = ADAM_B1 * mv + (1.0 - ADAM_B1) * gg
    nv = ADAM_B2 * vv + (1.0 - ADAM_B2) * (gg * gg)
    m_hat = nm / (1.0 - ADAM_B1 ** ADAM_STEP)
    v_hat = nv / (1.0 - ADAM_B2 ** ADAM_STEP)
    return -ADAM_LR * (m_hat / (jnp.sqrt(v_hat) + ADAM_EPS) + ADAM_WD * wv), nm, nv


def _adamw_halves(w, mine, theirs, m, v, cidx, name):
    r, cdim = w.shape
    rh = r // 2
    tr = _row_tile(rh)
    nblk = rh // tr

    def body(c_ref, w_ref, a_ref, b_ref, m_ref, v_ref, g_ref, d_ref, nm_ref, nv_ref):
        gg = jnp.where(pl.program_id(0) == c_ref[0], a_ref[...], b_ref[...])
        g_ref[...] = gg
        d_ref[...], nm_ref[...], nv_ref[...] = _adamw_math(w_ref[...], gg, m_ref[...], v_ref[...])

    full = pl.BlockSpec((tr, cdim), lambda hh, i, c: (hh * nblk + i, 0))
    half = pl.BlockSpec((tr, cdim), lambda hh, i, c: (i, 0))
    return pl.pallas_call(
        body, name=name,
        grid_spec=pltpu.PrefetchScalarGridSpec(
            num_scalar_prefetch=1, grid=(2, nblk),
            in_specs=[full, half, half, full, full], out_specs=[full] * 4),
        out_shape=[jax.ShapeDtypeStruct((r, cdim), F32)] * 4,
        compiler_params=_cp(("arbitrary", "arbitrary")),
    )(cidx, w, mine, theirs, m, v)


def _adamw(w, g, m, v, name):
    r, cdim = w.shape
    tr = _row_tile(r) if r % 8 == 0 else r

    def body(w_ref, g_ref, m_ref, v_ref, d_ref, nm_ref, nv_ref):
        d_ref[...], nm_ref[...], nv_ref[...] = _adamw_math(w_ref[...], g_ref[...], m_ref[...], v_ref[...])

    spec = pl.BlockSpec((tr, cdim), lambda i: (i, 0))
    return pl.pallas_call(
        body, name=name, grid=(r // tr,), in_specs=[spec] * 4, out_specs=[spec] * 3,
        out_shape=[jax.ShapeDtypeStruct((r, cdim), F32)] * 3,
        compiler_params=_cp(("arbitrary",)),
    )(w, g, m, v)


def _ada_fwd(c_all, w_ada, b_cols):
    nb, _ = c_all.shape
    n = w_ada.shape[1]

    def body(c_ref, w_ref, b_ref, o_ref):
        cv = c_ref[...]
        o_ref[...] = _mm(cv * _sigmoid(cv), w_ref[...]) + b_ref[...]

    return pl.pallas_call(
        body, name="ada_fwd", out_shape=jax.ShapeDtypeStruct((nb, n), F32),
        compiler_params=pltpu.CompilerParams(vmem_limit_bytes=VMEM_LIMIT),
    )(c_all, w_ada, b_cols)


def _ada_bwd(c_all, dmod_all, dmod_cols):
    d = c_all.shape[1]
    n = dmod_cols.shape[1]

    def body(c_ref, da_ref, dc_ref, gw_ref, gb_ref):
        cv = c_ref[...]
        gw_ref[...] = _mm_tn(cv * _sigmoid(cv), dc_ref[...])
        gb_ref[...] = _colsum(da_ref[...])

    return pl.pallas_call(
        body, name="ada_bwd",
        out_shape=[jax.ShapeDtypeStruct((d, n), F32), jax.ShapeDtypeStruct((1, dmod_all.shape[1]), F32)],
        compiler_params=pltpu.CompilerParams(vmem_limit_bytes=VMEM_LIMIT),
    )(c_all, dmod_all, dmod_cols)


def _proj_fwd(x2, modv, ws, cols, b_in, seq, name, proj_in=None):
    t, d = x2.shape
    n = len(ws)
    ns = ws[0].shape[1]
    tm = min(TM_PROJ, seq)
    tpb = seq // tm
    first = proj_in is None

    def body(c_ref, x_ref, mod_ref, *refs):
        w_refs, b_ref = refs[:n], refs[n]
        outs = refs[n + 1 if first else n + 2:]
        proj_ref, h_s = outs[0], outs[-1]
        s = pl.program_id(1)

        @pl.when(s == 0)
        def _():
            h = (x_ref[...] * (1.0 + mod_ref[1:2, :]) + mod_ref[0:1, :]).astype(BF16)
            h_s[...] = h
            if first:
                outs[1][...] = h

        for k in range(n):
            @pl.when(s == k)
            def _():
                proj_ref[...] = jnp.dot(h_s[...], w_refs[k][...], preferred_element_type=F32) + b_ref[...]

    in_specs = [pl.BlockSpec((tm, d), lambda i, s, c: (i, 0)),
                pl.BlockSpec((None, 8, d), lambda i, s, c: (i // tpb, 0, 0))]
    in_specs += [pl.BlockSpec((d, ns), lambda i, s, c: (0, 0))] * n
    in_specs += [pl.BlockSpec((1, ns), lambda i, s, c: (0, c[s]))]
    out_specs = [pl.BlockSpec((tm, ns), lambda i, s, c: (i, c[s]))]
    out_shape = [jax.ShapeDtypeStruct((t, N_CHIPS * ns), F32)]
    args = [cols, x2, modv, *ws, b_in]
    aliases = {}
    if first:
        out_specs.append(pl.BlockSpec((tm, d), lambda i, s, c: (i, 0)))
        out_shape.append(jax.ShapeDtypeStruct((t, d), BF16))
    else:
        in_specs.append(_ANY)
        args.append(proj_in)
        aliases = {len(args) - 1: 0}
    return pl.pallas_call(
        body, name=name,
        grid_spec=pltpu.PrefetchScalarGridSpec(
            num_scalar_prefetch=1, grid=(t // tm, n), in_specs=in_specs, out_specs=out_specs,
            scratch_shapes=[pltpu.VMEM((tm, d), BF16)]),
        out_shape=out_shape, input_output_aliases=aliases,
        compiler_params=_cp(("arbitrary", "arbitrary")),
    )(*args)


def _lru_gates(xl, wc_ref, bc_ref, wa_ref, ba_ref, wx_ref, bx_ref, lam_ref):
    xc = bc_ref[...] + wc_ref[CONV_WIDTH - 1:CONV_WIDTH, :] * xl
    for k in range(CONV_WIDTH - 1):
        xc = xc + wc_ref[k:k + 1, :] * _shift_down(xl, CONV_WIDTH - 1 - k)
    r = _sigmoid(_mm(xc, wa_ref[...]) + ba_ref[...])
    gi = _sigmoid_t(_mm(xc, wx_ref[...]) + bx_ref[...])
    nl = -lam_ref[...]
    e = jnp.exp(-jnp.abs(nl))
    u = 1.0 + e
    dlt = u - 1.0
    log1p_e = jnp.where(dlt == 0.0, e, jnp.log(u) * (e / jnp.where(dlt == 0.0, 1.0, dlt)))
    big_l = -LRU_C * (jnp.maximum(nl, 0.0) + log1p_e)
    la = big_l * r
    a = jnp.exp(la)
    m2 = jnp.tanh(-la) * (a * a + 1.0)
    return xc, r, gi, big_l, a, m2


def _lru_prep(proj, lru_w, nb, seq):
    t = proj.shape[0]
    w = LRU_HEADS * HEAD
    w_conv, b_conv, w_a, b_a, w_x, b_x, lam = lru_w

    def body(x_ref, wc_ref, bc_ref, wa_ref, ba_ref, wx_ref, bx_ref, lam_ref, a_ref, inp_ref):
        xc, r, gi, big_l, a, m2 = _lru_gates(x_ref[...], wc_ref, bc_ref, wa_ref, ba_ref, wx_ref, bx_ref, lam_ref)
        a_ref[...] = a
        inp_ref[...] = jnp.sqrt(m2) * (gi * xc)

    col = lambda b, hd: (0, hd)
    head = lambda b, hd: (hd, 0, 0)
    tok = lambda b, hd: (b, hd)
    return pl.pallas_call(
        body, name="lru_prep", grid=(nb, LRU_HEADS),
        in_specs=[pl.BlockSpec((seq, HEAD), tok),
                  pl.BlockSpec((CONV_WIDTH, HEAD), col), pl.BlockSpec((1, HEAD), col),
                  pl.BlockSpec((None, HEAD, HEAD), head), pl.BlockSpec((1, HEAD), col),
                  pl.BlockSpec((None, HEAD, HEAD), head), pl.BlockSpec((1, HEAD), col),
                  pl.BlockSpec((1, HEAD), col)],
        out_specs=[pl.BlockSpec((seq, HEAD), tok)] * 2,
        out_shape=[jax.ShapeDtypeStruct((t, w), F32)] * 2,
        compiler_params=_cp(("arbitrary", "arbitrary")),
    )(proj, w_conv, b_conv, w_a, b_a, w_x, b_x, lam)


def _scan(a3, b3, reverse, name):
    nb, seq, w = a3.shape
    tc = min(TC_SCAN, seq)
    nchunk = seq // tc
    ntile = tc // 8

    def combine(av, bv):
        rows = lax.broadcasted_iota(jnp.int32, av.shape, 0)
        for s in (1, 2, 4):
            if reverse:
                keep = rows < 8 - s
                a_sh, b_sh = pltpu.roll(av, 8 - s, 0), pltpu.roll(bv, 8 - s, 0)
            else:
                keep = rows >= s
                a_sh, b_sh = pltpu.roll(av, s, 0), pltpu.roll(bv, s, 0)
            bv = jnp.where(keep, bv + av * b_sh, bv)
            av = jnp.where(keep, av * a_sh, av)
        return av, bv

    def body(a_ref, b_ref, h_ref, carry):
        @pl.when(pl.program_id(0) == 0)
        def _():
            carry[...] = jnp.zeros_like(carry)

        for b in range(nb):
            def tile(j, hprev):
                jj = ntile - 1 - j if reverse else j
                base = pl.multiple_of(jj * 8, 8)
                av, bv = a_ref[b, pl.ds(base, 8), :], b_ref[b, pl.ds(base, 8), :]
                av, bv = combine(av, av * bv if reverse else bv)
                h = bv + av * hprev
                h_ref[b, pl.ds(base, 8), :] = h
                edge = h[0:1, :] if reverse else h[7:8, :]
                return jnp.broadcast_to(edge, (8, w))

            carry[b] = lax.fori_loop(0, ntile, tile, carry[b])

    imap = (lambda i: (0, nchunk - 1 - i, 0)) if reverse else (lambda i: (0, i, 0))
    spec = pl.BlockSpec((nb, tc, w), imap)
    return pl.pallas_call(
        body, name=name, grid=(nchunk,), in_specs=[spec, spec], out_specs=spec,
        out_shape=jax.ShapeDtypeStruct((nb, seq, w), F32),
        scratch_shapes=[pltpu.VMEM((nb, 8, w), F32)],
        compiler_params=_cp(("arbitrary",)),
    )(a3, b3)


def _sgu_mask():
    ti = lax.broadcasted_iota(jnp.int32, (HEAD, HEAD), 0) // SGU_CHUNK
    si = lax.broadcasted_iota(jnp.int32, (HEAD, HEAD), 1) // SGU_CHUNK
    return si <= ti


def _sgu_specs(tm, d_sgu):
    pw = 256
    first_u = (2 * LRU_HEADS * HEAD) // pw
    n_piece = d_sgu // pw
    specs = [pl.BlockSpec((tm, pw), functools.partial(lambda i, k: (i, k), k=first_u + j)) for j in range(2 * n_piece)]
    return specs, n_piece


def _sgu_fwd(proj, w_sp, b_sp_t, ln_g, ln_b):
    t = proj.shape[0]
    d_sgu = SGU_GROUPS * HEAD
    tm = min(TM_SGU, t)
    nblk = tm // HEAD
    specs, n_piece = _sgu_specs(tm, d_sgu)

    def body(*refs):
        u = jnp.concatenate([r[...] for r in refs[:n_piece]], axis=1)
        v = jnp.concatenate([r[...] for r in refs[n_piece:2 * n_piece]], axis=1)
        w_ref, bt_ref, g_ref, b_ref, y_ref = refs[2 * n_piece:]
        ug = _gelu(u)
        xhat, _ = _ln_stats(_gelu(v))
        vn = (xhat * g_ref[...] + b_ref[...]).astype(BF16)
        mask = _sgu_mask()
        for g in range(SGU_GROUPS):
            wm = jnp.where(mask, w_ref[g], 0.0).astype(BF16)
            cols = slice(g * HEAD, (g + 1) * HEAD)
            for n in range(nblk):
                rows = slice(n * HEAD, (n + 1) * HEAD)
                mixed = jnp.dot(wm, vn[rows, cols], preferred_element_type=F32) + bt_ref[:, g:g + 1]
                y_ref[rows, cols] = (ug[rows, cols] * mixed).astype(BF16)

    full = lambda shape: pl.BlockSpec(shape, lambda i: (0,) * len(shape))
    return pl.pallas_call(
        body, name="sgu_fwd", grid=(t // tm,),
        in_specs=specs + [full(w_sp.shape), full(b_sp_t.shape), full(ln_g.shape), full(ln_b.shape)],
        out_specs=pl.BlockSpec((tm, d_sgu), lambda i: (i, 0)),
        out_shape=jax.ShapeDtypeStruct((t, d_sgu), BF16),
        compiler_params=_cp(("arbitrary",)),
    )(*([proj] * (2 * n_piece)), w_sp, b_sp_t, ln_g, ln_b)


def _mix_fwd(hs, proj, y_sgu, x2, modv, w_o_lru_g, w_o_sgu_g, w_out_g, ln1_g, ln1_b, seq):
    t, d = x2.shape
    w = hs.shape[1]
    d_sgu = y_sgu.shape[1]
    nq, _, ns = w_o_sgu_g.shape
    tm = min(TM_MIX, seq)
    tpb = seq // tm

    def body(hs_ref, gl_ref, ys_ref, ga_ref, gb_ref, x_ref, mod_ref, wl_ref, ws_ref, wo_ref, g1_ref, b1_ref,
             yap_ref, ya_ref, yb_ref, mg_ref, mix_ref, x1_ref):
        yap = (hs_ref[...] * _gelu(gl_ref[...])).astype(BF16)
        yap_ref[...] = yap
        y_a = jnp.dot(yap, wl_ref[...], preferred_element_type=F32)
        ys = ys_ref[...]
        y_b = jnp.concatenate([jnp.dot(ys, ws_ref[q], preferred_element_type=F32) for q in range(nq)], axis=1)
        ya_ref[...] = y_a.astype(BF16)
        yb_ref[...] = y_b.astype(BF16)
        merged = (_sigmoid_t(ga_ref[...]) * y_a + _sigmoid_t(gb_ref[...]) * y_b).astype(BF16)
        mg_ref[...] = merged
        mix = jnp.dot(merged, wo_ref[...], preferred_element_type=F32)
        mix_ref[...] = mix
        xhat, _ = _ln_stats(ALPHA * x_ref[...] + (1.0 + mod_ref[2:3, :]) * mix)
        x1_ref[...] = xhat * g1_ref[...] + b1_ref[...]

    row = lambda width, col: pl.BlockSpec((tm, width), functools.partial(lambda i, k: (i, k), k=col))
    full = lambda shape: pl.BlockSpec(shape, lambda i: (0,) * len(shape))
    return pl.pallas_call(
        body, name="mix_fwd", grid=(t // tm,),
        in_specs=[row(w, 0), row(w, 1), row(d_sgu, 0), row(d, 4), row(d, 5), row(d, 0),
                  pl.BlockSpec((None, 8, d), lambda i: (i // tpb, 0, 0)),
                  full(w_o_lru_g.shape), full(w_o_sgu_g.shape), full(w_out_g.shape), full(ln1_g.shape), full(ln1_b.shape)],
        out_specs=[row(w, 0), row(d, 0), row(d, 0), row(d, 0), row(d, 0), row(d, 0)],
        out_shape=[jax.ShapeDtypeStruct((t, w), BF16), jax.ShapeDtypeStruct((t, d), BF16),
                   jax.ShapeDtypeStruct((t, d), BF16), jax.ShapeDtypeStruct((t, d), BF16),
                   jax.ShapeDtypeStruct((t, d), F32), jax.ShapeDtypeStruct((t, d), F32)],
        compiler_params=_cp(("arbitrary",)),
    )(hs, proj, y_sgu, proj, proj, x2, modv, w_o_lru_g, w_o_sgu_g, w_out_g, ln1_g, ln1_b)


def _mlp_fwd(x1, modv, w_up_g, w_down_g, ln2_g, ln2_b, target, nb, seq):
    t, d = x1.shape
    nq, _, ns = w_up_g.shape
    tm = min(TM_MLP, seq)
    tpb = seq // tm
    nt = t // tm

    def body(x1_ref, mod_ref, wu_ref, wd_ref, g2_ref, b2_ref, tg_ref,
             rl_ref, act_ref, h2_ref, dz2_ref, df_ref, st_ref, pb_ref, h2_s, acc):
        i, j = pl.program_id(0), pl.program_id(1)

        @pl.when(j == 0)
        def _():
            h2 =(x1_ref[...] * (1.0 + mod_ref[4:5, :]) + mod_ref[3:4, :]).astype(BF16)
            h2_s[...] = h2
            h2_ref[...] = h2
            acc[...] = jnp.zeros_like(acc)

        @pl.when((i == 0) & (j == 0))
        def _():
            st_ref[...] = jnp.zeros_like(st_ref)

        @pl.when((i % tpb == 0) & (j == 0))
        def _():
            pb_ref[...] = jnp.zeros_like(pb_ref)

        r = jnp.maximum(jnp.dot(h2_s[...], wu_ref[...], preferred_element_type=F32), 0.0)
        act = (r * r).astype(BF16)
        rl_ref[...] = r.astype(BF16)
        act_ref[...] = act
        acc[...] += jnp.dot(act, wd_ref[...], preferred_element_type=F32)

        @pl.when(j == nq - 1)
        def _():
            f = acc[...]
            xhat, rstd = _ln_stats(ALPHA * x1_ref[...] + (1.0 + mod_ref[5:6, :]) * f)
            y = xhat * g2_ref[...] + b2_ref[...]
            err = y - tg_ref[...]
            dy = err * (1.0 / d)
            dz2 = _ln_bwd(dy * g2_ref[...], xhat, rstd)
            dz2_ref[...] = dz2
            df_ref[...] = ((1.0 + mod_ref[5:6, :]) * dz2).astype(BF16)
            st_ref[0:1, :] += _colsum(dy * xhat)
            st_ref[1:2, :] += _colsum(dy)
            st_ref[2:3, :] += (0.5 / d) * jnp.sum(_colsum(err * err), axis=1, keepdims=True)
            pb_ref[0:1, :] += _colsum(dz2 * f)

    tok = lambda i, j: (i, 0)
    return pl.pallas_call(
        body, name="mlp_fwd", grid=(nt, nq),
        in_specs=[pl.BlockSpec((tm, d), tok), pl.BlockSpec((None, 8, d), lambda i, j: (i // tpb, 0, 0)),
                  pl.BlockSpec((None, d, ns), lambda i, j: (j, 0, 0)), pl.BlockSpec((ns, d), lambda i, j: (j, 0)),
                  pl.BlockSpec((1, d), lambda i, j: (0, 0)), pl.BlockSpec((1, d), lambda i, j: (0, 0)),
                  pl.BlockSpec((tm, d), tok)],
        out_specs=[pl.BlockSpec((tm, ns), lambda i, j: (i, j)), pl.BlockSpec((tm, ns), lambda i, j: (i, j)),
                   pl.BlockSpec((tm, d), tok), pl.BlockSpec((tm, d), tok), pl.BlockSpec((tm, d), tok),
                   pl.BlockSpec((8, d), lambda i, j: (0, 0)), pl.BlockSpec((None, 8, d), lambda i, j: (i // tpb, 0, 0))],
        out_shape=[jax.ShapeDtypeStruct((t, nq * ns), BF16), jax.ShapeDtypeStruct((t, nq * ns), BF16),
                   jax.ShapeDtypeStruct((t, d), BF16),
                   jax.ShapeDtypeStruct((t, d), F32), jax.ShapeDtypeStruct((t, d), BF16),
                   jax.ShapeDtypeStruct((8, d), F32), jax.ShapeDtypeStruct((nb, 8, d), F32)],
        scratch_shapes=[pltpu.VMEM((tm, d), BF16), pltpu.VMEM((tm, d), F32)],
        compiler_params=_cp(("arbitrary", "arbitrary")),
    )(x1, modv, w_up_g, w_down_g, ln2_g, ln2_b, target)


def _mlp_bwd(df, up, w_down_g, w_up_g, dz2, x2, mix, modv, ln1_g, ln1_b, nb, seq):
    t, d = x2.shape
    nq, _, ns = w_up_g.shape
    tm = min(TM_MLP, seq)
    tpb = seq // tm

    def body(df_ref, rl_ref, wd_ref, wu_ref, dz2_ref, x_ref, mix_ref, mod_ref, g1_ref, b1_ref,
             dup_ref, dz1_ref, dmix_ref, st_ref, pb_ref, acc):
        i, j = pl.program_id(0), pl.program_id(1)

        @pl.when(j == 0)
        def _():
            acc[...] = jnp.zeros_like(acc)

        @pl.when((i == 0) & (j == 0))
        def _():
            st_ref[...] = jnp.zeros_like(st_ref)

        @pl.when((i % tpb == 0) & (j == 0))
        def _():
            pb_ref[...] = jnp.zeros_like(pb_ref)

        dup = (_mm_nt(df_ref[...], wd_ref[...]) * (2.0 * rl_ref[...].astype(F32))).astype(BF16)
        dup_ref[...] = dup
        acc[...] += _mm_nt(dup, wu_ref[...])

        @pl.when(j == nq - 1)
        def _():
            dh2 = acc[...]
            mix = mix_ref[...]
            xhat, rstd = _ln_stats(ALPHA * x_ref[...] + (1.0 + mod_ref[2:3, :]) * mix)
            x1 = xhat * g1_ref[...] + b1_ref[...]
            dx1 = ALPHA * dz2_ref[...] + dh2 * (1.0 + mod_ref[4:5, :])
            dz1 = _ln_bwd(dx1 * g1_ref[...], xhat, rstd)
            dz1_ref[...] = dz1
            dmix_ref[...] = ((1.0 + mod_ref[2:3, :]) * dz1).astype(BF16)
            st_ref[0:1, :] += _colsum(dx1 * xhat)
            st_ref[1:2, :] += _colsum(dx1)
            pb_ref[0:1, :] += _colsum(dh2 * x1)
            pb_ref[1:2, :] += _colsum(dh2)
            pb_ref[2:3, :] += _colsum(dz1 * mix)

    tok = lambda i, j: (i, 0)
    chunk = lambda i, j: (i, j)
    return pl.pallas_call(
        body, name="mlp_bwd", grid=(t // tm, nq),
        in_specs=[pl.BlockSpec((tm, d), tok), pl.BlockSpec((tm, ns), chunk),
                  pl.BlockSpec((ns, d), lambda i, j: (j, 0)), pl.BlockSpec((None, d, ns), lambda i, j: (j, 0, 0)),
                  pl.BlockSpec((tm, d), tok), pl.BlockSpec((tm, d), tok), pl.BlockSpec((tm, d), tok),
                  pl.BlockSpec((None, 8, d), lambda i, j: (i // tpb, 0, 0)),
                  pl.BlockSpec((1, d), lambda i, j: (0, 0)), pl.BlockSpec((1, d), lambda i, j: (0, 0))],
        out_specs=[pl.BlockSpec((tm, ns), chunk),
                   pl.BlockSpec((tm, d), tok), pl.BlockSpec((tm, d), tok),
                   pl.BlockSpec((8, d), lambda i, j: (0, 0)), pl.BlockSpec((None, 8, d), lambda i, j: (i // tpb, 0, 0))],
        out_shape=[jax.ShapeDtypeStruct((t, nq * ns), BF16),
                   jax.ShapeDtypeStruct((t, d), F32), jax.ShapeDtypeStruct((t, d), BF16),
                   jax.ShapeDtypeStruct((8, d), F32), jax.ShapeDtypeStruct((nb, 8, d), F32)],
        scratch_shapes=[pltpu.VMEM((tm, d), F32)],
        compiler_params=_cp(("arbitrary", "arbitrary")),
    )(df, up, w_down_g, w_up_g, dz2, x2, mix, modv, ln1_g, ln1_b)


def _mix_bwd(dmix, proj, y_a, y_b, hs, w_out_g, w_o_lru_g, w_o_sgu_g, seq, after=()):
    t, d = dmix.shape
    w = hs.shape[1]
    nq, d_sgu, ns = w_o_sgu_g.shape
    tm = min(TM_MIX, seq)

    def body(dmix_ref, ga_ref, gb_ref, ya_ref, yb_ref, gl_ref, hs_ref, wo_ref, wl_ref, ws_ref,
             dya_ref, dyb_ref, dga_ref, dgb_ref, dgl_ref, dyl_ref, dys_ref):
        dmerged = _mm_nt(dmix_ref[...], wo_ref[...])
        sa, sb = _sigmoid_t(ga_ref[...]), _sigmoid_t(gb_ref[...])
        dy_a = (dmerged * sa).astype(BF16)
        dy_b = (dmerged * sb).astype(BF16)
        dya_ref[...] = dy_a
        dyb_ref[...] = dy_b
        dga_ref[...] = (dmerged * ya_ref[...].astype(F32) * (sa * (1.0 - sa))).astype(BF16)
        dgb_ref[...] = (dmerged * yb_ref[...].astype(F32) * (sb * (1.0 - sb))).astype(BF16)
        dyap = _mm_nt(dy_a, wl_ref[...])
        gel, dgel = _gelu_and_grad(gl_ref[...])
        dyl_ref[...] = dyap * gel
        dgl_ref[...] = (dyap * hs_ref[...] * dgel).astype(BF16)
        dys = _mm_nt(dy_b[:, 0:ns], ws_ref[0])
        for q in range(1, nq):
            dys = dys + _mm_nt(dy_b[:, q * ns:(q + 1) * ns], ws_ref[q])
        dys_ref[...] = dys

    row = lambda width, col: pl.BlockSpec((tm, width), functools.partial(lambda i, k: (i, k), k=col))
    full = lambda shape: pl.BlockSpec(shape, lambda i: (0,) * len(shape))
    return pl.pallas_call(
        _ordered(body, 10, after), name="mix_bwd", grid=(t // tm,),
        in_specs=[row(d, 0), row(d, 4), row(d, 5), row(d, 0), row(d, 0), row(w, 1), row(w, 0),
                  full(w_out_g.shape), full(w_o_lru_g.shape), full(w_o_sgu_g.shape)] + [_ANY] * len(after),
        out_specs=[row(d, 0), row(d, 0), row(d, 0), row(d, 0), row(w, 0), row(w, 0), row(d_sgu, 0)],
        out_shape=[jax.ShapeDtypeStruct((t, d), BF16), jax.ShapeDtypeStruct((t, d), BF16),
                   jax.ShapeDtypeStruct((t, d), BF16), jax.ShapeDtypeStruct((t, d), BF16),
                   jax.ShapeDtypeStruct((t, w), BF16), jax.ShapeDtypeStruct((t, w), F32),
                   jax.ShapeDtypeStruct((t, d_sgu), F32)],
        compiler_params=_cp(("arbitrary",)),
    )(dmix, proj, proj, y_a, y_b, proj, hs, w_out_g, w_o_lru_g, w_o_sgu_g, *after)


def _sgu_bwd(proj, dys, w_sp, b_sp_t, ln_g, ln_b, after=()):
    t = proj.shape[0]
    d_sgu = SGU_GROUPS * HEAD
    tm = min(TM_SGU, t)
    nblk = tm // HEAD
    specs, n_piece = _sgu_specs(tm, d_sgu)

    def body(*refs):
        u = jnp.concatenate([r[...] for r in refs[:n_piece]], axis=1)
        v = jnp.concatenate([r[...] for r in refs[n_piece:2 * n_piece]], axis=1)
        dys_ref, w_ref, bt_ref, g_ref, b_ref, du_ref, dv_ref, dw_ref, st_ref, dbt_ref, dvn_s = refs[2 * n_piece:]

        @pl.when(pl.program_id(0) == 0)
        def _():
            dw_ref[...] = jnp.zeros_like(dw_ref)
            st_ref[...] = jnp.zeros_like(st_ref)
            dbt_ref[...] = jnp.zeros_like(dbt_ref)

        ug, dug_du = _gelu_and_grad(u)
        vg, dvg_dv = _gelu_and_grad(v)
        xhat, rstd = _ln_stats(vg)
        vn = (xhat * g_ref[...] + b_ref[...]).astype(BF16)
        dys_v = dys_ref[...]
        mask = _sgu_mask()
        for g in range(SGU_GROUPS):
            wm = jnp.where(mask, w_ref[g], 0.0).astype(BF16)
            cols = slice(g * HEAD, (g + 1) * HEAD)
            dw_g = jnp.zeros((HEAD, HEAD), F32)
            db_g = jnp.zeros((HEAD, 1), F32)
            for n in range(nblk):
                rows = slice(n * HEAD, (n + 1) * HEAD)
                vn_blk = vn[rows, cols]
                mixed = jnp.dot(wm, vn_blk, preferred_element_type=F32) + bt_ref[:, g:g + 1]
                dy_blk = dys_v[rows, cols]
                du_ref[rows, cols] = (dy_blk * mixed * dug_du[rows, cols]).astype(BF16)
                dmx = dy_blk * ug[rows, cols]
                dvn_s[rows, cols] = _mm_tn(wm, dmx)
                dw_g = dw_g + _mm_nt(dmx, vn_blk)
                db_g = db_g + jnp.sum(dmx, axis=1, keepdims=True)
            dw_ref[g] += jnp.where(mask, dw_g, 0.0)
            dbt_ref[:, g:g + 1] += db_g
        dvn = dvn_s[...]
        st_ref[0:1, :] += _colsum(dvn * xhat)
        st_ref[1:2, :] += _colsum(dvn)
        dv_ref[...] = (_ln_bwd(dvn * g_ref[...], xhat, rstd) * dvg_dv).astype(BF16)

    full = lambda shape: pl.BlockSpec(shape, lambda i: (0,) * len(shape))
    tok = pl.BlockSpec((tm, d_sgu), lambda i: (i, 0))
    return pl.pallas_call(
        _ordered(body, 2 * n_piece + 5, after), name="sgu_bwd", grid=(t // tm,),
        in_specs=specs + [tok, full(w_sp.shape), full(b_sp_t.shape), full(ln_g.shape), full(ln_b.shape)]
        + [_ANY] * len(after),
        out_specs=[tok, tok, full(w_sp.shape), full((8, d_sgu)), full((HEAD, HEAD))],
        out_shape=[jax.ShapeDtypeStruct((t, d_sgu), BF16), jax.ShapeDtypeStruct((t, d_sgu), BF16),
                   jax.ShapeDtypeStruct(w_sp.shape, F32), jax.ShapeDtypeStruct((8, d_sgu), F32),
                   jax.ShapeDtypeStruct((HEAD, HEAD), F32)],
        scratch_shapes=[pltpu.VMEM((tm, d_sgu), F32)],
        compiler_params=_cp(("arbitrary",)),
    )(*([proj] * (2 * n_piece)), dys, w_sp, b_sp_t, ln_g, ln_b, *after)


def _lru_bwd(proj, hs, e, dyl, lru_w, nb, seq, after=()):
    t = proj.shape[0]
    w = LRU_HEADS * HEAD
    w_conv, b_conv, w_a, b_a, w_x, b_x, lam = lru_w

    def body(x_ref, hs_ref, e_ref, dy_ref, wc_ref, bc_ref, wa_ref, ba_ref, wx_ref, bx_ref, lam_ref,
             dxl_ref, dwa_ref, dwx_ref, st_ref):
        @pl.when(pl.program_id(1) == 0)
        def _():
            dwa_ref[...] = jnp.zeros_like(dwa_ref)
            dwx_ref[...] = jnp.zeros_like(dwx_ref)
            st_ref[...] = jnp.zeros_like(st_ref)

        xl = x_ref[...]
        xc, r, gi, big_l, a, m2 = _lru_gates(xl, wc_ref, bc_ref, wa_ref, ba_ref, wx_ref, bx_ref, lam_ref)
        inv_mult = lax.rsqrt(m2)
        mult = m2 * inv_mult
        dh = dy_ref[...] + _shift_up(e_ref[...], 1)
        da = dh * _shift_down(hs_ref[...], 1)
        dmult = dh * (gi * xc)
        d_i = dh * (mult * xc)
        dxc = dh * (mult * gi)
        dla = a * (da - dmult * (a * inv_mult))
        dr = dla * big_l
        d_big_l = _colsum(dla * r)
        dra = dr * (r * (1.0 - r))
        dia = d_i * (gi * (1.0 - gi))
        dwa_ref[...] += _mm_tn(xc, dra)
        dwx_ref[...] += _mm_tn(xc, dia)
        dxc = dxc + _mm_nt(dra, wa_ref[...]) + _mm_nt(dia, wx_ref[...])
        dxl = wc_ref[CONV_WIDTH - 1:CONV_WIDTH, :] * dxc
        st_ref[4 + CONV_WIDTH - 1:4 + CONV_WIDTH, :] += _colsum(dxc * xl)
        for k in range(CONV_WIDTH - 1):
            ahead = _shift_up(dxc, CONV_WIDTH - 1 - k)
            dxl = dxl + wc_ref[k:k + 1, :] * ahead
            st_ref[4 + k:5 + k, :] += _colsum(ahead * xl)
        dxl_ref[...] = dxl.astype(BF16)
        st_ref[0:1, :] += _colsum(dra)
        st_ref[1:2, :] += _colsum(dia)
        st_ref[2:3, :] += d_big_l * (LRU_C * _sigmoid(-lam_ref[...]))
        st_ref[3:4, :] += _colsum(dxc)

    col = lambda hd, b: (0, hd)
    head = lambda hd, b: (hd, 0, 0)
    tok = lambda hd, b: (b, hd)
    seq_blk = pl.BlockSpec((seq, HEAD), tok)
    return pl.pallas_call(
        _ordered(body, 11, after), name="lru_bwd", grid=(LRU_HEADS, nb),
        in_specs=[seq_blk, seq_blk, seq_blk, seq_blk,
                  pl.BlockSpec((CONV_WIDTH, HEAD), col), pl.BlockSpec((1, HEAD), col),
                  pl.BlockSpec((None, HEAD, HEAD), head), pl.BlockSpec((1, HEAD), col),
                  pl.BlockSpec((None, HEAD, HEAD), head), pl.BlockSpec((1, HEAD), col),
                  pl.BlockSpec((1, HEAD), col)] + [_ANY] * len(after),
        out_specs=[seq_blk, pl.BlockSpec((None, HEAD, HEAD), head), pl.BlockSpec((None, HEAD, HEAD), head),
                   pl.BlockSpec((8, HEAD), col)],
        out_shape=[jax.ShapeDtypeStruct((t, w), BF16), jax.ShapeDtypeStruct((LRU_HEADS, HEAD, HEAD), F32),
                   jax.ShapeDtypeStruct((LRU_HEADS, HEAD, HEAD), F32), jax.ShapeDtypeStruct((8, w), F32)],
        compiler_params=_cp(("arbitrary", "arbitrary")),
    )(proj, hs, e, dyl, w_conv, b_conv, w_a, b_a, w_x, b_x, lam, *after)


def _weight_grad(a, g, col_shards, name, after=()):
    t, k = a.shape
    n = g.shape[1]
    tt = min(TT_DW, t)
    tk = k if k <= 1536 else 1024
    ns = n // N_CHIPS if col_shards else n
    narrow = col_shards and ns < 512
    tn = n if narrow else min(ns, 768 if ns % 768 == 0 else 1024)
    while ns % tn and not narrow:
        tn //= 2
    per = max(ns // tn, 1)

    def body(a_ref, g_ref, o_ref):
        @pl.when(pl.program_id(2) == 0)
        def _():
            o_ref[...] = jnp.zeros_like(o_ref)

        res = _mm_tn(a_ref[...], g_ref[...])
        if narrow:
            for q in range(N_CHIPS):
                o_ref[q] += res[:, q * ns:(q + 1) * ns]
        else:
            o_ref[...] += res

    if narrow:
        out_spec = pl.BlockSpec((N_CHIPS, tk, ns), lambda i, j, s: (0, i, 0))
        out_shape = jax.ShapeDtypeStruct((N_CHIPS, k, ns), F32)
    elif col_shards:
        out_spec = pl.BlockSpec((None, tk, tn), lambda i, j, s: (j // per, i, j % per))
        out_shape = jax.ShapeDtypeStruct((N_CHIPS, k, ns), F32)
    else:
        out_spec = pl.BlockSpec((tk, tn), lambda i, j, s: (i, j))
        out_shape = jax.ShapeDtypeStruct((k, n), F32)
    return pl.pallas_call(
        _ordered(body, 2, after), name=name, grid=(k // tk, n // tn, t // tt),
        in_specs=[pl.BlockSpec((tt, tk), lambda i, j, s: (s, i)), pl.BlockSpec((tt, tn), lambda i, j, s: (s, j))]
        + [_ANY] * len(after),
        out_specs=out_spec, out_shape=out_shape,
        compiler_params=_cp(("arbitrary", "arbitrary", "arbitrary")),
    )(a, g, *after)


def _input_grad(dproj, w_in_g, dz1, x2, modv, nb, seq, after=()):
    t, d = x2.shape
    nq, _, ns = w_in_g.shape
    tm = min(TM_DH, seq)
    tpb = seq // tm

    def body(dp_ref, w_ref, dz1_ref, x_ref, mod_ref, gx_ref, db_ref, pb_ref, acc):
        i, q = pl.program_id(0), pl.program_id(1)

        @pl.when(q == 0)
        def _():
            acc[...] = jnp.zeros_like(acc)

        @pl.when((i == 0) & (q == 0))
        def _():
            db_ref[...] = jnp.zeros_like(db_ref)

        @pl.when((i % tpb == 0) & (q == 0))
        def _():
            pb_ref[...] = jnp.zeros_like(pb_ref)

        dp = dp_ref[...]
        acc[...] += _mm_nt(dp, w_ref[...])
        db_ref[q, 0:1, :] += _colsum(dp.astype(F32))

        @pl.when(q == nq - 1)
        def _():
            dh = acc[...]
            gx_ref[...] = ALPHA * dz1_ref[...] + dh * (1.0 + mod_ref[1:2, :])
            pb_ref[0:1, :] += _colsum(dh * x_ref[...])
            pb_ref[1:2, :] += _colsum(dh)

    tok = lambda i, q: (i, 0)
    return pl.pallas_call(
        _ordered(body, 5, after), name="input_grad", grid=(t // tm, nq),
        in_specs=[pl.BlockSpec((tm, ns), lambda i, q: (i, q)), pl.BlockSpec((None, d, ns), lambda i, q: (q, 0, 0)),
                  pl.BlockSpec((tm, d), tok), pl.BlockSpec((tm, d), tok),
                  pl.BlockSpec((None, 8, d), lambda i, q: (i // tpb, 0, 0))] + [_ANY] * len(after),
        out_specs=[pl.BlockSpec((tm, d), tok), pl.BlockSpec((nq, 8, ns), lambda i, q: (0, 0, 0)),
                   pl.BlockSpec((None, 8, d), lambda i, q: (i // tpb, 0, 0))],
        out_shape=[jax.ShapeDtypeStruct((t, d), F32), jax.ShapeDtypeStruct((nq, 8, ns), F32),
                   jax.ShapeDtypeStruct((nb, 8, d), F32)],
        scratch_shapes=[pltpu.VMEM((tm, d), F32)],
        compiler_params=_cp(("arbitrary", "arbitrary")),
    )(dproj, w_in_g, dz1, x2, modv, *after)


def _rows128(v):
    flat = v.reshape(-1, HEAD)
    pad = (-flat.shape[0]) % 8
    return jnp.pad(flat, ((0, pad), (0, 0))) if pad else flat


def kernel(x, c, w_ada, b_ada, w_in, b_in, w_conv, b_conv, w_rg_a, b_rg_a, w_rg_x, b_rg_x, lru_lambda, w_sp, b_sp, ln_v_g, ln_v_b, w_o_lru, w_o_sgu, w_out, ln1_g, ln1_b, w_up, w_down, ln2_g, ln2_b, loss_target, m_w_ada, m_b_ada, m_w_in, m_b_in, m_w_conv, m_b_conv, m_w_rg_a, m_b_rg_a, m_w_rg_x, m_b_rg_x, m_lru_lambda, m_w_sp, m_b_sp, m_ln_v_g, m_ln_v_b, m_w_o_lru, m_w_o_sgu, m_w_out, m_ln1_g, m_ln1_b, m_w_up, m_w_down, m_ln2_g, m_ln2_b, v_w_ada, v_b_ada, v_w_in, v_b_in, v_w_conv, v_b_conv, v_w_rg_a, v_b_rg_a, v_w_rg_x, v_b_rg_x, v_lru_lambda, v_w_sp, v_b_sp, v_ln_v_g, v_ln_v_b, v_w_o_lru, v_w_o_sgu, v_w_out, v_ln1_g, v_ln1_b, v_w_up, v_w_down, v_ln2_g, v_ln2_b):
    given = dict(locals())
    nb, seq, d = x.shape
    t = nb * seq
    w_lru = LRU_HEADS * HEAD
    d_sgu = SGU_GROUPS * HEAD
    xi, yi, ci = lax.axis_index("x"), lax.axis_index("y"), lax.axis_index("c")
    chip = 2 * xi + yi
    dev = 2 * chip + ci
    cidx = jnp.reshape(ci, (1,)).astype(jnp.int32)

    x2 = x.reshape(t, d)
    target = loss_target.reshape(t, d)

    big = ["w_in", "w_o_lru", "w_o_sgu", "w_out", "w_up", "w_down"]
    shards_a = [w_in[0].astype(BF16)]
    shards_b = [given[n][0].astype(BF16) for n in big[1:]]
    pidx = jnp.reshape(chip, (1,)).astype(jnp.int32)

    c_rows = _rows128(c)
    wconv_rows = _rows128(w_conv[0])
    slab0 = _all_gather_small(jnp.concatenate([c_rows, wconv_rows], axis=0), "gather_c_wconv")
    slab0 = slab0.reshape(N_DEV, -1, HEAD)
    c_all = slab0[:, :c_rows.shape[0]].reshape(N_DEV * nb, d)
    n_wc = CONV_WIDTH * (w_lru // N_CHIPS) // HEAD
    wc = slab0[0::2, c_rows.shape[0]:c_rows.shape[0] + n_wc].reshape(N_CHIPS, CONV_WIDTH, w_lru // N_CHIPS)
    w_conv_full = jnp.transpose(wc, (1, 0, 2)).reshape(CONV_WIDTH, w_lru)

    n_ada = w_ada.shape[2]
    b_ada_cols = lax.dynamic_slice(b_ada, (0, chip * n_ada), (1, n_ada))
    mod_cols = _ada_fwd(c_all, w_ada[0], b_ada_cols)
    half = (N_DEV * nb) // 2
    mod_half = lax.dynamic_slice(mod_cols, (ci * half, 0), (half, n_ada))
    mod_g = _all_gather_small(mod_half, "gather_mod").reshape(N_CHIPS, 2, half, n_ada)
    mod_all = jnp.transpose(mod_g, (1, 2, 0, 3)).reshape(N_DEV * nb, N_CHIPS * n_ada)
    mod_loc = lax.dynamic_slice(mod_all, (dev * nb, 0), (nb, N_CHIPS * n_ada)).reshape(nb, 6, d)
    modv = jnp.pad(mod_loc, ((0, 0), (0, 2), (0, 0)))

    lru_w = (w_conv_full, b_conv, w_rg_a[0], b_rg_a, w_rg_x[0], b_rg_x, lru_lambda)
    b_sp_t = jnp.transpose(b_sp[0])

    land = lambda s: jax.ShapeDtypeStruct((N_CHIPS,) + s.shape, s.dtype)
    sds = lambda s: jax.ShapeDtypeStruct(s.shape, s.dtype)
    started_a = _split_start(shards_a, [sds(shards_a[0])] * 3, _peer_gather_copies((0, 1, 2)), 3, "gather_w_in_start",
                             after=(modv,))
    started_b = _split_start(shards_b, [land(s) for s in shards_b], _gather_copies, 3 * len(shards_b),
                             "gather_w_rest_start", after=(started_a[-1],))

    ids = lambda *v: jnp.stack(v).astype(jnp.int32)
    modv_t = modv + started_b[-1][0:1, 0:1]
    proj, h = _proj_fwd(x2, modv_t, [started_a[2]], ids(chip), b_in, seq, "proj_fwd_own")
    own_a, lands_a = _split_wait(started_a, 1, _peer_gather_copies((0, 1)), "gather_w_in_wait_near", after=(proj,))
    (proj,) = _proj_fwd(x2, modv, lands_a[:2], ids(chip ^ 1, chip ^ 2), b_in, seq, "proj_fwd_near", proj_in=proj)
    own_a, lands_a = _split_wait((started_a[0], started_a[1], *own_a, *lands_a, started_a[-1]), 1,
                                 _peer_gather_copies((2,)), "gather_w_in_wait_far", after=(proj,))
    (proj,) = _proj_fwd(x2, modv, lands_a[2:], ids(chip ^ 3), b_in, seq, "proj_fwd_far", proj_in=proj)
    w_in_g = lax.empty((N_CHIPS,) + shards_a[0].shape, BF16)
    for k, (shard, slot) in enumerate(zip(own_a + lands_a, (chip, chip ^ 1, chip ^ 2, chip ^ 3))):
        (w_in_g,) = _fill_own_slot([w_in_g], [shard], ids(slot), ["place_w_in_%d" % k])
    a, inp = _lru_prep(proj, lru_w, nb, seq)
    a3 = a.reshape(nb, seq, w_lru)
    hs = _scan(a3, inp.reshape(nb, seq, w_lru), False, "lru_scan").reshape(t, w_lru)
    y_sgu = _sgu_fwd(proj, w_sp[0], b_sp_t, ln_v_g, ln_v_b)
    shards_b, lands_b = _split_wait(started_b, len(shards_b), _gather_copies, "gather_w_rest_wait", after=(hs, y_sgu))
    w_o_lru_g, w_o_sgu_g, w_out_g, w_up_g, w_down_g = _fill_own_slot(lands_b, shards_b, pidx, ["own_" + n for n in big[1:]])
    w_o_lru_g = w_o_lru_g.reshape(w_lru, d)
    w_out_g = w_out_g.reshape(d, d)
    w_down_g = w_down_g.reshape(-1, d)
    yap, y_a, y_b, merged, mix, x1 = _mix_fwd(hs, proj, y_sgu, x2, modv, w_o_lru_g, w_o_sgu_g, w_out_g, ln1_g, ln1_b, seq)
    up, act, h2, dz2, df, st2, pb2 = _mlp_fwd(x1, modv, w_up_g, w_down_g, ln2_g, ln2_b, target, nb, seq)
    loss = lax.psum(st2[2, 0], ("x", "y", "c"))

    part = {}

    def to_sibling_start(group, tag, after=()):
        g4 = []
        for n in group:
            shard = given[n].shape[1:]
            g4.append(part[n].reshape(N_CHIPS, 2, shard[0] // 2, shard[1]))
        shapes = [jax.ShapeDtypeStruct((N_CHIPS,) + g.shape[2:], F32) for g in g4]
        return _split_start(g4, shapes, _to_sibling_copies, len(g4), "grads_to_sibling_start_" + tag, after)

    def to_chips_start(group, started, tag, after=()):
        g4, recv = _split_wait(started, len(group), _to_sibling_copies, "grads_to_sibling_wait_" + tag, after)
        own4 = [_add_own_half(g4[k], recv[k], cidx, "grad_pair_sum_" + n) for k, n in enumerate(group)]
        shapes = [jax.ShapeDtypeStruct((3,) + o.shape[1:], BF16) for o in own4]
        return _split_start(own4, shapes, _chip_exchange_copies, 3 * len(own4), "grads_chip_exchange_start_" + tag)

    def chips_finish(group, started, tag, after=()):
        own4, slots = _split_wait(started, len(group), _chip_exchange_copies, "grads_chip_exchange_wait_" + tag, after)
        return [_sum_own_and_peers(own4[k], slots[k], pidx, "grad_chip_sum_" + n) for k, n in enumerate(group)]

    dup, dz1, dmix, st1, pb1 = _mlp_bwd(df, up, w_down_g, w_up_g, dz2, x2, mix, modv, ln1_g, ln1_b, nb, seq)
    group1 = ["w_up", "w_down"]
    part["w_up"] = _weight_grad(h2, dup, True, "grad_w_up")
    part["w_down"] = _weight_grad(act, df, False, "grad_w_down")
    sib1 = to_sibling_start(group1, "mlp")
    dy_a, dy_b, dga, dgb, dgl, dyl, dys = _mix_bwd(dmix, proj, y_a, y_b, hs, w_out_g, w_o_lru_g, w_o_sgu_g, seq,
                                                   after=(sib1[-1],))
    group2 = ["w_o_lru", "w_o_sgu", "w_out"]
    part["w_o_lru"] = _weight_grad(yap, dy_a, False, "grad_w_o_lru")
    part["w_o_sgu"] = _weight_grad(y_sgu, dy_b, True, "grad_w_o_sgu")
    part["w_out"] = _weight_grad(merged, dmix, False, "grad_w_out")
    chips1 = to_chips_start(group1, sib1, "mlp", after=(dys, part["w_o_lru"], part["w_o_sgu"], part["w_out"]))
    sib2 = to_sibling_start(group2, "mix", after=(chips1[-1],))
    du, dv, g_w_sp, st_sgu, g_b_sp_t = _sgu_bwd(proj, dys, w_sp[0], b_sp_t, ln_v_g, ln_v_b, after=(sib2[-1],))
    dyl3 = dyl.reshape(nb, seq, w_lru)
    e = _scan(a3, dyl3, True, "lru_scan_bwd").reshape(t, w_lru)
    chips2 = to_chips_start(group2, sib2, "mix", after=(e, du))
    dxl, g_w_rg_a, g_w_rg_x, st_lru = _lru_bwd(proj, hs, e, dyl, lru_w, nb, seq, after=(chips2[-1],))
    dproj = jnp.concatenate([dxl, dgl, du, dv, dga, dgb], axis=1)

    didx = jnp.reshape(dev, (1,)).astype(jnp.int32)
    early = [
        ("w_conv", st_lru[4:8]), ("b_conv", st_lru[3]), ("w_rg_a", g_w_rg_a), ("b_rg_a", st_lru[0]),
        ("w_rg_x", g_w_rg_x), ("b_rg_x", st_lru[1]), ("lru_lambda", st_lru[2]), ("w_sp", g_w_sp),
        ("b_sp", jnp.transpose(g_b_sp_t[:, :SGU_GROUPS])), ("ln_v_g", st_sgu[0]), ("ln_v_b", st_sgu[1]),
        ("ln1_g", st1[0]), ("ln1_b", st1[1]), ("ln2_g", st2[0]), ("ln2_b", st2[1]),
    ]
    pieces_e = [_rows128(v) for _, v in early]
    slab_e = jnp.concatenate(pieces_e, axis=0)
    slab_e = jnp.pad(slab_e, ((0, (-slab_e.shape[0]) % TR_EW), (0, 0)))
    small_st = _split_start([slab_e], [jax.ShapeDtypeStruct((N_DEV,) + slab_e.shape, F32)], _all_devices_copies, N_DEV - 1,
                            "small_grads_start")

    group3 = ["w_in"]
    part["w_in"] = _weight_grad(h, dproj, True, "grad_w_in", after=(small_st[-1],))
    sib3 = to_sibling_start(group3, "in")
    chips3 = to_chips_start(group3, sib3, "in")
    grad_x2, g_b_in4, pb0 = _input_grad(dproj, w_in_g, dz1, x2, modv, nb, seq, after=(chips3[-1],))
    halves12 = chips_finish(group1, chips1, "mlp", after=(grad_x2,)) + chips_finish(group2, chips2, "mix", after=(grad_x2,))
    swap12 = _split_start(halves12, [jax.ShapeDtypeStruct(hv.shape, F32) for hv in halves12], _swap_copies, len(halves12),
                          "grads_swap_start")
    grads = {}

    dmod_loc = jnp.stack([pb0[:, 1], pb0[:, 0], pb1[:, 2], pb1[:, 1], pb1[:, 0], pb2[:, 0]], axis=1)
    late = [("dmod", dmod_loc), ("b_in", g_b_in4[:, 0])]
    pieces_l = [_rows128(v) for _, v in late]
    slab_l = jnp.concatenate(pieces_l, axis=0)
    gathered = _all_gather_small(slab_l, "gather_small_grads", after=(swap12[-1],)).reshape(N_DEV, slab_l.shape[0], HEAD)
    rows_dmod = dmod_loc.size // HEAD
    dmod_all = gathered[:, :rows_dmod].reshape(N_DEV * nb, 6 * d)
    grads["b_in"] = _sum_slots(gathered[:, rows_dmod:], "grad_b_in_sum").reshape(1, -1)

    (slab_e,), (lands_e,) = _split_wait(small_st, 1, _all_devices_copies, "small_grads_wait", after=(gathered,))
    summed = _sum_devices(lands_e, slab_e, didx, "small_grad_sum")
    off = 0
    for (n, v), piece in zip(early, pieces_e):
        grads[n] = summed[off:off + v.size // HEAD].reshape(v.shape)
        off += piece.shape[0]

    mine12, theirs12 = _split_wait(swap12, len(halves12), _swap_copies, "grads_swap_wait", after=(summed,))
    (mine3,) = chips_finish(group3, chips3, "in", after=(summed,))
    (theirs3,) = _exchange([mine3], [jax.ShapeDtypeStruct(mine3.shape, F32)], _swap_copies, 1, "grads_swap_w_in")
    mine = dict(zip(group1 + group2 + group3, mine12 + [mine3]))
    theirs = dict(zip(group1 + group2 + group3, theirs12 + [theirs3]))

    dmod_cols = lax.dynamic_slice(dmod_all, (0, chip * n_ada), (N_DEV * nb, n_ada))
    grads["w_ada"], grads["b_ada"] = _ada_bwd(c_all, dmod_all, dmod_cols)
    n_wcs = w_lru // N_CHIPS
    grads["w_conv"] = lax.dynamic_slice(grads["w_conv"], (0, chip * n_wcs), (CONV_WIDTH, n_wcs))

    names = ['w_ada', 'b_ada', 'w_in', 'b_in', 'w_conv', 'b_conv', 'w_rg_a', 'b_rg_a', 'w_rg_x', 'b_rg_x', 'lru_lambda',
             'w_sp', 'b_sp', 'ln_v_g', 'ln_v_b', 'w_o_lru', 'w_o_sgu', 'w_out', 'ln1_g', 'ln1_b', 'w_up', 'w_down',
             'ln2_g', 'ln2_b']
    out_g, out_d, out_m, out_v = [], [], [], []
    for n in names:
        wv = given[n]
        shape2 = (-1, wv.shape[-1])
        w2, m2, v2 = wv.reshape(shape2), given["m_" + n].reshape(shape2), given["v_" + n].reshape(shape2)
        if n in big:
            g2, dlt, nm, nv = _adamw_halves(w2, mine[n], theirs[n], m2, v2, cidx, "adamw_" + n)
        else:
            g2 = grads[n].reshape(wv.shape).reshape(shape2)
            dlt, nm, nv = _adamw(w2, g2, m2, v2, "adamw_" + n)
        out_g.append(g2.reshape(wv.shape))
        out_d.append(dlt.reshape(wv.shape))
        out_m.append(nm.reshape(wv.shape))
        out_v.append(nv.reshape(wv.shape))

    return (loss, grad_x2.reshape(nb, seq, d), *out_g, *out_d, *out_m, *out_v)
```

```python
import functools
import math

import jax
import jax.numpy as jnp
from jax import lax
from jax.experimental import pallas as pl
from jax.experimental.pallas import tpu as pltpu

F32 = jnp.float32
BF16 = jnp.bfloat16
MESH = pl.DeviceIdType.MESH

N_CHIPS = 4
N_DEV = 8
LRU_HEADS = 10
HEAD = 128
SGU_GROUPS = 6
SGU_CHUNK = 64
CONV_WIDTH = 4
LRU_C = 8.0
ALPHA = 2.0 ** 0.25
LN_EPS = 1e-5
ADAM_LR, ADAM_B1, ADAM_B2, ADAM_EPS, ADAM_WD, ADAM_STEP = 0.001, 0.9, 0.999, 1e-08, 0.01, 10

VMEM_LIMIT = 56 * 1024 * 1024
TM_PROJ = 1024
TM_MIX = 256
TM_MLP = 512
TM_SGU = 512
TM_DH = 512
TT_DW = 1024
TC_SCAN = 256
TR_EW = 256


def _cp(sem=None):
    return pltpu.CompilerParams(dimension_semantics=sem, vmem_limit_bytes=VMEM_LIMIT)


def _mm(a, b):
    return jnp.dot(a.astype(BF16), b.astype(BF16), preferred_element_type=F32)


def _mm_nt(a, b):
    return lax.dot_general(a.astype(BF16), b.astype(BF16), (((1,), (1,)), ((), ())), preferred_element_type=F32)


def _mm_tn(a, b):
    return lax.dot_general(a.astype(BF16), b.astype(BF16), (((0,), (0,)), ((), ())), preferred_element_type=F32)


def _sigmoid(x):
    return 1.0 / (1.0 + jnp.exp(-x))


def _sigmoid_t(x):
    return 0.5 * jnp.tanh(0.5 * x) + 0.5


_GELU_K = math.sqrt(2.0 / math.pi)


def _gelu(x):
    t = jnp.tanh(_GELU_K * (x + 0.044715 * (x * x * x)))
    return 0.5 * x * (1.0 + t)


def _gelu_and_grad(x):
    x2 = x * x
    t = jnp.tanh(_GELU_K * (x + 0.044715 * (x2 * x)))
    g = 0.5 * x * (1.0 + t)
    dg = 0.5 * (1.0 + t) + 0.5 * x * (1.0 - t * t) * (_GELU_K * (1.0 + 3.0 * 0.044715 * x2))
    return g, dg


def _ln_stats(z):
    mu = jnp.mean(z, axis=-1, keepdims=True)
    zc = z - mu
    var = jnp.mean(zc * zc, axis=-1, keepdims=True)
    rstd = lax.rsqrt(var + LN_EPS)
    return zc * rstd, rstd


def _ln_bwd(dxh, xhat, rstd):
    m1 = jnp.mean(dxh, axis=-1, keepdims=True)
    m2 = jnp.mean(dxh * xhat, axis=-1, keepdims=True)
    return rstd * (dxh - m1 - xhat * m2)


def _colsum(v):
    return jnp.sum(v, axis=0, keepdims=True)


def _shift_down(v, j):
    if j == 0:
        return v
    rows = lax.broadcasted_iota(jnp.int32, v.shape, 0)
    return jnp.where(rows >= j, pltpu.roll(v, j, 0), 0.0)


def _shift_up(v, j):
    if j == 0:
        return v
    n = v.shape[0]
    rows = lax.broadcasted_iota(jnp.int32, v.shape, 0)
    return jnp.where(rows < n - j, pltpu.roll(v, n - j, 0), 0.0)


def _my_pos():
    return lax.axis_index("x"), lax.axis_index("y"), lax.axis_index("c")


def _all_gather_small(v, name, after=()):
    m_per, n = v.shape

    def body(x_ref, out_ref, send_sems, recv_sems, local_sem):
        x, y, c = _my_pos()
        me, sibling = (x, y, c), (x, y, 1 - c)
        chips = [(1 - x, y), (x, 1 - y), (1 - x, 1 - y)]

        def rows(px, py, pc):
            return out_ref.at[pl.ds((4 * px + 2 * py + pc) * m_per, m_per), :]

        def copy(k, block, to, src=None):
            return pltpu.make_async_remote_copy(
                src_ref=rows(*block) if src is None else src, dst_ref=rows(*block),
                send_sem=send_sems.at[k], recv_sem=recv_sems.at[k], device_id=to, device_id_type=MESH)

        mine = pltpu.make_async_copy(x_ref, rows(*me), local_sem)
        mine.start()
        first = [copy(0, me, sibling, src=x_ref)]
        first += [copy(1 + j, me, (*chip, c), src=x_ref) for j, chip in enumerate(chips)]
        for cp in first:
            cp.start()
        passed = [copy(4 + j, (*chip, c), sibling) for j, chip in enumerate(chips)]
        for j, chip in enumerate(chips):
            copy(1 + j, (*chip, c), me).wait_recv()
            passed[j].start()
        copy(0, sibling, me).wait_recv()
        for j, chip in enumerate(chips):
            copy(4 + j, (*chip, 1 - c), me).wait_recv()
        for cp in first + passed:
            cp.wait_send()
        mine.wait()

    return pl.pallas_call(
        _ordered(body, 1, after), name=name,
        out_shape=jax.ShapeDtypeStruct((N_DEV * m_per, n), v.dtype),
        in_specs=[pl.BlockSpec(memory_space=pltpu.VMEM)] + [pl.BlockSpec(memory_space=pl.ANY)] * len(after),
        out_specs=pl.BlockSpec(memory_space=pltpu.VMEM),
        scratch_shapes=[pltpu.SemaphoreType.DMA((7,)), pltpu.SemaphoreType.DMA((7,)), pltpu.SemaphoreType.DMA],
        compiler_params=pltpu.CompilerParams(vmem_limit_bytes=VMEM_LIMIT),
    )(v, *after)


_HBM = pl.BlockSpec(memory_space=pltpu.HBM)
_ANY = pl.BlockSpec(memory_space=pl.ANY)
_SEM = pl.BlockSpec(memory_space=pltpu.SEMAPHORE)
_EFFECT = pltpu.SideEffectType.DATAFLOW_SIDE_EFFECTING


def _ordered(body, n_in, after):
    k = len(after)
    if not k:
        return body
    return lambda *refs: body(*refs[:n_in], *refs[n_in + k:])


def _gather_copies(ins, lands, send_sems, recv_sems):
    x, y, c = _my_pos()
    p = 2 * x + y
    peers = [(x, 1 - y), (1 - x, y), (1 - x, 1 - y)]
    sends, recvs = [], []
    for k in range(len(ins)):
        for j, (qx, qy) in enumerate(peers):
            sems = dict(send_sem=send_sems.at[3 * k + j], recv_sem=recv_sems.at[3 * k + j],
                        device_id=(qx, qy, c), device_id_type=MESH)
            sends.append(pltpu.make_async_remote_copy(src_ref=ins[k], dst_ref=lands[k].at[p], **sems))
            recvs.append(pltpu.make_async_remote_copy(src_ref=ins[k], dst_ref=lands[k].at[2 * qx + qy], **sems))
    return sends, recvs


def _peer_gather_copies(peers):
    def copies(ins, lands, send_sems, recv_sems):
        x, y, c = _my_pos()
        where = [(x, 1 - y), (1 - x, y), (1 - x, 1 - y)]
        cps = [pltpu.make_async_remote_copy(
            src_ref=ins[0], dst_ref=lands[j], send_sem=send_sems.at[j], recv_sem=recv_sems.at[j],
            device_id=(*where[j], c), device_id_type=MESH) for j in peers]
        return cps, cps
    return copies


def _to_sibling_copies(ins, lands, send_sems, recv_sems):
    x, y, c = _my_pos()
    cps = [pltpu.make_async_remote_copy(
        src_ref=ins[k].at[:, 1 - c], dst_ref=lands[k], send_sem=send_sems.at[k], recv_sem=recv_sems.at[k],
        device_id=(x, y, 1 - c), device_id_type=MESH) for k in range(len(ins))]
    return cps, cps


def _chip_exchange_copies(ins, lands, send_sems, recv_sems):
    x, y, c = _my_pos()
    peers = [(x, 1 - y), (1 - x, y), (1 - x, 1 - y)]
    cps = []
    for k in range(len(ins)):
        for j, (qx, qy) in enumerate(peers):
            cps.append(pltpu.make_async_remote_copy(
                src_ref=ins[k].at[2 * qx + qy], dst_ref=lands[k].at[j], send_sem=send_sems.at[3 * k + j],
                recv_sem=recv_sems.at[3 * k + j], device_id=(qx, qy, c), device_id_type=MESH))
    return cps, cps


def _all_devices_copies(ins, lands, send_sems, recv_sems):
    x, y, c = _my_pos()
    me = 4 * x + 2 * y + c
    sends, recvs = [], []
    for r in range(1, N_DEV):
        px = 1 - x if r & 4 else x
        py = 1 - y if r & 2 else y
        pc = 1 - c if r & 1 else c
        sems = dict(send_sem=send_sems.at[r - 1], recv_sem=recv_sems.at[r - 1], device_id=(px, py, pc), device_id_type=MESH)
        sends.append(pltpu.make_async_remote_copy(src_ref=ins[0], dst_ref=lands[0].at[me], **sems))
        recvs.append(pltpu.make_async_remote_copy(src_ref=ins[0], dst_ref=lands[0].at[4 * px + 2 * py + pc], **sems))
    return sends, recvs


def _swap_copies(ins, lands, send_sems, recv_sems):
    x, y, c = _my_pos()
    cps = [pltpu.make_async_remote_copy(
        src_ref=ins[k], dst_ref=lands[k], send_sem=send_sems.at[k], recv_sem=recv_sems.at[k],
        device_id=(x, y, 1 - c), device_id_type=MESH) for k in range(len(ins))]
    return cps, cps


def _split_start(ins, land_shapes, copies, n_sems, name, after=()):
    n, nl = len(ins), len(land_shapes)
    first_out = n + nl + len(after)

    def body(*refs):
        in_refs, land_refs = refs[:n], refs[n:n + nl]
        send_sems, recv_sems = refs[first_out:first_out + 2]
        token = refs[-1]
        sends, _ = copies(in_refs, land_refs, send_sems, recv_sems)
        for cp in sends:
            cp.start()
        token[...] = jnp.zeros_like(token)

    lands = [pltpu.with_memory_space_constraint(lax.empty(s.shape, s.dtype), pltpu.HBM) for s in land_shapes]
    ins = [pltpu.with_memory_space_constraint(s, pltpu.HBM) for s in ins]
    return pl.pallas_call(
        body, name=name,
        out_shape=(pltpu.SemaphoreType.DMA((n_sems,)), pltpu.SemaphoreType.DMA((n_sems,)),
                   *[pltpu.HBM(s.shape, s.dtype) for s in ins], *[pltpu.HBM(s.shape, s.dtype) for s in lands],
                   jax.ShapeDtypeStruct((8, HEAD), F32)),
        in_specs=[_HBM] * (n + nl) + [pl.BlockSpec(memory_space=pl.ANY)] * len(after),
        out_specs=(_SEM, _SEM, *([_HBM] * (n + nl)), pl.BlockSpec(memory_space=pltpu.VMEM)),
        input_output_aliases={k: 2 + k for k in range(n + nl)},
        compiler_params=pltpu.CompilerParams(has_side_effects=_EFFECT),
    )(*ins, *lands, *after)


def _split_wait(started, n, copies, name, after=()):
    send_sems, recv_sems = started[0], started[1]
    bufs = started[2:-1]
    nb = len(bufs)

    def body(*refs):
        in_refs, land_refs = refs[:n], refs[n:nb]
        sends, recvs = copies(in_refs, land_refs, refs[nb], refs[nb + 1])
        for cp in sends:
            cp.wait_send()
        for cp in recvs:
            cp.wait_recv()

    outs = pl.pallas_call(
        body, name=name,
        out_shape=tuple(pltpu.HBM(s.shape, s.dtype) for s in bufs),
        in_specs=[_HBM] * nb + [_SEM, _SEM] + [pl.BlockSpec(memory_space=pl.ANY)] * len(after),
        out_specs=tuple([_HBM] * nb),
        input_output_aliases={k: k for k in range(nb)},
        compiler_params=pltpu.CompilerParams(has_side_effects=_EFFECT),
    )(*bufs, send_sems, recv_sems, *after)
    return list(outs[:n]), list(outs[n:])


def _fill_own_slot(gathered, shards, pidx, names):
    outs = []
    for g, s, name in zip(gathered, shards, names):
        r, cdim = s.shape
        tr = _row_tile(r)

        def body(p_ref, s_ref, g_ref, o_ref):
            o_ref[...] = s_ref[...]

        outs.append(pl.pallas_call(
            body, name=name,
            grid_spec=pltpu.PrefetchScalarGridSpec(
                num_scalar_prefetch=1, grid=(r // tr,),
                in_specs=[pl.BlockSpec((tr, cdim), lambda i, p: (i, 0)), pl.BlockSpec(memory_space=pl.ANY)],
                out_specs=pl.BlockSpec((None, tr, cdim), lambda i, p: (p[0], i, 0))),
            out_shape=jax.ShapeDtypeStruct(g.shape, g.dtype),
            input_output_aliases={2: 0},
            compiler_params=_cp(("arbitrary",)),
        )(pidx, s, g))
    return outs


def _sum_own_and_peers(own4, slots, pidx, name):
    _, rh, cdim = own4.shape
    tr = _row_tile(rh)

    def body(p_ref, own_ref, s_ref, o_ref):
        acc = own_ref[...].astype(F32)
        for j in range(3):
            acc = acc + s_ref[j].astype(F32)
        o_ref[...] = acc

    return pl.pallas_call(
        body, name=name,
        grid_spec=pltpu.PrefetchScalarGridSpec(
            num_scalar_prefetch=1, grid=(rh // tr,),
            in_specs=[pl.BlockSpec((None, tr, cdim), lambda i, p: (p[0], i, 0)),
                      pl.BlockSpec((3, tr, cdim), lambda i, p: (0, i, 0))],
            out_specs=pl.BlockSpec((tr, cdim), lambda i, p: (i, 0))),
        out_shape=jax.ShapeDtypeStruct((rh, cdim), F32),
        compiler_params=_cp(("arbitrary",)),
    )(pidx, own4, slots)


def _exchange(ins, land_shapes, copies, n_sems, name):
    n, nl = len(ins), len(land_shapes)

    def body(*refs):
        sends, recvs = copies(refs[:n], refs[n:n + nl], refs[n + nl], refs[n + nl + 1])
        for cp in sends:
            cp.start()
        for cp in sends:
            cp.wait_send()
        for cp in recvs:
            cp.wait_recv()

    any_spec = pl.BlockSpec(memory_space=pl.ANY)
    return pl.pallas_call(
        body, name=name,
        out_shape=[jax.ShapeDtypeStruct(s.shape, s.dtype) for s in land_shapes],
        in_specs=[any_spec] * n, out_specs=[any_spec] * nl,
        scratch_shapes=[pltpu.SemaphoreType.DMA((n_sems,)), pltpu.SemaphoreType.DMA((n_sems,))],
    )(*ins)


def _row_tile(r):
    t = min(TR_EW, r)
    while r % t:
        t //= 2
    return t


def _add_own_half(g4, recv, cidx, name):
    _, _, rh, cdim = g4.shape
    tr = _row_tile(rh)

    def body(c_ref, a_ref, b_ref, o_ref):
        o_ref[...] = (a_ref[...] + b_ref[...]).astype(BF16)

    return pl.pallas_call(
        body, name=name,
        grid_spec=pltpu.PrefetchScalarGridSpec(
            num_scalar_prefetch=1, grid=(N_CHIPS, rh // tr),
            in_specs=[pl.BlockSpec((None, None, tr, cdim), lambda q, i, c: (q, c[0], i, 0)),
                      pl.BlockSpec((None, tr, cdim), lambda q, i, c: (q, i, 0))],
            out_specs=pl.BlockSpec((None, tr, cdim), lambda q, i, c: (q, i, 0))),
        out_shape=jax.ShapeDtypeStruct(recv.shape, BF16),
        compiler_params=_cp(("arbitrary", "arbitrary")),
    )(cidx, g4, recv)


def _sum_slots(v, name):
    n, r, cdim = v.shape
    tr = _row_tile(r)

    def body(v_ref, o_ref):
        acc = v_ref[0].astype(F32)
        for k in range(1, n):
            acc = acc + v_ref[k].astype(F32)
        o_ref[...] = acc

    return pl.pallas_call(
        body, name=name, grid=(r // tr,),
        in_specs=[pl.BlockSpec((n, tr, cdim), lambda i: (0, i, 0))],
        out_specs=pl.BlockSpec((tr, cdim), lambda i: (i, 0)),
        out_shape=jax.ShapeDtypeStruct((r, cdim), F32),
        compiler_params=_cp(("arbitrary",)),
    )(v)


def _sum_devices(lands, own, didx, name):
    _, r, cdim = lands.shape
    tr = _row_tile(r)

    def body(d_ref, l_ref, own_ref, o_ref):
        acc = jnp.where(d_ref[0] == 0, own_ref[...], l_ref[0])
        for dv in range(1, N_DEV):
            acc = acc + jnp.where(d_ref[0] == dv, own_ref[...], l_ref[dv])
        o_ref[...] = acc

    return pl.pallas_call(
        body, name=name,
        grid_spec=pltpu.PrefetchScalarGridSpec(
            num_scalar_prefetch=1, grid=(r // tr,),
            in_specs=[pl.BlockSpec((N_DEV, tr, cdim), lambda i, dd: (0, i, 0)), pl.BlockSpec((tr, cdim), lambda i, dd: (i, 0))],
            out_specs=pl.BlockSpec((tr, cdim), lambda i, dd: (i, 0))),
        out_shape=jax.ShapeDtypeStruct((r, cdim), F32),
        compiler_params=_cp(("arbitrary",)),
    )(didx, lands, own)


def _adamw_math(wv, gg, mv, vv):
    nm = ADAM_B1 * mv + (1.0 - ADAM_B1) * gg
    nv = ADAM_B2 * vv + (1.0 - ADAM_B2) * (gg * gg)
    m_hat = nm / (1.0 - ADAM_B1 ** ADAM_STEP)
    v_hat = nv / (1.0 - ADAM_B2 ** ADAM_STEP)
    return -ADAM_LR * (m_hat / (jnp.sqrt(v_hat) + ADAM_EPS) + ADAM_WD * wv), nm, nv


def _adamw_halves(w, mine, theirs, m, v, cidx, name):
    r, cdim = w.shape
    rh = r // 2
    tr = _row_tile(rh)
    nblk = rh // tr

    def body(c_ref, w_ref, a_ref, b_ref, m_ref, v_ref, g_ref, d_ref, nm_ref, nv_ref):
        gg = jnp.where(pl.program_id(0) == c_ref[0], a_ref[...], b_ref[...])
        g_ref[...] = gg
        d_ref[...], nm_ref[...], nv_ref[...] = _adamw_math(w_ref[...], gg, m_ref[...], v_ref[...])

    full = pl.BlockSpec((tr, cdim), lambda hh, i, c: (hh * nblk + i, 0))
    half = pl.BlockSpec((tr, cdim), lambda hh, i, c: (i, 0))
    return pl.pallas_call(
        body, name=name,
        grid_spec=pltpu.PrefetchScalarGridSpec(
            num_scalar_prefetch=1, grid=(2, nblk),
            in_specs=[full, half, half, full, full], out_specs=[full] * 4),
        out_shape=[jax.ShapeDtypeStruct((r, cdim), F32)] * 4,
        compiler_params=_cp(("arbitrary", "arbitrary")),
    )(cidx, w, mine, theirs, m, v)


def _adamw(w, g, m, v, name):
    r, cdim = w.shape
    tr = _row_tile(r) if r % 8 == 0 else r

    def body(w_ref, g_ref, m_ref, v_ref, d_ref, nm_ref, nv_ref):
        d_ref[...], nm_ref[...], nv_ref[...] = _adamw_math(w_ref[...], g_ref[...], m_ref[...], v_ref[...])

    spec = pl.BlockSpec((tr, cdim), lambda i: (i, 0))
    return pl.pallas_call(
        body, name=name, grid=(r // tr,), in_specs=[spec] * 4, out_specs=[spec] * 3,
        out_shape=[jax.ShapeDtypeStruct((r, cdim), F32)] * 3,
        compiler_params=_cp(("arbitrary",)),
    )(w, g, m, v)


def _ada_fwd(c_all, w_ada, b_cols):
    nb, _ = c_all.shape
    n = w_ada.shape[1]

    def body(c_ref, w_ref, b_ref, o_ref):
        cv = c_ref[...]
        o_ref[...] = _mm(cv * _sigmoid(cv), w_ref[...]) + b_ref[...]

    return pl.pallas_call(
        body, name="ada_fwd", out_shape=jax.ShapeDtypeStruct((nb, n), F32),
        compiler_params=pltpu.CompilerParams(vmem_limit_bytes=VMEM_LIMIT),
    )(c_all, w_ada, b_cols)


def _ada_bwd(c_all, dmod_all, dmod_cols):
    d = c_all.shape[1]
    n = dmod_cols.shape[1]

    def body(c_ref, da_ref, dc_ref, gw_ref, gb_ref):
        cv = c_ref[...]
        gw_ref[...] = _mm_tn(cv * _sigmoid(cv), dc_ref[...])
        gb_ref[...] = _colsum(da_ref[...])

    return pl.pallas_call(
        body, name="ada_bwd",
        out_shape=[jax.ShapeDtypeStruct((d, n), F32), jax.ShapeDtypeStruct((1, dmod_all.shape[1]), F32)],
        compiler_params=pltpu.CompilerParams(vmem_limit_bytes=VMEM_LIMIT),
    )(c_all, dmod_all, dmod_cols)


def _proj_fwd(x2, modv, ws, cols, b_in, seq, name, proj_in=None):
    t, d = x2.shape
    n = len(ws)
    ns = ws[0].shape[1]
    tm = min(TM_PROJ, seq)
    tpb = seq // tm
    first = proj_in is None

    def body(c_ref, x_ref, mod_ref, *refs):
        w_refs, b_ref = refs[:n], refs[n]
        outs = refs[n + 1 if first else n + 2:]
        proj_ref, h_s = outs[0], outs[-1]
        s = pl.program_id(1)

        @pl.when(s == 0)
        def _():
            h = (x_ref[...] * (1.0 + mod_ref[1:2, :]) + mod_ref[0:1, :]).astype(BF16)
            h_s[...] = h
            if first:
                outs[1][...] = h

        for k in range(n):
            @pl.when(s == k)
            def _():
                proj_ref[...] = jnp.dot(h_s[...], w_refs[k][...], preferred_element_type=F32) + b_ref[...]

    in_specs = [pl.BlockSpec((tm, d), lambda i, s, c: (i, 0)),
                pl.BlockSpec((None, 8, d), lambda i, s, c: (i // tpb, 0, 0))]
    in_specs += [pl.BlockSpec((d, ns), lambda i, s, c: (0, 0))] * n
    in_specs += [pl.BlockSpec((1, ns), lambda i, s, c: (0, c[s]))]
    out_specs = [pl.BlockSpec((tm, ns), lambda i, s, c: (i, c[s]))]
    out_shape = [jax.ShapeDtypeStruct((t, N_CHIPS * ns), F32)]
    args = [cols, x2, modv, *ws, b_in]
    aliases = {}
    if first:
        out_specs.append(pl.BlockSpec((tm, d), lambda i, s, c: (i, 0)))
        out_shape.append(jax.ShapeDtypeStruct((t, d), BF16))
    else:
        in_specs.append(_ANY)
        args.append(proj_in)
        aliases = {len(args) - 1: 0}
    return pl.pallas_call(
        body, name=name,
        grid_spec=pltpu.PrefetchScalarGridSpec(
            num_scalar_prefetch=1, grid=(t // tm, n), in_specs=in_specs, out_specs=out_specs,
            scratch_shapes=[pltpu.VMEM((tm, d), BF16)]),
        out_shape=out_shape, input_output_aliases=aliases,
        compiler_params=_cp(("arbitrary", "arbitrary")),
    )(*args)


def _lru_gates(xl, wc_ref, bc_ref, wa_ref, ba_ref, wx_ref, bx_ref, lam_ref):
    xc = bc_ref[...] + wc_ref[CONV_WIDTH - 1:CONV_WIDTH, :] * xl
    for k in range(CONV_WIDTH - 1):
        xc = xc + wc_ref[k:k + 1, :] * _shift_down(xl, CONV_WIDTH - 1 - k)
    r = _sigmoid(_mm(xc, wa_ref[...]) + ba_ref[...])
    gi = _sigmoid_t(_mm(xc, wx_ref[...]) + bx_ref[...])
    nl = -lam_ref[...]
    e = jnp.exp(-jnp.abs(nl))
    u = 1.0 + e
    dlt = u - 1.0
    log1p_e = jnp.where(dlt == 0.0, e, jnp.log(u) * (e / jnp.where(dlt == 0.0, 1.0, dlt)))
    big_l = -LRU_C * (jnp.maximum(nl, 0.0) + log1p_e)
    la = big_l * r
    a = jnp.exp(la)
    m2 = jnp.tanh(-la) * (a * a + 1.0)
    return xc, r, gi, big_l, a, m2


def _lru_prep(proj, lru_w, nb, seq):
    t = proj.shape[0]
    w = LRU_HEADS * HEAD
    w_conv, b_conv, w_a, b_a, w_x, b_x, lam = lru_w

    def body(x_ref, wc_ref, bc_ref, wa_ref, ba_ref, wx_ref, bx_ref, lam_ref, a_ref, inp_ref):
        xc, r, gi, big_l, a, m2 = _lru_gates(x_ref[...], wc_ref, bc_ref, wa_ref, ba_ref, wx_ref, bx_ref, lam_ref)
        a_ref[...] = a
        inp_ref[...] = jnp.sqrt(m2) * (gi * xc)

    col = lambda b, hd: (0, hd)
    head = lambda b, hd: (hd, 0, 0)
    tok = lambda b, hd: (b, hd)
    return pl.pallas_call(
        body, name="lru_prep", grid=(nb, LRU_HEADS),
        in_specs=[pl.BlockSpec((seq, HEAD), tok),
                  pl.BlockSpec((CONV_WIDTH, HEAD), col), pl.BlockSpec((1, HEAD), col),
                  pl.BlockSpec((None, HEAD, HEAD), head), pl.BlockSpec((1, HEAD), col),
                  pl.BlockSpec((None, HEAD, HEAD), head), pl.BlockSpec((1, HEAD), col),
                  pl.BlockSpec((1, HEAD), col)],
        out_specs=[pl.BlockSpec((seq, HEAD), tok)] * 2,
        out_shape=[jax.ShapeDtypeStruct((t, w), F32)] * 2,
        compiler_params=_cp(("arbitrary", "arbitrary")),
    )(proj, w_conv, b_conv, w_a, b_a, w_x, b_x, lam)


def _scan(a3, b3, reverse, name):
    nb, seq, w = a3.shape
    tc = min(TC_SCAN, seq)
    nchunk = seq // tc
    ntile = tc // 8

    def combine(av, bv):
        rows = lax.broadcasted_iota(jnp.int32, av.shape, 0)
        for s in (1, 2, 4):
            if reverse:
                keep = rows < 8 - s
                a_sh, b_sh = pltpu.roll(av, 8 - s, 0), pltpu.roll(bv, 8 - s, 0)
            else:
                keep = rows >= s
                a_sh, b_sh = pltpu.roll(av, s, 0), pltpu.roll(bv, s, 0)
            bv = jnp.where(keep, bv + av * b_sh, bv)
            av = jnp.where(keep, av * a_sh, av)
        return av, bv

    def body(a_ref, b_ref, h_ref, carry):
        @pl.when(pl.program_id(0) == 0)
        def _():
            carry[...] = jnp.zeros_like(carry)

        for b in range(nb):
            def tile(j, hprev):
                jj = ntile - 1 - j if reverse else j
                base = pl.multiple_of(jj * 8, 8)
                av, bv = a_ref[b, pl.ds(base, 8), :], b_ref[b, pl.ds(base, 8), :]
                av, bv = combine(av, av * bv if reverse else bv)
                h = bv + av * hprev
                h_ref[b, pl.ds(base, 8), :] = h
                edge = h[0:1, :] if reverse else h[7:8, :]
                return jnp.broadcast_to(edge, (8, w))

            carry[b] = lax.fori_loop(0, ntile, tile, carry[b])

    imap = (lambda i: (0, nchunk - 1 - i, 0)) if reverse else (lambda i: (0, i, 0))
    spec = pl.BlockSpec((nb, tc, w), imap)
    return pl.pallas_call(
        body, name=name, grid=(nchunk,), in_specs=[spec, spec], out_specs=spec,
        out_shape=jax.ShapeDtypeStruct((nb, seq, w), F32),
        scratch_shapes=[pltpu.VMEM((nb, 8, w), F32)],
        compiler_params=_cp(("arbitrary",)),
    )(a3, b3)


def _sgu_mask():
    ti = lax.broadcasted_iota(jnp.int32, (HEAD, HEAD), 0) // SGU_CHUNK
    si = lax.broadcasted_iota(jnp.int32, (HEAD, HEAD), 1) // SGU_CHUNK
    return si <= ti


def _sgu_specs(tm, d_sgu):
    pw = 256
    first_u = (2 * LRU_HEADS * HEAD) // pw
    n_piece = d_sgu // pw
    specs = [pl.BlockSpec((tm, pw), functools.partial(lambda i, k: (i, k), k=first_u + j)) for j in range(2 * n_piece)]
    return specs, n_piece


def _sgu_fwd(proj, w_sp, b_sp_t, ln_g, ln_b):
    t = proj.shape[0]
    d_sgu = SGU_GROUPS * HEAD
    tm = min(TM_SGU, t)
    nblk = tm // HEAD
    specs, n_piece = _sgu_specs(tm, d_sgu)

    def body(*refs):
        u = jnp.concatenate([r[...] for r in refs[:n_piece]], axis=1)
        v = jnp.concatenate([r[...] for r in refs[n_piece:2 * n_piece]], axis=1)
        w_ref, bt_ref, g_ref, b_ref, y_ref = refs[2 * n_piece:]
        ug = _gelu(u)
        xhat, _ = _ln_stats(_gelu(v))
        vn = (xhat * g_ref[...] + b_ref[...]).astype(BF16)
        mask = _sgu_mask()
        for g in range(SGU_GROUPS):
            wm = jnp.where(mask, w_ref[g], 0.0).astype(BF16)
            cols = slice(g * HEAD, (g + 1) * HEAD)
            for n in range(nblk):
                rows = slice(n * HEAD, (n + 1) * HEAD)
                mixed = jnp.dot(wm, vn[rows, cols], preferred_element_type=F32) + bt_ref[:, g:g + 1]
                y_ref[rows, cols] = (ug[rows, cols] * mixed).astype(BF16)

    full = lambda shape: pl.BlockSpec(shape, lambda i: (0,) * len(shape))
    return pl.pallas_call(
        body, name="sgu_fwd", grid=(t // tm,),
        in_specs=specs + [full(w_sp.shape), full(b_sp_t.shape), full(ln_g.shape), full(ln_b.shape)],
        out_specs=pl.BlockSpec((tm, d_sgu), lambda i: (i, 0)),
        out_shape=jax.ShapeDtypeStruct((t, d_sgu), BF16),
        compiler_params=_cp(("arbitrary",)),
    )(*([proj] * (2 * n_piece)), w_sp, b_sp_t, ln_g, ln_b)


def _mix_fwd(hs, proj, y_sgu, x2, modv, w_o_lru_g, w_o_sgu_g, w_out_g, ln1_g, ln1_b, seq):
    t, d = x2.shape
    w = hs.shape[1]
    d_sgu = y_sgu.shape[1]
    nq, _, ns = w_o_sgu_g.shape
    tm = min(TM_MIX, seq)
    tpb = seq // tm

    def body(hs_ref, gl_ref, ys_ref, ga_ref, gb_ref, x_ref, mod_ref, wl_ref, ws_ref, wo_ref, g1_ref, b1_ref,
             yap_ref, ya_ref, yb_ref, mg_ref, mix_ref, x1_ref):
        yap = (hs_ref[...] * _gelu(gl_ref[...])).astype(BF16)
        yap_ref[...] = yap
        y_a = jnp.dot(yap, wl_ref[...], preferred_element_type=F32)
        ys = ys_ref[...]
        y_b = jnp.concatenate([jnp.dot(ys, ws_ref[q], preferred_element_type=F32) for q in range(nq)], axis=1)
        ya_ref[...] = y_a.astype(BF16)
        yb_ref[...] = y_b.astype(BF16)
        merged = (_sigmoid_t(ga_ref[...]) * y_a + _sigmoid_t(gb_ref[...]) * y_b).astype(BF16)
        mg_ref[...] = merged
        mix = jnp.dot(merged, wo_ref[...], preferred_element_type=F32)
        mix_ref[...] = mix
        xhat, _ = _ln_stats(ALPHA * x_ref[...] + (1.0 + mod_ref[2:3, :]) * mix)
        x1_ref[...] = xhat * g1_ref[...] + b1_ref[...]

    row = lambda width, col: pl.BlockSpec((tm, width), functools.partial(lambda i, k: (i, k), k=col))
    full = lambda shape: pl.BlockSpec(shape, lambda i: (0,) * len(shape))
    return pl.pallas_call(
        body, name="mix_fwd", grid=(t // tm,),
        in_specs=[row(w, 0), row(w, 1), row(d_sgu, 0), row(d, 4), row(d, 5), row(d, 0),
                  pl.BlockSpec((None, 8, d), lambda i: (i // tpb, 0, 0)),
                  full(w_o_lru_g.shape), full(w_o_sgu_g.shape), full(w_out_g.shape), full(ln1_g.shape), full(ln1_b.shape)],
        out_specs=[row(w, 0), row(d, 0), row(d, 0), row(d, 0), row(d, 0), row(d, 0)],
        out_shape=[jax.ShapeDtypeStruct((t, w), BF16), jax.ShapeDtypeStruct((t, d), BF16),
                   jax.ShapeDtypeStruct((t, d), BF16), jax.ShapeDtypeStruct((t, d), BF16),
                   jax.ShapeDtypeStruct((t, d), F32), jax.ShapeDtypeStruct((t, d), F32)],
        compiler_params=_cp(("arbitrary",)),
    )(hs, proj, y_sgu, proj, proj, x2, modv, w_o_lru_g, w_o_sgu_g, w_out_g, ln1_g, ln1_b)


def _mlp_fwd(x1, modv, w_up_g, w_down_g, ln2_g, ln2_b, target, nb, seq):
    t, d = x1.shape
    nq, _, ns = w_up_g.shape
    tm = min(TM_MLP, seq)
    tpb = seq // tm
    nt = t // tm

    def body(x1_ref, mod_ref, wu_ref, wd_ref, g2_ref, b2_ref, tg_ref,
             rl_ref, act_ref, h2_ref, dz2_ref, df_ref, st_ref, pb_ref, h2_s, acc):
        i, j = pl.program_id(0), pl.program_id(1)

        @pl.when(j == 0)
        def _():
            h2 =(x1_ref[...] * (1.0 + mod_ref[4:5, :]) + mod_ref[3:4, :]).astype(BF16)
            h2_s[...] = h2
            h2_ref[...] = h2
            acc[...] = jnp.zeros_like(acc)

        @pl.when((i == 0) & (j == 0))
        def _():
            st_ref[...] = jnp.zeros_like(st_ref)

        @pl.when((i % tpb == 0) & (j == 0))
        def _():
            pb_ref[...] = jnp.zeros_like(pb_ref)

        r = jnp.maximum(jnp.dot(h2_s[...], wu_ref[...], preferred_element_type=F32), 0.0)
        act = (r * r).astype(BF16)
        rl_ref[...] = r.astype(BF16)
        act_ref[...] = act
        acc[...] += jnp.dot(act, wd_ref[...], preferred_element_type=F32)

        @pl.when(j == nq - 1)
        def _():
            f = acc[...]
            xhat, rstd = _ln_stats(ALPHA * x1_ref[...] + (1.0 + mod_ref[5:6, :]) * f)
            y = xhat * g2_ref[...] + b2_ref[...]
            err = y - tg_ref[...]
            dy = err * (1.0 / d)
            dz2 = _ln_bwd(dy * g2_ref[...], xhat, rstd)
            dz2_ref[...] = dz2
            df_ref[...] = ((1.0 + mod_ref[5:6, :]) * dz2).astype(BF16)
            st_ref[0:1, :] += _colsum(dy * xhat)
            st_ref[1:2, :] += _colsum(dy)
            st_ref[2:3, :] += (0.5 / d) * jnp.sum(_colsum(err * err), axis=1, keepdims=True)
            pb_ref[0:1, :] += _colsum(dz2 * f)

    tok = lambda i, j: (i, 0)
    return pl.pallas_call(
        body, name="mlp_fwd", grid=(nt, nq),
        in_specs=[pl.BlockSpec((tm, d), tok), pl.BlockSpec((None, 8, d), lambda i, j: (i // tpb, 0, 0)),
                  pl.BlockSpec((None, d, ns), lambda i, j: (j, 0, 0)), pl.BlockSpec((ns, d), lambda i, j: (j, 0)),
                  pl.BlockSpec((1, d), lambda i, j: (0, 0)), pl.BlockSpec((1, d), lambda i, j: (0, 0)),
                  pl.BlockSpec((tm, d), tok)],
        out_specs=[pl.BlockSpec((tm, ns), lambda i, j: (i, j)), pl.BlockSpec((tm, ns), lambda i, j: (i, j)),
                   pl.BlockSpec((tm, d), tok), pl.BlockSpec((tm, d), tok), pl.BlockSpec((tm, d), tok),
                   pl.BlockSpec((8, d), lambda i, j: (0, 0)), pl.BlockSpec((None, 8, d), lambda i, j: (i // tpb, 0, 0))],
        out_shape=[jax.ShapeDtypeStruct((t, nq * ns), BF16), jax.ShapeDtypeStruct((t, nq * ns), BF16),
                   jax.ShapeDtypeStruct((t, d), BF16),
                   jax.ShapeDtypeStruct((t, d), F32), jax.ShapeDtypeStruct((t, d), BF16),
                   jax.ShapeDtypeStruct((8, d), F32), jax.ShapeDtypeStruct((nb, 8, d), F32)],
        scratch_shapes=[pltpu.VMEM((tm, d), BF16), pltpu.VMEM((tm, d), F32)],
        compiler_params=_cp(("arbitrary", "arbitrary")),
    )(x1, modv, w_up_g, w_down_g, ln2_g, ln2_b, target)


def _mlp_bwd(df, up, w_down_g, w_up_g, dz2, x2, mix, modv, ln1_g, ln1_b, nb, seq):
    t, d = x2.shape
    nq, _, ns = w_up_g.shape
    tm = min(TM_MLP, seq)
    tpb = seq // tm

    def body(df_ref, rl_ref, wd_ref, wu_ref, dz2_ref, x_ref, mix_ref, mod_ref, g1_ref, b1_ref,
             dup_ref, dz1_ref, dmix_ref, st_ref, pb_ref, acc):
        i, j = pl.program_id(0), pl.program_id(1)

        @pl.when(j == 0)
        def _():
            acc[...] = jnp.zeros_like(acc)

        @pl.when((i == 0) & (j == 0))
        def _():
            st_ref[...] = jnp.zeros_like(st_ref)

        @pl.when((i % tpb == 0) & (j == 0))
        def _():
            pb_ref[...] = jnp.zeros_like(pb_ref)

        dup = (_mm_nt(df_ref[...], wd_ref[...]) * (2.0 * rl_ref[...].astype(F32))).astype(BF16)
        dup_ref[...] = dup
        acc[...] += _mm_nt(dup, wu_ref[...])

        @pl.when(j == nq - 1)
        def _():
            dh2 = acc[...]
            mix = mix_ref[...]
            xhat, rstd = _ln_stats(ALPHA * x_ref[...] + (1.0 + mod_ref[2:3, :]) * mix)
            x1 = xhat * g1_ref[...] + b1_ref[...]
            dx1 = ALPHA * dz2_ref[...] + dh2 * (1.0 + mod_ref[4:5, :])
            dz1 = _ln_bwd(dx1 * g1_ref[...], xhat, rstd)
            dz1_ref[...] = dz1
            dmix_ref[...] = ((1.0 + mod_ref[2:3, :]) * dz1).astype(BF16)
            st_ref[0:1, :] += _colsum(dx1 * xhat)
            st_ref[1:2, :] += _colsum(dx1)
            pb_ref[0:1, :] += _colsum(dh2 * x1)
            pb_ref[1:2, :] += _colsum(dh2)
            pb_ref[2:3, :] += _colsum(dz1 * mix)

    tok = lambda i, j: (i, 0)
    chunk = lambda i, j: (i, j)
    return pl.pallas_call(
        body, name="mlp_bwd", grid=(t // tm, nq),
        in_specs=[pl.BlockSpec((tm, d), tok), pl.BlockSpec((tm, ns), chunk),
                  pl.BlockSpec((ns, d), lambda i, j: (j, 0)), pl.BlockSpec((None, d, ns), lambda i, j: (j, 0, 0)),
                  pl.BlockSpec((tm, d), tok), pl.BlockSpec((tm, d), tok), pl.BlockSpec((tm, d), tok),
                  pl.BlockSpec((None, 8, d), lambda i, j: (i // tpb, 0, 0)),
                  pl.BlockSpec((1, d), lambda i, j: (0, 0)), pl.BlockSpec((1, d), lambda i, j: (0, 0))],
        out_specs=[pl.BlockSpec((tm, ns), chunk),
                   pl.BlockSpec((tm, d), tok), pl.BlockSpec((tm, d), tok),
                   pl.BlockSpec((8, d), lambda i, j: (0, 0)), pl.BlockSpec((None, 8, d), lambda i, j: (i // tpb, 0, 0))],
        out_shape=[jax.ShapeDtypeStruct((t, nq * ns), BF16),
                   jax.ShapeDtypeStruct((t, d), F32), jax.ShapeDtypeStruct((t, d), BF16),
                   jax.ShapeDtypeStruct((8, d), F32), jax.ShapeDtypeStruct((nb, 8, d), F32)],
        scratch_shapes=[pltpu.VMEM((tm, d), F32)],
        compiler_params=_cp(("arbitrary", "arbitrary")),
    )(df, up, w_down_g, w_up_g, dz2, x2, mix, modv, ln1_g, ln1_b)


def _mix_bwd(dmix, proj, y_a, y_b, hs, w_out_g, w_o_lru_g, w_o_sgu_g, seq, after=()):
    t, d = dmix.shape
    w = hs.shape[1]
    nq, d_sgu, ns = w_o_sgu_g.shape
    tm = min(TM_MIX, seq)

    def body(dmix_ref, ga_ref, gb_ref, ya_ref, yb_ref, gl_ref, hs_ref, wo_ref, wl_ref, ws_ref,
             dya_ref, dyb_ref, dga_ref, dgb_ref, dgl_ref, dyl_ref, dys_ref):
        dmerged = _mm_nt(dmix_ref[...], wo_ref[...])
        sa, sb = _sigmoid_t(ga_ref[...]), _sigmoid_t(gb_ref[...])
        dy_a = (dmerged * sa).astype(BF16)
        dy_b = (dmerged * sb).astype(BF16)
        dya_ref[...] = dy_a
        dyb_ref[...] = dy_b
        dga_ref[...] = (dmerged * ya_ref[...].astype(F32) * (sa * (1.0 - sa))).astype(BF16)
        dgb_ref[...] = (dmerged * yb_ref[...].astype(F32) * (sb * (1.0 - sb))).astype(BF16)
        dyap = _mm_nt(dy_a, wl_ref[...])
        gel, dgel = _gelu_and_grad(gl_ref[...])
        dyl_ref[...] = dyap * gel
        dgl_ref[...] = (dyap * hs_ref[...] * dgel).astype(BF16)
        dys = _mm_nt(dy_b[:, 0:ns], ws_ref[0])
        for q in range(1, nq):
            dys = dys + _mm_nt(dy_b[:, q * ns:(q + 1) * ns], ws_ref[q])
        dys_ref[...] = dys

    row = lambda width, col: pl.BlockSpec((tm, width), functools.partial(lambda i, k: (i, k), k=col))
    full = lambda shape: pl.BlockSpec(shape, lambda i: (0,) * len(shape))
    return pl.pallas_call(
        _ordered(body, 10, after), name="mix_bwd", grid=(t // tm,),
        in_specs=[row(d, 0), row(d, 4), row(d, 5), row(d, 0), row(d, 0), row(w, 1), row(w, 0),
                  full(w_out_g.shape), full(w_o_lru_g.shape), full(w_o_sgu_g.shape)] + [_ANY] * len(after),
        out_specs=[row(d, 0), row(d, 0), row(d, 0), row(d, 0), row(w, 0), row(w, 0), row(d_sgu, 0)],
        out_shape=[jax.ShapeDtypeStruct((t, d), BF16), jax.ShapeDtypeStruct((t, d), BF16),
                   jax.ShapeDtypeStruct((t, d), BF16), jax.ShapeDtypeStruct((t, d), BF16),
                   jax.ShapeDtypeStruct((t, w), BF16), jax.ShapeDtypeStruct((t, w), F32),
                   jax.ShapeDtypeStruct((t, d_sgu), F32)],
        compiler_params=_cp(("arbitrary",)),
    )(dmix, proj, proj, y_a, y_b, proj, hs, w_out_g, w_o_lru_g, w_o_sgu_g, *after)


def _sgu_bwd(proj, dys, w_sp, b_sp_t, ln_g, ln_b, after=()):
    t = proj.shape[0]
    d_sgu = SGU_GROUPS * HEAD
    tm = min(TM_SGU, t)
    nblk = tm // HEAD
    specs, n_piece = _sgu_specs(tm, d_sgu)

    def body(*refs):
        u = jnp.concatenate([r[...] for r in refs[:n_piece]], axis=1)
        v = jnp.concatenate([r[...] for r in refs[n_piece:2 * n_piece]], axis=1)
        dys_ref, w_ref, bt_ref, g_ref, b_ref, du_ref, dv_ref, dw_ref, st_ref, dbt_ref, dvn_s = refs[2 * n_piece:]

        @pl.when(pl.program_id(0) == 0)
        def _():
            dw_ref[...] = jnp.zeros_like(dw_ref)
            st_ref[...] = jnp.zeros_like(st_ref)
            dbt_ref[...] = jnp.zeros_like(dbt_ref)

        ug, dug_du = _gelu_and_grad(u)
        vg, dvg_dv = _gelu_and_grad(v)
        xhat, rstd = _ln_stats(vg)
        vn = (xhat * g_ref[...] + b_ref[...]).astype(BF16)
        dys_v = dys_ref[...]
        mask = _sgu_mask()
        for g in range(SGU_GROUPS):
            wm = jnp.where(mask, w_ref[g], 0.0).astype(BF16)
            cols = slice(g * HEAD, (g + 1) * HEAD)
            dw_g = jnp.zeros((HEAD, HEAD), F32)
            db_g = jnp.zeros((HEAD, 1), F32)
            for n in range(nblk):
                rows = slice(n * HEAD, (n + 1) * HEAD)
                vn_blk = vn[rows, cols]
                mixed = jnp.dot(wm, vn_blk, preferred_element_type=F32) + bt_ref[:, g:g + 1]
                dy_blk = dys_v[rows, cols]
                du_ref[rows, cols] = (dy_blk * mixed * dug_du[rows, cols]).astype(BF16)
                dmx = dy_blk * ug[rows, cols]
                dvn_s[rows, cols] = _mm_tn(wm, dmx)
                dw_g = dw_g + _mm_nt(dmx, vn_blk)
                db_g = db_g + jnp.sum(dmx, axis=1, keepdims=True)
            dw_ref[g] += jnp.where(mask, dw_g, 0.0)
            dbt_ref[:, g:g + 1] += db_g
        dvn = dvn_s[...]
        st_ref[0:1, :] += _colsum(dvn * xhat)
        st_ref[1:2, :] += _colsum(dvn)
        dv_ref[...] = (_ln_bwd(dvn * g_ref[...], xhat, rstd) * dvg_dv).astype(BF16)

    full = lambda shape: pl.BlockSpec(shape, lambda i: (0,) * len(shape))
    tok = pl.BlockSpec((tm, d_sgu), lambda i: (i, 0))
    return pl.pallas_call(
        _ordered(body, 2 * n_piece + 5, after), name="sgu_bwd", grid=(t // tm,),
        in_specs=specs + [tok, full(w_sp.shape), full(b_sp_t.shape), full(ln_g.shape), full(ln_b.shape)]
        + [_ANY] * len(after),
        out_specs=[tok, tok, full(w_sp.shape), full((8, d_sgu)), full((HEAD, HEAD))],
        out_shape=[jax.ShapeDtypeStruct((t, d_sgu), BF16), jax.ShapeDtypeStruct((t, d_sgu), BF16),
                   jax.ShapeDtypeStruct(w_sp.shape, F32), jax.ShapeDtypeStruct((8, d_sgu), F32),
                   jax.ShapeDtypeStruct((HEAD, HEAD), F32)],
        scratch_shapes=[pltpu.VMEM((tm, d_sgu), F32)],
        compiler_params=_cp(("arbitrary",)),
    )(*([proj] * (2 * n_piece)), dys, w_sp, b_sp_t, ln_g, ln_b, *after)


def _lru_bwd(proj, hs, e, dyl, lru_w, nb, seq, after=()):
    t = proj.shape[0]
    w = LRU_HEADS * HEAD
    w_conv, b_conv, w_a, b_a, w_x, b_x, lam = lru_w

    def body(x_ref, hs_ref, e_ref, dy_ref, wc_ref, bc_ref, wa_ref, ba_ref, wx_ref, bx_ref, lam_ref,
             dxl_ref, dwa_ref, dwx_ref, st_ref):
        @pl.when(pl.program_id(1) == 0)
        def _():
            dwa_ref[...] = jnp.zeros_like(dwa_ref)
            dwx_ref[...] = jnp.zeros_like(dwx_ref)
            st_ref[...] = jnp.zeros_like(st_ref)

        xl = x_ref[...]
        xc, r, gi, big_l, a, m2 = _lru_gates(xl, wc_ref, bc_ref, wa_ref, ba_ref, wx_ref, bx_ref, lam_ref)
        inv_mult = lax.rsqrt(m2)
        mult = m2 * inv_mult
        dh = dy_ref[...] + _shift_up(e_ref[...], 1)
        da = dh * _shift_down(hs_ref[...], 1)
        dmult = dh * (gi * xc)
        d_i = dh * (mult * xc)
        dxc = dh * (mult * gi)
        dla = a * (da - dmult * (a * inv_mult))
        dr = dla * big_l
        d_big_l = _colsum(dla * r)
        dra = dr * (r * (1.0 - r))
        dia = d_i * (gi * (1.0 - gi))
        dwa_ref[...] += _mm_tn(xc, dra)
        dwx_ref[...] += _mm_tn(xc, dia)
        dxc = dxc + _mm_nt(dra, wa_ref[...]) + _mm_nt(dia, wx_ref[...])
        dxl = wc_ref[CONV_WIDTH - 1:CONV_WIDTH, :] * dxc
        st_ref[4 + CONV_WIDTH - 1:4 + CONV_WIDTH, :] += _colsum(dxc * xl)
        for k in range(CONV_WIDTH - 1):
            ahead = _shift_up(dxc, CONV_WIDTH - 1 - k)
            dxl = dxl + wc_ref[k:k + 1, :] * ahead
            st_ref[4 + k:5 + k, :] += _colsum(ahead * xl)
        dxl_ref[...] = dxl.astype(BF16)
        st_ref[0:1, :] += _colsum(dra)
        st_ref[1:2, :] += _colsum(dia)
        st_ref[2:3, :] += d_big_l * (LRU_C * _sigmoid(-lam_ref[...]))
        st_ref[3:4, :] += _colsum(dxc)

    col = lambda hd, b: (0, hd)
    head = lambda hd, b: (hd, 0, 0)
    tok = lambda hd, b: (b, hd)
    seq_blk = pl.BlockSpec((seq, HEAD), tok)
    return pl.pallas_call(
        _ordered(body, 11, after), name="lru_bwd", grid=(LRU_HEADS, nb),
        in_specs=[seq_blk, seq_blk, seq_blk, seq_blk,
                  pl.BlockSpec((CONV_WIDTH, HEAD), col), pl.BlockSpec((1, HEAD), col),
                  pl.BlockSpec((None, HEAD, HEAD), head), pl.BlockSpec((1, HEAD), col),
                  pl.BlockSpec((None, HEAD, HEAD), head), pl.BlockSpec((1, HEAD), col),
                  pl.BlockSpec((1, HEAD), col)] + [_ANY] * len(after),
        out_specs=[seq_blk, pl.BlockSpec((None, HEAD, HEAD), head), pl.BlockSpec((None, HEAD, HEAD), head),
                   pl.BlockSpec((8, HEAD), col)],
        out_shape=[jax.ShapeDtypeStruct((t, w), BF16), jax.ShapeDtypeStruct((LRU_HEADS, HEAD, HEAD), F32),
                   jax.ShapeDtypeStruct((LRU_HEADS, HEAD, HEAD), F32), jax.ShapeDtypeStruct((8, w), F32)],
        compiler_params=_cp(("arbitrary", "arbitrary")),
    )(proj, hs, e, dyl, w_conv, b_conv, w_a, b_a, w_x, b_x, lam, *after)


def _weight_grad(a, g, col_shards, name, after=()):
    t, k = a.shape
    n = g.shape[1]
    tt = min(TT_DW, t)
    tk = k if k <= 1536 else 1024
    ns = n // N_CHIPS if col_shards else n
    narrow = col_shards and ns < 512
    tn = n if narrow else min(ns, 768 if ns % 768 == 0 else 1024)
    while ns % tn and not narrow:
        tn //= 2
    per = max(ns // tn, 1)

    def body(a_ref, g_ref, o_ref):
        @pl.when(pl.program_id(2) == 0)
        def _():
            o_ref[...] = jnp.zeros_like(o_ref)

        res = _mm_tn(a_ref[...], g_ref[...])
        if narrow:
            for q in range(N_CHIPS):
                o_ref[q] += res[:, q * ns:(q + 1) * ns]
        else:
            o_ref[...] += res

    if narrow:
        out_spec = pl.BlockSpec((N_CHIPS, tk, ns), lambda i, j, s: (0, i, 0))
        out_shape = jax.ShapeDtypeStruct((N_CHIPS, k, ns), F32)
    elif col_shards:
        out_spec = pl.BlockSpec((None, tk, tn), lambda i, j, s: (j // per, i, j % per))
        out_shape = jax.ShapeDtypeStruct((N_CHIPS, k, ns), F32)
    else:
        out_spec = pl.BlockSpec((tk, tn), lambda i, j, s: (i, j))
        out_shape = jax.ShapeDtypeStruct((k, n), F32)
    return pl.pallas_call(
        _ordered(body, 2, after), name=name, grid=(k // tk, n // tn, t // tt),
        in_specs=[pl.BlockSpec((tt, tk), lambda i, j, s: (s, i)), pl.BlockSpec((tt, tn), lambda i, j, s: (s, j))]
        + [_ANY] * len(after),
        out_specs=out_spec, out_shape=out_shape,
        compiler_params=_cp(("arbitrary", "arbitrary", "arbitrary")),
    )(a, g, *after)


def _input_grad(dproj, w_in_g, dz1, x2, modv, nb, seq, after=()):
    t, d = x2.shape
    nq, _, ns = w_in_g.shape
    tm = min(TM_DH, seq)
    tpb = seq // tm

    def body(dp_ref, w_ref, dz1_ref, x_ref, mod_ref, gx_ref, db_ref, pb_ref, acc):
        i, q = pl.program_id(0), pl.program_id(1)

        @pl.when(q == 0)
        def _():
            acc[...] = jnp.zeros_like(acc)

        @pl.when((i == 0) & (q == 0))
        def _():
            db_ref[...] = jnp.zeros_like(db_ref)

        @pl.when((i % tpb == 0) & (q == 0))
        def _():
            pb_ref[...] = jnp.zeros_like(pb_ref)

        dp = dp_ref[...]
        acc[...] += _mm_nt(dp, w_ref[...])
        db_ref[q, 0:1, :] += _colsum(dp.astype(F32))

        @pl.when(q == nq - 1)
        def _():
            dh = acc[...]
            gx_ref[...] = ALPHA * dz1_ref[...] + dh * (1.0 + mod_ref[1:2, :])
            pb_ref[0:1, :] += _colsum(dh * x_ref[...])
            pb_ref[1:2, :] += _colsum(dh)

    tok = lambda i, q: (i, 0)
    return pl.pallas_call(
        _ordered(body, 5, after), name="input_grad", grid=(t // tm, nq),
        in_specs=[pl.BlockSpec((tm, ns), lambda i, q: (i, q)), pl.BlockSpec((None, d, ns), lambda i, q: (q, 0, 0)),
                  pl.BlockSpec((tm, d), tok), pl.BlockSpec((tm, d), tok),
                  pl.BlockSpec((None, 8, d), lambda i, q: (i // tpb, 0, 0))] + [_ANY] * len(after),
        out_specs=[pl.BlockSpec((tm, d), tok), pl.BlockSpec((nq, 8, ns), lambda i, q: (0, 0, 0)),
                   pl.BlockSpec((None, 8, d), lambda i, q: (i // tpb, 0, 0))],
        out_shape=[jax.ShapeDtypeStruct((t, d), F32), jax.ShapeDtypeStruct((nq, 8, ns), F32),
                   jax.ShapeDtypeStruct((nb, 8, d), F32)],
        scratch_shapes=[pltpu.VMEM((tm, d), F32)],
        compiler_params=_cp(("arbitrary", "arbitrary")),
    )(dproj, w_in_g, dz1, x2, modv, *after)


def _rows128(v):
    flat = v.reshape(-1, HEAD)
    pad = (-flat.shape[0]) % 8
    return jnp.pad(flat, ((0, pad), (0, 0))) if pad else flat


def kernel(x, c, w_ada, b_ada, w_in, b_in, w_conv, b_conv, w_rg_a, b_rg_a, w_rg_x, b_rg_x, lru_lambda, w_sp, b_sp, ln_v_g, ln_v_b, w_o_lru, w_o_sgu, w_out, ln1_g, ln1_b, w_up, w_down, ln2_g, ln2_b, loss_target, m_w_ada, m_b_ada, m_w_in, m_b_in, m_w_conv, m_b_conv, m_w_rg_a, m_b_rg_a, m_w_rg_x, m_b_rg_x, m_lru_lambda, m_w_sp, m_b_sp, m_ln_v_g, m_ln_v_b, m_w_o_lru, m_w_o_sgu, m_w_out, m_ln1_g, m_ln1_b, m_w_up, m_w_down, m_ln2_g, m_ln2_b, v_w_ada, v_b_ada, v_w_in, v_b_in, v_w_conv, v_b_conv, v_w_rg_a, v_b_rg_a, v_w_rg_x, v_b_rg_x, v_lru_lambda, v_w_sp, v_b_sp, v_ln_v_g, v_ln_v_b, v_w_o_lru, v_w_o_sgu, v_w_out, v_ln1_g, v_ln1_b, v_w_up, v_w_down, v_ln2_g, v_ln2_b):
    given = dict(locals())
    nb, seq, d = x.shape
    t = nb * seq
    w_lru = LRU_HEADS * HEAD
    d_sgu = SGU_GROUPS * HEAD
    xi, yi, ci = lax.axis_index("x"), lax.axis_index("y"), lax.axis_index("c")
    chip = 2 * xi + yi
    dev = 2 * chip + ci
    cidx = jnp.reshape(ci, (1,)).astype(jnp.int32)

    x2 = x.reshape(t, d)
    target = loss_target.reshape(t, d)

    big = ["w_in", "w_o_lru", "w_o_sgu", "w_out", "w_up", "w_down"]
    shards_a = [w_in[0].astype(BF16)]
    shards_b = [given[n][0].astype(BF16) for n in big[1:]]
    pidx = jnp.reshape(chip, (1,)).astype(jnp.int32)

    c_rows = _rows128(c)
    wconv_rows = _rows128(w_conv[0])
    slab0 = _all_gather_small(jnp.concatenate([c_rows, wconv_rows], axis=0), "gather_c_wconv")
    slab0 = slab0.reshape(N_DEV, -1, HEAD)
    c_all = slab0[:, :c_rows.shape[0]].reshape(N_DEV * nb, d)
    n_wc = CONV_WIDTH * (w_lru // N_CHIPS) // HEAD
    wc = slab0[0::2, c_rows.shape[0]:c_rows.shape[0] + n_wc].reshape(N_CHIPS, CONV_WIDTH, w_lru // N_CHIPS)
    w_conv_full = jnp.transpose(wc, (1, 0, 2)).reshape(CONV_WIDTH, w_lru)

    n_ada = w_ada.shape[2]
    b_ada_cols = lax.dynamic_slice(b_ada, (0, chip * n_ada), (1, n_ada))
    mod_cols = _ada_fwd(c_all, w_ada[0], b_ada_cols)
    half = (N_DEV * nb) // 2
    mod_half = lax.dynamic_slice(mod_cols, (ci * half, 0), (half, n_ada))
    mod_g = _all_gather_small(mod_half, "gather_mod").reshape(N_CHIPS, 2, half, n_ada)
    mod_all = jnp.transpose(mod_g, (1, 2, 0, 3)).reshape(N_DEV * nb, N_CHIPS * n_ada)
    mod_loc = lax.dynamic_slice(mod_all, (dev * nb, 0), (nb, N_CHIPS * n_ada)).reshape(nb, 6, d)
    modv = jnp.pad(mod_loc, ((0, 0), (0, 2), (0, 0)))

    lru_w = (w_conv_full, b_conv, w_rg_a[0], b_rg_a, w_rg_x[0], b_rg_x, lru_lambda)
    b_sp_t = jnp.transpose(b_sp[0])

    land = lambda s: jax.ShapeDtypeStruct((N_CHIPS,) + s.shape, s.dtype)
    sds = lambda s: jax.ShapeDtypeStruct(s.shape, s.dtype)
    started_a = _split_start(shards_a, [sds(shards_a[0])] * 3, _peer_gather_copies((0, 1, 2)), 3, "gather_w_in_start",
                             after=(modv,))
    shards_b, shards_c = shards_b[:3], shards_b[3:]
    started_b = _split_start(shards_b, [land(s) for s in shards_b], _gather_copies, 3 * len(shards_b),
                             "gather_w_mix_start", after=(started_a[-1],))
    started_c = _split_start(shards_c, [land(s) for s in shards_c], _gather_copies, 3 * len(shards_c),
                             "gather_w_mlp_start", after=(started_b[-1],))

    ids = lambda *v: jnp.stack(v).astype(jnp.int32)
    modv_t = modv + started_c[-1][0:1, 0:1]
    proj, h = _proj_fwd(x2, modv_t, [started_a[2]], ids(chip), b_in, seq, "proj_fwd_own")
    own_a, lands_a = _split_wait(started_a, 1, _peer_gather_copies((0, 1)), "gather_w_in_wait_near", after=(proj,))
    (proj,) = _proj_fwd(x2, modv, lands_a[:2], ids(chip ^ 1, chip ^ 2), b_in, seq, "proj_fwd_near", proj_in=proj)
    own_a, lands_a = _split_wait((started_a[0], started_a[1], *own_a, *lands_a, started_a[-1]), 1,
                                 _peer_gather_copies((2,)), "gather_w_in_wait_far", after=(proj,))
    (proj,) = _proj_fwd(x2, modv, lands_a[2:], ids(chip ^ 3), b_in, seq, "proj_fwd_far", proj_in=proj)
    w_in_g = lax.empty((N_CHIPS,) + shards_a[0].shape, BF16)
    for k, (shard, slot) in enumerate(zip(own_a + lands_a, (chip, chip ^ 1, chip ^ 2, chip ^ 3))):
        (w_in_g,) = _fill_own_slot([w_in_g], [shard], ids(slot), ["place_w_in_%d" % k])
    a, inp = _lru_prep(proj, lru_w, nb, seq)
    a3 = a.reshape(nb, seq, w_lru)
    hs = _scan(a3, inp.reshape(nb, seq, w_lru), False, "lru_scan").reshape(t, w_lru)
    y_sgu = _sgu_fwd(proj, w_sp[0], b_sp_t, ln_v_g, ln_v_b)
    shards_b, lands_b = _split_wait(started_b, len(shards_b), _gather_copies, "gather_w_mix_wait", after=(hs, y_sgu))
    w_o_lru_g, w_o_sgu_g, w_out_g = _fill_own_slot(lands_b, shards_b, pidx, ["own_" + n for n in big[1:4]])
    w_o_lru_g = w_o_lru_g.reshape(w_lru, d)
    w_out_g = w_out_g.reshape(d, d)
    yap, y_a, y_b, merged, mix, x1 = _mix_fwd(hs, proj, y_sgu, x2, modv, w_o_lru_g, w_o_sgu_g, w_out_g, ln1_g, ln1_b, seq)
    shards_c, lands_c = _split_wait(started_c, len(shards_c), _gather_copies, "gather_w_mlp_wait", after=(x1,))
    w_up_g, w_down_g = _fill_own_slot(lands_c, shards_c, pidx, ["own_" + n for n in big[4:]])
    w_down_g = w_down_g.reshape(-1, d)
    up, act, h2, dz2, df, st2, pb2 = _mlp_fwd(x1, modv, w_up_g, w_down_g, ln2_g, ln2_b, target, nb, seq)
    loss = lax.psum(st2[2, 0], ("x", "y", "c"))

    part = {}

    def to_sibling_start(group, tag, after=()):
        g4 = []
        for n in group:
            shard = given[n].shape[1:]
            g4.append(part[n].reshape(N_CHIPS, 2, shard[0] // 2, shard[1]))
        shapes = [jax.ShapeDtypeStruct((N_CHIPS,) + g.shape[2:], F32) for g in g4]
        return _split_start(g4, shapes, _to_sibling_copies, len(g4), "grads_to_sibling_start_" + tag, after)

    def to_chips_start(group, started, tag, after=()):
        g4, recv = _split_wait(started, len(group), _to_sibling_copies, "grads_to_sibling_wait_" + tag, after)
        own4 = [_add_own_half(g4[k], recv[k], cidx, "grad_pair_sum_" + n) for k, n in enumerate(group)]
        shapes = [jax.ShapeDtypeStruct((3,) + o.shape[1:], BF16) for o in own4]
        return _split_start(own4, shapes, _chip_exchange_copies, 3 * len(own4), "grads_chip_exchange_start_" + tag)

    def chips_finish(group, started, tag, after=()):
        own4, slots = _split_wait(started, len(group), _chip_exchange_copies, "grads_chip_exchange_wait_" + tag, after)
        return [_sum_own_and_peers(own4[k], slots[k], pidx, "grad_chip_sum_" + n) for k, n in enumerate(group)]

    dup, dz1, dmix, st1, pb1 = _mlp_bwd(df, up, w_down_g, w_up_g, dz2, x2, mix, modv, ln1_g, ln1_b, nb, seq)
    group1 = ["w_up", "w_down"]
    part["w_up"] = _weight_grad(h2, dup, True, "grad_w_up")
    part["w_down"] = _weight_grad(act, df, False, "grad_w_down")
    sib1 = to_sibling_start(group1, "mlp")
    dy_a, dy_b, dga, dgb, dgl, dyl, dys = _mix_bwd(dmix, proj, y_a, y_b, hs, w_out_g, w_o_lru_g, w_o_sgu_g, seq,
                                                   after=(sib1[-1],))
    group2 = ["w_o_lru", "w_o_sgu", "w_out"]
    part["w_o_lru"] = _weight_grad(yap, dy_a, False, "grad_w_o_lru")
    part["w_o_sgu"] = _weight_grad(y_sgu, dy_b, True, "grad_w_o_sgu")
    part["w_out"] = _weight_grad(merged, dmix, False, "grad_w_out")
    chips1 = to_chips_start(group1, sib1, "mlp", after=(dys, part["w_o_lru"], part["w_o_sgu"], part["w_out"]))
    sib2 = to_sibling_start(group2, "mix", after=(chips1[-1],))
    du, dv, g_w_sp, st_sgu, g_b_sp_t = _sgu_bwd(proj, dys, w_sp[0], b_sp_t, ln_v_g, ln_v_b, after=(sib2[-1],))
    dyl3 = dyl.reshape(nb, seq, w_lru)
    e = _scan(a3, dyl3, True, "lru_scan_bwd").reshape(t, w_lru)
    chips2 = to_chips_start(group2, sib2, "mix", after=(e, du))
    dxl, g_w_rg_a, g_w_rg_x, st_lru = _lru_bwd(proj, hs, e, dyl, lru_w, nb, seq, after=(chips2[-1],))
    dproj = jnp.concatenate([dxl, dgl, du, dv, dga, dgb], axis=1)

    didx = jnp.reshape(dev, (1,)).astype(jnp.int32)
    early = [
        ("w_conv", st_lru[4:8]), ("b_conv", st_lru[3]), ("w_rg_a", g_w_rg_a), ("b_rg_a", st_lru[0]),
        ("w_rg_x", g_w_rg_x), ("b_rg_x", st_lru[1]), ("lru_lambda", st_lru[2]), ("w_sp", g_w_sp),
        ("b_sp", jnp.transpose(g_b_sp_t[:, :SGU_GROUPS])), ("ln_v_g", st_sgu[0]), ("ln_v_b", st_sgu[1]),
        ("ln1_g", st1[0]), ("ln1_b", st1[1]), ("ln2_g", st2[0]), ("ln2_b", st2[1]),
    ]
    pieces_e = [_rows128(v) for _, v in early]
    slab_e = jnp.concatenate(pieces_e, axis=0)
    slab_e = jnp.pad(slab_e, ((0, (-slab_e.shape[0]) % TR_EW), (0, 0)))
    small_st = _split_start([slab_e], [jax.ShapeDtypeStruct((N_DEV,) + slab_e.shape, F32)], _all_devices_copies, N_DEV - 1,
                            "small_grads_start")

    group3 = ["w_in"]
    part["w_in"] = _weight_grad(h, dproj, True, "grad_w_in", after=(small_st[-1],))
    sib3 = to_sibling_start(group3, "in")
    chips3 = to_chips_start(group3, sib3, "in")
    grad_x2, g_b_in4, pb0 = _input_grad(dproj, w_in_g, dz1, x2, modv, nb, seq, after=(chips3[-1],))
    halves12 = chips_finish(group1, chips1, "mlp", after=(grad_x2,)) + chips_finish(group2, chips2, "mix", after=(grad_x2,))
    swap12 = _split_start(halves12, [jax.ShapeDtypeStruct(hv.shape, F32) for hv in halves12], _swap_copies, len(halves12),
                          "grads_swap_start")
    grads = {}

    dmod_loc = jnp.stack([pb0[:, 1], pb0[:, 0], pb1[:, 2], pb1[:, 1], pb1[:, 0], pb2[:, 0]], axis=1)
    late = [("dmod", dmod_loc), ("b_in", g_b_in4[:, 0])]
    pieces_l = [_rows128(v) for _, v in late]
    slab_l = jnp.concatenate(pieces_l, axis=0)
    gathered = _all_gather_small(slab_l, "gather_small_grads", after=(swap12[-1],)).reshape(N_DEV, slab_l.shape[0], HEAD)
    rows_dmod = dmod_loc.size // HEAD
    dmod_all = gathered[:, :rows_dmod].reshape(N_DEV * nb, 6 * d)
    grads["b_in"] = _sum_slots(gathered[:, rows_dmod:], "grad_b_in_sum").reshape(1, -1)

    (slab_e,), (lands_e,) = _split_wait(small_st, 1, _all_devices_copies, "small_grads_wait", after=(gathered,))
    summed = _sum_devices(lands_e, slab_e, didx, "small_grad_sum")
    off = 0
    for (n, v), piece in zip(early, pieces_e):
        grads[n] = summed[off:off + v.size // HEAD].reshape(v.shape)
        off += piece.shape[0]

    mine12, theirs12 = _split_wait(swap12, len(halves12), _swap_copies, "grads_swap_wait", after=(summed,))
    (mine3,) = chips_finish(group3, chips3, "in", after=(summed,))
    (theirs3,) = _exchange([mine3], [jax.ShapeDtypeStruct(mine3.shape, F32)], _swap_copies, 1, "grads_swap_w_in")
    mine = dict(zip(group1 + group2 + group3, mine12 + [mine3]))
    theirs = dict(zip(group1 + group2 + group3, theirs12 + [theirs3]))

    dmod_cols = lax.dynamic_slice(dmod_all, (0, chip * n_ada), (N_DEV * nb, n_ada))
    grads["w_ada"], grads["b_ada"] = _ada_bwd(c_all, dmod_all, dmod_cols)
    n_wcs = w_lru // N_CHIPS
    grads["w_conv"] = lax.dynamic_slice(grads["w_conv"], (0, chip * n_wcs), (CONV_WIDTH, n_wcs))

    names = ['w_ada', 'b_ada', 'w_in', 'b_in', 'w_conv', 'b_conv', 'w_rg_a', 'b_rg_a', 'w_rg_x', 'b_rg_x', 'lru_lambda',
             'w_sp', 'b_sp', 'ln_v_g', 'ln_v_b', 'w_o_lru', 'w_o_sgu', 'w_out', 'ln1_g', 'ln1_b', 'w_up', 'w_down',
             'ln2_g', 'ln2_b']
    out_g, out_d, out_m, out_v = [], [], [], []
    for n in names:
        wv = given[n]
        shape2 = (-1, wv.shape[-1])
        w2, m2, v2 = wv.reshape(shape2), given["m_" + n].reshape(shape2), given["v_" + n].reshape(shape2)
        if n in big:
            g2, dlt, nm, nv = _adamw_halves(w2, mine[n], theirs[n], m2, v2, cidx, "adamw_" + n)
        else:
            g2 = grads[n].reshape(wv.shape).reshape(shape2)
            dlt, nm, nv = _adamw(w2, g2, m2, v2, "adamw_" + n)
        out_g.append(g2.reshape(wv.shape))
        out_d.append(dlt.reshape(wv.shape))
        out_m.append(nm.reshape(wv.shape))
        out_v.append(nv.reshape(wv.shape))

    return (loss, grad_x2.reshape(nb, seq, d), *out_g, *out_d, *out_m, *out_v)
```

```python
import functools
import math

import jax
import jax.numpy as jnp
from jax import lax
from jax.experimental import pallas as pl
from jax.experimental.pallas import tpu as pltpu

F32 = jnp.float32
BF16 = jnp.bfloat16
MESH = pl.DeviceIdType.MESH

N_CHIPS = 4
N_DEV = 8
LRU_HEADS = 10
HEAD = 128
SGU_GROUPS = 6
SGU_CHUNK = 64
CONV_WIDTH = 4
LRU_C = 8.0
ALPHA = 2.0 ** 0.25
LN_EPS = 1e-5
ADAM_LR, ADAM_B1, ADAM_B2, ADAM_EPS, ADAM_WD, ADAM_STEP = 0.001, 0.9, 0.999, 1e-08, 0.01, 10

VMEM_LIMIT = 56 * 1024 * 1024
TM_PROJ = 1024
TM_MIX = 256
TM_MLP = 512
TM_SGU = 512
TM_DH = 512
TT_DW = 1024
TC_SCAN = 256
TR_EW = 256


def _cp(sem=None):
    return pltpu.CompilerParams(dimension_semantics=sem, vmem_limit_bytes=VMEM_LIMIT)


def _mm(a, b):
    return jnp.dot(a.astype(BF16), b.astype(BF16), preferred_element_type=F32)


def _mm_nt(a, b):
    return lax.dot_general(a.astype(BF16), b.astype(BF16), (((1,), (1,)), ((), ())), preferred_element_type=F32)


def _mm_tn(a, b):
    return lax.dot_general(a.astype(BF16), b.astype(BF16), (((0,), (0,)), ((), ())), preferred_element_type=F32)


def _sigmoid(x):
    return 1.0 / (1.0 + jnp.exp(-x))


def _sigmoid_t(x):
    return 0.5 * jnp.tanh(0.5 * x) + 0.5


_GELU_K = math.sqrt(2.0 / math.pi)


def _gelu(x):
    t = jnp.tanh(_GELU_K * (x + 0.044715 * (x * x * x)))
    return 0.5 * x * (1.0 + t)


def _gelu_and_grad(x):
    x2 = x * x
    t = jnp.tanh(_GELU_K * (x + 0.044715 * (x2 * x)))
    g = 0.5 * x * (1.0 + t)
    dg = 0.5 * (1.0 + t) + 0.5 * x * (1.0 - t * t) * (_GELU_K * (1.0 + 3.0 * 0.044715 * x2))
    return g, dg


def _ln_stats(z):
    mu = jnp.mean(z, axis=-1, keepdims=True)
    zc = z - mu
    var = jnp.mean(zc * zc, axis=-1, keepdims=True)
    rstd = lax.rsqrt(var + LN_EPS)
    return zc * rstd, rstd


def _ln_bwd(dxh, xhat, rstd):
    m1 = jnp.mean(dxh, axis=-1, keepdims=True)
    m2 = jnp.mean(dxh * xhat, axis=-1, keepdims=True)
    return rstd * (dxh - m1 - xhat * m2)


def _colsum(v):
    return jnp.sum(v, axis=0, keepdims=True)


def _shift_down(v, j):
    if j == 0:
        return v
    rows = lax.broadcasted_iota(jnp.int32, v.shape, 0)
    return jnp.where(rows >= j, pltpu.roll(v, j, 0), 0.0)


def _shift_up(v, j):
    if j == 0:
        return v
    n = v.shape[0]
    rows = lax.broadcasted_iota(jnp.int32, v.shape, 0)
    return jnp.where(rows < n - j, pltpu.roll(v, n - j, 0), 0.0)


def _my_pos():
    return lax.axis_index("x"), lax.axis_index("y"), lax.axis_index("c")


def _all_gather_small(v, name, after=()):
    m_per, n = v.shape

    def body(x_ref, out_ref, send_sems, recv_sems, local_sem):
        x, y, c = _my_pos()
        me, sibling = (x, y, c), (x, y, 1 - c)
        chips = [(1 - x, y), (x, 1 - y), (1 - x, 1 - y)]

        def rows(px, py, pc):
            return out_ref.at[pl.ds((4 * px + 2 * py + pc) * m_per, m_per), :]

        def copy(k, block, to, src=None):
            return pltpu.make_async_remote_copy(
                src_ref=rows(*block) if src is None else src, dst_ref=rows(*block),
                send_sem=send_sems.at[k], recv_sem=recv_sems.at[k], device_id=to, device_id_type=MESH)

        mine = pltpu.make_async_copy(x_ref, rows(*me), local_sem)
        mine.start()
        first = [copy(0, me, sibling, src=x_ref)]
        first += [copy(1 + j, me, (*chip, c), src=x_ref) for j, chip in enumerate(chips)]
        for cp in first:
            cp.start()
        passed = [copy(4 + j, (*chip, c), sibling) for j, chip in enumerate(chips)]
        for j, chip in enumerate(chips):
            copy(1 + j, (*chip, c), me).wait_recv()
            passed[j].start()
        copy(0, sibling, me).wait_recv()
        for j, chip in enumerate(chips):
            copy(4 + j, (*chip, 1 - c), me).wait_recv()
        for cp in first + passed:
            cp.wait_send()
        mine.wait()

    return pl.pallas_call(
        _ordered(body, 1, after), name=name,
        out_shape=jax.ShapeDtypeStruct((N_DEV * m_per, n), v.dtype),
        in_specs=[pl.BlockSpec(memory_space=pltpu.VMEM)] + [pl.BlockSpec(memory_space=pl.ANY)] * len(after),
        out_specs=pl.BlockSpec(memory_space=pltpu.VMEM),
        scratch_shapes=[pltpu.SemaphoreType.DMA((7,)), pltpu.SemaphoreType.DMA((7,)), pltpu.SemaphoreType.DMA],
        compiler_params=pltpu.CompilerParams(vmem_limit_bytes=VMEM_LIMIT),
    )(v, *after)


_HBM = pl.BlockSpec(memory_space=pltpu.HBM)
_ANY = pl.BlockSpec(memory_space=pl.ANY)
_SEM = pl.BlockSpec(memory_space=pltpu.SEMAPHORE)
_EFFECT = pltpu.SideEffectType.DATAFLOW_SIDE_EFFECTING


def _ordered(body, n_in, after):
    k = len(after)
    if not k:
        return body
    return lambda *refs: body(*refs[:n_in], *refs[n_in + k:])


def _gather_copies(ins, lands, send_sems, recv_sems):
    x, y, c = _my_pos()
    p = 2 * x + y
    peers = [(x, 1 - y), (1 - x, y), (1 - x, 1 - y)]
    sends, recvs = [], []
    for k in range(len(ins)):
        for j, (qx, qy) in enumerate(peers):
            sems = dict(send_sem=send_sems.at[3 * k + j], recv_sem=recv_sems.at[3 * k + j],
                        device_id=(qx, qy, c), device_id_type=MESH)
            sends.append(pltpu.make_async_remote_copy(src_ref=ins[k], dst_ref=lands[k].at[p], **sems))
            recvs.append(pltpu.make_async_remote_copy(src_ref=ins[k], dst_ref=lands[k].at[2 * qx + qy], **sems))
    return sends, recvs


def _peer_gather_copies(peers):
    def copies(ins, lands, send_sems, recv_sems):
        x, y, c = _my_pos()
        where = [(x, 1 - y), (1 - x, y), (1 - x, 1 - y)]
        cps = [pltpu.make_async_remote_copy(
            src_ref=ins[0], dst_ref=lands[j], send_sem=send_sems.at[j], recv_sem=recv_sems.at[j],
            device_id=(*where[j], c), device_id_type=MESH) for j in peers]
        return cps, cps
    return copies


def _to_sibling_copies(ins, lands, send_sems, recv_sems):
    x, y, c = _my_pos()
    cps = [pltpu.make_async_remote_copy(
        src_ref=ins[k].at[:, 1 - c], dst_ref=lands[k], send_sem=send_sems.at[k], recv_sem=recv_sems.at[k],
        device_id=(x, y, 1 - c), device_id_type=MESH) for k in range(len(ins))]
    return cps, cps


def _chip_exchange_copies(ins, lands, send_sems, recv_sems):
    x, y, c = _my_pos()
    peers = [(x, 1 - y), (1 - x, y), (1 - x, 1 - y)]
    cps = []
    for k in range(len(ins)):
        for j, (qx, qy) in enumerate(peers):
            cps.append(pltpu.make_async_remote_copy(
                src_ref=ins[k].at[2 * qx + qy], dst_ref=lands[k].at[j], send_sem=send_sems.at[3 * k + j],
                recv_sem=recv_sems.at[3 * k + j], device_id=(qx, qy, c), device_id_type=MESH))
    return cps, cps


def _all_devices_copies(ins, lands, send_sems, recv_sems):
    x, y, c = _my_pos()
    me = 4 * x + 2 * y + c
    sends, recvs = [], []
    for r in range(1, N_DEV):
        px = 1 - x if r & 4 else x
        py = 1 - y if r & 2 else y
        pc = 1 - c if r & 1 else c
        sems = dict(send_sem=send_sems.at[r - 1], recv_sem=recv_sems.at[r - 1], device_id=(px, py, pc), device_id_type=MESH)
        sends.append(pltpu.make_async_remote_copy(src_ref=ins[0], dst_ref=lands[0].at[me], **sems))
        recvs.append(pltpu.make_async_remote_copy(src_ref=ins[0], dst_ref=lands[0].at[4 * px + 2 * py + pc], **sems))
    return sends, recvs


def _swap_copies(ins, lands, send_sems, recv_sems):
    x, y, c = _my_pos()
    cps = [pltpu.make_async_remote_copy(
        src_ref=ins[k], dst_ref=lands[k], send_sem=send_sems.at[k], recv_sem=recv_sems.at[k],
        device_id=(x, y, 1 - c), device_id_type=MESH) for k in range(len(ins))]
    return cps, cps


def _split_start(ins, land_shapes, copies, n_sems, name, after=()):
    n, nl = len(ins), len(land_shapes)
    first_out = n + nl + len(after)

    def body(*refs):
        in_refs, land_refs = refs[:n], refs[n:n + nl]
        send_sems, recv_sems = refs[first_out:first_out + 2]
        token = refs[-1]
        sends, _ = copies(in_refs, land_refs, send_sems, recv_sems)
        for cp in sends:
            cp.start()
        token[...] = jnp.zeros_like(token)

    lands = [pltpu.with_memory_space_constraint(lax.empty(s.shape, s.dtype), pltpu.HBM) for s in land_shapes]
    ins = [pltpu.with_memory_space_constraint(s, pltpu.HBM) for s in ins]
    return pl.pallas_call(
        body, name=name,
        out_shape=(pltpu.SemaphoreType.DMA((n_sems,)), pltpu.SemaphoreType.DMA((n_sems,)),
                   *[pltpu.HBM(s.shape, s.dtype) for s in ins], *[pltpu.HBM(s.shape, s.dtype) for s in lands],
                   jax.ShapeDtypeStruct((8, HEAD), F32)),
        in_specs=[_HBM] * (n + nl) + [pl.BlockSpec(memory_space=pl.ANY)] * len(after),
        out_specs=(_SEM, _SEM, *([_HBM] * (n + nl)), pl.BlockSpec(memory_space=pltpu.VMEM)),
        input_output_aliases={k: 2 + k for k in range(n + nl)},
        compiler_params=pltpu.CompilerParams(has_side_effects=_EFFECT),
    )(*ins, *lands, *after)


def _split_wait(started, n, copies, name, after=()):
    send_sems, recv_sems = started[0], started[1]
    bufs = started[2:-1]
    nb = len(bufs)

    def body(*refs):
        in_refs, land_refs = refs[:n], refs[n:nb]
        sends, recvs = copies(in_refs, land_refs, refs[nb], refs[nb + 1])
        for cp in sends:
            cp.wait_send()
        for cp in recvs:
            cp.wait_recv()

    outs = pl.pallas_call(
        body, name=name,
        out_shape=tuple(pltpu.HBM(s.shape, s.dtype) for s in bufs),
        in_specs=[_HBM] * nb + [_SEM, _SEM] + [pl.BlockSpec(memory_space=pl.ANY)] * len(after),
        out_specs=tuple([_HBM] * nb),
        input_output_aliases={k: k for k in range(nb)},
        compiler_params=pltpu.CompilerParams(has_side_effects=_EFFECT),
    )(*bufs, send_sems, recv_sems, *after)
    return list(outs[:n]), list(outs[n:])


def _fill_own_slot(gathered, shards, pidx, names):
    outs = []
    for g, s, name in zip(gathered, shards, names):
        r, cdim = s.shape
        tr = _row_tile(r)

        def body(p_ref, s_ref, g_ref, o_ref):
            o_ref[...] = s_ref[...]

        outs.append(pl.pallas_call(
            body, name=name,
            grid_spec=pltpu.PrefetchScalarGridSpec(
                num_scalar_prefetch=1, grid=(r // tr,),
                in_specs=[pl.BlockSpec((tr, cdim), lambda i, p: (i, 0)), pl.BlockSpec(memory_space=pl.ANY)],
                out_specs=pl.BlockSpec((None, tr, cdim), lambda i, p: (p[0], i, 0))),
            out_shape=jax.ShapeDtypeStruct(g.shape, g.dtype),
            input_output_aliases={2: 0},
            compiler_params=_cp(("arbitrary",)),
        )(pidx, s, g))
    return outs


def _sum_own_and_peers(own4, slots, pidx, name):
    _, rh, cdim = own4.shape
    tr = _row_tile(rh)

    def body(p_ref, own_ref, s_ref, o_ref):
        acc = own_ref[...].astype(F32)
        for j in range(3):
            acc = acc + s_ref[j].astype(F32)
        o_ref[...] = acc

    return pl.pallas_call(
        body, name=name,
        grid_spec=pltpu.PrefetchScalarGridSpec(
            num_scalar_prefetch=1, grid=(rh // tr,),
            in_specs=[pl.BlockSpec((None, tr, cdim), lambda i, p: (p[0], i, 0)),
                      pl.BlockSpec((3, tr, cdim), lambda i, p: (0, i, 0))],
            out_specs=pl.BlockSpec((tr, cdim), lambda i, p: (i, 0))),
        out_shape=jax.ShapeDtypeStruct((rh, cdim), F32),
        compiler_params=_cp(("arbitrary",)),
    )(pidx, own4, slots)


def _exchange(ins, land_shapes, copies, n_sems, name):
    n, nl = len(ins), len(land_shapes)

    def body(*refs):
        sends, recvs = copies(refs[:n], refs[n:n + nl], refs[n + nl], refs[n + nl + 1])
        for cp in sends:
            cp.start()
        for cp in sends:
            cp.wait_send()
        for cp in recvs:
            cp.wait_recv()

    any_spec = pl.BlockSpec(memory_space=pl.ANY)
    return pl.pallas_call(
        body, name=name,
        out_shape=[jax.ShapeDtypeStruct(s.shape, s.dtype) for s in land_shapes],
        in_specs=[any_spec] * n, out_specs=[any_spec] * nl,
        scratch_shapes=[pltpu.SemaphoreType.DMA((n_sems,)), pltpu.SemaphoreType.DMA((n_sems,))],
    )(*ins)


def _row_tile(r):
    t = min(TR_EW, r)
    while r % t:
        t //= 2
    return t


def _add_own_half(g4, recv, cidx, name):
    _, _, rh, cdim = g4.shape
    tr = _row_tile(rh)

    def body(c_ref, a_ref, b_ref, o_ref):
        o_ref[...] = (a_ref[...] + b_ref[...]).astype(BF16)

    return pl.pallas_call(
        body, name=name,
        grid_spec=pltpu.PrefetchScalarGridSpec(
            num_scalar_prefetch=1, grid=(N_CHIPS, rh // tr),
            in_specs=[pl.BlockSpec((None, None, tr, cdim), lambda q, i, c: (q, c[0], i, 0)),
                      pl.BlockSpec((None, tr, cdim), lambda q, i, c: (q, i, 0))],
            out_specs=pl.BlockSpec((None, tr, cdim), lambda q, i, c: (q, i, 0))),
        out_shape=jax.ShapeDtypeStruct(recv.shape, BF16),
        compiler_params=_cp(("arbitrary", "arbitrary")),
    )(cidx, g4, recv)


def _sum_slots(v, name):
    n, r, cdim = v.shape
    tr = _row_tile(r)

    def body(v_ref, o_ref):
        acc = v_ref[0].astype(F32)
        for k in range(1, n):
            acc = acc + v_ref[k].astype(F32)
        o_ref[...] = acc

    return pl.pallas_call(
        body, name=name, grid=(r // tr,),
        in_specs=[pl.BlockSpec((n, tr, cdim), lambda i: (0, i, 0))],
        out_specs=pl.BlockSpec((tr, cdim), lambda i: (i, 0)),
        out_shape=jax.ShapeDtypeStruct((r, cdim), F32),
        compiler_params=_cp(("arbitrary",)),
    )(v)


def _sum_devices(lands, own, didx, name):
    _, r, cdim = lands.shape
    tr = _row_tile(r)

    def body(d_ref, l_ref, own_ref, o_ref):
        acc = jnp.where(d_ref[0] == 0, own_ref[...], l_ref[0])
        for dv in range(1, N_DEV):
            acc = acc + jnp.where(d_ref[0] == dv, own_ref[...], l_ref[dv])
        o_ref[...] = acc

    return pl.pallas_call(
        body, name=name,
        grid_spec=pltpu.PrefetchScalarGridSpec(
            num_scalar_prefetch=1, grid=(r // tr,),
            in_specs=[pl.BlockSpec((N_DEV, tr, cdim), lambda i, dd: (0, i, 0)), pl.BlockSpec((tr, cdim), lambda i, dd: (i, 0))],
            out_specs=pl.BlockSpec((tr, cdim), lambda i, dd: (i, 0))),
        out_shape=jax.ShapeDtypeStruct((r, cdim), F32),
        compiler_params=_cp(("arbitrary",)),
    )(didx, lands, own)


def _adamw_math(wv, gg, mv, vv):
    nm = ADAM_B1 * mv + (1.0 - ADAM_B1) * gg
    nv = ADAM_B2 * vv + (1.0 - ADAM_B2) * (gg * gg)
    m_hat = nm / (1.0 - ADAM_B1 ** ADAM_STEP)
    v_hat = nv / (1.0 - ADAM_B2 ** ADAM_STEP)
    return -ADAM_LR * (m_hat / (jnp.sqrt(v_hat) + ADAM_EPS) + ADAM_WD * wv), nm, nv


def _adamw_halves(w, mine, theirs, m, v, cidx, name):
    r, cdim = w.shape
    rh = r // 2
    tr = _row_tile(rh)
    nblk = rh // tr

    def body(c_ref, w_ref, a_ref, b_ref, m_ref, v_ref, g_ref, d_ref, nm_ref, nv_ref):
        gg = jnp.where(pl.program_id(0) == c_ref[0], a_ref[...], b_ref[...])
        g_ref[...] = gg
        d_ref[...], nm_ref[...], nv_ref[...] = _adamw_math(w_ref[...], gg, m_ref[...], v_ref[...])

    full = pl.BlockSpec((tr, cdim), lambda hh, i, c: (hh * nblk + i, 0))
    half = pl.BlockSpec((tr, cdim), lambda hh, i, c: (i, 0))
    return pl.pallas_call(
        body, name=name,
        grid_spec=pltpu.PrefetchScalarGridSpec(
            num_scalar_prefetch=1, grid=(2, nblk),
            in_specs=[full, half, half, full, full], out_specs=[full] * 4),
        out_shape=[jax.ShapeDtypeStruct((r, cdim), F32)] * 4,
        compiler_params=_cp(("arbitrary", "arbitrary")),
    )(cidx, w, mine, theirs, m, v)


def _adamw(w, g, m, v, name):
    r, cdim = w.shape
    tr = _row_tile(r) if r % 8 == 0 else r

    def body(w_ref, g_ref, m_ref, v_ref, d_ref, nm_ref, nv_ref):
        d_ref[...], nm_ref[...], nv_ref[...] = _adamw_math(w_ref[...], g_ref[...], m_ref[...], v_ref[...])

    spec = pl.BlockSpec((tr, cdim), lambda i: (i, 0))
    return pl.pallas_call(
        body, name=name, grid=(r // tr,), in_specs=[spec] * 4, out_specs=[spec] * 3,
        out_shape=[jax.ShapeDtypeStruct((r, cdim), F32)] * 3,
        compiler_params=_cp(("arbitrary",)),
    )(w, g, m, v)


def _ada_fwd(c_all, w_ada, b_cols):
    nb, _ = c_all.shape
    n = w_ada.shape[1]

    def body(c_ref, w_ref, b_ref, o_ref):
        cv = c_ref[...]
        o_ref[...] = _mm(cv * _sigmoid(cv), w_ref[...]) + b_ref[...]

    return pl.pallas_call(
        body, name="ada_fwd", out_shape=jax.ShapeDtypeStruct((nb, n), F32),
        compiler_params=pltpu.CompilerParams(vmem_limit_bytes=VMEM_LIMIT),
    )(c_all, w_ada, b_cols)


def _ada_bwd(c_all, dmod_all, dmod_cols):
    d = c_all.shape[1]
    n = dmod_cols.shape[1]

    def body(c_ref, da_ref, dc_ref, gw_ref, gb_ref):
        cv = c_ref[...]
        gw_ref[...] = _mm_tn(cv * _sigmoid(cv), dc_ref[...])
        gb_ref[...] = _colsum(da_ref[...])

    return pl.pallas_call(
        body, name="ada_bwd",
        out_shape=[jax.ShapeDtypeStruct((d, n), F32), jax.ShapeDtypeStruct((1, dmod_all.shape[1]), F32)],
        compiler_params=pltpu.CompilerParams(vmem_limit_bytes=VMEM_LIMIT),
    )(c_all, dmod_all, dmod_cols)


def _proj_fwd(x2, modv, ws, cols, b_in, seq, name, proj_in=None):
    t, d = x2.shape
    n = len(ws)
    ns = ws[0].shape[1]
    tm = min(TM_PROJ, seq)
    tpb = seq // tm
    first = proj_in is None

    def body(c_ref, x_ref, mod_ref, *refs):
        w_refs, b_ref = refs[:n], refs[n]
        outs = refs[n + 1 if first else n + 2:]
        proj_ref, h_s = outs[0], outs[-1]
        s = pl.program_id(1)

        @pl.when(s == 0)
        def _():
            h = (x_ref[...] * (1.0 + mod_ref[1:2, :]) + mod_ref[0:1, :]).astype(BF16)
            h_s[...] = h
            if first:
                outs[1][...] = h

        for k in range(n):
            @pl.when(s == k)
            def _():
                proj_ref[...] = (jnp.dot(h_s[...], w_refs[k][...], preferred_element_type=F32) + b_ref[...]).astype(BF16)

    in_specs = [pl.BlockSpec((tm, d), lambda i, s, c: (i, 0)),
                pl.BlockSpec((None, 8, d), lambda i, s, c: (i // tpb, 0, 0))]
    in_specs += [pl.BlockSpec((d, ns), lambda i, s, c: (0, 0))] * n
    in_specs += [pl.BlockSpec((1, ns), lambda i, s, c: (0, c[s]))]
    out_specs = [pl.BlockSpec((tm, ns), lambda i, s, c: (i, c[s]))]
    out_shape = [jax.ShapeDtypeStruct((t, N_CHIPS * ns), BF16)]
    args = [cols, x2, modv, *ws, b_in]
    aliases = {}
    if first:
        out_specs.append(pl.BlockSpec((tm, d), lambda i, s, c: (i, 0)))
        out_shape.append(jax.ShapeDtypeStruct((t, d), BF16))
    else:
        in_specs.append(_ANY)
        args.append(proj_in)
        aliases = {len(args) - 1: 0}
    return pl.pallas_call(
        body, name=name,
        grid_spec=pltpu.PrefetchScalarGridSpec(
            num_scalar_prefetch=1, grid=(t // tm, n), in_specs=in_specs, out_specs=out_specs,
            scratch_shapes=[pltpu.VMEM((tm, d), BF16)]),
        out_shape=out_shape, input_output_aliases=aliases,
        compiler_params=_cp(("arbitrary", "arbitrary")),
    )(*args)


def _lru_gates(xl, wc_ref, bc_ref, wa_ref, ba_ref, wx_ref, bx_ref, lam_ref):
    xc = bc_ref[...] + wc_ref[CONV_WIDTH - 1:CONV_WIDTH, :] * xl
    for k in range(CONV_WIDTH - 1):
        xc = xc + wc_ref[k:k + 1, :] * _shift_down(xl, CONV_WIDTH - 1 - k)
    r = _sigmoid(_mm(xc, wa_ref[...]) + ba_ref[...])
    gi = _sigmoid_t(_mm(xc, wx_ref[...]) + bx_ref[...])
    nl = -lam_ref[...]
    e = jnp.exp(-jnp.abs(nl))
    u = 1.0 + e
    dlt = u - 1.0
    log1p_e = jnp.where(dlt == 0.0, e, jnp.log(u) * (e / jnp.where(dlt == 0.0, 1.0, dlt)))
    big_l = -LRU_C * (jnp.maximum(nl, 0.0) + log1p_e)
    la = big_l * r
    a = jnp.exp(la)
    m2 = jnp.tanh(-la) * (a * a + 1.0)
    return xc, r, gi, big_l, a, m2


def _lru_prep(proj, lru_w, nb, seq):
    t = proj.shape[0]
    w = LRU_HEADS * HEAD
    w_conv, b_conv, w_a, b_a, w_x, b_x, lam = lru_w

    def body(x_ref, wc_ref, bc_ref, wa_ref, ba_ref, wx_ref, bx_ref, lam_ref, a_ref, inp_ref):
        xc, r, gi, big_l, a, m2 = _lru_gates(x_ref[...].astype(F32), wc_ref, bc_ref, wa_ref, ba_ref, wx_ref, bx_ref, lam_ref)
        a_ref[...] = a
        inp_ref[...] = jnp.sqrt(m2) * (gi * xc)

    col = lambda b, hd: (0, hd)
    head = lambda b, hd: (hd, 0, 0)
    tok = lambda b, hd: (b, hd)
    return pl.pallas_call(
        body, name="lru_prep", grid=(nb, LRU_HEADS),
        in_specs=[pl.BlockSpec((seq, HEAD), tok),
                  pl.BlockSpec((CONV_WIDTH, HEAD), col), pl.BlockSpec((1, HEAD), col),
                  pl.BlockSpec((None, HEAD, HEAD), head), pl.BlockSpec((1, HEAD), col),
                  pl.BlockSpec((None, HEAD, HEAD), head), pl.BlockSpec((1, HEAD), col),
                  pl.BlockSpec((1, HEAD), col)],
        out_specs=[pl.BlockSpec((seq, HEAD), tok)] * 2,
        out_shape=[jax.ShapeDtypeStruct((t, w), F32)] * 2,
        compiler_params=_cp(("arbitrary", "arbitrary")),
    )(proj, w_conv, b_conv, w_a, b_a, w_x, b_x, lam)


def _scan(a3, b3, reverse, name):
    nb, seq, w = a3.shape
    tc = min(TC_SCAN, seq)
    nchunk = seq // tc
    ntile = tc // 8

    def combine(av, bv):
        rows = lax.broadcasted_iota(jnp.int32, av.shape, 0)
        for s in (1, 2, 4):
            if reverse:
                keep = rows < 8 - s
                a_sh, b_sh = pltpu.roll(av, 8 - s, 0), pltpu.roll(bv, 8 - s, 0)
            else:
                keep = rows >= s
                a_sh, b_sh = pltpu.roll(av, s, 0), pltpu.roll(bv, s, 0)
            bv = jnp.where(keep, bv + av * b_sh, bv)
            av = jnp.where(keep, av * a_sh, av)
        return av, bv

    def body(a_ref, b_ref, h_ref, carry):
        @pl.when(pl.program_id(0) == 0)
        def _():
            carry[...] = jnp.zeros_like(carry)

        for b in range(nb):
            def tile(j, hprev):
                jj = ntile - 1 - j if reverse else j
                base = pl.multiple_of(jj * 8, 8)
                av, bv = a_ref[b, pl.ds(base, 8), :], b_ref[b, pl.ds(base, 8), :]
                av, bv = combine(av, av * bv if reverse else bv)
                h = bv + av * hprev
                h_ref[b, pl.ds(base, 8), :] = h
                edge = h[0:1, :] if reverse else h[7:8, :]
                return jnp.broadcast_to(edge, (8, w))

            carry[b] = lax.fori_loop(0, ntile, tile, carry[b])

    imap = (lambda i: (0, nchunk - 1 - i, 0)) if reverse else (lambda i: (0, i, 0))
    spec = pl.BlockSpec((nb, tc, w), imap)
    return pl.pallas_call(
        body, name=name, grid=(nchunk,), in_specs=[spec, spec], out_specs=spec,
        out_shape=jax.ShapeDtypeStruct((nb, seq, w), F32),
        scratch_shapes=[pltpu.VMEM((nb, 8, w), F32)],
        compiler_params=_cp(("arbitrary",)),
    )(a3, b3)


def _sgu_mask():
    ti = lax.broadcasted_iota(jnp.int32, (HEAD, HEAD), 0) // SGU_CHUNK
    si = lax.broadcasted_iota(jnp.int32, (HEAD, HEAD), 1) // SGU_CHUNK
    return si <= ti


def _sgu_specs(tm, d_sgu):
    pw = 256
    first_u = (2 * LRU_HEADS * HEAD) // pw
    n_piece = d_sgu // pw
    specs = [pl.BlockSpec((tm, pw), functools.partial(lambda i, k: (i, k), k=first_u + j)) for j in range(2 * n_piece)]
    return specs, n_piece


def _sgu_fwd(proj, w_sp, b_sp_t, ln_g, ln_b):
    t = proj.shape[0]
    d_sgu = SGU_GROUPS * HEAD
    tm = min(TM_SGU, t)
    nblk = tm // HEAD
    specs, n_piece = _sgu_specs(tm, d_sgu)

    def body(*refs):
        u = jnp.concatenate([r[...] for r in refs[:n_piece]], axis=1).astype(F32)
        v = jnp.concatenate([r[...] for r in refs[n_piece:2 * n_piece]], axis=1).astype(F32)
        w_ref, bt_ref, g_ref, b_ref, y_ref = refs[2 * n_piece:]
        ug = _gelu(u)
        xhat, _ = _ln_stats(_gelu(v))
        vn = (xhat * g_ref[...] + b_ref[...]).astype(BF16)
        mask = _sgu_mask()
        for g in range(SGU_GROUPS):
            wm = jnp.where(mask, w_ref[g], 0.0).astype(BF16)
            cols = slice(g * HEAD, (g + 1) * HEAD)
            for n in range(nblk):
                rows = slice(n * HEAD, (n + 1) * HEAD)
                mixed = jnp.dot(wm, vn[rows, cols], preferred_element_type=F32) + bt_ref[:, g:g + 1]
                y_ref[rows, cols] = (ug[rows, cols] * mixed).astype(BF16)

    full = lambda shape: pl.BlockSpec(shape, lambda i: (0,) * len(shape))
    return pl.pallas_call(
        body, name="sgu_fwd", grid=(t // tm,),
        in_specs=specs + [full(w_sp.shape), full(b_sp_t.shape), full(ln_g.shape), full(ln_b.shape)],
        out_specs=pl.BlockSpec((tm, d_sgu), lambda i: (i, 0)),
        out_shape=jax.ShapeDtypeStruct((t, d_sgu), BF16),
        compiler_params=_cp(("arbitrary",)),
    )(*([proj] * (2 * n_piece)), w_sp, b_sp_t, ln_g, ln_b)


def _mix_fwd(hs, proj, y_sgu, x2, modv, w_o_lru_g, w_o_sgu_g, w_out_g, ln1_g, ln1_b, seq):
    t, d = x2.shape
    w = hs.shape[1]
    d_sgu = y_sgu.shape[1]
    nq, _, ns = w_o_sgu_g.shape
    tm = min(TM_MIX, seq)
    tpb = seq // tm

    def body(hs_ref, gl_ref, ys_ref, ga_ref, gb_ref, x_ref, mod_ref, wl_ref, ws_ref, wo_ref, g1_ref, b1_ref,
             yap_ref, ya_ref, yb_ref, mg_ref, mix_ref, x1_ref):
        yap = (hs_ref[...] * _gelu(gl_ref[...].astype(F32))).astype(BF16)
        yap_ref[...] = yap
        y_a = jnp.dot(yap, wl_ref[...], preferred_element_type=F32)
        ys = ys_ref[...]
        y_b = jnp.concatenate([jnp.dot(ys, ws_ref[q], preferred_element_type=F32) for q in range(nq)], axis=1)
        ya_ref[...] = y_a.astype(BF16)
        yb_ref[...] = y_b.astype(BF16)
        merged = (_sigmoid_t(ga_ref[...].astype(F32)) * y_a + _sigmoid_t(gb_ref[...].astype(F32)) * y_b).astype(BF16)
        mg_ref[...] = merged
        mix = jnp.dot(merged, wo_ref[...], preferred_element_type=F32)
        mix_ref[...] = mix
        xhat, _ = _ln_stats(ALPHA * x_ref[...] + (1.0 + mod_ref[2:3, :]) * mix)
        x1_ref[...] = xhat * g1_ref[...] + b1_ref[...]

    row = lambda width, col: pl.BlockSpec((tm, width), functools.partial(lambda i, k: (i, k), k=col))
    full = lambda shape: pl.BlockSpec(shape, lambda i: (0,) * len(shape))
    return pl.pallas_call(
        body, name="mix_fwd", grid=(t // tm,),
        in_specs=[row(w, 0), row(w, 1), row(d_sgu, 0), row(d, 4), row(d, 5), row(d, 0),
                  pl.BlockSpec((None, 8, d), lambda i: (i // tpb, 0, 0)),
                  full(w_o_lru_g.shape), full(w_o_sgu_g.shape), full(w_out_g.shape), full(ln1_g.shape), full(ln1_b.shape)],
        out_specs=[row(w, 0), row(d, 0), row(d, 0), row(d, 0), row(d, 0), row(d, 0)],
        out_shape=[jax.ShapeDtypeStruct((t, w), BF16), jax.ShapeDtypeStruct((t, d), BF16),
                   jax.ShapeDtypeStruct((t, d), BF16), jax.ShapeDtypeStruct((t, d), BF16),
                   jax.ShapeDtypeStruct((t, d), F32), jax.ShapeDtypeStruct((t, d), F32)],
        compiler_params=_cp(("arbitrary",)),
    )(hs, proj, y_sgu, proj, proj, x2, modv, w_o_lru_g, w_o_sgu_g, w_out_g, ln1_g, ln1_b)


def _mlp_fwd(x1, modv, w_up_g, w_down_g, ln2_g, ln2_b, target, nb, seq):
    t, d = x1.shape
    nq, _, ns = w_up_g.shape
    tm = min(TM_MLP, seq)
    tpb = seq // tm
    nt = t // tm

    def body(x1_ref, mod_ref, wu_ref, wd_ref, g2_ref, b2_ref, tg_ref,
             rl_ref, act_ref, h2_ref, dz2_ref, df_ref, st_ref, pb_ref, h2_s, acc):
        i, j = pl.program_id(0), pl.program_id(1)

        @pl.when(j == 0)
        def _():
            h2 =(x1_ref[...] * (1.0 + mod_ref[4:5, :]) + mod_ref[3:4, :]).astype(BF16)
            h2_s[...] = h2
            h2_ref[...] = h2
            acc[...] = jnp.zeros_like(acc)

        @pl.when((i == 0) & (j == 0))
        def _():
            st_ref[...] = jnp.zeros_like(st_ref)

        @pl.when((i % tpb == 0) & (j == 0))
        def _():
            pb_ref[...] = jnp.zeros_like(pb_ref)

        r = jnp.maximum(jnp.dot(h2_s[...], wu_ref[...], preferred_element_type=F32), 0.0)
        act = (r * r).astype(BF16)
        rl_ref[...] = r.astype(BF16)
        act_ref[...] = act
        acc[...] += jnp.dot(act, wd_ref[...], preferred_element_type=F32)

        @pl.when(j == nq - 1)
        def _():
            f = acc[...]
            xhat, rstd = _ln_stats(ALPHA * x1_ref[...] + (1.0 + mod_ref[5:6, :]) * f)
            y = xhat * g2_ref[...] + b2_ref[...]
            err = y - tg_ref[...]
            dy = err * (1.0 / d)
            dz2 = _ln_bwd(dy * g2_ref[...], xhat, rstd)
            dz2_ref[...] = dz2
            df_ref[...] = ((1.0 + mod_ref[5:6, :]) * dz2).astype(BF16)
            st_ref[0:1, :] += _colsum(dy * xhat)
            st_ref[1:2, :] += _colsum(dy)
            st_ref[2:3, :] += (0.5 / d) * jnp.sum(_colsum(err * err), axis=1, keepdims=True)
            pb_ref[0:1, :] += _colsum(dz2 * f)

    tok = lambda i, j: (i, 0)
    return pl.pallas_call(
        body, name="mlp_fwd", grid=(nt, nq),
        in_specs=[pl.BlockSpec((tm, d), tok), pl.BlockSpec((None, 8, d), lambda i, j: (i // tpb, 0, 0)),
                  pl.BlockSpec((None, d, ns), lambda i, j: (j, 0, 0)), pl.BlockSpec((ns, d), lambda i, j: (j, 0)),
                  pl.BlockSpec((1, d), lambda i, j: (0, 0)), pl.BlockSpec((1, d), lambda i, j: (0, 0)),
                  pl.BlockSpec((tm, d), tok)],
        out_specs=[pl.BlockSpec((tm, ns), lambda i, j: (i, j)), pl.BlockSpec((tm, ns), lambda i, j: (i, j)),
                   pl.BlockSpec((tm, d), tok), pl.BlockSpec((tm, d), tok), pl.BlockSpec((tm, d), tok),
                   pl.BlockSpec((8, d), lambda i, j: (0, 0)), pl.BlockSpec((None, 8, d), lambda i, j: (i // tpb, 0, 0))],
        out_shape=[jax.ShapeDtypeStruct((t, nq * ns), BF16), jax.ShapeDtypeStruct((t, nq * ns), BF16),
                   jax.ShapeDtypeStruct((t, d), BF16),
                   jax.ShapeDtypeStruct((t, d), F32), jax.ShapeDtypeStruct((t, d), BF16),
                   jax.ShapeDtypeStruct((8, d), F32), jax.ShapeDtypeStruct((nb, 8, d), F32)],
        scratch_shapes=[pltpu.VMEM((tm, d), BF16), pltpu.VMEM((tm, d), F32)],
        compiler_params=_cp(("arbitrary", "arbitrary")),
    )(x1, modv, w_up_g, w_down_g, ln2_g, ln2_b, target)


def _mlp_bwd(df, up, w_down_g, w_up_g, dz2, x2, mix, modv, ln1_g, ln1_b, nb, seq):
    t, d = x2.shape
    nq, _, ns = w_up_g.shape
    tm = min(TM_MLP, seq)
    tpb = seq // tm

    def body(df_ref, rl_ref, wd_ref, wu_ref, dz2_ref, x_ref, mix_ref, mod_ref, g1_ref, b1_ref,
             dup_ref, dz1_ref, dmix_ref, st_ref, pb_ref, acc):
        i, j = pl.program_id(0), pl.program_id(1)

        @pl.when(j == 0)
        def _():
            acc[...] = jnp.zeros_like(acc)

        @pl.when((i == 0) & (j == 0))
        def _():
            st_ref[...] = jnp.zeros_like(st_ref)

        @pl.when((i % tpb == 0) & (j == 0))
        def _():
            pb_ref[...] = jnp.zeros_like(pb_ref)

        dup = (_mm_nt(df_ref[...], wd_ref[...]) * (2.0 * rl_ref[...].astype(F32))).astype(BF16)
        dup_ref[...] = dup
        acc[...] += _mm_nt(dup, wu_ref[...])

        @pl.when(j == nq - 1)
        def _():
            dh2 = acc[...]
            mix = mix_ref[...]
            xhat, rstd = _ln_stats(ALPHA * x_ref[...] + (1.0 + mod_ref[2:3, :]) * mix)
            x1 = xhat * g1_ref[...] + b1_ref[...]
            dx1 = ALPHA * dz2_ref[...] + dh2 * (1.0 + mod_ref[4:5, :])
            dz1 = _ln_bwd(dx1 * g1_ref[...], xhat, rstd)
            dz1_ref[...] = dz1
            dmix_ref[...] = ((1.0 + mod_ref[2:3, :]) * dz1).astype(BF16)
            st_ref[0:1, :] += _colsum(dx1 * xhat)
            st_ref[1:2, :] += _colsum(dx1)
            pb_ref[0:1, :] += _colsum(dh2 * x1)
            pb_ref[1:2, :] += _colsum(dh2)
            pb_ref[2:3, :] += _colsum(dz1 * mix)

    tok = lambda i, j: (i, 0)
    chunk = lambda i, j: (i, j)
    return pl.pallas_call(
        body, name="mlp_bwd", grid=(t // tm, nq),
        in_specs=[pl.BlockSpec((tm, d), tok), pl.BlockSpec((tm, ns), chunk),
                  pl.BlockSpec((ns, d), lambda i, j: (j, 0)), pl.BlockSpec((None, d, ns), lambda i, j: (j, 0, 0)),
                  pl.BlockSpec((tm, d), tok), pl.BlockSpec((tm, d), tok), pl.BlockSpec((tm, d), tok),
                  pl.BlockSpec((None, 8, d), lambda i, j: (i // tpb, 0, 0)),
                  pl.BlockSpec((1, d), lambda i, j: (0, 0)), pl.BlockSpec((1, d), lambda i, j: (0, 0))],
        out_specs=[pl.BlockSpec((tm, ns), chunk),
                   pl.BlockSpec((tm, d), tok), pl.BlockSpec((tm, d), tok),
                   pl.BlockSpec((8, d), lambda i, j: (0, 0)), pl.BlockSpec((None, 8, d), lambda i, j: (i // tpb, 0, 0))],
        out_shape=[jax.ShapeDtypeStruct((t, nq * ns), BF16),
                   jax.ShapeDtypeStruct((t, d), F32), jax.ShapeDtypeStruct((t, d), BF16),
                   jax.ShapeDtypeStruct((8, d), F32), jax.ShapeDtypeStruct((nb, 8, d), F32)],
        scratch_shapes=[pltpu.VMEM((tm, d), F32)],
        compiler_params=_cp(("arbitrary", "arbitrary")),
    )(df, up, w_down_g, w_up_g, dz2, x2, mix, modv, ln1_g, ln1_b)


def _mix_bwd(dmix, proj, y_a, y_b, hs, w_out_g, w_o_lru_g, w_o_sgu_g, seq, after=()):
    t, d = dmix.shape
    w = hs.shape[1]
    nq, d_sgu, ns = w_o_sgu_g.shape
    tm = min(TM_MIX, seq)

    def body(dmix_ref, ga_ref, gb_ref, ya_ref, yb_ref, gl_ref, hs_ref, wo_ref, wl_ref, ws_ref,
             dya_ref, dyb_ref, dga_ref, dgb_ref, dgl_ref, dyl_ref, dys_ref):
        dmerged = _mm_nt(dmix_ref[...], wo_ref[...])
        sa, sb = _sigmoid_t(ga_ref[...].astype(F32)), _sigmoid_t(gb_ref[...].astype(F32))
        dy_a = (dmerged * sa).astype(BF16)
        dy_b = (dmerged * sb).astype(BF16)
        dya_ref[...] = dy_a
        dyb_ref[...] = dy_b
        dga_ref[...] = (dmerged * ya_ref[...].astype(F32) * (sa * (1.0 - sa))).astype(BF16)
        dgb_ref[...] = (dmerged * yb_ref[...].astype(F32) * (sb * (1.0 - sb))).astype(BF16)
        dyap = _mm_nt(dy_a, wl_ref[...])
        gel, dgel = _gelu_and_grad(gl_ref[...].astype(F32))
        dyl_ref[...] = dyap * gel
        dgl_ref[...] = (dyap * hs_ref[...] * dgel).astype(BF16)
        dys = _mm_nt(dy_b[:, 0:ns], ws_ref[0])
        for q in range(1, nq):
            dys = dys + _mm_nt(dy_b[:, q * ns:(q + 1) * ns], ws_ref[q])
        dys_ref[...] = dys

    row = lambda width, col: pl.BlockSpec((tm, width), functools.partial(lambda i, k: (i, k), k=col))
    full = lambda shape: pl.BlockSpec(shape, lambda i: (0,) * len(shape))
    return pl.pallas_call(
        _ordered(body, 10, after), name="mix_bwd", grid=(t // tm,),
        in_specs=[row(d, 0), row(d, 4), row(d, 5), row(d, 0), row(d, 0), row(w, 1), row(w, 0),
                  full(w_out_g.shape), full(w_o_lru_g.shape), full(w_o_sgu_g.shape)] + [_ANY] * len(after),
        out_specs=[row(d, 0), row(d, 0), row(d, 0), row(d, 0), row(w, 0), row(w, 0), row(d_sgu, 0)],
        out_shape=[jax.ShapeDtypeStruct((t, d), BF16), jax.ShapeDtypeStruct((t, d), BF16),
                   jax.ShapeDtypeStruct((t, d), BF16), jax.ShapeDtypeStruct((t, d), BF16),
                   jax.ShapeDtypeStruct((t, w), BF16), jax.ShapeDtypeStruct((t, w), F32),
                   jax.ShapeDtypeStruct((t, d_sgu), F32)],
        compiler_params=_cp(("arbitrary",)),
    )(dmix, proj, proj, y_a, y_b, proj, hs, w_out_g, w_o_lru_g, w_o_sgu_g, *after)


def _sgu_bwd(proj, dys, w_sp, b_sp_t, ln_g, ln_b, after=()):
    t = proj.shape[0]
    d_sgu = SGU_GROUPS * HEAD
    tm = min(TM_SGU, t)
    nblk = tm // HEAD
    specs, n_piece = _sgu_specs(tm, d_sgu)

    def body(*refs):
        u = jnp.concatenate([r[...] for r in refs[:n_piece]], axis=1).astype(F32)
        v = jnp.concatenate([r[...] for r in refs[n_piece:2 * n_piece]], axis=1).astype(F32)
        dys_ref, w_ref, bt_ref, g_ref, b_ref, du_ref, dv_ref, dw_ref, st_ref, dbt_ref, dvn_s = refs[2 * n_piece:]

        @pl.when(pl.program_id(0) == 0)
        def _():
            dw_ref[...] = jnp.zeros_like(dw_ref)
            st_ref[...] = jnp.zeros_like(st_ref)
            dbt_ref[...] = jnp.zeros_like(dbt_ref)

        ug, dug_du = _gelu_and_grad(u)
        vg, dvg_dv = _gelu_and_grad(v)
        xhat, rstd = _ln_stats(vg)
        vn = (xhat * g_ref[...] + b_ref[...]).astype(BF16)
        dys_v = dys_ref[...]
        mask = _sgu_mask()
        for g in range(SGU_GROUPS):
            wm = jnp.where(mask, w_ref[g], 0.0).astype(BF16)
            cols = slice(g * HEAD, (g + 1) * HEAD)
            dw_g = jnp.zeros((HEAD, HEAD), F32)
            db_g = jnp.zeros((HEAD, 1), F32)
            for n in range(nblk):
                rows = slice(n * HEAD, (n + 1) * HEAD)
                vn_blk = vn[rows, cols]
                mixed = jnp.dot(wm, vn_blk, preferred_element_type=F32) + bt_ref[:, g:g + 1]
                dy_blk = dys_v[rows, cols]
                du_ref[rows, cols] = (dy_blk * mixed * dug_du[rows, cols]).astype(BF16)
                dmx = dy_blk * ug[rows, cols]
                dvn_s[rows, cols] = _mm_tn(wm, dmx)
                dw_g = dw_g + _mm_nt(dmx, vn_blk)
                db_g = db_g + jnp.sum(dmx, axis=1, keepdims=True)
            dw_ref[g] += jnp.where(mask, dw_g, 0.0)
            dbt_ref[:, g:g + 1] += db_g
        dvn = dvn_s[...]
        st_ref[0:1, :] += _colsum(dvn * xhat)
        st_ref[1:2, :] += _colsum(dvn)
        dv_ref[...] = (_ln_bwd(dvn * g_ref[...], xhat, rstd) * dvg_dv).astype(BF16)

    full = lambda shape: pl.BlockSpec(shape, lambda i: (0,) * len(shape))
    tok = pl.BlockSpec((tm, d_sgu), lambda i: (i, 0))
    return pl.pallas_call(
        _ordered(body, 2 * n_piece + 5, after), name="sgu_bwd", grid=(t // tm,),
        in_specs=specs + [tok, full(w_sp.shape), full(b_sp_t.shape), full(ln_g.shape), full(ln_b.shape)]
        + [_ANY] * len(after),
        out_specs=[tok, tok, full(w_sp.shape), full((8, d_sgu)), full((HEAD, HEAD))],
        out_shape=[jax.ShapeDtypeStruct((t, d_sgu), BF16), jax.ShapeDtypeStruct((t, d_sgu), BF16),
                   jax.ShapeDtypeStruct(w_sp.shape, F32), jax.ShapeDtypeStruct((8, d_sgu), F32),
                   jax.ShapeDtypeStruct((HEAD, HEAD), F32)],
        scratch_shapes=[pltpu.VMEM((tm, d_sgu), F32)],
        compiler_params=_cp(("arbitrary",)),
    )(*([proj] * (2 * n_piece)), dys, w_sp, b_sp_t, ln_g, ln_b, *after)


def _lru_bwd(proj, hs, e, dyl, lru_w, nb, seq, after=()):
    t = proj.shape[0]
    w = LRU_HEADS * HEAD
    w_conv, b_conv, w_a, b_a, w_x, b_x, lam = lru_w

    def body(x_ref, hs_ref, e_ref, dy_ref, wc_ref, bc_ref, wa_ref, ba_ref, wx_ref, bx_ref, lam_ref,
             dxl_ref, dwa_ref, dwx_ref, st_ref):
        @pl.when(pl.program_id(1) == 0)
        def _():
            dwa_ref[...] = jnp.zeros_like(dwa_ref)
            dwx_ref[...] = jnp.zeros_like(dwx_ref)
            st_ref[...] = jnp.zeros_like(st_ref)

        xl = x_ref[...].astype(F32)
        xc, r, gi, big_l, a, m2 = _lru_gates(xl, wc_ref, bc_ref, wa_ref, ba_ref, wx_ref, bx_ref, lam_ref)
        inv_mult = lax.rsqrt(m2)
        mult = m2 * inv_mult
        dh = dy_ref[...] + _shift_up(e_ref[...], 1)
        da = dh * _shift_down(hs_ref[...], 1)
        dmult = dh * (gi * xc)
        d_i = dh * (mult * xc)
        dxc = dh * (mult * gi)
        dla = a * (da - dmult * (a * inv_mult))
        dr = dla * big_l
        d_big_l = _colsum(dla * r)
        dra = dr * (r * (1.0 - r))
        dia = d_i * (gi * (1.0 - gi))
        dwa_ref[...] += _mm_tn(xc, dra)
        dwx_ref[...] += _mm_tn(xc, dia)
        dxc = dxc + _mm_nt(dra, wa_ref[...]) + _mm_nt(dia, wx_ref[...])
        dxl = wc_ref[CONV_WIDTH - 1:CONV_WIDTH, :] * dxc
        st_ref[4 + CONV_WIDTH - 1:4 + CONV_WIDTH, :] += _colsum(dxc * xl)
        for k in range(CONV_WIDTH - 1):
            ahead = _shift_up(dxc, CONV_WIDTH - 1 - k)
            dxl = dxl + wc_ref[k:k + 1, :] * ahead
            st_ref[4 + k:5 + k, :] += _colsum(ahead * xl)
        dxl_ref[...] = dxl.astype(BF16)
        st_ref[0:1, :] += _colsum(dra)
        st_ref[1:2, :] += _colsum(dia)
        st_ref[2:3, :] += d_big_l * (LRU_C * _sigmoid(-lam_ref[...]))
        st_ref[3:4, :] += _colsum(dxc)

    col = lambda hd, b: (0, hd)
    head = lambda hd, b: (hd, 0, 0)
    tok = lambda hd, b: (b, hd)
    seq_blk = pl.BlockSpec((seq, HEAD), tok)
    return pl.pallas_call(
        _ordered(body, 11, after), name="lru_bwd", grid=(LRU_HEADS, nb),
        in_specs=[seq_blk, seq_blk, seq_blk, seq_blk,
                  pl.BlockSpec((CONV_WIDTH, HEAD), col), pl.BlockSpec((1, HEAD), col),
                  pl.BlockSpec((None, HEAD, HEAD), head), pl.BlockSpec((1, HEAD), col),
                  pl.BlockSpec((None, HEAD, HEAD), head), pl.BlockSpec((1, HEAD), col),
                  pl.BlockSpec((1, HEAD), col)] + [_ANY] * len(after),
        out_specs=[seq_blk, pl.BlockSpec((None, HEAD, HEAD), head), pl.BlockSpec((None, HEAD, HEAD), head),
                   pl.BlockSpec((8, HEAD), col)],
        out_shape=[jax.ShapeDtypeStruct((t, w), BF16), jax.ShapeDtypeStruct((LRU_HEADS, HEAD, HEAD), F32),
                   jax.ShapeDtypeStruct((LRU_HEADS, HEAD, HEAD), F32), jax.ShapeDtypeStruct((8, w), F32)],
        compiler_params=_cp(("arbitrary", "arbitrary")),
    )(proj, hs, e, dyl, w_conv, b_conv, w_a, b_a, w_x, b_x, lam, *after)


def _weight_grad(a, g, col_shards, name, after=()):
    t, k = a.shape
    n = g.shape[1]
    tt = min(TT_DW, t)
    tk = k if k <= 1536 else 1024
    ns = n // N_CHIPS if col_shards else n
    narrow = col_shards and ns < 512
    tn = n if narrow else min(ns, 768 if ns % 768 == 0 else 1024)
    while ns % tn and not narrow:
        tn //= 2
    per = max(ns // tn, 1)

    def body(a_ref, g_ref, o_ref):
        @pl.when(pl.program_id(2) == 0)
        def _():
            o_ref[...] = jnp.zeros_like(o_ref)

        res = _mm_tn(a_ref[...], g_ref[...])
        if narrow:
            for q in range(N_CHIPS):
                o_ref[q] += res[:, q * ns:(q + 1) * ns]
        else:
            o_ref[...] += res

    if narrow:
        out_spec = pl.BlockSpec((N_CHIPS, tk, ns), lambda i, j, s: (0, i, 0))
        out_shape = jax.ShapeDtypeStruct((N_CHIPS, k, ns), F32)
    elif col_shards:
        out_spec = pl.BlockSpec((None, tk, tn), lambda i, j, s: (j // per, i, j % per))
        out_shape = jax.ShapeDtypeStruct((N_CHIPS, k, ns), F32)
    else:
        out_spec = pl.BlockSpec((tk, tn), lambda i, j, s: (i, j))
        out_shape = jax.ShapeDtypeStruct((k, n), F32)
    return pl.pallas_call(
        _ordered(body, 2, after), name=name, grid=(k // tk, n // tn, t // tt),
        in_specs=[pl.BlockSpec((tt, tk), lambda i, j, s: (s, i)), pl.BlockSpec((tt, tn), lambda i, j, s: (s, j))]
        + [_ANY] * len(after),
        out_specs=out_spec, out_shape=out_shape,
        compiler_params=_cp(("arbitrary", "arbitrary", "arbitrary")),
    )(a, g, *after)


def _input_grad(dproj, w_in_g, dz1, x2, modv, nb, seq, after=()):
    t, d = x2.shape
    nq, _, ns = w_in_g.shape
    tm = min(TM_DH, seq)
    tpb = seq // tm

    def body(dp_ref, w_ref, dz1_ref, x_ref, mod_ref, gx_ref, db_ref, pb_ref, acc):
        i, q = pl.program_id(0), pl.program_id(1)

        @pl.when(q == 0)
        def _():
            acc[...] = jnp.zeros_like(acc)

        @pl.when((i == 0) & (q == 0))
        def _():
            db_ref[...] = jnp.zeros_like(db_ref)

        @pl.when((i % tpb == 0) & (q == 0))
        def _():
            pb_ref[...] = jnp.zeros_like(pb_ref)

        dp = dp_ref[...]
        acc[...] += _mm_nt(dp, w_ref[...])
        db_ref[q, 0:1, :] += _colsum(dp.astype(F32))

        @pl.when(q == nq - 1)
        def _():
            dh = acc[...]
            gx_ref[...] = ALPHA * dz1_ref[...] + dh * (1.0 + mod_ref[1:2, :])
            pb_ref[0:1, :] += _colsum(dh * x_ref[...])
            pb_ref[1:2, :] += _colsum(dh)

    tok = lambda i, q: (i, 0)
    return pl.pallas_call(
        _ordered(body, 5, after), name="input_grad", grid=(t // tm, nq),
        in_specs=[pl.BlockSpec((tm, ns), lambda i, q: (i, q)), pl.BlockSpec((None, d, ns), lambda i, q: (q, 0, 0)),
                  pl.BlockSpec((tm, d), tok), pl.BlockSpec((tm, d), tok),
                  pl.BlockSpec((None, 8, d), lambda i, q: (i // tpb, 0, 0))] + [_ANY] * len(after),
        out_specs=[pl.BlockSpec((tm, d), tok), pl.BlockSpec((nq, 8, ns), lambda i, q: (0, 0, 0)),
                   pl.BlockSpec((None, 8, d), lambda i, q: (i // tpb, 0, 0))],
        out_shape=[jax.ShapeDtypeStruct((t, d), F32), jax.ShapeDtypeStruct((nq, 8, ns), F32),
                   jax.ShapeDtypeStruct((nb, 8, d), F32)],
        scratch_shapes=[pltpu.VMEM((tm, d), F32)],
        compiler_params=_cp(("arbitrary", "arbitrary")),
    )(dproj, w_in_g, dz1, x2, modv, *after)


def _rows128(v):
    flat = v.reshape(-1, HEAD)
    pad = (-flat.shape[0]) % 8
    return jnp.pad(flat, ((0, pad), (0, 0))) if pad else flat


def kernel(x, c, w_ada, b_ada, w_in, b_in, w_conv, b_conv, w_rg_a, b_rg_a, w_rg_x, b_rg_x, lru_lambda, w_sp, b_sp, ln_v_g, ln_v_b, w_o_lru, w_o_sgu, w_out, ln1_g, ln1_b, w_up, w_down, ln2_g, ln2_b, loss_target, m_w_ada, m_b_ada, m_w_in, m_b_in, m_w_conv, m_b_conv, m_w_rg_a, m_b_rg_a, m_w_rg_x, m_b_rg_x, m_lru_lambda, m_w_sp, m_b_sp, m_ln_v_g, m_ln_v_b, m_w_o_lru, m_w_o_sgu, m_w_out, m_ln1_g, m_ln1_b, m_w_up, m_w_down, m_ln2_g, m_ln2_b, v_w_ada, v_b_ada, v_w_in, v_b_in, v_w_conv, v_b_conv, v_w_rg_a, v_b_rg_a, v_w_rg_x, v_b_rg_x, v_lru_lambda, v_w_sp, v_b_sp, v_ln_v_g, v_ln_v_b, v_w_o_lru, v_w_o_sgu, v_w_out, v_ln1_g, v_ln1_b, v_w_up, v_w_down, v_ln2_g, v_ln2_b):
    given = dict(locals())
    nb, seq, d = x.shape
    t = nb * seq
    w_lru = LRU_HEADS * HEAD
    d_sgu = SGU_GROUPS * HEAD
    xi, yi, ci = lax.axis_index("x"), lax.axis_index("y"), lax.axis_index("c")
    chip = 2 * xi + yi
    dev = 2 * chip + ci
    cidx = jnp.reshape(ci, (1,)).astype(jnp.int32)

    x2 = x.reshape(t, d)
    target = loss_target.reshape(t, d)

    big = ["w_in", "w_o_lru", "w_o_sgu", "w_out", "w_up", "w_down"]
    shards_a = [w_in[0].astype(BF16)]
    shards_b = [given[n][0].astype(BF16) for n in big[1:]]
    pidx = jnp.reshape(chip, (1,)).astype(jnp.int32)

    c_rows = _rows128(c)
    wconv_rows = _rows128(w_conv[0])
    slab0 = _all_gather_small(jnp.concatenate([c_rows, wconv_rows], axis=0), "gather_c_wconv")
    slab0 = slab0.reshape(N_DEV, -1, HEAD)
    c_all = slab0[:, :c_rows.shape[0]].reshape(N_DEV * nb, d)
    n_wc = CONV_WIDTH * (w_lru // N_CHIPS) // HEAD
    wc = slab0[0::2, c_rows.shape[0]:c_rows.shape[0] + n_wc].reshape(N_CHIPS, CONV_WIDTH, w_lru // N_CHIPS)
    w_conv_full = jnp.transpose(wc, (1, 0, 2)).reshape(CONV_WIDTH, w_lru)

    n_ada = w_ada.shape[2]
    b_ada_cols = lax.dynamic_slice(b_ada, (0, chip * n_ada), (1, n_ada))
    mod_cols = _ada_fwd(c_all, w_ada[0], b_ada_cols)
    half = (N_DEV * nb) // 2
    mod_half = lax.dynamic_slice(mod_cols, (ci * half, 0), (half, n_ada))
    mod_g = _all_gather_small(mod_half, "gather_mod").reshape(N_CHIPS, 2, half, n_ada)
    mod_all = jnp.transpose(mod_g, (1, 2, 0, 3)).reshape(N_DEV * nb, N_CHIPS * n_ada)
    mod_loc = lax.dynamic_slice(mod_all, (dev * nb, 0), (nb, N_CHIPS * n_ada)).reshape(nb, 6, d)
    modv = jnp.pad(mod_loc, ((0, 0), (0, 2), (0, 0)))

    lru_w = (w_conv_full, b_conv, w_rg_a[0], b_rg_a, w_rg_x[0], b_rg_x, lru_lambda)
    b_sp_t = jnp.transpose(b_sp[0])

    land = lambda s: jax.ShapeDtypeStruct((N_CHIPS,) + s.shape, s.dtype)
    sds = lambda s: jax.ShapeDtypeStruct(s.shape, s.dtype)
    started_a = _split_start(shards_a, [sds(shards_a[0])] * 3, _peer_gather_copies((0, 1, 2)), 3, "gather_w_in_start",
                             after=(modv,))
    shards_b, shards_c = shards_b[:3], shards_b[3:]
    started_b = _split_start(shards_b, [land(s) for s in shards_b], _gather_copies, 3 * len(shards_b),
                             "gather_w_mix_start", after=(started_a[-1],))
    started_c = _split_start(shards_c, [land(s) for s in shards_c], _gather_copies, 3 * len(shards_c),
                             "gather_w_mlp_start", after=(started_b[-1],))

    ids = lambda *v: jnp.stack(v).astype(jnp.int32)
    modv_t = modv + started_c[-1][0:1, 0:1]
    proj, h = _proj_fwd(x2, modv_t, [started_a[2]], ids(chip), b_in, seq, "proj_fwd_own")
    own_a, lands_a = _split_wait(started_a, 1, _peer_gather_copies((0, 1)), "gather_w_in_wait_near", after=(proj,))
    (proj,) = _proj_fwd(x2, modv, lands_a[:2], ids(chip ^ 1, chip ^ 2), b_in, seq, "proj_fwd_near", proj_in=proj)
    own_a, lands_a = _split_wait((started_a[0], started_a[1], *own_a, *lands_a, started_a[-1]), 1,
                                 _peer_gather_copies((2,)), "gather_w_in_wait_far", after=(proj,))
    (proj,) = _proj_fwd(x2, modv, lands_a[2:], ids(chip ^ 3), b_in, seq, "proj_fwd_far", proj_in=proj)
    w_in_g = lax.empty((N_CHIPS,) + shards_a[0].shape, BF16)
    for k, (shard, slot) in enumerate(zip(own_a + lands_a, (chip, chip ^ 1, chip ^ 2, chip ^ 3))):
        (w_in_g,) = _fill_own_slot([w_in_g], [shard], ids(slot), ["place_w_in_%d" % k])
    a, inp = _lru_prep(proj, lru_w, nb, seq)
    a3 = a.reshape(nb, seq, w_lru)
    hs = _scan(a3, inp.reshape(nb, seq, w_lru), False, "lru_scan").reshape(t, w_lru)
    y_sgu = _sgu_fwd(proj, w_sp[0], b_sp_t, ln_v_g, ln_v_b)
    shards_b, lands_b = _split_wait(started_b, len(shards_b), _gather_copies, "gather_w_mix_wait", after=(hs, y_sgu))
    w_o_lru_g, w_o_sgu_g, w_out_g = _fill_own_slot(lands_b, shards_b, pidx, ["own_" + n for n in big[1:4]])
    w_o_lru_g = w_o_lru_g.reshape(w_lru, d)
    w_out_g = w_out_g.reshape(d, d)
    yap, y_a, y_b, merged, mix, x1 = _mix_fwd(hs, proj, y_sgu, x2, modv, w_o_lru_g, w_o_sgu_g, w_out_g, ln1_g, ln1_b, seq)
    shards_c, lands_c = _split_wait(started_c, len(shards_c), _gather_copies, "gather_w_mlp_wait", after=(x1,))
    w_up_g, w_down_g = _fill_own_slot(lands_c, shards_c, pidx, ["own_" + n for n in big[4:]])
    w_down_g = w_down_g.reshape(-1, d)
    up, act, h2, dz2, df, st2, pb2 = _mlp_fwd(x1, modv, w_up_g, w_down_g, ln2_g, ln2_b, target, nb, seq)
    loss = lax.psum(st2[2, 0], ("x", "y", "c"))

    part = {}

    def to_sibling_start(group, tag, after=()):
        g4 = []
        for n in group:
            shard = given[n].shape[1:]
            g4.append(part[n].reshape(N_CHIPS, 2, shard[0] // 2, shard[1]))
        shapes = [jax.ShapeDtypeStruct((N_CHIPS,) + g.shape[2:], F32) for g in g4]
        return _split_start(g4, shapes, _to_sibling_copies, len(g4), "grads_to_sibling_start_" + tag, after)

    def to_chips_start(group, started, tag, after=()):
        g4, recv = _split_wait(started, len(group), _to_sibling_copies, "grads_to_sibling_wait_" + tag, after)
        own4 = [_add_own_half(g4[k], recv[k], cidx, "grad_pair_sum_" + n) for k, n in enumerate(group)]
        shapes = [jax.ShapeDtypeStruct((3,) + o.shape[1:], BF16) for o in own4]
        return _split_start(own4, shapes, _chip_exchange_copies, 3 * len(own4), "grads_chip_exchange_start_" + tag)

    def chips_finish(group, started, tag, after=()):
        own4, slots = _split_wait(started, len(group), _chip_exchange_copies, "grads_chip_exchange_wait_" + tag, after)
        return [_sum_own_and_peers(own4[k], slots[k], pidx, "grad_chip_sum_" + n) for k, n in enumerate(group)]

    dup, dz1, dmix, st1, pb1 = _mlp_bwd(df, up, w_down_g, w_up_g, dz2, x2, mix, modv, ln1_g, ln1_b, nb, seq)
    group1 = ["w_up", "w_down"]
    part["w_up"] = _weight_grad(h2, dup, True, "grad_w_up")
    part["w_down"] = _weight_grad(act, df, False, "grad_w_down")
    sib1 = to_sibling_start(group1, "mlp")
    dy_a, dy_b, dga, dgb, dgl, dyl, dys = _mix_bwd(dmix, proj, y_a, y_b, hs, w_out_g, w_o_lru_g, w_o_sgu_g, seq,
                                                   after=(sib1[-1],))
    group2 = ["w_o_lru", "w_o_sgu", "w_out"]
    part["w_o_lru"] = _weight_grad(yap, dy_a, False, "grad_w_o_lru")
    part["w_o_sgu"] = _weight_grad(y_sgu, dy_b, True, "grad_w_o_sgu")
    part["w_out"] = _weight_grad(merged, dmix, False, "grad_w_out")
    chips1 = to_chips_start(group1, sib1, "mlp", after=(dys, part["w_o_lru"], part["w_o_sgu"], part["w_out"]))
    sib2 = to_sibling_start(group2, "mix", after=(chips1[-1],))
    du, dv, g_w_sp, st_sgu, g_b_sp_t = _sgu_bwd(proj, dys, w_sp[0], b_sp_t, ln_v_g, ln_v_b, after=(sib2[-1],))
    dyl3 = dyl.reshape(nb, seq, w_lru)
    e = _scan(a3, dyl3, True, "lru_scan_bwd").reshape(t, w_lru)
    chips2 = to_chips_start(group2, sib2, "mix", after=(e, du))
    dxl, g_w_rg_a, g_w_rg_x, st_lru = _lru_bwd(proj, hs, e, dyl, lru_w, nb, seq, after=(chips2[-1],))
    dproj = jnp.concatenate([dxl, dgl, du, dv, dga, dgb], axis=1)

    didx = jnp.reshape(dev, (1,)).astype(jnp.int32)
    early = [
        ("w_conv", st_lru[4:8]), ("b_conv", st_lru[3]), ("w_rg_a", g_w_rg_a), ("b_rg_a", st_lru[0]),
        ("w_rg_x", g_w_rg_x), ("b_rg_x", st_lru[1]), ("lru_lambda", st_lru[2]), ("w_sp", g_w_sp),
        ("b_sp", jnp.transpose(g_b_sp_t[:, :SGU_GROUPS])), ("ln_v_g", st_sgu[0]), ("ln_v_b", st_sgu[1]),
        ("ln1_g", st1[0]), ("ln1_b", st1[1]), ("ln2_g", st2[0]), ("ln2_b", st2[1]),
    ]
    pieces_e = [_rows128(v) for _, v in early]
    slab_e = jnp.concatenate(pieces_e, axis=0)
    slab_e = jnp.pad(slab_e, ((0, (-slab_e.shape[0]) % TR_EW), (0, 0)))
    small_st = _split_start([slab_e], [jax.ShapeDtypeStruct((N_DEV,) + slab_e.shape, F32)], _all_devices_copies, N_DEV - 1,
                            "small_grads_start")

    group3 = ["w_in"]
    part["w_in"] = _weight_grad(h, dproj, True, "grad_w_in", after=(small_st[-1],))
    sib3 = to_sibling_start(group3, "in")
    chips3 = to_chips_start(group3, sib3, "in")
    grad_x2, g_b_in4, pb0 = _input_grad(dproj, w_in_g, dz1, x2, modv, nb, seq, after=(chips3[-1],))
    halves12 = chips_finish(group1, chips1, "mlp", after=(grad_x2,)) + chips_finish(group2, chips2, "mix", after=(grad_x2,))
    swap12 = _split_start(halves12, [jax.ShapeDtypeStruct(hv.shape, F32) for hv in halves12], _swap_copies, len(halves12),
                          "grads_swap_start")
    grads = {}

    dmod_loc = jnp.stack([pb0[:, 1], pb0[:, 0], pb1[:, 2], pb1[:, 1], pb1[:, 0], pb2[:, 0]], axis=1)
    late = [("dmod", dmod_loc), ("b_in", g_b_in4[:, 0])]
    pieces_l = [_rows128(v) for _, v in late]
    slab_l = jnp.concatenate(pieces_l, axis=0)
    gathered = _all_gather_small(slab_l, "gather_small_grads", after=(swap12[-1],)).reshape(N_DEV, slab_l.shape[0], HEAD)
    rows_dmod = dmod_loc.size // HEAD
    dmod_all = gathered[:, :rows_dmod].reshape(N_DEV * nb, 6 * d)
    grads["b_in"] = _sum_slots(gathered[:, rows_dmod:], "grad_b_in_sum").reshape(1, -1)

    (slab_e,), (lands_e,) = _split_wait(small_st, 1, _all_devices_copies, "small_grads_wait", after=(gathered,))
    summed = _sum_devices(lands_e, slab_e, didx, "small_grad_sum")
    off = 0
    for (n, v), piece in zip(early, pieces_e):
        grads[n] = summed[off:off + v.size // HEAD].reshape(v.shape)
        off += piece.shape[0]

    mine12, theirs12 = _split_wait(swap12, len(halves12), _swap_copies, "grads_swap_wait", after=(summed,))
    (mine3,) = chips_finish(group3, chips3, "in", after=(summed,))
    (theirs3,) = _exchange([mine3], [jax.ShapeDtypeStruct(mine3.shape, F32)], _swap_copies, 1, "grads_swap_w_in")
    mine = dict(zip(group1 + group2 + group3, mine12 + [mine3]))
    theirs = dict(zip(group1 + group2 + group3, theirs12 + [theirs3]))

    dmod_cols = lax.dynamic_slice(dmod_all, (0, chip * n_ada), (N_DEV * nb, n_ada))
    grads["w_ada"], grads["b_ada"] = _ada_bwd(c_all, dmod_all, dmod_cols)
    n_wcs = w_lru // N_CHIPS
    grads["w_conv"] = lax.dynamic_slice(grads["w_conv"], (0, chip * n_wcs), (CONV_WIDTH, n_wcs))

    names = ['w_ada', 'b_ada', 'w_in', 'b_in', 'w_conv', 'b_conv', 'w_rg_a', 'b_rg_a', 'w_rg_x', 'b_rg_x', 'lru_lambda',
             'w_sp', 'b_sp', 'ln_v_g', 'ln_v_b', 'w_o_lru', 'w_o_sgu', 'w_out', 'ln1_g', 'ln1_b', 'w_up', 'w_down',
             'ln2_g', 'ln2_b']
    out_g, out_d, out_m, out_v = [], [], [], []
    for n in names:
        wv = given[n]
        shape2 = (-1, wv.shape[-1])
        w2, m2, v2 = wv.reshape(shape2), given["m_" + n].reshape(shape2), given["v_" + n].reshape(shape2)
        if n in big:
            g2, dlt, nm, nv = _adamw_halves(w2, mine[n], theirs[n], m2, v2, cidx, "adamw_" + n)
        else:
            g2 = grads[n].reshape(wv.shape).reshape(shape2)
            dlt, nm, nv = _adamw(w2, g2, m2, v2, "adamw_" + n)
        out_g.append(g2.reshape(wv.shape))
        out_d.append(dlt.reshape(wv.shape))
        out_m.append(nm.reshape(wv.shape))
        out_v.append(nv.reshape(wv.shape))

    return (loss, grad_x2.reshape(nb, seq, d), *out_g, *out_d, *out_m, *out_v)
```

```python
import functools
import math

import jax
import jax.numpy as jnp
from jax import lax
from jax.experimental import pallas as pl
from jax.experimental.pallas import tpu as pltpu

F32 = jnp.float32
BF16 = jnp.bfloat16
MESH = pl.DeviceIdType.MESH

N_CHIPS = 4
N_DEV = 8
LRU_HEADS = 10
HEAD = 128
SGU_GROUPS = 6
SGU_CHUNK = 64
CONV_WIDTH = 4
LRU_C = 8.0
ALPHA = 2.0 ** 0.25
LN_EPS = 1e-5
ADAM_LR, ADAM_B1, ADAM_B2, ADAM_EPS, ADAM_WD, ADAM_STEP = 0.001, 0.9, 0.999, 1e-08, 0.01, 10

VMEM_LIMIT = 56 * 1024 * 1024
VMEM_LIMIT_MAX = 62 * 1024 * 1024
TM_PROJ = 1024
TM_MIX = 256
TM_MLP = 512
TS_MLP = 256
TM_SGU = 512
TM_DH = 512
TT_DW = 1024
TC_SCAN = 256
TR_EW = 256


def _cp(sem=None, limit=None):
    return pltpu.CompilerParams(dimension_semantics=sem, vmem_limit_bytes=limit or VMEM_LIMIT)


def _mm(a, b):
    return jnp.dot(a.astype(BF16), b.astype(BF16), preferred_element_type=F32)


def _mm_nt(a, b):
    return lax.dot_general(a.astype(BF16), b.astype(BF16), (((1,), (1,)), ((), ())), preferred_element_type=F32)


def _mm_tn(a, b):
    return lax.dot_general(a.astype(BF16), b.astype(BF16), (((0,), (0,)), ((), ())), preferred_element_type=F32)


def _sigmoid(x):
    return 1.0 / (1.0 + jnp.exp(-x))


def _sigmoid_t(x):
    return 0.5 * jnp.tanh(0.5 * x) + 0.5


_GELU_K = math.sqrt(2.0 / math.pi)


def _gelu(x):
    t = jnp.tanh(_GELU_K * (x + 0.044715 * (x * x * x)))
    return 0.5 * x * (1.0 + t)


def _gelu_and_grad(x):
    x2 = x * x
    t = jnp.tanh(_GELU_K * (x + 0.044715 * (x2 * x)))
    g = 0.5 * x * (1.0 + t)
    dg = 0.5 * (1.0 + t) + 0.5 * x * (1.0 - t * t) * (_GELU_K * (1.0 + 3.0 * 0.044715 * x2))
    return g, dg


def _ln_stats(z):
    mu = jnp.mean(z, axis=-1, keepdims=True)
    zc = z - mu
    var = jnp.mean(zc * zc, axis=-1, keepdims=True)
    rstd = lax.rsqrt(var + LN_EPS)
    return zc * rstd, rstd


def _ln_bwd(dxh, xhat, rstd):
    m1 = jnp.mean(dxh, axis=-1, keepdims=True)
    m2 = jnp.mean(dxh * xhat, axis=-1, keepdims=True)
    return rstd * (dxh - m1 - xhat * m2)


def _colsum(v):
    return jnp.sum(v, axis=0, keepdims=True)


def _shift_down(v, j):
    if j == 0:
        return v
    rows = lax.broadcasted_iota(jnp.int32, v.shape, 0)
    return jnp.where(rows >= j, pltpu.roll(v, j, 0), 0.0)


def _shift_up(v, j):
    if j == 0:
        return v
    n = v.shape[0]
    rows = lax.broadcasted_iota(jnp.int32, v.shape, 0)
    return jnp.where(rows < n - j, pltpu.roll(v, n - j, 0), 0.0)


def _load_weights(srcs, dsts, sems):
    cps = [pltpu.make_async_copy(s, dd, sems.at[k]) for k, (s, dd) in enumerate(zip(srcs, dsts))]
    for cp in cps:
        cp.start()
    for cp in cps:
        cp.wait()


def _my_pos():
    return lax.axis_index("x"), lax.axis_index("y"), lax.axis_index("c")


def _all_gather_small(v, name, after=()):
    m_per, n = v.shape

    def body(x_ref, out_ref, send_sems, recv_sems, local_sem):
        x, y, c = _my_pos()
        me, sibling = (x, y, c), (x, y, 1 - c)
        chips = [(1 - x, y), (x, 1 - y), (1 - x, 1 - y)]

        def rows(px, py, pc):
            return out_ref.at[pl.ds((4 * px + 2 * py + pc) * m_per, m_per), :]

        def copy(k, block, to, src=None):
            return pltpu.make_async_remote_copy(
                src_ref=rows(*block) if src is None else src, dst_ref=rows(*block),
                send_sem=send_sems.at[k], recv_sem=recv_sems.at[k], device_id=to, device_id_type=MESH)

        mine = pltpu.make_async_copy(x_ref, rows(*me), local_sem)
        mine.start()
        first = [copy(0, me, sibling, src=x_ref)]
        first += [copy(1 + j, me, (*chip, c), src=x_ref) for j, chip in enumerate(chips)]
        for cp in first:
            cp.start()
        passed = [copy(4 + j, (*chip, c), sibling) for j, chip in enumerate(chips)]
        for j, chip in enumerate(chips):
            copy(1 + j, (*chip, c), me).wait_recv()
            passed[j].start()
        copy(0, sibling, me).wait_recv()
        for j, chip in enumerate(chips):
            copy(4 + j, (*chip, 1 - c), me).wait_recv()
        for cp in first + passed:
            cp.wait_send()
        mine.wait()

    return pl.pallas_call(
        _ordered(body, 1, after), name=name,
        out_shape=jax.ShapeDtypeStruct((N_DEV * m_per, n), v.dtype),
        in_specs=[pl.BlockSpec(memory_space=pltpu.VMEM)] + [pl.BlockSpec(memory_space=pl.ANY)] * len(after),
        out_specs=pl.BlockSpec(memory_space=pltpu.VMEM),
        scratch_shapes=[pltpu.SemaphoreType.DMA((7,)), pltpu.SemaphoreType.DMA((7,)), pltpu.SemaphoreType.DMA],
        compiler_params=pltpu.CompilerParams(vmem_limit_bytes=VMEM_LIMIT),
    )(v, *after)


_HBM = pl.BlockSpec(memory_space=pltpu.HBM)
_ANY = pl.BlockSpec(memory_space=pl.ANY)
_SEM = pl.BlockSpec(memory_space=pltpu.SEMAPHORE)
_EFFECT = pltpu.SideEffectType.DATAFLOW_SIDE_EFFECTING


def _ordered(body, n_in, after):
    k = len(after)
    if not k:
        return body
    return lambda *refs: body(*refs[:n_in], *refs[n_in + k:])


def _gather_copies(ins, lands, send_sems, recv_sems):
    x, y, c = _my_pos()
    p = 2 * x + y
    peers = [(x, 1 - y), (1 - x, y), (1 - x, 1 - y)]
    sends, recvs = [], []
    for k in range(len(ins)):
        for j, (qx, qy) in enumerate(peers):
            sems = dict(send_sem=send_sems.at[3 * k + j], recv_sem=recv_sems.at[3 * k + j],
                        device_id=(qx, qy, c), device_id_type=MESH)
            sends.append(pltpu.make_async_remote_copy(src_ref=ins[k], dst_ref=lands[k].at[p], **sems))
            recvs.append(pltpu.make_async_remote_copy(src_ref=ins[k], dst_ref=lands[k].at[2 * qx + qy], **sems))
    return sends, recvs


def _peer_gather_copies(peers):
    def copies(ins, lands, send_sems, recv_sems):
        x, y, c = _my_pos()
        where = [(x, 1 - y), (1 - x, y), (1 - x, 1 - y)]
        cps = [pltpu.make_async_remote_copy(
            src_ref=ins[0], dst_ref=lands[j], send_sem=send_sems.at[j], recv_sem=recv_sems.at[j],
            device_id=(*where[j], c), device_id_type=MESH) for j in peers]
        return cps, cps
    return copies


def _to_sibling_copies(ins, lands, send_sems, recv_sems):
    x, y, c = _my_pos()
    cps = [pltpu.make_async_remote_copy(
        src_ref=ins[k].at[:, 1 - c], dst_ref=lands[k], send_sem=send_sems.at[k], recv_sem=recv_sems.at[k],
        device_id=(x, y, 1 - c), device_id_type=MESH) for k in range(len(ins))]
    return cps, cps


def _chip_exchange_copies(ins, lands, send_sems, recv_sems):
    x, y, c = _my_pos()
    peers = [(x, 1 - y), (1 - x, y), (1 - x, 1 - y)]
    cps = []
    for k in range(len(ins)):
        for j, (qx, qy) in enumerate(peers):
            cps.append(pltpu.make_async_remote_copy(
                src_ref=ins[k].at[2 * qx + qy], dst_ref=lands[k].at[j], send_sem=send_sems.at[3 * k + j],
                recv_sem=recv_sems.at[3 * k + j], device_id=(qx, qy, c), device_id_type=MESH))
    return cps, cps


def _all_devices_copies(ins, lands, send_sems, recv_sems):
    x, y, c = _my_pos()
    me = 4 * x + 2 * y + c
    sends, recvs = [], []
    for r in range(1, N_DEV):
        px = 1 - x if r & 4 else x
        py = 1 - y if r & 2 else y
        pc = 1 - c if r & 1 else c
        sems = dict(send_sem=send_sems.at[r - 1], recv_sem=recv_sems.at[r - 1], device_id=(px, py, pc), device_id_type=MESH)
        sends.append(pltpu.make_async_remote_copy(src_ref=ins[0], dst_ref=lands[0].at[me], **sems))
        recvs.append(pltpu.make_async_remote_copy(src_ref=ins[0], dst_ref=lands[0].at[4 * px + 2 * py + pc], **sems))
    return sends, recvs


def _swap_copies(ins, lands, send_sems, recv_sems):
    x, y, c = _my_pos()
    cps = [pltpu.make_async_remote_copy(
        src_ref=ins[k], dst_ref=lands[k], send_sem=send_sems.at[k], recv_sem=recv_sems.at[k],
        device_id=(x, y, 1 - c), device_id_type=MESH) for k in range(len(ins))]
    return cps, cps


def _split_start(ins, land_shapes, copies, n_sems, name, after=()):
    n, nl = len(ins), len(land_shapes)
    first_out = n + nl + len(after)

    def body(*refs):
        in_refs, land_refs = refs[:n], refs[n:n + nl]
        send_sems, recv_sems = refs[first_out:first_out + 2]
        token = refs[-1]
        sends, _ = copies(in_refs, land_refs, send_sems, recv_sems)
        for cp in sends:
            cp.start()
        token[...] = jnp.zeros_like(token)

    lands = [pltpu.with_memory_space_constraint(lax.empty(s.shape, s.dtype), pltpu.HBM) for s in land_shapes]
    ins = [pltpu.with_memory_space_constraint(s, pltpu.HBM) for s in ins]
    return pl.pallas_call(
        body, name=name,
        out_shape=(pltpu.SemaphoreType.DMA((n_sems,)), pltpu.SemaphoreType.DMA((n_sems,)),
                   *[pltpu.HBM(s.shape, s.dtype) for s in ins], *[pltpu.HBM(s.shape, s.dtype) for s in lands],
                   jax.ShapeDtypeStruct((8, HEAD), F32)),
        in_specs=[_HBM] * (n + nl) + [pl.BlockSpec(memory_space=pl.ANY)] * len(after),
        out_specs=(_SEM, _SEM, *([_HBM] * (n + nl)), pl.BlockSpec(memory_space=pltpu.VMEM)),
        input_output_aliases={k: 2 + k for k in range(n + nl)},
        compiler_params=pltpu.CompilerParams(has_side_effects=_EFFECT),
    )(*ins, *lands, *after)


def _split_wait(started, n, copies, name, after=()):
    send_sems, recv_sems = started[0], started[1]
    bufs = started[2:-1]
    nb = len(bufs)

    def body(*refs):
        in_refs, land_refs = refs[:n], refs[n:nb]
        sends, recvs = copies(in_refs, land_refs, refs[nb], refs[nb + 1])
        for cp in sends:
            cp.wait_send()
        for cp in recvs:
            cp.wait_recv()

    outs = pl.pallas_call(
        body, name=name,
        out_shape=tuple(pltpu.HBM(s.shape, s.dtype) for s in bufs),
        in_specs=[_HBM] * nb + [_SEM, _SEM] + [pl.BlockSpec(memory_space=pl.ANY)] * len(after),
        out_specs=tuple([_HBM] * nb),
        input_output_aliases={k: k for k in range(nb)},
        compiler_params=pltpu.CompilerParams(has_side_effects=_EFFECT),
    )(*bufs, send_sems, recv_sems, *after)
    return list(outs[:n]), list(outs[n:])


def _fill_own_slot(gathered, shards, pidx, names):
    outs = []
    for g, s, name in zip(gathered, shards, names):
        r, cdim = s.shape
        tr = _row_tile(r)

        def body(p_ref, s_ref, g_ref, o_ref):
            o_ref[...] = s_ref[...]

        outs.append(pl.pallas_call(
            body, name=name,
            grid_spec=pltpu.PrefetchScalarGridSpec(
                num_scalar_prefetch=1, grid=(r // tr,),
                in_specs=[pl.BlockSpec((tr, cdim), lambda i, p: (i, 0)), pl.BlockSpec(memory_space=pl.ANY)],
                out_specs=pl.BlockSpec((None, tr, cdim), lambda i, p: (p[0], i, 0))),
            out_shape=jax.ShapeDtypeStruct(g.shape, g.dtype),
            input_output_aliases={2: 0},
            compiler_params=_cp(("arbitrary",)),
        )(pidx, s, g))
    return outs


def _sum_own_and_peers(own4, slots, pidx, name):
    _, rh, cdim = own4.shape
    tr = _row_tile(rh)

    def body(p_ref, own_ref, s_ref, o_ref):
        acc = own_ref[...].astype(F32)
        for j in range(3):
            acc = acc + s_ref[j].astype(F32)
        o_ref[...] = acc

    return pl.pallas_call(
        body, name=name,
        grid_spec=pltpu.PrefetchScalarGridSpec(
            num_scalar_prefetch=1, grid=(rh // tr,),
            in_specs=[pl.BlockSpec((None, tr, cdim), lambda i, p: (p[0], i, 0)),
                      pl.BlockSpec((3, tr, cdim), lambda i, p: (0, i, 0))],
            out_specs=pl.BlockSpec((tr, cdim), lambda i, p: (i, 0))),
        out_shape=jax.ShapeDtypeStruct((rh, cdim), F32),
        compiler_params=_cp(("arbitrary",)),
    )(pidx, own4, slots)


def _exchange(ins, land_shapes, copies, n_sems, name):
    n, nl = len(ins), len(land_shapes)

    def body(*refs):
        sends, recvs = copies(refs[:n], refs[n:n + nl], refs[n + nl], refs[n + nl + 1])
        for cp in sends:
            cp.start()
        for cp in sends:
            cp.wait_send()
        for cp in recvs:
            cp.wait_recv()

    any_spec = pl.BlockSpec(memory_space=pl.ANY)
    return pl.pallas_call(
        body, name=name,
        out_shape=[jax.ShapeDtypeStruct(s.shape, s.dtype) for s in land_shapes],
        in_specs=[any_spec] * n, out_specs=[any_spec] * nl,
        scratch_shapes=[pltpu.SemaphoreType.DMA((n_sems,)), pltpu.SemaphoreType.DMA((n_sems,))],
    )(*ins)


def _row_tile(r):
    t = min(TR_EW, r)
    while r % t:
        t //= 2
    return t


def _add_own_half(g4, recv, cidx, name):
    _, _, rh, cdim = g4.shape
    tr = _row_tile(rh)

    def body(c_ref, a_ref, b_ref, o_ref):
        o_ref[...] = (a_ref[...] + b_ref[...]).astype(BF16)

    return pl.pallas_call(
        body, name=name,
        grid_spec=pltpu.PrefetchScalarGridSpec(
            num_scalar_prefetch=1, grid=(N_CHIPS, rh // tr),
            in_specs=[pl.BlockSpec((None, None, tr, cdim), lambda q, i, c: (q, c[0], i, 0)),
                      pl.BlockSpec((None, tr, cdim), lambda q, i, c: (q, i, 0))],
            out_specs=pl.BlockSpec((None, tr, cdim), lambda q, i, c: (q, i, 0))),
        out_shape=jax.ShapeDtypeStruct(recv.shape, BF16),
        compiler_params=_cp(("arbitrary", "arbitrary")),
    )(cidx, g4, recv)


def _sum_slots(v, name):
    n, r, cdim = v.shape
    tr = _row_tile(r)

    def body(v_ref, o_ref):
        acc = v_ref[0].astype(F32)
        for k in range(1, n):
            acc = acc + v_ref[k].astype(F32)
        o_ref[...] = acc

    return pl.pallas_call(
        body, name=name, grid=(r // tr,),
        in_specs=[pl.BlockSpec((n, tr, cdim), lambda i: (0, i, 0))],
        out_specs=pl.BlockSpec((tr, cdim), lambda i: (i, 0)),
        out_shape=jax.ShapeDtypeStruct((r, cdim), F32),
        compiler_params=_cp(("arbitrary",)),
    )(v)


def _sum_devices(lands, own, didx, name):
    _, r, cdim = lands.shape
    tr = _row_tile(r)

    def body(d_ref, l_ref, own_ref, o_ref):
        acc = jnp.where(d_ref[0] == 0, own_ref[...], l_ref[0])
        for dv in range(1, N_DEV):
            acc = acc + jnp.where(d_ref[0] == dv, own_ref[...], l_ref[dv])
        o_ref[...] = acc

    return pl.pallas_call(
        body, name=name,
        grid_spec=pltpu.PrefetchScalarGridSpec(
            num_scalar_prefetch=1, grid=(r // tr,),
            in_specs=[pl.BlockSpec((N_DEV, tr, cdim), lambda i, dd: (0, i, 0)), pl.BlockSpec((tr, cdim), lambda i, dd: (i, 0))],
            out_specs=pl.BlockSpec((tr, cdim), lambda i, dd: (i, 0))),
        out_shape=jax.ShapeDtypeStruct((r, cdim), F32),
        compiler_params=_cp(("arbitrary",)),
    )(didx, lands, own)


def _adamw_math(wv, gg, mv, vv):
    nm = ADAM_B1 * mv + (1.0 - ADAM_B1) * gg
    nv = ADAM_B2 * vv + (1.0 - ADAM_B2) * (gg * gg)
    m_hat = nm / (1.0 - ADAM_B1 ** ADAM_STEP)
    v_hat = nv / (1.0 - ADAM_B2 ** ADAM_STEP)
    return -ADAM_LR * (m_hat / (jnp.sqrt(v_hat) + ADAM_EPS) + ADAM_WD * wv), nm, nv


def _adamw_halves(w, mine, theirs, m, v, cidx, name):
    r, cdim = w.shape
    rh = r // 2
    tr = _row_tile(rh)
    nblk = rh // tr

    def body(c_ref, w_ref, a_ref, b_ref, m_ref, v_ref, g_ref, d_ref, nm_ref, nv_ref):
        gg = jnp.where(pl.program_id(0) == c_ref[0], a_ref[...], b_ref[...])
        g_ref[...] = gg
        d_ref[...], nm_ref[...], nv_ref[...] = _adamw_math(w_ref[...], gg, m_ref[...], v_ref[...])

    full = pl.BlockSpec((tr, cdim), lambda hh, i, c: (hh * nblk + i, 0))
    half = pl.BlockSpec((tr, cdim), lambda hh, i, c: (i, 0))
    return pl.pallas_call(
        body, name=name,
        grid_spec=pltpu.PrefetchScalarGridSpec(
            num_scalar_prefetch=1, grid=(2, nblk),
            in_specs=[full, half, half, full, full], out_specs=[full] * 4),
        out_shape=[jax.ShapeDtypeStruct((r, cdim), F32)] * 4,
        compiler_params=_cp(("arbitrary", "arbitrary")),
    )(cidx, w, mine, theirs, m, v)


def _adamw(w, g, m, v, name):
    r, cdim = w.shape
    tr = _row_tile(r) if r % 8 == 0 else r

    def body(w_ref, g_ref, m_ref, v_ref, d_ref, nm_ref, nv_ref):
        d_ref[...], nm_ref[...], nv_ref[...] = _adamw_math(w_ref[...], g_ref[...], m_ref[...], v_ref[...])

    spec = pl.BlockSpec((tr, cdim), lambda i: (i, 0))
    return pl.pallas_call(
        body, name=name, grid=(r // tr,), in_specs=[spec] * 4, out_specs=[spec] * 3,
        out_shape=[jax.ShapeDtypeStruct((r, cdim), F32)] * 3,
        compiler_params=_cp(("arbitrary",)),
    )(w, g, m, v)


def _ada_fwd(c_all, w_ada, b_cols):
    nb, _ = c_all.shape
    n = w_ada.shape[1]

    def body(c_ref, w_ref, b_ref, o_ref):
        cv = c_ref[...]
        o_ref[...] = _mm(cv * _sigmoid(cv), w_ref[...]) + b_ref[...]

    return pl.pallas_call(
        body, name="ada_fwd", out_shape=jax.ShapeDtypeStruct((nb, n), F32),
        compiler_params=pltpu.CompilerParams(vmem_limit_bytes=VMEM_LIMIT),
    )(c_all, w_ada, b_cols)


def _ada_bwd(c_all, dmod_all, dmod_cols):
    d = c_all.shape[1]
    n = dmod_cols.shape[1]

    def body(c_ref, da_ref, dc_ref, gw_ref, gb_ref):
        cv = c_ref[...]
        gw_ref[...] = _mm_tn(cv * _sigmoid(cv), dc_ref[...])
        gb_ref[...] = _colsum(da_ref[...])

    return pl.pallas_call(
        body, name="ada_bwd",
        out_shape=[jax.ShapeDtypeStruct((d, n), F32), jax.ShapeDtypeStruct((1, dmod_all.shape[1]), F32)],
        compiler_params=pltpu.CompilerParams(vmem_limit_bytes=VMEM_LIMIT),
    )(c_all, dmod_all, dmod_cols)


def _proj_fwd(x2, modv, ws, cols, b_in, seq, name, proj_in=None):
    t, d = x2.shape
    n = len(ws)
    ns = ws[0].shape[1]
    tm = min(TM_PROJ, seq)
    tpb = seq // tm
    first = proj_in is None

    def body(c_ref, x_ref, mod_ref, *refs):
        w_refs, b_ref = refs[:n], refs[n]
        outs = refs[n + 1 if first else n + 2:]
        proj_ref, h_s = outs[0], outs[-1]
        s = pl.program_id(1)

        @pl.when(s == 0)
        def _():
            h = (x_ref[...] * (1.0 + mod_ref[1:2, :]) + mod_ref[0:1, :]).astype(BF16)
            h_s[...] = h
            if first:
                outs[1][...] = h

        for k in range(n):
            @pl.when(s == k)
            def _():
                proj_ref[...] = (jnp.dot(h_s[...], w_refs[k][...], preferred_element_type=F32) + b_ref[...]).astype(BF16)

    in_specs = [pl.BlockSpec((tm, d), lambda i, s, c: (i, 0)),
                pl.BlockSpec((None, 8, d), lambda i, s, c: (i // tpb, 0, 0))]
    in_specs += [pl.BlockSpec((d, ns), lambda i, s, c: (0, 0))] * n
    in_specs += [pl.BlockSpec((1, ns), lambda i, s, c: (0, c[s]))]
    out_specs = [pl.BlockSpec((tm, ns), lambda i, s, c: (i, c[s]))]
    out_shape = [jax.ShapeDtypeStruct((t, N_CHIPS * ns), BF16)]
    args = [cols, x2, modv, *ws, b_in]
    aliases = {}
    if first:
        out_specs.append(pl.BlockSpec((tm, d), lambda i, s, c: (i, 0)))
        out_shape.append(jax.ShapeDtypeStruct((t, d), BF16))
    else:
        in_specs.append(_ANY)
        args.append(proj_in)
        aliases = {len(args) - 1: 0}
    return pl.pallas_call(
        body, name=name,
        grid_spec=pltpu.PrefetchScalarGridSpec(
            num_scalar_prefetch=1, grid=(t // tm, n), in_specs=in_specs, out_specs=out_specs,
            scratch_shapes=[pltpu.VMEM((tm, d), BF16)]),
        out_shape=out_shape, input_output_aliases=aliases,
        compiler_params=_cp(("arbitrary", "arbitrary")),
    )(*args)


def _lru_gates(xl, wc_ref, bc_ref, wa_ref, ba_ref, wx_ref, bx_ref, lam_ref):
    xc = bc_ref[...] + wc_ref[CONV_WIDTH - 1:CONV_WIDTH, :] * xl
    for k in range(CONV_WIDTH - 1):
        xc = xc + wc_ref[k:k + 1, :] * _shift_down(xl, CONV_WIDTH - 1 - k)
    r = _sigmoid(_mm(xc, wa_ref[...]) + ba_ref[...])
    gi = _sigmoid_t(_mm(xc, wx_ref[...]) + bx_ref[...])
    nl = -lam_ref[...]
    e = jnp.exp(-jnp.abs(nl))
    u = 1.0 + e
    dlt = u - 1.0
    log1p_e = jnp.where(dlt == 0.0, e, jnp.log(u) * (e / jnp.where(dlt == 0.0, 1.0, dlt)))
    big_l = -LRU_C * (jnp.maximum(nl, 0.0) + log1p_e)
    la = big_l * r
    a = jnp.exp(la)
    m2 = jnp.tanh(-la) * (a * a + 1.0)
    return xc, r, gi, big_l, a, m2


def _lru_prep(proj, lru_w, nb, seq):
    t = proj.shape[0]
    w = LRU_HEADS * HEAD
    w_conv, b_conv, w_a, b_a, w_x, b_x, lam = lru_w

    def body(x_ref, wc_ref, bc_ref, wa_ref, ba_ref, wx_ref, bx_ref, lam_ref, a_ref, inp_ref):
        xc, r, gi, big_l, a, m2 = _lru_gates(x_ref[...].astype(F32), wc_ref, bc_ref, wa_ref, ba_ref, wx_ref, bx_ref, lam_ref)
        a_ref[...] = a
        inp_ref[...] = jnp.sqrt(m2) * (gi * xc)

    col = lambda b, hd: (0, hd)
    head = lambda b, hd: (hd, 0, 0)
    tok = lambda b, hd: (b, hd)
    return pl.pallas_call(
        body, name="lru_prep", grid=(nb, LRU_HEADS),
        in_specs=[pl.BlockSpec((seq, HEAD), tok),
                  pl.BlockSpec((CONV_WIDTH, HEAD), col), pl.BlockSpec((1, HEAD), col),
                  pl.BlockSpec((None, HEAD, HEAD), head), pl.BlockSpec((1, HEAD), col),
                  pl.BlockSpec((None, HEAD, HEAD), head), pl.BlockSpec((1, HEAD), col),
                  pl.BlockSpec((1, HEAD), col)],
        out_specs=[pl.BlockSpec((seq, HEAD), tok)] * 2,
        out_shape=[jax.ShapeDtypeStruct((t, w), F32)] * 2,
        compiler_params=_cp(("arbitrary", "arbitrary")),
    )(proj, w_conv, b_conv, w_a, b_a, w_x, b_x, lam)


def _scan(a3, b3, reverse, name):
    nb, seq, w = a3.shape
    tc = min(TC_SCAN, seq)
    nchunk = seq // tc
    ntile = tc // 8

    def combine(av, bv):
        rows = lax.broadcasted_iota(jnp.int32, av.shape, 0)
        for s in (1, 2, 4):
            if reverse:
                keep = rows < 8 - s
                a_sh, b_sh = pltpu.roll(av, 8 - s, 0), pltpu.roll(bv, 8 - s, 0)
            else:
                keep = rows >= s
                a_sh, b_sh = pltpu.roll(av, s, 0), pltpu.roll(bv, s, 0)
            bv = jnp.where(keep, bv + av * b_sh, bv)
            av = jnp.where(keep, av * a_sh, av)
        return av, bv

    def body(a_ref, b_ref, h_ref, carry):
        @pl.when(pl.program_id(0) == 0)
        def _():
            carry[...] = jnp.zeros_like(carry)

        for b in range(nb):
            def tile(j, hprev):
                jj = ntile - 1 - j if reverse else j
                base = pl.multiple_of(jj * 8, 8)
                av, bv = a_ref[b, pl.ds(base, 8), :], b_ref[b, pl.ds(base, 8), :]
                av, bv = combine(av, av * bv if reverse else bv)
                h = bv + av * hprev
                h_ref[b, pl.ds(base, 8), :] = h
                edge = h[0:1, :] if reverse else h[7:8, :]
                return jnp.broadcast_to(edge, (8, w))

            carry[b] = lax.fori_loop(0, ntile, tile, carry[b])

    imap = (lambda i: (0, nchunk - 1 - i, 0)) if reverse else (lambda i: (0, i, 0))
    spec = pl.BlockSpec((nb, tc, w), imap)
    return pl.pallas_call(
        body, name=name, grid=(nchunk,), in_specs=[spec, spec], out_specs=spec,
        out_shape=jax.ShapeDtypeStruct((nb, seq, w), F32),
        scratch_shapes=[pltpu.VMEM((nb, 8, w), F32)],
        compiler_params=_cp(("arbitrary",)),
    )(a3, b3)


def _sgu_mask():
    ti = lax.broadcasted_iota(jnp.int32, (HEAD, HEAD), 0) // SGU_CHUNK
    si = lax.broadcasted_iota(jnp.int32, (HEAD, HEAD), 1) // SGU_CHUNK
    return si <= ti


def _sgu_specs(tm, d_sgu):
    pw = 256
    first_u = (2 * LRU_HEADS * HEAD) // pw
    n_piece = d_sgu // pw
    specs = [pl.BlockSpec((tm, pw), functools.partial(lambda i, k: (i, k), k=first_u + j)) for j in range(2 * n_piece)]
    return specs, n_piece


def _sgu_fwd(proj, w_sp, b_sp_t, ln_g, ln_b):
    t = proj.shape[0]
    d_sgu = SGU_GROUPS * HEAD
    tm = min(TM_SGU, t)
    nblk = tm // HEAD
    specs, n_piece = _sgu_specs(tm, d_sgu)

    def body(*refs):
        u = jnp.concatenate([r[...] for r in refs[:n_piece]], axis=1).astype(F32)
        v = jnp.concatenate([r[...] for r in refs[n_piece:2 * n_piece]], axis=1).astype(F32)
        w_ref, bt_ref, g_ref, b_ref, y_ref = refs[2 * n_piece:]
        ug = _gelu(u)
        xhat, _ = _ln_stats(_gelu(v))
        vn = (xhat * g_ref[...] + b_ref[...]).astype(BF16)
        mask = _sgu_mask()
        for g in range(SGU_GROUPS):
            wm = jnp.where(mask, w_ref[g], 0.0).astype(BF16)
            cols = slice(g * HEAD, (g + 1) * HEAD)
            for n in range(nblk):
                rows = slice(n * HEAD, (n + 1) * HEAD)
                mixed = jnp.dot(wm, vn[rows, cols], preferred_element_type=F32) + bt_ref[:, g:g + 1]
                y_ref[rows, cols] = (ug[rows, cols] * mixed).astype(BF16)

    full = lambda shape: pl.BlockSpec(shape, lambda i: (0,) * len(shape))
    return pl.pallas_call(
        body, name="sgu_fwd", grid=(t // tm,),
        in_specs=specs + [full(w_sp.shape), full(b_sp_t.shape), full(ln_g.shape), full(ln_b.shape)],
        out_specs=pl.BlockSpec((tm, d_sgu), lambda i: (i, 0)),
        out_shape=jax.ShapeDtypeStruct((t, d_sgu), BF16),
        compiler_params=_cp(("arbitrary",)),
    )(*([proj] * (2 * n_piece)), w_sp, b_sp_t, ln_g, ln_b)


def _mix_fwd(hs, proj, y_sgu, x2, modv, w_o_lru_g, w_o_sgu_g, w_out_g, ln1_g, ln1_b, seq):
    t, d = x2.shape
    w = hs.shape[1]
    d_sgu = y_sgu.shape[1]
    nq, _, ns = w_o_sgu_g.shape
    tm = min(TM_MIX, seq)
    tpb = seq // tm

    def body(hs_ref, gl_ref, ys_ref, ga_ref, gb_ref, x_ref, mod_ref, wl_ref, ws_ref, wo_ref, g1_ref, b1_ref,
             yap_ref, ya_ref, yb_ref, mg_ref, mix_ref, x1_ref):
        yap = (hs_ref[...] * _gelu(gl_ref[...].astype(F32))).astype(BF16)
        yap_ref[...] = yap
        y_a = jnp.dot(yap, wl_ref[...], preferred_element_type=F32)
        ys = ys_ref[...]
        y_b = jnp.concatenate([jnp.dot(ys, ws_ref[q], preferred_element_type=F32) for q in range(nq)], axis=1)
        ya_ref[...] = y_a.astype(BF16)
        yb_ref[...] = y_b.astype(BF16)
        merged = (_sigmoid_t(ga_ref[...].astype(F32)) * y_a + _sigmoid_t(gb_ref[...].astype(F32)) * y_b).astype(BF16)
        mg_ref[...] = merged
        mix = jnp.dot(merged, wo_ref[...], preferred_element_type=F32)
        mix_ref[...] = mix
        xhat, _ = _ln_stats(ALPHA * x_ref[...] + (1.0 + mod_ref[2:3, :]) * mix)
        x1_ref[...] = xhat * g1_ref[...] + b1_ref[...]

    row = lambda width, col: pl.BlockSpec((tm, width), functools.partial(lambda i, k: (i, k), k=col))
    full = lambda shape: pl.BlockSpec(shape, lambda i: (0,) * len(shape))
    return pl.pallas_call(
        body, name="mix_fwd", grid=(t // tm,),
        in_specs=[row(w, 0), row(w, 1), row(d_sgu, 0), row(d, 4), row(d, 5), row(d, 0),
                  pl.BlockSpec((None, 8, d), lambda i: (i // tpb, 0, 0)),
                  full(w_o_lru_g.shape), full(w_o_sgu_g.shape), full(w_out_g.shape), full(ln1_g.shape), full(ln1_b.shape)],
        out_specs=[row(w, 0), row(d, 0), row(d, 0), row(d, 0), row(d, 0), row(d, 0)],
        out_shape=[jax.ShapeDtypeStruct((t, w), BF16), jax.ShapeDtypeStruct((t, d), BF16),
                   jax.ShapeDtypeStruct((t, d), BF16), jax.ShapeDtypeStruct((t, d), BF16),
                   jax.ShapeDtypeStruct((t, d), F32), jax.ShapeDtypeStruct((t, d), F32)],
        compiler_params=_cp(("arbitrary",)),
    )(hs, proj, y_sgu, proj, proj, x2, modv, w_o_lru_g, w_o_sgu_g, w_out_g, ln1_g, ln1_b)


def _mlp_fwd(x1, modv, w_up_g, w_down_g, ln2_g, ln2_b, target, nb, seq):
    t, d = x1.shape
    nq, _, ns = w_up_g.shape
    tm = min(TM_MLP, seq)
    ts = min(TS_MLP, tm)
    tpb = seq // tm

    def body(x1_ref, mod_ref, wu_hbm, wd_hbm, g2_ref, b2_ref, tg_ref,
             rl_ref, act_ref, h2_ref, dz2_ref, df_ref, st_ref, pb_ref, wu_s, wd_s, acc, sems):
        i = pl.program_id(0)

        @pl.when(i == 0)
        def _():
            _load_weights((wu_hbm, wd_hbm), (wu_s, wd_s), sems)
            st_ref[...] = jnp.zeros_like(st_ref)

        @pl.when(i % tpb == 0)
        def _():
            pb_ref[...] = jnp.zeros_like(pb_ref)

        for sub in range(tm // ts):
            rows = slice(sub * ts, (sub + 1) * ts)
            x1v = x1_ref[rows, :]
            h2 = (x1v * (1.0 + mod_ref[4:5, :]) + mod_ref[3:4, :]).astype(BF16)
            h2_ref[rows, :] = h2
            for k in range(nq):
                cols = slice(k * ns, (k + 1) * ns)
                r = jnp.maximum(jnp.dot(h2, wu_s[k], preferred_element_type=F32), 0.0)
                act = (r * r).astype(BF16)
                rl_ref[rows, cols] = r.astype(BF16)
                act_ref[rows, cols] = act
                part = jnp.dot(act, wd_s[cols, :], preferred_element_type=F32)
                if k == 0:
                    acc[sub] = part
                else:
                    acc[sub] += part
            f = acc[sub]
            xhat, rstd = _ln_stats(ALPHA * x1v + (1.0 + mod_ref[5:6, :]) * f)
            y = xhat * g2_ref[...] + b2_ref[...]
            err = y - tg_ref[rows, :]
            dy = err * (1.0 / d)
            dz2 = _ln_bwd(dy * g2_ref[...], xhat, rstd)
            dz2_ref[rows, :] = dz2
            df_ref[rows, :] = ((1.0 + mod_ref[5:6, :]) * dz2).astype(BF16)
            st_ref[0:1, :] += _colsum(dy * xhat)
            st_ref[1:2, :] += _colsum(dy)
            st_ref[2:3, :] += (0.5 / d) * jnp.sum(_colsum(err * err), axis=1, keepdims=True)
            pb_ref[0:1, :] += _colsum(dz2 * f)

    tok = lambda i: (i, 0)
    return pl.pallas_call(
        body, name="mlp_fwd", grid=(t // tm,),
        in_specs=[pl.BlockSpec((tm, d), tok), pl.BlockSpec((None, 8, d), lambda i: (i // tpb, 0, 0)), _ANY, _ANY,
                  pl.BlockSpec((1, d), lambda i: (0, 0)), pl.BlockSpec((1, d), lambda i: (0, 0)),
                  pl.BlockSpec((tm, d), tok)],
        out_specs=[pl.BlockSpec((tm, nq * ns), tok), pl.BlockSpec((tm, nq * ns), tok),
                   pl.BlockSpec((tm, d), tok), pl.BlockSpec((tm, d), tok), pl.BlockSpec((tm, d), tok),
                   pl.BlockSpec((8, d), lambda i: (0, 0)), pl.BlockSpec((None, 8, d), lambda i: (i // tpb, 0, 0))],
        out_shape=[jax.ShapeDtypeStruct((t, nq * ns), BF16), jax.ShapeDtypeStruct((t, nq * ns), BF16),
                   jax.ShapeDtypeStruct((t, d), BF16),
                   jax.ShapeDtypeStruct((t, d), F32), jax.ShapeDtypeStruct((t, d), BF16),
                   jax.ShapeDtypeStruct((8, d), F32), jax.ShapeDtypeStruct((nb, 8, d), F32)],
        scratch_shapes=[pltpu.VMEM(w_up_g.shape, BF16), pltpu.VMEM(w_down_g.shape, BF16),
                        pltpu.VMEM((tm // ts, ts, d), F32), pltpu.SemaphoreType.DMA((2,))],
        compiler_params=_cp(("arbitrary",)),
    )(x1, modv, w_up_g, w_down_g, ln2_g, ln2_b, target)


def _mlp_bwd(df, up, w_down_g, w_up_g, dz2, x2, mix, modv, ln1_g, ln1_b, nb, seq):
    t, d = x2.shape
    nq, _, ns = w_up_g.shape
    tm = min(TM_MLP, seq)
    ts = min(TS_MLP, tm)
    tpb = seq // tm

    def body(df_ref, rl_ref, wd_hbm, wu_hbm, dz2_ref, x_ref, mix_ref, mod_ref, g1_ref, b1_ref,
             dup_ref, dz1_ref, dmix_ref, st_ref, pb_ref, wd_s, wu_s, acc, sems):
        i = pl.program_id(0)

        @pl.when(i == 0)
        def _():
            _load_weights((wd_hbm, wu_hbm), (wd_s, wu_s), sems)
            st_ref[...] = jnp.zeros_like(st_ref)

        @pl.when(i % tpb == 0)
        def _():
            pb_ref[...] = jnp.zeros_like(pb_ref)

        for sub in range(tm // ts):
            rows = slice(sub * ts, (sub + 1) * ts)
            dfv = df_ref[rows, :]
            for k in range(nq):
                cols = slice(k * ns, (k + 1) * ns)
                dup = (_mm_nt(dfv, wd_s[cols, :]) * (2.0 * rl_ref[rows, cols].astype(F32))).astype(BF16)
                dup_ref[rows, cols] = dup
                part = _mm_nt(dup, wu_s[k])
                if k == 0:
                    acc[sub] = part
                else:
                    acc[sub] += part
            dh2 = acc[sub]
            mix = mix_ref[rows, :]
            xhat, rstd = _ln_stats(ALPHA * x_ref[rows, :] + (1.0 + mod_ref[2:3, :]) * mix)
            x1 = xhat * g1_ref[...] + b1_ref[...]
            dx1 = ALPHA * dz2_ref[rows, :] + dh2 * (1.0 + mod_ref[4:5, :])
            dz1 = _ln_bwd(dx1 * g1_ref[...], xhat, rstd)
            dz1_ref[rows, :] = dz1
            dmix_ref[rows, :] = ((1.0 + mod_ref[2:3, :]) * dz1).astype(BF16)
            st_ref[0:1, :] += _colsum(dx1 * xhat)
            st_ref[1:2, :] += _colsum(dx1)
            pb_ref[0:1, :] += _colsum(dh2 * x1)
            pb_ref[1:2, :] += _colsum(dh2)
            pb_ref[2:3, :] += _colsum(dz1 * mix)

    tok = lambda i: (i, 0)
    return pl.pallas_call(
        body, name="mlp_bwd", grid=(t // tm,),
        in_specs=[pl.BlockSpec((tm, d), tok), pl.BlockSpec((tm, nq * ns), tok), _ANY, _ANY,
                  pl.BlockSpec((tm, d), tok), pl.BlockSpec((tm, d), tok), pl.BlockSpec((tm, d), tok),
                  pl.BlockSpec((None, 8, d), lambda i: (i // tpb, 0, 0)),
                  pl.BlockSpec((1, d), lambda i: (0, 0)), pl.BlockSpec((1, d), lambda i: (0, 0))],
        out_specs=[pl.BlockSpec((tm, nq * ns), tok),
                   pl.BlockSpec((tm, d), tok), pl.BlockSpec((tm, d), tok),
                   pl.BlockSpec((8, d), lambda i: (0, 0)), pl.BlockSpec((None, 8, d), lambda i: (i // tpb, 0, 0))],
        out_shape=[jax.ShapeDtypeStruct((t, nq * ns), BF16),
                   jax.ShapeDtypeStruct((t, d), F32), jax.ShapeDtypeStruct((t, d), BF16),
                   jax.ShapeDtypeStruct((8, d), F32), jax.ShapeDtypeStruct((nb, 8, d), F32)],
        scratch_shapes=[pltpu.VMEM(w_down_g.shape, BF16), pltpu.VMEM(w_up_g.shape, BF16),
                        pltpu.VMEM((tm // ts, ts, d), F32), pltpu.SemaphoreType.DMA((2,))],
        compiler_params=_cp(("arbitrary",), VMEM_LIMIT_MAX),
    )(df, up, w_down_g, w_up_g, dz2, x2, mix, modv, ln1_g, ln1_b)


def _mix_bwd(dmix, proj, y_a, y_b, hs, w_out_g, w_o_lru_g, w_o_sgu_g, seq, after=()):
    t, d = dmix.shape
    w = hs.shape[1]
    nq, d_sgu, ns = w_o_sgu_g.shape
    tm = min(TM_MIX, seq)

    def body(dmix_ref, ga_ref, gb_ref, ya_ref, yb_ref, gl_ref, hs_ref, wo_ref, wl_ref, ws_ref,
             dya_ref, dyb_ref, dga_ref, dgb_ref, dgl_ref, dyl_ref, dys_ref):
        dmerged = _mm_nt(dmix_ref[...], wo_ref[...])
        sa, sb = _sigmoid_t(ga_ref[...].astype(F32)), _sigmoid_t(gb_ref[...].astype(F32))
        dy_a = (dmerged * sa).astype(BF16)
        dy_b = (dmerged * sb).astype(BF16)
        dya_ref[...] = dy_a
        dyb_ref[...] = dy_b
        dga_ref[...] = (dmerged * ya_ref[...].astype(F32) * (sa * (1.0 - sa))).astype(BF16)
        dgb_ref[...] = (dmerged * yb_ref[...].astype(F32) * (sb * (1.0 - sb))).astype(BF16)
        dyap = _mm_nt(dy_a, wl_ref[...])
        gel, dgel = _gelu_and_grad(gl_ref[...].astype(F32))
        dyl_ref[...] = dyap * gel
        dgl_ref[...] = (dyap * hs_ref[...] * dgel).astype(BF16)
        dys = _mm_nt(dy_b[:, 0:ns], ws_ref[0])
        for q in range(1, nq):
            dys = dys + _mm_nt(dy_b[:, q * ns:(q + 1) * ns], ws_ref[q])
        dys_ref[...] = dys

    row = lambda width, col: pl.BlockSpec((tm, width), functools.partial(lambda i, k: (i, k), k=col))
    full = lambda shape: pl.BlockSpec(shape, lambda i: (0,) * len(shape))
    return pl.pallas_call(
        _ordered(body, 10, after), name="mix_bwd", grid=(t // tm,),
        in_specs=[row(d, 0), row(d, 4), row(d, 5), row(d, 0), row(d, 0), row(w, 1), row(w, 0),
                  full(w_out_g.shape), full(w_o_lru_g.shape), full(w_o_sgu_g.shape)] + [_ANY] * len(after),
        out_specs=[row(d, 0), row(d, 0), row(d, 0), row(d, 0), row(w, 0), row(w, 0), row(d_sgu, 0)],
        out_shape=[jax.ShapeDtypeStruct((t, d), BF16), jax.ShapeDtypeStruct((t, d), BF16),
                   jax.ShapeDtypeStruct((t, d), BF16), jax.ShapeDtypeStruct((t, d), BF16),
                   jax.ShapeDtypeStruct((t, w), BF16), jax.ShapeDtypeStruct((t, w), F32),
                   jax.ShapeDtypeStruct((t, d_sgu), F32)],
        compiler_params=_cp(("arbitrary",)),
    )(dmix, proj, proj, y_a, y_b, proj, hs, w_out_g, w_o_lru_g, w_o_sgu_g, *after)


def _sgu_bwd(proj, dys, w_sp, b_sp_t, ln_g, ln_b, after=()):
    t = proj.shape[0]
    d_sgu = SGU_GROUPS * HEAD
    tm = min(TM_SGU, t)
    nblk = tm // HEAD
    specs, n_piece = _sgu_specs(tm, d_sgu)

    def body(*refs):
        u = jnp.concatenate([r[...] for r in refs[:n_piece]], axis=1).astype(F32)
        v = jnp.concatenate([r[...] for r in refs[n_piece:2 * n_piece]], axis=1).astype(F32)
        dys_ref, w_ref, bt_ref, g_ref, b_ref, du_ref, dv_ref, dw_ref, st_ref, dbt_ref, dvn_s = refs[2 * n_piece:]

        @pl.when(pl.program_id(0) == 0)
        def _():
            dw_ref[...] = jnp.zeros_like(dw_ref)
            st_ref[...] = jnp.zeros_like(st_ref)
            dbt_ref[...] = jnp.zeros_like(dbt_ref)

        ug, dug_du = _gelu_and_grad(u)
        vg, dvg_dv = _gelu_and_grad(v)
        xhat, rstd = _ln_stats(vg)
        vn = (xhat * g_ref[...] + b_ref[...]).astype(BF16)
        dys_v = dys_ref[...]
        mask = _sgu_mask()
        for g in range(SGU_GROUPS):
            wm = jnp.where(mask, w_ref[g], 0.0).astype(BF16)
            cols = slice(g * HEAD, (g + 1) * HEAD)
            dw_g = jnp.zeros((HEAD, HEAD), F32)
            db_g = jnp.zeros((HEAD, 1), F32)
            for n in range(nblk):
                rows = slice(n * HEAD, (n + 1) * HEAD)
                vn_blk = vn[rows, cols]
                mixed = jnp.dot(wm, vn_blk, preferred_element_type=F32) + bt_ref[:, g:g + 1]
                dy_blk = dys_v[rows, cols]
                du_ref[rows, cols] = (dy_blk * mixed * dug_du[rows, cols]).astype(BF16)
                dmx = dy_blk * ug[rows, cols]
                dvn_s[rows, cols] = _mm_tn(wm, dmx)
                dw_g = dw_g + _mm_nt(dmx, vn_blk)
                db_g = db_g + jnp.sum(dmx, axis=1, keepdims=True)
            dw_ref[g] += jnp.where(mask, dw_g, 0.0)
            dbt_ref[:, g:g + 1] += db_g
        dvn = dvn_s[...]
        st_ref[0:1, :] += _colsum(dvn * xhat)
        st_ref[1:2, :] += _colsum(dvn)
        dv_ref[...] = (_ln_bwd(dvn * g_ref[...], xhat, rstd) * dvg_dv).astype(BF16)

    full = lambda shape: pl.BlockSpec(shape, lambda i: (0,) * len(shape))
    tok = pl.BlockSpec((tm, d_sgu), lambda i: (i, 0))
    return pl.pallas_call(
        _ordered(body, 2 * n_piece + 5, after), name="sgu_bwd", grid=(t // tm,),
        in_specs=specs + [tok, full(w_sp.shape), full(b_sp_t.shape), full(ln_g.shape), full(ln_b.shape)]
        + [_ANY] * len(after),
        out_specs=[tok, tok, full(w_sp.shape), full((8, d_sgu)), full((HEAD, HEAD))],
        out_shape=[jax.ShapeDtypeStruct((t, d_sgu), BF16), jax.ShapeDtypeStruct((t, d_sgu), BF16),
                   jax.ShapeDtypeStruct(w_sp.shape, F32), jax.ShapeDtypeStruct((8, d_sgu), F32),
                   jax.ShapeDtypeStruct((HEAD, HEAD), F32)],
        scratch_shapes=[pltpu.VMEM((tm, d_sgu), F32)],
        compiler_params=_cp(("arbitrary",)),
    )(*([proj] * (2 * n_piece)), dys, w_sp, b_sp_t, ln_g, ln_b, *after)


def _lru_bwd(proj, hs, e, dyl, lru_w, nb, seq, after=()):
    t = proj.shape[0]
    w = LRU_HEADS * HEAD
    w_conv, b_conv, w_a, b_a, w_x, b_x, lam = lru_w

    def body(x_ref, hs_ref, e_ref, dy_ref, wc_ref, bc_ref, wa_ref, ba_ref, wx_ref, bx_ref, lam_ref,
             dxl_ref, dwa_ref, dwx_ref, st_ref):
        @pl.when(pl.program_id(1) == 0)
        def _():
            dwa_ref[...] = jnp.zeros_like(dwa_ref)
            dwx_ref[...] = jnp.zeros_like(dwx_ref)
            st_ref[...] = jnp.zeros_like(st_ref)

        xl = x_ref[...].astype(F32)
        xc, r, gi, big_l, a, m2 = _lru_gates(xl, wc_ref, bc_ref, wa_ref, ba_ref, wx_ref, bx_ref, lam_ref)
        inv_mult = lax.rsqrt(m2)
        mult = m2 * inv_mult
        dh = dy_ref[...] + _shift_up(e_ref[...], 1)
        da = dh * _shift_down(hs_ref[...], 1)
        dmult = dh * (gi * xc)
        d_i = dh * (mult * xc)
        dxc = dh * (mult * gi)
        dla = a * (da - dmult * (a * inv_mult))
        dr = dla * big_l
        d_big_l = _colsum(dla * r)
        dra = dr * (r * (1.0 - r))
        dia = d_i * (gi * (1.0 - gi))
        dwa_ref[...] += _mm_tn(xc, dra)
        dwx_ref[...] += _mm_tn(xc, dia)
        dxc = dxc + _mm_nt(dra, wa_ref[...]) + _mm_nt(dia, wx_ref[...])
        dxl = wc_ref[CONV_WIDTH - 1:CONV_WIDTH, :] * dxc
        st_ref[4 + CONV_WIDTH - 1:4 + CONV_WIDTH, :] += _colsum(dxc * xl)
        for k in range(CONV_WIDTH - 1):
            ahead = _shift_up(dxc, CONV_WIDTH - 1 - k)
            dxl = dxl + wc_ref[k:k + 1, :] * ahead
            st_ref[4 + k:5 + k, :] += _colsum(ahead * xl)
        dxl_ref[...] = dxl.astype(BF16)
        st_ref[0:1, :] += _colsum(dra)
        st_ref[1:2, :] += _colsum(dia)
        st_ref[2:3, :] += d_big_l * (LRU_C * _sigmoid(-lam_ref[...]))
        st_ref[3:4, :] += _colsum(dxc)

    col = lambda hd, b: (0, hd)
    head = lambda hd, b: (hd, 0, 0)
    tok = lambda hd, b: (b, hd)
    seq_blk = pl.BlockSpec((seq, HEAD), tok)
    return pl.pallas_call(
        _ordered(body, 11, after), name="lru_bwd", grid=(LRU_HEADS, nb),
        in_specs=[seq_blk, seq_blk, seq_blk, seq_blk,
                  pl.BlockSpec((CONV_WIDTH, HEAD), col), pl.BlockSpec((1, HEAD), col),
                  pl.BlockSpec((None, HEAD, HEAD), head), pl.BlockSpec((1, HEAD), col),
                  pl.BlockSpec((None, HEAD, HEAD), head), pl.BlockSpec((1, HEAD), col),
                  pl.BlockSpec((1, HEAD), col)] + [_ANY] * len(after),
        out_specs=[seq_blk, pl.BlockSpec((None, HEAD, HEAD), head), pl.BlockSpec((None, HEAD, HEAD), head),
                   pl.BlockSpec((8, HEAD), col)],
        out_shape=[jax.ShapeDtypeStruct((t, w), BF16), jax.ShapeDtypeStruct((LRU_HEADS, HEAD, HEAD), F32),
                   jax.ShapeDtypeStruct((LRU_HEADS, HEAD, HEAD), F32), jax.ShapeDtypeStruct((8, w), F32)],
        compiler_params=_cp(("arbitrary", "arbitrary")),
    )(proj, hs, e, dyl, w_conv, b_conv, w_a, b_a, w_x, b_x, lam, *after)


def _weight_grad(a, g, col_shards, name, after=()):
    t, k = a.shape
    n = g.shape[1]
    tt = min(TT_DW, t)
    tk = k if k <= 1536 else 1024
    ns = n // N_CHIPS if col_shards else n
    narrow = col_shards and ns < 512
    tn = n if narrow else min(ns, 768 if ns % 768 == 0 else 1024)
    while ns % tn and not narrow:
        tn //= 2
    per = max(ns // tn, 1)

    def body(a_ref, g_ref, o_ref):
        @pl.when(pl.program_id(2) == 0)
        def _():
            o_ref[...] = jnp.zeros_like(o_ref)

        res = _mm_tn(a_ref[...], g_ref[...])
        if narrow:
            for q in range(N_CHIPS):
                o_ref[q] += res[:, q * ns:(q + 1) * ns]
        else:
            o_ref[...] += res

    if narrow:
        out_spec = pl.BlockSpec((N_CHIPS, tk, ns), lambda i, j, s: (0, i, 0))
        out_shape = jax.ShapeDtypeStruct((N_CHIPS, k, ns), F32)
    elif col_shards:
        out_spec = pl.BlockSpec((None, tk, tn), lambda i, j, s: (j // per, i, j % per))
        out_shape = jax.ShapeDtypeStruct((N_CHIPS, k, ns), F32)
    else:
        out_spec = pl.BlockSpec((tk, tn), lambda i, j, s: (i, j))
        out_shape = jax.ShapeDtypeStruct((k, n), F32)
    return pl.pallas_call(
        _ordered(body, 2, after), name=name, grid=(k // tk, n // tn, t // tt),
        in_specs=[pl.BlockSpec((tt, tk), lambda i, j, s: (s, i)), pl.BlockSpec((tt, tn), lambda i, j, s: (s, j))]
        + [_ANY] * len(after),
        out_specs=out_spec, out_shape=out_shape,
        compiler_params=_cp(("arbitrary", "arbitrary", "arbitrary")),
    )(a, g, *after)


def _input_grad(dproj, w_in_g, dz1, x2, modv, nb, seq, after=()):
    t, d = x2.shape
    nq, _, ns = w_in_g.shape
    tm = min(TM_DH, seq)
    tpb = seq // tm

    def body(dp_ref, w_ref, dz1_ref, x_ref, mod_ref, gx_ref, db_ref, pb_ref, acc):
        i, q = pl.program_id(0), pl.program_id(1)

        @pl.when(q == 0)
        def _():
            acc[...] = jnp.zeros_like(acc)

        @pl.when((i == 0) & (q == 0))
        def _():
            db_ref[...] = jnp.zeros_like(db_ref)

        @pl.when((i % tpb == 0) & (q == 0))
        def _():
            pb_ref[...] = jnp.zeros_like(pb_ref)

        dp = dp_ref[...]
        acc[...] += _mm_nt(dp, w_ref[...])
        db_ref[q, 0:1, :] += _colsum(dp.astype(F32))

        @pl.when(q == nq - 1)
        def _():
            dh = acc[...]
            gx_ref[...] = ALPHA * dz1_ref[...] + dh * (1.0 + mod_ref[1:2, :])
            pb_ref[0:1, :] += _colsum(dh * x_ref[...])
            pb_ref[1:2, :] += _colsum(dh)

    tok = lambda i, q: (i, 0)
    return pl.pallas_call(
        _ordered(body, 5, after), name="input_grad", grid=(t // tm, nq),
        in_specs=[pl.BlockSpec((tm, ns), lambda i, q: (i, q)), pl.BlockSpec((None, d, ns), lambda i, q: (q, 0, 0)),
                  pl.BlockSpec((tm, d), tok), pl.BlockSpec((tm, d), tok),
                  pl.BlockSpec((None, 8, d), lambda i, q: (i // tpb, 0, 0))] + [_ANY] * len(after),
        out_specs=[pl.BlockSpec((tm, d), tok), pl.BlockSpec((nq, 8, ns), lambda i, q: (0, 0, 0)),
                   pl.BlockSpec((None, 8, d), lambda i, q: (i // tpb, 0, 0))],
        out_shape=[jax.ShapeDtypeStruct((t, d), F32), jax.ShapeDtypeStruct((nq, 8, ns), F32),
                   jax.ShapeDtypeStruct((nb, 8, d), F32)],
        scratch_shapes=[pltpu.VMEM((tm, d), F32)],
        compiler_params=_cp(("arbitrary", "arbitrary")),
    )(dproj, w_in_g, dz1, x2, modv, *after)


def _rows128(v):
    flat = v.reshape(-1, HEAD)
    pad = (-flat.shape[0]) % 8
    return jnp.pad(flat, ((0, pad), (0, 0))) if pad else flat


def kernel(x, c, w_ada, b_ada, w_in, b_in, w_conv, b_conv, w_rg_a, b_rg_a, w_rg_x, b_rg_x, lru_lambda, w_sp, b_sp, ln_v_g, ln_v_b, w_o_lru, w_o_sgu, w_out, ln1_g, ln1_b, w_up, w_down, ln2_g, ln2_b, loss_target, m_w_ada, m_b_ada, m_w_in, m_b_in, m_w_conv, m_b_conv, m_w_rg_a, m_b_rg_a, m_w_rg_x, m_b_rg_x, m_lru_lambda, m_w_sp, m_b_sp, m_ln_v_g, m_ln_v_b, m_w_o_lru, m_w_o_sgu, m_w_out, m_ln1_g, m_ln1_b, m_w_up, m_w_down, m_ln2_g, m_ln2_b, v_w_ada, v_b_ada, v_w_in, v_b_in, v_w_conv, v_b_conv, v_w_rg_a, v_b_rg_a, v_w_rg_x, v_b_rg_x, v_lru_lambda, v_w_sp, v_b_sp, v_ln_v_g, v_ln_v_b, v_w_o_lru, v_w_o_sgu, v_w_out, v_ln1_g, v_ln1_b, v_w_up, v_w_down, v_ln2_g, v_ln2_b):
    given = dict(locals())
    nb, seq, d = x.shape
    t = nb * seq
    w_lru = LRU_HEADS * HEAD
    d_sgu = SGU_GROUPS * HEAD
    xi, yi, ci = lax.axis_index("x"), lax.axis_index("y"), lax.axis_index("c")
    chip = 2 * xi + yi
    dev = 2 * chip + ci
    cidx = jnp.reshape(ci, (1,)).astype(jnp.int32)

    x2 = x.reshape(t, d)
    target = loss_target.reshape(t, d)

    big = ["w_in", "w_o_lru", "w_o_sgu", "w_out", "w_up", "w_down"]
    shards_a = [w_in[0].astype(BF16)]
    shards_b = [given[n][0].astype(BF16) for n in big[1:]]
    pidx = jnp.reshape(chip, (1,)).astype(jnp.int32)

    c_rows = _rows128(c)
    wconv_rows = _rows128(w_conv[0])
    slab0 = _all_gather_small(jnp.concatenate([c_rows, wconv_rows], axis=0), "gather_c_wconv")
    slab0 = slab0.reshape(N_DEV, -1, HEAD)
    c_all = slab0[:, :c_rows.shape[0]].reshape(N_DEV * nb, d)
    n_wc = CONV_WIDTH * (w_lru // N_CHIPS) // HEAD
    wc = slab0[0::2, c_rows.shape[0]:c_rows.shape[0] + n_wc].reshape(N_CHIPS, CONV_WIDTH, w_lru // N_CHIPS)
    w_conv_full = jnp.transpose(wc, (1, 0, 2)).reshape(CONV_WIDTH, w_lru)

    n_ada = w_ada.shape[2]
    b_ada_cols = lax.dynamic_slice(b_ada, (0, chip * n_ada), (1, n_ada))
    mod_cols = _ada_fwd(c_all, w_ada[0], b_ada_cols)
    half = (N_DEV * nb) // 2
    mod_half = lax.dynamic_slice(mod_cols, (ci * half, 0), (half, n_ada))
    mod_g = _all_gather_small(mod_half, "gather_mod").reshape(N_CHIPS, 2, half, n_ada)
    mod_all = jnp.transpose(mod_g, (1, 2, 0, 3)).reshape(N_DEV * nb, N_CHIPS * n_ada)
    mod_loc = lax.dynamic_slice(mod_all, (dev * nb, 0), (nb, N_CHIPS * n_ada)).reshape(nb, 6, d)
    modv = jnp.pad(mod_loc, ((0, 0), (0, 2), (0, 0)))

    lru_w = (w_conv_full, b_conv, w_rg_a[0], b_rg_a, w_rg_x[0], b_rg_x, lru_lambda)
    b_sp_t = jnp.transpose(b_sp[0])

    land = lambda s: jax.ShapeDtypeStruct((N_CHIPS,) + s.shape, s.dtype)
    sds = lambda s: jax.ShapeDtypeStruct(s.shape, s.dtype)
    started_a = _split_start(shards_a, [sds(shards_a[0])] * 3, _peer_gather_copies((0, 1, 2)), 3, "gather_w_in_start",
                             after=(modv,))
    shards_b, shards_c = shards_b[:3], shards_b[3:]
    started_b = _split_start(shards_b, [land(s) for s in shards_b], _gather_copies, 3 * len(shards_b),
                             "gather_w_mix_start", after=(started_a[-1],))
    started_c = _split_start(shards_c, [land(s) for s in shards_c], _gather_copies, 3 * len(shards_c),
                             "gather_w_mlp_start", after=(started_b[-1],))

    ids = lambda *v: jnp.stack(v).astype(jnp.int32)
    modv_t = modv + started_c[-1][0:1, 0:1]
    proj, h = _proj_fwd(x2, modv_t, [started_a[2]], ids(chip), b_in, seq, "proj_fwd_own")
    own_a, lands_a = _split_wait(started_a, 1, _peer_gather_copies((0, 1)), "gather_w_in_wait_near", after=(proj,))
    (proj,) = _proj_fwd(x2, modv, lands_a[:2], ids(chip ^ 1, chip ^ 2), b_in, seq, "proj_fwd_near", proj_in=proj)
    own_a, lands_a = _split_wait((started_a[0], started_a[1], *own_a, *lands_a, started_a[-1]), 1,
                                 _peer_gather_copies((2,)), "gather_w_in_wait_far", after=(proj,))
    (proj,) = _proj_fwd(x2, modv, lands_a[2:], ids(chip ^ 3), b_in, seq, "proj_fwd_far", proj_in=proj)
    w_in_g = lax.empty((N_CHIPS,) + shards_a[0].shape, BF16)
    for k, (shard, slot) in enumerate(zip(own_a + lands_a, (chip, chip ^ 1, chip ^ 2, chip ^ 3))):
        (w_in_g,) = _fill_own_slot([w_in_g], [shard], ids(slot), ["place_w_in_%d" % k])
    a, inp = _lru_prep(proj, lru_w, nb, seq)
    a3 = a.reshape(nb, seq, w_lru)
    hs = _scan(a3, inp.reshape(nb, seq, w_lru), False, "lru_scan").reshape(t, w_lru)
    y_sgu = _sgu_fwd(proj, w_sp[0], b_sp_t, ln_v_g, ln_v_b)
    shards_b, lands_b = _split_wait(started_b, len(shards_b), _gather_copies, "gather_w_mix_wait", after=(hs, y_sgu))
    w_o_lru_g, w_o_sgu_g, w_out_g = _fill_own_slot(lands_b, shards_b, pidx, ["own_" + n for n in big[1:4]])
    w_o_lru_g = w_o_lru_g.reshape(w_lru, d)
    w_out_g = w_out_g.reshape(d, d)
    yap, y_a, y_b, merged, mix, x1 = _mix_fwd(hs, proj, y_sgu, x2, modv, w_o_lru_g, w_o_sgu_g, w_out_g, ln1_g, ln1_b, seq)
    shards_c, lands_c = _split_wait(started_c, len(shards_c), _gather_copies, "gather_w_mlp_wait", after=(x1,))
    w_up_g, w_down_g = _fill_own_slot(lands_c, shards_c, pidx, ["own_" + n for n in big[4:]])
    w_down_g = w_down_g.reshape(-1, d)
    up, act, h2, dz2, df, st2, pb2 = _mlp_fwd(x1, modv, w_up_g, w_down_g, ln2_g, ln2_b, target, nb, seq)
    loss = lax.psum(st2[2, 0], ("x", "y", "c"))

    part = {}

    def to_sibling_start(group, tag, after=()):
        g4 = []
        for n in group:
            shard = given[n].shape[1:]
            g4.append(part[n].reshape(N_CHIPS, 2, shard[0] // 2, shard[1]))
        shapes = [jax.ShapeDtypeStruct((N_CHIPS,) + g.shape[2:], F32) for g in g4]
        return _split_start(g4, shapes, _to_sibling_copies, len(g4), "grads_to_sibling_start_" + tag, after)

    def to_chips_start(group, started, tag, after=()):
        g4, recv = _split_wait(started, len(group), _to_sibling_copies, "grads_to_sibling_wait_" + tag, after)
        own4 = [_add_own_half(g4[k], recv[k], cidx, "grad_pair_sum_" + n) for k, n in enumerate(group)]
        shapes = [jax.ShapeDtypeStruct((3,) + o.shape[1:], BF16) for o in own4]
        return _split_start(own4, shapes, _chip_exchange_copies, 3 * len(own4), "grads_chip_exchange_start_" + tag)

    def chips_finish(group, started, tag, after=()):
        own4, slots = _split_wait(started, len(group), _chip_exchange_copies, "grads_chip_exchange_wait_" + tag, after)
        return [_sum_own_and_peers(own4[k], slots[k], pidx, "grad_chip_sum_" + n) for k, n in enumerate(group)]

    dup, dz1, dmix, st1, pb1 = _mlp_bwd(df, up, w_down_g, w_up_g, dz2, x2, mix, modv, ln1_g, ln1_b, nb, seq)
    group1 = ["w_up", "w_down"]
    part["w_up"] = _weight_grad(h2, dup, True, "grad_w_up")
    part["w_down"] = _weight_grad(act, df, False, "grad_w_down")
    sib1 = to_sibling_start(group1, "mlp")
    dy_a, dy_b, dga, dgb, dgl, dyl, dys = _mix_bwd(dmix, proj, y_a, y_b, hs, w_out_g, w_o_lru_g, w_o_sgu_g, seq,
                                                   after=(sib1[-1],))
    group2 = ["w_o_lru", "w_o_sgu", "w_out"]
    part["w_o_lru"] = _weight_grad(yap, dy_a, False, "grad_w_o_lru")
    part["w_o_sgu"] = _weight_grad(y_sgu, dy_b, True, "grad_w_o_sgu")
    part["w_out"] = _weight_grad(merged, dmix, False, "grad_w_out")
    chips1 = to_chips_start(group1, sib1, "mlp", after=(dys, part["w_o_lru"], part["w_o_sgu"], part["w_out"]))
    sib2 = to_sibling_start(group2, "mix", after=(chips1[-1],))
    du, dv, g_w_sp, st_sgu, g_b_sp_t = _sgu_bwd(proj, dys, w_sp[0], b_sp_t, ln_v_g, ln_v_b, after=(sib2[-1],))
    dyl3 = dyl.reshape(nb, seq, w_lru)
    e = _scan(a3, dyl3, True, "lru_scan_bwd").reshape(t, w_lru)
    chips2 = to_chips_start(group2, sib2, "mix", after=(e, du))
    dxl, g_w_rg_a, g_w_rg_x, st_lru = _lru_bwd(proj, hs, e, dyl, lru_w, nb, seq, after=(chips2[-1],))
    dproj = jnp.concatenate([dxl, dgl, du, dv, dga, dgb], axis=1)

    didx = jnp.reshape(dev, (1,)).astype(jnp.int32)
    early = [
        ("w_conv", st_lru[4:8]), ("b_conv", st_lru[3]), ("w_rg_a", g_w_rg_a), ("b_rg_a", st_lru[0]),
        ("w_rg_x", g_w_rg_x), ("b_rg_x", st_lru[1]), ("lru_lambda", st_lru[2]), ("w_sp", g_w_sp),
        ("b_sp", jnp.transpose(g_b_sp_t[:, :SGU_GROUPS])), ("ln_v_g", st_sgu[0]), ("ln_v_b", st_sgu[1]),
        ("ln1_g", st1[0]), ("ln1_b", st1[1]), ("ln2_g", st2[0]), ("ln2_b", st2[1]),
    ]
    pieces_e = [_rows128(v) for _, v in early]
    slab_e = jnp.concatenate(pieces_e, axis=0)
    slab_e = jnp.pad(slab_e, ((0, (-slab_e.shape[0]) % TR_EW), (0, 0)))
    small_st = _split_start([slab_e], [jax.ShapeDtypeStruct((N_DEV,) + slab_e.shape, F32)], _all_devices_copies, N_DEV - 1,
                            "small_grads_start")

    group3 = ["w_in"]
    part["w_in"] = _weight_grad(h, dproj, True, "grad_w_in", after=(small_st[-1],))
    sib3 = to_sibling_start(group3, "in")
    chips3 = to_chips_start(group3, sib3, "in")
    grad_x2, g_b_in4, pb0 = _input_grad(dproj, w_in_g, dz1, x2, modv, nb, seq, after=(chips3[-1],))
    halves12 = chips_finish(group1, chips1, "mlp", after=(grad_x2,)) + chips_finish(group2, chips2, "mix", after=(grad_x2,))
    swap12 = _split_start(halves12, [jax.ShapeDtypeStruct(hv.shape, F32) for hv in halves12], _swap_copies, len(halves12),
                          "grads_swap_start")
    grads = {}

    dmod_loc = jnp.stack([pb0[:, 1], pb0[:, 0], pb1[:, 2], pb1[:, 1], pb1[:, 0], pb2[:, 0]], axis=1)
    late = [("dmod", dmod_loc), ("b_in", g_b_in4[:, 0])]
    pieces_l = [_rows128(v) for _, v in late]
    slab_l = jnp.concatenate(pieces_l, axis=0)
    gathered = _all_gather_small(slab_l, "gather_small_grads", after=(swap12[-1],)).reshape(N_DEV, slab_l.shape[0], HEAD)
    rows_dmod = dmod_loc.size // HEAD
    dmod_all = gathered[:, :rows_dmod].reshape(N_DEV * nb, 6 * d)
    grads["b_in"] = _sum_slots(gathered[:, rows_dmod:], "grad_b_in_sum").reshape(1, -1)

    (slab_e,), (lands_e,) = _split_wait(small_st, 1, _all_devices_copies, "small_grads_wait", after=(gathered,))
    summed = _sum_devices(lands_e, slab_e, didx, "small_grad_sum")
    off = 0
    for (n, v), piece in zip(early, pieces_e):
        grads[n] = summed[off:off + v.size // HEAD].reshape(v.shape)
        off += piece.shape[0]

    mine12, theirs12 = _split_wait(swap12, len(halves12), _swap_copies, "grads_swap_wait", after=(summed,))
    (mine3,) = chips_finish(group3, chips3, "in", after=(summed,))
    (theirs3,) = _exchange([mine3], [jax.ShapeDtypeStruct(mine3.shape, F32)], _swap_copies, 1, "grads_swap_w_in")
    mine = dict(zip(group1 + group2 + group3, mine12 + [mine3]))
    theirs = dict(zip(group1 + group2 + group3, theirs12 + [theirs3]))

    dmod_cols = lax.dynamic_slice(dmod_all, (0, chip * n_ada), (N_DEV * nb, n_ada))
    grads["w_ada"], grads["b_ada"] = _ada_bwd(c_all, dmod_all, dmod_cols)
    n_wcs = w_lru // N_CHIPS
    grads["w_conv"] = lax.dynamic_slice(grads["w_conv"], (0, chip * n_wcs), (CONV_WIDTH, n_wcs))

    names = ['w_ada', 'b_ada', 'w_in', 'b_in', 'w_conv', 'b_conv', 'w_rg_a', 'b_rg_a', 'w_rg_x', 'b_rg_x', 'lru_lambda',
             'w_sp', 'b_sp', 'ln_v_g', 'ln_v_b', 'w_o_lru', 'w_o_sgu', 'w_out', 'ln1_g', 'ln1_b', 'w_up', 'w_down',
             'ln2_g', 'ln2_b']
    out_g, out_d, out_m, out_v = [], [], [], []
    for n in names:
        wv = given[n]
        shape2 = (-1, wv.shape[-1])
        w2, m2, v2 = wv.reshape(shape2), given["m_" + n].reshape(shape2), given["v_" + n].reshape(shape2)
        if n in big:
            g2, dlt, nm, nv = _adamw_halves(w2, mine[n], theirs[n], m2, v2, cidx, "adamw_" + n)
        else:
            g2 = grads[n].reshape(wv.shape).reshape(shape2)
            dlt, nm, nv = _adamw(w2, g2, m2, v2, "adamw_" + n)
        out_g.append(g2.reshape(wv.shape))
        out_d.append(dlt.reshape(wv.shape))
        out_m.append(nm.reshape(wv.shape))
        out_v.append(nv.reshape(wv.shape))

    return (loss, grad_x2.reshape(nb, seq, d), *out_g, *out_d, *out_m, *out_v)
```

```python
import functools
import math

import jax
import jax.numpy as jnp
from jax import lax
from jax.experimental import pallas as pl
from jax.experimental.pallas import tpu as pltpu

F32 = jnp.float32
BF16 = jnp.bfloat16
MESH = pl.DeviceIdType.MESH

N_CHIPS = 4
N_DEV = 8
LRU_HEADS = 10
HEAD = 128
SGU_GROUPS = 6
SGU_CHUNK = 64
CONV_WIDTH = 4
LRU_C = 8.0
ALPHA = 2.0 ** 0.25
LN_EPS = 1e-5
ADAM_LR, ADAM_B1, ADAM_B2, ADAM_EPS, ADAM_WD, ADAM_STEP = 0.001, 0.9, 0.999, 1e-08, 0.01, 10

VMEM_LIMIT = 56 * 1024 * 1024
VMEM_LIMIT_MAX = 62 * 1024 * 1024
TM_PROJ = 1024
TM_MIX = 256
TM_MLP = 512
TS_MLP = 256
TM_SGU = 512
TM_DH = 512
TT_DW = 1024
TC_SCAN = 256
TR_EW = 256


def _cp(sem=None, limit=None):
    return pltpu.CompilerParams(dimension_semantics=sem, vmem_limit_bytes=limit or VMEM_LIMIT)


def _mm(a, b):
    return jnp.dot(a.astype(BF16), b.astype(BF16), preferred_element_type=F32)


def _mm_nt(a, b):
    return lax.dot_general(a.astype(BF16), b.astype(BF16), (((1,), (1,)), ((), ())), preferred_element_type=F32)


def _mm_tn(a, b):
    return lax.dot_general(a.astype(BF16), b.astype(BF16), (((0,), (0,)), ((), ())), preferred_element_type=F32)


def _sigmoid(x):
    return 1.0 / (1.0 + jnp.exp(-x))


def _sigmoid_t(x):
    return 0.5 * jnp.tanh(0.5 * x) + 0.5


_GELU_K = math.sqrt(2.0 / math.pi)


def _gelu(x):
    t = jnp.tanh(_GELU_K * (x + 0.044715 * (x * x * x)))
    return 0.5 * x * (1.0 + t)


def _gelu_and_grad(x):
    x2 = x * x
    t = jnp.tanh(_GELU_K * (x + 0.044715 * (x2 * x)))
    g = 0.5 * x * (1.0 + t)
    dg = 0.5 * (1.0 + t) + 0.5 * x * (1.0 - t * t) * (_GELU_K * (1.0 + 3.0 * 0.044715 * x2))
    return g, dg


def _ln_stats(z):
    mu = jnp.mean(z, axis=-1, keepdims=True)
    zc = z - mu
    var = jnp.mean(zc * zc, axis=-1, keepdims=True)
    rstd = lax.rsqrt(var + LN_EPS)
    return zc * rstd, rstd


def _ln_bwd(dxh, xhat, rstd):
    m1 = jnp.mean(dxh, axis=-1, keepdims=True)
    m2 = jnp.mean(dxh * xhat, axis=-1, keepdims=True)
    return rstd * (dxh - m1 - xhat * m2)


def _colsum(v):
    return jnp.sum(v, axis=0, keepdims=True)


def _shift_down(v, j):
    if j == 0:
        return v
    rows = lax.broadcasted_iota(jnp.int32, v.shape, 0)
    return jnp.where(rows >= j, pltpu.roll(v, j, 0), 0.0)


def _shift_up(v, j):
    if j == 0:
        return v
    n = v.shape[0]
    rows = lax.broadcasted_iota(jnp.int32, v.shape, 0)
    return jnp.where(rows < n - j, pltpu.roll(v, n - j, 0), 0.0)


def _load_weights(srcs, dsts, sems):
    cps = [pltpu.make_async_copy(s, dd, sems.at[k]) for k, (s, dd) in enumerate(zip(srcs, dsts))]
    for cp in cps:
        cp.start()
    for cp in cps:
        cp.wait()


def _my_pos():
    return lax.axis_index("x"), lax.axis_index("y"), lax.axis_index("c")


def _all_gather_small(v, name, after=()):
    m_per, n = v.shape

    def body(x_ref, out_ref, send_sems, recv_sems, local_sem):
        x, y, c = _my_pos()
        me, sibling = (x, y, c), (x, y, 1 - c)
        chips = [(1 - x, y), (x, 1 - y), (1 - x, 1 - y)]

        def rows(px, py, pc):
            return out_ref.at[pl.ds((4 * px + 2 * py + pc) * m_per, m_per), :]

        def copy(k, block, to, src=None):
            return pltpu.make_async_remote_copy(
                src_ref=rows(*block) if src is None else src, dst_ref=rows(*block),
                send_sem=send_sems.at[k], recv_sem=recv_sems.at[k], device_id=to, device_id_type=MESH)

        mine = pltpu.make_async_copy(x_ref, rows(*me), local_sem)
        mine.start()
        first = [copy(0, me, sibling, src=x_ref)]
        first += [copy(1 + j, me, (*chip, c), src=x_ref) for j, chip in enumerate(chips)]
        for cp in first:
            cp.start()
        passed = [copy(4 + j, (*chip, c), sibling) for j, chip in enumerate(chips)]
        for j, chip in enumerate(chips):
            copy(1 + j, (*chip, c), me).wait_recv()
            passed[j].start()
        copy(0, sibling, me).wait_recv()
        for j, chip in enumerate(chips):
            copy(4 + j, (*chip, 1 - c), me).wait_recv()
        for cp in first + passed:
            cp.wait_send()
        mine.wait()

    return pl.pallas_call(
        _ordered(body, 1, after), name=name,
        out_shape=jax.ShapeDtypeStruct((N_DEV * m_per, n), v.dtype),
        in_specs=[pl.BlockSpec(memory_space=pltpu.VMEM)] + [pl.BlockSpec(memory_space=pl.ANY)] * len(after),
        out_specs=pl.BlockSpec(memory_space=pltpu.VMEM),
        scratch_shapes=[pltpu.SemaphoreType.DMA((7,)), pltpu.SemaphoreType.DMA((7,)), pltpu.SemaphoreType.DMA],
        compiler_params=pltpu.CompilerParams(vmem_limit_bytes=VMEM_LIMIT),
    )(v, *after)


_HBM = pl.BlockSpec(memory_space=pltpu.HBM)
_ANY = pl.BlockSpec(memory_space=pl.ANY)
_SEM = pl.BlockSpec(memory_space=pltpu.SEMAPHORE)
_EFFECT = pltpu.SideEffectType.DATAFLOW_SIDE_EFFECTING


def _ordered(body, n_in, after):
    k = len(after)
    if not k:
        return body
    return lambda *refs: body(*refs[:n_in], *refs[n_in + k:])


def _gather_copies(ins, lands, send_sems, recv_sems):
    x, y, c = _my_pos()
    p = 2 * x + y
    peers = [(x, 1 - y), (1 - x, y), (1 - x, 1 - y)]
    sends, recvs = [], []
    for k in range(len(ins)):
        for j, (qx, qy) in enumerate(peers):
            sems = dict(send_sem=send_sems.at[3 * k + j], recv_sem=recv_sems.at[3 * k + j],
                        device_id=(qx, qy, c), device_id_type=MESH)
            sends.append(pltpu.make_async_remote_copy(src_ref=ins[k], dst_ref=lands[k].at[p], **sems))
            recvs.append(pltpu.make_async_remote_copy(src_ref=ins[k], dst_ref=lands[k].at[2 * qx + qy], **sems))
    return sends, recvs


def _peer_gather_copies(peers):
    def copies(ins, lands, send_sems, recv_sems):
        x, y, c = _my_pos()
        where = [(x, 1 - y), (1 - x, y), (1 - x, 1 - y)]
        cps = [pltpu.make_async_remote_copy(
            src_ref=ins[0], dst_ref=lands[j], send_sem=send_sems.at[j], recv_sem=recv_sems.at[j],
            device_id=(*where[j], c), device_id_type=MESH) for j in peers]
        return cps, cps
    return copies


def _to_sibling_copies(ins, lands, send_sems, recv_sems):
    x, y, c = _my_pos()
    cps = [pltpu.make_async_remote_copy(
        src_ref=ins[k].at[:, 1 - c], dst_ref=lands[k], send_sem=send_sems.at[k], recv_sem=recv_sems.at[k],
        device_id=(x, y, 1 - c), device_id_type=MESH) for k in range(len(ins))]
    return cps, cps


def _chip_exchange_copies(ins, lands, send_sems, recv_sems):
    x, y, c = _my_pos()
    peers = [(x, 1 - y), (1 - x, y), (1 - x, 1 - y)]
    cps = []
    for k in range(len(ins)):
        for j, (qx, qy) in enumerate(peers):
            cps.append(pltpu.make_async_remote_copy(
                src_ref=ins[k].at[2 * qx + qy], dst_ref=lands[k].at[j], send_sem=send_sems.at[3 * k + j],
                recv_sem=recv_sems.at[3 * k + j], device_id=(qx, qy, c), device_id_type=MESH))
    return cps, cps


def _all_devices_copies(ins, lands, send_sems, recv_sems):
    x, y, c = _my_pos()
    me = 4 * x + 2 * y + c
    sends, recvs = [], []
    for r in range(1, N_DEV):
        px = 1 - x if r & 4 else x
        py = 1 - y if r & 2 else y
        pc = 1 - c if r & 1 else c
        sems = dict(send_sem=send_sems.at[r - 1], recv_sem=recv_sems.at[r - 1], device_id=(px, py, pc), device_id_type=MESH)
        sends.append(pltpu.make_async_remote_copy(src_ref=ins[0], dst_ref=lands[0].at[me], **sems))
        recvs.append(pltpu.make_async_remote_copy(src_ref=ins[0], dst_ref=lands[0].at[4 * px + 2 * py + pc], **sems))
    return sends, recvs


def _swap_copies(ins, lands, send_sems, recv_sems):
    x, y, c = _my_pos()
    cps = [pltpu.make_async_remote_copy(
        src_ref=ins[k], dst_ref=lands[k], send_sem=send_sems.at[k], recv_sem=recv_sems.at[k],
        device_id=(x, y, 1 - c), device_id_type=MESH) for k in range(len(ins))]
    return cps, cps


def _split_start(ins, land_shapes, copies, n_sems, name, after=()):
    n, nl = len(ins), len(land_shapes)
    first_out = n + nl + len(after)

    def body(*refs):
        in_refs, land_refs = refs[:n], refs[n:n + nl]
        send_sems, recv_sems = refs[first_out:first_out + 2]
        token = refs[-1]
        sends, _ = copies(in_refs, land_refs, send_sems, recv_sems)
        for cp in sends:
            cp.start()
        token[...] = jnp.zeros_like(token)

    lands = [pltpu.with_memory_space_constraint(lax.empty(s.shape, s.dtype), pltpu.HBM) for s in land_shapes]
    ins = [pltpu.with_memory_space_constraint(s, pltpu.HBM) for s in ins]
    return pl.pallas_call(
        body, name=name,
        out_shape=(pltpu.SemaphoreType.DMA((n_sems,)), pltpu.SemaphoreType.DMA((n_sems,)),
                   *[pltpu.HBM(s.shape, s.dtype) for s in ins], *[pltpu.HBM(s.shape, s.dtype) for s in lands],
                   jax.ShapeDtypeStruct((8, HEAD), F32)),
        in_specs=[_HBM] * (n + nl) + [pl.BlockSpec(memory_space=pl.ANY)] * len(after),
        out_specs=(_SEM, _SEM, *([_HBM] * (n + nl)), pl.BlockSpec(memory_space=pltpu.VMEM)),
        input_output_aliases={k: 2 + k for k in range(n + nl)},
        compiler_params=pltpu.CompilerParams(has_side_effects=_EFFECT),
    )(*ins, *lands, *after)


def _split_wait(started, n, copies, name, after=()):
    send_sems, recv_sems = started[0], started[1]
    bufs = started[2:-1]
    nb = len(bufs)

    def body(*refs):
        in_refs, land_refs = refs[:n], refs[n:nb]
        sends, recvs = copies(in_refs, land_refs, refs[nb], refs[nb + 1])
        for cp in sends:
            cp.wait_send()
        for cp in recvs:
            cp.wait_recv()

    outs = pl.pallas_call(
        body, name=name,
        out_shape=tuple(pltpu.HBM(s.shape, s.dtype) for s in bufs),
        in_specs=[_HBM] * nb + [_SEM, _SEM] + [pl.BlockSpec(memory_space=pl.ANY)] * len(after),
        out_specs=tuple([_HBM] * nb),
        input_output_aliases={k: k for k in range(nb)},
        compiler_params=pltpu.CompilerParams(has_side_effects=_EFFECT),
    )(*bufs, send_sems, recv_sems, *after)
    return list(outs[:n]), list(outs[n:])


def _fill_own_slot(gathered, shards, pidx, names):
    outs = []
    for g, s, name in zip(gathered, shards, names):
        r, cdim = s.shape
        tr = _row_tile(r)

        def body(p_ref, s_ref, g_ref, o_ref):
            o_ref[...] = s_ref[...]

        outs.append(pl.pallas_call(
            body, name=name,
            grid_spec=pltpu.PrefetchScalarGridSpec(
                num_scalar_prefetch=1, grid=(r // tr,),
                in_specs=[pl.BlockSpec((tr, cdim), lambda i, p: (i, 0)), pl.BlockSpec(memory_space=pl.ANY)],
                out_specs=pl.BlockSpec((None, tr, cdim), lambda i, p: (p[0], i, 0))),
            out_shape=jax.ShapeDtypeStruct(g.shape, g.dtype),
            input_output_aliases={2: 0},
            compiler_params=_cp(("arbitrary",)),
        )(pidx, s, g))
    return outs


def _sum_own_and_peers(own4, slots, pidx, name):
    _, rh, cdim = own4.shape
    tr = _row_tile(rh)

    def body(p_ref, own_ref, s_ref, o_ref):
        acc = own_ref[...].astype(F32)
        for j in range(3):
            acc = acc + s_ref[j].astype(F32)
        o_ref[...] = acc

    return pl.pallas_call(
        body, name=name,
        grid_spec=pltpu.PrefetchScalarGridSpec(
            num_scalar_prefetch=1, grid=(rh // tr,),
            in_specs=[pl.BlockSpec((None, tr, cdim), lambda i, p: (p[0], i, 0)),
                      pl.BlockSpec((3, tr, cdim), lambda i, p: (0, i, 0))],
            out_specs=pl.BlockSpec((tr, cdim), lambda i, p: (i, 0))),
        out_shape=jax.ShapeDtypeStruct((rh, cdim), F32),
        compiler_params=_cp(("arbitrary",)),
    )(pidx, own4, slots)


def _exchange(ins, land_shapes, copies, n_sems, name):
    n, nl = len(ins), len(land_shapes)

    def body(*refs):
        sends, recvs = copies(refs[:n], refs[n:n + nl], refs[n + nl], refs[n + nl + 1])
        for cp in sends:
            cp.start()
        for cp in sends:
            cp.wait_send()
        for cp in recvs:
            cp.wait_recv()

    any_spec = pl.BlockSpec(memory_space=pl.ANY)
    return pl.pallas_call(
        body, name=name,
        out_shape=[jax.ShapeDtypeStruct(s.shape, s.dtype) for s in land_shapes],
        in_specs=[any_spec] * n, out_specs=[any_spec] * nl,
        scratch_shapes=[pltpu.SemaphoreType.DMA((n_sems,)), pltpu.SemaphoreType.DMA((n_sems,))],
    )(*ins)


def _row_tile(r):
    t = min(TR_EW, r)
    while r % t:
        t //= 2
    return t


def _add_own_half(g4, recv, cidx, name):
    _, _, rh, cdim = g4.shape
    tr = _row_tile(rh)

    def body(c_ref, a_ref, b_ref, o_ref):
        o_ref[...] = (a_ref[...] + b_ref[...]).astype(BF16)

    return pl.pallas_call(
        body, name=name,
        grid_spec=pltpu.PrefetchScalarGridSpec(
            num_scalar_prefetch=1, grid=(N_CHIPS, rh // tr),
            in_specs=[pl.BlockSpec((None, None, tr, cdim), lambda q, i, c: (q, c[0], i, 0)),
                      pl.BlockSpec((None, tr, cdim), lambda q, i, c: (q, i, 0))],
            out_specs=pl.BlockSpec((None, tr, cdim), lambda q, i, c: (q, i, 0))),
        out_shape=jax.ShapeDtypeStruct(recv.shape, BF16),
        compiler_params=_cp(("arbitrary", "arbitrary")),
    )(cidx, g4, recv)


def _sum_slots(v, name):
    n, r, cdim = v.shape
    tr = _row_tile(r)

    def body(v_ref, o_ref):
        acc = v_ref[0].astype(F32)
        for k in range(1, n):
            acc = acc + v_ref[k].astype(F32)
        o_ref[...] = acc

    return pl.pallas_call(
        body, name=name, grid=(r // tr,),
        in_specs=[pl.BlockSpec((n, tr, cdim), lambda i: (0, i, 0))],
        out_specs=pl.BlockSpec((tr, cdim), lambda i: (i, 0)),
        out_shape=jax.ShapeDtypeStruct((r, cdim), F32),
        compiler_params=_cp(("arbitrary",)),
    )(v)


def _sum_devices(lands, own, didx, name):
    _, r, cdim = lands.shape
    tr = _row_tile(r)

    def body(d_ref, l_ref, own_ref, o_ref):
        acc = jnp.where(d_ref[0] == 0, own_ref[...], l_ref[0])
        for dv in range(1, N_DEV):
            acc = acc + jnp.where(d_ref[0] == dv, own_ref[...], l_ref[dv])
        o_ref[...] = acc

    return pl.pallas_call(
        body, name=name,
        grid_spec=pltpu.PrefetchScalarGridSpec(
            num_scalar_prefetch=1, grid=(r // tr,),
            in_specs=[pl.BlockSpec((N_DEV, tr, cdim), lambda i, dd: (0, i, 0)), pl.BlockSpec((tr, cdim), lambda i, dd: (i, 0))],
            out_specs=pl.BlockSpec((tr, cdim), lambda i, dd: (i, 0))),
        out_shape=jax.ShapeDtypeStruct((r, cdim), F32),
        compiler_params=_cp(("arbitrary",)),
    )(didx, lands, own)


def _adamw_math(wv, gg, mv, vv):
    nm = ADAM_B1 * mv + (1.0 - ADAM_B1) * gg
    nv = ADAM_B2 * vv + (1.0 - ADAM_B2) * (gg * gg)
    m_hat = nm / (1.0 - ADAM_B1 ** ADAM_STEP)
    v_hat = nv / (1.0 - ADAM_B2 ** ADAM_STEP)
    return -ADAM_LR * (m_hat / (jnp.sqrt(v_hat) + ADAM_EPS) + ADAM_WD * wv), nm, nv


def _adamw_halves(w, mine, theirs, m, v, cidx, name):
    r, cdim = w.shape
    rh = r // 2
    tr = _row_tile(rh)
    nblk = rh // tr

    def body(c_ref, w_ref, a_ref, b_ref, m_ref, v_ref, g_ref, d_ref, nm_ref, nv_ref):
        gg = jnp.where(pl.program_id(0) == c_ref[0], a_ref[...], b_ref[...])
        g_ref[...] = gg
        d_ref[...], nm_ref[...], nv_ref[...] = _adamw_math(w_ref[...], gg, m_ref[...], v_ref[...])

    full = pl.BlockSpec((tr, cdim), lambda hh, i, c: (hh * nblk + i, 0))
    half = pl.BlockSpec((tr, cdim), lambda hh, i, c: (i, 0))
    return pl.pallas_call(
        body, name=name,
        grid_spec=pltpu.PrefetchScalarGridSpec(
            num_scalar_prefetch=1, grid=(2, nblk),
            in_specs=[full, half, half, full, full], out_specs=[full] * 4),
        out_shape=[jax.ShapeDtypeStruct((r, cdim), F32)] * 4,
        compiler_params=_cp(("arbitrary", "arbitrary")),
    )(cidx, w, mine, theirs, m, v)


def _adamw(w, g, m, v, name):
    r, cdim = w.shape
    tr = _row_tile(r) if r % 8 == 0 else r

    def body(w_ref, g_ref, m_ref, v_ref, d_ref, nm_ref, nv_ref):
        d_ref[...], nm_ref[...], nv_ref[...] = _adamw_math(w_ref[...], g_ref[...], m_ref[...], v_ref[...])

    spec = pl.BlockSpec((tr, cdim), lambda i: (i, 0))
    return pl.pallas_call(
        body, name=name, grid=(r // tr,), in_specs=[spec] * 4, out_specs=[spec] * 3,
        out_shape=[jax.ShapeDtypeStruct((r, cdim), F32)] * 3,
        compiler_params=_cp(("arbitrary",)),
    )(w, g, m, v)


def _ada_fwd(c_all, w_ada, b_cols):
    nb, _ = c_all.shape
    n = w_ada.shape[1]

    def body(c_ref, w_ref, b_ref, o_ref):
        cv = c_ref[...]
        o_ref[...] = _mm(cv * _sigmoid(cv), w_ref[...]) + b_ref[...]

    return pl.pallas_call(
        body, name="ada_fwd", out_shape=jax.ShapeDtypeStruct((nb, n), F32),
        compiler_params=pltpu.CompilerParams(vmem_limit_bytes=VMEM_LIMIT),
    )(c_all, w_ada, b_cols)


def _ada_bwd(c_all, dmod_all, dmod_cols):
    d = c_all.shape[1]
    n = dmod_cols.shape[1]

    def body(c_ref, da_ref, dc_ref, gw_ref, gb_ref):
        cv = c_ref[...]
        gw_ref[...] = _mm_tn(cv * _sigmoid(cv), dc_ref[...])
        gb_ref[...] = _colsum(da_ref[...])

    return pl.pallas_call(
        body, name="ada_bwd",
        out_shape=[jax.ShapeDtypeStruct((d, n), F32), jax.ShapeDtypeStruct((1, dmod_all.shape[1]), F32)],
        compiler_params=pltpu.CompilerParams(vmem_limit_bytes=VMEM_LIMIT),
    )(c_all, dmod_all, dmod_cols)


def _proj_fwd(x2, modv, ws, cols, b_in, seq, name, proj_in=None):
    t, d = x2.shape
    n = len(ws)
    ns = ws[0].shape[1]
    tm = min(TM_PROJ, seq)
    tpb = seq // tm
    first = proj_in is None

    def body(c_ref, x_ref, mod_ref, *refs):
        w_refs, b_ref = refs[:n], refs[n]
        outs = refs[n + 1 if first else n + 2:]
        proj_ref, h_s = outs[0], outs[-1]
        s = pl.program_id(1)

        @pl.when(s == 0)
        def _():
            h = (x_ref[...] * (1.0 + mod_ref[1:2, :]) + mod_ref[0:1, :]).astype(BF16)
            h_s[...] = h
            if first:
                outs[1][...] = h

        for k in range(n):
            @pl.when(s == k)
            def _():
                proj_ref[...] = (jnp.dot(h_s[...], w_refs[k][...], preferred_element_type=F32) + b_ref[...]).astype(BF16)

    in_specs = [pl.BlockSpec((tm, d), lambda i, s, c: (i, 0)),
                pl.BlockSpec((None, 8, d), lambda i, s, c: (i // tpb, 0, 0))]
    in_specs += [pl.BlockSpec((d, ns), lambda i, s, c: (0, 0))] * n
    in_specs += [pl.BlockSpec((1, ns), lambda i, s, c: (0, c[s]))]
    out_specs = [pl.BlockSpec((tm, ns), lambda i, s, c: (i, c[s]))]
    out_shape = [jax.ShapeDtypeStruct((t, N_CHIPS * ns), BF16)]
    args = [cols, x2, modv, *ws, b_in]
    aliases = {}
    if first:
        out_specs.append(pl.BlockSpec((tm, d), lambda i, s, c: (i, 0)))
        out_shape.append(jax.ShapeDtypeStruct((t, d), BF16))
    else:
        in_specs.append(_ANY)
        args.append(proj_in)
        aliases = {len(args) - 1: 0}
    return pl.pallas_call(
        body, name=name,
        grid_spec=pltpu.PrefetchScalarGridSpec(
            num_scalar_prefetch=1, grid=(t // tm, n), in_specs=in_specs, out_specs=out_specs,
            scratch_shapes=[pltpu.VMEM((tm, d), BF16)]),
        out_shape=out_shape, input_output_aliases=aliases,
        compiler_params=_cp(("arbitrary", "arbitrary")),
    )(*args)


def _lru_gates(xl, wc_ref, bc_ref, wa_ref, ba_ref, wx_ref, bx_ref, lam_ref):
    xc = bc_ref[...] + wc_ref[CONV_WIDTH - 1:CONV_WIDTH, :] * xl
    for k in range(CONV_WIDTH - 1):
        xc = xc + wc_ref[k:k + 1, :] * _shift_down(xl, CONV_WIDTH - 1 - k)
    r = _sigmoid(_mm(xc, wa_ref[...]) + ba_ref[...])
    gi = _sigmoid_t(_mm(xc, wx_ref[...]) + bx_ref[...])
    nl = -lam_ref[...]
    e = jnp.exp(-jnp.abs(nl))
    u = 1.0 + e
    dlt = u - 1.0
    log1p_e = jnp.where(dlt == 0.0, e, jnp.log(u) * (e / jnp.where(dlt == 0.0, 1.0, dlt)))
    big_l = -LRU_C * (jnp.maximum(nl, 0.0) + log1p_e)
    la = big_l * r
    a = jnp.exp(la)
    m2 = jnp.tanh(-la) * (a * a + 1.0)
    return xc, r, gi, big_l, a, m2


def _lru_prep(proj, lru_w, nb, seq):
    t = proj.shape[0]
    w = LRU_HEADS * HEAD
    w_conv, b_conv, w_a, b_a, w_x, b_x, lam = lru_w

    def body(x_ref, wc_ref, bc_ref, wa_ref, ba_ref, wx_ref, bx_ref, lam_ref, a_ref, inp_ref):
        xc, r, gi, big_l, a, m2 = _lru_gates(x_ref[...].astype(F32), wc_ref, bc_ref, wa_ref, ba_ref, wx_ref, bx_ref, lam_ref)
        a_ref[...] = a
        inp_ref[...] = jnp.sqrt(m2) * (gi * xc)

    col = lambda b, hd: (0, hd)
    head = lambda b, hd: (hd, 0, 0)
    tok = lambda b, hd: (b, hd)
    return pl.pallas_call(
        body, name="lru_prep", grid=(nb, LRU_HEADS),
        in_specs=[pl.BlockSpec((seq, HEAD), tok),
                  pl.BlockSpec((CONV_WIDTH, HEAD), col), pl.BlockSpec((1, HEAD), col),
                  pl.BlockSpec((None, HEAD, HEAD), head), pl.BlockSpec((1, HEAD), col),
                  pl.BlockSpec((None, HEAD, HEAD), head), pl.BlockSpec((1, HEAD), col),
                  pl.BlockSpec((1, HEAD), col)],
        out_specs=[pl.BlockSpec((seq, HEAD), tok)] * 2,
        out_shape=[jax.ShapeDtypeStruct((t, w), F32)] * 2,
        compiler_params=_cp(("arbitrary", "arbitrary")),
    )(proj, w_conv, b_conv, w_a, b_a, w_x, b_x, lam)


def _scan(a3, b3, reverse, name):
    nb, seq, w = a3.shape
    tc = min(TC_SCAN, seq)
    nchunk = seq // tc
    ntile = tc // 8

    def combine(av, bv):
        rows = lax.broadcasted_iota(jnp.int32, av.shape, 0)
        for s in (1, 2, 4):
            if reverse:
                keep = rows < 8 - s
                a_sh, b_sh = pltpu.roll(av, 8 - s, 0), pltpu.roll(bv, 8 - s, 0)
            else:
                keep = rows >= s
                a_sh, b_sh = pltpu.roll(av, s, 0), pltpu.roll(bv, s, 0)
            bv = jnp.where(keep, bv + av * b_sh, bv)
            av = jnp.where(keep, av * a_sh, av)
        return av, bv

    def body(a_ref, b_ref, h_ref, carry):
        @pl.when(pl.program_id(0) == 0)
        def _():
            carry[...] = jnp.zeros_like(carry)

        for b in range(nb):
            def tile(j, hprev):
                jj = ntile - 1 - j if reverse else j
                base = pl.multiple_of(jj * 8, 8)
                av, bv = a_ref[b, pl.ds(base, 8), :], b_ref[b, pl.ds(base, 8), :]
                av, bv = combine(av, av * bv if reverse else bv)
                h = bv + av * hprev
                h_ref[b, pl.ds(base, 8), :] = h
                edge = h[0:1, :] if reverse else h[7:8, :]
                return jnp.broadcast_to(edge, (8, w))

            carry[b] = lax.fori_loop(0, ntile, tile, carry[b])

    imap = (lambda i: (0, nchunk - 1 - i, 0)) if reverse else (lambda i: (0, i, 0))
    spec = pl.BlockSpec((nb, tc, w), imap)
    return pl.pallas_call(
        body, name=name, grid=(nchunk,), in_specs=[spec, spec], out_specs=spec,
        out_shape=jax.ShapeDtypeStruct((nb, seq, w), F32),
        scratch_shapes=[pltpu.VMEM((nb, 8, w), F32)],
        compiler_params=_cp(("arbitrary",)),
    )(a3, b3)


def _sgu_mask():
    ti = lax.broadcasted_iota(jnp.int32, (HEAD, HEAD), 0) // SGU_CHUNK
    si = lax.broadcasted_iota(jnp.int32, (HEAD, HEAD), 1) // SGU_CHUNK
    return si <= ti


def _sgu_specs(tm, d_sgu):
    pw = 256
    first_u = (2 * LRU_HEADS * HEAD) // pw
    n_piece = d_sgu // pw
    specs = [pl.BlockSpec((tm, pw), functools.partial(lambda i, k: (i, k), k=first_u + j)) for j in range(2 * n_piece)]
    return specs, n_piece


def _sgu_fwd(proj, w_sp, b_sp_t, ln_g, ln_b):
    t = proj.shape[0]
    d_sgu = SGU_GROUPS * HEAD
    tm = min(TM_SGU, t)
    nblk = tm // HEAD
    specs, n_piece = _sgu_specs(tm, d_sgu)

    def body(*refs):
        u = jnp.concatenate([r[...] for r in refs[:n_piece]], axis=1).astype(F32)
        v = jnp.concatenate([r[...] for r in refs[n_piece:2 * n_piece]], axis=1).astype(F32)
        w_ref, bt_ref, g_ref, b_ref, y_ref = refs[2 * n_piece:]
        ug = _gelu(u)
        xhat, _ = _ln_stats(_gelu(v))
        vn = (xhat * g_ref[...] + b_ref[...]).astype(BF16)
        mask = _sgu_mask()
        for g in range(SGU_GROUPS):
            wm = jnp.where(mask, w_ref[g], 0.0).astype(BF16)
            cols = slice(g * HEAD, (g + 1) * HEAD)
            for n in range(nblk):
                rows = slice(n * HEAD, (n + 1) * HEAD)
                mixed = jnp.dot(wm, vn[rows, cols], preferred_element_type=F32) + bt_ref[:, g:g + 1]
                y_ref[rows, cols] = (ug[rows, cols] * mixed).astype(BF16)

    full = lambda shape: pl.BlockSpec(shape, lambda i: (0,) * len(shape))
    return pl.pallas_call(
        body, name="sgu_fwd", grid=(t // tm,),
        in_specs=specs + [full(w_sp.shape), full(b_sp_t.shape), full(ln_g.shape), full(ln_b.shape)],
        out_specs=pl.BlockSpec((tm, d_sgu), lambda i: (i, 0)),
        out_shape=jax.ShapeDtypeStruct((t, d_sgu), BF16),
        compiler_params=_cp(("arbitrary",)),
    )(*([proj] * (2 * n_piece)), w_sp, b_sp_t, ln_g, ln_b)


def _mix_fwd(hs, proj, y_sgu, x2, modv, w_o_lru_g, w_o_sgu_g, w_out_g, ln1_g, ln1_b, seq):
    t, d = x2.shape
    w = hs.shape[1]
    d_sgu = y_sgu.shape[1]
    nq, _, ns = w_o_sgu_g.shape
    tm = min(TM_MIX, seq)
    tpb = seq // tm

    def body(hs_ref, gl_ref, ys_ref, ga_ref, gb_ref, x_ref, mod_ref, wl_ref, ws_ref, wo_ref, g1_ref, b1_ref,
             yap_ref, ya_ref, yb_ref, mg_ref, mix_ref, x1_ref):
        yap = (hs_ref[...] * _gelu(gl_ref[...].astype(F32))).astype(BF16)
        yap_ref[...] = yap
        y_a = jnp.dot(yap, wl_ref[...], preferred_element_type=F32)
        ys = ys_ref[...]
        y_b = jnp.concatenate([jnp.dot(ys, ws_ref[q], preferred_element_type=F32) for q in range(nq)], axis=1)
        ya_ref[...] = y_a.astype(BF16)
        yb_ref[...] = y_b.astype(BF16)
        merged = (_sigmoid_t(ga_ref[...].astype(F32)) * y_a + _sigmoid_t(gb_ref[...].astype(F32)) * y_b).astype(BF16)
        mg_ref[...] = merged
        mix = jnp.dot(merged, wo_ref[...], preferred_element_type=F32)
        mix_ref[...] = mix
        xhat, _ = _ln_stats(ALPHA * x_ref[...] + (1.0 + mod_ref[2:3, :]) * mix)
        x1_ref[...] = xhat * g1_ref[...] + b1_ref[...]

    row = lambda width, col: pl.BlockSpec((tm, width), functools.partial(lambda i, k: (i, k), k=col))
    full = lambda shape: pl.BlockSpec(shape, lambda i: (0,) * len(shape))
    return pl.pallas_call(
        body, name="mix_fwd", grid=(t // tm,),
        in_specs=[row(w, 0), row(w, 1), row(d_sgu, 0), row(d, 4), row(d, 5), row(d, 0),
                  pl.BlockSpec((None, 8, d), lambda i: (i // tpb, 0, 0)),
                  full(w_o_lru_g.shape), full(w_o_sgu_g.shape), full(w_out_g.shape), full(ln1_g.shape), full(ln1_b.shape)],
        out_specs=[row(w, 0), row(d, 0), row(d, 0), row(d, 0), row(d, 0), row(d, 0)],
        out_shape=[jax.ShapeDtypeStruct((t, w), BF16), jax.ShapeDtypeStruct((t, d), BF16),
                   jax.ShapeDtypeStruct((t, d), BF16), jax.ShapeDtypeStruct((t, d), BF16),
                   jax.ShapeDtypeStruct((t, d), F32), jax.ShapeDtypeStruct((t, d), F32)],
        compiler_params=_cp(("arbitrary",)),
    )(hs, proj, y_sgu, proj, proj, x2, modv, w_o_lru_g, w_o_sgu_g, w_out_g, ln1_g, ln1_b)


def _mlp_fwd(x1, modv, w_up_g, w_down_g, ln2_g, ln2_b, target, nb, seq):
    t, d = x1.shape
    nq, _, ns = w_up_g.shape
    tm = min(TM_MLP, seq)
    ts = min(TS_MLP, tm)
    tpb = seq // tm

    def body(x1_ref, mod_ref, wu_hbm, wd_hbm, g2_ref, b2_ref, tg_ref,
             rl_ref, act_ref, h2_ref, dz2_ref, df_ref, st_ref, pb_ref, wu_s, wd_s, acc, sems):
        i = pl.program_id(0)

        @pl.when(i == 0)
        def _():
            _load_weights((wu_hbm, wd_hbm), (wu_s, wd_s), sems)
            st_ref[...] = jnp.zeros_like(st_ref)

        @pl.when(i % tpb == 0)
        def _():
            pb_ref[...] = jnp.zeros_like(pb_ref)

        for sub in range(tm // ts):
            rows = slice(sub * ts, (sub + 1) * ts)
            x1v = x1_ref[rows, :]
            h2 = (x1v * (1.0 + mod_ref[4:5, :]) + mod_ref[3:4, :]).astype(BF16)
            h2_ref[rows, :] = h2
            for k in range(nq):
                cols = slice(k * ns, (k + 1) * ns)
                r = jnp.maximum(jnp.dot(h2, wu_s[k], preferred_element_type=F32), 0.0)
                act = (r * r).astype(BF16)
                rl_ref[rows, cols] = r.astype(BF16)
                act_ref[rows, cols] = act
                part = jnp.dot(act, wd_s[cols, :], preferred_element_type=F32)
                if k == 0:
                    acc[sub] = part
                else:
                    acc[sub] += part
            f = acc[sub]
            xhat, rstd = _ln_stats(ALPHA * x1v + (1.0 + mod_ref[5:6, :]) * f)
            y = xhat * g2_ref[...] + b2_ref[...]
            err = y - tg_ref[rows, :]
            dy = err * (1.0 / d)
            dz2 = _ln_bwd(dy * g2_ref[...], xhat, rstd)
            dz2_ref[rows, :] = dz2
            df_ref[rows, :] = ((1.0 + mod_ref[5:6, :]) * dz2).astype(BF16)
            st_ref[0:1, :] += _colsum(dy * xhat)
            st_ref[1:2, :] += _colsum(dy)
            st_ref[2:3, :] += (0.5 / d) * jnp.sum(_colsum(err * err), axis=1, keepdims=True)
            pb_ref[0:1, :] += _colsum(dz2 * f)

    tok = lambda i: (i, 0)
    return pl.pallas_call(
        body, name="mlp_fwd", grid=(t // tm,),
        in_specs=[pl.BlockSpec((tm, d), tok), pl.BlockSpec((None, 8, d), lambda i: (i // tpb, 0, 0)), _ANY, _ANY,
                  pl.BlockSpec((1, d), lambda i: (0, 0)), pl.BlockSpec((1, d), lambda i: (0, 0)),
                  pl.BlockSpec((tm, d), tok)],
        out_specs=[pl.BlockSpec((tm, nq * ns), tok), pl.BlockSpec((tm, nq * ns), tok),
                   pl.BlockSpec((tm, d), tok), pl.BlockSpec((tm, d), tok), pl.BlockSpec((tm, d), tok),
                   pl.BlockSpec((8, d), lambda i: (0, 0)), pl.BlockSpec((None, 8, d), lambda i: (i // tpb, 0, 0))],
        out_shape=[jax.ShapeDtypeStruct((t, nq * ns), BF16), jax.ShapeDtypeStruct((t, nq * ns), BF16),
                   jax.ShapeDtypeStruct((t, d), BF16),
                   jax.ShapeDtypeStruct((t, d), F32), jax.ShapeDtypeStruct((t, d), BF16),
                   jax.ShapeDtypeStruct((8, d), F32), jax.ShapeDtypeStruct((nb, 8, d), F32)],
        scratch_shapes=[pltpu.VMEM(w_up_g.shape, BF16), pltpu.VMEM(w_down_g.shape, BF16),
                        pltpu.VMEM((tm // ts, ts, d), F32), pltpu.SemaphoreType.DMA((2,))],
        compiler_params=_cp(("arbitrary",)),
    )(x1, modv, w_up_g, w_down_g, ln2_g, ln2_b, target)


def _mlp_bwd(df, up, w_down_g, w_up_g, dz2, x2, mix, modv, ln1_g, ln1_b, nb, seq):
    t, d = x2.shape
    nq, _, ns = w_up_g.shape
    tm = min(TM_MLP, seq)
    ts = min(TS_MLP, tm)
    tpb = seq // tm

    def body(df_ref, rl_ref, wd_hbm, wu_hbm, dz2_ref, x_ref, mix_ref, mod_ref, g1_ref, b1_ref,
             dup_ref, dz1_ref, dmix_ref, st_ref, pb_ref, wd_s, wu_s, acc, sems):
        i = pl.program_id(0)

        @pl.when(i == 0)
        def _():
            _load_weights((wd_hbm, wu_hbm), (wd_s, wu_s), sems)
            st_ref[...] = jnp.zeros_like(st_ref)

        @pl.when(i % tpb == 0)
        def _():
            pb_ref[...] = jnp.zeros_like(pb_ref)

        for sub in range(tm // ts):
            rows = slice(sub * ts, (sub + 1) * ts)
            dfv = df_ref[rows, :]
            for k in range(nq):
                cols = slice(k * ns, (k + 1) * ns)
                dup = (_mm_nt(dfv, wd_s[cols, :]) * (2.0 * rl_ref[rows, cols].astype(F32))).astype(BF16)
                dup_ref[rows, cols] = dup
                part = _mm_nt(dup, wu_s[k])
                if k == 0:
                    acc[sub] = part
                else:
                    acc[sub] += part
            dh2 = acc[sub]
            mix = mix_ref[rows, :]
            xhat, rstd = _ln_stats(ALPHA * x_ref[rows, :] + (1.0 + mod_ref[2:3, :]) * mix)
            x1 = xhat * g1_ref[...] + b1_ref[...]
            dx1 = ALPHA * dz2_ref[rows, :] + dh2 * (1.0 + mod_ref[4:5, :])
            dz1 = _ln_bwd(dx1 * g1_ref[...], xhat, rstd)
            dz1_ref[rows, :] = dz1
            dmix_ref[rows, :] = ((1.0 + mod_ref[2:3, :]) * dz1).astype(BF16)
            st_ref[0:1, :] += _colsum(dx1 * xhat)
            st_ref[1:2, :] += _colsum(dx1)
            pb_ref[0:1, :] += _colsum(dh2 * x1)
            pb_ref[1:2, :] += _colsum(dh2)
            pb_ref[2:3, :] += _colsum(dz1 * mix)

    tok = lambda i: (i, 0)
    return pl.pallas_call(
        body, name="mlp_bwd", grid=(t // tm,),
        in_specs=[pl.BlockSpec((tm, d), tok), pl.BlockSpec((tm, nq * ns), tok), _ANY, _ANY,
                  pl.BlockSpec((tm, d), tok), pl.BlockSpec((tm, d), tok), pl.BlockSpec((tm, d), tok),
                  pl.BlockSpec((None, 8, d), lambda i: (i // tpb, 0, 0)),
                  pl.BlockSpec((1, d), lambda i: (0, 0)), pl.BlockSpec((1, d), lambda i: (0, 0))],
        out_specs=[pl.BlockSpec((tm, nq * ns), tok),
                   pl.BlockSpec((tm, d), tok), pl.BlockSpec((tm, d), tok),
                   pl.BlockSpec((8, d), lambda i: (0, 0)), pl.BlockSpec((None, 8, d), lambda i: (i // tpb, 0, 0))],
        out_shape=[jax.ShapeDtypeStruct((t, nq * ns), BF16),
                   jax.ShapeDtypeStruct((t, d), F32), jax.ShapeDtypeStruct((t, d), BF16),
                   jax.ShapeDtypeStruct((8, d), F32), jax.ShapeDtypeStruct((nb, 8, d), F32)],
        scratch_shapes=[pltpu.VMEM(w_down_g.shape, BF16), pltpu.VMEM(w_up_g.shape, BF16),
                        pltpu.VMEM((tm // ts, ts, d), F32), pltpu.SemaphoreType.DMA((2,))],
        compiler_params=_cp(("arbitrary",), VMEM_LIMIT_MAX),
    )(df, up, w_down_g, w_up_g, dz2, x2, mix, modv, ln1_g, ln1_b)


def _mix_bwd(dmix, proj, y_a, y_b, hs, w_out_g, w_o_lru_g, w_o_sgu_g, seq, after=()):
    t, d = dmix.shape
    w = hs.shape[1]
    nq, d_sgu, ns = w_o_sgu_g.shape
    tm = min(TM_MIX, seq)

    def body(dmix_ref, ga_ref, gb_ref, ya_ref, yb_ref, gl_ref, hs_ref, wo_ref, wl_ref, ws_ref,
             dya_ref, dyb_ref, dga_ref, dgb_ref, dgl_ref, dyl_ref, dys_ref):
        dmerged = _mm_nt(dmix_ref[...], wo_ref[...])
        sa, sb = _sigmoid_t(ga_ref[...].astype(F32)), _sigmoid_t(gb_ref[...].astype(F32))
        dy_a = (dmerged * sa).astype(BF16)
        dy_b = (dmerged * sb).astype(BF16)
        dya_ref[...] = dy_a
        dyb_ref[...] = dy_b
        dga_ref[...] = (dmerged * ya_ref[...].astype(F32) * (sa * (1.0 - sa))).astype(BF16)
        dgb_ref[...] = (dmerged * yb_ref[...].astype(F32) * (sb * (1.0 - sb))).astype(BF16)
        dyap = _mm_nt(dy_a, wl_ref[...])
        gel, dgel = _gelu_and_grad(gl_ref[...].astype(F32))
        dyl_ref[...] = dyap * gel
        dgl_ref[...] = (dyap * hs_ref[...] * dgel).astype(BF16)
        dys = _mm_nt(dy_b[:, 0:ns], ws_ref[0])
        for q in range(1, nq):
            dys = dys + _mm_nt(dy_b[:, q * ns:(q + 1) * ns], ws_ref[q])
        dys_ref[...] = dys

    row = lambda width, col: pl.BlockSpec((tm, width), functools.partial(lambda i, k: (i, k), k=col))
    full = lambda shape: pl.BlockSpec(shape, lambda i: (0,) * len(shape))
    return pl.pallas_call(
        _ordered(body, 10, after), name="mix_bwd", grid=(t // tm,),
        in_specs=[row(d, 0), row(d, 4), row(d, 5), row(d, 0), row(d, 0), row(w, 1), row(w, 0),
                  full(w_out_g.shape), full(w_o_lru_g.shape), full(w_o_sgu_g.shape)] + [_ANY] * len(after),
        out_specs=[row(d, 0), row(d, 0), row(d, 0), row(d, 0), row(w, 0), row(w, 0), row(d_sgu, 0)],
        out_shape=[jax.ShapeDtypeStruct((t, d), BF16), jax.ShapeDtypeStruct((t, d), BF16),
                   jax.ShapeDtypeStruct((t, d), BF16), jax.ShapeDtypeStruct((t, d), BF16),
                   jax.ShapeDtypeStruct((t, w), BF16), jax.ShapeDtypeStruct((t, w), F32),
                   jax.ShapeDtypeStruct((t, d_sgu), F32)],
        compiler_params=_cp(("arbitrary",)),
    )(dmix, proj, proj, y_a, y_b, proj, hs, w_out_g, w_o_lru_g, w_o_sgu_g, *after)


def _sgu_bwd(proj, dys, w_sp, b_sp_t, ln_g, ln_b, after=()):
    t = proj.shape[0]
    d_sgu = SGU_GROUPS * HEAD
    tm = min(TM_SGU, t)
    nblk = tm // HEAD
    specs, n_piece = _sgu_specs(tm, d_sgu)

    def body(*refs):
        u = jnp.concatenate([r[...] for r in refs[:n_piece]], axis=1).astype(F32)
        v = jnp.concatenate([r[...] for r in refs[n_piece:2 * n_piece]], axis=1).astype(F32)
        dys_ref, w_ref, bt_ref, g_ref, b_ref, du_ref, dv_ref, dw_ref, st_ref, dbt_ref, dvn_s = refs[2 * n_piece:]

        @pl.when(pl.program_id(0) == 0)
        def _():
            dw_ref[...] = jnp.zeros_like(dw_ref)
            st_ref[...] = jnp.zeros_like(st_ref)
            dbt_ref[...] = jnp.zeros_like(dbt_ref)

        ug, dug_du = _gelu_and_grad(u)
        vg, dvg_dv = _gelu_and_grad(v)
        xhat, rstd = _ln_stats(vg)
        vn = (xhat * g_ref[...] + b_ref[...]).astype(BF16)
        dys_v = dys_ref[...]
        mask = _sgu_mask()
        for g in range(SGU_GROUPS):
            wm = jnp.where(mask, w_ref[g], 0.0).astype(BF16)
            cols = slice(g * HEAD, (g + 1) * HEAD)
            dw_g = jnp.zeros((HEAD, HEAD), F32)
            db_g = jnp.zeros((HEAD, 1), F32)
            for n in range(nblk):
                rows = slice(n * HEAD, (n + 1) * HEAD)
                vn_blk = vn[rows, cols]
                mixed = jnp.dot(wm, vn_blk, preferred_element_type=F32) + bt_ref[:, g:g + 1]
                dy_blk = dys_v[rows, cols]
                du_ref[rows, cols] = (dy_blk * mixed * dug_du[rows, cols]).astype(BF16)
                dmx = dy_blk * ug[rows, cols]
                dvn_s[rows, cols] = _mm_tn(wm, dmx)
                dw_g = dw_g + _mm_nt(dmx, vn_blk)
                db_g = db_g + jnp.sum(dmx, axis=1, keepdims=True)
            dw_ref[g] += jnp.where(mask, dw_g, 0.0)
            dbt_ref[:, g:g + 1] += db_g
        dvn = dvn_s[...]
        st_ref[0:1, :] += _colsum(dvn * xhat)
        st_ref[1:2, :] += _colsum(dvn)
        dv_ref[...] = (_ln_bwd(dvn * g_ref[...], xhat, rstd) * dvg_dv).astype(BF16)

    full = lambda shape: pl.BlockSpec(shape, lambda i: (0,) * len(shape))
    tok = pl.BlockSpec((tm, d_sgu), lambda i: (i, 0))
    return pl.pallas_call(
        _ordered(body, 2 * n_piece + 5, after), name="sgu_bwd", grid=(t // tm,),
        in_specs=specs + [tok, full(w_sp.shape), full(b_sp_t.shape), full(ln_g.shape), full(ln_b.shape)]
        + [_ANY] * len(after),
        out_specs=[tok, tok, full(w_sp.shape), full((8, d_sgu)), full((HEAD, HEAD))],
        out_shape=[jax.ShapeDtypeStruct((t, d_sgu), BF16), jax.ShapeDtypeStruct((t, d_sgu), BF16),
                   jax.ShapeDtypeStruct(w_sp.shape, F32), jax.ShapeDtypeStruct((8, d_sgu), F32),
                   jax.ShapeDtypeStruct((HEAD, HEAD), F32)],
        scratch_shapes=[pltpu.VMEM((tm, d_sgu), F32)],
        compiler_params=_cp(("arbitrary",)),
    )(*([proj] * (2 * n_piece)), dys, w_sp, b_sp_t, ln_g, ln_b, *after)


def _lru_bwd(proj, hs, e, dyl, lru_w, nb, seq, after=()):
    t = proj.shape[0]
    w = LRU_HEADS * HEAD
    w_conv, b_conv, w_a, b_a, w_x, b_x, lam = lru_w

    def body(x_ref, hs_ref, e_ref, dy_ref, wc_ref, bc_ref, wa_ref, ba_ref, wx_ref, bx_ref, lam_ref,
             dxl_ref, dwa_ref, dwx_ref, st_ref):
        @pl.when(pl.program_id(1) == 0)
        def _():
            dwa_ref[...] = jnp.zeros_like(dwa_ref)
            dwx_ref[...] = jnp.zeros_like(dwx_ref)
            st_ref[...] = jnp.zeros_like(st_ref)

        xl = x_ref[...].astype(F32)
        xc, r, gi, big_l, a, m2 = _lru_gates(xl, wc_ref, bc_ref, wa_ref, ba_ref, wx_ref, bx_ref, lam_ref)
        inv_mult = lax.rsqrt(m2)
        mult = m2 * inv_mult
        dh = dy_ref[...] + _shift_up(e_ref[...], 1)
        da = dh * _shift_down(hs_ref[...], 1)
        dmult = dh * (gi * xc)
        d_i = dh * (mult * xc)
        dxc = dh * (mult * gi)
        dla = a * (da - dmult * (a * inv_mult))
        dr = dla * big_l
        d_big_l = _colsum(dla * r)
        dra = dr * (r * (1.0 - r))
        dia = d_i * (gi * (1.0 - gi))
        dwa_ref[...] += _mm_tn(xc, dra)
        dwx_ref[...] += _mm_tn(xc, dia)
        dxc = dxc + _mm_nt(dra, wa_ref[...]) + _mm_nt(dia, wx_ref[...])
        dxl = wc_ref[CONV_WIDTH - 1:CONV_WIDTH, :] * dxc
        st_ref[4 + CONV_WIDTH - 1:4 + CONV_WIDTH, :] += _colsum(dxc * xl)
        for k in range(CONV_WIDTH - 1):
            ahead = _shift_up(dxc, CONV_WIDTH - 1 - k)
            dxl = dxl + wc_ref[k:k + 1, :] * ahead
            st_ref[4 + k:5 + k, :] += _colsum(ahead * xl)
        dxl_ref[...] = dxl.astype(BF16)
        st_ref[0:1, :] += _colsum(dra)
        st_ref[1:2, :] += _colsum(dia)
        st_ref[2:3, :] += d_big_l * (LRU_C * _sigmoid(-lam_ref[...]))
        st_ref[3:4, :] += _colsum(dxc)

    col = lambda hd, b: (0, hd)
    head = lambda hd, b: (hd, 0, 0)
    tok = lambda hd, b: (b, hd)
    seq_blk = pl.BlockSpec((seq, HEAD), tok)
    return pl.pallas_call(
        _ordered(body, 11, after), name="lru_bwd", grid=(LRU_HEADS, nb),
        in_specs=[seq_blk, seq_blk, seq_blk, seq_blk,
                  pl.BlockSpec((CONV_WIDTH, HEAD), col), pl.BlockSpec((1, HEAD), col),
                  pl.BlockSpec((None, HEAD, HEAD), head), pl.BlockSpec((1, HEAD), col),
                  pl.BlockSpec((None, HEAD, HEAD), head), pl.BlockSpec((1, HEAD), col),
                  pl.BlockSpec((1, HEAD), col)] + [_ANY] * len(after),
        out_specs=[seq_blk, pl.BlockSpec((None, HEAD, HEAD), head), pl.BlockSpec((None, HEAD, HEAD), head),
                   pl.BlockSpec((8, HEAD), col)],
        out_shape=[jax.ShapeDtypeStruct((t, w), BF16), jax.ShapeDtypeStruct((LRU_HEADS, HEAD, HEAD), F32),
                   jax.ShapeDtypeStruct((LRU_HEADS, HEAD, HEAD), F32), jax.ShapeDtypeStruct((8, w), F32)],
        compiler_params=_cp(("arbitrary", "arbitrary")),
    )(proj, hs, e, dyl, w_conv, b_conv, w_a, b_a, w_x, b_x, lam, *after)


def _weight_grad(a, g, col_shards, name, after=()):
    t, k = a.shape
    n = g.shape[1]
    tt = min(TT_DW, t)
    tk = k if k <= 1536 else 1024
    ns = n // N_CHIPS if col_shards else n
    narrow = col_shards and ns < 512
    tn = n if narrow else min(ns, 768 if ns % 768 == 0 else 1024)
    while ns % tn and not narrow:
        tn //= 2
    per = max(ns // tn, 1)

    def body(a_ref, g_ref, o_ref):
        @pl.when(pl.program_id(2) == 0)
        def _():
            o_ref[...] = jnp.zeros_like(o_ref)

        res = _mm_tn(a_ref[...], g_ref[...])
        if narrow:
            for q in range(N_CHIPS):
                o_ref[q] += res[:, q * ns:(q + 1) * ns]
        else:
            o_ref[...] += res

    if narrow:
        out_spec = pl.BlockSpec((N_CHIPS, tk, ns), lambda i, j, s: (0, i, 0))
        out_shape = jax.ShapeDtypeStruct((N_CHIPS, k, ns), F32)
    elif col_shards:
        out_spec = pl.BlockSpec((None, tk, tn), lambda i, j, s: (j // per, i, j % per))
        out_shape = jax.ShapeDtypeStruct((N_CHIPS, k, ns), F32)
    else:
        out_spec = pl.BlockSpec((tk, tn), lambda i, j, s: (i, j))
        out_shape = jax.ShapeDtypeStruct((k, n), F32)
    return pl.pallas_call(
        _ordered(body, 2, after), name=name, grid=(k // tk, n // tn, t // tt),
        in_specs=[pl.BlockSpec((tt, tk), lambda i, j, s: (s, i)), pl.BlockSpec((tt, tn), lambda i, j, s: (s, j))]
        + [_ANY] * len(after),
        out_specs=out_spec, out_shape=out_shape,
        compiler_params=_cp(("arbitrary", "arbitrary", "arbitrary")),
    )(a, g, *after)


def _input_grad(dproj, ws, slots, dz1, x2, modv, nb, seq, after=()):
    t, d = x2.shape
    nq = len(ws)
    ns = ws[0].shape[1]
    tm = min(TM_DH, seq)
    ts = min(TS_MLP, tm)
    tpb = seq // tm

    def body(slot_ref, dp_ref, *refs):
        w_hbm = refs[:nq]
        dz1_ref, x_ref, mod_ref, gx_ref, db_ref, pb_ref, w_s, acc, sems = refs[nq:]
        i = pl.program_id(0)

        @pl.when(i == 0)
        def _():
            _load_weights(w_hbm, [w_s.at[slot_ref[k]] for k in range(nq)], sems)
            db_ref[...] = jnp.zeros_like(db_ref)

        @pl.when(i % tpb == 0)
        def _():
            pb_ref[...] = jnp.zeros_like(pb_ref)

        for sub in range(tm // ts):
            rows = slice(sub * ts, (sub + 1) * ts)
            for q in range(nq):
                dp = dp_ref[rows, q * ns:(q + 1) * ns]
                part = _mm_nt(dp, w_s[q])
                if q == 0:
                    acc[sub] = part
                else:
                    acc[sub] += part
                db_ref[q, 0:1, :] += _colsum(dp.astype(F32))
            dh = acc[sub]
            gx_ref[rows, :] = ALPHA * dz1_ref[rows, :] + dh * (1.0 + mod_ref[1:2, :])
            pb_ref[0:1, :] += _colsum(dh * x_ref[rows, :])
            pb_ref[1:2, :] += _colsum(dh)

    tok = lambda i, s: (i, 0)
    in_specs = [pl.BlockSpec((tm, nq * ns), tok)] + [_ANY] * nq
    in_specs += [pl.BlockSpec((tm, d), tok), pl.BlockSpec((tm, d), tok),
                 pl.BlockSpec((None, 8, d), lambda i, s: (i // tpb, 0, 0))] + [_ANY] * len(after)
    return pl.pallas_call(
        _ordered(body, 5 + nq, after), name="input_grad",
        grid_spec=pltpu.PrefetchScalarGridSpec(
            num_scalar_prefetch=1, grid=(t // tm,), in_specs=in_specs,
            out_specs=[pl.BlockSpec((tm, d), tok), pl.BlockSpec((nq, 8, ns), lambda i, s: (0, 0, 0)),
                       pl.BlockSpec((None, 8, d), lambda i, s: (i // tpb, 0, 0))],
            scratch_shapes=[pltpu.VMEM((nq, d, ns), BF16), pltpu.VMEM((tm // ts, ts, d), F32),
                            pltpu.SemaphoreType.DMA((nq,))]),
        out_shape=[jax.ShapeDtypeStruct((t, d), F32), jax.ShapeDtypeStruct((nq, 8, ns), F32),
                   jax.ShapeDtypeStruct((nb, 8, d), F32)],
        compiler_params=_cp(("arbitrary",)),
    )(slots, dproj, *ws, dz1, x2, modv, *after)


def _rows128(v):
    flat = v.reshape(-1, HEAD)
    pad = (-flat.shape[0]) % 8
    return jnp.pad(flat, ((0, pad), (0, 0))) if pad else flat


def kernel(x, c, w_ada, b_ada, w_in, b_in, w_conv, b_conv, w_rg_a, b_rg_a, w_rg_x, b_rg_x, lru_lambda, w_sp, b_sp, ln_v_g, ln_v_b, w_o_lru, w_o_sgu, w_out, ln1_g, ln1_b, w_up, w_down, ln2_g, ln2_b, loss_target, m_w_ada, m_b_ada, m_w_in, m_b_in, m_w_conv, m_b_conv, m_w_rg_a, m_b_rg_a, m_w_rg_x, m_b_rg_x, m_lru_lambda, m_w_sp, m_b_sp, m_ln_v_g, m_ln_v_b, m_w_o_lru, m_w_o_sgu, m_w_out, m_ln1_g, m_ln1_b, m_w_up, m_w_down, m_ln2_g, m_ln2_b, v_w_ada, v_b_ada, v_w_in, v_b_in, v_w_conv, v_b_conv, v_w_rg_a, v_b_rg_a, v_w_rg_x, v_b_rg_x, v_lru_lambda, v_w_sp, v_b_sp, v_ln_v_g, v_ln_v_b, v_w_o_lru, v_w_o_sgu, v_w_out, v_ln1_g, v_ln1_b, v_w_up, v_w_down, v_ln2_g, v_ln2_b):
    given = dict(locals())
    nb, seq, d = x.shape
    t = nb * seq
    w_lru = LRU_HEADS * HEAD
    d_sgu = SGU_GROUPS * HEAD
    xi, yi, ci = lax.axis_index("x"), lax.axis_index("y"), lax.axis_index("c")
    chip = 2 * xi + yi
    dev = 2 * chip + ci
    cidx = jnp.reshape(ci, (1,)).astype(jnp.int32)

    x2 = x.reshape(t, d)
    target = loss_target.reshape(t, d)

    big = ["w_in", "w_o_lru", "w_o_sgu", "w_out", "w_up", "w_down"]
    shards_a = [w_in[0].astype(BF16)]
    shards_b = [given[n][0].astype(BF16) for n in big[1:]]
    pidx = jnp.reshape(chip, (1,)).astype(jnp.int32)

    c_rows = _rows128(c)
    wconv_rows = _rows128(w_conv[0])
    slab0 = _all_gather_small(jnp.concatenate([c_rows, wconv_rows], axis=0), "gather_c_wconv")
    slab0 = slab0.reshape(N_DEV, -1, HEAD)
    c_all = slab0[:, :c_rows.shape[0]].reshape(N_DEV * nb, d)
    n_wc = CONV_WIDTH * (w_lru // N_CHIPS) // HEAD
    wc = slab0[0::2, c_rows.shape[0]:c_rows.shape[0] + n_wc].reshape(N_CHIPS, CONV_WIDTH, w_lru // N_CHIPS)
    w_conv_full = jnp.transpose(wc, (1, 0, 2)).reshape(CONV_WIDTH, w_lru)

    n_ada = w_ada.shape[2]
    b_ada_cols = lax.dynamic_slice(b_ada, (0, chip * n_ada), (1, n_ada))
    mod_cols = _ada_fwd(c_all, w_ada[0], b_ada_cols)
    half = (N_DEV * nb) // 2
    mod_half = lax.dynamic_slice(mod_cols, (ci * half, 0), (half, n_ada))
    mod_g = _all_gather_small(mod_half, "gather_mod").reshape(N_CHIPS, 2, half, n_ada)
    mod_all = jnp.transpose(mod_g, (1, 2, 0, 3)).reshape(N_DEV * nb, N_CHIPS * n_ada)
    mod_loc = lax.dynamic_slice(mod_all, (dev * nb, 0), (nb, N_CHIPS * n_ada)).reshape(nb, 6, d)
    modv = jnp.pad(mod_loc, ((0, 0), (0, 2), (0, 0)))

    lru_w = (w_conv_full, b_conv, w_rg_a[0], b_rg_a, w_rg_x[0], b_rg_x, lru_lambda)
    b_sp_t = jnp.transpose(b_sp[0])

    land = lambda s: jax.ShapeDtypeStruct((N_CHIPS,) + s.shape, s.dtype)
    sds = lambda s: jax.ShapeDtypeStruct(s.shape, s.dtype)
    started_a = _split_start(shards_a, [sds(shards_a[0])] * 3, _peer_gather_copies((0, 1, 2)), 3, "gather_w_in_start",
                             after=(modv,))
    shards_b, shards_c = shards_b[:3], shards_b[3:]
    started_b = _split_start(shards_b, [land(s) for s in shards_b], _gather_copies, 3 * len(shards_b),
                             "gather_w_mix_start", after=(started_a[-1],))
    started_c = _split_start(shards_c, [land(s) for s in shards_c], _gather_copies, 3 * len(shards_c),
                             "gather_w_mlp_start", after=(started_b[-1],))

    ids = lambda *v: jnp.stack(v).astype(jnp.int32)
    modv_t = modv + started_c[-1][0:1, 0:1]
    proj, h = _proj_fwd(x2, modv_t, [started_a[2]], ids(chip), b_in, seq, "proj_fwd_own")
    own_a, lands_a = _split_wait(started_a, 1, _peer_gather_copies((0, 1)), "gather_w_in_wait_near", after=(proj,))
    (proj,) = _proj_fwd(x2, modv, lands_a[:2], ids(chip ^ 1, chip ^ 2), b_in, seq, "proj_fwd_near", proj_in=proj)
    own_a, lands_a = _split_wait((started_a[0], started_a[1], *own_a, *lands_a, started_a[-1]), 1,
                                 _peer_gather_copies((2,)), "gather_w_in_wait_far", after=(proj,))
    (proj,) = _proj_fwd(x2, modv, lands_a[2:], ids(chip ^ 3), b_in, seq, "proj_fwd_far", proj_in=proj)
    w_in_shards, w_in_chips = own_a + lands_a, ids(chip, chip ^ 1, chip ^ 2, chip ^ 3)
    a, inp = _lru_prep(proj, lru_w, nb, seq)
    a3 = a.reshape(nb, seq, w_lru)
    hs = _scan(a3, inp.reshape(nb, seq, w_lru), False, "lru_scan").reshape(t, w_lru)
    y_sgu = _sgu_fwd(proj, w_sp[0], b_sp_t, ln_v_g, ln_v_b)
    shards_b, lands_b = _split_wait(started_b, len(shards_b), _gather_copies, "gather_w_mix_wait", after=(hs, y_sgu))
    w_o_lru_g, w_o_sgu_g, w_out_g = _fill_own_slot(lands_b, shards_b, pidx, ["own_" + n for n in big[1:4]])
    w_o_lru_g = w_o_lru_g.reshape(w_lru, d)
    w_out_g = w_out_g.reshape(d, d)
    yap, y_a, y_b, merged, mix, x1 = _mix_fwd(hs, proj, y_sgu, x2, modv, w_o_lru_g, w_o_sgu_g, w_out_g, ln1_g, ln1_b, seq)
    shards_c, lands_c = _split_wait(started_c, len(shards_c), _gather_copies, "gather_w_mlp_wait", after=(x1,))
    w_up_g, w_down_g = _fill_own_slot(lands_c, shards_c, pidx, ["own_" + n for n in big[4:]])
    w_down_g = w_down_g.reshape(-1, d)
    up, act, h2, dz2, df, st2, pb2 = _mlp_fwd(x1, modv, w_up_g, w_down_g, ln2_g, ln2_b, target, nb, seq)
    loss = lax.psum(st2[2, 0], ("x", "y", "c"))

    part = {}

    def to_sibling_start(group, tag, after=()):
        g4 = []
        for n in group:
            shard = given[n].shape[1:]
            g4.append(part[n].reshape(N_CHIPS, 2, shard[0] // 2, shard[1]))
        shapes = [jax.ShapeDtypeStruct((N_CHIPS,) + g.shape[2:], F32) for g in g4]
        return _split_start(g4, shapes, _to_sibling_copies, len(g4), "grads_to_sibling_start_" + tag, after)

    def to_chips_start(group, started, tag, after=()):
        g4, recv = _split_wait(started, len(group), _to_sibling_copies, "grads_to_sibling_wait_" + tag, after)
        own4 = [_add_own_half(g4[k], recv[k], cidx, "grad_pair_sum_" + n) for k, n in enumerate(group)]
        shapes = [jax.ShapeDtypeStruct((3,) + o.shape[1:], BF16) for o in own4]
        return _split_start(own4, shapes, _chip_exchange_copies, 3 * len(own4), "grads_chip_exchange_start_" + tag)

    def chips_finish(group, started, tag, after=()):
        own4, slots = _split_wait(started, len(group), _chip_exchange_copies, "grads_chip_exchange_wait_" + tag, after)
        return [_sum_own_and_peers(own4[k], slots[k], pidx, "grad_chip_sum_" + n) for k, n in enumerate(group)]

    dup, dz1, dmix, st1, pb1 = _mlp_bwd(df, up, w_down_g, w_up_g, dz2, x2, mix, modv, ln1_g, ln1_b, nb, seq)
    group1 = ["w_up", "w_down"]
    part["w_up"] = _weight_grad(h2, dup, True, "grad_w_up")
    part["w_down"] = _weight_grad(act, df, False, "grad_w_down")
    sib1 = to_sibling_start(group1, "mlp")
    dy_a, dy_b, dga, dgb, dgl, dyl, dys = _mix_bwd(dmix, proj, y_a, y_b, hs, w_out_g, w_o_lru_g, w_o_sgu_g, seq,
                                                   after=(sib1[-1],))
    group2 = ["w_o_lru", "w_o_sgu", "w_out"]
    part["w_o_lru"] = _weight_grad(yap, dy_a, False, "grad_w_o_lru")
    part["w_o_sgu"] = _weight_grad(y_sgu, dy_b, True, "grad_w_o_sgu")
    part["w_out"] = _weight_grad(merged, dmix, False, "grad_w_out")
    chips1 = to_chips_start(group1, sib1, "mlp", after=(dys, part["w_o_lru"], part["w_o_sgu"], part["w_out"]))
    sib2 = to_sibling_start(group2, "mix", after=(chips1[-1],))
    du, dv, g_w_sp, st_sgu, g_b_sp_t = _sgu_bwd(proj, dys, w_sp[0], b_sp_t, ln_v_g, ln_v_b, after=(sib2[-1],))
    dyl3 = dyl.reshape(nb, seq, w_lru)
    e = _scan(a3, dyl3, True, "lru_scan_bwd").reshape(t, w_lru)
    chips2 = to_chips_start(group2, sib2, "mix", after=(e, du))
    dxl, g_w_rg_a, g_w_rg_x, st_lru = _lru_bwd(proj, hs, e, dyl, lru_w, nb, seq, after=(chips2[-1],))
    dproj = jnp.concatenate([dxl, dgl, du, dv, dga, dgb], axis=1)

    didx = jnp.reshape(dev, (1,)).astype(jnp.int32)
    early = [
        ("w_conv", st_lru[4:8]), ("b_conv", st_lru[3]), ("w_rg_a", g_w_rg_a), ("b_rg_a", st_lru[0]),
        ("w_rg_x", g_w_rg_x), ("b_rg_x", st_lru[1]), ("lru_lambda", st_lru[2]), ("w_sp", g_w_sp),
        ("b_sp", jnp.transpose(g_b_sp_t[:, :SGU_GROUPS])), ("ln_v_g", st_sgu[0]), ("ln_v_b", st_sgu[1]),
        ("ln1_g", st1[0]), ("ln1_b", st1[1]), ("ln2_g", st2[0]), ("ln2_b", st2[1]),
    ]
    pieces_e = [_rows128(v) for _, v in early]
    slab_e = jnp.concatenate(pieces_e, axis=0)
    slab_e = jnp.pad(slab_e, ((0, (-slab_e.shape[0]) % TR_EW), (0, 0)))
    small_st = _split_start([slab_e], [jax.ShapeDtypeStruct((N_DEV,) + slab_e.shape, F32)], _all_devices_copies, N_DEV - 1,
                            "small_grads_start")

    group3 = ["w_in"]
    part["w_in"] = _weight_grad(h, dproj, True, "grad_w_in", after=(small_st[-1],))
    sib3 = to_sibling_start(group3, "in")
    chips3 = to_chips_start(group3, sib3, "in")
    grad_x2, g_b_in4, pb0 = _input_grad(dproj, w_in_shards, w_in_chips, dz1, x2, modv, nb, seq, after=(chips3[-1],))
    halves12 = chips_finish(group1, chips1, "mlp", after=(grad_x2,)) + chips_finish(group2, chips2, "mix", after=(grad_x2,))
    swap12 = _split_start(halves12, [jax.ShapeDtypeStruct(hv.shape, F32) for hv in halves12], _swap_copies, len(halves12),
                          "grads_swap_start")
    grads = {}

    dmod_loc = jnp.stack([pb0[:, 1], pb0[:, 0], pb1[:, 2], pb1[:, 1], pb1[:, 0], pb2[:, 0]], axis=1)
    late = [("dmod", dmod_loc), ("b_in", g_b_in4[:, 0])]
    pieces_l = [_rows128(v) for _, v in late]
    slab_l = jnp.concatenate(pieces_l, axis=0)
    gathered = _all_gather_small(slab_l, "gather_small_grads", after=(swap12[-1],)).reshape(N_DEV, slab_l.shape[0], HEAD)
    rows_dmod = dmod_loc.size // HEAD
    dmod_all = gathered[:, :rows_dmod].reshape(N_DEV * nb, 6 * d)
    grads["b_in"] = _sum_slots(gathered[:, rows_dmod:], "grad_b_in_sum").reshape(1, -1)

    (slab_e,), (lands_e,) = _split_wait(small_st, 1, _all_devices_copies, "small_grads_wait", after=(gathered,))
    summed = _sum_devices(lands_e, slab_e, didx, "small_grad_sum")
    off = 0
    for (n, v), piece in zip(early, pieces_e):
        grads[n] = summed[off:off + v.size // HEAD].reshape(v.shape)
        off += piece.shape[0]

    mine12, theirs12 = _split_wait(swap12, len(halves12), _swap_copies, "grads_swap_wait", after=(summed,))
    (mine3,) = chips_finish(group3, chips3, "in", after=(summed,))
    (theirs3,) = _exchange([mine3], [jax.ShapeDtypeStruct(mine3.shape, F32)], _swap_copies, 1, "grads_swap_w_in")
    mine = dict(zip(group1 + group2 + group3, mine12 + [mine3]))
    theirs = dict(zip(group1 + group2 + group3, theirs12 + [theirs3]))

    dmod_cols = lax.dynamic_slice(dmod_all, (0, chip * n_ada), (N_DEV * nb, n_ada))
    grads["w_ada"], grads["b_ada"] = _ada_bwd(c_all, dmod_all, dmod_cols)
    n_wcs = w_lru // N_CHIPS
    grads["w_conv"] = lax.dynamic_slice(grads["w_conv"], (0, chip * n_wcs), (CONV_WIDTH, n_wcs))

    names = ['w_ada', 'b_ada', 'w_in', 'b_in', 'w_conv', 'b_conv', 'w_rg_a', 'b_rg_a', 'w_rg_x', 'b_rg_x', 'lru_lambda',
             'w_sp', 'b_sp', 'ln_v_g', 'ln_v_b', 'w_o_lru', 'w_o_sgu', 'w_out', 'ln1_g', 'ln1_b', 'w_up', 'w_down',
             'ln2_g', 'ln2_b']
    out_g, out_d, out_m, out_v = [], [], [], []
    for n in names:
        wv = given[n]
        shape2 = (-1, wv.shape[-1])
        w2, m2, v2 = wv.reshape(shape2), given["m_" + n].reshape(shape2), given["v_" + n].reshape(shape2)
        if n in big:
            g2, dlt, nm, nv = _adamw_halves(w2, mine[n], theirs[n], m2, v2, cidx, "adamw_" + n)
        else:
            g2 = grads[n].reshape(wv.shape).reshape(shape2)
            dlt, nm, nv = _adamw(w2, g2, m2, v2, "adamw_" + n)
        out_g.append(g2.reshape(wv.shape))
        out_d.append(dlt.reshape(wv.shape))
        out_m.append(nm.reshape(wv.shape))
        out_v.append(nv.reshape(wv.shape))

    return (loss, grad_x2.reshape(nb, seq, d), *out_g, *out_d, *out_m, *out_v)
```

```python
import functools
import math

import jax
import jax.numpy as jnp
from jax import lax
from jax.experimental import pallas as pl
from jax.experimental.pallas import tpu as pltpu

F32 = jnp.float32
BF16 = jnp.bfloat16
MESH = pl.DeviceIdType.MESH

N_CHIPS = 4
N_DEV = 8
LRU_HEADS = 10
HEAD = 128
SGU_GROUPS = 6
SGU_CHUNK = 64
CONV_WIDTH = 4
LRU_C = 8.0
ALPHA = 2.0 ** 0.25
LN_EPS = 1e-5
ADAM_LR, ADAM_B1, ADAM_B2, ADAM_EPS, ADAM_WD, ADAM_STEP = 0.001, 0.9, 0.999, 1e-08, 0.01, 10

VMEM_LIMIT = 56 * 1024 * 1024
VMEM_LIMIT_MAX = 62 * 1024 * 1024
TM_PROJ = 1024
TM_MIX = 256
TM_MLP = 512
TS_MLP = 256
TM_SGU = 512
TM_DH = 512
TT_DW = 1024
TC_SCAN = 256
TR_EW = 256


def _cp(sem=None, limit=None):
    return pltpu.CompilerParams(dimension_semantics=sem, vmem_limit_bytes=limit or VMEM_LIMIT)


def _mm(a, b):
    return jnp.dot(a.astype(BF16), b.astype(BF16), preferred_element_type=F32)


def _mm_nt(a, b):
    return lax.dot_general(a.astype(BF16), b.astype(BF16), (((1,), (1,)), ((), ())), preferred_element_type=F32)


def _mm_tn(a, b):
    return lax.dot_general(a.astype(BF16), b.astype(BF16), (((0,), (0,)), ((), ())), preferred_element_type=F32)


def _sigmoid(x):
    return 1.0 / (1.0 + jnp.exp(-x))


def _sigmoid_t(x):
    return 0.5 * jnp.tanh(0.5 * x) + 0.5


_GELU_K = math.sqrt(2.0 / math.pi)


def _gelu(x):
    t = jnp.tanh(_GELU_K * (x + 0.044715 * (x * x * x)))
    return 0.5 * x * (1.0 + t)


def _gelu_and_grad(x):
    x2 = x * x
    t = jnp.tanh(_GELU_K * (x + 0.044715 * (x2 * x)))
    g = 0.5 * x * (1.0 + t)
    dg = 0.5 * (1.0 + t) + 0.5 * x * (1.0 - t * t) * (_GELU_K * (1.0 + 3.0 * 0.044715 * x2))
    return g, dg


def _ln_stats(z):
    mu = jnp.mean(z, axis=-1, keepdims=True)
    zc = z - mu
    var = jnp.mean(zc * zc, axis=-1, keepdims=True)
    rstd = lax.rsqrt(var + LN_EPS)
    return zc * rstd, rstd


def _ln_bwd(dxh, xhat, rstd):
    m1 = jnp.mean(dxh, axis=-1, keepdims=True)
    m2 = jnp.mean(dxh * xhat, axis=-1, keepdims=True)
    return rstd * (dxh - m1 - xhat * m2)


def _colsum(v):
    return jnp.sum(v, axis=0, keepdims=True)


def _shift_down(v, j):
    if j == 0:
        return v
    rows = lax.broadcasted_iota(jnp.int32, v.shape, 0)
    return jnp.where(rows >= j, pltpu.roll(v, j, 0), 0.0)


def _shift_up(v, j):
    if j == 0:
        return v
    n = v.shape[0]
    rows = lax.broadcasted_iota(jnp.int32, v.shape, 0)
    return jnp.where(rows < n - j, pltpu.roll(v, n - j, 0), 0.0)


def _load_weights(srcs, dsts, sems):
    cps = [pltpu.make_async_copy(s, dd, sems.at[k]) for k, (s, dd) in enumerate(zip(srcs, dsts))]
    for cp in cps:
        cp.start()
    for cp in cps:
        cp.wait()


def _my_pos():
    return lax.axis_index("x"), lax.axis_index("y"), lax.axis_index("c")


def _all_gather_small(v, name, after=()):
    m_per, n = v.shape

    def body(x_ref, out_ref, send_sems, recv_sems, local_sem):
        x, y, c = _my_pos()
        me, sibling = (x, y, c), (x, y, 1 - c)
        chips = [(1 - x, y), (x, 1 - y), (1 - x, 1 - y)]

        def rows(px, py, pc):
            return out_ref.at[pl.ds((4 * px + 2 * py + pc) * m_per, m_per), :]

        def copy(k, block, to, src=None):
            return pltpu.make_async_remote_copy(
                src_ref=rows(*block) if src is None else src, dst_ref=rows(*block),
                send_sem=send_sems.at[k], recv_sem=recv_sems.at[k], device_id=to, device_id_type=MESH)

        mine = pltpu.make_async_copy(x_ref, rows(*me), local_sem)
        mine.start()
        first = [copy(0, me, sibling, src=x_ref)]
        first += [copy(1 + j, me, (*chip, c), src=x_ref) for j, chip in enumerate(chips)]
        for cp in first:
            cp.start()
        passed = [copy(4 + j, (*chip, c), sibling) for j, chip in enumerate(chips)]
        for j, chip in enumerate(chips):
            copy(1 + j, (*chip, c), me).wait_recv()
            passed[j].start()
        copy(0, sibling, me).wait_recv()
        for j, chip in enumerate(chips):
            copy(4 + j, (*chip, 1 - c), me).wait_recv()
        for cp in first + passed:
            cp.wait_send()
        mine.wait()

    return pl.pallas_call(
        _ordered(body, 1, after), name=name,
        out_shape=jax.ShapeDtypeStruct((N_DEV * m_per, n), v.dtype),
        in_specs=[pl.BlockSpec(memory_space=pltpu.VMEM)] + [pl.BlockSpec(memory_space=pl.ANY)] * len(after),
        out_specs=pl.BlockSpec(memory_space=pltpu.VMEM),
        scratch_shapes=[pltpu.SemaphoreType.DMA((7,)), pltpu.SemaphoreType.DMA((7,)), pltpu.SemaphoreType.DMA],
        compiler_params=pltpu.CompilerParams(vmem_limit_bytes=VMEM_LIMIT),
    )(v, *after)


_HBM = pl.BlockSpec(memory_space=pltpu.HBM)
_ANY = pl.BlockSpec(memory_space=pl.ANY)
_SEM = pl.BlockSpec(memory_space=pltpu.SEMAPHORE)
_EFFECT = pltpu.SideEffectType.DATAFLOW_SIDE_EFFECTING


def _ordered(body, n_in, after):
    k = len(after)
    if not k:
        return body
    return lambda *refs: body(*refs[:n_in], *refs[n_in + k:])


def _gather_copies(ins, lands, send_sems, recv_sems):
    x, y, c = _my_pos()
    p = 2 * x + y
    peers = [(x, 1 - y), (1 - x, y), (1 - x, 1 - y)]
    sends, recvs = [], []
    for k in range(len(ins)):
        for j, (qx, qy) in enumerate(peers):
            sems = dict(send_sem=send_sems.at[3 * k + j], recv_sem=recv_sems.at[3 * k + j],
                        device_id=(qx, qy, c), device_id_type=MESH)
            sends.append(pltpu.make_async_remote_copy(src_ref=ins[k], dst_ref=lands[k].at[p], **sems))
            recvs.append(pltpu.make_async_remote_copy(src_ref=ins[k], dst_ref=lands[k].at[2 * qx + qy], **sems))
    return sends, recvs


def _peer_gather_copies(peers):
    def copies(ins, lands, send_sems, recv_sems):
        x, y, c = _my_pos()
        where = [(x, 1 - y), (1 - x, y), (1 - x, 1 - y)]
        cps = [pltpu.make_async_remote_copy(
            src_ref=ins[0], dst_ref=lands[j], send_sem=send_sems.at[j], recv_sem=recv_sems.at[j],
            device_id=(*where[j], c), device_id_type=MESH) for j in peers]
        return cps, cps
    return copies


def _to_sibling_copies(ins, lands, send_sems, recv_sems):
    x, y, c = _my_pos()
    cps = [pltpu.make_async_remote_copy(
        src_ref=ins[k].at[:, 1 - c], dst_ref=lands[k], send_sem=send_sems.at[k], recv_sem=recv_sems.at[k],
        device_id=(x, y, 1 - c), device_id_type=MESH) for k in range(len(ins))]
    return cps, cps


def _chip_exchange_copies(ins, lands, send_sems, recv_sems):
    x, y, c = _my_pos()
    peers = [(x, 1 - y), (1 - x, y), (1 - x, 1 - y)]
    cps = []
    for k in range(len(ins)):
        for j, (qx, qy) in enumerate(peers):
            cps.append(pltpu.make_async_remote_copy(
                src_ref=ins[k].at[2 * qx + qy], dst_ref=lands[k].at[j], send_sem=send_sems.at[3 * k + j],
                recv_sem=recv_sems.at[3 * k + j], device_id=(qx, qy, c), device_id_type=MESH))
    return cps, cps


def _all_devices_copies(ins, lands, send_sems, recv_sems):
    x, y, c = _my_pos()
    me = 4 * x + 2 * y + c
    sends, recvs = [], []
    for r in range(1, N_DEV):
        px = 1 - x if r & 4 else x
        py = 1 - y if r & 2 else y
        pc = 1 - c if r & 1 else c
        sems = dict(send_sem=send_sems.at[r - 1], recv_sem=recv_sems.at[r - 1], device_id=(px, py, pc), device_id_type=MESH)
        sends.append(pltpu.make_async_remote_copy(src_ref=ins[0], dst_ref=lands[0].at[me], **sems))
        recvs.append(pltpu.make_async_remote_copy(src_ref=ins[0], dst_ref=lands[0].at[4 * px + 2 * py + pc], **sems))
    return sends, recvs


def _swap_copies(ins, lands, send_sems, recv_sems):
    x, y, c = _my_pos()
    cps = [pltpu.make_async_remote_copy(
        src_ref=ins[k], dst_ref=lands[k], send_sem=send_sems.at[k], recv_sem=recv_sems.at[k],
        device_id=(x, y, 1 - c), device_id_type=MESH) for k in range(len(ins))]
    return cps, cps


def _split_start(ins, land_shapes, copies, n_sems, name, after=()):
    n, nl = len(ins), len(land_shapes)
    first_out = n + nl + len(after)

    def body(*refs):
        in_refs, land_refs = refs[:n], refs[n:n + nl]
        send_sems, recv_sems = refs[first_out:first_out + 2]
        token = refs[-1]
        sends, _ = copies(in_refs, land_refs, send_sems, recv_sems)
        for cp in sends:
            cp.start()
        token[...] = jnp.zeros_like(token)

    lands = [pltpu.with_memory_space_constraint(lax.empty(s.shape, s.dtype), pltpu.HBM) for s in land_shapes]
    ins = [pltpu.with_memory_space_constraint(s, pltpu.HBM) for s in ins]
    return pl.pallas_call(
        body, name=name,
        out_shape=(pltpu.SemaphoreType.DMA((n_sems,)), pltpu.SemaphoreType.DMA((n_sems,)),
                   *[pltpu.HBM(s.shape, s.dtype) for s in ins], *[pltpu.HBM(s.shape, s.dtype) for s in lands],
                   jax.ShapeDtypeStruct((8, HEAD), F32)),
        in_specs=[_HBM] * (n + nl) + [pl.BlockSpec(memory_space=pl.ANY)] * len(after),
        out_specs=(_SEM, _SEM, *([_HBM] * (n + nl)), pl.BlockSpec(memory_space=pltpu.VMEM)),
        input_output_aliases={k: 2 + k for k in range(n + nl)},
        compiler_params=pltpu.CompilerParams(has_side_effects=_EFFECT),
    )(*ins, *lands, *after)


def _split_wait(started, n, copies, name, after=()):
    send_sems, recv_sems = started[0], started[1]
    bufs = started[2:-1]
    nb = len(bufs)

    def body(*refs):
        in_refs, land_refs = refs[:n], refs[n:nb]
        sends, recvs = copies(in_refs, land_refs, refs[nb], refs[nb + 1])
        for cp in sends:
            cp.wait_send()
        for cp in recvs:
            cp.wait_recv()

    outs = pl.pallas_call(
        body, name=name,
        out_shape=tuple(pltpu.HBM(s.shape, s.dtype) for s in bufs),
        in_specs=[_HBM] * nb + [_SEM, _SEM] + [pl.BlockSpec(memory_space=pl.ANY)] * len(after),
        out_specs=tuple([_HBM] * nb),
        input_output_aliases={k: k for k in range(nb)},
        compiler_params=pltpu.CompilerParams(has_side_effects=_EFFECT),
    )(*bufs, send_sems, recv_sems, *after)
    return list(outs[:n]), list(outs[n:])


def _fill_own_slot(gathered, shards, pidx, names):
    outs = []
    for g, s, name in zip(gathered, shards, names):
        r, cdim = s.shape
        tr = _row_tile(r)

        def body(p_ref, s_ref, g_ref, o_ref):
            o_ref[...] = s_ref[...]

        outs.append(pl.pallas_call(
            body, name=name,
            grid_spec=pltpu.PrefetchScalarGridSpec(
                num_scalar_prefetch=1, grid=(r // tr,),
                in_specs=[pl.BlockSpec((tr, cdim), lambda i, p: (i, 0)), pl.BlockSpec(memory_space=pl.ANY)],
                out_specs=pl.BlockSpec((None, tr, cdim), lambda i, p: (p[0], i, 0))),
            out_shape=jax.ShapeDtypeStruct(g.shape, g.dtype),
            input_output_aliases={2: 0},
            compiler_params=_cp(("arbitrary",)),
        )(pidx, s, g))
    return outs


def _sum_own_and_peers(own4, slots, pidx, name):
    _, rh, cdim = own4.shape
    tr = _row_tile(rh)

    def body(p_ref, own_ref, s_ref, o_ref):
        acc = own_ref[...].astype(F32)
        for j in range(3):
            acc = acc + s_ref[j].astype(F32)
        o_ref[...] = acc

    return pl.pallas_call(
        body, name=name,
        grid_spec=pltpu.PrefetchScalarGridSpec(
            num_scalar_prefetch=1, grid=(rh // tr,),
            in_specs=[pl.BlockSpec((None, tr, cdim), lambda i, p: (p[0], i, 0)),
                      pl.BlockSpec((3, tr, cdim), lambda i, p: (0, i, 0))],
            out_specs=pl.BlockSpec((tr, cdim), lambda i, p: (i, 0))),
        out_shape=jax.ShapeDtypeStruct((rh, cdim), F32),
        compiler_params=_cp(("arbitrary",)),
    )(pidx, own4, slots)


def _exchange(ins, land_shapes, copies, n_sems, name):
    n, nl = len(ins), len(land_shapes)

    def body(*refs):
        sends, recvs = copies(refs[:n], refs[n:n + nl], refs[n + nl], refs[n + nl + 1])
        for cp in sends:
            cp.start()
        for cp in sends:
            cp.wait_send()
        for cp in recvs:
            cp.wait_recv()

    any_spec = pl.BlockSpec(memory_space=pl.ANY)
    return pl.pallas_call(
        body, name=name,
        out_shape=[jax.ShapeDtypeStruct(s.shape, s.dtype) for s in land_shapes],
        in_specs=[any_spec] * n, out_specs=[any_spec] * nl,
        scratch_shapes=[pltpu.SemaphoreType.DMA((n_sems,)), pltpu.SemaphoreType.DMA((n_sems,))],
    )(*ins)


def _row_tile(r):
    t = min(TR_EW, r)
    while r % t:
        t //= 2
    return t


def _add_own_half(g4, recv, cidx, name):
    _, _, rh, cdim = g4.shape
    tr = _row_tile(rh)

    def body(c_ref, a_ref, b_ref, o_ref):
        o_ref[...] = (a_ref[...] + b_ref[...]).astype(BF16)

    return pl.pallas_call(
        body, name=name,
        grid_spec=pltpu.PrefetchScalarGridSpec(
            num_scalar_prefetch=1, grid=(N_CHIPS, rh // tr),
            in_specs=[pl.BlockSpec((None, None, tr, cdim), lambda q, i, c: (q, c[0], i, 0)),
                      pl.BlockSpec((None, tr, cdim), lambda q, i, c: (q, i, 0))],
            out_specs=pl.BlockSpec((None, tr, cdim), lambda q, i, c: (q, i, 0))),
        out_shape=jax.ShapeDtypeStruct(recv.shape, BF16),
        compiler_params=_cp(("arbitrary", "arbitrary")),
    )(cidx, g4, recv)


def _sum_slots(v, name):
    n, r, cdim = v.shape
    tr = _row_tile(r)

    def body(v_ref, o_ref):
        acc = v_ref[0].astype(F32)
        for k in range(1, n):
            acc = acc + v_ref[k].astype(F32)
        o_ref[...] = acc

    return pl.pallas_call(
        body, name=name, grid=(r // tr,),
        in_specs=[pl.BlockSpec((n, tr, cdim), lambda i: (0, i, 0))],
        out_specs=pl.BlockSpec((tr, cdim), lambda i: (i, 0)),
        out_shape=jax.ShapeDtypeStruct((r, cdim), F32),
        compiler_params=_cp(("arbitrary",)),
    )(v)


def _sum_devices(lands, own, didx, name):
    _, r, cdim = lands.shape
    tr = _row_tile(r)

    def body(d_ref, l_ref, own_ref, o_ref):
        acc = jnp.where(d_ref[0] == 0, own_ref[...], l_ref[0])
        for dv in range(1, N_DEV):
            acc = acc + jnp.where(d_ref[0] == dv, own_ref[...], l_ref[dv])
        o_ref[...] = acc

    return pl.pallas_call(
        body, name=name,
        grid_spec=pltpu.PrefetchScalarGridSpec(
            num_scalar_prefetch=1, grid=(r // tr,),
            in_specs=[pl.BlockSpec((N_DEV, tr, cdim), lambda i, dd: (0, i, 0)), pl.BlockSpec((tr, cdim), lambda i, dd: (i, 0))],
            out_specs=pl.BlockSpec((tr, cdim), lambda i, dd: (i, 0))),
        out_shape=jax.ShapeDtypeStruct((r, cdim), F32),
        compiler_params=_cp(("arbitrary",)),
    )(didx, lands, own)


def _adamw_math(wv, gg, mv, vv):
    nm = ADAM_B1 * mv + (1.0 - ADAM_B1) * gg
    nv = ADAM_B2 * vv + (1.0 - ADAM_B2) * (gg * gg)
    m_hat = nm / (1.0 - ADAM_B1 ** ADAM_STEP)
    v_hat = nv / (1.0 - ADAM_B2 ** ADAM_STEP)
    return -ADAM_LR * (m_hat / (jnp.sqrt(v_hat) + ADAM_EPS) + ADAM_WD * wv), nm, nv


def _adamw_halves(w, mine, theirs, m, v, cidx, name):
    r, cdim = w.shape
    rh = r // 2
    tr = _row_tile(rh)
    nblk = rh // tr

    def body(c_ref, w_ref, a_ref, b_ref, m_ref, v_ref, g_ref, d_ref, nm_ref, nv_ref):
        gg = jnp.where(pl.program_id(0) == c_ref[0], a_ref[...], b_ref[...])
        g_ref[...] = gg
        d_ref[...], nm_ref[...], nv_ref[...] = _adamw_math(w_ref[...], gg, m_ref[...], v_ref[...])

    full = pl.BlockSpec((tr, cdim), lambda hh, i, c: (hh * nblk + i, 0))
    half = pl.BlockSpec((tr, cdim), lambda hh, i, c: (i, 0))
    return pl.pallas_call(
        body, name=name,
        grid_spec=pltpu.PrefetchScalarGridSpec(
            num_scalar_prefetch=1, grid=(2, nblk),
            in_specs=[full, half, half, full, full], out_specs=[full] * 4),
        out_shape=[jax.ShapeDtypeStruct((r, cdim), F32)] * 4,
        compiler_params=_cp(("arbitrary", "arbitrary")),
    )(cidx, w, mine, theirs, m, v)


def _adamw_many(params, name):
    n = len(params)

    def body(*refs):
        ins, outs = refs[:4 * n], refs[4 * n:]
        for k in range(n):
            w_ref, g_ref, m_ref, v_ref = ins[4 * k:4 * k + 4]
            outs[3 * k][...], outs[3 * k + 1][...], outs[3 * k + 2][...] = _adamw_math(
                w_ref[...], g_ref[...], m_ref[...], v_ref[...])

    flat = [a for p in params for a in p]
    res = pl.pallas_call(
        body, name=name,
        out_shape=[jax.ShapeDtypeStruct(p[0].shape, F32) for p in params for _ in range(3)],
        compiler_params=pltpu.CompilerParams(vmem_limit_bytes=VMEM_LIMIT),
    )(*flat)
    return [res[3 * k:3 * k + 3] for k in range(n)]


def _adamw(w, g, m, v, name):
    r, cdim = w.shape
    tr = _row_tile(r) if r % 8 == 0 else r

    def body(w_ref, g_ref, m_ref, v_ref, d_ref, nm_ref, nv_ref):
        d_ref[...], nm_ref[...], nv_ref[...] = _adamw_math(w_ref[...], g_ref[...], m_ref[...], v_ref[...])

    spec = pl.BlockSpec((tr, cdim), lambda i: (i, 0))
    return pl.pallas_call(
        body, name=name, grid=(r // tr,), in_specs=[spec] * 4, out_specs=[spec] * 3,
        out_shape=[jax.ShapeDtypeStruct((r, cdim), F32)] * 3,
        compiler_params=_cp(("arbitrary",)),
    )(w, g, m, v)


def _ada_fwd(c_all, w_ada, b_cols):
    nb, _ = c_all.shape
    n = w_ada.shape[1]

    def body(c_ref, w_ref, b_ref, o_ref):
        cv = c_ref[...]
        o_ref[...] = _mm(cv * _sigmoid(cv), w_ref[...]) + b_ref[...]

    return pl.pallas_call(
        body, name="ada_fwd", out_shape=jax.ShapeDtypeStruct((nb, n), F32),
        compiler_params=pltpu.CompilerParams(vmem_limit_bytes=VMEM_LIMIT),
    )(c_all, w_ada, b_cols)


def _ada_bwd(c_all, dmod_all, dmod_cols):
    d = c_all.shape[1]
    n = dmod_cols.shape[1]

    def body(c_ref, da_ref, dc_ref, gw_ref, gb_ref):
        cv = c_ref[...]
        gw_ref[...] = _mm_tn(cv * _sigmoid(cv), dc_ref[...])
        gb_ref[...] = _colsum(da_ref[...])

    return pl.pallas_call(
        body, name="ada_bwd",
        out_shape=[jax.ShapeDtypeStruct((d, n), F32), jax.ShapeDtypeStruct((1, dmod_all.shape[1]), F32)],
        compiler_params=pltpu.CompilerParams(vmem_limit_bytes=VMEM_LIMIT),
    )(c_all, dmod_all, dmod_cols)


def _proj_fwd(x2, modv, ws, cols, b_in, seq, name, proj_in=None):
    t, d = x2.shape
    n = len(ws)
    ns = ws[0].shape[1]
    tm = min(TM_PROJ, seq)
    tpb = seq // tm
    first = proj_in is None

    def body(c_ref, x_ref, mod_ref, *refs):
        w_refs, b_ref = refs[:n], refs[n]
        outs = refs[n + 1 if first else n + 2:]
        proj_ref, h_s = outs[0], outs[-1]
        s = pl.program_id(1)

        @pl.when(s == 0)
        def _():
            h = (x_ref[...] * (1.0 + mod_ref[1:2, :]) + mod_ref[0:1, :]).astype(BF16)
            h_s[...] = h
            if first:
                outs[1][...] = h

        for k in range(n):
            @pl.when(s == k)
            def _():
                proj_ref[...] = (jnp.dot(h_s[...], w_refs[k][...], preferred_element_type=F32) + b_ref[...]).astype(BF16)

    in_specs = [pl.BlockSpec((tm, d), lambda i, s, c: (i, 0)),
                pl.BlockSpec((None, 8, d), lambda i, s, c: (i // tpb, 0, 0))]
    in_specs += [pl.BlockSpec((d, ns), lambda i, s, c: (0, 0))] * n
    in_specs += [pl.BlockSpec((1, ns), lambda i, s, c: (0, c[s]))]
    out_specs = [pl.BlockSpec((tm, ns), lambda i, s, c: (i, c[s]))]
    out_shape = [jax.ShapeDtypeStruct((t, N_CHIPS * ns), BF16)]
    args = [cols, x2, modv, *ws, b_in]
    aliases = {}
    if first:
        out_specs.append(pl.BlockSpec((tm, d), lambda i, s, c: (i, 0)))
        out_shape.append(jax.ShapeDtypeStruct((t, d), BF16))
    else:
        in_specs.append(_ANY)
        args.append(proj_in)
        aliases = {len(args) - 1: 0}
    return pl.pallas_call(
        body, name=name,
        grid_spec=pltpu.PrefetchScalarGridSpec(
            num_scalar_prefetch=1, grid=(t // tm, n), in_specs=in_specs, out_specs=out_specs,
            scratch_shapes=[pltpu.VMEM((tm, d), BF16)]),
        out_shape=out_shape, input_output_aliases=aliases,
        compiler_params=_cp(("arbitrary", "arbitrary")),
    )(*args)


def _lru_gates(xl, wc_ref, bc_ref, wa_ref, ba_ref, wx_ref, bx_ref, lam_ref):
    xc = bc_ref[...] + wc_ref[CONV_WIDTH - 1:CONV_WIDTH, :] * xl
    for k in range(CONV_WIDTH - 1):
        xc = xc + wc_ref[k:k + 1, :] * _shift_down(xl, CONV_WIDTH - 1 - k)
    r = _sigmoid(_mm(xc, wa_ref[...]) + ba_ref[...])
    gi = _sigmoid_t(_mm(xc, wx_ref[...]) + bx_ref[...])
    nl = -lam_ref[...]
    e = jnp.exp(-jnp.abs(nl))
    u = 1.0 + e
    dlt = u - 1.0
    log1p_e = jnp.where(dlt == 0.0, e, jnp.log(u) * (e / jnp.where(dlt == 0.0, 1.0, dlt)))
    big_l = -LRU_C * (jnp.maximum(nl, 0.0) + log1p_e)
    la = big_l * r
    a = jnp.exp(la)
    m2 = jnp.tanh(-la) * (a * a + 1.0)
    return xc, r, gi, big_l, a, m2


def _lru_prep(proj, lru_w, nb, seq):
    t = proj.shape[0]
    w = LRU_HEADS * HEAD
    w_conv, b_conv, w_a, b_a, w_x, b_x, lam = lru_w

    def body(x_ref, wc_ref, bc_ref, wa_ref, ba_ref, wx_ref, bx_ref, lam_ref, a_ref, inp_ref):
        xc, r, gi, big_l, a, m2 = _lru_gates(x_ref[...].astype(F32), wc_ref, bc_ref, wa_ref, ba_ref, wx_ref, bx_ref, lam_ref)
        a_ref[...] = a
        inp_ref[...] = jnp.sqrt(m2) * (gi * xc)

    col = lambda b, hd: (0, hd)
    head = lambda b, hd: (hd, 0, 0)
    tok = lambda b, hd: (b, hd)
    return pl.pallas_call(
        body, name="lru_prep", grid=(nb, LRU_HEADS),
        in_specs=[pl.BlockSpec((seq, HEAD), tok),
                  pl.BlockSpec((CONV_WIDTH, HEAD), col), pl.BlockSpec((1, HEAD), col),
                  pl.BlockSpec((None, HEAD, HEAD), head), pl.BlockSpec((1, HEAD), col),
                  pl.BlockSpec((None, HEAD, HEAD), head), pl.BlockSpec((1, HEAD), col),
                  pl.BlockSpec((1, HEAD), col)],
        out_specs=[pl.BlockSpec((seq, HEAD), tok)] * 2,
        out_shape=[jax.ShapeDtypeStruct((t, w), F32)] * 2,
        compiler_params=_cp(("arbitrary", "arbitrary")),
    )(proj, w_conv, b_conv, w_a, b_a, w_x, b_x, lam)


def _scan(a3, b3, reverse, name):
    nb, seq, w = a3.shape
    tc = min(TC_SCAN, seq)
    nchunk = seq // tc
    ntile = tc // 8

    def combine(av, bv):
        rows = lax.broadcasted_iota(jnp.int32, av.shape, 0)
        for s in (1, 2, 4):
            if reverse:
                keep = rows < 8 - s
                a_sh, b_sh = pltpu.roll(av, 8 - s, 0), pltpu.roll(bv, 8 - s, 0)
            else:
                keep = rows >= s
                a_sh, b_sh = pltpu.roll(av, s, 0), pltpu.roll(bv, s, 0)
            bv = jnp.where(keep, bv + av * b_sh, bv)
            av = jnp.where(keep, av * a_sh, av)
        return av, bv

    def body(a_ref, b_ref, h_ref, carry):
        @pl.when(pl.program_id(0) == 0)
        def _():
            carry[...] = jnp.zeros_like(carry)

        for b in range(nb):
            def tile(j, hprev):
                jj = ntile - 1 - j if reverse else j
                base = pl.multiple_of(jj * 8, 8)
                av, bv = a_ref[b, pl.ds(base, 8), :], b_ref[b, pl.ds(base, 8), :]
                av, bv = combine(av, av * bv if reverse else bv)
                h = bv + av * hprev
                h_ref[b, pl.ds(base, 8), :] = h
                edge = h[0:1, :] if reverse else h[7:8, :]
                return jnp.broadcast_to(edge, (8, w))

            carry[b] = lax.fori_loop(0, ntile, tile, carry[b])

    imap = (lambda i: (0, nchunk - 1 - i, 0)) if reverse else (lambda i: (0, i, 0))
    spec = pl.BlockSpec((nb, tc, w), imap)
    return pl.pallas_call(
        body, name=name, grid=(nchunk,), in_specs=[spec, spec], out_specs=spec,
        out_shape=jax.ShapeDtypeStruct((nb, seq, w), F32),
        scratch_shapes=[pltpu.VMEM((nb, 8, w), F32)],
        compiler_params=_cp(("arbitrary",)),
    )(a3, b3)


def _sgu_mask():
    ti = lax.broadcasted_iota(jnp.int32, (HEAD, HEAD), 0) // SGU_CHUNK
    si = lax.broadcasted_iota(jnp.int32, (HEAD, HEAD), 1) // SGU_CHUNK
    return si <= ti


def _sgu_specs(tm, d_sgu):
    pw = 256
    first_u = (2 * LRU_HEADS * HEAD) // pw
    n_piece = d_sgu // pw
    specs = [pl.BlockSpec((tm, pw), functools.partial(lambda i, k: (i, k), k=first_u + j)) for j in range(2 * n_piece)]
    return specs, n_piece


def _sgu_fwd(proj, w_sp, b_sp_t, ln_g, ln_b):
    t = proj.shape[0]
    d_sgu = SGU_GROUPS * HEAD
    tm = min(TM_SGU, t)
    nblk = tm // HEAD
    specs, n_piece = _sgu_specs(tm, d_sgu)

    def body(*refs):
        u = jnp.concatenate([r[...] for r in refs[:n_piece]], axis=1).astype(F32)
        v = jnp.concatenate([r[...] for r in refs[n_piece:2 * n_piece]], axis=1).astype(F32)
        w_ref, bt_ref, g_ref, b_ref, y_ref = refs[2 * n_piece:]
        ug = _gelu(u)
        xhat, _ = _ln_stats(_gelu(v))
        vn = (xhat * g_ref[...] + b_ref[...]).astype(BF16)
        mask = _sgu_mask()
        for g in range(SGU_GROUPS):
            wm = jnp.where(mask, w_ref[g], 0.0).astype(BF16)
            cols = slice(g * HEAD, (g + 1) * HEAD)
            for n in range(nblk):
                rows = slice(n * HEAD, (n + 1) * HEAD)
                mixed = jnp.dot(wm, vn[rows, cols], preferred_element_type=F32) + bt_ref[:, g:g + 1]
                y_ref[rows, cols] = (ug[rows, cols] * mixed).astype(BF16)

    full = lambda shape: pl.BlockSpec(shape, lambda i: (0,) * len(shape))
    return pl.pallas_call(
        body, name="sgu_fwd", grid=(t // tm,),
        in_specs=specs + [full(w_sp.shape), full(b_sp_t.shape), full(ln_g.shape), full(ln_b.shape)],
        out_specs=pl.BlockSpec((tm, d_sgu), lambda i: (i, 0)),
        out_shape=jax.ShapeDtypeStruct((t, d_sgu), BF16),
        compiler_params=_cp(("arbitrary",)),
    )(*([proj] * (2 * n_piece)), w_sp, b_sp_t, ln_g, ln_b)


def _mix_fwd(hs, proj, y_sgu, x2, modv, w_o_lru_g, w_o_sgu_g, w_out_g, ln1_g, ln1_b, seq):
    t, d = x2.shape
    w = hs.shape[1]
    d_sgu = y_sgu.shape[1]
    nq, _, ns = w_o_sgu_g.shape
    tm = min(TM_MIX, seq)
    tpb = seq // tm

    def body(hs_ref, gl_ref, ys_ref, ga_ref, gb_ref, x_ref, mod_ref, wl_ref, ws_ref, wo_ref, g1_ref, b1_ref,
             yap_ref, ya_ref, yb_ref, mg_ref, mix_ref, x1_ref):
        yap = (hs_ref[...] * _gelu(gl_ref[...].astype(F32))).astype(BF16)
        yap_ref[...] = yap
        y_a = jnp.dot(yap, wl_ref[...], preferred_element_type=F32)
        ys = ys_ref[...]
        y_b = jnp.concatenate([jnp.dot(ys, ws_ref[q], preferred_element_type=F32) for q in range(nq)], axis=1)
        ya_ref[...] = y_a.astype(BF16)
        yb_ref[...] = y_b.astype(BF16)
        merged = (_sigmoid_t(ga_ref[...].astype(F32)) * y_a + _sigmoid_t(gb_ref[...].astype(F32)) * y_b).astype(BF16)
        mg_ref[...] = merged
        mix = jnp.dot(merged, wo_ref[...], preferred_element_type=F32)
        mix_ref[...] = mix
        xhat, _ = _ln_stats(ALPHA * x_ref[...] + (1.0 + mod_ref[2:3, :]) * mix)
        x1_ref[...] = xhat * g1_ref[...] + b1_ref[...]

    row = lambda width, col: pl.BlockSpec((tm, width), functools.partial(lambda i, k: (i, k), k=col))
    full = lambda shape: pl.BlockSpec(shape, lambda i: (0,) * len(shape))
    return pl.pallas_call(
        body, name="mix_fwd", grid=(t // tm,),
        in_specs=[row(w, 0), row(w, 1), row(d_sgu, 0), row(d, 4), row(d, 5), row(d, 0),
                  pl.BlockSpec((None, 8, d), lambda i: (i // tpb, 0, 0)),
                  full(w_o_lru_g.shape), full(w_o_sgu_g.shape), full(w_out_g.shape), full(ln1_g.shape), full(ln1_b.shape)],
        out_specs=[row(w, 0), row(d, 0), row(d, 0), row(d, 0), row(d, 0), row(d, 0)],
        out_shape=[jax.ShapeDtypeStruct((t, w), BF16), jax.ShapeDtypeStruct((t, d), BF16),
                   jax.ShapeDtypeStruct((t, d), BF16), jax.ShapeDtypeStruct((t, d), BF16),
                   jax.ShapeDtypeStruct((t, d), F32), jax.ShapeDtypeStruct((t, d), F32)],
        compiler_params=_cp(("arbitrary",)),
    )(hs, proj, y_sgu, proj, proj, x2, modv, w_o_lru_g, w_o_sgu_g, w_out_g, ln1_g, ln1_b)


def _mlp_fwd(x1, modv, w_up_g, w_down_g, ln2_g, ln2_b, target, nb, seq):
    t, d = x1.shape
    nq, _, ns = w_up_g.shape
    tm = min(TM_MLP, seq)
    ts = min(TS_MLP, tm)
    tpb = seq // tm

    def body(x1_ref, mod_ref, wu_hbm, wd_hbm, g2_ref, b2_ref, tg_ref,
             rl_ref, act_ref, h2_ref, dz2_ref, df_ref, st_ref, pb_ref, wu_s, wd_s, acc, sems):
        i = pl.program_id(0)

        @pl.when(i == 0)
        def _():
            _load_weights((wu_hbm, wd_hbm), (wu_s, wd_s), sems)
            st_ref[...] = jnp.zeros_like(st_ref)

        @pl.when(i % tpb == 0)
        def _():
            pb_ref[...] = jnp.zeros_like(pb_ref)

        for sub in range(tm // ts):
            rows = slice(sub * ts, (sub + 1) * ts)
            x1v = x1_ref[rows, :]
            h2 = (x1v * (1.0 + mod_ref[4:5, :]) + mod_ref[3:4, :]).astype(BF16)
            h2_ref[rows, :] = h2
            for k in range(nq):
                cols = slice(k * ns, (k + 1) * ns)
                r = jnp.maximum(jnp.dot(h2, wu_s[k], preferred_element_type=F32), 0.0)
                act = (r * r).astype(BF16)
                rl_ref[rows, cols] = r.astype(BF16)
                act_ref[rows, cols] = act
                part = jnp.dot(act, wd_s[cols, :], preferred_element_type=F32)
                if k == 0:
                    acc[sub] = part
                else:
                    acc[sub] += part
            f = acc[sub]
            xhat, rstd = _ln_stats(ALPHA * x1v + (1.0 + mod_ref[5:6, :]) * f)
            y = xhat * g2_ref[...] + b2_ref[...]
            err = y - tg_ref[rows, :]
            dy = err * (1.0 / d)
            dz2 = _ln_bwd(dy * g2_ref[...], xhat, rstd)
            dz2_ref[rows, :] = dz2
            df_ref[rows, :] = ((1.0 + mod_ref[5:6, :]) * dz2).astype(BF16)
            st_ref[0:1, :] += _colsum(dy * xhat)
            st_ref[1:2, :] += _colsum(dy)
            st_ref[2:3, :] += (0.5 / d) * jnp.sum(_colsum(err * err), axis=1, keepdims=True)
            pb_ref[0:1, :] += _colsum(dz2 * f)

    tok = lambda i: (i, 0)
    return pl.pallas_call(
        body, name="mlp_fwd", grid=(t // tm,),
        in_specs=[pl.BlockSpec((tm, d), tok), pl.BlockSpec((None, 8, d), lambda i: (i // tpb, 0, 0)), _ANY, _ANY,
                  pl.BlockSpec((1, d), lambda i: (0, 0)), pl.BlockSpec((1, d), lambda i: (0, 0)),
                  pl.BlockSpec((tm, d), tok)],
        out_specs=[pl.BlockSpec((tm, nq * ns), tok), pl.BlockSpec((tm, nq * ns), tok),
                   pl.BlockSpec((tm, d), tok), pl.BlockSpec((tm, d), tok), pl.BlockSpec((tm, d), tok),
                   pl.BlockSpec((8, d), lambda i: (0, 0)), pl.BlockSpec((None, 8, d), lambda i: (i // tpb, 0, 0))],
        out_shape=[jax.ShapeDtypeStruct((t, nq * ns), BF16), jax.ShapeDtypeStruct((t, nq * ns), BF16),
                   jax.ShapeDtypeStruct((t, d), BF16),
                   jax.ShapeDtypeStruct((t, d), F32), jax.ShapeDtypeStruct((t, d), BF16),
                   jax.ShapeDtypeStruct((8, d), F32), jax.ShapeDtypeStruct((nb, 8, d), F32)],
        scratch_shapes=[pltpu.VMEM(w_up_g.shape, BF16), pltpu.VMEM(w_down_g.shape, BF16),
                        pltpu.VMEM((tm // ts, ts, d), F32), pltpu.SemaphoreType.DMA((2,))],
        compiler_params=_cp(("arbitrary",)),
    )(x1, modv, w_up_g, w_down_g, ln2_g, ln2_b, target)


def _mlp_bwd(df, up, w_down_g, w_up_g, dz2, x2, mix, modv, ln1_g, ln1_b, nb, seq):
    t, d = x2.shape
    nq, _, ns = w_up_g.shape
    tm = min(TM_MLP, seq)
    ts = min(TS_MLP, tm)
    tpb = seq // tm

    def body(df_ref, rl_ref, wd_hbm, wu_hbm, dz2_ref, x_ref, mix_ref, mod_ref, g1_ref, b1_ref,
             dup_ref, dz1_ref, dmix_ref, st_ref, pb_ref, wd_s, wu_s, acc, sems):
        i = pl.program_id(0)

        @pl.when(i == 0)
        def _():
            _load_weights((wd_hbm, wu_hbm), (wd_s, wu_s), sems)
            st_ref[...] = jnp.zeros_like(st_ref)

        @pl.when(i % tpb == 0)
        def _():
            pb_ref[...] = jnp.zeros_like(pb_ref)

        for sub in range(tm // ts):
            rows = slice(sub * ts, (sub + 1) * ts)
            dfv = df_ref[rows, :]
            for k in range(nq):
                cols = slice(k * ns, (k + 1) * ns)
                dup = (_mm_nt(dfv, wd_s[cols, :]) * (2.0 * rl_ref[rows, cols].astype(F32))).astype(BF16)
                dup_ref[rows, cols] = dup
                part = _mm_nt(dup, wu_s[k])
                if k == 0:
                    acc[sub] = part
                else:
                    acc[sub] += part
            dh2 = acc[sub]
            mix = mix_ref[rows, :]
            xhat, rstd = _ln_stats(ALPHA * x_ref[rows, :] + (1.0 + mod_ref[2:3, :]) * mix)
            x1 = xhat * g1_ref[...] + b1_ref[...]
            dx1 = ALPHA * dz2_ref[rows, :] + dh2 * (1.0 + mod_ref[4:5, :])
            dz1 = _ln_bwd(dx1 * g1_ref[...], xhat, rstd)
            dz1_ref[rows, :] = dz1
            dmix_ref[rows, :] = ((1.0 + mod_ref[2:3, :]) * dz1).astype(BF16)
            st_ref[0:1, :] += _colsum(dx1 * xhat)
            st_ref[1:2, :] += _colsum(dx1)
            pb_ref[0:1, :] += _colsum(dh2 * x1)
            pb_ref[1:2, :] += _colsum(dh2)
            pb_ref[2:3, :] += _colsum(dz1 * mix)

    tok = lambda i: (i, 0)
    return pl.pallas_call(
        body, name="mlp_bwd", grid=(t // tm,),
        in_specs=[pl.BlockSpec((tm, d), tok), pl.BlockSpec((tm, nq * ns), tok), _ANY, _ANY,
                  pl.BlockSpec((tm, d), tok), pl.BlockSpec((tm, d), tok), pl.BlockSpec((tm, d), tok),
                  pl.BlockSpec((None, 8, d), lambda i: (i // tpb, 0, 0)),
                  pl.BlockSpec((1, d), lambda i: (0, 0)), pl.BlockSpec((1, d), lambda i: (0, 0))],
        out_specs=[pl.BlockSpec((tm, nq * ns), tok),
                   pl.BlockSpec((tm, d), tok), pl.BlockSpec((tm, d), tok),
                   pl.BlockSpec((8, d), lambda i: (0, 0)), pl.BlockSpec((None, 8, d), lambda i: (i // tpb, 0, 0))],
        out_shape=[jax.ShapeDtypeStruct((t, nq * ns), BF16),
                   jax.ShapeDtypeStruct((t, d), F32), jax.ShapeDtypeStruct((t, d), BF16),
                   jax.ShapeDtypeStruct((8, d), F32), jax.ShapeDtypeStruct((nb, 8, d), F32)],
        scratch_shapes=[pltpu.VMEM(w_down_g.shape, BF16), pltpu.VMEM(w_up_g.shape, BF16),
                        pltpu.VMEM((tm // ts, ts, d), F32), pltpu.SemaphoreType.DMA((2,))],
        compiler_params=_cp(("arbitrary",), VMEM_LIMIT_MAX),
    )(df, up, w_down_g, w_up_g, dz2, x2, mix, modv, ln1_g, ln1_b)


def _mix_bwd(dmix, proj, y_a, y_b, hs, w_out_g, w_o_lru_g, w_o_sgu_g, seq, after=()):
    t, d = dmix.shape
    w = hs.shape[1]
    nq, d_sgu, ns = w_o_sgu_g.shape
    tm = min(TM_MIX, seq)

    def body(dmix_ref, ga_ref, gb_ref, ya_ref, yb_ref, gl_ref, hs_ref, wo_ref, wl_ref, ws_ref,
             dya_ref, dyb_ref, dga_ref, dgb_ref, dgl_ref, dyl_ref, dys_ref):
        dmerged = _mm_nt(dmix_ref[...], wo_ref[...])
        sa, sb = _sigmoid_t(ga_ref[...].astype(F32)), _sigmoid_t(gb_ref[...].astype(F32))
        dy_a = (dmerged * sa).astype(BF16)
        dy_b = (dmerged * sb).astype(BF16)
        dya_ref[...] = dy_a
        dyb_ref[...] = dy_b
        dga_ref[...] = (dmerged * ya_ref[...].astype(F32) * (sa * (1.0 - sa))).astype(BF16)
        dgb_ref[...] = (dmerged * yb_ref[...].astype(F32) * (sb * (1.0 - sb))).astype(BF16)
        dyap = _mm_nt(dy_a, wl_ref[...])
        gel, dgel = _gelu_and_grad(gl_ref[...].astype(F32))
        dyl_ref[...] = dyap * gel
        dgl_ref[...] = (dyap * hs_ref[...] * dgel).astype(BF16)
        dys = _mm_nt(dy_b[:, 0:ns], ws_ref[0])
        for q in range(1, nq):
            dys = dys + _mm_nt(dy_b[:, q * ns:(q + 1) * ns], ws_ref[q])
        dys_ref[...] = dys

    row = lambda width, col: pl.BlockSpec((tm, width), functools.partial(lambda i, k: (i, k), k=col))
    full = lambda shape: pl.BlockSpec(shape, lambda i: (0,) * len(shape))
    return pl.pallas_call(
        _ordered(body, 10, after), name="mix_bwd", grid=(t // tm,),
        in_specs=[row(d, 0), row(d, 4), row(d, 5), row(d, 0), row(d, 0), row(w, 1), row(w, 0),
                  full(w_out_g.shape), full(w_o_lru_g.shape), full(w_o_sgu_g.shape)] + [_ANY] * len(after),
        out_specs=[row(d, 0), row(d, 0), row(d, 0), row(d, 0), row(w, 0), row(w, 0), row(d_sgu, 0)],
        out_shape=[jax.ShapeDtypeStruct((t, d), BF16), jax.ShapeDtypeStruct((t, d), BF16),
                   jax.ShapeDtypeStruct((t, d), BF16), jax.ShapeDtypeStruct((t, d), BF16),
                   jax.ShapeDtypeStruct((t, w), BF16), jax.ShapeDtypeStruct((t, w), F32),
                   jax.ShapeDtypeStruct((t, d_sgu), F32)],
        compiler_params=_cp(("arbitrary",)),
    )(dmix, proj, proj, y_a, y_b, proj, hs, w_out_g, w_o_lru_g, w_o_sgu_g, *after)


def _sgu_bwd(proj, dys, w_sp, b_sp_t, ln_g, ln_b, after=()):
    t = proj.shape[0]
    d_sgu = SGU_GROUPS * HEAD
    tm = min(TM_SGU, t)
    nblk = tm // HEAD
    specs, n_piece = _sgu_specs(tm, d_sgu)

    def body(*refs):
        u = jnp.concatenate([r[...] for r in refs[:n_piece]], axis=1).astype(F32)
        v = jnp.concatenate([r[...] for r in refs[n_piece:2 * n_piece]], axis=1).astype(F32)
        dys_ref, w_ref, bt_ref, g_ref, b_ref, du_ref, dv_ref, dw_ref, st_ref, dbt_ref, dvn_s = refs[2 * n_piece:]

        @pl.when(pl.program_id(0) == 0)
        def _():
            dw_ref[...] = jnp.zeros_like(dw_ref)
            st_ref[...] = jnp.zeros_like(st_ref)
            dbt_ref[...] = jnp.zeros_like(dbt_ref)

        ug, dug_du = _gelu_and_grad(u)
        vg, dvg_dv = _gelu_and_grad(v)
        xhat, rstd = _ln_stats(vg)
        vn = (xhat * g_ref[...] + b_ref[...]).astype(BF16)
        dys_v = dys_ref[...]
        mask = _sgu_mask()
        for g in range(SGU_GROUPS):
            wm = jnp.where(mask, w_ref[g], 0.0).astype(BF16)
            cols = slice(g * HEAD, (g + 1) * HEAD)
            dw_g = jnp.zeros((HEAD, HEAD), F32)
            db_g = jnp.zeros((HEAD, 1), F32)
            for n in range(nblk):
                rows = slice(n * HEAD, (n + 1) * HEAD)
                vn_blk = vn[rows, cols]
                mixed = jnp.dot(wm, vn_blk, preferred_element_type=F32) + bt_ref[:, g:g + 1]
                dy_blk = dys_v[rows, cols]
                du_ref[rows, cols] = (dy_blk * mixed * dug_du[rows, cols]).astype(BF16)
                dmx = dy_blk * ug[rows, cols]
                dvn_s[rows, cols] = _mm_tn(wm, dmx)
                dw_g = dw_g + _mm_nt(dmx, vn_blk)
                db_g = db_g + jnp.sum(dmx, axis=1, keepdims=True)
            dw_ref[g] += jnp.where(mask, dw_g, 0.0)
            dbt_ref[:, g:g + 1] += db_g
        dvn = dvn_s[...]
        st_ref[0:1, :] += _colsum(dvn * xhat)
        st_ref[1:2, :] += _colsum(dvn)
        dv_ref[...] = (_ln_bwd(dvn * g_ref[...], xhat, rstd) * dvg_dv).astype(BF16)

    full = lambda shape: pl.BlockSpec(shape, lambda i: (0,) * len(shape))
    tok = pl.BlockSpec((tm, d_sgu), lambda i: (i, 0))
    return pl.pallas_call(
        _ordered(body, 2 * n_piece + 5, after), name="sgu_bwd", grid=(t // tm,),
        in_specs=specs + [tok, full(w_sp.shape), full(b_sp_t.shape), full(ln_g.shape), full(ln_b.shape)]
        + [_ANY] * len(after),
        out_specs=[tok, tok, full(w_sp.shape), full((8, d_sgu)), full((HEAD, HEAD))],
        out_shape=[jax.ShapeDtypeStruct((t, d_sgu), BF16), jax.ShapeDtypeStruct((t, d_sgu), BF16),
                   jax.ShapeDtypeStruct(w_sp.shape, F32), jax.ShapeDtypeStruct((8, d_sgu), F32),
                   jax.ShapeDtypeStruct((HEAD, HEAD), F32)],
        scratch_shapes=[pltpu.VMEM((tm, d_sgu), F32)],
        compiler_params=_cp(("arbitrary",)),
    )(*([proj] * (2 * n_piece)), dys, w_sp, b_sp_t, ln_g, ln_b, *after)


def _lru_bwd(proj, hs, e, dyl, lru_w, nb, seq, after=()):
    t = proj.shape[0]
    w = LRU_HEADS * HEAD
    w_conv, b_conv, w_a, b_a, w_x, b_x, lam = lru_w

    def body(x_ref, hs_ref, e_ref, dy_ref, wc_ref, bc_ref, wa_ref, ba_ref, wx_ref, bx_ref, lam_ref,
             dxl_ref, dwa_ref, dwx_ref, st_ref):
        @pl.when(pl.program_id(1) == 0)
        def _():
            dwa_ref[...] = jnp.zeros_like(dwa_ref)
            dwx_ref[...] = jnp.zeros_like(dwx_ref)
            st_ref[...] = jnp.zeros_like(st_ref)

        xl = x_ref[...].astype(F32)
        xc, r, gi, big_l, a, m2 = _lru_gates(xl, wc_ref, bc_ref, wa_ref, ba_ref, wx_ref, bx_ref, lam_ref)
        inv_mult = lax.rsqrt(m2)
        mult = m2 * inv_mult
        dh = dy_ref[...] + _shift_up(e_ref[...], 1)
        da = dh * _shift_down(hs_ref[...], 1)
        dmult = dh * (gi * xc)
        d_i = dh * (mult * xc)
        dxc = dh * (mult * gi)
        dla = a * (da - dmult * (a * inv_mult))
        dr = dla * big_l
        d_big_l = _colsum(dla * r)
        dra = dr * (r * (1.0 - r))
        dia = d_i * (gi * (1.0 - gi))
        dwa_ref[...] += _mm_tn(xc, dra)
        dwx_ref[...] += _mm_tn(xc, dia)
        dxc = dxc + _mm_nt(dra, wa_ref[...]) + _mm_nt(dia, wx_ref[...])
        dxl = wc_ref[CONV_WIDTH - 1:CONV_WIDTH, :] * dxc
        st_ref[4 + CONV_WIDTH - 1:4 + CONV_WIDTH, :] += _colsum(dxc * xl)
        for k in range(CONV_WIDTH - 1):
            ahead = _shift_up(dxc, CONV_WIDTH - 1 - k)
            dxl = dxl + wc_ref[k:k + 1, :] * ahead
            st_ref[4 + k:5 + k, :] += _colsum(ahead * xl)
        dxl_ref[...] = dxl.astype(BF16)
        st_ref[0:1, :] += _colsum(dra)
        st_ref[1:2, :] += _colsum(dia)
        st_ref[2:3, :] += d_big_l * (LRU_C * _sigmoid(-lam_ref[...]))
        st_ref[3:4, :] += _colsum(dxc)

    col = lambda hd, b: (0, hd)
    head = lambda hd, b: (hd, 0, 0)
    tok = lambda hd, b: (b, hd)
    seq_blk = pl.BlockSpec((seq, HEAD), tok)
    return pl.pallas_call(
        _ordered(body, 11, after), name="lru_bwd", grid=(LRU_HEADS, nb),
        in_specs=[seq_blk, seq_blk, seq_blk, seq_blk,
                  pl.BlockSpec((CONV_WIDTH, HEAD), col), pl.BlockSpec((1, HEAD), col),
                  pl.BlockSpec((None, HEAD, HEAD), head), pl.BlockSpec((1, HEAD), col),
                  pl.BlockSpec((None, HEAD, HEAD), head), pl.BlockSpec((1, HEAD), col),
                  pl.BlockSpec((1, HEAD), col)] + [_ANY] * len(after),
        out_specs=[seq_blk, pl.BlockSpec((None, HEAD, HEAD), head), pl.BlockSpec((None, HEAD, HEAD), head),
                   pl.BlockSpec((8, HEAD), col)],
        out_shape=[jax.ShapeDtypeStruct((t, w), BF16), jax.ShapeDtypeStruct((LRU_HEADS, HEAD, HEAD), F32),
                   jax.ShapeDtypeStruct((LRU_HEADS, HEAD, HEAD), F32), jax.ShapeDtypeStruct((8, w), F32)],
        compiler_params=_cp(("arbitrary", "arbitrary")),
    )(proj, hs, e, dyl, w_conv, b_conv, w_a, b_a, w_x, b_x, lam, *after)


def _weight_grad(a, g, col_shards, name, after=()):
    t, k = a.shape
    n = g.shape[1]
    tt = min(TT_DW, t)
    tk = k if k <= 1536 else 1024
    ns = n // N_CHIPS if col_shards else n
    narrow = col_shards and ns < 512
    tn = n if narrow else min(ns, 768 if ns % 768 == 0 else 1024)
    while ns % tn and not narrow:
        tn //= 2
    per = max(ns // tn, 1)

    def body(a_ref, g_ref, o_ref):
        @pl.when(pl.program_id(2) == 0)
        def _():
            o_ref[...] = jnp.zeros_like(o_ref)

        res = _mm_tn(a_ref[...], g_ref[...])
        if narrow:
            for q in range(N_CHIPS):
                o_ref[q] += res[:, q * ns:(q + 1) * ns]
        else:
            o_ref[...] += res

    if narrow:
        out_spec = pl.BlockSpec((N_CHIPS, tk, ns), lambda i, j, s: (0, i, 0))
        out_shape = jax.ShapeDtypeStruct((N_CHIPS, k, ns), F32)
    elif col_shards:
        out_spec = pl.BlockSpec((None, tk, tn), lambda i, j, s: (j // per, i, j % per))
        out_shape = jax.ShapeDtypeStruct((N_CHIPS, k, ns), F32)
    else:
        out_spec = pl.BlockSpec((tk, tn), lambda i, j, s: (i, j))
        out_shape = jax.ShapeDtypeStruct((k, n), F32)
    return pl.pallas_call(
        _ordered(body, 2, after), name=name, grid=(k // tk, n // tn, t // tt),
        in_specs=[pl.BlockSpec((tt, tk), lambda i, j, s: (s, i)), pl.BlockSpec((tt, tn), lambda i, j, s: (s, j))]
        + [_ANY] * len(after),
        out_specs=out_spec, out_shape=out_shape,
        compiler_params=_cp(("arbitrary", "arbitrary", "arbitrary")),
    )(a, g, *after)


def _input_grad(dproj, ws, slots, dz1, x2, modv, nb, seq, after=()):
    t, d = x2.shape
    nq = len(ws)
    ns = ws[0].shape[1]
    tm = min(TM_DH, seq)
    ts = min(TS_MLP, tm)
    tpb = seq // tm

    def body(slot_ref, dp_ref, *refs):
        w_hbm = refs[:nq]
        dz1_ref, x_ref, mod_ref, gx_ref, db_ref, pb_ref, w_s, acc, sems = refs[nq:]
        i = pl.program_id(0)

        @pl.when(i == 0)
        def _():
            _load_weights(w_hbm, [w_s.at[slot_ref[k]] for k in range(nq)], sems)
            db_ref[...] = jnp.zeros_like(db_ref)

        @pl.when(i % tpb == 0)
        def _():
            pb_ref[...] = jnp.zeros_like(pb_ref)

        for sub in range(tm // ts):
            rows = slice(sub * ts, (sub + 1) * ts)
            for q in range(nq):
                dp = dp_ref[rows, q * ns:(q + 1) * ns]
                part = _mm_nt(dp, w_s[q])
                if q == 0:
                    acc[sub] = part
                else:
                    acc[sub] += part
                db_ref[q, 0:1, :] += _colsum(dp.astype(F32))
            dh = acc[sub]
            gx_ref[rows, :] = ALPHA * dz1_ref[rows, :] + dh * (1.0 + mod_ref[1:2, :])
            pb_ref[0:1, :] += _colsum(dh * x_ref[rows, :])
            pb_ref[1:2, :] += _colsum(dh)

    tok = lambda i, s: (i, 0)
    in_specs = [pl.BlockSpec((tm, nq * ns), tok)] + [_ANY] * nq
    in_specs += [pl.BlockSpec((tm, d), tok), pl.BlockSpec((tm, d), tok),
                 pl.BlockSpec((None, 8, d), lambda i, s: (i // tpb, 0, 0))] + [_ANY] * len(after)
    return pl.pallas_call(
        _ordered(body, 5 + nq, after), name="input_grad",
        grid_spec=pltpu.PrefetchScalarGridSpec(
            num_scalar_prefetch=1, grid=(t // tm,), in_specs=in_specs,
            out_specs=[pl.BlockSpec((tm, d), tok), pl.BlockSpec((nq, 8, ns), lambda i, s: (0, 0, 0)),
                       pl.BlockSpec((None, 8, d), lambda i, s: (i // tpb, 0, 0))],
            scratch_shapes=[pltpu.VMEM((nq, d, ns), BF16), pltpu.VMEM((tm // ts, ts, d), F32),
                            pltpu.SemaphoreType.DMA((nq,))]),
        out_shape=[jax.ShapeDtypeStruct((t, d), F32), jax.ShapeDtypeStruct((nq, 8, ns), F32),
                   jax.ShapeDtypeStruct((nb, 8, d), F32)],
        compiler_params=_cp(("arbitrary",)),
    )(slots, dproj, *ws, dz1, x2, modv, *after)


def _rows128(v):
    flat = v.reshape(-1, HEAD)
    pad = (-flat.shape[0]) % 8
    return jnp.pad(flat, ((0, pad), (0, 0))) if pad else flat


def kernel(x, c, w_ada, b_ada, w_in, b_in, w_conv, b_conv, w_rg_a, b_rg_a, w_rg_x, b_rg_x, lru_lambda, w_sp, b_sp, ln_v_g, ln_v_b, w_o_lru, w_o_sgu, w_out, ln1_g, ln1_b, w_up, w_down, ln2_g, ln2_b, loss_target, m_w_ada, m_b_ada, m_w_in, m_b_in, m_w_conv, m_b_conv, m_w_rg_a, m_b_rg_a, m_w_rg_x, m_b_rg_x, m_lru_lambda, m_w_sp, m_b_sp, m_ln_v_g, m_ln_v_b, m_w_o_lru, m_w_o_sgu, m_w_out, m_ln1_g, m_ln1_b, m_w_up, m_w_down, m_ln2_g, m_ln2_b, v_w_ada, v_b_ada, v_w_in, v_b_in, v_w_conv, v_b_conv, v_w_rg_a, v_b_rg_a, v_w_rg_x, v_b_rg_x, v_lru_lambda, v_w_sp, v_b_sp, v_ln_v_g, v_ln_v_b, v_w_o_lru, v_w_o_sgu, v_w_out, v_ln1_g, v_ln1_b, v_w_up, v_w_down, v_ln2_g, v_ln2_b):
    given = dict(locals())
    nb, seq, d = x.shape
    t = nb * seq
    w_lru = LRU_HEADS * HEAD
    d_sgu = SGU_GROUPS * HEAD
    xi, yi, ci = lax.axis_index("x"), lax.axis_index("y"), lax.axis_index("c")
    chip = 2 * xi + yi
    dev = 2 * chip + ci
    cidx = jnp.reshape(ci, (1,)).astype(jnp.int32)

    x2 = x.reshape(t, d)
    target = loss_target.reshape(t, d)

    big = ["w_in", "w_o_lru", "w_o_sgu", "w_out", "w_up", "w_down"]
    shards_a = [w_in[0].astype(BF16)]
    shards_b = [given[n][0].astype(BF16) for n in big[1:]]
    pidx = jnp.reshape(chip, (1,)).astype(jnp.int32)

    c_rows = _rows128(c)
    wconv_rows = _rows128(w_conv[0])
    slab0 = _all_gather_small(jnp.concatenate([c_rows, wconv_rows], axis=0), "gather_c_wconv")
    slab0 = slab0.reshape(N_DEV, -1, HEAD)
    c_all = slab0[:, :c_rows.shape[0]].reshape(N_DEV * nb, d)
    n_wc = CONV_WIDTH * (w_lru // N_CHIPS) // HEAD
    wc = slab0[0::2, c_rows.shape[0]:c_rows.shape[0] + n_wc].reshape(N_CHIPS, CONV_WIDTH, w_lru // N_CHIPS)
    w_conv_full = jnp.transpose(wc, (1, 0, 2)).reshape(CONV_WIDTH, w_lru)

    n_ada = w_ada.shape[2]
    b_ada_cols = lax.dynamic_slice(b_ada, (0, chip * n_ada), (1, n_ada))
    mod_cols = _ada_fwd(c_all, w_ada[0], b_ada_cols)
    half = (N_DEV * nb) // 2
    mod_half = lax.dynamic_slice(mod_cols, (ci * half, 0), (half, n_ada))
    mod_g = _all_gather_small(mod_half, "gather_mod").reshape(N_CHIPS, 2, half, n_ada)
    mod_all = jnp.transpose(mod_g, (1, 2, 0, 3)).reshape(N_DEV * nb, N_CHIPS * n_ada)
    mod_loc = lax.dynamic_slice(mod_all, (dev * nb, 0), (nb, N_CHIPS * n_ada)).reshape(nb, 6, d)
    modv = jnp.pad(mod_loc, ((0, 0), (0, 2), (0, 0)))

    lru_w = (w_conv_full, b_conv, w_rg_a[0], b_rg_a, w_rg_x[0], b_rg_x, lru_lambda)
    b_sp_t = jnp.transpose(b_sp[0])

    land = lambda s: jax.ShapeDtypeStruct((N_CHIPS,) + s.shape, s.dtype)
    sds = lambda s: jax.ShapeDtypeStruct(s.shape, s.dtype)
    started_a = _split_start(shards_a, [sds(shards_a[0])] * 3, _peer_gather_copies((0, 1, 2)), 3, "gather_w_in_start",
                             after=(modv,))
    shards_b, shards_c = shards_b[:3], shards_b[3:]

    ids = lambda *v: jnp.stack(v).astype(jnp.int32)
    modv_t = modv + started_a[-1][0:1, 0:1]
    proj, h = _proj_fwd(x2, modv_t, [started_a[2]], ids(chip), b_in, seq, "proj_fwd_own")
    own_a, lands_a = _split_wait(started_a, 1, _peer_gather_copies((0, 1)), "gather_w_in_wait_near", after=(proj,))
    started_b = _split_start(shards_b, [land(s) for s in shards_b], _gather_copies, 3 * len(shards_b),
                             "gather_w_mix_start", after=(lands_a[0],))
    started_c = _split_start(shards_c, [land(s) for s in shards_c], _gather_copies, 3 * len(shards_c),
                             "gather_w_mlp_start", after=(started_b[-1],))
    modv_t = modv + started_c[-1][0:1, 0:1]
    (proj,) = _proj_fwd(x2, modv_t, lands_a[:2], ids(chip ^ 1, chip ^ 2), b_in, seq, "proj_fwd_near", proj_in=proj)
    own_a, lands_a = _split_wait((started_a[0], started_a[1], *own_a, *lands_a, started_a[-1]), 1,
                                 _peer_gather_copies((2,)), "gather_w_in_wait_far", after=(proj,))
    (proj,) = _proj_fwd(x2, modv, lands_a[2:], ids(chip ^ 3), b_in, seq, "proj_fwd_far", proj_in=proj)
    w_in_shards, w_in_chips = own_a + lands_a, ids(chip, chip ^ 1, chip ^ 2, chip ^ 3)
    a, inp = _lru_prep(proj, lru_w, nb, seq)
    a3 = a.reshape(nb, seq, w_lru)
    hs = _scan(a3, inp.reshape(nb, seq, w_lru), False, "lru_scan").reshape(t, w_lru)
    y_sgu = _sgu_fwd(proj, w_sp[0], b_sp_t, ln_v_g, ln_v_b)
    shards_b, lands_b = _split_wait(started_b, len(shards_b), _gather_copies, "gather_w_mix_wait", after=(hs, y_sgu))
    w_o_lru_g, w_o_sgu_g, w_out_g = _fill_own_slot(lands_b, shards_b, pidx, ["own_" + n for n in big[1:4]])
    w_o_lru_g = w_o_lru_g.reshape(w_lru, d)
    w_out_g = w_out_g.reshape(d, d)
    yap, y_a, y_b, merged, mix, x1 = _mix_fwd(hs, proj, y_sgu, x2, modv, w_o_lru_g, w_o_sgu_g, w_out_g, ln1_g, ln1_b, seq)
    shards_c, lands_c = _split_wait(started_c, len(shards_c), _gather_copies, "gather_w_mlp_wait", after=(x1,))
    w_up_g, w_down_g = _fill_own_slot(lands_c, shards_c, pidx, ["own_" + n for n in big[4:]])
    w_down_g = w_down_g.reshape(-1, d)
    up, act, h2, dz2, df, st2, pb2 = _mlp_fwd(x1, modv, w_up_g, w_down_g, ln2_g, ln2_b, target, nb, seq)

    part = {}

    def to_sibling_start(group, tag, after=()):
        g4 = []
        for n in group:
            shard = given[n].shape[1:]
            g4.append(part[n].reshape(N_CHIPS, 2, shard[0] // 2, shard[1]))
        shapes = [jax.ShapeDtypeStruct((N_CHIPS,) + g.shape[2:], F32) for g in g4]
        return _split_start(g4, shapes, _to_sibling_copies, len(g4), "grads_to_sibling_start_" + tag, after)

    def to_chips_start(group, started, tag, after=()):
        g4, recv = _split_wait(started, len(group), _to_sibling_copies, "grads_to_sibling_wait_" + tag, after)
        own4 = [_add_own_half(g4[k], recv[k], cidx, "grad_pair_sum_" + n) for k, n in enumerate(group)]
        shapes = [jax.ShapeDtypeStruct((3,) + o.shape[1:], BF16) for o in own4]
        return _split_start(own4, shapes, _chip_exchange_copies, 3 * len(own4), "grads_chip_exchange_start_" + tag)

    def chips_finish(group, started, tag, after=()):
        own4, slots = _split_wait(started, len(group), _chip_exchange_copies, "grads_chip_exchange_wait_" + tag, after)
        return [_sum_own_and_peers(own4[k], slots[k], pidx, "grad_chip_sum_" + n) for k, n in enumerate(group)]

    dup, dz1, dmix, st1, pb1 = _mlp_bwd(df, up, w_down_g, w_up_g, dz2, x2, mix, modv, ln1_g, ln1_b, nb, seq)
    group1 = ["w_up", "w_down"]
    part["w_up"] = _weight_grad(h2, dup, True, "grad_w_up")
    part["w_down"] = _weight_grad(act, df, False, "grad_w_down")
    sib1 = to_sibling_start(group1, "mlp")
    dy_a, dy_b, dga, dgb, dgl, dyl, dys = _mix_bwd(dmix, proj, y_a, y_b, hs, w_out_g, w_o_lru_g, w_o_sgu_g, seq,
                                                   after=(sib1[-1],))
    group2 = ["w_o_lru", "w_o_sgu", "w_out"]
    part["w_o_lru"] = _weight_grad(yap, dy_a, False, "grad_w_o_lru")
    part["w_o_sgu"] = _weight_grad(y_sgu, dy_b, True, "grad_w_o_sgu")
    part["w_out"] = _weight_grad(merged, dmix, False, "grad_w_out")
    chips1 = to_chips_start(group1, sib1, "mlp", after=(dys, part["w_o_lru"], part["w_o_sgu"], part["w_out"]))
    sib2 = to_sibling_start(group2, "mix", after=(chips1[-1],))
    du, dv, g_w_sp, st_sgu, g_b_sp_t = _sgu_bwd(proj, dys, w_sp[0], b_sp_t, ln_v_g, ln_v_b, after=(sib2[-1],))
    dyl3 = dyl.reshape(nb, seq, w_lru)
    e = _scan(a3, dyl3, True, "lru_scan_bwd").reshape(t, w_lru)
    chips2 = to_chips_start(group2, sib2, "mix", after=(e, du))
    dxl, g_w_rg_a, g_w_rg_x, st_lru = _lru_bwd(proj, hs, e, dyl, lru_w, nb, seq, after=(chips2[-1],))
    dproj = jnp.concatenate([dxl, dgl, du, dv, dga, dgb], axis=1)

    didx = jnp.reshape(dev, (1,)).astype(jnp.int32)
    early = [
        ("w_conv", st_lru[4:8]), ("b_conv", st_lru[3]), ("w_rg_a", g_w_rg_a), ("b_rg_a", st_lru[0]),
        ("w_rg_x", g_w_rg_x), ("b_rg_x", st_lru[1]), ("lru_lambda", st_lru[2]), ("w_sp", g_w_sp),
        ("b_sp", jnp.transpose(g_b_sp_t[:, :SGU_GROUPS])), ("ln_v_g", st_sgu[0]), ("ln_v_b", st_sgu[1]),
        ("ln1_g", st1[0]), ("ln1_b", st1[1]), ("ln2_g", st2[0]), ("ln2_b", st2[1]),
    ]
    pieces_e = [_rows128(v) for _, v in early]
    slab_e = jnp.concatenate(pieces_e, axis=0)
    slab_e = jnp.pad(slab_e, ((0, (-slab_e.shape[0]) % TR_EW), (0, 0)))
    small_st = _split_start([slab_e], [jax.ShapeDtypeStruct((N_DEV,) + slab_e.shape, F32)], _all_devices_copies, N_DEV - 1,
                            "small_grads_start")

    group3 = ["w_in"]
    part["w_in"] = _weight_grad(h, dproj, True, "grad_w_in", after=(small_st[-1],))
    sib3 = to_sibling_start(group3, "in")
    chips3 = to_chips_start(group3, sib3, "in")
    grad_x2, g_b_in4, pb0 = _input_grad(dproj, w_in_shards, w_in_chips, dz1, x2, modv, nb, seq, after=(chips3[-1],))
    halves12 = chips_finish(group1, chips1, "mlp", after=(grad_x2,)) + chips_finish(group2, chips2, "mix", after=(grad_x2,))
    swap12 = _split_start(halves12, [jax.ShapeDtypeStruct(hv.shape, F32) for hv in halves12], _swap_copies, len(halves12),
                          "grads_swap_start")
    loss = lax.psum(st2[2, 0] + swap12[-1][0, 0], ("x", "y", "c"))
    grads = {}

    dmod_loc = jnp.stack([pb0[:, 1], pb0[:, 0], pb1[:, 2], pb1[:, 1], pb1[:, 0], pb2[:, 0]], axis=1)
    late = [("dmod", dmod_loc), ("b_in", g_b_in4[:, 0])]
    pieces_l = [_rows128(v) for _, v in late]
    slab_l = jnp.concatenate(pieces_l, axis=0)
    gathered = _all_gather_small(slab_l, "gather_small_grads", after=(swap12[-1],)).reshape(N_DEV, slab_l.shape[0], HEAD)
    rows_dmod = dmod_loc.size // HEAD
    dmod_all = gathered[:, :rows_dmod].reshape(N_DEV * nb, 6 * d)
    grads["b_in"] = _sum_slots(gathered[:, rows_dmod:], "grad_b_in_sum").reshape(1, -1)

    (slab_e,), (lands_e,) = _split_wait(small_st, 1, _all_devices_copies, "small_grads_wait", after=(gathered,))
    summed = _sum_devices(lands_e, slab_e, didx, "small_grad_sum")
    off = 0
    for (n, v), piece in zip(early, pieces_e):
        grads[n] = summed[off:off + v.size // HEAD].reshape(v.shape)
        off += piece.shape[0]

    mine12, theirs12 = _split_wait(swap12, len(halves12), _swap_copies, "grads_swap_wait", after=(summed,))
    (mine3,) = chips_finish(group3, chips3, "in", after=(summed,))
    (theirs3,) = _exchange([mine3], [jax.ShapeDtypeStruct(mine3.shape, F32)], _swap_copies, 1, "grads_swap_w_in")
    mine = dict(zip(group1 + group2 + group3, mine12 + [mine3]))
    theirs = dict(zip(group1 + group2 + group3, theirs12 + [theirs3]))

    dmod_cols = lax.dynamic_slice(dmod_all, (0, chip * n_ada), (N_DEV * nb, n_ada))
    grads["w_ada"], grads["b_ada"] = _ada_bwd(c_all, dmod_all, dmod_cols)
    n_wcs = w_lru // N_CHIPS
    grads["w_conv"] = lax.dynamic_slice(grads["w_conv"], (0, chip * n_wcs), (CONV_WIDTH, n_wcs))

    names = ['w_ada', 'b_ada', 'w_in', 'b_in', 'w_conv', 'b_conv', 'w_rg_a', 'b_rg_a', 'w_rg_x', 'b_rg_x', 'lru_lambda',
             'w_sp', 'b_sp', 'ln_v_g', 'ln_v_b', 'w_o_lru', 'w_o_sgu', 'w_out', 'ln1_g', 'ln1_b', 'w_up', 'w_down',
             'ln2_g', 'ln2_b']
    two_d = lambda v: v.reshape(-1, v.shape[-1])
    done = {}
    small_names = [n for n in names if n not in big and n != "w_ada"]
    small_out = _adamw_many([(two_d(given[n]), two_d(grads[n].reshape(given[n].shape)), two_d(given["m_" + n]),
                              two_d(given["v_" + n])) for n in small_names], "adamw_small")
    for n, res in zip(small_names, small_out):
        done[n] = (grads[n],) + tuple(res)
    for n in big + ["w_ada"]:
        w2, m2, v2 = two_d(given[n]), two_d(given["m_" + n]), two_d(given["v_" + n])
        if n in big:
            done[n] = _adamw_halves(w2, mine[n], theirs[n], m2, v2, cidx, "adamw_" + n)
        else:
            done[n] = (grads[n],) + tuple(_adamw(w2, two_d(grads[n]), m2, v2, "adamw_" + n))
    outs = [[done[n][k].reshape(given[n].shape) for n in names] for k in range(4)]
    return (loss, grad_x2.reshape(nb, seq, d), *outs[0], *outs[1], *outs[2], *outs[3])
```

```python
import functools
import math

import jax
import jax.numpy as jnp
from jax import lax
from jax.experimental import pallas as pl
from jax.experimental.pallas import tpu as pltpu

F32 = jnp.float32
BF16 = jnp.bfloat16
MESH = pl.DeviceIdType.MESH

N_CHIPS = 4
N_DEV = 8
LRU_HEADS = 10
HEAD = 128
SGU_GROUPS = 6
SGU_CHUNK = 64
CONV_WIDTH = 4
LRU_C = 8.0
ALPHA = 2.0 ** 0.25
LN_EPS = 1e-5
ADAM_LR, ADAM_B1, ADAM_B2, ADAM_EPS, ADAM_WD, ADAM_STEP = 0.001, 0.9, 0.999, 1e-08, 0.01, 10

VMEM_LIMIT = 56 * 1024 * 1024
VMEM_LIMIT_MAX = 62 * 1024 * 1024
TM_PROJ = 1024
TM_MIX = 256
TM_MLP = 512
TS_MLP = 256
TM_SGU = 512
TM_DH = 512
TT_DW = 1024
TC_SCAN = 256
TR_EW = 256


def _cp(sem=None, limit=None):
    return pltpu.CompilerParams(dimension_semantics=sem, vmem_limit_bytes=limit or VMEM_LIMIT)


def _mm(a, b):
    return jnp.dot(a.astype(BF16), b.astype(BF16), preferred_element_type=F32)


def _mm_nt(a, b):
    return lax.dot_general(a.astype(BF16), b.astype(BF16), (((1,), (1,)), ((), ())), preferred_element_type=F32)


def _mm_tn(a, b):
    return lax.dot_general(a.astype(BF16), b.astype(BF16), (((0,), (0,)), ((), ())), preferred_element_type=F32)


def _sigmoid(x):
    return 1.0 / (1.0 + jnp.exp(-x))


def _sigmoid_t(x):
    return 0.5 * jnp.tanh(0.5 * x) + 0.5


_GELU_K = math.sqrt(2.0 / math.pi)


def _gelu(x):
    t = jnp.tanh(_GELU_K * (x + 0.044715 * (x * x * x)))
    return 0.5 * x * (1.0 + t)


def _gelu_and_grad(x):
    x2 = x * x
    t = jnp.tanh(_GELU_K * (x + 0.044715 * (x2 * x)))
    g = 0.5 * x * (1.0 + t)
    dg = 0.5 * (1.0 + t) + 0.5 * x * (1.0 - t * t) * (_GELU_K * (1.0 + 3.0 * 0.044715 * x2))
    return g, dg


def _ln_stats(z):
    mu = jnp.mean(z, axis=-1, keepdims=True)
    zc = z - mu
    var = jnp.mean(zc * zc, axis=-1, keepdims=True)
    rstd = lax.rsqrt(var + LN_EPS)
    return zc * rstd, rstd


def _ln_bwd(dxh, xhat, rstd):
    m1 = jnp.mean(dxh, axis=-1, keepdims=True)
    m2 = jnp.mean(dxh * xhat, axis=-1, keepdims=True)
    return rstd * (dxh - m1 - xhat * m2)


def _colsum(v):
    return jnp.sum(v, axis=0, keepdims=True)


def _shift_down(v, j):
    if j == 0:
        return v
    rows = lax.broadcasted_iota(jnp.int32, v.shape, 0)
    return jnp.where(rows >= j, pltpu.roll(v, j, 0), 0.0)


def _shift_up(v, j):
    if j == 0:
        return v
    n = v.shape[0]
    rows = lax.broadcasted_iota(jnp.int32, v.shape, 0)
    return jnp.where(rows < n - j, pltpu.roll(v, n - j, 0), 0.0)


def _load_weights(srcs, dsts, sems):
    cps = [pltpu.make_async_copy(s, dd, sems.at[k]) for k, (s, dd) in enumerate(zip(srcs, dsts))]
    for cp in cps:
        cp.start()
    for cp in cps:
        cp.wait()


def _my_pos():
    return lax.axis_index("x"), lax.axis_index("y"), lax.axis_index("c")


def _all_gather_small(v, name, after=()):
    m_per, n = v.shape

    def body(x_ref, out_ref, send_sems, recv_sems, local_sem):
        x, y, c = _my_pos()
        me, sibling = (x, y, c), (x, y, 1 - c)
        chips = [(1 - x, y), (x, 1 - y), (1 - x, 1 - y)]

        def rows(px, py, pc):
            return out_ref.at[pl.ds((4 * px + 2 * py + pc) * m_per, m_per), :]

        def copy(k, block, to, src=None):
            return pltpu.make_async_remote_copy(
                src_ref=rows(*block) if src is None else src, dst_ref=rows(*block),
                send_sem=send_sems.at[k], recv_sem=recv_sems.at[k], device_id=to, device_id_type=MESH)

        mine = pltpu.make_async_copy(x_ref, rows(*me), local_sem)
        mine.start()
        first = [copy(0, me, sibling, src=x_ref)]
        first += [copy(1 + j, me, (*chip, c), src=x_ref) for j, chip in enumerate(chips)]
        for cp in first:
            cp.start()
        passed = [copy(4 + j, (*chip, c), sibling) for j, chip in enumerate(chips)]
        for j, chip in enumerate(chips):
            copy(1 + j, (*chip, c), me).wait_recv()
            passed[j].start()
        copy(0, sibling, me).wait_recv()
        for j, chip in enumerate(chips):
            copy(4 + j, (*chip, 1 - c), me).wait_recv()
        for cp in first + passed:
            cp.wait_send()
        mine.wait()

    return pl.pallas_call(
        _ordered(body, 1, after), name=name,
        out_shape=jax.ShapeDtypeStruct((N_DEV * m_per, n), v.dtype),
        in_specs=[pl.BlockSpec(memory_space=pltpu.VMEM)] + [pl.BlockSpec(memory_space=pl.ANY)] * len(after),
        out_specs=pl.BlockSpec(memory_space=pltpu.VMEM),
        scratch_shapes=[pltpu.SemaphoreType.DMA((7,)), pltpu.SemaphoreType.DMA((7,)), pltpu.SemaphoreType.DMA],
        compiler_params=pltpu.CompilerParams(vmem_limit_bytes=VMEM_LIMIT),
    )(v, *after)


_HBM = pl.BlockSpec(memory_space=pltpu.HBM)
_ANY = pl.BlockSpec(memory_space=pl.ANY)
_SEM = pl.BlockSpec(memory_space=pltpu.SEMAPHORE)
_EFFECT = pltpu.SideEffectType.DATAFLOW_SIDE_EFFECTING


def _ordered(body, n_in, after):
    k = len(after)
    if not k:
        return body
    return lambda *refs: body(*refs[:n_in], *refs[n_in + k:])


def _gather_copies(ins, lands, send_sems, recv_sems):
    x, y, c = _my_pos()
    p = 2 * x + y
    peers = [(x, 1 - y), (1 - x, y), (1 - x, 1 - y)]
    sends, recvs = [], []
    for k in range(len(ins)):
        for j, (qx, qy) in enumerate(peers):
            sems = dict(send_sem=send_sems.at[3 * k + j], recv_sem=recv_sems.at[3 * k + j],
                        device_id=(qx, qy, c), device_id_type=MESH)
            sends.append(pltpu.make_async_remote_copy(src_ref=ins[k], dst_ref=lands[k].at[p], **sems))
            recvs.append(pltpu.make_async_remote_copy(src_ref=ins[k], dst_ref=lands[k].at[2 * qx + qy], **sems))
    return sends, recvs


def _peer_gather_copies(peers):
    def copies(ins, lands, send_sems, recv_sems):
        x, y, c = _my_pos()
        where = [(x, 1 - y), (1 - x, y), (1 - x, 1 - y)]
        cps = [pltpu.make_async_remote_copy(
            src_ref=ins[0], dst_ref=lands[j], send_sem=send_sems.at[j], recv_sem=recv_sems.at[j],
            device_id=(*where[j], c), device_id_type=MESH) for j in peers]
        return cps, cps
    return copies


def _far_gather_copies(ins, lands, send_sems, recv_sems):
    x, y, c = _my_pos()
    cps = [pltpu.make_async_remote_copy(
        src_ref=ins[0], dst_ref=lands[0], send_sem=send_sems.at[0], recv_sem=recv_sems.at[0],
        device_id=(1 - x, 1 - y, c), device_id_type=MESH)]
    return cps, cps


def _to_sibling_copies(ins, lands, send_sems, recv_sems):
    x, y, c = _my_pos()
    cps = [pltpu.make_async_remote_copy(
        src_ref=ins[k].at[:, 1 - c], dst_ref=lands[k], send_sem=send_sems.at[k], recv_sem=recv_sems.at[k],
        device_id=(x, y, 1 - c), device_id_type=MESH) for k in range(len(ins))]
    return cps, cps


def _chip_exchange_copies(ins, lands, send_sems, recv_sems):
    x, y, c = _my_pos()
    peers = [(x, 1 - y), (1 - x, y), (1 - x, 1 - y)]
    cps = []
    for k in range(len(ins)):
        for j, (qx, qy) in enumerate(peers):
            cps.append(pltpu.make_async_remote_copy(
                src_ref=ins[k].at[2 * qx + qy], dst_ref=lands[k].at[j], send_sem=send_sems.at[3 * k + j],
                recv_sem=recv_sems.at[3 * k + j], device_id=(qx, qy, c), device_id_type=MESH))
    return cps, cps


def _all_devices_copies(ins, lands, send_sems, recv_sems):
    x, y, c = _my_pos()
    me = 4 * x + 2 * y + c
    sends, recvs = [], []
    for r in range(1, N_DEV):
        px = 1 - x if r & 4 else x
        py = 1 - y if r & 2 else y
        pc = 1 - c if r & 1 else c
        sems = dict(send_sem=send_sems.at[r - 1], recv_sem=recv_sems.at[r - 1], device_id=(px, py, pc), device_id_type=MESH)
        sends.append(pltpu.make_async_remote_copy(src_ref=ins[0], dst_ref=lands[0].at[me], **sems))
        recvs.append(pltpu.make_async_remote_copy(src_ref=ins[0], dst_ref=lands[0].at[4 * px + 2 * py + pc], **sems))
    return sends, recvs


def _swap_copies(ins, lands, send_sems, recv_sems):
    x, y, c = _my_pos()
    cps = [pltpu.make_async_remote_copy(
        src_ref=ins[k], dst_ref=lands[k], send_sem=send_sems.at[k], recv_sem=recv_sems.at[k],
        device_id=(x, y, 1 - c), device_id_type=MESH) for k in range(len(ins))]
    return cps, cps


def _split_start(ins, land_shapes, copies, n_sems, name, after=()):
    n, nl = len(ins), len(land_shapes)
    first_out = n + nl + len(after)

    def body(*refs):
        in_refs, land_refs = refs[:n], refs[n:n + nl]
        send_sems, recv_sems = refs[first_out:first_out + 2]
        token = refs[-1]
        sends, _ = copies(in_refs, land_refs, send_sems, recv_sems)
        for cp in sends:
            cp.start()
        token[...] = jnp.zeros_like(token)

    lands = [pltpu.with_memory_space_constraint(lax.empty(s.shape, s.dtype), pltpu.HBM) for s in land_shapes]
    ins = [pltpu.with_memory_space_constraint(s, pltpu.HBM) for s in ins]
    return pl.pallas_call(
        body, name=name,
        out_shape=(pltpu.SemaphoreType.DMA((n_sems,)), pltpu.SemaphoreType.DMA((n_sems,)),
                   *[pltpu.HBM(s.shape, s.dtype) for s in ins], *[pltpu.HBM(s.shape, s.dtype) for s in lands],
                   jax.ShapeDtypeStruct((8, HEAD), F32)),
        in_specs=[_HBM] * (n + nl) + [pl.BlockSpec(memory_space=pl.ANY)] * len(after),
        out_specs=(_SEM, _SEM, *([_HBM] * (n + nl)), pl.BlockSpec(memory_space=pltpu.VMEM)),
        input_output_aliases={k: 2 + k for k in range(n + nl)},
        compiler_params=pltpu.CompilerParams(has_side_effects=_EFFECT),
    )(*ins, *lands, *after)


def _split_wait(started, n, copies, name, after=()):
    send_sems, recv_sems = started[0], started[1]
    bufs = started[2:-1]
    nb = len(bufs)

    def body(*refs):
        in_refs, land_refs = refs[:n], refs[n:nb]
        sends, recvs = copies(in_refs, land_refs, refs[nb], refs[nb + 1])
        for cp in sends:
            cp.wait_send()
        for cp in recvs:
            cp.wait_recv()

    outs = pl.pallas_call(
        body, name=name,
        out_shape=tuple(pltpu.HBM(s.shape, s.dtype) for s in bufs),
        in_specs=[_HBM] * nb + [_SEM, _SEM] + [pl.BlockSpec(memory_space=pl.ANY)] * len(after),
        out_specs=tuple([_HBM] * nb),
        input_output_aliases={k: k for k in range(nb)},
        compiler_params=pltpu.CompilerParams(has_side_effects=_EFFECT),
    )(*bufs, send_sems, recv_sems, *after)
    return list(outs[:n]), list(outs[n:])


def _fill_own_slot(gathered, shards, pidx, names):
    outs = []
    for g, s, name in zip(gathered, shards, names):
        r, cdim = s.shape
        tr = _row_tile(r)

        def body(p_ref, s_ref, g_ref, o_ref):
            o_ref[...] = s_ref[...]

        outs.append(pl.pallas_call(
            body, name=name,
            grid_spec=pltpu.PrefetchScalarGridSpec(
                num_scalar_prefetch=1, grid=(r // tr,),
                in_specs=[pl.BlockSpec((tr, cdim), lambda i, p: (i, 0)), pl.BlockSpec(memory_space=pl.ANY)],
                out_specs=pl.BlockSpec((None, tr, cdim), lambda i, p: (p[0], i, 0))),
            out_shape=jax.ShapeDtypeStruct(g.shape, g.dtype),
            input_output_aliases={2: 0},
            compiler_params=_cp(("arbitrary",)),
        )(pidx, s, g))
    return outs


def _sum_own_and_peers(own4, slots, pidx, name):
    _, rh, cdim = own4.shape
    tr = _row_tile(rh)

    def body(p_ref, own_ref, s_ref, o_ref):
        acc = own_ref[...].astype(F32)
        for j in range(3):
            acc = acc + s_ref[j].astype(F32)
        o_ref[...] = acc

    return pl.pallas_call(
        body, name=name,
        grid_spec=pltpu.PrefetchScalarGridSpec(
            num_scalar_prefetch=1, grid=(rh // tr,),
            in_specs=[pl.BlockSpec((None, tr, cdim), lambda i, p: (p[0], i, 0)),
                      pl.BlockSpec((3, tr, cdim), lambda i, p: (0, i, 0))],
            out_specs=pl.BlockSpec((tr, cdim), lambda i, p: (i, 0))),
        out_shape=jax.ShapeDtypeStruct((rh, cdim), F32),
        compiler_params=_cp(("arbitrary",)),
    )(pidx, own4, slots)


def _exchange(ins, land_shapes, copies, n_sems, name):
    n, nl = len(ins), len(land_shapes)

    def body(*refs):
        sends, recvs = copies(refs[:n], refs[n:n + nl], refs[n + nl], refs[n + nl + 1])
        for cp in sends:
            cp.start()
        for cp in sends:
            cp.wait_send()
        for cp in recvs:
            cp.wait_recv()

    any_spec = pl.BlockSpec(memory_space=pl.ANY)
    return pl.pallas_call(
        body, name=name,
        out_shape=[jax.ShapeDtypeStruct(s.shape, s.dtype) for s in land_shapes],
        in_specs=[any_spec] * n, out_specs=[any_spec] * nl,
        scratch_shapes=[pltpu.SemaphoreType.DMA((n_sems,)), pltpu.SemaphoreType.DMA((n_sems,))],
    )(*ins)


def _row_tile(r):
    t = min(TR_EW, r)
    while r % t:
        t //= 2
    return t


def _add_own_half(g4, recv, cidx, name):
    _, _, rh, cdim = g4.shape
    tr = _row_tile(rh)

    def body(c_ref, a_ref, b_ref, o_ref):
        o_ref[...] = (a_ref[...] + b_ref[...]).astype(BF16)

    return pl.pallas_call(
        body, name=name,
        grid_spec=pltpu.PrefetchScalarGridSpec(
            num_scalar_prefetch=1, grid=(N_CHIPS, rh // tr),
            in_specs=[pl.BlockSpec((None, None, tr, cdim), lambda q, i, c: (q, c[0], i, 0)),
                      pl.BlockSpec((None, tr, cdim), lambda q, i, c: (q, i, 0))],
            out_specs=pl.BlockSpec((None, tr, cdim), lambda q, i, c: (q, i, 0))),
        out_shape=jax.ShapeDtypeStruct(recv.shape, BF16),
        compiler_params=_cp(("arbitrary", "arbitrary")),
    )(cidx, g4, recv)


def _sum_slots(v, name):
    n, r, cdim = v.shape
    tr = _row_tile(r)

    def body(v_ref, o_ref):
        acc = v_ref[0].astype(F32)
        for k in range(1, n):
            acc = acc + v_ref[k].astype(F32)
        o_ref[...] = acc

    return pl.pallas_call(
        body, name=name, grid=(r // tr,),
        in_specs=[pl.BlockSpec((n, tr, cdim), lambda i: (0, i, 0))],
        out_specs=pl.BlockSpec((tr, cdim), lambda i: (i, 0)),
        out_shape=jax.ShapeDtypeStruct((r, cdim), F32),
        compiler_params=_cp(("arbitrary",)),
    )(v)


def _sum_devices(lands, own, didx, name):
    _, r, cdim = lands.shape
    tr = _row_tile(r)

    def body(d_ref, l_ref, own_ref, o_ref):
        acc = jnp.where(d_ref[0] == 0, own_ref[...], l_ref[0])
        for dv in range(1, N_DEV):
            acc = acc + jnp.where(d_ref[0] == dv, own_ref[...], l_ref[dv])
        o_ref[...] = acc

    return pl.pallas_call(
        body, name=name,
        grid_spec=pltpu.PrefetchScalarGridSpec(
            num_scalar_prefetch=1, grid=(r // tr,),
            in_specs=[pl.BlockSpec((N_DEV, tr, cdim), lambda i, dd: (0, i, 0)), pl.BlockSpec((tr, cdim), lambda i, dd: (i, 0))],
            out_specs=pl.BlockSpec((tr, cdim), lambda i, dd: (i, 0))),
        out_shape=jax.ShapeDtypeStruct((r, cdim), F32),
        compiler_params=_cp(("arbitrary",)),
    )(didx, lands, own)


def _adamw_math(wv, gg, mv, vv):
    nm = ADAM_B1 * mv + (1.0 - ADAM_B1) * gg
    nv = ADAM_B2 * vv + (1.0 - ADAM_B2) * (gg * gg)
    m_hat = nm / (1.0 - ADAM_B1 ** ADAM_STEP)
    v_hat = nv / (1.0 - ADAM_B2 ** ADAM_STEP)
    return -ADAM_LR * (m_hat / (jnp.sqrt(v_hat) + ADAM_EPS) + ADAM_WD * wv), nm, nv


def _adamw_halves(w, mine, theirs, m, v, cidx, name):
    r, cdim = w.shape
    rh = r // 2
    tr = _row_tile(rh)
    nblk = rh // tr

    def body(c_ref, w_ref, a_ref, b_ref, m_ref, v_ref, g_ref, d_ref, nm_ref, nv_ref):
        gg = jnp.where(pl.program_id(0) == c_ref[0], a_ref[...], b_ref[...])
        g_ref[...] = gg
        d_ref[...], nm_ref[...], nv_ref[...] = _adamw_math(w_ref[...], gg, m_ref[...], v_ref[...])

    full = pl.BlockSpec((tr, cdim), lambda hh, i, c: (hh * nblk + i, 0))
    half = pl.BlockSpec((tr, cdim), lambda hh, i, c: (i, 0))
    return pl.pallas_call(
        body, name=name,
        grid_spec=pltpu.PrefetchScalarGridSpec(
            num_scalar_prefetch=1, grid=(2, nblk),
            in_specs=[full, half, half, full, full], out_specs=[full] * 4),
        out_shape=[jax.ShapeDtypeStruct((r, cdim), F32)] * 4,
        compiler_params=_cp(("arbitrary", "arbitrary")),
    )(cidx, w, mine, theirs, m, v)


def _adamw_many(params, name):
    n = len(params)

    def body(*refs):
        ins, outs = refs[:4 * n], refs[4 * n:]
        for k in range(n):
            w_ref, g_ref, m_ref, v_ref = ins[4 * k:4 * k + 4]
            outs[3 * k][...], outs[3 * k + 1][...], outs[3 * k + 2][...] = _adamw_math(
                w_ref[...], g_ref[...], m_ref[...], v_ref[...])

    flat = [a for p in params for a in p]
    res = pl.pallas_call(
        body, name=name,
        out_shape=[jax.ShapeDtypeStruct(p[0].shape, F32) for p in params for _ in range(3)],
        compiler_params=pltpu.CompilerParams(vmem_limit_bytes=VMEM_LIMIT),
    )(*flat)
    return [res[3 * k:3 * k + 3] for k in range(n)]


def _adamw(w, g, m, v, name):
    r, cdim = w.shape
    tr = _row_tile(r) if r % 8 == 0 else r

    def body(w_ref, g_ref, m_ref, v_ref, d_ref, nm_ref, nv_ref):
        d_ref[...], nm_ref[...], nv_ref[...] = _adamw_math(w_ref[...], g_ref[...], m_ref[...], v_ref[...])

    spec = pl.BlockSpec((tr, cdim), lambda i: (i, 0))
    return pl.pallas_call(
        body, name=name, grid=(r // tr,), in_specs=[spec] * 4, out_specs=[spec] * 3,
        out_shape=[jax.ShapeDtypeStruct((r, cdim), F32)] * 3,
        compiler_params=_cp(("arbitrary",)),
    )(w, g, m, v)


def _ada_fwd(c_all, w_ada, b_cols):
    nb, _ = c_all.shape
    n = w_ada.shape[1]

    def body(c_ref, w_ref, b_ref, o_ref):
        cv = c_ref[...]
        o_ref[...] = _mm(cv * _sigmoid(cv), w_ref[...]) + b_ref[...]

    return pl.pallas_call(
        body, name="ada_fwd", out_shape=jax.ShapeDtypeStruct((nb, n), F32),
        compiler_params=pltpu.CompilerParams(vmem_limit_bytes=VMEM_LIMIT),
    )(c_all, w_ada, b_cols)


def _ada_bwd(c_all, dmod_all, dmod_cols):
    d = c_all.shape[1]
    n = dmod_cols.shape[1]

    def body(c_ref, da_ref, dc_ref, gw_ref, gb_ref):
        cv = c_ref[...]
        gw_ref[...] = _mm_tn(cv * _sigmoid(cv), dc_ref[...])
        gb_ref[...] = _colsum(da_ref[...])

    return pl.pallas_call(
        body, name="ada_bwd",
        out_shape=[jax.ShapeDtypeStruct((d, n), F32), jax.ShapeDtypeStruct((1, dmod_all.shape[1]), F32)],
        compiler_params=pltpu.CompilerParams(vmem_limit_bytes=VMEM_LIMIT),
    )(c_all, dmod_all, dmod_cols)


def _proj_fwd(x2, modv, ws, cols, b_in, seq, name, proj_in=None):
    t, d = x2.shape
    n = len(ws)
    ns = ws[0].shape[1]
    tm = min(TM_PROJ, seq)
    tpb = seq // tm
    first = proj_in is None

    def body(c_ref, x_ref, mod_ref, *refs):
        w_refs, b_ref = refs[:n], refs[n]
        outs = refs[n + 1 if first else n + 2:]
        proj_ref, h_s = outs[0], outs[-1]
        s = pl.program_id(1)

        @pl.when(s == 0)
        def _():
            h = (x_ref[...] * (1.0 + mod_ref[1:2, :]) + mod_ref[0:1, :]).astype(BF16)
            h_s[...] = h
            if first:
                outs[1][...] = h

        for k in range(n):
            @pl.when(s == k)
            def _():
                proj_ref[...] = (jnp.dot(h_s[...], w_refs[k][...], preferred_element_type=F32) + b_ref[...]).astype(BF16)

    in_specs = [pl.BlockSpec((tm, d), lambda i, s, c: (i, 0)),
                pl.BlockSpec((None, 8, d), lambda i, s, c: (i // tpb, 0, 0))]
    in_specs += [pl.BlockSpec((d, ns), lambda i, s, c: (0, 0))] * n
    in_specs += [pl.BlockSpec((1, ns), lambda i, s, c: (0, c[s]))]
    out_specs = [pl.BlockSpec((tm, ns), lambda i, s, c: (i, c[s]))]
    out_shape = [jax.ShapeDtypeStruct((t, N_CHIPS * ns), BF16)]
    args = [cols, x2, modv, *ws, b_in]
    aliases = {}
    if first:
        out_specs.append(pl.BlockSpec((tm, d), lambda i, s, c: (i, 0)))
        out_shape.append(jax.ShapeDtypeStruct((t, d), BF16))
    else:
        in_specs.append(_ANY)
        args.append(proj_in)
        aliases = {len(args) - 1: 0}
    return pl.pallas_call(
        body, name=name,
        grid_spec=pltpu.PrefetchScalarGridSpec(
            num_scalar_prefetch=1, grid=(t // tm, n), in_specs=in_specs, out_specs=out_specs,
            scratch_shapes=[pltpu.VMEM((tm, d), BF16)]),
        out_shape=out_shape, input_output_aliases=aliases,
        compiler_params=_cp(("arbitrary", "arbitrary")),
    )(*args)


def _lru_gates(xl, wc_ref, bc_ref, wa_ref, ba_ref, wx_ref, bx_ref, lam_ref):
    xc = bc_ref[...] + wc_ref[CONV_WIDTH - 1:CONV_WIDTH, :] * xl
    for k in range(CONV_WIDTH - 1):
        xc = xc + wc_ref[k:k + 1, :] * _shift_down(xl, CONV_WIDTH - 1 - k)
    r = _sigmoid(_mm(xc, wa_ref[...]) + ba_ref[...])
    gi = _sigmoid_t(_mm(xc, wx_ref[...]) + bx_ref[...])
    nl = -lam_ref[...]
    e = jnp.exp(-jnp.abs(nl))
    u = 1.0 + e
    dlt = u - 1.0
    log1p_e = jnp.where(dlt == 0.0, e, jnp.log(u) * (e / jnp.where(dlt == 0.0, 1.0, dlt)))
    big_l = -LRU_C * (jnp.maximum(nl, 0.0) + log1p_e)
    la = big_l * r
    a = jnp.exp(la)
    m2 = jnp.tanh(-la) * (a * a + 1.0)
    return xc, r, gi, big_l, a, m2


def _lru_prep(proj, lru_w, nb, seq):
    t = proj.shape[0]
    w = LRU_HEADS * HEAD
    w_conv, b_conv, w_a, b_a, w_x, b_x, lam = lru_w

    def body(x_ref, wc_ref, bc_ref, wa_ref, ba_ref, wx_ref, bx_ref, lam_ref, a_ref, inp_ref):
        xc, r, gi, big_l, a, m2 = _lru_gates(x_ref[...].astype(F32), wc_ref, bc_ref, wa_ref, ba_ref, wx_ref, bx_ref, lam_ref)
        a_ref[...] = a
        inp_ref[...] = jnp.sqrt(m2) * (gi * xc)

    col = lambda b, hd: (0, hd)
    head = lambda b, hd: (hd, 0, 0)
    tok = lambda b, hd: (b, hd)
    return pl.pallas_call(
        body, name="lru_prep", grid=(nb, LRU_HEADS),
        in_specs=[pl.BlockSpec((seq, HEAD), tok),
                  pl.BlockSpec((CONV_WIDTH, HEAD), col), pl.BlockSpec((1, HEAD), col),
                  pl.BlockSpec((None, HEAD, HEAD), head), pl.BlockSpec((1, HEAD), col),
                  pl.BlockSpec((None, HEAD, HEAD), head), pl.BlockSpec((1, HEAD), col),
                  pl.BlockSpec((1, HEAD), col)],
        out_specs=[pl.BlockSpec((seq, HEAD), tok)] * 2,
        out_shape=[jax.ShapeDtypeStruct((t, w), F32)] * 2,
        compiler_params=_cp(("arbitrary", "arbitrary")),
    )(proj, w_conv, b_conv, w_a, b_a, w_x, b_x, lam)


def _scan(a3, b3, reverse, name):
    nb, seq, w = a3.shape
    tc = min(TC_SCAN, seq)
    nchunk = seq // tc
    ntile = tc // 8

    def combine(av, bv):
        rows = lax.broadcasted_iota(jnp.int32, av.shape, 0)
        for s in (1, 2, 4):
            if reverse:
                keep = rows < 8 - s
                a_sh, b_sh = pltpu.roll(av, 8 - s, 0), pltpu.roll(bv, 8 - s, 0)
            else:
                keep = rows >= s
                a_sh, b_sh = pltpu.roll(av, s, 0), pltpu.roll(bv, s, 0)
            bv = jnp.where(keep, bv + av * b_sh, bv)
            av = jnp.where(keep, av * a_sh, av)
        return av, bv

    def body(a_ref, b_ref, h_ref, carry):
        @pl.when(pl.program_id(0) == 0)
        def _():
            carry[...] = jnp.zeros_like(carry)

        for b in range(nb):
            def tile(j, hprev):
                jj = ntile - 1 - j if reverse else j
                base = pl.multiple_of(jj * 8, 8)
                av, bv = a_ref[b, pl.ds(base, 8), :], b_ref[b, pl.ds(base, 8), :]
                av, bv = combine(av, av * bv if reverse else bv)
                h = bv + av * hprev
                h_ref[b, pl.ds(base, 8), :] = h
                edge = h[0:1, :] if reverse else h[7:8, :]
                return jnp.broadcast_to(edge, (8, w))

            carry[b] = lax.fori_loop(0, ntile, tile, carry[b])

    imap = (lambda i: (0, nchunk - 1 - i, 0)) if reverse else (lambda i: (0, i, 0))
    spec = pl.BlockSpec((nb, tc, w), imap)
    return pl.pallas_call(
        body, name=name, grid=(nchunk,), in_specs=[spec, spec], out_specs=spec,
        out_shape=jax.ShapeDtypeStruct((nb, seq, w), F32),
        scratch_shapes=[pltpu.VMEM((nb, 8, w), F32)],
        compiler_params=_cp(("arbitrary",)),
    )(a3, b3)


def _sgu_mask():
    ti = lax.broadcasted_iota(jnp.int32, (HEAD, HEAD), 0) // SGU_CHUNK
    si = lax.broadcasted_iota(jnp.int32, (HEAD, HEAD), 1) // SGU_CHUNK
    return si <= ti


def _sgu_specs(tm, d_sgu):
    pw = 256
    first_u = (2 * LRU_HEADS * HEAD) // pw
    n_piece = d_sgu // pw
    specs = [pl.BlockSpec((tm, pw), functools.partial(lambda i, k: (i, k), k=first_u + j)) for j in range(2 * n_piece)]
    return specs, n_piece


def _sgu_fwd(proj, w_sp, b_sp_t, ln_g, ln_b):
    t = proj.shape[0]
    d_sgu = SGU_GROUPS * HEAD
    tm = min(TM_SGU, t)
    nblk = tm // HEAD
    specs, n_piece = _sgu_specs(tm, d_sgu)

    def body(*refs):
        u = jnp.concatenate([r[...] for r in refs[:n_piece]], axis=1).astype(F32)
        v = jnp.concatenate([r[...] for r in refs[n_piece:2 * n_piece]], axis=1).astype(F32)
        w_ref, bt_ref, g_ref, b_ref, y_ref = refs[2 * n_piece:]
        ug = _gelu(u)
        xhat, _ = _ln_stats(_gelu(v))
        vn = (xhat * g_ref[...] + b_ref[...]).astype(BF16)
        mask = _sgu_mask()
        for g in range(SGU_GROUPS):
            wm = jnp.where(mask, w_ref[g], 0.0).astype(BF16)
            cols = slice(g * HEAD, (g + 1) * HEAD)
            for n in range(nblk):
                rows = slice(n * HEAD, (n + 1) * HEAD)
                mixed = jnp.dot(wm, vn[rows, cols], preferred_element_type=F32) + bt_ref[:, g:g + 1]
                y_ref[rows, cols] = (ug[rows, cols] * mixed).astype(BF16)

    full = lambda shape: pl.BlockSpec(shape, lambda i: (0,) * len(shape))
    return pl.pallas_call(
        body, name="sgu_fwd", grid=(t // tm,),
        in_specs=specs + [full(w_sp.shape), full(b_sp_t.shape), full(ln_g.shape), full(ln_b.shape)],
        out_specs=pl.BlockSpec((tm, d_sgu), lambda i: (i, 0)),
        out_shape=jax.ShapeDtypeStruct((t, d_sgu), BF16),
        compiler_params=_cp(("arbitrary",)),
    )(*([proj] * (2 * n_piece)), w_sp, b_sp_t, ln_g, ln_b)


def _mix_fwd(hs, proj, y_sgu, x2, modv, w_o_lru_g, w_o_sgu_g, w_out_g, ln1_g, ln1_b, seq):
    t, d = x2.shape
    w = hs.shape[1]
    d_sgu = y_sgu.shape[1]
    nq, _, ns = w_o_sgu_g.shape
    tm = min(TM_MIX, seq)
    tpb = seq // tm

    def body(hs_ref, gl_ref, ys_ref, ga_ref, gb_ref, x_ref, mod_ref, wl_ref, ws_ref, wo_ref, g1_ref, b1_ref,
             yap_ref, ya_ref, yb_ref, mg_ref, mix_ref, x1_ref):
        yap = (hs_ref[...] * _gelu(gl_ref[...].astype(F32))).astype(BF16)
        yap_ref[...] = yap
        y_a = jnp.dot(yap, wl_ref[...], preferred_element_type=F32)
        ys = ys_ref[...]
        y_b = jnp.concatenate([jnp.dot(ys, ws_ref[q], preferred_element_type=F32) for q in range(nq)], axis=1)
        ya_ref[...] = y_a.astype(BF16)
        yb_ref[...] = y_b.astype(BF16)
        merged = (_sigmoid_t(ga_ref[...].astype(F32)) * y_a + _sigmoid_t(gb_ref[...].astype(F32)) * y_b).astype(BF16)
        mg_ref[...] = merged
        mix = jnp.dot(merged, wo_ref[...], preferred_element_type=F32)
        mix_ref[...] = mix
        xhat, _ = _ln_stats(ALPHA * x_ref[...] + (1.0 + mod_ref[2:3, :]) * mix)
        x1_ref[...] = xhat * g1_ref[...] + b1_ref[...]

    row = lambda width, col: pl.BlockSpec((tm, width), functools.partial(lambda i, k: (i, k), k=col))
    full = lambda shape: pl.BlockSpec(shape, lambda i: (0,) * len(shape))
    return pl.pallas_call(
        body, name="mix_fwd", grid=(t // tm,),
        in_specs=[row(w, 0), row(w, 1), row(d_sgu, 0), row(d, 4), row(d, 5), row(d, 0),
                  pl.BlockSpec((None, 8, d), lambda i: (i // tpb, 0, 0)),
                  full(w_o_lru_g.shape), full(w_o_sgu_g.shape), full(w_out_g.shape), full(ln1_g.shape), full(ln1_b.shape)],
        out_specs=[row(w, 0), row(d, 0), row(d, 0), row(d, 0), row(d, 0), row(d, 0)],
        out_shape=[jax.ShapeDtypeStruct((t, w), BF16), jax.ShapeDtypeStruct((t, d), BF16),
                   jax.ShapeDtypeStruct((t, d), BF16), jax.ShapeDtypeStruct((t, d), BF16),
                   jax.ShapeDtypeStruct((t, d), F32), jax.ShapeDtypeStruct((t, d), F32)],
        compiler_params=_cp(("arbitrary",)),
    )(hs, proj, y_sgu, proj, proj, x2, modv, w_o_lru_g, w_o_sgu_g, w_out_g, ln1_g, ln1_b)


def _mlp_fwd(x1, modv, w_up_g, w_down_g, ln2_g, ln2_b, target, nb, seq):
    t, d = x1.shape
    nq, _, ns = w_up_g.shape
    tm = min(TM_MLP, seq)
    ts = min(TS_MLP, tm)
    tpb = seq // tm

    def body(x1_ref, mod_ref, wu_hbm, wd_hbm, g2_ref, b2_ref, tg_ref,
             rl_ref, act_ref, h2_ref, dz2_ref, df_ref, st_ref, pb_ref, wu_s, wd_s, acc, sems):
        i = pl.program_id(0)

        @pl.when(i == 0)
        def _():
            _load_weights((wu_hbm, wd_hbm), (wu_s, wd_s), sems)
            st_ref[...] = jnp.zeros_like(st_ref)

        @pl.when(i % tpb == 0)
        def _():
            pb_ref[...] = jnp.zeros_like(pb_ref)

        for sub in range(tm // ts):
            rows = slice(sub * ts, (sub + 1) * ts)
            x1v = x1_ref[rows, :]
            h2 = (x1v * (1.0 + mod_ref[4:5, :]) + mod_ref[3:4, :]).astype(BF16)
            h2_ref[rows, :] = h2
            for k in range(nq):
                cols = slice(k * ns, (k + 1) * ns)
                r = jnp.maximum(jnp.dot(h2, wu_s[k], preferred_element_type=F32), 0.0)
                act = (r * r).astype(BF16)
                rl_ref[rows, cols] = r.astype(BF16)
                act_ref[rows, cols] = act
                part = jnp.dot(act, wd_s[cols, :], preferred_element_type=F32)
                if k == 0:
                    acc[sub] = part
                else:
                    acc[sub] += part
            f = acc[sub]
            xhat, rstd = _ln_stats(ALPHA * x1v + (1.0 + mod_ref[5:6, :]) * f)
            y = xhat * g2_ref[...] + b2_ref[...]
            err = y - tg_ref[rows, :]
            dy = err * (1.0 / d)
            dz2 = _ln_bwd(dy * g2_ref[...], xhat, rstd)
            dz2_ref[rows, :] = dz2
            df_ref[rows, :] = ((1.0 + mod_ref[5:6, :]) * dz2).astype(BF16)
            st_ref[0:1, :] += _colsum(dy * xhat)
            st_ref[1:2, :] += _colsum(dy)
            st_ref[2:3, :] += (0.5 / d) * jnp.sum(_colsum(err * err), axis=1, keepdims=True)
            pb_ref[0:1, :] += _colsum(dz2 * f)

    tok = lambda i: (i, 0)
    return pl.pallas_call(
        body, name="mlp_fwd", grid=(t // tm,),
        in_specs=[pl.BlockSpec((tm, d), tok), pl.BlockSpec((None, 8, d), lambda i: (i // tpb, 0, 0)), _ANY, _ANY,
                  pl.BlockSpec((1, d), lambda i: (0, 0)), pl.BlockSpec((1, d), lambda i: (0, 0)),
                  pl.BlockSpec((tm, d), tok)],
        out_specs=[pl.BlockSpec((tm, nq * ns), tok), pl.BlockSpec((tm, nq * ns), tok),
                   pl.BlockSpec((tm, d), tok), pl.BlockSpec((tm, d), tok), pl.BlockSpec((tm, d), tok),
                   pl.BlockSpec((8, d), lambda i: (0, 0)), pl.BlockSpec((None, 8, d), lambda i: (i // tpb, 0, 0))],
        out_shape=[jax.ShapeDtypeStruct((t, nq * ns), BF16), jax.ShapeDtypeStruct((t, nq * ns), BF16),
                   jax.ShapeDtypeStruct((t, d), BF16),
                   jax.ShapeDtypeStruct((t, d), F32), jax.ShapeDtypeStruct((t, d), BF16),
                   jax.ShapeDtypeStruct((8, d), F32), jax.ShapeDtypeStruct((nb, 8, d), F32)],
        scratch_shapes=[pltpu.VMEM(w_up_g.shape, BF16), pltpu.VMEM(w_down_g.shape, BF16),
                        pltpu.VMEM((tm // ts, ts, d), F32), pltpu.SemaphoreType.DMA((2,))],
        compiler_params=_cp(("arbitrary",)),
    )(x1, modv, w_up_g, w_down_g, ln2_g, ln2_b, target)


def _mlp_bwd(df, up, w_down_g, w_up_g, dz2, x2, mix, modv, ln1_g, ln1_b, nb, seq):
    t, d = x2.shape
    nq, _, ns = w_up_g.shape
    tm = min(TM_MLP, seq)
    ts = min(TS_MLP, tm)
    tpb = seq // tm

    def body(df_ref, rl_ref, wd_hbm, wu_hbm, dz2_ref, x_ref, mix_ref, mod_ref, g1_ref, b1_ref,
             dup_ref, dz1_ref, dmix_ref, st_ref, pb_ref, wd_s, wu_s, acc, sems):
        i = pl.program_id(0)

        @pl.when(i == 0)
        def _():
            _load_weights((wd_hbm, wu_hbm), (wd_s, wu_s), sems)
            st_ref[...] = jnp.zeros_like(st_ref)

        @pl.when(i % tpb == 0)
        def _():
            pb_ref[...] = jnp.zeros_like(pb_ref)

        for sub in range(tm // ts):
            rows = slice(sub * ts, (sub + 1) * ts)
            dfv = df_ref[rows, :]
            for k in range(nq):
                cols = slice(k * ns, (k + 1) * ns)
                dup = (_mm_nt(dfv, wd_s[cols, :]) * (2.0 * rl_ref[rows, cols].astype(F32))).astype(BF16)
                dup_ref[rows, cols] = dup
                part = _mm_nt(dup, wu_s[k])
                if k == 0:
                    acc[sub] = part
                else:
                    acc[sub] += part
            dh2 = acc[sub]
            mix = mix_ref[rows, :]
            xhat, rstd = _ln_stats(ALPHA * x_ref[rows, :] + (1.0 + mod_ref[2:3, :]) * mix)
            x1 = xhat * g1_ref[...] + b1_ref[...]
            dx1 = ALPHA * dz2_ref[rows, :] + dh2 * (1.0 + mod_ref[4:5, :])
            dz1 = _ln_bwd(dx1 * g1_ref[...], xhat, rstd)
            dz1_ref[rows, :] = dz1
            dmix_ref[rows, :] = ((1.0 + mod_ref[2:3, :]) * dz1).astype(BF16)
            st_ref[0:1, :] += _colsum(dx1 * xhat)
            st_ref[1:2, :] += _colsum(dx1)
            pb_ref[0:1, :] += _colsum(dh2 * x1)
            pb_ref[1:2, :] += _colsum(dh2)
            pb_ref[2:3, :] += _colsum(dz1 * mix)

    tok = lambda i: (i, 0)
    return pl.pallas_call(
        body, name="mlp_bwd", grid=(t // tm,),
        in_specs=[pl.BlockSpec((tm, d), tok), pl.BlockSpec((tm, nq * ns), tok), _ANY, _ANY,
                  pl.BlockSpec((tm, d), tok), pl.BlockSpec((tm, d), tok), pl.BlockSpec((tm, d), tok),
                  pl.BlockSpec((None, 8, d), lambda i: (i // tpb, 0, 0)),
                  pl.BlockSpec((1, d), lambda i: (0, 0)), pl.BlockSpec((1, d), lambda i: (0, 0))],
        out_specs=[pl.BlockSpec((tm, nq * ns), tok),
                   pl.BlockSpec((tm, d), tok), pl.BlockSpec((tm, d), tok),
                   pl.BlockSpec((8, d), lambda i: (0, 0)), pl.BlockSpec((None, 8, d), lambda i: (i // tpb, 0, 0))],
        out_shape=[jax.ShapeDtypeStruct((t, nq * ns), BF16),
                   jax.ShapeDtypeStruct((t, d), F32), jax.ShapeDtypeStruct((t, d), BF16),
                   jax.ShapeDtypeStruct((8, d), F32), jax.ShapeDtypeStruct((nb, 8, d), F32)],
        scratch_shapes=[pltpu.VMEM(w_down_g.shape, BF16), pltpu.VMEM(w_up_g.shape, BF16),
                        pltpu.VMEM((tm // ts, ts, d), F32), pltpu.SemaphoreType.DMA((2,))],
        compiler_params=_cp(("arbitrary",), VMEM_LIMIT_MAX),
    )(df, up, w_down_g, w_up_g, dz2, x2, mix, modv, ln1_g, ln1_b)


def _mix_bwd(dmix, proj, y_a, y_b, hs, w_out_g, w_o_lru_g, w_o_sgu_g, seq, after=()):
    t, d = dmix.shape
    w = hs.shape[1]
    nq, d_sgu, ns = w_o_sgu_g.shape
    tm = min(TM_MIX, seq)

    def body(dmix_ref, ga_ref, gb_ref, ya_ref, yb_ref, gl_ref, hs_ref, wo_ref, wl_ref, ws_ref,
             dya_ref, dyb_ref, dga_ref, dgb_ref, dgl_ref, dyl_ref, dys_ref):
        dmerged = _mm_nt(dmix_ref[...], wo_ref[...])
        sa, sb = _sigmoid_t(ga_ref[...].astype(F32)), _sigmoid_t(gb_ref[...].astype(F32))
        dy_a = (dmerged * sa).astype(BF16)
        dy_b = (dmerged * sb).astype(BF16)
        dya_ref[...] = dy_a
        dyb_ref[...] = dy_b
        dga_ref[...] = (dmerged * ya_ref[...].astype(F32) * (sa * (1.0 - sa))).astype(BF16)
        dgb_ref[...] = (dmerged * yb_ref[...].astype(F32) * (sb * (1.0 - sb))).astype(BF16)
        dyap = _mm_nt(dy_a, wl_ref[...])
        gel, dgel = _gelu_and_grad(gl_ref[...].astype(F32))
        dyl_ref[...] = dyap * gel
        dgl_ref[...] = (dyap * hs_ref[...] * dgel).astype(BF16)
        dys = _mm_nt(dy_b[:, 0:ns], ws_ref[0])
        for q in range(1, nq):
            dys = dys + _mm_nt(dy_b[:, q * ns:(q + 1) * ns], ws_ref[q])
        dys_ref[...] = dys

    row = lambda width, col: pl.BlockSpec((tm, width), functools.partial(lambda i, k: (i, k), k=col))
    full = lambda shape: pl.BlockSpec(shape, lambda i: (0,) * len(shape))
    return pl.pallas_call(
        _ordered(body, 10, after), name="mix_bwd", grid=(t // tm,),
        in_specs=[row(d, 0), row(d, 4), row(d, 5), row(d, 0), row(d, 0), row(w, 1), row(w, 0),
                  full(w_out_g.shape), full(w_o_lru_g.shape), full(w_o_sgu_g.shape)] + [_ANY] * len(after),
        out_specs=[row(d, 0), row(d, 0), row(d, 0), row(d, 0), row(w, 0), row(w, 0), row(d_sgu, 0)],
        out_shape=[jax.ShapeDtypeStruct((t, d), BF16), jax.ShapeDtypeStruct((t, d), BF16),
                   jax.ShapeDtypeStruct((t, d), BF16), jax.ShapeDtypeStruct((t, d), BF16),
                   jax.ShapeDtypeStruct((t, w), BF16), jax.ShapeDtypeStruct((t, w), F32),
                   jax.ShapeDtypeStruct((t, d_sgu), F32)],
        compiler_params=_cp(("arbitrary",)),
    )(dmix, proj, proj, y_a, y_b, proj, hs, w_out_g, w_o_lru_g, w_o_sgu_g, *after)


def _sgu_bwd(proj, dys, w_sp, b_sp_t, ln_g, ln_b, after=()):
    t = proj.shape[0]
    d_sgu = SGU_GROUPS * HEAD
    tm = min(TM_SGU, t)
    nblk = tm // HEAD
    specs, n_piece = _sgu_specs(tm, d_sgu)

    def body(*refs):
        u = jnp.concatenate([r[...] for r in refs[:n_piece]], axis=1).astype(F32)
        v = jnp.concatenate([r[...] for r in refs[n_piece:2 * n_piece]], axis=1).astype(F32)
        dys_ref, w_ref, bt_ref, g_ref, b_ref, du_ref, dv_ref, dw_ref, st_ref, dbt_ref, dvn_s = refs[2 * n_piece:]

        @pl.when(pl.program_id(0) == 0)
        def _():
            dw_ref[...] = jnp.zeros_like(dw_ref)
            st_ref[...] = jnp.zeros_like(st_ref)
            dbt_ref[...] = jnp.zeros_like(dbt_ref)

        ug, dug_du = _gelu_and_grad(u)
        vg, dvg_dv = _gelu_and_grad(v)
        xhat, rstd = _ln_stats(vg)
        vn = (xhat * g_ref[...] + b_ref[...]).astype(BF16)
        dys_v = dys_ref[...]
        mask = _sgu_mask()
        for g in range(SGU_GROUPS):
            wm = jnp.where(mask, w_ref[g], 0.0).astype(BF16)
            cols = slice(g * HEAD, (g + 1) * HEAD)
            dw_g = jnp.zeros((HEAD, HEAD), F32)
            db_g = jnp.zeros((HEAD, 1), F32)
            for n in range(nblk):
                rows = slice(n * HEAD, (n + 1) * HEAD)
                vn_blk = vn[rows, cols]
                mixed = jnp.dot(wm, vn_blk, preferred_element_type=F32) + bt_ref[:, g:g + 1]
                dy_blk = dys_v[rows, cols]
                du_ref[rows, cols] = (dy_blk * mixed * dug_du[rows, cols]).astype(BF16)
                dmx = dy_blk * ug[rows, cols]
                dvn_s[rows, cols] = _mm_tn(wm, dmx)
                dw_g = dw_g + _mm_nt(dmx, vn_blk)
                db_g = db_g + jnp.sum(dmx, axis=1, keepdims=True)
            dw_ref[g] += jnp.where(mask, dw_g, 0.0)
            dbt_ref[:, g:g + 1] += db_g
        dvn = dvn_s[...]
        st_ref[0:1, :] += _colsum(dvn * xhat)
        st_ref[1:2, :] += _colsum(dvn)
        dv_ref[...] = (_ln_bwd(dvn * g_ref[...], xhat, rstd) * dvg_dv).astype(BF16)

    full = lambda shape: pl.BlockSpec(shape, lambda i: (0,) * len(shape))
    tok = pl.BlockSpec((tm, d_sgu), lambda i: (i, 0))
    return pl.pallas_call(
        _ordered(body, 2 * n_piece + 5, after), name="sgu_bwd", grid=(t // tm,),
        in_specs=specs + [tok, full(w_sp.shape), full(b_sp_t.shape), full(ln_g.shape), full(ln_b.shape)]
        + [_ANY] * len(after),
        out_specs=[tok, tok, full(w_sp.shape), full((8, d_sgu)), full((HEAD, HEAD))],
        out_shape=[jax.ShapeDtypeStruct((t, d_sgu), BF16), jax.ShapeDtypeStruct((t, d_sgu), BF16),
                   jax.ShapeDtypeStruct(w_sp.shape, F32), jax.ShapeDtypeStruct((8, d_sgu), F32),
                   jax.ShapeDtypeStruct((HEAD, HEAD), F32)],
        scratch_shapes=[pltpu.VMEM((tm, d_sgu), F32)],
        compiler_params=_cp(("arbitrary",)),
    )(*([proj] * (2 * n_piece)), dys, w_sp, b_sp_t, ln_g, ln_b, *after)


def _lru_bwd(proj, hs, e, dyl, lru_w, nb, seq, after=()):
    t = proj.shape[0]
    w = LRU_HEADS * HEAD
    w_conv, b_conv, w_a, b_a, w_x, b_x, lam = lru_w

    def body(x_ref, hs_ref, e_ref, dy_ref, wc_ref, bc_ref, wa_ref, ba_ref, wx_ref, bx_ref, lam_ref,
             dxl_ref, dwa_ref, dwx_ref, st_ref):
        @pl.when(pl.program_id(1) == 0)
        def _():
            dwa_ref[...] = jnp.zeros_like(dwa_ref)
            dwx_ref[...] = jnp.zeros_like(dwx_ref)
            st_ref[...] = jnp.zeros_like(st_ref)

        xl = x_ref[...].astype(F32)
        xc, r, gi, big_l, a, m2 = _lru_gates(xl, wc_ref, bc_ref, wa_ref, ba_ref, wx_ref, bx_ref, lam_ref)
        inv_mult = lax.rsqrt(m2)
        mult = m2 * inv_mult
        dh = dy_ref[...] + _shift_up(e_ref[...], 1)
        da = dh * _shift_down(hs_ref[...], 1)
        dmult = dh * (gi * xc)
        d_i = dh * (mult * xc)
        dxc = dh * (mult * gi)
        dla = a * (da - dmult * (a * inv_mult))
        dr = dla * big_l
        d_big_l = _colsum(dla * r)
        dra = dr * (r * (1.0 - r))
        dia = d_i * (gi * (1.0 - gi))
        dwa_ref[...] += _mm_tn(xc, dra)
        dwx_ref[...] += _mm_tn(xc, dia)
        dxc = dxc + _mm_nt(dra, wa_ref[...]) + _mm_nt(dia, wx_ref[...])
        dxl = wc_ref[CONV_WIDTH - 1:CONV_WIDTH, :] * dxc
        st_ref[4 + CONV_WIDTH - 1:4 + CONV_WIDTH, :] += _colsum(dxc * xl)
        for k in range(CONV_WIDTH - 1):
            ahead = _shift_up(dxc, CONV_WIDTH - 1 - k)
            dxl = dxl + wc_ref[k:k + 1, :] * ahead
            st_ref[4 + k:5 + k, :] += _colsum(ahead * xl)
        dxl_ref[...] = dxl.astype(BF16)
        st_ref[0:1, :] += _colsum(dra)
        st_ref[1:2, :] += _colsum(dia)
        st_ref[2:3, :] += d_big_l * (LRU_C * _sigmoid(-lam_ref[...]))
        st_ref[3:4, :] += _colsum(dxc)

    col = lambda hd, b: (0, hd)
    head = lambda hd, b: (hd, 0, 0)
    tok = lambda hd, b: (b, hd)
    seq_blk = pl.BlockSpec((seq, HEAD), tok)
    return pl.pallas_call(
        _ordered(body, 11, after), name="lru_bwd", grid=(LRU_HEADS, nb),
        in_specs=[seq_blk, seq_blk, seq_blk, seq_blk,
                  pl.BlockSpec((CONV_WIDTH, HEAD), col), pl.BlockSpec((1, HEAD), col),
                  pl.BlockSpec((None, HEAD, HEAD), head), pl.BlockSpec((1, HEAD), col),
                  pl.BlockSpec((None, HEAD, HEAD), head), pl.BlockSpec((1, HEAD), col),
                  pl.BlockSpec((1, HEAD), col)] + [_ANY] * len(after),
        out_specs=[seq_blk, pl.BlockSpec((None, HEAD, HEAD), head), pl.BlockSpec((None, HEAD, HEAD), head),
                   pl.BlockSpec((8, HEAD), col)],
        out_shape=[jax.ShapeDtypeStruct((t, w), BF16), jax.ShapeDtypeStruct((LRU_HEADS, HEAD, HEAD), F32),
                   jax.ShapeDtypeStruct((LRU_HEADS, HEAD, HEAD), F32), jax.ShapeDtypeStruct((8, w), F32)],
        compiler_params=_cp(("arbitrary", "arbitrary")),
    )(proj, hs, e, dyl, w_conv, b_conv, w_a, b_a, w_x, b_x, lam, *after)


def _weight_grad(a, g, col_shards, name, after=()):
    t, k = a.shape
    n = g.shape[1]
    tt = min(TT_DW, t)
    tk = k if k <= 1536 else 1024
    ns = n // N_CHIPS if col_shards else n
    narrow = col_shards and ns < 512
    tn = n if narrow else min(ns, 768 if ns % 768 == 0 else 1024)
    while ns % tn and not narrow:
        tn //= 2
    per = max(ns // tn, 1)

    def body(a_ref, g_ref, o_ref):
        @pl.when(pl.program_id(2) == 0)
        def _():
            o_ref[...] = jnp.zeros_like(o_ref)

        res = _mm_tn(a_ref[...], g_ref[...])
        if narrow:
            for q in range(N_CHIPS):
                o_ref[q] += res[:, q * ns:(q + 1) * ns]
        else:
            o_ref[...] += res

    if narrow:
        out_spec = pl.BlockSpec((N_CHIPS, tk, ns), lambda i, j, s: (0, i, 0))
        out_shape = jax.ShapeDtypeStruct((N_CHIPS, k, ns), F32)
    elif col_shards:
        out_spec = pl.BlockSpec((None, tk, tn), lambda i, j, s: (j // per, i, j % per))
        out_shape = jax.ShapeDtypeStruct((N_CHIPS, k, ns), F32)
    else:
        out_spec = pl.BlockSpec((tk, tn), lambda i, j, s: (i, j))
        out_shape = jax.ShapeDtypeStruct((k, n), F32)
    return pl.pallas_call(
        _ordered(body, 2, after), name=name, grid=(k // tk, n // tn, t // tt),
        in_specs=[pl.BlockSpec((tt, tk), lambda i, j, s: (s, i)), pl.BlockSpec((tt, tn), lambda i, j, s: (s, j))]
        + [_ANY] * len(after),
        out_specs=out_spec, out_shape=out_shape,
        compiler_params=_cp(("arbitrary", "arbitrary", "arbitrary")),
    )(a, g, *after)


def _input_grad(dproj, ws, slots, dz1, x2, modv, nb, seq, after=()):
    t, d = x2.shape
    nq = len(ws)
    ns = ws[0].shape[1]
    tm = min(TM_DH, seq)
    ts = min(TS_MLP, tm)
    tpb = seq // tm

    def body(slot_ref, dp_ref, *refs):
        w_hbm = refs[:nq]
        dz1_ref, x_ref, mod_ref, gx_ref, db_ref, pb_ref, w_s, acc, sems = refs[nq:]
        i = pl.program_id(0)

        @pl.when(i == 0)
        def _():
            _load_weights(w_hbm, [w_s.at[slot_ref[k]] for k in range(nq)], sems)
            db_ref[...] = jnp.zeros_like(db_ref)

        @pl.when(i % tpb == 0)
        def _():
            pb_ref[...] = jnp.zeros_like(pb_ref)

        for sub in range(tm // ts):
            rows = slice(sub * ts, (sub + 1) * ts)
            for q in range(nq):
                dp = dp_ref[rows, q * ns:(q + 1) * ns]
                part = _mm_nt(dp, w_s[q])
                if q == 0:
                    acc[sub] = part
                else:
                    acc[sub] += part
                db_ref[q, 0:1, :] += _colsum(dp.astype(F32))
            dh = acc[sub]
            gx_ref[rows, :] = ALPHA * dz1_ref[rows, :] + dh * (1.0 + mod_ref[1:2, :])
            pb_ref[0:1, :] += _colsum(dh * x_ref[rows, :])
            pb_ref[1:2, :] += _colsum(dh)

    tok = lambda i, s: (i, 0)
    in_specs = [pl.BlockSpec((tm, nq * ns), tok)] + [_ANY] * nq
    in_specs += [pl.BlockSpec((tm, d), tok), pl.BlockSpec((tm, d), tok),
                 pl.BlockSpec((None, 8, d), lambda i, s: (i // tpb, 0, 0))] + [_ANY] * len(after)
    return pl.pallas_call(
        _ordered(body, 5 + nq, after), name="input_grad",
        grid_spec=pltpu.PrefetchScalarGridSpec(
            num_scalar_prefetch=1, grid=(t // tm,), in_specs=in_specs,
            out_specs=[pl.BlockSpec((tm, d), tok), pl.BlockSpec((nq, 8, ns), lambda i, s: (0, 0, 0)),
                       pl.BlockSpec((None, 8, d), lambda i, s: (i // tpb, 0, 0))],
            scratch_shapes=[pltpu.VMEM((nq, d, ns), BF16), pltpu.VMEM((tm // ts, ts, d), F32),
                            pltpu.SemaphoreType.DMA((nq,))]),
        out_shape=[jax.ShapeDtypeStruct((t, d), F32), jax.ShapeDtypeStruct((nq, 8, ns), F32),
                   jax.ShapeDtypeStruct((nb, 8, d), F32)],
        compiler_params=_cp(("arbitrary",)),
    )(slots, dproj, *ws, dz1, x2, modv, *after)


def _rows128(v):
    flat = v.reshape(-1, HEAD)
    pad = (-flat.shape[0]) % 8
    return jnp.pad(flat, ((0, pad), (0, 0))) if pad else flat


def kernel(x, c, w_ada, b_ada, w_in, b_in, w_conv, b_conv, w_rg_a, b_rg_a, w_rg_x, b_rg_x, lru_lambda, w_sp, b_sp, ln_v_g, ln_v_b, w_o_lru, w_o_sgu, w_out, ln1_g, ln1_b, w_up, w_down, ln2_g, ln2_b, loss_target, m_w_ada, m_b_ada, m_w_in, m_b_in, m_w_conv, m_b_conv, m_w_rg_a, m_b_rg_a, m_w_rg_x, m_b_rg_x, m_lru_lambda, m_w_sp, m_b_sp, m_ln_v_g, m_ln_v_b, m_w_o_lru, m_w_o_sgu, m_w_out, m_ln1_g, m_ln1_b, m_w_up, m_w_down, m_ln2_g, m_ln2_b, v_w_ada, v_b_ada, v_w_in, v_b_in, v_w_conv, v_b_conv, v_w_rg_a, v_b_rg_a, v_w_rg_x, v_b_rg_x, v_lru_lambda, v_w_sp, v_b_sp, v_ln_v_g, v_ln_v_b, v_w_o_lru, v_w_o_sgu, v_w_out, v_ln1_g, v_ln1_b, v_w_up, v_w_down, v_ln2_g, v_ln2_b):
    given = dict(locals())
    nb, seq, d = x.shape
    t = nb * seq
    w_lru = LRU_HEADS * HEAD
    d_sgu = SGU_GROUPS * HEAD
    xi, yi, ci = lax.axis_index("x"), lax.axis_index("y"), lax.axis_index("c")
    chip = 2 * xi + yi
    dev = 2 * chip + ci
    cidx = jnp.reshape(ci, (1,)).astype(jnp.int32)

    x2 = x.reshape(t, d)
    target = loss_target.reshape(t, d)

    big = ["w_in", "w_o_lru", "w_o_sgu", "w_out", "w_up", "w_down"]
    shards_a = [w_in[0].astype(BF16)]
    shards_b = [given[n][0].astype(BF16) for n in big[1:]]
    pidx = jnp.reshape(chip, (1,)).astype(jnp.int32)

    c_rows = _rows128(c)
    wconv_rows = _rows128(w_conv[0])
    slab0 = _all_gather_small(jnp.concatenate([c_rows, wconv_rows], axis=0), "gather_c_wconv")
    slab0 = slab0.reshape(N_DEV, -1, HEAD)
    c_all = slab0[:, :c_rows.shape[0]].reshape(N_DEV * nb, d)
    n_wc = CONV_WIDTH * (w_lru // N_CHIPS) // HEAD
    wc = slab0[0::2, c_rows.shape[0]:c_rows.shape[0] + n_wc].reshape(N_CHIPS, CONV_WIDTH, w_lru // N_CHIPS)
    w_conv_full = jnp.transpose(wc, (1, 0, 2)).reshape(CONV_WIDTH, w_lru)

    n_ada = w_ada.shape[2]
    b_ada_cols = lax.dynamic_slice(b_ada, (0, chip * n_ada), (1, n_ada))
    mod_cols = _ada_fwd(c_all, w_ada[0], b_ada_cols)
    half = (N_DEV * nb) // 2
    mod_half = lax.dynamic_slice(mod_cols, (ci * half, 0), (half, n_ada))
    mod_g = _all_gather_small(mod_half, "gather_mod").reshape(N_CHIPS, 2, half, n_ada)
    mod_all = jnp.transpose(mod_g, (1, 2, 0, 3)).reshape(N_DEV * nb, N_CHIPS * n_ada)
    mod_loc = lax.dynamic_slice(mod_all, (dev * nb, 0), (nb, N_CHIPS * n_ada)).reshape(nb, 6, d)
    modv = jnp.pad(mod_loc, ((0, 0), (0, 2), (0, 0)))

    lru_w = (w_conv_full, b_conv, w_rg_a[0], b_rg_a, w_rg_x[0], b_rg_x, lru_lambda)
    b_sp_t = jnp.transpose(b_sp[0])

    land = lambda s: jax.ShapeDtypeStruct((N_CHIPS,) + s.shape, s.dtype)
    sds = lambda s: jax.ShapeDtypeStruct(s.shape, s.dtype)
    started_a = _split_start(shards_a, [sds(shards_a[0])] * 2, _peer_gather_copies((0, 1)), 2, "gather_w_in_near_start",
                             after=(modv,))
    shards_b, shards_c = shards_b[:3], shards_b[3:]

    ids = lambda *v: jnp.stack(v).astype(jnp.int32)
    modv_t = modv + started_a[-1][0:1, 0:1]
    proj, h = _proj_fwd(x2, modv_t, [started_a[2]], ids(chip), b_in, seq, "proj_fwd_own")
    own_a, lands_a = _split_wait(started_a, 1, _peer_gather_copies((0, 1)), "gather_w_in_near_wait", after=(proj,))
    started_f = _split_start(own_a, [sds(own_a[0])], _far_gather_copies, 1, "gather_w_in_far_start", after=(lands_a[0],))
    started_b = _split_start(shards_b, [land(s) for s in shards_b], _gather_copies, 3 * len(shards_b),
                             "gather_w_mix_start", after=(started_f[-1],))
    started_c = _split_start(shards_c, [land(s) for s in shards_c], _gather_copies, 3 * len(shards_c),
                             "gather_w_mlp_start", after=(started_b[-1],))
    modv_t = modv + started_c[-1][0:1, 0:1]
    (proj,) = _proj_fwd(x2, modv_t, lands_a, ids(chip ^ 1, chip ^ 2), b_in, seq, "proj_fwd_near", proj_in=proj)
    own_a, land_f = _split_wait(started_f, 1, _far_gather_copies, "gather_w_in_far_wait", after=(proj,))
    (proj,) = _proj_fwd(x2, modv, land_f, ids(chip ^ 3), b_in, seq, "proj_fwd_far", proj_in=proj)
    w_in_shards, w_in_chips = own_a + lands_a + land_f, ids(chip, chip ^ 1, chip ^ 2, chip ^ 3)
    a, inp = _lru_prep(proj, lru_w, nb, seq)
    a3 = a.reshape(nb, seq, w_lru)
    hs = _scan(a3, inp.reshape(nb, seq, w_lru), False, "lru_scan").reshape(t, w_lru)
    y_sgu = _sgu_fwd(proj, w_sp[0], b_sp_t, ln_v_g, ln_v_b)
    shards_b, lands_b = _split_wait(started_b, len(shards_b), _gather_copies, "gather_w_mix_wait", after=(hs, y_sgu))
    w_o_lru_g, w_o_sgu_g, w_out_g = _fill_own_slot(lands_b, shards_b, pidx, ["own_" + n for n in big[1:4]])
    w_o_lru_g = w_o_lru_g.reshape(w_lru, d)
    w_out_g = w_out_g.reshape(d, d)
    yap, y_a, y_b, merged, mix, x1 = _mix_fwd(hs, proj, y_sgu, x2, modv, w_o_lru_g, w_o_sgu_g, w_out_g, ln1_g, ln1_b, seq)
    shards_c, lands_c = _split_wait(started_c, len(shards_c), _gather_copies, "gather_w_mlp_wait", after=(x1,))
    w_up_g, w_down_g = _fill_own_slot(lands_c, shards_c, pidx, ["own_" + n for n in big[4:]])
    w_down_g = w_down_g.reshape(-1, d)
    up, act, h2, dz2, df, st2, pb2 = _mlp_fwd(x1, modv, w_up_g, w_down_g, ln2_g, ln2_b, target, nb, seq)

    part = {}

    def to_sibling_start(group, tag, after=()):
        g4 = []
        for n in group:
            shard = given[n].shape[1:]
            g4.append(part[n].reshape(N_CHIPS, 2, shard[0] // 2, shard[1]))
        shapes = [jax.ShapeDtypeStruct((N_CHIPS,) + g.shape[2:], F32) for g in g4]
        return _split_start(g4, shapes, _to_sibling_copies, len(g4), "grads_to_sibling_start_" + tag, after)

    def to_chips_start(group, started, tag, after=()):
        g4, recv = _split_wait(started, len(group), _to_sibling_copies, "grads_to_sibling_wait_" + tag, after)
        own4 = [_add_own_half(g4[k], recv[k], cidx, "grad_pair_sum_" + n) for k, n in enumerate(group)]
        shapes = [jax.ShapeDtypeStruct((3,) + o.shape[1:], BF16) for o in own4]
        return _split_start(own4, shapes, _chip_exchange_copies, 3 * len(own4), "grads_chip_exchange_start_" + tag)

    def chips_finish(group, started, tag, after=()):
        own4, slots = _split_wait(started, len(group), _chip_exchange_copies, "grads_chip_exchange_wait_" + tag, after)
        return [_sum_own_and_peers(own4[k], slots[k], pidx, "grad_chip_sum_" + n) for k, n in enumerate(group)]

    dup, dz1, dmix, st1, pb1 = _mlp_bwd(df, up, w_down_g, w_up_g, dz2, x2, mix, modv, ln1_g, ln1_b, nb, seq)
    group1 = ["w_up", "w_down"]
    part["w_up"] = _weight_grad(h2, dup, True, "grad_w_up")
    part["w_down"] = _weight_grad(act, df, False, "grad_w_down")
    sib1 = to_sibling_start(group1, "mlp")
    dy_a, dy_b, dga, dgb, dgl, dyl, dys = _mix_bwd(dmix, proj, y_a, y_b, hs, w_out_g, w_o_lru_g, w_o_sgu_g, seq,
                                                   after=(sib1[-1],))
    group2 = ["w_o_lru", "w_o_sgu", "w_out"]
    part["w_o_lru"] = _weight_grad(yap, dy_a, False, "grad_w_o_lru")
    part["w_o_sgu"] = _weight_grad(y_sgu, dy_b, True, "grad_w_o_sgu")
    part["w_out"] = _weight_grad(merged, dmix, False, "grad_w_out")
    chips1 = to_chips_start(group1, sib1, "mlp", after=(dys, part["w_o_lru"], part["w_o_sgu"], part["w_out"]))
    sib2 = to_sibling_start(group2, "mix", after=(chips1[-1],))
    du, dv, g_w_sp, st_sgu, g_b_sp_t = _sgu_bwd(proj, dys, w_sp[0], b_sp_t, ln_v_g, ln_v_b, after=(sib2[-1],))
    dyl3 = dyl.reshape(nb, seq, w_lru)
    e = _scan(a3, dyl3, True, "lru_scan_bwd").reshape(t, w_lru)
    chips2 = to_chips_start(group2, sib2, "mix", after=(e, du))
    dxl, g_w_rg_a, g_w_rg_x, st_lru = _lru_bwd(proj, hs, e, dyl, lru_w, nb, seq, after=(chips2[-1],))
    dproj = jnp.concatenate([dxl, dgl, du, dv, dga, dgb], axis=1)

    didx = jnp.reshape(dev, (1,)).astype(jnp.int32)
    early = [
        ("w_conv", st_lru[4:8]), ("b_conv", st_lru[3]), ("w_rg_a", g_w_rg_a), ("b_rg_a", st_lru[0]),
        ("w_rg_x", g_w_rg_x), ("b_rg_x", st_lru[1]), ("lru_lambda", st_lru[2]), ("w_sp", g_w_sp),
        ("b_sp", jnp.transpose(g_b_sp_t[:, :SGU_GROUPS])), ("ln_v_g", st_sgu[0]), ("ln_v_b", st_sgu[1]),
        ("ln1_g", st1[0]), ("ln1_b", st1[1]), ("ln2_g", st2[0]), ("ln2_b", st2[1]),
    ]
    pieces_e = [_rows128(v) for _, v in early]
    slab_e = jnp.concatenate(pieces_e, axis=0)
    slab_e = jnp.pad(slab_e, ((0, (-slab_e.shape[0]) % TR_EW), (0, 0)))
    small_st = _split_start([slab_e], [jax.ShapeDtypeStruct((N_DEV,) + slab_e.shape, F32)], _all_devices_copies, N_DEV - 1,
                            "small_grads_start")

    group3 = ["w_in"]
    part["w_in"] = _weight_grad(h, dproj, True, "grad_w_in", after=(small_st[-1],))
    sib3 = to_sibling_start(group3, "in")
    chips3 = to_chips_start(group3, sib3, "in")
    grad_x2, g_b_in4, pb0 = _input_grad(dproj, w_in_shards, w_in_chips, dz1, x2, modv, nb, seq, after=(chips3[-1],))
    halves12 = chips_finish(group1, chips1, "mlp", after=(grad_x2,)) + chips_finish(group2, chips2, "mix", after=(grad_x2,))
    swap12 = _split_start(halves12, [jax.ShapeDtypeStruct(hv.shape, F32) for hv in halves12], _swap_copies, len(halves12),
                          "grads_swap_start")
    loss = lax.psum(st2[2, 0] + swap12[-1][0, 0], ("x", "y", "c"))
    grads = {}

    dmod_loc = jnp.stack([pb0[:, 1], pb0[:, 0], pb1[:, 2], pb1[:, 1], pb1[:, 0], pb2[:, 0]], axis=1)
    late = [("dmod", dmod_loc), ("b_in", g_b_in4[:, 0])]
    pieces_l = [_rows128(v) for _, v in late]
    slab_l = jnp.concatenate(pieces_l, axis=0)
    gathered = _all_gather_small(slab_l, "gather_small_grads", after=(swap12[-1],)).reshape(N_DEV, slab_l.shape[0], HEAD)
    rows_dmod = dmod_loc.size // HEAD
    dmod_all = gathered[:, :rows_dmod].reshape(N_DEV * nb, 6 * d)
    grads["b_in"] = _sum_slots(gathered[:, rows_dmod:], "grad_b_in_sum").reshape(1, -1)

    (slab_e,), (lands_e,) = _split_wait(small_st, 1, _all_devices_copies, "small_grads_wait", after=(gathered,))
    summed = _sum_devices(lands_e, slab_e, didx, "small_grad_sum")
    off = 0
    for (n, v), piece in zip(early, pieces_e):
        grads[n] = summed[off:off + v.size // HEAD].reshape(v.shape)
        off += piece.shape[0]

    mine12, theirs12 = _split_wait(swap12, len(halves12), _swap_copies, "grads_swap_wait", after=(summed,))
    (mine3,) = chips_finish(group3, chips3, "in", after=(summed,))
    (theirs3,) = _exchange([mine3], [jax.ShapeDtypeStruct(mine3.shape, F32)], _swap_copies, 1, "grads_swap_w_in")
    mine = dict(zip(group1 + group2 + group3, mine12 + [mine3]))
    theirs = dict(zip(group1 + group2 + group3, theirs12 + [theirs3]))

    dmod_cols = lax.dynamic_slice(dmod_all, (0, chip * n_ada), (N_DEV * nb, n_ada))
    grads["w_ada"], grads["b_ada"] = _ada_bwd(c_all, dmod_all, dmod_cols)
    n_wcs = w_lru // N_CHIPS
    grads["w_conv"] = lax.dynamic_slice(grads["w_conv"], (0, chip * n_wcs), (CONV_WIDTH, n_wcs))

    names = ['w_ada', 'b_ada', 'w_in', 'b_in', 'w_conv', 'b_conv', 'w_rg_a', 'b_rg_a', 'w_rg_x', 'b_rg_x', 'lru_lambda',
             'w_sp', 'b_sp', 'ln_v_g', 'ln_v_b', 'w_o_lru', 'w_o_sgu', 'w_out', 'ln1_g', 'ln1_b', 'w_up', 'w_down',
             'ln2_g', 'ln2_b']
    two_d = lambda v: v.reshape(-1, v.shape[-1])
    done = {}
    small_names = [n for n in names if n not in big and n != "w_ada"]
    small_out = _adamw_many([(two_d(given[n]), two_d(grads[n].reshape(given[n].shape)), two_d(given["m_" + n]),
                              two_d(given["v_" + n])) for n in small_names], "adamw_small")
    for n, res in zip(small_names, small_out):
        done[n] = (grads[n],) + tuple(res)
    for n in big + ["w_ada"]:
        w2, m2, v2 = two_d(given[n]), two_d(given["m_" + n]), two_d(given["v_" + n])
        if n in big:
            done[n] = _adamw_halves(w2, mine[n], theirs[n], m2, v2, cidx, "adamw_" + n)
        else:
            done[n] = (grads[n],) + tuple(_adamw(w2, two_d(grads[n]), m2, v2, "adamw_" + n))
    outs = [[done[n][k].reshape(given[n].shape) for n in names] for k in range(4)]
    return (loss, grad_x2.reshape(nb, seq, d), *outs[0], *outs[1], *outs[2], *outs[3])
```

```python
import functools
import math

import jax
import jax.numpy as jnp
from jax import lax
from jax.experimental import pallas as pl
from jax.experimental.pallas import tpu as pltpu

F32 = jnp.float32
BF16 = jnp.bfloat16
MESH = pl.DeviceIdType.MESH

N_CHIPS = 4
N_DEV = 8
LRU_HEADS = 10
HEAD = 128
SGU_GROUPS = 6
SGU_CHUNK = 64
CONV_WIDTH = 4
LRU_C = 8.0
ALPHA = 2.0 ** 0.25
LN_EPS = 1e-5
ADAM_LR, ADAM_B1, ADAM_B2, ADAM_EPS, ADAM_WD, ADAM_STEP = 0.001, 0.9, 0.999, 1e-08, 0.01, 10

VMEM_LIMIT = 56 * 1024 * 1024
VMEM_LIMIT_MAX = 62 * 1024 * 1024
TM_PROJ = 1024
TM_MIX = 256
TM_MLP = 512
TS_MLP = 256
TM_SGU = 512
TM_DH = 512
TT_DW = 2048
TC_SCAN = 256
TR_EW = 256


def _cp(sem=None, limit=None):
    return pltpu.CompilerParams(dimension_semantics=sem, vmem_limit_bytes=limit or VMEM_LIMIT)


def _mm(a, b):
    return jnp.dot(a.astype(BF16), b.astype(BF16), preferred_element_type=F32)


def _mm_nt(a, b):
    return lax.dot_general(a.astype(BF16), b.astype(BF16), (((1,), (1,)), ((), ())), preferred_element_type=F32)


def _mm_tn(a, b):
    return lax.dot_general(a.astype(BF16), b.astype(BF16), (((0,), (0,)), ((), ())), preferred_element_type=F32)


def _sigmoid(x):
    return 1.0 / (1.0 + jnp.exp(-x))


def _sigmoid_t(x):
    return 0.5 * jnp.tanh(0.5 * x) + 0.5


_GELU_K = math.sqrt(2.0 / math.pi)


def _gelu(x):
    t = jnp.tanh(_GELU_K * (x + 0.044715 * (x * x * x)))
    return 0.5 * x * (1.0 + t)


def _gelu_and_grad(x):
    x2 = x * x
    t = jnp.tanh(_GELU_K * (x + 0.044715 * (x2 * x)))
    g = 0.5 * x * (1.0 + t)
    dg = 0.5 * (1.0 + t) + 0.5 * x * (1.0 - t * t) * (_GELU_K * (1.0 + 3.0 * 0.044715 * x2))
    return g, dg


def _ln_stats(z):
    mu = jnp.mean(z, axis=-1, keepdims=True)
    zc = z - mu
    var = jnp.mean(zc * zc, axis=-1, keepdims=True)
    rstd = lax.rsqrt(var + LN_EPS)
    return zc * rstd, rstd


def _ln_bwd(dxh, xhat, rstd):
    m1 = jnp.mean(dxh, axis=-1, keepdims=True)
    m2 = jnp.mean(dxh * xhat, axis=-1, keepdims=True)
    return rstd * (dxh - m1 - xhat * m2)


def _colsum(v):
    return jnp.sum(v, axis=0, keepdims=True)


def _shift_down(v, j):
    if j == 0:
        return v
    rows = lax.broadcasted_iota(jnp.int32, v.shape, 0)
    return jnp.where(rows >= j, pltpu.roll(v, j, 0), 0.0)


def _shift_up(v, j):
    if j == 0:
        return v
    n = v.shape[0]
    rows = lax.broadcasted_iota(jnp.int32, v.shape, 0)
    return jnp.where(rows < n - j, pltpu.roll(v, n - j, 0), 0.0)


def _load_weights(srcs, dsts, sems):
    cps = [pltpu.make_async_copy(s, dd, sems.at[k]) for k, (s, dd) in enumerate(zip(srcs, dsts))]
    for cp in cps:
        cp.start()
    for cp in cps:
        cp.wait()


def _my_pos():
    return lax.axis_index("x"), lax.axis_index("y"), lax.axis_index("c")


def _all_gather_small(v, name, after=()):
    m_per, n = v.shape

    def body(x_ref, out_ref, send_sems, recv_sems, local_sem):
        x, y, c = _my_pos()
        me, sibling = (x, y, c), (x, y, 1 - c)
        chips = [(1 - x, y), (x, 1 - y), (1 - x, 1 - y)]

        def rows(px, py, pc):
            return out_ref.at[pl.ds((4 * px + 2 * py + pc) * m_per, m_per), :]

        def copy(k, block, to, src=None):
            return pltpu.make_async_remote_copy(
                src_ref=rows(*block) if src is None else src, dst_ref=rows(*block),
                send_sem=send_sems.at[k], recv_sem=recv_sems.at[k], device_id=to, device_id_type=MESH)

        mine = pltpu.make_async_copy(x_ref, rows(*me), local_sem)
        mine.start()
        first = [copy(0, me, sibling, src=x_ref)]
        first += [copy(1 + j, me, (*chip, c), src=x_ref) for j, chip in enumerate(chips)]
        for cp in first:
            cp.start()
        passed = [copy(4 + j, (*chip, c), sibling) for j, chip in enumerate(chips)]
        for j, chip in enumerate(chips):
            copy(1 + j, (*chip, c), me).wait_recv()
            passed[j].start()
        copy(0, sibling, me).wait_recv()
        for j, chip in enumerate(chips):
            copy(4 + j, (*chip, 1 - c), me).wait_recv()
        for cp in first + passed:
            cp.wait_send()
        mine.wait()

    return pl.pallas_call(
        _ordered(body, 1, after), name=name,
        out_shape=jax.ShapeDtypeStruct((N_DEV * m_per, n), v.dtype),
        in_specs=[pl.BlockSpec(memory_space=pltpu.VMEM)] + [pl.BlockSpec(memory_space=pl.ANY)] * len(after),
        out_specs=pl.BlockSpec(memory_space=pltpu.VMEM),
        scratch_shapes=[pltpu.SemaphoreType.DMA((7,)), pltpu.SemaphoreType.DMA((7,)), pltpu.SemaphoreType.DMA],
        compiler_params=pltpu.CompilerParams(vmem_limit_bytes=VMEM_LIMIT),
    )(v, *after)


_HBM = pl.BlockSpec(memory_space=pltpu.HBM)
_ANY = pl.BlockSpec(memory_space=pl.ANY)
_SEM = pl.BlockSpec(memory_space=pltpu.SEMAPHORE)
_EFFECT = pltpu.SideEffectType.DATAFLOW_SIDE_EFFECTING


def _ordered(body, n_in, after):
    k = len(after)
    if not k:
        return body
    return lambda *refs: body(*refs[:n_in], *refs[n_in + k:])


def _gather_copies(ins, lands, send_sems, recv_sems):
    x, y, c = _my_pos()
    p = 2 * x + y
    peers = [(x, 1 - y), (1 - x, y), (1 - x, 1 - y)]
    sends, recvs = [], []
    for k in range(len(ins)):
        for j, (qx, qy) in enumerate(peers):
            sems = dict(send_sem=send_sems.at[3 * k + j], recv_sem=recv_sems.at[3 * k + j],
                        device_id=(qx, qy, c), device_id_type=MESH)
            sends.append(pltpu.make_async_remote_copy(src_ref=ins[k], dst_ref=lands[k].at[p], **sems))
            recvs.append(pltpu.make_async_remote_copy(src_ref=ins[k], dst_ref=lands[k].at[2 * qx + qy], **sems))
    return sends, recvs


def _peer_gather_copies(peers):
    def copies(ins, lands, send_sems, recv_sems):
        x, y, c = _my_pos()
        where = [(x, 1 - y), (1 - x, y), (1 - x, 1 - y)]
        cps = [pltpu.make_async_remote_copy(
            src_ref=ins[0], dst_ref=lands[j], send_sem=send_sems.at[j], recv_sem=recv_sems.at[j],
            device_id=(*where[j], c), device_id_type=MESH) for j in peers]
        return cps, cps
    return copies


def _far_gather_copies(ins, lands, send_sems, recv_sems):
    x, y, c = _my_pos()
    cps = [pltpu.make_async_remote_copy(
        src_ref=ins[0], dst_ref=lands[0], send_sem=send_sems.at[0], recv_sem=recv_sems.at[0],
        device_id=(1 - x, 1 - y, c), device_id_type=MESH)]
    return cps, cps


def _to_sibling_copies(ins, lands, send_sems, recv_sems):
    x, y, c = _my_pos()
    cps = [pltpu.make_async_remote_copy(
        src_ref=ins[k].at[:, 1 - c], dst_ref=lands[k], send_sem=send_sems.at[k], recv_sem=recv_sems.at[k],
        device_id=(x, y, 1 - c), device_id_type=MESH) for k in range(len(ins))]
    return cps, cps


def _chip_exchange_copies(ins, lands, send_sems, recv_sems):
    x, y, c = _my_pos()
    peers = [(x, 1 - y), (1 - x, y), (1 - x, 1 - y)]
    cps = []
    for k in range(len(ins)):
        for j, (qx, qy) in enumerate(peers):
            cps.append(pltpu.make_async_remote_copy(
                src_ref=ins[k].at[2 * qx + qy], dst_ref=lands[k].at[j], send_sem=send_sems.at[3 * k + j],
                recv_sem=recv_sems.at[3 * k + j], device_id=(qx, qy, c), device_id_type=MESH))
    return cps, cps


def _all_devices_copies(ins, lands, send_sems, recv_sems):
    x, y, c = _my_pos()
    me = 4 * x + 2 * y + c
    sends, recvs = [], []
    for r in range(1, N_DEV):
        px = 1 - x if r & 4 else x
        py = 1 - y if r & 2 else y
        pc = 1 - c if r & 1 else c
        sems = dict(send_sem=send_sems.at[r - 1], recv_sem=recv_sems.at[r - 1], device_id=(px, py, pc), device_id_type=MESH)
        sends.append(pltpu.make_async_remote_copy(src_ref=ins[0], dst_ref=lands[0].at[me], **sems))
        recvs.append(pltpu.make_async_remote_copy(src_ref=ins[0], dst_ref=lands[0].at[4 * px + 2 * py + pc], **sems))
    return sends, recvs


def _swap_copies(ins, lands, send_sems, recv_sems):
    x, y, c = _my_pos()
    cps = [pltpu.make_async_remote_copy(
        src_ref=ins[k], dst_ref=lands[k], send_sem=send_sems.at[k], recv_sem=recv_sems.at[k],
        device_id=(x, y, 1 - c), device_id_type=MESH) for k in range(len(ins))]
    return cps, cps


def _split_start(ins, land_shapes, copies, n_sems, name, after=()):
    n, nl = len(ins), len(land_shapes)
    first_out = n + nl + len(after)

    def body(*refs):
        in_refs, land_refs = refs[:n], refs[n:n + nl]
        send_sems, recv_sems = refs[first_out:first_out + 2]
        token = refs[-1]
        sends, _ = copies(in_refs, land_refs, send_sems, recv_sems)
        for cp in sends:
            cp.start()
        token[...] = jnp.zeros_like(token)

    lands = [pltpu.with_memory_space_constraint(lax.empty(s.shape, s.dtype), pltpu.HBM) for s in land_shapes]
    ins = [pltpu.with_memory_space_constraint(s, pltpu.HBM) for s in ins]
    return pl.pallas_call(
        body, name=name,
        out_shape=(pltpu.SemaphoreType.DMA((n_sems,)), pltpu.SemaphoreType.DMA((n_sems,)),
                   *[pltpu.HBM(s.shape, s.dtype) for s in ins], *[pltpu.HBM(s.shape, s.dtype) for s in lands],
                   jax.ShapeDtypeStruct((8, HEAD), F32)),
        in_specs=[_HBM] * (n + nl) + [pl.BlockSpec(memory_space=pl.ANY)] * len(after),
        out_specs=(_SEM, _SEM, *([_HBM] * (n + nl)), pl.BlockSpec(memory_space=pltpu.VMEM)),
        input_output_aliases={k: 2 + k for k in range(n + nl)},
        compiler_params=pltpu.CompilerParams(has_side_effects=_EFFECT),
    )(*ins, *lands, *after)


def _split_wait(started, n, copies, name, after=()):
    send_sems, recv_sems = started[0], started[1]
    bufs = started[2:-1]
    nb = len(bufs)

    def body(*refs):
        in_refs, land_refs = refs[:n], refs[n:nb]
        sends, recvs = copies(in_refs, land_refs, refs[nb], refs[nb + 1])
        for cp in sends:
            cp.wait_send()
        for cp in recvs:
            cp.wait_recv()

    outs = pl.pallas_call(
        body, name=name,
        out_shape=tuple(pltpu.HBM(s.shape, s.dtype) for s in bufs),
        in_specs=[_HBM] * nb + [_SEM, _SEM] + [pl.BlockSpec(memory_space=pl.ANY)] * len(after),
        out_specs=tuple([_HBM] * nb),
        input_output_aliases={k: k for k in range(nb)},
        compiler_params=pltpu.CompilerParams(has_side_effects=_EFFECT),
    )(*bufs, send_sems, recv_sems, *after)
    return list(outs[:n]), list(outs[n:])


def _fill_own_slot(gathered, shards, pidx, names):
    outs = []
    for g, s, name in zip(gathered, shards, names):
        r, cdim = s.shape
        tr = _row_tile(r)

        def body(p_ref, s_ref, g_ref, o_ref):
            o_ref[...] = s_ref[...]

        outs.append(pl.pallas_call(
            body, name=name,
            grid_spec=pltpu.PrefetchScalarGridSpec(
                num_scalar_prefetch=1, grid=(r // tr,),
                in_specs=[pl.BlockSpec((tr, cdim), lambda i, p: (i, 0)), pl.BlockSpec(memory_space=pl.ANY)],
                out_specs=pl.BlockSpec((None, tr, cdim), lambda i, p: (p[0], i, 0))),
            out_shape=jax.ShapeDtypeStruct(g.shape, g.dtype),
            input_output_aliases={2: 0},
            compiler_params=_cp(("arbitrary",)),
        )(pidx, s, g))
    return outs


def _sum_own_and_peers(own4, slots, pidx, name):
    _, rh, cdim = own4.shape
    tr = _row_tile(rh)

    def body(p_ref, own_ref, s_ref, o_ref):
        acc = own_ref[...].astype(F32)
        for j in range(3):
            acc = acc + s_ref[j].astype(F32)
        o_ref[...] = acc

    return pl.pallas_call(
        body, name=name,
        grid_spec=pltpu.PrefetchScalarGridSpec(
            num_scalar_prefetch=1, grid=(rh // tr,),
            in_specs=[pl.BlockSpec((None, tr, cdim), lambda i, p: (p[0], i, 0)),
                      pl.BlockSpec((3, tr, cdim), lambda i, p: (0, i, 0))],
            out_specs=pl.BlockSpec((tr, cdim), lambda i, p: (i, 0))),
        out_shape=jax.ShapeDtypeStruct((rh, cdim), F32),
        compiler_params=_cp(("arbitrary",)),
    )(pidx, own4, slots)


def _exchange(ins, land_shapes, copies, n_sems, name):
    n, nl = len(ins), len(land_shapes)

    def body(*refs):
        sends, recvs = copies(refs[:n], refs[n:n + nl], refs[n + nl], refs[n + nl + 1])
        for cp in sends:
            cp.start()
        for cp in sends:
            cp.wait_send()
        for cp in recvs:
            cp.wait_recv()

    any_spec = pl.BlockSpec(memory_space=pl.ANY)
    return pl.pallas_call(
        body, name=name,
        out_shape=[jax.ShapeDtypeStruct(s.shape, s.dtype) for s in land_shapes],
        in_specs=[any_spec] * n, out_specs=[any_spec] * nl,
        scratch_shapes=[pltpu.SemaphoreType.DMA((n_sems,)), pltpu.SemaphoreType.DMA((n_sems,))],
    )(*ins)


def _row_tile(r):
    t = min(TR_EW, r)
    while r % t:
        t //= 2
    return t


def _add_own_half(g4, recv, cidx, name):
    _, _, rh, cdim = g4.shape
    tr = _row_tile(rh)

    def body(c_ref, a_ref, b_ref, o_ref):
        o_ref[...] = (a_ref[...] + b_ref[...]).astype(BF16)

    return pl.pallas_call(
        body, name=name,
        grid_spec=pltpu.PrefetchScalarGridSpec(
            num_scalar_prefetch=1, grid=(N_CHIPS, rh // tr),
            in_specs=[pl.BlockSpec((None, None, tr, cdim), lambda q, i, c: (q, c[0], i, 0)),
                      pl.BlockSpec((None, tr, cdim), lambda q, i, c: (q, i, 0))],
            out_specs=pl.BlockSpec((None, tr, cdim), lambda q, i, c: (q, i, 0))),
        out_shape=jax.ShapeDtypeStruct(recv.shape, BF16),
        compiler_params=_cp(("arbitrary", "arbitrary")),
    )(cidx, g4, recv)


def _sum_slots(v, name):
    n, r, cdim = v.shape
    tr = _row_tile(r)

    def body(v_ref, o_ref):
        acc = v_ref[0].astype(F32)
        for k in range(1, n):
            acc = acc + v_ref[k].astype(F32)
        o_ref[...] = acc

    return pl.pallas_call(
        body, name=name, grid=(r // tr,),
        in_specs=[pl.BlockSpec((n, tr, cdim), lambda i: (0, i, 0))],
        out_specs=pl.BlockSpec((tr, cdim), lambda i: (i, 0)),
        out_shape=jax.ShapeDtypeStruct((r, cdim), F32),
        compiler_params=_cp(("arbitrary",)),
    )(v)


def _sum_devices(lands, own, didx, name):
    _, r, cdim = lands.shape
    tr = _row_tile(r)

    def body(d_ref, l_ref, own_ref, o_ref):
        acc = jnp.where(d_ref[0] == 0, own_ref[...], l_ref[0])
        for dv in range(1, N_DEV):
            acc = acc + jnp.where(d_ref[0] == dv, own_ref[...], l_ref[dv])
        o_ref[...] = acc

    return pl.pallas_call(
        body, name=name,
        grid_spec=pltpu.PrefetchScalarGridSpec(
            num_scalar_prefetch=1, grid=(r // tr,),
            in_specs=[pl.BlockSpec((N_DEV, tr, cdim), lambda i, dd: (0, i, 0)), pl.BlockSpec((tr, cdim), lambda i, dd: (i, 0))],
            out_specs=pl.BlockSpec((tr, cdim), lambda i, dd: (i, 0))),
        out_shape=jax.ShapeDtypeStruct((r, cdim), F32),
        compiler_params=_cp(("arbitrary",)),
    )(didx, lands, own)


def _adamw_math(wv, gg, mv, vv):
    nm = ADAM_B1 * mv + (1.0 - ADAM_B1) * gg
    nv = ADAM_B2 * vv + (1.0 - ADAM_B2) * (gg * gg)
    m_hat = nm / (1.0 - ADAM_B1 ** ADAM_STEP)
    v_hat = nv / (1.0 - ADAM_B2 ** ADAM_STEP)
    return -ADAM_LR * (m_hat / (jnp.sqrt(v_hat) + ADAM_EPS) + ADAM_WD * wv), nm, nv


def _adamw_halves(w, mine, theirs, m, v, cidx, name):
    r, cdim = w.shape
    rh = r // 2
    tr = _row_tile(rh)
    nblk = rh // tr

    def body(c_ref, w_ref, a_ref, b_ref, m_ref, v_ref, g_ref, d_ref, nm_ref, nv_ref):
        gg = jnp.where(pl.program_id(0) == c_ref[0], a_ref[...], b_ref[...])
        g_ref[...] = gg
        d_ref[...], nm_ref[...], nv_ref[...] = _adamw_math(w_ref[...], gg, m_ref[...], v_ref[...])

    full = pl.BlockSpec((tr, cdim), lambda hh, i, c: (hh * nblk + i, 0))
    half = pl.BlockSpec((tr, cdim), lambda hh, i, c: (i, 0))
    return pl.pallas_call(
        body, name=name,
        grid_spec=pltpu.PrefetchScalarGridSpec(
            num_scalar_prefetch=1, grid=(2, nblk),
            in_specs=[full, half, half, full, full], out_specs=[full] * 4),
        out_shape=[jax.ShapeDtypeStruct((r, cdim), F32)] * 4,
        compiler_params=_cp(("arbitrary", "arbitrary")),
    )(cidx, w, mine, theirs, m, v)


def _adamw_many(params, name):
    n = len(params)

    def body(*refs):
        ins, outs = refs[:4 * n], refs[4 * n:]
        for k in range(n):
            w_ref, g_ref, m_ref, v_ref = ins[4 * k:4 * k + 4]
            outs[3 * k][...], outs[3 * k + 1][...], outs[3 * k + 2][...] = _adamw_math(
                w_ref[...], g_ref[...], m_ref[...], v_ref[...])

    flat = [a for p in params for a in p]
    res = pl.pallas_call(
        body, name=name,
        out_shape=[jax.ShapeDtypeStruct(p[0].shape, F32) for p in params for _ in range(3)],
        compiler_params=pltpu.CompilerParams(vmem_limit_bytes=VMEM_LIMIT),
    )(*flat)
    return [res[3 * k:3 * k + 3] for k in range(n)]


def _adamw(w, g, m, v, name):
    r, cdim = w.shape
    tr = _row_tile(r) if r % 8 == 0 else r

    def body(w_ref, g_ref, m_ref, v_ref, d_ref, nm_ref, nv_ref):
        d_ref[...], nm_ref[...], nv_ref[...] = _adamw_math(w_ref[...], g_ref[...], m_ref[...], v_ref[...])

    spec = pl.BlockSpec((tr, cdim), lambda i: (i, 0))
    return pl.pallas_call(
        body, name=name, grid=(r // tr,), in_specs=[spec] * 4, out_specs=[spec] * 3,
        out_shape=[jax.ShapeDtypeStruct((r, cdim), F32)] * 3,
        compiler_params=_cp(("arbitrary",)),
    )(w, g, m, v)


def _ada_fwd(c_all, w_ada, b_cols):
    nb, _ = c_all.shape
    n = w_ada.shape[1]

    def body(c_ref, w_ref, b_ref, o_ref):
        cv = c_ref[...]
        o_ref[...] = _mm(cv * _sigmoid(cv), w_ref[...]) + b_ref[...]

    return pl.pallas_call(
        body, name="ada_fwd", out_shape=jax.ShapeDtypeStruct((nb, n), F32),
        compiler_params=pltpu.CompilerParams(vmem_limit_bytes=VMEM_LIMIT),
    )(c_all, w_ada, b_cols)


def _ada_bwd(c_all, dmod_all, dmod_cols):
    d = c_all.shape[1]
    n = dmod_cols.shape[1]

    def body(c_ref, da_ref, dc_ref, gw_ref, gb_ref):
        cv = c_ref[...]
        gw_ref[...] = _mm_tn(cv * _sigmoid(cv), dc_ref[...])
        gb_ref[...] = _colsum(da_ref[...])

    return pl.pallas_call(
        body, name="ada_bwd",
        out_shape=[jax.ShapeDtypeStruct((d, n), F32), jax.ShapeDtypeStruct((1, dmod_all.shape[1]), F32)],
        compiler_params=pltpu.CompilerParams(vmem_limit_bytes=VMEM_LIMIT),
    )(c_all, dmod_all, dmod_cols)


def _proj_fwd(x2, modv, ws, cols, b_in, seq, name, proj_in=None):
    t, d = x2.shape
    n = len(ws)
    ns = ws[0].shape[1]
    tm = min(TM_PROJ, seq)
    tpb = seq // tm
    first = proj_in is None

    def body(c_ref, x_ref, mod_ref, *refs):
        w_refs, b_ref = refs[:n], refs[n]
        outs = refs[n + 1 if first else n + 2:]
        proj_ref, h_s = outs[0], outs[-1]
        s = pl.program_id(1)

        @pl.when(s == 0)
        def _():
            h = (x_ref[...] * (1.0 + mod_ref[1:2, :]) + mod_ref[0:1, :]).astype(BF16)
            h_s[...] = h
            if first:
                outs[1][...] = h

        for k in range(n):
            @pl.when(s == k)
            def _():
                proj_ref[...] = (jnp.dot(h_s[...], w_refs[k][...], preferred_element_type=F32) + b_ref[...]).astype(BF16)

    in_specs = [pl.BlockSpec((tm, d), lambda i, s, c: (i, 0)),
                pl.BlockSpec((None, 8, d), lambda i, s, c: (i // tpb, 0, 0))]
    in_specs += [pl.BlockSpec((d, ns), lambda i, s, c: (0, 0))] * n
    in_specs += [pl.BlockSpec((1, ns), lambda i, s, c: (0, c[s]))]
    out_specs = [pl.BlockSpec((tm, ns), lambda i, s, c: (i, c[s]))]
    out_shape = [jax.ShapeDtypeStruct((t, N_CHIPS * ns), BF16)]
    args = [cols, x2, modv, *ws, b_in]
    aliases = {}
    if first:
        out_specs.append(pl.BlockSpec((tm, d), lambda i, s, c: (i, 0)))
        out_shape.append(jax.ShapeDtypeStruct((t, d), BF16))
    else:
        in_specs.append(_ANY)
        args.append(proj_in)
        aliases = {len(args) - 1: 0}
    return pl.pallas_call(
        body, name=name,
        grid_spec=pltpu.PrefetchScalarGridSpec(
            num_scalar_prefetch=1, grid=(t // tm, n), in_specs=in_specs, out_specs=out_specs,
            scratch_shapes=[pltpu.VMEM((tm, d), BF16)]),
        out_shape=out_shape, input_output_aliases=aliases,
        compiler_params=_cp(("arbitrary", "arbitrary")),
    )(*args)


def _lru_gates(xl, wc_ref, bc_ref, wa_ref, ba_ref, wx_ref, bx_ref, lam_ref):
    xc = bc_ref[...] + wc_ref[CONV_WIDTH - 1:CONV_WIDTH, :] * xl
    for k in range(CONV_WIDTH - 1):
        xc = xc + wc_ref[k:k + 1, :] * _shift_down(xl, CONV_WIDTH - 1 - k)
    r = _sigmoid(_mm(xc, wa_ref[...]) + ba_ref[...])
    gi = _sigmoid_t(_mm(xc, wx_ref[...]) + bx_ref[...])
    nl = -lam_ref[...]
    e = jnp.exp(-jnp.abs(nl))
    u = 1.0 + e
    dlt = u - 1.0
    log1p_e = jnp.where(dlt == 0.0, e, jnp.log(u) * (e / jnp.where(dlt == 0.0, 1.0, dlt)))
    big_l = -LRU_C * (jnp.maximum(nl, 0.0) + log1p_e)
    la = big_l * r
    a = jnp.exp(la)
    m2 = jnp.tanh(-la) * (a * a + 1.0)
    return xc, r, gi, big_l, a, m2


def _lru_prep(proj, lru_w, nb, seq):
    t = proj.shape[0]
    w = LRU_HEADS * HEAD
    w_conv, b_conv, w_a, b_a, w_x, b_x, lam = lru_w

    def body(x_ref, wc_ref, bc_ref, wa_ref, ba_ref, wx_ref, bx_ref, lam_ref, a_ref, inp_ref):
        xc, r, gi, big_l, a, m2 = _lru_gates(x_ref[...].astype(F32), wc_ref, bc_ref, wa_ref, ba_ref, wx_ref, bx_ref, lam_ref)
        a_ref[...] = a
        inp_ref[...] = jnp.sqrt(m2) * (gi * xc)

    col = lambda b, hd: (0, hd)
    head = lambda b, hd: (hd, 0, 0)
    tok = lambda b, hd: (b, hd)
    return pl.pallas_call(
        body, name="lru_prep", grid=(nb, LRU_HEADS),
        in_specs=[pl.BlockSpec((seq, HEAD), tok),
                  pl.BlockSpec((CONV_WIDTH, HEAD), col), pl.BlockSpec((1, HEAD), col),
                  pl.BlockSpec((None, HEAD, HEAD), head), pl.BlockSpec((1, HEAD), col),
                  pl.BlockSpec((None, HEAD, HEAD), head), pl.BlockSpec((1, HEAD), col),
                  pl.BlockSpec((1, HEAD), col)],
        out_specs=[pl.BlockSpec((seq, HEAD), tok)] * 2,
        out_shape=[jax.ShapeDtypeStruct((t, w), F32)] * 2,
        compiler_params=_cp(("arbitrary", "arbitrary")),
    )(proj, w_conv, b_conv, w_a, b_a, w_x, b_x, lam)


def _scan(a3, b3, reverse, name, out_dtype):
    nb, seq, w = a3.shape
    tc = min(TC_SCAN, seq)
    nchunk = seq // tc
    npair = tc // 16

    def combine(av, bv):
        rows = lax.broadcasted_iota(jnp.int32, av.shape, 0)
        for s in (1, 2, 4):
            if reverse:
                keep = rows < 8 - s
                a_sh, b_sh = pltpu.roll(av, 8 - s, 0), pltpu.roll(bv, 8 - s, 0)
            else:
                keep = rows >= s
                a_sh, b_sh = pltpu.roll(av, s, 0), pltpu.roll(bv, s, 0)
            bv = jnp.where(keep, bv + av * b_sh, bv)
            av = jnp.where(keep, av * a_sh, av)
        return av, bv

    def body(a_ref, b_ref, h_ref, carry):
        @pl.when(pl.program_id(0) == 0)
        def _():
            carry[...] = jnp.zeros_like(carry)

        for b in range(nb):
            def pair(j, hprev):
                jj = npair - 1 - j if reverse else j
                base = pl.multiple_of(jj * 16, 16)
                a16 = a_ref[b, pl.ds(base, 16), :]
                b16 = b_ref[b, pl.ds(base, 16), :].astype(F32)
                outs = [None, None]
                for k in ((1, 0) if reverse else (0, 1)):
                    av, bv = a16[8 * k:8 * k + 8, :], b16[8 * k:8 * k + 8, :]
                    av, bv = combine(av, av * bv if reverse else bv)
                    h = bv + av * hprev
                    outs[k] = h
                    hprev = jnp.broadcast_to(h[0:1, :] if reverse else h[7:8, :], (8, w))
                h_ref[b, pl.ds(base, 16), :] = jnp.concatenate(outs, axis=0).astype(out_dtype)
                return hprev

            carry[b] = lax.fori_loop(0, npair, pair, carry[b])

    imap = (lambda i: (0, nchunk - 1 - i, 0)) if reverse else (lambda i: (0, i, 0))
    spec = pl.BlockSpec((nb, tc, w), imap)
    return pl.pallas_call(
        body, name=name, grid=(nchunk,), in_specs=[spec, spec], out_specs=spec,
        out_shape=jax.ShapeDtypeStruct((nb, seq, w), out_dtype),
        scratch_shapes=[pltpu.VMEM((nb, 8, w), F32)],
        compiler_params=_cp(("arbitrary",)),
    )(a3, b3)


def _sgu_mask():
    ti = lax.broadcasted_iota(jnp.int32, (HEAD, HEAD), 0) // SGU_CHUNK
    si = lax.broadcasted_iota(jnp.int32, (HEAD, HEAD), 1) // SGU_CHUNK
    return si <= ti


def _sgu_specs(tm, d_sgu):
    pw = 256
    first_u = (2 * LRU_HEADS * HEAD) // pw
    n_piece = d_sgu // pw
    specs = [pl.BlockSpec((tm, pw), functools.partial(lambda i, k: (i, k), k=first_u + j)) for j in range(2 * n_piece)]
    return specs, n_piece


def _sgu_fwd(proj, w_sp, b_sp_t, ln_g, ln_b):
    t = proj.shape[0]
    d_sgu = SGU_GROUPS * HEAD
    tm = min(TM_SGU, t)
    nblk = tm // HEAD
    specs, n_piece = _sgu_specs(tm, d_sgu)

    def body(*refs):
        u = jnp.concatenate([r[...] for r in refs[:n_piece]], axis=1).astype(F32)
        v = jnp.concatenate([r[...] for r in refs[n_piece:2 * n_piece]], axis=1).astype(F32)
        w_ref, bt_ref, g_ref, b_ref, y_ref = refs[2 * n_piece:]
        ug = _gelu(u)
        xhat, _ = _ln_stats(_gelu(v))
        vn = (xhat * g_ref[...] + b_ref[...]).astype(BF16)
        mask = _sgu_mask()
        for g in range(SGU_GROUPS):
            wm = jnp.where(mask, w_ref[g], 0.0).astype(BF16)
            cols = slice(g * HEAD, (g + 1) * HEAD)
            for n in range(nblk):
                rows = slice(n * HEAD, (n + 1) * HEAD)
                mixed = jnp.dot(wm, vn[rows, cols], preferred_element_type=F32) + bt_ref[:, g:g + 1]
                y_ref[rows, cols] = (ug[rows, cols] * mixed).astype(BF16)

    full = lambda shape: pl.BlockSpec(shape, lambda i: (0,) * len(shape))
    return pl.pallas_call(
        body, name="sgu_fwd", grid=(t // tm,),
        in_specs=specs + [full(w_sp.shape), full(b_sp_t.shape), full(ln_g.shape), full(ln_b.shape)],
        out_specs=pl.BlockSpec((tm, d_sgu), lambda i: (i, 0)),
        out_shape=jax.ShapeDtypeStruct((t, d_sgu), BF16),
        compiler_params=_cp(("arbitrary",)),
    )(*([proj] * (2 * n_piece)), w_sp, b_sp_t, ln_g, ln_b)


def _mix_fwd(hs, proj, y_sgu, x2, modv, w_o_lru_g, w_o_sgu_g, w_out_g, ln1_g, ln1_b, seq):
    t, d = x2.shape
    w = hs.shape[1]
    d_sgu = y_sgu.shape[1]
    nq, _, ns = w_o_sgu_g.shape
    tm = min(TM_MIX, seq)
    tpb = seq // tm

    def body(hs_ref, gl_ref, ys_ref, ga_ref, gb_ref, x_ref, mod_ref, wl_ref, ws_ref, wo_ref, g1_ref, b1_ref,
             yap_ref, ya_ref, yb_ref, mg_ref, mix_ref, x1_ref):
        yap = (hs_ref[...].astype(F32) * _gelu(gl_ref[...].astype(F32))).astype(BF16)
        yap_ref[...] = yap
        y_a = jnp.dot(yap, wl_ref[...], preferred_element_type=F32)
        ys = ys_ref[...]
        y_b = jnp.concatenate([jnp.dot(ys, ws_ref[q], preferred_element_type=F32) for q in range(nq)], axis=1)
        ya_ref[...] = y_a.astype(BF16)
        yb_ref[...] = y_b.astype(BF16)
        merged = (_sigmoid_t(ga_ref[...].astype(F32)) * y_a + _sigmoid_t(gb_ref[...].astype(F32)) * y_b).astype(BF16)
        mg_ref[...] = merged
        mix = jnp.dot(merged, wo_ref[...], preferred_element_type=F32)
        mix_ref[...] = mix
        xhat, _ = _ln_stats(ALPHA * x_ref[...] + (1.0 + mod_ref[2:3, :]) * mix)
        x1_ref[...] = xhat * g1_ref[...] + b1_ref[...]

    row = lambda width, col: pl.BlockSpec((tm, width), functools.partial(lambda i, k: (i, k), k=col))
    full = lambda shape: pl.BlockSpec(shape, lambda i: (0,) * len(shape))
    return pl.pallas_call(
        body, name="mix_fwd", grid=(t // tm,),
        in_specs=[row(w, 0), row(w, 1), row(d_sgu, 0), row(d, 4), row(d, 5), row(d, 0),
                  pl.BlockSpec((None, 8, d), lambda i: (i // tpb, 0, 0)),
                  full(w_o_lru_g.shape), full(w_o_sgu_g.shape), full(w_out_g.shape), full(ln1_g.shape), full(ln1_b.shape)],
        out_specs=[row(w, 0), row(d, 0), row(d, 0), row(d, 0), row(d, 0), row(d, 0)],
        out_shape=[jax.ShapeDtypeStruct((t, w), BF16), jax.ShapeDtypeStruct((t, d), BF16),
                   jax.ShapeDtypeStruct((t, d), BF16), jax.ShapeDtypeStruct((t, d), BF16),
                   jax.ShapeDtypeStruct((t, d), F32), jax.ShapeDtypeStruct((t, d), F32)],
        compiler_params=_cp(("arbitrary",)),
    )(hs, proj, y_sgu, proj, proj, x2, modv, w_o_lru_g, w_o_sgu_g, w_out_g, ln1_g, ln1_b)


def _mlp_fwd(x1, modv, w_up_g, w_down_g, ln2_g, ln2_b, target, nb, seq):
    t, d = x1.shape
    nq, _, ns = w_up_g.shape
    tm = min(TM_MLP, seq)
    ts = min(TS_MLP, tm)
    tpb = seq // tm

    def body(x1_ref, mod_ref, wu_hbm, wd_hbm, g2_ref, b2_ref, tg_ref,
             rl_ref, act_ref, h2_ref, dz2_ref, df_ref, st_ref, pb_ref, wu_s, wd_s, acc, sems):
        i = pl.program_id(0)

        @pl.when(i == 0)
        def _():
            _load_weights((wu_hbm, wd_hbm), (wu_s, wd_s), sems)
            st_ref[...] = jnp.zeros_like(st_ref)

        @pl.when(i % tpb == 0)
        def _():
            pb_ref[...] = jnp.zeros_like(pb_ref)

        for sub in range(tm // ts):
            rows = slice(sub * ts, (sub + 1) * ts)
            x1v = x1_ref[rows, :]
            h2 = (x1v * (1.0 + mod_ref[4:5, :]) + mod_ref[3:4, :]).astype(BF16)
            h2_ref[rows, :] = h2
            for k in range(nq):
                cols = slice(k * ns, (k + 1) * ns)
                r = jnp.maximum(jnp.dot(h2, wu_s[k], preferred_element_type=F32), 0.0)
                act = (r * r).astype(BF16)
                rl_ref[rows, cols] = r.astype(BF16)
                act_ref[rows, cols] = act
                part = jnp.dot(act, wd_s[cols, :], preferred_element_type=F32)
                if k == 0:
                    acc[sub] = part
                else:
                    acc[sub] += part
            f = acc[sub]
            xhat, rstd = _ln_stats(ALPHA * x1v + (1.0 + mod_ref[5:6, :]) * f)
            y = xhat * g2_ref[...] + b2_ref[...]
            err = y - tg_ref[rows, :]
            dy = err * (1.0 / d)
            dz2 = _ln_bwd(dy * g2_ref[...], xhat, rstd)
            dz2_ref[rows, :] = dz2
            df_ref[rows, :] = ((1.0 + mod_ref[5:6, :]) * dz2).astype(BF16)
            st_ref[0:1, :] += _colsum(dy * xhat)
            st_ref[1:2, :] += _colsum(dy)
            st_ref[2:3, :] += (0.5 / d) * jnp.sum(_colsum(err * err), axis=1, keepdims=True)
            pb_ref[0:1, :] += _colsum(dz2 * f)

    tok = lambda i: (i, 0)
    return pl.pallas_call(
        body, name="mlp_fwd", grid=(t // tm,),
        in_specs=[pl.BlockSpec((tm, d), tok), pl.BlockSpec((None, 8, d), lambda i: (i // tpb, 0, 0)), _ANY, _ANY,
                  pl.BlockSpec((1, d), lambda i: (0, 0)), pl.BlockSpec((1, d), lambda i: (0, 0)),
                  pl.BlockSpec((tm, d), tok)],
        out_specs=[pl.BlockSpec((tm, nq * ns), tok), pl.BlockSpec((tm, nq * ns), tok),
                   pl.BlockSpec((tm, d), tok), pl.BlockSpec((tm, d), tok), pl.BlockSpec((tm, d), tok),
                   pl.BlockSpec((8, d), lambda i: (0, 0)), pl.BlockSpec((None, 8, d), lambda i: (i // tpb, 0, 0))],
        out_shape=[jax.ShapeDtypeStruct((t, nq * ns), BF16), jax.ShapeDtypeStruct((t, nq * ns), BF16),
                   jax.ShapeDtypeStruct((t, d), BF16),
                   jax.ShapeDtypeStruct((t, d), F32), jax.ShapeDtypeStruct((t, d), BF16),
                   jax.ShapeDtypeStruct((8, d), F32), jax.ShapeDtypeStruct((nb, 8, d), F32)],
        scratch_shapes=[pltpu.VMEM(w_up_g.shape, BF16), pltpu.VMEM(w_down_g.shape, BF16),
                        pltpu.VMEM((tm // ts, ts, d), F32), pltpu.SemaphoreType.DMA((2,))],
        compiler_params=_cp(("arbitrary",)),
    )(x1, modv, w_up_g, w_down_g, ln2_g, ln2_b, target)


def _mlp_bwd(df, up, w_down_g, w_up_g, dz2, x2, mix, modv, ln1_g, ln1_b, nb, seq):
    t, d = x2.shape
    nq, _, ns = w_up_g.shape
    tm = min(TM_MLP, seq)
    ts = min(TS_MLP, tm)
    tpb = seq // tm

    def body(df_ref, rl_ref, wd_hbm, wu_hbm, dz2_ref, x_ref, mix_ref, mod_ref, g1_ref, b1_ref,
             dup_ref, dz1_ref, dmix_ref, st_ref, pb_ref, wd_s, wu_s, acc, sems):
        i = pl.program_id(0)

        @pl.when(i == 0)
        def _():
            _load_weights((wd_hbm, wu_hbm), (wd_s, wu_s), sems)
            st_ref[...] = jnp.zeros_like(st_ref)

        @pl.when(i % tpb == 0)
        def _():
            pb_ref[...] = jnp.zeros_like(pb_ref)

        for sub in range(tm // ts):
            rows = slice(sub * ts, (sub + 1) * ts)
            dfv = df_ref[rows, :]
            for k in range(nq):
                cols = slice(k * ns, (k + 1) * ns)
                dup = (_mm_nt(dfv, wd_s[cols, :]) * (2.0 * rl_ref[rows, cols].astype(F32))).astype(BF16)
                dup_ref[rows, cols] = dup
                part = _mm_nt(dup, wu_s[k])
                if k == 0:
                    acc[sub] = part
                else:
                    acc[sub] += part
            dh2 = acc[sub]
            mix = mix_ref[rows, :]
            xhat, rstd = _ln_stats(ALPHA * x_ref[rows, :] + (1.0 + mod_ref[2:3, :]) * mix)
            x1 = xhat * g1_ref[...] + b1_ref[...]
            dx1 = ALPHA * dz2_ref[rows, :] + dh2 * (1.0 + mod_ref[4:5, :])
            dz1 = _ln_bwd(dx1 * g1_ref[...], xhat, rstd)
            dz1_ref[rows, :] = dz1
            dmix_ref[rows, :] = ((1.0 + mod_ref[2:3, :]) * dz1).astype(BF16)
            st_ref[0:1, :] += _colsum(dx1 * xhat)
            st_ref[1:2, :] += _colsum(dx1)
            pb_ref[0:1, :] += _colsum(dh2 * x1)
            pb_ref[1:2, :] += _colsum(dh2)
            pb_ref[2:3, :] += _colsum(dz1 * mix)

    tok = lambda i: (i, 0)
    return pl.pallas_call(
        body, name="mlp_bwd", grid=(t // tm,),
        in_specs=[pl.BlockSpec((tm, d), tok), pl.BlockSpec((tm, nq * ns), tok), _ANY, _ANY,
                  pl.BlockSpec((tm, d), tok), pl.BlockSpec((tm, d), tok), pl.BlockSpec((tm, d), tok),
                  pl.BlockSpec((None, 8, d), lambda i: (i // tpb, 0, 0)),
                  pl.BlockSpec((1, d), lambda i: (0, 0)), pl.BlockSpec((1, d), lambda i: (0, 0))],
        out_specs=[pl.BlockSpec((tm, nq * ns), tok),
                   pl.BlockSpec((tm, d), tok), pl.BlockSpec((tm, d), tok),
                   pl.BlockSpec((8, d), lambda i: (0, 0)), pl.BlockSpec((None, 8, d), lambda i: (i // tpb, 0, 0))],
        out_shape=[jax.ShapeDtypeStruct((t, nq * ns), BF16),
                   jax.ShapeDtypeStruct((t, d), F32), jax.ShapeDtypeStruct((t, d), BF16),
                   jax.ShapeDtypeStruct((8, d), F32), jax.ShapeDtypeStruct((nb, 8, d), F32)],
        scratch_shapes=[pltpu.VMEM(w_down_g.shape, BF16), pltpu.VMEM(w_up_g.shape, BF16),
                        pltpu.VMEM((tm // ts, ts, d), F32), pltpu.SemaphoreType.DMA((2,))],
        compiler_params=_cp(("arbitrary",), VMEM_LIMIT_MAX),
    )(df, up, w_down_g, w_up_g, dz2, x2, mix, modv, ln1_g, ln1_b)


def _mix_bwd(dmix, proj, y_a, y_b, hs, w_out_g, w_o_lru_g, w_o_sgu_g, seq, after=()):
    t, d = dmix.shape
    w = hs.shape[1]
    nq, d_sgu, ns = w_o_sgu_g.shape
    tm = min(TM_MIX, seq)

    def body(dmix_ref, ga_ref, gb_ref, ya_ref, yb_ref, gl_ref, hs_ref, wo_ref, wl_ref, ws_ref,
             dya_ref, dyb_ref, dga_ref, dgb_ref, dgl_ref, dyl_ref, dys_ref):
        dmerged = _mm_nt(dmix_ref[...], wo_ref[...])
        sa, sb = _sigmoid_t(ga_ref[...].astype(F32)), _sigmoid_t(gb_ref[...].astype(F32))
        dy_a = (dmerged * sa).astype(BF16)
        dy_b = (dmerged * sb).astype(BF16)
        dya_ref[...] = dy_a
        dyb_ref[...] = dy_b
        dga_ref[...] = (dmerged * ya_ref[...].astype(F32) * (sa * (1.0 - sa))).astype(BF16)
        dgb_ref[...] = (dmerged * yb_ref[...].astype(F32) * (sb * (1.0 - sb))).astype(BF16)
        dyap = _mm_nt(dy_a, wl_ref[...])
        gel, dgel = _gelu_and_grad(gl_ref[...].astype(F32))
        dyl_ref[...] = (dyap * gel).astype(BF16)
        dgl_ref[...] = (dyap * hs_ref[...].astype(F32) * dgel).astype(BF16)
        dys = _mm_nt(dy_b[:, 0:ns], ws_ref[0])
        for q in range(1, nq):
            dys = dys + _mm_nt(dy_b[:, q * ns:(q + 1) * ns], ws_ref[q])
        dys_ref[...] = dys

    row = lambda width, col: pl.BlockSpec((tm, width), functools.partial(lambda i, k: (i, k), k=col))
    full = lambda shape: pl.BlockSpec(shape, lambda i: (0,) * len(shape))
    return pl.pallas_call(
        _ordered(body, 10, after), name="mix_bwd", grid=(t // tm,),
        in_specs=[row(d, 0), row(d, 4), row(d, 5), row(d, 0), row(d, 0), row(w, 1), row(w, 0),
                  full(w_out_g.shape), full(w_o_lru_g.shape), full(w_o_sgu_g.shape)] + [_ANY] * len(after),
        out_specs=[row(d, 0), row(d, 0), row(d, 0), row(d, 0), row(w, 0), row(w, 0), row(d_sgu, 0)],
        out_shape=[jax.ShapeDtypeStruct((t, d), BF16), jax.ShapeDtypeStruct((t, d), BF16),
                   jax.ShapeDtypeStruct((t, d), BF16), jax.ShapeDtypeStruct((t, d), BF16),
                   jax.ShapeDtypeStruct((t, w), BF16), jax.ShapeDtypeStruct((t, w), BF16),
                   jax.ShapeDtypeStruct((t, d_sgu), F32)],
        compiler_params=_cp(("arbitrary",)),
    )(dmix, proj, proj, y_a, y_b, proj, hs, w_out_g, w_o_lru_g, w_o_sgu_g, *after)


def _sgu_bwd(proj, dys, w_sp, b_sp_t, ln_g, ln_b, after=()):
    t = proj.shape[0]
    d_sgu = SGU_GROUPS * HEAD
    tm = min(TM_SGU, t)
    nblk = tm // HEAD
    specs, n_piece = _sgu_specs(tm, d_sgu)

    def body(*refs):
        u = jnp.concatenate([r[...] for r in refs[:n_piece]], axis=1).astype(F32)
        v = jnp.concatenate([r[...] for r in refs[n_piece:2 * n_piece]], axis=1).astype(F32)
        dys_ref, w_ref, bt_ref, g_ref, b_ref, du_ref, dv_ref, dw_ref, st_ref, dbt_ref, dvn_s = refs[2 * n_piece:]

        @pl.when(pl.program_id(0) == 0)
        def _():
            dw_ref[...] = jnp.zeros_like(dw_ref)
            st_ref[...] = jnp.zeros_like(st_ref)
            dbt_ref[...] = jnp.zeros_like(dbt_ref)

        ug, dug_du = _gelu_and_grad(u)
        vg, dvg_dv = _gelu_and_grad(v)
        xhat, rstd = _ln_stats(vg)
        vn = (xhat * g_ref[...] + b_ref[...]).astype(BF16)
        dys_v = dys_ref[...]
        mask = _sgu_mask()
        for g in range(SGU_GROUPS):
            wm = jnp.where(mask, w_ref[g], 0.0).astype(BF16)
            cols = slice(g * HEAD, (g + 1) * HEAD)
            dw_g = jnp.zeros((HEAD, HEAD), F32)
            db_g = jnp.zeros((HEAD, 1), F32)
            for n in range(nblk):
                rows = slice(n * HEAD, (n + 1) * HEAD)
                vn_blk = vn[rows, cols]
                mixed = jnp.dot(wm, vn_blk, preferred_element_type=F32) + bt_ref[:, g:g + 1]
                dy_blk = dys_v[rows, cols]
                du_ref[rows, cols] = (dy_blk * mixed * dug_du[rows, cols]).astype(BF16)
                dmx = dy_blk * ug[rows, cols]
                dvn_s[rows, cols] = _mm_tn(wm, dmx)
                dw_g = dw_g + _mm_nt(dmx, vn_blk)
                db_g = db_g + jnp.sum(dmx, axis=1, keepdims=True)
            dw_ref[g] += jnp.where(mask, dw_g, 0.0)
            dbt_ref[:, g:g + 1] += db_g
        dvn = dvn_s[...]
        st_ref[0:1, :] += _colsum(dvn * xhat)
        st_ref[1:2, :] += _colsum(dvn)
        dv_ref[...] = (_ln_bwd(dvn * g_ref[...], xhat, rstd) * dvg_dv).astype(BF16)

    full = lambda shape: pl.BlockSpec(shape, lambda i: (0,) * len(shape))
    tok = pl.BlockSpec((tm, d_sgu), lambda i: (i, 0))
    return pl.pallas_call(
        _ordered(body, 2 * n_piece + 5, after), name="sgu_bwd", grid=(t // tm,),
        in_specs=specs + [tok, full(w_sp.shape), full(b_sp_t.shape), full(ln_g.shape), full(ln_b.shape)]
        + [_ANY] * len(after),
        out_specs=[tok, tok, full(w_sp.shape), full((8, d_sgu)), full((HEAD, HEAD))],
        out_shape=[jax.ShapeDtypeStruct((t, d_sgu), BF16), jax.ShapeDtypeStruct((t, d_sgu), BF16),
                   jax.ShapeDtypeStruct(w_sp.shape, F32), jax.ShapeDtypeStruct((8, d_sgu), F32),
                   jax.ShapeDtypeStruct((HEAD, HEAD), F32)],
        scratch_shapes=[pltpu.VMEM((tm, d_sgu), F32)],
        compiler_params=_cp(("arbitrary",)),
    )(*([proj] * (2 * n_piece)), dys, w_sp, b_sp_t, ln_g, ln_b, *after)


def _lru_bwd(proj, hs, e, dyl, lru_w, nb, seq, after=()):
    t = proj.shape[0]
    w = LRU_HEADS * HEAD
    w_conv, b_conv, w_a, b_a, w_x, b_x, lam = lru_w

    def body(x_ref, hs_ref, e_ref, dy_ref, wc_ref, bc_ref, wa_ref, ba_ref, wx_ref, bx_ref, lam_ref,
             dxl_ref, dwa_ref, dwx_ref, st_ref):
        @pl.when(pl.program_id(1) == 0)
        def _():
            dwa_ref[...] = jnp.zeros_like(dwa_ref)
            dwx_ref[...] = jnp.zeros_like(dwx_ref)
            st_ref[...] = jnp.zeros_like(st_ref)

        xl = x_ref[...].astype(F32)
        xc, r, gi, big_l, a, m2 = _lru_gates(xl, wc_ref, bc_ref, wa_ref, ba_ref, wx_ref, bx_ref, lam_ref)
        inv_mult = lax.rsqrt(m2)
        mult = m2 * inv_mult
        dh = dy_ref[...].astype(F32) + _shift_up(e_ref[...], 1)
        da = dh * _shift_down(hs_ref[...].astype(F32), 1)
        dmult = dh * (gi * xc)
        d_i = dh * (mult * xc)
        dxc = dh * (mult * gi)
        dla = a * (da - dmult * (a * inv_mult))
        dr = dla * big_l
        d_big_l = _colsum(dla * r)
        dra = dr * (r * (1.0 - r))
        dia = d_i * (gi * (1.0 - gi))
        dwa_ref[...] += _mm_tn(xc, dra)
        dwx_ref[...] += _mm_tn(xc, dia)
        dxc = dxc + _mm_nt(dra, wa_ref[...]) + _mm_nt(dia, wx_ref[...])
        dxl = wc_ref[CONV_WIDTH - 1:CONV_WIDTH, :] * dxc
        st_ref[4 + CONV_WIDTH - 1:4 + CONV_WIDTH, :] += _colsum(dxc * xl)
        for k in range(CONV_WIDTH - 1):
            ahead = _shift_up(dxc, CONV_WIDTH - 1 - k)
            dxl = dxl + wc_ref[k:k + 1, :] * ahead
            st_ref[4 + k:5 + k, :] += _colsum(ahead * xl)
        dxl_ref[...] = dxl.astype(BF16)
        st_ref[0:1, :] += _colsum(dra)
        st_ref[1:2, :] += _colsum(dia)
        st_ref[2:3, :] += d_big_l * (LRU_C * _sigmoid(-lam_ref[...]))
        st_ref[3:4, :] += _colsum(dxc)

    col = lambda hd, b: (0, hd)
    head = lambda hd, b: (hd, 0, 0)
    tok = lambda hd, b: (b, hd)
    seq_blk = pl.BlockSpec((seq, HEAD), tok)
    return pl.pallas_call(
        _ordered(body, 11, after), name="lru_bwd", grid=(LRU_HEADS, nb),
        in_specs=[seq_blk, seq_blk, seq_blk, seq_blk,
                  pl.BlockSpec((CONV_WIDTH, HEAD), col), pl.BlockSpec((1, HEAD), col),
                  pl.BlockSpec((None, HEAD, HEAD), head), pl.BlockSpec((1, HEAD), col),
                  pl.BlockSpec((None, HEAD, HEAD), head), pl.BlockSpec((1, HEAD), col),
                  pl.BlockSpec((1, HEAD), col)] + [_ANY] * len(after),
        out_specs=[seq_blk, pl.BlockSpec((None, HEAD, HEAD), head), pl.BlockSpec((None, HEAD, HEAD), head),
                   pl.BlockSpec((8, HEAD), col)],
        out_shape=[jax.ShapeDtypeStruct((t, w), BF16), jax.ShapeDtypeStruct((LRU_HEADS, HEAD, HEAD), F32),
                   jax.ShapeDtypeStruct((LRU_HEADS, HEAD, HEAD), F32), jax.ShapeDtypeStruct((8, w), F32)],
        compiler_params=_cp(("arbitrary", "arbitrary")),
    )(proj, hs, e, dyl, w_conv, b_conv, w_a, b_a, w_x, b_x, lam, *after)


def _weight_grad(a, g, col_shards, name, after=()):
    t, k = a.shape
    n = g.shape[1]
    tt = min(TT_DW, t)
    tk = k if k <= 1536 else 1024
    ns = n // N_CHIPS if col_shards else n
    narrow = col_shards and ns < 512
    tn = n if narrow else min(ns, 768 if ns % 768 == 0 else 1024)
    while ns % tn and not narrow:
        tn //= 2
    per = max(ns // tn, 1)

    def body(a_ref, g_ref, o_ref):
        @pl.when(pl.program_id(2) == 0)
        def _():
            o_ref[...] = jnp.zeros_like(o_ref)

        res = _mm_tn(a_ref[...], g_ref[...])
        if narrow:
            for q in range(N_CHIPS):
                o_ref[q] += res[:, q * ns:(q + 1) * ns]
        else:
            o_ref[...] += res

    if narrow:
        out_spec = pl.BlockSpec((N_CHIPS, tk, ns), lambda i, j, s: (0, i, 0))
        out_shape = jax.ShapeDtypeStruct((N_CHIPS, k, ns), F32)
    elif col_shards:
        out_spec = pl.BlockSpec((None, tk, tn), lambda i, j, s: (j // per, i, j % per))
        out_shape = jax.ShapeDtypeStruct((N_CHIPS, k, ns), F32)
    else:
        out_spec = pl.BlockSpec((tk, tn), lambda i, j, s: (i, j))
        out_shape = jax.ShapeDtypeStruct((k, n), F32)
    return pl.pallas_call(
        _ordered(body, 2, after), name=name, grid=(k // tk, n // tn, t // tt),
        in_specs=[pl.BlockSpec((tt, tk), lambda i, j, s: (s, i)), pl.BlockSpec((tt, tn), lambda i, j, s: (s, j))]
        + [_ANY] * len(after),
        out_specs=out_spec, out_shape=out_shape,
        compiler_params=_cp(("arbitrary", "arbitrary", "arbitrary")),
    )(a, g, *after)


def _input_grad(dproj, ws, slots, dz1, x2, modv, nb, seq, after=()):
    t, d = x2.shape
    nq = len(ws)
    ns = ws[0].shape[1]
    tm = min(TM_DH, seq)
    ts = min(TS_MLP, tm)
    tpb = seq // tm

    def body(slot_ref, dp_ref, *refs):
        w_hbm = refs[:nq]
        dz1_ref, x_ref, mod_ref, gx_ref, db_ref, pb_ref, w_s, acc, sems = refs[nq:]
        i = pl.program_id(0)

        @pl.when(i == 0)
        def _():
            _load_weights(w_hbm, [w_s.at[slot_ref[k]] for k in range(nq)], sems)
            db_ref[...] = jnp.zeros_like(db_ref)

        @pl.when(i % tpb == 0)
        def _():
            pb_ref[...] = jnp.zeros_like(pb_ref)

        for sub in range(tm // ts):
            rows = slice(sub * ts, (sub + 1) * ts)
            for q in range(nq):
                dp = dp_ref[rows, q * ns:(q + 1) * ns]
                part = _mm_nt(dp, w_s[q])
                if q == 0:
                    acc[sub] = part
                else:
                    acc[sub] += part
                db_ref[q, 0:1, :] += _colsum(dp.astype(F32))
            dh = acc[sub]
            gx_ref[rows, :] = ALPHA * dz1_ref[rows, :] + dh * (1.0 + mod_ref[1:2, :])
            pb_ref[0:1, :] += _colsum(dh * x_ref[rows, :])
            pb_ref[1:2, :] += _colsum(dh)

    tok = lambda i, s: (i, 0)
    in_specs = [pl.BlockSpec((tm, nq * ns), tok)] + [_ANY] * nq
    in_specs += [pl.BlockSpec((tm, d), tok), pl.BlockSpec((tm, d), tok),
                 pl.BlockSpec((None, 8, d), lambda i, s: (i // tpb, 0, 0))] + [_ANY] * len(after)
    return pl.pallas_call(
        _ordered(body, 5 + nq, after), name="input_grad",
        grid_spec=pltpu.PrefetchScalarGridSpec(
            num_scalar_prefetch=1, grid=(t // tm,), in_specs=in_specs,
            out_specs=[pl.BlockSpec((tm, d), tok), pl.BlockSpec((nq, 8, ns), lambda i, s: (0, 0, 0)),
                       pl.BlockSpec((None, 8, d), lambda i, s: (i // tpb, 0, 0))],
            scratch_shapes=[pltpu.VMEM((nq, d, ns), BF16), pltpu.VMEM((tm // ts, ts, d), F32),
                            pltpu.SemaphoreType.DMA((nq,))]),
        out_shape=[jax.ShapeDtypeStruct((t, d), F32), jax.ShapeDtypeStruct((nq, 8, ns), F32),
                   jax.ShapeDtypeStruct((nb, 8, d), F32)],
        compiler_params=_cp(("arbitrary",)),
    )(slots, dproj, *ws, dz1, x2, modv, *after)


def _rows128(v):
    flat = v.reshape(-1, HEAD)
    pad = (-flat.shape[0]) % 8
    return jnp.pad(flat, ((0, pad), (0, 0))) if pad else flat


def kernel(x, c, w_ada, b_ada, w_in, b_in, w_conv, b_conv, w_rg_a, b_rg_a, w_rg_x, b_rg_x, lru_lambda, w_sp, b_sp, ln_v_g, ln_v_b, w_o_lru, w_o_sgu, w_out, ln1_g, ln1_b, w_up, w_down, ln2_g, ln2_b, loss_target, m_w_ada, m_b_ada, m_w_in, m_b_in, m_w_conv, m_b_conv, m_w_rg_a, m_b_rg_a, m_w_rg_x, m_b_rg_x, m_lru_lambda, m_w_sp, m_b_sp, m_ln_v_g, m_ln_v_b, m_w_o_lru, m_w_o_sgu, m_w_out, m_ln1_g, m_ln1_b, m_w_up, m_w_down, m_ln2_g, m_ln2_b, v_w_ada, v_b_ada, v_w_in, v_b_in, v_w_conv, v_b_conv, v_w_rg_a, v_b_rg_a, v_w_rg_x, v_b_rg_x, v_lru_lambda, v_w_sp, v_b_sp, v_ln_v_g, v_ln_v_b, v_w_o_lru, v_w_o_sgu, v_w_out, v_ln1_g, v_ln1_b, v_w_up, v_w_down, v_ln2_g, v_ln2_b):
    given = dict(locals())
    nb, seq, d = x.shape
    t = nb * seq
    w_lru = LRU_HEADS * HEAD
    d_sgu = SGU_GROUPS * HEAD
    xi, yi, ci = lax.axis_index("x"), lax.axis_index("y"), lax.axis_index("c")
    chip = 2 * xi + yi
    dev = 2 * chip + ci
    cidx = jnp.reshape(ci, (1,)).astype(jnp.int32)

    x2 = x.reshape(t, d)
    target = loss_target.reshape(t, d)

    big = ["w_in", "w_o_lru", "w_o_sgu", "w_out", "w_up", "w_down"]
    shards_a = [w_in[0].astype(BF16)]
    shards_b = [given[n][0].astype(BF16) for n in big[1:]]
    pidx = jnp.reshape(chip, (1,)).astype(jnp.int32)

    c_rows = _rows128(c)
    wconv_rows = _rows128(w_conv[0])
    slab0 = _all_gather_small(jnp.concatenate([c_rows, wconv_rows], axis=0), "gather_c_wconv")
    slab0 = slab0.reshape(N_DEV, -1, HEAD)
    c_all = slab0[:, :c_rows.shape[0]].reshape(N_DEV * nb, d)
    n_wc = CONV_WIDTH * (w_lru // N_CHIPS) // HEAD
    wc = slab0[0::2, c_rows.shape[0]:c_rows.shape[0] + n_wc].reshape(N_CHIPS, CONV_WIDTH, w_lru // N_CHIPS)
    w_conv_full = jnp.transpose(wc, (1, 0, 2)).reshape(CONV_WIDTH, w_lru)

    n_ada = w_ada.shape[2]
    b_ada_cols = lax.dynamic_slice(b_ada, (0, chip * n_ada), (1, n_ada))
    mod_cols = _ada_fwd(c_all, w_ada[0], b_ada_cols)
    half = (N_DEV * nb) // 2
    mod_half = lax.dynamic_slice(mod_cols, (ci * half, 0), (half, n_ada))
    mod_g = _all_gather_small(mod_half, "gather_mod").reshape(N_CHIPS, 2, half, n_ada)
    mod_all = jnp.transpose(mod_g, (1, 2, 0, 3)).reshape(N_DEV * nb, N_CHIPS * n_ada)
    mod_loc = lax.dynamic_slice(mod_all, (dev * nb, 0), (nb, N_CHIPS * n_ada)).reshape(nb, 6, d)
    modv = jnp.pad(mod_loc, ((0, 0), (0, 2), (0, 0)))

    lru_w = (w_conv_full, b_conv, w_rg_a[0], b_rg_a, w_rg_x[0], b_rg_x, lru_lambda)
    b_sp_t = jnp.transpose(b_sp[0])

    land = lambda s: jax.ShapeDtypeStruct((N_CHIPS,) + s.shape, s.dtype)
    sds = lambda s: jax.ShapeDtypeStruct(s.shape, s.dtype)
    started_a = _split_start(shards_a, [sds(shards_a[0])] * 2, _peer_gather_copies((0, 1)), 2, "gather_w_in_near_start",
                             after=(modv,))
    shards_b, shards_c = shards_b[:3], shards_b[3:]

    ids = lambda *v: jnp.stack(v).astype(jnp.int32)
    modv_t = modv + started_a[-1][0:1, 0:1]
    proj, h = _proj_fwd(x2, modv_t, [started_a[2]], ids(chip), b_in, seq, "proj_fwd_own")
    own_a, lands_a = _split_wait(started_a, 1, _peer_gather_copies((0, 1)), "gather_w_in_near_wait", after=(proj,))
    started_f = _split_start(own_a, [sds(own_a[0])], _far_gather_copies, 1, "gather_w_in_far_start", after=(lands_a[0],))
    started_b = _split_start(shards_b, [land(s) for s in shards_b], _gather_copies, 3 * len(shards_b),
                             "gather_w_mix_start", after=(started_f[-1],))
    started_c = _split_start(shards_c, [land(s) for s in shards_c], _gather_copies, 3 * len(shards_c),
                             "gather_w_mlp_start", after=(started_b[-1],))
    modv_t = modv + started_c[-1][0:1, 0:1]
    (proj,) = _proj_fwd(x2, modv_t, lands_a, ids(chip ^ 1, chip ^ 2), b_in, seq, "proj_fwd_near", proj_in=proj)
    own_a, land_f = _split_wait(started_f, 1, _far_gather_copies, "gather_w_in_far_wait", after=(proj,))
    (proj,) = _proj_fwd(x2, modv, land_f, ids(chip ^ 3), b_in, seq, "proj_fwd_far", proj_in=proj)
    w_in_shards, w_in_chips = own_a + lands_a + land_f, ids(chip, chip ^ 1, chip ^ 2, chip ^ 3)
    a, inp = _lru_prep(proj, lru_w, nb, seq)
    a3 = a.reshape(nb, seq, w_lru)
    hs = _scan(a3, inp.reshape(nb, seq, w_lru), False, "lru_scan", BF16).reshape(t, w_lru)
    y_sgu = _sgu_fwd(proj, w_sp[0], b_sp_t, ln_v_g, ln_v_b)
    shards_b, lands_b = _split_wait(started_b, len(shards_b), _gather_copies, "gather_w_mix_wait", after=(hs, y_sgu))
    w_o_lru_g, w_o_sgu_g, w_out_g = _fill_own_slot(lands_b, shards_b, pidx, ["own_" + n for n in big[1:4]])
    w_o_lru_g = w_o_lru_g.reshape(w_lru, d)
    w_out_g = w_out_g.reshape(d, d)
    yap, y_a, y_b, merged, mix, x1 = _mix_fwd(hs, proj, y_sgu, x2, modv, w_o_lru_g, w_o_sgu_g, w_out_g, ln1_g, ln1_b, seq)
    shards_c, lands_c = _split_wait(started_c, len(shards_c), _gather_copies, "gather_w_mlp_wait", after=(x1,))
    w_up_g, w_down_g = _fill_own_slot(lands_c, shards_c, pidx, ["own_" + n for n in big[4:]])
    w_down_g = w_down_g.reshape(-1, d)
    up, act, h2, dz2, df, st2, pb2 = _mlp_fwd(x1, modv, w_up_g, w_down_g, ln2_g, ln2_b, target, nb, seq)

    part = {}

    def to_sibling_start(group, tag, after=()):
        g4 = []
        for n in group:
            shard = given[n].shape[1:]
            g4.append(part[n].reshape(N_CHIPS, 2, shard[0] // 2, shard[1]))
        shapes = [jax.ShapeDtypeStruct((N_CHIPS,) + g.shape[2:], F32) for g in g4]
        return _split_start(g4, shapes, _to_sibling_copies, len(g4), "grads_to_sibling_start_" + tag, after)

    def to_chips_start(group, started, tag, after=()):
        g4, recv = _split_wait(started, len(group), _to_sibling_copies, "grads_to_sibling_wait_" + tag, after)
        own4 = [_add_own_half(g4[k], recv[k], cidx, "grad_pair_sum_" + n) for k, n in enumerate(group)]
        shapes = [jax.ShapeDtypeStruct((3,) + o.shape[1:], BF16) for o in own4]
        return _split_start(own4, shapes, _chip_exchange_copies, 3 * len(own4), "grads_chip_exchange_start_" + tag)

    def chips_finish(group, started, tag, after=()):
        own4, slots = _split_wait(started, len(group), _chip_exchange_copies, "grads_chip_exchange_wait_" + tag, after)
        return [_sum_own_and_peers(own4[k], slots[k], pidx, "grad_chip_sum_" + n) for k, n in enumerate(group)]

    dup, dz1, dmix, st1, pb1 = _mlp_bwd(df, up, w_down_g, w_up_g, dz2, x2, mix, modv, ln1_g, ln1_b, nb, seq)
    group1 = ["w_up", "w_down"]
    part["w_up"] = _weight_grad(h2, dup, True, "grad_w_up")
    part["w_down"] = _weight_grad(act, df, False, "grad_w_down")
    sib1 = to_sibling_start(group1, "mlp")
    dy_a, dy_b, dga, dgb, dgl, dyl, dys = _mix_bwd(dmix, proj, y_a, y_b, hs, w_out_g, w_o_lru_g, w_o_sgu_g, seq,
                                                   after=(sib1[-1],))
    group2 = ["w_o_lru", "w_o_sgu", "w_out"]
    part["w_o_lru"] = _weight_grad(yap, dy_a, False, "grad_w_o_lru")
    part["w_o_sgu"] = _weight_grad(y_sgu, dy_b, True, "grad_w_o_sgu")
    part["w_out"] = _weight_grad(merged, dmix, False, "grad_w_out")
    chips1 = to_chips_start(group1, sib1, "mlp", after=(dys, part["w_o_lru"], part["w_o_sgu"], part["w_out"]))
    sib2 = to_sibling_start(group2, "mix", after=(chips1[-1],))
    du, dv, g_w_sp, st_sgu, g_b_sp_t = _sgu_bwd(proj, dys, w_sp[0], b_sp_t, ln_v_g, ln_v_b, after=(sib2[-1],))
    dyl3 = dyl.reshape(nb, seq, w_lru)
    e = _scan(a3, dyl3, True, "lru_scan_bwd", F32).reshape(t, w_lru)
    chips2 = to_chips_start(group2, sib2, "mix", after=(e, du))
    dxl, g_w_rg_a, g_w_rg_x, st_lru = _lru_bwd(proj, hs, e, dyl, lru_w, nb, seq, after=(chips2[-1],))
    dproj = jnp.concatenate([dxl, dgl, du, dv, dga, dgb], axis=1)

    didx = jnp.reshape(dev, (1,)).astype(jnp.int32)
    early = [
        ("w_conv", st_lru[4:8]), ("b_conv", st_lru[3]), ("w_rg_a", g_w_rg_a), ("b_rg_a", st_lru[0]),
        ("w_rg_x", g_w_rg_x), ("b_rg_x", st_lru[1]), ("lru_lambda", st_lru[2]), ("w_sp", g_w_sp),
        ("b_sp", jnp.transpose(g_b_sp_t[:, :SGU_GROUPS])), ("ln_v_g", st_sgu[0]), ("ln_v_b", st_sgu[1]),
        ("ln1_g", st1[0]), ("ln1_b", st1[1]), ("ln2_g", st2[0]), ("ln2_b", st2[1]),
    ]
    pieces_e = [_rows128(v) for _, v in early]
    slab_e = jnp.concatenate(pieces_e, axis=0)
    slab_e = jnp.pad(slab_e, ((0, (-slab_e.shape[0]) % TR_EW), (0, 0)))
    small_st = _split_start([slab_e], [jax.ShapeDtypeStruct((N_DEV,) + slab_e.shape, F32)], _all_devices_copies, N_DEV - 1,
                            "small_grads_start")

    group3 = ["w_in"]
    part["w_in"] = _weight_grad(h, dproj, True, "grad_w_in", after=(small_st[-1],))
    sib3 = to_sibling_start(group3, "in")
    chips3 = to_chips_start(group3, sib3, "in")
    grad_x2, g_b_in4, pb0 = _input_grad(dproj, w_in_shards, w_in_chips, dz1, x2, modv, nb, seq, after=(chips3[-1],))
    halves12 = chips_finish(group1, chips1, "mlp", after=(grad_x2,)) + chips_finish(group2, chips2, "mix", after=(grad_x2,))
    swap12 = _split_start(halves12, [jax.ShapeDtypeStruct(hv.shape, F32) for hv in halves12], _swap_copies, len(halves12),
                          "grads_swap_start")
    loss = lax.psum(st2[2, 0] + swap12[-1][0, 0], ("x", "y", "c"))
    grads = {}

    dmod_loc = jnp.stack([pb0[:, 1], pb0[:, 0], pb1[:, 2], pb1[:, 1], pb1[:, 0], pb2[:, 0]], axis=1)
    late = [("dmod", dmod_loc), ("b_in", g_b_in4[:, 0])]
    pieces_l = [_rows128(v) for _, v in late]
    slab_l = jnp.concatenate(pieces_l, axis=0)
    gathered = _all_gather_small(slab_l, "gather_small_grads", after=(swap12[-1],)).reshape(N_DEV, slab_l.shape[0], HEAD)
    rows_dmod = dmod_loc.size // HEAD
    dmod_all = gathered[:, :rows_dmod].reshape(N_DEV * nb, 6 * d)
    grads["b_in"] = _sum_slots(gathered[:, rows_dmod:], "grad_b_in_sum").reshape(1, -1)

    (slab_e,), (lands_e,) = _split_wait(small_st, 1, _all_devices_copies, "small_grads_wait", after=(gathered,))
    summed = _sum_devices(lands_e, slab_e, didx, "small_grad_sum")
    off = 0
    for (n, v), piece in zip(early, pieces_e):
        grads[n] = summed[off:off + v.size // HEAD].reshape(v.shape)
        off += piece.shape[0]

    mine12, theirs12 = _split_wait(swap12, len(halves12), _swap_copies, "grads_swap_wait", after=(summed,))
    (mine3,) = chips_finish(group3, chips3, "in", after=(summed,))
    (theirs3,) = _exchange([mine3], [jax.ShapeDtypeStruct(mine3.shape, F32)], _swap_copies, 1, "grads_swap_w_in")
    mine = dict(zip(group1 + group2 + group3, mine12 + [mine3]))
    theirs = dict(zip(group1 + group2 + group3, theirs12 + [theirs3]))

    dmod_cols = lax.dynamic_slice(dmod_all, (0, chip * n_ada), (N_DEV * nb, n_ada))
    grads["w_ada"], grads["b_ada"] = _ada_bwd(c_all, dmod_all, dmod_cols)
    n_wcs = w_lru // N_CHIPS
    grads["w_conv"] = lax.dynamic_slice(grads["w_conv"], (0, chip * n_wcs), (CONV_WIDTH, n_wcs))

    names = ['w_ada', 'b_ada', 'w_in', 'b_in', 'w_conv', 'b_conv', 'w_rg_a', 'b_rg_a', 'w_rg_x', 'b_rg_x', 'lru_lambda',
             'w_sp', 'b_sp', 'ln_v_g', 'ln_v_b', 'w_o_lru', 'w_o_sgu', 'w_out', 'ln1_g', 'ln1_b', 'w_up', 'w_down',
             'ln2_g', 'ln2_b']
    two_d = lambda v: v.reshape(-1, v.shape[-1])
    done = {}
    small_names = [n for n in names if n not in big and n != "w_ada"]
    small_out = _adamw_many([(two_d(given[n]), two_d(grads[n].reshape(given[n].shape)), two_d(given["m_" + n]),
                              two_d(given["v_" + n])) for n in small_names], "adamw_small")
    for n, res in zip(small_names, small_out):
        done[n] = (grads[n],) + tuple(res)
    for n in big + ["w_ada"]:
        w2, m2, v2 = two_d(given[n]), two_d(given["m_" + n]), two_d(given["v_" + n])
        if n in big:
            done[n] = _adamw_halves(w2, mine[n], theirs[n], m2, v2, cidx, "adamw_" + n)
        else:
            done[n] = (grads[n],) + tuple(_adamw(w2, two_d(grads[n]), m2, v2, "adamw_" + n))
    outs = [[done[n][k].reshape(given[n].shape) for n in names] for k in range(4)]
    return (loss, grad_x2.reshape(nb, seq, d), *outs[0], *outs[1], *outs[2], *outs[3])
```

```python
import functools
import math

import jax
import jax.numpy as jnp
from jax import lax
from jax.experimental import pallas as pl
from jax.experimental.pallas import tpu as pltpu

F32 = jnp.float32
BF16 = jnp.bfloat16
MESH = pl.DeviceIdType.MESH

N_CHIPS = 4
N_DEV = 8
LRU_HEADS = 10
HEAD = 128
SGU_GROUPS = 6
SGU_CHUNK = 64
CONV_WIDTH = 4
LRU_C = 8.0
ALPHA = 2.0 ** 0.25
LN_EPS = 1e-5
ADAM_LR, ADAM_B1, ADAM_B2, ADAM_EPS, ADAM_WD, ADAM_STEP = 0.001, 0.9, 0.999, 1e-08, 0.01, 10

VMEM_LIMIT = 56 * 1024 * 1024
VMEM_LIMIT_MAX = 62 * 1024 * 1024
TM_PROJ = 1024
TM_MIX = 256
TM_MLP = 512
TS_MLP = 256
TM_SGU = 512
TM_DH = 512
TT_DW = 2048
TC_SCAN = 256
TR_EW = 256


def _cp(sem=None, limit=None):
    return pltpu.CompilerParams(dimension_semantics=sem, vmem_limit_bytes=limit or VMEM_LIMIT)


def _mm(a, b):
    return jnp.dot(a.astype(BF16), b.astype(BF16), preferred_element_type=F32)


def _mm_nt(a, b):
    return lax.dot_general(a.astype(BF16), b.astype(BF16), (((1,), (1,)), ((), ())), preferred_element_type=F32)


def _mm_tn(a, b):
    return lax.dot_general(a.astype(BF16), b.astype(BF16), (((0,), (0,)), ((), ())), preferred_element_type=F32)


def _sigmoid(x):
    return 1.0 / (1.0 + jnp.exp(-x))


def _sigmoid_t(x):
    return 0.5 * jnp.tanh(0.5 * x) + 0.5


_GELU_K = math.sqrt(2.0 / math.pi)


def _gelu(x):
    t = jnp.tanh(_GELU_K * (x + 0.044715 * (x * x * x)))
    return 0.5 * x * (1.0 + t)


def _gelu_and_grad(x):
    x2 = x * x
    t = jnp.tanh(_GELU_K * (x + 0.044715 * (x2 * x)))
    g = 0.5 * x * (1.0 + t)
    dg = 0.5 * (1.0 + t) + 0.5 * x * (1.0 - t * t) * (_GELU_K * (1.0 + 3.0 * 0.044715 * x2))
    return g, dg


def _ln_stats(z):
    mu = jnp.mean(z, axis=-1, keepdims=True)
    zc = z - mu
    var = jnp.mean(zc * zc, axis=-1, keepdims=True)
    rstd = lax.rsqrt(var + LN_EPS)
    return zc * rstd, rstd


def _ln_bwd(dxh, xhat, rstd):
    m1 = jnp.mean(dxh, axis=-1, keepdims=True)
    m2 = jnp.mean(dxh * xhat, axis=-1, keepdims=True)
    return rstd * (dxh - m1 - xhat * m2)


def _colsum(v):
    return jnp.sum(v, axis=0, keepdims=True)


def _shift_down(v, j):
    if j == 0:
        return v
    rows = lax.broadcasted_iota(jnp.int32, v.shape, 0)
    return jnp.where(rows >= j, pltpu.roll(v, j, 0), 0.0)


def _shift_up(v, j):
    if j == 0:
        return v
    n = v.shape[0]
    rows = lax.broadcasted_iota(jnp.int32, v.shape, 0)
    return jnp.where(rows < n - j, pltpu.roll(v, n - j, 0), 0.0)


def _load_weights(srcs, dsts, sems):
    cps = [pltpu.make_async_copy(s, dd, sems.at[k]) for k, (s, dd) in enumerate(zip(srcs, dsts))]
    for cp in cps:
        cp.start()
    for cp in cps:
        cp.wait()


def _my_pos():
    return lax.axis_index("x"), lax.axis_index("y"), lax.axis_index("c")


def _all_gather_small(v, name, after=()):
    m_per, n = v.shape

    def body(x_ref, out_ref, send_sems, recv_sems, local_sem):
        x, y, c = _my_pos()
        me, sibling = (x, y, c), (x, y, 1 - c)
        chips = [(1 - x, y), (x, 1 - y), (1 - x, 1 - y)]

        def rows(px, py, pc):
            return out_ref.at[pl.ds((4 * px + 2 * py + pc) * m_per, m_per), :]

        def copy(k, block, to, src=None):
            return pltpu.make_async_remote_copy(
                src_ref=rows(*block) if src is None else src, dst_ref=rows(*block),
                send_sem=send_sems.at[k], recv_sem=recv_sems.at[k], device_id=to, device_id_type=MESH)

        mine = pltpu.make_async_copy(x_ref, rows(*me), local_sem)
        mine.start()
        first = [copy(0, me, sibling, src=x_ref)]
        first += [copy(1 + j, me, (*chip, c), src=x_ref) for j, chip in enumerate(chips)]
        for cp in first:
            cp.start()
        passed = [copy(4 + j, (*chip, c), sibling) for j, chip in enumerate(chips)]
        for j, chip in enumerate(chips):
            copy(1 + j, (*chip, c), me).wait_recv()
            passed[j].start()
        copy(0, sibling, me).wait_recv()
        for j, chip in enumerate(chips):
            copy(4 + j, (*chip, 1 - c), me).wait_recv()
        for cp in first + passed:
            cp.wait_send()
        mine.wait()

    return pl.pallas_call(
        _ordered(body, 1, after), name=name,
        out_shape=jax.ShapeDtypeStruct((N_DEV * m_per, n), v.dtype),
        in_specs=[pl.BlockSpec(memory_space=pltpu.VMEM)] + [pl.BlockSpec(memory_space=pl.ANY)] * len(after),
        out_specs=pl.BlockSpec(memory_space=pltpu.VMEM),
        scratch_shapes=[pltpu.SemaphoreType.DMA((7,)), pltpu.SemaphoreType.DMA((7,)), pltpu.SemaphoreType.DMA],
        compiler_params=pltpu.CompilerParams(vmem_limit_bytes=VMEM_LIMIT),
    )(v, *after)


_HBM = pl.BlockSpec(memory_space=pltpu.HBM)
_ANY = pl.BlockSpec(memory_space=pl.ANY)
_SEM = pl.BlockSpec(memory_space=pltpu.SEMAPHORE)
_EFFECT = pltpu.SideEffectType.DATAFLOW_SIDE_EFFECTING


def _ordered(body, n_in, after):
    k = len(after)
    if not k:
        return body
    return lambda *refs: body(*refs[:n_in], *refs[n_in + k:])


def _gather_copies(ins, lands, send_sems, recv_sems):
    x, y, c = _my_pos()
    p = 2 * x + y
    peers = [(x, 1 - y), (1 - x, y), (1 - x, 1 - y)]
    sends, recvs = [], []
    for k in range(len(ins)):
        for j, (qx, qy) in enumerate(peers):
            sems = dict(send_sem=send_sems.at[3 * k + j], recv_sem=recv_sems.at[3 * k + j],
                        device_id=(qx, qy, c), device_id_type=MESH)
            sends.append(pltpu.make_async_remote_copy(src_ref=ins[k], dst_ref=lands[k].at[p], **sems))
            recvs.append(pltpu.make_async_remote_copy(src_ref=ins[k], dst_ref=lands[k].at[2 * qx + qy], **sems))
    return sends, recvs


def _peer_gather_copies(peers):
    def copies(ins, lands, send_sems, recv_sems):
        x, y, c = _my_pos()
        where = [(x, 1 - y), (1 - x, y), (1 - x, 1 - y)]
        cps = [pltpu.make_async_remote_copy(
            src_ref=ins[0], dst_ref=lands[j], send_sem=send_sems.at[j], recv_sem=recv_sems.at[j],
            device_id=(*where[j], c), device_id_type=MESH) for j in peers]
        return cps, cps
    return copies


def _far_gather_copies(ins, lands, send_sems, recv_sems):
    x, y, c = _my_pos()
    cps = [pltpu.make_async_remote_copy(
        src_ref=ins[0], dst_ref=lands[0], send_sem=send_sems.at[0], recv_sem=recv_sems.at[0],
        device_id=(1 - x, 1 - y, c), device_id_type=MESH)]
    return cps, cps


def _to_sibling_copies(ins, lands, send_sems, recv_sems):
    x, y, c = _my_pos()
    cps = [pltpu.make_async_remote_copy(
        src_ref=ins[k].at[:, 1 - c], dst_ref=lands[k], send_sem=send_sems.at[k], recv_sem=recv_sems.at[k],
        device_id=(x, y, 1 - c), device_id_type=MESH) for k in range(len(ins))]
    return cps, cps


def _chip_exchange_copies(ins, lands, send_sems, recv_sems):
    x, y, c = _my_pos()
    peers = [(x, 1 - y), (1 - x, y), (1 - x, 1 - y)]
    cps = []
    for k in range(len(ins)):
        for j, (qx, qy) in enumerate(peers):
            cps.append(pltpu.make_async_remote_copy(
                src_ref=ins[k].at[2 * qx + qy], dst_ref=lands[k].at[j], send_sem=send_sems.at[3 * k + j],
                recv_sem=recv_sems.at[3 * k + j], device_id=(qx, qy, c), device_id_type=MESH))
    return cps, cps


def _all_devices_copies(ins, lands, send_sems, recv_sems):
    x, y, c = _my_pos()
    me = 4 * x + 2 * y + c
    sends, recvs = [], []
    for r in range(1, N_DEV):
        px = 1 - x if r & 4 else x
        py = 1 - y if r & 2 else y
        pc = 1 - c if r & 1 else c
        sems = dict(send_sem=send_sems.at[r - 1], recv_sem=recv_sems.at[r - 1], device_id=(px, py, pc), device_id_type=MESH)
        sends.append(pltpu.make_async_remote_copy(src_ref=ins[0], dst_ref=lands[0].at[me], **sems))
        recvs.append(pltpu.make_async_remote_copy(src_ref=ins[0], dst_ref=lands[0].at[4 * px + 2 * py + pc], **sems))
    return sends, recvs


def _swap_copies(ins, lands, send_sems, recv_sems):
    x, y, c = _my_pos()
    cps = [pltpu.make_async_remote_copy(
        src_ref=ins[k], dst_ref=lands[k], send_sem=send_sems.at[k], recv_sem=recv_sems.at[k],
        device_id=(x, y, 1 - c), device_id_type=MESH) for k in range(len(ins))]
    return cps, cps


def _split_start(ins, land_shapes, copies, n_sems, name, after=()):
    n, nl = len(ins), len(land_shapes)
    first_out = n + nl + len(after)

    def body(*refs):
        in_refs, land_refs = refs[:n], refs[n:n + nl]
        send_sems, recv_sems = refs[first_out:first_out + 2]
        token = refs[-1]
        sends, _ = copies(in_refs, land_refs, send_sems, recv_sems)
        for cp in sends:
            cp.start()
        token[...] = jnp.zeros_like(token)

    lands = [pltpu.with_memory_space_constraint(lax.empty(s.shape, s.dtype), pltpu.HBM) for s in land_shapes]
    ins = [pltpu.with_memory_space_constraint(s, pltpu.HBM) for s in ins]
    return pl.pallas_call(
        body, name=name,
        out_shape=(pltpu.SemaphoreType.DMA((n_sems,)), pltpu.SemaphoreType.DMA((n_sems,)),
                   *[pltpu.HBM(s.shape, s.dtype) for s in ins], *[pltpu.HBM(s.shape, s.dtype) for s in lands],
                   jax.ShapeDtypeStruct((8, HEAD), F32)),
        in_specs=[_HBM] * (n + nl) + [pl.BlockSpec(memory_space=pl.ANY)] * len(after),
        out_specs=(_SEM, _SEM, *([_HBM] * (n + nl)), pl.BlockSpec(memory_space=pltpu.VMEM)),
        input_output_aliases={k: 2 + k for k in range(n + nl)},
        compiler_params=pltpu.CompilerParams(has_side_effects=_EFFECT),
    )(*ins, *lands, *after)


def _split_wait(started, n, copies, name, after=()):
    send_sems, recv_sems = started[0], started[1]
    bufs = started[2:-1]
    nb = len(bufs)

    def body(*refs):
        in_refs, land_refs = refs[:n], refs[n:nb]
        sends, recvs = copies(in_refs, land_refs, refs[nb], refs[nb + 1])
        for cp in sends:
            cp.wait_send()
        for cp in recvs:
            cp.wait_recv()

    outs = pl.pallas_call(
        body, name=name,
        out_shape=tuple(pltpu.HBM(s.shape, s.dtype) for s in bufs),
        in_specs=[_HBM] * nb + [_SEM, _SEM] + [pl.BlockSpec(memory_space=pl.ANY)] * len(after),
        out_specs=tuple([_HBM] * nb),
        input_output_aliases={k: k for k in range(nb)},
        compiler_params=pltpu.CompilerParams(has_side_effects=_EFFECT),
    )(*bufs, send_sems, recv_sems, *after)
    return list(outs[:n]), list(outs[n:])


def _fill_own_slot(gathered, shards, pidx, names):
    outs = []
    for g, s, name in zip(gathered, shards, names):
        r, cdim = s.shape
        tr = _row_tile(r)

        def body(p_ref, s_ref, g_ref, o_ref):
            o_ref[...] = s_ref[...]

        outs.append(pl.pallas_call(
            body, name=name,
            grid_spec=pltpu.PrefetchScalarGridSpec(
                num_scalar_prefetch=1, grid=(r // tr,),
                in_specs=[pl.BlockSpec((tr, cdim), lambda i, p: (i, 0)), pl.BlockSpec(memory_space=pl.ANY)],
                out_specs=pl.BlockSpec((None, tr, cdim), lambda i, p: (p[0], i, 0))),
            out_shape=jax.ShapeDtypeStruct(g.shape, g.dtype),
            input_output_aliases={2: 0},
            compiler_params=_cp(("arbitrary",)),
        )(pidx, s, g))
    return outs


def _sum_own_and_peers(own4, slots, pidx, name):
    _, rh, cdim = own4.shape
    tr = _row_tile(rh)

    def body(p_ref, own_ref, s_ref, o_ref):
        acc = own_ref[...].astype(F32)
        for j in range(3):
            acc = acc + s_ref[j].astype(F32)
        o_ref[...] = acc

    return pl.pallas_call(
        body, name=name,
        grid_spec=pltpu.PrefetchScalarGridSpec(
            num_scalar_prefetch=1, grid=(rh // tr,),
            in_specs=[pl.BlockSpec((None, tr, cdim), lambda i, p: (p[0], i, 0)),
                      pl.BlockSpec((3, tr, cdim), lambda i, p: (0, i, 0))],
            out_specs=pl.BlockSpec((tr, cdim), lambda i, p: (i, 0))),
        out_shape=jax.ShapeDtypeStruct((rh, cdim), F32),
        compiler_params=_cp(("arbitrary",)),
    )(pidx, own4, slots)


def _exchange(ins, land_shapes, copies, n_sems, name):
    n, nl = len(ins), len(land_shapes)

    def body(*refs):
        sends, recvs = copies(refs[:n], refs[n:n + nl], refs[n + nl], refs[n + nl + 1])
        for cp in sends:
            cp.start()
        for cp in sends:
            cp.wait_send()
        for cp in recvs:
            cp.wait_recv()

    any_spec = pl.BlockSpec(memory_space=pl.ANY)
    return pl.pallas_call(
        body, name=name,
        out_shape=[jax.ShapeDtypeStruct(s.shape, s.dtype) for s in land_shapes],
        in_specs=[any_spec] * n, out_specs=[any_spec] * nl,
        scratch_shapes=[pltpu.SemaphoreType.DMA((n_sems,)), pltpu.SemaphoreType.DMA((n_sems,))],
    )(*ins)


def _row_tile(r):
    t = min(TR_EW, r)
    while r % t:
        t //= 2
    return t


def _add_own_half(g4, recv, cidx, name):
    _, _, rh, cdim = g4.shape
    tr = _row_tile(rh)

    def body(c_ref, a_ref, b_ref, o_ref):
        o_ref[...] = (a_ref[...] + b_ref[...]).astype(BF16)

    return pl.pallas_call(
        body, name=name,
        grid_spec=pltpu.PrefetchScalarGridSpec(
            num_scalar_prefetch=1, grid=(N_CHIPS, rh // tr),
            in_specs=[pl.BlockSpec((None, None, tr, cdim), lambda q, i, c: (q, c[0], i, 0)),
                      pl.BlockSpec((None, tr, cdim), lambda q, i, c: (q, i, 0))],
            out_specs=pl.BlockSpec((None, tr, cdim), lambda q, i, c: (q, i, 0))),
        out_shape=jax.ShapeDtypeStruct(recv.shape, BF16),
        compiler_params=_cp(("arbitrary", "arbitrary")),
    )(cidx, g4, recv)


def _sum_slots(v, name):
    n, r, cdim = v.shape
    tr = _row_tile(r)

    def body(v_ref, o_ref):
        acc = v_ref[0].astype(F32)
        for k in range(1, n):
            acc = acc + v_ref[k].astype(F32)
        o_ref[...] = acc

    return pl.pallas_call(
        body, name=name, grid=(r // tr,),
        in_specs=[pl.BlockSpec((n, tr, cdim), lambda i: (0, i, 0))],
        out_specs=pl.BlockSpec((tr, cdim), lambda i: (i, 0)),
        out_shape=jax.ShapeDtypeStruct((r, cdim), F32),
        compiler_params=_cp(("arbitrary",)),
    )(v)


def _sum_devices(lands, own, didx, name):
    _, r, cdim = lands.shape
    tr = _row_tile(r)

    def body(d_ref, l_ref, own_ref, o_ref):
        acc = jnp.where(d_ref[0] == 0, own_ref[...], l_ref[0])
        for dv in range(1, N_DEV):
            acc = acc + jnp.where(d_ref[0] == dv, own_ref[...], l_ref[dv])
        o_ref[...] = acc

    return pl.pallas_call(
        body, name=name,
        grid_spec=pltpu.PrefetchScalarGridSpec(
            num_scalar_prefetch=1, grid=(r // tr,),
            in_specs=[pl.BlockSpec((N_DEV, tr, cdim), lambda i, dd: (0, i, 0)), pl.BlockSpec((tr, cdim), lambda i, dd: (i, 0))],
            out_specs=pl.BlockSpec((tr, cdim), lambda i, dd: (i, 0))),
        out_shape=jax.ShapeDtypeStruct((r, cdim), F32),
        compiler_params=_cp(("arbitrary",)),
    )(didx, lands, own)


def _adamw_math(wv, gg, mv, vv):
    nm = ADAM_B1 * mv + (1.0 - ADAM_B1) * gg
    nv = ADAM_B2 * vv + (1.0 - ADAM_B2) * (gg * gg)
    m_hat = nm / (1.0 - ADAM_B1 ** ADAM_STEP)
    v_hat = nv / (1.0 - ADAM_B2 ** ADAM_STEP)
    return -ADAM_LR * (m_hat / (jnp.sqrt(v_hat) + ADAM_EPS) + ADAM_WD * wv), nm, nv


def _adamw_halves(w, mine, theirs, m, v, cidx, name):
    r, cdim = w.shape
    rh = r // 2
    tr = _row_tile(rh)
    nblk = rh // tr

    def body(c_ref, w_ref, a_ref, b_ref, m_ref, v_ref, g_ref, d_ref, nm_ref, nv_ref):
        gg = jnp.where(pl.program_id(0) == c_ref[0], a_ref[...], b_ref[...])
        g_ref[...] = gg
        d_ref[...], nm_ref[...], nv_ref[...] = _adamw_math(w_ref[...], gg, m_ref[...], v_ref[...])

    full = pl.BlockSpec((tr, cdim), lambda hh, i, c: (hh * nblk + i, 0))
    half = pl.BlockSpec((tr, cdim), lambda hh, i, c: (i, 0))
    return pl.pallas_call(
        body, name=name,
        grid_spec=pltpu.PrefetchScalarGridSpec(
            num_scalar_prefetch=1, grid=(2, nblk),
            in_specs=[full, half, half, full, full], out_specs=[full] * 4),
        out_shape=[jax.ShapeDtypeStruct((r, cdim), F32)] * 4,
        compiler_params=_cp(("arbitrary", "arbitrary")),
    )(cidx, w, mine, theirs, m, v)


def _adamw_many(params, name):
    n = len(params)

    def body(*refs):
        ins, outs = refs[:4 * n], refs[4 * n:]
        for k in range(n):
            w_ref, g_ref, m_ref, v_ref = ins[4 * k:4 * k + 4]
            outs[3 * k][...], outs[3 * k + 1][...], outs[3 * k + 2][...] = _adamw_math(
                w_ref[...], g_ref[...], m_ref[...], v_ref[...])

    flat = [a for p in params for a in p]
    res = pl.pallas_call(
        body, name=name,
        out_shape=[jax.ShapeDtypeStruct(p[0].shape, F32) for p in params for _ in range(3)],
        compiler_params=pltpu.CompilerParams(vmem_limit_bytes=VMEM_LIMIT),
    )(*flat)
    return [res[3 * k:3 * k + 3] for k in range(n)]


def _adamw(w, g, m, v, name):
    r, cdim = w.shape
    tr = _row_tile(r) if r % 8 == 0 else r

    def body(w_ref, g_ref, m_ref, v_ref, d_ref, nm_ref, nv_ref):
        d_ref[...], nm_ref[...], nv_ref[...] = _adamw_math(w_ref[...], g_ref[...], m_ref[...], v_ref[...])

    spec = pl.BlockSpec((tr, cdim), lambda i: (i, 0))
    return pl.pallas_call(
        body, name=name, grid=(r // tr,), in_specs=[spec] * 4, out_specs=[spec] * 3,
        out_shape=[jax.ShapeDtypeStruct((r, cdim), F32)] * 3,
        compiler_params=_cp(("arbitrary",)),
    )(w, g, m, v)


def _ada_fwd(c_all, w_ada, b_cols):
    nb, _ = c_all.shape
    n = w_ada.shape[1]

    def body(c_ref, w_ref, b_ref, o_ref):
        cv = c_ref[...]
        o_ref[...] = _mm(cv * _sigmoid(cv), w_ref[...]) + b_ref[...]

    return pl.pallas_call(
        body, name="ada_fwd", out_shape=jax.ShapeDtypeStruct((nb, n), F32),
        compiler_params=pltpu.CompilerParams(vmem_limit_bytes=VMEM_LIMIT),
    )(c_all, w_ada, b_cols)


def _ada_bwd(c_all, dmod_all, dmod_cols):
    d = c_all.shape[1]
    n = dmod_cols.shape[1]

    def body(c_ref, da_ref, dc_ref, gw_ref, gb_ref):
        cv = c_ref[...]
        gw_ref[...] = _mm_tn(cv * _sigmoid(cv), dc_ref[...])
        gb_ref[...] = _colsum(da_ref[...])

    return pl.pallas_call(
        body, name="ada_bwd",
        out_shape=[jax.ShapeDtypeStruct((d, n), F32), jax.ShapeDtypeStruct((1, dmod_all.shape[1]), F32)],
        compiler_params=pltpu.CompilerParams(vmem_limit_bytes=VMEM_LIMIT),
    )(c_all, dmod_all, dmod_cols)


def _proj_fwd(x2, modv, ws, cols, b_in, seq, name, proj_in=None):
    t, d = x2.shape
    n = len(ws)
    ns = ws[0].shape[1]
    tm = min(TM_PROJ, seq)
    tpb = seq // tm
    first = proj_in is None

    def body(c_ref, x_ref, mod_ref, *refs):
        w_refs, b_ref = refs[:n], refs[n]
        outs = refs[n + 1 if first else n + 2:]
        proj_ref, h_s = outs[0], outs[-1]
        s = pl.program_id(1)

        @pl.when(s == 0)
        def _():
            h = (x_ref[...] * (1.0 + mod_ref[1:2, :]) + mod_ref[0:1, :]).astype(BF16)
            h_s[...] = h
            if first:
                outs[1][...] = h

        for k in range(n):
            @pl.when(s == k)
            def _():
                proj_ref[...] = (jnp.dot(h_s[...], w_refs[k][...], preferred_element_type=F32) + b_ref[...]).astype(BF16)

    in_specs = [pl.BlockSpec((tm, d), lambda i, s, c: (i, 0)),
                pl.BlockSpec((None, 8, d), lambda i, s, c: (i // tpb, 0, 0))]
    in_specs += [pl.BlockSpec((d, ns), lambda i, s, c: (0, 0))] * n
    in_specs += [pl.BlockSpec((1, ns), lambda i, s, c: (0, c[s]))]
    out_specs = [pl.BlockSpec((tm, ns), lambda i, s, c: (i, c[s]))]
    out_shape = [jax.ShapeDtypeStruct((t, N_CHIPS * ns), BF16)]
    args = [cols, x2, modv, *ws, b_in]
    aliases = {}
    if first:
        out_specs.append(pl.BlockSpec((tm, d), lambda i, s, c: (i, 0)))
        out_shape.append(jax.ShapeDtypeStruct((t, d), BF16))
    else:
        in_specs.append(_ANY)
        args.append(proj_in)
        aliases = {len(args) - 1: 0}
    return pl.pallas_call(
        body, name=name,
        grid_spec=pltpu.PrefetchScalarGridSpec(
            num_scalar_prefetch=1, grid=(t // tm, n), in_specs=in_specs, out_specs=out_specs,
            scratch_shapes=[pltpu.VMEM((tm, d), BF16)]),
        out_shape=out_shape, input_output_aliases=aliases,
        compiler_params=_cp(("arbitrary", "arbitrary")),
    )(*args)


def _lru_gates(xl, wc_ref, bc_ref, wa_ref, ba_ref, wx_ref, bx_ref, lam_ref):
    xc = bc_ref[...] + wc_ref[CONV_WIDTH - 1:CONV_WIDTH, :] * xl
    for k in range(CONV_WIDTH - 1):
        xc = xc + wc_ref[k:k + 1, :] * _shift_down(xl, CONV_WIDTH - 1 - k)
    r = _sigmoid(_mm(xc, wa_ref[...]) + ba_ref[...])
    gi = _sigmoid_t(_mm(xc, wx_ref[...]) + bx_ref[...])
    nl = -lam_ref[...]
    e = jnp.exp(-jnp.abs(nl))
    u = 1.0 + e
    dlt = u - 1.0
    log1p_e = jnp.where(dlt == 0.0, e, jnp.log(u) * (e / jnp.where(dlt == 0.0, 1.0, dlt)))
    big_l = -LRU_C * (jnp.maximum(nl, 0.0) + log1p_e)
    la = big_l * r
    a = jnp.exp(la)
    m2 = jnp.tanh(-la) * (a * a + 1.0)
    return xc, r, gi, big_l, a, m2


def _lru_prep(proj, lru_w, nb, seq):
    t = proj.shape[0]
    w = LRU_HEADS * HEAD
    w_conv, b_conv, w_a, b_a, w_x, b_x, lam = lru_w

    def body(x_ref, wc_ref, bc_ref, wa_ref, ba_ref, wx_ref, bx_ref, lam_ref, a_ref, inp_ref):
        xc, r, gi, big_l, a, m2 = _lru_gates(x_ref[...].astype(F32), wc_ref, bc_ref, wa_ref, ba_ref, wx_ref, bx_ref, lam_ref)
        a_ref[...] = a
        inp_ref[...] = jnp.sqrt(m2) * (gi * xc)

    col = lambda b, hd: (0, hd)
    head = lambda b, hd: (hd, 0, 0)
    tok = lambda b, hd: (b, hd)
    return pl.pallas_call(
        body, name="lru_prep", grid=(nb, LRU_HEADS),
        in_specs=[pl.BlockSpec((seq, HEAD), tok),
                  pl.BlockSpec((CONV_WIDTH, HEAD), col), pl.BlockSpec((1, HEAD), col),
                  pl.BlockSpec((None, HEAD, HEAD), head), pl.BlockSpec((1, HEAD), col),
                  pl.BlockSpec((None, HEAD, HEAD), head), pl.BlockSpec((1, HEAD), col),
                  pl.BlockSpec((1, HEAD), col)],
        out_specs=[pl.BlockSpec((seq, HEAD), tok)] * 2,
        out_shape=[jax.ShapeDtypeStruct((t, w), F32)] * 2,
        compiler_params=_cp(("arbitrary", "arbitrary")),
    )(proj, w_conv, b_conv, w_a, b_a, w_x, b_x, lam)


def _scan(a3, b3, reverse, name, out_dtype):
    nb, seq, w = a3.shape
    tc = min(TC_SCAN, seq)
    nchunk = seq // tc
    npair = tc // 16

    def combine(av, bv):
        rows = lax.broadcasted_iota(jnp.int32, av.shape, 0)
        for s in (1, 2, 4):
            if reverse:
                keep = rows < 8 - s
                a_sh, b_sh = pltpu.roll(av, 8 - s, 0), pltpu.roll(bv, 8 - s, 0)
            else:
                keep = rows >= s
                a_sh, b_sh = pltpu.roll(av, s, 0), pltpu.roll(bv, s, 0)
            bv = jnp.where(keep, bv + av * b_sh, bv)
            av = jnp.where(keep, av * a_sh, av)
        return av, bv

    def body(a_ref, b_ref, h_ref, carry):
        @pl.when(pl.program_id(0) == 0)
        def _():
            carry[...] = jnp.zeros_like(carry)

        for b in range(nb):
            def pair(j, hprev):
                jj = npair - 1 - j if reverse else j
                base = pl.multiple_of(jj * 16, 16)
                a16 = a_ref[b, pl.ds(base, 16), :]
                b16 = b_ref[b, pl.ds(base, 16), :].astype(F32)
                outs = [None, None]
                for k in ((1, 0) if reverse else (0, 1)):
                    av, bv = a16[8 * k:8 * k + 8, :], b16[8 * k:8 * k + 8, :]
                    av, bv = combine(av, av * bv if reverse else bv)
                    h = bv + av * hprev
                    outs[k] = h
                    hprev = jnp.broadcast_to(h[0:1, :] if reverse else h[7:8, :], (8, w))
                h_ref[b, pl.ds(base, 16), :] = jnp.concatenate(outs, axis=0).astype(out_dtype)
                return hprev

            carry[b] = lax.fori_loop(0, npair, pair, carry[b])

    imap = (lambda i: (0, nchunk - 1 - i, 0)) if reverse else (lambda i: (0, i, 0))
    spec = pl.BlockSpec((nb, tc, w), imap)
    return pl.pallas_call(
        body, name=name, grid=(nchunk,), in_specs=[spec, spec], out_specs=spec,
        out_shape=jax.ShapeDtypeStruct((nb, seq, w), out_dtype),
        scratch_shapes=[pltpu.VMEM((nb, 8, w), F32)],
        compiler_params=_cp(("arbitrary",)),
    )(a3, b3)


def _sgu_mask():
    ti = lax.broadcasted_iota(jnp.int32, (HEAD, HEAD), 0) // SGU_CHUNK
    si = lax.broadcasted_iota(jnp.int32, (HEAD, HEAD), 1) // SGU_CHUNK
    return si <= ti


def _sgu_specs(tm, d_sgu):
    pw = 256
    first_u = (2 * LRU_HEADS * HEAD) // pw
    n_piece = d_sgu // pw
    specs = [pl.BlockSpec((tm, pw), functools.partial(lambda i, k: (i, k), k=first_u + j)) for j in range(2 * n_piece)]
    return specs, n_piece


def _sgu_fwd(proj, w_sp, b_sp_t, ln_g, ln_b):
    t = proj.shape[0]
    d_sgu = SGU_GROUPS * HEAD
    tm = min(TM_SGU, t)
    nblk = tm // HEAD
    specs, n_piece = _sgu_specs(tm, d_sgu)

    def body(*refs):
        u = jnp.concatenate([r[...] for r in refs[:n_piece]], axis=1).astype(F32)
        v = jnp.concatenate([r[...] for r in refs[n_piece:2 * n_piece]], axis=1).astype(F32)
        w_ref, bt_ref, g_ref, b_ref, y_ref = refs[2 * n_piece:]
        ug = _gelu(u)
        xhat, _ = _ln_stats(_gelu(v))
        vn = (xhat * g_ref[...] + b_ref[...]).astype(BF16)
        mask = _sgu_mask()
        for g in range(SGU_GROUPS):
            wm = jnp.where(mask, w_ref[g], 0.0).astype(BF16)
            cols = slice(g * HEAD, (g + 1) * HEAD)
            for n in range(nblk):
                rows = slice(n * HEAD, (n + 1) * HEAD)
                mixed = jnp.dot(wm, vn[rows, cols], preferred_element_type=F32) + bt_ref[:, g:g + 1]
                y_ref[rows, cols] = (ug[rows, cols] * mixed).astype(BF16)

    full = lambda shape: pl.BlockSpec(shape, lambda i: (0,) * len(shape))
    return pl.pallas_call(
        body, name="sgu_fwd", grid=(t // tm,),
        in_specs=specs + [full(w_sp.shape), full(b_sp_t.shape), full(ln_g.shape), full(ln_b.shape)],
        out_specs=pl.BlockSpec((tm, d_sgu), lambda i: (i, 0)),
        out_shape=jax.ShapeDtypeStruct((t, d_sgu), BF16),
        compiler_params=_cp(("arbitrary",)),
    )(*([proj] * (2 * n_piece)), w_sp, b_sp_t, ln_g, ln_b)


def _mix_fwd(hs, proj, y_sgu, x2, modv, w_o_lru_g, w_o_sgu_g, w_out_g, ln1_g, ln1_b, seq):
    t, d = x2.shape
    w = hs.shape[1]
    d_sgu = y_sgu.shape[1]
    nq, _, ns = w_o_sgu_g.shape
    tm = min(TM_MIX, seq)
    tpb = seq // tm

    def body(hs_ref, gl_ref, ys_ref, ga_ref, gb_ref, x_ref, mod_ref, wl_ref, ws_ref, wo_ref, g1_ref, b1_ref,
             yap_ref, ya_ref, yb_ref, mg_ref, mix_ref, x1_ref):
        yap = (hs_ref[...].astype(F32) * _gelu(gl_ref[...].astype(F32))).astype(BF16)
        yap_ref[...] = yap
        y_a = jnp.dot(yap, wl_ref[...], preferred_element_type=F32)
        ys = ys_ref[...]
        y_b = jnp.concatenate([jnp.dot(ys, ws_ref[q], preferred_element_type=F32) for q in range(nq)], axis=1)
        ya_ref[...] = y_a.astype(BF16)
        yb_ref[...] = y_b.astype(BF16)
        merged = (_sigmoid_t(ga_ref[...].astype(F32)) * y_a + _sigmoid_t(gb_ref[...].astype(F32)) * y_b).astype(BF16)
        mg_ref[...] = merged
        mix = jnp.dot(merged, wo_ref[...], preferred_element_type=F32)
        mix_ref[...] = mix
        xhat, _ = _ln_stats(ALPHA * x_ref[...] + (1.0 + mod_ref[2:3, :]) * mix)
        x1_ref[...] = xhat * g1_ref[...] + b1_ref[...]

    row = lambda width, col: pl.BlockSpec((tm, width), functools.partial(lambda i, k: (i, k), k=col))
    full = lambda shape: pl.BlockSpec(shape, lambda i: (0,) * len(shape))
    return pl.pallas_call(
        body, name="mix_fwd", grid=(t // tm,),
        in_specs=[row(w, 0), row(w, 1), row(d_sgu, 0), row(d, 4), row(d, 5), row(d, 0),
                  pl.BlockSpec((None, 8, d), lambda i: (i // tpb, 0, 0)),
                  full(w_o_lru_g.shape), full(w_o_sgu_g.shape), full(w_out_g.shape), full(ln1_g.shape), full(ln1_b.shape)],
        out_specs=[row(w, 0), row(d, 0), row(d, 0), row(d, 0), row(d, 0), row(d, 0)],
        out_shape=[jax.ShapeDtypeStruct((t, w), BF16), jax.ShapeDtypeStruct((t, d), BF16),
                   jax.ShapeDtypeStruct((t, d), BF16), jax.ShapeDtypeStruct((t, d), BF16),
                   jax.ShapeDtypeStruct((t, d), F32), jax.ShapeDtypeStruct((t, d), F32)],
        compiler_params=_cp(("arbitrary",)),
    )(hs, proj, y_sgu, proj, proj, x2, modv, w_o_lru_g, w_o_sgu_g, w_out_g, ln1_g, ln1_b)


def _mlp_fwd(x1, modv, w_up_g, w_down_g, ln2_g, ln2_b, target, nb, seq):
    t, d = x1.shape
    nq, _, ns = w_up_g.shape
    tm = min(TM_MLP, seq)
    ts = min(TS_MLP, tm)
    tpb = seq // tm

    def body(x1_ref, mod_ref, wu_hbm, wd_hbm, g2_ref, b2_ref, tg_ref,
             rl_ref, act_ref, h2_ref, dz2_ref, df_ref, st_ref, pb_ref, wu_s, wd_s, acc, sems):
        i = pl.program_id(0)

        @pl.when(i == 0)
        def _():
            _load_weights((wu_hbm, wd_hbm), (wu_s, wd_s), sems)
            st_ref[...] = jnp.zeros_like(st_ref)

        @pl.when(i % tpb == 0)
        def _():
            pb_ref[...] = jnp.zeros_like(pb_ref)

        for sub in range(tm // ts):
            rows = slice(sub * ts, (sub + 1) * ts)
            x1v = x1_ref[rows, :]
            h2 = (x1v * (1.0 + mod_ref[4:5, :]) + mod_ref[3:4, :]).astype(BF16)
            h2_ref[rows, :] = h2
            for k in range(nq):
                cols = slice(k * ns, (k + 1) * ns)
                r = jnp.maximum(jnp.dot(h2, wu_s[k], preferred_element_type=F32), 0.0)
                act = (r * r).astype(BF16)
                rl_ref[rows, cols] = r.astype(BF16)
                act_ref[rows, cols] = act
                part = jnp.dot(act, wd_s[cols, :], preferred_element_type=F32)
                if k == 0:
                    acc[sub] = part
                else:
                    acc[sub] += part
            f = acc[sub]
            xhat, rstd = _ln_stats(ALPHA * x1v + (1.0 + mod_ref[5:6, :]) * f)
            y = xhat * g2_ref[...] + b2_ref[...]
            err = y - tg_ref[rows, :]
            dy = err * (1.0 / d)
            dz2 = _ln_bwd(dy * g2_ref[...], xhat, rstd)
            dz2_ref[rows, :] = dz2
            df_ref[rows, :] = ((1.0 + mod_ref[5:6, :]) * dz2).astype(BF16)
            st_ref[0:1, :] += _colsum(dy * xhat)
            st_ref[1:2, :] += _colsum(dy)
            st_ref[2:3, :] += (0.5 / d) * jnp.sum(_colsum(err * err), axis=1, keepdims=True)
            pb_ref[0:1, :] += _colsum(dz2 * f)

    tok = lambda i: (i, 0)
    return pl.pallas_call(
        body, name="mlp_fwd", grid=(t // tm,),
        in_specs=[pl.BlockSpec((tm, d), tok), pl.BlockSpec((None, 8, d), lambda i: (i // tpb, 0, 0)), _ANY, _ANY,
                  pl.BlockSpec((1, d), lambda i: (0, 0)), pl.BlockSpec((1, d), lambda i: (0, 0)),
                  pl.BlockSpec((tm, d), tok)],
        out_specs=[pl.BlockSpec((tm, nq * ns), tok), pl.BlockSpec((tm, nq * ns), tok),
                   pl.BlockSpec((tm, d), tok), pl.BlockSpec((tm, d), tok), pl.BlockSpec((tm, d), tok),
                   pl.BlockSpec((8, d), lambda i: (0, 0)), pl.BlockSpec((None, 8, d), lambda i: (i // tpb, 0, 0))],
        out_shape=[jax.ShapeDtypeStruct((t, nq * ns), BF16), jax.ShapeDtypeStruct((t, nq * ns), BF16),
                   jax.ShapeDtypeStruct((t, d), BF16),
                   jax.ShapeDtypeStruct((t, d), F32), jax.ShapeDtypeStruct((t, d), BF16),
                   jax.ShapeDtypeStruct((8, d), F32), jax.ShapeDtypeStruct((nb, 8, d), F32)],
        scratch_shapes=[pltpu.VMEM(w_up_g.shape, BF16), pltpu.VMEM(w_down_g.shape, BF16),
                        pltpu.VMEM((tm // ts, ts, d), F32), pltpu.SemaphoreType.DMA((2,))],
        compiler_params=_cp(("arbitrary",)),
    )(x1, modv, w_up_g, w_down_g, ln2_g, ln2_b, target)


def _mlp_bwd(df, up, w_down_g, w_up_g, dz2, x2, mix, modv, ln1_g, ln1_b, nb, seq):
    t, d = x2.shape
    nq, _, ns = w_up_g.shape
    tm = min(TM_MLP, seq)
    ts = min(TS_MLP, tm)
    tpb = seq // tm

    def body(df_ref, rl_ref, wd_hbm, wu_hbm, dz2_ref, x_ref, mix_ref, mod_ref, g1_ref, b1_ref,
             dup_ref, dz1_ref, dmix_ref, st_ref, pb_ref, wd_s, wu_s, acc, sems):
        i = pl.program_id(0)

        @pl.when(i == 0)
        def _():
            _load_weights((wd_hbm, wu_hbm), (wd_s, wu_s), sems)
            st_ref[...] = jnp.zeros_like(st_ref)

        @pl.when(i % tpb == 0)
        def _():
            pb_ref[...] = jnp.zeros_like(pb_ref)

        for sub in range(tm // ts):
            rows = slice(sub * ts, (sub + 1) * ts)
            dfv = df_ref[rows, :]
            for k in range(nq):
                cols = slice(k * ns, (k + 1) * ns)
                dup = (_mm_nt(dfv, wd_s[cols, :]) * (2.0 * rl_ref[rows, cols].astype(F32))).astype(BF16)
                dup_ref[rows, cols] = dup
                part = _mm_nt(dup, wu_s[k])
                if k == 0:
                    acc[sub] = part
                else:
                    acc[sub] += part
            dh2 = acc[sub]
            mix = mix_ref[rows, :]
            xhat, rstd = _ln_stats(ALPHA * x_ref[rows, :] + (1.0 + mod_ref[2:3, :]) * mix)
            x1 = xhat * g1_ref[...] + b1_ref[...]
            dx1 = ALPHA * dz2_ref[rows, :] + dh2 * (1.0 + mod_ref[4:5, :])
            dz1 = _ln_bwd(dx1 * g1_ref[...], xhat, rstd)
            dz1_ref[rows, :] = dz1
            dmix_ref[rows, :] = ((1.0 + mod_ref[2:3, :]) * dz1).astype(BF16)
            st_ref[0:1, :] += _colsum(dx1 * xhat)
            st_ref[1:2, :] += _colsum(dx1)
            pb_ref[0:1, :] += _colsum(dh2 * x1)
            pb_ref[1:2, :] += _colsum(dh2)
            pb_ref[2:3, :] += _colsum(dz1 * mix)

    tok = lambda i: (i, 0)
    return pl.pallas_call(
        body, name="mlp_bwd", grid=(t // tm,),
        in_specs=[pl.BlockSpec((tm, d), tok), pl.BlockSpec((tm, nq * ns), tok), _ANY, _ANY,
                  pl.BlockSpec((tm, d), tok), pl.BlockSpec((tm, d), tok), pl.BlockSpec((tm, d), tok),
                  pl.BlockSpec((None, 8, d), lambda i: (i // tpb, 0, 0)),
                  pl.BlockSpec((1, d), lambda i: (0, 0)), pl.BlockSpec((1, d), lambda i: (0, 0))],
        out_specs=[pl.BlockSpec((tm, nq * ns), tok),
                   pl.BlockSpec((tm, d), tok), pl.BlockSpec((tm, d), tok),
                   pl.BlockSpec((8, d), lambda i: (0, 0)), pl.BlockSpec((None, 8, d), lambda i: (i // tpb, 0, 0))],
        out_shape=[jax.ShapeDtypeStruct((t, nq * ns), BF16),
                   jax.ShapeDtypeStruct((t, d), F32), jax.ShapeDtypeStruct((t, d), BF16),
                   jax.ShapeDtypeStruct((8, d), F32), jax.ShapeDtypeStruct((nb, 8, d), F32)],
        scratch_shapes=[pltpu.VMEM(w_down_g.shape, BF16), pltpu.VMEM(w_up_g.shape, BF16),
                        pltpu.VMEM((tm // ts, ts, d), F32), pltpu.SemaphoreType.DMA((2,))],
        compiler_params=_cp(("arbitrary",), VMEM_LIMIT_MAX),
    )(df, up, w_down_g, w_up_g, dz2, x2, mix, modv, ln1_g, ln1_b)


def _mix_bwd(dmix, proj, y_a, y_b, hs, w_out_g, w_o_lru_g, w_o_sgu_g, seq, after=()):
    t, d = dmix.shape
    w = hs.shape[1]
    nq, d_sgu, ns = w_o_sgu_g.shape
    tm = min(TM_MIX, seq)

    def body(dmix_ref, ga_ref, gb_ref, ya_ref, yb_ref, gl_ref, hs_ref, wo_ref, wl_ref, ws_ref,
             dya_ref, dyb_ref, dg_ref, dgl_ref, dyl_ref, dys_ref):
        dmerged = _mm_nt(dmix_ref[...], wo_ref[...])
        sa, sb = _sigmoid_t(ga_ref[...].astype(F32)), _sigmoid_t(gb_ref[...].astype(F32))
        dy_a = (dmerged * sa).astype(BF16)
        dy_b = (dmerged * sb).astype(BF16)
        dya_ref[...] = dy_a
        dyb_ref[...] = dy_b
        dg_ref[:, 0:d] = (dmerged * ya_ref[...].astype(F32) * (sa * (1.0 - sa))).astype(BF16)
        dg_ref[:, d:2 * d] = (dmerged * yb_ref[...].astype(F32) * (sb * (1.0 - sb))).astype(BF16)
        dyap = _mm_nt(dy_a, wl_ref[...])
        gel, dgel = _gelu_and_grad(gl_ref[...].astype(F32))
        dyl_ref[...] = (dyap * gel).astype(BF16)
        dgl_ref[...] = (dyap * hs_ref[...].astype(F32) * dgel).astype(BF16)
        dys = _mm_nt(dy_b[:, 0:ns], ws_ref[0])
        for q in range(1, nq):
            dys = dys + _mm_nt(dy_b[:, q * ns:(q + 1) * ns], ws_ref[q])
        dys_ref[...] = dys

    row = lambda width, col: pl.BlockSpec((tm, width), functools.partial(lambda i, k: (i, k), k=col))
    full = lambda shape: pl.BlockSpec(shape, lambda i: (0,) * len(shape))
    return pl.pallas_call(
        _ordered(body, 10, after), name="mix_bwd", grid=(t // tm,),
        in_specs=[row(d, 0), row(d, 4), row(d, 5), row(d, 0), row(d, 0), row(w, 1), row(w, 0),
                  full(w_out_g.shape), full(w_o_lru_g.shape), full(w_o_sgu_g.shape)] + [_ANY] * len(after),
        out_specs=[row(d, 0), row(d, 0), row(2 * d, 2), row(w, 0), row(w, 0), row(d_sgu, 0)],
        out_shape=[jax.ShapeDtypeStruct((t, d), BF16), jax.ShapeDtypeStruct((t, d), BF16),
                   jax.ShapeDtypeStruct((t, 6 * d), BF16),
                   jax.ShapeDtypeStruct((t, w), BF16), jax.ShapeDtypeStruct((t, w), BF16),
                   jax.ShapeDtypeStruct((t, d_sgu), F32)],
        compiler_params=_cp(("arbitrary",)),
    )(dmix, proj, proj, y_a, y_b, proj, hs, w_out_g, w_o_lru_g, w_o_sgu_g, *after)


def _sgu_bwd(proj, dys, w_sp, b_sp_t, ln_g, ln_b, after=()):
    t = proj.shape[0]
    d_sgu = SGU_GROUPS * HEAD
    tm = min(TM_SGU, t)
    nblk = tm // HEAD
    specs, n_piece = _sgu_specs(tm, d_sgu)

    def body(*refs):
        u = jnp.concatenate([r[...] for r in refs[:n_piece]], axis=1).astype(F32)
        v = jnp.concatenate([r[...] for r in refs[n_piece:2 * n_piece]], axis=1).astype(F32)
        dys_ref, w_ref, bt_ref, g_ref, b_ref, du_ref, dv_ref, dw_ref, st_ref, dbt_ref, dvn_s = refs[2 * n_piece:]

        @pl.when(pl.program_id(0) == 0)
        def _():
            dw_ref[...] = jnp.zeros_like(dw_ref)
            st_ref[...] = jnp.zeros_like(st_ref)
            dbt_ref[...] = jnp.zeros_like(dbt_ref)

        ug, dug_du = _gelu_and_grad(u)
        vg, dvg_dv = _gelu_and_grad(v)
        xhat, rstd = _ln_stats(vg)
        vn = (xhat * g_ref[...] + b_ref[...]).astype(BF16)
        dys_v = dys_ref[...]
        mask = _sgu_mask()
        for g in range(SGU_GROUPS):
            wm = jnp.where(mask, w_ref[g], 0.0).astype(BF16)
            cols = slice(g * HEAD, (g + 1) * HEAD)
            dw_g = jnp.zeros((HEAD, HEAD), F32)
            db_g = jnp.zeros((HEAD, 1), F32)
            for n in range(nblk):
                rows = slice(n * HEAD, (n + 1) * HEAD)
                vn_blk = vn[rows, cols]
                mixed = jnp.dot(wm, vn_blk, preferred_element_type=F32) + bt_ref[:, g:g + 1]
                dy_blk = dys_v[rows, cols]
                du_ref[rows, cols] = (dy_blk * mixed * dug_du[rows, cols]).astype(BF16)
                dmx = dy_blk * ug[rows, cols]
                dvn_s[rows, cols] = _mm_tn(wm, dmx)
                dw_g = dw_g + _mm_nt(dmx, vn_blk)
                db_g = db_g + jnp.sum(dmx, axis=1, keepdims=True)
            dw_ref[g] += jnp.where(mask, dw_g, 0.0)
            dbt_ref[:, g:g + 1] += db_g
        dvn = dvn_s[...]
        st_ref[0:1, :] += _colsum(dvn * xhat)
        st_ref[1:2, :] += _colsum(dvn)
        dv_ref[...] = (_ln_bwd(dvn * g_ref[...], xhat, rstd) * dvg_dv).astype(BF16)

    full = lambda shape: pl.BlockSpec(shape, lambda i: (0,) * len(shape))
    tok = pl.BlockSpec((tm, d_sgu), lambda i: (i, 0))
    return pl.pallas_call(
        _ordered(body, 2 * n_piece + 5, after), name="sgu_bwd", grid=(t // tm,),
        in_specs=specs + [tok, full(w_sp.shape), full(b_sp_t.shape), full(ln_g.shape), full(ln_b.shape)]
        + [_ANY] * len(after),
        out_specs=[tok, tok, full(w_sp.shape), full((8, d_sgu)), full((HEAD, HEAD))],
        out_shape=[jax.ShapeDtypeStruct((t, d_sgu), BF16), jax.ShapeDtypeStruct((t, d_sgu), BF16),
                   jax.ShapeDtypeStruct(w_sp.shape, F32), jax.ShapeDtypeStruct((8, d_sgu), F32),
                   jax.ShapeDtypeStruct((HEAD, HEAD), F32)],
        scratch_shapes=[pltpu.VMEM((tm, d_sgu), F32)],
        compiler_params=_cp(("arbitrary",)),
    )(*([proj] * (2 * n_piece)), dys, w_sp, b_sp_t, ln_g, ln_b, *after)


def _lru_bwd(proj, hs, e, dyl, lru_w, nb, seq, dproj, after=()):
    t = proj.shape[0]
    w = LRU_HEADS * HEAD
    w_conv, b_conv, w_a, b_a, w_x, b_x, lam = lru_w

    def body(x_ref, hs_ref, e_ref, dy_ref, wc_ref, bc_ref, wa_ref, ba_ref, wx_ref, bx_ref, lam_ref,
             dxl_ref, dwa_ref, dwx_ref, st_ref):
        @pl.when(pl.program_id(1) == 0)
        def _():
            dwa_ref[...] = jnp.zeros_like(dwa_ref)
            dwx_ref[...] = jnp.zeros_like(dwx_ref)
            st_ref[...] = jnp.zeros_like(st_ref)

        xl = x_ref[...].astype(F32)
        xc, r, gi, big_l, a, m2 = _lru_gates(xl, wc_ref, bc_ref, wa_ref, ba_ref, wx_ref, bx_ref, lam_ref)
        inv_mult = lax.rsqrt(m2)
        mult = m2 * inv_mult
        dh = dy_ref[...].astype(F32) + _shift_up(e_ref[...], 1)
        da = dh * _shift_down(hs_ref[...].astype(F32), 1)
        dmult = dh * (gi * xc)
        d_i = dh * (mult * xc)
        dxc = dh * (mult * gi)
        dla = a * (da - dmult * (a * inv_mult))
        dr = dla * big_l
        d_big_l = _colsum(dla * r)
        dra = dr * (r * (1.0 - r))
        dia = d_i * (gi * (1.0 - gi))
        dwa_ref[...] += _mm_tn(xc, dra)
        dwx_ref[...] += _mm_tn(xc, dia)
        dxc = dxc + _mm_nt(dra, wa_ref[...]) + _mm_nt(dia, wx_ref[...])
        dxl = wc_ref[CONV_WIDTH - 1:CONV_WIDTH, :] * dxc
        st_ref[4 + CONV_WIDTH - 1:4 + CONV_WIDTH, :] += _colsum(dxc * xl)
        for k in range(CONV_WIDTH - 1):
            ahead = _shift_up(dxc, CONV_WIDTH - 1 - k)
            dxl = dxl + wc_ref[k:k + 1, :] * ahead
            st_ref[4 + k:5 + k, :] += _colsum(ahead * xl)
        dxl_ref[...] = dxl.astype(BF16)
        st_ref[0:1, :] += _colsum(dra)
        st_ref[1:2, :] += _colsum(dia)
        st_ref[2:3, :] += d_big_l * (LRU_C * _sigmoid(-lam_ref[...]))
        st_ref[3:4, :] += _colsum(dxc)

    col = lambda hd, b: (0, hd)
    head = lambda hd, b: (hd, 0, 0)
    tok = lambda hd, b: (b, hd)
    seq_blk = pl.BlockSpec((seq, HEAD), tok)
    return pl.pallas_call(
        _ordered(body, 11, (dproj,) + tuple(after)), name="lru_bwd", grid=(LRU_HEADS, nb),
        in_specs=[seq_blk, seq_blk, seq_blk, seq_blk,
                  pl.BlockSpec((CONV_WIDTH, HEAD), col), pl.BlockSpec((1, HEAD), col),
                  pl.BlockSpec((None, HEAD, HEAD), head), pl.BlockSpec((1, HEAD), col),
                  pl.BlockSpec((None, HEAD, HEAD), head), pl.BlockSpec((1, HEAD), col),
                  pl.BlockSpec((1, HEAD), col)] + [_ANY] * (1 + len(after)),
        out_specs=[seq_blk, pl.BlockSpec((None, HEAD, HEAD), head), pl.BlockSpec((None, HEAD, HEAD), head),
                   pl.BlockSpec((8, HEAD), col)],
        out_shape=[jax.ShapeDtypeStruct(dproj.shape, BF16), jax.ShapeDtypeStruct((LRU_HEADS, HEAD, HEAD), F32),
                   jax.ShapeDtypeStruct((LRU_HEADS, HEAD, HEAD), F32), jax.ShapeDtypeStruct((8, w), F32)],
        input_output_aliases={11: 0},
        compiler_params=_cp(("arbitrary", "arbitrary")),
    )(proj, hs, e, dyl, w_conv, b_conv, w_a, b_a, w_x, b_x, lam, dproj, *after)


def _weight_grad(a, g, col_shards, name, after=()):
    t, k = a.shape
    n = g.shape[1]
    tt = min(TT_DW, t)
    tk = k if k <= 1536 else 1024
    ns = n // N_CHIPS if col_shards else n
    narrow = col_shards and ns < 512
    tn = n if narrow else min(ns, 768 if ns % 768 == 0 else 1024)
    while ns % tn and not narrow:
        tn //= 2
    per = max(ns // tn, 1)

    def body(a_ref, g_ref, o_ref):
        @pl.when(pl.program_id(2) == 0)
        def _():
            o_ref[...] = jnp.zeros_like(o_ref)

        res = _mm_tn(a_ref[...], g_ref[...])
        if narrow:
            for q in range(N_CHIPS):
                o_ref[q] += res[:, q * ns:(q + 1) * ns]
        else:
            o_ref[...] += res

    if narrow:
        out_spec = pl.BlockSpec((N_CHIPS, tk, ns), lambda i, j, s: (0, i, 0))
        out_shape = jax.ShapeDtypeStruct((N_CHIPS, k, ns), F32)
    elif col_shards:
        out_spec = pl.BlockSpec((None, tk, tn), lambda i, j, s: (j // per, i, j % per))
        out_shape = jax.ShapeDtypeStruct((N_CHIPS, k, ns), F32)
    else:
        out_spec = pl.BlockSpec((tk, tn), lambda i, j, s: (i, j))
        out_shape = jax.ShapeDtypeStruct((k, n), F32)
    return pl.pallas_call(
        _ordered(body, 2, after), name=name, grid=(k // tk, n // tn, t // tt),
        in_specs=[pl.BlockSpec((tt, tk), lambda i, j, s: (s, i)), pl.BlockSpec((tt, tn), lambda i, j, s: (s, j))]
        + [_ANY] * len(after),
        out_specs=out_spec, out_shape=out_shape,
        compiler_params=_cp(("arbitrary", "arbitrary", "arbitrary")),
    )(a, g, *after)


def _input_grad(dproj, ws, slots, dz1, x2, modv, nb, seq, after=()):
    t, d = x2.shape
    nq = len(ws)
    ns = ws[0].shape[1]
    tm = min(TM_DH, seq)
    ts = min(TS_MLP, tm)
    tpb = seq // tm

    def body(slot_ref, dp_ref, *refs):
        w_hbm = refs[:nq]
        dz1_ref, x_ref, mod_ref, gx_ref, db_ref, pb_ref, w_s, acc, sems = refs[nq:]
        i = pl.program_id(0)

        @pl.when(i == 0)
        def _():
            _load_weights(w_hbm, [w_s.at[slot_ref[k]] for k in range(nq)], sems)
            db_ref[...] = jnp.zeros_like(db_ref)

        @pl.when(i % tpb == 0)
        def _():
            pb_ref[...] = jnp.zeros_like(pb_ref)

        for sub in range(tm // ts):
            rows = slice(sub * ts, (sub + 1) * ts)
            for q in range(nq):
                dp = dp_ref[rows, q * ns:(q + 1) * ns]
                part = _mm_nt(dp, w_s[q])
                if q == 0:
                    acc[sub] = part
                else:
                    acc[sub] += part
                db_ref[q, 0:1, :] += _colsum(dp.astype(F32))
            dh = acc[sub]
            gx_ref[rows, :] = ALPHA * dz1_ref[rows, :] + dh * (1.0 + mod_ref[1:2, :])
            pb_ref[0:1, :] += _colsum(dh * x_ref[rows, :])
            pb_ref[1:2, :] += _colsum(dh)

    tok = lambda i, s: (i, 0)
    in_specs = [pl.BlockSpec((tm, nq * ns), tok)] + [_ANY] * nq
    in_specs += [pl.BlockSpec((tm, d), tok), pl.BlockSpec((tm, d), tok),
                 pl.BlockSpec((None, 8, d), lambda i, s: (i // tpb, 0, 0))] + [_ANY] * len(after)
    return pl.pallas_call(
        _ordered(body, 5 + nq, after), name="input_grad",
        grid_spec=pltpu.PrefetchScalarGridSpec(
            num_scalar_prefetch=1, grid=(t // tm,), in_specs=in_specs,
            out_specs=[pl.BlockSpec((tm, d), tok), pl.BlockSpec((nq, 8, ns), lambda i, s: (0, 0, 0)),
                       pl.BlockSpec((None, 8, d), lambda i, s: (i // tpb, 0, 0))],
            scratch_shapes=[pltpu.VMEM((nq, d, ns), BF16), pltpu.VMEM((tm // ts, ts, d), F32),
                            pltpu.SemaphoreType.DMA((nq,))]),
        out_shape=[jax.ShapeDtypeStruct((t, d), F32), jax.ShapeDtypeStruct((nq, 8, ns), F32),
                   jax.ShapeDtypeStruct((nb, 8, d), F32)],
        compiler_params=_cp(("arbitrary",)),
    )(slots, dproj, *ws, dz1, x2, modv, *after)


def _rows128(v):
    flat = v.reshape(-1, HEAD)
    pad = (-flat.shape[0]) % 8
    return jnp.pad(flat, ((0, pad), (0, 0))) if pad else flat


def kernel(x, c, w_ada, b_ada, w_in, b_in, w_conv, b_conv, w_rg_a, b_rg_a, w_rg_x, b_rg_x, lru_lambda, w_sp, b_sp, ln_v_g, ln_v_b, w_o_lru, w_o_sgu, w_out, ln1_g, ln1_b, w_up, w_down, ln2_g, ln2_b, loss_target, m_w_ada, m_b_ada, m_w_in, m_b_in, m_w_conv, m_b_conv, m_w_rg_a, m_b_rg_a, m_w_rg_x, m_b_rg_x, m_lru_lambda, m_w_sp, m_b_sp, m_ln_v_g, m_ln_v_b, m_w_o_lru, m_w_o_sgu, m_w_out, m_ln1_g, m_ln1_b, m_w_up, m_w_down, m_ln2_g, m_ln2_b, v_w_ada, v_b_ada, v_w_in, v_b_in, v_w_conv, v_b_conv, v_w_rg_a, v_b_rg_a, v_w_rg_x, v_b_rg_x, v_lru_lambda, v_w_sp, v_b_sp, v_ln_v_g, v_ln_v_b, v_w_o_lru, v_w_o_sgu, v_w_out, v_ln1_g, v_ln1_b, v_w_up, v_w_down, v_ln2_g, v_ln2_b):
    given = dict(locals())
    nb, seq, d = x.shape
    t = nb * seq
    w_lru = LRU_HEADS * HEAD
    d_sgu = SGU_GROUPS * HEAD
    xi, yi, ci = lax.axis_index("x"), lax.axis_index("y"), lax.axis_index("c")
    chip = 2 * xi + yi
    dev = 2 * chip + ci
    cidx = jnp.reshape(ci, (1,)).astype(jnp.int32)

    x2 = x.reshape(t, d)
    target = loss_target.reshape(t, d)

    big = ["w_in", "w_o_lru", "w_o_sgu", "w_out", "w_up", "w_down"]
    shards_a = [w_in[0].astype(BF16)]
    shards_b = [given[n][0].astype(BF16) for n in big[1:]]
    pidx = jnp.reshape(chip, (1,)).astype(jnp.int32)

    c_rows = _rows128(c)
    wconv_rows = _rows128(w_conv[0])
    slab0 = _all_gather_small(jnp.concatenate([c_rows, wconv_rows], axis=0), "gather_c_wconv")
    slab0 = slab0.reshape(N_DEV, -1, HEAD)
    c_all = slab0[:, :c_rows.shape[0]].reshape(N_DEV * nb, d)
    n_wc = CONV_WIDTH * (w_lru // N_CHIPS) // HEAD
    wc = slab0[0::2, c_rows.shape[0]:c_rows.shape[0] + n_wc].reshape(N_CHIPS, CONV_WIDTH, w_lru // N_CHIPS)
    w_conv_full = jnp.transpose(wc, (1, 0, 2)).reshape(CONV_WIDTH, w_lru)

    n_ada = w_ada.shape[2]
    b_ada_cols = lax.dynamic_slice(b_ada, (0, chip * n_ada), (1, n_ada))
    mod_cols = _ada_fwd(c_all, w_ada[0], b_ada_cols)
    half = (N_DEV * nb) // 2
    mod_half = lax.dynamic_slice(mod_cols, (ci * half, 0), (half, n_ada))
    mod_g = _all_gather_small(mod_half, "gather_mod").reshape(N_CHIPS, 2, half, n_ada)
    mod_all = jnp.transpose(mod_g, (1, 2, 0, 3)).reshape(N_DEV * nb, N_CHIPS * n_ada)
    mod_loc = lax.dynamic_slice(mod_all, (dev * nb, 0), (nb, N_CHIPS * n_ada)).reshape(nb, 6, d)
    modv = jnp.pad(mod_loc, ((0, 0), (0, 2), (0, 0)))

    lru_w = (w_conv_full, b_conv, w_rg_a[0], b_rg_a, w_rg_x[0], b_rg_x, lru_lambda)
    b_sp_t = jnp.transpose(b_sp[0])

    land = lambda s: jax.ShapeDtypeStruct((N_CHIPS,) + s.shape, s.dtype)
    sds = lambda s: jax.ShapeDtypeStruct(s.shape, s.dtype)
    started_a = _split_start(shards_a, [sds(shards_a[0])] * 2, _peer_gather_copies((0, 1)), 2, "gather_w_in_near_start",
                             after=(modv,))
    shards_b, shards_c = shards_b[:3], shards_b[3:]

    ids = lambda *v: jnp.stack(v).astype(jnp.int32)
    modv_t = modv + started_a[-1][0:1, 0:1]
    proj, h = _proj_fwd(x2, modv_t, [started_a[2]], ids(chip), b_in, seq, "proj_fwd_own")
    own_a, lands_a = _split_wait(started_a, 1, _peer_gather_copies((0, 1)), "gather_w_in_near_wait", after=(proj,))
    started_f = _split_start(own_a, [sds(own_a[0])], _far_gather_copies, 1, "gather_w_in_far_start", after=(lands_a[0],))
    started_b = _split_start(shards_b, [land(s) for s in shards_b], _gather_copies, 3 * len(shards_b),
                             "gather_w_mix_start", after=(started_f[-1],))
    started_c = _split_start(shards_c, [land(s) for s in shards_c], _gather_copies, 3 * len(shards_c),
                             "gather_w_mlp_start", after=(started_b[-1],))
    modv_t = modv + started_c[-1][0:1, 0:1]
    (proj,) = _proj_fwd(x2, modv_t, lands_a, ids(chip ^ 1, chip ^ 2), b_in, seq, "proj_fwd_near", proj_in=proj)
    own_a, land_f = _split_wait(started_f, 1, _far_gather_copies, "gather_w_in_far_wait", after=(proj,))
    (proj,) = _proj_fwd(x2, modv, land_f, ids(chip ^ 3), b_in, seq, "proj_fwd_far", proj_in=proj)
    w_in_shards, w_in_chips = own_a + lands_a + land_f, ids(chip, chip ^ 1, chip ^ 2, chip ^ 3)
    a, inp = _lru_prep(proj, lru_w, nb, seq)
    a3 = a.reshape(nb, seq, w_lru)
    hs = _scan(a3, inp.reshape(nb, seq, w_lru), False, "lru_scan", BF16).reshape(t, w_lru)
    y_sgu = _sgu_fwd(proj, w_sp[0], b_sp_t, ln_v_g, ln_v_b)
    shards_b, lands_b = _split_wait(started_b, len(shards_b), _gather_copies, "gather_w_mix_wait", after=(hs, y_sgu))
    w_o_lru_g, w_o_sgu_g, w_out_g = _fill_own_slot(lands_b, shards_b, pidx, ["own_" + n for n in big[1:4]])
    w_o_lru_g = w_o_lru_g.reshape(w_lru, d)
    w_out_g = w_out_g.reshape(d, d)
    yap, y_a, y_b, merged, mix, x1 = _mix_fwd(hs, proj, y_sgu, x2, modv, w_o_lru_g, w_o_sgu_g, w_out_g, ln1_g, ln1_b, seq)
    shards_c, lands_c = _split_wait(started_c, len(shards_c), _gather_copies, "gather_w_mlp_wait", after=(x1,))
    w_up_g, w_down_g = _fill_own_slot(lands_c, shards_c, pidx, ["own_" + n for n in big[4:]])
    w_down_g = w_down_g.reshape(-1, d)
    up, act, h2, dz2, df, st2, pb2 = _mlp_fwd(x1, modv, w_up_g, w_down_g, ln2_g, ln2_b, target, nb, seq)

    part = {}

    def to_sibling_start(group, tag, after=()):
        g4 = []
        for n in group:
            shard = given[n].shape[1:]
            g4.append(part[n].reshape(N_CHIPS, 2, shard[0] // 2, shard[1]))
        shapes = [jax.ShapeDtypeStruct((N_CHIPS,) + g.shape[2:], F32) for g in g4]
        return _split_start(g4, shapes, _to_sibling_copies, len(g4), "grads_to_sibling_start_" + tag, after)

    def to_chips_start(group, started, tag, after=()):
        g4, recv = _split_wait(started, len(group), _to_sibling_copies, "grads_to_sibling_wait_" + tag, after)
        own4 = [_add_own_half(g4[k], recv[k], cidx, "grad_pair_sum_" + n) for k, n in enumerate(group)]
        shapes = [jax.ShapeDtypeStruct((3,) + o.shape[1:], BF16) for o in own4]
        return _split_start(own4, shapes, _chip_exchange_copies, 3 * len(own4), "grads_chip_exchange_start_" + tag)

    def chips_finish(group, started, tag, after=()):
        own4, slots = _split_wait(started, len(group), _chip_exchange_copies, "grads_chip_exchange_wait_" + tag, after)
        return [_sum_own_and_peers(own4[k], slots[k], pidx, "grad_chip_sum_" + n) for k, n in enumerate(group)]

    dup, dz1, dmix, st1, pb1 = _mlp_bwd(df, up, w_down_g, w_up_g, dz2, x2, mix, modv, ln1_g, ln1_b, nb, seq)
    group1 = ["w_up", "w_down"]
    part["w_up"] = _weight_grad(h2, dup, True, "grad_w_up")
    part["w_down"] = _weight_grad(act, df, False, "grad_w_down")
    sib1 = to_sibling_start(group1, "mlp")
    dy_a, dy_b, dproj, dgl, dyl, dys = _mix_bwd(dmix, proj, y_a, y_b, hs, w_out_g, w_o_lru_g, w_o_sgu_g, seq,
                                                after=(sib1[-1],))
    group2 = ["w_o_lru", "w_o_sgu", "w_out"]
    part["w_o_lru"] = _weight_grad(yap, dy_a, False, "grad_w_o_lru")
    part["w_o_sgu"] = _weight_grad(y_sgu, dy_b, True, "grad_w_o_sgu")
    part["w_out"] = _weight_grad(merged, dmix, False, "grad_w_out")
    chips1 = to_chips_start(group1, sib1, "mlp", after=(dys, part["w_o_lru"], part["w_o_sgu"], part["w_out"]))
    sib2 = to_sibling_start(group2, "mix", after=(chips1[-1],))
    du, dv, g_w_sp, st_sgu, g_b_sp_t = _sgu_bwd(proj, dys, w_sp[0], b_sp_t, ln_v_g, ln_v_b, after=(sib2[-1],))
    dyl3 = dyl.reshape(nb, seq, w_lru)
    e = _scan(a3, dyl3, True, "lru_scan_bwd", F32).reshape(t, w_lru)
    chips2 = to_chips_start(group2, sib2, "mix", after=(e, du))
    dproj = lax.dynamic_update_slice(dproj, dgl, (0, w_lru))
    dproj = lax.dynamic_update_slice(dproj, du, (0, 2 * w_lru))
    dproj = lax.dynamic_update_slice(dproj, dv, (0, 2 * w_lru + d_sgu))
    dproj, g_w_rg_a, g_w_rg_x, st_lru = _lru_bwd(proj, hs, e, dyl, lru_w, nb, seq, dproj, after=(chips2[-1],))

    didx = jnp.reshape(dev, (1,)).astype(jnp.int32)
    early = [
        ("w_conv", st_lru[4:8]), ("b_conv", st_lru[3]), ("w_rg_a", g_w_rg_a), ("b_rg_a", st_lru[0]),
        ("w_rg_x", g_w_rg_x), ("b_rg_x", st_lru[1]), ("lru_lambda", st_lru[2]), ("w_sp", g_w_sp),
        ("b_sp", jnp.transpose(g_b_sp_t[:, :SGU_GROUPS])), ("ln_v_g", st_sgu[0]), ("ln_v_b", st_sgu[1]),
        ("ln1_g", st1[0]), ("ln1_b", st1[1]), ("ln2_g", st2[0]), ("ln2_b", st2[1]),
    ]
    pieces_e = [_rows128(v) for _, v in early]
    slab_e = jnp.concatenate(pieces_e, axis=0)
    slab_e = jnp.pad(slab_e, ((0, (-slab_e.shape[0]) % TR_EW), (0, 0)))
    small_st = _split_start([slab_e], [jax.ShapeDtypeStruct((N_DEV,) + slab_e.shape, F32)], _all_devices_copies, N_DEV - 1,
                            "small_grads_start")

    group3 = ["w_in"]
    part["w_in"] = _weight_grad(h, dproj, True, "grad_w_in", after=(small_st[-1],))
    sib3 = to_sibling_start(group3, "in")
    chips3 = to_chips_start(group3, sib3, "in")
    grad_x2, g_b_in4, pb0 = _input_grad(dproj, w_in_shards, w_in_chips, dz1, x2, modv, nb, seq, after=(chips3[-1],))
    halves12 = chips_finish(group1, chips1, "mlp", after=(grad_x2,)) + chips_finish(group2, chips2, "mix", after=(grad_x2,))
    swap12 = _split_start(halves12, [jax.ShapeDtypeStruct(hv.shape, F32) for hv in halves12], _swap_copies, len(halves12),
                          "grads_swap_start")
    loss = lax.psum(st2[2, 0] + swap12[-1][0, 0], ("x", "y", "c"))
    grads = {}

    dmod_loc = jnp.stack([pb0[:, 1], pb0[:, 0], pb1[:, 2], pb1[:, 1], pb1[:, 0], pb2[:, 0]], axis=1)
    late = [("dmod", dmod_loc), ("b_in", g_b_in4[:, 0])]
    pieces_l = [_rows128(v) for _, v in late]
    slab_l = jnp.concatenate(pieces_l, axis=0)
    gathered = _all_gather_small(slab_l, "gather_small_grads", after=(swap12[-1],)).reshape(N_DEV, slab_l.shape[0], HEAD)
    rows_dmod = dmod_loc.size // HEAD
    dmod_all = gathered[:, :rows_dmod].reshape(N_DEV * nb, 6 * d)
    grads["b_in"] = _sum_slots(gathered[:, rows_dmod:], "grad_b_in_sum").reshape(1, -1)

    (slab_e,), (lands_e,) = _split_wait(small_st, 1, _all_devices_copies, "small_grads_wait", after=(gathered,))
    summed = _sum_devices(lands_e, slab_e, didx, "small_grad_sum")
    off = 0
    for (n, v), piece in zip(early, pieces_e):
        grads[n] = summed[off:off + v.size // HEAD].reshape(v.shape)
        off += piece.shape[0]

    mine12, theirs12 = _split_wait(swap12, len(halves12), _swap_copies, "grads_swap_wait", after=(summed,))
    (mine3,) = chips_finish(group3, chips3, "in", after=(summed,))
    (theirs3,) = _exchange([mine3], [jax.ShapeDtypeStruct(mine3.shape, F32)], _swap_copies, 1, "grads_swap_w_in")
    mine = dict(zip(group1 + group2 + group3, mine12 + [mine3]))
    theirs = dict(zip(group1 + group2 + group3, theirs12 + [theirs3]))

    dmod_cols = lax.dynamic_slice(dmod_all, (0, chip * n_ada), (N_DEV * nb, n_ada))
    grads["w_ada"], grads["b_ada"] = _ada_bwd(c_all, dmod_all, dmod_cols)
    n_wcs = w_lru // N_CHIPS
    grads["w_conv"] = lax.dynamic_slice(grads["w_conv"], (0, chip * n_wcs), (CONV_WIDTH, n_wcs))

    names = ['w_ada', 'b_ada', 'w_in', 'b_in', 'w_conv', 'b_conv', 'w_rg_a', 'b_rg_a', 'w_rg_x', 'b_rg_x', 'lru_lambda',
             'w_sp', 'b_sp', 'ln_v_g', 'ln_v_b', 'w_o_lru', 'w_o_sgu', 'w_out', 'ln1_g', 'ln1_b', 'w_up', 'w_down',
             'ln2_g', 'ln2_b']
    two_d = lambda v: v.reshape(-1, v.shape[-1])
    done = {}
    small_names = [n for n in names if n not in big and n != "w_ada"]
    small_out = _adamw_many([(two_d(given[n]), two_d(grads[n].reshape(given[n].shape)), two_d(given["m_" + n]),
                              two_d(given["v_" + n])) for n in small_names], "adamw_small")
    for n, res in zip(small_names, small_out):
        done[n] = (grads[n],) + tuple(res)
    for n in big + ["w_ada"]:
        w2, m2, v2 = two_d(given[n]), two_d(given["m_" + n]), two_d(given["v_" + n])
        if n in big:
            done[n] = _adamw_halves(w2, mine[n], theirs[n], m2, v2, cidx, "adamw_" + n)
        else:
            done[n] = (grads[n],) + tuple(_adamw(w2, two_d(grads[n]), m2, v2, "adamw_" + n))
    outs = [[done[n][k].reshape(given[n].shape) for n in names] for k in range(4)]
    return (loss, grad_x2.reshape(nb, seq, d), *outs[0], *outs[1], *outs[2], *outs[3])
```

```python
import functools
import math

import jax
import jax.numpy as jnp
from jax import lax
from jax.experimental import pallas as pl
from jax.experimental.pallas import tpu as pltpu

F32 = jnp.float32
BF16 = jnp.bfloat16
MESH = pl.DeviceIdType.MESH

N_CHIPS = 4
N_DEV = 8
LRU_HEADS = 10
HEAD = 128
SGU_GROUPS = 6
SGU_CHUNK = 64
CONV_WIDTH = 4
LRU_C = 8.0
ALPHA = 2.0 ** 0.25
LN_EPS = 1e-5
ADAM_LR, ADAM_B1, ADAM_B2, ADAM_EPS, ADAM_WD, ADAM_STEP = 0.001, 0.9, 0.999, 1e-08, 0.01, 10

VMEM_LIMIT = 56 * 1024 * 1024
VMEM_LIMIT_MAX = 62 * 1024 * 1024
TM_PROJ = 1024
TM_MIX = 256
TM_MLP = 512
TS_MLP = 256
TM_SGU = 512
TM_DH = 512
TT_DW = 2048
TC_SCAN = 256
TR_EW = 256


def _cp(sem=None, limit=None):
    return pltpu.CompilerParams(dimension_semantics=sem, vmem_limit_bytes=limit or VMEM_LIMIT)


def _mm(a, b):
    return jnp.dot(a.astype(BF16), b.astype(BF16), preferred_element_type=F32)


def _mm_nt(a, b):
    return lax.dot_general(a.astype(BF16), b.astype(BF16), (((1,), (1,)), ((), ())), preferred_element_type=F32)


def _mm_tn(a, b):
    return lax.dot_general(a.astype(BF16), b.astype(BF16), (((0,), (0,)), ((), ())), preferred_element_type=F32)


def _sigmoid(x):
    return 1.0 / (1.0 + jnp.exp(-x))


def _sigmoid_t(x):
    return 0.5 * jnp.tanh(0.5 * x) + 0.5


_GELU_K = math.sqrt(2.0 / math.pi)


def _gelu(x):
    t = jnp.tanh(_GELU_K * (x + 0.044715 * (x * x * x)))
    return 0.5 * x * (1.0 + t)


def _gelu_and_grad(x):
    x2 = x * x
    t = jnp.tanh(_GELU_K * (x + 0.044715 * (x2 * x)))
    g = 0.5 * x * (1.0 + t)
    dg = 0.5 * (1.0 + t) + 0.5 * x * (1.0 - t * t) * (_GELU_K * (1.0 + 3.0 * 0.044715 * x2))
    return g, dg


def _ln_stats(z):
    mu = jnp.mean(z, axis=-1, keepdims=True)
    zc = z - mu
    var = jnp.mean(zc * zc, axis=-1, keepdims=True)
    rstd = lax.rsqrt(var + LN_EPS)
    return zc * rstd, rstd


def _ln_bwd(dxh, xhat, rstd):
    m1 = jnp.mean(dxh, axis=-1, keepdims=True)
    m2 = jnp.mean(dxh * xhat, axis=-1, keepdims=True)
    return rstd * (dxh - m1 - xhat * m2)


def _colsum(v):
    return jnp.sum(v, axis=0, keepdims=True)


def _shift_down(v, j):
    if j == 0:
        return v
    rows = lax.broadcasted_iota(jnp.int32, v.shape, 0)
    return jnp.where(rows >= j, pltpu.roll(v, j, 0), 0.0)


def _shift_up(v, j):
    if j == 0:
        return v
    n = v.shape[0]
    rows = lax.broadcasted_iota(jnp.int32, v.shape, 0)
    return jnp.where(rows < n - j, pltpu.roll(v, n - j, 0), 0.0)


def _load_weights(srcs, dsts, sems):
    cps = [pltpu.make_async_copy(s, dd, sems.at[k]) for k, (s, dd) in enumerate(zip(srcs, dsts))]
    for cp in cps:
        cp.start()
    for cp in cps:
        cp.wait()


def _my_pos():
    return lax.axis_index("x"), lax.axis_index("y"), lax.axis_index("c")


def _all_gather_small(v, name, after=()):
    m_per, n = v.shape

    def body(x_ref, out_ref, send_sems, recv_sems, local_sem):
        x, y, c = _my_pos()
        me, sibling = (x, y, c), (x, y, 1 - c)
        chips = [(1 - x, y), (x, 1 - y), (1 - x, 1 - y)]

        def rows(px, py, pc):
            return out_ref.at[pl.ds((4 * px + 2 * py + pc) * m_per, m_per), :]

        def copy(k, block, to, src=None):
            return pltpu.make_async_remote_copy(
                src_ref=rows(*block) if src is None else src, dst_ref=rows(*block),
                send_sem=send_sems.at[k], recv_sem=recv_sems.at[k], device_id=to, device_id_type=MESH)

        mine = pltpu.make_async_copy(x_ref, rows(*me), local_sem)
        mine.start()
        first = [copy(0, me, sibling, src=x_ref)]
        first += [copy(1 + j, me, (*chip, c), src=x_ref) for j, chip in enumerate(chips)]
        for cp in first:
            cp.start()
        passed = [copy(4 + j, (*chip, c), sibling) for j, chip in enumerate(chips)]
        for j, chip in enumerate(chips):
            copy(1 + j, (*chip, c), me).wait_recv()
            passed[j].start()
        copy(0, sibling, me).wait_recv()
        for j, chip in enumerate(chips):
            copy(4 + j, (*chip, 1 - c), me).wait_recv()
        for cp in first + passed:
            cp.wait_send()
        mine.wait()

    return pl.pallas_call(
        _ordered(body, 1, after), name=name,
        out_shape=jax.ShapeDtypeStruct((N_DEV * m_per, n), v.dtype),
        in_specs=[pl.BlockSpec(memory_space=pltpu.VMEM)] + [pl.BlockSpec(memory_space=pl.ANY)] * len(after),
        out_specs=pl.BlockSpec(memory_space=pltpu.VMEM),
        scratch_shapes=[pltpu.SemaphoreType.DMA((7,)), pltpu.SemaphoreType.DMA((7,)), pltpu.SemaphoreType.DMA],
        compiler_params=pltpu.CompilerParams(vmem_limit_bytes=VMEM_LIMIT),
    )(v, *after)


_HBM = pl.BlockSpec(memory_space=pltpu.HBM)
_ANY = pl.BlockSpec(memory_space=pl.ANY)
_SEM = pl.BlockSpec(memory_space=pltpu.SEMAPHORE)
_EFFECT = pltpu.SideEffectType.DATAFLOW_SIDE_EFFECTING


def _ordered(body, n_in, after):
    k = len(after)
    if not k:
        return body
    return lambda *refs: body(*refs[:n_in], *refs[n_in + k:])


def _gather_copies(ins, lands, send_sems, recv_sems):
    x, y, c = _my_pos()
    p = 2 * x + y
    peers = [(x, 1 - y), (1 - x, y), (1 - x, 1 - y)]
    sends, recvs = [], []
    for k in range(len(ins)):
        for j, (qx, qy) in enumerate(peers):
            sems = dict(send_sem=send_sems.at[3 * k + j], recv_sem=recv_sems.at[3 * k + j],
                        device_id=(qx, qy, c), device_id_type=MESH)
            sends.append(pltpu.make_async_remote_copy(src_ref=ins[k], dst_ref=lands[k].at[p], **sems))
            recvs.append(pltpu.make_async_remote_copy(src_ref=ins[k], dst_ref=lands[k].at[2 * qx + qy], **sems))
    return sends, recvs


def _peer_gather_copies(peers):
    def copies(ins, lands, send_sems, recv_sems):
        x, y, c = _my_pos()
        where = [(x, 1 - y), (1 - x, y), (1 - x, 1 - y)]
        cps = [pltpu.make_async_remote_copy(
            src_ref=ins[0], dst_ref=lands[j], send_sem=send_sems.at[j], recv_sem=recv_sems.at[j],
            device_id=(*where[j], c), device_id_type=MESH) for j in peers]
        return cps, cps
    return copies


def _far_gather_copies(ins, lands, send_sems, recv_sems):
    x, y, c = _my_pos()
    cps = [pltpu.make_async_remote_copy(
        src_ref=ins[0], dst_ref=lands[0], send_sem=send_sems.at[0], recv_sem=recv_sems.at[0],
        device_id=(1 - x, 1 - y, c), device_id_type=MESH)]
    return cps, cps


def _to_sibling_copies(ins, lands, send_sems, recv_sems):
    x, y, c = _my_pos()
    cps = [pltpu.make_async_remote_copy(
        src_ref=ins[k].at[:, 1 - c], dst_ref=lands[k], send_sem=send_sems.at[k], recv_sem=recv_sems.at[k],
        device_id=(x, y, 1 - c), device_id_type=MESH) for k in range(len(ins))]
    return cps, cps


def _chip_exchange_copies(ins, lands, send_sems, recv_sems):
    x, y, c = _my_pos()
    peers = [(x, 1 - y), (1 - x, y), (1 - x, 1 - y)]
    cps = []
    for k in range(len(ins)):
        for j, (qx, qy) in enumerate(peers):
            cps.append(pltpu.make_async_remote_copy(
                src_ref=ins[k].at[2 * qx + qy], dst_ref=lands[k].at[j], send_sem=send_sems.at[3 * k + j],
                recv_sem=recv_sems.at[3 * k + j], device_id=(qx, qy, c), device_id_type=MESH))
    return cps, cps


def _all_devices_copies(ins, lands, send_sems, recv_sems):
    x, y, c = _my_pos()
    me = 4 * x + 2 * y + c
    sends, recvs = [], []
    for r in range(1, N_DEV):
        px = 1 - x if r & 4 else x
        py = 1 - y if r & 2 else y
        pc = 1 - c if r & 1 else c
        sems = dict(send_sem=send_sems.at[r - 1], recv_sem=recv_sems.at[r - 1], device_id=(px, py, pc), device_id_type=MESH)
        sends.append(pltpu.make_async_remote_copy(src_ref=ins[0], dst_ref=lands[0].at[me], **sems))
        recvs.append(pltpu.make_async_remote_copy(src_ref=ins[0], dst_ref=lands[0].at[4 * px + 2 * py + pc], **sems))
    return sends, recvs


def _swap_copies(ins, lands, send_sems, recv_sems):
    x, y, c = _my_pos()
    cps = [pltpu.make_async_remote_copy(
        src_ref=ins[k], dst_ref=lands[k], send_sem=send_sems.at[k], recv_sem=recv_sems.at[k],
        device_id=(x, y, 1 - c), device_id_type=MESH) for k in range(len(ins))]
    return cps, cps


def _split_start(ins, land_shapes, copies, n_sems, name, after=()):
    n, nl = len(ins), len(land_shapes)
    first_out = n + nl + len(after)

    def body(*refs):
        in_refs, land_refs = refs[:n], refs[n:n + nl]
        send_sems, recv_sems = refs[first_out:first_out + 2]
        token = refs[-1]
        sends, _ = copies(in_refs, land_refs, send_sems, recv_sems)
        for cp in sends:
            cp.start()
        token[...] = jnp.zeros_like(token)

    lands = [pltpu.with_memory_space_constraint(lax.empty(s.shape, s.dtype), pltpu.HBM) for s in land_shapes]
    ins = [pltpu.with_memory_space_constraint(s, pltpu.HBM) for s in ins]
    return pl.pallas_call(
        body, name=name,
        out_shape=(pltpu.SemaphoreType.DMA((n_sems,)), pltpu.SemaphoreType.DMA((n_sems,)),
                   *[pltpu.HBM(s.shape, s.dtype) for s in ins], *[pltpu.HBM(s.shape, s.dtype) for s in lands],
                   jax.ShapeDtypeStruct((8, HEAD), F32)),
        in_specs=[_HBM] * (n + nl) + [pl.BlockSpec(memory_space=pl.ANY)] * len(after),
        out_specs=(_SEM, _SEM, *([_HBM] * (n + nl)), pl.BlockSpec(memory_space=pltpu.VMEM)),
        input_output_aliases={k: 2 + k for k in range(n + nl)},
        compiler_params=pltpu.CompilerParams(has_side_effects=_EFFECT),
    )(*ins, *lands, *after)


def _split_wait(started, n, copies, name, after=()):
    send_sems, recv_sems = started[0], started[1]
    bufs = started[2:-1]
    nb = len(bufs)

    def body(*refs):
        in_refs, land_refs = refs[:n], refs[n:nb]
        sends, recvs = copies(in_refs, land_refs, refs[nb], refs[nb + 1])
        for cp in sends:
            cp.wait_send()
        for cp in recvs:
            cp.wait_recv()

    outs = pl.pallas_call(
        body, name=name,
        out_shape=tuple(pltpu.HBM(s.shape, s.dtype) for s in bufs),
        in_specs=[_HBM] * nb + [_SEM, _SEM] + [pl.BlockSpec(memory_space=pl.ANY)] * len(after),
        out_specs=tuple([_HBM] * nb),
        input_output_aliases={k: k for k in range(nb)},
        compiler_params=pltpu.CompilerParams(has_side_effects=_EFFECT),
    )(*bufs, send_sems, recv_sems, *after)
    return list(outs[:n]), list(outs[n:])


def _fill_own_slot(gathered, shards, pidx, names):
    outs = []
    for g, s, name in zip(gathered, shards, names):
        r, cdim = s.shape
        tr = _row_tile(r)

        def body(p_ref, s_ref, g_ref, o_ref):
            o_ref[...] = s_ref[...]

        outs.append(pl.pallas_call(
            body, name=name,
            grid_spec=pltpu.PrefetchScalarGridSpec(
                num_scalar_prefetch=1, grid=(r // tr,),
                in_specs=[pl.BlockSpec((tr, cdim), lambda i, p: (i, 0)), pl.BlockSpec(memory_space=pl.ANY)],
                out_specs=pl.BlockSpec((None, tr, cdim), lambda i, p: (p[0], i, 0))),
            out_shape=jax.ShapeDtypeStruct(g.shape, g.dtype),
            input_output_aliases={2: 0},
            compiler_params=_cp(("arbitrary",)),
        )(pidx, s, g))
    return outs


def _sum_own_and_peers(own4, slots, pidx, name):
    _, rh, cdim = own4.shape
    tr = _row_tile(rh)

    def body(p_ref, own_ref, s_ref, o_ref):
        acc = own_ref[...].astype(F32)
        for j in range(3):
            acc = acc + s_ref[j].astype(F32)
        o_ref[...] = acc

    return pl.pallas_call(
        body, name=name,
        grid_spec=pltpu.PrefetchScalarGridSpec(
            num_scalar_prefetch=1, grid=(rh // tr,),
            in_specs=[pl.BlockSpec((None, tr, cdim), lambda i, p: (p[0], i, 0)),
                      pl.BlockSpec((3, tr, cdim), lambda i, p: (0, i, 0))],
            out_specs=pl.BlockSpec((tr, cdim), lambda i, p: (i, 0))),
        out_shape=jax.ShapeDtypeStruct((rh, cdim), F32),
        compiler_params=_cp(("arbitrary",)),
    )(pidx, own4, slots)


def _exchange(ins, land_shapes, copies, n_sems, name):
    n, nl = len(ins), len(land_shapes)

    def body(*refs):
        sends, recvs = copies(refs[:n], refs[n:n + nl], refs[n + nl], refs[n + nl + 1])
        for cp in sends:
            cp.start()
        for cp in sends:
            cp.wait_send()
        for cp in recvs:
            cp.wait_recv()

    any_spec = pl.BlockSpec(memory_space=pl.ANY)
    return pl.pallas_call(
        body, name=name,
        out_shape=[jax.ShapeDtypeStruct(s.shape, s.dtype) for s in land_shapes],
        in_specs=[any_spec] * n, out_specs=[any_spec] * nl,
        scratch_shapes=[pltpu.SemaphoreType.DMA((n_sems,)), pltpu.SemaphoreType.DMA((n_sems,))],
    )(*ins)


def _row_tile(r):
    t = min(TR_EW, r)
    while r % t:
        t //= 2
    return t


def _add_own_half(g4, recv, cidx, name):
    _, _, rh, cdim = g4.shape
    tr = _row_tile(rh)

    def body(c_ref, a_ref, b_ref, o_ref):
        o_ref[...] = (a_ref[...] + b_ref[...]).astype(BF16)

    return pl.pallas_call(
        body, name=name,
        grid_spec=pltpu.PrefetchScalarGridSpec(
            num_scalar_prefetch=1, grid=(N_CHIPS, rh // tr),
            in_specs=[pl.BlockSpec((None, None, tr, cdim), lambda q, i, c: (q, c[0], i, 0)),
                      pl.BlockSpec((None, tr, cdim), lambda q, i, c: (q, i, 0))],
            out_specs=pl.BlockSpec((None, tr, cdim), lambda q, i, c: (q, i, 0))),
        out_shape=jax.ShapeDtypeStruct(recv.shape, BF16),
        compiler_params=_cp(("arbitrary", "arbitrary")),
    )(cidx, g4, recv)


def _sum_slots(v, name):
    n, r, cdim = v.shape
    tr = _row_tile(r)

    def body(v_ref, o_ref):
        acc = v_ref[0].astype(F32)
        for k in range(1, n):
            acc = acc + v_ref[k].astype(F32)
        o_ref[...] = acc

    return pl.pallas_call(
        body, name=name, grid=(r // tr,),
        in_specs=[pl.BlockSpec((n, tr, cdim), lambda i: (0, i, 0))],
        out_specs=pl.BlockSpec((tr, cdim), lambda i: (i, 0)),
        out_shape=jax.ShapeDtypeStruct((r, cdim), F32),
        compiler_params=_cp(("arbitrary",)),
    )(v)


def _sum_devices(lands, own, didx, name):
    _, r, cdim = lands.shape
    tr = _row_tile(r)

    def body(d_ref, l_ref, own_ref, o_ref):
        acc = jnp.where(d_ref[0] == 0, own_ref[...], l_ref[0])
        for dv in range(1, N_DEV):
            acc = acc + jnp.where(d_ref[0] == dv, own_ref[...], l_ref[dv])
        o_ref[...] = acc

    return pl.pallas_call(
        body, name=name,
        grid_spec=pltpu.PrefetchScalarGridSpec(
            num_scalar_prefetch=1, grid=(r // tr,),
            in_specs=[pl.BlockSpec((N_DEV, tr, cdim), lambda i, dd: (0, i, 0)), pl.BlockSpec((tr, cdim), lambda i, dd: (i, 0))],
            out_specs=pl.BlockSpec((tr, cdim), lambda i, dd: (i, 0))),
        out_shape=jax.ShapeDtypeStruct((r, cdim), F32),
        compiler_params=_cp(("arbitrary",)),
    )(didx, lands, own)


def _adamw_math(wv, gg, mv, vv):
    nm = ADAM_B1 * mv + (1.0 - ADAM_B1) * gg
    nv = ADAM_B2 * vv + (1.0 - ADAM_B2) * (gg * gg)
    m_hat = nm / (1.0 - ADAM_B1 ** ADAM_STEP)
    v_hat = nv / (1.0 - ADAM_B2 ** ADAM_STEP)
    return -ADAM_LR * (m_hat / (jnp.sqrt(v_hat) + ADAM_EPS) + ADAM_WD * wv), nm, nv


def _adamw_halves(w, mine, theirs, m, v, cidx, name):
    r, cdim = w.shape
    rh = r // 2
    tr = _row_tile(rh)
    nblk = rh // tr

    def body(c_ref, w_ref, a_ref, b_ref, m_ref, v_ref, g_ref, d_ref, nm_ref, nv_ref):
        gg = jnp.where(pl.program_id(0) == c_ref[0], a_ref[...], b_ref[...])
        g_ref[...] = gg
        d_ref[...], nm_ref[...], nv_ref[...] = _adamw_math(w_ref[...], gg, m_ref[...], v_ref[...])

    full = pl.BlockSpec((tr, cdim), lambda hh, i, c: (hh * nblk + i, 0))
    half = pl.BlockSpec((tr, cdim), lambda hh, i, c: (i, 0))
    return pl.pallas_call(
        body, name=name,
        grid_spec=pltpu.PrefetchScalarGridSpec(
            num_scalar_prefetch=1, grid=(2, nblk),
            in_specs=[full, half, half, full, full], out_specs=[full] * 4),
        out_shape=[jax.ShapeDtypeStruct((r, cdim), F32)] * 4,
        compiler_params=_cp(("arbitrary", "arbitrary")),
    )(cidx, w, mine, theirs, m, v)


def _adamw_many(params, name):
    n = len(params)

    def body(*refs):
        ins, outs = refs[:4 * n], refs[4 * n:]
        for k in range(n):
            w_ref, g_ref, m_ref, v_ref = ins[4 * k:4 * k + 4]
            outs[3 * k][...], outs[3 * k + 1][...], outs[3 * k + 2][...] = _adamw_math(
                w_ref[...], g_ref[...], m_ref[...], v_ref[...])

    flat = [a for p in params for a in p]
    res = pl.pallas_call(
        body, name=name,
        out_shape=[jax.ShapeDtypeStruct(p[0].shape, F32) for p in params for _ in range(3)],
        compiler_params=pltpu.CompilerParams(vmem_limit_bytes=VMEM_LIMIT),
    )(*flat)
    return [res[3 * k:3 * k + 3] for k in range(n)]


def _adamw(w, g, m, v, name):
    r, cdim = w.shape
    tr = _row_tile(r) if r % 8 == 0 else r

    def body(w_ref, g_ref, m_ref, v_ref, d_ref, nm_ref, nv_ref):
        d_ref[...], nm_ref[...], nv_ref[...] = _adamw_math(w_ref[...], g_ref[...], m_ref[...], v_ref[...])

    spec = pl.BlockSpec((tr, cdim), lambda i: (i, 0))
    return pl.pallas_call(
        body, name=name, grid=(r // tr,), in_specs=[spec] * 4, out_specs=[spec] * 3,
        out_shape=[jax.ShapeDtypeStruct((r, cdim), F32)] * 3,
        compiler_params=_cp(("arbitrary",)),
    )(w, g, m, v)


def _ada_fwd(c_all, w_ada, b_cols):
    nb, _ = c_all.shape
    n = w_ada.shape[1]

    def body(c_ref, w_ref, b_ref, o_ref):
        cv = c_ref[...]
        o_ref[...] = _mm(cv * _sigmoid(cv), w_ref[...]) + b_ref[...]

    return pl.pallas_call(
        body, name="ada_fwd", out_shape=jax.ShapeDtypeStruct((nb, n), F32),
        compiler_params=pltpu.CompilerParams(vmem_limit_bytes=VMEM_LIMIT),
    )(c_all, w_ada, b_cols)


def _ada_bwd(c_all, dmod_all, dmod_cols):
    d = c_all.shape[1]
    n = dmod_cols.shape[1]

    def body(c_ref, da_ref, dc_ref, gw_ref, gb_ref):
        cv = c_ref[...]
        gw_ref[...] = _mm_tn(cv * _sigmoid(cv), dc_ref[...])
        gb_ref[...] = _colsum(da_ref[...])

    return pl.pallas_call(
        body, name="ada_bwd",
        out_shape=[jax.ShapeDtypeStruct((d, n), F32), jax.ShapeDtypeStruct((1, dmod_all.shape[1]), F32)],
        compiler_params=pltpu.CompilerParams(vmem_limit_bytes=VMEM_LIMIT),
    )(c_all, dmod_all, dmod_cols)


def _proj_fwd(x2, modv, ws, cols, b_in, seq, name, proj_in=None):
    t, d = x2.shape
    n = len(ws)
    ns = ws[0].shape[1]
    tm = min(TM_PROJ, seq)
    tpb = seq // tm
    first = proj_in is None

    def body(c_ref, x_ref, mod_ref, *refs):
        w_refs, b_ref = refs[:n], refs[n]
        outs = refs[n + 1 if first else n + 2:]
        proj_ref, h_s = outs[0], outs[-1]
        s = pl.program_id(1)

        @pl.when(s == 0)
        def _():
            h = (x_ref[...] * (1.0 + mod_ref[1:2, :]) + mod_ref[0:1, :]).astype(BF16)
            h_s[...] = h
            if first:
                outs[1][...] = h

        for k in range(n):
            @pl.when(s == k)
            def _():
                proj_ref[...] = (jnp.dot(h_s[...], w_refs[k][...], preferred_element_type=F32) + b_ref[...]).astype(BF16)

    in_specs = [pl.BlockSpec((tm, d), lambda i, s, c: (i, 0)),
                pl.BlockSpec((None, 8, d), lambda i, s, c: (i // tpb, 0, 0))]
    in_specs += [pl.BlockSpec((d, ns), lambda i, s, c: (0, 0))] * n
    in_specs += [pl.BlockSpec((1, ns), lambda i, s, c: (0, c[s]))]
    out_specs = [pl.BlockSpec((tm, ns), lambda i, s, c: (i, c[s]))]
    out_shape = [jax.ShapeDtypeStruct((t, N_CHIPS * ns), BF16)]
    args = [cols, x2, modv, *ws, b_in]
    aliases = {}
    if first:
        out_specs.append(pl.BlockSpec((tm, d), lambda i, s, c: (i, 0)))
        out_shape.append(jax.ShapeDtypeStruct((t, d), BF16))
    else:
        in_specs.append(_ANY)
        args.append(proj_in)
        aliases = {len(args) - 1: 0}
    return pl.pallas_call(
        body, name=name,
        grid_spec=pltpu.PrefetchScalarGridSpec(
            num_scalar_prefetch=1, grid=(t // tm, n), in_specs=in_specs, out_specs=out_specs,
            scratch_shapes=[pltpu.VMEM((tm, d), BF16)]),
        out_shape=out_shape, input_output_aliases=aliases,
        compiler_params=_cp(("arbitrary", "arbitrary")),
    )(*args)


def _lru_gates(xl, wc_ref, bc_ref, wa_ref, ba_ref, wx_ref, bx_ref, lam_ref):
    xc = bc_ref[...] + wc_ref[CONV_WIDTH - 1:CONV_WIDTH, :] * xl
    for k in range(CONV_WIDTH - 1):
        xc = xc + wc_ref[k:k + 1, :] * _shift_down(xl, CONV_WIDTH - 1 - k)
    r = _sigmoid(_mm(xc, wa_ref[...]) + ba_ref[...])
    gi = _sigmoid_t(_mm(xc, wx_ref[...]) + bx_ref[...])
    nl = -lam_ref[...]
    e = jnp.exp(-jnp.abs(nl))
    u = 1.0 + e
    dlt = u - 1.0
    log1p_e = jnp.where(dlt == 0.0, e, jnp.log(u) * (e / jnp.where(dlt == 0.0, 1.0, dlt)))
    big_l = -LRU_C * (jnp.maximum(nl, 0.0) + log1p_e)
    la = big_l * r
    a = jnp.exp(la)
    m2 = jnp.tanh(-la) * (a * a + 1.0)
    return xc, r, gi, big_l, a, m2


def _lru_prep(proj, lru_w, nb, seq):
    t = proj.shape[0]
    w = LRU_HEADS * HEAD
    w_conv, b_conv, w_a, b_a, w_x, b_x, lam = lru_w

    def body(x_ref, wc_ref, bc_ref, wa_ref, ba_ref, wx_ref, bx_ref, lam_ref, a_ref, inp_ref):
        xc, r, gi, big_l, a, m2 = _lru_gates(x_ref[...].astype(F32), wc_ref, bc_ref, wa_ref, ba_ref, wx_ref, bx_ref, lam_ref)
        a_ref[...] = a
        inp_ref[...] = jnp.sqrt(m2) * (gi * xc)

    col = lambda b, hd: (0, hd)
    head = lambda b, hd: (hd, 0, 0)
    tok = lambda b, hd: (b, hd)
    return pl.pallas_call(
        body, name="lru_prep", grid=(nb, LRU_HEADS),
        in_specs=[pl.BlockSpec((seq, HEAD), tok),
                  pl.BlockSpec((CONV_WIDTH, HEAD), col), pl.BlockSpec((1, HEAD), col),
                  pl.BlockSpec((None, HEAD, HEAD), head), pl.BlockSpec((1, HEAD), col),
                  pl.BlockSpec((None, HEAD, HEAD), head), pl.BlockSpec((1, HEAD), col),
                  pl.BlockSpec((1, HEAD), col)],
        out_specs=[pl.BlockSpec((seq, HEAD), tok)] * 2,
        out_shape=[jax.ShapeDtypeStruct((t, w), F32)] * 2,
        compiler_params=_cp(("arbitrary", "arbitrary")),
    )(proj, w_conv, b_conv, w_a, b_a, w_x, b_x, lam)


def _scan(a3, b3, reverse, name, out_dtype):
    nb, seq, w = a3.shape
    tc = min(TC_SCAN, seq)
    nchunk = seq // tc
    npair = tc // 16

    def combine(av, bv):
        rows = lax.broadcasted_iota(jnp.int32, av.shape, 0)
        for s in (1, 2, 4):
            if reverse:
                keep = rows < 8 - s
                a_sh, b_sh = pltpu.roll(av, 8 - s, 0), pltpu.roll(bv, 8 - s, 0)
            else:
                keep = rows >= s
                a_sh, b_sh = pltpu.roll(av, s, 0), pltpu.roll(bv, s, 0)
            bv = jnp.where(keep, bv + av * b_sh, bv)
            av = jnp.where(keep, av * a_sh, av)
        return av, bv

    def body(a_ref, b_ref, h_ref, carry):
        @pl.when(pl.program_id(0) == 0)
        def _():
            carry[...] = jnp.zeros_like(carry)

        for b in range(nb):
            def pair(j, hprev):
                jj = npair - 1 - j if reverse else j
                base = pl.multiple_of(jj * 16, 16)
                a16 = a_ref[b, pl.ds(base, 16), :]
                b16 = b_ref[b, pl.ds(base, 16), :].astype(F32)
                outs = [None, None]
                for k in ((1, 0) if reverse else (0, 1)):
                    av, bv = a16[8 * k:8 * k + 8, :], b16[8 * k:8 * k + 8, :]
                    av, bv = combine(av, av * bv if reverse else bv)
                    h = bv + av * hprev
                    outs[k] = h
                    hprev = jnp.broadcast_to(h[0:1, :] if reverse else h[7:8, :], (8, w))
                h_ref[b, pl.ds(base, 16), :] = jnp.concatenate(outs, axis=0).astype(out_dtype)
                return hprev

            carry[b] = lax.fori_loop(0, npair, pair, carry[b])

    imap = (lambda i: (0, nchunk - 1 - i, 0)) if reverse else (lambda i: (0, i, 0))
    spec = pl.BlockSpec((nb, tc, w), imap)
    return pl.pallas_call(
        body, name=name, grid=(nchunk,), in_specs=[spec, spec], out_specs=spec,
        out_shape=jax.ShapeDtypeStruct((nb, seq, w), out_dtype),
        scratch_shapes=[pltpu.VMEM((nb, 8, w), F32)],
        compiler_params=_cp(("arbitrary",)),
    )(a3, b3)


def _sgu_mask():
    ti = lax.broadcasted_iota(jnp.int32, (HEAD, HEAD), 0) // SGU_CHUNK
    si = lax.broadcasted_iota(jnp.int32, (HEAD, HEAD), 1) // SGU_CHUNK
    return si <= ti


def _sgu_specs(tm, d_sgu):
    pw = 256
    first_u = (2 * LRU_HEADS * HEAD) // pw
    n_piece = d_sgu // pw
    specs = [pl.BlockSpec((tm, pw), functools.partial(lambda i, k: (i, k), k=first_u + j)) for j in range(2 * n_piece)]
    return specs, n_piece


def _sgu_fwd(proj, w_sp, b_sp_t, ln_g, ln_b):
    t = proj.shape[0]
    d_sgu = SGU_GROUPS * HEAD
    tm = min(TM_SGU, t)
    nblk = tm // HEAD
    specs, n_piece = _sgu_specs(tm, d_sgu)

    def body(*refs):
        u = jnp.concatenate([r[...] for r in refs[:n_piece]], axis=1).astype(F32)
        v = jnp.concatenate([r[...] for r in refs[n_piece:2 * n_piece]], axis=1).astype(F32)
        w_ref, bt_ref, g_ref, b_ref, y_ref = refs[2 * n_piece:]
        ug = _gelu(u)
        xhat, _ = _ln_stats(_gelu(v))
        vn = (xhat * g_ref[...] + b_ref[...]).astype(BF16)
        mask = _sgu_mask()
        for g in range(SGU_GROUPS):
            wm = jnp.where(mask, w_ref[g], 0.0).astype(BF16)
            cols = slice(g * HEAD, (g + 1) * HEAD)
            for n in range(nblk):
                rows = slice(n * HEAD, (n + 1) * HEAD)
                mixed = jnp.dot(wm, vn[rows, cols], preferred_element_type=F32) + bt_ref[:, g:g + 1]
                y_ref[rows, cols] = (ug[rows, cols] * mixed).astype(BF16)

    full = lambda shape: pl.BlockSpec(shape, lambda i: (0,) * len(shape))
    return pl.pallas_call(
        body, name="sgu_fwd", grid=(t // tm,),
        in_specs=specs + [full(w_sp.shape), full(b_sp_t.shape), full(ln_g.shape), full(ln_b.shape)],
        out_specs=pl.BlockSpec((tm, d_sgu), lambda i: (i, 0)),
        out_shape=jax.ShapeDtypeStruct((t, d_sgu), BF16),
        compiler_params=_cp(("arbitrary",)),
    )(*([proj] * (2 * n_piece)), w_sp, b_sp_t, ln_g, ln_b)


def _mix_fwd(hs, proj, y_sgu, x2, modv, w_o_lru_g, w_o_sgu_g, w_out_g, ln1_g, ln1_b, seq):
    t, d = x2.shape
    w = hs.shape[1]
    d_sgu = y_sgu.shape[1]
    nq, _, ns = w_o_sgu_g.shape
    tm = min(TM_MIX, seq)
    tpb = seq // tm

    def body(hs_ref, gl_ref, ys_ref, ga_ref, gb_ref, x_ref, mod_ref, wl_ref, ws_ref, wo_ref, g1_ref, b1_ref,
             yap_ref, ya_ref, yb_ref, mg_ref, mix_ref, x1_ref):
        yap = (hs_ref[...].astype(F32) * _gelu(gl_ref[...].astype(F32))).astype(BF16)
        yap_ref[...] = yap
        y_a = jnp.dot(yap, wl_ref[...], preferred_element_type=F32)
        ys = ys_ref[...]
        y_b = jnp.concatenate([jnp.dot(ys, ws_ref[q], preferred_element_type=F32) for q in range(nq)], axis=1)
        ya_ref[...] = y_a.astype(BF16)
        yb_ref[...] = y_b.astype(BF16)
        merged = (_sigmoid_t(ga_ref[...].astype(F32)) * y_a + _sigmoid_t(gb_ref[...].astype(F32)) * y_b).astype(BF16)
        mg_ref[...] = merged
        mix = jnp.dot(merged, wo_ref[...], preferred_element_type=F32)
        mix_ref[...] = mix
        xhat, _ = _ln_stats(ALPHA * x_ref[...] + (1.0 + mod_ref[2:3, :]) * mix)
        x1_ref[...] = xhat * g1_ref[...] + b1_ref[...]

    row = lambda width, col: pl.BlockSpec((tm, width), functools.partial(lambda i, k: (i, k), k=col))
    full = lambda shape: pl.BlockSpec(shape, lambda i: (0,) * len(shape))
    return pl.pallas_call(
        body, name="mix_fwd", grid=(t // tm,),
        in_specs=[row(w, 0), row(w, 1), row(d_sgu, 0), row(d, 4), row(d, 5), row(d, 0),
                  pl.BlockSpec((None, 8, d), lambda i: (i // tpb, 0, 0)),
                  full(w_o_lru_g.shape), full(w_o_sgu_g.shape), full(w_out_g.shape), full(ln1_g.shape), full(ln1_b.shape)],
        out_specs=[row(w, 0), row(d, 0), row(d, 0), row(d, 0), row(d, 0), row(d, 0)],
        out_shape=[jax.ShapeDtypeStruct((t, w), BF16), jax.ShapeDtypeStruct((t, d), BF16),
                   jax.ShapeDtypeStruct((t, d), BF16), jax.ShapeDtypeStruct((t, d), BF16),
                   jax.ShapeDtypeStruct((t, d), F32), jax.ShapeDtypeStruct((t, d), F32)],
        compiler_params=_cp(("arbitrary",)),
    )(hs, proj, y_sgu, proj, proj, x2, modv, w_o_lru_g, w_o_sgu_g, w_out_g, ln1_g, ln1_b)


def _mlp_fwd(x1, modv, w_up_g, w_down_g, ln2_g, ln2_b, target, nb, seq):
    t, d = x1.shape
    nq, _, ns = w_up_g.shape
    tm = min(TM_MLP, seq)
    ts = min(TS_MLP, tm)
    tpb = seq // tm

    def body(x1_ref, mod_ref, wu_hbm, wd_hbm, g2_ref, b2_ref, tg_ref,
             rl_ref, act_ref, h2_ref, dz2_ref, df_ref, st_ref, pb_ref, wu_s, wd_s, acc, sems):
        i = pl.program_id(0)

        @pl.when(i == 0)
        def _():
            _load_weights((wu_hbm, wd_hbm), (wu_s, wd_s), sems)
            st_ref[...] = jnp.zeros_like(st_ref)

        @pl.when(i % tpb == 0)
        def _():
            pb_ref[...] = jnp.zeros_like(pb_ref)

        for sub in range(tm // ts):
            rows = slice(sub * ts, (sub + 1) * ts)
            x1v = x1_ref[rows, :]
            h2 = (x1v * (1.0 + mod_ref[4:5, :]) + mod_ref[3:4, :]).astype(BF16)
            h2_ref[rows, :] = h2
            for k in range(nq):
                cols = slice(k * ns, (k + 1) * ns)
                r = jnp.maximum(jnp.dot(h2, wu_s[k], preferred_element_type=F32), 0.0)
                act = (r * r).astype(BF16)
                rl_ref[rows, cols] = r.astype(BF16)
                act_ref[rows, cols] = act
                part = jnp.dot(act, wd_s[cols, :], preferred_element_type=F32)
                if k == 0:
                    acc[sub] = part
                else:
                    acc[sub] += part
            f = acc[sub]
            xhat, rstd = _ln_stats(ALPHA * x1v + (1.0 + mod_ref[5:6, :]) * f)
            y = xhat * g2_ref[...] + b2_ref[...]
            err = y - tg_ref[rows, :]
            dy = err * (1.0 / d)
            dz2 = _ln_bwd(dy * g2_ref[...], xhat, rstd)
            dz2_ref[rows, :] = dz2
            df_ref[rows, :] = ((1.0 + mod_ref[5:6, :]) * dz2).astype(BF16)
            st_ref[0:1, :] += _colsum(dy * xhat)
            st_ref[1:2, :] += _colsum(dy)
            st_ref[2:3, :] += (0.5 / d) * jnp.sum(_colsum(err * err), axis=1, keepdims=True)
            pb_ref[0:1, :] += _colsum(dz2 * f)

    tok = lambda i: (i, 0)
    return pl.pallas_call(
        body, name="mlp_fwd", grid=(t // tm,),
        in_specs=[pl.BlockSpec((tm, d), tok), pl.BlockSpec((None, 8, d), lambda i: (i // tpb, 0, 0)), _ANY, _ANY,
                  pl.BlockSpec((1, d), lambda i: (0, 0)), pl.BlockSpec((1, d), lambda i: (0, 0)),
                  pl.BlockSpec((tm, d), tok)],
        out_specs=[pl.BlockSpec((tm, nq * ns), tok), pl.BlockSpec((tm, nq * ns), tok),
                   pl.BlockSpec((tm, d), tok), pl.BlockSpec((tm, d), tok), pl.BlockSpec((tm, d), tok),
                   pl.BlockSpec((8, d), lambda i: (0, 0)), pl.BlockSpec((None, 8, d), lambda i: (i // tpb, 0, 0))],
        out_shape=[jax.ShapeDtypeStruct((t, nq * ns), BF16), jax.ShapeDtypeStruct((t, nq * ns), BF16),
                   jax.ShapeDtypeStruct((t, d), BF16),
                   jax.ShapeDtypeStruct((t, d), F32), jax.ShapeDtypeStruct((t, d), BF16),
                   jax.ShapeDtypeStruct((8, d), F32), jax.ShapeDtypeStruct((nb, 8, d), F32)],
        scratch_shapes=[pltpu.VMEM(w_up_g.shape, BF16), pltpu.VMEM(w_down_g.shape, BF16),
                        pltpu.VMEM((tm // ts, ts, d), F32), pltpu.SemaphoreType.DMA((2,))],
        compiler_params=_cp(("arbitrary",)),
    )(x1, modv, w_up_g, w_down_g, ln2_g, ln2_b, target)


def _mlp_bwd(df, up, w_down_g, w_up_g, dz2, x2, mix, modv, ln1_g, ln1_b, nb, seq):
    t, d = x2.shape
    nq, _, ns = w_up_g.shape
    tm = min(TM_MLP, seq)
    ts = min(TS_MLP, tm)
    tpb = seq // tm

    def body(df_ref, rl_ref, wd_hbm, wu_hbm, dz2_ref, x_ref, mix_ref, mod_ref, g1_ref, b1_ref,
             dup_ref, dz1_ref, dmix_ref, st_ref, pb_ref, wd_s, wu_s, acc, sems):
        i = pl.program_id(0)

        @pl.when(i == 0)
        def _():
            _load_weights((wd_hbm, wu_hbm), (wd_s, wu_s), sems)
            st_ref[...] = jnp.zeros_like(st_ref)

        @pl.when(i % tpb == 0)
        def _():
            pb_ref[...] = jnp.zeros_like(pb_ref)

        for sub in range(tm // ts):
            rows = slice(sub * ts, (sub + 1) * ts)
            dfv = df_ref[rows, :]
            for k in range(nq):
                cols = slice(k * ns, (k + 1) * ns)
                dup = (_mm_nt(dfv, wd_s[cols, :]) * (2.0 * rl_ref[rows, cols].astype(F32))).astype(BF16)
                dup_ref[rows, cols] = dup
                part = _mm_nt(dup, wu_s[k])
                if k == 0:
                    acc[sub] = part
                else:
                    acc[sub] += part
            dh2 = acc[sub]
            mix = mix_ref[rows, :]
            xhat, rstd = _ln_stats(ALPHA * x_ref[rows, :] + (1.0 + mod_ref[2:3, :]) * mix)
            x1 = xhat * g1_ref[...] + b1_ref[...]
            dx1 = ALPHA * dz2_ref[rows, :] + dh2 * (1.0 + mod_ref[4:5, :])
            dz1 = _ln_bwd(dx1 * g1_ref[...], xhat, rstd)
            dz1_ref[rows, :] = dz1
            dmix_ref[rows, :] = ((1.0 + mod_ref[2:3, :]) * dz1).astype(BF16)
            st_ref[0:1, :] += _colsum(dx1 * xhat)
            st_ref[1:2, :] += _colsum(dx1)
            pb_ref[0:1, :] += _colsum(dh2 * x1)
            pb_ref[1:2, :] += _colsum(dh2)
            pb_ref[2:3, :] += _colsum(dz1 * mix)

    tok = lambda i: (i, 0)
    return pl.pallas_call(
        body, name="mlp_bwd", grid=(t // tm,),
        in_specs=[pl.BlockSpec((tm, d), tok), pl.BlockSpec((tm, nq * ns), tok), _ANY, _ANY,
                  pl.BlockSpec((tm, d), tok), pl.BlockSpec((tm, d), tok), pl.BlockSpec((tm, d), tok),
                  pl.BlockSpec((None, 8, d), lambda i: (i // tpb, 0, 0)),
                  pl.BlockSpec((1, d), lambda i: (0, 0)), pl.BlockSpec((1, d), lambda i: (0, 0))],
        out_specs=[pl.BlockSpec((tm, nq * ns), tok),
                   pl.BlockSpec((tm, d), tok), pl.BlockSpec((tm, d), tok),
                   pl.BlockSpec((8, d), lambda i: (0, 0)), pl.BlockSpec((None, 8, d), lambda i: (i // tpb, 0, 0))],
        out_shape=[jax.ShapeDtypeStruct((t, nq * ns), BF16),
                   jax.ShapeDtypeStruct((t, d), F32), jax.ShapeDtypeStruct((t, d), BF16),
                   jax.ShapeDtypeStruct((8, d), F32), jax.ShapeDtypeStruct((nb, 8, d), F32)],
        scratch_shapes=[pltpu.VMEM(w_down_g.shape, BF16), pltpu.VMEM(w_up_g.shape, BF16),
                        pltpu.VMEM((tm // ts, ts, d), F32), pltpu.SemaphoreType.DMA((2,))],
        compiler_params=_cp(("arbitrary",), VMEM_LIMIT_MAX),
    )(df, up, w_down_g, w_up_g, dz2, x2, mix, modv, ln1_g, ln1_b)


def _mix_bwd(dmix, proj, y_a, y_b, hs, w_out_g, w_o_lru_g, w_o_sgu_g, seq, after=()):
    t, d = dmix.shape
    w = hs.shape[1]
    nq, d_sgu, ns = w_o_sgu_g.shape
    tm = min(TM_MIX, seq)

    def body(dmix_ref, ga_ref, gb_ref, ya_ref, yb_ref, gl_ref, hs_ref, wo_ref, wl_ref, ws_ref,
             dya_ref, dyb_ref, dg_ref, dyl_ref, dys_ref):
        dmerged = _mm_nt(dmix_ref[...], wo_ref[...])
        sa, sb = _sigmoid_t(ga_ref[...].astype(F32)), _sigmoid_t(gb_ref[...].astype(F32))
        dy_a = (dmerged * sa).astype(BF16)
        dy_b = (dmerged * sb).astype(BF16)
        dya_ref[...] = dy_a
        dyb_ref[...] = dy_b
        dg_ref[:, 4 * d:5 * d] = (dmerged * ya_ref[...].astype(F32) * (sa * (1.0 - sa))).astype(BF16)
        dg_ref[:, 5 * d:6 * d] = (dmerged * yb_ref[...].astype(F32) * (sb * (1.0 - sb))).astype(BF16)
        dyap = _mm_nt(dy_a, wl_ref[...])
        gel, dgel = _gelu_and_grad(gl_ref[...].astype(F32))
        dyl_ref[...] = (dyap * gel).astype(BF16)
        dg_ref[:, w:2 * w] = (dyap * hs_ref[...].astype(F32) * dgel).astype(BF16)
        dys = _mm_nt(dy_b[:, 0:ns], ws_ref[0])
        for q in range(1, nq):
            dys = dys + _mm_nt(dy_b[:, q * ns:(q + 1) * ns], ws_ref[q])
        dys_ref[...] = dys

    row = lambda width, col: pl.BlockSpec((tm, width), functools.partial(lambda i, k: (i, k), k=col))
    full = lambda shape: pl.BlockSpec(shape, lambda i: (0,) * len(shape))
    return pl.pallas_call(
        _ordered(body, 10, after), name="mix_bwd", grid=(t // tm,),
        in_specs=[row(d, 0), row(d, 4), row(d, 5), row(d, 0), row(d, 0), row(w, 1), row(w, 0),
                  full(w_out_g.shape), full(w_o_lru_g.shape), full(w_o_sgu_g.shape)] + [_ANY] * len(after),
        out_specs=[row(d, 0), row(d, 0), row(6 * d, 0), row(w, 0), row(d_sgu, 0)],
        out_shape=[jax.ShapeDtypeStruct((t, d), BF16), jax.ShapeDtypeStruct((t, d), BF16),
                   jax.ShapeDtypeStruct((t, 6 * d), BF16), jax.ShapeDtypeStruct((t, w), BF16),
                   jax.ShapeDtypeStruct((t, d_sgu), F32)],
        compiler_params=_cp(("arbitrary",)),
    )(dmix, proj, proj, y_a, y_b, proj, hs, w_out_g, w_o_lru_g, w_o_sgu_g, *after)


def _sgu_bwd(proj, dys, w_sp, b_sp_t, ln_g, ln_b, after=()):
    t = proj.shape[0]
    d_sgu = SGU_GROUPS * HEAD
    tm = min(TM_SGU, t)
    nblk = tm // HEAD
    specs, n_piece = _sgu_specs(tm, d_sgu)

    def body(*refs):
        u = jnp.concatenate([r[...] for r in refs[:n_piece]], axis=1).astype(F32)
        v = jnp.concatenate([r[...] for r in refs[n_piece:2 * n_piece]], axis=1).astype(F32)
        dys_ref, w_ref, bt_ref, g_ref, b_ref, du_ref, dv_ref, dw_ref, st_ref, dbt_ref, dvn_s = refs[2 * n_piece:]

        @pl.when(pl.program_id(0) == 0)
        def _():
            dw_ref[...] = jnp.zeros_like(dw_ref)
            st_ref[...] = jnp.zeros_like(st_ref)
            dbt_ref[...] = jnp.zeros_like(dbt_ref)

        ug, dug_du = _gelu_and_grad(u)
        vg, dvg_dv = _gelu_and_grad(v)
        xhat, rstd = _ln_stats(vg)
        vn = (xhat * g_ref[...] + b_ref[...]).astype(BF16)
        dys_v = dys_ref[...]
        mask = _sgu_mask()
        for g in range(SGU_GROUPS):
            wm = jnp.where(mask, w_ref[g], 0.0).astype(BF16)
            cols = slice(g * HEAD, (g + 1) * HEAD)
            dw_g = jnp.zeros((HEAD, HEAD), F32)
            db_g = jnp.zeros((HEAD, 1), F32)
            for n in range(nblk):
                rows = slice(n * HEAD, (n + 1) * HEAD)
                vn_blk = vn[rows, cols]
                mixed = jnp.dot(wm, vn_blk, preferred_element_type=F32) + bt_ref[:, g:g + 1]
                dy_blk = dys_v[rows, cols]
                du_ref[rows, cols] = (dy_blk * mixed * dug_du[rows, cols]).astype(BF16)
                dmx = dy_blk * ug[rows, cols]
                dvn_s[rows, cols] = _mm_tn(wm, dmx)
                dw_g = dw_g + _mm_nt(dmx, vn_blk)
                db_g = db_g + jnp.sum(dmx, axis=1, keepdims=True)
            dw_ref[g] += jnp.where(mask, dw_g, 0.0)
            dbt_ref[:, g:g + 1] += db_g
        dvn = dvn_s[...]
        st_ref[0:1, :] += _colsum(dvn * xhat)
        st_ref[1:2, :] += _colsum(dvn)
        dv_ref[...] = (_ln_bwd(dvn * g_ref[...], xhat, rstd) * dvg_dv).astype(BF16)

    full = lambda shape: pl.BlockSpec(shape, lambda i: (0,) * len(shape))
    tok = pl.BlockSpec((tm, d_sgu), lambda i: (i, 0))
    return pl.pallas_call(
        _ordered(body, 2 * n_piece + 5, after), name="sgu_bwd", grid=(t // tm,),
        in_specs=specs + [tok, full(w_sp.shape), full(b_sp_t.shape), full(ln_g.shape), full(ln_b.shape)]
        + [_ANY] * len(after),
        out_specs=[tok, tok, full(w_sp.shape), full((8, d_sgu)), full((HEAD, HEAD))],
        out_shape=[jax.ShapeDtypeStruct((t, d_sgu), BF16), jax.ShapeDtypeStruct((t, d_sgu), BF16),
                   jax.ShapeDtypeStruct(w_sp.shape, F32), jax.ShapeDtypeStruct((8, d_sgu), F32),
                   jax.ShapeDtypeStruct((HEAD, HEAD), F32)],
        scratch_shapes=[pltpu.VMEM((tm, d_sgu), F32)],
        compiler_params=_cp(("arbitrary",)),
    )(*([proj] * (2 * n_piece)), dys, w_sp, b_sp_t, ln_g, ln_b, *after)


def _lru_bwd(proj, hs, e, dyl, lru_w, nb, seq, dproj, after=()):
    t = proj.shape[0]
    w = LRU_HEADS * HEAD
    w_conv, b_conv, w_a, b_a, w_x, b_x, lam = lru_w

    def body(x_ref, hs_ref, e_ref, dy_ref, wc_ref, bc_ref, wa_ref, ba_ref, wx_ref, bx_ref, lam_ref,
             dxl_ref, dwa_ref, dwx_ref, st_ref):
        @pl.when(pl.program_id(1) == 0)
        def _():
            dwa_ref[...] = jnp.zeros_like(dwa_ref)
            dwx_ref[...] = jnp.zeros_like(dwx_ref)
            st_ref[...] = jnp.zeros_like(st_ref)

        xl = x_ref[...].astype(F32)
        xc, r, gi, big_l, a, m2 = _lru_gates(xl, wc_ref, bc_ref, wa_ref, ba_ref, wx_ref, bx_ref, lam_ref)
        inv_mult = lax.rsqrt(m2)
        mult = m2 * inv_mult
        dh = dy_ref[...].astype(F32) + _shift_up(e_ref[...], 1)
        da = dh * _shift_down(hs_ref[...].astype(F32), 1)
        dmult = dh * (gi * xc)
        d_i = dh * (mult * xc)
        dxc = dh * (mult * gi)
        dla = a * (da - dmult * (a * inv_mult))
        dr = dla * big_l
        d_big_l = _colsum(dla * r)
        dra = dr * (r * (1.0 - r))
        dia = d_i * (gi * (1.0 - gi))
        dwa_ref[...] += _mm_tn(xc, dra)
        dwx_ref[...] += _mm_tn(xc, dia)
        dxc = dxc + _mm_nt(dra, wa_ref[...]) + _mm_nt(dia, wx_ref[...])
        dxl = wc_ref[CONV_WIDTH - 1:CONV_WIDTH, :] * dxc
        st_ref[4 + CONV_WIDTH - 1:4 + CONV_WIDTH, :] += _colsum(dxc * xl)
        for k in range(CONV_WIDTH - 1):
            ahead = _shift_up(dxc, CONV_WIDTH - 1 - k)
            dxl = dxl + wc_ref[k:k + 1, :] * ahead
            st_ref[4 + k:5 + k, :] += _colsum(ahead * xl)
        dxl_ref[...] = dxl.astype(BF16)
        st_ref[0:1, :] += _colsum(dra)
        st_ref[1:2, :] += _colsum(dia)
        st_ref[2:3, :] += d_big_l * (LRU_C * _sigmoid(-lam_ref[...]))
        st_ref[3:4, :] += _colsum(dxc)

    col = lambda hd, b: (0, hd)
    head = lambda hd, b: (hd, 0, 0)
    tok = lambda hd, b: (b, hd)
    seq_blk = pl.BlockSpec((seq, HEAD), tok)
    return pl.pallas_call(
        _ordered(body, 11, (dproj,) + tuple(after)), name="lru_bwd", grid=(LRU_HEADS, nb),
        in_specs=[seq_blk, seq_blk, seq_blk, seq_blk,
                  pl.BlockSpec((CONV_WIDTH, HEAD), col), pl.BlockSpec((1, HEAD), col),
                  pl.BlockSpec((None, HEAD, HEAD), head), pl.BlockSpec((1, HEAD), col),
                  pl.BlockSpec((None, HEAD, HEAD), head), pl.BlockSpec((1, HEAD), col),
                  pl.BlockSpec((1, HEAD), col)] + [_ANY] * (1 + len(after)),
        out_specs=[seq_blk, pl.BlockSpec((None, HEAD, HEAD), head), pl.BlockSpec((None, HEAD, HEAD), head),
                   pl.BlockSpec((8, HEAD), col)],
        out_shape=[jax.ShapeDtypeStruct(dproj.shape, BF16), jax.ShapeDtypeStruct((LRU_HEADS, HEAD, HEAD), F32),
                   jax.ShapeDtypeStruct((LRU_HEADS, HEAD, HEAD), F32), jax.ShapeDtypeStruct((8, w), F32)],
        input_output_aliases={11: 0},
        compiler_params=_cp(("arbitrary", "arbitrary")),
    )(proj, hs, e, dyl, w_conv, b_conv, w_a, b_a, w_x, b_x, lam, dproj, *after)


def _weight_grad(a, g, col_shards, name, after=()):
    t, k = a.shape
    n = g.shape[1]
    tt = min(TT_DW, t)
    tk = k if k <= 1536 else 1024
    ns = n // N_CHIPS if col_shards else n
    narrow = col_shards and ns < 512
    tn = n if narrow else min(ns, 768 if ns % 768 == 0 else 1024)
    while ns % tn and not narrow:
        tn //= 2
    per = max(ns // tn, 1)

    def body(a_ref, g_ref, o_ref):
        @pl.when(pl.program_id(2) == 0)
        def _():
            o_ref[...] = jnp.zeros_like(o_ref)

        res = _mm_tn(a_ref[...], g_ref[...])
        if narrow:
            for q in range(N_CHIPS):
                o_ref[q] += res[:, q * ns:(q + 1) * ns]
        else:
            o_ref[...] += res

    if narrow:
        out_spec = pl.BlockSpec((N_CHIPS, tk, ns), lambda i, j, s: (0, i, 0))
        out_shape = jax.ShapeDtypeStruct((N_CHIPS, k, ns), F32)
    elif col_shards:
        out_spec = pl.BlockSpec((None, tk, tn), lambda i, j, s: (j // per, i, j % per))
        out_shape = jax.ShapeDtypeStruct((N_CHIPS, k, ns), F32)
    else:
        out_spec = pl.BlockSpec((tk, tn), lambda i, j, s: (i, j))
        out_shape = jax.ShapeDtypeStruct((k, n), F32)
    return pl.pallas_call(
        _ordered(body, 2, after), name=name, grid=(k // tk, n // tn, t // tt),
        in_specs=[pl.BlockSpec((tt, tk), lambda i, j, s: (s, i)), pl.BlockSpec((tt, tn), lambda i, j, s: (s, j))]
        + [_ANY] * len(after),
        out_specs=out_spec, out_shape=out_shape,
        compiler_params=_cp(("arbitrary", "arbitrary", "arbitrary")),
    )(a, g, *after)


def _input_grad(dproj, ws, slots, dz1, x2, modv, nb, seq, after=()):
    t, d = x2.shape
    nq = len(ws)
    ns = ws[0].shape[1]
    tm = min(TM_DH, seq)
    ts = min(TS_MLP, tm)
    tpb = seq // tm

    def body(slot_ref, dp_ref, *refs):
        w_hbm = refs[:nq]
        dz1_ref, x_ref, mod_ref, gx_ref, db_ref, pb_ref, w_s, acc, sems = refs[nq:]
        i = pl.program_id(0)

        @pl.when(i == 0)
        def _():
            _load_weights(w_hbm, [w_s.at[slot_ref[k]] for k in range(nq)], sems)
            db_ref[...] = jnp.zeros_like(db_ref)

        @pl.when(i % tpb == 0)
        def _():
            pb_ref[...] = jnp.zeros_like(pb_ref)

        for sub in range(tm // ts):
            rows = slice(sub * ts, (sub + 1) * ts)
            for q in range(nq):
                dp = dp_ref[rows, q * ns:(q + 1) * ns]
                part = _mm_nt(dp, w_s[q])
                if q == 0:
                    acc[sub] = part
                else:
                    acc[sub] += part
                db_ref[q, 0:1, :] += _colsum(dp.astype(F32))
            dh = acc[sub]
            gx_ref[rows, :] = ALPHA * dz1_ref[rows, :] + dh * (1.0 + mod_ref[1:2, :])
            pb_ref[0:1, :] += _colsum(dh * x_ref[rows, :])
            pb_ref[1:2, :] += _colsum(dh)

    tok = lambda i, s: (i, 0)
    in_specs = [pl.BlockSpec((tm, nq * ns), tok)] + [_ANY] * nq
    in_specs += [pl.BlockSpec((tm, d), tok), pl.BlockSpec((tm, d), tok),
                 pl.BlockSpec((None, 8, d), lambda i, s: (i // tpb, 0, 0))] + [_ANY] * len(after)
    return pl.pallas_call(
        _ordered(body, 5 + nq, after), name="input_grad",
        grid_spec=pltpu.PrefetchScalarGridSpec(
            num_scalar_prefetch=1, grid=(t // tm,), in_specs=in_specs,
            out_specs=[pl.BlockSpec((tm, d), tok), pl.BlockSpec((nq, 8, ns), lambda i, s: (0, 0, 0)),
                       pl.BlockSpec((None, 8, d), lambda i, s: (i // tpb, 0, 0))],
            scratch_shapes=[pltpu.VMEM((nq, d, ns), BF16), pltpu.VMEM((tm // ts, ts, d), F32),
                            pltpu.SemaphoreType.DMA((nq,))]),
        out_shape=[jax.ShapeDtypeStruct((t, d), F32), jax.ShapeDtypeStruct((nq, 8, ns), F32),
                   jax.ShapeDtypeStruct((nb, 8, d), F32)],
        compiler_params=_cp(("arbitrary",)),
    )(slots, dproj, *ws, dz1, x2, modv, *after)


def _rows128(v):
    flat = v.reshape(-1, HEAD)
    pad = (-flat.shape[0]) % 8
    return jnp.pad(flat, ((0, pad), (0, 0))) if pad else flat


def kernel(x, c, w_ada, b_ada, w_in, b_in, w_conv, b_conv, w_rg_a, b_rg_a, w_rg_x, b_rg_x, lru_lambda, w_sp, b_sp, ln_v_g, ln_v_b, w_o_lru, w_o_sgu, w_out, ln1_g, ln1_b, w_up, w_down, ln2_g, ln2_b, loss_target, m_w_ada, m_b_ada, m_w_in, m_b_in, m_w_conv, m_b_conv, m_w_rg_a, m_b_rg_a, m_w_rg_x, m_b_rg_x, m_lru_lambda, m_w_sp, m_b_sp, m_ln_v_g, m_ln_v_b, m_w_o_lru, m_w_o_sgu, m_w_out, m_ln1_g, m_ln1_b, m_w_up, m_w_down, m_ln2_g, m_ln2_b, v_w_ada, v_b_ada, v_w_in, v_b_in, v_w_conv, v_b_conv, v_w_rg_a, v_b_rg_a, v_w_rg_x, v_b_rg_x, v_lru_lambda, v_w_sp, v_b_sp, v_ln_v_g, v_ln_v_b, v_w_o_lru, v_w_o_sgu, v_w_out, v_ln1_g, v_ln1_b, v_w_up, v_w_down, v_ln2_g, v_ln2_b):
    given = dict(locals())
    nb, seq, d = x.shape
    t = nb * seq
    w_lru = LRU_HEADS * HEAD
    d_sgu = SGU_GROUPS * HEAD
    xi, yi, ci = lax.axis_index("x"), lax.axis_index("y"), lax.axis_index("c")
    chip = 2 * xi + yi
    dev = 2 * chip + ci
    cidx = jnp.reshape(ci, (1,)).astype(jnp.int32)

    x2 = x.reshape(t, d)
    target = loss_target.reshape(t, d)

    big = ["w_in", "w_o_lru", "w_o_sgu", "w_out", "w_up", "w_down"]
    shards_a = [w_in[0].astype(BF16)]
    shards_b = [given[n][0].astype(BF16) for n in big[1:]]
    pidx = jnp.reshape(chip, (1,)).astype(jnp.int32)

    c_rows = _rows128(c)
    wconv_rows = _rows128(w_conv[0])
    slab0 = _all_gather_small(jnp.concatenate([c_rows, wconv_rows], axis=0), "gather_c_wconv")
    slab0 = slab0.reshape(N_DEV, -1, HEAD)
    c_all = slab0[:, :c_rows.shape[0]].reshape(N_DEV * nb, d)
    n_wc = CONV_WIDTH * (w_lru // N_CHIPS) // HEAD
    wc = slab0[0::2, c_rows.shape[0]:c_rows.shape[0] + n_wc].reshape(N_CHIPS, CONV_WIDTH, w_lru // N_CHIPS)
    w_conv_full = jnp.transpose(wc, (1, 0, 2)).reshape(CONV_WIDTH, w_lru)

    n_ada = w_ada.shape[2]
    b_ada_cols = lax.dynamic_slice(b_ada, (0, chip * n_ada), (1, n_ada))
    mod_cols = _ada_fwd(c_all, w_ada[0], b_ada_cols)
    half = (N_DEV * nb) // 2
    mod_half = lax.dynamic_slice(mod_cols, (ci * half, 0), (half, n_ada))
    mod_g = _all_gather_small(mod_half, "gather_mod").reshape(N_CHIPS, 2, half, n_ada)
    mod_all = jnp.transpose(mod_g, (1, 2, 0, 3)).reshape(N_DEV * nb, N_CHIPS * n_ada)
    mod_loc = lax.dynamic_slice(mod_all, (dev * nb, 0), (nb, N_CHIPS * n_ada)).reshape(nb, 6, d)
    modv = jnp.pad(mod_loc, ((0, 0), (0, 2), (0, 0)))

    lru_w = (w_conv_full, b_conv, w_rg_a[0], b_rg_a, w_rg_x[0], b_rg_x, lru_lambda)
    b_sp_t = jnp.transpose(b_sp[0])

    land = lambda s: jax.ShapeDtypeStruct((N_CHIPS,) + s.shape, s.dtype)
    sds = lambda s: jax.ShapeDtypeStruct(s.shape, s.dtype)
    started_a = _split_start(shards_a, [sds(shards_a[0])] * 2, _peer_gather_copies((0, 1)), 2, "gather_w_in_near_start",
                             after=(modv,))
    shards_b, shards_c = shards_b[:3], shards_b[3:]

    ids = lambda *v: jnp.stack(v).astype(jnp.int32)
    modv_t = modv + started_a[-1][0:1, 0:1]
    proj, h = _proj_fwd(x2, modv_t, [started_a[2]], ids(chip), b_in, seq, "proj_fwd_own")
    own_a, lands_a = _split_wait(started_a, 1, _peer_gather_copies((0, 1)), "gather_w_in_near_wait",
                                 after=(proj, *shards_b, *shards_c))
    started_f = _split_start(own_a, [sds(own_a[0])], _far_gather_copies, 1, "gather_w_in_far_start", after=(lands_a[0],))
    started_b = _split_start(shards_b, [land(s) for s in shards_b], _gather_copies, 3 * len(shards_b),
                             "gather_w_mix_start", after=(started_f[-1],))
    started_c = _split_start(shards_c, [land(s) for s in shards_c], _gather_copies, 3 * len(shards_c),
                             "gather_w_mlp_start", after=(started_b[-1],))
    modv_t = modv + started_c[-1][0:1, 0:1]
    (proj,) = _proj_fwd(x2, modv_t, lands_a, ids(chip ^ 1, chip ^ 2), b_in, seq, "proj_fwd_near", proj_in=proj)
    own_a, land_f = _split_wait(started_f, 1, _far_gather_copies, "gather_w_in_far_wait", after=(proj,))
    (proj,) = _proj_fwd(x2, modv, land_f, ids(chip ^ 3), b_in, seq, "proj_fwd_far", proj_in=proj)
    w_in_shards, w_in_chips = own_a + lands_a + land_f, ids(chip, chip ^ 1, chip ^ 2, chip ^ 3)
    a, inp = _lru_prep(proj, lru_w, nb, seq)
    a3 = a.reshape(nb, seq, w_lru)
    hs = _scan(a3, inp.reshape(nb, seq, w_lru), False, "lru_scan", BF16).reshape(t, w_lru)
    y_sgu = _sgu_fwd(proj, w_sp[0], b_sp_t, ln_v_g, ln_v_b)
    shards_b, lands_b = _split_wait(started_b, len(shards_b), _gather_copies, "gather_w_mix_wait", after=(hs, y_sgu))
    w_o_lru_g, w_o_sgu_g, w_out_g = _fill_own_slot(lands_b, shards_b, pidx, ["own_" + n for n in big[1:4]])
    w_o_lru_g = w_o_lru_g.reshape(w_lru, d)
    w_out_g = w_out_g.reshape(d, d)
    yap, y_a, y_b, merged, mix, x1 = _mix_fwd(hs, proj, y_sgu, x2, modv, w_o_lru_g, w_o_sgu_g, w_out_g, ln1_g, ln1_b, seq)
    shards_c, lands_c = _split_wait(started_c, len(shards_c), _gather_copies, "gather_w_mlp_wait", after=(x1,))
    w_up_g, w_down_g = _fill_own_slot(lands_c, shards_c, pidx, ["own_" + n for n in big[4:]])
    w_down_g = w_down_g.reshape(-1, d)
    up, act, h2, dz2, df, st2, pb2 = _mlp_fwd(x1, modv, w_up_g, w_down_g, ln2_g, ln2_b, target, nb, seq)

    part = {}

    def to_sibling_start(group, tag, after=()):
        g4 = []
        for n in group:
            shard = given[n].shape[1:]
            g4.append(part[n].reshape(N_CHIPS, 2, shard[0] // 2, shard[1]))
        shapes = [jax.ShapeDtypeStruct((N_CHIPS,) + g.shape[2:], F32) for g in g4]
        return _split_start(g4, shapes, _to_sibling_copies, len(g4), "grads_to_sibling_start_" + tag, after)

    def to_chips_start(group, started, tag, after=()):
        g4, recv = _split_wait(started, len(group), _to_sibling_copies, "grads_to_sibling_wait_" + tag, after)
        own4 = [_add_own_half(g4[k], recv[k], cidx, "grad_pair_sum_" + n) for k, n in enumerate(group)]
        shapes = [jax.ShapeDtypeStruct((3,) + o.shape[1:], BF16) for o in own4]
        return _split_start(own4, shapes, _chip_exchange_copies, 3 * len(own4), "grads_chip_exchange_start_" + tag)

    def chips_finish(group, started, tag, after=()):
        own4, slots = _split_wait(started, len(group), _chip_exchange_copies, "grads_chip_exchange_wait_" + tag, after)
        return [_sum_own_and_peers(own4[k], slots[k], pidx, "grad_chip_sum_" + n) for k, n in enumerate(group)]

    dup, dz1, dmix, st1, pb1 = _mlp_bwd(df, up, w_down_g, w_up_g, dz2, x2, mix, modv, ln1_g, ln1_b, nb, seq)
    group1 = ["w_up", "w_down"]
    part["w_up"] = _weight_grad(h2, dup, True, "grad_w_up")
    part["w_down"] = _weight_grad(act, df, False, "grad_w_down")
    sib1 = to_sibling_start(group1, "mlp")
    dy_a, dy_b, dproj, dyl, dys = _mix_bwd(dmix, proj, y_a, y_b, hs, w_out_g, w_o_lru_g, w_o_sgu_g, seq,
                                                after=(sib1[-1],))
    group2 = ["w_o_lru", "w_o_sgu", "w_out"]
    part["w_o_lru"] = _weight_grad(yap, dy_a, False, "grad_w_o_lru")
    part["w_o_sgu"] = _weight_grad(y_sgu, dy_b, True, "grad_w_o_sgu")
    part["w_out"] = _weight_grad(merged, dmix, False, "grad_w_out")
    chips1 = to_chips_start(group1, sib1, "mlp", after=(dys, part["w_o_lru"], part["w_o_sgu"], part["w_out"]))
    sib2 = to_sibling_start(group2, "mix", after=(chips1[-1],))
    du, dv, g_w_sp, st_sgu, g_b_sp_t = _sgu_bwd(proj, dys, w_sp[0], b_sp_t, ln_v_g, ln_v_b, after=(sib2[-1],))
    dyl3 = dyl.reshape(nb, seq, w_lru)
    e = _scan(a3, dyl3, True, "lru_scan_bwd", F32).reshape(t, w_lru)
    chips2 = to_chips_start(group2, sib2, "mix", after=(e, du))
    dproj = lax.dynamic_update_slice(dproj, du, (0, 2 * w_lru))
    dproj = lax.dynamic_update_slice(dproj, dv, (0, 2 * w_lru + d_sgu))
    dproj, g_w_rg_a, g_w_rg_x, st_lru = _lru_bwd(proj, hs, e, dyl, lru_w, nb, seq, dproj, after=(chips2[-1],))

    didx = jnp.reshape(dev, (1,)).astype(jnp.int32)
    early = [
        ("w_conv", st_lru[4:8]), ("b_conv", st_lru[3]), ("w_rg_a", g_w_rg_a), ("b_rg_a", st_lru[0]),
        ("w_rg_x", g_w_rg_x), ("b_rg_x", st_lru[1]), ("lru_lambda", st_lru[2]), ("w_sp", g_w_sp),
        ("b_sp", jnp.transpose(g_b_sp_t[:, :SGU_GROUPS])), ("ln_v_g", st_sgu[0]), ("ln_v_b", st_sgu[1]),
        ("ln1_g", st1[0]), ("ln1_b", st1[1]), ("ln2_g", st2[0]), ("ln2_b", st2[1]),
    ]
    pieces_e = [_rows128(v) for _, v in early]
    slab_e = jnp.concatenate(pieces_e, axis=0)
    slab_e = jnp.pad(slab_e, ((0, (-slab_e.shape[0]) % TR_EW), (0, 0)))
    small_st = _split_start([slab_e], [jax.ShapeDtypeStruct((N_DEV,) + slab_e.shape, F32)], _all_devices_copies, N_DEV - 1,
                            "small_grads_start")

    group3 = ["w_in"]
    part["w_in"] = _weight_grad(h, dproj, True, "grad_w_in", after=(small_st[-1],))
    sib3 = to_sibling_start(group3, "in")
    halves12 = (chips_finish(group1, chips1, "mlp", after=(sib3[-1],))
                + chips_finish(group2, chips2, "mix", after=(sib3[-1],)))
    swap12 = _split_start(halves12, [jax.ShapeDtypeStruct(hv.shape, F32) for hv in halves12], _swap_copies, len(halves12),
                          "grads_swap_start")
    chips3 = to_chips_start(group3, sib3, "in", after=(swap12[-1],))
    grad_x2, g_b_in4, pb0 = _input_grad(dproj, w_in_shards, w_in_chips, dz1, x2, modv, nb, seq, after=(chips3[-1],))
    loss = lax.psum(st2[2, 0] + swap12[-1][0, 0], ("x", "y", "c"))
    grads = {}

    dmod_loc = jnp.stack([pb0[:, 1], pb0[:, 0], pb1[:, 2], pb1[:, 1], pb1[:, 0], pb2[:, 0]], axis=1)
    late = [("dmod", dmod_loc), ("b_in", g_b_in4[:, 0])]
    pieces_l = [_rows128(v) for _, v in late]
    slab_l = jnp.concatenate(pieces_l, axis=0)
    gathered = _all_gather_small(slab_l, "gather_small_grads", after=(swap12[-1],)).reshape(N_DEV, slab_l.shape[0], HEAD)
    rows_dmod = dmod_loc.size // HEAD
    dmod_all = gathered[:, :rows_dmod].reshape(N_DEV * nb, 6 * d)
    grads["b_in"] = _sum_slots(gathered[:, rows_dmod:], "grad_b_in_sum").reshape(1, -1)

    (slab_e,), (lands_e,) = _split_wait(small_st, 1, _all_devices_copies, "small_grads_wait", after=(gathered,))
    summed = _sum_devices(lands_e, slab_e, didx, "small_grad_sum")
    off = 0
    for (n, v), piece in zip(early, pieces_e):
        grads[n] = summed[off:off + v.size // HEAD].reshape(v.shape)
        off += piece.shape[0]

    mine12, theirs12 = _split_wait(swap12, len(halves12), _swap_copies, "grads_swap_wait", after=(summed,))
    (mine3,) = chips_finish(group3, chips3, "in", after=(summed,))
    (theirs3,) = _exchange([mine3], [jax.ShapeDtypeStruct(mine3.shape, F32)], _swap_copies, 1, "grads_swap_w_in")
    mine = dict(zip(group1 + group2 + group3, mine12 + [mine3]))
    theirs = dict(zip(group1 + group2 + group3, theirs12 + [theirs3]))

    dmod_cols = lax.dynamic_slice(dmod_all, (0, chip * n_ada), (N_DEV * nb, n_ada))
    grads["w_ada"], grads["b_ada"] = _ada_bwd(c_all, dmod_all, dmod_cols)
    n_wcs = w_lru // N_CHIPS
    grads["w_conv"] = lax.dynamic_slice(grads["w_conv"], (0, chip * n_wcs), (CONV_WIDTH, n_wcs))

    names = ['w_ada', 'b_ada', 'w_in', 'b_in', 'w_conv', 'b_conv', 'w_rg_a', 'b_rg_a', 'w_rg_x', 'b_rg_x', 'lru_lambda',
             'w_sp', 'b_sp', 'ln_v_g', 'ln_v_b', 'w_o_lru', 'w_o_sgu', 'w_out', 'ln1_g', 'ln1_b', 'w_up', 'w_down',
             'ln2_g', 'ln2_b']
    two_d = lambda v: v.reshape(-1, v.shape[-1])
    done = {}
    small_names = [n for n in names if n not in big and n != "w_ada"]
    small_out = _adamw_many([(two_d(given[n]), two_d(grads[n].reshape(given[n].shape)), two_d(given["m_" + n]),
                              two_d(given["v_" + n])) for n in small_names], "adamw_small")
    for n, res in zip(small_names, small_out):
        done[n] = (grads[n],) + tuple(res)
    for n in big + ["w_ada"]:
        w2, m2, v2 = two_d(given[n]), two_d(given["m_" + n]), two_d(given["v_" + n])
        if n in big:
            done[n] = _adamw_halves(w2, mine[n], theirs[n], m2, v2, cidx, "adamw_" + n)
        else:
            done[n] = (grads[n],) + tuple(_adamw(w2, two_d(grads[n]), m2, v2, "adamw_" + n))
    outs = [[done[n][k].reshape(given[n].shape) for n in names] for k in range(4)]
    return (loss, grad_x2.reshape(nb, seq, d), *outs[0], *outs[1], *outs[2], *outs[3])
```

```python
import functools
import math

import jax
import jax.numpy as jnp
from jax import lax
from jax.experimental import pallas as pl
from jax.experimental.pallas import tpu as pltpu

F32 = jnp.float32
BF16 = jnp.bfloat16
MESH = pl.DeviceIdType.MESH

N_CHIPS = 4
N_DEV = 8
LRU_HEADS = 10
HEAD = 128
SGU_GROUPS = 6
SGU_CHUNK = 64
CONV_WIDTH = 4
LRU_C = 8.0
ALPHA = 2.0 ** 0.25
LN_EPS = 1e-5
ADAM_LR, ADAM_B1, ADAM_B2, ADAM_EPS, ADAM_WD, ADAM_STEP = 0.001, 0.9, 0.999, 1e-08, 0.01, 10

VMEM_LIMIT = 56 * 1024 * 1024
VMEM_LIMIT_MAX = 62 * 1024 * 1024
TM_PROJ = 1024
TM_MIX = 256
TM_MLP = 512
TS_MLP = 256
TM_SGU = 512
TM_DH = 512
TT_DW = 2048
TC_SCAN = 256
TR_EW = 256


def _cp(sem=None, limit=None):
    return pltpu.CompilerParams(dimension_semantics=sem, vmem_limit_bytes=limit or VMEM_LIMIT)


def _mm(a, b):
    return jnp.dot(a.astype(BF16), b.astype(BF16), preferred_element_type=F32)


def _mm_nt(a, b):
    return lax.dot_general(a.astype(BF16), b.astype(BF16), (((1,), (1,)), ((), ())), preferred_element_type=F32)


def _mm_tn(a, b):
    return lax.dot_general(a.astype(BF16), b.astype(BF16), (((0,), (0,)), ((), ())), preferred_element_type=F32)


def _sigmoid(x):
    return 1.0 / (1.0 + jnp.exp(-x))


def _sigmoid_t(x):
    return 0.5 * jnp.tanh(0.5 * x) + 0.5


_GELU_K = math.sqrt(2.0 / math.pi)


def _gelu(x):
    t = jnp.tanh(_GELU_K * (x + 0.044715 * (x * x * x)))
    return 0.5 * x * (1.0 + t)


def _gelu_and_grad(x):
    x2 = x * x
    t = jnp.tanh(_GELU_K * (x + 0.044715 * (x2 * x)))
    g = 0.5 * x * (1.0 + t)
    dg = 0.5 * (1.0 + t) + 0.5 * x * (1.0 - t * t) * (_GELU_K * (1.0 + 3.0 * 0.044715 * x2))
    return g, dg


def _ln_stats(z):
    mu = jnp.mean(z, axis=-1, keepdims=True)
    zc = z - mu
    var = jnp.mean(zc * zc, axis=-1, keepdims=True)
    rstd = lax.rsqrt(var + LN_EPS)
    return zc * rstd, rstd


def _ln_bwd(dxh, xhat, rstd):
    m1 = jnp.mean(dxh, axis=-1, keepdims=True)
    m2 = jnp.mean(dxh * xhat, axis=-1, keepdims=True)
    return rstd * (dxh - m1 - xhat * m2)


def _colsum(v):
    return jnp.sum(v, axis=0, keepdims=True)


def _shift_down(v, j):
    if j == 0:
        return v
    rows = lax.broadcasted_iota(jnp.int32, v.shape, 0)
    return jnp.where(rows >= j, pltpu.roll(v, j, 0), 0.0)


def _shift_up(v, j):
    if j == 0:
        return v
    n = v.shape[0]
    rows = lax.broadcasted_iota(jnp.int32, v.shape, 0)
    return jnp.where(rows < n - j, pltpu.roll(v, n - j, 0), 0.0)


def _load_weights(srcs, dsts, sems):
    cps = [pltpu.make_async_copy(s, dd, sems.at[k]) for k, (s, dd) in enumerate(zip(srcs, dsts))]
    for cp in cps:
        cp.start()
    for cp in cps:
        cp.wait()


def _my_pos():
    return lax.axis_index("x"), lax.axis_index("y"), lax.axis_index("c")


def _all_gather_small(v, name, after=()):
    m_per, n = v.shape

    def body(x_ref, out_ref, send_sems, recv_sems, local_sem):
        x, y, c = _my_pos()
        me, sibling = (x, y, c), (x, y, 1 - c)
        chips = [(1 - x, y), (x, 1 - y), (1 - x, 1 - y)]

        def rows(px, py, pc):
            return out_ref.at[pl.ds((4 * px + 2 * py + pc) * m_per, m_per), :]

        def copy(k, block, to, src=None):
            return pltpu.make_async_remote_copy(
                src_ref=rows(*block) if src is None else src, dst_ref=rows(*block),
                send_sem=send_sems.at[k], recv_sem=recv_sems.at[k], device_id=to, device_id_type=MESH)

        mine = pltpu.make_async_copy(x_ref, rows(*me), local_sem)
        mine.start()
        first = [copy(0, me, sibling, src=x_ref)]
        first += [copy(1 + j, me, (*chip, c), src=x_ref) for j, chip in enumerate(chips)]
        for cp in first:
            cp.start()
        passed = [copy(4 + j, (*chip, c), sibling) for j, chip in enumerate(chips)]
        for j, chip in enumerate(chips):
            copy(1 + j, (*chip, c), me).wait_recv()
            passed[j].start()
        copy(0, sibling, me).wait_recv()
        for j, chip in enumerate(chips):
            copy(4 + j, (*chip, 1 - c), me).wait_recv()
        for cp in first + passed:
            cp.wait_send()
        mine.wait()

    return pl.pallas_call(
        _ordered(body, 1, after), name=name,
        out_shape=jax.ShapeDtypeStruct((N_DEV * m_per, n), v.dtype),
        in_specs=[pl.BlockSpec(memory_space=pltpu.VMEM)] + [pl.BlockSpec(memory_space=pl.ANY)] * len(after),
        out_specs=pl.BlockSpec(memory_space=pltpu.VMEM),
        scratch_shapes=[pltpu.SemaphoreType.DMA((7,)), pltpu.SemaphoreType.DMA((7,)), pltpu.SemaphoreType.DMA],
        compiler_params=pltpu.CompilerParams(vmem_limit_bytes=VMEM_LIMIT),
    )(v, *after)


_HBM = pl.BlockSpec(memory_space=pltpu.HBM)
_ANY = pl.BlockSpec(memory_space=pl.ANY)
_SEM = pl.BlockSpec(memory_space=pltpu.SEMAPHORE)
_EFFECT = pltpu.SideEffectType.DATAFLOW_SIDE_EFFECTING


def _ordered(body, n_in, after):
    k = len(after)
    if not k:
        return body
    return lambda *refs: body(*refs[:n_in], *refs[n_in + k:])


def _gather_copies(ins, lands, send_sems, recv_sems):
    x, y, c = _my_pos()
    p = 2 * x + y
    peers = [(x, 1 - y), (1 - x, y), (1 - x, 1 - y)]
    sends, recvs = [], []
    for k in range(len(ins)):
        for j, (qx, qy) in enumerate(peers):
            sems = dict(send_sem=send_sems.at[3 * k + j], recv_sem=recv_sems.at[3 * k + j],
                        device_id=(qx, qy, c), device_id_type=MESH)
            sends.append(pltpu.make_async_remote_copy(src_ref=ins[k], dst_ref=lands[k].at[p], **sems))
            recvs.append(pltpu.make_async_remote_copy(src_ref=ins[k], dst_ref=lands[k].at[2 * qx + qy], **sems))
    return sends, recvs


def _peer_gather_copies(peers):
    def copies(ins, lands, send_sems, recv_sems):
        x, y, c = _my_pos()
        where = [(x, 1 - y), (1 - x, y), (1 - x, 1 - y)]
        cps = [pltpu.make_async_remote_copy(
            src_ref=ins[0], dst_ref=lands[j], send_sem=send_sems.at[j], recv_sem=recv_sems.at[j],
            device_id=(*where[j], c), device_id_type=MESH) for j in peers]
        return cps, cps
    return copies


def _far_gather_copies(ins, lands, send_sems, recv_sems):
    x, y, c = _my_pos()
    cps = [pltpu.make_async_remote_copy(
        src_ref=ins[0], dst_ref=lands[0], send_sem=send_sems.at[0], recv_sem=recv_sems.at[0],
        device_id=(1 - x, 1 - y, c), device_id_type=MESH)]
    return cps, cps


def _to_sibling_copies(ins, lands, send_sems, recv_sems):
    x, y, c = _my_pos()
    cps = [pltpu.make_async_remote_copy(
        src_ref=ins[k].at[:, 1 - c], dst_ref=lands[k], send_sem=send_sems.at[k], recv_sem=recv_sems.at[k],
        device_id=(x, y, 1 - c), device_id_type=MESH) for k in range(len(ins))]
    return cps, cps


def _chip_exchange_copies(ins, lands, send_sems, recv_sems):
    x, y, c = _my_pos()
    peers = [(x, 1 - y), (1 - x, y), (1 - x, 1 - y)]
    cps = []
    for k in range(len(ins)):
        for j, (qx, qy) in enumerate(peers):
            cps.append(pltpu.make_async_remote_copy(
                src_ref=ins[k].at[2 * qx + qy], dst_ref=lands[k].at[j], send_sem=send_sems.at[3 * k + j],
                recv_sem=recv_sems.at[3 * k + j], device_id=(qx, qy, c), device_id_type=MESH))
    return cps, cps


def _all_devices_copies(ins, lands, send_sems, recv_sems):
    x, y, c = _my_pos()
    me = 4 * x + 2 * y + c
    sends, recvs = [], []
    for r in range(1, N_DEV):
        px = 1 - x if r & 4 else x
        py = 1 - y if r & 2 else y
        pc = 1 - c if r & 1 else c
        sems = dict(send_sem=send_sems.at[r - 1], recv_sem=recv_sems.at[r - 1], device_id=(px, py, pc), device_id_type=MESH)
        sends.append(pltpu.make_async_remote_copy(src_ref=ins[0], dst_ref=lands[0].at[me], **sems))
        recvs.append(pltpu.make_async_remote_copy(src_ref=ins[0], dst_ref=lands[0].at[4 * px + 2 * py + pc], **sems))
    return sends, recvs


def _swap_copies(ins, lands, send_sems, recv_sems):
    x, y, c = _my_pos()
    cps = [pltpu.make_async_remote_copy(
        src_ref=ins[k], dst_ref=lands[k], send_sem=send_sems.at[k], recv_sem=recv_sems.at[k],
        device_id=(x, y, 1 - c), device_id_type=MESH) for k in range(len(ins))]
    return cps, cps


def _split_start(ins, land_shapes, copies, n_sems, name, after=()):
    n, nl = len(ins), len(land_shapes)
    first_out = n + nl + len(after)

    def body(*refs):
        in_refs, land_refs = refs[:n], refs[n:n + nl]
        send_sems, recv_sems = refs[first_out:first_out + 2]
        token = refs[-1]
        sends, _ = copies(in_refs, land_refs, send_sems, recv_sems)
        for cp in sends:
            cp.start()
        token[...] = jnp.zeros_like(token)

    lands = [pltpu.with_memory_space_constraint(lax.empty(s.shape, s.dtype), pltpu.HBM) for s in land_shapes]
    ins = [pltpu.with_memory_space_constraint(s, pltpu.HBM) for s in ins]
    return pl.pallas_call(
        body, name=name,
        out_shape=(pltpu.SemaphoreType.DMA((n_sems,)), pltpu.SemaphoreType.DMA((n_sems,)),
                   *[pltpu.HBM(s.shape, s.dtype) for s in ins], *[pltpu.HBM(s.shape, s.dtype) for s in lands],
                   jax.ShapeDtypeStruct((8, HEAD), F32)),
        in_specs=[_HBM] * (n + nl) + [pl.BlockSpec(memory_space=pl.ANY)] * len(after),
        out_specs=(_SEM, _SEM, *([_HBM] * (n + nl)), pl.BlockSpec(memory_space=pltpu.VMEM)),
        input_output_aliases={k: 2 + k for k in range(n + nl)},
        compiler_params=pltpu.CompilerParams(has_side_effects=_EFFECT),
    )(*ins, *lands, *after)


def _split_wait(started, n, copies, name, after=()):
    send_sems, recv_sems = started[0], started[1]
    bufs = started[2:-1]
    nb = len(bufs)

    def body(*refs):
        in_refs, land_refs = refs[:n], refs[n:nb]
        sends, recvs = copies(in_refs, land_refs, refs[nb], refs[nb + 1])
        for cp in sends:
            cp.wait_send()
        for cp in recvs:
            cp.wait_recv()

    outs = pl.pallas_call(
        body, name=name,
        out_shape=tuple(pltpu.HBM(s.shape, s.dtype) for s in bufs),
        in_specs=[_HBM] * nb + [_SEM, _SEM] + [pl.BlockSpec(memory_space=pl.ANY)] * len(after),
        out_specs=tuple([_HBM] * nb),
        input_output_aliases={k: k for k in range(nb)},
        compiler_params=pltpu.CompilerParams(has_side_effects=_EFFECT),
    )(*bufs, send_sems, recv_sems, *after)
    return list(outs[:n]), list(outs[n:])


def _fill_own_slot(gathered, shards, pidx, names):
    outs = []
    for g, s, name in zip(gathered, shards, names):
        r, cdim = s.shape
        tr = _row_tile(r)

        def body(p_ref, s_ref, g_ref, o_ref):
            o_ref[...] = s_ref[...]

        outs.append(pl.pallas_call(
            body, name=name,
            grid_spec=pltpu.PrefetchScalarGridSpec(
                num_scalar_prefetch=1, grid=(r // tr,),
                in_specs=[pl.BlockSpec((tr, cdim), lambda i, p: (i, 0)), pl.BlockSpec(memory_space=pl.ANY)],
                out_specs=pl.BlockSpec((None, tr, cdim), lambda i, p: (p[0], i, 0))),
            out_shape=jax.ShapeDtypeStruct(g.shape, g.dtype),
            input_output_aliases={2: 0},
            compiler_params=_cp(("arbitrary",)),
        )(pidx, s, g))
    return outs


def _sum_own_and_peers(own4, slots, pidx, name):
    _, rh, cdim = own4.shape
    tr = _row_tile(rh)

    def body(p_ref, own_ref, s_ref, o_ref):
        acc = own_ref[...].astype(F32)
        for j in range(3):
            acc = acc + s_ref[j].astype(F32)
        o_ref[...] = acc

    return pl.pallas_call(
        body, name=name,
        grid_spec=pltpu.PrefetchScalarGridSpec(
            num_scalar_prefetch=1, grid=(rh // tr,),
            in_specs=[pl.BlockSpec((None, tr, cdim), lambda i, p: (p[0], i, 0)),
                      pl.BlockSpec((3, tr, cdim), lambda i, p: (0, i, 0))],
            out_specs=pl.BlockSpec((tr, cdim), lambda i, p: (i, 0))),
        out_shape=jax.ShapeDtypeStruct((rh, cdim), F32),
        compiler_params=_cp(("arbitrary",)),
    )(pidx, own4, slots)


def _exchange(ins, land_shapes, copies, n_sems, name):
    n, nl = len(ins), len(land_shapes)

    def body(*refs):
        sends, recvs = copies(refs[:n], refs[n:n + nl], refs[n + nl], refs[n + nl + 1])
        for cp in sends:
            cp.start()
        for cp in sends:
            cp.wait_send()
        for cp in recvs:
            cp.wait_recv()

    any_spec = pl.BlockSpec(memory_space=pl.ANY)
    return pl.pallas_call(
        body, name=name,
        out_shape=[jax.ShapeDtypeStruct(s.shape, s.dtype) for s in land_shapes],
        in_specs=[any_spec] * n, out_specs=[any_spec] * nl,
        scratch_shapes=[pltpu.SemaphoreType.DMA((n_sems,)), pltpu.SemaphoreType.DMA((n_sems,))],
    )(*ins)


def _row_tile(r):
    t = min(TR_EW, r)
    while r % t:
        t //= 2
    return t


def _add_own_half(g4, recv, cidx, name):
    _, _, rh, cdim = g4.shape
    tr = _row_tile(rh)

    def body(c_ref, a_ref, b_ref, o_ref):
        o_ref[...] = (a_ref[...] + b_ref[...]).astype(BF16)

    return pl.pallas_call(
        body, name=name,
        grid_spec=pltpu.PrefetchScalarGridSpec(
            num_scalar_prefetch=1, grid=(N_CHIPS, rh // tr),
            in_specs=[pl.BlockSpec((None, None, tr, cdim), lambda q, i, c: (q, c[0], i, 0)),
                      pl.BlockSpec((None, tr, cdim), lambda q, i, c: (q, i, 0))],
            out_specs=pl.BlockSpec((None, tr, cdim), lambda q, i, c: (q, i, 0))),
        out_shape=jax.ShapeDtypeStruct(recv.shape, BF16),
        compiler_params=_cp(("arbitrary", "arbitrary")),
    )(cidx, g4, recv)


def _sum_slots(v, name):
    n, r, cdim = v.shape
    tr = _row_tile(r)

    def body(v_ref, o_ref):
        acc = v_ref[0].astype(F32)
        for k in range(1, n):
            acc = acc + v_ref[k].astype(F32)
        o_ref[...] = acc

    return pl.pallas_call(
        body, name=name, grid=(r // tr,),
        in_specs=[pl.BlockSpec((n, tr, cdim), lambda i: (0, i, 0))],
        out_specs=pl.BlockSpec((tr, cdim), lambda i: (i, 0)),
        out_shape=jax.ShapeDtypeStruct((r, cdim), F32),
        compiler_params=_cp(("arbitrary",)),
    )(v)


def _sum_devices(lands, own, didx, name):
    _, r, cdim = lands.shape
    tr = _row_tile(r)

    def body(d_ref, l_ref, own_ref, o_ref):
        acc = jnp.where(d_ref[0] == 0, own_ref[...], l_ref[0])
        for dv in range(1, N_DEV):
            acc = acc + jnp.where(d_ref[0] == dv, own_ref[...], l_ref[dv])
        o_ref[...] = acc

    return pl.pallas_call(
        body, name=name,
        grid_spec=pltpu.PrefetchScalarGridSpec(
            num_scalar_prefetch=1, grid=(r // tr,),
            in_specs=[pl.BlockSpec((N_DEV, tr, cdim), lambda i, dd: (0, i, 0)), pl.BlockSpec((tr, cdim), lambda i, dd: (i, 0))],
            out_specs=pl.BlockSpec((tr, cdim), lambda i, dd: (i, 0))),
        out_shape=jax.ShapeDtypeStruct((r, cdim), F32),
        compiler_params=_cp(("arbitrary",)),
    )(didx, lands, own)


def _adamw_math(wv, gg, mv, vv):
    nm = ADAM_B1 * mv + (1.0 - ADAM_B1) * gg
    nv = ADAM_B2 * vv + (1.0 - ADAM_B2) * (gg * gg)
    m_hat = nm / (1.0 - ADAM_B1 ** ADAM_STEP)
    v_hat = nv / (1.0 - ADAM_B2 ** ADAM_STEP)
    return -ADAM_LR * (m_hat / (jnp.sqrt(v_hat) + ADAM_EPS) + ADAM_WD * wv), nm, nv


def _adamw_halves(w, mine, theirs, m, v, cidx, name):
    r, cdim = w.shape
    rh = r // 2
    tr = _row_tile(rh)
    nblk = rh // tr

    def body(c_ref, w_ref, a_ref, b_ref, m_ref, v_ref, g_ref, d_ref, nm_ref, nv_ref):
        gg = jnp.where(pl.program_id(0) == c_ref[0], a_ref[...], b_ref[...])
        g_ref[...] = gg
        d_ref[...], nm_ref[...], nv_ref[...] = _adamw_math(w_ref[...], gg, m_ref[...], v_ref[...])

    full = pl.BlockSpec((tr, cdim), lambda hh, i, c: (hh * nblk + i, 0))
    half = pl.BlockSpec((tr, cdim), lambda hh, i, c: (i, 0))
    return pl.pallas_call(
        body, name=name,
        grid_spec=pltpu.PrefetchScalarGridSpec(
            num_scalar_prefetch=1, grid=(2, nblk),
            in_specs=[full, half, half, full, full], out_specs=[full] * 4),
        out_shape=[jax.ShapeDtypeStruct((r, cdim), F32)] * 4,
        compiler_params=_cp(("arbitrary", "arbitrary")),
    )(cidx, w, mine, theirs, m, v)


def _adamw_many(params, name):
    n = len(params)

    def body(*refs):
        ins, outs = refs[:4 * n], refs[4 * n:]
        for k in range(n):
            w_ref, g_ref, m_ref, v_ref = ins[4 * k:4 * k + 4]
            outs[3 * k][...], outs[3 * k + 1][...], outs[3 * k + 2][...] = _adamw_math(
                w_ref[...], g_ref[...], m_ref[...], v_ref[...])

    flat = [a for p in params for a in p]
    res = pl.pallas_call(
        body, name=name,
        out_shape=[jax.ShapeDtypeStruct(p[0].shape, F32) for p in params for _ in range(3)],
        compiler_params=pltpu.CompilerParams(vmem_limit_bytes=VMEM_LIMIT),
    )(*flat)
    return [res[3 * k:3 * k + 3] for k in range(n)]


def _adamw(w, g, m, v, name):
    r, cdim = w.shape
    tr = _row_tile(r) if r % 8 == 0 else r

    def body(w_ref, g_ref, m_ref, v_ref, d_ref, nm_ref, nv_ref):
        d_ref[...], nm_ref[...], nv_ref[...] = _adamw_math(w_ref[...], g_ref[...], m_ref[...], v_ref[...])

    spec = pl.BlockSpec((tr, cdim), lambda i: (i, 0))
    return pl.pallas_call(
        body, name=name, grid=(r // tr,), in_specs=[spec] * 4, out_specs=[spec] * 3,
        out_shape=[jax.ShapeDtypeStruct((r, cdim), F32)] * 3,
        compiler_params=_cp(("arbitrary",)),
    )(w, g, m, v)


def _ada_fwd(c_all, w_ada, b_cols):
    nb, _ = c_all.shape
    n = w_ada.shape[1]

    def body(c_ref, w_ref, b_ref, o_ref):
        cv = c_ref[...]
        o_ref[...] = _mm(cv * _sigmoid(cv), w_ref[...]) + b_ref[...]

    return pl.pallas_call(
        body, name="ada_fwd", out_shape=jax.ShapeDtypeStruct((nb, n), F32),
        compiler_params=pltpu.CompilerParams(vmem_limit_bytes=VMEM_LIMIT),
    )(c_all, w_ada, b_cols)


def _ada_bwd(c_all, dmod_all, dmod_cols):
    d = c_all.shape[1]
    n = dmod_cols.shape[1]

    def body(c_ref, da_ref, dc_ref, gw_ref, gb_ref):
        cv = c_ref[...]
        gw_ref[...] = _mm_tn(cv * _sigmoid(cv), dc_ref[...])
        gb_ref[...] = _colsum(da_ref[...])

    return pl.pallas_call(
        body, name="ada_bwd",
        out_shape=[jax.ShapeDtypeStruct((d, n), F32), jax.ShapeDtypeStruct((1, dmod_all.shape[1]), F32)],
        compiler_params=pltpu.CompilerParams(vmem_limit_bytes=VMEM_LIMIT),
    )(c_all, dmod_all, dmod_cols)


def _proj_fwd(x2, modv, ws, cols, b_in, seq, name, proj_in=None):
    t, d = x2.shape
    n = len(ws)
    ns = ws[0].shape[1]
    tm = min(TM_PROJ, seq)
    tpb = seq // tm
    first = proj_in is None

    def body(c_ref, x_ref, mod_ref, *refs):
        w_refs, b_ref = refs[:n], refs[n]
        outs = refs[n + 1 if first else n + 2:]
        proj_ref, h_s = outs[0], outs[-1]
        s = pl.program_id(1)

        @pl.when(s == 0)
        def _():
            h = (x_ref[...] * (1.0 + mod_ref[1:2, :]) + mod_ref[0:1, :]).astype(BF16)
            h_s[...] = h
            if first:
                outs[1][...] = h

        for k in range(n):
            @pl.when(s == k)
            def _():
                proj_ref[...] = (jnp.dot(h_s[...], w_refs[k][...], preferred_element_type=F32) + b_ref[...]).astype(BF16)

    in_specs = [pl.BlockSpec((tm, d), lambda i, s, c: (i, 0)),
                pl.BlockSpec((None, 8, d), lambda i, s, c: (i // tpb, 0, 0))]
    in_specs += [pl.BlockSpec((d, ns), lambda i, s, c: (0, 0))] * n
    in_specs += [pl.BlockSpec((1, ns), lambda i, s, c: (0, c[s]))]
    out_specs = [pl.BlockSpec((tm, ns), lambda i, s, c: (i, c[s]))]
    out_shape = [jax.ShapeDtypeStruct((t, N_CHIPS * ns), BF16)]
    args = [cols, x2, modv, *ws, b_in]
    aliases = {}
    if first:
        out_specs.append(pl.BlockSpec((tm, d), lambda i, s, c: (i, 0)))
        out_shape.append(jax.ShapeDtypeStruct((t, d), BF16))
    else:
        in_specs.append(_ANY)
        args.append(proj_in)
        aliases = {len(args) - 1: 0}
    return pl.pallas_call(
        body, name=name,
        grid_spec=pltpu.PrefetchScalarGridSpec(
            num_scalar_prefetch=1, grid=(t // tm, n), in_specs=in_specs, out_specs=out_specs,
            scratch_shapes=[pltpu.VMEM((tm, d), BF16)]),
        out_shape=out_shape, input_output_aliases=aliases,
        compiler_params=_cp(("arbitrary", "arbitrary")),
    )(*args)


def _lru_rate(lam_ref):
    nl = -lam_ref[...]
    e = jnp.exp(-jnp.abs(nl))
    u = 1.0 + e
    dlt = u - 1.0
    log1p_e = jnp.where(dlt == 0.0, e, jnp.log(u) * (e / jnp.where(dlt == 0.0, 1.0, dlt)))
    return -LRU_C * (jnp.maximum(nl, 0.0) + log1p_e)


def _lru_gates(xl, wc_ref, bc_ref, wa_ref, ba_ref, wx_ref, bx_ref, lam_ref):
    xc = bc_ref[...] + wc_ref[CONV_WIDTH - 1:CONV_WIDTH, :] * xl
    for k in range(CONV_WIDTH - 1):
        xc = xc + wc_ref[k:k + 1, :] * _shift_down(xl, CONV_WIDTH - 1 - k)
    r = _sigmoid(_mm(xc, wa_ref[...]) + ba_ref[...])
    gi = _sigmoid_t(_mm(xc, wx_ref[...]) + bx_ref[...])
    big_l = _lru_rate(lam_ref)
    la = big_l * r
    a = jnp.exp(la)
    m2 = jnp.tanh(-la) * (a * a + 1.0)
    return xc, r, gi, big_l, a, m2


def _lru_prep(proj, lru_w, nb, seq):
    t = proj.shape[0]
    w = LRU_HEADS * HEAD
    w_conv, b_conv, w_a, b_a, w_x, b_x, lam = lru_w

    def body(x_ref, wc_ref, bc_ref, wa_ref, ba_ref, wx_ref, bx_ref, lam_ref, a_ref, inp_ref, r_ref, gi_ref, xc_ref):
        xc, r, gi, big_l, a, m2 = _lru_gates(x_ref[...].astype(F32), wc_ref, bc_ref, wa_ref, ba_ref, wx_ref, bx_ref, lam_ref)
        a_ref[...] = a
        inp_ref[...] = jnp.sqrt(m2) * (gi * xc)
        r_ref[...] = r.astype(BF16)
        gi_ref[...] = gi.astype(BF16)
        xc_ref[...] = xc.astype(BF16)

    col = lambda b, hd: (0, hd)
    head = lambda b, hd: (hd, 0, 0)
    tok = lambda b, hd: (b, hd)
    return pl.pallas_call(
        body, name="lru_prep", grid=(nb, LRU_HEADS),
        in_specs=[pl.BlockSpec((seq, HEAD), tok),
                  pl.BlockSpec((CONV_WIDTH, HEAD), col), pl.BlockSpec((1, HEAD), col),
                  pl.BlockSpec((None, HEAD, HEAD), head), pl.BlockSpec((1, HEAD), col),
                  pl.BlockSpec((None, HEAD, HEAD), head), pl.BlockSpec((1, HEAD), col),
                  pl.BlockSpec((1, HEAD), col)],
        out_specs=[pl.BlockSpec((seq, HEAD), tok)] * 5,
        out_shape=[jax.ShapeDtypeStruct((t, w), F32)] * 2 + [jax.ShapeDtypeStruct((t, w), BF16)] * 3,
        compiler_params=_cp(("arbitrary", "arbitrary")),
    )(proj, w_conv, b_conv, w_a, b_a, w_x, b_x, lam)


def _scan(a3, b3, reverse, name, out_dtype):
    nb, seq, w = a3.shape
    tc = min(TC_SCAN, seq)
    nchunk = seq // tc
    npair = tc // 16

    def combine(av, bv):
        rows = lax.broadcasted_iota(jnp.int32, av.shape, 0)
        for s in (1, 2, 4):
            if reverse:
                keep = rows < 8 - s
                a_sh, b_sh = pltpu.roll(av, 8 - s, 0), pltpu.roll(bv, 8 - s, 0)
            else:
                keep = rows >= s
                a_sh, b_sh = pltpu.roll(av, s, 0), pltpu.roll(bv, s, 0)
            bv = jnp.where(keep, bv + av * b_sh, bv)
            av = jnp.where(keep, av * a_sh, av)
        return av, bv

    def body(a_ref, b_ref, h_ref, carry):
        @pl.when(pl.program_id(0) == 0)
        def _():
            carry[...] = jnp.zeros_like(carry)

        for b in range(nb):
            def pair(j, hprev):
                jj = npair - 1 - j if reverse else j
                base = pl.multiple_of(jj * 16, 16)
                a16 = a_ref[b, pl.ds(base, 16), :]
                b16 = b_ref[b, pl.ds(base, 16), :].astype(F32)
                outs = [None, None]
                for k in ((1, 0) if reverse else (0, 1)):
                    av, bv = a16[8 * k:8 * k + 8, :], b16[8 * k:8 * k + 8, :]
                    av, bv = combine(av, av * bv if reverse else bv)
                    h = bv + av * hprev
                    outs[k] = h
                    hprev = jnp.broadcast_to(h[0:1, :] if reverse else h[7:8, :], (8, w))
                h_ref[b, pl.ds(base, 16), :] = jnp.concatenate(outs, axis=0).astype(out_dtype)
                return hprev

            carry[b] = lax.fori_loop(0, npair, pair, carry[b])

    imap = (lambda i: (0, nchunk - 1 - i, 0)) if reverse else (lambda i: (0, i, 0))
    spec = pl.BlockSpec((nb, tc, w), imap)
    return pl.pallas_call(
        body, name=name, grid=(nchunk,), in_specs=[spec, spec], out_specs=spec,
        out_shape=jax.ShapeDtypeStruct((nb, seq, w), out_dtype),
        scratch_shapes=[pltpu.VMEM((nb, 8, w), F32)],
        compiler_params=_cp(("arbitrary",)),
    )(a3, b3)


def _sgu_mask():
    ti = lax.broadcasted_iota(jnp.int32, (HEAD, HEAD), 0) // SGU_CHUNK
    si = lax.broadcasted_iota(jnp.int32, (HEAD, HEAD), 1) // SGU_CHUNK
    return si <= ti


def _sgu_specs(tm, d_sgu):
    pw = 256
    first_u = (2 * LRU_HEADS * HEAD) // pw
    n_piece = d_sgu // pw
    specs = [pl.BlockSpec((tm, pw), functools.partial(lambda i, k: (i, k), k=first_u + j)) for j in range(2 * n_piece)]
    return specs, n_piece


def _sgu_fwd(proj, w_sp, b_sp_t, ln_g, ln_b):
    t = proj.shape[0]
    d_sgu = SGU_GROUPS * HEAD
    tm = min(TM_SGU, t)
    nblk = tm // HEAD
    specs, n_piece = _sgu_specs(tm, d_sgu)

    def body(*refs):
        u = jnp.concatenate([r[...] for r in refs[:n_piece]], axis=1).astype(F32)
        v = jnp.concatenate([r[...] for r in refs[n_piece:2 * n_piece]], axis=1).astype(F32)
        w_ref, bt_ref, g_ref, b_ref, y_ref = refs[2 * n_piece:]
        ug = _gelu(u)
        xhat, _ = _ln_stats(_gelu(v))
        vn = (xhat * g_ref[...] + b_ref[...]).astype(BF16)
        mask = _sgu_mask()
        for g in range(SGU_GROUPS):
            wm = jnp.where(mask, w_ref[g], 0.0).astype(BF16)
            cols = slice(g * HEAD, (g + 1) * HEAD)
            for n in range(nblk):
                rows = slice(n * HEAD, (n + 1) * HEAD)
                mixed = jnp.dot(wm, vn[rows, cols], preferred_element_type=F32) + bt_ref[:, g:g + 1]
                y_ref[rows, cols] = (ug[rows, cols] * mixed).astype(BF16)

    full = lambda shape: pl.BlockSpec(shape, lambda i: (0,) * len(shape))
    return pl.pallas_call(
        body, name="sgu_fwd", grid=(t // tm,),
        in_specs=specs + [full(w_sp.shape), full(b_sp_t.shape), full(ln_g.shape), full(ln_b.shape)],
        out_specs=pl.BlockSpec((tm, d_sgu), lambda i: (i, 0)),
        out_shape=jax.ShapeDtypeStruct((t, d_sgu), BF16),
        compiler_params=_cp(("arbitrary",)),
    )(*([proj] * (2 * n_piece)), w_sp, b_sp_t, ln_g, ln_b)


def _mix_fwd(hs, proj, y_sgu, x2, modv, w_o_lru_g, w_o_sgu_g, w_out_g, ln1_g, ln1_b, seq):
    t, d = x2.shape
    w = hs.shape[1]
    d_sgu = y_sgu.shape[1]
    nq, _, ns = w_o_sgu_g.shape
    tm = min(TM_MIX, seq)
    tpb = seq // tm

    def body(hs_ref, gl_ref, ys_ref, ga_ref, gb_ref, x_ref, mod_ref, wl_ref, ws_ref, wo_ref, g1_ref, b1_ref,
             yap_ref, ya_ref, yb_ref, mg_ref, mix_ref, x1_ref):
        yap = (hs_ref[...].astype(F32) * _gelu(gl_ref[...].astype(F32))).astype(BF16)
        yap_ref[...] = yap
        y_a = jnp.dot(yap, wl_ref[...], preferred_element_type=F32)
        ys = ys_ref[...]
        y_b = jnp.concatenate([jnp.dot(ys, ws_ref[q], preferred_element_type=F32) for q in range(nq)], axis=1)
        ya_ref[...] = y_a.astype(BF16)
        yb_ref[...] = y_b.astype(BF16)
        merged = (_sigmoid_t(ga_ref[...].astype(F32)) * y_a + _sigmoid_t(gb_ref[...].astype(F32)) * y_b).astype(BF16)
        mg_ref[...] = merged
        mix = jnp.dot(merged, wo_ref[...], preferred_element_type=F32)
        mix_ref[...] = mix
        xhat, _ = _ln_stats(ALPHA * x_ref[...] + (1.0 + mod_ref[2:3, :]) * mix)
        x1_ref[...] = xhat * g1_ref[...] + b1_ref[...]

    row = lambda width, col: pl.BlockSpec((tm, width), functools.partial(lambda i, k: (i, k), k=col))
    full = lambda shape: pl.BlockSpec(shape, lambda i: (0,) * len(shape))
    return pl.pallas_call(
        body, name="mix_fwd", grid=(t // tm,),
        in_specs=[row(w, 0), row(w, 1), row(d_sgu, 0), row(d, 4), row(d, 5), row(d, 0),
                  pl.BlockSpec((None, 8, d), lambda i: (i // tpb, 0, 0)),
                  full(w_o_lru_g.shape), full(w_o_sgu_g.shape), full(w_out_g.shape), full(ln1_g.shape), full(ln1_b.shape)],
        out_specs=[row(w, 0), row(d, 0), row(d, 0), row(d, 0), row(d, 0), row(d, 0)],
        out_shape=[jax.ShapeDtypeStruct((t, w), BF16), jax.ShapeDtypeStruct((t, d), BF16),
                   jax.ShapeDtypeStruct((t, d), BF16), jax.ShapeDtypeStruct((t, d), BF16),
                   jax.ShapeDtypeStruct((t, d), F32), jax.ShapeDtypeStruct((t, d), F32)],
        compiler_params=_cp(("arbitrary",)),
    )(hs, proj, y_sgu, proj, proj, x2, modv, w_o_lru_g, w_o_sgu_g, w_out_g, ln1_g, ln1_b)


def _mlp_fwd(x1, modv, w_up_g, w_down_g, ln2_g, ln2_b, target, nb, seq):
    t, d = x1.shape
    nq, _, ns = w_up_g.shape
    tm = min(TM_MLP, seq)
    ts = min(TS_MLP, tm)
    tpb = seq // tm

    def body(x1_ref, mod_ref, wu_hbm, wd_hbm, g2_ref, b2_ref, tg_ref,
             rl_ref, act_ref, h2_ref, dz2_ref, df_ref, st_ref, pb_ref, wu_s, wd_s, acc, sems):
        i = pl.program_id(0)

        @pl.when(i == 0)
        def _():
            _load_weights((wu_hbm, wd_hbm), (wu_s, wd_s), sems)
            st_ref[...] = jnp.zeros_like(st_ref)

        @pl.when(i % tpb == 0)
        def _():
            pb_ref[...] = jnp.zeros_like(pb_ref)

        for sub in range(tm // ts):
            rows = slice(sub * ts, (sub + 1) * ts)
            x1v = x1_ref[rows, :]
            h2 = (x1v * (1.0 + mod_ref[4:5, :]) + mod_ref[3:4, :]).astype(BF16)
            h2_ref[rows, :] = h2
            for k in range(nq):
                cols = slice(k * ns, (k + 1) * ns)
                r = jnp.maximum(jnp.dot(h2, wu_s[k], preferred_element_type=F32), 0.0)
                act = (r * r).astype(BF16)
                rl_ref[rows, cols] = r.astype(BF16)
                act_ref[rows, cols] = act
                part = jnp.dot(act, wd_s[cols, :], preferred_element_type=F32)
                if k == 0:
                    acc[sub] = part
                else:
                    acc[sub] += part
            f = acc[sub]
            xhat, rstd = _ln_stats(ALPHA * x1v + (1.0 + mod_ref[5:6, :]) * f)
            y = xhat * g2_ref[...] + b2_ref[...]
            err = y - tg_ref[rows, :]
            dy = err * (1.0 / d)
            dz2 = _ln_bwd(dy * g2_ref[...], xhat, rstd)
            dz2_ref[rows, :] = dz2
            df_ref[rows, :] = ((1.0 + mod_ref[5:6, :]) * dz2).astype(BF16)
            st_ref[0:1, :] += _colsum(dy * xhat)
            st_ref[1:2, :] += _colsum(dy)
            st_ref[2:3, :] += (0.5 / d) * jnp.sum(_colsum(err * err), axis=1, keepdims=True)
            pb_ref[0:1, :] += _colsum(dz2 * f)

    tok = lambda i: (i, 0)
    return pl.pallas_call(
        body, name="mlp_fwd", grid=(t // tm,),
        in_specs=[pl.BlockSpec((tm, d), tok), pl.BlockSpec((None, 8, d), lambda i: (i // tpb, 0, 0)), _ANY, _ANY,
                  pl.BlockSpec((1, d), lambda i: (0, 0)), pl.BlockSpec((1, d), lambda i: (0, 0)),
                  pl.BlockSpec((tm, d), tok)],
        out_specs=[pl.BlockSpec((tm, nq * ns), tok), pl.BlockSpec((tm, nq * ns), tok),
                   pl.BlockSpec((tm, d), tok), pl.BlockSpec((tm, d), tok), pl.BlockSpec((tm, d), tok),
                   pl.BlockSpec((8, d), lambda i: (0, 0)), pl.BlockSpec((None, 8, d), lambda i: (i // tpb, 0, 0))],
        out_shape=[jax.ShapeDtypeStruct((t, nq * ns), BF16), jax.ShapeDtypeStruct((t, nq * ns), BF16),
                   jax.ShapeDtypeStruct((t, d), BF16),
                   jax.ShapeDtypeStruct((t, d), F32), jax.ShapeDtypeStruct((t, d), BF16),
                   jax.ShapeDtypeStruct((8, d), F32), jax.ShapeDtypeStruct((nb, 8, d), F32)],
        scratch_shapes=[pltpu.VMEM(w_up_g.shape, BF16), pltpu.VMEM(w_down_g.shape, BF16),
                        pltpu.VMEM((tm // ts, ts, d), F32), pltpu.SemaphoreType.DMA((2,))],
        compiler_params=_cp(("arbitrary",)),
    )(x1, modv, w_up_g, w_down_g, ln2_g, ln2_b, target)


def _mlp_bwd(df, up, w_down_g, w_up_g, dz2, x2, mix, modv, ln1_g, ln1_b, nb, seq):
    t, d = x2.shape
    nq, _, ns = w_up_g.shape
    tm = min(TM_MLP, seq)
    ts = min(TS_MLP, tm)
    tpb = seq // tm

    def body(df_ref, rl_ref, wd_hbm, wu_hbm, dz2_ref, x_ref, mix_ref, mod_ref, g1_ref, b1_ref,
             dup_ref, dz1_ref, dmix_ref, st_ref, pb_ref, wd_s, wu_s, acc, sems):
        i = pl.program_id(0)

        @pl.when(i == 0)
        def _():
            _load_weights((wd_hbm, wu_hbm), (wd_s, wu_s), sems)
            st_ref[...] = jnp.zeros_like(st_ref)

        @pl.when(i % tpb == 0)
        def _():
            pb_ref[...] = jnp.zeros_like(pb_ref)

        for sub in range(tm // ts):
            rows = slice(sub * ts, (sub + 1) * ts)
            dfv = df_ref[rows, :]
            for k in range(nq):
                cols = slice(k * ns, (k + 1) * ns)
                dup = (_mm_nt(dfv, wd_s[cols, :]) * (2.0 * rl_ref[rows, cols].astype(F32))).astype(BF16)
                dup_ref[rows, cols] = dup
                part = _mm_nt(dup, wu_s[k])
                if k == 0:
                    acc[sub] = part
                else:
                    acc[sub] += part
            dh2 = acc[sub]
            mix = mix_ref[rows, :]
            xhat, rstd = _ln_stats(ALPHA * x_ref[rows, :] + (1.0 + mod_ref[2:3, :]) * mix)
            x1 = xhat * g1_ref[...] + b1_ref[...]
            dx1 = ALPHA * dz2_ref[rows, :] + dh2 * (1.0 + mod_ref[4:5, :])
            dz1 = _ln_bwd(dx1 * g1_ref[...], xhat, rstd)
            dz1_ref[rows, :] = dz1
            dmix_ref[rows, :] = ((1.0 + mod_ref[2:3, :]) * dz1).astype(BF16)
            st_ref[0:1, :] += _colsum(dx1 * xhat)
            st_ref[1:2, :] += _colsum(dx1)
            pb_ref[0:1, :] += _colsum(dh2 * x1)
            pb_ref[1:2, :] += _colsum(dh2)
            pb_ref[2:3, :] += _colsum(dz1 * mix)

    tok = lambda i: (i, 0)
    return pl.pallas_call(
        body, name="mlp_bwd", grid=(t // tm,),
        in_specs=[pl.BlockSpec((tm, d), tok), pl.BlockSpec((tm, nq * ns), tok), _ANY, _ANY,
                  pl.BlockSpec((tm, d), tok), pl.BlockSpec((tm, d), tok), pl.BlockSpec((tm, d), tok),
                  pl.BlockSpec((None, 8, d), lambda i: (i // tpb, 0, 0)),
                  pl.BlockSpec((1, d), lambda i: (0, 0)), pl.BlockSpec((1, d), lambda i: (0, 0))],
        out_specs=[pl.BlockSpec((tm, nq * ns), tok),
                   pl.BlockSpec((tm, d), tok), pl.BlockSpec((tm, d), tok),
                   pl.BlockSpec((8, d), lambda i: (0, 0)), pl.BlockSpec((None, 8, d), lambda i: (i // tpb, 0, 0))],
        out_shape=[jax.ShapeDtypeStruct((t, nq * ns), BF16),
                   jax.ShapeDtypeStruct((t, d), F32), jax.ShapeDtypeStruct((t, d), BF16),
                   jax.ShapeDtypeStruct((8, d), F32), jax.ShapeDtypeStruct((nb, 8, d), F32)],
        scratch_shapes=[pltpu.VMEM(w_down_g.shape, BF16), pltpu.VMEM(w_up_g.shape, BF16),
                        pltpu.VMEM((tm // ts, ts, d), F32), pltpu.SemaphoreType.DMA((2,))],
        compiler_params=_cp(("arbitrary",), VMEM_LIMIT_MAX),
    )(df, up, w_down_g, w_up_g, dz2, x2, mix, modv, ln1_g, ln1_b)


def _mix_bwd(dmix, proj, y_a, y_b, hs, w_out_g, w_o_lru_g, w_o_sgu_g, seq, after=()):
    t, d = dmix.shape
    w = hs.shape[1]
    nq, d_sgu, ns = w_o_sgu_g.shape
    tm = min(TM_MIX, seq)

    def body(dmix_ref, ga_ref, gb_ref, ya_ref, yb_ref, gl_ref, hs_ref, wo_ref, wl_ref, ws_ref,
             dya_ref, dyb_ref, dg_ref, dyl_ref, dys_ref):
        dmerged = _mm_nt(dmix_ref[...], wo_ref[...])
        sa, sb = _sigmoid_t(ga_ref[...].astype(F32)), _sigmoid_t(gb_ref[...].astype(F32))
        dy_a = (dmerged * sa).astype(BF16)
        dy_b = (dmerged * sb).astype(BF16)
        dya_ref[...] = dy_a
        dyb_ref[...] = dy_b
        dg_ref[:, 4 * d:5 * d] = (dmerged * ya_ref[...].astype(F32) * (sa * (1.0 - sa))).astype(BF16)
        dg_ref[:, 5 * d:6 * d] = (dmerged * yb_ref[...].astype(F32) * (sb * (1.0 - sb))).astype(BF16)
        dyap = _mm_nt(dy_a, wl_ref[...])
        gel, dgel = _gelu_and_grad(gl_ref[...].astype(F32))
        dyl_ref[...] = (dyap * gel).astype(BF16)
        dg_ref[:, w:2 * w] = (dyap * hs_ref[...].astype(F32) * dgel).astype(BF16)
        dys = _mm_nt(dy_b[:, 0:ns], ws_ref[0])
        for q in range(1, nq):
            dys = dys + _mm_nt(dy_b[:, q * ns:(q + 1) * ns], ws_ref[q])
        dys_ref[...] = dys

    row = lambda width, col: pl.BlockSpec((tm, width), functools.partial(lambda i, k: (i, k), k=col))
    full = lambda shape: pl.BlockSpec(shape, lambda i: (0,) * len(shape))
    return pl.pallas_call(
        _ordered(body, 10, after), name="mix_bwd", grid=(t // tm,),
        in_specs=[row(d, 0), row(d, 4), row(d, 5), row(d, 0), row(d, 0), row(w, 1), row(w, 0),
                  full(w_out_g.shape), full(w_o_lru_g.shape), full(w_o_sgu_g.shape)] + [_ANY] * len(after),
        out_specs=[row(d, 0), row(d, 0), row(6 * d, 0), row(w, 0), row(d_sgu, 0)],
        out_shape=[jax.ShapeDtypeStruct((t, d), BF16), jax.ShapeDtypeStruct((t, d), BF16),
                   jax.ShapeDtypeStruct((t, 6 * d), BF16), jax.ShapeDtypeStruct((t, w), BF16),
                   jax.ShapeDtypeStruct((t, d_sgu), F32)],
        compiler_params=_cp(("arbitrary",)),
    )(dmix, proj, proj, y_a, y_b, proj, hs, w_out_g, w_o_lru_g, w_o_sgu_g, *after)


def _sgu_bwd(proj, dys, w_sp, b_sp_t, ln_g, ln_b, after=()):
    t = proj.shape[0]
    d_sgu = SGU_GROUPS * HEAD
    tm = min(TM_SGU, t)
    nblk = tm // HEAD
    specs, n_piece = _sgu_specs(tm, d_sgu)

    def body(*refs):
        u = jnp.concatenate([r[...] for r in refs[:n_piece]], axis=1).astype(F32)
        v = jnp.concatenate([r[...] for r in refs[n_piece:2 * n_piece]], axis=1).astype(F32)
        dys_ref, w_ref, bt_ref, g_ref, b_ref, du_ref, dv_ref, dw_ref, st_ref, dbt_ref, dvn_s = refs[2 * n_piece:]

        @pl.when(pl.program_id(0) == 0)
        def _():
            dw_ref[...] = jnp.zeros_like(dw_ref)
            st_ref[...] = jnp.zeros_like(st_ref)
            dbt_ref[...] = jnp.zeros_like(dbt_ref)

        ug, dug_du = _gelu_and_grad(u)
        vg, dvg_dv = _gelu_and_grad(v)
        xhat, rstd = _ln_stats(vg)
        vn = (xhat * g_ref[...] + b_ref[...]).astype(BF16)
        dys_v = dys_ref[...]
        mask = _sgu_mask()
        for g in range(SGU_GROUPS):
            wm = jnp.where(mask, w_ref[g], 0.0).astype(BF16)
            cols = slice(g * HEAD, (g + 1) * HEAD)
            dw_g = jnp.zeros((HEAD, HEAD), F32)
            db_g = jnp.zeros((HEAD, 1), F32)
            for n in range(nblk):
                rows = slice(n * HEAD, (n + 1) * HEAD)
                vn_blk = vn[rows, cols]
                mixed = jnp.dot(wm, vn_blk, preferred_element_type=F32) + bt_ref[:, g:g + 1]
                dy_blk = dys_v[rows, cols]
                du_ref[rows, cols] = (dy_blk * mixed * dug_du[rows, cols]).astype(BF16)
                dmx = dy_blk * ug[rows, cols]
                dvn_s[rows, cols] = _mm_tn(wm, dmx)
                dw_g = dw_g + _mm_nt(dmx, vn_blk)
                db_g = db_g + jnp.sum(dmx, axis=1, keepdims=True)
            dw_ref[g] += jnp.where(mask, dw_g, 0.0)
            dbt_ref[:, g:g + 1] += db_g
        dvn = dvn_s[...]
        st_ref[0:1, :] += _colsum(dvn * xhat)
        st_ref[1:2, :] += _colsum(dvn)
        dv_ref[...] = (_ln_bwd(dvn * g_ref[...], xhat, rstd) * dvg_dv).astype(BF16)

    full = lambda shape: pl.BlockSpec(shape, lambda i: (0,) * len(shape))
    tok = pl.BlockSpec((tm, d_sgu), lambda i: (i, 0))
    return pl.pallas_call(
        _ordered(body, 2 * n_piece + 5, after), name="sgu_bwd", grid=(t // tm,),
        in_specs=specs + [tok, full(w_sp.shape), full(b_sp_t.shape), full(ln_g.shape), full(ln_b.shape)]
        + [_ANY] * len(after),
        out_specs=[tok, tok, full(w_sp.shape), full((8, d_sgu)), full((HEAD, HEAD))],
        out_shape=[jax.ShapeDtypeStruct((t, d_sgu), BF16), jax.ShapeDtypeStruct((t, d_sgu), BF16),
                   jax.ShapeDtypeStruct(w_sp.shape, F32), jax.ShapeDtypeStruct((8, d_sgu), F32),
                   jax.ShapeDtypeStruct((HEAD, HEAD), F32)],
        scratch_shapes=[pltpu.VMEM((tm, d_sgu), F32)],
        compiler_params=_cp(("arbitrary",)),
    )(*([proj] * (2 * n_piece)), dys, w_sp, b_sp_t, ln_g, ln_b, *after)


def _lru_bwd(proj, hs, e, dyl, saved, lru_w, nb, seq, dproj, after=()):
    t = proj.shape[0]
    w = LRU_HEADS * HEAD
    w_conv, b_conv, w_a, b_a, w_x, b_x, lam = lru_w

    def body(x_ref, hs_ref, e_ref, dy_ref, a_ref, r_ref, gi_ref, xc_ref, wc_ref, wa_ref, wx_ref, lam_ref,
             dxl_ref, dwa_ref, dwx_ref, st_ref):
        @pl.when(pl.program_id(1) == 0)
        def _():
            dwa_ref[...] = jnp.zeros_like(dwa_ref)
            dwx_ref[...] = jnp.zeros_like(dwx_ref)
            st_ref[...] = jnp.zeros_like(st_ref)

        xl = x_ref[...].astype(F32)
        a, r, gi, xc = a_ref[...], r_ref[...].astype(F32), gi_ref[...].astype(F32), xc_ref[...].astype(F32)
        big_l = _lru_rate(lam_ref)
        m2 = (1.0 - a) * (1.0 + a)
        inv_mult = lax.rsqrt(m2)
        mult = m2 * inv_mult
        dh = dy_ref[...].astype(F32) + _shift_up(e_ref[...], 1)
        da = dh * _shift_down(hs_ref[...].astype(F32), 1)
        dmult = dh * (gi * xc)
        d_i = dh * (mult * xc)
        dxc = dh * (mult * gi)
        dla = a * (da - dmult * (a * inv_mult))
        dr = dla * big_l
        d_big_l = _colsum(dla * r)
        dra = dr * (r * (1.0 - r))
        dia = d_i * (gi * (1.0 - gi))
        dwa_ref[...] += _mm_tn(xc, dra)
        dwx_ref[...] += _mm_tn(xc, dia)
        dxc = dxc + _mm_nt(dra, wa_ref[...]) + _mm_nt(dia, wx_ref[...])
        dxl = wc_ref[CONV_WIDTH - 1:CONV_WIDTH, :] * dxc
        st_ref[4 + CONV_WIDTH - 1:4 + CONV_WIDTH, :] += _colsum(dxc * xl)
        for k in range(CONV_WIDTH - 1):
            ahead = _shift_up(dxc, CONV_WIDTH - 1 - k)
            dxl = dxl + wc_ref[k:k + 1, :] * ahead
            st_ref[4 + k:5 + k, :] += _colsum(ahead * xl)
        dxl_ref[...] = dxl.astype(BF16)
        st_ref[0:1, :] += _colsum(dra)
        st_ref[1:2, :] += _colsum(dia)
        st_ref[2:3, :] += d_big_l * (LRU_C * _sigmoid(-lam_ref[...]))
        st_ref[3:4, :] += _colsum(dxc)

    col = lambda hd, b: (0, hd)
    head = lambda hd, b: (hd, 0, 0)
    tok = lambda hd, b: (b, hd)
    seq_blk = pl.BlockSpec((seq, HEAD), tok)
    return pl.pallas_call(
        _ordered(body, 12, (dproj,) + tuple(after)), name="lru_bwd", grid=(LRU_HEADS, nb),
        in_specs=[seq_blk] * 8 + [pl.BlockSpec((CONV_WIDTH, HEAD), col), pl.BlockSpec((None, HEAD, HEAD), head),
                                  pl.BlockSpec((None, HEAD, HEAD), head), pl.BlockSpec((1, HEAD), col)]
        + [_ANY] * (1 + len(after)),
        out_specs=[seq_blk, pl.BlockSpec((None, HEAD, HEAD), head), pl.BlockSpec((None, HEAD, HEAD), head),
                   pl.BlockSpec((8, HEAD), col)],
        out_shape=[jax.ShapeDtypeStruct(dproj.shape, BF16), jax.ShapeDtypeStruct((LRU_HEADS, HEAD, HEAD), F32),
                   jax.ShapeDtypeStruct((LRU_HEADS, HEAD, HEAD), F32), jax.ShapeDtypeStruct((8, w), F32)],
        input_output_aliases={12: 0},
        compiler_params=_cp(("arbitrary", "arbitrary")),
    )(proj, hs, e, dyl, *saved, w_conv, w_a, w_x, lam, dproj, *after)


def _weight_grad(a, g, col_shards, name, after=()):
    t, k = a.shape
    n = g.shape[1]
    tt = min(TT_DW, t)
    tk = k if k <= 1536 else 1024
    ns = n // N_CHIPS if col_shards else n
    narrow = col_shards and ns < 512
    tn = n if narrow else min(ns, 768 if ns % 768 == 0 else 1024)
    while ns % tn and not narrow:
        tn //= 2
    per = max(ns // tn, 1)

    def body(a_ref, g_ref, o_ref):
        @pl.when(pl.program_id(2) == 0)
        def _():
            o_ref[...] = jnp.zeros_like(o_ref)

        res = _mm_tn(a_ref[...], g_ref[...])
        if narrow:
            for q in range(N_CHIPS):
                o_ref[q] += res[:, q * ns:(q + 1) * ns]
        else:
            o_ref[...] += res

    if narrow:
        out_spec = pl.BlockSpec((N_CHIPS, tk, ns), lambda i, j, s: (0, i, 0))
        out_shape = jax.ShapeDtypeStruct((N_CHIPS, k, ns), F32)
    elif col_shards:
        out_spec = pl.BlockSpec((None, tk, tn), lambda i, j, s: (j // per, i, j % per))
        out_shape = jax.ShapeDtypeStruct((N_CHIPS, k, ns), F32)
    else:
        out_spec = pl.BlockSpec((tk, tn), lambda i, j, s: (i, j))
        out_shape = jax.ShapeDtypeStruct((k, n), F32)
    return pl.pallas_call(
        _ordered(body, 2, after), name=name, grid=(k // tk, n // tn, t // tt),
        in_specs=[pl.BlockSpec((tt, tk), lambda i, j, s: (s, i)), pl.BlockSpec((tt, tn), lambda i, j, s: (s, j))]
        + [_ANY] * len(after),
        out_specs=out_spec, out_shape=out_shape,
        compiler_params=_cp(("arbitrary", "arbitrary", "arbitrary")),
    )(a, g, *after)


def _input_grad(dproj, ws, slots, dz1, x2, modv, nb, seq, after=()):
    t, d = x2.shape
    nq = len(ws)
    ns = ws[0].shape[1]
    tm = min(TM_DH, seq)
    ts = min(TS_MLP, tm)
    tpb = seq // tm

    def body(slot_ref, dp_ref, *refs):
        w_hbm = refs[:nq]
        dz1_ref, x_ref, mod_ref, gx_ref, db_ref, pb_ref, w_s, acc, sems = refs[nq:]
        i = pl.program_id(0)

        @pl.when(i == 0)
        def _():
            _load_weights(w_hbm, [w_s.at[slot_ref[k]] for k in range(nq)], sems)
            db_ref[...] = jnp.zeros_like(db_ref)

        @pl.when(i % tpb == 0)
        def _():
            pb_ref[...] = jnp.zeros_like(pb_ref)

        for sub in range(tm // ts):
            rows = slice(sub * ts, (sub + 1) * ts)
            for q in range(nq):
                dp = dp_ref[rows, q * ns:(q + 1) * ns]
                part = _mm_nt(dp, w_s[q])
                if q == 0:
                    acc[sub] = part
                else:
                    acc[sub] += part
                db_ref[q, 0:1, :] += _colsum(dp.astype(F32))
            dh = acc[sub]
            gx_ref[rows, :] = ALPHA * dz1_ref[rows, :] + dh * (1.0 + mod_ref[1:2, :])
            pb_ref[0:1, :] += _colsum(dh * x_ref[rows, :])
            pb_ref[1:2, :] += _colsum(dh)

    tok = lambda i, s: (i, 0)
    in_specs = [pl.BlockSpec((tm, nq * ns), tok)] + [_ANY] * nq
    in_specs += [pl.BlockSpec((tm, d), tok), pl.BlockSpec((tm, d), tok),
                 pl.BlockSpec((None, 8, d), lambda i, s: (i // tpb, 0, 0))] + [_ANY] * len(after)
    return pl.pallas_call(
        _ordered(body, 5 + nq, after), name="input_grad",
        grid_spec=pltpu.PrefetchScalarGridSpec(
            num_scalar_prefetch=1, grid=(t // tm,), in_specs=in_specs,
            out_specs=[pl.BlockSpec((tm, d), tok), pl.BlockSpec((nq, 8, ns), lambda i, s: (0, 0, 0)),
                       pl.BlockSpec((None, 8, d), lambda i, s: (i // tpb, 0, 0))],
            scratch_shapes=[pltpu.VMEM((nq, d, ns), BF16), pltpu.VMEM((tm // ts, ts, d), F32),
                            pltpu.SemaphoreType.DMA((nq,))]),
        out_shape=[jax.ShapeDtypeStruct((t, d), F32), jax.ShapeDtypeStruct((nq, 8, ns), F32),
                   jax.ShapeDtypeStruct((nb, 8, d), F32)],
        compiler_params=_cp(("arbitrary",)),
    )(slots, dproj, *ws, dz1, x2, modv, *after)


def _rows128(v):
    flat = v.reshape(-1, HEAD)
    pad = (-flat.shape[0]) % 8
    return jnp.pad(flat, ((0, pad), (0, 0))) if pad else flat


def kernel(x, c, w_ada, b_ada, w_in, b_in, w_conv, b_conv, w_rg_a, b_rg_a, w_rg_x, b_rg_x, lru_lambda, w_sp, b_sp, ln_v_g, ln_v_b, w_o_lru, w_o_sgu, w_out, ln1_g, ln1_b, w_up, w_down, ln2_g, ln2_b, loss_target, m_w_ada, m_b_ada, m_w_in, m_b_in, m_w_conv, m_b_conv, m_w_rg_a, m_b_rg_a, m_w_rg_x, m_b_rg_x, m_lru_lambda, m_w_sp, m_b_sp, m_ln_v_g, m_ln_v_b, m_w_o_lru, m_w_o_sgu, m_w_out, m_ln1_g, m_ln1_b, m_w_up, m_w_down, m_ln2_g, m_ln2_b, v_w_ada, v_b_ada, v_w_in, v_b_in, v_w_conv, v_b_conv, v_w_rg_a, v_b_rg_a, v_w_rg_x, v_b_rg_x, v_lru_lambda, v_w_sp, v_b_sp, v_ln_v_g, v_ln_v_b, v_w_o_lru, v_w_o_sgu, v_w_out, v_ln1_g, v_ln1_b, v_w_up, v_w_down, v_ln2_g, v_ln2_b):
    given = dict(locals())
    nb, seq, d = x.shape
    t = nb * seq
    w_lru = LRU_HEADS * HEAD
    d_sgu = SGU_GROUPS * HEAD
    xi, yi, ci = lax.axis_index("x"), lax.axis_index("y"), lax.axis_index("c")
    chip = 2 * xi + yi
    dev = 2 * chip + ci
    cidx = jnp.reshape(ci, (1,)).astype(jnp.int32)

    x2 = x.reshape(t, d)
    target = loss_target.reshape(t, d)

    big = ["w_in", "w_o_lru", "w_o_sgu", "w_out", "w_up", "w_down"]
    shards_a = [w_in[0].astype(BF16)]
    shards_b = [given[n][0].astype(BF16) for n in big[1:]]
    pidx = jnp.reshape(chip, (1,)).astype(jnp.int32)

    c_rows = _rows128(c)
    wconv_rows = _rows128(w_conv[0])
    slab0 = _all_gather_small(jnp.concatenate([c_rows, wconv_rows], axis=0), "gather_c_wconv")
    slab0 = slab0.reshape(N_DEV, -1, HEAD)
    c_all = slab0[:, :c_rows.shape[0]].reshape(N_DEV * nb, d)
    n_wc = CONV_WIDTH * (w_lru // N_CHIPS) // HEAD
    wc = slab0[0::2, c_rows.shape[0]:c_rows.shape[0] + n_wc].reshape(N_CHIPS, CONV_WIDTH, w_lru // N_CHIPS)
    w_conv_full = jnp.transpose(wc, (1, 0, 2)).reshape(CONV_WIDTH, w_lru)

    n_ada = w_ada.shape[2]
    b_ada_cols = lax.dynamic_slice(b_ada, (0, chip * n_ada), (1, n_ada))
    mod_cols = _ada_fwd(c_all, w_ada[0], b_ada_cols)
    half = (N_DEV * nb) // 2
    mod_half = lax.dynamic_slice(mod_cols, (ci * half, 0), (half, n_ada))
    mod_g = _all_gather_small(mod_half, "gather_mod").reshape(N_CHIPS, 2, half, n_ada)
    mod_all = jnp.transpose(mod_g, (1, 2, 0, 3)).reshape(N_DEV * nb, N_CHIPS * n_ada)
    mod_loc = lax.dynamic_slice(mod_all, (dev * nb, 0), (nb, N_CHIPS * n_ada)).reshape(nb, 6, d)
    modv = jnp.pad(mod_loc, ((0, 0), (0, 2), (0, 0)))

    lru_w = (w_conv_full, b_conv, w_rg_a[0], b_rg_a, w_rg_x[0], b_rg_x, lru_lambda)
    b_sp_t = jnp.transpose(b_sp[0])

    land = lambda s: jax.ShapeDtypeStruct((N_CHIPS,) + s.shape, s.dtype)
    sds = lambda s: jax.ShapeDtypeStruct(s.shape, s.dtype)
    started_a = _split_start(shards_a, [sds(shards_a[0])] * 2, _peer_gather_copies((0, 1)), 2, "gather_w_in_near_start",
                             after=(modv,))
    shards_b, shards_c = shards_b[:3], shards_b[3:]

    ids = lambda *v: jnp.stack(v).astype(jnp.int32)
    modv_t = modv + started_a[-1][0:1, 0:1]
    proj, h = _proj_fwd(x2, modv_t, [started_a[2]], ids(chip), b_in, seq, "proj_fwd_own")
    own_a, lands_a = _split_wait(started_a, 1, _peer_gather_copies((0, 1)), "gather_w_in_near_wait",
                                 after=(proj, *shards_b, *shards_c))
    started_f = _split_start(own_a, [sds(own_a[0])], _far_gather_copies, 1, "gather_w_in_far_start", after=(lands_a[0],))
    started_b = _split_start(shards_b, [land(s) for s in shards_b], _gather_copies, 3 * len(shards_b),
                             "gather_w_mix_start", after=(started_f[-1],))
    started_c = _split_start(shards_c, [land(s) for s in shards_c], _gather_copies, 3 * len(shards_c),
                             "gather_w_mlp_start", after=(started_b[-1],))
    modv_t = modv + started_c[-1][0:1, 0:1]
    (proj,) = _proj_fwd(x2, modv_t, lands_a, ids(chip ^ 1, chip ^ 2), b_in, seq, "proj_fwd_near", proj_in=proj)
    own_a, land_f = _split_wait(started_f, 1, _far_gather_copies, "gather_w_in_far_wait", after=(proj,))
    (proj,) = _proj_fwd(x2, modv, land_f, ids(chip ^ 3), b_in, seq, "proj_fwd_far", proj_in=proj)
    w_in_shards, w_in_chips = own_a + lands_a + land_f, ids(chip, chip ^ 1, chip ^ 2, chip ^ 3)
    a, inp, r16, gi16, xc16 = _lru_prep(proj, lru_w, nb, seq)
    a3 = a.reshape(nb, seq, w_lru)
    hs = _scan(a3, inp.reshape(nb, seq, w_lru), False, "lru_scan", BF16).reshape(t, w_lru)
    y_sgu = _sgu_fwd(proj, w_sp[0], b_sp_t, ln_v_g, ln_v_b)
    shards_b, lands_b = _split_wait(started_b, len(shards_b), _gather_copies, "gather_w_mix_wait", after=(hs, y_sgu))
    w_o_lru_g, w_o_sgu_g, w_out_g = _fill_own_slot(lands_b, shards_b, pidx, ["own_" + n for n in big[1:4]])
    w_o_lru_g = w_o_lru_g.reshape(w_lru, d)
    w_out_g = w_out_g.reshape(d, d)
    yap, y_a, y_b, merged, mix, x1 = _mix_fwd(hs, proj, y_sgu, x2, modv, w_o_lru_g, w_o_sgu_g, w_out_g, ln1_g, ln1_b, seq)
    shards_c, lands_c = _split_wait(started_c, len(shards_c), _gather_copies, "gather_w_mlp_wait", after=(x1,))
    w_up_g, w_down_g = _fill_own_slot(lands_c, shards_c, pidx, ["own_" + n for n in big[4:]])
    w_down_g = w_down_g.reshape(-1, d)
    up, act, h2, dz2, df, st2, pb2 = _mlp_fwd(x1, modv, w_up_g, w_down_g, ln2_g, ln2_b, target, nb, seq)

    part = {}

    def to_sibling_start(group, tag, after=()):
        g4 = []
        for n in group:
            shard = given[n].shape[1:]
            g4.append(part[n].reshape(N_CHIPS, 2, shard[0] // 2, shard[1]))
        shapes = [jax.ShapeDtypeStruct((N_CHIPS,) + g.shape[2:], F32) for g in g4]
        return _split_start(g4, shapes, _to_sibling_copies, len(g4), "grads_to_sibling_start_" + tag, after)

    def to_chips_start(group, started, tag, after=()):
        g4, recv = _split_wait(started, len(group), _to_sibling_copies, "grads_to_sibling_wait_" + tag, after)
        own4 = [_add_own_half(g4[k], recv[k], cidx, "grad_pair_sum_" + n) for k, n in enumerate(group)]
        shapes = [jax.ShapeDtypeStruct((3,) + o.shape[1:], BF16) for o in own4]
        return _split_start(own4, shapes, _chip_exchange_copies, 3 * len(own4), "grads_chip_exchange_start_" + tag)

    def chips_finish(group, started, tag, after=()):
        own4, slots = _split_wait(started, len(group), _chip_exchange_copies, "grads_chip_exchange_wait_" + tag, after)
        return [_sum_own_and_peers(own4[k], slots[k], pidx, "grad_chip_sum_" + n) for k, n in enumerate(group)]

    dup, dz1, dmix, st1, pb1 = _mlp_bwd(df, up, w_down_g, w_up_g, dz2, x2, mix, modv, ln1_g, ln1_b, nb, seq)
    group1 = ["w_up", "w_down"]
    part["w_up"] = _weight_grad(h2, dup, True, "grad_w_up")
    part["w_down"] = _weight_grad(act, df, False, "grad_w_down")
    sib1 = to_sibling_start(group1, "mlp")
    dy_a, dy_b, dproj, dyl, dys = _mix_bwd(dmix, proj, y_a, y_b, hs, w_out_g, w_o_lru_g, w_o_sgu_g, seq,
                                                after=(sib1[-1],))
    group2 = ["w_o_lru", "w_o_sgu", "w_out"]
    part["w_o_lru"] = _weight_grad(yap, dy_a, False, "grad_w_o_lru")
    part["w_o_sgu"] = _weight_grad(y_sgu, dy_b, True, "grad_w_o_sgu")
    part["w_out"] = _weight_grad(merged, dmix, False, "grad_w_out")
    chips1 = to_chips_start(group1, sib1, "mlp", after=(dys, part["w_o_lru"], part["w_o_sgu"], part["w_out"]))
    sib2 = to_sibling_start(group2, "mix", after=(chips1[-1],))
    du, dv, g_w_sp, st_sgu, g_b_sp_t = _sgu_bwd(proj, dys, w_sp[0], b_sp_t, ln_v_g, ln_v_b, after=(sib2[-1],))
    dyl3 = dyl.reshape(nb, seq, w_lru)
    e = _scan(a3, dyl3, True, "lru_scan_bwd", F32).reshape(t, w_lru)
    chips2 = to_chips_start(group2, sib2, "mix", after=(e, du))
    dproj = lax.dynamic_update_slice(dproj, du, (0, 2 * w_lru))
    dproj = lax.dynamic_update_slice(dproj, dv, (0, 2 * w_lru + d_sgu))
    dproj, g_w_rg_a, g_w_rg_x, st_lru = _lru_bwd(proj, hs, e, dyl, (a, r16, gi16, xc16), lru_w, nb, seq, dproj,
                                                 after=(chips2[-1],))

    didx = jnp.reshape(dev, (1,)).astype(jnp.int32)
    early = [
        ("w_conv", st_lru[4:8]), ("b_conv", st_lru[3]), ("w_rg_a", g_w_rg_a), ("b_rg_a", st_lru[0]),
        ("w_rg_x", g_w_rg_x), ("b_rg_x", st_lru[1]), ("lru_lambda", st_lru[2]), ("w_sp", g_w_sp),
        ("b_sp", jnp.transpose(g_b_sp_t[:, :SGU_GROUPS])), ("ln_v_g", st_sgu[0]), ("ln_v_b", st_sgu[1]),
        ("ln1_g", st1[0]), ("ln1_b", st1[1]), ("ln2_g", st2[0]), ("ln2_b", st2[1]),
    ]
    pieces_e = [_rows128(v) for _, v in early]
    slab_e = jnp.concatenate(pieces_e, axis=0)
    slab_e = jnp.pad(slab_e, ((0, (-slab_e.shape[0]) % TR_EW), (0, 0)))
    small_st = _split_start([slab_e], [jax.ShapeDtypeStruct((N_DEV,) + slab_e.shape, F32)], _all_devices_copies, N_DEV - 1,
                            "small_grads_start")

    group3 = ["w_in"]
    part["w_in"] = _weight_grad(h, dproj, True, "grad_w_in", after=(small_st[-1],))
    sib3 = to_sibling_start(group3, "in")
    halves12 = (chips_finish(group1, chips1, "mlp", after=(sib3[-1],))
                + chips_finish(group2, chips2, "mix", after=(sib3[-1],)))
    swap12 = _split_start(halves12, [jax.ShapeDtypeStruct(hv.shape, F32) for hv in halves12], _swap_copies, len(halves12),
                          "grads_swap_start")
    chips3 = to_chips_start(group3, sib3, "in", after=(swap12[-1],))
    grad_x2, g_b_in4, pb0 = _input_grad(dproj, w_in_shards, w_in_chips, dz1, x2, modv, nb, seq, after=(chips3[-1],))
    loss = lax.psum(st2[2, 0] + swap12[-1][0, 0], ("x", "y", "c"))
    grads = {}

    dmod_loc = jnp.stack([pb0[:, 1], pb0[:, 0], pb1[:, 2], pb1[:, 1], pb1[:, 0], pb2[:, 0]], axis=1)
    late = [("dmod", dmod_loc), ("b_in", g_b_in4[:, 0])]
    pieces_l = [_rows128(v) for _, v in late]
    slab_l = jnp.concatenate(pieces_l, axis=0)
    gathered = _all_gather_small(slab_l, "gather_small_grads", after=(swap12[-1],)).reshape(N_DEV, slab_l.shape[0], HEAD)
    rows_dmod = dmod_loc.size // HEAD
    dmod_all = gathered[:, :rows_dmod].reshape(N_DEV * nb, 6 * d)
    grads["b_in"] = _sum_slots(gathered[:, rows_dmod:], "grad_b_in_sum").reshape(1, -1)

    (slab_e,), (lands_e,) = _split_wait(small_st, 1, _all_devices_copies, "small_grads_wait", after=(gathered,))
    summed = _sum_devices(lands_e, slab_e, didx, "small_grad_sum")
    off = 0
    for (n, v), piece in zip(early, pieces_e):
        grads[n] = summed[off:off + v.size // HEAD].reshape(v.shape)
        off += piece.shape[0]

    mine12, theirs12 = _split_wait(swap12, len(halves12), _swap_copies, "grads_swap_wait", after=(summed,))
    (mine3,) = chips_finish(group3, chips3, "in", after=(summed,))
    (theirs3,) = _exchange([mine3], [jax.ShapeDtypeStruct(mine3.shape, F32)], _swap_copies, 1, "grads_swap_w_in")
    mine = dict(zip(group1 + group2 + group3, mine12 + [mine3]))
    theirs = dict(zip(group1 + group2 + group3, theirs12 + [theirs3]))

    dmod_cols = lax.dynamic_slice(dmod_all, (0, chip * n_ada), (N_DEV * nb, n_ada))
    grads["w_ada"], grads["b_ada"] = _ada_bwd(c_all, dmod_all, dmod_cols)
    n_wcs = w_lru // N_CHIPS
    grads["w_conv"] = lax.dynamic_slice(grads["w_conv"], (0, chip * n_wcs), (CONV_WIDTH, n_wcs))

    names = ['w_ada', 'b_ada', 'w_in', 'b_in', 'w_conv', 'b_conv', 'w_rg_a', 'b_rg_a', 'w_rg_x', 'b_rg_x', 'lru_lambda',
             'w_sp', 'b_sp', 'ln_v_g', 'ln_v_b', 'w_o_lru', 'w_o_sgu', 'w_out', 'ln1_g', 'ln1_b', 'w_up', 'w_down',
             'ln2_g', 'ln2_b']
    two_d = lambda v: v.reshape(-1, v.shape[-1])
    done = {}
    small_names = [n for n in names if n not in big and n != "w_ada"]
    small_out = _adamw_many([(two_d(given[n]), two_d(grads[n].reshape(given[n].shape)), two_d(given["m_" + n]),
                              two_d(given["v_" + n])) for n in small_names], "adamw_small")
    for n, res in zip(small_names, small_out):
        done[n] = (grads[n],) + tuple(res)
    for n in big + ["w_ada"]:
        w2, m2, v2 = two_d(given[n]), two_d(given["m_" + n]), two_d(given["v_" + n])
        if n in big:
            done[n] = _adamw_halves(w2, mine[n], theirs[n], m2, v2, cidx, "adamw_" + n)
        else:
            done[n] = (grads[n],) + tuple(_adamw(w2, two_d(grads[n]), m2, v2, "adamw_" + n))
    outs = [[done[n][k].reshape(given[n].shape) for n in names] for k in range(4)]
    return (loss, grad_x2.reshape(nb, seq, d), *outs[0], *outs[1], *outs[2], *outs[3])
```

```python
import functools
import math

import jax
import jax.numpy as jnp
from jax import lax
from jax.experimental import pallas as pl
from jax.experimental.pallas import tpu as pltpu

F32 = jnp.float32
BF16 = jnp.bfloat16
MESH = pl.DeviceIdType.MESH

N_CHIPS = 4
N_DEV = 8
LRU_HEADS = 10
HEAD = 128
SGU_GROUPS = 6
SGU_CHUNK = 64
CONV_WIDTH = 4
LRU_C = 8.0
ALPHA = 2.0 ** 0.25
LN_EPS = 1e-5
ADAM_LR, ADAM_B1, ADAM_B2, ADAM_EPS, ADAM_WD, ADAM_STEP = 0.001, 0.9, 0.999, 1e-08, 0.01, 10

VMEM_LIMIT = 56 * 1024 * 1024
VMEM_LIMIT_MAX = 62 * 1024 * 1024
TM_PROJ = 1024
TM_MIX = 512
TM_MLP = 512
TS_MLP = 256
TM_SGU = 512
TM_DH = 512
TT_DW = 4096
TC_SCAN = 256
TR_EW = 256


def _cp(sem=None, limit=None):
    return pltpu.CompilerParams(dimension_semantics=sem, vmem_limit_bytes=limit or VMEM_LIMIT)


def _mm(a, b):
    return jnp.dot(a.astype(BF16), b.astype(BF16), preferred_element_type=F32)


def _mm_nt(a, b):
    return lax.dot_general(a.astype(BF16), b.astype(BF16), (((1,), (1,)), ((), ())), preferred_element_type=F32)


def _mm_tn(a, b):
    return lax.dot_general(a.astype(BF16), b.astype(BF16), (((0,), (0,)), ((), ())), preferred_element_type=F32)


def _sigmoid(x):
    return 1.0 / (1.0 + jnp.exp(-x))


def _sigmoid_t(x):
    return 0.5 * jnp.tanh(0.5 * x) + 0.5


_GELU_K = math.sqrt(2.0 / math.pi)


def _gelu(x):
    t = jnp.tanh(_GELU_K * (x + 0.044715 * (x * x * x)))
    return 0.5 * x * (1.0 + t)


def _gelu_and_grad(x):
    x2 = x * x
    t = jnp.tanh(_GELU_K * (x + 0.044715 * (x2 * x)))
    g = 0.5 * x * (1.0 + t)
    dg = 0.5 * (1.0 + t) + 0.5 * x * (1.0 - t * t) * (_GELU_K * (1.0 + 3.0 * 0.044715 * x2))
    return g, dg


def _ln_stats(z):
    mu = jnp.mean(z, axis=-1, keepdims=True)
    zc = z - mu
    var = jnp.mean(zc * zc, axis=-1, keepdims=True)
    rstd = lax.rsqrt(var + LN_EPS)
    return zc * rstd, rstd


def _ln_bwd(dxh, xhat, rstd):
    m1 = jnp.mean(dxh, axis=-1, keepdims=True)
    m2 = jnp.mean(dxh * xhat, axis=-1, keepdims=True)
    return rstd * (dxh - m1 - xhat * m2)


def _colsum(v):
    return jnp.sum(v, axis=0, keepdims=True)


def _shift_down(v, j):
    if j == 0:
        return v
    rows = lax.broadcasted_iota(jnp.int32, v.shape, 0)
    return jnp.where(rows >= j, pltpu.roll(v, j, 0), 0.0)


def _shift_up(v, j):
    if j == 0:
        return v
    n = v.shape[0]
    rows = lax.broadcasted_iota(jnp.int32, v.shape, 0)
    return jnp.where(rows < n - j, pltpu.roll(v, n - j, 0), 0.0)


def _load_weights(srcs, dsts, sems):
    cps = [pltpu.make_async_copy(s, dd, sems.at[k]) for k, (s, dd) in enumerate(zip(srcs, dsts))]
    for cp in cps:
        cp.start()
    for cp in cps:
        cp.wait()


def _my_pos():
    return lax.axis_index("x"), lax.axis_index("y"), lax.axis_index("c")


def _all_gather_small(v, name, after=()):
    m_per, n = v.shape

    def body(x_ref, out_ref, send_sems, recv_sems, local_sem):
        x, y, c = _my_pos()
        me, sibling = (x, y, c), (x, y, 1 - c)
        chips = [(1 - x, y), (x, 1 - y), (1 - x, 1 - y)]

        def rows(px, py, pc):
            return out_ref.at[pl.ds((4 * px + 2 * py + pc) * m_per, m_per), :]

        def copy(k, block, to, src=None):
            return pltpu.make_async_remote_copy(
                src_ref=rows(*block) if src is None else src, dst_ref=rows(*block),
                send_sem=send_sems.at[k], recv_sem=recv_sems.at[k], device_id=to, device_id_type=MESH)

        mine = pltpu.make_async_copy(x_ref, rows(*me), local_sem)
        mine.start()
        first = [copy(0, me, sibling, src=x_ref)]
        first += [copy(1 + j, me, (*chip, c), src=x_ref) for j, chip in enumerate(chips)]
        for cp in first:
            cp.start()
        passed = [copy(4 + j, (*chip, c), sibling) for j, chip in enumerate(chips)]
        for j, chip in enumerate(chips):
            copy(1 + j, (*chip, c), me).wait_recv()
            passed[j].start()
        copy(0, sibling, me).wait_recv()
        for j, chip in enumerate(chips):
            copy(4 + j, (*chip, 1 - c), me).wait_recv()
        for cp in first + passed:
            cp.wait_send()
        mine.wait()

    return pl.pallas_call(
        _ordered(body, 1, after), name=name,
        out_shape=jax.ShapeDtypeStruct((N_DEV * m_per, n), v.dtype),
        in_specs=[pl.BlockSpec(memory_space=pltpu.VMEM)] + [pl.BlockSpec(memory_space=pl.ANY)] * len(after),
        out_specs=pl.BlockSpec(memory_space=pltpu.VMEM),
        scratch_shapes=[pltpu.SemaphoreType.DMA((7,)), pltpu.SemaphoreType.DMA((7,)), pltpu.SemaphoreType.DMA],
        compiler_params=pltpu.CompilerParams(vmem_limit_bytes=VMEM_LIMIT),
    )(v, *after)


_HBM = pl.BlockSpec(memory_space=pltpu.HBM)
_ANY = pl.BlockSpec(memory_space=pl.ANY)
_SEM = pl.BlockSpec(memory_space=pltpu.SEMAPHORE)
_EFFECT = pltpu.SideEffectType.DATAFLOW_SIDE_EFFECTING


def _ordered(body, n_in, after):
    k = len(after)
    if not k:
        return body
    return lambda *refs: body(*refs[:n_in], *refs[n_in + k:])


def _gather_copies(ins, lands, send_sems, recv_sems):
    x, y, c = _my_pos()
    p = 2 * x + y
    peers = [(x, 1 - y), (1 - x, y), (1 - x, 1 - y)]
    sends, recvs = [], []
    for k in range(len(ins)):
        for j, (qx, qy) in enumerate(peers):
            sems = dict(send_sem=send_sems.at[3 * k + j], recv_sem=recv_sems.at[3 * k + j],
                        device_id=(qx, qy, c), device_id_type=MESH)
            sends.append(pltpu.make_async_remote_copy(src_ref=ins[k], dst_ref=lands[k].at[p], **sems))
            recvs.append(pltpu.make_async_remote_copy(src_ref=ins[k], dst_ref=lands[k].at[2 * qx + qy], **sems))
    return sends, recvs


def _peer_gather_copies(peers):
    def copies(ins, lands, send_sems, recv_sems):
        x, y, c = _my_pos()
        where = [(x, 1 - y), (1 - x, y), (1 - x, 1 - y)]
        cps = [pltpu.make_async_remote_copy(
            src_ref=ins[0], dst_ref=lands[j], send_sem=send_sems.at[j], recv_sem=recv_sems.at[j],
            device_id=(*where[j], c), device_id_type=MESH) for j in peers]
        return cps, cps
    return copies


def _far_gather_copies(ins, lands, send_sems, recv_sems):
    x, y, c = _my_pos()
    cps = [pltpu.make_async_remote_copy(
        src_ref=ins[0], dst_ref=lands[0], send_sem=send_sems.at[0], recv_sem=recv_sems.at[0],
        device_id=(1 - x, 1 - y, c), device_id_type=MESH)]
    return cps, cps


def _to_sibling_copies(ins, lands, send_sems, recv_sems):
    x, y, c = _my_pos()
    cps = [pltpu.make_async_remote_copy(
        src_ref=ins[k].at[:, 1 - c], dst_ref=lands[k], send_sem=send_sems.at[k], recv_sem=recv_sems.at[k],
        device_id=(x, y, 1 - c), device_id_type=MESH) for k in range(len(ins))]
    return cps, cps


def _chip_exchange_copies(ins, lands, send_sems, recv_sems):
    x, y, c = _my_pos()
    peers = [(x, 1 - y), (1 - x, y), (1 - x, 1 - y)]
    cps = []
    for k in range(len(ins)):
        for j, (qx, qy) in enumerate(peers):
            cps.append(pltpu.make_async_remote_copy(
                src_ref=ins[k].at[2 * qx + qy], dst_ref=lands[k].at[j], send_sem=send_sems.at[3 * k + j],
                recv_sem=recv_sems.at[3 * k + j], device_id=(qx, qy, c), device_id_type=MESH))
    return cps, cps


def _all_devices_copies(ins, lands, send_sems, recv_sems):
    x, y, c = _my_pos()
    me = 4 * x + 2 * y + c
    sends, recvs = [], []
    for r in range(1, N_DEV):
        px = 1 - x if r & 4 else x
        py = 1 - y if r & 2 else y
        pc = 1 - c if r & 1 else c
        sems = dict(send_sem=send_sems.at[r - 1], recv_sem=recv_sems.at[r - 1], device_id=(px, py, pc), device_id_type=MESH)
        sends.append(pltpu.make_async_remote_copy(src_ref=ins[0], dst_ref=lands[0].at[me], **sems))
        recvs.append(pltpu.make_async_remote_copy(src_ref=ins[0], dst_ref=lands[0].at[4 * px + 2 * py + pc], **sems))
    return sends, recvs


def _swap_copies(ins, lands, send_sems, recv_sems):
    x, y, c = _my_pos()
    cps = [pltpu.make_async_remote_copy(
        src_ref=ins[k], dst_ref=lands[k], send_sem=send_sems.at[k], recv_sem=recv_sems.at[k],
        device_id=(x, y, 1 - c), device_id_type=MESH) for k in range(len(ins))]
    return cps, cps


def _split_start(ins, land_shapes, copies, n_sems, name, after=()):
    n, nl = len(ins), len(land_shapes)
    first_out = n + nl + len(after)

    def body(*refs):
        in_refs, land_refs = refs[:n], refs[n:n + nl]
        send_sems, recv_sems = refs[first_out:first_out + 2]
        token = refs[-1]
        sends, _ = copies(in_refs, land_refs, send_sems, recv_sems)
        for cp in sends:
            cp.start()
        token[...] = jnp.zeros_like(token)

    lands = [pltpu.with_memory_space_constraint(lax.empty(s.shape, s.dtype), pltpu.HBM) for s in land_shapes]
    ins = [pltpu.with_memory_space_constraint(s, pltpu.HBM) for s in ins]
    return pl.pallas_call(
        body, name=name,
        out_shape=(pltpu.SemaphoreType.DMA((n_sems,)), pltpu.SemaphoreType.DMA((n_sems,)),
                   *[pltpu.HBM(s.shape, s.dtype) for s in ins], *[pltpu.HBM(s.shape, s.dtype) for s in lands],
                   jax.ShapeDtypeStruct((8, HEAD), F32)),
        in_specs=[_HBM] * (n + nl) + [pl.BlockSpec(memory_space=pl.ANY)] * len(after),
        out_specs=(_SEM, _SEM, *([_HBM] * (n + nl)), pl.BlockSpec(memory_space=pltpu.VMEM)),
        input_output_aliases={k: 2 + k for k in range(n + nl)},
        compiler_params=pltpu.CompilerParams(has_side_effects=_EFFECT),
    )(*ins, *lands, *after)


def _split_wait(started, n, copies, name, after=()):
    send_sems, recv_sems = started[0], started[1]
    bufs = started[2:-1]
    nb = len(bufs)

    def body(*refs):
        in_refs, land_refs = refs[:n], refs[n:nb]
        sends, recvs = copies(in_refs, land_refs, refs[nb], refs[nb + 1])
        for cp in sends:
            cp.wait_send()
        for cp in recvs:
            cp.wait_recv()

    outs = pl.pallas_call(
        body, name=name,
        out_shape=tuple(pltpu.HBM(s.shape, s.dtype) for s in bufs),
        in_specs=[_HBM] * nb + [_SEM, _SEM] + [pl.BlockSpec(memory_space=pl.ANY)] * len(after),
        out_specs=tuple([_HBM] * nb),
        input_output_aliases={k: k for k in range(nb)},
        compiler_params=pltpu.CompilerParams(has_side_effects=_EFFECT),
    )(*bufs, send_sems, recv_sems, *after)
    return list(outs[:n]), list(outs[n:])


def _fill_own_slot(gathered, shards, pidx, names):
    outs = []
    for g, s, name in zip(gathered, shards, names):
        r, cdim = s.shape
        tr = _row_tile(r)

        def body(p_ref, s_ref, g_ref, o_ref):
            o_ref[...] = s_ref[...]

        outs.append(pl.pallas_call(
            body, name=name,
            grid_spec=pltpu.PrefetchScalarGridSpec(
                num_scalar_prefetch=1, grid=(r // tr,),
                in_specs=[pl.BlockSpec((tr, cdim), lambda i, p: (i, 0)), pl.BlockSpec(memory_space=pl.ANY)],
                out_specs=pl.BlockSpec((None, tr, cdim), lambda i, p: (p[0], i, 0))),
            out_shape=jax.ShapeDtypeStruct(g.shape, g.dtype),
            input_output_aliases={2: 0},
            compiler_params=_cp(("arbitrary",)),
        )(pidx, s, g))
    return outs


def _sum_own_and_peers(own4, slots, pidx, name):
    _, rh, cdim = own4.shape
    tr = _row_tile(rh)

    def body(p_ref, own_ref, s_ref, o_ref):
        acc = own_ref[...].astype(F32)
        for j in range(3):
            acc = acc + s_ref[j].astype(F32)
        o_ref[...] = acc

    return pl.pallas_call(
        body, name=name,
        grid_spec=pltpu.PrefetchScalarGridSpec(
            num_scalar_prefetch=1, grid=(rh // tr,),
            in_specs=[pl.BlockSpec((None, tr, cdim), lambda i, p: (p[0], i, 0)),
                      pl.BlockSpec((3, tr, cdim), lambda i, p: (0, i, 0))],
            out_specs=pl.BlockSpec((tr, cdim), lambda i, p: (i, 0))),
        out_shape=jax.ShapeDtypeStruct((rh, cdim), F32),
        compiler_params=_cp(("arbitrary",)),
    )(pidx, own4, slots)


def _exchange(ins, land_shapes, copies, n_sems, name):
    n, nl = len(ins), len(land_shapes)

    def body(*refs):
        sends, recvs = copies(refs[:n], refs[n:n + nl], refs[n + nl], refs[n + nl + 1])
        for cp in sends:
            cp.start()
        for cp in sends:
            cp.wait_send()
        for cp in recvs:
            cp.wait_recv()

    any_spec = pl.BlockSpec(memory_space=pl.ANY)
    return pl.pallas_call(
        body, name=name,
        out_shape=[jax.ShapeDtypeStruct(s.shape, s.dtype) for s in land_shapes],
        in_specs=[any_spec] * n, out_specs=[any_spec] * nl,
        scratch_shapes=[pltpu.SemaphoreType.DMA((n_sems,)), pltpu.SemaphoreType.DMA((n_sems,))],
    )(*ins)


def _row_tile(r):
    t = min(TR_EW, r)
    while r % t:
        t //= 2
    return t


def _add_own_half(g4, recv, cidx, name):
    _, _, rh, cdim = g4.shape
    tr = _row_tile(rh)

    def body(c_ref, a_ref, b_ref, o_ref):
        o_ref[...] = (a_ref[...] + b_ref[...]).astype(BF16)

    return pl.pallas_call(
        body, name=name,
        grid_spec=pltpu.PrefetchScalarGridSpec(
            num_scalar_prefetch=1, grid=(N_CHIPS, rh // tr),
            in_specs=[pl.BlockSpec((None, None, tr, cdim), lambda q, i, c: (q, c[0], i, 0)),
                      pl.BlockSpec((None, tr, cdim), lambda q, i, c: (q, i, 0))],
            out_specs=pl.BlockSpec((None, tr, cdim), lambda q, i, c: (q, i, 0))),
        out_shape=jax.ShapeDtypeStruct(recv.shape, BF16),
        compiler_params=_cp(("arbitrary", "arbitrary")),
    )(cidx, g4, recv)


def _sum_slots(v, name):
    n, r, cdim = v.shape
    tr = _row_tile(r)

    def body(v_ref, o_ref):
        acc = v_ref[0].astype(F32)
        for k in range(1, n):
            acc = acc + v_ref[k].astype(F32)
        o_ref[...] = acc

    return pl.pallas_call(
        body, name=name, grid=(r // tr,),
        in_specs=[pl.BlockSpec((n, tr, cdim), lambda i: (0, i, 0))],
        out_specs=pl.BlockSpec((tr, cdim), lambda i: (i, 0)),
        out_shape=jax.ShapeDtypeStruct((r, cdim), F32),
        compiler_params=_cp(("arbitrary",)),
    )(v)


def _sum_devices(lands, own, didx, name):
    _, r, cdim = lands.shape
    tr = _row_tile(r)

    def body(d_ref, l_ref, own_ref, o_ref):
        acc = jnp.where(d_ref[0] == 0, own_ref[...], l_ref[0])
        for dv in range(1, N_DEV):
            acc = acc + jnp.where(d_ref[0] == dv, own_ref[...], l_ref[dv])
        o_ref[...] = acc

    return pl.pallas_call(
        body, name=name,
        grid_spec=pltpu.PrefetchScalarGridSpec(
            num_scalar_prefetch=1, grid=(r // tr,),
            in_specs=[pl.BlockSpec((N_DEV, tr, cdim), lambda i, dd: (0, i, 0)), pl.BlockSpec((tr, cdim), lambda i, dd: (i, 0))],
            out_specs=pl.BlockSpec((tr, cdim), lambda i, dd: (i, 0))),
        out_shape=jax.ShapeDtypeStruct((r, cdim), F32),
        compiler_params=_cp(("arbitrary",)),
    )(didx, lands, own)


def _adamw_math(wv, gg, mv, vv):
    nm = ADAM_B1 * mv + (1.0 - ADAM_B1) * gg
    nv = ADAM_B2 * vv + (1.0 - ADAM_B2) * (gg * gg)
    m_hat = nm / (1.0 - ADAM_B1 ** ADAM_STEP)
    v_hat = nv / (1.0 - ADAM_B2 ** ADAM_STEP)
    return -ADAM_LR * (m_hat / (jnp.sqrt(v_hat) + ADAM_EPS) + ADAM_WD * wv), nm, nv


def _adamw_halves(w, mine, theirs, m, v, cidx, name):
    r, cdim = w.shape
    rh = r // 2
    tr = _row_tile(rh)
    nblk = rh // tr

    def body(c_ref, w_ref, a_ref, b_ref, m_ref, v_ref, g_ref, d_ref, nm_ref, nv_ref):
        gg = jnp.where(pl.program_id(0) == c_ref[0], a_ref[...], b_ref[...])
        g_ref[...] = gg
        d_ref[...], nm_ref[...], nv_ref[...] = _adamw_math(w_ref[...], gg, m_ref[...], v_ref[...])

    full = pl.BlockSpec((tr, cdim), lambda hh, i, c: (hh * nblk + i, 0))
    half = pl.BlockSpec((tr, cdim), lambda hh, i, c: (i, 0))
    return pl.pallas_call(
        body, name=name,
        grid_spec=pltpu.PrefetchScalarGridSpec(
            num_scalar_prefetch=1, grid=(2, nblk),
            in_specs=[full, half, half, full, full], out_specs=[full] * 4),
        out_shape=[jax.ShapeDtypeStruct((r, cdim), F32)] * 4,
        compiler_params=_cp(("arbitrary", "arbitrary")),
    )(cidx, w, mine, theirs, m, v)


def _adamw_many(params, name):
    n = len(params)

    def body(*refs):
        ins, outs = refs[:4 * n], refs[4 * n:]
        for k in range(n):
            w_ref, g_ref, m_ref, v_ref = ins[4 * k:4 * k + 4]
            outs[3 * k][...], outs[3 * k + 1][...], outs[3 * k + 2][...] = _adamw_math(
                w_ref[...], g_ref[...], m_ref[...], v_ref[...])

    flat = [a for p in params for a in p]
    res = pl.pallas_call(
        body, name=name,
        out_shape=[jax.ShapeDtypeStruct(p[0].shape, F32) for p in params for _ in range(3)],
        compiler_params=pltpu.CompilerParams(vmem_limit_bytes=VMEM_LIMIT),
    )(*flat)
    return [res[3 * k:3 * k + 3] for k in range(n)]


def _adamw(w, g, m, v, name):
    r, cdim = w.shape
    tr = _row_tile(r) if r % 8 == 0 else r

    def body(w_ref, g_ref, m_ref, v_ref, d_ref, nm_ref, nv_ref):
        d_ref[...], nm_ref[...], nv_ref[...] = _adamw_math(w_ref[...], g_ref[...], m_ref[...], v_ref[...])

    spec = pl.BlockSpec((tr, cdim), lambda i: (i, 0))
    return pl.pallas_call(
        body, name=name, grid=(r // tr,), in_specs=[spec] * 4, out_specs=[spec] * 3,
        out_shape=[jax.ShapeDtypeStruct((r, cdim), F32)] * 3,
        compiler_params=_cp(("arbitrary",)),
    )(w, g, m, v)


def _ada_fwd(c_all, w_ada, b_cols):
    nb, _ = c_all.shape
    n = w_ada.shape[1]

    def body(c_ref, w_ref, b_ref, o_ref):
        cv = c_ref[...]
        o_ref[...] = _mm(cv * _sigmoid(cv), w_ref[...]) + b_ref[...]

    return pl.pallas_call(
        body, name="ada_fwd", out_shape=jax.ShapeDtypeStruct((nb, n), F32),
        compiler_params=pltpu.CompilerParams(vmem_limit_bytes=VMEM_LIMIT),
    )(c_all, w_ada, b_cols)


def _ada_bwd(c_all, dmod_all, dmod_cols):
    d = c_all.shape[1]
    n = dmod_cols.shape[1]

    def body(c_ref, da_ref, dc_ref, gw_ref, gb_ref):
        cv = c_ref[...]
        gw_ref[...] = _mm_tn(cv * _sigmoid(cv), dc_ref[...])
        gb_ref[...] = _colsum(da_ref[...])

    return pl.pallas_call(
        body, name="ada_bwd",
        out_shape=[jax.ShapeDtypeStruct((d, n), F32), jax.ShapeDtypeStruct((1, dmod_all.shape[1]), F32)],
        compiler_params=pltpu.CompilerParams(vmem_limit_bytes=VMEM_LIMIT),
    )(c_all, dmod_all, dmod_cols)


def _proj_fwd(x2, modv, ws, cols, b_in, seq, name, proj_in=None):
    t, d = x2.shape
    n = len(ws)
    ns = ws[0].shape[1]
    tm = min(TM_PROJ, seq)
    tpb = seq // tm
    first = proj_in is None

    def body(c_ref, x_ref, mod_ref, *refs):
        w_refs, b_ref = refs[:n], refs[n]
        outs = refs[n + 1 if first else n + 2:]
        proj_ref, h_s = outs[0], outs[-1]
        s = pl.program_id(1)

        @pl.when(s == 0)
        def _():
            h = (x_ref[...] * (1.0 + mod_ref[1:2, :]) + mod_ref[0:1, :]).astype(BF16)
            h_s[...] = h
            if first:
                outs[1][...] = h

        for k in range(n):
            @pl.when(s == k)
            def _():
                proj_ref[...] = (jnp.dot(h_s[...], w_refs[k][...], preferred_element_type=F32) + b_ref[...]).astype(BF16)

    in_specs = [pl.BlockSpec((tm, d), lambda i, s, c: (i, 0)),
                pl.BlockSpec((None, 8, d), lambda i, s, c: (i // tpb, 0, 0))]
    in_specs += [pl.BlockSpec((d, ns), lambda i, s, c: (0, 0))] * n
    in_specs += [pl.BlockSpec((1, ns), lambda i, s, c: (0, c[s]))]
    out_specs = [pl.BlockSpec((tm, ns), lambda i, s, c: (i, c[s]))]
    out_shape = [jax.ShapeDtypeStruct((t, N_CHIPS * ns), BF16)]
    args = [cols, x2, modv, *ws, b_in]
    aliases = {}
    if first:
        out_specs.append(pl.BlockSpec((tm, d), lambda i, s, c: (i, 0)))
        out_shape.append(jax.ShapeDtypeStruct((t, d), BF16))
    else:
        in_specs.append(_ANY)
        args.append(proj_in)
        aliases = {len(args) - 1: 0}
    return pl.pallas_call(
        body, name=name,
        grid_spec=pltpu.PrefetchScalarGridSpec(
            num_scalar_prefetch=1, grid=(t // tm, n), in_specs=in_specs, out_specs=out_specs,
            scratch_shapes=[pltpu.VMEM((tm, d), BF16)]),
        out_shape=out_shape, input_output_aliases=aliases,
        compiler_params=_cp(("arbitrary", "arbitrary")),
    )(*args)


def _lru_rate(lam_ref):
    nl = -lam_ref[...]
    e = jnp.exp(-jnp.abs(nl))
    u = 1.0 + e
    dlt = u - 1.0
    log1p_e = jnp.where(dlt == 0.0, e, jnp.log(u) * (e / jnp.where(dlt == 0.0, 1.0, dlt)))
    return -LRU_C * (jnp.maximum(nl, 0.0) + log1p_e)


def _lru_gates(xl, wc_ref, bc_ref, wa_ref, ba_ref, wx_ref, bx_ref, lam_ref):
    xc = bc_ref[...] + wc_ref[CONV_WIDTH - 1:CONV_WIDTH, :] * xl
    for k in range(CONV_WIDTH - 1):
        xc = xc + wc_ref[k:k + 1, :] * _shift_down(xl, CONV_WIDTH - 1 - k)
    r = _sigmoid(_mm(xc, wa_ref[...]) + ba_ref[...])
    gi = _sigmoid_t(_mm(xc, wx_ref[...]) + bx_ref[...])
    big_l = _lru_rate(lam_ref)
    la = big_l * r
    a = jnp.exp(la)
    m2 = jnp.tanh(-la) * (a * a + 1.0)
    return xc, r, gi, big_l, a, m2


def _lru_prep(proj, lru_w, nb, seq):
    t = proj.shape[0]
    w = LRU_HEADS * HEAD
    w_conv, b_conv, w_a, b_a, w_x, b_x, lam = lru_w

    def body(x_ref, wc_ref, bc_ref, wa_ref, ba_ref, wx_ref, bx_ref, lam_ref, a_ref, inp_ref, r_ref, gi_ref, xc_ref):
        xc, r, gi, big_l, a, m2 = _lru_gates(x_ref[...].astype(F32), wc_ref, bc_ref, wa_ref, ba_ref, wx_ref, bx_ref, lam_ref)
        a_ref[...] = a
        inp_ref[...] = jnp.sqrt(m2) * (gi * xc)
        r_ref[...] = r.astype(BF16)
        gi_ref[...] = gi.astype(BF16)
        xc_ref[...] = xc.astype(BF16)

    col = lambda b, hd: (0, hd)
    head = lambda b, hd: (hd, 0, 0)
    tok = lambda b, hd: (b, hd)
    return pl.pallas_call(
        body, name="lru_prep", grid=(nb, LRU_HEADS),
        in_specs=[pl.BlockSpec((seq, HEAD), tok),
                  pl.BlockSpec((CONV_WIDTH, HEAD), col), pl.BlockSpec((1, HEAD), col),
                  pl.BlockSpec((None, HEAD, HEAD), head), pl.BlockSpec((1, HEAD), col),
                  pl.BlockSpec((None, HEAD, HEAD), head), pl.BlockSpec((1, HEAD), col),
                  pl.BlockSpec((1, HEAD), col)],
        out_specs=[pl.BlockSpec((seq, HEAD), tok)] * 5,
        out_shape=[jax.ShapeDtypeStruct((t, w), F32)] * 2 + [jax.ShapeDtypeStruct((t, w), BF16)] * 3,
        compiler_params=_cp(("arbitrary", "arbitrary")),
    )(proj, w_conv, b_conv, w_a, b_a, w_x, b_x, lam)


def _scan(a3, b3, reverse, name, out_dtype):
    nb, seq, w = a3.shape
    tc = min(TC_SCAN, seq)
    nchunk = seq // tc
    npair = tc // 16

    def combine(av, bv):
        rows = lax.broadcasted_iota(jnp.int32, av.shape, 0)
        for s in (1, 2, 4):
            if reverse:
                keep = rows < 8 - s
                a_sh, b_sh = pltpu.roll(av, 8 - s, 0), pltpu.roll(bv, 8 - s, 0)
            else:
                keep = rows >= s
                a_sh, b_sh = pltpu.roll(av, s, 0), pltpu.roll(bv, s, 0)
            bv = jnp.where(keep, bv + av * b_sh, bv)
            av = jnp.where(keep, av * a_sh, av)
        return av, bv

    def body(a_ref, b_ref, h_ref, carry):
        @pl.when(pl.program_id(0) == 0)
        def _():
            carry[...] = jnp.zeros_like(carry)

        for b in range(nb):
            def pair(j, hprev):
                jj = npair - 1 - j if reverse else j
                base = pl.multiple_of(jj * 16, 16)
                a16 = a_ref[b, pl.ds(base, 16), :]
                b16 = b_ref[b, pl.ds(base, 16), :].astype(F32)
                outs = [None, None]
                for k in ((1, 0) if reverse else (0, 1)):
                    av, bv = a16[8 * k:8 * k + 8, :], b16[8 * k:8 * k + 8, :]
                    av, bv = combine(av, av * bv if reverse else bv)
                    h = bv + av * hprev
                    outs[k] = h
                    hprev = jnp.broadcast_to(h[0:1, :] if reverse else h[7:8, :], (8, w))
                h_ref[b, pl.ds(base, 16), :] = jnp.concatenate(outs, axis=0).astype(out_dtype)
                return hprev

            carry[b] = lax.fori_loop(0, npair, pair, carry[b])

    imap = (lambda i: (0, nchunk - 1 - i, 0)) if reverse else (lambda i: (0, i, 0))
    spec = pl.BlockSpec((nb, tc, w), imap)
    return pl.pallas_call(
        body, name=name, grid=(nchunk,), in_specs=[spec, spec], out_specs=spec,
        out_shape=jax.ShapeDtypeStruct((nb, seq, w), out_dtype),
        scratch_shapes=[pltpu.VMEM((nb, 8, w), F32)],
        compiler_params=_cp(("arbitrary",)),
    )(a3, b3)


def _sgu_mask():
    ti = lax.broadcasted_iota(jnp.int32, (HEAD, HEAD), 0) // SGU_CHUNK
    si = lax.broadcasted_iota(jnp.int32, (HEAD, HEAD), 1) // SGU_CHUNK
    return si <= ti


def _sgu_specs(tm, d_sgu):
    pw = 256
    first_u = (2 * LRU_HEADS * HEAD) // pw
    n_piece = d_sgu // pw
    specs = [pl.BlockSpec((tm, pw), functools.partial(lambda i, k: (i, k), k=first_u + j)) for j in range(2 * n_piece)]
    return specs, n_piece


def _sgu_fwd(proj, w_sp, b_sp_t, ln_g, ln_b):
    t = proj.shape[0]
    d_sgu = SGU_GROUPS * HEAD
    tm = min(TM_SGU, t)
    nblk = tm // HEAD
    specs, n_piece = _sgu_specs(tm, d_sgu)

    def body(*refs):
        u = jnp.concatenate([r[...] for r in refs[:n_piece]], axis=1).astype(F32)
        v = jnp.concatenate([r[...] for r in refs[n_piece:2 * n_piece]], axis=1).astype(F32)
        w_ref, bt_ref, g_ref, b_ref, y_ref = refs[2 * n_piece:]
        ug = _gelu(u)
        xhat, _ = _ln_stats(_gelu(v))
        vn = (xhat * g_ref[...] + b_ref[...]).astype(BF16)
        mask = _sgu_mask()
        for g in range(SGU_GROUPS):
            wm = jnp.where(mask, w_ref[g], 0.0).astype(BF16)
            cols = slice(g * HEAD, (g + 1) * HEAD)
            for n in range(nblk):
                rows = slice(n * HEAD, (n + 1) * HEAD)
                mixed = jnp.dot(wm, vn[rows, cols], preferred_element_type=F32) + bt_ref[:, g:g + 1]
                y_ref[rows, cols] = (ug[rows, cols] * mixed).astype(BF16)

    full = lambda shape: pl.BlockSpec(shape, lambda i: (0,) * len(shape))
    return pl.pallas_call(
        body, name="sgu_fwd", grid=(t // tm,),
        in_specs=specs + [full(w_sp.shape), full(b_sp_t.shape), full(ln_g.shape), full(ln_b.shape)],
        out_specs=pl.BlockSpec((tm, d_sgu), lambda i: (i, 0)),
        out_shape=jax.ShapeDtypeStruct((t, d_sgu), BF16),
        compiler_params=_cp(("arbitrary",)),
    )(*([proj] * (2 * n_piece)), w_sp, b_sp_t, ln_g, ln_b)


def _mix_fwd(hs, proj, y_sgu, x2, modv, w_o_lru_g, w_o_sgu_g, w_out_g, ln1_g, ln1_b, seq):
    t, d = x2.shape
    w = hs.shape[1]
    d_sgu = y_sgu.shape[1]
    nq, _, ns = w_o_sgu_g.shape
    tm = min(TM_MIX, seq)
    ts = min(TS_MLP, tm)
    tpb = seq // tm

    def body(hs_ref, gl_ref, ys_ref, ga_ref, gb_ref, x_ref, mod_ref, wl_hbm, ws_hbm, wo_hbm, g1_ref, b1_ref,
             yap_ref, ya_ref, yb_ref, mg_ref, mix_ref, x1_ref, wl_ref, ws_ref, wo_ref, sems):
        @pl.when(pl.program_id(0) == 0)
        def _():
            _load_weights((wl_hbm, ws_hbm, wo_hbm), (wl_ref, ws_ref, wo_ref), sems)

        for sub in range(tm // ts):
            rows = slice(sub * ts, (sub + 1) * ts)
            yap = (hs_ref[rows, :].astype(F32) * _gelu(gl_ref[rows, :].astype(F32))).astype(BF16)
            yap_ref[rows, :] = yap
            y_a = jnp.dot(yap, wl_ref[...], preferred_element_type=F32)
            ys = ys_ref[rows, :]
            y_b = jnp.concatenate([jnp.dot(ys, ws_ref[q], preferred_element_type=F32) for q in range(nq)], axis=1)
            ya_ref[rows, :] = y_a.astype(BF16)
            yb_ref[rows, :] = y_b.astype(BF16)
            merged = (_sigmoid_t(ga_ref[rows, :].astype(F32)) * y_a
                      + _sigmoid_t(gb_ref[rows, :].astype(F32)) * y_b).astype(BF16)
            mg_ref[rows, :] = merged
            mix = jnp.dot(merged, wo_ref[...], preferred_element_type=F32)
            mix_ref[rows, :] = mix
            xhat, _ = _ln_stats(ALPHA * x_ref[rows, :] + (1.0 + mod_ref[2:3, :]) * mix)
            x1_ref[rows, :] = xhat * g1_ref[...] + b1_ref[...]

    row = lambda width, col: pl.BlockSpec((tm, width), functools.partial(lambda i, k: (i, k), k=col))
    full = lambda shape: pl.BlockSpec(shape, lambda i: (0,) * len(shape))
    return pl.pallas_call(
        body, name="mix_fwd", grid=(t // tm,),
        in_specs=[row(w, 0), row(w, 1), row(d_sgu, 0), row(d, 4), row(d, 5), row(d, 0),
                  pl.BlockSpec((None, 8, d), lambda i: (i // tpb, 0, 0)),
                  _ANY, _ANY, _ANY, full(ln1_g.shape), full(ln1_b.shape)],
        out_specs=[row(w, 0), row(d, 0), row(d, 0), row(d, 0), row(d, 0), row(d, 0)],
        out_shape=[jax.ShapeDtypeStruct((t, w), BF16), jax.ShapeDtypeStruct((t, d), BF16),
                   jax.ShapeDtypeStruct((t, d), BF16), jax.ShapeDtypeStruct((t, d), BF16),
                   jax.ShapeDtypeStruct((t, d), F32), jax.ShapeDtypeStruct((t, d), F32)],
        scratch_shapes=[pltpu.VMEM(w_o_lru_g.shape, BF16), pltpu.VMEM(w_o_sgu_g.shape, BF16),
                        pltpu.VMEM(w_out_g.shape, BF16), pltpu.SemaphoreType.DMA((3,))],
        compiler_params=_cp(("arbitrary",)),
    )(hs, proj, y_sgu, proj, proj, x2, modv, w_o_lru_g, w_o_sgu_g, w_out_g, ln1_g, ln1_b)


def _mlp_fwd(x1, modv, w_up_g, w_down_g, ln2_g, ln2_b, target, nb, seq):
    t, d = x1.shape
    nq, _, ns = w_up_g.shape
    tm = min(TM_MLP, seq)
    ts = min(TS_MLP, tm)
    tpb = seq // tm

    def body(x1_ref, mod_ref, wu_hbm, wd_hbm, g2_ref, b2_ref, tg_ref,
             rl_ref, act_ref, h2_ref, dz2_ref, df_ref, st_ref, pb_ref, wu_s, wd_s, acc, sems):
        i = pl.program_id(0)

        @pl.when(i == 0)
        def _():
            _load_weights((wu_hbm, wd_hbm), (wu_s, wd_s), sems)
            st_ref[...] = jnp.zeros_like(st_ref)

        @pl.when(i % tpb == 0)
        def _():
            pb_ref[...] = jnp.zeros_like(pb_ref)

        for sub in range(tm // ts):
            rows = slice(sub * ts, (sub + 1) * ts)
            x1v = x1_ref[rows, :]
            h2 = (x1v * (1.0 + mod_ref[4:5, :]) + mod_ref[3:4, :]).astype(BF16)
            h2_ref[rows, :] = h2
            for k in range(nq):
                cols = slice(k * ns, (k + 1) * ns)
                r = jnp.maximum(jnp.dot(h2, wu_s[k], preferred_element_type=F32), 0.0)
                act = (r * r).astype(BF16)
                rl_ref[rows, cols] = r.astype(BF16)
                act_ref[rows, cols] = act
                part = jnp.dot(act, wd_s[cols, :], preferred_element_type=F32)
                if k == 0:
                    acc[sub] = part
                else:
                    acc[sub] += part
            f = acc[sub]
            xhat, rstd = _ln_stats(ALPHA * x1v + (1.0 + mod_ref[5:6, :]) * f)
            y = xhat * g2_ref[...] + b2_ref[...]
            err = y - tg_ref[rows, :]
            dy = err * (1.0 / d)
            dz2 = _ln_bwd(dy * g2_ref[...], xhat, rstd)
            dz2_ref[rows, :] = dz2
            df_ref[rows, :] = ((1.0 + mod_ref[5:6, :]) * dz2).astype(BF16)
            st_ref[0:1, :] += _colsum(dy * xhat)
            st_ref[1:2, :] += _colsum(dy)
            st_ref[2:3, :] += (0.5 / d) * jnp.sum(_colsum(err * err), axis=1, keepdims=True)
            pb_ref[0:1, :] += _colsum(dz2 * f)

    tok = lambda i: (i, 0)
    return pl.pallas_call(
        body, name="mlp_fwd", grid=(t // tm,),
        in_specs=[pl.BlockSpec((tm, d), tok), pl.BlockSpec((None, 8, d), lambda i: (i // tpb, 0, 0)), _ANY, _ANY,
                  pl.BlockSpec((1, d), lambda i: (0, 0)), pl.BlockSpec((1, d), lambda i: (0, 0)),
                  pl.BlockSpec((tm, d), tok)],
        out_specs=[pl.BlockSpec((tm, nq * ns), tok), pl.BlockSpec((tm, nq * ns), tok),
                   pl.BlockSpec((tm, d), tok), pl.BlockSpec((tm, d), tok), pl.BlockSpec((tm, d), tok),
                   pl.BlockSpec((8, d), lambda i: (0, 0)), pl.BlockSpec((None, 8, d), lambda i: (i // tpb, 0, 0))],
        out_shape=[jax.ShapeDtypeStruct((t, nq * ns), BF16), jax.ShapeDtypeStruct((t, nq * ns), BF16),
                   jax.ShapeDtypeStruct((t, d), BF16),
                   jax.ShapeDtypeStruct((t, d), F32), jax.ShapeDtypeStruct((t, d), BF16),
                   jax.ShapeDtypeStruct((8, d), F32), jax.ShapeDtypeStruct((nb, 8, d), F32)],
        scratch_shapes=[pltpu.VMEM(w_up_g.shape, BF16), pltpu.VMEM(w_down_g.shape, BF16),
                        pltpu.VMEM((tm // ts, ts, d), F32), pltpu.SemaphoreType.DMA((2,))],
        compiler_params=_cp(("arbitrary",)),
    )(x1, modv, w_up_g, w_down_g, ln2_g, ln2_b, target)


def _mlp_bwd(df, up, w_down_g, w_up_g, dz2, x2, mix, modv, ln1_g, ln1_b, nb, seq):
    t, d = x2.shape
    nq, _, ns = w_up_g.shape
    tm = min(TM_MLP, seq)
    ts = min(TS_MLP, tm)
    tpb = seq // tm

    def body(df_ref, rl_ref, wd_hbm, wu_hbm, dz2_ref, x_ref, mix_ref, mod_ref, g1_ref, b1_ref,
             dup_ref, dz1_ref, dmix_ref, st_ref, pb_ref, wd_s, wu_s, acc, sems):
        i = pl.program_id(0)

        @pl.when(i == 0)
        def _():
            _load_weights((wd_hbm, wu_hbm), (wd_s, wu_s), sems)
            st_ref[...] = jnp.zeros_like(st_ref)

        @pl.when(i % tpb == 0)
        def _():
            pb_ref[...] = jnp.zeros_like(pb_ref)

        for sub in range(tm // ts):
            rows = slice(sub * ts, (sub + 1) * ts)
            dfv = df_ref[rows, :]
            for k in range(nq):
                cols = slice(k * ns, (k + 1) * ns)
                dup = (_mm_nt(dfv, wd_s[cols, :]) * (2.0 * rl_ref[rows, cols].astype(F32))).astype(BF16)
                dup_ref[rows, cols] = dup
                part = _mm_nt(dup, wu_s[k])
                if k == 0:
                    acc[sub] = part
                else:
                    acc[sub] += part
            dh2 = acc[sub]
            mix = mix_ref[rows, :]
            xhat, rstd = _ln_stats(ALPHA * x_ref[rows, :] + (1.0 + mod_ref[2:3, :]) * mix)
            x1 = xhat * g1_ref[...] + b1_ref[...]
            dx1 = ALPHA * dz2_ref[rows, :] + dh2 * (1.0 + mod_ref[4:5, :])
            dz1 = _ln_bwd(dx1 * g1_ref[...], xhat, rstd)
            dz1_ref[rows, :] = dz1
            dmix_ref[rows, :] = ((1.0 + mod_ref[2:3, :]) * dz1).astype(BF16)
            st_ref[0:1, :] += _colsum(dx1 * xhat)
            st_ref[1:2, :] += _colsum(dx1)
            pb_ref[0:1, :] += _colsum(dh2 * x1)
            pb_ref[1:2, :] += _colsum(dh2)
            pb_ref[2:3, :] += _colsum(dz1 * mix)

    tok = lambda i: (i, 0)
    return pl.pallas_call(
        body, name="mlp_bwd", grid=(t // tm,),
        in_specs=[pl.BlockSpec((tm, d), tok), pl.BlockSpec((tm, nq * ns), tok), _ANY, _ANY,
                  pl.BlockSpec((tm, d), tok), pl.BlockSpec((tm, d), tok), pl.BlockSpec((tm, d), tok),
                  pl.BlockSpec((None, 8, d), lambda i: (i // tpb, 0, 0)),
                  pl.BlockSpec((1, d), lambda i: (0, 0)), pl.BlockSpec((1, d), lambda i: (0, 0))],
        out_specs=[pl.BlockSpec((tm, nq * ns), tok),
                   pl.BlockSpec((tm, d), tok), pl.BlockSpec((tm, d), tok),
                   pl.BlockSpec((8, d), lambda i: (0, 0)), pl.BlockSpec((None, 8, d), lambda i: (i // tpb, 0, 0))],
        out_shape=[jax.ShapeDtypeStruct((t, nq * ns), BF16),
                   jax.ShapeDtypeStruct((t, d), F32), jax.ShapeDtypeStruct((t, d), BF16),
                   jax.ShapeDtypeStruct((8, d), F32), jax.ShapeDtypeStruct((nb, 8, d), F32)],
        scratch_shapes=[pltpu.VMEM(w_down_g.shape, BF16), pltpu.VMEM(w_up_g.shape, BF16),
                        pltpu.VMEM((tm // ts, ts, d), F32), pltpu.SemaphoreType.DMA((2,))],
        compiler_params=_cp(("arbitrary",), VMEM_LIMIT_MAX),
    )(df, up, w_down_g, w_up_g, dz2, x2, mix, modv, ln1_g, ln1_b)


def _mix_bwd(dmix, proj, y_a, y_b, hs, w_out_g, w_o_lru_g, w_o_sgu_g, seq, after=()):
    t, d = dmix.shape
    w = hs.shape[1]
    nq, d_sgu, ns = w_o_sgu_g.shape
    tm = min(TM_MIX, seq)
    ts = min(TS_MLP, tm)

    def body(dmix_ref, ga_ref, gb_ref, ya_ref, yb_ref, gl_ref, hs_ref, wo_hbm, wl_hbm, ws_hbm,
             dya_ref, dyb_ref, dg_ref, dyl_ref, dys_ref, wo_ref, wl_ref, ws_ref, sems):
        @pl.when(pl.program_id(0) == 0)
        def _():
            _load_weights((wo_hbm, wl_hbm, ws_hbm), (wo_ref, wl_ref, ws_ref), sems)

        for sub in range(tm // ts):
            rows = slice(sub * ts, (sub + 1) * ts)
            dmerged = _mm_nt(dmix_ref[rows, :], wo_ref[...])
            sa, sb = _sigmoid_t(ga_ref[rows, :].astype(F32)), _sigmoid_t(gb_ref[rows, :].astype(F32))
            dy_a = (dmerged * sa).astype(BF16)
            dy_b = (dmerged * sb).astype(BF16)
            dya_ref[rows, :] = dy_a
            dyb_ref[rows, :] = dy_b
            dg_ref[rows, 4 * d:5 * d] = (dmerged * ya_ref[rows, :].astype(F32) * (sa * (1.0 - sa))).astype(BF16)
            dg_ref[rows, 5 * d:6 * d] = (dmerged * yb_ref[rows, :].astype(F32) * (sb * (1.0 - sb))).astype(BF16)
            dyap = _mm_nt(dy_a, wl_ref[...])
            gel, dgel = _gelu_and_grad(gl_ref[rows, :].astype(F32))
            dyl_ref[rows, :] = (dyap * gel).astype(BF16)
            dg_ref[rows, w:2 * w] = (dyap * hs_ref[rows, :].astype(F32) * dgel).astype(BF16)
            dys = _mm_nt(dy_b[:, 0:ns], ws_ref[0])
            for q in range(1, nq):
                dys = dys + _mm_nt(dy_b[:, q * ns:(q + 1) * ns], ws_ref[q])
            dys_ref[rows, :] = dys

    row = lambda width, col: pl.BlockSpec((tm, width), functools.partial(lambda i, k: (i, k), k=col))
    return pl.pallas_call(
        _ordered(body, 10, after), name="mix_bwd", grid=(t // tm,),
        in_specs=[row(d, 0), row(d, 4), row(d, 5), row(d, 0), row(d, 0), row(w, 1), row(w, 0),
                  _ANY, _ANY, _ANY] + [_ANY] * len(after),
        scratch_shapes=[pltpu.VMEM(w_out_g.shape, BF16), pltpu.VMEM(w_o_lru_g.shape, BF16),
                        pltpu.VMEM(w_o_sgu_g.shape, BF16), pltpu.SemaphoreType.DMA((3,))],
        out_specs=[row(d, 0), row(d, 0), row(6 * d, 0), row(w, 0), row(d_sgu, 0)],
        out_shape=[jax.ShapeDtypeStruct((t, d), BF16), jax.ShapeDtypeStruct((t, d), BF16),
                   jax.ShapeDtypeStruct((t, 6 * d), BF16), jax.ShapeDtypeStruct((t, w), BF16),
                   jax.ShapeDtypeStruct((t, d_sgu), F32)],
        compiler_params=_cp(("arbitrary",)),
    )(dmix, proj, proj, y_a, y_b, proj, hs, w_out_g, w_o_lru_g, w_o_sgu_g, *after)


def _sgu_bwd(proj, dys, w_sp, b_sp_t, ln_g, ln_b, after=()):
    t = proj.shape[0]
    d_sgu = SGU_GROUPS * HEAD
    tm = min(TM_SGU, t)
    nblk = tm // HEAD
    specs, n_piece = _sgu_specs(tm, d_sgu)

    def body(*refs):
        u = jnp.concatenate([r[...] for r in refs[:n_piece]], axis=1).astype(F32)
        v = jnp.concatenate([r[...] for r in refs[n_piece:2 * n_piece]], axis=1).astype(F32)
        dys_ref, w_ref, bt_ref, g_ref, b_ref, du_ref, dv_ref, dw_ref, st_ref, dbt_ref, dvn_s = refs[2 * n_piece:]

        @pl.when(pl.program_id(0) == 0)
        def _():
            dw_ref[...] = jnp.zeros_like(dw_ref)
            st_ref[...] = jnp.zeros_like(st_ref)
            dbt_ref[...] = jnp.zeros_like(dbt_ref)

        ug, dug_du = _gelu_and_grad(u)
        vg, dvg_dv = _gelu_and_grad(v)
        xhat, rstd = _ln_stats(vg)
        vn = (xhat * g_ref[...] + b_ref[...]).astype(BF16)
        dys_v = dys_ref[...]
        mask = _sgu_mask()
        for g in range(SGU_GROUPS):
            wm = jnp.where(mask, w_ref[g], 0.0).astype(BF16)
            cols = slice(g * HEAD, (g + 1) * HEAD)
            dw_g = jnp.zeros((HEAD, HEAD), F32)
            db_g = jnp.zeros((HEAD, 1), F32)
            for n in range(nblk):
                rows = slice(n * HEAD, (n + 1) * HEAD)
                vn_blk = vn[rows, cols]
                mixed = jnp.dot(wm, vn_blk, preferred_element_type=F32) + bt_ref[:, g:g + 1]
                dy_blk = dys_v[rows, cols]
                du_ref[rows, cols] = (dy_blk * mixed * dug_du[rows, cols]).astype(BF16)
                dmx = dy_blk * ug[rows, cols]
                dvn_s[rows, cols] = _mm_tn(wm, dmx)
                dw_g = dw_g + _mm_nt(dmx, vn_blk)
                db_g = db_g + jnp.sum(dmx, axis=1, keepdims=True)
            dw_ref[g] += jnp.where(mask, dw_g, 0.0)
            dbt_ref[:, g:g + 1] += db_g
        dvn = dvn_s[...]
        st_ref[0:1, :] += _colsum(dvn * xhat)
        st_ref[1:2, :] += _colsum(dvn)
        dv_ref[...] = (_ln_bwd(dvn * g_ref[...], xhat, rstd) * dvg_dv).astype(BF16)

    full = lambda shape: pl.BlockSpec(shape, lambda i: (0,) * len(shape))
    tok = pl.BlockSpec((tm, d_sgu), lambda i: (i, 0))
    return pl.pallas_call(
        _ordered(body, 2 * n_piece + 5, after), name="sgu_bwd", grid=(t // tm,),
        in_specs=specs + [tok, full(w_sp.shape), full(b_sp_t.shape), full(ln_g.shape), full(ln_b.shape)]
        + [_ANY] * len(after),
        out_specs=[tok, tok, full(w_sp.shape), full((8, d_sgu)), full((HEAD, HEAD))],
        out_shape=[jax.ShapeDtypeStruct((t, d_sgu), BF16), jax.ShapeDtypeStruct((t, d_sgu), BF16),
                   jax.ShapeDtypeStruct(w_sp.shape, F32), jax.ShapeDtypeStruct((8, d_sgu), F32),
                   jax.ShapeDtypeStruct((HEAD, HEAD), F32)],
        scratch_shapes=[pltpu.VMEM((tm, d_sgu), F32)],
        compiler_params=_cp(("arbitrary",)),
    )(*([proj] * (2 * n_piece)), dys, w_sp, b_sp_t, ln_g, ln_b, *after)


def _lru_bwd(proj, hs, e, dyl, saved, lru_w, nb, seq, dproj, after=()):
    t = proj.shape[0]
    w = LRU_HEADS * HEAD
    w_conv, b_conv, w_a, b_a, w_x, b_x, lam = lru_w

    def body(x_ref, hs_ref, e_ref, dy_ref, a_ref, r_ref, gi_ref, xc_ref, wc_ref, wa_ref, wx_ref, lam_ref,
             dxl_ref, dwa_ref, dwx_ref, st_ref):
        @pl.when(pl.program_id(1) == 0)
        def _():
            dwa_ref[...] = jnp.zeros_like(dwa_ref)
            dwx_ref[...] = jnp.zeros_like(dwx_ref)
            st_ref[...] = jnp.zeros_like(st_ref)

        xl = x_ref[...].astype(F32)
        a, r, gi, xc = a_ref[...], r_ref[...].astype(F32), gi_ref[...].astype(F32), xc_ref[...].astype(F32)
        big_l = _lru_rate(lam_ref)
        m2 = (1.0 - a) * (1.0 + a)
        inv_mult = lax.rsqrt(m2)
        mult = m2 * inv_mult
        dh = dy_ref[...].astype(F32) + _shift_up(e_ref[...], 1)
        da = dh * _shift_down(hs_ref[...].astype(F32), 1)
        dmult = dh * (gi * xc)
        d_i = dh * (mult * xc)
        dxc = dh * (mult * gi)
        dla = a * (da - dmult * (a * inv_mult))
        dr = dla * big_l
        d_big_l = _colsum(dla * r)
        dra = dr * (r * (1.0 - r))
        dia = d_i * (gi * (1.0 - gi))
        dwa_ref[...] += _mm_tn(xc, dra)
        dwx_ref[...] += _mm_tn(xc, dia)
        dxc = dxc + _mm_nt(dra, wa_ref[...]) + _mm_nt(dia, wx_ref[...])
        dxl = wc_ref[CONV_WIDTH - 1:CONV_WIDTH, :] * dxc
        st_ref[4 + CONV_WIDTH - 1:4 + CONV_WIDTH, :] += _colsum(dxc * xl)
        for k in range(CONV_WIDTH - 1):
            ahead = _shift_up(dxc, CONV_WIDTH - 1 - k)
            dxl = dxl + wc_ref[k:k + 1, :] * ahead
            st_ref[4 + k:5 + k, :] += _colsum(ahead * xl)
        dxl_ref[...] = dxl.astype(BF16)
        st_ref[0:1, :] += _colsum(dra)
        st_ref[1:2, :] += _colsum(dia)
        st_ref[2:3, :] += d_big_l * (LRU_C * _sigmoid(-lam_ref[...]))
        st_ref[3:4, :] += _colsum(dxc)

    col = lambda hd, b: (0, hd)
    head = lambda hd, b: (hd, 0, 0)
    tok = lambda hd, b: (b, hd)
    seq_blk = pl.BlockSpec((seq, HEAD), tok)
    return pl.pallas_call(
        _ordered(body, 12, (dproj,) + tuple(after)), name="lru_bwd", grid=(LRU_HEADS, nb),
        in_specs=[seq_blk] * 8 + [pl.BlockSpec((CONV_WIDTH, HEAD), col), pl.BlockSpec((None, HEAD, HEAD), head),
                                  pl.BlockSpec((None, HEAD, HEAD), head), pl.BlockSpec((1, HEAD), col)]
        + [_ANY] * (1 + len(after)),
        out_specs=[seq_blk, pl.BlockSpec((None, HEAD, HEAD), head), pl.BlockSpec((None, HEAD, HEAD), head),
                   pl.BlockSpec((8, HEAD), col)],
        out_shape=[jax.ShapeDtypeStruct(dproj.shape, BF16), jax.ShapeDtypeStruct((LRU_HEADS, HEAD, HEAD), F32),
                   jax.ShapeDtypeStruct((LRU_HEADS, HEAD, HEAD), F32), jax.ShapeDtypeStruct((8, w), F32)],
        input_output_aliases={12: 0},
        compiler_params=_cp(("arbitrary", "arbitrary")),
    )(proj, hs, e, dyl, *saved, w_conv, w_a, w_x, lam, dproj, *after)


def _weight_grad(a, g, col_shards, name, after=()):
    t, k = a.shape
    n = g.shape[1]
    tt = min(TT_DW, t)
    tk = k if k <= 1536 else 1024
    ns = n // N_CHIPS if col_shards else n
    narrow = col_shards and ns < 512
    tn = n if narrow else min(ns, 768 if ns % 768 == 0 else 1024)
    while ns % tn and not narrow:
        tn //= 2
    per = max(ns // tn, 1)

    def body(a_ref, g_ref, o_ref):
        @pl.when(pl.program_id(2) == 0)
        def _():
            o_ref[...] = jnp.zeros_like(o_ref)

        res = _mm_tn(a_ref[...], g_ref[...])
        if narrow:
            for q in range(N_CHIPS):
                o_ref[q] += res[:, q * ns:(q + 1) * ns]
        else:
            o_ref[...] += res

    if narrow:
        out_spec = pl.BlockSpec((N_CHIPS, tk, ns), lambda i, j, s: (0, i, 0))
        out_shape = jax.ShapeDtypeStruct((N_CHIPS, k, ns), F32)
    elif col_shards:
        out_spec = pl.BlockSpec((None, tk, tn), lambda i, j, s: (j // per, i, j % per))
        out_shape = jax.ShapeDtypeStruct((N_CHIPS, k, ns), F32)
    else:
        out_spec = pl.BlockSpec((tk, tn), lambda i, j, s: (i, j))
        out_shape = jax.ShapeDtypeStruct((k, n), F32)
    return pl.pallas_call(
        _ordered(body, 2, after), name=name, grid=(k // tk, n // tn, t // tt),
        in_specs=[pl.BlockSpec((tt, tk), lambda i, j, s: (s, i)), pl.BlockSpec((tt, tn), lambda i, j, s: (s, j))]
        + [_ANY] * len(after),
        out_specs=out_spec, out_shape=out_shape,
        compiler_params=_cp(("arbitrary", "arbitrary", "arbitrary")),
    )(a, g, *after)


def _input_grad(dproj, ws, slots, dz1, x2, modv, nb, seq, after=()):
    t, d = x2.shape
    nq = len(ws)
    ns = ws[0].shape[1]
    tm = min(TM_DH, seq)
    ts = min(TS_MLP, tm)
    tpb = seq // tm

    def body(slot_ref, dp_ref, *refs):
        w_hbm = refs[:nq]
        dz1_ref, x_ref, mod_ref, gx_ref, db_ref, pb_ref, w_s, acc, sems = refs[nq:]
        i = pl.program_id(0)

        @pl.when(i == 0)
        def _():
            _load_weights(w_hbm, [w_s.at[slot_ref[k]] for k in range(nq)], sems)
            db_ref[...] = jnp.zeros_like(db_ref)

        @pl.when(i % tpb == 0)
        def _():
            pb_ref[...] = jnp.zeros_like(pb_ref)

        for sub in range(tm // ts):
            rows = slice(sub * ts, (sub + 1) * ts)
            for q in range(nq):
                dp = dp_ref[rows, q * ns:(q + 1) * ns]
                part = _mm_nt(dp, w_s[q])
                if q == 0:
                    acc[sub] = part
                else:
                    acc[sub] += part
                db_ref[q, 0:1, :] += _colsum(dp.astype(F32))
            dh = acc[sub]
            gx_ref[rows, :] = ALPHA * dz1_ref[rows, :] + dh * (1.0 + mod_ref[1:2, :])
            pb_ref[0:1, :] += _colsum(dh * x_ref[rows, :])
            pb_ref[1:2, :] += _colsum(dh)

    tok = lambda i, s: (i, 0)
    in_specs = [pl.BlockSpec((tm, nq * ns), tok)] + [_ANY] * nq
    in_specs += [pl.BlockSpec((tm, d), tok), pl.BlockSpec((tm, d), tok),
                 pl.BlockSpec((None, 8, d), lambda i, s: (i // tpb, 0, 0))] + [_ANY] * len(after)
    return pl.pallas_call(
        _ordered(body, 5 + nq, after), name="input_grad",
        grid_spec=pltpu.PrefetchScalarGridSpec(
            num_scalar_prefetch=1, grid=(t // tm,), in_specs=in_specs,
            out_specs=[pl.BlockSpec((tm, d), tok), pl.BlockSpec((nq, 8, ns), lambda i, s: (0, 0, 0)),
                       pl.BlockSpec((None, 8, d), lambda i, s: (i // tpb, 0, 0))],
            scratch_shapes=[pltpu.VMEM((nq, d, ns), BF16), pltpu.VMEM((tm // ts, ts, d), F32),
                            pltpu.SemaphoreType.DMA((nq,))]),
        out_shape=[jax.ShapeDtypeStruct((t, d), F32), jax.ShapeDtypeStruct((nq, 8, ns), F32),
                   jax.ShapeDtypeStruct((nb, 8, d), F32)],
        compiler_params=_cp(("arbitrary",)),
    )(slots, dproj, *ws, dz1, x2, modv, *after)


def _rows128(v):
    flat = v.reshape(-1, HEAD)
    pad = (-flat.shape[0]) % 8
    return jnp.pad(flat, ((0, pad), (0, 0))) if pad else flat


def kernel(x, c, w_ada, b_ada, w_in, b_in, w_conv, b_conv, w_rg_a, b_rg_a, w_rg_x, b_rg_x, lru_lambda, w_sp, b_sp, ln_v_g, ln_v_b, w_o_lru, w_o_sgu, w_out, ln1_g, ln1_b, w_up, w_down, ln2_g, ln2_b, loss_target, m_w_ada, m_b_ada, m_w_in, m_b_in, m_w_conv, m_b_conv, m_w_rg_a, m_b_rg_a, m_w_rg_x, m_b_rg_x, m_lru_lambda, m_w_sp, m_b_sp, m_ln_v_g, m_ln_v_b, m_w_o_lru, m_w_o_sgu, m_w_out, m_ln1_g, m_ln1_b, m_w_up, m_w_down, m_ln2_g, m_ln2_b, v_w_ada, v_b_ada, v_w_in, v_b_in, v_w_conv, v_b_conv, v_w_rg_a, v_b_rg_a, v_w_rg_x, v_b_rg_x, v_lru_lambda, v_w_sp, v_b_sp, v_ln_v_g, v_ln_v_b, v_w_o_lru, v_w_o_sgu, v_w_out, v_ln1_g, v_ln1_b, v_w_up, v_w_down, v_ln2_g, v_ln2_b):
    given = dict(locals())
    nb, seq, d = x.shape
    t = nb * seq
    w_lru = LRU_HEADS * HEAD
    d_sgu = SGU_GROUPS * HEAD
    xi, yi, ci = lax.axis_index("x"), lax.axis_index("y"), lax.axis_index("c")
    chip = 2 * xi + yi
    dev = 2 * chip + ci
    cidx = jnp.reshape(ci, (1,)).astype(jnp.int32)

    x2 = x.reshape(t, d)
    target = loss_target.reshape(t, d)

    big = ["w_in", "w_o_lru", "w_o_sgu", "w_out", "w_up", "w_down"]
    shards_a = [w_in[0].astype(BF16)]
    shards_b = [given[n][0].astype(BF16) for n in big[1:]]
    pidx = jnp.reshape(chip, (1,)).astype(jnp.int32)

    c_rows = _rows128(c)
    wconv_rows = _rows128(w_conv[0])
    slab0 = _all_gather_small(jnp.concatenate([c_rows, wconv_rows], axis=0), "gather_c_wconv")
    slab0 = slab0.reshape(N_DEV, -1, HEAD)
    c_all = slab0[:, :c_rows.shape[0]].reshape(N_DEV * nb, d)
    n_wc = CONV_WIDTH * (w_lru // N_CHIPS) // HEAD
    wc = slab0[0::2, c_rows.shape[0]:c_rows.shape[0] + n_wc].reshape(N_CHIPS, CONV_WIDTH, w_lru // N_CHIPS)
    w_conv_full = jnp.transpose(wc, (1, 0, 2)).reshape(CONV_WIDTH, w_lru)

    n_ada = w_ada.shape[2]
    b_ada_cols = lax.dynamic_slice(b_ada, (0, chip * n_ada), (1, n_ada))
    mod_cols = _ada_fwd(c_all, w_ada[0], b_ada_cols)
    half = (N_DEV * nb) // 2
    mod_half = lax.dynamic_slice(mod_cols, (ci * half, 0), (half, n_ada))
    mod_g = _all_gather_small(mod_half, "gather_mod").reshape(N_CHIPS, 2, half, n_ada)
    mod_all = jnp.transpose(mod_g, (1, 2, 0, 3)).reshape(N_DEV * nb, N_CHIPS * n_ada)
    mod_loc = lax.dynamic_slice(mod_all, (dev * nb, 0), (nb, N_CHIPS * n_ada)).reshape(nb, 6, d)
    modv = jnp.pad(mod_loc, ((0, 0), (0, 2), (0, 0)))

    lru_w = (w_conv_full, b_conv, w_rg_a[0], b_rg_a, w_rg_x[0], b_rg_x, lru_lambda)
    b_sp_t = jnp.transpose(b_sp[0])

    land = lambda s: jax.ShapeDtypeStruct((N_CHIPS,) + s.shape, s.dtype)
    sds = lambda s: jax.ShapeDtypeStruct(s.shape, s.dtype)
    started_a = _split_start(shards_a, [sds(shards_a[0])] * 2, _peer_gather_copies((0, 1)), 2, "gather_w_in_near_start",
                             after=(modv,))
    shards_b, shards_c = shards_b[:3], shards_b[3:]

    ids = lambda *v: jnp.stack(v).astype(jnp.int32)
    modv_t = modv + started_a[-1][0:1, 0:1]
    proj, h = _proj_fwd(x2, modv_t, [started_a[2]], ids(chip), b_in, seq, "proj_fwd_own")
    own_a, lands_a = _split_wait(started_a, 1, _peer_gather_copies((0, 1)), "gather_w_in_near_wait",
                                 after=(proj, *shards_b, *shards_c))
    started_f = _split_start(own_a, [sds(own_a[0])], _far_gather_copies, 1, "gather_w_in_far_start", after=(lands_a[0],))
    started_b = _split_start(shards_b, [land(s) for s in shards_b], _gather_copies, 3 * len(shards_b),
                             "gather_w_mix_start", after=(started_f[-1],))
    started_c = _split_start(shards_c, [land(s) for s in shards_c], _gather_copies, 3 * len(shards_c),
                             "gather_w_mlp_start", after=(started_b[-1],))
    modv_t = modv + started_c[-1][0:1, 0:1]
    (proj,) = _proj_fwd(x2, modv_t, lands_a, ids(chip ^ 1, chip ^ 2), b_in, seq, "proj_fwd_near", proj_in=proj)
    own_a, land_f = _split_wait(started_f, 1, _far_gather_copies, "gather_w_in_far_wait", after=(proj,))
    (proj,) = _proj_fwd(x2, modv, land_f, ids(chip ^ 3), b_in, seq, "proj_fwd_far", proj_in=proj)
    w_in_shards, w_in_chips = own_a + lands_a + land_f, ids(chip, chip ^ 1, chip ^ 2, chip ^ 3)
    a, inp, r16, gi16, xc16 = _lru_prep(proj, lru_w, nb, seq)
    a3 = a.reshape(nb, seq, w_lru)
    hs = _scan(a3, inp.reshape(nb, seq, w_lru), False, "lru_scan", BF16).reshape(t, w_lru)
    y_sgu = _sgu_fwd(proj, w_sp[0], b_sp_t, ln_v_g, ln_v_b)
    shards_b, lands_b = _split_wait(started_b, len(shards_b), _gather_copies, "gather_w_mix_wait", after=(hs, y_sgu))
    w_o_lru_g, w_o_sgu_g, w_out_g = _fill_own_slot(lands_b, shards_b, pidx, ["own_" + n for n in big[1:4]])
    w_o_lru_g = w_o_lru_g.reshape(w_lru, d)
    w_out_g = w_out_g.reshape(d, d)
    yap, y_a, y_b, merged, mix, x1 = _mix_fwd(hs, proj, y_sgu, x2, modv, w_o_lru_g, w_o_sgu_g, w_out_g, ln1_g, ln1_b, seq)
    shards_c, lands_c = _split_wait(started_c, len(shards_c), _gather_copies, "gather_w_mlp_wait", after=(x1,))
    w_up_g, w_down_g = _fill_own_slot(lands_c, shards_c, pidx, ["own_" + n for n in big[4:]])
    w_down_g = w_down_g.reshape(-1, d)
    up, act, h2, dz2, df, st2, pb2 = _mlp_fwd(x1, modv, w_up_g, w_down_g, ln2_g, ln2_b, target, nb, seq)

    part = {}

    def to_sibling_start(group, tag, after=()):
        g4 = []
        for n in group:
            shard = given[n].shape[1:]
            g4.append(part[n].reshape(N_CHIPS, 2, shard[0] // 2, shard[1]))
        shapes = [jax.ShapeDtypeStruct((N_CHIPS,) + g.shape[2:], F32) for g in g4]
        return _split_start(g4, shapes, _to_sibling_copies, len(g4), "grads_to_sibling_start_" + tag, after)

    def to_chips_start(group, started, tag, after=()):
        g4, recv = _split_wait(started, len(group), _to_sibling_copies, "grads_to_sibling_wait_" + tag, after)
        own4 = [_add_own_half(g4[k], recv[k], cidx, "grad_pair_sum_" + n) for k, n in enumerate(group)]
        shapes = [jax.ShapeDtypeStruct((3,) + o.shape[1:], BF16) for o in own4]
        return _split_start(own4, shapes, _chip_exchange_copies, 3 * len(own4), "grads_chip_exchange_start_" + tag)

    def chips_finish(group, started, tag, after=()):
        own4, slots = _split_wait(started, len(group), _chip_exchange_copies, "grads_chip_exchange_wait_" + tag, after)
        return [_sum_own_and_peers(own4[k], slots[k], pidx, "grad_chip_sum_" + n) for k, n in enumerate(group)]

    dup, dz1, dmix, st1, pb1 = _mlp_bwd(df, up, w_down_g, w_up_g, dz2, x2, mix, modv, ln1_g, ln1_b, nb, seq)
    group1 = ["w_up", "w_down"]
    part["w_up"] = _weight_grad(h2, dup, True, "grad_w_up")
    part["w_down"] = _weight_grad(act, df, False, "grad_w_down")
    sib1 = to_sibling_start(group1, "mlp")
    dy_a, dy_b, dproj, dyl, dys = _mix_bwd(dmix, proj, y_a, y_b, hs, w_out_g, w_o_lru_g, w_o_sgu_g, seq,
                                                after=(sib1[-1],))
    group2 = ["w_o_lru", "w_o_sgu", "w_out"]
    part["w_o_lru"] = _weight_grad(yap, dy_a, False, "grad_w_o_lru")
    part["w_o_sgu"] = _weight_grad(y_sgu, dy_b, True, "grad_w_o_sgu")
    part["w_out"] = _weight_grad(merged, dmix, False, "grad_w_out")
    chips1 = to_chips_start(group1, sib1, "mlp", after=(dys, part["w_o_lru"], part["w_o_sgu"], part["w_out"]))
    sib2 = to_sibling_start(group2, "mix", after=(chips1[-1],))
    du, dv, g_w_sp, st_sgu, g_b_sp_t = _sgu_bwd(proj, dys, w_sp[0], b_sp_t, ln_v_g, ln_v_b, after=(sib2[-1],))
    dyl3 = dyl.reshape(nb, seq, w_lru)
    e = _scan(a3, dyl3, True, "lru_scan_bwd", F32).reshape(t, w_lru)
    chips2 = to_chips_start(group2, sib2, "mix", after=(e, du))
    dproj = lax.dynamic_update_slice(dproj, du, (0, 2 * w_lru))
    dproj = lax.dynamic_update_slice(dproj, dv, (0, 2 * w_lru + d_sgu))
    dproj, g_w_rg_a, g_w_rg_x, st_lru = _lru_bwd(proj, hs, e, dyl, (a, r16, gi16, xc16), lru_w, nb, seq, dproj,
                                                 after=(chips2[-1],))

    didx = jnp.reshape(dev, (1,)).astype(jnp.int32)
    early = [
        ("w_conv", st_lru[4:8]), ("b_conv", st_lru[3]), ("w_rg_a", g_w_rg_a), ("b_rg_a", st_lru[0]),
        ("w_rg_x", g_w_rg_x), ("b_rg_x", st_lru[1]), ("lru_lambda", st_lru[2]), ("w_sp", g_w_sp),
        ("b_sp", jnp.transpose(g_b_sp_t[:, :SGU_GROUPS])), ("ln_v_g", st_sgu[0]), ("ln_v_b", st_sgu[1]),
        ("ln1_g", st1[0]), ("ln1_b", st1[1]), ("ln2_g", st2[0]), ("ln2_b", st2[1]),
    ]
    pieces_e = [_rows128(v) for _, v in early]
    slab_e = jnp.concatenate(pieces_e, axis=0)
    slab_e = jnp.pad(slab_e, ((0, (-slab_e.shape[0]) % TR_EW), (0, 0)))
    small_st = _split_start([slab_e], [jax.ShapeDtypeStruct((N_DEV,) + slab_e.shape, F32)], _all_devices_copies, N_DEV - 1,
                            "small_grads_start")

    group3 = ["w_in"]
    part["w_in"] = _weight_grad(h, dproj, True, "grad_w_in", after=(small_st[-1],))
    sib3 = to_sibling_start(group3, "in")
    halves12 = (chips_finish(group1, chips1, "mlp", after=(sib3[-1],))
                + chips_finish(group2, chips2, "mix", after=(sib3[-1],)))
    swap12 = _split_start(halves12, [jax.ShapeDtypeStruct(hv.shape, F32) for hv in halves12], _swap_copies, len(halves12),
                          "grads_swap_start")
    chips3 = to_chips_start(group3, sib3, "in", after=(swap12[-1],))
    grad_x2, g_b_in4, pb0 = _input_grad(dproj, w_in_shards, w_in_chips, dz1, x2, modv, nb, seq, after=(chips3[-1],))
    loss = lax.psum(st2[2, 0] + swap12[-1][0, 0], ("x", "y", "c"))
    grads = {}

    dmod_loc = jnp.stack([pb0[:, 1], pb0[:, 0], pb1[:, 2], pb1[:, 1], pb1[:, 0], pb2[:, 0]], axis=1)
    late = [("dmod", dmod_loc), ("b_in", g_b_in4[:, 0])]
    pieces_l = [_rows128(v) for _, v in late]
    slab_l = jnp.concatenate(pieces_l, axis=0)
    gathered = _all_gather_small(slab_l, "gather_small_grads", after=(swap12[-1],)).reshape(N_DEV, slab_l.shape[0], HEAD)
    rows_dmod = dmod_loc.size // HEAD
    dmod_all = gathered[:, :rows_dmod].reshape(N_DEV * nb, 6 * d)
    grads["b_in"] = _sum_slots(gathered[:, rows_dmod:], "grad_b_in_sum").reshape(1, -1)

    (slab_e,), (lands_e,) = _split_wait(small_st, 1, _all_devices_copies, "small_grads_wait", after=(gathered,))
    summed = _sum_devices(lands_e, slab_e, didx, "small_grad_sum")
    off = 0
    for (n, v), piece in zip(early, pieces_e):
        grads[n] = summed[off:off + v.size // HEAD].reshape(v.shape)
        off += piece.shape[0]

    mine12, theirs12 = _split_wait(swap12, len(halves12), _swap_copies, "grads_swap_wait", after=(summed,))
    (mine3,) = chips_finish(group3, chips3, "in", after=(summed,))
    (theirs3,) = _exchange([mine3], [jax.ShapeDtypeStruct(mine3.shape, F32)], _swap_copies, 1, "grads_swap_w_in")
    mine = dict(zip(group1 + group2 + group3, mine12 + [mine3]))
    theirs = dict(zip(group1 + group2 + group3, theirs12 + [theirs3]))

    dmod_cols = lax.dynamic_slice(dmod_all, (0, chip * n_ada), (N_DEV * nb, n_ada))
    grads["w_ada"], grads["b_ada"] = _ada_bwd(c_all, dmod_all, dmod_cols)
    n_wcs = w_lru // N_CHIPS
    grads["w_conv"] = lax.dynamic_slice(grads["w_conv"], (0, chip * n_wcs), (CONV_WIDTH, n_wcs))

    names = ['w_ada', 'b_ada', 'w_in', 'b_in', 'w_conv', 'b_conv', 'w_rg_a', 'b_rg_a', 'w_rg_x', 'b_rg_x', 'lru_lambda',
             'w_sp', 'b_sp', 'ln_v_g', 'ln_v_b', 'w_o_lru', 'w_o_sgu', 'w_out', 'ln1_g', 'ln1_b', 'w_up', 'w_down',
             'ln2_g', 'ln2_b']
    two_d = lambda v: v.reshape(-1, v.shape[-1])
    done = {}
    small_names = [n for n in names if n not in big and n != "w_ada"]
    small_out = _adamw_many([(two_d(given[n]), two_d(grads[n].reshape(given[n].shape)), two_d(given["m_" + n]),
                              two_d(given["v_" + n])) for n in small_names], "adamw_small")
    for n, res in zip(small_names, small_out):
        done[n] = (grads[n],) + tuple(res)
    for n in big + ["w_ada"]:
        w2, m2, v2 = two_d(given[n]), two_d(given["m_" + n]), two_d(given["v_" + n])
        if n in big:
            done[n] = _adamw_halves(w2, mine[n], theirs[n], m2, v2, cidx, "adamw_" + n)
        else:
            done[n] = (grads[n],) + tuple(_adamw(w2, two_d(grads[n]), m2, v2, "adamw_" + n))
    outs = [[done[n][k].reshape(given[n].shape) for n in names] for k in range(4)]
    return (loss, grad_x2.reshape(nb, seq, d), *outs[0], *outs[1], *outs[2], *outs[3])
```

```python
import functools
import math

import jax
import jax.numpy as jnp
from jax import lax
from jax.experimental import pallas as pl
from jax.experimental.pallas import tpu as pltpu

F32 = jnp.float32
BF16 = jnp.bfloat16
MESH = pl.DeviceIdType.MESH

N_CHIPS = 4
N_DEV = 8
LRU_HEADS = 10
HEAD = 128
SGU_GROUPS = 6
SGU_CHUNK = 64
CONV_WIDTH = 4
LRU_C = 8.0
ALPHA = 2.0 ** 0.25
LN_EPS = 1e-5
ADAM_LR, ADAM_B1, ADAM_B2, ADAM_EPS, ADAM_WD, ADAM_STEP = 0.001, 0.9, 0.999, 1e-08, 0.01, 10

VMEM_LIMIT = 56 * 1024 * 1024
VMEM_LIMIT_MAX = 62 * 1024 * 1024
TM_PROJ = 1024
TM_MIX = 512
TM_MLP = 512
TS_MLP = 256
TM_SGU = 512
TM_DH = 512
TT_DW = 4096
TC_SCAN = 256
TR_EW = 256


def _cp(sem=None, limit=None):
    return pltpu.CompilerParams(dimension_semantics=sem, vmem_limit_bytes=limit or VMEM_LIMIT)


def _mm(a, b):
    return jnp.dot(a.astype(BF16), b.astype(BF16), preferred_element_type=F32)


def _mm_nt(a, b):
    return lax.dot_general(a.astype(BF16), b.astype(BF16), (((1,), (1,)), ((), ())), preferred_element_type=F32)


def _mm_tn(a, b):
    return lax.dot_general(a.astype(BF16), b.astype(BF16), (((0,), (0,)), ((), ())), preferred_element_type=F32)


def _sigmoid(x):
    return 1.0 / (1.0 + jnp.exp(-x))


def _sigmoid_t(x):
    return 0.5 * jnp.tanh(0.5 * x) + 0.5


_GELU_K = math.sqrt(2.0 / math.pi)


def _gelu(x):
    t = jnp.tanh(_GELU_K * (x + 0.044715 * (x * x * x)))
    return 0.5 * x * (1.0 + t)


def _gelu_and_grad(x):
    x2 = x * x
    t = jnp.tanh(_GELU_K * (x + 0.044715 * (x2 * x)))
    g = 0.5 * x * (1.0 + t)
    dg = 0.5 * (1.0 + t) + 0.5 * x * (1.0 - t * t) * (_GELU_K * (1.0 + 3.0 * 0.044715 * x2))
    return g, dg


def _ln_stats(z):
    mu = jnp.mean(z, axis=-1, keepdims=True)
    zc = z - mu
    var = jnp.mean(zc * zc, axis=-1, keepdims=True)
    rstd = lax.rsqrt(var + LN_EPS)
    return zc * rstd, rstd


def _ln_bwd(dxh, xhat, rstd):
    m1 = jnp.mean(dxh, axis=-1, keepdims=True)
    m2 = jnp.mean(dxh * xhat, axis=-1, keepdims=True)
    return rstd * (dxh - m1 - xhat * m2)


def _colsum(v):
    return jnp.sum(v, axis=0, keepdims=True)


def _shift_down(v, j):
    if j == 0:
        return v
    rows = lax.broadcasted_iota(jnp.int32, v.shape, 0)
    return jnp.where(rows >= j, pltpu.roll(v, j, 0), 0.0)


def _shift_up(v, j):
    if j == 0:
        return v
    n = v.shape[0]
    rows = lax.broadcasted_iota(jnp.int32, v.shape, 0)
    return jnp.where(rows < n - j, pltpu.roll(v, n - j, 0), 0.0)


def _load_weights(srcs, dsts, sems):
    cps = [pltpu.make_async_copy(s, dd, sems.at[k]) for k, (s, dd) in enumerate(zip(srcs, dsts))]
    for cp in cps:
        cp.start()
    for cp in cps:
        cp.wait()


def _my_pos():
    return lax.axis_index("x"), lax.axis_index("y"), lax.axis_index("c")


def _all_gather_small(v, name, after=()):
    m_per, n = v.shape

    def body(x_ref, out_ref, send_sems, recv_sems, local_sem):
        x, y, c = _my_pos()
        me, sibling = (x, y, c), (x, y, 1 - c)
        chips = [(1 - x, y), (x, 1 - y), (1 - x, 1 - y)]

        def rows(px, py, pc):
            return out_ref.at[pl.ds((4 * px + 2 * py + pc) * m_per, m_per), :]

        def copy(k, block, to, src=None):
            return pltpu.make_async_remote_copy(
                src_ref=rows(*block) if src is None else src, dst_ref=rows(*block),
                send_sem=send_sems.at[k], recv_sem=recv_sems.at[k], device_id=to, device_id_type=MESH)

        mine = pltpu.make_async_copy(x_ref, rows(*me), local_sem)
        mine.start()
        first = [copy(0, me, sibling, src=x_ref)]
        first += [copy(1 + j, me, (*chip, c), src=x_ref) for j, chip in enumerate(chips)]
        for cp in first:
            cp.start()
        passed = [copy(4 + j, (*chip, c), sibling) for j, chip in enumerate(chips)]
        for j, chip in enumerate(chips):
            copy(1 + j, (*chip, c), me).wait_recv()
            passed[j].start()
        copy(0, sibling, me).wait_recv()
        for j, chip in enumerate(chips):
            copy(4 + j, (*chip, 1 - c), me).wait_recv()
        for cp in first + passed:
            cp.wait_send()
        mine.wait()

    return pl.pallas_call(
        _ordered(body, 1, after), name=name,
        out_shape=jax.ShapeDtypeStruct((N_DEV * m_per, n), v.dtype),
        in_specs=[pl.BlockSpec(memory_space=pltpu.VMEM)] + [pl.BlockSpec(memory_space=pl.ANY)] * len(after),
        out_specs=pl.BlockSpec(memory_space=pltpu.VMEM),
        scratch_shapes=[pltpu.SemaphoreType.DMA((7,)), pltpu.SemaphoreType.DMA((7,)), pltpu.SemaphoreType.DMA],
        compiler_params=pltpu.CompilerParams(vmem_limit_bytes=VMEM_LIMIT),
    )(v, *after)


_HBM = pl.BlockSpec(memory_space=pltpu.HBM)
_ANY = pl.BlockSpec(memory_space=pl.ANY)
_SEM = pl.BlockSpec(memory_space=pltpu.SEMAPHORE)
_EFFECT = pltpu.SideEffectType.DATAFLOW_SIDE_EFFECTING


def _ordered(body, n_in, after):
    k = len(after)
    if not k:
        return body
    return lambda *refs: body(*refs[:n_in], *refs[n_in + k:])


def _gather_copies(ins, lands, send_sems, recv_sems):
    x, y, c = _my_pos()
    p = 2 * x + y
    peers = [(x, 1 - y), (1 - x, y), (1 - x, 1 - y)]
    sends, recvs = [], []
    for k in range(len(ins)):
        for j, (qx, qy) in enumerate(peers):
            sems = dict(send_sem=send_sems.at[3 * k + j], recv_sem=recv_sems.at[3 * k + j],
                        device_id=(qx, qy, c), device_id_type=MESH)
            sends.append(pltpu.make_async_remote_copy(src_ref=ins[k], dst_ref=lands[k].at[p], **sems))
            recvs.append(pltpu.make_async_remote_copy(src_ref=ins[k], dst_ref=lands[k].at[2 * qx + qy], **sems))
    return sends, recvs


def _peer_gather_copies(peers):
    def copies(ins, lands, send_sems, recv_sems):
        x, y, c = _my_pos()
        where = [(x, 1 - y), (1 - x, y), (1 - x, 1 - y)]
        cps = [pltpu.make_async_remote_copy(
            src_ref=ins[0], dst_ref=lands[j], send_sem=send_sems.at[j], recv_sem=recv_sems.at[j],
            device_id=(*where[j], c), device_id_type=MESH) for j in peers]
        return cps, cps
    return copies


def _far_gather_copies(ins, lands, send_sems, recv_sems):
    x, y, c = _my_pos()
    cps = [pltpu.make_async_remote_copy(
        src_ref=ins[0], dst_ref=lands[0], send_sem=send_sems.at[0], recv_sem=recv_sems.at[0],
        device_id=(1 - x, 1 - y, c), device_id_type=MESH)]
    return cps, cps


def _to_sibling_copies(ins, lands, send_sems, recv_sems):
    x, y, c = _my_pos()
    cps = [pltpu.make_async_remote_copy(
        src_ref=ins[k].at[:, 1 - c], dst_ref=lands[k], send_sem=send_sems.at[k], recv_sem=recv_sems.at[k],
        device_id=(x, y, 1 - c), device_id_type=MESH) for k in range(len(ins))]
    return cps, cps


def _chip_exchange_copies(ins, lands, send_sems, recv_sems):
    x, y, c = _my_pos()
    peers = [(x, 1 - y), (1 - x, y), (1 - x, 1 - y)]
    cps = []
    for k in range(len(ins)):
        for j, (qx, qy) in enumerate(peers):
            cps.append(pltpu.make_async_remote_copy(
                src_ref=ins[k].at[2 * qx + qy], dst_ref=lands[k].at[j], send_sem=send_sems.at[3 * k + j],
                recv_sem=recv_sems.at[3 * k + j], device_id=(qx, qy, c), device_id_type=MESH))
    return cps, cps


def _all_devices_copies(ins, lands, send_sems, recv_sems):
    x, y, c = _my_pos()
    me = 4 * x + 2 * y + c
    sends, recvs = [], []
    for r in range(1, N_DEV):
        px = 1 - x if r & 4 else x
        py = 1 - y if r & 2 else y
        pc = 1 - c if r & 1 else c
        sems = dict(send_sem=send_sems.at[r - 1], recv_sem=recv_sems.at[r - 1], device_id=(px, py, pc), device_id_type=MESH)
        sends.append(pltpu.make_async_remote_copy(src_ref=ins[0], dst_ref=lands[0].at[me], **sems))
        recvs.append(pltpu.make_async_remote_copy(src_ref=ins[0], dst_ref=lands[0].at[4 * px + 2 * py + pc], **sems))
    return sends, recvs


def _swap_copies(ins, lands, send_sems, recv_sems):
    x, y, c = _my_pos()
    cps = [pltpu.make_async_remote_copy(
        src_ref=ins[k], dst_ref=lands[k], send_sem=send_sems.at[k], recv_sem=recv_sems.at[k],
        device_id=(x, y, 1 - c), device_id_type=MESH) for k in range(len(ins))]
    return cps, cps


def _split_start(ins, land_shapes, copies, n_sems, name, after=()):
    n, nl = len(ins), len(land_shapes)
    first_out = n + nl + len(after)

    def body(*refs):
        in_refs, land_refs = refs[:n], refs[n:n + nl]
        send_sems, recv_sems = refs[first_out:first_out + 2]
        token = refs[-1]
        sends, _ = copies(in_refs, land_refs, send_sems, recv_sems)
        for cp in sends:
            cp.start()
        token[...] = jnp.zeros_like(token)

    lands = [pltpu.with_memory_space_constraint(lax.empty(s.shape, s.dtype), pltpu.HBM) for s in land_shapes]
    ins = [pltpu.with_memory_space_constraint(s, pltpu.HBM) for s in ins]
    return pl.pallas_call(
        body, name=name,
        out_shape=(pltpu.SemaphoreType.DMA((n_sems,)), pltpu.SemaphoreType.DMA((n_sems,)),
                   *[pltpu.HBM(s.shape, s.dtype) for s in ins], *[pltpu.HBM(s.shape, s.dtype) for s in lands],
                   jax.ShapeDtypeStruct((8, HEAD), F32)),
        in_specs=[_HBM] * (n + nl) + [pl.BlockSpec(memory_space=pl.ANY)] * len(after),
        out_specs=(_SEM, _SEM, *([_HBM] * (n + nl)), pl.BlockSpec(memory_space=pltpu.VMEM)),
        input_output_aliases={k: 2 + k for k in range(n + nl)},
        compiler_params=pltpu.CompilerParams(has_side_effects=_EFFECT),
    )(*ins, *lands, *after)


def _split_wait(started, n, copies, name, after=()):
    send_sems, recv_sems = started[0], started[1]
    bufs = started[2:-1]
    nb = len(bufs)

    def body(*refs):
        in_refs, land_refs = refs[:n], refs[n:nb]
        sends, recvs = copies(in_refs, land_refs, refs[nb], refs[nb + 1])
        for cp in sends:
            cp.wait_send()
        for cp in recvs:
            cp.wait_recv()

    outs = pl.pallas_call(
        body, name=name,
        out_shape=tuple(pltpu.HBM(s.shape, s.dtype) for s in bufs),
        in_specs=[_HBM] * nb + [_SEM, _SEM] + [pl.BlockSpec(memory_space=pl.ANY)] * len(after),
        out_specs=tuple([_HBM] * nb),
        input_output_aliases={k: k for k in range(nb)},
        compiler_params=pltpu.CompilerParams(has_side_effects=_EFFECT),
    )(*bufs, send_sems, recv_sems, *after)
    return list(outs[:n]), list(outs[n:])


def _fill_own_slot(gathered, shards, pidx, names):
    outs = []
    for g, s, name in zip(gathered, shards, names):
        r, cdim = s.shape
        tr = _row_tile(r)

        def body(p_ref, s_ref, g_ref, o_ref):
            o_ref[...] = s_ref[...]

        outs.append(pl.pallas_call(
            body, name=name,
            grid_spec=pltpu.PrefetchScalarGridSpec(
                num_scalar_prefetch=1, grid=(r // tr,),
                in_specs=[pl.BlockSpec((tr, cdim), lambda i, p: (i, 0)), pl.BlockSpec(memory_space=pl.ANY)],
                out_specs=pl.BlockSpec((None, tr, cdim), lambda i, p: (p[0], i, 0))),
            out_shape=jax.ShapeDtypeStruct(g.shape, g.dtype),
            input_output_aliases={2: 0},
            compiler_params=_cp(("arbitrary",)),
        )(pidx, s, g))
    return outs


def _sum_own_and_peers(own4, slots, pidx, name):
    _, rh, cdim = own4.shape
    tr = _row_tile(rh)

    def body(p_ref, own_ref, s_ref, o_ref):
        acc = own_ref[...].astype(F32)
        for j in range(3):
            acc = acc + s_ref[j].astype(F32)
        o_ref[...] = acc

    return pl.pallas_call(
        body, name=name,
        grid_spec=pltpu.PrefetchScalarGridSpec(
            num_scalar_prefetch=1, grid=(rh // tr,),
            in_specs=[pl.BlockSpec((None, tr, cdim), lambda i, p: (p[0], i, 0)),
                      pl.BlockSpec((3, tr, cdim), lambda i, p: (0, i, 0))],
            out_specs=pl.BlockSpec((tr, cdim), lambda i, p: (i, 0))),
        out_shape=jax.ShapeDtypeStruct((rh, cdim), F32),
        compiler_params=_cp(("arbitrary",)),
    )(pidx, own4, slots)


def _exchange(ins, land_shapes, copies, n_sems, name):
    n, nl = len(ins), len(land_shapes)

    def body(*refs):
        sends, recvs = copies(refs[:n], refs[n:n + nl], refs[n + nl], refs[n + nl + 1])
        for cp in sends:
            cp.start()
        for cp in sends:
            cp.wait_send()
        for cp in recvs:
            cp.wait_recv()

    any_spec = pl.BlockSpec(memory_space=pl.ANY)
    return pl.pallas_call(
        body, name=name,
        out_shape=[jax.ShapeDtypeStruct(s.shape, s.dtype) for s in land_shapes],
        in_specs=[any_spec] * n, out_specs=[any_spec] * nl,
        scratch_shapes=[pltpu.SemaphoreType.DMA((n_sems,)), pltpu.SemaphoreType.DMA((n_sems,))],
    )(*ins)


def _row_tile(r):
    t = min(TR_EW, r)
    while r % t:
        t //= 2
    return t


def _add_own_half(g4, recv, cidx, name):
    _, _, rh, cdim = g4.shape
    tr = _row_tile(rh)

    def body(c_ref, a_ref, b_ref, o_ref):
        o_ref[...] = (a_ref[...] + b_ref[...]).astype(BF16)

    return pl.pallas_call(
        body, name=name,
        grid_spec=pltpu.PrefetchScalarGridSpec(
            num_scalar_prefetch=1, grid=(N_CHIPS, rh // tr),
            in_specs=[pl.BlockSpec((None, None, tr, cdim), lambda q, i, c: (q, c[0], i, 0)),
                      pl.BlockSpec((None, tr, cdim), lambda q, i, c: (q, i, 0))],
            out_specs=pl.BlockSpec((None, tr, cdim), lambda q, i, c: (q, i, 0))),
        out_shape=jax.ShapeDtypeStruct(recv.shape, BF16),
        compiler_params=_cp(("arbitrary", "arbitrary")),
    )(cidx, g4, recv)


def _sum_slots(v, name):
    n, r, cdim = v.shape
    tr = _row_tile(r)

    def body(v_ref, o_ref):
        acc = v_ref[0].astype(F32)
        for k in range(1, n):
            acc = acc + v_ref[k].astype(F32)
        o_ref[...] = acc

    return pl.pallas_call(
        body, name=name, grid=(r // tr,),
        in_specs=[pl.BlockSpec((n, tr, cdim), lambda i: (0, i, 0))],
        out_specs=pl.BlockSpec((tr, cdim), lambda i: (i, 0)),
        out_shape=jax.ShapeDtypeStruct((r, cdim), F32),
        compiler_params=_cp(("arbitrary",)),
    )(v)


def _sum_devices(lands, own, didx, name):
    _, r, cdim = lands.shape
    tr = _row_tile(r)

    def body(d_ref, l_ref, own_ref, o_ref):
        acc = jnp.where(d_ref[0] == 0, own_ref[...], l_ref[0])
        for dv in range(1, N_DEV):
            acc = acc + jnp.where(d_ref[0] == dv, own_ref[...], l_ref[dv])
        o_ref[...] = acc

    return pl.pallas_call(
        body, name=name,
        grid_spec=pltpu.PrefetchScalarGridSpec(
            num_scalar_prefetch=1, grid=(r // tr,),
            in_specs=[pl.BlockSpec((N_DEV, tr, cdim), lambda i, dd: (0, i, 0)), pl.BlockSpec((tr, cdim), lambda i, dd: (i, 0))],
            out_specs=pl.BlockSpec((tr, cdim), lambda i, dd: (i, 0))),
        out_shape=jax.ShapeDtypeStruct((r, cdim), F32),
        compiler_params=_cp(("arbitrary",)),
    )(didx, lands, own)


def _adamw_math(wv, gg, mv, vv):
    nm = ADAM_B1 * mv + (1.0 - ADAM_B1) * gg
    nv = ADAM_B2 * vv + (1.0 - ADAM_B2) * (gg * gg)
    m_hat = nm / (1.0 - ADAM_B1 ** ADAM_STEP)
    v_hat = nv / (1.0 - ADAM_B2 ** ADAM_STEP)
    return -ADAM_LR * (m_hat / (jnp.sqrt(v_hat) + ADAM_EPS) + ADAM_WD * wv), nm, nv


def _adamw_halves(w, mine, theirs, m, v, cidx, name):
    r, cdim = w.shape
    rh = r // 2
    tr = _row_tile(rh)
    nblk = rh // tr

    def body(c_ref, w_ref, a_ref, b_ref, m_ref, v_ref, g_ref, d_ref, nm_ref, nv_ref):
        gg = jnp.where(pl.program_id(0) == c_ref[0], a_ref[...], b_ref[...])
        g_ref[...] = gg
        d_ref[...], nm_ref[...], nv_ref[...] = _adamw_math(w_ref[...], gg, m_ref[...], v_ref[...])

    full = pl.BlockSpec((tr, cdim), lambda hh, i, c: (hh * nblk + i, 0))
    half = pl.BlockSpec((tr, cdim), lambda hh, i, c: (i, 0))
    return pl.pallas_call(
        body, name=name,
        grid_spec=pltpu.PrefetchScalarGridSpec(
            num_scalar_prefetch=1, grid=(2, nblk),
            in_specs=[full, half, half, full, full], out_specs=[full] * 4),
        out_shape=[jax.ShapeDtypeStruct((r, cdim), F32)] * 4,
        compiler_params=_cp(("arbitrary", "arbitrary")),
    )(cidx, w, mine, theirs, m, v)


def _adamw_many(params, name):
    n = len(params)

    def body(*refs):
        ins, outs = refs[:4 * n], refs[4 * n:]
        for k in range(n):
            w_ref, g_ref, m_ref, v_ref = ins[4 * k:4 * k + 4]
            outs[3 * k][...], outs[3 * k + 1][...], outs[3 * k + 2][...] = _adamw_math(
                w_ref[...], g_ref[...], m_ref[...], v_ref[...])

    flat = [a for p in params for a in p]
    res = pl.pallas_call(
        body, name=name,
        out_shape=[jax.ShapeDtypeStruct(p[0].shape, F32) for p in params for _ in range(3)],
        compiler_params=pltpu.CompilerParams(vmem_limit_bytes=VMEM_LIMIT),
    )(*flat)
    return [res[3 * k:3 * k + 3] for k in range(n)]


def _adamw(w, g, m, v, name):
    r, cdim = w.shape
    tr = _row_tile(r) if r % 8 == 0 else r

    def body(w_ref, g_ref, m_ref, v_ref, d_ref, nm_ref, nv_ref):
        d_ref[...], nm_ref[...], nv_ref[...] = _adamw_math(w_ref[...], g_ref[...], m_ref[...], v_ref[...])

    spec = pl.BlockSpec((tr, cdim), lambda i: (i, 0))
    return pl.pallas_call(
        body, name=name, grid=(r // tr,), in_specs=[spec] * 4, out_specs=[spec] * 3,
        out_shape=[jax.ShapeDtypeStruct((r, cdim), F32)] * 3,
        compiler_params=_cp(("arbitrary",)),
    )(w, g, m, v)


def _ada_fwd(c_all, w_ada, b_cols):
    nb, _ = c_all.shape
    n = w_ada.shape[1]

    def body(c_ref, w_ref, b_ref, o_ref):
        cv = c_ref[...]
        o_ref[...] = _mm(cv * _sigmoid(cv), w_ref[...]) + b_ref[...]

    return pl.pallas_call(
        body, name="ada_fwd", out_shape=jax.ShapeDtypeStruct((nb, n), F32),
        compiler_params=pltpu.CompilerParams(vmem_limit_bytes=VMEM_LIMIT),
    )(c_all, w_ada, b_cols)


def _ada_bwd(c_all, dmod_all, dmod_cols):
    d = c_all.shape[1]
    n = dmod_cols.shape[1]

    def body(c_ref, da_ref, dc_ref, gw_ref, gb_ref):
        cv = c_ref[...]
        gw_ref[...] = _mm_tn(cv * _sigmoid(cv), dc_ref[...])
        gb_ref[...] = _colsum(da_ref[...])

    return pl.pallas_call(
        body, name="ada_bwd",
        out_shape=[jax.ShapeDtypeStruct((d, n), F32), jax.ShapeDtypeStruct((1, dmod_all.shape[1]), F32)],
        compiler_params=pltpu.CompilerParams(vmem_limit_bytes=VMEM_LIMIT),
    )(c_all, dmod_all, dmod_cols)


def _proj_fwd(x2, modv, ws, cols, b_in, seq, name, proj_in=None):
    t, d = x2.shape
    n = len(ws)
    ns = ws[0].shape[1]
    tm = min(TM_PROJ, seq)
    tpb = seq // tm
    first = proj_in is None

    def body(c_ref, x_ref, mod_ref, *refs):
        w_refs, b_ref = refs[:n], refs[n]
        outs = refs[n + 1 if first else n + 2:]
        proj_ref, h_s = outs[0], outs[-1]
        s = pl.program_id(1)

        @pl.when(s == 0)
        def _():
            h = (x_ref[...] * (1.0 + mod_ref[1:2, :]) + mod_ref[0:1, :]).astype(BF16)
            h_s[...] = h
            if first:
                outs[1][...] = h

        for k in range(n):
            @pl.when(s == k)
            def _():
                proj_ref[...] = (jnp.dot(h_s[...], w_refs[k][...], preferred_element_type=F32) + b_ref[...]).astype(BF16)

    in_specs = [pl.BlockSpec((tm, d), lambda i, s, c: (i, 0)),
                pl.BlockSpec((None, 8, d), lambda i, s, c: (i // tpb, 0, 0))]
    in_specs += [pl.BlockSpec((d, ns), lambda i, s, c: (0, 0))] * n
    in_specs += [pl.BlockSpec((1, ns), lambda i, s, c: (0, c[s]))]
    out_specs = [pl.BlockSpec((tm, ns), lambda i, s, c: (i, c[s]))]
    out_shape = [jax.ShapeDtypeStruct((t, N_CHIPS * ns), BF16)]
    args = [cols, x2, modv, *ws, b_in]
    aliases = {}
    if first:
        out_specs.append(pl.BlockSpec((tm, d), lambda i, s, c: (i, 0)))
        out_shape.append(jax.ShapeDtypeStruct((t, d), BF16))
    else:
        in_specs.append(_ANY)
        args.append(proj_in)
        aliases = {len(args) - 1: 0}
    return pl.pallas_call(
        body, name=name,
        grid_spec=pltpu.PrefetchScalarGridSpec(
            num_scalar_prefetch=1, grid=(t // tm, n), in_specs=in_specs, out_specs=out_specs,
            scratch_shapes=[pltpu.VMEM((tm, d), BF16)]),
        out_shape=out_shape, input_output_aliases=aliases,
        compiler_params=_cp(("arbitrary", "arbitrary")),
    )(*args)


def _lru_rate(lam_ref):
    nl = -lam_ref[...]
    e = jnp.exp(-jnp.abs(nl))
    u = 1.0 + e
    dlt = u - 1.0
    log1p_e = jnp.where(dlt == 0.0, e, jnp.log(u) * (e / jnp.where(dlt == 0.0, 1.0, dlt)))
    return -LRU_C * (jnp.maximum(nl, 0.0) + log1p_e)


def _lru_gates(xl, wc_ref, bc_ref, wa_ref, ba_ref, wx_ref, bx_ref, lam_ref):
    xc = bc_ref[...] + wc_ref[CONV_WIDTH - 1:CONV_WIDTH, :] * xl
    for k in range(CONV_WIDTH - 1):
        xc = xc + wc_ref[k:k + 1, :] * _shift_down(xl, CONV_WIDTH - 1 - k)
    r = _sigmoid(_mm(xc, wa_ref[...]) + ba_ref[...])
    gi = _sigmoid_t(_mm(xc, wx_ref[...]) + bx_ref[...])
    big_l = _lru_rate(lam_ref)
    la = big_l * r
    a = jnp.exp(la)
    m2 = jnp.tanh(-la) * (a * a + 1.0)
    return xc, r, gi, big_l, a, m2


def _lru_prep(proj, lru_w, nb, seq):
    t = proj.shape[0]
    w = LRU_HEADS * HEAD
    w_conv, b_conv, w_a, b_a, w_x, b_x, lam = lru_w

    def body(x_ref, wc_ref, bc_ref, wa_ref, ba_ref, wx_ref, bx_ref, lam_ref, a_ref, inp_ref, r_ref, gi_ref, xc_ref):
        xc, r, gi, big_l, a, m2 = _lru_gates(x_ref[...].astype(F32), wc_ref, bc_ref, wa_ref, ba_ref, wx_ref, bx_ref, lam_ref)
        a_ref[...] = a
        inp_ref[...] = (jnp.sqrt(m2) * (gi * xc)).astype(BF16)
        r_ref[...] = r.astype(BF16)
        gi_ref[...] = gi.astype(BF16)
        xc_ref[...] = xc.astype(BF16)

    col = lambda b, hd: (0, hd)
    head = lambda b, hd: (hd, 0, 0)
    tok = lambda b, hd: (b, hd)
    return pl.pallas_call(
        body, name="lru_prep", grid=(nb, LRU_HEADS),
        in_specs=[pl.BlockSpec((seq, HEAD), tok),
                  pl.BlockSpec((CONV_WIDTH, HEAD), col), pl.BlockSpec((1, HEAD), col),
                  pl.BlockSpec((None, HEAD, HEAD), head), pl.BlockSpec((1, HEAD), col),
                  pl.BlockSpec((None, HEAD, HEAD), head), pl.BlockSpec((1, HEAD), col),
                  pl.BlockSpec((1, HEAD), col)],
        out_specs=[pl.BlockSpec((seq, HEAD), tok)] * 5,
        out_shape=[jax.ShapeDtypeStruct((t, w), F32)] + [jax.ShapeDtypeStruct((t, w), BF16)] * 4,
        compiler_params=_cp(("arbitrary", "arbitrary")),
    )(proj, w_conv, b_conv, w_a, b_a, w_x, b_x, lam)


def _scan(a3, b3, reverse, name, out_dtype):
    nb, seq, w = a3.shape
    tc = min(TC_SCAN, seq)
    nchunk = seq // tc
    npair = tc // 16

    def combine(av, bv):
        rows = lax.broadcasted_iota(jnp.int32, av.shape, 0)
        for s in (1, 2, 4):
            if reverse:
                keep = rows < 8 - s
                a_sh, b_sh = pltpu.roll(av, 8 - s, 0), pltpu.roll(bv, 8 - s, 0)
            else:
                keep = rows >= s
                a_sh, b_sh = pltpu.roll(av, s, 0), pltpu.roll(bv, s, 0)
            bv = jnp.where(keep, bv + av * b_sh, bv)
            av = jnp.where(keep, av * a_sh, av)
        return av, bv

    def body(a_ref, b_ref, h_ref, carry):
        @pl.when(pl.program_id(0) == 0)
        def _():
            carry[...] = jnp.zeros_like(carry)

        for b in range(nb):
            def pair(j, hprev):
                jj = npair - 1 - j if reverse else j
                base = pl.multiple_of(jj * 16, 16)
                a16 = a_ref[b, pl.ds(base, 16), :]
                b16 = b_ref[b, pl.ds(base, 16), :].astype(F32)
                outs = [None, None]
                for k in ((1, 0) if reverse else (0, 1)):
                    av, bv = a16[8 * k:8 * k + 8, :], b16[8 * k:8 * k + 8, :]
                    av, bv = combine(av, av * bv if reverse else bv)
                    h = bv + av * hprev
                    outs[k] = h
                    hprev = jnp.broadcast_to(h[0:1, :] if reverse else h[7:8, :], (8, w))
                h_ref[b, pl.ds(base, 16), :] = jnp.concatenate(outs, axis=0).astype(out_dtype)
                return hprev

            carry[b] = lax.fori_loop(0, npair, pair, carry[b])

    imap = (lambda i: (0, nchunk - 1 - i, 0)) if reverse else (lambda i: (0, i, 0))
    spec = pl.BlockSpec((nb, tc, w), imap)
    return pl.pallas_call(
        body, name=name, grid=(nchunk,), in_specs=[spec, spec], out_specs=spec,
        out_shape=jax.ShapeDtypeStruct((nb, seq, w), out_dtype),
        scratch_shapes=[pltpu.VMEM((nb, 8, w), F32)],
        compiler_params=_cp(("arbitrary",)),
    )(a3, b3)


def _sgu_mask():
    ti = lax.broadcasted_iota(jnp.int32, (HEAD, HEAD), 0) // SGU_CHUNK
    si = lax.broadcasted_iota(jnp.int32, (HEAD, HEAD), 1) // SGU_CHUNK
    return si <= ti


def _sgu_specs(tm, d_sgu):
    pw = 256
    first_u = (2 * LRU_HEADS * HEAD) // pw
    n_piece = d_sgu // pw
    specs = [pl.BlockSpec((tm, pw), functools.partial(lambda i, k: (i, k), k=first_u + j)) for j in range(2 * n_piece)]
    return specs, n_piece


def _sgu_fwd(proj, w_sp, b_sp_t, ln_g, ln_b):
    t = proj.shape[0]
    d_sgu = SGU_GROUPS * HEAD
    tm = min(TM_SGU, t)
    nblk = tm // HEAD
    specs, n_piece = _sgu_specs(tm, d_sgu)

    def body(*refs):
        u = jnp.concatenate([r[...] for r in refs[:n_piece]], axis=1).astype(F32)
        v = jnp.concatenate([r[...] for r in refs[n_piece:2 * n_piece]], axis=1).astype(F32)
        w_ref, bt_ref, g_ref, b_ref, y_ref = refs[2 * n_piece:]
        ug = _gelu(u)
        xhat, _ = _ln_stats(_gelu(v))
        vn = (xhat * g_ref[...] + b_ref[...]).astype(BF16)
        mask = _sgu_mask()
        for g in range(SGU_GROUPS):
            wm = jnp.where(mask, w_ref[g], 0.0).astype(BF16)
            cols = slice(g * HEAD, (g + 1) * HEAD)
            for n in range(nblk):
                rows = slice(n * HEAD, (n + 1) * HEAD)
                mixed = jnp.dot(wm, vn[rows, cols], preferred_element_type=F32) + bt_ref[:, g:g + 1]
                y_ref[rows, cols] = (ug[rows, cols] * mixed).astype(BF16)

    full = lambda shape: pl.BlockSpec(shape, lambda i: (0,) * len(shape))
    return pl.pallas_call(
        body, name="sgu_fwd", grid=(t // tm,),
        in_specs=specs + [full(w_sp.shape), full(b_sp_t.shape), full(ln_g.shape), full(ln_b.shape)],
        out_specs=pl.BlockSpec((tm, d_sgu), lambda i: (i, 0)),
        out_shape=jax.ShapeDtypeStruct((t, d_sgu), BF16),
        compiler_params=_cp(("arbitrary",)),
    )(*([proj] * (2 * n_piece)), w_sp, b_sp_t, ln_g, ln_b)


def _mix_fwd(hs, proj, y_sgu, x2, modv, w_o_lru_g, w_o_sgu_g, w_out_g, ln1_g, ln1_b, seq):
    t, d = x2.shape
    w = hs.shape[1]
    d_sgu = y_sgu.shape[1]
    nq, _, ns = w_o_sgu_g.shape
    tm = min(TM_MIX, seq)
    ts = min(TS_MLP, tm)
    tpb = seq // tm

    def body(hs_ref, gl_ref, ys_ref, ga_ref, gb_ref, x_ref, mod_ref, wl_hbm, ws_hbm, wo_hbm, g1_ref, b1_ref,
             yap_ref, ya_ref, yb_ref, mg_ref, mix_ref, x1_ref, wl_ref, ws_ref, wo_ref, sems):
        @pl.when(pl.program_id(0) == 0)
        def _():
            _load_weights((wl_hbm, ws_hbm, wo_hbm), (wl_ref, ws_ref, wo_ref), sems)

        for sub in range(tm // ts):
            rows = slice(sub * ts, (sub + 1) * ts)
            yap = (hs_ref[rows, :].astype(F32) * _gelu(gl_ref[rows, :].astype(F32))).astype(BF16)
            yap_ref[rows, :] = yap
            y_a = jnp.dot(yap, wl_ref[...], preferred_element_type=F32)
            ys = ys_ref[rows, :]
            y_b = jnp.concatenate([jnp.dot(ys, ws_ref[q], preferred_element_type=F32) for q in range(nq)], axis=1)
            ya_ref[rows, :] = y_a.astype(BF16)
            yb_ref[rows, :] = y_b.astype(BF16)
            merged = (_sigmoid_t(ga_ref[rows, :].astype(F32)) * y_a
                      + _sigmoid_t(gb_ref[rows, :].astype(F32)) * y_b).astype(BF16)
            mg_ref[rows, :] = merged
            mix = jnp.dot(merged, wo_ref[...], preferred_element_type=F32)
            mix_ref[rows, :] = mix
            xhat, _ = _ln_stats(ALPHA * x_ref[rows, :] + (1.0 + mod_ref[2:3, :]) * mix)
            x1_ref[rows, :] = xhat * g1_ref[...] + b1_ref[...]

    row = lambda width, col: pl.BlockSpec((tm, width), functools.partial(lambda i, k: (i, k), k=col))
    full = lambda shape: pl.BlockSpec(shape, lambda i: (0,) * len(shape))
    return pl.pallas_call(
        body, name="mix_fwd", grid=(t // tm,),
        in_specs=[row(w, 0), row(w, 1), row(d_sgu, 0), row(d, 4), row(d, 5), row(d, 0),
                  pl.BlockSpec((None, 8, d), lambda i: (i // tpb, 0, 0)),
                  _ANY, _ANY, _ANY, full(ln1_g.shape), full(ln1_b.shape)],
        out_specs=[row(w, 0), row(d, 0), row(d, 0), row(d, 0), row(d, 0), row(d, 0)],
        out_shape=[jax.ShapeDtypeStruct((t, w), BF16), jax.ShapeDtypeStruct((t, d), BF16),
                   jax.ShapeDtypeStruct((t, d), BF16), jax.ShapeDtypeStruct((t, d), BF16),
                   jax.ShapeDtypeStruct((t, d), F32), jax.ShapeDtypeStruct((t, d), F32)],
        scratch_shapes=[pltpu.VMEM(w_o_lru_g.shape, BF16), pltpu.VMEM(w_o_sgu_g.shape, BF16),
                        pltpu.VMEM(w_out_g.shape, BF16), pltpu.SemaphoreType.DMA((3,))],
        compiler_params=_cp(("arbitrary",)),
    )(hs, proj, y_sgu, proj, proj, x2, modv, w_o_lru_g, w_o_sgu_g, w_out_g, ln1_g, ln1_b)


def _mlp_fwd(x1, modv, w_up_g, w_down_g, ln2_g, ln2_b, target, nb, seq):
    t, d = x1.shape
    nq, _, ns = w_up_g.shape
    tm = min(TM_MLP, seq)
    ts = min(TS_MLP, tm)
    tpb = seq // tm

    def body(x1_ref, mod_ref, wu_hbm, wd_hbm, g2_ref, b2_ref, tg_ref,
             rl_ref, act_ref, h2_ref, dz2_ref, df_ref, st_ref, pb_ref, wu_s, wd_s, acc, sems):
        i = pl.program_id(0)

        @pl.when(i == 0)
        def _():
            _load_weights((wu_hbm, wd_hbm), (wu_s, wd_s), sems)
            st_ref[...] = jnp.zeros_like(st_ref)

        @pl.when(i % tpb == 0)
        def _():
            pb_ref[...] = jnp.zeros_like(pb_ref)

        for sub in range(tm // ts):
            rows = slice(sub * ts, (sub + 1) * ts)
            x1v = x1_ref[rows, :]
            h2 = (x1v * (1.0 + mod_ref[4:5, :]) + mod_ref[3:4, :]).astype(BF16)
            h2_ref[rows, :] = h2
            for k in range(nq):
                cols = slice(k * ns, (k + 1) * ns)
                r = jnp.maximum(jnp.dot(h2, wu_s[k], preferred_element_type=F32), 0.0)
                act = (r * r).astype(BF16)
                rl_ref[rows, cols] = r.astype(BF16)
                act_ref[rows, cols] = act
                part = jnp.dot(act, wd_s[cols, :], preferred_element_type=F32)
                if k == 0:
                    acc[sub] = part
                else:
                    acc[sub] += part
            f = acc[sub]
            xhat, rstd = _ln_stats(ALPHA * x1v + (1.0 + mod_ref[5:6, :]) * f)
            y = xhat * g2_ref[...] + b2_ref[...]
            err = y - tg_ref[rows, :]
            dy = err * (1.0 / d)
            dz2 = _ln_bwd(dy * g2_ref[...], xhat, rstd)
            dz2_ref[rows, :] = dz2
            df_ref[rows, :] = ((1.0 + mod_ref[5:6, :]) * dz2).astype(BF16)
            st_ref[0:1, :] += _colsum(dy * xhat)
            st_ref[1:2, :] += _colsum(dy)
            st_ref[2:3, :] += (0.5 / d) * jnp.sum(_colsum(err * err), axis=1, keepdims=True)
            pb_ref[0:1, :] += _colsum(dz2 * f)

    tok = lambda i: (i, 0)
    return pl.pallas_call(
        body, name="mlp_fwd", grid=(t // tm,),
        in_specs=[pl.BlockSpec((tm, d), tok), pl.BlockSpec((None, 8, d), lambda i: (i // tpb, 0, 0)), _ANY, _ANY,
                  pl.BlockSpec((1, d), lambda i: (0, 0)), pl.BlockSpec((1, d), lambda i: (0, 0)),
                  pl.BlockSpec((tm, d), tok)],
        out_specs=[pl.BlockSpec((tm, nq * ns), tok), pl.BlockSpec((tm, nq * ns), tok),
                   pl.BlockSpec((tm, d), tok), pl.BlockSpec((tm, d), tok), pl.BlockSpec((tm, d), tok),
                   pl.BlockSpec((8, d), lambda i: (0, 0)), pl.BlockSpec((None, 8, d), lambda i: (i // tpb, 0, 0))],
        out_shape=[jax.ShapeDtypeStruct((t, nq * ns), BF16), jax.ShapeDtypeStruct((t, nq * ns), BF16),
                   jax.ShapeDtypeStruct((t, d), BF16),
                   jax.ShapeDtypeStruct((t, d), F32), jax.ShapeDtypeStruct((t, d), BF16),
                   jax.ShapeDtypeStruct((8, d), F32), jax.ShapeDtypeStruct((nb, 8, d), F32)],
        scratch_shapes=[pltpu.VMEM(w_up_g.shape, BF16), pltpu.VMEM(w_down_g.shape, BF16),
                        pltpu.VMEM((tm // ts, ts, d), F32), pltpu.SemaphoreType.DMA((2,))],
        compiler_params=_cp(("arbitrary",)),
    )(x1, modv, w_up_g, w_down_g, ln2_g, ln2_b, target)


def _mlp_bwd(df, up, w_down_g, w_up_g, dz2, x2, mix, modv, ln1_g, ln1_b, nb, seq):
    t, d = x2.shape
    nq, _, ns = w_up_g.shape
    tm = min(TM_MLP, seq)
    ts = min(TS_MLP, tm)
    tpb = seq // tm

    def body(df_ref, rl_ref, wd_hbm, wu_hbm, dz2_ref, x_ref, mix_ref, mod_ref, g1_ref, b1_ref,
             dup_ref, dz1_ref, dmix_ref, st_ref, pb_ref, wd_s, wu_s, acc, sems):
        i = pl.program_id(0)

        @pl.when(i == 0)
        def _():
            _load_weights((wd_hbm, wu_hbm), (wd_s, wu_s), sems)
            st_ref[...] = jnp.zeros_like(st_ref)

        @pl.when(i % tpb == 0)
        def _():
            pb_ref[...] = jnp.zeros_like(pb_ref)

        for sub in range(tm // ts):
            rows = slice(sub * ts, (sub + 1) * ts)
            dfv = df_ref[rows, :]
            for k in range(nq):
                cols = slice(k * ns, (k + 1) * ns)
                dup = (_mm_nt(dfv, wd_s[cols, :]) * (2.0 * rl_ref[rows, cols].astype(F32))).astype(BF16)
                dup_ref[rows, cols] = dup
                part = _mm_nt(dup, wu_s[k])
                if k == 0:
                    acc[sub] = part
                else:
                    acc[sub] += part
            dh2 = acc[sub]
            mix = mix_ref[rows, :]
            xhat, rstd = _ln_stats(ALPHA * x_ref[rows, :] + (1.0 + mod_ref[2:3, :]) * mix)
            x1 = xhat * g1_ref[...] + b1_ref[...]
            dx1 = ALPHA * dz2_ref[rows, :] + dh2 * (1.0 + mod_ref[4:5, :])
            dz1 = _ln_bwd(dx1 * g1_ref[...], xhat, rstd)
            dz1_ref[rows, :] = dz1
            dmix_ref[rows, :] = ((1.0 + mod_ref[2:3, :]) * dz1).astype(BF16)
            st_ref[0:1, :] += _colsum(dx1 * xhat)
            st_ref[1:2, :] += _colsum(dx1)
            pb_ref[0:1, :] += _colsum(dh2 * x1)
            pb_ref[1:2, :] += _colsum(dh2)
            pb_ref[2:3, :] += _colsum(dz1 * mix)

    tok = lambda i: (i, 0)
    return pl.pallas_call(
        body, name="mlp_bwd", grid=(t // tm,),
        in_specs=[pl.BlockSpec((tm, d), tok), pl.BlockSpec((tm, nq * ns), tok), _ANY, _ANY,
                  pl.BlockSpec((tm, d), tok), pl.BlockSpec((tm, d), tok), pl.BlockSpec((tm, d), tok),
                  pl.BlockSpec((None, 8, d), lambda i: (i // tpb, 0, 0)),
                  pl.BlockSpec((1, d), lambda i: (0, 0)), pl.BlockSpec((1, d), lambda i: (0, 0))],
        out_specs=[pl.BlockSpec((tm, nq * ns), tok),
                   pl.BlockSpec((tm, d), tok), pl.BlockSpec((tm, d), tok),
                   pl.BlockSpec((8, d), lambda i: (0, 0)), pl.BlockSpec((None, 8, d), lambda i: (i // tpb, 0, 0))],
        out_shape=[jax.ShapeDtypeStruct((t, nq * ns), BF16),
                   jax.ShapeDtypeStruct((t, d), F32), jax.ShapeDtypeStruct((t, d), BF16),
                   jax.ShapeDtypeStruct((8, d), F32), jax.ShapeDtypeStruct((nb, 8, d), F32)],
        scratch_shapes=[pltpu.VMEM(w_down_g.shape, BF16), pltpu.VMEM(w_up_g.shape, BF16),
                        pltpu.VMEM((tm // ts, ts, d), F32), pltpu.SemaphoreType.DMA((2,))],
        compiler_params=_cp(("arbitrary",), VMEM_LIMIT_MAX),
    )(df, up, w_down_g, w_up_g, dz2, x2, mix, modv, ln1_g, ln1_b)


def _mix_bwd(dmix, proj, y_a, y_b, hs, w_out_g, w_o_lru_g, w_o_sgu_g, seq, after=()):
    t, d = dmix.shape
    w = hs.shape[1]
    nq, d_sgu, ns = w_o_sgu_g.shape
    tm = min(TM_MIX, seq)
    ts = min(TS_MLP, tm)

    def body(dmix_ref, ga_ref, gb_ref, ya_ref, yb_ref, gl_ref, hs_ref, wo_hbm, wl_hbm, ws_hbm,
             dya_ref, dyb_ref, dg_ref, dyl_ref, dys_ref, wo_ref, wl_ref, ws_ref, sems):
        @pl.when(pl.program_id(0) == 0)
        def _():
            _load_weights((wo_hbm, wl_hbm, ws_hbm), (wo_ref, wl_ref, ws_ref), sems)

        for sub in range(tm // ts):
            rows = slice(sub * ts, (sub + 1) * ts)
            dmerged = _mm_nt(dmix_ref[rows, :], wo_ref[...])
            sa, sb = _sigmoid_t(ga_ref[rows, :].astype(F32)), _sigmoid_t(gb_ref[rows, :].astype(F32))
            dy_a = (dmerged * sa).astype(BF16)
            dy_b = (dmerged * sb).astype(BF16)
            dya_ref[rows, :] = dy_a
            dyb_ref[rows, :] = dy_b
            dg_ref[rows, 4 * d:5 * d] = (dmerged * ya_ref[rows, :].astype(F32) * (sa * (1.0 - sa))).astype(BF16)
            dg_ref[rows, 5 * d:6 * d] = (dmerged * yb_ref[rows, :].astype(F32) * (sb * (1.0 - sb))).astype(BF16)
            dyap = _mm_nt(dy_a, wl_ref[...])
            gel, dgel = _gelu_and_grad(gl_ref[rows, :].astype(F32))
            dyl_ref[rows, :] = (dyap * gel).astype(BF16)
            dg_ref[rows, w:2 * w] = (dyap * hs_ref[rows, :].astype(F32) * dgel).astype(BF16)
            dys = _mm_nt(dy_b[:, 0:ns], ws_ref[0])
            for q in range(1, nq):
                dys = dys + _mm_nt(dy_b[:, q * ns:(q + 1) * ns], ws_ref[q])
            dys_ref[rows, :] = dys

    row = lambda width, col: pl.BlockSpec((tm, width), functools.partial(lambda i, k: (i, k), k=col))
    return pl.pallas_call(
        _ordered(body, 10, after), name="mix_bwd", grid=(t // tm,),
        in_specs=[row(d, 0), row(d, 4), row(d, 5), row(d, 0), row(d, 0), row(w, 1), row(w, 0),
                  _ANY, _ANY, _ANY] + [_ANY] * len(after),
        scratch_shapes=[pltpu.VMEM(w_out_g.shape, BF16), pltpu.VMEM(w_o_lru_g.shape, BF16),
                        pltpu.VMEM(w_o_sgu_g.shape, BF16), pltpu.SemaphoreType.DMA((3,))],
        out_specs=[row(d, 0), row(d, 0), row(6 * d, 0), row(w, 0), row(d_sgu, 0)],
        out_shape=[jax.ShapeDtypeStruct((t, d), BF16), jax.ShapeDtypeStruct((t, d), BF16),
                   jax.ShapeDtypeStruct((t, 6 * d), BF16), jax.ShapeDtypeStruct((t, w), BF16),
                   jax.ShapeDtypeStruct((t, d_sgu), F32)],
        compiler_params=_cp(("arbitrary",)),
    )(dmix, proj, proj, y_a, y_b, proj, hs, w_out_g, w_o_lru_g, w_o_sgu_g, *after)


def _sgu_bwd(proj, dys, w_sp, b_sp_t, ln_g, ln_b, after=()):
    t = proj.shape[0]
    d_sgu = SGU_GROUPS * HEAD
    tm = min(TM_SGU, t)
    nblk = tm // HEAD
    specs, n_piece = _sgu_specs(tm, d_sgu)

    def body(*refs):
        u = jnp.concatenate([r[...] for r in refs[:n_piece]], axis=1).astype(F32)
        v = jnp.concatenate([r[...] for r in refs[n_piece:2 * n_piece]], axis=1).astype(F32)
        dys_ref, w_ref, bt_ref, g_ref, b_ref, du_ref, dv_ref, dw_ref, st_ref, dbt_ref, dvn_s = refs[2 * n_piece:]

        @pl.when(pl.program_id(0) == 0)
        def _():
            dw_ref[...] = jnp.zeros_like(dw_ref)
            st_ref[...] = jnp.zeros_like(st_ref)
            dbt_ref[...] = jnp.zeros_like(dbt_ref)

        ug, dug_du = _gelu_and_grad(u)
        vg, dvg_dv = _gelu_and_grad(v)
        xhat, rstd = _ln_stats(vg)
        vn = (xhat * g_ref[...] + b_ref[...]).astype(BF16)
        dys_v = dys_ref[...]
        mask = _sgu_mask()
        for g in range(SGU_GROUPS):
            wm = jnp.where(mask, w_ref[g], 0.0).astype(BF16)
            cols = slice(g * HEAD, (g + 1) * HEAD)
            dw_g = jnp.zeros((HEAD, HEAD), F32)
            db_g = jnp.zeros((HEAD, 1), F32)
            for n in range(nblk):
                rows = slice(n * HEAD, (n + 1) * HEAD)
                vn_blk = vn[rows, cols]
                mixed = jnp.dot(wm, vn_blk, preferred_element_type=F32) + bt_ref[:, g:g + 1]
                dy_blk = dys_v[rows, cols]
                du_ref[rows, cols] = (dy_blk * mixed * dug_du[rows, cols]).astype(BF16)
                dmx = dy_blk * ug[rows, cols]
                dvn_s[rows, cols] = _mm_tn(wm, dmx)
                dw_g = dw_g + _mm_nt(dmx, vn_blk)
                db_g = db_g + jnp.sum(dmx, axis=1, keepdims=True)
            dw_ref[g] += jnp.where(mask, dw_g, 0.0)
            dbt_ref[:, g:g + 1] += db_g
        dvn = dvn_s[...]
        st_ref[0:1, :] += _colsum(dvn * xhat)
        st_ref[1:2, :] += _colsum(dvn)
        dv_ref[...] = (_ln_bwd(dvn * g_ref[...], xhat, rstd) * dvg_dv).astype(BF16)

    full = lambda shape: pl.BlockSpec(shape, lambda i: (0,) * len(shape))
    tok = pl.BlockSpec((tm, d_sgu), lambda i: (i, 0))
    return pl.pallas_call(
        _ordered(body, 2 * n_piece + 5, after), name="sgu_bwd", grid=(t // tm,),
        in_specs=specs + [tok, full(w_sp.shape), full(b_sp_t.shape), full(ln_g.shape), full(ln_b.shape)]
        + [_ANY] * len(after),
        out_specs=[tok, tok, full(w_sp.shape), full((8, d_sgu)), full((HEAD, HEAD))],
        out_shape=[jax.ShapeDtypeStruct((t, d_sgu), BF16), jax.ShapeDtypeStruct((t, d_sgu), BF16),
                   jax.ShapeDtypeStruct(w_sp.shape, F32), jax.ShapeDtypeStruct((8, d_sgu), F32),
                   jax.ShapeDtypeStruct((HEAD, HEAD), F32)],
        scratch_shapes=[pltpu.VMEM((tm, d_sgu), F32)],
        compiler_params=_cp(("arbitrary",)),
    )(*([proj] * (2 * n_piece)), dys, w_sp, b_sp_t, ln_g, ln_b, *after)


def _lru_bwd(proj, hs, e, dyl, saved, lru_w, nb, seq, dproj, after=()):
    t = proj.shape[0]
    w = LRU_HEADS * HEAD
    w_conv, b_conv, w_a, b_a, w_x, b_x, lam = lru_w

    def body(x_ref, hs_ref, e_ref, dy_ref, a_ref, r_ref, gi_ref, xc_ref, wc_ref, wa_ref, wx_ref, lam_ref,
             dxl_ref, dwa_ref, dwx_ref, st_ref):
        @pl.when(pl.program_id(1) == 0)
        def _():
            dwa_ref[...] = jnp.zeros_like(dwa_ref)
            dwx_ref[...] = jnp.zeros_like(dwx_ref)
            st_ref[...] = jnp.zeros_like(st_ref)

        xl = x_ref[...].astype(F32)
        a, r, gi, xc = a_ref[...], r_ref[...].astype(F32), gi_ref[...].astype(F32), xc_ref[...].astype(F32)
        big_l = _lru_rate(lam_ref)
        m2 = (1.0 - a) * (1.0 + a)
        inv_mult = lax.rsqrt(m2)
        mult = m2 * inv_mult
        dh = dy_ref[...].astype(F32) + _shift_up(e_ref[...].astype(F32), 1)
        da = dh * _shift_down(hs_ref[...].astype(F32), 1)
        dmult = dh * (gi * xc)
        d_i = dh * (mult * xc)
        dxc = dh * (mult * gi)
        dla = a * (da - dmult * (a * inv_mult))
        dr = dla * big_l
        d_big_l = _colsum(dla * r)
        dra = dr * (r * (1.0 - r))
        dia = d_i * (gi * (1.0 - gi))
        dwa_ref[...] += _mm_tn(xc, dra)
        dwx_ref[...] += _mm_tn(xc, dia)
        dxc = dxc + _mm_nt(dra, wa_ref[...]) + _mm_nt(dia, wx_ref[...])
        dxl = wc_ref[CONV_WIDTH - 1:CONV_WIDTH, :] * dxc
        st_ref[4 + CONV_WIDTH - 1:4 + CONV_WIDTH, :] += _colsum(dxc * xl)
        for k in range(CONV_WIDTH - 1):
            ahead = _shift_up(dxc, CONV_WIDTH - 1 - k)
            dxl = dxl + wc_ref[k:k + 1, :] * ahead
            st_ref[4 + k:5 + k, :] += _colsum(ahead * xl)
        dxl_ref[...] = dxl.astype(BF16)
        st_ref[0:1, :] += _colsum(dra)
        st_ref[1:2, :] += _colsum(dia)
        st_ref[2:3, :] += d_big_l * (LRU_C * _sigmoid(-lam_ref[...]))
        st_ref[3:4, :] += _colsum(dxc)

    col = lambda hd, b: (0, hd)
    head = lambda hd, b: (hd, 0, 0)
    tok = lambda hd, b: (b, hd)
    seq_blk = pl.BlockSpec((seq, HEAD), tok)
    return pl.pallas_call(
        _ordered(body, 12, (dproj,) + tuple(after)), name="lru_bwd", grid=(LRU_HEADS, nb),
        in_specs=[seq_blk] * 8 + [pl.BlockSpec((CONV_WIDTH, HEAD), col), pl.BlockSpec((None, HEAD, HEAD), head),
                                  pl.BlockSpec((None, HEAD, HEAD), head), pl.BlockSpec((1, HEAD), col)]
        + [_ANY] * (1 + len(after)),
        out_specs=[seq_blk, pl.BlockSpec((None, HEAD, HEAD), head), pl.BlockSpec((None, HEAD, HEAD), head),
                   pl.BlockSpec((8, HEAD), col)],
        out_shape=[jax.ShapeDtypeStruct(dproj.shape, BF16), jax.ShapeDtypeStruct((LRU_HEADS, HEAD, HEAD), F32),
                   jax.ShapeDtypeStruct((LRU_HEADS, HEAD, HEAD), F32), jax.ShapeDtypeStruct((8, w), F32)],
        input_output_aliases={12: 0},
        compiler_params=_cp(("arbitrary", "arbitrary")),
    )(proj, hs, e, dyl, *saved, w_conv, w_a, w_x, lam, dproj, *after)


def _weight_grad(a, g, col_shards, name, after=()):
    t, k = a.shape
    n = g.shape[1]
    tt = min(TT_DW, t)
    tk = k if k <= 1536 else 1024
    ns = n // N_CHIPS if col_shards else n
    narrow = col_shards and ns < 512
    tn = n if narrow else min(ns, 768 if ns % 768 == 0 else 1024)
    while ns % tn and not narrow:
        tn //= 2
    per = max(ns // tn, 1)

    def body(a_ref, g_ref, o_ref):
        @pl.when(pl.program_id(2) == 0)
        def _():
            o_ref[...] = jnp.zeros_like(o_ref)

        res = _mm_tn(a_ref[...], g_ref[...])
        if narrow:
            for q in range(N_CHIPS):
                o_ref[q] += res[:, q * ns:(q + 1) * ns]
        else:
            o_ref[...] += res

    if narrow:
        out_spec = pl.BlockSpec((N_CHIPS, tk, ns), lambda i, j, s: (0, i, 0))
        out_shape = jax.ShapeDtypeStruct((N_CHIPS, k, ns), F32)
    elif col_shards:
        out_spec = pl.BlockSpec((None, tk, tn), lambda i, j, s: (j // per, i, j % per))
        out_shape = jax.ShapeDtypeStruct((N_CHIPS, k, ns), F32)
    else:
        out_spec = pl.BlockSpec((tk, tn), lambda i, j, s: (i, j))
        out_shape = jax.ShapeDtypeStruct((k, n), F32)
    return pl.pallas_call(
        _ordered(body, 2, after), name=name, grid=(k // tk, n // tn, t // tt),
        in_specs=[pl.BlockSpec((tt, tk), lambda i, j, s: (s, i)), pl.BlockSpec((tt, tn), lambda i, j, s: (s, j))]
        + [_ANY] * len(after),
        out_specs=out_spec, out_shape=out_shape,
        compiler_params=_cp(("arbitrary", "arbitrary", "arbitrary")),
    )(a, g, *after)


def _input_grad(dproj, ws, slots, dz1, x2, modv, nb, seq, after=()):
    t, d = x2.shape
    nq = len(ws)
    ns = ws[0].shape[1]
    tm = min(TM_DH, seq)
    ts = min(TS_MLP, tm)
    tpb = seq // tm

    def body(slot_ref, dp_ref, *refs):
        w_hbm = refs[:nq]
        dz1_ref, x_ref, mod_ref, gx_ref, db_ref, pb_ref, w_s, acc, sems = refs[nq:]
        i = pl.program_id(0)

        @pl.when(i == 0)
        def _():
            _load_weights(w_hbm, [w_s.at[slot_ref[k]] for k in range(nq)], sems)
            db_ref[...] = jnp.zeros_like(db_ref)

        @pl.when(i % tpb == 0)
        def _():
            pb_ref[...] = jnp.zeros_like(pb_ref)

        for sub in range(tm // ts):
            rows = slice(sub * ts, (sub + 1) * ts)
            for q in range(nq):
                dp = dp_ref[rows, q * ns:(q + 1) * ns]
                part = _mm_nt(dp, w_s[q])
                if q == 0:
                    acc[sub] = part
                else:
                    acc[sub] += part
                db_ref[q, 0:1, :] += _colsum(dp.astype(F32))
            dh = acc[sub]
            gx_ref[rows, :] = ALPHA * dz1_ref[rows, :] + dh * (1.0 + mod_ref[1:2, :])
            pb_ref[0:1, :] += _colsum(dh * x_ref[rows, :])
            pb_ref[1:2, :] += _colsum(dh)

    tok = lambda i, s: (i, 0)
    in_specs = [pl.BlockSpec((tm, nq * ns), tok)] + [_ANY] * nq
    in_specs += [pl.BlockSpec((tm, d), tok), pl.BlockSpec((tm, d), tok),
                 pl.BlockSpec((None, 8, d), lambda i, s: (i // tpb, 0, 0))] + [_ANY] * len(after)
    return pl.pallas_call(
        _ordered(body, 5 + nq, after), name="input_grad",
        grid_spec=pltpu.PrefetchScalarGridSpec(
            num_scalar_prefetch=1, grid=(t // tm,), in_specs=in_specs,
            out_specs=[pl.BlockSpec((tm, d), tok), pl.BlockSpec((nq, 8, ns), lambda i, s: (0, 0, 0)),
                       pl.BlockSpec((None, 8, d), lambda i, s: (i // tpb, 0, 0))],
            scratch_shapes=[pltpu.VMEM((nq, d, ns), BF16), pltpu.VMEM((tm // ts, ts, d), F32),
                            pltpu.SemaphoreType.DMA((nq,))]),
        out_shape=[jax.ShapeDtypeStruct((t, d), F32), jax.ShapeDtypeStruct((nq, 8, ns), F32),
                   jax.ShapeDtypeStruct((nb, 8, d), F32)],
        compiler_params=_cp(("arbitrary",)),
    )(slots, dproj, *ws, dz1, x2, modv, *after)


def _rows128(v):
    flat = v.reshape(-1, HEAD)
    pad = (-flat.shape[0]) % 8
    return jnp.pad(flat, ((0, pad), (0, 0))) if pad else flat


def kernel(x, c, w_ada, b_ada, w_in, b_in, w_conv, b_conv, w_rg_a, b_rg_a, w_rg_x, b_rg_x, lru_lambda, w_sp, b_sp, ln_v_g, ln_v_b, w_o_lru, w_o_sgu, w_out, ln1_g, ln1_b, w_up, w_down, ln2_g, ln2_b, loss_target, m_w_ada, m_b_ada, m_w_in, m_b_in, m_w_conv, m_b_conv, m_w_rg_a, m_b_rg_a, m_w_rg_x, m_b_rg_x, m_lru_lambda, m_w_sp, m_b_sp, m_ln_v_g, m_ln_v_b, m_w_o_lru, m_w_o_sgu, m_w_out, m_ln1_g, m_ln1_b, m_w_up, m_w_down, m_ln2_g, m_ln2_b, v_w_ada, v_b_ada, v_w_in, v_b_in, v_w_conv, v_b_conv, v_w_rg_a, v_b_rg_a, v_w_rg_x, v_b_rg_x, v_lru_lambda, v_w_sp, v_b_sp, v_ln_v_g, v_ln_v_b, v_w_o_lru, v_w_o_sgu, v_w_out, v_ln1_g, v_ln1_b, v_w_up, v_w_down, v_ln2_g, v_ln2_b):
    given = dict(locals())
    nb, seq, d = x.shape
    t = nb * seq
    w_lru = LRU_HEADS * HEAD
    d_sgu = SGU_GROUPS * HEAD
    xi, yi, ci = lax.axis_index("x"), lax.axis_index("y"), lax.axis_index("c")
    chip = 2 * xi + yi
    dev = 2 * chip + ci
    cidx = jnp.reshape(ci, (1,)).astype(jnp.int32)

    x2 = x.reshape(t, d)
    target = loss_target.reshape(t, d)

    big = ["w_in", "w_o_lru", "w_o_sgu", "w_out", "w_up", "w_down"]
    shards_a = [w_in[0].astype(BF16)]
    shards_b = [given[n][0].astype(BF16) for n in big[1:]]
    pidx = jnp.reshape(chip, (1,)).astype(jnp.int32)

    c_rows = _rows128(c)
    wconv_rows = _rows128(w_conv[0])
    slab0 = _all_gather_small(jnp.concatenate([c_rows, wconv_rows], axis=0), "gather_c_wconv")
    slab0 = slab0.reshape(N_DEV, -1, HEAD)
    c_all = slab0[:, :c_rows.shape[0]].reshape(N_DEV * nb, d)
    n_wc = CONV_WIDTH * (w_lru // N_CHIPS) // HEAD
    wc = slab0[0::2, c_rows.shape[0]:c_rows.shape[0] + n_wc].reshape(N_CHIPS, CONV_WIDTH, w_lru // N_CHIPS)
    w_conv_full = jnp.transpose(wc, (1, 0, 2)).reshape(CONV_WIDTH, w_lru)

    n_ada = w_ada.shape[2]
    b_ada_cols = lax.dynamic_slice(b_ada, (0, chip * n_ada), (1, n_ada))
    mod_cols = _ada_fwd(c_all, w_ada[0], b_ada_cols)
    half = (N_DEV * nb) // 2
    mod_half = lax.dynamic_slice(mod_cols, (ci * half, 0), (half, n_ada))
    mod_g = _all_gather_small(mod_half, "gather_mod").reshape(N_CHIPS, 2, half, n_ada)
    mod_all = jnp.transpose(mod_g, (1, 2, 0, 3)).reshape(N_DEV * nb, N_CHIPS * n_ada)
    mod_loc = lax.dynamic_slice(mod_all, (dev * nb, 0), (nb, N_CHIPS * n_ada)).reshape(nb, 6, d)
    modv = jnp.pad(mod_loc, ((0, 0), (0, 2), (0, 0)))

    lru_w = (w_conv_full, b_conv, w_rg_a[0], b_rg_a, w_rg_x[0], b_rg_x, lru_lambda)
    b_sp_t = jnp.transpose(b_sp[0])

    land = lambda s: jax.ShapeDtypeStruct((N_CHIPS,) + s.shape, s.dtype)
    sds = lambda s: jax.ShapeDtypeStruct(s.shape, s.dtype)
    started_a = _split_start(shards_a, [sds(shards_a[0])] * 2, _peer_gather_copies((0, 1)), 2, "gather_w_in_near_start",
                             after=(modv,))
    shards_b, shards_c = shards_b[:3], shards_b[3:]

    ids = lambda *v: jnp.stack(v).astype(jnp.int32)
    modv_t = modv + started_a[-1][0:1, 0:1]
    proj, h = _proj_fwd(x2, modv_t, [started_a[2]], ids(chip), b_in, seq, "proj_fwd_own")
    own_a, lands_a = _split_wait(started_a, 1, _peer_gather_copies((0, 1)), "gather_w_in_near_wait",
                                 after=(proj, *shards_b, *shards_c))
    started_f = _split_start(own_a, [sds(own_a[0])], _far_gather_copies, 1, "gather_w_in_far_start", after=(lands_a[0],))
    started_b = _split_start(shards_b, [land(s) for s in shards_b], _gather_copies, 3 * len(shards_b),
                             "gather_w_mix_start", after=(started_f[-1],))
    started_c = _split_start(shards_c, [land(s) for s in shards_c], _gather_copies, 3 * len(shards_c),
                             "gather_w_mlp_start", after=(started_b[-1],))
    modv_t = modv + started_c[-1][0:1, 0:1]
    (proj,) = _proj_fwd(x2, modv_t, lands_a, ids(chip ^ 1, chip ^ 2), b_in, seq, "proj_fwd_near", proj_in=proj)
    own_a, land_f = _split_wait(started_f, 1, _far_gather_copies, "gather_w_in_far_wait", after=(proj,))
    (proj,) = _proj_fwd(x2, modv, land_f, ids(chip ^ 3), b_in, seq, "proj_fwd_far", proj_in=proj)
    w_in_shards, w_in_chips = own_a + lands_a + land_f, ids(chip, chip ^ 1, chip ^ 2, chip ^ 3)
    a, inp, r16, gi16, xc16 = _lru_prep(proj, lru_w, nb, seq)
    a3 = a.reshape(nb, seq, w_lru)
    hs = _scan(a3, inp.reshape(nb, seq, w_lru), False, "lru_scan", BF16).reshape(t, w_lru)
    y_sgu = _sgu_fwd(proj, w_sp[0], b_sp_t, ln_v_g, ln_v_b)
    shards_b, lands_b = _split_wait(started_b, len(shards_b), _gather_copies, "gather_w_mix_wait", after=(hs, y_sgu))
    w_o_lru_g, w_o_sgu_g, w_out_g = _fill_own_slot(lands_b, shards_b, pidx, ["own_" + n for n in big[1:4]])
    w_o_lru_g = w_o_lru_g.reshape(w_lru, d)
    w_out_g = w_out_g.reshape(d, d)
    yap, y_a, y_b, merged, mix, x1 = _mix_fwd(hs, proj, y_sgu, x2, modv, w_o_lru_g, w_o_sgu_g, w_out_g, ln1_g, ln1_b, seq)
    shards_c, lands_c = _split_wait(started_c, len(shards_c), _gather_copies, "gather_w_mlp_wait", after=(x1,))
    w_up_g, w_down_g = _fill_own_slot(lands_c, shards_c, pidx, ["own_" + n for n in big[4:]])
    w_down_g = w_down_g.reshape(-1, d)
    up, act, h2, dz2, df, st2, pb2 = _mlp_fwd(x1, modv, w_up_g, w_down_g, ln2_g, ln2_b, target, nb, seq)

    part = {}

    def to_sibling_start(group, tag, after=()):
        g4 = []
        for n in group:
            shard = given[n].shape[1:]
            g4.append(part[n].reshape(N_CHIPS, 2, shard[0] // 2, shard[1]))
        shapes = [jax.ShapeDtypeStruct((N_CHIPS,) + g.shape[2:], F32) for g in g4]
        return _split_start(g4, shapes, _to_sibling_copies, len(g4), "grads_to_sibling_start_" + tag, after)

    def to_chips_start(group, started, tag, after=()):
        g4, recv = _split_wait(started, len(group), _to_sibling_copies, "grads_to_sibling_wait_" + tag, after)
        own4 = [_add_own_half(g4[k], recv[k], cidx, "grad_pair_sum_" + n) for k, n in enumerate(group)]
        shapes = [jax.ShapeDtypeStruct((3,) + o.shape[1:], BF16) for o in own4]
        return _split_start(own4, shapes, _chip_exchange_copies, 3 * len(own4), "grads_chip_exchange_start_" + tag)

    def chips_finish(group, started, tag, after=()):
        own4, slots = _split_wait(started, len(group), _chip_exchange_copies, "grads_chip_exchange_wait_" + tag, after)
        return [_sum_own_and_peers(own4[k], slots[k], pidx, "grad_chip_sum_" + n) for k, n in enumerate(group)]

    dup, dz1, dmix, st1, pb1 = _mlp_bwd(df, up, w_down_g, w_up_g, dz2, x2, mix, modv, ln1_g, ln1_b, nb, seq)
    group1 = ["w_up", "w_down"]
    part["w_up"] = _weight_grad(h2, dup, True, "grad_w_up")
    part["w_down"] = _weight_grad(act, df, False, "grad_w_down")
    sib1 = to_sibling_start(group1, "mlp")
    dy_a, dy_b, dproj, dyl, dys = _mix_bwd(dmix, proj, y_a, y_b, hs, w_out_g, w_o_lru_g, w_o_sgu_g, seq,
                                                after=(sib1[-1],))
    group2 = ["w_o_lru", "w_o_sgu", "w_out"]
    part["w_o_lru"] = _weight_grad(yap, dy_a, False, "grad_w_o_lru")
    part["w_o_sgu"] = _weight_grad(y_sgu, dy_b, True, "grad_w_o_sgu")
    part["w_out"] = _weight_grad(merged, dmix, False, "grad_w_out")
    chips1 = to_chips_start(group1, sib1, "mlp", after=(dys, part["w_o_lru"], part["w_o_sgu"], part["w_out"]))
    sib2 = to_sibling_start(group2, "mix", after=(chips1[-1],))
    du, dv, g_w_sp, st_sgu, g_b_sp_t = _sgu_bwd(proj, dys, w_sp[0], b_sp_t, ln_v_g, ln_v_b, after=(sib2[-1],))
    dyl3 = dyl.reshape(nb, seq, w_lru)
    e = _scan(a3, dyl3, True, "lru_scan_bwd", BF16).reshape(t, w_lru)
    chips2 = to_chips_start(group2, sib2, "mix", after=(e, du))
    dproj = lax.dynamic_update_slice(dproj, du, (0, 2 * w_lru))
    dproj = lax.dynamic_update_slice(dproj, dv, (0, 2 * w_lru + d_sgu))
    dproj, g_w_rg_a, g_w_rg_x, st_lru = _lru_bwd(proj, hs, e, dyl, (a, r16, gi16, xc16), lru_w, nb, seq, dproj,
                                                 after=(chips2[-1],))

    didx = jnp.reshape(dev, (1,)).astype(jnp.int32)
    early = [
        ("w_conv", st_lru[4:8]), ("b_conv", st_lru[3]), ("w_rg_a", g_w_rg_a), ("b_rg_a", st_lru[0]),
        ("w_rg_x", g_w_rg_x), ("b_rg_x", st_lru[1]), ("lru_lambda", st_lru[2]), ("w_sp", g_w_sp),
        ("b_sp", jnp.transpose(g_b_sp_t[:, :SGU_GROUPS])), ("ln_v_g", st_sgu[0]), ("ln_v_b", st_sgu[1]),
        ("ln1_g", st1[0]), ("ln1_b", st1[1]), ("ln2_g", st2[0]), ("ln2_b", st2[1]),
    ]
    pieces_e = [_rows128(v) for _, v in early]
    slab_e = jnp.concatenate(pieces_e, axis=0)
    slab_e = jnp.pad(slab_e, ((0, (-slab_e.shape[0]) % TR_EW), (0, 0)))
    small_st = _split_start([slab_e], [jax.ShapeDtypeStruct((N_DEV,) + slab_e.shape, F32)], _all_devices_copies, N_DEV - 1,
                            "small_grads_start")

    group3 = ["w_in"]
    part["w_in"] = _weight_grad(h, dproj, True, "grad_w_in", after=(small_st[-1],))
    sib3 = to_sibling_start(group3, "in")
    halves12 = (chips_finish(group1, chips1, "mlp", after=(sib3[-1],))
                + chips_finish(group2, chips2, "mix", after=(sib3[-1],)))
    swap12 = _split_start(halves12, [jax.ShapeDtypeStruct(hv.shape, F32) for hv in halves12], _swap_copies, len(halves12),
                          "grads_swap_start")
    chips3 = to_chips_start(group3, sib3, "in", after=(swap12[-1],))
    grad_x2, g_b_in4, pb0 = _input_grad(dproj, w_in_shards, w_in_chips, dz1, x2, modv, nb, seq, after=(chips3[-1],))
    loss = lax.psum(st2[2, 0] + swap12[-1][0, 0], ("x", "y", "c"))
    grads = {}

    dmod_loc = jnp.stack([pb0[:, 1], pb0[:, 0], pb1[:, 2], pb1[:, 1], pb1[:, 0], pb2[:, 0]], axis=1)
    late = [("dmod", dmod_loc), ("b_in", g_b_in4[:, 0])]
    pieces_l = [_rows128(v) for _, v in late]
    slab_l = jnp.concatenate(pieces_l, axis=0)
    gathered = _all_gather_small(slab_l, "gather_small_grads", after=(swap12[-1],)).reshape(N_DEV, slab_l.shape[0], HEAD)
    rows_dmod = dmod_loc.size // HEAD
    dmod_all = gathered[:, :rows_dmod].reshape(N_DEV * nb, 6 * d)
    grads["b_in"] = _sum_slots(gathered[:, rows_dmod:], "grad_b_in_sum").reshape(1, -1)

    (slab_e,), (lands_e,) = _split_wait(small_st, 1, _all_devices_copies, "small_grads_wait", after=(gathered,))
    summed = _sum_devices(lands_e, slab_e, didx, "small_grad_sum")
    off = 0
    for (n, v), piece in zip(early, pieces_e):
        grads[n] = summed[off:off + v.size // HEAD].reshape(v.shape)
        off += piece.shape[0]

    mine12, theirs12 = _split_wait(swap12, len(halves12), _swap_copies, "grads_swap_wait", after=(summed,))
    (mine3,) = chips_finish(group3, chips3, "in", after=(summed,))
    (theirs3,) = _exchange([mine3], [jax.ShapeDtypeStruct(mine3.shape, F32)], _swap_copies, 1, "grads_swap_w_in")
    mine = dict(zip(group1 + group2 + group3, mine12 + [mine3]))
    theirs = dict(zip(group1 + group2 + group3, theirs12 + [theirs3]))

    dmod_cols = lax.dynamic_slice(dmod_all, (0, chip * n_ada), (N_DEV * nb, n_ada))
    grads["w_ada"], grads["b_ada"] = _ada_bwd(c_all, dmod_all, dmod_cols)
    n_wcs = w_lru // N_CHIPS
    grads["w_conv"] = lax.dynamic_slice(grads["w_conv"], (0, chip * n_wcs), (CONV_WIDTH, n_wcs))

    names = ['w_ada', 'b_ada', 'w_in', 'b_in', 'w_conv', 'b_conv', 'w_rg_a', 'b_rg_a', 'w_rg_x', 'b_rg_x', 'lru_lambda',
             'w_sp', 'b_sp', 'ln_v_g', 'ln_v_b', 'w_o_lru', 'w_o_sgu', 'w_out', 'ln1_g', 'ln1_b', 'w_up', 'w_down',
             'ln2_g', 'ln2_b']
    two_d = lambda v: v.reshape(-1, v.shape[-1])
    done = {}
    small_names = [n for n in names if n not in big and n != "w_ada"]
    small_out = _adamw_many([(two_d(given[n]), two_d(grads[n].reshape(given[n].shape)), two_d(given["m_" + n]),
                              two_d(given["v_" + n])) for n in small_names], "adamw_small")
    for n, res in zip(small_names, small_out):
        done[n] = (grads[n],) + tuple(res)
    for n in big + ["w_ada"]:
        w2, m2, v2 = two_d(given[n]), two_d(given["m_" + n]), two_d(given["v_" + n])
        if n in big:
            done[n] = _adamw_halves(w2, mine[n], theirs[n], m2, v2, cidx, "adamw_" + n)
        else:
            done[n] = (grads[n],) + tuple(_adamw(w2, two_d(grads[n]), m2, v2, "adamw_" + n))
    outs = [[done[n][k].reshape(given[n].shape) for n in names] for k in range(4)]
    return (loss, grad_x2.reshape(nb, seq, d), *outs[0], *outs[1], *outs[2], *outs[3])
```

```python
import functools
import math

import jax
import jax.numpy as jnp
from jax import lax
from jax.experimental import pallas as pl
from jax.experimental.pallas import tpu as pltpu

F32 = jnp.float32
BF16 = jnp.bfloat16
MESH = pl.DeviceIdType.MESH

N_CHIPS = 4
N_DEV = 8
LRU_HEADS = 10
HEAD = 128
SGU_GROUPS = 6
SGU_CHUNK = 64
CONV_WIDTH = 4
LRU_C = 8.0
ALPHA = 2.0 ** 0.25
LN_EPS = 1e-5
ADAM_LR, ADAM_B1, ADAM_B2, ADAM_EPS, ADAM_WD, ADAM_STEP = 0.001, 0.9, 0.999, 1e-08, 0.01, 10

VMEM_LIMIT = 56 * 1024 * 1024
VMEM_LIMIT_MAX = 62 * 1024 * 1024
TM_PROJ = 1024
TM_MIX = 512
TM_MLP = 512
TS_MLP = 256
TM_SGU = 512
TM_DH = 512
TT_DW = 4096
TC_SCAN = 256
TR_EW = 256


def _cp(sem=None, limit=None):
    return pltpu.CompilerParams(dimension_semantics=sem, vmem_limit_bytes=limit or VMEM_LIMIT)


def _mm(a, b):
    return jnp.dot(a.astype(BF16), b.astype(BF16), preferred_element_type=F32)


def _mm_nt(a, b):
    return lax.dot_general(a.astype(BF16), b.astype(BF16), (((1,), (1,)), ((), ())), preferred_element_type=F32)


def _mm_tn(a, b):
    return lax.dot_general(a.astype(BF16), b.astype(BF16), (((0,), (0,)), ((), ())), preferred_element_type=F32)


def _sigmoid(x):
    return 1.0 / (1.0 + jnp.exp(-x))


def _sigmoid_t(x):
    return 0.5 * jnp.tanh(0.5 * x) + 0.5


_GELU_K = math.sqrt(2.0 / math.pi)


def _gelu(x):
    t = jnp.tanh(_GELU_K * (x + 0.044715 * (x * x * x)))
    return 0.5 * x * (1.0 + t)


def _gelu_and_grad(x):
    x2 = x * x
    t = jnp.tanh(_GELU_K * (x + 0.044715 * (x2 * x)))
    g = 0.5 * x * (1.0 + t)
    dg = 0.5 * (1.0 + t) + 0.5 * x * (1.0 - t * t) * (_GELU_K * (1.0 + 3.0 * 0.044715 * x2))
    return g, dg


def _ln_stats(z):
    mu = jnp.mean(z, axis=-1, keepdims=True)
    zc = z - mu
    var = jnp.mean(zc * zc, axis=-1, keepdims=True)
    rstd = lax.rsqrt(var + LN_EPS)
    return zc * rstd, rstd


def _ln_bwd(dxh, xhat, rstd):
    m1 = jnp.mean(dxh, axis=-1, keepdims=True)
    m2 = jnp.mean(dxh * xhat, axis=-1, keepdims=True)
    return rstd * (dxh - m1 - xhat * m2)


def _colsum(v):
    return jnp.sum(v, axis=0, keepdims=True)


def _shift_down(v, j):
    if j == 0:
        return v
    rows = lax.broadcasted_iota(jnp.int32, v.shape, 0)
    return jnp.where(rows >= j, pltpu.roll(v, j, 0), 0.0)


def _shift_up(v, j):
    if j == 0:
        return v
    n = v.shape[0]
    rows = lax.broadcasted_iota(jnp.int32, v.shape, 0)
    return jnp.where(rows < n - j, pltpu.roll(v, n - j, 0), 0.0)


def _load_weights(srcs, dsts, sems):
    cps = [pltpu.make_async_copy(s, dd, sems.at[k]) for k, (s, dd) in enumerate(zip(srcs, dsts))]
    for cp in cps:
        cp.start()
    for cp in cps:
        cp.wait()


def _my_pos():
    return lax.axis_index("x"), lax.axis_index("y"), lax.axis_index("c")


def _all_gather_small(v, name, after=()):
    m_per, n = v.shape

    def body(x_ref, out_ref, send_sems, recv_sems, local_sem):
        x, y, c = _my_pos()
        me, sibling = (x, y, c), (x, y, 1 - c)
        chips = [(1 - x, y), (x, 1 - y), (1 - x, 1 - y)]

        def rows(px, py, pc):
            return out_ref.at[pl.ds((4 * px + 2 * py + pc) * m_per, m_per), :]

        def copy(k, block, to, src=None):
            return pltpu.make_async_remote_copy(
                src_ref=rows(*block) if src is None else src, dst_ref=rows(*block),
                send_sem=send_sems.at[k], recv_sem=recv_sems.at[k], device_id=to, device_id_type=MESH)

        mine = pltpu.make_async_copy(x_ref, rows(*me), local_sem)
        mine.start()
        first = [copy(0, me, sibling, src=x_ref)]
        first += [copy(1 + j, me, (*chip, c), src=x_ref) for j, chip in enumerate(chips)]
        for cp in first:
            cp.start()
        passed = [copy(4 + j, (*chip, c), sibling) for j, chip in enumerate(chips)]
        for j, chip in enumerate(chips):
            copy(1 + j, (*chip, c), me).wait_recv()
            passed[j].start()
        copy(0, sibling, me).wait_recv()
        for j, chip in enumerate(chips):
            copy(4 + j, (*chip, 1 - c), me).wait_recv()
        for cp in first + passed:
            cp.wait_send()
        mine.wait()

    return pl.pallas_call(
        _ordered(body, 1, after), name=name,
        out_shape=jax.ShapeDtypeStruct((N_DEV * m_per, n), v.dtype),
        in_specs=[pl.BlockSpec(memory_space=pltpu.VMEM)] + [pl.BlockSpec(memory_space=pl.ANY)] * len(after),
        out_specs=pl.BlockSpec(memory_space=pltpu.VMEM),
        scratch_shapes=[pltpu.SemaphoreType.DMA((7,)), pltpu.SemaphoreType.DMA((7,)), pltpu.SemaphoreType.DMA],
        compiler_params=pltpu.CompilerParams(vmem_limit_bytes=VMEM_LIMIT),
    )(v, *after)


_HBM = pl.BlockSpec(memory_space=pltpu.HBM)
_ANY = pl.BlockSpec(memory_space=pl.ANY)
_SEM = pl.BlockSpec(memory_space=pltpu.SEMAPHORE)
_EFFECT = pltpu.SideEffectType.DATAFLOW_SIDE_EFFECTING


def _ordered(body, n_in, after):
    k = len(after)
    if not k:
        return body
    return lambda *refs: body(*refs[:n_in], *refs[n_in + k:])


def _gather_copies(ins, lands, send_sems, recv_sems):
    x, y, c = _my_pos()
    p = 2 * x + y
    peers = [(x, 1 - y), (1 - x, y), (1 - x, 1 - y)]
    sends, recvs = [], []
    for k in range(len(ins)):
        for j, (qx, qy) in enumerate(peers):
            sems = dict(send_sem=send_sems.at[3 * k + j], recv_sem=recv_sems.at[3 * k + j],
                        device_id=(qx, qy, c), device_id_type=MESH)
            sends.append(pltpu.make_async_remote_copy(src_ref=ins[k], dst_ref=lands[k].at[p], **sems))
            recvs.append(pltpu.make_async_remote_copy(src_ref=ins[k], dst_ref=lands[k].at[2 * qx + qy], **sems))
    return sends, recvs


def _peer_gather_copies(peers):
    def copies(ins, lands, send_sems, recv_sems):
        x, y, c = _my_pos()
        where = [(x, 1 - y), (1 - x, y), (1 - x, 1 - y)]
        cps = [pltpu.make_async_remote_copy(
            src_ref=ins[0], dst_ref=lands[j], send_sem=send_sems.at[j], recv_sem=recv_sems.at[j],
            device_id=(*where[j], c), device_id_type=MESH) for j in peers]
        return cps, cps
    return copies


def _far_gather_copies(ins, lands, send_sems, recv_sems):
    x, y, c = _my_pos()
    cps = [pltpu.make_async_remote_copy(
        src_ref=ins[0], dst_ref=lands[0], send_sem=send_sems.at[0], recv_sem=recv_sems.at[0],
        device_id=(1 - x, 1 - y, c), device_id_type=MESH)]
    return cps, cps


def _to_sibling_copies(ins, lands, send_sems, recv_sems):
    x, y, c = _my_pos()
    cps = [pltpu.make_async_remote_copy(
        src_ref=ins[k].at[:, 1 - c], dst_ref=lands[k], send_sem=send_sems.at[k], recv_sem=recv_sems.at[k],
        device_id=(x, y, 1 - c), device_id_type=MESH) for k in range(len(ins))]
    return cps, cps


def _chip_exchange_copies(ins, lands, send_sems, recv_sems):
    x, y, c = _my_pos()
    peers = [(x, 1 - y), (1 - x, y), (1 - x, 1 - y)]
    cps = []
    for k in range(len(ins)):
        for j, (qx, qy) in enumerate(peers):
            cps.append(pltpu.make_async_remote_copy(
                src_ref=ins[k].at[2 * qx + qy], dst_ref=lands[k].at[j], send_sem=send_sems.at[3 * k + j],
                recv_sem=recv_sems.at[3 * k + j], device_id=(qx, qy, c), device_id_type=MESH))
    return cps, cps


def _all_devices_copies(ins, lands, send_sems, recv_sems):
    x, y, c = _my_pos()
    me = 4 * x + 2 * y + c
    sends, recvs = [], []
    for r in range(1, N_DEV):
        px = 1 - x if r & 4 else x
        py = 1 - y if r & 2 else y
        pc = 1 - c if r & 1 else c
        sems = dict(send_sem=send_sems.at[r - 1], recv_sem=recv_sems.at[r - 1], device_id=(px, py, pc), device_id_type=MESH)
        sends.append(pltpu.make_async_remote_copy(src_ref=ins[0], dst_ref=lands[0].at[me], **sems))
        recvs.append(pltpu.make_async_remote_copy(src_ref=ins[0], dst_ref=lands[0].at[4 * px + 2 * py + pc], **sems))
    return sends, recvs


def _swap_copies(ins, lands, send_sems, recv_sems):
    x, y, c = _my_pos()
    cps = [pltpu.make_async_remote_copy(
        src_ref=ins[k], dst_ref=lands[k], send_sem=send_sems.at[k], recv_sem=recv_sems.at[k],
        device_id=(x, y, 1 - c), device_id_type=MESH) for k in range(len(ins))]
    return cps, cps


def _split_start(ins, land_shapes, copies, n_sems, name, after=()):
    n, nl = len(ins), len(land_shapes)
    first_out = n + nl + len(after)

    def body(*refs):
        in_refs, land_refs = refs[:n], refs[n:n + nl]
        send_sems, recv_sems = refs[first_out:first_out + 2]
        token = refs[-1]
        sends, _ = copies(in_refs, land_refs, send_sems, recv_sems)
        for cp in sends:
            cp.start()
        token[...] = jnp.zeros_like(token)

    lands = [pltpu.with_memory_space_constraint(lax.empty(s.shape, s.dtype), pltpu.HBM) for s in land_shapes]
    ins = [pltpu.with_memory_space_constraint(s, pltpu.HBM) for s in ins]
    return pl.pallas_call(
        body, name=name,
        out_shape=(pltpu.SemaphoreType.DMA((n_sems,)), pltpu.SemaphoreType.DMA((n_sems,)),
                   *[pltpu.HBM(s.shape, s.dtype) for s in ins], *[pltpu.HBM(s.shape, s.dtype) for s in lands],
                   jax.ShapeDtypeStruct((8, HEAD), F32)),
        in_specs=[_HBM] * (n + nl) + [pl.BlockSpec(memory_space=pl.ANY)] * len(after),
        out_specs=(_SEM, _SEM, *([_HBM] * (n + nl)), pl.BlockSpec(memory_space=pltpu.VMEM)),
        input_output_aliases={k: 2 + k for k in range(n + nl)},
        compiler_params=pltpu.CompilerParams(has_side_effects=_EFFECT),
    )(*ins, *lands, *after)


def _split_wait(started, n, copies, name, after=()):
    send_sems, recv_sems = started[0], started[1]
    bufs = started[2:-1]
    nb = len(bufs)

    def body(*refs):
        in_refs, land_refs = refs[:n], refs[n:nb]
        sends, recvs = copies(in_refs, land_refs, refs[nb], refs[nb + 1])
        for cp in sends:
            cp.wait_send()
        for cp in recvs:
            cp.wait_recv()

    outs = pl.pallas_call(
        body, name=name,
        out_shape=tuple(pltpu.HBM(s.shape, s.dtype) for s in bufs),
        in_specs=[_HBM] * nb + [_SEM, _SEM] + [pl.BlockSpec(memory_space=pl.ANY)] * len(after),
        out_specs=tuple([_HBM] * nb),
        input_output_aliases={k: k for k in range(nb)},
        compiler_params=pltpu.CompilerParams(has_side_effects=_EFFECT),
    )(*bufs, send_sems, recv_sems, *after)
    return list(outs[:n]), list(outs[n:])


def _fill_own_slot(gathered, shards, pidx, names):
    outs = []
    for g, s, name in zip(gathered, shards, names):
        r, cdim = s.shape
        tr = _row_tile(r)

        def body(p_ref, s_ref, g_ref, o_ref):
            o_ref[...] = s_ref[...]

        outs.append(pl.pallas_call(
            body, name=name,
            grid_spec=pltpu.PrefetchScalarGridSpec(
                num_scalar_prefetch=1, grid=(r // tr,),
                in_specs=[pl.BlockSpec((tr, cdim), lambda i, p: (i, 0)), pl.BlockSpec(memory_space=pl.ANY)],
                out_specs=pl.BlockSpec((None, tr, cdim), lambda i, p: (p[0], i, 0))),
            out_shape=jax.ShapeDtypeStruct(g.shape, g.dtype),
            input_output_aliases={2: 0},
            compiler_params=_cp(("arbitrary",)),
        )(pidx, s, g))
    return outs


def _sum_own_and_peers(own4, slots, pidx, name):
    _, rh, cdim = own4.shape
    tr = _row_tile(rh)

    def body(p_ref, own_ref, s_ref, o_ref):
        acc = own_ref[...].astype(F32)
        for j in range(3):
            acc = acc + s_ref[j].astype(F32)
        o_ref[...] = acc

    return pl.pallas_call(
        body, name=name,
        grid_spec=pltpu.PrefetchScalarGridSpec(
            num_scalar_prefetch=1, grid=(rh // tr,),
            in_specs=[pl.BlockSpec((None, tr, cdim), lambda i, p: (p[0], i, 0)),
                      pl.BlockSpec((3, tr, cdim), lambda i, p: (0, i, 0))],
            out_specs=pl.BlockSpec((tr, cdim), lambda i, p: (i, 0))),
        out_shape=jax.ShapeDtypeStruct((rh, cdim), F32),
        compiler_params=_cp(("arbitrary",)),
    )(pidx, own4, slots)


def _exchange(ins, land_shapes, copies, n_sems, name):
    n, nl = len(ins), len(land_shapes)

    def body(*refs):
        sends, recvs = copies(refs[:n], refs[n:n + nl], refs[n + nl], refs[n + nl + 1])
        for cp in sends:
            cp.start()
        for cp in sends:
            cp.wait_send()
        for cp in recvs:
            cp.wait_recv()

    any_spec = pl.BlockSpec(memory_space=pl.ANY)
    return pl.pallas_call(
        body, name=name,
        out_shape=[jax.ShapeDtypeStruct(s.shape, s.dtype) for s in land_shapes],
        in_specs=[any_spec] * n, out_specs=[any_spec] * nl,
        scratch_shapes=[pltpu.SemaphoreType.DMA((n_sems,)), pltpu.SemaphoreType.DMA((n_sems,))],
    )(*ins)


def _row_tile(r):
    t = min(TR_EW, r)
    while r % t:
        t //= 2
    return t


def _add_own_half(g4, recv, cidx, name):
    _, _, rh, cdim = g4.shape
    tr = _row_tile(rh)

    def body(c_ref, a_ref, b_ref, o_ref):
        o_ref[...] = (a_ref[...] + b_ref[...]).astype(BF16)

    return pl.pallas_call(
        body, name=name,
        grid_spec=pltpu.PrefetchScalarGridSpec(
            num_scalar_prefetch=1, grid=(N_CHIPS, rh // tr),
            in_specs=[pl.BlockSpec((None, None, tr, cdim), lambda q, i, c: (q, c[0], i, 0)),
                      pl.BlockSpec((None, tr, cdim), lambda q, i, c: (q, i, 0))],
            out_specs=pl.BlockSpec((None, tr, cdim), lambda q, i, c: (q, i, 0))),
        out_shape=jax.ShapeDtypeStruct(recv.shape, BF16),
        compiler_params=_cp(("arbitrary", "arbitrary")),
    )(cidx, g4, recv)


def _sum_slots(v, name):
    n, r, cdim = v.shape
    tr = _row_tile(r)

    def body(v_ref, o_ref):
        acc = v_ref[0].astype(F32)
        for k in range(1, n):
            acc = acc + v_ref[k].astype(F32)
        o_ref[...] = acc

    return pl.pallas_call(
        body, name=name, grid=(r // tr,),
        in_specs=[pl.BlockSpec((n, tr, cdim), lambda i: (0, i, 0))],
        out_specs=pl.BlockSpec((tr, cdim), lambda i: (i, 0)),
        out_shape=jax.ShapeDtypeStruct((r, cdim), F32),
        compiler_params=_cp(("arbitrary",)),
    )(v)


def _sum_devices(lands, own, didx, name):
    _, r, cdim = lands.shape
    tr = _row_tile(r)

    def body(d_ref, l_ref, own_ref, o_ref):
        acc = jnp.where(d_ref[0] == 0, own_ref[...], l_ref[0])
        for dv in range(1, N_DEV):
            acc = acc + jnp.where(d_ref[0] == dv, own_ref[...], l_ref[dv])
        o_ref[...] = acc

    return pl.pallas_call(
        body, name=name,
        grid_spec=pltpu.PrefetchScalarGridSpec(
            num_scalar_prefetch=1, grid=(r // tr,),
            in_specs=[pl.BlockSpec((N_DEV, tr, cdim), lambda i, dd: (0, i, 0)), pl.BlockSpec((tr, cdim), lambda i, dd: (i, 0))],
            out_specs=pl.BlockSpec((tr, cdim), lambda i, dd: (i, 0))),
        out_shape=jax.ShapeDtypeStruct((r, cdim), F32),
        compiler_params=_cp(("arbitrary",)),
    )(didx, lands, own)


def _adamw_math(wv, gg, mv, vv):
    nm = ADAM_B1 * mv + (1.0 - ADAM_B1) * gg
    nv = ADAM_B2 * vv + (1.0 - ADAM_B2) * (gg * gg)
    m_hat = nm / (1.0 - ADAM_B1 ** ADAM_STEP)
    v_hat = nv / (1.0 - ADAM_B2 ** ADAM_STEP)
    return -ADAM_LR * (m_hat / (jnp.sqrt(v_hat) + ADAM_EPS) + ADAM_WD * wv), nm, nv


def _adamw_halves(w, mine, theirs, m, v, cidx, name):
    r, cdim = w.shape
    rh = r // 2
    tr = _row_tile(rh)
    nblk = rh // tr

    def body(c_ref, w_ref, a_ref, b_ref, m_ref, v_ref, g_ref, d_ref, nm_ref, nv_ref):
        gg = jnp.where(pl.program_id(0) == c_ref[0], a_ref[...], b_ref[...])
        g_ref[...] = gg
        d_ref[...], nm_ref[...], nv_ref[...] = _adamw_math(w_ref[...], gg, m_ref[...], v_ref[...])

    full = pl.BlockSpec((tr, cdim), lambda hh, i, c: (hh * nblk + i, 0))
    half = pl.BlockSpec((tr, cdim), lambda hh, i, c: (i, 0))
    return pl.pallas_call(
        body, name=name,
        grid_spec=pltpu.PrefetchScalarGridSpec(
            num_scalar_prefetch=1, grid=(2, nblk),
            in_specs=[full, half, half, full, full], out_specs=[full] * 4),
        out_shape=[jax.ShapeDtypeStruct((r, cdim), F32)] * 4,
        compiler_params=_cp(("arbitrary", "arbitrary")),
    )(cidx, w, mine, theirs, m, v)


def _adamw_many(params, name):
    n = len(params)

    def body(*refs):
        ins, outs = refs[:4 * n], refs[4 * n:]
        for k in range(n):
            w_ref, g_ref, m_ref, v_ref = ins[4 * k:4 * k + 4]
            outs[3 * k][...], outs[3 * k + 1][...], outs[3 * k + 2][...] = _adamw_math(
                w_ref[...], g_ref[...], m_ref[...], v_ref[...])

    flat = [a for p in params for a in p]
    res = pl.pallas_call(
        body, name=name,
        out_shape=[jax.ShapeDtypeStruct(p[0].shape, F32) for p in params for _ in range(3)],
        compiler_params=pltpu.CompilerParams(vmem_limit_bytes=VMEM_LIMIT),
    )(*flat)
    return [res[3 * k:3 * k + 3] for k in range(n)]


def _adamw(w, g, m, v, name):
    r, cdim = w.shape
    tr = _row_tile(r) if r % 8 == 0 else r

    def body(w_ref, g_ref, m_ref, v_ref, d_ref, nm_ref, nv_ref):
        d_ref[...], nm_ref[...], nv_ref[...] = _adamw_math(w_ref[...], g_ref[...], m_ref[...], v_ref[...])

    spec = pl.BlockSpec((tr, cdim), lambda i: (i, 0))
    return pl.pallas_call(
        body, name=name, grid=(r // tr,), in_specs=[spec] * 4, out_specs=[spec] * 3,
        out_shape=[jax.ShapeDtypeStruct((r, cdim), F32)] * 3,
        compiler_params=_cp(("arbitrary",)),
    )(w, g, m, v)


def _ada_fwd(c_all, w_ada, b_cols):
    nb, _ = c_all.shape
    n = w_ada.shape[1]

    def body(c_ref, w_ref, b_ref, o_ref):
        cv = c_ref[...]
        o_ref[...] = _mm(cv * _sigmoid(cv), w_ref[...]) + b_ref[...]

    return pl.pallas_call(
        body, name="ada_fwd", out_shape=jax.ShapeDtypeStruct((nb, n), F32),
        compiler_params=pltpu.CompilerParams(vmem_limit_bytes=VMEM_LIMIT),
    )(c_all, w_ada, b_cols)


def _ada_bwd(c_all, dmod_all, dmod_cols):
    d = c_all.shape[1]
    n = dmod_cols.shape[1]

    def body(c_ref, da_ref, dc_ref, gw_ref, gb_ref):
        cv = c_ref[...]
        gw_ref[...] = _mm_tn(cv * _sigmoid(cv), dc_ref[...])
        gb_ref[...] = _colsum(da_ref[...])

    return pl.pallas_call(
        body, name="ada_bwd",
        out_shape=[jax.ShapeDtypeStruct((d, n), F32), jax.ShapeDtypeStruct((1, dmod_all.shape[1]), F32)],
        compiler_params=pltpu.CompilerParams(vmem_limit_bytes=VMEM_LIMIT),
    )(c_all, dmod_all, dmod_cols)


def _proj_fwd(x2, modv, ws, cols, b_in, seq, name, proj_in=None):
    t, d = x2.shape
    n = len(ws)
    ns = ws[0].shape[1]
    tm = min(TM_PROJ, seq)
    tpb = seq // tm
    first = proj_in is None

    def body(c_ref, x_ref, mod_ref, *refs):
        w_refs, b_ref = refs[:n], refs[n]
        outs = refs[n + 1 if first else n + 2:]
        proj_ref, h_s = outs[0], outs[-1]
        s = pl.program_id(1)

        @pl.when(s == 0)
        def _():
            h = (x_ref[...] * (1.0 + mod_ref[1:2, :]) + mod_ref[0:1, :]).astype(BF16)
            h_s[...] = h
            if first:
                outs[1][...] = h

        for k in range(n):
            @pl.when(s == k)
            def _():
                proj_ref[...] = (jnp.dot(h_s[...], w_refs[k][...], preferred_element_type=F32) + b_ref[...]).astype(BF16)

    in_specs = [pl.BlockSpec((tm, d), lambda i, s, c: (i, 0)),
                pl.BlockSpec((None, 8, d), lambda i, s, c: (i // tpb, 0, 0))]
    in_specs += [pl.BlockSpec((d, ns), lambda i, s, c: (0, 0))] * n
    in_specs += [pl.BlockSpec((1, ns), lambda i, s, c: (0, c[s]))]
    out_specs = [pl.BlockSpec((tm, ns), lambda i, s, c: (i, c[s]))]
    out_shape = [jax.ShapeDtypeStruct((t, N_CHIPS * ns), BF16)]
    args = [cols, x2, modv, *ws, b_in]
    aliases = {}
    if first:
        out_specs.append(pl.BlockSpec((tm, d), lambda i, s, c: (i, 0)))
        out_shape.append(jax.ShapeDtypeStruct((t, d), BF16))
    else:
        in_specs.append(_ANY)
        args.append(proj_in)
        aliases = {len(args) - 1: 0}
    return pl.pallas_call(
        body, name=name,
        grid_spec=pltpu.PrefetchScalarGridSpec(
            num_scalar_prefetch=1, grid=(t // tm, n), in_specs=in_specs, out_specs=out_specs,
            scratch_shapes=[pltpu.VMEM((tm, d), BF16)]),
        out_shape=out_shape, input_output_aliases=aliases,
        compiler_params=_cp(("arbitrary", "arbitrary")),
    )(*args)


def _lru_rate(lam_ref):
    nl = -lam_ref[...]
    e = jnp.exp(-jnp.abs(nl))
    u = 1.0 + e
    dlt = u - 1.0
    log1p_e = jnp.where(dlt == 0.0, e, jnp.log(u) * (e / jnp.where(dlt == 0.0, 1.0, dlt)))
    return -LRU_C * (jnp.maximum(nl, 0.0) + log1p_e)


def _lru_gates(xl, wc_ref, bc_ref, wa_ref, ba_ref, wx_ref, bx_ref, lam_ref):
    xc = bc_ref[...] + wc_ref[CONV_WIDTH - 1:CONV_WIDTH, :] * xl
    for k in range(CONV_WIDTH - 1):
        xc = xc + wc_ref[k:k + 1, :] * _shift_down(xl, CONV_WIDTH - 1 - k)
    r = _sigmoid(_mm(xc, wa_ref[...]) + ba_ref[...])
    gi = _sigmoid_t(_mm(xc, wx_ref[...]) + bx_ref[...])
    big_l = _lru_rate(lam_ref)
    la = big_l * r
    a = jnp.exp(la)
    m2 = jnp.tanh(-la) * (a * a + 1.0)
    return xc, r, gi, big_l, a, m2


def _lru_prep(proj, lru_w, nb, seq):
    t = proj.shape[0]
    w = LRU_HEADS * HEAD
    w_conv, b_conv, w_a, b_a, w_x, b_x, lam = lru_w

    def body(x_ref, wc_ref, bc_ref, wa_ref, ba_ref, wx_ref, bx_ref, lam_ref, a_ref, inp_ref, r_ref, gi_ref, xc_ref):
        xc, r, gi, big_l, a, m2 = _lru_gates(x_ref[...].astype(F32), wc_ref, bc_ref, wa_ref, ba_ref, wx_ref, bx_ref, lam_ref)
        a_ref[...] = a
        inp_ref[...] = (jnp.sqrt(m2) * (gi * xc)).astype(BF16)
        r_ref[...] = r.astype(BF16)
        gi_ref[...] = gi.astype(BF16)
        xc_ref[...] = xc.astype(BF16)

    col = lambda b, hd: (0, hd)
    head = lambda b, hd: (hd, 0, 0)
    tok = lambda b, hd: (b, hd)
    return pl.pallas_call(
        body, name="lru_prep", grid=(nb, LRU_HEADS),
        in_specs=[pl.BlockSpec((seq, HEAD), tok),
                  pl.BlockSpec((CONV_WIDTH, HEAD), col), pl.BlockSpec((1, HEAD), col),
                  pl.BlockSpec((None, HEAD, HEAD), head), pl.BlockSpec((1, HEAD), col),
                  pl.BlockSpec((None, HEAD, HEAD), head), pl.BlockSpec((1, HEAD), col),
                  pl.BlockSpec((1, HEAD), col)],
        out_specs=[pl.BlockSpec((seq, HEAD), tok)] * 5,
        out_shape=[jax.ShapeDtypeStruct((t, w), F32)] + [jax.ShapeDtypeStruct((t, w), BF16)] * 4,
        compiler_params=_cp(("arbitrary", "arbitrary")),
    )(proj, w_conv, b_conv, w_a, b_a, w_x, b_x, lam)


def _scan(a3, b3, reverse, name, out_dtype):
    nb, seq, w = a3.shape
    tc = min(TC_SCAN, seq)
    nchunk = seq // tc
    npair = tc // 16

    def combine(av, bv):
        rows = lax.broadcasted_iota(jnp.int32, av.shape, 0)
        for s in (1, 2, 4):
            if reverse:
                keep = rows < 8 - s
                a_sh, b_sh = pltpu.roll(av, 8 - s, 0), pltpu.roll(bv, 8 - s, 0)
            else:
                keep = rows >= s
                a_sh, b_sh = pltpu.roll(av, s, 0), pltpu.roll(bv, s, 0)
            bv = jnp.where(keep, bv + av * b_sh, bv)
            av = jnp.where(keep, av * a_sh, av)
        return av, bv

    def body(a_ref, b_ref, h_ref, carry):
        @pl.when(pl.program_id(0) == 0)
        def _():
            carry[...] = jnp.zeros_like(carry)

        for b in range(nb):
            def pair(j, hprev):
                jj = npair - 1 - j if reverse else j
                base = pl.multiple_of(jj * 16, 16)
                a16 = a_ref[b, pl.ds(base, 16), :]
                b16 = b_ref[b, pl.ds(base, 16), :].astype(F32)
                outs = [None, None]
                for k in ((1, 0) if reverse else (0, 1)):
                    av, bv = a16[8 * k:8 * k + 8, :], b16[8 * k:8 * k + 8, :]
                    av, bv = combine(av, av * bv if reverse else bv)
                    h = bv + av * hprev
                    outs[k] = h
                    hprev = jnp.broadcast_to(h[0:1, :] if reverse else h[7:8, :], (8, w))
                h_ref[b, pl.ds(base, 16), :] = jnp.concatenate(outs, axis=0).astype(out_dtype)
                return hprev

            carry[b] = lax.fori_loop(0, npair, pair, carry[b])

    imap = (lambda i: (0, nchunk - 1 - i, 0)) if reverse else (lambda i: (0, i, 0))
    spec = pl.BlockSpec((nb, tc, w), imap)
    return pl.pallas_call(
        body, name=name, grid=(nchunk,), in_specs=[spec, spec], out_specs=spec,
        out_shape=jax.ShapeDtypeStruct((nb, seq, w), out_dtype),
        scratch_shapes=[pltpu.VMEM((nb, 8, w), F32)],
        compiler_params=_cp(("arbitrary",)),
    )(a3, b3)


def _sgu_mask():
    ti = lax.broadcasted_iota(jnp.int32, (HEAD, HEAD), 0) // SGU_CHUNK
    si = lax.broadcasted_iota(jnp.int32, (HEAD, HEAD), 1) // SGU_CHUNK
    return si <= ti


def _sgu_specs(tm, d_sgu):
    pw = 256
    first_u = (2 * LRU_HEADS * HEAD) // pw
    n_piece = d_sgu // pw
    specs = [pl.BlockSpec((tm, pw), functools.partial(lambda i, k: (i, k), k=first_u + j)) for j in range(2 * n_piece)]
    return specs, n_piece


def _sgu_fwd(proj, w_sp, b_sp_t, ln_g, ln_b):
    t = proj.shape[0]
    d_sgu = SGU_GROUPS * HEAD
    tm = min(TM_SGU, t)
    nblk = tm // HEAD
    specs, n_piece = _sgu_specs(tm, d_sgu)

    def body(*refs):
        u = jnp.concatenate([r[...] for r in refs[:n_piece]], axis=1).astype(F32)
        v = jnp.concatenate([r[...] for r in refs[n_piece:2 * n_piece]], axis=1).astype(F32)
        w_ref, bt_ref, g_ref, b_ref, y_ref = refs[2 * n_piece:]
        ug = _gelu(u)
        xhat, _ = _ln_stats(_gelu(v))
        vn = (xhat * g_ref[...] + b_ref[...]).astype(BF16)
        mask = _sgu_mask()
        for g in range(SGU_GROUPS):
            wm = jnp.where(mask, w_ref[g], 0.0).astype(BF16)
            cols = slice(g * HEAD, (g + 1) * HEAD)
            for n in range(nblk):
                rows = slice(n * HEAD, (n + 1) * HEAD)
                mixed = jnp.dot(wm, vn[rows, cols], preferred_element_type=F32) + bt_ref[:, g:g + 1]
                y_ref[rows, cols] = (ug[rows, cols] * mixed).astype(BF16)

    full = lambda shape: pl.BlockSpec(shape, lambda i: (0,) * len(shape))
    return pl.pallas_call(
        body, name="sgu_fwd", grid=(t // tm,),
        in_specs=specs + [full(w_sp.shape), full(b_sp_t.shape), full(ln_g.shape), full(ln_b.shape)],
        out_specs=pl.BlockSpec((tm, d_sgu), lambda i: (i, 0)),
        out_shape=jax.ShapeDtypeStruct((t, d_sgu), BF16),
        compiler_params=_cp(("arbitrary",)),
    )(*([proj] * (2 * n_piece)), w_sp, b_sp_t, ln_g, ln_b)


def _mix_fwd(hs, proj, y_sgu, x2, modv, w_o_lru_g, w_o_sgu_g, w_out_g, ln1_g, ln1_b, seq):
    t, d = x2.shape
    w = hs.shape[1]
    d_sgu = y_sgu.shape[1]
    nq, _, ns = w_o_sgu_g.shape
    tm = min(TM_MIX, seq)
    ts = min(TS_MLP, tm)
    tpb = seq // tm

    def body(hs_ref, gl_ref, ys_ref, ga_ref, gb_ref, x_ref, mod_ref, wl_hbm, ws_hbm, wo_hbm, g1_ref, b1_ref,
             yap_ref, ya_ref, yb_ref, mg_ref, mix_ref, x1_ref, wl_ref, ws_ref, wo_ref, sems):
        @pl.when(pl.program_id(0) == 0)
        def _():
            _load_weights((wl_hbm, ws_hbm, wo_hbm), (wl_ref, ws_ref, wo_ref), sems)

        for sub in range(tm // ts):
            rows = slice(sub * ts, (sub + 1) * ts)
            yap = (hs_ref[rows, :].astype(F32) * _gelu(gl_ref[rows, :].astype(F32))).astype(BF16)
            yap_ref[rows, :] = yap
            y_a = jnp.dot(yap, wl_ref[...], preferred_element_type=F32)
            ys = ys_ref[rows, :]
            y_b = jnp.concatenate([jnp.dot(ys, ws_ref[q], preferred_element_type=F32) for q in range(nq)], axis=1)
            ya_ref[rows, :] = y_a.astype(BF16)
            yb_ref[rows, :] = y_b.astype(BF16)
            merged = (_sigmoid_t(ga_ref[rows, :].astype(F32)) * y_a
                      + _sigmoid_t(gb_ref[rows, :].astype(F32)) * y_b).astype(BF16)
            mg_ref[rows, :] = merged
            mix = jnp.dot(merged, wo_ref[...], preferred_element_type=F32)
            mix_ref[rows, :] = mix
            xhat, _ = _ln_stats(ALPHA * x_ref[rows, :] + (1.0 + mod_ref[2:3, :]) * mix)
            x1_ref[rows, :] = xhat * g1_ref[...] + b1_ref[...]

    row = lambda width, col: pl.BlockSpec((tm, width), functools.partial(lambda i, k: (i, k), k=col))
    full = lambda shape: pl.BlockSpec(shape, lambda i: (0,) * len(shape))
    return pl.pallas_call(
        body, name="mix_fwd", grid=(t // tm,),
        in_specs=[row(w, 0), row(w, 1), row(d_sgu, 0), row(d, 4), row(d, 5), row(d, 0),
                  pl.BlockSpec((None, 8, d), lambda i: (i // tpb, 0, 0)),
                  _ANY, _ANY, _ANY, full(ln1_g.shape), full(ln1_b.shape)],
        out_specs=[row(w, 0), row(d, 0), row(d, 0), row(d, 0), row(d, 0), row(d, 0)],
        out_shape=[jax.ShapeDtypeStruct((t, w), BF16), jax.ShapeDtypeStruct((t, d), BF16),
                   jax.ShapeDtypeStruct((t, d), BF16), jax.ShapeDtypeStruct((t, d), BF16),
                   jax.ShapeDtypeStruct((t, d), F32), jax.ShapeDtypeStruct((t, d), F32)],
        scratch_shapes=[pltpu.VMEM(w_o_lru_g.shape, BF16), pltpu.VMEM(w_o_sgu_g.shape, BF16),
                        pltpu.VMEM(w_out_g.shape, BF16), pltpu.SemaphoreType.DMA((3,))],
        compiler_params=_cp(("arbitrary",)),
    )(hs, proj, y_sgu, proj, proj, x2, modv, w_o_lru_g, w_o_sgu_g, w_out_g, ln1_g, ln1_b)


def _mlp_fwd(x1, modv, w_up_g, w_down_g, ln2_g, ln2_b, target, nb, seq):
    t, d = x1.shape
    nq, _, ns = w_up_g.shape
    tm = min(TM_MLP, seq)
    ts = min(TS_MLP, tm)
    tpb = seq // tm

    def body(x1_ref, mod_ref, wu_hbm, wd_hbm, g2_ref, b2_ref, tg_ref,
             rl_ref, act_ref, h2_ref, dz2_ref, df_ref, st_ref, pb_ref, wu_s, wd_s, acc, sems):
        i = pl.program_id(0)

        @pl.when(i == 0)
        def _():
            _load_weights((wu_hbm, wd_hbm), (wu_s, wd_s), sems)
            st_ref[...] = jnp.zeros_like(st_ref)

        @pl.when(i % tpb == 0)
        def _():
            pb_ref[...] = jnp.zeros_like(pb_ref)

        for sub in range(tm // ts):
            rows = slice(sub * ts, (sub + 1) * ts)
            x1v = x1_ref[rows, :]
            h2 = (x1v * (1.0 + mod_ref[4:5, :]) + mod_ref[3:4, :]).astype(BF16)
            h2_ref[rows, :] = h2
            for k in range(nq):
                cols = slice(k * ns, (k + 1) * ns)
                r = jnp.maximum(jnp.dot(h2, wu_s[k], preferred_element_type=F32), 0.0)
                act = (r * r).astype(BF16)
                rl_ref[rows, cols] = r.astype(BF16)
                act_ref[rows, cols] = act
                part = jnp.dot(act, wd_s[cols, :], preferred_element_type=F32)
                if k == 0:
                    acc[sub] = part
                else:
                    acc[sub] += part
            f = acc[sub]
            xhat, rstd = _ln_stats(ALPHA * x1v + (1.0 + mod_ref[5:6, :]) * f)
            y = xhat * g2_ref[...] + b2_ref[...]
            err = y - tg_ref[rows, :]
            dy = err * (1.0 / d)
            dz2 = _ln_bwd(dy * g2_ref[...], xhat, rstd)
            dz2_ref[rows, :] = dz2
            df_ref[rows, :] = ((1.0 + mod_ref[5:6, :]) * dz2).astype(BF16)
            st_ref[0:1, :] += _colsum(dy * xhat)
            st_ref[1:2, :] += _colsum(dy)
            st_ref[2:3, :] += (0.5 / d) * jnp.sum(_colsum(err * err), axis=1, keepdims=True)
            pb_ref[0:1, :] += _colsum(dz2 * f)

    tok = lambda i: (i, 0)
    return pl.pallas_call(
        body, name="mlp_fwd", grid=(t // tm,),
        in_specs=[pl.BlockSpec((tm, d), tok), pl.BlockSpec((None, 8, d), lambda i: (i // tpb, 0, 0)), _ANY, _ANY,
                  pl.BlockSpec((1, d), lambda i: (0, 0)), pl.BlockSpec((1, d), lambda i: (0, 0)),
                  pl.BlockSpec((tm, d), tok)],
        out_specs=[pl.BlockSpec((tm, nq * ns), tok), pl.BlockSpec((tm, nq * ns), tok),
                   pl.BlockSpec((tm, d), tok), pl.BlockSpec((tm, d), tok), pl.BlockSpec((tm, d), tok),
                   pl.BlockSpec((8, d), lambda i: (0, 0)), pl.BlockSpec((None, 8, d), lambda i: (i // tpb, 0, 0))],
        out_shape=[jax.ShapeDtypeStruct((t, nq * ns), BF16), jax.ShapeDtypeStruct((t, nq * ns), BF16),
                   jax.ShapeDtypeStruct((t, d), BF16),
                   jax.ShapeDtypeStruct((t, d), F32), jax.ShapeDtypeStruct((t, d), BF16),
                   jax.ShapeDtypeStruct((8, d), F32), jax.ShapeDtypeStruct((nb, 8, d), F32)],
        scratch_shapes=[pltpu.VMEM(w_up_g.shape, BF16), pltpu.VMEM(w_down_g.shape, BF16),
                        pltpu.VMEM((tm // ts, ts, d), F32), pltpu.SemaphoreType.DMA((2,))],
        compiler_params=_cp(("arbitrary",)),
    )(x1, modv, w_up_g, w_down_g, ln2_g, ln2_b, target)


def _mlp_bwd(df, up, w_down_g, w_up_g, dz2, x2, mix, modv, ln1_g, ln1_b, nb, seq):
    t, d = x2.shape
    nq, _, ns = w_up_g.shape
    tm = min(TM_MLP, seq)
    ts = min(TS_MLP, tm)
    tpb = seq // tm

    def body(df_ref, rl_ref, wd_hbm, wu_hbm, dz2_ref, x_ref, mix_ref, mod_ref, g1_ref, b1_ref,
             dup_ref, dz1_ref, dmix_ref, st_ref, pb_ref, wd_s, wu_s, acc, sems):
        i = pl.program_id(0)

        @pl.when(i == 0)
        def _():
            _load_weights((wd_hbm, wu_hbm), (wd_s, wu_s), sems)
            st_ref[...] = jnp.zeros_like(st_ref)

        @pl.when(i % tpb == 0)
        def _():
            pb_ref[...] = jnp.zeros_like(pb_ref)

        for sub in range(tm // ts):
            rows = slice(sub * ts, (sub + 1) * ts)
            dfv = df_ref[rows, :]
            for k in range(nq):
                cols = slice(k * ns, (k + 1) * ns)
                dup = (_mm_nt(dfv, wd_s[cols, :]) * (2.0 * rl_ref[rows, cols].astype(F32))).astype(BF16)
                dup_ref[rows, cols] = dup
                part = _mm_nt(dup, wu_s[k])
                if k == 0:
                    acc[sub] = part
                else:
                    acc[sub] += part
            dh2 = acc[sub]
            mix = mix_ref[rows, :]
            xhat, rstd = _ln_stats(ALPHA * x_ref[rows, :] + (1.0 + mod_ref[2:3, :]) * mix)
            x1 = xhat * g1_ref[...] + b1_ref[...]
            dx1 = ALPHA * dz2_ref[rows, :] + dh2 * (1.0 + mod_ref[4:5, :])
            dz1 = _ln_bwd(dx1 * g1_ref[...], xhat, rstd)
            dz1_ref[rows, :] = dz1
            dmix_ref[rows, :] = ((1.0 + mod_ref[2:3, :]) * dz1).astype(BF16)
            st_ref[0:1, :] += _colsum(dx1 * xhat)
            st_ref[1:2, :] += _colsum(dx1)
            pb_ref[0:1, :] += _colsum(dh2 * x1)
            pb_ref[1:2, :] += _colsum(dh2)
            pb_ref[2:3, :] += _colsum(dz1 * mix)

    tok = lambda i: (i, 0)
    return pl.pallas_call(
        body, name="mlp_bwd", grid=(t // tm,),
        in_specs=[pl.BlockSpec((tm, d), tok), pl.BlockSpec((tm, nq * ns), tok), _ANY, _ANY,
                  pl.BlockSpec((tm, d), tok), pl.BlockSpec((tm, d), tok), pl.BlockSpec((tm, d), tok),
                  pl.BlockSpec((None, 8, d), lambda i: (i // tpb, 0, 0)),
                  pl.BlockSpec((1, d), lambda i: (0, 0)), pl.BlockSpec((1, d), lambda i: (0, 0))],
        out_specs=[pl.BlockSpec((tm, nq * ns), tok),
                   pl.BlockSpec((tm, d), tok), pl.BlockSpec((tm, d), tok),
                   pl.BlockSpec((8, d), lambda i: (0, 0)), pl.BlockSpec((None, 8, d), lambda i: (i // tpb, 0, 0))],
        out_shape=[jax.ShapeDtypeStruct((t, nq * ns), BF16),
                   jax.ShapeDtypeStruct((t, d), F32), jax.ShapeDtypeStruct((t, d), BF16),
                   jax.ShapeDtypeStruct((8, d), F32), jax.ShapeDtypeStruct((nb, 8, d), F32)],
        scratch_shapes=[pltpu.VMEM(w_down_g.shape, BF16), pltpu.VMEM(w_up_g.shape, BF16),
                        pltpu.VMEM((tm // ts, ts, d), F32), pltpu.SemaphoreType.DMA((2,))],
        compiler_params=_cp(("arbitrary",), VMEM_LIMIT_MAX),
    )(df, up, w_down_g, w_up_g, dz2, x2, mix, modv, ln1_g, ln1_b)


def _mix_bwd(dmix, proj, y_a, y_b, hs, w_out_g, w_o_lru_g, w_o_sgu_g, seq, after=()):
    t, d = dmix.shape
    w = hs.shape[1]
    nq, d_sgu, ns = w_o_sgu_g.shape
    tm = min(TM_MIX, seq)
    ts = min(TS_MLP, tm)

    def body(dmix_ref, ga_ref, gb_ref, ya_ref, yb_ref, gl_ref, hs_ref, wo_hbm, wl_hbm, ws_hbm,
             dya_ref, dyb_ref, dg_ref, dyl_ref, dys_ref, wo_ref, wl_ref, ws_ref, sems):
        @pl.when(pl.program_id(0) == 0)
        def _():
            _load_weights((wo_hbm, wl_hbm, ws_hbm), (wo_ref, wl_ref, ws_ref), sems)

        for sub in range(tm // ts):
            rows = slice(sub * ts, (sub + 1) * ts)
            dmerged = _mm_nt(dmix_ref[rows, :], wo_ref[...])
            sa, sb = _sigmoid_t(ga_ref[rows, :].astype(F32)), _sigmoid_t(gb_ref[rows, :].astype(F32))
            dy_a = (dmerged * sa).astype(BF16)
            dy_b = (dmerged * sb).astype(BF16)
            dya_ref[rows, :] = dy_a
            dyb_ref[rows, :] = dy_b
            dg_ref[rows, 4 * d:5 * d] = (dmerged * ya_ref[rows, :].astype(F32) * (sa * (1.0 - sa))).astype(BF16)
            dg_ref[rows, 5 * d:6 * d] = (dmerged * yb_ref[rows, :].astype(F32) * (sb * (1.0 - sb))).astype(BF16)
            dyap = _mm_nt(dy_a, wl_ref[...])
            gel, dgel = _gelu_and_grad(gl_ref[rows, :].astype(F32))
            dyl_ref[rows, :] = (dyap * gel).astype(BF16)
            dg_ref[rows, w:2 * w] = (dyap * hs_ref[rows, :].astype(F32) * dgel).astype(BF16)
            dys = _mm_nt(dy_b[:, 0:ns], ws_ref[0])
            for q in range(1, nq):
                dys = dys + _mm_nt(dy_b[:, q * ns:(q + 1) * ns], ws_ref[q])
            dys_ref[rows, :] = dys

    row = lambda width, col: pl.BlockSpec((tm, width), functools.partial(lambda i, k: (i, k), k=col))
    return pl.pallas_call(
        _ordered(body, 10, after), name="mix_bwd", grid=(t // tm,),
        in_specs=[row(d, 0), row(d, 4), row(d, 5), row(d, 0), row(d, 0), row(w, 1), row(w, 0),
                  _ANY, _ANY, _ANY] + [_ANY] * len(after),
        scratch_shapes=[pltpu.VMEM(w_out_g.shape, BF16), pltpu.VMEM(w_o_lru_g.shape, BF16),
                        pltpu.VMEM(w_o_sgu_g.shape, BF16), pltpu.SemaphoreType.DMA((3,))],
        out_specs=[row(d, 0), row(d, 0), row(6 * d, 0), row(w, 0), row(d_sgu, 0)],
        out_shape=[jax.ShapeDtypeStruct((t, d), BF16), jax.ShapeDtypeStruct((t, d), BF16),
                   jax.ShapeDtypeStruct((t, 6 * d), BF16), jax.ShapeDtypeStruct((t, w), BF16),
                   jax.ShapeDtypeStruct((t, d_sgu), F32)],
        compiler_params=_cp(("arbitrary",)),
    )(dmix, proj, proj, y_a, y_b, proj, hs, w_out_g, w_o_lru_g, w_o_sgu_g, *after)


def _sgu_bwd(proj, dys, w_sp, b_sp_t, ln_g, ln_b, after=()):
    t = proj.shape[0]
    d_sgu = SGU_GROUPS * HEAD
    tm = min(TM_SGU, t)
    nblk = tm // HEAD
    specs, n_piece = _sgu_specs(tm, d_sgu)

    def body(*refs):
        u = jnp.concatenate([r[...] for r in refs[:n_piece]], axis=1).astype(F32)
        v = jnp.concatenate([r[...] for r in refs[n_piece:2 * n_piece]], axis=1).astype(F32)
        dys_ref, w_ref, bt_ref, g_ref, b_ref, du_ref, dv_ref, dw_ref, st_ref, dbt_ref, dvn_s = refs[2 * n_piece:]

        @pl.when(pl.program_id(0) == 0)
        def _():
            dw_ref[...] = jnp.zeros_like(dw_ref)
            st_ref[...] = jnp.zeros_like(st_ref)
            dbt_ref[...] = jnp.zeros_like(dbt_ref)

        ug, dug_du = _gelu_and_grad(u)
        vg, dvg_dv = _gelu_and_grad(v)
        xhat, rstd = _ln_stats(vg)
        vn = (xhat * g_ref[...] + b_ref[...]).astype(BF16)
        dys_v = dys_ref[...]
        mask = _sgu_mask()
        for g in range(SGU_GROUPS):
            wm = jnp.where(mask, w_ref[g], 0.0).astype(BF16)
            cols = slice(g * HEAD, (g + 1) * HEAD)
            dw_g = jnp.zeros((HEAD, HEAD), F32)
            db_g = jnp.zeros((HEAD, 1), F32)
            for n in range(nblk):
                rows = slice(n * HEAD, (n + 1) * HEAD)
                vn_blk = vn[rows, cols]
                mixed = jnp.dot(wm, vn_blk, preferred_element_type=F32) + bt_ref[:, g:g + 1]
                dy_blk = dys_v[rows, cols]
                du_ref[rows, cols] = (dy_blk * mixed * dug_du[rows, cols]).astype(BF16)
                dmx = dy_blk * ug[rows, cols]
                dvn_s[rows, cols] = _mm_tn(wm, dmx)
                dw_g = dw_g + _mm_nt(dmx, vn_blk)
                db_g = db_g + jnp.sum(dmx, axis=1, keepdims=True)
            dw_ref[g] += jnp.where(mask, dw_g, 0.0)
            dbt_ref[:, g:g + 1] += db_g
        dvn = dvn_s[...]
        st_ref[0:1, :] += _colsum(dvn * xhat)
        st_ref[1:2, :] += _colsum(dvn)
        dv_ref[...] = (_ln_bwd(dvn * g_ref[...], xhat, rstd) * dvg_dv).astype(BF16)

    full = lambda shape: pl.BlockSpec(shape, lambda i: (0,) * len(shape))
    tok = pl.BlockSpec((tm, d_sgu), lambda i: (i, 0))
    return pl.pallas_call(
        _ordered(body, 2 * n_piece + 5, after), name="sgu_bwd", grid=(t // tm,),
        in_specs=specs + [tok, full(w_sp.shape), full(b_sp_t.shape), full(ln_g.shape), full(ln_b.shape)]
        + [_ANY] * len(after),
        out_specs=[tok, tok, full(w_sp.shape), full((8, d_sgu)), full((HEAD, HEAD))],
        out_shape=[jax.ShapeDtypeStruct((t, d_sgu), BF16), jax.ShapeDtypeStruct((t, d_sgu), BF16),
                   jax.ShapeDtypeStruct(w_sp.shape, F32), jax.ShapeDtypeStruct((8, d_sgu), F32),
                   jax.ShapeDtypeStruct((HEAD, HEAD), F32)],
        scratch_shapes=[pltpu.VMEM((tm, d_sgu), F32)],
        compiler_params=_cp(("arbitrary",)),
    )(*([proj] * (2 * n_piece)), dys, w_sp, b_sp_t, ln_g, ln_b, *after)


def _lru_bwd(proj, hs, e, dyl, saved, lru_w, nb, seq, dproj, after=()):
    t = proj.shape[0]
    w = LRU_HEADS * HEAD
    w_conv, b_conv, w_a, b_a, w_x, b_x, lam = lru_w

    def body(x_ref, hs_ref, e_ref, dy_ref, a_ref, r_ref, gi_ref, xc_ref, wc_ref, wa_ref, wx_ref, lam_ref,
             dxl_ref, dwa_ref, dwx_ref, st_ref):
        @pl.when(pl.program_id(1) == 0)
        def _():
            dwa_ref[...] = jnp.zeros_like(dwa_ref)
            dwx_ref[...] = jnp.zeros_like(dwx_ref)
            st_ref[...] = jnp.zeros_like(st_ref)

        xl = x_ref[...].astype(F32)
        a, r, gi, xc = a_ref[...], r_ref[...].astype(F32), gi_ref[...].astype(F32), xc_ref[...].astype(F32)
        big_l = _lru_rate(lam_ref)
        m2 = (1.0 - a) * (1.0 + a)
        inv_mult = lax.rsqrt(m2)
        mult = m2 * inv_mult
        dh = dy_ref[...].astype(F32) + _shift_up(e_ref[...].astype(F32), 1)
        da = dh * _shift_down(hs_ref[...].astype(F32), 1)
        dmult = dh * (gi * xc)
        d_i = dh * (mult * xc)
        dxc = dh * (mult * gi)
        dla = a * (da - dmult * (a * inv_mult))
        dr = dla * big_l
        d_big_l = _colsum(dla * r)
        dra = dr * (r * (1.0 - r))
        dia = d_i * (gi * (1.0 - gi))
        dwa_ref[...] += _mm_tn(xc, dra)
        dwx_ref[...] += _mm_tn(xc, dia)
        dxc = dxc + _mm_nt(dra, wa_ref[...]) + _mm_nt(dia, wx_ref[...])
        dxl = wc_ref[CONV_WIDTH - 1:CONV_WIDTH, :] * dxc
        st_ref[4 + CONV_WIDTH - 1:4 + CONV_WIDTH, :] += _colsum(dxc * xl)
        for k in range(CONV_WIDTH - 1):
            ahead = _shift_up(dxc, CONV_WIDTH - 1 - k)
            dxl = dxl + wc_ref[k:k + 1, :] * ahead
            st_ref[4 + k:5 + k, :] += _colsum(ahead * xl)
        dxl_ref[...] = dxl.astype(BF16)
        st_ref[0:1, :] += _colsum(dra)
        st_ref[1:2, :] += _colsum(dia)
        st_ref[2:3, :] += d_big_l * (LRU_C * _sigmoid(-lam_ref[...]))
        st_ref[3:4, :] += _colsum(dxc)

    col = lambda hd, b: (0, hd)
    head = lambda hd, b: (hd, 0, 0)
    tok = lambda hd, b: (b, hd)
    seq_blk = pl.BlockSpec((seq, HEAD), tok)
    return pl.pallas_call(
        _ordered(body, 12, (dproj,) + tuple(after)), name="lru_bwd", grid=(LRU_HEADS, nb),
        in_specs=[seq_blk] * 8 + [pl.BlockSpec((CONV_WIDTH, HEAD), col), pl.BlockSpec((None, HEAD, HEAD), head),
                                  pl.BlockSpec((None, HEAD, HEAD), head), pl.BlockSpec((1, HEAD), col)]
        + [_ANY] * (1 + len(after)),
        out_specs=[seq_blk, pl.BlockSpec((None, HEAD, HEAD), head), pl.BlockSpec((None, HEAD, HEAD), head),
                   pl.BlockSpec((8, HEAD), col)],
        out_shape=[jax.ShapeDtypeStruct(dproj.shape, BF16), jax.ShapeDtypeStruct((LRU_HEADS, HEAD, HEAD), F32),
                   jax.ShapeDtypeStruct((LRU_HEADS, HEAD, HEAD), F32), jax.ShapeDtypeStruct((8, w), F32)],
        input_output_aliases={12: 0},
        compiler_params=_cp(("arbitrary", "arbitrary")),
    )(proj, hs, e, dyl, *saved, w_conv, w_a, w_x, lam, dproj, *after)


def _weight_grad(a, g, col_shards, name, after=()):
    t, k = a.shape
    n = g.shape[1]
    tt = min(TT_DW, t)
    tk = k if k <= 1536 else 1024
    ns = n // N_CHIPS if col_shards else n
    narrow = col_shards and ns < 512
    tn = n if narrow else min(ns, 768 if ns % 768 == 0 else 1024)
    while ns % tn and not narrow:
        tn //= 2
    per = max(ns // tn, 1)

    def body(a_ref, g_ref, o_ref):
        @pl.when(pl.program_id(2) == 0)
        def _():
            o_ref[...] = jnp.zeros_like(o_ref)

        res = _mm_tn(a_ref[...], g_ref[...])
        if narrow:
            for q in range(N_CHIPS):
                o_ref[q] += res[:, q * ns:(q + 1) * ns]
        else:
            o_ref[...] += res

    if narrow:
        out_spec = pl.BlockSpec((N_CHIPS, tk, ns), lambda i, j, s: (0, i, 0))
        out_shape = jax.ShapeDtypeStruct((N_CHIPS, k, ns), F32)
    elif col_shards:
        out_spec = pl.BlockSpec((None, tk, tn), lambda i, j, s: (j // per, i, j % per))
        out_shape = jax.ShapeDtypeStruct((N_CHIPS, k, ns), F32)
    else:
        out_spec = pl.BlockSpec((tk, tn), lambda i, j, s: (i, j))
        out_shape = jax.ShapeDtypeStruct((k, n), F32)
    return pl.pallas_call(
        _ordered(body, 2, after), name=name, grid=(k // tk, n // tn, t // tt),
        in_specs=[pl.BlockSpec((tt, tk), lambda i, j, s: (s, i)), pl.BlockSpec((tt, tn), lambda i, j, s: (s, j))]
        + [_ANY] * len(after),
        out_specs=out_spec, out_shape=out_shape,
        compiler_params=_cp(("arbitrary", "arbitrary", "arbitrary")),
    )(a, g, *after)


def _input_grad(dproj, ws, slots, dz1, x2, modv, nb, seq, after=()):
    t, d = x2.shape
    nq = len(ws)
    ns = ws[0].shape[1]
    tm = min(TM_DH, seq)
    ts = min(TS_MLP, tm)
    tpb = seq // tm

    def body(slot_ref, dp_ref, *refs):
        w_hbm = refs[:nq]
        dz1_ref, x_ref, mod_ref, gx_ref, db_ref, pb_ref, w_s, acc, sems = refs[nq:]
        i = pl.program_id(0)

        @pl.when(i == 0)
        def _():
            _load_weights(w_hbm, [w_s.at[slot_ref[k]] for k in range(nq)], sems)
            db_ref[...] = jnp.zeros_like(db_ref)

        @pl.when(i % tpb == 0)
        def _():
            pb_ref[...] = jnp.zeros_like(pb_ref)

        for sub in range(tm // ts):
            rows = slice(sub * ts, (sub + 1) * ts)
            for q in range(nq):
                dp = dp_ref[rows, q * ns:(q + 1) * ns]
                part = _mm_nt(dp, w_s[q])
                if q == 0:
                    acc[sub] = part
                else:
                    acc[sub] += part
                db_ref[q, 0:1, :] += _colsum(dp.astype(F32))
            dh = acc[sub]
            gx_ref[rows, :] = ALPHA * dz1_ref[rows, :] + dh * (1.0 + mod_ref[1:2, :])
            pb_ref[0:1, :] += _colsum(dh * x_ref[rows, :])
            pb_ref[1:2, :] += _colsum(dh)

    tok = lambda i, s: (i, 0)
    in_specs = [pl.BlockSpec((tm, nq * ns), tok)] + [_ANY] * nq
    in_specs += [pl.BlockSpec((tm, d), tok), pl.BlockSpec((tm, d), tok),
                 pl.BlockSpec((None, 8, d), lambda i, s: (i // tpb, 0, 0))] + [_ANY] * len(after)
    return pl.pallas_call(
        _ordered(body, 5 + nq, after), name="input_grad",
        grid_spec=pltpu.PrefetchScalarGridSpec(
            num_scalar_prefetch=1, grid=(t // tm,), in_specs=in_specs,
            out_specs=[pl.BlockSpec((tm, d), tok), pl.BlockSpec((nq, 8, ns), lambda i, s: (0, 0, 0)),
                       pl.BlockSpec((None, 8, d), lambda i, s: (i // tpb, 0, 0))],
            scratch_shapes=[pltpu.VMEM((nq, d, ns), BF16), pltpu.VMEM((tm // ts, ts, d), F32),
                            pltpu.SemaphoreType.DMA((nq,))]),
        out_shape=[jax.ShapeDtypeStruct((t, d), F32), jax.ShapeDtypeStruct((nq, 8, ns), F32),
                   jax.ShapeDtypeStruct((nb, 8, d), F32)],
        compiler_params=_cp(("arbitrary",)),
    )(slots, dproj, *ws, dz1, x2, modv, *after)


def _rows128(v):
    flat = v.reshape(-1, HEAD)
    pad = (-flat.shape[0]) % 8
    return jnp.pad(flat, ((0, pad), (0, 0))) if pad else flat


def kernel(x, c, w_ada, b_ada, w_in, b_in, w_conv, b_conv, w_rg_a, b_rg_a, w_rg_x, b_rg_x, lru_lambda, w_sp, b_sp, ln_v_g, ln_v_b, w_o_lru, w_o_sgu, w_out, ln1_g, ln1_b, w_up, w_down, ln2_g, ln2_b, loss_target, m_w_ada, m_b_ada, m_w_in, m_b_in, m_w_conv, m_b_conv, m_w_rg_a, m_b_rg_a, m_w_rg_x, m_b_rg_x, m_lru_lambda, m_w_sp, m_b_sp, m_ln_v_g, m_ln_v_b, m_w_o_lru, m_w_o_sgu, m_w_out, m_ln1_g, m_ln1_b, m_w_up, m_w_down, m_ln2_g, m_ln2_b, v_w_ada, v_b_ada, v_w_in, v_b_in, v_w_conv, v_b_conv, v_w_rg_a, v_b_rg_a, v_w_rg_x, v_b_rg_x, v_lru_lambda, v_w_sp, v_b_sp, v_ln_v_g, v_ln_v_b, v_w_o_lru, v_w_o_sgu, v_w_out, v_ln1_g, v_ln1_b, v_w_up, v_w_down, v_ln2_g, v_ln2_b):
    given = dict(locals())
    nb, seq, d = x.shape
    t = nb * seq
    w_lru = LRU_HEADS * HEAD
    d_sgu = SGU_GROUPS * HEAD
    xi, yi, ci = lax.axis_index("x"), lax.axis_index("y"), lax.axis_index("c")
    chip = 2 * xi + yi
    dev = 2 * chip + ci
    cidx = jnp.reshape(ci, (1,)).astype(jnp.int32)

    x2 = x.reshape(t, d)
    target = loss_target.reshape(t, d)

    big = ["w_in", "w_o_lru", "w_o_sgu", "w_out", "w_up", "w_down"]
    shards_a = [w_in[0].astype(BF16)]
    shards_b = [given[n][0].astype(BF16) for n in big[1:]]
    pidx = jnp.reshape(chip, (1,)).astype(jnp.int32)

    c_rows = _rows128(c)
    wconv_rows = _rows128(w_conv[0])
    slab0 = _all_gather_small(jnp.concatenate([c_rows, wconv_rows], axis=0), "gather_c_wconv")
    slab0 = slab0.reshape(N_DEV, -1, HEAD)
    c_all = slab0[:, :c_rows.shape[0]].reshape(N_DEV * nb, d)
    n_wc = CONV_WIDTH * (w_lru // N_CHIPS) // HEAD
    wc = slab0[0::2, c_rows.shape[0]:c_rows.shape[0] + n_wc].reshape(N_CHIPS, CONV_WIDTH, w_lru // N_CHIPS)
    w_conv_full = jnp.transpose(wc, (1, 0, 2)).reshape(CONV_WIDTH, w_lru)

    n_ada = w_ada.shape[2]
    b_ada_cols = lax.dynamic_slice(b_ada, (0, chip * n_ada), (1, n_ada))
    mod_cols = _ada_fwd(c_all, w_ada[0], b_ada_cols)
    half = (N_DEV * nb) // 2
    mod_half = lax.dynamic_slice(mod_cols, (ci * half, 0), (half, n_ada))
    mod_g = _all_gather_small(mod_half, "gather_mod").reshape(N_CHIPS, 2, half, n_ada)
    mod_all = jnp.transpose(mod_g, (1, 2, 0, 3)).reshape(N_DEV * nb, N_CHIPS * n_ada)
    mod_loc = lax.dynamic_slice(mod_all, (dev * nb, 0), (nb, N_CHIPS * n_ada)).reshape(nb, 6, d)
    modv = jnp.pad(mod_loc, ((0, 0), (0, 2), (0, 0)))

    lru_w = (w_conv_full, b_conv, w_rg_a[0], b_rg_a, w_rg_x[0], b_rg_x, lru_lambda)
    b_sp_t = jnp.transpose(b_sp[0])

    land = lambda s: jax.ShapeDtypeStruct((N_CHIPS,) + s.shape, s.dtype)
    sds = lambda s: jax.ShapeDtypeStruct(s.shape, s.dtype)
    started_a = _split_start(shards_a, [sds(shards_a[0])] * 2, _peer_gather_copies((0, 1)), 2, "gather_w_in_near_start",
                             after=(modv,))
    shards_b, shards_c = shards_b[:3], shards_b[3:]

    ids = lambda *v: jnp.stack(v).astype(jnp.int32)
    modv_t = modv + started_a[-1][0:1, 0:1]
    proj, h = _proj_fwd(x2, modv_t, [started_a[2]], ids(chip), b_in, seq, "proj_fwd_own")
    own_a, lands_a = _split_wait(started_a, 1, _peer_gather_copies((0, 1)), "gather_w_in_near_wait",
                                 after=(proj, *shards_b, *shards_c))
    started_f = _split_start(own_a, [sds(own_a[0])], _far_gather_copies, 1, "gather_w_in_far_start", after=(lands_a[0],))
    started_b = _split_start(shards_b, [land(s) for s in shards_b], _gather_copies, 3 * len(shards_b),
                             "gather_w_mix_start", after=(started_f[-1],))
    started_c = _split_start(shards_c, [land(s) for s in shards_c], _gather_copies, 3 * len(shards_c),
                             "gather_w_mlp_start", after=(started_b[-1],))
    modv_t = modv + started_c[-1][0:1, 0:1]
    (proj,) = _proj_fwd(x2, modv_t, lands_a, ids(chip ^ 1, chip ^ 2), b_in, seq, "proj_fwd_near", proj_in=proj)
    own_a, land_f = _split_wait(started_f, 1, _far_gather_copies, "gather_w_in_far_wait", after=(proj,))
    (proj,) = _proj_fwd(x2, modv, land_f, ids(chip ^ 3), b_in, seq, "proj_fwd_far", proj_in=proj)
    w_in_shards, w_in_chips = own_a + lands_a + land_f, ids(chip, chip ^ 1, chip ^ 2, chip ^ 3)
    a, inp, r16, gi16, xc16 = _lru_prep(proj, lru_w, nb, seq)
    a3 = a.reshape(nb, seq, w_lru)
    hs = _scan(a3, inp.reshape(nb, seq, w_lru), False, "lru_scan", BF16).reshape(t, w_lru)
    y_sgu = _sgu_fwd(proj, w_sp[0], b_sp_t, ln_v_g, ln_v_b)
    shards_b, lands_b = _split_wait(started_b, len(shards_b), _gather_copies, "gather_w_mix_wait", after=(hs, y_sgu))
    w_o_lru_g, w_o_sgu_g, w_out_g = _fill_own_slot(lands_b, shards_b, pidx, ["own_" + n for n in big[1:4]])
    w_o_lru_g = w_o_lru_g.reshape(w_lru, d)
    w_out_g = w_out_g.reshape(d, d)
    yap, y_a, y_b, merged, mix, x1 = _mix_fwd(hs, proj, y_sgu, x2, modv, w_o_lru_g, w_o_sgu_g, w_out_g, ln1_g, ln1_b, seq)
    shards_c, lands_c = _split_wait(started_c, len(shards_c), _gather_copies, "gather_w_mlp_wait", after=(x1,))
    w_up_g, w_down_g = _fill_own_slot(lands_c, shards_c, pidx, ["own_" + n for n in big[4:]])
    w_down_g = w_down_g.reshape(-1, d)
    up, act, h2, dz2, df, st2, pb2 = _mlp_fwd(x1, modv, w_up_g, w_down_g, ln2_g, ln2_b, target, nb, seq)

    part = {}

    def to_sibling_start(group, tag, after=()):
        g4 = []
        for n in group:
            shard = given[n].shape[1:]
            g4.append(part[n].reshape(N_CHIPS, 2, shard[0] // 2, shard[1]))
        shapes = [jax.ShapeDtypeStruct((N_CHIPS,) + g.shape[2:], F32) for g in g4]
        return _split_start(g4, shapes, _to_sibling_copies, len(g4), "grads_to_sibling_start_" + tag, after)

    def to_chips_start(group, started, tag, after=()):
        g4, recv = _split_wait(started, len(group), _to_sibling_copies, "grads_to_sibling_wait_" + tag, after)
        own4 = [_add_own_half(g4[k], recv[k], cidx, "grad_pair_sum_" + n) for k, n in enumerate(group)]
        shapes = [jax.ShapeDtypeStruct((3,) + o.shape[1:], BF16) for o in own4]
        return _split_start(own4, shapes, _chip_exchange_copies, 3 * len(own4), "grads_chip_exchange_start_" + tag)

    def chips_finish(group, started, tag, after=()):
        own4, slots = _split_wait(started, len(group), _chip_exchange_copies, "grads_chip_exchange_wait_" + tag, after)
        return [_sum_own_and_peers(own4[k], slots[k], pidx, "grad_chip_sum_" + n) for k, n in enumerate(group)]

    dup, dz1, dmix, st1, pb1 = _mlp_bwd(df, up, w_down_g, w_up_g, dz2, x2, mix, modv, ln1_g, ln1_b, nb, seq)
    group1 = ["w_up", "w_down"]
    part["w_up"] = _weight_grad(h2, dup, True, "grad_w_up")
    part["w_down"] = _weight_grad(act, df, False, "grad_w_down")
    sib1 = to_sibling_start(group1, "mlp")
    dy_a, dy_b, dproj, dyl, dys = _mix_bwd(dmix, proj, y_a, y_b, hs, w_out_g, w_o_lru_g, w_o_sgu_g, seq,
                                                after=(sib1[-1],))
    group2 = ["w_o_lru", "w_o_sgu", "w_out"]
    part["w_o_lru"] = _weight_grad(yap, dy_a, False, "grad_w_o_lru")
    part["w_o_sgu"] = _weight_grad(y_sgu, dy_b, True, "grad_w_o_sgu")
    part["w_out"] = _weight_grad(merged, dmix, False, "grad_w_out")
    chips1 = to_chips_start(group1, sib1, "mlp", after=(dys, part["w_o_lru"], part["w_o_sgu"], part["w_out"]))
    sib2 = to_sibling_start(group2, "mix", after=(chips1[-1],))
    du, dv, g_w_sp, st_sgu, g_b_sp_t = _sgu_bwd(proj, dys, w_sp[0], b_sp_t, ln_v_g, ln_v_b, after=(sib2[-1],))
    dyl3 = dyl.reshape(nb, seq, w_lru)
    e = _scan(a3, dyl3, True, "lru_scan_bwd", BF16).reshape(t, w_lru)
    chips2 = to_chips_start(group2, sib2, "mix", after=(e, du))
    dproj = lax.dynamic_update_slice(dproj, du, (0, 2 * w_lru))
    dproj = lax.dynamic_update_slice(dproj, dv, (0, 2 * w_lru + d_sgu))
    dproj, g_w_rg_a, g_w_rg_x, st_lru = _lru_bwd(proj, hs, e, dyl, (a, r16, gi16, xc16), lru_w, nb, seq, dproj,
                                                 after=(chips2[-1],))

    didx = jnp.reshape(dev, (1,)).astype(jnp.int32)
    early = [
        ("w_conv", st_lru[4:8]), ("b_conv", st_lru[3]), ("w_rg_a", g_w_rg_a), ("b_rg_a", st_lru[0]),
        ("w_rg_x", g_w_rg_x), ("b_rg_x", st_lru[1]), ("lru_lambda", st_lru[2]), ("w_sp", g_w_sp),
        ("b_sp", jnp.transpose(g_b_sp_t[:, :SGU_GROUPS])), ("ln_v_g", st_sgu[0]), ("ln_v_b", st_sgu[1]),
        ("ln1_g", st1[0]), ("ln1_b", st1[1]), ("ln2_g", st2[0]), ("ln2_b", st2[1]),
        ("loss", st2[2:3, 0:HEAD]),
    ]
    pieces_e = [_rows128(v) for _, v in early]
    slab_e = jnp.concatenate(pieces_e, axis=0)
    slab_e = jnp.pad(slab_e, ((0, (-slab_e.shape[0]) % TR_EW), (0, 0)))
    small_st = _split_start([slab_e], [jax.ShapeDtypeStruct((N_DEV,) + slab_e.shape, F32)], _all_devices_copies, N_DEV - 1,
                            "small_grads_start")

    group3 = ["w_in"]
    part["w_in"] = _weight_grad(h, dproj, True, "grad_w_in", after=(small_st[-1],))
    sib3 = to_sibling_start(group3, "in")
    halves12 = (chips_finish(group1, chips1, "mlp", after=(sib3[-1],))
                + chips_finish(group2, chips2, "mix", after=(sib3[-1],)))
    swap12 = _split_start(halves12, [jax.ShapeDtypeStruct(hv.shape, F32) for hv in halves12], _swap_copies, len(halves12),
                          "grads_swap_start")
    chips3 = to_chips_start(group3, sib3, "in", after=(swap12[-1],))
    grad_x2, g_b_in4, pb0 = _input_grad(dproj, w_in_shards, w_in_chips, dz1, x2, modv, nb, seq, after=(chips3[-1],))
    grads = {}
    two_d = lambda v: v.reshape(-1, v.shape[-1])
    done = {}

    def adamw_big(n, mine_n, theirs_n):
        done[n] = _adamw_halves(two_d(given[n]), mine_n, theirs_n, two_d(given["m_" + n]), two_d(given["v_" + n]), cidx,
                                "adamw_" + n)

    dmod_loc = jnp.stack([pb0[:, 1], pb0[:, 0], pb1[:, 2], pb1[:, 1], pb1[:, 0], pb2[:, 0]], axis=1)
    rows_dmod = dmod_loc.size // HEAD
    slab_l = jnp.concatenate([_rows128(dmod_loc), _rows128(g_b_in4[:, 0])], axis=0)
    late_st = _split_start([slab_l], [jax.ShapeDtypeStruct((N_DEV,) + slab_l.shape, F32)], _all_devices_copies, N_DEV - 1,
                           "late_grads_start")
    mine12, theirs12 = _split_wait(swap12, len(halves12), _swap_copies, "grads_swap_wait", after=(late_st[-1],))
    for n, mine_n, theirs_n in zip(group1 + group2, mine12, theirs12):
        adamw_big(n, mine_n, theirs_n)
    (slab_l,), (lands_l,) = _split_wait(late_st, 1, _all_devices_copies, "late_grads_wait",
                                        after=tuple(done[n][1] for n in group1 + group2))
    every = jnp.where(lax.broadcasted_iota(jnp.int32, (N_DEV, 1, 1), 0) == dev, slab_l[None], lands_l)
    dmod_all = every[:, :rows_dmod].reshape(N_DEV * nb, 6 * d)
    grads["b_in"] = _sum_slots(every[:, rows_dmod:], "grad_b_in_sum").reshape(1, -1)

    (slab_e,), (lands_e,) = _split_wait(small_st, 1, _all_devices_copies, "small_grads_wait", after=(dmod_all,))
    summed = _sum_devices(lands_e, slab_e, didx, "small_grad_sum")
    off = 0
    for (n, v), piece in zip(early, pieces_e):
        grads[n] = summed[off:off + v.size // HEAD].reshape(v.shape)
        off += piece.shape[0]
    loss = grads.pop("loss")[0, 0]

    (mine3,) = chips_finish(group3, chips3, "in", after=(summed,))
    (theirs3,) = _exchange([mine3], [jax.ShapeDtypeStruct(mine3.shape, F32)], _swap_copies, 1, "grads_swap_w_in")
    adamw_big("w_in", mine3, theirs3)

    dmod_cols = lax.dynamic_slice(dmod_all, (0, chip * n_ada), (N_DEV * nb, n_ada))
    grads["w_ada"], grads["b_ada"] = _ada_bwd(c_all, dmod_all, dmod_cols)
    n_wcs = w_lru // N_CHIPS
    grads["w_conv"] = lax.dynamic_slice(grads["w_conv"], (0, chip * n_wcs), (CONV_WIDTH, n_wcs))

    names = ['w_ada', 'b_ada', 'w_in', 'b_in', 'w_conv', 'b_conv', 'w_rg_a', 'b_rg_a', 'w_rg_x', 'b_rg_x', 'lru_lambda',
             'w_sp', 'b_sp', 'ln_v_g', 'ln_v_b', 'w_o_lru', 'w_o_sgu', 'w_out', 'ln1_g', 'ln1_b', 'w_up', 'w_down',
             'ln2_g', 'ln2_b']
    small_names = [n for n in names if n not in big and n != "w_ada"]
    small_out = _adamw_many([(two_d(given[n]), two_d(grads[n].reshape(given[n].shape)), two_d(given["m_" + n]),
                              two_d(given["v_" + n])) for n in small_names], "adamw_small")
    for n, res in zip(small_names, small_out):
        done[n] = (grads[n],) + tuple(res)
    done["w_ada"] = (grads["w_ada"],) + tuple(_adamw(two_d(given["w_ada"]), two_d(grads["w_ada"]), two_d(given["m_w_ada"]),
                                                     two_d(given["v_w_ada"]), "adamw_w_ada"))
    outs = [[done[n][k].reshape(given[n].shape) for n in names] for k in range(4)]
    return (loss, grad_x2.reshape(nb, seq, d), *outs[0], *outs[1], *outs[2], *outs[3])
```

```python
import functools
import math

import jax
import jax.numpy as jnp
from jax import lax
from jax.experimental import pallas as pl
from jax.experimental.pallas import tpu as pltpu

F32 = jnp.float32
BF16 = jnp.bfloat16
MESH = pl.DeviceIdType.MESH

N_CHIPS = 4
N_DEV = 8
LRU_HEADS = 10
HEAD = 128
SGU_GROUPS = 6
SGU_CHUNK = 64
CONV_WIDTH = 4
LRU_C = 8.0
ALPHA = 2.0 ** 0.25
LN_EPS = 1e-5
ADAM_LR, ADAM_B1, ADAM_B2, ADAM_EPS, ADAM_WD, ADAM_STEP = 0.001, 0.9, 0.999, 1e-08, 0.01, 10

VMEM_LIMIT = 56 * 1024 * 1024
VMEM_LIMIT_MAX = 62 * 1024 * 1024
TM_PROJ = 1024
TM_MIX = 512
TM_MLP = 512
TS_MLP = 256
TM_SGU = 512
TM_DH = 512
TT_DW = 4096
TC_SCAN = 256
TR_EW = 256


def _cp(sem=None, limit=None):
    return pltpu.CompilerParams(dimension_semantics=sem, vmem_limit_bytes=limit or VMEM_LIMIT)


def _mm(a, b):
    return jnp.dot(a.astype(BF16), b.astype(BF16), preferred_element_type=F32)


def _mm_nt(a, b):
    return lax.dot_general(a.astype(BF16), b.astype(BF16), (((1,), (1,)), ((), ())), preferred_element_type=F32)


def _mm_tn(a, b):
    return lax.dot_general(a.astype(BF16), b.astype(BF16), (((0,), (0,)), ((), ())), preferred_element_type=F32)


def _sigmoid(x):
    return 1.0 / (1.0 + jnp.exp(-x))


def _sigmoid_t(x):
    return 0.5 * jnp.tanh(0.5 * x) + 0.5


_GELU_K = math.sqrt(2.0 / math.pi)


def _gelu(x):
    t = jnp.tanh(_GELU_K * (x + 0.044715 * (x * x * x)))
    return 0.5 * x * (1.0 + t)


def _gelu_and_grad(x):
    x2 = x * x
    t = jnp.tanh(_GELU_K * (x + 0.044715 * (x2 * x)))
    g = 0.5 * x * (1.0 + t)
    dg = 0.5 * (1.0 + t) + 0.5 * x * (1.0 - t * t) * (_GELU_K * (1.0 + 3.0 * 0.044715 * x2))
    return g, dg


def _ln_stats(z):
    mu = jnp.mean(z, axis=-1, keepdims=True)
    zc = z - mu
    var = jnp.mean(zc * zc, axis=-1, keepdims=True)
    rstd = lax.rsqrt(var + LN_EPS)
    return zc * rstd, rstd


def _ln_bwd(dxh, xhat, rstd):
    m1 = jnp.mean(dxh, axis=-1, keepdims=True)
    m2 = jnp.mean(dxh * xhat, axis=-1, keepdims=True)
    return rstd * (dxh - m1 - xhat * m2)


def _colsum(v):
    return jnp.sum(v, axis=0, keepdims=True)


def _shift_down(v, j):
    if j == 0:
        return v
    rows = lax.broadcasted_iota(jnp.int32, v.shape, 0)
    return jnp.where(rows >= j, pltpu.roll(v, j, 0), 0.0)


def _shift_up(v, j):
    if j == 0:
        return v
    n = v.shape[0]
    rows = lax.broadcasted_iota(jnp.int32, v.shape, 0)
    return jnp.where(rows < n - j, pltpu.roll(v, n - j, 0), 0.0)


def _load_weights(srcs, dsts, sems):
    cps = [pltpu.make_async_copy(s, dd, sems.at[k]) for k, (s, dd) in enumerate(zip(srcs, dsts))]
    for cp in cps:
        cp.start()
    for cp in cps:
        cp.wait()


def _my_pos():
    return lax.axis_index("x"), lax.axis_index("y"), lax.axis_index("c")


def _all_gather_small(v, name, after=()):
    m_per, n = v.shape

    def body(x_ref, out_ref, send_sems, recv_sems, local_sem):
        x, y, c = _my_pos()
        me, sibling = (x, y, c), (x, y, 1 - c)
        chips = [(1 - x, y), (x, 1 - y), (1 - x, 1 - y)]

        def rows(px, py, pc):
            return out_ref.at[pl.ds((4 * px + 2 * py + pc) * m_per, m_per), :]

        def copy(k, block, to, src=None):
            return pltpu.make_async_remote_copy(
                src_ref=rows(*block) if src is None else src, dst_ref=rows(*block),
                send_sem=send_sems.at[k], recv_sem=recv_sems.at[k], device_id=to, device_id_type=MESH)

        mine = pltpu.make_async_copy(x_ref, rows(*me), local_sem)
        mine.start()
        first = [copy(0, me, sibling, src=x_ref)]
        first += [copy(1 + j, me, (*chip, c), src=x_ref) for j, chip in enumerate(chips)]
        for cp in first:
            cp.start()
        passed = [copy(4 + j, (*chip, c), sibling) for j, chip in enumerate(chips)]
        for j, chip in enumerate(chips):
            copy(1 + j, (*chip, c), me).wait_recv()
            passed[j].start()
        copy(0, sibling, me).wait_recv()
        for j, chip in enumerate(chips):
            copy(4 + j, (*chip, 1 - c), me).wait_recv()
        for cp in first + passed:
            cp.wait_send()
        mine.wait()

    return pl.pallas_call(
        _ordered(body, 1, after), name=name,
        out_shape=jax.ShapeDtypeStruct((N_DEV * m_per, n), v.dtype),
        in_specs=[pl.BlockSpec(memory_space=pltpu.VMEM)] + [pl.BlockSpec(memory_space=pl.ANY)] * len(after),
        out_specs=pl.BlockSpec(memory_space=pltpu.VMEM),
        scratch_shapes=[pltpu.SemaphoreType.DMA((7,)), pltpu.SemaphoreType.DMA((7,)), pltpu.SemaphoreType.DMA],
        compiler_params=pltpu.CompilerParams(vmem_limit_bytes=VMEM_LIMIT),
    )(v, *after)


_HBM = pl.BlockSpec(memory_space=pltpu.HBM)
_ANY = pl.BlockSpec(memory_space=pl.ANY)
_SEM = pl.BlockSpec(memory_space=pltpu.SEMAPHORE)
_EFFECT = pltpu.SideEffectType.DATAFLOW_SIDE_EFFECTING


def _ordered(body, n_in, after):
    k = len(after)
    if not k:
        return body
    return lambda *refs: body(*refs[:n_in], *refs[n_in + k:])


def _gather_copies(ins, lands, send_sems, recv_sems):
    x, y, c = _my_pos()
    p = 2 * x + y
    peers = [(x, 1 - y), (1 - x, y), (1 - x, 1 - y)]
    sends, recvs = [], []
    for k in range(len(ins)):
        for j, (qx, qy) in enumerate(peers):
            sems = dict(send_sem=send_sems.at[3 * k + j], recv_sem=recv_sems.at[3 * k + j],
                        device_id=(qx, qy, c), device_id_type=MESH)
            sends.append(pltpu.make_async_remote_copy(src_ref=ins[k], dst_ref=lands[k].at[p], **sems))
            recvs.append(pltpu.make_async_remote_copy(src_ref=ins[k], dst_ref=lands[k].at[2 * qx + qy], **sems))
    return sends, recvs


def _peer_gather_copies(peers):
    def copies(ins, lands, send_sems, recv_sems):
        x, y, c = _my_pos()
        where = [(x, 1 - y), (1 - x, y), (1 - x, 1 - y)]
        cps = [pltpu.make_async_remote_copy(
            src_ref=ins[0], dst_ref=lands[j], send_sem=send_sems.at[j], recv_sem=recv_sems.at[j],
            device_id=(*where[j], c), device_id_type=MESH) for j in peers]
        return cps, cps
    return copies


def _far_gather_copies(ins, lands, send_sems, recv_sems):
    x, y, c = _my_pos()
    cps = [pltpu.make_async_remote_copy(
        src_ref=ins[0], dst_ref=lands[0], send_sem=send_sems.at[0], recv_sem=recv_sems.at[0],
        device_id=(1 - x, 1 - y, c), device_id_type=MESH)]
    return cps, cps


def _to_sibling_copies(ins, lands, send_sems, recv_sems):
    x, y, c = _my_pos()
    cps = [pltpu.make_async_remote_copy(
        src_ref=ins[k].at[:, 1 - c], dst_ref=lands[k], send_sem=send_sems.at[k], recv_sem=recv_sems.at[k],
        device_id=(x, y, 1 - c), device_id_type=MESH) for k in range(len(ins))]
    return cps, cps


def _chip_exchange_copies(ins, lands, send_sems, recv_sems):
    x, y, c = _my_pos()
    peers = [(x, 1 - y), (1 - x, y), (1 - x, 1 - y)]
    cps = []
    for k in range(len(ins)):
        for j, (qx, qy) in enumerate(peers):
            cps.append(pltpu.make_async_remote_copy(
                src_ref=ins[k].at[2 * qx + qy], dst_ref=lands[k].at[j], send_sem=send_sems.at[3 * k + j],
                recv_sem=recv_sems.at[3 * k + j], device_id=(qx, qy, c), device_id_type=MESH))
    return cps, cps


def _all_devices_copies(ins, lands, send_sems, recv_sems):
    x, y, c = _my_pos()
    me = 4 * x + 2 * y + c
    sends, recvs = [], []
    for r in range(1, N_DEV):
        px = 1 - x if r & 4 else x
        py = 1 - y if r & 2 else y
        pc = 1 - c if r & 1 else c
        sems = dict(send_sem=send_sems.at[r - 1], recv_sem=recv_sems.at[r - 1], device_id=(px, py, pc), device_id_type=MESH)
        sends.append(pltpu.make_async_remote_copy(src_ref=ins[0], dst_ref=lands[0].at[me], **sems))
        recvs.append(pltpu.make_async_remote_copy(src_ref=ins[0], dst_ref=lands[0].at[4 * px + 2 * py + pc], **sems))
    return sends, recvs


def _swap_copies(ins, lands, send_sems, recv_sems):
    x, y, c = _my_pos()
    cps = [pltpu.make_async_remote_copy(
        src_ref=ins[k], dst_ref=lands[k], send_sem=send_sems.at[k], recv_sem=recv_sems.at[k],
        device_id=(x, y, 1 - c), device_id_type=MESH) for k in range(len(ins))]
    return cps, cps


def _split_start(ins, land_shapes, copies, n_sems, name, after=()):
    n, nl = len(ins), len(land_shapes)
    first_out = n + nl + len(after)

    def body(*refs):
        in_refs, land_refs = refs[:n], refs[n:n + nl]
        send_sems, recv_sems = refs[first_out:first_out + 2]
        token = refs[-1]
        sends, _ = copies(in_refs, land_refs, send_sems, recv_sems)
        for cp in sends:
            cp.start()
        token[...] = jnp.zeros_like(token)

    lands = [pltpu.with_memory_space_constraint(lax.empty(s.shape, s.dtype), pltpu.HBM) for s in land_shapes]
    ins = [pltpu.with_memory_space_constraint(s, pltpu.HBM) for s in ins]
    return pl.pallas_call(
        body, name=name,
        out_shape=(pltpu.SemaphoreType.DMA((n_sems,)), pltpu.SemaphoreType.DMA((n_sems,)),
                   *[pltpu.HBM(s.shape, s.dtype) for s in ins], *[pltpu.HBM(s.shape, s.dtype) for s in lands],
                   jax.ShapeDtypeStruct((8, HEAD), F32)),
        in_specs=[_HBM] * (n + nl) + [pl.BlockSpec(memory_space=pl.ANY)] * len(after),
        out_specs=(_SEM, _SEM, *([_HBM] * (n + nl)), pl.BlockSpec(memory_space=pltpu.VMEM)),
        input_output_aliases={k: 2 + k for k in range(n + nl)},
        compiler_params=pltpu.CompilerParams(has_side_effects=_EFFECT),
    )(*ins, *lands, *after)


def _split_wait(started, n, copies, name, after=()):
    send_sems, recv_sems = started[0], started[1]
    bufs = started[2:-1]
    nb = len(bufs)

    def body(*refs):
        in_refs, land_refs = refs[:n], refs[n:nb]
        sends, recvs = copies(in_refs, land_refs, refs[nb], refs[nb + 1])
        for cp in sends:
            cp.wait_send()
        for cp in recvs:
            cp.wait_recv()

    outs = pl.pallas_call(
        body, name=name,
        out_shape=tuple(pltpu.HBM(s.shape, s.dtype) for s in bufs),
        in_specs=[_HBM] * nb + [_SEM, _SEM] + [pl.BlockSpec(memory_space=pl.ANY)] * len(after),
        out_specs=tuple([_HBM] * nb),
        input_output_aliases={k: k for k in range(nb)},
        compiler_params=pltpu.CompilerParams(has_side_effects=_EFFECT),
    )(*bufs, send_sems, recv_sems, *after)
    return list(outs[:n]), list(outs[n:])


def _fill_own_slot(gathered, shards, pidx, names):
    outs = []
    for g, s, name in zip(gathered, shards, names):
        r, cdim = s.shape
        tr = _row_tile(r)

        def body(p_ref, s_ref, g_ref, o_ref):
            o_ref[...] = s_ref[...]

        outs.append(pl.pallas_call(
            body, name=name,
            grid_spec=pltpu.PrefetchScalarGridSpec(
                num_scalar_prefetch=1, grid=(r // tr,),
                in_specs=[pl.BlockSpec((tr, cdim), lambda i, p: (i, 0)), pl.BlockSpec(memory_space=pl.ANY)],
                out_specs=pl.BlockSpec((None, tr, cdim), lambda i, p: (p[0], i, 0))),
            out_shape=jax.ShapeDtypeStruct(g.shape, g.dtype),
            input_output_aliases={2: 0},
            compiler_params=_cp(("arbitrary",)),
        )(pidx, s, g))
    return outs


def _sum_own_and_peers(own4, slots, pidx, name):
    _, rh, cdim = own4.shape
    tr = _row_tile(rh)

    def body(p_ref, own_ref, s_ref, o_ref):
        acc = own_ref[...].astype(F32)
        for j in range(3):
            acc = acc + s_ref[j].astype(F32)
        o_ref[...] = acc

    return pl.pallas_call(
        body, name=name,
        grid_spec=pltpu.PrefetchScalarGridSpec(
            num_scalar_prefetch=1, grid=(rh // tr,),
            in_specs=[pl.BlockSpec((None, tr, cdim), lambda i, p: (p[0], i, 0)),
                      pl.BlockSpec((3, tr, cdim), lambda i, p: (0, i, 0))],
            out_specs=pl.BlockSpec((tr, cdim), lambda i, p: (i, 0))),
        out_shape=jax.ShapeDtypeStruct((rh, cdim), F32),
        compiler_params=_cp(("arbitrary",)),
    )(pidx, own4, slots)


def _exchange(ins, land_shapes, copies, n_sems, name):
    n, nl = len(ins), len(land_shapes)

    def body(*refs):
        sends, recvs = copies(refs[:n], refs[n:n + nl], refs[n + nl], refs[n + nl + 1])
        for cp in sends:
            cp.start()
        for cp in sends:
            cp.wait_send()
        for cp in recvs:
            cp.wait_recv()

    any_spec = pl.BlockSpec(memory_space=pl.ANY)
    return pl.pallas_call(
        body, name=name,
        out_shape=[jax.ShapeDtypeStruct(s.shape, s.dtype) for s in land_shapes],
        in_specs=[any_spec] * n, out_specs=[any_spec] * nl,
        scratch_shapes=[pltpu.SemaphoreType.DMA((n_sems,)), pltpu.SemaphoreType.DMA((n_sems,))],
    )(*ins)


def _row_tile(r):
    t = min(TR_EW, r)
    while r % t:
        t //= 2
    return t


def _add_own_half(g4, recv, cidx, name):
    _, _, rh, cdim = g4.shape
    tr = _row_tile(rh)

    def body(c_ref, a_ref, b_ref, o_ref):
        o_ref[...] = (a_ref[...] + b_ref[...]).astype(BF16)

    return pl.pallas_call(
        body, name=name,
        grid_spec=pltpu.PrefetchScalarGridSpec(
            num_scalar_prefetch=1, grid=(N_CHIPS, rh // tr),
            in_specs=[pl.BlockSpec((None, None, tr, cdim), lambda q, i, c: (q, c[0], i, 0)),
                      pl.BlockSpec((None, tr, cdim), lambda q, i, c: (q, i, 0))],
            out_specs=pl.BlockSpec((None, tr, cdim), lambda q, i, c: (q, i, 0))),
        out_shape=jax.ShapeDtypeStruct(recv.shape, BF16),
        compiler_params=_cp(("arbitrary", "arbitrary")),
    )(cidx, g4, recv)


def _sum_slots(v, name):
    n, r, cdim = v.shape
    tr = _row_tile(r)

    def body(v_ref, o_ref):
        acc = v_ref[0].astype(F32)
        for k in range(1, n):
            acc = acc + v_ref[k].astype(F32)
        o_ref[...] = acc

    return pl.pallas_call(
        body, name=name, grid=(r // tr,),
        in_specs=[pl.BlockSpec((n, tr, cdim), lambda i: (0, i, 0))],
        out_specs=pl.BlockSpec((tr, cdim), lambda i: (i, 0)),
        out_shape=jax.ShapeDtypeStruct((r, cdim), F32),
        compiler_params=_cp(("arbitrary",)),
    )(v)


def _sum_devices(lands, own, didx, name):
    _, r, cdim = lands.shape
    tr = _row_tile(r)

    def body(d_ref, l_ref, own_ref, o_ref):
        acc = jnp.where(d_ref[0] == 0, own_ref[...], l_ref[0])
        for dv in range(1, N_DEV):
            acc = acc + jnp.where(d_ref[0] == dv, own_ref[...], l_ref[dv])
        o_ref[...] = acc

    return pl.pallas_call(
        body, name=name,
        grid_spec=pltpu.PrefetchScalarGridSpec(
            num_scalar_prefetch=1, grid=(r // tr,),
            in_specs=[pl.BlockSpec((N_DEV, tr, cdim), lambda i, dd: (0, i, 0)), pl.BlockSpec((tr, cdim), lambda i, dd: (i, 0))],
            out_specs=pl.BlockSpec((tr, cdim), lambda i, dd: (i, 0))),
        out_shape=jax.ShapeDtypeStruct((r, cdim), F32),
        compiler_params=_cp(("arbitrary",)),
    )(didx, lands, own)


def _adamw_math(wv, gg, mv, vv):
    nm = ADAM_B1 * mv + (1.0 - ADAM_B1) * gg
    nv = ADAM_B2 * vv + (1.0 - ADAM_B2) * (gg * gg)
    m_hat = nm / (1.0 - ADAM_B1 ** ADAM_STEP)
    v_hat = nv / (1.0 - ADAM_B2 ** ADAM_STEP)
    return -ADAM_LR * (m_hat / (jnp.sqrt(v_hat) + ADAM_EPS) + ADAM_WD * wv), nm, nv


def _adamw_halves(w, mine, theirs, m, v, cidx, name):
    r, cdim = w.shape
    rh = r // 2
    tr = _row_tile(rh)
    nblk = rh // tr

    def body(c_ref, w_ref, a_ref, b_ref, m_ref, v_ref, g_ref, d_ref, nm_ref, nv_ref):
        gg = jnp.where(pl.program_id(0) == c_ref[0], a_ref[...], b_ref[...])
        g_ref[...] = gg
        d_ref[...], nm_ref[...], nv_ref[...] = _adamw_math(w_ref[...], gg, m_ref[...], v_ref[...])

    full = pl.BlockSpec((tr, cdim), lambda hh, i, c: (hh * nblk + i, 0))
    half = pl.BlockSpec((tr, cdim), lambda hh, i, c: (i, 0))
    return pl.pallas_call(
        body, name=name,
        grid_spec=pltpu.PrefetchScalarGridSpec(
            num_scalar_prefetch=1, grid=(2, nblk),
            in_specs=[full, half, half, full, full], out_specs=[full] * 4),
        out_shape=[jax.ShapeDtypeStruct((r, cdim), F32)] * 4,
        compiler_params=_cp(("arbitrary", "arbitrary")),
    )(cidx, w, mine, theirs, m, v)


def _adamw_many(params, name):
    n = len(params)

    def body(*refs):
        ins, outs = refs[:4 * n], refs[4 * n:]
        for k in range(n):
            w_ref, g_ref, m_ref, v_ref = ins[4 * k:4 * k + 4]
            outs[3 * k][...], outs[3 * k + 1][...], outs[3 * k + 2][...] = _adamw_math(
                w_ref[...], g_ref[...], m_ref[...], v_ref[...])

    flat = [a for p in params for a in p]
    res = pl.pallas_call(
        body, name=name,
        out_shape=[jax.ShapeDtypeStruct(p[0].shape, F32) for p in params for _ in range(3)],
        compiler_params=pltpu.CompilerParams(vmem_limit_bytes=VMEM_LIMIT),
    )(*flat)
    return [res[3 * k:3 * k + 3] for k in range(n)]


def _adamw(w, g, m, v, name):
    r, cdim = w.shape
    tr = _row_tile(r) if r % 8 == 0 else r

    def body(w_ref, g_ref, m_ref, v_ref, d_ref, nm_ref, nv_ref):
        d_ref[...], nm_ref[...], nv_ref[...] = _adamw_math(w_ref[...], g_ref[...], m_ref[...], v_ref[...])

    spec = pl.BlockSpec((tr, cdim), lambda i: (i, 0))
    return pl.pallas_call(
        body, name=name, grid=(r // tr,), in_specs=[spec] * 4, out_specs=[spec] * 3,
        out_shape=[jax.ShapeDtypeStruct((r, cdim), F32)] * 3,
        compiler_params=_cp(("arbitrary",)),
    )(w, g, m, v)


def _ada_fwd(c_all, w_ada, b_cols):
    nb, _ = c_all.shape
    n = w_ada.shape[1]

    def body(c_ref, w_ref, b_ref, o_ref):
        cv = c_ref[...]
        o_ref[...] = _mm(cv * _sigmoid(cv), w_ref[...]) + b_ref[...]

    return pl.pallas_call(
        body, name="ada_fwd", out_shape=jax.ShapeDtypeStruct((nb, n), F32),
        compiler_params=pltpu.CompilerParams(vmem_limit_bytes=VMEM_LIMIT),
    )(c_all, w_ada, b_cols)


def _ada_bwd(c_all, dmod_all, dmod_cols):
    d = c_all.shape[1]
    n = dmod_cols.shape[1]

    def body(c_ref, da_ref, dc_ref, gw_ref, gb_ref):
        cv = c_ref[...]
        gw_ref[...] = _mm_tn(cv * _sigmoid(cv), dc_ref[...])
        gb_ref[...] = _colsum(da_ref[...])

    return pl.pallas_call(
        body, name="ada_bwd",
        out_shape=[jax.ShapeDtypeStruct((d, n), F32), jax.ShapeDtypeStruct((1, dmod_all.shape[1]), F32)],
        compiler_params=pltpu.CompilerParams(vmem_limit_bytes=VMEM_LIMIT),
    )(c_all, dmod_all, dmod_cols)


def _proj_fwd(x2, modv, ws, cols, b_in, seq, name, proj_in=None):
    t, d = x2.shape
    n = len(ws)
    ns = ws[0].shape[1]
    tm = min(TM_PROJ, seq)
    tpb = seq // tm
    first = proj_in is None

    def body(c_ref, x_ref, mod_ref, *refs):
        w_refs, b_ref = refs[:n], refs[n]
        outs = refs[n + 1 if first else n + 2:]
        proj_ref, h_s = outs[0], outs[-1]
        s = pl.program_id(1)

        @pl.when(s == 0)
        def _():
            h = (x_ref[...] * (1.0 + mod_ref[1:2, :]) + mod_ref[0:1, :]).astype(BF16)
            h_s[...] = h
            if first:
                outs[1][...] = h

        for k in range(n):
            @pl.when(s == k)
            def _():
                proj_ref[...] = (jnp.dot(h_s[...], w_refs[k][...], preferred_element_type=F32) + b_ref[...]).astype(BF16)

    in_specs = [pl.BlockSpec((tm, d), lambda i, s, c: (i, 0)),
                pl.BlockSpec((None, 8, d), lambda i, s, c: (i // tpb, 0, 0))]
    in_specs += [pl.BlockSpec((d, ns), lambda i, s, c: (0, 0))] * n
    in_specs += [pl.BlockSpec((1, ns), lambda i, s, c: (0, c[s]))]
    out_specs = [pl.BlockSpec((tm, ns), lambda i, s, c: (i, c[s]))]
    out_shape = [jax.ShapeDtypeStruct((t, N_CHIPS * ns), BF16)]
    args = [cols, x2, modv, *ws, b_in]
    aliases = {}
    if first:
        out_specs.append(pl.BlockSpec((tm, d), lambda i, s, c: (i, 0)))
        out_shape.append(jax.ShapeDtypeStruct((t, d), BF16))
    else:
        in_specs.append(_ANY)
        args.append(proj_in)
        aliases = {len(args) - 1: 0}
    return pl.pallas_call(
        body, name=name,
        grid_spec=pltpu.PrefetchScalarGridSpec(
            num_scalar_prefetch=1, grid=(t // tm, n), in_specs=in_specs, out_specs=out_specs,
            scratch_shapes=[pltpu.VMEM((tm, d), BF16)]),
        out_shape=out_shape, input_output_aliases=aliases,
        compiler_params=_cp(("arbitrary", "arbitrary")),
    )(*args)


def _lru_rate(lam_ref):
    nl = -lam_ref[...]
    e = jnp.exp(-jnp.abs(nl))
    u = 1.0 + e
    dlt = u - 1.0
    log1p_e = jnp.where(dlt == 0.0, e, jnp.log(u) * (e / jnp.where(dlt == 0.0, 1.0, dlt)))
    return -LRU_C * (jnp.maximum(nl, 0.0) + log1p_e)


def _lru_gates(xl, wc_ref, bc_ref, wa_ref, ba_ref, wx_ref, bx_ref, lam_ref):
    xc = bc_ref[...] + wc_ref[CONV_WIDTH - 1:CONV_WIDTH, :] * xl
    for k in range(CONV_WIDTH - 1):
        xc = xc + wc_ref[k:k + 1, :] * _shift_down(xl, CONV_WIDTH - 1 - k)
    r = _sigmoid(_mm(xc, wa_ref[...]) + ba_ref[...])
    gi = _sigmoid_t(_mm(xc, wx_ref[...]) + bx_ref[...])
    big_l = _lru_rate(lam_ref)
    la = big_l * r
    a = jnp.exp(la)
    m2 = jnp.tanh(-la) * (a * a + 1.0)
    return xc, r, gi, big_l, a, m2


def _lru_prep(proj, lru_w, nb, seq):
    t = proj.shape[0]
    w = LRU_HEADS * HEAD
    w_conv, b_conv, w_a, b_a, w_x, b_x, lam = lru_w

    def body(x_ref, wc_ref, bc_ref, wa_ref, ba_ref, wx_ref, bx_ref, lam_ref, a_ref, inp_ref, r_ref, gi_ref, xc_ref):
        xc, r, gi, big_l, a, m2 = _lru_gates(x_ref[...].astype(F32), wc_ref, bc_ref, wa_ref, ba_ref, wx_ref, bx_ref, lam_ref)
        a_ref[...] = a
        inp_ref[...] = (jnp.sqrt(m2) * (gi * xc)).astype(BF16)
        r_ref[...] = r.astype(BF16)
        gi_ref[...] = gi.astype(BF16)
        xc_ref[...] = xc.astype(BF16)

    col = lambda b, hd: (0, hd)
    head = lambda b, hd: (hd, 0, 0)
    tok = lambda b, hd: (b, hd)
    return pl.pallas_call(
        body, name="lru_prep", grid=(nb, LRU_HEADS),
        in_specs=[pl.BlockSpec((seq, HEAD), tok),
                  pl.BlockSpec((CONV_WIDTH, HEAD), col), pl.BlockSpec((1, HEAD), col),
                  pl.BlockSpec((None, HEAD, HEAD), head), pl.BlockSpec((1, HEAD), col),
                  pl.BlockSpec((None, HEAD, HEAD), head), pl.BlockSpec((1, HEAD), col),
                  pl.BlockSpec((1, HEAD), col)],
        out_specs=[pl.BlockSpec((seq, HEAD), tok)] * 5,
        out_shape=[jax.ShapeDtypeStruct((t, w), F32)] + [jax.ShapeDtypeStruct((t, w), BF16)] * 4,
        compiler_params=_cp(("arbitrary", "arbitrary")),
    )(proj, w_conv, b_conv, w_a, b_a, w_x, b_x, lam)


def _scan(a3, b3, reverse, name, out_dtype):
    nb, seq, w = a3.shape
    tc = min(TC_SCAN, seq)
    nchunk = seq // tc
    npair = tc // 16

    def combine(av, bv):
        rows = lax.broadcasted_iota(jnp.int32, av.shape, 0)
        for s in (1, 2, 4):
            if reverse:
                keep = rows < 8 - s
                a_sh, b_sh = pltpu.roll(av, 8 - s, 0), pltpu.roll(bv, 8 - s, 0)
            else:
                keep = rows >= s
                a_sh, b_sh = pltpu.roll(av, s, 0), pltpu.roll(bv, s, 0)
            bv = jnp.where(keep, bv + av * b_sh, bv)
            av = jnp.where(keep, av * a_sh, av)
        return av, bv

    def body(a_ref, b_ref, h_ref, carry):
        @pl.when(pl.program_id(0) == 0)
        def _():
            carry[...] = jnp.zeros_like(carry)

        for b in range(nb):
            def pair(j, hprev):
                jj = npair - 1 - j if reverse else j
                base = pl.multiple_of(jj * 16, 16)
                a16 = a_ref[b, pl.ds(base, 16), :]
                b16 = b_ref[b, pl.ds(base, 16), :].astype(F32)
                outs = [None, None]
                for k in ((1, 0) if reverse else (0, 1)):
                    av, bv = a16[8 * k:8 * k + 8, :], b16[8 * k:8 * k + 8, :]
                    av, bv = combine(av, av * bv if reverse else bv)
                    h = bv + av * hprev
                    outs[k] = h
                    hprev = jnp.broadcast_to(h[0:1, :] if reverse else h[7:8, :], (8, w))
                h_ref[b, pl.ds(base, 16), :] = jnp.concatenate(outs, axis=0).astype(out_dtype)
                return hprev

            carry[b] = lax.fori_loop(0, npair, pair, carry[b])

    imap = (lambda i: (0, nchunk - 1 - i, 0)) if reverse else (lambda i: (0, i, 0))
    spec = pl.BlockSpec((nb, tc, w), imap)
    return pl.pallas_call(
        body, name=name, grid=(nchunk,), in_specs=[spec, spec], out_specs=spec,
        out_shape=jax.ShapeDtypeStruct((nb, seq, w), out_dtype),
        scratch_shapes=[pltpu.VMEM((nb, 8, w), F32)],
        compiler_params=_cp(("arbitrary",)),
    )(a3, b3)


def _sgu_mask():
    ti = lax.broadcasted_iota(jnp.int32, (HEAD, HEAD), 0) // SGU_CHUNK
    si = lax.broadcasted_iota(jnp.int32, (HEAD, HEAD), 1) // SGU_CHUNK
    return si <= ti


def _sgu_specs(tm, d_sgu):
    pw = 256
    first_u = (2 * LRU_HEADS * HEAD) // pw
    n_piece = d_sgu // pw
    specs = [pl.BlockSpec((tm, pw), functools.partial(lambda i, k: (i, k), k=first_u + j)) for j in range(2 * n_piece)]
    return specs, n_piece


def _sgu_fwd(proj, w_sp, b_sp_t, ln_g, ln_b):
    t = proj.shape[0]
    d_sgu = SGU_GROUPS * HEAD
    tm = min(TM_SGU, t)
    nblk = tm // HEAD
    specs, n_piece = _sgu_specs(tm, d_sgu)

    def body(*refs):
        u = jnp.concatenate([r[...] for r in refs[:n_piece]], axis=1).astype(F32)
        v = jnp.concatenate([r[...] for r in refs[n_piece:2 * n_piece]], axis=1).astype(F32)
        w_ref, bt_ref, g_ref, b_ref, y_ref = refs[2 * n_piece:]
        ug = _gelu(u)
        xhat, _ = _ln_stats(_gelu(v))
        vn = (xhat * g_ref[...] + b_ref[...]).astype(BF16)
        mask = _sgu_mask()
        for g in range(SGU_GROUPS):
            wm = jnp.where(mask, w_ref[g], 0.0).astype(BF16)
            cols = slice(g * HEAD, (g + 1) * HEAD)
            for n in range(nblk):
                rows = slice(n * HEAD, (n + 1) * HEAD)
                mixed = jnp.dot(wm, vn[rows, cols], preferred_element_type=F32) + bt_ref[:, g:g + 1]
                y_ref[rows, cols] = (ug[rows, cols] * mixed).astype(BF16)

    full = lambda shape: pl.BlockSpec(shape, lambda i: (0,) * len(shape))
    return pl.pallas_call(
        body, name="sgu_fwd", grid=(t // tm,),
        in_specs=specs + [full(w_sp.shape), full(b_sp_t.shape), full(ln_g.shape), full(ln_b.shape)],
        out_specs=pl.BlockSpec((tm, d_sgu), lambda i: (i, 0)),
        out_shape=jax.ShapeDtypeStruct((t, d_sgu), BF16),
        compiler_params=_cp(("arbitrary",)),
    )(*([proj] * (2 * n_piece)), w_sp, b_sp_t, ln_g, ln_b)


def _mix_fwd(hs, proj, y_sgu, x2, modv, w_o_lru_g, w_o_sgu_g, w_out_g, ln1_g, ln1_b, seq):
    t, d = x2.shape
    w = hs.shape[1]
    d_sgu = y_sgu.shape[1]
    nq, _, ns = w_o_sgu_g.shape
    tm = min(TM_MIX, seq)
    ts = min(TS_MLP, tm)
    tpb = seq // tm

    def body(hs_ref, gl_ref, ys_ref, ga_ref, gb_ref, x_ref, mod_ref, wl_hbm, ws_hbm, wo_hbm, g1_ref, b1_ref,
             yap_ref, ya_ref, yb_ref, mg_ref, mix_ref, x1_ref, wl_ref, ws_ref, wo_ref, sems):
        @pl.when(pl.program_id(0) == 0)
        def _():
            _load_weights((wl_hbm, ws_hbm, wo_hbm), (wl_ref, ws_ref, wo_ref), sems)

        for sub in range(tm // ts):
            rows = slice(sub * ts, (sub + 1) * ts)
            yap = (hs_ref[rows, :].astype(F32) * _gelu(gl_ref[rows, :].astype(F32))).astype(BF16)
            yap_ref[rows, :] = yap
            y_a = jnp.dot(yap, wl_ref[...], preferred_element_type=F32)
            ys = ys_ref[rows, :]
            y_b = jnp.concatenate([jnp.dot(ys, ws_ref[q], preferred_element_type=F32) for q in range(nq)], axis=1)
            ya_ref[rows, :] = y_a.astype(BF16)
            yb_ref[rows, :] = y_b.astype(BF16)
            merged = (_sigmoid_t(ga_ref[rows, :].astype(F32)) * y_a
                      + _sigmoid_t(gb_ref[rows, :].astype(F32)) * y_b).astype(BF16)
            mg_ref[rows, :] = merged
            mix = jnp.dot(merged, wo_ref[...], preferred_element_type=F32)
            mix_ref[rows, :] = mix
            xhat, _ = _ln_stats(ALPHA * x_ref[rows, :] + (1.0 + mod_ref[2:3, :]) * mix)
            x1_ref[rows, :] = xhat * g1_ref[...] + b1_ref[...]

    row = lambda width, col: pl.BlockSpec((tm, width), functools.partial(lambda i, k: (i, k), k=col))
    full = lambda shape: pl.BlockSpec(shape, lambda i: (0,) * len(shape))
    return pl.pallas_call(
        body, name="mix_fwd", grid=(t // tm,),
        in_specs=[row(w, 0), row(w, 1), row(d_sgu, 0), row(d, 4), row(d, 5), row(d, 0),
                  pl.BlockSpec((None, 8, d), lambda i: (i // tpb, 0, 0)),
                  _ANY, _ANY, _ANY, full(ln1_g.shape), full(ln1_b.shape)],
        out_specs=[row(w, 0), row(d, 0), row(d, 0), row(d, 0), row(d, 0), row(d, 0)],
        out_shape=[jax.ShapeDtypeStruct((t, w), BF16), jax.ShapeDtypeStruct((t, d), BF16),
                   jax.ShapeDtypeStruct((t, d), BF16), jax.ShapeDtypeStruct((t, d), BF16),
                   jax.ShapeDtypeStruct((t, d), F32), jax.ShapeDtypeStruct((t, d), F32)],
        scratch_shapes=[pltpu.VMEM(w_o_lru_g.shape, BF16), pltpu.VMEM(w_o_sgu_g.shape, BF16),
                        pltpu.VMEM(w_out_g.shape, BF16), pltpu.SemaphoreType.DMA((3,))],
        compiler_params=_cp(("arbitrary",)),
    )(hs, proj, y_sgu, proj, proj, x2, modv, w_o_lru_g, w_o_sgu_g, w_out_g, ln1_g, ln1_b)


def _mlp_fwd(x1, modv, w_up_g, w_down_g, ln2_g, ln2_b, target, nb, seq):
    t, d = x1.shape
    nq, _, ns = w_up_g.shape
    tm = min(TM_MLP, seq)
    ts = min(TS_MLP, tm)
    tpb = seq // tm

    def body(x1_ref, mod_ref, wu_hbm, wd_hbm, g2_ref, b2_ref, tg_ref,
             rl_ref, act_ref, h2_ref, dz2_ref, df_ref, st_ref, pb_ref, wu_s, wd_s, acc, sems):
        i = pl.program_id(0)

        @pl.when(i == 0)
        def _():
            _load_weights((wu_hbm, wd_hbm), (wu_s, wd_s), sems)
            st_ref[...] = jnp.zeros_like(st_ref)

        @pl.when(i % tpb == 0)
        def _():
            pb_ref[...] = jnp.zeros_like(pb_ref)

        for sub in range(tm // ts):
            rows = slice(sub * ts, (sub + 1) * ts)
            x1v = x1_ref[rows, :]
            h2 = (x1v * (1.0 + mod_ref[4:5, :]) + mod_ref[3:4, :]).astype(BF16)
            h2_ref[rows, :] = h2
            for k in range(nq):
                cols = slice(k * ns, (k + 1) * ns)
                r = jnp.maximum(jnp.dot(h2, wu_s[k], preferred_element_type=F32), 0.0)
                act = (r * r).astype(BF16)
                rl_ref[rows, cols] = r.astype(BF16)
                act_ref[rows, cols] = act
                part = jnp.dot(act, wd_s[cols, :], preferred_element_type=F32)
                if k == 0:
                    acc[sub] = part
                else:
                    acc[sub] += part
            f = acc[sub]
            xhat, rstd = _ln_stats(ALPHA * x1v + (1.0 + mod_ref[5:6, :]) * f)
            y = xhat * g2_ref[...] + b2_ref[...]
            err = y - tg_ref[rows, :]
            dy = err * (1.0 / d)
            dz2 = _ln_bwd(dy * g2_ref[...], xhat, rstd)
            dz2_ref[rows, :] = dz2
            df_ref[rows, :] = ((1.0 + mod_ref[5:6, :]) * dz2).astype(BF16)
            st_ref[0:1, :] += _colsum(dy * xhat)
            st_ref[1:2, :] += _colsum(dy)
            st_ref[2:3, :] += (0.5 / d) * jnp.sum(_colsum(err * err), axis=1, keepdims=True)
            pb_ref[0:1, :] += _colsum(dz2 * f)

    tok = lambda i: (i, 0)
    return pl.pallas_call(
        body, name="mlp_fwd", grid=(t // tm,),
        in_specs=[pl.BlockSpec((tm, d), tok), pl.BlockSpec((None, 8, d), lambda i: (i // tpb, 0, 0)), _ANY, _ANY,
                  pl.BlockSpec((1, d), lambda i: (0, 0)), pl.BlockSpec((1, d), lambda i: (0, 0)),
                  pl.BlockSpec((tm, d), tok)],
        out_specs=[pl.BlockSpec((tm, nq * ns), tok), pl.BlockSpec((tm, nq * ns), tok),
                   pl.BlockSpec((tm, d), tok), pl.BlockSpec((tm, d), tok), pl.BlockSpec((tm, d), tok),
                   pl.BlockSpec((8, d), lambda i: (0, 0)), pl.BlockSpec((None, 8, d), lambda i: (i // tpb, 0, 0))],
        out_shape=[jax.ShapeDtypeStruct((t, nq * ns), BF16), jax.ShapeDtypeStruct((t, nq * ns), BF16),
                   jax.ShapeDtypeStruct((t, d), BF16),
                   jax.ShapeDtypeStruct((t, d), F32), jax.ShapeDtypeStruct((t, d), BF16),
                   jax.ShapeDtypeStruct((8, d), F32), jax.ShapeDtypeStruct((nb, 8, d), F32)],
        scratch_shapes=[pltpu.VMEM(w_up_g.shape, BF16), pltpu.VMEM(w_down_g.shape, BF16),
                        pltpu.VMEM((tm // ts, ts, d), F32), pltpu.SemaphoreType.DMA((2,))],
        compiler_params=_cp(("arbitrary",)),
    )(x1, modv, w_up_g, w_down_g, ln2_g, ln2_b, target)


def _mlp_bwd(df, up, w_down_g, w_up_g, dz2, x2, mix, modv, ln1_g, ln1_b, nb, seq):
    t, d = x2.shape
    nq, _, ns = w_up_g.shape
    tm = min(TM_MLP, seq)
    ts = min(TS_MLP, tm)
    tpb = seq // tm

    def body(df_ref, rl_ref, wd_hbm, wu_hbm, dz2_ref, x_ref, mix_ref, mod_ref, g1_ref, b1_ref,
             dup_ref, dz1_ref, dmix_ref, st_ref, pb_ref, wd_s, wu_s, acc, sems):
        i = pl.program_id(0)

        @pl.when(i == 0)
        def _():
            _load_weights((wd_hbm, wu_hbm), (wd_s, wu_s), sems)
            st_ref[...] = jnp.zeros_like(st_ref)

        @pl.when(i % tpb == 0)
        def _():
            pb_ref[...] = jnp.zeros_like(pb_ref)

        for sub in range(tm // ts):
            rows = slice(sub * ts, (sub + 1) * ts)
            dfv = df_ref[rows, :]
            for k in range(nq):
                cols = slice(k * ns, (k + 1) * ns)
                dup = (_mm_nt(dfv, wd_s[cols, :]) * (2.0 * rl_ref[rows, cols].astype(F32))).astype(BF16)
                dup_ref[rows, cols] = dup
                part = _mm_nt(dup, wu_s[k])
                if k == 0:
                    acc[sub] = part
                else:
                    acc[sub] += part
            dh2 = acc[sub]
            mix = mix_ref[rows, :]
            xhat, rstd = _ln_stats(ALPHA * x_ref[rows, :] + (1.0 + mod_ref[2:3, :]) * mix)
            x1 = xhat * g1_ref[...] + b1_ref[...]
            dx1 = ALPHA * dz2_ref[rows, :] + dh2 * (1.0 + mod_ref[4:5, :])
            dz1 = _ln_bwd(dx1 * g1_ref[...], xhat, rstd)
            dz1_ref[rows, :] = dz1
            dmix_ref[rows, :] = ((1.0 + mod_ref[2:3, :]) * dz1).astype(BF16)
            st_ref[0:1, :] += _colsum(dx1 * xhat)
            st_ref[1:2, :] += _colsum(dx1)
            pb_ref[0:1, :] += _colsum(dh2 * x1)
            pb_ref[1:2, :] += _colsum(dh2)
            pb_ref[2:3, :] += _colsum(dz1 * mix)

    tok = lambda i: (i, 0)
    return pl.pallas_call(
        body, name="mlp_bwd", grid=(t // tm,),
        in_specs=[pl.BlockSpec((tm, d), tok), pl.BlockSpec((tm, nq * ns), tok), _ANY, _ANY,
                  pl.BlockSpec((tm, d), tok), pl.BlockSpec((tm, d), tok), pl.BlockSpec((tm, d), tok),
                  pl.BlockSpec((None, 8, d), lambda i: (i // tpb, 0, 0)),
                  pl.BlockSpec((1, d), lambda i: (0, 0)), pl.BlockSpec((1, d), lambda i: (0, 0))],
        out_specs=[pl.BlockSpec((tm, nq * ns), tok),
                   pl.BlockSpec((tm, d), tok), pl.BlockSpec((tm, d), tok),
                   pl.BlockSpec((8, d), lambda i: (0, 0)), pl.BlockSpec((None, 8, d), lambda i: (i // tpb, 0, 0))],
        out_shape=[jax.ShapeDtypeStruct((t, nq * ns), BF16),
                   jax.ShapeDtypeStruct((t, d), F32), jax.ShapeDtypeStruct((t, d), BF16),
                   jax.ShapeDtypeStruct((8, d), F32), jax.ShapeDtypeStruct((nb, 8, d), F32)],
        scratch_shapes=[pltpu.VMEM(w_down_g.shape, BF16), pltpu.VMEM(w_up_g.shape, BF16),
                        pltpu.VMEM((tm // ts, ts, d), F32), pltpu.SemaphoreType.DMA((2,))],
        compiler_params=_cp(("arbitrary",), VMEM_LIMIT_MAX),
    )(df, up, w_down_g, w_up_g, dz2, x2, mix, modv, ln1_g, ln1_b)


def _mix_bwd(dmix, proj, y_a, y_b, hs, w_out_g, w_o_lru_g, w_o_sgu_g, seq, after=()):
    t, d = dmix.shape
    w = hs.shape[1]
    nq, d_sgu, ns = w_o_sgu_g.shape
    tm = min(TM_MIX, seq)
    ts = min(TS_MLP, tm)

    def body(dmix_ref, ga_ref, gb_ref, ya_ref, yb_ref, gl_ref, hs_ref, wo_hbm, wl_hbm, ws_hbm,
             dya_ref, dyb_ref, dg_ref, dyl_ref, dys_ref, wo_ref, wl_ref, ws_ref, sems):
        @pl.when(pl.program_id(0) == 0)
        def _():
            _load_weights((wo_hbm, wl_hbm, ws_hbm), (wo_ref, wl_ref, ws_ref), sems)

        for sub in range(tm // ts):
            rows = slice(sub * ts, (sub + 1) * ts)
            dmerged = _mm_nt(dmix_ref[rows, :], wo_ref[...])
            sa, sb = _sigmoid_t(ga_ref[rows, :].astype(F32)), _sigmoid_t(gb_ref[rows, :].astype(F32))
            dy_a = (dmerged * sa).astype(BF16)
            dy_b = (dmerged * sb).astype(BF16)
            dya_ref[rows, :] = dy_a
            dyb_ref[rows, :] = dy_b
            dg_ref[rows, 4 * d:5 * d] = (dmerged * ya_ref[rows, :].astype(F32) * (sa * (1.0 - sa))).astype(BF16)
            dg_ref[rows, 5 * d:6 * d] = (dmerged * yb_ref[rows, :].astype(F32) * (sb * (1.0 - sb))).astype(BF16)
            dyap = _mm_nt(dy_a, wl_ref[...])
            gel, dgel = _gelu_and_grad(gl_ref[rows, :].astype(F32))
            dyl_ref[rows, :] = (dyap * gel).astype(BF16)
            dg_ref[rows, w:2 * w] = (dyap * hs_ref[rows, :].astype(F32) * dgel).astype(BF16)
            dys = _mm_nt(dy_b[:, 0:ns], ws_ref[0])
            for q in range(1, nq):
                dys = dys + _mm_nt(dy_b[:, q * ns:(q + 1) * ns], ws_ref[q])
            dys_ref[rows, :] = dys

    row = lambda width, col: pl.BlockSpec((tm, width), functools.partial(lambda i, k: (i, k), k=col))
    return pl.pallas_call(
        _ordered(body, 10, after), name="mix_bwd", grid=(t // tm,),
        in_specs=[row(d, 0), row(d, 4), row(d, 5), row(d, 0), row(d, 0), row(w, 1), row(w, 0),
                  _ANY, _ANY, _ANY] + [_ANY] * len(after),
        scratch_shapes=[pltpu.VMEM(w_out_g.shape, BF16), pltpu.VMEM(w_o_lru_g.shape, BF16),
                        pltpu.VMEM(w_o_sgu_g.shape, BF16), pltpu.SemaphoreType.DMA((3,))],
        out_specs=[row(d, 0), row(d, 0), row(6 * d, 0), row(w, 0), row(d_sgu, 0)],
        out_shape=[jax.ShapeDtypeStruct((t, d), BF16), jax.ShapeDtypeStruct((t, d), BF16),
                   jax.ShapeDtypeStruct((t, 6 * d), BF16), jax.ShapeDtypeStruct((t, w), BF16),
                   jax.ShapeDtypeStruct((t, d_sgu), F32)],
        compiler_params=_cp(("arbitrary",)),
    )(dmix, proj, proj, y_a, y_b, proj, hs, w_out_g, w_o_lru_g, w_o_sgu_g, *after)


def _sgu_bwd(proj, dys, w_sp, b_sp_t, ln_g, ln_b, after=()):
    t = proj.shape[0]
    d_sgu = SGU_GROUPS * HEAD
    tm = min(TM_SGU, t)
    nblk = tm // HEAD
    specs, n_piece = _sgu_specs(tm, d_sgu)

    def body(*refs):
        u = jnp.concatenate([r[...] for r in refs[:n_piece]], axis=1).astype(F32)
        v = jnp.concatenate([r[...] for r in refs[n_piece:2 * n_piece]], axis=1).astype(F32)
        dys_ref, w_ref, bt_ref, g_ref, b_ref, du_ref, dv_ref, dw_ref, st_ref, dbt_ref, dvn_s = refs[2 * n_piece:]

        @pl.when(pl.program_id(0) == 0)
        def _():
            dw_ref[...] = jnp.zeros_like(dw_ref)
            st_ref[...] = jnp.zeros_like(st_ref)
            dbt_ref[...] = jnp.zeros_like(dbt_ref)

        ug, dug_du = _gelu_and_grad(u)
        vg, dvg_dv = _gelu_and_grad(v)
        xhat, rstd = _ln_stats(vg)
        vn = (xhat * g_ref[...] + b_ref[...]).astype(BF16)
        dys_v = dys_ref[...]
        mask = _sgu_mask()
        for g in range(SGU_GROUPS):
            wm = jnp.where(mask, w_ref[g], 0.0).astype(BF16)
            cols = slice(g * HEAD, (g + 1) * HEAD)
            dw_g = jnp.zeros((HEAD, HEAD), F32)
            db_g = jnp.zeros((HEAD, 1), F32)
            for n in range(nblk):
                rows = slice(n * HEAD, (n + 1) * HEAD)
                vn_blk = vn[rows, cols]
                mixed = jnp.dot(wm, vn_blk, preferred_element_type=F32) + bt_ref[:, g:g + 1]
                dy_blk = dys_v[rows, cols]
                du_ref[rows, cols] = (dy_blk * mixed * dug_du[rows, cols]).astype(BF16)
                dmx = dy_blk * ug[rows, cols]
                dvn_s[rows, cols] = _mm_tn(wm, dmx)
                dw_g = dw_g + _mm_nt(dmx, vn_blk)
                db_g = db_g + jnp.sum(dmx, axis=1, keepdims=True)
            dw_ref[g] += jnp.where(mask, dw_g, 0.0)
            dbt_ref[:, g:g + 1] += db_g
        dvn = dvn_s[...]
        st_ref[0:1, :] += _colsum(dvn * xhat)
        st_ref[1:2, :] += _colsum(dvn)
        dv_ref[...] = (_ln_bwd(dvn * g_ref[...], xhat, rstd) * dvg_dv).astype(BF16)

    full = lambda shape: pl.BlockSpec(shape, lambda i: (0,) * len(shape))
    tok = pl.BlockSpec((tm, d_sgu), lambda i: (i, 0))
    return pl.pallas_call(
        _ordered(body, 2 * n_piece + 5, after), name="sgu_bwd", grid=(t // tm,),
        in_specs=specs + [tok, full(w_sp.shape), full(b_sp_t.shape), full(ln_g.shape), full(ln_b.shape)]
        + [_ANY] * len(after),
        out_specs=[tok, tok, full(w_sp.shape), full((8, d_sgu)), full((HEAD, HEAD))],
        out_shape=[jax.ShapeDtypeStruct((t, d_sgu), BF16), jax.ShapeDtypeStruct((t, d_sgu), BF16),
                   jax.ShapeDtypeStruct(w_sp.shape, F32), jax.ShapeDtypeStruct((8, d_sgu), F32),
                   jax.ShapeDtypeStruct((HEAD, HEAD), F32)],
        scratch_shapes=[pltpu.VMEM((tm, d_sgu), F32)],
        compiler_params=_cp(("arbitrary",)),
    )(*([proj] * (2 * n_piece)), dys, w_sp, b_sp_t, ln_g, ln_b, *after)


def _lru_bwd(proj, hs, e, dyl, saved, lru_w, nb, seq, dproj, after=()):
    t = proj.shape[0]
    w = LRU_HEADS * HEAD
    w_conv, b_conv, w_a, b_a, w_x, b_x, lam = lru_w

    def body(x_ref, hs_ref, e_ref, dy_ref, a_ref, r_ref, gi_ref, xc_ref, wc_ref, wa_ref, wx_ref, lam_ref,
             dxl_ref, dwa_ref, dwx_ref, st_ref):
        @pl.when(pl.program_id(1) == 0)
        def _():
            dwa_ref[...] = jnp.zeros_like(dwa_ref)
            dwx_ref[...] = jnp.zeros_like(dwx_ref)
            st_ref[...] = jnp.zeros_like(st_ref)

        xl = x_ref[...].astype(F32)
        a, r, gi, xc = a_ref[...], r_ref[...].astype(F32), gi_ref[...].astype(F32), xc_ref[...].astype(F32)
        big_l = _lru_rate(lam_ref)
        m2 = (1.0 - a) * (1.0 + a)
        inv_mult = lax.rsqrt(m2)
        mult = m2 * inv_mult
        dh = dy_ref[...].astype(F32) + _shift_up(e_ref[...].astype(F32), 1)
        da = dh * _shift_down(hs_ref[...].astype(F32), 1)
        dmult = dh * (gi * xc)
        d_i = dh * (mult * xc)
        dxc = dh * (mult * gi)
        dla = a * (da - dmult * (a * inv_mult))
        dr = dla * big_l
        d_big_l = _colsum(dla * r)
        dra = dr * (r * (1.0 - r))
        dia = d_i * (gi * (1.0 - gi))
        dwa_ref[...] += _mm_tn(xc, dra)
        dwx_ref[...] += _mm_tn(xc, dia)
        dxc = dxc + _mm_nt(dra, wa_ref[...]) + _mm_nt(dia, wx_ref[...])
        dxl = wc_ref[CONV_WIDTH - 1:CONV_WIDTH, :] * dxc
        st_ref[4 + CONV_WIDTH - 1:4 + CONV_WIDTH, :] += _colsum(dxc * xl)
        for k in range(CONV_WIDTH - 1):
            ahead = _shift_up(dxc, CONV_WIDTH - 1 - k)
            dxl = dxl + wc_ref[k:k + 1, :] * ahead
            st_ref[4 + k:5 + k, :] += _colsum(ahead * xl)
        dxl_ref[...] = dxl.astype(BF16)
        st_ref[0:1, :] += _colsum(dra)
        st_ref[1:2, :] += _colsum(dia)
        st_ref[2:3, :] += d_big_l * (LRU_C * _sigmoid(-lam_ref[...]))
        st_ref[3:4, :] += _colsum(dxc)

    col = lambda hd, b: (0, hd)
    head = lambda hd, b: (hd, 0, 0)
    tok = lambda hd, b: (b, hd)
    seq_blk = pl.BlockSpec((seq, HEAD), tok)
    return pl.pallas_call(
        _ordered(body, 12, (dproj,) + tuple(after)), name="lru_bwd", grid=(LRU_HEADS, nb),
        in_specs=[seq_blk] * 8 + [pl.BlockSpec((CONV_WIDTH, HEAD), col), pl.BlockSpec((None, HEAD, HEAD), head),
                                  pl.BlockSpec((None, HEAD, HEAD), head), pl.BlockSpec((1, HEAD), col)]
        + [_ANY] * (1 + len(after)),
        out_specs=[seq_blk, pl.BlockSpec((None, HEAD, HEAD), head), pl.BlockSpec((None, HEAD, HEAD), head),
                   pl.BlockSpec((8, HEAD), col)],
        out_shape=[jax.ShapeDtypeStruct(dproj.shape, BF16), jax.ShapeDtypeStruct((LRU_HEADS, HEAD, HEAD), F32),
                   jax.ShapeDtypeStruct((LRU_HEADS, HEAD, HEAD), F32), jax.ShapeDtypeStruct((8, w), F32)],
        input_output_aliases={12: 0},
        compiler_params=_cp(("arbitrary", "arbitrary")),
    )(proj, hs, e, dyl, *saved, w_conv, w_a, w_x, lam, dproj, *after)


def _weight_grad(a, g, col_shards, name, after=()):
    t, k = a.shape
    n = g.shape[1]
    tk = k if k <= 1536 else 1024
    ns = n // N_CHIPS if col_shards else n
    narrow = col_shards and ns < 512
    tn = n if narrow else min(ns, 768 if ns % 768 == 0 else 1024)
    while ns % tn and not narrow:
        tn //= 2
    per = max(ns // tn, 1)
    tt = min(TT_DW if (k // tk) * (n // tn) > 1 else TT_DW // 4, t)

    def body(a_ref, g_ref, o_ref):
        @pl.when(pl.program_id(2) == 0)
        def _():
            o_ref[...] = jnp.zeros_like(o_ref)

        res = _mm_tn(a_ref[...], g_ref[...])
        if narrow:
            for q in range(N_CHIPS):
                o_ref[q] += res[:, q * ns:(q + 1) * ns]
        else:
            o_ref[...] += res

    if narrow:
        out_spec = pl.BlockSpec((N_CHIPS, tk, ns), lambda i, j, s: (0, i, 0))
        out_shape = jax.ShapeDtypeStruct((N_CHIPS, k, ns), F32)
    elif col_shards:
        out_spec = pl.BlockSpec((None, tk, tn), lambda i, j, s: (j // per, i, j % per))
        out_shape = jax.ShapeDtypeStruct((N_CHIPS, k, ns), F32)
    else:
        out_spec = pl.BlockSpec((tk, tn), lambda i, j, s: (i, j))
        out_shape = jax.ShapeDtypeStruct((k, n), F32)
    return pl.pallas_call(
        _ordered(body, 2, after), name=name, grid=(k // tk, n // tn, t // tt),
        in_specs=[pl.BlockSpec((tt, tk), lambda i, j, s: (s, i)), pl.BlockSpec((tt, tn), lambda i, j, s: (s, j))]
        + [_ANY] * len(after),
        out_specs=out_spec, out_shape=out_shape,
        compiler_params=_cp(("arbitrary", "arbitrary", "arbitrary")),
    )(a, g, *after)


def _input_grad(dproj, ws, slots, dz1, x2, modv, nb, seq, after=()):
    t, d = x2.shape
    nq = len(ws)
    ns = ws[0].shape[1]
    tm = min(TM_DH, seq)
    ts = min(TS_MLP, tm)
    tpb = seq // tm

    def body(slot_ref, dp_ref, *refs):
        w_hbm = refs[:nq]
        dz1_ref, x_ref, mod_ref, gx_ref, db_ref, pb_ref, w_s, acc, sems = refs[nq:]
        i = pl.program_id(0)

        @pl.when(i == 0)
        def _():
            _load_weights(w_hbm, [w_s.at[slot_ref[k]] for k in range(nq)], sems)
            db_ref[...] = jnp.zeros_like(db_ref)

        @pl.when(i % tpb == 0)
        def _():
            pb_ref[...] = jnp.zeros_like(pb_ref)

        for sub in range(tm // ts):
            rows = slice(sub * ts, (sub + 1) * ts)
            for q in range(nq):
                dp = dp_ref[rows, q * ns:(q + 1) * ns]
                part = _mm_nt(dp, w_s[q])
                if q == 0:
                    acc[sub] = part
                else:
                    acc[sub] += part
                db_ref[q, 0:1, :] += _colsum(dp.astype(F32))
            dh = acc[sub]
            gx_ref[rows, :] = ALPHA * dz1_ref[rows, :] + dh * (1.0 + mod_ref[1:2, :])
            pb_ref[0:1, :] += _colsum(dh * x_ref[rows, :])
            pb_ref[1:2, :] += _colsum(dh)

    tok = lambda i, s: (i, 0)
    in_specs = [pl.BlockSpec((tm, nq * ns), tok)] + [_ANY] * nq
    in_specs += [pl.BlockSpec((tm, d), tok), pl.BlockSpec((tm, d), tok),
                 pl.BlockSpec((None, 8, d), lambda i, s: (i // tpb, 0, 0))] + [_ANY] * len(after)
    return pl.pallas_call(
        _ordered(body, 5 + nq, after), name="input_grad",
        grid_spec=pltpu.PrefetchScalarGridSpec(
            num_scalar_prefetch=1, grid=(t // tm,), in_specs=in_specs,
            out_specs=[pl.BlockSpec((tm, d), tok), pl.BlockSpec((nq, 8, ns), lambda i, s: (0, 0, 0)),
                       pl.BlockSpec((None, 8, d), lambda i, s: (i // tpb, 0, 0))],
            scratch_shapes=[pltpu.VMEM((nq, d, ns), BF16), pltpu.VMEM((tm // ts, ts, d), F32),
                            pltpu.SemaphoreType.DMA((nq,))]),
        out_shape=[jax.ShapeDtypeStruct((t, d), F32), jax.ShapeDtypeStruct((nq, 8, ns), F32),
                   jax.ShapeDtypeStruct((nb, 8, d), F32)],
        compiler_params=_cp(("arbitrary",)),
    )(slots, dproj, *ws, dz1, x2, modv, *after)


def _rows128(v):
    flat = v.reshape(-1, HEAD)
    pad = (-flat.shape[0]) % 8
    return jnp.pad(flat, ((0, pad), (0, 0))) if pad else flat


def kernel(x, c, w_ada, b_ada, w_in, b_in, w_conv, b_conv, w_rg_a, b_rg_a, w_rg_x, b_rg_x, lru_lambda, w_sp, b_sp, ln_v_g, ln_v_b, w_o_lru, w_o_sgu, w_out, ln1_g, ln1_b, w_up, w_down, ln2_g, ln2_b, loss_target, m_w_ada, m_b_ada, m_w_in, m_b_in, m_w_conv, m_b_conv, m_w_rg_a, m_b_rg_a, m_w_rg_x, m_b_rg_x, m_lru_lambda, m_w_sp, m_b_sp, m_ln_v_g, m_ln_v_b, m_w_o_lru, m_w_o_sgu, m_w_out, m_ln1_g, m_ln1_b, m_w_up, m_w_down, m_ln2_g, m_ln2_b, v_w_ada, v_b_ada, v_w_in, v_b_in, v_w_conv, v_b_conv, v_w_rg_a, v_b_rg_a, v_w_rg_x, v_b_rg_x, v_lru_lambda, v_w_sp, v_b_sp, v_ln_v_g, v_ln_v_b, v_w_o_lru, v_w_o_sgu, v_w_out, v_ln1_g, v_ln1_b, v_w_up, v_w_down, v_ln2_g, v_ln2_b):
    given = dict(locals())
    nb, seq, d = x.shape
    t = nb * seq
    w_lru = LRU_HEADS * HEAD
    d_sgu = SGU_GROUPS * HEAD
    xi, yi, ci = lax.axis_index("x"), lax.axis_index("y"), lax.axis_index("c")
    chip = 2 * xi + yi
    dev = 2 * chip + ci
    cidx = jnp.reshape(ci, (1,)).astype(jnp.int32)

    x2 = x.reshape(t, d)
    target = loss_target.reshape(t, d)

    big = ["w_in", "w_o_lru", "w_o_sgu", "w_out", "w_up", "w_down"]
    shards_a = [w_in[0].astype(BF16)]
    shards_b = [given[n][0].astype(BF16) for n in big[1:]]
    pidx = jnp.reshape(chip, (1,)).astype(jnp.int32)

    c_rows = _rows128(c)
    wconv_rows = _rows128(w_conv[0])
    slab0 = _all_gather_small(jnp.concatenate([c_rows, wconv_rows], axis=0), "gather_c_wconv")
    slab0 = slab0.reshape(N_DEV, -1, HEAD)
    c_all = slab0[:, :c_rows.shape[0]].reshape(N_DEV * nb, d)
    n_wc = CONV_WIDTH * (w_lru // N_CHIPS) // HEAD
    wc = slab0[0::2, c_rows.shape[0]:c_rows.shape[0] + n_wc].reshape(N_CHIPS, CONV_WIDTH, w_lru // N_CHIPS)
    w_conv_full = jnp.transpose(wc, (1, 0, 2)).reshape(CONV_WIDTH, w_lru)

    n_ada = w_ada.shape[2]
    b_ada_cols = lax.dynamic_slice(b_ada, (0, chip * n_ada), (1, n_ada))
    mod_cols = _ada_fwd(c_all, w_ada[0], b_ada_cols)
    half = (N_DEV * nb) // 2
    mod_half = lax.dynamic_slice(mod_cols, (ci * half, 0), (half, n_ada))
    mod_g = _all_gather_small(mod_half, "gather_mod").reshape(N_CHIPS, 2, half, n_ada)
    mod_all = jnp.transpose(mod_g, (1, 2, 0, 3)).reshape(N_DEV * nb, N_CHIPS * n_ada)
    mod_loc = lax.dynamic_slice(mod_all, (dev * nb, 0), (nb, N_CHIPS * n_ada)).reshape(nb, 6, d)
    modv = jnp.pad(mod_loc, ((0, 0), (0, 2), (0, 0)))

    lru_w = (w_conv_full, b_conv, w_rg_a[0], b_rg_a, w_rg_x[0], b_rg_x, lru_lambda)
    b_sp_t = jnp.transpose(b_sp[0])

    land = lambda s: jax.ShapeDtypeStruct((N_CHIPS,) + s.shape, s.dtype)
    sds = lambda s: jax.ShapeDtypeStruct(s.shape, s.dtype)
    started_a = _split_start(shards_a, [sds(shards_a[0])] * 2, _peer_gather_copies((0, 1)), 2, "gather_w_in_near_start",
                             after=(modv,))
    shards_b, shards_c = shards_b[:3], shards_b[3:]

    ids = lambda *v: jnp.stack(v).astype(jnp.int32)
    modv_t = modv + started_a[-1][0:1, 0:1]
    proj, h = _proj_fwd(x2, modv_t, [started_a[2]], ids(chip), b_in, seq, "proj_fwd_own")
    own_a, lands_a = _split_wait(started_a, 1, _peer_gather_copies((0, 1)), "gather_w_in_near_wait",
                                 after=(proj, *shards_b, *shards_c))
    started_f = _split_start(own_a, [sds(own_a[0])], _far_gather_copies, 1, "gather_w_in_far_start", after=(lands_a[0],))
    started_b = _split_start(shards_b, [land(s) for s in shards_b], _gather_copies, 3 * len(shards_b),
                             "gather_w_mix_start", after=(started_f[-1],))
    started_c = _split_start(shards_c, [land(s) for s in shards_c], _gather_copies, 3 * len(shards_c),
                             "gather_w_mlp_start", after=(started_b[-1],))
    modv_t = modv + started_c[-1][0:1, 0:1]
    (proj,) = _proj_fwd(x2, modv_t, lands_a, ids(chip ^ 1, chip ^ 2), b_in, seq, "proj_fwd_near", proj_in=proj)
    own_a, land_f = _split_wait(started_f, 1, _far_gather_copies, "gather_w_in_far_wait", after=(proj,))
    (proj,) = _proj_fwd(x2, modv, land_f, ids(chip ^ 3), b_in, seq, "proj_fwd_far", proj_in=proj)
    w_in_shards, w_in_chips = own_a + lands_a + land_f, ids(chip, chip ^ 1, chip ^ 2, chip ^ 3)
    a, inp, r16, gi16, xc16 = _lru_prep(proj, lru_w, nb, seq)
    a3 = a.reshape(nb, seq, w_lru)
    hs = _scan(a3, inp.reshape(nb, seq, w_lru), False, "lru_scan", BF16).reshape(t, w_lru)
    y_sgu = _sgu_fwd(proj, w_sp[0], b_sp_t, ln_v_g, ln_v_b)
    shards_b, lands_b = _split_wait(started_b, len(shards_b), _gather_copies, "gather_w_mix_wait", after=(hs, y_sgu))
    w_o_lru_g, w_o_sgu_g, w_out_g = _fill_own_slot(lands_b, shards_b, pidx, ["own_" + n for n in big[1:4]])
    w_o_lru_g = w_o_lru_g.reshape(w_lru, d)
    w_out_g = w_out_g.reshape(d, d)
    yap, y_a, y_b, merged, mix, x1 = _mix_fwd(hs, proj, y_sgu, x2, modv, w_o_lru_g, w_o_sgu_g, w_out_g, ln1_g, ln1_b, seq)
    shards_c, lands_c = _split_wait(started_c, len(shards_c), _gather_copies, "gather_w_mlp_wait", after=(x1,))
    w_up_g, w_down_g = _fill_own_slot(lands_c, shards_c, pidx, ["own_" + n for n in big[4:]])
    w_down_g = w_down_g.reshape(-1, d)
    up, act, h2, dz2, df, st2, pb2 = _mlp_fwd(x1, modv, w_up_g, w_down_g, ln2_g, ln2_b, target, nb, seq)

    part = {}

    def to_sibling_start(group, tag, after=()):
        g4 = []
        for n in group:
            shard = given[n].shape[1:]
            g4.append(part[n].reshape(N_CHIPS, 2, shard[0] // 2, shard[1]))
        shapes = [jax.ShapeDtypeStruct((N_CHIPS,) + g.shape[2:], F32) for g in g4]
        return _split_start(g4, shapes, _to_sibling_copies, len(g4), "grads_to_sibling_start_" + tag, after)

    def to_chips_start(group, started, tag, after=()):
        g4, recv = _split_wait(started, len(group), _to_sibling_copies, "grads_to_sibling_wait_" + tag, after)
        own4 = [_add_own_half(g4[k], recv[k], cidx, "grad_pair_sum_" + n) for k, n in enumerate(group)]
        shapes = [jax.ShapeDtypeStruct((3,) + o.shape[1:], BF16) for o in own4]
        return _split_start(own4, shapes, _chip_exchange_copies, 3 * len(own4), "grads_chip_exchange_start_" + tag)

    def chips_finish(group, started, tag, after=()):
        own4, slots = _split_wait(started, len(group), _chip_exchange_copies, "grads_chip_exchange_wait_" + tag, after)
        return [_sum_own_and_peers(own4[k], slots[k], pidx, "grad_chip_sum_" + n) for k, n in enumerate(group)]

    dup, dz1, dmix, st1, pb1 = _mlp_bwd(df, up, w_down_g, w_up_g, dz2, x2, mix, modv, ln1_g, ln1_b, nb, seq)
    group1 = ["w_up", "w_down"]
    part["w_up"] = _weight_grad(h2, dup, True, "grad_w_up")
    part["w_down"] = _weight_grad(act, df, False, "grad_w_down")
    sib1 = to_sibling_start(group1, "mlp")
    dy_a, dy_b, dproj, dyl, dys = _mix_bwd(dmix, proj, y_a, y_b, hs, w_out_g, w_o_lru_g, w_o_sgu_g, seq,
                                                after=(sib1[-1],))
    group2 = ["w_o_lru", "w_o_sgu", "w_out"]
    part["w_o_lru"] = _weight_grad(yap, dy_a, False, "grad_w_o_lru")
    part["w_o_sgu"] = _weight_grad(y_sgu, dy_b, True, "grad_w_o_sgu")
    part["w_out"] = _weight_grad(merged, dmix, False, "grad_w_out")
    chips1 = to_chips_start(group1, sib1, "mlp", after=(dys, part["w_o_lru"], part["w_o_sgu"], part["w_out"]))
    sib2 = to_sibling_start(group2, "mix", after=(chips1[-1],))
    du, dv, g_w_sp, st_sgu, g_b_sp_t = _sgu_bwd(proj, dys, w_sp[0], b_sp_t, ln_v_g, ln_v_b, after=(sib2[-1],))
    dyl3 = dyl.reshape(nb, seq, w_lru)
    e = _scan(a3, dyl3, True, "lru_scan_bwd", BF16).reshape(t, w_lru)
    chips2 = to_chips_start(group2, sib2, "mix", after=(e, du))
    dproj = lax.dynamic_update_slice(dproj, du, (0, 2 * w_lru))
    dproj = lax.dynamic_update_slice(dproj, dv, (0, 2 * w_lru + d_sgu))
    dproj, g_w_rg_a, g_w_rg_x, st_lru = _lru_bwd(proj, hs, e, dyl, (a, r16, gi16, xc16), lru_w, nb, seq, dproj,
                                                 after=(chips2[-1],))

    didx = jnp.reshape(dev, (1,)).astype(jnp.int32)
    early = [
        ("w_conv", st_lru[4:8]), ("b_conv", st_lru[3]), ("w_rg_a", g_w_rg_a), ("b_rg_a", st_lru[0]),
        ("w_rg_x", g_w_rg_x), ("b_rg_x", st_lru[1]), ("lru_lambda", st_lru[2]), ("w_sp", g_w_sp),
        ("b_sp", jnp.transpose(g_b_sp_t[:, :SGU_GROUPS])), ("ln_v_g", st_sgu[0]), ("ln_v_b", st_sgu[1]),
        ("ln1_g", st1[0]), ("ln1_b", st1[1]), ("ln2_g", st2[0]), ("ln2_b", st2[1]),
        ("loss", st2[2:3, 0:HEAD]),
    ]
    pieces_e = [_rows128(v) for _, v in early]
    slab_e = jnp.concatenate(pieces_e, axis=0)
    slab_e = jnp.pad(slab_e, ((0, (-slab_e.shape[0]) % TR_EW), (0, 0)))
    small_st = _split_start([slab_e], [jax.ShapeDtypeStruct((N_DEV,) + slab_e.shape, F32)], _all_devices_copies, N_DEV - 1,
                            "small_grads_start")

    group3 = ["w_in"]
    part["w_in"] = _weight_grad(h, dproj, True, "grad_w_in", after=(small_st[-1],))
    sib3 = to_sibling_start(group3, "in")
    halves12 = (chips_finish(group1, chips1, "mlp", after=(sib3[-1],))
                + chips_finish(group2, chips2, "mix", after=(sib3[-1],)))
    swap12 = _split_start(halves12, [jax.ShapeDtypeStruct(hv.shape, F32) for hv in halves12], _swap_copies, len(halves12),
                          "grads_swap_start")
    chips3 = to_chips_start(group3, sib3, "in", after=(swap12[-1],))
    grad_x2, g_b_in4, pb0 = _input_grad(dproj, w_in_shards, w_in_chips, dz1, x2, modv, nb, seq, after=(chips3[-1],))
    grads = {}
    two_d = lambda v: v.reshape(-1, v.shape[-1])
    done = {}

    def adamw_big(n, mine_n, theirs_n):
        done[n] = _adamw_halves(two_d(given[n]), mine_n, theirs_n, two_d(given["m_" + n]), two_d(given["v_" + n]), cidx,
                                "adamw_" + n)

    dmod_loc = jnp.stack([pb0[:, 1], pb0[:, 0], pb1[:, 2], pb1[:, 1], pb1[:, 0], pb2[:, 0]], axis=1)
    rows_dmod = dmod_loc.size // HEAD
    slab_l = jnp.concatenate([_rows128(dmod_loc), _rows128(g_b_in4[:, 0])], axis=0)
    late_st = _split_start([slab_l], [jax.ShapeDtypeStruct((N_DEV,) + slab_l.shape, F32)], _all_devices_copies, N_DEV - 1,
                           "late_grads_start")
    mine12, theirs12 = _split_wait(swap12, len(halves12), _swap_copies, "grads_swap_wait", after=(late_st[-1],))
    for n, mine_n, theirs_n in zip(group1 + group2, mine12, theirs12):
        adamw_big(n, mine_n, theirs_n)
    (slab_l,), (lands_l,) = _split_wait(late_st, 1, _all_devices_copies, "late_grads_wait",
                                        after=tuple(done[n][1] for n in group1 + group2))
    every = jnp.where(lax.broadcasted_iota(jnp.int32, (N_DEV, 1, 1), 0) == dev, slab_l[None], lands_l)
    dmod_all = every[:, :rows_dmod].reshape(N_DEV * nb, 6 * d)
    grads["b_in"] = _sum_slots(every[:, rows_dmod:], "grad_b_in_sum").reshape(1, -1)

    (slab_e,), (lands_e,) = _split_wait(small_st, 1, _all_devices_copies, "small_grads_wait", after=(dmod_all,))
    summed = _sum_devices(lands_e, slab_e, didx, "small_grad_sum")
    off = 0
    for (n, v), piece in zip(early, pieces_e):
        grads[n] = summed[off:off + v.size // HEAD].reshape(v.shape)
        off += piece.shape[0]
    loss = grads.pop("loss")[0, 0]

    (mine3,) = chips_finish(group3, chips3, "in", after=(summed,))
    (theirs3,) = _exchange([mine3], [jax.ShapeDtypeStruct(mine3.shape, F32)], _swap_copies, 1, "grads_swap_w_in")
    adamw_big("w_in", mine3, theirs3)

    dmod_cols = lax.dynamic_slice(dmod_all, (0, chip * n_ada), (N_DEV * nb, n_ada))
    grads["w_ada"], grads["b_ada"] = _ada_bwd(c_all, dmod_all, dmod_cols)
    n_wcs = w_lru // N_CHIPS
    grads["w_conv"] = lax.dynamic_slice(grads["w_conv"], (0, chip * n_wcs), (CONV_WIDTH, n_wcs))

    names = ['w_ada', 'b_ada', 'w_in', 'b_in', 'w_conv', 'b_conv', 'w_rg_a', 'b_rg_a', 'w_rg_x', 'b_rg_x', 'lru_lambda',
             'w_sp', 'b_sp', 'ln_v_g', 'ln_v_b', 'w_o_lru', 'w_o_sgu', 'w_out', 'ln1_g', 'ln1_b', 'w_up', 'w_down',
             'ln2_g', 'ln2_b']
    small_names = [n for n in names if n not in big and n != "w_ada"]
    small_out = _adamw_many([(two_d(given[n]), two_d(grads[n].reshape(given[n].shape)), two_d(given["m_" + n]),
                              two_d(given["v_" + n])) for n in small_names], "adamw_small")
    for n, res in zip(small_names, small_out):
        done[n] = (grads[n],) + tuple(res)
    done["w_ada"] = (grads["w_ada"],) + tuple(_adamw(two_d(given["w_ada"]), two_d(grads["w_ada"]), two_d(given["m_w_ada"]),
                                                     two_d(given["v_w_ada"]), "adamw_w_ada"))
    outs = [[done[n][k].reshape(given[n].shape) for n in names] for k in range(4)]
    return (loss, grad_x2.reshape(nb, seq, d), *outs[0], *outs[1], *outs[2], *outs[3])
```

```python
import functools
import math

import jax
import jax.numpy as jnp
from jax import lax
from jax.experimental import pallas as pl
from jax.experimental.pallas import tpu as pltpu

F32 = jnp.float32
BF16 = jnp.bfloat16
MESH = pl.DeviceIdType.MESH

N_CHIPS = 4
N_DEV = 8
LRU_HEADS = 10
HEAD = 128
SGU_GROUPS = 6
SGU_CHUNK = 64
CONV_WIDTH = 4
LRU_C = 8.0
ALPHA = 2.0 ** 0.25
LN_EPS = 1e-5
ADAM_LR, ADAM_B1, ADAM_B2, ADAM_EPS, ADAM_WD, ADAM_STEP = 0.001, 0.9, 0.999, 1e-08, 0.01, 10

VMEM_LIMIT = 56 * 1024 * 1024
VMEM_LIMIT_MAX = 62 * 1024 * 1024
TM_PROJ = 1024
TM_MIX = 512
TM_MLP = 512
TS_MLP = 256
TM_SGU = 512
TM_DH = 512
TT_DW = 4096
TC_SCAN = 512
TR_EW = 256


def _cp(sem=None, limit=None):
    return pltpu.CompilerParams(dimension_semantics=sem, vmem_limit_bytes=limit or VMEM_LIMIT)


def _mm(a, b):
    return jnp.dot(a.astype(BF16), b.astype(BF16), preferred_element_type=F32)


def _mm_nt(a, b):
    return lax.dot_general(a.astype(BF16), b.astype(BF16), (((1,), (1,)), ((), ())), preferred_element_type=F32)


def _mm_tn(a, b):
    return lax.dot_general(a.astype(BF16), b.astype(BF16), (((0,), (0,)), ((), ())), preferred_element_type=F32)


def _sigmoid(x):
    return 1.0 / (1.0 + jnp.exp(-x))


def _sigmoid_t(x):
    return 0.5 * jnp.tanh(0.5 * x) + 0.5


_GELU_K = math.sqrt(2.0 / math.pi)


def _gelu(x):
    t = jnp.tanh(_GELU_K * (x + 0.044715 * (x * x * x)))
    return 0.5 * x * (1.0 + t)


def _gelu_and_grad(x):
    x2 = x * x
    t = jnp.tanh(_GELU_K * (x + 0.044715 * (x2 * x)))
    g = 0.5 * x * (1.0 + t)
    dg = 0.5 * (1.0 + t) + 0.5 * x * (1.0 - t * t) * (_GELU_K * (1.0 + 3.0 * 0.044715 * x2))
    return g, dg


def _ln_stats(z):
    mu = jnp.mean(z, axis=-1, keepdims=True)
    zc = z - mu
    var = jnp.mean(zc * zc, axis=-1, keepdims=True)
    rstd = lax.rsqrt(var + LN_EPS)
    return zc * rstd, rstd


def _ln_bwd(dxh, xhat, rstd):
    m1 = jnp.mean(dxh, axis=-1, keepdims=True)
    m2 = jnp.mean(dxh * xhat, axis=-1, keepdims=True)
    return rstd * (dxh - m1 - xhat * m2)


def _colsum(v):
    return jnp.sum(v, axis=0, keepdims=True)


def _shift_down(v, j):
    if j == 0:
        return v
    rows = lax.broadcasted_iota(jnp.int32, v.shape, 0)
    return jnp.where(rows >= j, pltpu.roll(v, j, 0), 0.0)


def _shift_up(v, j):
    if j == 0:
        return v
    n = v.shape[0]
    rows = lax.broadcasted_iota(jnp.int32, v.shape, 0)
    return jnp.where(rows < n - j, pltpu.roll(v, n - j, 0), 0.0)


def _load_weights(srcs, dsts, sems):
    cps = [pltpu.make_async_copy(s, dd, sems.at[k]) for k, (s, dd) in enumerate(zip(srcs, dsts))]
    for cp in cps:
        cp.start()
    for cp in cps:
        cp.wait()


def _my_pos():
    return lax.axis_index("x"), lax.axis_index("y"), lax.axis_index("c")


def _all_gather_small(v, name, after=()):
    m_per, n = v.shape

    def body(x_ref, out_ref, send_sems, recv_sems, local_sem):
        x, y, c = _my_pos()
        me, sibling = (x, y, c), (x, y, 1 - c)
        chips = [(1 - x, y), (x, 1 - y), (1 - x, 1 - y)]

        def rows(px, py, pc):
            return out_ref.at[pl.ds((4 * px + 2 * py + pc) * m_per, m_per), :]

        def copy(k, block, to, src=None):
            return pltpu.make_async_remote_copy(
                src_ref=rows(*block) if src is None else src, dst_ref=rows(*block),
                send_sem=send_sems.at[k], recv_sem=recv_sems.at[k], device_id=to, device_id_type=MESH)

        mine = pltpu.make_async_copy(x_ref, rows(*me), local_sem)
        mine.start()
        first = [copy(0, me, sibling, src=x_ref)]
        first += [copy(1 + j, me, (*chip, c), src=x_ref) for j, chip in enumerate(chips)]
        for cp in first:
            cp.start()
        passed = [copy(4 + j, (*chip, c), sibling) for j, chip in enumerate(chips)]
        for j, chip in enumerate(chips):
            copy(1 + j, (*chip, c), me).wait_recv()
            passed[j].start()
        copy(0, sibling, me).wait_recv()
        for j, chip in enumerate(chips):
            copy(4 + j, (*chip, 1 - c), me).wait_recv()
        for cp in first + passed:
            cp.wait_send()
        mine.wait()

    return pl.pallas_call(
        _ordered(body, 1, after), name=name,
        out_shape=jax.ShapeDtypeStruct((N_DEV * m_per, n), v.dtype),
        in_specs=[pl.BlockSpec(memory_space=pltpu.VMEM)] + [pl.BlockSpec(memory_space=pl.ANY)] * len(after),
        out_specs=pl.BlockSpec(memory_space=pltpu.VMEM),
        scratch_shapes=[pltpu.SemaphoreType.DMA((7,)), pltpu.SemaphoreType.DMA((7,)), pltpu.SemaphoreType.DMA],
        compiler_params=pltpu.CompilerParams(vmem_limit_bytes=VMEM_LIMIT),
    )(v, *after)


_HBM = pl.BlockSpec(memory_space=pltpu.HBM)
_ANY = pl.BlockSpec(memory_space=pl.ANY)
_SEM = pl.BlockSpec(memory_space=pltpu.SEMAPHORE)
_EFFECT = pltpu.SideEffectType.DATAFLOW_SIDE_EFFECTING


def _ordered(body, n_in, after):
    k = len(after)
    if not k:
        return body
    return lambda *refs: body(*refs[:n_in], *refs[n_in + k:])


def _gather_copies(ins, lands, send_sems, recv_sems):
    x, y, c = _my_pos()
    p = 2 * x + y
    peers = [(x, 1 - y), (1 - x, y), (1 - x, 1 - y)]
    sends, recvs = [], []
    for k in range(len(ins)):
        for j, (qx, qy) in enumerate(peers):
            sems = dict(send_sem=send_sems.at[3 * k + j], recv_sem=recv_sems.at[3 * k + j],
                        device_id=(qx, qy, c), device_id_type=MESH)
            sends.append(pltpu.make_async_remote_copy(src_ref=ins[k], dst_ref=lands[k].at[p], **sems))
            recvs.append(pltpu.make_async_remote_copy(src_ref=ins[k], dst_ref=lands[k].at[2 * qx + qy], **sems))
    return sends, recvs


def _peer_gather_copies(peers):
    def copies(ins, lands, send_sems, recv_sems):
        x, y, c = _my_pos()
        where = [(x, 1 - y), (1 - x, y), (1 - x, 1 - y)]
        cps = [pltpu.make_async_remote_copy(
            src_ref=ins[0], dst_ref=lands[j], send_sem=send_sems.at[j], recv_sem=recv_sems.at[j],
            device_id=(*where[j], c), device_id_type=MESH) for j in peers]
        return cps, cps
    return copies


def _far_gather_copies(ins, lands, send_sems, recv_sems):
    x, y, c = _my_pos()
    cps = [pltpu.make_async_remote_copy(
        src_ref=ins[0], dst_ref=lands[0], send_sem=send_sems.at[0], recv_sem=recv_sems.at[0],
        device_id=(1 - x, 1 - y, c), device_id_type=MESH)]
    return cps, cps


def _to_sibling_copies(ins, lands, send_sems, recv_sems):
    x, y, c = _my_pos()
    cps = [pltpu.make_async_remote_copy(
        src_ref=ins[k].at[:, 1 - c], dst_ref=lands[k], send_sem=send_sems.at[k], recv_sem=recv_sems.at[k],
        device_id=(x, y, 1 - c), device_id_type=MESH) for k in range(len(ins))]
    return cps, cps


def _chip_exchange_copies(ins, lands, send_sems, recv_sems):
    x, y, c = _my_pos()
    peers = [(x, 1 - y), (1 - x, y), (1 - x, 1 - y)]
    cps = []
    for k in range(len(ins)):
        for j, (qx, qy) in enumerate(peers):
            cps.append(pltpu.make_async_remote_copy(
                src_ref=ins[k].at[2 * qx + qy], dst_ref=lands[k].at[j], send_sem=send_sems.at[3 * k + j],
                recv_sem=recv_sems.at[3 * k + j], device_id=(qx, qy, c), device_id_type=MESH))
    return cps, cps


def _all_devices_copies(ins, lands, send_sems, recv_sems):
    x, y, c = _my_pos()
    me = 4 * x + 2 * y + c
    sends, recvs = [], []
    for r in range(1, N_DEV):
        px = 1 - x if r & 4 else x
        py = 1 - y if r & 2 else y
        pc = 1 - c if r & 1 else c
        sems = dict(send_sem=send_sems.at[r - 1], recv_sem=recv_sems.at[r - 1], device_id=(px, py, pc), device_id_type=MESH)
        sends.append(pltpu.make_async_remote_copy(src_ref=ins[0], dst_ref=lands[0].at[me], **sems))
        recvs.append(pltpu.make_async_remote_copy(src_ref=ins[0], dst_ref=lands[0].at[4 * px + 2 * py + pc], **sems))
    return sends, recvs


def _swap_copies(ins, lands, send_sems, recv_sems):
    x, y, c = _my_pos()
    cps = [pltpu.make_async_remote_copy(
        src_ref=ins[k], dst_ref=lands[k], send_sem=send_sems.at[k], recv_sem=recv_sems.at[k],
        device_id=(x, y, 1 - c), device_id_type=MESH) for k in range(len(ins))]
    return cps, cps


def _split_start(ins, land_shapes, copies, n_sems, name, after=()):
    n, nl = len(ins), len(land_shapes)
    first_out = n + nl + len(after)

    def body(*refs):
        in_refs, land_refs = refs[:n], refs[n:n + nl]
        send_sems, recv_sems = refs[first_out:first_out + 2]
        token = refs[-1]
        sends, _ = copies(in_refs, land_refs, send_sems, recv_sems)
        for cp in sends:
            cp.start()
        token[...] = jnp.zeros_like(token)

    lands = [pltpu.with_memory_space_constraint(lax.empty(s.shape, s.dtype), pltpu.HBM) for s in land_shapes]
    ins = [pltpu.with_memory_space_constraint(s, pltpu.HBM) for s in ins]
    return pl.pallas_call(
        body, name=name,
        out_shape=(pltpu.SemaphoreType.DMA((n_sems,)), pltpu.SemaphoreType.DMA((n_sems,)),
                   *[pltpu.HBM(s.shape, s.dtype) for s in ins], *[pltpu.HBM(s.shape, s.dtype) for s in lands],
                   jax.ShapeDtypeStruct((8, HEAD), F32)),
        in_specs=[_HBM] * (n + nl) + [pl.BlockSpec(memory_space=pl.ANY)] * len(after),
        out_specs=(_SEM, _SEM, *([_HBM] * (n + nl)), pl.BlockSpec(memory_space=pltpu.VMEM)),
        input_output_aliases={k: 2 + k for k in range(n + nl)},
        compiler_params=pltpu.CompilerParams(has_side_effects=_EFFECT),
    )(*ins, *lands, *after)


def _split_wait(started, n, copies, name, after=()):
    send_sems, recv_sems = started[0], started[1]
    bufs = started[2:-1]
    nb = len(bufs)

    def body(*refs):
        in_refs, land_refs = refs[:n], refs[n:nb]
        sends, recvs = copies(in_refs, land_refs, refs[nb], refs[nb + 1])
        for cp in sends:
            cp.wait_send()
        for cp in recvs:
            cp.wait_recv()

    outs = pl.pallas_call(
        body, name=name,
        out_shape=tuple(pltpu.HBM(s.shape, s.dtype) for s in bufs),
        in_specs=[_HBM] * nb + [_SEM, _SEM] + [pl.BlockSpec(memory_space=pl.ANY)] * len(after),
        out_specs=tuple([_HBM] * nb),
        input_output_aliases={k: k for k in range(nb)},
        compiler_params=pltpu.CompilerParams(has_side_effects=_EFFECT),
    )(*bufs, send_sems, recv_sems, *after)
    return list(outs[:n]), list(outs[n:])


def _fill_own_slot(gathered, shards, pidx, names):
    outs = []
    for g, s, name in zip(gathered, shards, names):
        r, cdim = s.shape
        tr = _row_tile(r)

        def body(p_ref, s_ref, g_ref, o_ref):
            o_ref[...] = s_ref[...]

        outs.append(pl.pallas_call(
            body, name=name,
            grid_spec=pltpu.PrefetchScalarGridSpec(
                num_scalar_prefetch=1, grid=(r // tr,),
                in_specs=[pl.BlockSpec((tr, cdim), lambda i, p: (i, 0)), pl.BlockSpec(memory_space=pl.ANY)],
                out_specs=pl.BlockSpec((None, tr, cdim), lambda i, p: (p[0], i, 0))),
            out_shape=jax.ShapeDtypeStruct(g.shape, g.dtype),
            input_output_aliases={2: 0},
            compiler_params=_cp(("arbitrary",)),
        )(pidx, s, g))
    return outs


def _sum_own_and_peers(own4, slots, pidx, name):
    _, rh, cdim = own4.shape
    tr = _row_tile(rh)

    def body(p_ref, own_ref, s_ref, o_ref):
        acc = own_ref[...].astype(F32)
        for j in range(3):
            acc = acc + s_ref[j].astype(F32)
        o_ref[...] = acc

    return pl.pallas_call(
        body, name=name,
        grid_spec=pltpu.PrefetchScalarGridSpec(
            num_scalar_prefetch=1, grid=(rh // tr,),
            in_specs=[pl.BlockSpec((None, tr, cdim), lambda i, p: (p[0], i, 0)),
                      pl.BlockSpec((3, tr, cdim), lambda i, p: (0, i, 0))],
            out_specs=pl.BlockSpec((tr, cdim), lambda i, p: (i, 0))),
        out_shape=jax.ShapeDtypeStruct((rh, cdim), F32),
        compiler_params=_cp(("arbitrary",)),
    )(pidx, own4, slots)


def _exchange(ins, land_shapes, copies, n_sems, name):
    n, nl = len(ins), len(land_shapes)

    def body(*refs):
        sends, recvs = copies(refs[:n], refs[n:n + nl], refs[n + nl], refs[n + nl + 1])
        for cp in sends:
            cp.start()
        for cp in sends:
            cp.wait_send()
        for cp in recvs:
            cp.wait_recv()

    any_spec = pl.BlockSpec(memory_space=pl.ANY)
    return pl.pallas_call(
        body, name=name,
        out_shape=[jax.ShapeDtypeStruct(s.shape, s.dtype) for s in land_shapes],
        in_specs=[any_spec] * n, out_specs=[any_spec] * nl,
        scratch_shapes=[pltpu.SemaphoreType.DMA((n_sems,)), pltpu.SemaphoreType.DMA((n_sems,))],
    )(*ins)


def _row_tile(r):
    t = min(TR_EW, r)
    while r % t:
        t //= 2
    return t


def _add_own_half(g4, recv, cidx, name):
    _, _, rh, cdim = g4.shape
    tr = _row_tile(rh)

    def body(c_ref, a_ref, b_ref, o_ref):
        o_ref[...] = (a_ref[...] + b_ref[...]).astype(BF16)

    return pl.pallas_call(
        body, name=name,
        grid_spec=pltpu.PrefetchScalarGridSpec(
            num_scalar_prefetch=1, grid=(N_CHIPS, rh // tr),
            in_specs=[pl.BlockSpec((None, None, tr, cdim), lambda q, i, c: (q, c[0], i, 0)),
                      pl.BlockSpec((None, tr, cdim), lambda q, i, c: (q, i, 0))],
            out_specs=pl.BlockSpec((None, tr, cdim), lambda q, i, c: (q, i, 0))),
        out_shape=jax.ShapeDtypeStruct(recv.shape, BF16),
        compiler_params=_cp(("arbitrary", "arbitrary")),
    )(cidx, g4, recv)


def _sum_slots(v, name):
    n, r, cdim = v.shape
    tr = _row_tile(r)

    def body(v_ref, o_ref):
        acc = v_ref[0].astype(F32)
        for k in range(1, n):
            acc = acc + v_ref[k].astype(F32)
        o_ref[...] = acc

    return pl.pallas_call(
        body, name=name, grid=(r // tr,),
        in_specs=[pl.BlockSpec((n, tr, cdim), lambda i: (0, i, 0))],
        out_specs=pl.BlockSpec((tr, cdim), lambda i: (i, 0)),
        out_shape=jax.ShapeDtypeStruct((r, cdim), F32),
        compiler_params=_cp(("arbitrary",)),
    )(v)


def _sum_devices(lands, own, didx, name):
    _, r, cdim = lands.shape
    tr = _row_tile(r)

    def body(d_ref, l_ref, own_ref, o_ref):
        acc = jnp.where(d_ref[0] == 0, own_ref[...], l_ref[0])
        for dv in range(1, N_DEV):
            acc = acc + jnp.where(d_ref[0] == dv, own_ref[...], l_ref[dv])
        o_ref[...] = acc

    return pl.pallas_call(
        body, name=name,
        grid_spec=pltpu.PrefetchScalarGridSpec(
            num_scalar_prefetch=1, grid=(r // tr,),
            in_specs=[pl.BlockSpec((N_DEV, tr, cdim), lambda i, dd: (0, i, 0)), pl.BlockSpec((tr, cdim), lambda i, dd: (i, 0))],
            out_specs=pl.BlockSpec((tr, cdim), lambda i, dd: (i, 0))),
        out_shape=jax.ShapeDtypeStruct((r, cdim), F32),
        compiler_params=_cp(("arbitrary",)),
    )(didx, lands, own)


def _adamw_math(wv, gg, mv, vv):
    nm = ADAM_B1 * mv + (1.0 - ADAM_B1) * gg
    nv = ADAM_B2 * vv + (1.0 - ADAM_B2) * (gg * gg)
    m_hat = nm / (1.0 - ADAM_B1 ** ADAM_STEP)
    v_hat = nv / (1.0 - ADAM_B2 ** ADAM_STEP)
    return -ADAM_LR * (m_hat / (jnp.sqrt(v_hat) + ADAM_EPS) + ADAM_WD * wv), nm, nv


def _adamw_halves(w, mine, theirs, m, v, cidx, name):
    r, cdim = w.shape
    rh = r // 2
    tr = _row_tile(rh)
    nblk = rh // tr

    def body(c_ref, w_ref, a_ref, b_ref, m_ref, v_ref, g_ref, d_ref, nm_ref, nv_ref):
        gg = jnp.where(pl.program_id(0) == c_ref[0], a_ref[...], b_ref[...])
        g_ref[...] = gg
        d_ref[...], nm_ref[...], nv_ref[...] = _adamw_math(w_ref[...], gg, m_ref[...], v_ref[...])

    full = pl.BlockSpec((tr, cdim), lambda hh, i, c: (hh * nblk + i, 0))
    half = pl.BlockSpec((tr, cdim), lambda hh, i, c: (i, 0))
    return pl.pallas_call(
        body, name=name,
        grid_spec=pltpu.PrefetchScalarGridSpec(
            num_scalar_prefetch=1, grid=(2, nblk),
            in_specs=[full, half, half, full, full], out_specs=[full] * 4),
        out_shape=[jax.ShapeDtypeStruct((r, cdim), F32)] * 4,
        compiler_params=_cp(("arbitrary", "arbitrary")),
    )(cidx, w, mine, theirs, m, v)


def _adamw_many(params, name):
    n = len(params)

    def body(*refs):
        ins, outs = refs[:4 * n], refs[4 * n:]
        for k in range(n):
            w_ref, g_ref, m_ref, v_ref = ins[4 * k:4 * k + 4]
            outs[3 * k][...], outs[3 * k + 1][...], outs[3 * k + 2][...] = _adamw_math(
                w_ref[...], g_ref[...], m_ref[...], v_ref[...])

    flat = [a for p in params for a in p]
    res = pl.pallas_call(
        body, name=name,
        out_shape=[jax.ShapeDtypeStruct(p[0].shape, F32) for p in params for _ in range(3)],
        compiler_params=pltpu.CompilerParams(vmem_limit_bytes=VMEM_LIMIT),
    )(*flat)
    return [res[3 * k:3 * k + 3] for k in range(n)]


def _adamw(w, g, m, v, name):
    r, cdim = w.shape
    tr = _row_tile(r) if r % 8 == 0 else r

    def body(w_ref, g_ref, m_ref, v_ref, d_ref, nm_ref, nv_ref):
        d_ref[...], nm_ref[...], nv_ref[...] = _adamw_math(w_ref[...], g_ref[...], m_ref[...], v_ref[...])

    spec = pl.BlockSpec((tr, cdim), lambda i: (i, 0))
    return pl.pallas_call(
        body, name=name, grid=(r // tr,), in_specs=[spec] * 4, out_specs=[spec] * 3,
        out_shape=[jax.ShapeDtypeStruct((r, cdim), F32)] * 3,
        compiler_params=_cp(("arbitrary",)),
    )(w, g, m, v)


def _ada_fwd(c_all, w_ada, b_cols):
    nb, _ = c_all.shape
    n = w_ada.shape[1]

    def body(c_ref, w_ref, b_ref, o_ref):
        cv = c_ref[...]
        o_ref[...] = _mm(cv * _sigmoid(cv), w_ref[...]) + b_ref[...]

    return pl.pallas_call(
        body, name="ada_fwd", out_shape=jax.ShapeDtypeStruct((nb, n), F32),
        compiler_params=pltpu.CompilerParams(vmem_limit_bytes=VMEM_LIMIT),
    )(c_all, w_ada, b_cols)


def _ada_bwd(c_all, dmod_all, dmod_cols):
    d = c_all.shape[1]
    n = dmod_cols.shape[1]

    def body(c_ref, da_ref, dc_ref, gw_ref, gb_ref):
        cv = c_ref[...]
        gw_ref[...] = _mm_tn(cv * _sigmoid(cv), dc_ref[...])
        gb_ref[...] = _colsum(da_ref[...])

    return pl.pallas_call(
        body, name="ada_bwd",
        out_shape=[jax.ShapeDtypeStruct((d, n), F32), jax.ShapeDtypeStruct((1, dmod_all.shape[1]), F32)],
        compiler_params=pltpu.CompilerParams(vmem_limit_bytes=VMEM_LIMIT),
    )(c_all, dmod_all, dmod_cols)


def _proj_fwd(x2, modv, ws, cols, b_in, seq, name, proj_in=None):
    t, d = x2.shape
    n = len(ws)
    ns = ws[0].shape[1]
    tm = min(TM_PROJ, seq)
    tpb = seq // tm
    first = proj_in is None

    def body(c_ref, x_ref, mod_ref, *refs):
        w_refs, b_ref = refs[:n], refs[n]
        outs = refs[n + 1 if first else n + 2:]
        proj_ref, h_s = outs[0], outs[-1]
        s = pl.program_id(1)

        @pl.when(s == 0)
        def _():
            h = (x_ref[...] * (1.0 + mod_ref[1:2, :]) + mod_ref[0:1, :]).astype(BF16)
            h_s[...] = h
            if first:
                outs[1][...] = h

        for k in range(n):
            @pl.when(s == k)
            def _():
                proj_ref[...] = (jnp.dot(h_s[...], w_refs[k][...], preferred_element_type=F32) + b_ref[...]).astype(BF16)

    in_specs = [pl.BlockSpec((tm, d), lambda i, s, c: (i, 0)),
                pl.BlockSpec((None, 8, d), lambda i, s, c: (i // tpb, 0, 0))]
    in_specs += [pl.BlockSpec((d, ns), lambda i, s, c: (0, 0))] * n
    in_specs += [pl.BlockSpec((1, ns), lambda i, s, c: (0, c[s]))]
    out_specs = [pl.BlockSpec((tm, ns), lambda i, s, c: (i, c[s]))]
    out_shape = [jax.ShapeDtypeStruct((t, N_CHIPS * ns), BF16)]
    args = [cols, x2, modv, *ws, b_in]
    aliases = {}
    if first:
        out_specs.append(pl.BlockSpec((tm, d), lambda i, s, c: (i, 0)))
        out_shape.append(jax.ShapeDtypeStruct((t, d), BF16))
    else:
        in_specs.append(_ANY)
        args.append(proj_in)
        aliases = {len(args) - 1: 0}
    return pl.pallas_call(
        body, name=name,
        grid_spec=pltpu.PrefetchScalarGridSpec(
            num_scalar_prefetch=1, grid=(t // tm, n), in_specs=in_specs, out_specs=out_specs,
            scratch_shapes=[pltpu.VMEM((tm, d), BF16)]),
        out_shape=out_shape, input_output_aliases=aliases,
        compiler_params=_cp(("arbitrary", "arbitrary")),
    )(*args)


def _lru_rate(lam_ref):
    nl = -lam_ref[...]
    e = jnp.exp(-jnp.abs(nl))
    u = 1.0 + e
    dlt = u - 1.0
    log1p_e = jnp.where(dlt == 0.0, e, jnp.log(u) * (e / jnp.where(dlt == 0.0, 1.0, dlt)))
    return -LRU_C * (jnp.maximum(nl, 0.0) + log1p_e)


def _lru_gates(xl, wc_ref, bc_ref, wa_ref, ba_ref, wx_ref, bx_ref, lam_ref):
    xc = bc_ref[...] + wc_ref[CONV_WIDTH - 1:CONV_WIDTH, :] * xl
    for k in range(CONV_WIDTH - 1):
        xc = xc + wc_ref[k:k + 1, :] * _shift_down(xl, CONV_WIDTH - 1 - k)
    r = _sigmoid(_mm(xc, wa_ref[...]) + ba_ref[...])
    gi = _sigmoid_t(_mm(xc, wx_ref[...]) + bx_ref[...])
    big_l = _lru_rate(lam_ref)
    la = big_l * r
    a = jnp.exp(la)
    m2 = jnp.tanh(-la) * (a * a + 1.0)
    return xc, r, gi, big_l, a, m2


def _lru_prep(proj, lru_w, nb, seq):
    t = proj.shape[0]
    w = LRU_HEADS * HEAD
    w_conv, b_conv, w_a, b_a, w_x, b_x, lam = lru_w

    def body(x_ref, wc_ref, bc_ref, wa_ref, ba_ref, wx_ref, bx_ref, lam_ref, a_ref, inp_ref, r_ref, gi_ref, xc_ref):
        xc, r, gi, big_l, a, m2 = _lru_gates(x_ref[...].astype(F32), wc_ref, bc_ref, wa_ref, ba_ref, wx_ref, bx_ref, lam_ref)
        a_ref[...] = a
        inp_ref[...] = (jnp.sqrt(m2) * (gi * xc)).astype(BF16)
        r_ref[...] = r.astype(BF16)
        gi_ref[...] = gi.astype(BF16)
        xc_ref[...] = xc.astype(BF16)

    col = lambda b, hd: (0, hd)
    head = lambda b, hd: (hd, 0, 0)
    tok = lambda b, hd: (b, hd)
    return pl.pallas_call(
        body, name="lru_prep", grid=(nb, LRU_HEADS),
        in_specs=[pl.BlockSpec((seq, HEAD), tok),
                  pl.BlockSpec((CONV_WIDTH, HEAD), col), pl.BlockSpec((1, HEAD), col),
                  pl.BlockSpec((None, HEAD, HEAD), head), pl.BlockSpec((1, HEAD), col),
                  pl.BlockSpec((None, HEAD, HEAD), head), pl.BlockSpec((1, HEAD), col),
                  pl.BlockSpec((1, HEAD), col)],
        out_specs=[pl.BlockSpec((seq, HEAD), tok)] * 5,
        out_shape=[jax.ShapeDtypeStruct((t, w), F32)] + [jax.ShapeDtypeStruct((t, w), BF16)] * 4,
        compiler_params=_cp(("arbitrary", "arbitrary")),
    )(proj, w_conv, b_conv, w_a, b_a, w_x, b_x, lam)


def _scan(a3, b3, reverse, name, out_dtype):
    nb, seq, w = a3.shape
    tc = min(TC_SCAN, seq)
    nchunk = seq // tc
    npair = tc // 16

    def combine(av, bv):
        rows = lax.broadcasted_iota(jnp.int32, av.shape, 0)
        for s in (1, 2, 4):
            if reverse:
                keep = rows < 8 - s
                a_sh, b_sh = pltpu.roll(av, 8 - s, 0), pltpu.roll(bv, 8 - s, 0)
            else:
                keep = rows >= s
                a_sh, b_sh = pltpu.roll(av, s, 0), pltpu.roll(bv, s, 0)
            bv = jnp.where(keep, bv + av * b_sh, bv)
            av = jnp.where(keep, av * a_sh, av)
        return av, bv

    def body(a_ref, b_ref, h_ref, carry):
        @pl.when(pl.program_id(0) == 0)
        def _():
            carry[...] = jnp.zeros_like(carry)

        for b in range(nb):
            def pair(j, hprev):
                jj = npair - 1 - j if reverse else j
                base = pl.multiple_of(jj * 16, 16)
                a16 = a_ref[b, pl.ds(base, 16), :]
                b16 = b_ref[b, pl.ds(base, 16), :].astype(F32)
                outs = [None, None]
                for k in ((1, 0) if reverse else (0, 1)):
                    av, bv = a16[8 * k:8 * k + 8, :], b16[8 * k:8 * k + 8, :]
                    av, bv = combine(av, av * bv if reverse else bv)
                    h = bv + av * hprev
                    outs[k] = h
                    hprev = jnp.broadcast_to(h[0:1, :] if reverse else h[7:8, :], (8, w))
                h_ref[b, pl.ds(base, 16), :] = jnp.concatenate(outs, axis=0).astype(out_dtype)
                return hprev

            carry[b] = lax.fori_loop(0, npair, pair, carry[b])

    imap = (lambda i: (0, nchunk - 1 - i, 0)) if reverse else (lambda i: (0, i, 0))
    spec = pl.BlockSpec((nb, tc, w), imap)
    return pl.pallas_call(
        body, name=name, grid=(nchunk,), in_specs=[spec, spec], out_specs=spec,
        out_shape=jax.ShapeDtypeStruct((nb, seq, w), out_dtype),
        scratch_shapes=[pltpu.VMEM((nb, 8, w), F32)],
        compiler_params=_cp(("arbitrary",)),
    )(a3, b3)


def _sgu_mask():
    ti = lax.broadcasted_iota(jnp.int32, (HEAD, HEAD), 0) // SGU_CHUNK
    si = lax.broadcasted_iota(jnp.int32, (HEAD, HEAD), 1) // SGU_CHUNK
    return si <= ti


def _sgu_specs(tm, d_sgu):
    pw = 256
    first_u = (2 * LRU_HEADS * HEAD) // pw
    n_piece = d_sgu // pw
    specs = [pl.BlockSpec((tm, pw), functools.partial(lambda i, k: (i, k), k=first_u + j)) for j in range(2 * n_piece)]
    return specs, n_piece


def _sgu_fwd(proj, w_sp, b_sp_t, ln_g, ln_b):
    t = proj.shape[0]
    d_sgu = SGU_GROUPS * HEAD
    tm = min(TM_SGU, t)
    nblk = tm // HEAD
    specs, n_piece = _sgu_specs(tm, d_sgu)

    def body(*refs):
        u = jnp.concatenate([r[...] for r in refs[:n_piece]], axis=1).astype(F32)
        v = jnp.concatenate([r[...] for r in refs[n_piece:2 * n_piece]], axis=1).astype(F32)
        w_ref, bt_ref, g_ref, b_ref, y_ref = refs[2 * n_piece:]
        ug = _gelu(u)
        xhat, _ = _ln_stats(_gelu(v))
        vn = (xhat * g_ref[...] + b_ref[...]).astype(BF16)
        mask = _sgu_mask()
        for g in range(SGU_GROUPS):
            wm = jnp.where(mask, w_ref[g], 0.0).astype(BF16)
            cols = slice(g * HEAD, (g + 1) * HEAD)
            for n in range(nblk):
                rows = slice(n * HEAD, (n + 1) * HEAD)
                mixed = jnp.dot(wm, vn[rows, cols], preferred_element_type=F32) + bt_ref[:, g:g + 1]
                y_ref[rows, cols] = (ug[rows, cols] * mixed).astype(BF16)

    full = lambda shape: pl.BlockSpec(shape, lambda i: (0,) * len(shape))
    return pl.pallas_call(
        body, name="sgu_fwd", grid=(t // tm,),
        in_specs=specs + [full(w_sp.shape), full(b_sp_t.shape), full(ln_g.shape), full(ln_b.shape)],
        out_specs=pl.BlockSpec((tm, d_sgu), lambda i: (i, 0)),
        out_shape=jax.ShapeDtypeStruct((t, d_sgu), BF16),
        compiler_params=_cp(("arbitrary",)),
    )(*([proj] * (2 * n_piece)), w_sp, b_sp_t, ln_g, ln_b)


def _mix_fwd(hs, proj, y_sgu, x2, modv, w_o_lru_g, w_o_sgu_g, w_out_g, ln1_g, ln1_b, seq):
    t, d = x2.shape
    w = hs.shape[1]
    d_sgu = y_sgu.shape[1]
    nq, _, ns = w_o_sgu_g.shape
    tm = min(TM_MIX, seq)
    ts = min(TS_MLP, tm)
    tpb = seq // tm

    def body(hs_ref, gl_ref, ys_ref, ga_ref, gb_ref, x_ref, mod_ref, wl_hbm, ws_hbm, wo_hbm, g1_ref, b1_ref,
             yap_ref, ya_ref, yb_ref, mg_ref, mix_ref, x1_ref, wl_ref, ws_ref, wo_ref, sems):
        @pl.when(pl.program_id(0) == 0)
        def _():
            _load_weights((wl_hbm, ws_hbm, wo_hbm), (wl_ref, ws_ref, wo_ref), sems)

        for sub in range(tm // ts):
            rows = slice(sub * ts, (sub + 1) * ts)
            yap = (hs_ref[rows, :].astype(F32) * _gelu(gl_ref[rows, :].astype(F32))).astype(BF16)
            yap_ref[rows, :] = yap
            y_a = jnp.dot(yap, wl_ref[...], preferred_element_type=F32)
            ys = ys_ref[rows, :]
            y_b = jnp.concatenate([jnp.dot(ys, ws_ref[q], preferred_element_type=F32) for q in range(nq)], axis=1)
            ya_ref[rows, :] = y_a.astype(BF16)
            yb_ref[rows, :] = y_b.astype(BF16)
            merged = (_sigmoid_t(ga_ref[rows, :].astype(F32)) * y_a
                      + _sigmoid_t(gb_ref[rows, :].astype(F32)) * y_b).astype(BF16)
            mg_ref[rows, :] = merged
            mix = jnp.dot(merged, wo_ref[...], preferred_element_type=F32)
            mix_ref[rows, :] = mix
            xhat, _ = _ln_stats(ALPHA * x_ref[rows, :] + (1.0 + mod_ref[2:3, :]) * mix)
            x1_ref[rows, :] = xhat * g1_ref[...] + b1_ref[...]

    row = lambda width, col: pl.BlockSpec((tm, width), functools.partial(lambda i, k: (i, k), k=col))
    full = lambda shape: pl.BlockSpec(shape, lambda i: (0,) * len(shape))
    return pl.pallas_call(
        body, name="mix_fwd", grid=(t // tm,),
        in_specs=[row(w, 0), row(w, 1), row(d_sgu, 0), row(d, 4), row(d, 5), row(d, 0),
                  pl.BlockSpec((None, 8, d), lambda i: (i // tpb, 0, 0)),
                  _ANY, _ANY, _ANY, full(ln1_g.shape), full(ln1_b.shape)],
        out_specs=[row(w, 0), row(d, 0), row(d, 0), row(d, 0), row(d, 0), row(d, 0)],
        out_shape=[jax.ShapeDtypeStruct((t, w), BF16), jax.ShapeDtypeStruct((t, d), BF16),
                   jax.ShapeDtypeStruct((t, d), BF16), jax.ShapeDtypeStruct((t, d), BF16),
                   jax.ShapeDtypeStruct((t, d), F32), jax.ShapeDtypeStruct((t, d), F32)],
        scratch_shapes=[pltpu.VMEM(w_o_lru_g.shape, BF16), pltpu.VMEM(w_o_sgu_g.shape, BF16),
                        pltpu.VMEM(w_out_g.shape, BF16), pltpu.SemaphoreType.DMA((3,))],
        compiler_params=_cp(("arbitrary",)),
    )(hs, proj, y_sgu, proj, proj, x2, modv, w_o_lru_g, w_o_sgu_g, w_out_g, ln1_g, ln1_b)


def _mlp_fwd(x1, modv, w_up_g, w_down_g, ln2_g, ln2_b, target, nb, seq):
    t, d = x1.shape
    nq, _, ns = w_up_g.shape
    tm = min(TM_MLP, seq)
    ts = min(TS_MLP, tm)
    tpb = seq // tm

    def body(x1_ref, mod_ref, wu_hbm, wd_hbm, g2_ref, b2_ref, tg_ref,
             rl_ref, act_ref, h2_ref, dz2_ref, df_ref, st_ref, pb_ref, wu_s, wd_s, acc, sems):
        i = pl.program_id(0)

        @pl.when(i == 0)
        def _():
            _load_weights((wu_hbm, wd_hbm), (wu_s, wd_s), sems)
            st_ref[...] = jnp.zeros_like(st_ref)

        @pl.when(i % tpb == 0)
        def _():
            pb_ref[...] = jnp.zeros_like(pb_ref)

        for sub in range(tm // ts):
            rows = slice(sub * ts, (sub + 1) * ts)
            x1v = x1_ref[rows, :]
            h2 = (x1v * (1.0 + mod_ref[4:5, :]) + mod_ref[3:4, :]).astype(BF16)
            h2_ref[rows, :] = h2
            for k in range(nq):
                cols = slice(k * ns, (k + 1) * ns)
                r = jnp.maximum(jnp.dot(h2, wu_s[k], preferred_element_type=F32), 0.0)
                act = (r * r).astype(BF16)
                rl_ref[rows, cols] = r.astype(BF16)
                act_ref[rows, cols] = act
                part = jnp.dot(act, wd_s[cols, :], preferred_element_type=F32)
                if k == 0:
                    acc[sub] = part
                else:
                    acc[sub] += part
            f = acc[sub]
            xhat, rstd = _ln_stats(ALPHA * x1v + (1.0 + mod_ref[5:6, :]) * f)
            y = xhat * g2_ref[...] + b2_ref[...]
            err = y - tg_ref[rows, :]
            dy = err * (1.0 / d)
            dz2 = _ln_bwd(dy * g2_ref[...], xhat, rstd)
            dz2_ref[rows, :] = dz2
            df_ref[rows, :] = ((1.0 + mod_ref[5:6, :]) * dz2).astype(BF16)
            st_ref[0:1, :] += _colsum(dy * xhat)
            st_ref[1:2, :] += _colsum(dy)
            st_ref[2:3, :] += (0.5 / d) * jnp.sum(_colsum(err * err), axis=1, keepdims=True)
            pb_ref[0:1, :] += _colsum(dz2 * f)

    tok = lambda i: (i, 0)
    return pl.pallas_call(
        body, name="mlp_fwd", grid=(t // tm,),
        in_specs=[pl.BlockSpec((tm, d), tok), pl.BlockSpec((None, 8, d), lambda i: (i // tpb, 0, 0)), _ANY, _ANY,
                  pl.BlockSpec((1, d), lambda i: (0, 0)), pl.BlockSpec((1, d), lambda i: (0, 0)),
                  pl.BlockSpec((tm, d), tok)],
        out_specs=[pl.BlockSpec((tm, nq * ns), tok), pl.BlockSpec((tm, nq * ns), tok),
                   pl.BlockSpec((tm, d), tok), pl.BlockSpec((tm, d), tok), pl.BlockSpec((tm, d), tok),
                   pl.BlockSpec((8, d), lambda i: (0, 0)), pl.BlockSpec((None, 8, d), lambda i: (i // tpb, 0, 0))],
        out_shape=[jax.ShapeDtypeStruct((t, nq * ns), BF16), jax.ShapeDtypeStruct((t, nq * ns), BF16),
                   jax.ShapeDtypeStruct((t, d), BF16),
                   jax.ShapeDtypeStruct((t, d), F32), jax.ShapeDtypeStruct((t, d), BF16),
                   jax.ShapeDtypeStruct((8, d), F32), jax.ShapeDtypeStruct((nb, 8, d), F32)],
        scratch_shapes=[pltpu.VMEM(w_up_g.shape, BF16), pltpu.VMEM(w_down_g.shape, BF16),
                        pltpu.VMEM((tm // ts, ts, d), F32), pltpu.SemaphoreType.DMA((2,))],
        compiler_params=_cp(("arbitrary",)),
    )(x1, modv, w_up_g, w_down_g, ln2_g, ln2_b, target)


def _mlp_bwd(df, up, w_down_g, w_up_g, dz2, x2, mix, modv, ln1_g, ln1_b, nb, seq):
    t, d = x2.shape
    nq, _, ns = w_up_g.shape
    tm = min(TM_MLP, seq)
    ts = min(TS_MLP, tm)
    tpb = seq // tm

    def body(df_ref, rl_ref, wd_hbm, wu_hbm, dz2_ref, x_ref, mix_ref, mod_ref, g1_ref, b1_ref,
             dup_ref, dz1_ref, dmix_ref, st_ref, pb_ref, wd_s, wu_s, acc, sems):
        i = pl.program_id(0)

        @pl.when(i == 0)
        def _():
            _load_weights((wd_hbm, wu_hbm), (wd_s, wu_s), sems)
            st_ref[...] = jnp.zeros_like(st_ref)

        @pl.when(i % tpb == 0)
        def _():
            pb_ref[...] = jnp.zeros_like(pb_ref)

        for sub in range(tm // ts):
            rows = slice(sub * ts, (sub + 1) * ts)
            dfv = df_ref[rows, :]
            for k in range(nq):
                cols = slice(k * ns, (k + 1) * ns)
                dup = (_mm_nt(dfv, wd_s[cols, :]) * (2.0 * rl_ref[rows, cols].astype(F32))).astype(BF16)
                dup_ref[rows, cols] = dup
                part = _mm_nt(dup, wu_s[k])
                if k == 0:
                    acc[sub] = part
                else:
                    acc[sub] += part
            dh2 = acc[sub]
            mix = mix_ref[rows, :]
            xhat, rstd = _ln_stats(ALPHA * x_ref[rows, :] + (1.0 + mod_ref[2:3, :]) * mix)
            x1 = xhat * g1_ref[...] + b1_ref[...]
            dx1 = ALPHA * dz2_ref[rows, :] + dh2 * (1.0 + mod_ref[4:5, :])
            dz1 = _ln_bwd(dx1 * g1_ref[...], xhat, rstd)
            dz1_ref[rows, :] = dz1
            dmix_ref[rows, :] = ((1.0 + mod_ref[2:3, :]) * dz1).astype(BF16)
            st_ref[0:1, :] += _colsum(dx1 * xhat)
            st_ref[1:2, :] += _colsum(dx1)
            pb_ref[0:1, :] += _colsum(dh2 * x1)
            pb_ref[1:2, :] += _colsum(dh2)
            pb_ref[2:3, :] += _colsum(dz1 * mix)

    tok = lambda i: (i, 0)
    return pl.pallas_call(
        body, name="mlp_bwd", grid=(t // tm,),
        in_specs=[pl.BlockSpec((tm, d), tok), pl.BlockSpec((tm, nq * ns), tok), _ANY, _ANY,
                  pl.BlockSpec((tm, d), tok), pl.BlockSpec((tm, d), tok), pl.BlockSpec((tm, d), tok),
                  pl.BlockSpec((None, 8, d), lambda i: (i // tpb, 0, 0)),
                  pl.BlockSpec((1, d), lambda i: (0, 0)), pl.BlockSpec((1, d), lambda i: (0, 0))],
        out_specs=[pl.BlockSpec((tm, nq * ns), tok),
                   pl.BlockSpec((tm, d), tok), pl.BlockSpec((tm, d), tok),
                   pl.BlockSpec((8, d), lambda i: (0, 0)), pl.BlockSpec((None, 8, d), lambda i: (i // tpb, 0, 0))],
        out_shape=[jax.ShapeDtypeStruct((t, nq * ns), BF16),
                   jax.ShapeDtypeStruct((t, d), F32), jax.ShapeDtypeStruct((t, d), BF16),
                   jax.ShapeDtypeStruct((8, d), F32), jax.ShapeDtypeStruct((nb, 8, d), F32)],
        scratch_shapes=[pltpu.VMEM(w_down_g.shape, BF16), pltpu.VMEM(w_up_g.shape, BF16),
                        pltpu.VMEM((tm // ts, ts, d), F32), pltpu.SemaphoreType.DMA((2,))],
        compiler_params=_cp(("arbitrary",), VMEM_LIMIT_MAX),
    )(df, up, w_down_g, w_up_g, dz2, x2, mix, modv, ln1_g, ln1_b)


def _mix_bwd(dmix, proj, y_a, y_b, hs, w_out_g, w_o_lru_g, w_o_sgu_g, seq, after=()):
    t, d = dmix.shape
    w = hs.shape[1]
    nq, d_sgu, ns = w_o_sgu_g.shape
    tm = min(TM_MIX, seq)
    ts = min(TS_MLP, tm)

    def body(dmix_ref, ga_ref, gb_ref, ya_ref, yb_ref, gl_ref, hs_ref, wo_hbm, wl_hbm, ws_hbm,
             dya_ref, dyb_ref, dg_ref, dyl_ref, dys_ref, wo_ref, wl_ref, ws_ref, sems):
        @pl.when(pl.program_id(0) == 0)
        def _():
            _load_weights((wo_hbm, wl_hbm, ws_hbm), (wo_ref, wl_ref, ws_ref), sems)

        for sub in range(tm // ts):
            rows = slice(sub * ts, (sub + 1) * ts)
            dmerged = _mm_nt(dmix_ref[rows, :], wo_ref[...])
            sa, sb = _sigmoid_t(ga_ref[rows, :].astype(F32)), _sigmoid_t(gb_ref[rows, :].astype(F32))
            dy_a = (dmerged * sa).astype(BF16)
            dy_b = (dmerged * sb).astype(BF16)
            dya_ref[rows, :] = dy_a
            dyb_ref[rows, :] = dy_b
            dg_ref[rows, 4 * d:5 * d] = (dmerged * ya_ref[rows, :].astype(F32) * (sa * (1.0 - sa))).astype(BF16)
            dg_ref[rows, 5 * d:6 * d] = (dmerged * yb_ref[rows, :].astype(F32) * (sb * (1.0 - sb))).astype(BF16)
            dyap = _mm_nt(dy_a, wl_ref[...])
            gel, dgel = _gelu_and_grad(gl_ref[rows, :].astype(F32))
            dyl_ref[rows, :] = (dyap * gel).astype(BF16)
            dg_ref[rows, w:2 * w] = (dyap * hs_ref[rows, :].astype(F32) * dgel).astype(BF16)
            dys = _mm_nt(dy_b[:, 0:ns], ws_ref[0])
            for q in range(1, nq):
                dys = dys + _mm_nt(dy_b[:, q * ns:(q + 1) * ns], ws_ref[q])
            dys_ref[rows, :] = dys

    row = lambda width, col: pl.BlockSpec((tm, width), functools.partial(lambda i, k: (i, k), k=col))
    return pl.pallas_call(
        _ordered(body, 10, after), name="mix_bwd", grid=(t // tm,),
        in_specs=[row(d, 0), row(d, 4), row(d, 5), row(d, 0), row(d, 0), row(w, 1), row(w, 0),
                  _ANY, _ANY, _ANY] + [_ANY] * len(after),
        scratch_shapes=[pltpu.VMEM(w_out_g.shape, BF16), pltpu.VMEM(w_o_lru_g.shape, BF16),
                        pltpu.VMEM(w_o_sgu_g.shape, BF16), pltpu.SemaphoreType.DMA((3,))],
        out_specs=[row(d, 0), row(d, 0), row(6 * d, 0), row(w, 0), row(d_sgu, 0)],
        out_shape=[jax.ShapeDtypeStruct((t, d), BF16), jax.ShapeDtypeStruct((t, d), BF16),
                   jax.ShapeDtypeStruct((t, 6 * d), BF16), jax.ShapeDtypeStruct((t, w), BF16),
                   jax.ShapeDtypeStruct((t, d_sgu), F32)],
        compiler_params=_cp(("arbitrary",)),
    )(dmix, proj, proj, y_a, y_b, proj, hs, w_out_g, w_o_lru_g, w_o_sgu_g, *after)


def _sgu_bwd(proj, dys, w_sp, b_sp_t, ln_g, ln_b, after=()):
    t = proj.shape[0]
    d_sgu = SGU_GROUPS * HEAD
    tm = min(TM_SGU, t)
    nblk = tm // HEAD
    specs, n_piece = _sgu_specs(tm, d_sgu)

    def body(*refs):
        u = jnp.concatenate([r[...] for r in refs[:n_piece]], axis=1).astype(F32)
        v = jnp.concatenate([r[...] for r in refs[n_piece:2 * n_piece]], axis=1).astype(F32)
        dys_ref, w_ref, bt_ref, g_ref, b_ref, du_ref, dv_ref, dw_ref, st_ref, dbt_ref, dvn_s = refs[2 * n_piece:]

        @pl.when(pl.program_id(0) == 0)
        def _():
            dw_ref[...] = jnp.zeros_like(dw_ref)
            st_ref[...] = jnp.zeros_like(st_ref)
            dbt_ref[...] = jnp.zeros_like(dbt_ref)

        ug, dug_du = _gelu_and_grad(u)
        vg, dvg_dv = _gelu_and_grad(v)
        xhat, rstd = _ln_stats(vg)
        vn = (xhat * g_ref[...] + b_ref[...]).astype(BF16)
        dys_v = dys_ref[...]
        mask = _sgu_mask()
        for g in range(SGU_GROUPS):
            wm = jnp.where(mask, w_ref[g], 0.0).astype(BF16)
            cols = slice(g * HEAD, (g + 1) * HEAD)
            dw_g = jnp.zeros((HEAD, HEAD), F32)
            db_g = jnp.zeros((HEAD, 1), F32)
            for n in range(nblk):
                rows = slice(n * HEAD, (n + 1) * HEAD)
                vn_blk = vn[rows, cols]
                mixed = jnp.dot(wm, vn_blk, preferred_element_type=F32) + bt_ref[:, g:g + 1]
                dy_blk = dys_v[rows, cols]
                du_ref[rows, cols] = (dy_blk * mixed * dug_du[rows, cols]).astype(BF16)
                dmx = dy_blk * ug[rows, cols]
                dvn_s[rows, cols] = _mm_tn(wm, dmx)
                dw_g = dw_g + _mm_nt(dmx, vn_blk)
                db_g = db_g + jnp.sum(dmx, axis=1, keepdims=True)
            dw_ref[g] += jnp.where(mask, dw_g, 0.0)
            dbt_ref[:, g:g + 1] += db_g
        dvn = dvn_s[...]
        st_ref[0:1, :] += _colsum(dvn * xhat)
        st_ref[1:2, :] += _colsum(dvn)
        dv_ref[...] = (_ln_bwd(dvn * g_ref[...], xhat, rstd) * dvg_dv).astype(BF16)

    full = lambda shape: pl.BlockSpec(shape, lambda i: (0,) * len(shape))
    tok = pl.BlockSpec((tm, d_sgu), lambda i: (i, 0))
    return pl.pallas_call(
        _ordered(body, 2 * n_piece + 5, after), name="sgu_bwd", grid=(t // tm,),
        in_specs=specs + [tok, full(w_sp.shape), full(b_sp_t.shape), full(ln_g.shape), full(ln_b.shape)]
        + [_ANY] * len(after),
        out_specs=[tok, tok, full(w_sp.shape), full((8, d_sgu)), full((HEAD, HEAD))],
        out_shape=[jax.ShapeDtypeStruct((t, d_sgu), BF16), jax.ShapeDtypeStruct((t, d_sgu), BF16),
                   jax.ShapeDtypeStruct(w_sp.shape, F32), jax.ShapeDtypeStruct((8, d_sgu), F32),
                   jax.ShapeDtypeStruct((HEAD, HEAD), F32)],
        scratch_shapes=[pltpu.VMEM((tm, d_sgu), F32)],
        compiler_params=_cp(("arbitrary",)),
    )(*([proj] * (2 * n_piece)), dys, w_sp, b_sp_t, ln_g, ln_b, *after)


def _lru_bwd(proj, hs, e, dyl, saved, lru_w, nb, seq, dproj, after=()):
    t = proj.shape[0]
    w = LRU_HEADS * HEAD
    w_conv, b_conv, w_a, b_a, w_x, b_x, lam = lru_w

    def body(x_ref, hs_ref, e_ref, dy_ref, a_ref, r_ref, gi_ref, xc_ref, wc_ref, wa_ref, wx_ref, lam_ref,
             dxl_ref, dwa_ref, dwx_ref, st_ref):
        @pl.when(pl.program_id(1) == 0)
        def _():
            dwa_ref[...] = jnp.zeros_like(dwa_ref)
            dwx_ref[...] = jnp.zeros_like(dwx_ref)
            st_ref[...] = jnp.zeros_like(st_ref)

        xl = x_ref[...].astype(F32)
        a, r, gi, xc = a_ref[...], r_ref[...].astype(F32), gi_ref[...].astype(F32), xc_ref[...].astype(F32)
        big_l = _lru_rate(lam_ref)
        m2 = (1.0 - a) * (1.0 + a)
        inv_mult = lax.rsqrt(m2)
        mult = m2 * inv_mult
        dh = dy_ref[...].astype(F32) + _shift_up(e_ref[...].astype(F32), 1)
        da = dh * _shift_down(hs_ref[...].astype(F32), 1)
        dmult = dh * (gi * xc)
        d_i = dh * (mult * xc)
        dxc = dh * (mult * gi)
        dla = a * (da - dmult * (a * inv_mult))
        dr = dla * big_l
        d_big_l = _colsum(dla * r)
        dra = dr * (r * (1.0 - r))
        dia = d_i * (gi * (1.0 - gi))
        dwa_ref[...] += _mm_tn(xc, dra)
        dwx_ref[...] += _mm_tn(xc, dia)
        dxc = dxc + _mm_nt(dra, wa_ref[...]) + _mm_nt(dia, wx_ref[...])
        dxl = wc_ref[CONV_WIDTH - 1:CONV_WIDTH, :] * dxc
        st_ref[4 + CONV_WIDTH - 1:4 + CONV_WIDTH, :] += _colsum(dxc * xl)
        for k in range(CONV_WIDTH - 1):
            ahead = _shift_up(dxc, CONV_WIDTH - 1 - k)
            dxl = dxl + wc_ref[k:k + 1, :] * ahead
            st_ref[4 + k:5 + k, :] += _colsum(ahead * xl)
        dxl_ref[...] = dxl.astype(BF16)
        st_ref[0:1, :] += _colsum(dra)
        st_ref[1:2, :] += _colsum(dia)
        st_ref[2:3, :] += d_big_l * (LRU_C * _sigmoid(-lam_ref[...]))
        st_ref[3:4, :] += _colsum(dxc)

    col = lambda hd, b: (0, hd)
    head = lambda hd, b: (hd, 0, 0)
    tok = lambda hd, b: (b, hd)
    seq_blk = pl.BlockSpec((seq, HEAD), tok)
    return pl.pallas_call(
        _ordered(body, 12, (dproj,) + tuple(after)), name="lru_bwd", grid=(LRU_HEADS, nb),
        in_specs=[seq_blk] * 8 + [pl.BlockSpec((CONV_WIDTH, HEAD), col), pl.BlockSpec((None, HEAD, HEAD), head),
                                  pl.BlockSpec((None, HEAD, HEAD), head), pl.BlockSpec((1, HEAD), col)]
        + [_ANY] * (1 + len(after)),
        out_specs=[seq_blk, pl.BlockSpec((None, HEAD, HEAD), head), pl.BlockSpec((None, HEAD, HEAD), head),
                   pl.BlockSpec((8, HEAD), col)],
        out_shape=[jax.ShapeDtypeStruct(dproj.shape, BF16), jax.ShapeDtypeStruct((LRU_HEADS, HEAD, HEAD), F32),
                   jax.ShapeDtypeStruct((LRU_HEADS, HEAD, HEAD), F32), jax.ShapeDtypeStruct((8, w), F32)],
        input_output_aliases={12: 0},
        compiler_params=_cp(("arbitrary", "arbitrary")),
    )(proj, hs, e, dyl, *saved, w_conv, w_a, w_x, lam, dproj, *after)


def _weight_grad(a, g, col_shards, name, after=()):
    t, k = a.shape
    n = g.shape[1]
    tk = k if k <= 1536 else 1024
    ns = n // N_CHIPS if col_shards else n
    narrow = col_shards and ns < 512
    tn = n if narrow else min(ns, 768 if ns % 768 == 0 else 1024)
    while ns % tn and not narrow:
        tn //= 2
    per = max(ns // tn, 1)
    tt = min(TT_DW if (k // tk) * (n // tn) > 1 else TT_DW // 4, t)

    def body(a_ref, g_ref, o_ref):
        @pl.when(pl.program_id(2) == 0)
        def _():
            o_ref[...] = jnp.zeros_like(o_ref)

        res = _mm_tn(a_ref[...], g_ref[...])
        if narrow:
            for q in range(N_CHIPS):
                o_ref[q] += res[:, q * ns:(q + 1) * ns]
        else:
            o_ref[...] += res

    if narrow:
        out_spec = pl.BlockSpec((N_CHIPS, tk, ns), lambda i, j, s: (0, i, 0))
        out_shape = jax.ShapeDtypeStruct((N_CHIPS, k, ns), F32)
    elif col_shards:
        out_spec = pl.BlockSpec((None, tk, tn), lambda i, j, s: (j // per, i, j % per))
        out_shape = jax.ShapeDtypeStruct((N_CHIPS, k, ns), F32)
    else:
        out_spec = pl.BlockSpec((tk, tn), lambda i, j, s: (i, j))
        out_shape = jax.ShapeDtypeStruct((k, n), F32)
    return pl.pallas_call(
        _ordered(body, 2, after), name=name, grid=(k // tk, n // tn, t // tt),
        in_specs=[pl.BlockSpec((tt, tk), lambda i, j, s: (s, i)), pl.BlockSpec((tt, tn), lambda i, j, s: (s, j))]
        + [_ANY] * len(after),
        out_specs=out_spec, out_shape=out_shape,
        compiler_params=_cp(("arbitrary", "arbitrary", "arbitrary")),
    )(a, g, *after)


def _input_grad(dproj, ws, slots, dz1, x2, modv, nb, seq, after=()):
    t, d = x2.shape
    nq = len(ws)
    ns = ws[0].shape[1]
    tm = min(TM_DH, seq)
    ts = min(TS_MLP, tm)
    tpb = seq // tm

    def body(slot_ref, dp_ref, *refs):
        w_hbm = refs[:nq]
        dz1_ref, x_ref, mod_ref, gx_ref, db_ref, pb_ref, w_s, acc, sems = refs[nq:]
        i = pl.program_id(0)

        @pl.when(i == 0)
        def _():
            _load_weights(w_hbm, [w_s.at[slot_ref[k]] for k in range(nq)], sems)
            db_ref[...] = jnp.zeros_like(db_ref)

        @pl.when(i % tpb == 0)
        def _():
            pb_ref[...] = jnp.zeros_like(pb_ref)

        for sub in range(tm // ts):
            rows = slice(sub * ts, (sub + 1) * ts)
            for q in range(nq):
                dp = dp_ref[rows, q * ns:(q + 1) * ns]
                part = _mm_nt(dp, w_s[q])
                if q == 0:
                    acc[sub] = part
                else:
                    acc[sub] += part
                db_ref[q, 0:1, :] += _colsum(dp.astype(F32))
            dh = acc[sub]
            gx_ref[rows, :] = ALPHA * dz1_ref[rows, :] + dh * (1.0 + mod_ref[1:2, :])
            pb_ref[0:1, :] += _colsum(dh * x_ref[rows, :])
            pb_ref[1:2, :] += _colsum(dh)

    tok = lambda i, s: (i, 0)
    in_specs = [pl.BlockSpec((tm, nq * ns), tok)] + [_ANY] * nq
    in_specs += [pl.BlockSpec((tm, d), tok), pl.BlockSpec((tm, d), tok),
                 pl.BlockSpec((None, 8, d), lambda i, s: (i // tpb, 0, 0))] + [_ANY] * len(after)
    return pl.pallas_call(
        _ordered(body, 5 + nq, after), name="input_grad",
        grid_spec=pltpu.PrefetchScalarGridSpec(
            num_scalar_prefetch=1, grid=(t // tm,), in_specs=in_specs,
            out_specs=[pl.BlockSpec((tm, d), tok), pl.BlockSpec((nq, 8, ns), lambda i, s: (0, 0, 0)),
                       pl.BlockSpec((None, 8, d), lambda i, s: (i // tpb, 0, 0))],
            scratch_shapes=[pltpu.VMEM((nq, d, ns), BF16), pltpu.VMEM((tm // ts, ts, d), F32),
                            pltpu.SemaphoreType.DMA((nq,))]),
        out_shape=[jax.ShapeDtypeStruct((t, d), F32), jax.ShapeDtypeStruct((nq, 8, ns), F32),
                   jax.ShapeDtypeStruct((nb, 8, d), F32)],
        compiler_params=_cp(("arbitrary",)),
    )(slots, dproj, *ws, dz1, x2, modv, *after)


def _rows128(v):
    flat = v.reshape(-1, HEAD)
    pad = (-flat.shape[0]) % 8
    return jnp.pad(flat, ((0, pad), (0, 0))) if pad else flat


def kernel(x, c, w_ada, b_ada, w_in, b_in, w_conv, b_conv, w_rg_a, b_rg_a, w_rg_x, b_rg_x, lru_lambda, w_sp, b_sp, ln_v_g, ln_v_b, w_o_lru, w_o_sgu, w_out, ln1_g, ln1_b, w_up, w_down, ln2_g, ln2_b, loss_target, m_w_ada, m_b_ada, m_w_in, m_b_in, m_w_conv, m_b_conv, m_w_rg_a, m_b_rg_a, m_w_rg_x, m_b_rg_x, m_lru_lambda, m_w_sp, m_b_sp, m_ln_v_g, m_ln_v_b, m_w_o_lru, m_w_o_sgu, m_w_out, m_ln1_g, m_ln1_b, m_w_up, m_w_down, m_ln2_g, m_ln2_b, v_w_ada, v_b_ada, v_w_in, v_b_in, v_w_conv, v_b_conv, v_w_rg_a, v_b_rg_a, v_w_rg_x, v_b_rg_x, v_lru_lambda, v_w_sp, v_b_sp, v_ln_v_g, v_ln_v_b, v_w_o_lru, v_w_o_sgu, v_w_out, v_ln1_g, v_ln1_b, v_w_up, v_w_down, v_ln2_g, v_ln2_b):
    given = dict(locals())
    nb, seq, d = x.shape
    t = nb * seq
    w_lru = LRU_HEADS * HEAD
    d_sgu = SGU_GROUPS * HEAD
    xi, yi, ci = lax.axis_index("x"), lax.axis_index("y"), lax.axis_index("c")
    chip = 2 * xi + yi
    dev = 2 * chip + ci
    cidx = jnp.reshape(ci, (1,)).astype(jnp.int32)

    x2 = x.reshape(t, d)
    target = loss_target.reshape(t, d)

    big = ["w_in", "w_o_lru", "w_o_sgu", "w_out", "w_up", "w_down"]
    shards_a = [w_in[0].astype(BF16)]
    shards_b = [given[n][0].astype(BF16) for n in big[1:]]
    pidx = jnp.reshape(chip, (1,)).astype(jnp.int32)

    c_rows = _rows128(c)
    wconv_rows = _rows128(w_conv[0])
    slab0 = _all_gather_small(jnp.concatenate([c_rows, wconv_rows], axis=0), "gather_c_wconv")
    slab0 = slab0.reshape(N_DEV, -1, HEAD)
    c_all = slab0[:, :c_rows.shape[0]].reshape(N_DEV * nb, d)
    n_wc = CONV_WIDTH * (w_lru // N_CHIPS) // HEAD
    wc = slab0[0::2, c_rows.shape[0]:c_rows.shape[0] + n_wc].reshape(N_CHIPS, CONV_WIDTH, w_lru // N_CHIPS)
    w_conv_full = jnp.transpose(wc, (1, 0, 2)).reshape(CONV_WIDTH, w_lru)

    n_ada = w_ada.shape[2]
    b_ada_cols = lax.dynamic_slice(b_ada, (0, chip * n_ada), (1, n_ada))
    mod_cols = _ada_fwd(c_all, w_ada[0], b_ada_cols)
    half = (N_DEV * nb) // 2
    mod_half = lax.dynamic_slice(mod_cols, (ci * half, 0), (half, n_ada))
    mod_g = _all_gather_small(mod_half, "gather_mod").reshape(N_CHIPS, 2, half, n_ada)
    mod_all = jnp.transpose(mod_g, (1, 2, 0, 3)).reshape(N_DEV * nb, N_CHIPS * n_ada)
    mod_loc = lax.dynamic_slice(mod_all, (dev * nb, 0), (nb, N_CHIPS * n_ada)).reshape(nb, 6, d)
    modv = jnp.pad(mod_loc, ((0, 0), (0, 2), (0, 0)))

    lru_w = (w_conv_full, b_conv, w_rg_a[0], b_rg_a, w_rg_x[0], b_rg_x, lru_lambda)
    b_sp_t = jnp.transpose(b_sp[0])

    land = lambda s: jax.ShapeDtypeStruct((N_CHIPS,) + s.shape, s.dtype)
    sds = lambda s: jax.ShapeDtypeStruct(s.shape, s.dtype)
    started_a = _split_start(shards_a, [sds(shards_a[0])] * 2, _peer_gather_copies((0, 1)), 2, "gather_w_in_near_start",
                             after=(mod_g,))
    shards_b, shards_c = shards_b[:3], shards_b[3:]

    ids = lambda *v: jnp.stack(v).astype(jnp.int32)
    modv_t = modv + started_a[-1][0:1, 0:1]
    proj, h = _proj_fwd(x2, modv_t, [started_a[2]], ids(chip), b_in, seq, "proj_fwd_own")
    own_a, lands_a = _split_wait(started_a, 1, _peer_gather_copies((0, 1)), "gather_w_in_near_wait",
                                 after=(proj, *shards_b, *shards_c))
    started_f = _split_start(own_a, [sds(own_a[0])], _far_gather_copies, 1, "gather_w_in_far_start", after=(lands_a[0],))
    started_b = _split_start(shards_b, [land(s) for s in shards_b], _gather_copies, 3 * len(shards_b),
                             "gather_w_mix_start", after=(started_f[-1],))
    started_c = _split_start(shards_c, [land(s) for s in shards_c], _gather_copies, 3 * len(shards_c),
                             "gather_w_mlp_start", after=(started_b[-1],))
    modv_t = modv + started_c[-1][0:1, 0:1]
    (proj,) = _proj_fwd(x2, modv_t, lands_a, ids(chip ^ 1, chip ^ 2), b_in, seq, "proj_fwd_near", proj_in=proj)
    own_a, land_f = _split_wait(started_f, 1, _far_gather_copies, "gather_w_in_far_wait", after=(proj,))
    (proj,) = _proj_fwd(x2, modv, land_f, ids(chip ^ 3), b_in, seq, "proj_fwd_far", proj_in=proj)
    w_in_shards, w_in_chips = own_a + lands_a + land_f, ids(chip, chip ^ 1, chip ^ 2, chip ^ 3)
    a, inp, r16, gi16, xc16 = _lru_prep(proj, lru_w, nb, seq)
    a3 = a.reshape(nb, seq, w_lru)
    hs = _scan(a3, inp.reshape(nb, seq, w_lru), False, "lru_scan", BF16).reshape(t, w_lru)
    y_sgu = _sgu_fwd(proj, w_sp[0], b_sp_t, ln_v_g, ln_v_b)
    shards_b, lands_b = _split_wait(started_b, len(shards_b), _gather_copies, "gather_w_mix_wait", after=(hs, y_sgu))
    w_o_lru_g, w_o_sgu_g, w_out_g = _fill_own_slot(lands_b, shards_b, pidx, ["own_" + n for n in big[1:4]])
    w_o_lru_g = w_o_lru_g.reshape(w_lru, d)
    w_out_g = w_out_g.reshape(d, d)
    yap, y_a, y_b, merged, mix, x1 = _mix_fwd(hs, proj, y_sgu, x2, modv, w_o_lru_g, w_o_sgu_g, w_out_g, ln1_g, ln1_b, seq)
    shards_c, lands_c = _split_wait(started_c, len(shards_c), _gather_copies, "gather_w_mlp_wait", after=(x1,))
    w_up_g, w_down_g = _fill_own_slot(lands_c, shards_c, pidx, ["own_" + n for n in big[4:]])
    w_down_g = w_down_g.reshape(-1, d)
    up, act, h2, dz2, df, st2, pb2 = _mlp_fwd(x1, modv, w_up_g, w_down_g, ln2_g, ln2_b, target, nb, seq)

    part = {}

    def to_sibling_start(group, tag, after=()):
        g4 = []
        for n in group:
            shard = given[n].shape[1:]
            g4.append(part[n].reshape(N_CHIPS, 2, shard[0] // 2, shard[1]))
        shapes = [jax.ShapeDtypeStruct((N_CHIPS,) + g.shape[2:], F32) for g in g4]
        return _split_start(g4, shapes, _to_sibling_copies, len(g4), "grads_to_sibling_start_" + tag, after)

    def to_chips_start(group, started, tag, after=()):
        g4, recv = _split_wait(started, len(group), _to_sibling_copies, "grads_to_sibling_wait_" + tag, after)
        own4 = [_add_own_half(g4[k], recv[k], cidx, "grad_pair_sum_" + n) for k, n in enumerate(group)]
        shapes = [jax.ShapeDtypeStruct((3,) + o.shape[1:], BF16) for o in own4]
        return _split_start(own4, shapes, _chip_exchange_copies, 3 * len(own4), "grads_chip_exchange_start_" + tag)

    def chips_finish(group, started, tag, after=()):
        own4, slots = _split_wait(started, len(group), _chip_exchange_copies, "grads_chip_exchange_wait_" + tag, after)
        return [_sum_own_and_peers(own4[k], slots[k], pidx, "grad_chip_sum_" + n) for k, n in enumerate(group)]

    dup, dz1, dmix, st1, pb1 = _mlp_bwd(df, up, w_down_g, w_up_g, dz2, x2, mix, modv, ln1_g, ln1_b, nb, seq)
    group1 = ["w_up", "w_down"]
    part["w_up"] = _weight_grad(h2, dup, True, "grad_w_up")
    part["w_down"] = _weight_grad(act, df, False, "grad_w_down")
    sib1 = to_sibling_start(group1, "mlp")
    dy_a, dy_b, dproj, dyl, dys = _mix_bwd(dmix, proj, y_a, y_b, hs, w_out_g, w_o_lru_g, w_o_sgu_g, seq,
                                                after=(sib1[-1],))
    group2 = ["w_o_lru", "w_o_sgu", "w_out"]
    part["w_o_lru"] = _weight_grad(yap, dy_a, False, "grad_w_o_lru")
    part["w_o_sgu"] = _weight_grad(y_sgu, dy_b, True, "grad_w_o_sgu")
    part["w_out"] = _weight_grad(merged, dmix, False, "grad_w_out")
    chips1 = to_chips_start(group1, sib1, "mlp", after=(dys, part["w_o_lru"], part["w_o_sgu"], part["w_out"]))
    sib2 = to_sibling_start(group2, "mix", after=(chips1[-1],))
    du, dv, g_w_sp, st_sgu, g_b_sp_t = _sgu_bwd(proj, dys, w_sp[0], b_sp_t, ln_v_g, ln_v_b, after=(sib2[-1],))
    dyl3 = dyl.reshape(nb, seq, w_lru)
    e = _scan(a3, dyl3, True, "lru_scan_bwd", BF16).reshape(t, w_lru)
    chips2 = to_chips_start(group2, sib2, "mix", after=(e, du))
    dproj = lax.dynamic_update_slice(dproj, du, (0, 2 * w_lru))
    dproj = lax.dynamic_update_slice(dproj, dv, (0, 2 * w_lru + d_sgu))
    dproj, g_w_rg_a, g_w_rg_x, st_lru = _lru_bwd(proj, hs, e, dyl, (a, r16, gi16, xc16), lru_w, nb, seq, dproj,
                                                 after=(chips2[-1],))

    didx = jnp.reshape(dev, (1,)).astype(jnp.int32)
    early = [
        ("w_conv", st_lru[4:8]), ("b_conv", st_lru[3]), ("w_rg_a", g_w_rg_a), ("b_rg_a", st_lru[0]),
        ("w_rg_x", g_w_rg_x), ("b_rg_x", st_lru[1]), ("lru_lambda", st_lru[2]), ("w_sp", g_w_sp),
        ("b_sp", jnp.transpose(g_b_sp_t[:, :SGU_GROUPS])), ("ln_v_g", st_sgu[0]), ("ln_v_b", st_sgu[1]),
        ("ln1_g", st1[0]), ("ln1_b", st1[1]), ("ln2_g", st2[0]), ("ln2_b", st2[1]),
        ("loss", st2[2:3, 0:HEAD]),
    ]
    pieces_e = [_rows128(v) for _, v in early]
    slab_e = jnp.concatenate(pieces_e, axis=0)
    slab_e = jnp.pad(slab_e, ((0, (-slab_e.shape[0]) % TR_EW), (0, 0)))
    small_st = _split_start([slab_e], [jax.ShapeDtypeStruct((N_DEV,) + slab_e.shape, F32)], _all_devices_copies, N_DEV - 1,
                            "small_grads_start")

    group3 = ["w_in"]
    part["w_in"] = _weight_grad(h, dproj, True, "grad_w_in", after=(small_st[-1],))
    sib3 = to_sibling_start(group3, "in")
    halves12 = (chips_finish(group1, chips1, "mlp", after=(sib3[-1],))
                + chips_finish(group2, chips2, "mix", after=(sib3[-1],)))
    swap12 = _split_start(halves12, [jax.ShapeDtypeStruct(hv.shape, F32) for hv in halves12], _swap_copies, len(halves12),
                          "grads_swap_start")
    chips3 = to_chips_start(group3, sib3, "in", after=(swap12[-1],))
    grad_x2, g_b_in4, pb0 = _input_grad(dproj, w_in_shards, w_in_chips, dz1, x2, modv, nb, seq, after=(chips3[-1],))
    grads = {}
    two_d = lambda v: v.reshape(-1, v.shape[-1])
    done = {}

    def adamw_big(n, mine_n, theirs_n):
        done[n] = _adamw_halves(two_d(given[n]), mine_n, theirs_n, two_d(given["m_" + n]), two_d(given["v_" + n]), cidx,
                                "adamw_" + n)

    dmod_loc = jnp.stack([pb0[:, 1], pb0[:, 0], pb1[:, 2], pb1[:, 1], pb1[:, 0], pb2[:, 0]], axis=1)
    rows_dmod = dmod_loc.size // HEAD
    slab_l = jnp.concatenate([_rows128(dmod_loc), _rows128(g_b_in4[:, 0])], axis=0)
    late_st = _split_start([slab_l], [jax.ShapeDtypeStruct((N_DEV,) + slab_l.shape, F32)], _all_devices_copies, N_DEV - 1,
                           "late_grads_start")
    mine12, theirs12 = _split_wait(swap12, len(halves12), _swap_copies, "grads_swap_wait", after=(late_st[-1],))
    for n, mine_n, theirs_n in zip(group1 + group2, mine12, theirs12):
        adamw_big(n, mine_n, theirs_n)
    (slab_l,), (lands_l,) = _split_wait(late_st, 1, _all_devices_copies, "late_grads_wait",
                                        after=tuple(done[n][1] for n in group1 + group2))
    every = jnp.where(lax.broadcasted_iota(jnp.int32, (N_DEV, 1, 1), 0) == dev, slab_l[None], lands_l)
    dmod_all = every[:, :rows_dmod].reshape(N_DEV * nb, 6 * d)
    grads["b_in"] = _sum_slots(every[:, rows_dmod:], "grad_b_in_sum").reshape(1, -1)

    (slab_e,), (lands_e,) = _split_wait(small_st, 1, _all_devices_copies, "small_grads_wait", after=(dmod_all,))
    summed = _sum_devices(lands_e, slab_e, didx, "small_grad_sum")
    off = 0
    for (n, v), piece in zip(early, pieces_e):
        grads[n] = summed[off:off + v.size // HEAD].reshape(v.shape)
        off += piece.shape[0]
    loss = grads.pop("loss")[0, 0]

    (mine3,) = chips_finish(group3, chips3, "in", after=(summed,))
    (theirs3,) = _exchange([mine3], [jax.ShapeDtypeStruct(mine3.shape, F32)], _swap_copies, 1, "grads_swap_w_in")
    adamw_big("w_in", mine3, theirs3)

    dmod_cols = lax.dynamic_slice(dmod_all, (0, chip * n_ada), (N_DEV * nb, n_ada))
    grads["w_ada"], grads["b_ada"] = _ada_bwd(c_all, dmod_all, dmod_cols)
    n_wcs = w_lru // N_CHIPS
    grads["w_conv"] = lax.dynamic_slice(grads["w_conv"], (0, chip * n_wcs), (CONV_WIDTH, n_wcs))

    names = ['w_ada', 'b_ada', 'w_in', 'b_in', 'w_conv', 'b_conv', 'w_rg_a', 'b_rg_a', 'w_rg_x', 'b_rg_x', 'lru_lambda',
             'w_sp', 'b_sp', 'ln_v_g', 'ln_v_b', 'w_o_lru', 'w_o_sgu', 'w_out', 'ln1_g', 'ln1_b', 'w_up', 'w_down',
             'ln2_g', 'ln2_b']
    small_names = [n for n in names if n not in big and n != "w_ada"]
    small_out = _adamw_many([(two_d(given[n]), two_d(grads[n].reshape(given[n].shape)), two_d(given["m_" + n]),
                              two_d(given["v_" + n])) for n in small_names], "adamw_small")
    for n, res in zip(small_names, small_out):
        done[n] = (grads[n],) + tuple(res)
    done["w_ada"] = (grads["w_ada"],) + tuple(_adamw(two_d(given["w_ada"]), two_d(grads["w_ada"]), two_d(given["m_w_ada"]),
                                                     two_d(given["v_w_ada"]), "adamw_w_ada"))
    outs = [[done[n][k].reshape(given[n].shape) for n in names] for k in range(4)]
    return (loss, grad_x2.reshape(nb, seq, d), *outs[0], *outs[1], *outs[2], *outs[3])
```

```python
import functools
import math

import jax
import jax.numpy as jnp
from jax import lax
from jax.experimental import pallas as pl
from jax.experimental.pallas import tpu as pltpu

F32 = jnp.float32
BF16 = jnp.bfloat16
MESH = pl.DeviceIdType.MESH

N_CHIPS = 4
N_DEV = 8
LRU_HEADS = 10
HEAD = 128
SGU_GROUPS = 6
SGU_CHUNK = 64
CONV_WIDTH = 4
LRU_C = 8.0
ALPHA = 2.0 ** 0.25
LN_EPS = 1e-5
ADAM_LR, ADAM_B1, ADAM_B2, ADAM_EPS, ADAM_WD, ADAM_STEP = 0.001, 0.9, 0.999, 1e-08, 0.01, 10

VMEM_LIMIT = 56 * 1024 * 1024
VMEM_LIMIT_MAX = 62 * 1024 * 1024
TM_PROJ = 1024
TM_MIX = 512
TM_MLP = 512
TS_MLP = 256
TM_SGU = 512
TM_DH = 512
TT_DW = 4096
TC_SCAN = 256
TR_EW = 256


def _cp(sem=None, limit=None):
    return pltpu.CompilerParams(dimension_semantics=sem, vmem_limit_bytes=limit or VMEM_LIMIT)


def _mm(a, b):
    return jnp.dot(a.astype(BF16), b.astype(BF16), preferred_element_type=F32)


def _mm_nt(a, b):
    return lax.dot_general(a.astype(BF16), b.astype(BF16), (((1,), (1,)), ((), ())), preferred_element_type=F32)


def _mm_tn(a, b):
    return lax.dot_general(a.astype(BF16), b.astype(BF16), (((0,), (0,)), ((), ())), preferred_element_type=F32)


def _sigmoid(x):
    return 1.0 / (1.0 + jnp.exp(-x))


def _sigmoid_t(x):
    return 0.5 * jnp.tanh(0.5 * x) + 0.5


_GELU_K = math.sqrt(2.0 / math.pi)


def _gelu(x):
    t = jnp.tanh(_GELU_K * (x + 0.044715 * (x * x * x)))
    return 0.5 * x * (1.0 + t)


def _gelu_and_grad(x):
    x2 = x * x
    t = jnp.tanh(_GELU_K * (x + 0.044715 * (x2 * x)))
    g = 0.5 * x * (1.0 + t)
    dg = 0.5 * (1.0 + t) + 0.5 * x * (1.0 - t * t) * (_GELU_K * (1.0 + 3.0 * 0.044715 * x2))
    return g, dg


def _ln_stats(z):
    mu = jnp.mean(z, axis=-1, keepdims=True)
    zc = z - mu
    var = jnp.mean(zc * zc, axis=-1, keepdims=True)
    rstd = lax.rsqrt(var + LN_EPS)
    return zc * rstd, rstd


def _ln_bwd(dxh, xhat, rstd):
    m1 = jnp.mean(dxh, axis=-1, keepdims=True)
    m2 = jnp.mean(dxh * xhat, axis=-1, keepdims=True)
    return rstd * (dxh - m1 - xhat * m2)


def _colsum(v):
    return jnp.sum(v, axis=0, keepdims=True)


def _shift_down(v, j):
    if j == 0:
        return v
    rows = lax.broadcasted_iota(jnp.int32, v.shape, 0)
    return jnp.where(rows >= j, pltpu.roll(v, j, 0), 0.0)


def _shift_up(v, j):
    if j == 0:
        return v
    n = v.shape[0]
    rows = lax.broadcasted_iota(jnp.int32, v.shape, 0)
    return jnp.where(rows < n - j, pltpu.roll(v, n - j, 0), 0.0)


def _load_weights(srcs, dsts, sems):
    cps = [pltpu.make_async_copy(s, dd, sems.at[k]) for k, (s, dd) in enumerate(zip(srcs, dsts))]
    for cp in cps:
        cp.start()
    for cp in cps:
        cp.wait()


def _my_pos():
    return lax.axis_index("x"), lax.axis_index("y"), lax.axis_index("c")


_HBM = pl.BlockSpec(memory_space=pltpu.HBM)
_ANY = pl.BlockSpec(memory_space=pl.ANY)
_SEM = pl.BlockSpec(memory_space=pltpu.SEMAPHORE)
_EFFECT = pltpu.SideEffectType.DATAFLOW_SIDE_EFFECTING


def _ordered(body, n_in, after):
    k = len(after)
    if not k:
        return body
    return lambda *refs: body(*refs[:n_in], *refs[n_in + k:])


def _gather_copies(ins, lands, send_sems, recv_sems):
    x, y, c = _my_pos()
    p = 2 * x + y
    peers = [(x, 1 - y), (1 - x, y), (1 - x, 1 - y)]
    sends, recvs = [], []
    for k in range(len(ins)):
        for j, (qx, qy) in enumerate(peers):
            sems = dict(send_sem=send_sems.at[3 * k + j], recv_sem=recv_sems.at[3 * k + j],
                        device_id=(qx, qy, c), device_id_type=MESH)
            sends.append(pltpu.make_async_remote_copy(src_ref=ins[k], dst_ref=lands[k].at[p], **sems))
            recvs.append(pltpu.make_async_remote_copy(src_ref=ins[k], dst_ref=lands[k].at[2 * qx + qy], **sems))
    return sends, recvs


def _peer_gather_copies(peers):
    def copies(ins, lands, send_sems, recv_sems):
        x, y, c = _my_pos()
        where = [(x, 1 - y), (1 - x, y), (1 - x, 1 - y)]
        cps = [pltpu.make_async_remote_copy(
            src_ref=ins[0], dst_ref=lands[j], send_sem=send_sems.at[j], recv_sem=recv_sems.at[j],
            device_id=(*where[j], c), device_id_type=MESH) for j in peers]
        return cps, cps
    return copies


def _far_gather_copies(ins, lands, send_sems, recv_sems):
    x, y, c = _my_pos()
    cps = [pltpu.make_async_remote_copy(
        src_ref=ins[0], dst_ref=lands[0], send_sem=send_sems.at[0], recv_sem=recv_sems.at[0],
        device_id=(1 - x, 1 - y, c), device_id_type=MESH)]
    return cps, cps


def _to_sibling_copies(ins, lands, send_sems, recv_sems):
    x, y, c = _my_pos()
    cps = [pltpu.make_async_remote_copy(
        src_ref=ins[k].at[:, 1 - c], dst_ref=lands[k], send_sem=send_sems.at[k], recv_sem=recv_sems.at[k],
        device_id=(x, y, 1 - c), device_id_type=MESH) for k in range(len(ins))]
    return cps, cps


def _chip_exchange_copies(ins, lands, send_sems, recv_sems):
    x, y, c = _my_pos()
    peers = [(x, 1 - y), (1 - x, y), (1 - x, 1 - y)]
    cps = []
    for k in range(len(ins)):
        for j, (qx, qy) in enumerate(peers):
            cps.append(pltpu.make_async_remote_copy(
                src_ref=ins[k].at[2 * qx + qy], dst_ref=lands[k].at[j], send_sem=send_sems.at[3 * k + j],
                recv_sem=recv_sems.at[3 * k + j], device_id=(qx, qy, c), device_id_type=MESH))
    return cps, cps


def _all_devices_copies(ins, lands, send_sems, recv_sems):
    x, y, c = _my_pos()
    me = 4 * x + 2 * y + c
    sends, recvs = [], []
    for r in range(1, N_DEV):
        px = 1 - x if r & 4 else x
        py = 1 - y if r & 2 else y
        pc = 1 - c if r & 1 else c
        sems = dict(send_sem=send_sems.at[r - 1], recv_sem=recv_sems.at[r - 1], device_id=(px, py, pc), device_id_type=MESH)
        sends.append(pltpu.make_async_remote_copy(src_ref=ins[0], dst_ref=lands[0].at[me], **sems))
        recvs.append(pltpu.make_async_remote_copy(src_ref=ins[0], dst_ref=lands[0].at[4 * px + 2 * py + pc], **sems))
    return sends, recvs


def _swap_copies(ins, lands, send_sems, recv_sems):
    x, y, c = _my_pos()
    cps = [pltpu.make_async_remote_copy(
        src_ref=ins[k], dst_ref=lands[k], send_sem=send_sems.at[k], recv_sem=recv_sems.at[k],
        device_id=(x, y, 1 - c), device_id_type=MESH) for k in range(len(ins))]
    return cps, cps


def _split_start(ins, land_shapes, copies, n_sems, name, after=()):
    n, nl = len(ins), len(land_shapes)
    first_out = n + nl + len(after)

    def body(*refs):
        in_refs, land_refs = refs[:n], refs[n:n + nl]
        send_sems, recv_sems = refs[first_out:first_out + 2]
        token = refs[-1]
        sends, _ = copies(in_refs, land_refs, send_sems, recv_sems)
        for cp in sends:
            cp.start()
        token[...] = jnp.zeros_like(token)

    lands = [pltpu.with_memory_space_constraint(lax.empty(s.shape, s.dtype), pltpu.HBM) for s in land_shapes]
    ins = [pltpu.with_memory_space_constraint(s, pltpu.HBM) for s in ins]
    return pl.pallas_call(
        body, name=name,
        out_shape=(pltpu.SemaphoreType.DMA((n_sems,)), pltpu.SemaphoreType.DMA((n_sems,)),
                   *[pltpu.HBM(s.shape, s.dtype) for s in ins], *[pltpu.HBM(s.shape, s.dtype) for s in lands],
                   jax.ShapeDtypeStruct((8, HEAD), F32)),
        in_specs=[_HBM] * (n + nl) + [pl.BlockSpec(memory_space=pl.ANY)] * len(after),
        out_specs=(_SEM, _SEM, *([_HBM] * (n + nl)), pl.BlockSpec(memory_space=pltpu.VMEM)),
        input_output_aliases={k: 2 + k for k in range(n + nl)},
        compiler_params=pltpu.CompilerParams(has_side_effects=_EFFECT),
    )(*ins, *lands, *after)


def _split_wait(started, n, copies, name, after=()):
    send_sems, recv_sems = started[0], started[1]
    bufs = started[2:-1]
    nb = len(bufs)

    def body(*refs):
        in_refs, land_refs = refs[:n], refs[n:nb]
        sends, recvs = copies(in_refs, land_refs, refs[nb], refs[nb + 1])
        for cp in sends:
            cp.wait_send()
        for cp in recvs:
            cp.wait_recv()

    outs = pl.pallas_call(
        body, name=name,
        out_shape=tuple(pltpu.HBM(s.shape, s.dtype) for s in bufs),
        in_specs=[_HBM] * nb + [_SEM, _SEM] + [pl.BlockSpec(memory_space=pl.ANY)] * len(after),
        out_specs=tuple([_HBM] * nb),
        input_output_aliases={k: k for k in range(nb)},
        compiler_params=pltpu.CompilerParams(has_side_effects=_EFFECT),
    )(*bufs, send_sems, recv_sems, *after)
    return list(outs[:n]), list(outs[n:])


def _fill_own_slot(gathered, shards, pidx, names):
    outs = []
    for g, s, name in zip(gathered, shards, names):
        r, cdim = s.shape
        tr = _row_tile(r)

        def body(p_ref, s_ref, g_ref, o_ref):
            o_ref[...] = s_ref[...]

        outs.append(pl.pallas_call(
            body, name=name,
            grid_spec=pltpu.PrefetchScalarGridSpec(
                num_scalar_prefetch=1, grid=(r // tr,),
                in_specs=[pl.BlockSpec((tr, cdim), lambda i, p: (i, 0)), pl.BlockSpec(memory_space=pl.ANY)],
                out_specs=pl.BlockSpec((None, tr, cdim), lambda i, p: (p[0], i, 0))),
            out_shape=jax.ShapeDtypeStruct(g.shape, g.dtype),
            input_output_aliases={2: 0},
            compiler_params=_cp(("arbitrary",)),
        )(pidx, s, g))
    return outs


def _sum_own_and_peers(own4, slots, pidx, name):
    _, rh, cdim = own4.shape
    tr = _row_tile(rh)

    def body(p_ref, own_ref, s_ref, o_ref):
        acc = own_ref[...].astype(F32)
        for j in range(3):
            acc = acc + s_ref[j].astype(F32)
        o_ref[...] = acc

    return pl.pallas_call(
        body, name=name,
        grid_spec=pltpu.PrefetchScalarGridSpec(
            num_scalar_prefetch=1, grid=(rh // tr,),
            in_specs=[pl.BlockSpec((None, tr, cdim), lambda i, p: (p[0], i, 0)),
                      pl.BlockSpec((3, tr, cdim), lambda i, p: (0, i, 0))],
            out_specs=pl.BlockSpec((tr, cdim), lambda i, p: (i, 0))),
        out_shape=jax.ShapeDtypeStruct((rh, cdim), F32),
        compiler_params=_cp(("arbitrary",)),
    )(pidx, own4, slots)


def _exchange(ins, land_shapes, copies, n_sems, name):
    n, nl = len(ins), len(land_shapes)

    def body(*refs):
        sends, recvs = copies(refs[:n], refs[n:n + nl], refs[n + nl], refs[n + nl + 1])
        for cp in sends:
            cp.start()
        for cp in sends:
            cp.wait_send()
        for cp in recvs:
            cp.wait_recv()

    any_spec = pl.BlockSpec(memory_space=pl.ANY)
    return pl.pallas_call(
        body, name=name,
        out_shape=[jax.ShapeDtypeStruct(s.shape, s.dtype) for s in land_shapes],
        in_specs=[any_spec] * n, out_specs=[any_spec] * nl,
        scratch_shapes=[pltpu.SemaphoreType.DMA((n_sems,)), pltpu.SemaphoreType.DMA((n_sems,))],
    )(*ins)


def _row_tile(r):
    t = min(TR_EW, r)
    while r % t:
        t //= 2
    return t


def _add_own_half(g4, recv, cidx, name):
    _, _, rh, cdim = g4.shape
    tr = _row_tile(rh)

    def body(c_ref, a_ref, b_ref, o_ref):
        o_ref[...] = (a_ref[...] + b_ref[...]).astype(BF16)

    return pl.pallas_call(
        body, name=name,
        grid_spec=pltpu.PrefetchScalarGridSpec(
            num_scalar_prefetch=1, grid=(N_CHIPS, rh // tr),
            in_specs=[pl.BlockSpec((None, None, tr, cdim), lambda q, i, c: (q, c[0], i, 0)),
                      pl.BlockSpec((None, tr, cdim), lambda q, i, c: (q, i, 0))],
            out_specs=pl.BlockSpec((None, tr, cdim), lambda q, i, c: (q, i, 0))),
        out_shape=jax.ShapeDtypeStruct(recv.shape, BF16),
        compiler_params=_cp(("arbitrary", "arbitrary")),
    )(cidx, g4, recv)


def _sum_slots(v, name):
    n, r, cdim = v.shape
    tr = _row_tile(r)

    def body(v_ref, o_ref):
        acc = v_ref[0].astype(F32)
        for k in range(1, n):
            acc = acc + v_ref[k].astype(F32)
        o_ref[...] = acc

    return pl.pallas_call(
        body, name=name, grid=(r // tr,),
        in_specs=[pl.BlockSpec((n, tr, cdim), lambda i: (0, i, 0))],
        out_specs=pl.BlockSpec((tr, cdim), lambda i: (i, 0)),
        out_shape=jax.ShapeDtypeStruct((r, cdim), F32),
        compiler_params=_cp(("arbitrary",)),
    )(v)


def _sum_devices(lands, own, didx, name):
    _, r, cdim = lands.shape
    tr = _row_tile(r)

    def body(d_ref, l_ref, own_ref, o_ref):
        acc = jnp.where(d_ref[0] == 0, own_ref[...], l_ref[0])
        for dv in range(1, N_DEV):
            acc = acc + jnp.where(d_ref[0] == dv, own_ref[...], l_ref[dv])
        o_ref[...] = acc

    return pl.pallas_call(
        body, name=name,
        grid_spec=pltpu.PrefetchScalarGridSpec(
            num_scalar_prefetch=1, grid=(r // tr,),
            in_specs=[pl.BlockSpec((N_DEV, tr, cdim), lambda i, dd: (0, i, 0)), pl.BlockSpec((tr, cdim), lambda i, dd: (i, 0))],
            out_specs=pl.BlockSpec((tr, cdim), lambda i, dd: (i, 0))),
        out_shape=jax.ShapeDtypeStruct((r, cdim), F32),
        compiler_params=_cp(("arbitrary",)),
    )(didx, lands, own)


def _adamw_math(wv, gg, mv, vv):
    nm = ADAM_B1 * mv + (1.0 - ADAM_B1) * gg
    nv = ADAM_B2 * vv + (1.0 - ADAM_B2) * (gg * gg)
    m_hat = nm / (1.0 - ADAM_B1 ** ADAM_STEP)
    v_hat = nv / (1.0 - ADAM_B2 ** ADAM_STEP)
    return -ADAM_LR * (m_hat / (jnp.sqrt(v_hat) + ADAM_EPS) + ADAM_WD * wv), nm, nv


def _adamw_halves(w, mine, theirs, m, v, cidx, name):
    r, cdim = w.shape
    rh = r // 2
    tr = _row_tile(rh)
    nblk = rh // tr

    def body(c_ref, w_ref, a_ref, b_ref, m_ref, v_ref, g_ref, d_ref, nm_ref, nv_ref):
        gg = jnp.where(pl.program_id(0) == c_ref[0], a_ref[...], b_ref[...])
        g_ref[...] = gg
        d_ref[...], nm_ref[...], nv_ref[...] = _adamw_math(w_ref[...], gg, m_ref[...], v_ref[...])

    full = pl.BlockSpec((tr, cdim), lambda hh, i, c: (hh * nblk + i, 0))
    half = pl.BlockSpec((tr, cdim), lambda hh, i, c: (i, 0))
    return pl.pallas_call(
        body, name=name,
        grid_spec=pltpu.PrefetchScalarGridSpec(
            num_scalar_prefetch=1, grid=(2, nblk),
            in_specs=[full, half, half, full, full], out_specs=[full] * 4),
        out_shape=[jax.ShapeDtypeStruct((r, cdim), F32)] * 4,
        compiler_params=_cp(("arbitrary", "arbitrary")),
    )(cidx, w, mine, theirs, m, v)


def _adamw_many(params, name):
    n = len(params)

    def body(*refs):
        ins, outs = refs[:4 * n], refs[4 * n:]
        for k in range(n):
            w_ref, g_ref, m_ref, v_ref = ins[4 * k:4 * k + 4]
            outs[3 * k][...], outs[3 * k + 1][...], outs[3 * k + 2][...] = _adamw_math(
                w_ref[...], g_ref[...], m_ref[...], v_ref[...])

    flat = [a for p in params for a in p]
    res = pl.pallas_call(
        body, name=name,
        out_shape=[jax.ShapeDtypeStruct(p[0].shape, F32) for p in params for _ in range(3)],
        compiler_params=pltpu.CompilerParams(vmem_limit_bytes=VMEM_LIMIT),
    )(*flat)
    return [res[3 * k:3 * k + 3] for k in range(n)]


def _adamw(w, g, m, v, name):
    r, cdim = w.shape
    tr = _row_tile(r) if r % 8 == 0 else r

    def body(w_ref, g_ref, m_ref, v_ref, d_ref, nm_ref, nv_ref):
        d_ref[...], nm_ref[...], nv_ref[...] = _adamw_math(w_ref[...], g_ref[...], m_ref[...], v_ref[...])

    spec = pl.BlockSpec((tr, cdim), lambda i: (i, 0))
    return pl.pallas_call(
        body, name=name, grid=(r // tr,), in_specs=[spec] * 4, out_specs=[spec] * 3,
        out_shape=[jax.ShapeDtypeStruct((r, cdim), F32)] * 3,
        compiler_params=_cp(("arbitrary",)),
    )(w, g, m, v)


def _ada_fwd(c_all, w_ada, b_cols):
    nb, _ = c_all.shape
    n = w_ada.shape[1]

    def body(c_ref, w_ref, b_ref, o_ref):
        cv = c_ref[...]
        o_ref[...] = _mm(cv * _sigmoid(cv), w_ref[...]) + b_ref[...]

    return pl.pallas_call(
        body, name="ada_fwd", out_shape=jax.ShapeDtypeStruct((nb, n), F32),
        compiler_params=pltpu.CompilerParams(vmem_limit_bytes=VMEM_LIMIT),
    )(c_all, w_ada, b_cols)


def _ada_bwd(c_all, dmod_all, dmod_cols):
    d = c_all.shape[1]
    n = dmod_cols.shape[1]

    def body(c_ref, da_ref, dc_ref, gw_ref, gb_ref):
        cv = c_ref[...]
        gw_ref[...] = _mm_tn(cv * _sigmoid(cv), dc_ref[...])
        gb_ref[...] = _colsum(da_ref[...])

    return pl.pallas_call(
        body, name="ada_bwd",
        out_shape=[jax.ShapeDtypeStruct((d, n), F32), jax.ShapeDtypeStruct((1, dmod_all.shape[1]), F32)],
        compiler_params=pltpu.CompilerParams(vmem_limit_bytes=VMEM_LIMIT),
    )(c_all, dmod_all, dmod_cols)


def _proj_fwd(x2, modv, ws, cols, b_in, seq, name, proj_in=None):
    t, d = x2.shape
    n = len(ws)
    ns = ws[0].shape[1]
    tm = min(TM_PROJ, seq)
    tpb = seq // tm
    first = proj_in is None

    def body(c_ref, x_ref, mod_ref, *refs):
        w_refs, b_ref = refs[:n], refs[n]
        outs = refs[n + 1 if first else n + 2:]
        proj_ref, h_s = outs[0], outs[-1]
        s = pl.program_id(1)

        @pl.when(s == 0)
        def _():
            h = (x_ref[...] * (1.0 + mod_ref[1:2, :]) + mod_ref[0:1, :]).astype(BF16)
            h_s[...] = h
            if first:
                outs[1][...] = h

        for k in range(n):
            @pl.when(s == k)
            def _():
                proj_ref[...] = (jnp.dot(h_s[...], w_refs[k][...], preferred_element_type=F32) + b_ref[...]).astype(BF16)

    in_specs = [pl.BlockSpec((tm, d), lambda i, s, c: (i, 0)),
                pl.BlockSpec((None, 8, d), lambda i, s, c: (i // tpb, 0, 0))]
    in_specs += [pl.BlockSpec((d, ns), lambda i, s, c: (0, 0))] * n
    in_specs += [pl.BlockSpec((1, ns), lambda i, s, c: (0, c[s]))]
    out_specs = [pl.BlockSpec((tm, ns), lambda i, s, c: (i, c[s]))]
    out_shape = [jax.ShapeDtypeStruct((t, N_CHIPS * ns), BF16)]
    args = [cols, x2, modv, *ws, b_in]
    aliases = {}
    if first:
        out_specs.append(pl.BlockSpec((tm, d), lambda i, s, c: (i, 0)))
        out_shape.append(jax.ShapeDtypeStruct((t, d), BF16))
    else:
        in_specs.append(_ANY)
        args.append(proj_in)
        aliases = {len(args) - 1: 0}
    return pl.pallas_call(
        body, name=name,
        grid_spec=pltpu.PrefetchScalarGridSpec(
            num_scalar_prefetch=1, grid=(t // tm, n), in_specs=in_specs, out_specs=out_specs,
            scratch_shapes=[pltpu.VMEM((tm, d), BF16)]),
        out_shape=out_shape, input_output_aliases=aliases,
        compiler_params=_cp(("arbitrary", "arbitrary")),
    )(*args)


def _lru_rate(lam_ref):
    nl = -lam_ref[...]
    e = jnp.exp(-jnp.abs(nl))
    u = 1.0 + e
    dlt = u - 1.0
    log1p_e = jnp.where(dlt == 0.0, e, jnp.log(u) * (e / jnp.where(dlt == 0.0, 1.0, dlt)))
    return -LRU_C * (jnp.maximum(nl, 0.0) + log1p_e)


def _lru_gates(xl, wc_ref, bc_ref, wa_ref, ba_ref, wx_ref, bx_ref, lam_ref):
    xc = bc_ref[...] + wc_ref[CONV_WIDTH - 1:CONV_WIDTH, :] * xl
    for k in range(CONV_WIDTH - 1):
        xc = xc + wc_ref[k:k + 1, :] * _shift_down(xl, CONV_WIDTH - 1 - k)
    r = _sigmoid(_mm(xc, wa_ref[...]) + ba_ref[...])
    gi = _sigmoid_t(_mm(xc, wx_ref[...]) + bx_ref[...])
    big_l = _lru_rate(lam_ref)
    la = big_l * r
    a = jnp.exp(la)
    m2 = jnp.tanh(-la) * (a * a + 1.0)
    return xc, r, gi, big_l, a, m2


def _lru_prep(proj, lru_w, nb, seq):
    t = proj.shape[0]
    w = LRU_HEADS * HEAD
    w_conv, b_conv, w_a, b_a, w_x, b_x, lam = lru_w

    def body(x_ref, wc_ref, bc_ref, wa_ref, ba_ref, wx_ref, bx_ref, lam_ref, a_ref, inp_ref, r_ref, gi_ref, xc_ref):
        xc, r, gi, big_l, a, m2 = _lru_gates(x_ref[...].astype(F32), wc_ref, bc_ref, wa_ref, ba_ref, wx_ref, bx_ref, lam_ref)
        a_ref[...] = a
        inp_ref[...] = (jnp.sqrt(m2) * (gi * xc)).astype(BF16)
        r_ref[...] = r.astype(BF16)
        gi_ref[...] = gi.astype(BF16)
        xc_ref[...] = xc.astype(BF16)

    col = lambda b, hd: (0, hd)
    head = lambda b, hd: (hd, 0, 0)
    tok = lambda b, hd: (b, hd)
    return pl.pallas_call(
        body, name="lru_prep", grid=(nb, LRU_HEADS),
        in_specs=[pl.BlockSpec((seq, HEAD), tok),
                  pl.BlockSpec((CONV_WIDTH, HEAD), col), pl.BlockSpec((1, HEAD), col),
                  pl.BlockSpec((None, HEAD, HEAD), head), pl.BlockSpec((1, HEAD), col),
                  pl.BlockSpec((None, HEAD, HEAD), head), pl.BlockSpec((1, HEAD), col),
                  pl.BlockSpec((1, HEAD), col)],
        out_specs=[pl.BlockSpec((seq, HEAD), tok)] * 5,
        out_shape=[jax.ShapeDtypeStruct((t, w), F32)] + [jax.ShapeDtypeStruct((t, w), BF16)] * 4,
        compiler_params=_cp(("arbitrary", "arbitrary")),
    )(proj, w_conv, b_conv, w_a, b_a, w_x, b_x, lam)


def _scan(a3, b3, reverse, name, out_dtype):
    nb, seq, w = a3.shape
    tc = min(TC_SCAN, seq)
    nchunk = seq // tc
    npair = tc // 16

    def combine(av, bv):
        rows = lax.broadcasted_iota(jnp.int32, av.shape, 0)
        for s in (1, 2, 4):
            if reverse:
                keep = rows < 8 - s
                a_sh, b_sh = pltpu.roll(av, 8 - s, 0), pltpu.roll(bv, 8 - s, 0)
            else:
                keep = rows >= s
                a_sh, b_sh = pltpu.roll(av, s, 0), pltpu.roll(bv, s, 0)
            bv = jnp.where(keep, bv + av * b_sh, bv)
            av = jnp.where(keep, av * a_sh, av)
        return av, bv

    def body(a_ref, b_ref, h_ref, carry):
        @pl.when(pl.program_id(0) == 0)
        def _():
            carry[...] = jnp.zeros_like(carry)

        for b in range(nb):
            def pair(j, hprev):
                jj = npair - 1 - j if reverse else j
                base = pl.multiple_of(jj * 16, 16)
                a16 = a_ref[b, pl.ds(base, 16), :]
                b16 = b_ref[b, pl.ds(base, 16), :].astype(F32)
                outs = [None, None]
                for k in ((1, 0) if reverse else (0, 1)):
                    av, bv = a16[8 * k:8 * k + 8, :], b16[8 * k:8 * k + 8, :]
                    av, bv = combine(av, av * bv if reverse else bv)
                    h = bv + av * hprev
                    outs[k] = h
                    hprev = jnp.broadcast_to(h[0:1, :] if reverse else h[7:8, :], (8, w))
                h_ref[b, pl.ds(base, 16), :] = jnp.concatenate(outs, axis=0).astype(out_dtype)
                return hprev

            carry[b] = lax.fori_loop(0, npair, pair, carry[b])

    imap = (lambda i: (0, nchunk - 1 - i, 0)) if reverse else (lambda i: (0, i, 0))
    spec = pl.BlockSpec((nb, tc, w), imap)
    return pl.pallas_call(
        body, name=name, grid=(nchunk,), in_specs=[spec, spec], out_specs=spec,
        out_shape=jax.ShapeDtypeStruct((nb, seq, w), out_dtype),
        scratch_shapes=[pltpu.VMEM((nb, 8, w), F32)],
        compiler_params=_cp(("arbitrary",)),
    )(a3, b3)


def _sgu_mask():
    ti = lax.broadcasted_iota(jnp.int32, (HEAD, HEAD), 0) // SGU_CHUNK
    si = lax.broadcasted_iota(jnp.int32, (HEAD, HEAD), 1) // SGU_CHUNK
    return si <= ti


def _sgu_specs(tm, d_sgu):
    pw = 256
    first_u = (2 * LRU_HEADS * HEAD) // pw
    n_piece = d_sgu // pw
    specs = [pl.BlockSpec((tm, pw), functools.partial(lambda i, k: (i, k), k=first_u + j)) for j in range(2 * n_piece)]
    return specs, n_piece


def _sgu_fwd(proj, w_sp, b_sp_t, ln_g, ln_b):
    t = proj.shape[0]
    d_sgu = SGU_GROUPS * HEAD
    tm = min(TM_SGU, t)
    nblk = tm // HEAD
    specs, n_piece = _sgu_specs(tm, d_sgu)

    def body(*refs):
        u = jnp.concatenate([r[...] for r in refs[:n_piece]], axis=1).astype(F32)
        v = jnp.concatenate([r[...] for r in refs[n_piece:2 * n_piece]], axis=1).astype(F32)
        w_ref, bt_ref, g_ref, b_ref, y_ref = refs[2 * n_piece:]
        ug = _gelu(u)
        xhat, _ = _ln_stats(_gelu(v))
        vn = (xhat * g_ref[...] + b_ref[...]).astype(BF16)
        mask = _sgu_mask()
        for g in range(SGU_GROUPS):
            wm = jnp.where(mask, w_ref[g], 0.0).astype(BF16)
            cols = slice(g * HEAD, (g + 1) * HEAD)
            for n in range(nblk):
                rows = slice(n * HEAD, (n + 1) * HEAD)
                mixed = jnp.dot(wm, vn[rows, cols], preferred_element_type=F32) + bt_ref[:, g:g + 1]
                y_ref[rows, cols] = (ug[rows, cols] * mixed).astype(BF16)

    full = lambda shape: pl.BlockSpec(shape, lambda i: (0,) * len(shape))
    return pl.pallas_call(
        body, name="sgu_fwd", grid=(t // tm,),
        in_specs=specs + [full(w_sp.shape), full(b_sp_t.shape), full(ln_g.shape), full(ln_b.shape)],
        out_specs=pl.BlockSpec((tm, d_sgu), lambda i: (i, 0)),
        out_shape=jax.ShapeDtypeStruct((t, d_sgu), BF16),
        compiler_params=_cp(("arbitrary",)),
    )(*([proj] * (2 * n_piece)), w_sp, b_sp_t, ln_g, ln_b)


def _mix_fwd(hs, proj, y_sgu, x2, modv, w_o_lru_g, w_o_sgu_g, w_out_g, ln1_g, ln1_b, seq):
    t, d = x2.shape
    w = hs.shape[1]
    d_sgu = y_sgu.shape[1]
    nq, _, ns = w_o_sgu_g.shape
    tm = min(TM_MIX, seq)
    ts = min(TS_MLP, tm)
    tpb = seq // tm

    def body(hs_ref, gl_ref, ys_ref, ga_ref, gb_ref, x_ref, mod_ref, wl_hbm, ws_hbm, wo_hbm, g1_ref, b1_ref,
             yap_ref, ya_ref, yb_ref, mg_ref, mix_ref, x1_ref, wl_ref, ws_ref, wo_ref, sems):
        @pl.when(pl.program_id(0) == 0)
        def _():
            _load_weights((wl_hbm, ws_hbm, wo_hbm), (wl_ref, ws_ref, wo_ref), sems)

        for sub in range(tm // ts):
            rows = slice(sub * ts, (sub + 1) * ts)
            yap = (hs_ref[rows, :].astype(F32) * _gelu(gl_ref[rows, :].astype(F32))).astype(BF16)
            yap_ref[rows, :] = yap
            y_a = jnp.dot(yap, wl_ref[...], preferred_element_type=F32)
            ys = ys_ref[rows, :]
            y_b = jnp.concatenate([jnp.dot(ys, ws_ref[q], preferred_element_type=F32) for q in range(nq)], axis=1)
            ya_ref[rows, :] = y_a.astype(BF16)
            yb_ref[rows, :] = y_b.astype(BF16)
            merged = (_sigmoid_t(ga_ref[rows, :].astype(F32)) * y_a
                      + _sigmoid_t(gb_ref[rows, :].astype(F32)) * y_b).astype(BF16)
            mg_ref[rows, :] = merged
            mix = jnp.dot(merged, wo_ref[...], preferred_element_type=F32)
            mix_ref[rows, :] = mix
            xhat, _ = _ln_stats(ALPHA * x_ref[rows, :] + (1.0 + mod_ref[2:3, :]) * mix)
            x1_ref[rows, :] = xhat * g1_ref[...] + b1_ref[...]

    row = lambda width, col: pl.BlockSpec((tm, width), functools.partial(lambda i, k: (i, k), k=col))
    full = lambda shape: pl.BlockSpec(shape, lambda i: (0,) * len(shape))
    return pl.pallas_call(
        body, name="mix_fwd", grid=(t // tm,),
        in_specs=[row(w, 0), row(w, 1), row(d_sgu, 0), row(d, 4), row(d, 5), row(d, 0),
                  pl.BlockSpec((None, 8, d), lambda i: (i // tpb, 0, 0)),
                  _ANY, _ANY, _ANY, full(ln1_g.shape), full(ln1_b.shape)],
        out_specs=[row(w, 0), row(d, 0), row(d, 0), row(d, 0), row(d, 0), row(d, 0)],
        out_shape=[jax.ShapeDtypeStruct((t, w), BF16), jax.ShapeDtypeStruct((t, d), BF16),
                   jax.ShapeDtypeStruct((t, d), BF16), jax.ShapeDtypeStruct((t, d), BF16),
                   jax.ShapeDtypeStruct((t, d), F32), jax.ShapeDtypeStruct((t, d), F32)],
        scratch_shapes=[pltpu.VMEM(w_o_lru_g.shape, BF16), pltpu.VMEM(w_o_sgu_g.shape, BF16),
                        pltpu.VMEM(w_out_g.shape, BF16), pltpu.SemaphoreType.DMA((3,))],
        compiler_params=_cp(("arbitrary",)),
    )(hs, proj, y_sgu, proj, proj, x2, modv, w_o_lru_g, w_o_sgu_g, w_out_g, ln1_g, ln1_b)


def _mlp_fwd(x1, modv, w_up_g, w_down_g, ln2_g, ln2_b, target, nb, seq):
    t, d = x1.shape
    nq, _, ns = w_up_g.shape
    tm = min(TM_MLP, seq)
    ts = min(TS_MLP, tm)
    tpb = seq // tm

    def body(x1_ref, mod_ref, wu_hbm, wd_hbm, g2_ref, b2_ref, tg_ref,
             rl_ref, act_ref, h2_ref, dz2_ref, df_ref, st_ref, pb_ref, wu_s, wd_s, acc, sems):
        i = pl.program_id(0)

        @pl.when(i == 0)
        def _():
            _load_weights((wu_hbm, wd_hbm), (wu_s, wd_s), sems)
            st_ref[...] = jnp.zeros_like(st_ref)

        @pl.when(i % tpb == 0)
        def _():
            pb_ref[...] = jnp.zeros_like(pb_ref)

        for sub in range(tm // ts):
            rows = slice(sub * ts, (sub + 1) * ts)
            x1v = x1_ref[rows, :]
            h2 = (x1v * (1.0 + mod_ref[4:5, :]) + mod_ref[3:4, :]).astype(BF16)
            h2_ref[rows, :] = h2
            for k in range(nq):
                cols = slice(k * ns, (k + 1) * ns)
                r = jnp.maximum(jnp.dot(h2, wu_s[k], preferred_element_type=F32), 0.0)
                act = (r * r).astype(BF16)
                rl_ref[rows, cols] = r.astype(BF16)
                act_ref[rows, cols] = act
                part = jnp.dot(act, wd_s[cols, :], preferred_element_type=F32)
                if k == 0:
                    acc[sub] = part
                else:
                    acc[sub] += part
            f = acc[sub]
            xhat, rstd = _ln_stats(ALPHA * x1v + (1.0 + mod_ref[5:6, :]) * f)
            y = xhat * g2_ref[...] + b2_ref[...]
            err = y - tg_ref[rows, :]
            dy = err * (1.0 / d)
            dz2 = _ln_bwd(dy * g2_ref[...], xhat, rstd)
            dz2_ref[rows, :] = dz2
            df_ref[rows, :] = ((1.0 + mod_ref[5:6, :]) * dz2).astype(BF16)
            st_ref[0:1, :] += _colsum(dy * xhat)
            st_ref[1:2, :] += _colsum(dy)
            st_ref[2:3, :] += (0.5 / d) * jnp.sum(_colsum(err * err), axis=1, keepdims=True)
            pb_ref[0:1, :] += _colsum(dz2 * f)

    tok = lambda i: (i, 0)
    return pl.pallas_call(
        body, name="mlp_fwd", grid=(t // tm,),
        in_specs=[pl.BlockSpec((tm, d), tok), pl.BlockSpec((None, 8, d), lambda i: (i // tpb, 0, 0)), _ANY, _ANY,
                  pl.BlockSpec((1, d), lambda i: (0, 0)), pl.BlockSpec((1, d), lambda i: (0, 0)),
                  pl.BlockSpec((tm, d), tok)],
        out_specs=[pl.BlockSpec((tm, nq * ns), tok), pl.BlockSpec((tm, nq * ns), tok),
                   pl.BlockSpec((tm, d), tok), pl.BlockSpec((tm, d), tok), pl.BlockSpec((tm, d), tok),
                   pl.BlockSpec((8, d), lambda i: (0, 0)), pl.BlockSpec((None, 8, d), lambda i: (i // tpb, 0, 0))],
        out_shape=[jax.ShapeDtypeStruct((t, nq * ns), BF16), jax.ShapeDtypeStruct((t, nq * ns), BF16),
                   jax.ShapeDtypeStruct((t, d), BF16),
                   jax.ShapeDtypeStruct((t, d), F32), jax.ShapeDtypeStruct((t, d), BF16),
                   jax.ShapeDtypeStruct((8, d), F32), jax.ShapeDtypeStruct((nb, 8, d), F32)],
        scratch_shapes=[pltpu.VMEM(w_up_g.shape, BF16), pltpu.VMEM(w_down_g.shape, BF16),
                        pltpu.VMEM((tm // ts, ts, d), F32), pltpu.SemaphoreType.DMA((2,))],
        compiler_params=_cp(("arbitrary",)),
    )(x1, modv, w_up_g, w_down_g, ln2_g, ln2_b, target)


def _mlp_bwd(df, up, w_down_g, w_up_g, dz2, x2, mix, modv, ln1_g, ln1_b, nb, seq):
    t, d = x2.shape
    nq, _, ns = w_up_g.shape
    tm = min(TM_MLP, seq)
    ts = min(TS_MLP, tm)
    tpb = seq // tm

    def body(df_ref, rl_ref, wd_hbm, wu_hbm, dz2_ref, x_ref, mix_ref, mod_ref, g1_ref, b1_ref,
             dup_ref, dz1_ref, dmix_ref, st_ref, pb_ref, wd_s, wu_s, acc, sems):
        i = pl.program_id(0)

        @pl.when(i == 0)
        def _():
            _load_weights((wd_hbm, wu_hbm), (wd_s, wu_s), sems)
            st_ref[...] = jnp.zeros_like(st_ref)

        @pl.when(i % tpb == 0)
        def _():
            pb_ref[...] = jnp.zeros_like(pb_ref)

        for sub in range(tm // ts):
            rows = slice(sub * ts, (sub + 1) * ts)
            dfv = df_ref[rows, :]
            for k in range(nq):
                cols = slice(k * ns, (k + 1) * ns)
                dup = (_mm_nt(dfv, wd_s[cols, :]) * (2.0 * rl_ref[rows, cols].astype(F32))).astype(BF16)
                dup_ref[rows, cols] = dup
                part = _mm_nt(dup, wu_s[k])
                if k == 0:
                    acc[sub] = part
                else:
                    acc[sub] += part
            dh2 = acc[sub]
            mix = mix_ref[rows, :]
            xhat, rstd = _ln_stats(ALPHA * x_ref[rows, :] + (1.0 + mod_ref[2:3, :]) * mix)
            x1 = xhat * g1_ref[...] + b1_ref[...]
            dx1 = ALPHA * dz2_ref[rows, :] + dh2 * (1.0 + mod_ref[4:5, :])
            dz1 = _ln_bwd(dx1 * g1_ref[...], xhat, rstd)
            dz1_ref[rows, :] = dz1
            dmix_ref[rows, :] = ((1.0 + mod_ref[2:3, :]) * dz1).astype(BF16)
            st_ref[0:1, :] += _colsum(dx1 * xhat)
            st_ref[1:2, :] += _colsum(dx1)
            pb_ref[0:1, :] += _colsum(dh2 * x1)
            pb_ref[1:2, :] += _colsum(dh2)
            pb_ref[2:3, :] += _colsum(dz1 * mix)

    tok = lambda i: (i, 0)
    return pl.pallas_call(
        body, name="mlp_bwd", grid=(t // tm,),
        in_specs=[pl.BlockSpec((tm, d), tok), pl.BlockSpec((tm, nq * ns), tok), _ANY, _ANY,
                  pl.BlockSpec((tm, d), tok), pl.BlockSpec((tm, d), tok), pl.BlockSpec((tm, d), tok),
                  pl.BlockSpec((None, 8, d), lambda i: (i // tpb, 0, 0)),
                  pl.BlockSpec((1, d), lambda i: (0, 0)), pl.BlockSpec((1, d), lambda i: (0, 0))],
        out_specs=[pl.BlockSpec((tm, nq * ns), tok),
                   pl.BlockSpec((tm, d), tok), pl.BlockSpec((tm, d), tok),
                   pl.BlockSpec((8, d), lambda i: (0, 0)), pl.BlockSpec((None, 8, d), lambda i: (i // tpb, 0, 0))],
        out_shape=[jax.ShapeDtypeStruct((t, nq * ns), BF16),
                   jax.ShapeDtypeStruct((t, d), F32), jax.ShapeDtypeStruct((t, d), BF16),
                   jax.ShapeDtypeStruct((8, d), F32), jax.ShapeDtypeStruct((nb, 8, d), F32)],
        scratch_shapes=[pltpu.VMEM(w_down_g.shape, BF16), pltpu.VMEM(w_up_g.shape, BF16),
                        pltpu.VMEM((tm // ts, ts, d), F32), pltpu.SemaphoreType.DMA((2,))],
        compiler_params=_cp(("arbitrary",), VMEM_LIMIT_MAX),
    )(df, up, w_down_g, w_up_g, dz2, x2, mix, modv, ln1_g, ln1_b)


def _mix_bwd(dmix, proj, y_a, y_b, hs, w_out_g, w_o_lru_g, w_o_sgu_g, seq, after=()):
    t, d = dmix.shape
    w = hs.shape[1]
    nq, d_sgu, ns = w_o_sgu_g.shape
    tm = min(TM_MIX, seq)
    ts = min(TS_MLP, tm)

    def body(dmix_ref, ga_ref, gb_ref, ya_ref, yb_ref, gl_ref, hs_ref, wo_hbm, wl_hbm, ws_hbm,
             dya_ref, dyb_ref, dg_ref, dyl_ref, dys_ref, wo_ref, wl_ref, ws_ref, sems):
        @pl.when(pl.program_id(0) == 0)
        def _():
            _load_weights((wo_hbm, wl_hbm, ws_hbm), (wo_ref, wl_ref, ws_ref), sems)

        for sub in range(tm // ts):
            rows = slice(sub * ts, (sub + 1) * ts)
            dmerged = _mm_nt(dmix_ref[rows, :], wo_ref[...])
            sa, sb = _sigmoid_t(ga_ref[rows, :].astype(F32)), _sigmoid_t(gb_ref[rows, :].astype(F32))
            dy_a = (dmerged * sa).astype(BF16)
            dy_b = (dmerged * sb).astype(BF16)
            dya_ref[rows, :] = dy_a
            dyb_ref[rows, :] = dy_b
            dg_ref[rows, 4 * d:5 * d] = (dmerged * ya_ref[rows, :].astype(F32) * (sa * (1.0 - sa))).astype(BF16)
            dg_ref[rows, 5 * d:6 * d] = (dmerged * yb_ref[rows, :].astype(F32) * (sb * (1.0 - sb))).astype(BF16)
            dyap = _mm_nt(dy_a, wl_ref[...])
            gel, dgel = _gelu_and_grad(gl_ref[rows, :].astype(F32))
            dyl_ref[rows, :] = (dyap * gel).astype(BF16)
            dg_ref[rows, w:2 * w] = (dyap * hs_ref[rows, :].astype(F32) * dgel).astype(BF16)
            dys = _mm_nt(dy_b[:, 0:ns], ws_ref[0])
            for q in range(1, nq):
                dys = dys + _mm_nt(dy_b[:, q * ns:(q + 1) * ns], ws_ref[q])
            dys_ref[rows, :] = dys

    row = lambda width, col: pl.BlockSpec((tm, width), functools.partial(lambda i, k: (i, k), k=col))
    return pl.pallas_call(
        _ordered(body, 10, after), name="mix_bwd", grid=(t // tm,),
        in_specs=[row(d, 0), row(d, 4), row(d, 5), row(d, 0), row(d, 0), row(w, 1), row(w, 0),
                  _ANY, _ANY, _ANY] + [_ANY] * len(after),
        scratch_shapes=[pltpu.VMEM(w_out_g.shape, BF16), pltpu.VMEM(w_o_lru_g.shape, BF16),
                        pltpu.VMEM(w_o_sgu_g.shape, BF16), pltpu.SemaphoreType.DMA((3,))],
        out_specs=[row(d, 0), row(d, 0), row(6 * d, 0), row(w, 0), row(d_sgu, 0)],
        out_shape=[jax.ShapeDtypeStruct((t, d), BF16), jax.ShapeDtypeStruct((t, d), BF16),
                   jax.ShapeDtypeStruct((t, 6 * d), BF16), jax.ShapeDtypeStruct((t, w), BF16),
                   jax.ShapeDtypeStruct((t, d_sgu), F32)],
        compiler_params=_cp(("arbitrary",)),
    )(dmix, proj, proj, y_a, y_b, proj, hs, w_out_g, w_o_lru_g, w_o_sgu_g, *after)


def _sgu_bwd(proj, dys, w_sp, b_sp_t, ln_g, ln_b, after=()):
    t = proj.shape[0]
    d_sgu = SGU_GROUPS * HEAD
    tm = min(TM_SGU, t)
    nblk = tm // HEAD
    specs, n_piece = _sgu_specs(tm, d_sgu)

    def body(*refs):
        u = jnp.concatenate([r[...] for r in refs[:n_piece]], axis=1).astype(F32)
        v = jnp.concatenate([r[...] for r in refs[n_piece:2 * n_piece]], axis=1).astype(F32)
        dys_ref, w_ref, bt_ref, g_ref, b_ref, du_ref, dv_ref, dw_ref, st_ref, dbt_ref, dvn_s = refs[2 * n_piece:]

        @pl.when(pl.program_id(0) == 0)
        def _():
            dw_ref[...] = jnp.zeros_like(dw_ref)
            st_ref[...] = jnp.zeros_like(st_ref)
            dbt_ref[...] = jnp.zeros_like(dbt_ref)

        ug, dug_du = _gelu_and_grad(u)
        vg, dvg_dv = _gelu_and_grad(v)
        xhat, rstd = _ln_stats(vg)
        vn = (xhat * g_ref[...] + b_ref[...]).astype(BF16)
        dys_v = dys_ref[...]
        mask = _sgu_mask()
        for g in range(SGU_GROUPS):
            wm = jnp.where(mask, w_ref[g], 0.0).astype(BF16)
            cols = slice(g * HEAD, (g + 1) * HEAD)
            dw_g = jnp.zeros((HEAD, HEAD), F32)
            db_g = jnp.zeros((HEAD, 1), F32)
            for n in range(nblk):
                rows = slice(n * HEAD, (n + 1) * HEAD)
                vn_blk = vn[rows, cols]
                mixed = jnp.dot(wm, vn_blk, preferred_element_type=F32) + bt_ref[:, g:g + 1]
                dy_blk = dys_v[rows, cols]
                du_ref[rows, cols] = (dy_blk * mixed * dug_du[rows, cols]).astype(BF16)
                dmx = dy_blk * ug[rows, cols]
                dvn_s[rows, cols] = _mm_tn(wm, dmx)
                dw_g = dw_g + _mm_nt(dmx, vn_blk)
                db_g = db_g + jnp.sum(dmx, axis=1, keepdims=True)
            dw_ref[g] += jnp.where(mask, dw_g, 0.0)
            dbt_ref[:, g:g + 1] += db_g
        dvn = dvn_s[...]
        st_ref[0:1, :] += _colsum(dvn * xhat)
        st_ref[1:2, :] += _colsum(dvn)
        dv_ref[...] = (_ln_bwd(dvn * g_ref[...], xhat, rstd) * dvg_dv).astype(BF16)

    full = lambda shape: pl.BlockSpec(shape, lambda i: (0,) * len(shape))
    tok = pl.BlockSpec((tm, d_sgu), lambda i: (i, 0))
    return pl.pallas_call(
        _ordered(body, 2 * n_piece + 5, after), name="sgu_bwd", grid=(t // tm,),
        in_specs=specs + [tok, full(w_sp.shape), full(b_sp_t.shape), full(ln_g.shape), full(ln_b.shape)]
        + [_ANY] * len(after),
        out_specs=[tok, tok, full(w_sp.shape), full((8, d_sgu)), full((HEAD, HEAD))],
        out_shape=[jax.ShapeDtypeStruct((t, d_sgu), BF16), jax.ShapeDtypeStruct((t, d_sgu), BF16),
                   jax.ShapeDtypeStruct(w_sp.shape, F32), jax.ShapeDtypeStruct((8, d_sgu), F32),
                   jax.ShapeDtypeStruct((HEAD, HEAD), F32)],
        scratch_shapes=[pltpu.VMEM((tm, d_sgu), F32)],
        compiler_params=_cp(("arbitrary",)),
    )(*([proj] * (2 * n_piece)), dys, w_sp, b_sp_t, ln_g, ln_b, *after)


def _lru_bwd(proj, hs, e, dyl, saved, lru_w, nb, seq, dproj, after=()):
    t = proj.shape[0]
    w = LRU_HEADS * HEAD
    w_conv, b_conv, w_a, b_a, w_x, b_x, lam = lru_w

    def body(x_ref, hs_ref, e_ref, dy_ref, a_ref, r_ref, gi_ref, xc_ref, wc_ref, wa_ref, wx_ref, lam_ref,
             dxl_ref, dwa_ref, dwx_ref, st_ref):
        @pl.when(pl.program_id(1) == 0)
        def _():
            dwa_ref[...] = jnp.zeros_like(dwa_ref)
            dwx_ref[...] = jnp.zeros_like(dwx_ref)
            st_ref[...] = jnp.zeros_like(st_ref)

        xl = x_ref[...].astype(F32)
        a, r, gi, xc = a_ref[...], r_ref[...].astype(F32), gi_ref[...].astype(F32), xc_ref[...].astype(F32)
        big_l = _lru_rate(lam_ref)
        m2 = (1.0 - a) * (1.0 + a)
        inv_mult = lax.rsqrt(m2)
        mult = m2 * inv_mult
        dh = dy_ref[...].astype(F32) + _shift_up(e_ref[...].astype(F32), 1)
        da = dh * _shift_down(hs_ref[...].astype(F32), 1)
        dmult = dh * (gi * xc)
        d_i = dh * (mult * xc)
        dxc = dh * (mult * gi)
        dla = a * (da - dmult * (a * inv_mult))
        dr = dla * big_l
        d_big_l = _colsum(dla * r)
        dra = dr * (r * (1.0 - r))
        dia = d_i * (gi * (1.0 - gi))
        dwa_ref[...] += _mm_tn(xc, dra)
        dwx_ref[...] += _mm_tn(xc, dia)
        dxc = dxc + _mm_nt(dra, wa_ref[...]) + _mm_nt(dia, wx_ref[...])
        dxl = wc_ref[CONV_WIDTH - 1:CONV_WIDTH, :] * dxc
        st_ref[4 + CONV_WIDTH - 1:4 + CONV_WIDTH, :] += _colsum(dxc * xl)
        for k in range(CONV_WIDTH - 1):
            ahead = _shift_up(dxc, CONV_WIDTH - 1 - k)
            dxl = dxl + wc_ref[k:k + 1, :] * ahead
            st_ref[4 + k:5 + k, :] += _colsum(ahead * xl)
        dxl_ref[...] = dxl.astype(BF16)
        st_ref[0:1, :] += _colsum(dra)
        st_ref[1:2, :] += _colsum(dia)
        st_ref[2:3, :] += d_big_l * (LRU_C * _sigmoid(-lam_ref[...]))
        st_ref[3:4, :] += _colsum(dxc)

    col = lambda hd, b: (0, hd)
    head = lambda hd, b: (hd, 0, 0)
    tok = lambda hd, b: (b, hd)
    seq_blk = pl.BlockSpec((seq, HEAD), tok)
    return pl.pallas_call(
        _ordered(body, 12, (dproj,) + tuple(after)), name="lru_bwd", grid=(LRU_HEADS, nb),
        in_specs=[seq_blk] * 8 + [pl.BlockSpec((CONV_WIDTH, HEAD), col), pl.BlockSpec((None, HEAD, HEAD), head),
                                  pl.BlockSpec((None, HEAD, HEAD), head), pl.BlockSpec((1, HEAD), col)]
        + [_ANY] * (1 + len(after)),
        out_specs=[seq_blk, pl.BlockSpec((None, HEAD, HEAD), head), pl.BlockSpec((None, HEAD, HEAD), head),
                   pl.BlockSpec((8, HEAD), col)],
        out_shape=[jax.ShapeDtypeStruct(dproj.shape, BF16), jax.ShapeDtypeStruct((LRU_HEADS, HEAD, HEAD), F32),
                   jax.ShapeDtypeStruct((LRU_HEADS, HEAD, HEAD), F32), jax.ShapeDtypeStruct((8, w), F32)],
        input_output_aliases={12: 0},
        compiler_params=_cp(("arbitrary", "arbitrary")),
    )(proj, hs, e, dyl, *saved, w_conv, w_a, w_x, lam, dproj, *after)


def _weight_grad(a, g, col_shards, name, after=()):
    t, k = a.shape
    n = g.shape[1]
    tk = k if k <= 1536 else 1024
    ns = n // N_CHIPS if col_shards else n
    narrow = col_shards and ns < 512
    tn = n if narrow else min(ns, 768 if ns % 768 == 0 else 1024)
    while ns % tn and not narrow:
        tn //= 2
    per = max(ns // tn, 1)
    tt = min(TT_DW if (k // tk) * (n // tn) > 1 else TT_DW // 4, t)

    def body(a_ref, g_ref, o_ref):
        @pl.when(pl.program_id(2) == 0)
        def _():
            o_ref[...] = jnp.zeros_like(o_ref)

        res = _mm_tn(a_ref[...], g_ref[...])
        if narrow:
            for q in range(N_CHIPS):
                o_ref[q] += res[:, q * ns:(q + 1) * ns]
        else:
            o_ref[...] += res

    if narrow:
        out_spec = pl.BlockSpec((N_CHIPS, tk, ns), lambda i, j, s: (0, i, 0))
        out_shape = jax.ShapeDtypeStruct((N_CHIPS, k, ns), F32)
    elif col_shards:
        out_spec = pl.BlockSpec((None, tk, tn), lambda i, j, s: (j // per, i, j % per))
        out_shape = jax.ShapeDtypeStruct((N_CHIPS, k, ns), F32)
    else:
        out_spec = pl.BlockSpec((tk, tn), lambda i, j, s: (i, j))
        out_shape = jax.ShapeDtypeStruct((k, n), F32)
    return pl.pallas_call(
        _ordered(body, 2, after), name=name, grid=(k // tk, n // tn, t // tt),
        in_specs=[pl.BlockSpec((tt, tk), lambda i, j, s: (s, i)), pl.BlockSpec((tt, tn), lambda i, j, s: (s, j))]
        + [_ANY] * len(after),
        out_specs=out_spec, out_shape=out_shape,
        compiler_params=_cp(("arbitrary", "arbitrary", "arbitrary")),
    )(a, g, *after)


def _input_grad(dproj, ws, slots, dz1, x2, modv, nb, seq, after=()):
    t, d = x2.shape
    nq = len(ws)
    ns = ws[0].shape[1]
    tm = min(TM_DH, seq)
    ts = min(TS_MLP, tm)
    tpb = seq // tm

    def body(slot_ref, dp_ref, *refs):
        w_hbm = refs[:nq]
        dz1_ref, x_ref, mod_ref, gx_ref, db_ref, pb_ref, w_s, acc, sems = refs[nq:]
        i = pl.program_id(0)

        @pl.when(i == 0)
        def _():
            _load_weights(w_hbm, [w_s.at[slot_ref[k]] for k in range(nq)], sems)
            db_ref[...] = jnp.zeros_like(db_ref)

        @pl.when(i % tpb == 0)
        def _():
            pb_ref[...] = jnp.zeros_like(pb_ref)

        for sub in range(tm // ts):
            rows = slice(sub * ts, (sub + 1) * ts)
            for q in range(nq):
                dp = dp_ref[rows, q * ns:(q + 1) * ns]
                part = _mm_nt(dp, w_s[q])
                if q == 0:
                    acc[sub] = part
                else:
                    acc[sub] += part
                db_ref[q, 0:1, :] += _colsum(dp.astype(F32))
            dh = acc[sub]
            gx_ref[rows, :] = ALPHA * dz1_ref[rows, :] + dh * (1.0 + mod_ref[1:2, :])
            pb_ref[0:1, :] += _colsum(dh * x_ref[rows, :])
            pb_ref[1:2, :] += _colsum(dh)

    tok = lambda i, s: (i, 0)
    in_specs = [pl.BlockSpec((tm, nq * ns), tok)] + [_ANY] * nq
    in_specs += [pl.BlockSpec((tm, d), tok), pl.BlockSpec((tm, d), tok),
                 pl.BlockSpec((None, 8, d), lambda i, s: (i // tpb, 0, 0))] + [_ANY] * len(after)
    return pl.pallas_call(
        _ordered(body, 5 + nq, after), name="input_grad",
        grid_spec=pltpu.PrefetchScalarGridSpec(
            num_scalar_prefetch=1, grid=(t // tm,), in_specs=in_specs,
            out_specs=[pl.BlockSpec((tm, d), tok), pl.BlockSpec((nq, 8, ns), lambda i, s: (0, 0, 0)),
                       pl.BlockSpec((None, 8, d), lambda i, s: (i // tpb, 0, 0))],
            scratch_shapes=[pltpu.VMEM((nq, d, ns), BF16), pltpu.VMEM((tm // ts, ts, d), F32),
                            pltpu.SemaphoreType.DMA((nq,))]),
        out_shape=[jax.ShapeDtypeStruct((t, d), F32), jax.ShapeDtypeStruct((nq, 8, ns), F32),
                   jax.ShapeDtypeStruct((nb, 8, d), F32)],
        compiler_params=_cp(("arbitrary",)),
    )(slots, dproj, *ws, dz1, x2, modv, *after)


def _rows128(v):
    flat = v.reshape(-1, HEAD)
    pad = (-flat.shape[0]) % 8
    return jnp.pad(flat, ((0, pad), (0, 0))) if pad else flat


def kernel(x, c, w_ada, b_ada, w_in, b_in, w_conv, b_conv, w_rg_a, b_rg_a, w_rg_x, b_rg_x, lru_lambda, w_sp, b_sp, ln_v_g, ln_v_b, w_o_lru, w_o_sgu, w_out, ln1_g, ln1_b, w_up, w_down, ln2_g, ln2_b, loss_target, m_w_ada, m_b_ada, m_w_in, m_b_in, m_w_conv, m_b_conv, m_w_rg_a, m_b_rg_a, m_w_rg_x, m_b_rg_x, m_lru_lambda, m_w_sp, m_b_sp, m_ln_v_g, m_ln_v_b, m_w_o_lru, m_w_o_sgu, m_w_out, m_ln1_g, m_ln1_b, m_w_up, m_w_down, m_ln2_g, m_ln2_b, v_w_ada, v_b_ada, v_w_in, v_b_in, v_w_conv, v_b_conv, v_w_rg_a, v_b_rg_a, v_w_rg_x, v_b_rg_x, v_lru_lambda, v_w_sp, v_b_sp, v_ln_v_g, v_ln_v_b, v_w_o_lru, v_w_o_sgu, v_w_out, v_ln1_g, v_ln1_b, v_w_up, v_w_down, v_ln2_g, v_ln2_b):
    given = dict(locals())
    nb, seq, d = x.shape
    t = nb * seq
    w_lru = LRU_HEADS * HEAD
    d_sgu = SGU_GROUPS * HEAD
    xi, yi, ci = lax.axis_index("x"), lax.axis_index("y"), lax.axis_index("c")
    chip = 2 * xi + yi
    dev = 2 * chip + ci
    cidx = jnp.reshape(ci, (1,)).astype(jnp.int32)

    x2 = x.reshape(t, d)
    target = loss_target.reshape(t, d)

    big = ["w_in", "w_o_lru", "w_o_sgu", "w_out", "w_up", "w_down"]
    shards_a = [w_in[0].astype(BF16)]
    shards_b = [given[n][0].astype(BF16) for n in big[1:]]
    pidx = jnp.reshape(chip, (1,)).astype(jnp.int32)

    c_rows = _rows128(c)
    wconv_rows = _rows128(w_conv[0])
    def gather_direct(v, name):
        (lands,) = _exchange([v], [jax.ShapeDtypeStruct((N_DEV,) + v.shape, v.dtype)], _all_devices_copies, N_DEV - 1, name)
        mine = lax.broadcasted_iota(jnp.int32, (N_DEV, 1, 1), 0) == dev
        return jnp.where(mine, v[None], lands).reshape(N_DEV * v.shape[0], v.shape[1])

    slab0 = gather_direct(jnp.concatenate([c_rows, wconv_rows], axis=0), "gather_c_wconv")
    slab0 = slab0.reshape(N_DEV, -1, HEAD)
    c_all = slab0[:, :c_rows.shape[0]].reshape(N_DEV * nb, d)
    n_wc = CONV_WIDTH * (w_lru // N_CHIPS) // HEAD
    wc = slab0[0::2, c_rows.shape[0]:c_rows.shape[0] + n_wc].reshape(N_CHIPS, CONV_WIDTH, w_lru // N_CHIPS)
    w_conv_full = jnp.transpose(wc, (1, 0, 2)).reshape(CONV_WIDTH, w_lru)

    n_ada = w_ada.shape[2]
    b_ada_cols = lax.dynamic_slice(b_ada, (0, chip * n_ada), (1, n_ada))
    mod_cols = _ada_fwd(c_all, w_ada[0], b_ada_cols)
    half = (N_DEV * nb) // 2
    mod_half = lax.dynamic_slice(mod_cols, (ci * half, 0), (half, n_ada))
    mod_g = gather_direct(mod_half, "gather_mod").reshape(N_CHIPS, 2, half, n_ada)
    mod_all = jnp.transpose(mod_g, (1, 2, 0, 3)).reshape(N_DEV * nb, N_CHIPS * n_ada)
    mod_loc = lax.dynamic_slice(mod_all, (dev * nb, 0), (nb, N_CHIPS * n_ada)).reshape(nb, 6, d)
    modv = jnp.pad(mod_loc, ((0, 0), (0, 2), (0, 0)))

    lru_w = (w_conv_full, b_conv, w_rg_a[0], b_rg_a, w_rg_x[0], b_rg_x, lru_lambda)
    b_sp_t = jnp.transpose(b_sp[0])

    land = lambda s: jax.ShapeDtypeStruct((N_CHIPS,) + s.shape, s.dtype)
    sds = lambda s: jax.ShapeDtypeStruct(s.shape, s.dtype)
    started_a = _split_start(shards_a, [sds(shards_a[0])] * 2, _peer_gather_copies((0, 1)), 2, "gather_w_in_near_start",
                             after=(modv,))
    shards_b, shards_c = shards_b[:3], shards_b[3:]

    ids = lambda *v: jnp.stack(v).astype(jnp.int32)
    modv_t = modv + started_a[-1][0:1, 0:1]
    proj, h = _proj_fwd(x2, modv_t, [started_a[2]], ids(chip), b_in, seq, "proj_fwd_own")
    own_a, lands_a = _split_wait(started_a, 1, _peer_gather_copies((0, 1)), "gather_w_in_near_wait",
                                 after=(proj, *shards_b, *shards_c))
    started_f = _split_start(own_a, [sds(own_a[0])], _far_gather_copies, 1, "gather_w_in_far_start", after=(lands_a[0],))
    started_b = _split_start(shards_b, [land(s) for s in shards_b], _gather_copies, 3 * len(shards_b),
                             "gather_w_mix_start", after=(started_f[-1],))
    started_c = _split_start(shards_c, [land(s) for s in shards_c], _gather_copies, 3 * len(shards_c),
                             "gather_w_mlp_start", after=(started_b[-1],))
    modv_t = modv + started_c[-1][0:1, 0:1]
    (proj,) = _proj_fwd(x2, modv_t, lands_a, ids(chip ^ 1, chip ^ 2), b_in, seq, "proj_fwd_near", proj_in=proj)
    own_a, land_f = _split_wait(started_f, 1, _far_gather_copies, "gather_w_in_far_wait", after=(proj,))
    (proj,) = _proj_fwd(x2, modv, land_f, ids(chip ^ 3), b_in, seq, "proj_fwd_far", proj_in=proj)
    w_in_shards, w_in_chips = own_a + lands_a + land_f, ids(chip, chip ^ 1, chip ^ 2, chip ^ 3)
    a, inp, r16, gi16, xc16 = _lru_prep(proj, lru_w, nb, seq)
    a3 = a.reshape(nb, seq, w_lru)
    hs = _scan(a3, inp.reshape(nb, seq, w_lru), False, "lru_scan", BF16).reshape(t, w_lru)
    y_sgu = _sgu_fwd(proj, w_sp[0], b_sp_t, ln_v_g, ln_v_b)
    shards_b, lands_b = _split_wait(started_b, len(shards_b), _gather_copies, "gather_w_mix_wait", after=(hs, y_sgu))
    w_o_lru_g, w_o_sgu_g, w_out_g = _fill_own_slot(lands_b, shards_b, pidx, ["own_" + n for n in big[1:4]])
    w_o_lru_g = w_o_lru_g.reshape(w_lru, d)
    w_out_g = w_out_g.reshape(d, d)
    yap, y_a, y_b, merged, mix, x1 = _mix_fwd(hs, proj, y_sgu, x2, modv, w_o_lru_g, w_o_sgu_g, w_out_g, ln1_g, ln1_b, seq)
    shards_c, lands_c = _split_wait(started_c, len(shards_c), _gather_copies, "gather_w_mlp_wait", after=(x1,))
    w_up_g, w_down_g = _fill_own_slot(lands_c, shards_c, pidx, ["own_" + n for n in big[4:]])
    w_down_g = w_down_g.reshape(-1, d)
    up, act, h2, dz2, df, st2, pb2 = _mlp_fwd(x1, modv, w_up_g, w_down_g, ln2_g, ln2_b, target, nb, seq)

    part = {}

    def to_sibling_start(group, tag, after=()):
        g4 = []
        for n in group:
            shard = given[n].shape[1:]
            g4.append(part[n].reshape(N_CHIPS, 2, shard[0] // 2, shard[1]))
        shapes = [jax.ShapeDtypeStruct((N_CHIPS,) + g.shape[2:], F32) for g in g4]
        return _split_start(g4, shapes, _to_sibling_copies, len(g4), "grads_to_sibling_start_" + tag, after)

    def to_chips_start(group, started, tag, after=()):
        g4, recv = _split_wait(started, len(group), _to_sibling_copies, "grads_to_sibling_wait_" + tag, after)
        own4 = [_add_own_half(g4[k], recv[k], cidx, "grad_pair_sum_" + n) for k, n in enumerate(group)]
        shapes = [jax.ShapeDtypeStruct((3,) + o.shape[1:], BF16) for o in own4]
        return _split_start(own4, shapes, _chip_exchange_copies, 3 * len(own4), "grads_chip_exchange_start_" + tag)

    def chips_finish(group, started, tag, after=()):
        own4, slots = _split_wait(started, len(group), _chip_exchange_copies, "grads_chip_exchange_wait_" + tag, after)
        return [_sum_own_and_peers(own4[k], slots[k], pidx, "grad_chip_sum_" + n) for k, n in enumerate(group)]

    dup, dz1, dmix, st1, pb1 = _mlp_bwd(df, up, w_down_g, w_up_g, dz2, x2, mix, modv, ln1_g, ln1_b, nb, seq)
    group1 = ["w_up", "w_down"]
    part["w_up"] = _weight_grad(h2, dup, True, "grad_w_up")
    part["w_down"] = _weight_grad(act, df, False, "grad_w_down")
    sib1 = to_sibling_start(group1, "mlp")
    dy_a, dy_b, dproj, dyl, dys = _mix_bwd(dmix, proj, y_a, y_b, hs, w_out_g, w_o_lru_g, w_o_sgu_g, seq,
                                                after=(sib1[-1],))
    group2 = ["w_o_lru", "w_o_sgu", "w_out"]
    part["w_o_lru"] = _weight_grad(yap, dy_a, False, "grad_w_o_lru")
    part["w_o_sgu"] = _weight_grad(y_sgu, dy_b, True, "grad_w_o_sgu")
    part["w_out"] = _weight_grad(merged, dmix, False, "grad_w_out")
    chips1 = to_chips_start(group1, sib1, "mlp", after=(dys, part["w_o_lru"], part["w_o_sgu"], part["w_out"]))
    sib2 = to_sibling_start(group2, "mix", after=(chips1[-1],))
    du, dv, g_w_sp, st_sgu, g_b_sp_t = _sgu_bwd(proj, dys, w_sp[0], b_sp_t, ln_v_g, ln_v_b, after=(sib2[-1],))
    dyl3 = dyl.reshape(nb, seq, w_lru)
    e = _scan(a3, dyl3, True, "lru_scan_bwd", BF16).reshape(t, w_lru)
    chips2 = to_chips_start(group2, sib2, "mix", after=(e, du))
    dproj = lax.dynamic_update_slice(dproj, du, (0, 2 * w_lru))
    dproj = lax.dynamic_update_slice(dproj, dv, (0, 2 * w_lru + d_sgu))
    dproj, g_w_rg_a, g_w_rg_x, st_lru = _lru_bwd(proj, hs, e, dyl, (a, r16, gi16, xc16), lru_w, nb, seq, dproj,
                                                 after=(chips2[-1],))

    didx = jnp.reshape(dev, (1,)).astype(jnp.int32)
    early = [
        ("w_conv", st_lru[4:8]), ("b_conv", st_lru[3]), ("w_rg_a", g_w_rg_a), ("b_rg_a", st_lru[0]),
        ("w_rg_x", g_w_rg_x), ("b_rg_x", st_lru[1]), ("lru_lambda", st_lru[2]), ("w_sp", g_w_sp),
        ("b_sp", jnp.transpose(g_b_sp_t[:, :SGU_GROUPS])), ("ln_v_g", st_sgu[0]), ("ln_v_b", st_sgu[1]),
        ("ln1_g", st1[0]), ("ln1_b", st1[1]), ("ln2_g", st2[0]), ("ln2_b", st2[1]),
        ("loss", st2[2:3, 0:HEAD]),
    ]
    pieces_e = [_rows128(v) for _, v in early]
    slab_e = jnp.concatenate(pieces_e, axis=0)
    slab_e = jnp.pad(slab_e, ((0, (-slab_e.shape[0]) % TR_EW), (0, 0)))
    small_st = _split_start([slab_e], [jax.ShapeDtypeStruct((N_DEV,) + slab_e.shape, F32)], _all_devices_copies, N_DEV - 1,
                            "small_grads_start")

    group3 = ["w_in"]
    part["w_in"] = _weight_grad(h, dproj, True, "grad_w_in", after=(small_st[-1],))
    sib3 = to_sibling_start(group3, "in")
    halves12 = (chips_finish(group1, chips1, "mlp", after=(sib3[-1],))
                + chips_finish(group2, chips2, "mix", after=(sib3[-1],)))
    swap12 = _split_start(halves12, [jax.ShapeDtypeStruct(hv.shape, F32) for hv in halves12], _swap_copies, len(halves12),
                          "grads_swap_start")
    chips3 = to_chips_start(group3, sib3, "in", after=(swap12[-1],))
    grad_x2, g_b_in4, pb0 = _input_grad(dproj, w_in_shards, w_in_chips, dz1, x2, modv, nb, seq, after=(chips3[-1],))
    grads = {}
    two_d = lambda v: v.reshape(-1, v.shape[-1])
    done = {}

    def adamw_big(n, mine_n, theirs_n):
        done[n] = _adamw_halves(two_d(given[n]), mine_n, theirs_n, two_d(given["m_" + n]), two_d(given["v_" + n]), cidx,
                                "adamw_" + n)

    dmod_loc = jnp.stack([pb0[:, 1], pb0[:, 0], pb1[:, 2], pb1[:, 1], pb1[:, 0], pb2[:, 0]], axis=1)
    rows_dmod = dmod_loc.size // HEAD
    slab_l = jnp.concatenate([_rows128(dmod_loc), _rows128(g_b_in4[:, 0])], axis=0)
    late_st = _split_start([slab_l], [jax.ShapeDtypeStruct((N_DEV,) + slab_l.shape, F32)], _all_devices_copies, N_DEV - 1,
                           "late_grads_start")
    mine12, theirs12 = _split_wait(swap12, len(halves12), _swap_copies, "grads_swap_wait", after=(late_st[-1],))
    for n, mine_n, theirs_n in zip(group1 + group2, mine12, theirs12):
        adamw_big(n, mine_n, theirs_n)
    (slab_l,), (lands_l,) = _split_wait(late_st, 1, _all_devices_copies, "late_grads_wait",
                                        after=tuple(done[n][1] for n in group1 + group2))
    every = jnp.where(lax.broadcasted_iota(jnp.int32, (N_DEV, 1, 1), 0) == dev, slab_l[None], lands_l)
    dmod_all = every[:, :rows_dmod].reshape(N_DEV * nb, 6 * d)
    grads["b_in"] = _sum_slots(every[:, rows_dmod:], "grad_b_in_sum").reshape(1, -1)

    (slab_e,), (lands_e,) = _split_wait(small_st, 1, _all_devices_copies, "small_grads_wait", after=(dmod_all,))
    summed = _sum_devices(lands_e, slab_e, didx, "small_grad_sum")
    off = 0
    for (n, v), piece in zip(early, pieces_e):
        grads[n] = summed[off:off + v.size // HEAD].reshape(v.shape)
        off += piece.shape[0]
    loss = grads.pop("loss")[0, 0]

    (mine3,) = chips_finish(group3, chips3, "in", after=(summed,))
    (theirs3,) = _exchange([mine3], [jax.ShapeDtypeStruct(mine3.shape, F32)], _swap_copies, 1, "grads_swap_w_in")
    adamw_big("w_in", mine3, theirs3)

    dmod_cols = lax.dynamic_slice(dmod_all, (0, chip * n_ada), (N_DEV * nb, n_ada))
    grads["w_ada"], grads["b_ada"] = _ada_bwd(c_all, dmod_all, dmod_cols)
    n_wcs = w_lru // N_CHIPS
    grads["w_conv"] = lax.dynamic_slice(grads["w_conv"], (0, chip * n_wcs), (CONV_WIDTH, n_wcs))

    names = ['w_ada', 'b_ada', 'w_in', 'b_in', 'w_conv', 'b_conv', 'w_rg_a', 'b_rg_a', 'w_rg_x', 'b_rg_x', 'lru_lambda',
             'w_sp', 'b_sp', 'ln_v_g', 'ln_v_b', 'w_o_lru', 'w_o_sgu', 'w_out', 'ln1_g', 'ln1_b', 'w_up', 'w_down',
             'ln2_g', 'ln2_b']
    small_names = [n for n in names if n not in big and n != "w_ada"]
    small_out = _adamw_many([(two_d(given[n]), two_d(grads[n].reshape(given[n].shape)), two_d(given["m_" + n]),
                              two_d(given["v_" + n])) for n in small_names], "adamw_small")
    for n, res in zip(small_names, small_out):
        done[n] = (grads[n],) + tuple(res)
    done["w_ada"] = (grads["w_ada"],) + tuple(_adamw(two_d(given["w_ada"]), two_d(grads["w_ada"]), two_d(given["m_w_ada"]),
                                                     two_d(given["v_w_ada"]), "adamw_w_ada"))
    outs = [[done[n][k].reshape(given[n].shape) for n in names] for k in range(4)]
    return (loss, grad_x2.reshape(nb, seq, d), *outs[0], *outs[1], *outs[2], *outs[3])
```

```python
import functools
import math

import jax
import jax.numpy as jnp
from jax import lax
from jax.experimental import pallas as pl
from jax.experimental.pallas import tpu as pltpu

F32 = jnp.float32
BF16 = jnp.bfloat16
MESH = pl.DeviceIdType.MESH

N_CHIPS = 4
N_DEV = 8
LRU_HEADS = 10
HEAD = 128
SGU_GROUPS = 6
SGU_CHUNK = 64
CONV_WIDTH = 4
LRU_C = 8.0
ALPHA = 2.0 ** 0.25
LN_EPS = 1e-5
ADAM_LR, ADAM_B1, ADAM_B2, ADAM_EPS, ADAM_WD, ADAM_STEP = 0.001, 0.9, 0.999, 1e-08, 0.01, 10

VMEM_LIMIT = 56 * 1024 * 1024
VMEM_LIMIT_MAX = 62 * 1024 * 1024
TM_PROJ = 1024
TM_MIX = 512
TM_MLP = 512
TS_MLP = 256
TM_SGU = 512
TM_DH = 512
TT_DW = 4096
TC_SCAN = 256
TR_EW = 256


def _cp(sem=None, limit=None):
    return pltpu.CompilerParams(dimension_semantics=sem, vmem_limit_bytes=limit or VMEM_LIMIT)


def _mm(a, b):
    return jnp.dot(a.astype(BF16), b.astype(BF16), preferred_element_type=F32)


def _mm_nt(a, b):
    return lax.dot_general(a.astype(BF16), b.astype(BF16), (((1,), (1,)), ((), ())), preferred_element_type=F32)


def _mm_tn(a, b):
    return lax.dot_general(a.astype(BF16), b.astype(BF16), (((0,), (0,)), ((), ())), preferred_element_type=F32)


def _sigmoid(x):
    return 1.0 / (1.0 + jnp.exp(-x))


def _sigmoid_t(x):
    return 0.5 * jnp.tanh(0.5 * x) + 0.5


_GELU_K = math.sqrt(2.0 / math.pi)


def _gelu(x):
    t = jnp.tanh(_GELU_K * (x + 0.044715 * (x * x * x)))
    return 0.5 * x * (1.0 + t)


def _gelu_and_grad(x):
    x2 = x * x
    t = jnp.tanh(_GELU_K * (x + 0.044715 * (x2 * x)))
    g = 0.5 * x * (1.0 + t)
    dg = 0.5 * (1.0 + t) + 0.5 * x * (1.0 - t * t) * (_GELU_K * (1.0 + 3.0 * 0.044715 * x2))
    return g, dg


def _ln_stats(z):
    mu = jnp.mean(z, axis=-1, keepdims=True)
    zc = z - mu
    var = jnp.mean(zc * zc, axis=-1, keepdims=True)
    rstd = lax.rsqrt(var + LN_EPS)
    return zc * rstd, rstd


def _ln_bwd(dxh, xhat, rstd):
    m1 = jnp.mean(dxh, axis=-1, keepdims=True)
    m2 = jnp.mean(dxh * xhat, axis=-1, keepdims=True)
    return rstd * (dxh - m1 - xhat * m2)


def _colsum(v):
    return jnp.sum(v, axis=0, keepdims=True)


def _shift_down(v, j):
    if j == 0:
        return v
    rows = lax.broadcasted_iota(jnp.int32, v.shape, 0)
    return jnp.where(rows >= j, pltpu.roll(v, j, 0), 0.0)


def _shift_up(v, j):
    if j == 0:
        return v
    n = v.shape[0]
    rows = lax.broadcasted_iota(jnp.int32, v.shape, 0)
    return jnp.where(rows < n - j, pltpu.roll(v, n - j, 0), 0.0)


def _load_weights(srcs, dsts, sems):
    cps = [pltpu.make_async_copy(s, dd, sems.at[k]) for k, (s, dd) in enumerate(zip(srcs, dsts))]
    for cp in cps:
        cp.start()
    for cp in cps:
        cp.wait()


def _my_pos():
    return lax.axis_index("x"), lax.axis_index("y"), lax.axis_index("c")


_HBM = pl.BlockSpec(memory_space=pltpu.HBM)
_ANY = pl.BlockSpec(memory_space=pl.ANY)
_SEM = pl.BlockSpec(memory_space=pltpu.SEMAPHORE)
_EFFECT = pltpu.SideEffectType.DATAFLOW_SIDE_EFFECTING


def _ordered(body, n_in, after):
    k = len(after)
    if not k:
        return body
    return lambda *refs: body(*refs[:n_in], *refs[n_in + k:])


def _gather_copies(ins, lands, send_sems, recv_sems):
    x, y, c = _my_pos()
    p = 2 * x + y
    peers = [(x, 1 - y), (1 - x, y), (1 - x, 1 - y)]
    sends, recvs = [], []
    for k in range(len(ins)):
        for j, (qx, qy) in enumerate(peers):
            sems = dict(send_sem=send_sems.at[3 * k + j], recv_sem=recv_sems.at[3 * k + j],
                        device_id=(qx, qy, c), device_id_type=MESH)
            sends.append(pltpu.make_async_remote_copy(src_ref=ins[k], dst_ref=lands[k].at[p], **sems))
            recvs.append(pltpu.make_async_remote_copy(src_ref=ins[k], dst_ref=lands[k].at[2 * qx + qy], **sems))
    return sends, recvs


def _peer_gather_copies(peers):
    def copies(ins, lands, send_sems, recv_sems):
        x, y, c = _my_pos()
        where = [(x, 1 - y), (1 - x, y), (1 - x, 1 - y)]
        cps = [pltpu.make_async_remote_copy(
            src_ref=ins[0], dst_ref=lands[j], send_sem=send_sems.at[j], recv_sem=recv_sems.at[j],
            device_id=(*where[j], c), device_id_type=MESH) for j in peers]
        return cps, cps
    return copies


def _far_gather_copies(ins, lands, send_sems, recv_sems):
    x, y, c = _my_pos()
    cps = [pltpu.make_async_remote_copy(
        src_ref=ins[0], dst_ref=lands[0], send_sem=send_sems.at[0], recv_sem=recv_sems.at[0],
        device_id=(1 - x, 1 - y, c), device_id_type=MESH)]
    return cps, cps


def _to_sibling_copies(ins, lands, send_sems, recv_sems):
    x, y, c = _my_pos()
    cps = [pltpu.make_async_remote_copy(
        src_ref=ins[k].at[:, 1 - c], dst_ref=lands[k], send_sem=send_sems.at[k], recv_sem=recv_sems.at[k],
        device_id=(x, y, 1 - c), device_id_type=MESH) for k in range(len(ins))]
    return cps, cps


def _chip_exchange_copies(ins, lands, send_sems, recv_sems):
    x, y, c = _my_pos()
    peers = [(x, 1 - y), (1 - x, y), (1 - x, 1 - y)]
    cps = []
    for k in range(len(ins)):
        for j, (qx, qy) in enumerate(peers):
            cps.append(pltpu.make_async_remote_copy(
                src_ref=ins[k].at[2 * qx + qy], dst_ref=lands[k].at[j], send_sem=send_sems.at[3 * k + j],
                recv_sem=recv_sems.at[3 * k + j], device_id=(qx, qy, c), device_id_type=MESH))
    return cps, cps


def _all_devices_copies(ins, lands, send_sems, recv_sems):
    x, y, c = _my_pos()
    me = 4 * x + 2 * y + c
    sends, recvs = [], []
    for r in range(1, N_DEV):
        px = 1 - x if r & 4 else x
        py = 1 - y if r & 2 else y
        pc = 1 - c if r & 1 else c
        sems = dict(send_sem=send_sems.at[r - 1], recv_sem=recv_sems.at[r - 1], device_id=(px, py, pc), device_id_type=MESH)
        sends.append(pltpu.make_async_remote_copy(src_ref=ins[0], dst_ref=lands[0].at[me], **sems))
        recvs.append(pltpu.make_async_remote_copy(src_ref=ins[0], dst_ref=lands[0].at[4 * px + 2 * py + pc], **sems))
    return sends, recvs


def _swap_copies(ins, lands, send_sems, recv_sems):
    x, y, c = _my_pos()
    cps = [pltpu.make_async_remote_copy(
        src_ref=ins[k], dst_ref=lands[k], send_sem=send_sems.at[k], recv_sem=recv_sems.at[k],
        device_id=(x, y, 1 - c), device_id_type=MESH) for k in range(len(ins))]
    return cps, cps


def _split_start(ins, land_shapes, copies, n_sems, name, after=()):
    n, nl = len(ins), len(land_shapes)
    first_out = n + nl + len(after)

    def body(*refs):
        in_refs, land_refs = refs[:n], refs[n:n + nl]
        send_sems, recv_sems = refs[first_out:first_out + 2]
        token = refs[-1]
        sends, _ = copies(in_refs, land_refs, send_sems, recv_sems)
        for cp in sends:
            cp.start()
        token[...] = jnp.zeros_like(token)

    lands = [pltpu.with_memory_space_constraint(lax.empty(s.shape, s.dtype), pltpu.HBM) for s in land_shapes]
    ins = [pltpu.with_memory_space_constraint(s, pltpu.HBM) for s in ins]
    return pl.pallas_call(
        body, name=name,
        out_shape=(pltpu.SemaphoreType.DMA((n_sems,)), pltpu.SemaphoreType.DMA((n_sems,)),
                   *[pltpu.HBM(s.shape, s.dtype) for s in ins], *[pltpu.HBM(s.shape, s.dtype) for s in lands],
                   jax.ShapeDtypeStruct((8, HEAD), F32)),
        in_specs=[_HBM] * (n + nl) + [pl.BlockSpec(memory_space=pl.ANY)] * len(after),
        out_specs=(_SEM, _SEM, *([_HBM] * (n + nl)), pl.BlockSpec(memory_space=pltpu.VMEM)),
        input_output_aliases={k: 2 + k for k in range(n + nl)},
        compiler_params=pltpu.CompilerParams(has_side_effects=_EFFECT),
    )(*ins, *lands, *after)


def _split_wait(started, n, copies, name, after=()):
    send_sems, recv_sems = started[0], started[1]
    bufs = started[2:-1]
    nb = len(bufs)

    def body(*refs):
        in_refs, land_refs = refs[:n], refs[n:nb]
        sends, recvs = copies(in_refs, land_refs, refs[nb], refs[nb + 1])
        for cp in sends:
            cp.wait_send()
        for cp in recvs:
            cp.wait_recv()

    outs = pl.pallas_call(
        body, name=name,
        out_shape=tuple(pltpu.HBM(s.shape, s.dtype) for s in bufs),
        in_specs=[_HBM] * nb + [_SEM, _SEM] + [pl.BlockSpec(memory_space=pl.ANY)] * len(after),
        out_specs=tuple([_HBM] * nb),
        input_output_aliases={k: k for k in range(nb)},
        compiler_params=pltpu.CompilerParams(has_side_effects=_EFFECT),
    )(*bufs, send_sems, recv_sems, *after)
    return list(outs[:n]), list(outs[n:])


def _fill_own_slot(gathered, shards, pidx, names):
    outs = []
    for g, s, name in zip(gathered, shards, names):
        r, cdim = s.shape
        tr = _row_tile(r)

        def body(p_ref, s_ref, g_ref, o_ref):
            o_ref[...] = s_ref[...]

        outs.append(pl.pallas_call(
            body, name=name,
            grid_spec=pltpu.PrefetchScalarGridSpec(
                num_scalar_prefetch=1, grid=(r // tr,),
                in_specs=[pl.BlockSpec((tr, cdim), lambda i, p: (i, 0)), pl.BlockSpec(memory_space=pl.ANY)],
                out_specs=pl.BlockSpec((None, tr, cdim), lambda i, p: (p[0], i, 0))),
            out_shape=jax.ShapeDtypeStruct(g.shape, g.dtype),
            input_output_aliases={2: 0},
            compiler_params=_cp(("arbitrary",)),
        )(pidx, s, g))
    return outs


def _sum_own_and_peers(own4, slots, pidx, name):
    _, rh, cdim = own4.shape
    tr = _row_tile(rh)

    def body(p_ref, own_ref, s_ref, o_ref):
        acc = own_ref[...].astype(F32)
        for j in range(3):
            acc = acc + s_ref[j].astype(F32)
        o_ref[...] = acc

    return pl.pallas_call(
        body, name=name,
        grid_spec=pltpu.PrefetchScalarGridSpec(
            num_scalar_prefetch=1, grid=(rh // tr,),
            in_specs=[pl.BlockSpec((None, tr, cdim), lambda i, p: (p[0], i, 0)),
                      pl.BlockSpec((3, tr, cdim), lambda i, p: (0, i, 0))],
            out_specs=pl.BlockSpec((tr, cdim), lambda i, p: (i, 0))),
        out_shape=jax.ShapeDtypeStruct((rh, cdim), F32),
        compiler_params=_cp(("arbitrary",)),
    )(pidx, own4, slots)


def _exchange(ins, land_shapes, copies, n_sems, name):
    n, nl = len(ins), len(land_shapes)

    def body(*refs):
        sends, recvs = copies(refs[:n], refs[n:n + nl], refs[n + nl], refs[n + nl + 1])
        for cp in sends:
            cp.start()
        for cp in sends:
            cp.wait_send()
        for cp in recvs:
            cp.wait_recv()

    any_spec = pl.BlockSpec(memory_space=pl.ANY)
    return pl.pallas_call(
        body, name=name,
        out_shape=[jax.ShapeDtypeStruct(s.shape, s.dtype) for s in land_shapes],
        in_specs=[any_spec] * n, out_specs=[any_spec] * nl,
        scratch_shapes=[pltpu.SemaphoreType.DMA((n_sems,)), pltpu.SemaphoreType.DMA((n_sems,))],
    )(*ins)


def _row_tile(r):
    t = min(TR_EW, r)
    while r % t:
        t //= 2
    return t


def _add_own_half(g4, recv, cidx, name):
    _, _, rh, cdim = g4.shape
    tr = _row_tile(rh)

    def body(c_ref, a_ref, b_ref, o_ref):
        o_ref[...] = (a_ref[...] + b_ref[...]).astype(BF16)

    return pl.pallas_call(
        body, name=name,
        grid_spec=pltpu.PrefetchScalarGridSpec(
            num_scalar_prefetch=1, grid=(N_CHIPS, rh // tr),
            in_specs=[pl.BlockSpec((None, None, tr, cdim), lambda q, i, c: (q, c[0], i, 0)),
                      pl.BlockSpec((None, tr, cdim), lambda q, i, c: (q, i, 0))],
            out_specs=pl.BlockSpec((None, tr, cdim), lambda q, i, c: (q, i, 0))),
        out_shape=jax.ShapeDtypeStruct(recv.shape, BF16),
        compiler_params=_cp(("arbitrary", "arbitrary")),
    )(cidx, g4, recv)


def _sum_slots(v, name):
    n, r, cdim = v.shape
    tr = _row_tile(r)

    def body(v_ref, o_ref):
        acc = v_ref[0].astype(F32)
        for k in range(1, n):
            acc = acc + v_ref[k].astype(F32)
        o_ref[...] = acc

    return pl.pallas_call(
        body, name=name, grid=(r // tr,),
        in_specs=[pl.BlockSpec((n, tr, cdim), lambda i: (0, i, 0))],
        out_specs=pl.BlockSpec((tr, cdim), lambda i: (i, 0)),
        out_shape=jax.ShapeDtypeStruct((r, cdim), F32),
        compiler_params=_cp(("arbitrary",)),
    )(v)


def _sum_devices(lands, own, didx, name):
    _, r, cdim = lands.shape
    tr = _row_tile(r)

    def body(d_ref, l_ref, own_ref, o_ref):
        acc = jnp.where(d_ref[0] == 0, own_ref[...], l_ref[0])
        for dv in range(1, N_DEV):
            acc = acc + jnp.where(d_ref[0] == dv, own_ref[...], l_ref[dv])
        o_ref[...] = acc

    return pl.pallas_call(
        body, name=name,
        grid_spec=pltpu.PrefetchScalarGridSpec(
            num_scalar_prefetch=1, grid=(r // tr,),
            in_specs=[pl.BlockSpec((N_DEV, tr, cdim), lambda i, dd: (0, i, 0)), pl.BlockSpec((tr, cdim), lambda i, dd: (i, 0))],
            out_specs=pl.BlockSpec((tr, cdim), lambda i, dd: (i, 0))),
        out_shape=jax.ShapeDtypeStruct((r, cdim), F32),
        compiler_params=_cp(("arbitrary",)),
    )(didx, lands, own)


def _adamw_math(wv, gg, mv, vv):
    nm = ADAM_B1 * mv + (1.0 - ADAM_B1) * gg
    nv = ADAM_B2 * vv + (1.0 - ADAM_B2) * (gg * gg)
    m_hat = nm / (1.0 - ADAM_B1 ** ADAM_STEP)
    v_hat = nv / (1.0 - ADAM_B2 ** ADAM_STEP)
    return -ADAM_LR * (m_hat / (jnp.sqrt(v_hat) + ADAM_EPS) + ADAM_WD * wv), nm, nv


def _adamw_halves(w, mine, theirs, m, v, cidx, name):
    r, cdim = w.shape
    rh = r // 2
    tr = _row_tile(rh)
    nblk = rh // tr

    def body(c_ref, w_ref, a_ref, b_ref, m_ref, v_ref, g_ref, d_ref, nm_ref, nv_ref):
        gg = jnp.where(pl.program_id(0) == c_ref[0], a_ref[...], b_ref[...])
        g_ref[...] = gg
        d_ref[...], nm_ref[...], nv_ref[...] = _adamw_math(w_ref[...], gg, m_ref[...], v_ref[...])

    full = pl.BlockSpec((tr, cdim), lambda hh, i, c: (hh * nblk + i, 0))
    half = pl.BlockSpec((tr, cdim), lambda hh, i, c: (i, 0))
    return pl.pallas_call(
        body, name=name,
        grid_spec=pltpu.PrefetchScalarGridSpec(
            num_scalar_prefetch=1, grid=(2, nblk),
            in_specs=[full, half, half, full, full], out_specs=[full] * 4),
        out_shape=[jax.ShapeDtypeStruct((r, cdim), F32)] * 4,
        compiler_params=_cp(("arbitrary", "arbitrary")),
    )(cidx, w, mine, theirs, m, v)


def _adamw_many(params, name):
    n = len(params)

    def body(*refs):
        ins, outs = refs[:4 * n], refs[4 * n:]
        for k in range(n):
            w_ref, g_ref, m_ref, v_ref = ins[4 * k:4 * k + 4]
            outs[3 * k][...], outs[3 * k + 1][...], outs[3 * k + 2][...] = _adamw_math(
                w_ref[...], g_ref[...], m_ref[...], v_ref[...])

    flat = [a for p in params for a in p]
    res = pl.pallas_call(
        body, name=name,
        out_shape=[jax.ShapeDtypeStruct(p[0].shape, F32) for p in params for _ in range(3)],
        compiler_params=pltpu.CompilerParams(vmem_limit_bytes=VMEM_LIMIT),
    )(*flat)
    return [res[3 * k:3 * k + 3] for k in range(n)]


def _adamw(w, g, m, v, name):
    r, cdim = w.shape
    tr = _row_tile(r) if r % 8 == 0 else r

    def body(w_ref, g_ref, m_ref, v_ref, d_ref, nm_ref, nv_ref):
        d_ref[...], nm_ref[...], nv_ref[...] = _adamw_math(w_ref[...], g_ref[...], m_ref[...], v_ref[...])

    spec = pl.BlockSpec((tr, cdim), lambda i: (i, 0))
    return pl.pallas_call(
        body, name=name, grid=(r // tr,), in_specs=[spec] * 4, out_specs=[spec] * 3,
        out_shape=[jax.ShapeDtypeStruct((r, cdim), F32)] * 3,
        compiler_params=_cp(("arbitrary",)),
    )(w, g, m, v)


def _ada_fwd(c_all, w_ada, b_cols):
    nb, _ = c_all.shape
    n = w_ada.shape[1]

    def body(c_ref, w_ref, b_ref, o_ref):
        cv = c_ref[...]
        o_ref[...] = _mm(cv * _sigmoid(cv), w_ref[...]) + b_ref[...]

    return pl.pallas_call(
        body, name="ada_fwd", out_shape=jax.ShapeDtypeStruct((nb, n), F32),
        compiler_params=pltpu.CompilerParams(vmem_limit_bytes=VMEM_LIMIT),
    )(c_all, w_ada, b_cols)


def _ada_bwd(c_all, dmod_all, dmod_cols):
    d = c_all.shape[1]
    n = dmod_cols.shape[1]

    def body(c_ref, da_ref, dc_ref, gw_ref, gb_ref):
        cv = c_ref[...]
        gw_ref[...] = _mm_tn(cv * _sigmoid(cv), dc_ref[...])
        gb_ref[...] = _colsum(da_ref[...])

    return pl.pallas_call(
        body, name="ada_bwd",
        out_shape=[jax.ShapeDtypeStruct((d, n), F32), jax.ShapeDtypeStruct((1, dmod_all.shape[1]), F32)],
        compiler_params=pltpu.CompilerParams(vmem_limit_bytes=VMEM_LIMIT),
    )(c_all, dmod_all, dmod_cols)


def _proj_fwd(x2, modv, ws, cols, b_in, seq, name, proj_in=None):
    t, d = x2.shape
    n = len(ws)
    ns = ws[0].shape[1]
    tm = min(TM_PROJ, seq)
    tpb = seq // tm
    first = proj_in is None

    def body(c_ref, x_ref, mod_ref, *refs):
        w_refs, b_ref = refs[:n], refs[n]
        outs = refs[n + 1 if first else n + 2:]
        proj_ref, h_s = outs[0], outs[-1]
        s = pl.program_id(1)

        @pl.when(s == 0)
        def _():
            if first:
                h = (x_ref[...] * (1.0 + mod_ref[1:2, :]) + mod_ref[0:1, :]).astype(BF16)
                h_s[...] = h
                outs[1][...] = h
            else:
                h_s[...] = x_ref[...]

        for k in range(n):
            @pl.when(s == k)
            def _():
                proj_ref[...] = (jnp.dot(h_s[...], w_refs[k][...], preferred_element_type=F32) + b_ref[...]).astype(BF16)

    in_specs = [pl.BlockSpec((tm, d), lambda i, s, c: (i, 0)),
                pl.BlockSpec((None, 8, d), lambda i, s, c: (i // tpb, 0, 0))]
    in_specs += [pl.BlockSpec((d, ns), lambda i, s, c: (0, 0))] * n
    in_specs += [pl.BlockSpec((1, ns), lambda i, s, c: (0, c[s]))]
    out_specs = [pl.BlockSpec((tm, ns), lambda i, s, c: (i, c[s]))]
    out_shape = [jax.ShapeDtypeStruct((t, N_CHIPS * ns), BF16)]
    args = [cols, x2, modv, *ws, b_in]
    aliases = {}
    if first:
        out_specs.append(pl.BlockSpec((tm, d), lambda i, s, c: (i, 0)))
        out_shape.append(jax.ShapeDtypeStruct((t, d), BF16))
    else:
        in_specs.append(_ANY)
        args.append(proj_in)
        aliases = {len(args) - 1: 0}
    return pl.pallas_call(
        body, name=name,
        grid_spec=pltpu.PrefetchScalarGridSpec(
            num_scalar_prefetch=1, grid=(t // tm, n), in_specs=in_specs, out_specs=out_specs,
            scratch_shapes=[pltpu.VMEM((tm, d), BF16)]),
        out_shape=out_shape, input_output_aliases=aliases,
        compiler_params=_cp(("arbitrary", "arbitrary")),
    )(*args)


def _lru_rate(lam_ref):
    nl = -lam_ref[...]
    e = jnp.exp(-jnp.abs(nl))
    u = 1.0 + e
    dlt = u - 1.0
    log1p_e = jnp.where(dlt == 0.0, e, jnp.log(u) * (e / jnp.where(dlt == 0.0, 1.0, dlt)))
    return -LRU_C * (jnp.maximum(nl, 0.0) + log1p_e)


def _lru_gates(xl, wc_ref, bc_ref, wa_ref, ba_ref, wx_ref, bx_ref, lam_ref):
    xc = bc_ref[...] + wc_ref[CONV_WIDTH - 1:CONV_WIDTH, :] * xl
    for k in range(CONV_WIDTH - 1):
        xc = xc + wc_ref[k:k + 1, :] * _shift_down(xl, CONV_WIDTH - 1 - k)
    r = _sigmoid(_mm(xc, wa_ref[...]) + ba_ref[...])
    gi = _sigmoid_t(_mm(xc, wx_ref[...]) + bx_ref[...])
    big_l = _lru_rate(lam_ref)
    la = big_l * r
    a = jnp.exp(la)
    m2 = jnp.tanh(-la) * (a * a + 1.0)
    return xc, r, gi, big_l, a, m2


def _lru_prep(proj, lru_w, nb, seq):
    t = proj.shape[0]
    w = LRU_HEADS * HEAD
    w_conv, b_conv, w_a, b_a, w_x, b_x, lam = lru_w

    def body(x_ref, wc_ref, bc_ref, wa_ref, ba_ref, wx_ref, bx_ref, lam_ref, a_ref, inp_ref, r_ref, gi_ref, xc_ref):
        xc, r, gi, big_l, a, m2 = _lru_gates(x_ref[...].astype(F32), wc_ref, bc_ref, wa_ref, ba_ref, wx_ref, bx_ref, lam_ref)
        a_ref[...] = a
        inp_ref[...] = (jnp.sqrt(m2) * (gi * xc)).astype(BF16)
        r_ref[...] = r.astype(BF16)
        gi_ref[...] = gi.astype(BF16)
        xc_ref[...] = xc.astype(BF16)

    col = lambda b, hd: (0, hd)
    head = lambda b, hd: (hd, 0, 0)
    tok = lambda b, hd: (b, hd)
    return pl.pallas_call(
        body, name="lru_prep", grid=(nb, LRU_HEADS),
        in_specs=[pl.BlockSpec((seq, HEAD), tok),
                  pl.BlockSpec((CONV_WIDTH, HEAD), col), pl.BlockSpec((1, HEAD), col),
                  pl.BlockSpec((None, HEAD, HEAD), head), pl.BlockSpec((1, HEAD), col),
                  pl.BlockSpec((None, HEAD, HEAD), head), pl.BlockSpec((1, HEAD), col),
                  pl.BlockSpec((1, HEAD), col)],
        out_specs=[pl.BlockSpec((seq, HEAD), tok)] * 5,
        out_shape=[jax.ShapeDtypeStruct((t, w), F32)] + [jax.ShapeDtypeStruct((t, w), BF16)] * 4,
        compiler_params=_cp(("arbitrary", "arbitrary")),
    )(proj, w_conv, b_conv, w_a, b_a, w_x, b_x, lam)


def _scan(a3, b3, reverse, name, out_dtype):
    nb, seq, w = a3.shape
    tc = min(TC_SCAN, seq)
    nchunk = seq // tc
    npair = tc // 16

    def combine(av, bv):
        rows = lax.broadcasted_iota(jnp.int32, av.shape, 0)
        for s in (1, 2, 4):
            if reverse:
                keep = rows < 8 - s
                a_sh, b_sh = pltpu.roll(av, 8 - s, 0), pltpu.roll(bv, 8 - s, 0)
            else:
                keep = rows >= s
                a_sh, b_sh = pltpu.roll(av, s, 0), pltpu.roll(bv, s, 0)
            bv = jnp.where(keep, bv + av * b_sh, bv)
            av = jnp.where(keep, av * a_sh, av)
        return av, bv

    def body(a_ref, b_ref, h_ref, carry):
        @pl.when(pl.program_id(0) == 0)
        def _():
            carry[...] = jnp.zeros_like(carry)

        for b in range(nb):
            def pair(j, hprev):
                jj = npair - 1 - j if reverse else j
                base = pl.multiple_of(jj * 16, 16)
                a16 = a_ref[b, pl.ds(base, 16), :]
                b16 = b_ref[b, pl.ds(base, 16), :].astype(F32)
                outs = [None, None]
                for k in ((1, 0) if reverse else (0, 1)):
                    av, bv = a16[8 * k:8 * k + 8, :], b16[8 * k:8 * k + 8, :]
                    av, bv = combine(av, av * bv if reverse else bv)
                    h = bv + av * hprev
                    outs[k] = h
                    hprev = jnp.broadcast_to(h[0:1, :] if reverse else h[7:8, :], (8, w))
                h_ref[b, pl.ds(base, 16), :] = jnp.concatenate(outs, axis=0).astype(out_dtype)
                return hprev

            carry[b] = lax.fori_loop(0, npair, pair, carry[b])

    imap = (lambda i: (0, nchunk - 1 - i, 0)) if reverse else (lambda i: (0, i, 0))
    spec = pl.BlockSpec((nb, tc, w), imap)
    return pl.pallas_call(
        body, name=name, grid=(nchunk,), in_specs=[spec, spec], out_specs=spec,
        out_shape=jax.ShapeDtypeStruct((nb, seq, w), out_dtype),
        scratch_shapes=[pltpu.VMEM((nb, 8, w), F32)],
        compiler_params=_cp(("arbitrary",)),
    )(a3, b3)


def _sgu_mask():
    ti = lax.broadcasted_iota(jnp.int32, (HEAD, HEAD), 0) // SGU_CHUNK
    si = lax.broadcasted_iota(jnp.int32, (HEAD, HEAD), 1) // SGU_CHUNK
    return si <= ti


def _sgu_specs(tm, d_sgu):
    pw = 256
    first_u = (2 * LRU_HEADS * HEAD) // pw
    n_piece = d_sgu // pw
    specs = [pl.BlockSpec((tm, pw), functools.partial(lambda i, k: (i, k), k=first_u + j)) for j in range(2 * n_piece)]
    return specs, n_piece


def _sgu_fwd(proj, w_sp, b_sp_t, ln_g, ln_b):
    t = proj.shape[0]
    d_sgu = SGU_GROUPS * HEAD
    tm = min(TM_SGU, t)
    nblk = tm // HEAD
    specs, n_piece = _sgu_specs(tm, d_sgu)

    def body(*refs):
        u = jnp.concatenate([r[...] for r in refs[:n_piece]], axis=1).astype(F32)
        v = jnp.concatenate([r[...] for r in refs[n_piece:2 * n_piece]], axis=1).astype(F32)
        w_ref, bt_ref, g_ref, b_ref, y_ref = refs[2 * n_piece:]
        ug = _gelu(u)
        xhat, _ = _ln_stats(_gelu(v))
        vn = (xhat * g_ref[...] + b_ref[...]).astype(BF16)
        mask = _sgu_mask()
        for g in range(SGU_GROUPS):
            wm = jnp.where(mask, w_ref[g], 0.0).astype(BF16)
            cols = slice(g * HEAD, (g + 1) * HEAD)
            for n in range(nblk):
                rows = slice(n * HEAD, (n + 1) * HEAD)
                mixed = jnp.dot(wm, vn[rows, cols], preferred_element_type=F32) + bt_ref[:, g:g + 1]
                y_ref[rows, cols] = (ug[rows, cols] * mixed).astype(BF16)

    full = lambda shape: pl.BlockSpec(shape, lambda i: (0,) * len(shape))
    return pl.pallas_call(
        body, name="sgu_fwd", grid=(t // tm,),
        in_specs=specs + [full(w_sp.shape), full(b_sp_t.shape), full(ln_g.shape), full(ln_b.shape)],
        out_specs=pl.BlockSpec((tm, d_sgu), lambda i: (i, 0)),
        out_shape=jax.ShapeDtypeStruct((t, d_sgu), BF16),
        compiler_params=_cp(("arbitrary",)),
    )(*([proj] * (2 * n_piece)), w_sp, b_sp_t, ln_g, ln_b)


def _mix_fwd(hs, proj, y_sgu, x2, modv, w_o_lru_g, w_o_sgu_g, w_out_g, ln1_g, ln1_b, seq):
    t, d = x2.shape
    w = hs.shape[1]
    d_sgu = y_sgu.shape[1]
    nq, _, ns = w_o_sgu_g.shape
    tm = min(TM_MIX, seq)
    ts = min(TS_MLP, tm)
    tpb = seq // tm

    def body(hs_ref, gl_ref, ys_ref, ga_ref, gb_ref, x_ref, mod_ref, wl_hbm, ws_hbm, wo_hbm, g1_ref, b1_ref,
             yap_ref, ya_ref, yb_ref, mg_ref, mix_ref, x1_ref, wl_ref, ws_ref, wo_ref, sems):
        @pl.when(pl.program_id(0) == 0)
        def _():
            _load_weights((wl_hbm, ws_hbm, wo_hbm), (wl_ref, ws_ref, wo_ref), sems)

        for sub in range(tm // ts):
            rows = slice(sub * ts, (sub + 1) * ts)
            yap = (hs_ref[rows, :].astype(F32) * _gelu(gl_ref[rows, :].astype(F32))).astype(BF16)
            yap_ref[rows, :] = yap
            y_a = jnp.dot(yap, wl_ref[...], preferred_element_type=F32)
            ys = ys_ref[rows, :]
            y_b = jnp.concatenate([jnp.dot(ys, ws_ref[q], preferred_element_type=F32) for q in range(nq)], axis=1)
            ya_ref[rows, :] = y_a.astype(BF16)
            yb_ref[rows, :] = y_b.astype(BF16)
            merged = (_sigmoid_t(ga_ref[rows, :].astype(F32)) * y_a
                      + _sigmoid_t(gb_ref[rows, :].astype(F32)) * y_b).astype(BF16)
            mg_ref[rows, :] = merged
            mix = jnp.dot(merged, wo_ref[...], preferred_element_type=F32)
            mix_ref[rows, :] = mix
            xhat, _ = _ln_stats(ALPHA * x_ref[rows, :] + (1.0 + mod_ref[2:3, :]) * mix)
            x1_ref[rows, :] = xhat * g1_ref[...] + b1_ref[...]

    row = lambda width, col: pl.BlockSpec((tm, width), functools.partial(lambda i, k: (i, k), k=col))
    full = lambda shape: pl.BlockSpec(shape, lambda i: (0,) * len(shape))
    return pl.pallas_call(
        body, name="mix_fwd", grid=(t // tm,),
        in_specs=[row(w, 0), row(w, 1), row(d_sgu, 0), row(d, 4), row(d, 5), row(d, 0),
                  pl.BlockSpec((None, 8, d), lambda i: (i // tpb, 0, 0)),
                  _ANY, _ANY, _ANY, full(ln1_g.shape), full(ln1_b.shape)],
        out_specs=[row(w, 0), row(d, 0), row(d, 0), row(d, 0), row(d, 0), row(d, 0)],
        out_shape=[jax.ShapeDtypeStruct((t, w), BF16), jax.ShapeDtypeStruct((t, d), BF16),
                   jax.ShapeDtypeStruct((t, d), BF16), jax.ShapeDtypeStruct((t, d), BF16),
                   jax.ShapeDtypeStruct((t, d), F32), jax.ShapeDtypeStruct((t, d), F32)],
        scratch_shapes=[pltpu.VMEM(w_o_lru_g.shape, BF16), pltpu.VMEM(w_o_sgu_g.shape, BF16),
                        pltpu.VMEM(w_out_g.shape, BF16), pltpu.SemaphoreType.DMA((3,))],
        compiler_params=_cp(("arbitrary",)),
    )(hs, proj, y_sgu, proj, proj, x2, modv, w_o_lru_g, w_o_sgu_g, w_out_g, ln1_g, ln1_b)


def _mlp_fwd(x1, modv, w_up_g, w_down_g, ln2_g, ln2_b, target, nb, seq):
    t, d = x1.shape
    nq, _, ns = w_up_g.shape
    tm = min(TM_MLP, seq)
    ts = min(TS_MLP, tm)
    tpb = seq // tm

    def body(x1_ref, mod_ref, wu_hbm, wd_hbm, g2_ref, b2_ref, tg_ref,
             rl_ref, act_ref, h2_ref, dz2_ref, df_ref, st_ref, pb_ref, wu_s, wd_s, acc, sems):
        i = pl.program_id(0)

        @pl.when(i == 0)
        def _():
            _load_weights((wu_hbm, wd_hbm), (wu_s, wd_s), sems)
            st_ref[...] = jnp.zeros_like(st_ref)

        @pl.when(i % tpb == 0)
        def _():
            pb_ref[...] = jnp.zeros_like(pb_ref)

        for sub in range(tm // ts):
            rows = slice(sub * ts, (sub + 1) * ts)
            x1v = x1_ref[rows, :]
            h2 = (x1v * (1.0 + mod_ref[4:5, :]) + mod_ref[3:4, :]).astype(BF16)
            h2_ref[rows, :] = h2
            for k in range(nq):
                cols = slice(k * ns, (k + 1) * ns)
                r = jnp.maximum(jnp.dot(h2, wu_s[k], preferred_element_type=F32), 0.0)
                act = (r * r).astype(BF16)
                rl_ref[rows, cols] = r.astype(BF16)
                act_ref[rows, cols] = act
                part = jnp.dot(act, wd_s[cols, :], preferred_element_type=F32)
                if k == 0:
                    acc[sub] = part
                else:
                    acc[sub] += part
            f = acc[sub]
            xhat, rstd = _ln_stats(ALPHA * x1v + (1.0 + mod_ref[5:6, :]) * f)
            y = xhat * g2_ref[...] + b2_ref[...]
            err = y - tg_ref[rows, :]
            dy = err * (1.0 / d)
            dz2 = _ln_bwd(dy * g2_ref[...], xhat, rstd)
            dz2_ref[rows, :] = dz2
            df_ref[rows, :] = ((1.0 + mod_ref[5:6, :]) * dz2).astype(BF16)
            st_ref[0:1, :] += _colsum(dy * xhat)
            st_ref[1:2, :] += _colsum(dy)
            st_ref[2:3, :] += (0.5 / d) * jnp.sum(_colsum(err * err), axis=1, keepdims=True)
            pb_ref[0:1, :] += _colsum(dz2 * f)

    tok = lambda i: (i, 0)
    return pl.pallas_call(
        body, name="mlp_fwd", grid=(t // tm,),
        in_specs=[pl.BlockSpec((tm, d), tok), pl.BlockSpec((None, 8, d), lambda i: (i // tpb, 0, 0)), _ANY, _ANY,
                  pl.BlockSpec((1, d), lambda i: (0, 0)), pl.BlockSpec((1, d), lambda i: (0, 0)),
                  pl.BlockSpec((tm, d), tok)],
        out_specs=[pl.BlockSpec((tm, nq * ns), tok), pl.BlockSpec((tm, nq * ns), tok),
                   pl.BlockSpec((tm, d), tok), pl.BlockSpec((tm, d), tok), pl.BlockSpec((tm, d), tok),
                   pl.BlockSpec((8, d), lambda i: (0, 0)), pl.BlockSpec((None, 8, d), lambda i: (i // tpb, 0, 0))],
        out_shape=[jax.ShapeDtypeStruct((t, nq * ns), BF16), jax.ShapeDtypeStruct((t, nq * ns), BF16),
                   jax.ShapeDtypeStruct((t, d), BF16),
                   jax.ShapeDtypeStruct((t, d), F32), jax.ShapeDtypeStruct((t, d), BF16),
                   jax.ShapeDtypeStruct((8, d), F32), jax.ShapeDtypeStruct((nb, 8, d), F32)],
        scratch_shapes=[pltpu.VMEM(w_up_g.shape, BF16), pltpu.VMEM(w_down_g.shape, BF16),
                        pltpu.VMEM((tm // ts, ts, d), F32), pltpu.SemaphoreType.DMA((2,))],
        compiler_params=_cp(("arbitrary",)),
    )(x1, modv, w_up_g, w_down_g, ln2_g, ln2_b, target)


def _mlp_bwd(df, up, w_down_g, w_up_g, dz2, x2, mix, modv, ln1_g, ln1_b, nb, seq):
    t, d = x2.shape
    nq, _, ns = w_up_g.shape
    tm = min(TM_MLP, seq)
    ts = min(TS_MLP, tm)
    tpb = seq // tm

    def body(df_ref, rl_ref, wd_hbm, wu_hbm, dz2_ref, x_ref, mix_ref, mod_ref, g1_ref, b1_ref,
             dup_ref, dz1_ref, dmix_ref, st_ref, pb_ref, wd_s, wu_s, acc, sems):
        i = pl.program_id(0)

        @pl.when(i == 0)
        def _():
            _load_weights((wd_hbm, wu_hbm), (wd_s, wu_s), sems)
            st_ref[...] = jnp.zeros_like(st_ref)

        @pl.when(i % tpb == 0)
        def _():
            pb_ref[...] = jnp.zeros_like(pb_ref)

        for sub in range(tm // ts):
            rows = slice(sub * ts, (sub + 1) * ts)
            dfv = df_ref[rows, :]
            for k in range(nq):
                cols = slice(k * ns, (k + 1) * ns)
                dup = (_mm_nt(dfv, wd_s[cols, :]) * (2.0 * rl_ref[rows, cols].astype(F32))).astype(BF16)
                dup_ref[rows, cols] = dup
                part = _mm_nt(dup, wu_s[k])
                if k == 0:
                    acc[sub] = part
                else:
                    acc[sub] += part
            dh2 = acc[sub]
            mix = mix_ref[rows, :]
            xhat, rstd = _ln_stats(ALPHA * x_ref[rows, :] + (1.0 + mod_ref[2:3, :]) * mix)
            x1 = xhat * g1_ref[...] + b1_ref[...]
            dx1 = ALPHA * dz2_ref[rows, :] + dh2 * (1.0 + mod_ref[4:5, :])
            dz1 = _ln_bwd(dx1 * g1_ref[...], xhat, rstd)
            dz1_ref[rows, :] = dz1
            dmix_ref[rows, :] = ((1.0 + mod_ref[2:3, :]) * dz1).astype(BF16)
            st_ref[0:1, :] += _colsum(dx1 * xhat)
            st_ref[1:2, :] += _colsum(dx1)
            pb_ref[0:1, :] += _colsum(dh2 * x1)
            pb_ref[1:2, :] += _colsum(dh2)
            pb_ref[2:3, :] += _colsum(dz1 * mix)

    tok = lambda i: (i, 0)
    return pl.pallas_call(
        body, name="mlp_bwd", grid=(t // tm,),
        in_specs=[pl.BlockSpec((tm, d), tok), pl.BlockSpec((tm, nq * ns), tok), _ANY, _ANY,
                  pl.BlockSpec((tm, d), tok), pl.BlockSpec((tm, d), tok), pl.BlockSpec((tm, d), tok),
                  pl.BlockSpec((None, 8, d), lambda i: (i // tpb, 0, 0)),
                  pl.BlockSpec((1, d), lambda i: (0, 0)), pl.BlockSpec((1, d), lambda i: (0, 0))],
        out_specs=[pl.BlockSpec((tm, nq * ns), tok),
                   pl.BlockSpec((tm, d), tok), pl.BlockSpec((tm, d), tok),
                   pl.BlockSpec((8, d), lambda i: (0, 0)), pl.BlockSpec((None, 8, d), lambda i: (i // tpb, 0, 0))],
        out_shape=[jax.ShapeDtypeStruct((t, nq * ns), BF16),
                   jax.ShapeDtypeStruct((t, d), F32), jax.ShapeDtypeStruct((t, d), BF16),
                   jax.ShapeDtypeStruct((8, d), F32), jax.ShapeDtypeStruct((nb, 8, d), F32)],
        scratch_shapes=[pltpu.VMEM(w_down_g.shape, BF16), pltpu.VMEM(w_up_g.shape, BF16),
                        pltpu.VMEM((tm // ts, ts, d), F32), pltpu.SemaphoreType.DMA((2,))],
        compiler_params=_cp(("arbitrary",), VMEM_LIMIT_MAX),
    )(df, up, w_down_g, w_up_g, dz2, x2, mix, modv, ln1_g, ln1_b)


def _mix_bwd(dmix, proj, y_a, y_b, hs, w_out_g, w_o_lru_g, w_o_sgu_g, seq, after=()):
    t, d = dmix.shape
    w = hs.shape[1]
    nq, d_sgu, ns = w_o_sgu_g.shape
    tm = min(TM_MIX, seq)
    ts = min(TS_MLP, tm)

    def body(dmix_ref, ga_ref, gb_ref, ya_ref, yb_ref, gl_ref, hs_ref, wo_hbm, wl_hbm, ws_hbm,
             dya_ref, dyb_ref, dg_ref, dyl_ref, dys_ref, wo_ref, wl_ref, ws_ref, sems):
        @pl.when(pl.program_id(0) == 0)
        def _():
            _load_weights((wo_hbm, wl_hbm, ws_hbm), (wo_ref, wl_ref, ws_ref), sems)

        for sub in range(tm // ts):
            rows = slice(sub * ts, (sub + 1) * ts)
            dmerged = _mm_nt(dmix_ref[rows, :], wo_ref[...])
            sa, sb = _sigmoid_t(ga_ref[rows, :].astype(F32)), _sigmoid_t(gb_ref[rows, :].astype(F32))
            dy_a = (dmerged * sa).astype(BF16)
            dy_b = (dmerged * sb).astype(BF16)
            dya_ref[rows, :] = dy_a
            dyb_ref[rows, :] = dy_b
            dg_ref[rows, 4 * d:5 * d] = (dmerged * ya_ref[rows, :].astype(F32) * (sa * (1.0 - sa))).astype(BF16)
            dg_ref[rows, 5 * d:6 * d] = (dmerged * yb_ref[rows, :].astype(F32) * (sb * (1.0 - sb))).astype(BF16)
            dyap = _mm_nt(dy_a, wl_ref[...])
            gel, dgel = _gelu_and_grad(gl_ref[rows, :].astype(F32))
            dyl_ref[rows, :] = (dyap * gel).astype(BF16)
            dg_ref[rows, w:2 * w] = (dyap * hs_ref[rows, :].astype(F32) * dgel).astype(BF16)
            dys = _mm_nt(dy_b[:, 0:ns], ws_ref[0])
            for q in range(1, nq):
                dys = dys + _mm_nt(dy_b[:, q * ns:(q + 1) * ns], ws_ref[q])
            dys_ref[rows, :] = dys

    row = lambda width, col: pl.BlockSpec((tm, width), functools.partial(lambda i, k: (i, k), k=col))
    return pl.pallas_call(
        _ordered(body, 10, after), name="mix_bwd", grid=(t // tm,),
        in_specs=[row(d, 0), row(d, 4), row(d, 5), row(d, 0), row(d, 0), row(w, 1), row(w, 0),
                  _ANY, _ANY, _ANY] + [_ANY] * len(after),
        scratch_shapes=[pltpu.VMEM(w_out_g.shape, BF16), pltpu.VMEM(w_o_lru_g.shape, BF16),
                        pltpu.VMEM(w_o_sgu_g.shape, BF16), pltpu.SemaphoreType.DMA((3,))],
        out_specs=[row(d, 0), row(d, 0), row(6 * d, 0), row(w, 0), row(d_sgu, 0)],
        out_shape=[jax.ShapeDtypeStruct((t, d), BF16), jax.ShapeDtypeStruct((t, d), BF16),
                   jax.ShapeDtypeStruct((t, 6 * d), BF16), jax.ShapeDtypeStruct((t, w), BF16),
                   jax.ShapeDtypeStruct((t, d_sgu), F32)],
        compiler_params=_cp(("arbitrary",)),
    )(dmix, proj, proj, y_a, y_b, proj, hs, w_out_g, w_o_lru_g, w_o_sgu_g, *after)


def _sgu_bwd(proj, dys, w_sp, b_sp_t, ln_g, ln_b, after=()):
    t = proj.shape[0]
    d_sgu = SGU_GROUPS * HEAD
    tm = min(TM_SGU, t)
    nblk = tm // HEAD
    specs, n_piece = _sgu_specs(tm, d_sgu)

    def body(*refs):
        u = jnp.concatenate([r[...] for r in refs[:n_piece]], axis=1).astype(F32)
        v = jnp.concatenate([r[...] for r in refs[n_piece:2 * n_piece]], axis=1).astype(F32)
        dys_ref, w_ref, bt_ref, g_ref, b_ref, du_ref, dv_ref, dw_ref, st_ref, dbt_ref, dvn_s = refs[2 * n_piece:]

        @pl.when(pl.program_id(0) == 0)
        def _():
            dw_ref[...] = jnp.zeros_like(dw_ref)
            st_ref[...] = jnp.zeros_like(st_ref)
            dbt_ref[...] = jnp.zeros_like(dbt_ref)

        ug, dug_du = _gelu_and_grad(u)
        vg, dvg_dv = _gelu_and_grad(v)
        xhat, rstd = _ln_stats(vg)
        vn = (xhat * g_ref[...] + b_ref[...]).astype(BF16)
        dys_v = dys_ref[...]
        mask = _sgu_mask()
        for g in range(SGU_GROUPS):
            wm = jnp.where(mask, w_ref[g], 0.0).astype(BF16)
            cols = slice(g * HEAD, (g + 1) * HEAD)
            dw_g = jnp.zeros((HEAD, HEAD), F32)
            db_g = jnp.zeros((HEAD, 1), F32)
            for n in range(nblk):
                rows = slice(n * HEAD, (n + 1) * HEAD)
                vn_blk = vn[rows, cols]
                mixed = jnp.dot(wm, vn_blk, preferred_element_type=F32) + bt_ref[:, g:g + 1]
                dy_blk = dys_v[rows, cols]
                du_ref[rows, cols] = (dy_blk * mixed * dug_du[rows, cols]).astype(BF16)
                dmx = dy_blk * ug[rows, cols]
                dvn_s[rows, cols] = _mm_tn(wm, dmx)
                dw_g = dw_g + _mm_nt(dmx, vn_blk)
                db_g = db_g + jnp.sum(dmx, axis=1, keepdims=True)
            dw_ref[g] += jnp.where(mask, dw_g, 0.0)
            dbt_ref[:, g:g + 1] += db_g
        dvn = dvn_s[...]
        st_ref[0:1, :] += _colsum(dvn * xhat)
        st_ref[1:2, :] += _colsum(dvn)
        dv_ref[...] = (_ln_bwd(dvn * g_ref[...], xhat, rstd) * dvg_dv).astype(BF16)

    full = lambda shape: pl.BlockSpec(shape, lambda i: (0,) * len(shape))
    tok = pl.BlockSpec((tm, d_sgu), lambda i: (i, 0))
    return pl.pallas_call(
        _ordered(body, 2 * n_piece + 5, after), name="sgu_bwd", grid=(t // tm,),
        in_specs=specs + [tok, full(w_sp.shape), full(b_sp_t.shape), full(ln_g.shape), full(ln_b.shape)]
        + [_ANY] * len(after),
        out_specs=[tok, tok, full(w_sp.shape), full((8, d_sgu)), full((HEAD, HEAD))],
        out_shape=[jax.ShapeDtypeStruct((t, d_sgu), BF16), jax.ShapeDtypeStruct((t, d_sgu), BF16),
                   jax.ShapeDtypeStruct(w_sp.shape, F32), jax.ShapeDtypeStruct((8, d_sgu), F32),
                   jax.ShapeDtypeStruct((HEAD, HEAD), F32)],
        scratch_shapes=[pltpu.VMEM((tm, d_sgu), F32)],
        compiler_params=_cp(("arbitrary",)),
    )(*([proj] * (2 * n_piece)), dys, w_sp, b_sp_t, ln_g, ln_b, *after)


def _lru_bwd(proj, hs, e, dyl, saved, lru_w, nb, seq, dproj, after=()):
    t = proj.shape[0]
    w = LRU_HEADS * HEAD
    w_conv, b_conv, w_a, b_a, w_x, b_x, lam = lru_w

    def body(x_ref, hs_ref, e_ref, dy_ref, a_ref, r_ref, gi_ref, xc_ref, wc_ref, wa_ref, wx_ref, lam_ref,
             dxl_ref, dwa_ref, dwx_ref, st_ref):
        @pl.when(pl.program_id(1) == 0)
        def _():
            dwa_ref[...] = jnp.zeros_like(dwa_ref)
            dwx_ref[...] = jnp.zeros_like(dwx_ref)
            st_ref[...] = jnp.zeros_like(st_ref)

        xl = x_ref[...].astype(F32)
        a, r, gi, xc = a_ref[...], r_ref[...].astype(F32), gi_ref[...].astype(F32), xc_ref[...].astype(F32)
        big_l = _lru_rate(lam_ref)
        m2 = (1.0 - a) * (1.0 + a)
        inv_mult = lax.rsqrt(m2)
        mult = m2 * inv_mult
        dh = dy_ref[...].astype(F32) + _shift_up(e_ref[...].astype(F32), 1)
        da = dh * _shift_down(hs_ref[...].astype(F32), 1)
        dmult = dh * (gi * xc)
        d_i = dh * (mult * xc)
        dxc = dh * (mult * gi)
        dla = a * (da - dmult * (a * inv_mult))
        dr = dla * big_l
        d_big_l = _colsum(dla * r)
        dra = dr * (r * (1.0 - r))
        dia = d_i * (gi * (1.0 - gi))
        dwa_ref[...] += _mm_tn(xc, dra)
        dwx_ref[...] += _mm_tn(xc, dia)
        dxc = dxc + _mm_nt(dra, wa_ref[...]) + _mm_nt(dia, wx_ref[...])
        dxl = wc_ref[CONV_WIDTH - 1:CONV_WIDTH, :] * dxc
        st_ref[4 + CONV_WIDTH - 1:4 + CONV_WIDTH, :] += _colsum(dxc * xl)
        for k in range(CONV_WIDTH - 1):
            ahead = _shift_up(dxc, CONV_WIDTH - 1 - k)
            dxl = dxl + wc_ref[k:k + 1, :] * ahead
            st_ref[4 + k:5 + k, :] += _colsum(ahead * xl)
        dxl_ref[...] = dxl.astype(BF16)
        st_ref[0:1, :] += _colsum(dra)
        st_ref[1:2, :] += _colsum(dia)
        st_ref[2:3, :] += d_big_l * (LRU_C * _sigmoid(-lam_ref[...]))
        st_ref[3:4, :] += _colsum(dxc)

    col = lambda hd, b: (0, hd)
    head = lambda hd, b: (hd, 0, 0)
    tok = lambda hd, b: (b, hd)
    seq_blk = pl.BlockSpec((seq, HEAD), tok)
    return pl.pallas_call(
        _ordered(body, 12, (dproj,) + tuple(after)), name="lru_bwd", grid=(LRU_HEADS, nb),
        in_specs=[seq_blk] * 8 + [pl.BlockSpec((CONV_WIDTH, HEAD), col), pl.BlockSpec((None, HEAD, HEAD), head),
                                  pl.BlockSpec((None, HEAD, HEAD), head), pl.BlockSpec((1, HEAD), col)]
        + [_ANY] * (1 + len(after)),
        out_specs=[seq_blk, pl.BlockSpec((None, HEAD, HEAD), head), pl.BlockSpec((None, HEAD, HEAD), head),
                   pl.BlockSpec((8, HEAD), col)],
        out_shape=[jax.ShapeDtypeStruct(dproj.shape, BF16), jax.ShapeDtypeStruct((LRU_HEADS, HEAD, HEAD), F32),
                   jax.ShapeDtypeStruct((LRU_HEADS, HEAD, HEAD), F32), jax.ShapeDtypeStruct((8, w), F32)],
        input_output_aliases={12: 0},
        compiler_params=_cp(("arbitrary", "arbitrary")),
    )(proj, hs, e, dyl, *saved, w_conv, w_a, w_x, lam, dproj, *after)


def _weight_grad(a, g, col_shards, name, after=()):
    t, k = a.shape
    n = g.shape[1]
    tk = k if k <= 1536 else 1024
    ns = n // N_CHIPS if col_shards else n
    narrow = col_shards and ns < 512
    tn = n if narrow else min(ns, 768 if ns % 768 == 0 else 1024)
    while ns % tn and not narrow:
        tn //= 2
    per = max(ns // tn, 1)
    tt = min(TT_DW if (k // tk) * (n // tn) > 1 else TT_DW // 4, t)

    def body(a_ref, g_ref, o_ref):
        @pl.when(pl.program_id(2) == 0)
        def _():
            o_ref[...] = jnp.zeros_like(o_ref)

        res = _mm_tn(a_ref[...], g_ref[...])
        if narrow:
            for q in range(N_CHIPS):
                o_ref[q] += res[:, q * ns:(q + 1) * ns]
        else:
            o_ref[...] += res

    if narrow:
        out_spec = pl.BlockSpec((N_CHIPS, tk, ns), lambda i, j, s: (0, i, 0))
        out_shape = jax.ShapeDtypeStruct((N_CHIPS, k, ns), F32)
    elif col_shards:
        out_spec = pl.BlockSpec((None, tk, tn), lambda i, j, s: (j // per, i, j % per))
        out_shape = jax.ShapeDtypeStruct((N_CHIPS, k, ns), F32)
    else:
        out_spec = pl.BlockSpec((tk, tn), lambda i, j, s: (i, j))
        out_shape = jax.ShapeDtypeStruct((k, n), F32)
    return pl.pallas_call(
        _ordered(body, 2, after), name=name, grid=(k // tk, n // tn, t // tt),
        in_specs=[pl.BlockSpec((tt, tk), lambda i, j, s: (s, i)), pl.BlockSpec((tt, tn), lambda i, j, s: (s, j))]
        + [_ANY] * len(after),
        out_specs=out_spec, out_shape=out_shape,
        compiler_params=_cp(("arbitrary", "arbitrary", "arbitrary")),
    )(a, g, *after)


def _input_grad(dproj, ws, slots, dz1, x2, modv, nb, seq, after=()):
    t, d = x2.shape
    nq = len(ws)
    ns = ws[0].shape[1]
    tm = min(TM_DH, seq)
    ts = min(TS_MLP, tm)
    tpb = seq // tm

    def body(slot_ref, dp_ref, *refs):
        w_hbm = refs[:nq]
        dz1_ref, x_ref, mod_ref, gx_ref, db_ref, pb_ref, w_s, acc, sems = refs[nq:]
        i = pl.program_id(0)

        @pl.when(i == 0)
        def _():
            _load_weights(w_hbm, [w_s.at[slot_ref[k]] for k in range(nq)], sems)
            db_ref[...] = jnp.zeros_like(db_ref)

        @pl.when(i % tpb == 0)
        def _():
            pb_ref[...] = jnp.zeros_like(pb_ref)

        for sub in range(tm // ts):
            rows = slice(sub * ts, (sub + 1) * ts)
            for q in range(nq):
                dp = dp_ref[rows, q * ns:(q + 1) * ns]
                part = _mm_nt(dp, w_s[q])
                if q == 0:
                    acc[sub] = part
                else:
                    acc[sub] += part
                db_ref[q, 0:1, :] += _colsum(dp.astype(F32))
            dh = acc[sub]
            gx_ref[rows, :] = ALPHA * dz1_ref[rows, :] + dh * (1.0 + mod_ref[1:2, :])
            pb_ref[0:1, :] += _colsum(dh * x_ref[rows, :])
            pb_ref[1:2, :] += _colsum(dh)

    tok = lambda i, s: (i, 0)
    in_specs = [pl.BlockSpec((tm, nq * ns), tok)] + [_ANY] * nq
    in_specs += [pl.BlockSpec((tm, d), tok), pl.BlockSpec((tm, d), tok),
                 pl.BlockSpec((None, 8, d), lambda i, s: (i // tpb, 0, 0))] + [_ANY] * len(after)
    return pl.pallas_call(
        _ordered(body, 5 + nq, after), name="input_grad",
        grid_spec=pltpu.PrefetchScalarGridSpec(
            num_scalar_prefetch=1, grid=(t // tm,), in_specs=in_specs,
            out_specs=[pl.BlockSpec((tm, d), tok), pl.BlockSpec((nq, 8, ns), lambda i, s: (0, 0, 0)),
                       pl.BlockSpec((None, 8, d), lambda i, s: (i // tpb, 0, 0))],
            scratch_shapes=[pltpu.VMEM((nq, d, ns), BF16), pltpu.VMEM((tm // ts, ts, d), F32),
                            pltpu.SemaphoreType.DMA((nq,))]),
        out_shape=[jax.ShapeDtypeStruct((t, d), F32), jax.ShapeDtypeStruct((nq, 8, ns), F32),
                   jax.ShapeDtypeStruct((nb, 8, d), F32)],
        compiler_params=_cp(("arbitrary",)),
    )(slots, dproj, *ws, dz1, x2, modv, *after)


def _rows128(v):
    flat = v.reshape(-1, HEAD)
    pad = (-flat.shape[0]) % 8
    return jnp.pad(flat, ((0, pad), (0, 0))) if pad else flat


def kernel(x, c, w_ada, b_ada, w_in, b_in, w_conv, b_conv, w_rg_a, b_rg_a, w_rg_x, b_rg_x, lru_lambda, w_sp, b_sp, ln_v_g, ln_v_b, w_o_lru, w_o_sgu, w_out, ln1_g, ln1_b, w_up, w_down, ln2_g, ln2_b, loss_target, m_w_ada, m_b_ada, m_w_in, m_b_in, m_w_conv, m_b_conv, m_w_rg_a, m_b_rg_a, m_w_rg_x, m_b_rg_x, m_lru_lambda, m_w_sp, m_b_sp, m_ln_v_g, m_ln_v_b, m_w_o_lru, m_w_o_sgu, m_w_out, m_ln1_g, m_ln1_b, m_w_up, m_w_down, m_ln2_g, m_ln2_b, v_w_ada, v_b_ada, v_w_in, v_b_in, v_w_conv, v_b_conv, v_w_rg_a, v_b_rg_a, v_w_rg_x, v_b_rg_x, v_lru_lambda, v_w_sp, v_b_sp, v_ln_v_g, v_ln_v_b, v_w_o_lru, v_w_o_sgu, v_w_out, v_ln1_g, v_ln1_b, v_w_up, v_w_down, v_ln2_g, v_ln2_b):
    given = dict(locals())
    nb, seq, d = x.shape
    t = nb * seq
    w_lru = LRU_HEADS * HEAD
    d_sgu = SGU_GROUPS * HEAD
    xi, yi, ci = lax.axis_index("x"), lax.axis_index("y"), lax.axis_index("c")
    chip = 2 * xi + yi
    dev = 2 * chip + ci
    cidx = jnp.reshape(ci, (1,)).astype(jnp.int32)

    x2 = x.reshape(t, d)
    target = loss_target.reshape(t, d)

    big = ["w_in", "w_o_lru", "w_o_sgu", "w_out", "w_up", "w_down"]
    shards_a = [w_in[0].astype(BF16)]
    shards_b = [given[n][0].astype(BF16) for n in big[1:]]
    pidx = jnp.reshape(chip, (1,)).astype(jnp.int32)

    c_rows = _rows128(c)
    wconv_rows = _rows128(w_conv[0])
    def gather_direct(v, name):
        (lands,) = _exchange([v], [jax.ShapeDtypeStruct((N_DEV,) + v.shape, v.dtype)], _all_devices_copies, N_DEV - 1, name)
        mine = lax.broadcasted_iota(jnp.int32, (N_DEV, 1, 1), 0) == dev
        return jnp.where(mine, v[None], lands).reshape(N_DEV * v.shape[0], v.shape[1])

    slab0 = gather_direct(jnp.concatenate([c_rows, wconv_rows], axis=0), "gather_c_wconv")
    slab0 = slab0.reshape(N_DEV, -1, HEAD)
    c_all = slab0[:, :c_rows.shape[0]].reshape(N_DEV * nb, d)
    n_wc = CONV_WIDTH * (w_lru // N_CHIPS) // HEAD
    wc = slab0[0::2, c_rows.shape[0]:c_rows.shape[0] + n_wc].reshape(N_CHIPS, CONV_WIDTH, w_lru // N_CHIPS)
    w_conv_full = jnp.transpose(wc, (1, 0, 2)).reshape(CONV_WIDTH, w_lru)

    n_ada = w_ada.shape[2]
    b_ada_cols = lax.dynamic_slice(b_ada, (0, chip * n_ada), (1, n_ada))
    mod_cols = _ada_fwd(c_all, w_ada[0], b_ada_cols)
    half = (N_DEV * nb) // 2
    mod_half = lax.dynamic_slice(mod_cols, (ci * half, 0), (half, n_ada))
    mod_g = gather_direct(mod_half, "gather_mod").reshape(N_CHIPS, 2, half, n_ada)
    mod_all = jnp.transpose(mod_g, (1, 2, 0, 3)).reshape(N_DEV * nb, N_CHIPS * n_ada)
    mod_loc = lax.dynamic_slice(mod_all, (dev * nb, 0), (nb, N_CHIPS * n_ada)).reshape(nb, 6, d)
    modv = jnp.pad(mod_loc, ((0, 0), (0, 2), (0, 0)))

    lru_w = (w_conv_full, b_conv, w_rg_a[0], b_rg_a, w_rg_x[0], b_rg_x, lru_lambda)
    b_sp_t = jnp.transpose(b_sp[0])

    land = lambda s: jax.ShapeDtypeStruct((N_CHIPS,) + s.shape, s.dtype)
    sds = lambda s: jax.ShapeDtypeStruct(s.shape, s.dtype)
    started_a = _split_start(shards_a, [sds(shards_a[0])] * 2, _peer_gather_copies((0, 1)), 2, "gather_w_in_near_start",
                             after=(modv,))
    shards_b, shards_c = shards_b[:3], shards_b[3:]

    ids = lambda *v: jnp.stack(v).astype(jnp.int32)
    modv_t = modv + started_a[-1][0:1, 0:1]
    proj, h = _proj_fwd(x2, modv_t, [started_a[2]], ids(chip), b_in, seq, "proj_fwd_own")
    own_a, lands_a = _split_wait(started_a, 1, _peer_gather_copies((0, 1)), "gather_w_in_near_wait",
                                 after=(proj, *shards_b, *shards_c))
    started_f = _split_start(own_a, [sds(own_a[0])], _far_gather_copies, 1, "gather_w_in_far_start", after=(lands_a[0],))
    started_b = _split_start(shards_b, [land(s) for s in shards_b], _gather_copies, 3 * len(shards_b),
                             "gather_w_mix_start", after=(started_f[-1],))
    started_c = _split_start(shards_c, [land(s) for s in shards_c], _gather_copies, 3 * len(shards_c),
                             "gather_w_mlp_start", after=(started_b[-1],))
    modv_t = modv + started_c[-1][0:1, 0:1]
    (proj,) = _proj_fwd(h, modv_t, lands_a, ids(chip ^ 1, chip ^ 2), b_in, seq, "proj_fwd_near", proj_in=proj)
    own_a, land_f = _split_wait(started_f, 1, _far_gather_copies, "gather_w_in_far_wait", after=(proj,))
    (proj,) = _proj_fwd(h, modv, land_f, ids(chip ^ 3), b_in, seq, "proj_fwd_far", proj_in=proj)
    w_in_shards, w_in_chips = own_a + lands_a + land_f, ids(chip, chip ^ 1, chip ^ 2, chip ^ 3)
    a, inp, r16, gi16, xc16 = _lru_prep(proj, lru_w, nb, seq)
    a3 = a.reshape(nb, seq, w_lru)
    hs = _scan(a3, inp.reshape(nb, seq, w_lru), False, "lru_scan", BF16).reshape(t, w_lru)
    y_sgu = _sgu_fwd(proj, w_sp[0], b_sp_t, ln_v_g, ln_v_b)
    shards_b, lands_b = _split_wait(started_b, len(shards_b), _gather_copies, "gather_w_mix_wait", after=(hs, y_sgu))
    w_o_lru_g, w_o_sgu_g, w_out_g = _fill_own_slot(lands_b, shards_b, pidx, ["own_" + n for n in big[1:4]])
    w_o_lru_g = w_o_lru_g.reshape(w_lru, d)
    w_out_g = w_out_g.reshape(d, d)
    yap, y_a, y_b, merged, mix, x1 = _mix_fwd(hs, proj, y_sgu, x2, modv, w_o_lru_g, w_o_sgu_g, w_out_g, ln1_g, ln1_b, seq)
    shards_c, lands_c = _split_wait(started_c, len(shards_c), _gather_copies, "gather_w_mlp_wait", after=(x1,))
    w_up_g, w_down_g = _fill_own_slot(lands_c, shards_c, pidx, ["own_" + n for n in big[4:]])
    w_down_g = w_down_g.reshape(-1, d)
    up, act, h2, dz2, df, st2, pb2 = _mlp_fwd(x1, modv, w_up_g, w_down_g, ln2_g, ln2_b, target, nb, seq)

    part = {}

    def to_sibling_start(group, tag, after=()):
        g4 = []
        for n in group:
            shard = given[n].shape[1:]
            g4.append(part[n].reshape(N_CHIPS, 2, shard[0] // 2, shard[1]))
        shapes = [jax.ShapeDtypeStruct((N_CHIPS,) + g.shape[2:], F32) for g in g4]
        return _split_start(g4, shapes, _to_sibling_copies, len(g4), "grads_to_sibling_start_" + tag, after)

    def to_chips_start(group, started, tag, after=()):
        g4, recv = _split_wait(started, len(group), _to_sibling_copies, "grads_to_sibling_wait_" + tag, after)
        own4 = [_add_own_half(g4[k], recv[k], cidx, "grad_pair_sum_" + n) for k, n in enumerate(group)]
        shapes = [jax.ShapeDtypeStruct((3,) + o.shape[1:], BF16) for o in own4]
        return _split_start(own4, shapes, _chip_exchange_copies, 3 * len(own4), "grads_chip_exchange_start_" + tag)

    def chips_finish(group, started, tag, after=()):
        own4, slots = _split_wait(started, len(group), _chip_exchange_copies, "grads_chip_exchange_wait_" + tag, after)
        return [_sum_own_and_peers(own4[k], slots[k], pidx, "grad_chip_sum_" + n) for k, n in enumerate(group)]

    dup, dz1, dmix, st1, pb1 = _mlp_bwd(df, up, w_down_g, w_up_g, dz2, x2, mix, modv, ln1_g, ln1_b, nb, seq)
    group1 = ["w_up", "w_down"]
    part["w_up"] = _weight_grad(h2, dup, True, "grad_w_up")
    part["w_down"] = _weight_grad(act, df, False, "grad_w_down")
    sib1 = to_sibling_start(group1, "mlp")
    dy_a, dy_b, dproj, dyl, dys = _mix_bwd(dmix, proj, y_a, y_b, hs, w_out_g, w_o_lru_g, w_o_sgu_g, seq,
                                                after=(sib1[-1],))
    group2 = ["w_o_lru", "w_o_sgu", "w_out"]
    part["w_o_lru"] = _weight_grad(yap, dy_a, False, "grad_w_o_lru")
    part["w_o_sgu"] = _weight_grad(y_sgu, dy_b, True, "grad_w_o_sgu")
    part["w_out"] = _weight_grad(merged, dmix, False, "grad_w_out")
    chips1 = to_chips_start(group1, sib1, "mlp", after=(dys, part["w_o_lru"], part["w_o_sgu"], part["w_out"]))
    sib2 = to_sibling_start(group2, "mix", after=(chips1[-1],))
    du, dv, g_w_sp, st_sgu, g_b_sp_t = _sgu_bwd(proj, dys, w_sp[0], b_sp_t, ln_v_g, ln_v_b, after=(sib2[-1],))
    dyl3 = dyl.reshape(nb, seq, w_lru)
    e = _scan(a3, dyl3, True, "lru_scan_bwd", BF16).reshape(t, w_lru)
    chips2 = to_chips_start(group2, sib2, "mix", after=(e, du))
    dproj = lax.dynamic_update_slice(dproj, du, (0, 2 * w_lru))
    dproj = lax.dynamic_update_slice(dproj, dv, (0, 2 * w_lru + d_sgu))
    dproj, g_w_rg_a, g_w_rg_x, st_lru = _lru_bwd(proj, hs, e, dyl, (a, r16, gi16, xc16), lru_w, nb, seq, dproj,
                                                 after=(chips2[-1],))

    didx = jnp.reshape(dev, (1,)).astype(jnp.int32)
    early = [
        ("w_conv", st_lru[4:8]), ("b_conv", st_lru[3]), ("w_rg_a", g_w_rg_a), ("b_rg_a", st_lru[0]),
        ("w_rg_x", g_w_rg_x), ("b_rg_x", st_lru[1]), ("lru_lambda", st_lru[2]), ("w_sp", g_w_sp),
        ("b_sp", jnp.transpose(g_b_sp_t[:, :SGU_GROUPS])), ("ln_v_g", st_sgu[0]), ("ln_v_b", st_sgu[1]),
        ("ln1_g", st1[0]), ("ln1_b", st1[1]), ("ln2_g", st2[0]), ("ln2_b", st2[1]),
        ("loss", st2[2:3, 0:HEAD]),
    ]
    pieces_e = [_rows128(v) for _, v in early]
    slab_e = jnp.concatenate(pieces_e, axis=0)
    slab_e = jnp.pad(slab_e, ((0, (-slab_e.shape[0]) % TR_EW), (0, 0)))
    small_st = _split_start([slab_e], [jax.ShapeDtypeStruct((N_DEV,) + slab_e.shape, F32)], _all_devices_copies, N_DEV - 1,
                            "small_grads_start")

    group3 = ["w_in"]
    part["w_in"] = _weight_grad(h, dproj, True, "grad_w_in", after=(small_st[-1],))
    sib3 = to_sibling_start(group3, "in")
    halves12 = (chips_finish(group1, chips1, "mlp", after=(sib3[-1],))
                + chips_finish(group2, chips2, "mix", after=(sib3[-1],)))
    swap12 = _split_start(halves12, [jax.ShapeDtypeStruct(hv.shape, F32) for hv in halves12], _swap_copies, len(halves12),
                          "grads_swap_start")
    chips3 = to_chips_start(group3, sib3, "in", after=(swap12[-1],))
    grad_x2, g_b_in4, pb0 = _input_grad(dproj, w_in_shards, w_in_chips, dz1, x2, modv, nb, seq, after=(chips3[-1],))
    grads = {}
    two_d = lambda v: v.reshape(-1, v.shape[-1])
    done = {}

    def adamw_big(n, mine_n, theirs_n):
        done[n] = _adamw_halves(two_d(given[n]), mine_n, theirs_n, two_d(given["m_" + n]), two_d(given["v_" + n]), cidx,
                                "adamw_" + n)

    dmod_loc = jnp.stack([pb0[:, 1], pb0[:, 0], pb1[:, 2], pb1[:, 1], pb1[:, 0], pb2[:, 0]], axis=1)
    rows_dmod = dmod_loc.size // HEAD
    slab_l = jnp.concatenate([_rows128(dmod_loc), _rows128(g_b_in4[:, 0])], axis=0)
    late_st = _split_start([slab_l], [jax.ShapeDtypeStruct((N_DEV,) + slab_l.shape, F32)], _all_devices_copies, N_DEV - 1,
                           "late_grads_start")
    mine12, theirs12 = _split_wait(swap12, len(halves12), _swap_copies, "grads_swap_wait", after=(late_st[-1],))
    for n, mine_n, theirs_n in zip(group1 + group2, mine12, theirs12):
        adamw_big(n, mine_n, theirs_n)
    (slab_l,), (lands_l,) = _split_wait(late_st, 1, _all_devices_copies, "late_grads_wait",
                                        after=tuple(done[n][1] for n in group1 + group2))
    every = jnp.where(lax.broadcasted_iota(jnp.int32, (N_DEV, 1, 1), 0) == dev, slab_l[None], lands_l)
    dmod_all = every[:, :rows_dmod].reshape(N_DEV * nb, 6 * d)
    grads["b_in"] = _sum_slots(every[:, rows_dmod:], "grad_b_in_sum").reshape(1, -1)

    (slab_e,), (lands_e,) = _split_wait(small_st, 1, _all_devices_copies, "small_grads_wait", after=(dmod_all,))
    summed = _sum_devices(lands_e, slab_e, didx, "small_grad_sum")
    off = 0
    for (n, v), piece in zip(early, pieces_e):
        grads[n] = summed[off:off + v.size // HEAD].reshape(v.shape)
        off += piece.shape[0]
    loss = grads.pop("loss")[0, 0]

    (mine3,) = chips_finish(group3, chips3, "in", after=(summed,))
    (theirs3,) = _exchange([mine3], [jax.ShapeDtypeStruct(mine3.shape, F32)], _swap_copies, 1, "grads_swap_w_in")
    adamw_big("w_in", mine3, theirs3)

    dmod_cols = lax.dynamic_slice(dmod_all, (0, chip * n_ada), (N_DEV * nb, n_ada))
    grads["w_ada"], grads["b_ada"] = _ada_bwd(c_all, dmod_all, dmod_cols)
    n_wcs = w_lru // N_CHIPS
    grads["w_conv"] = lax.dynamic_slice(grads["w_conv"], (0, chip * n_wcs), (CONV_WIDTH, n_wcs))

    names = ['w_ada', 'b_ada', 'w_in', 'b_in', 'w_conv', 'b_conv', 'w_rg_a', 'b_rg_a', 'w_rg_x', 'b_rg_x', 'lru_lambda',
             'w_sp', 'b_sp', 'ln_v_g', 'ln_v_b', 'w_o_lru', 'w_o_sgu', 'w_out', 'ln1_g', 'ln1_b', 'w_up', 'w_down',
             'ln2_g', 'ln2_b']
    small_names = [n for n in names if n not in big and n != "w_ada"]
    small_out = _adamw_many([(two_d(given[n]), two_d(grads[n].reshape(given[n].shape)), two_d(given["m_" + n]),
                              two_d(given["v_" + n])) for n in small_names], "adamw_small")
    for n, res in zip(small_names, small_out):
        done[n] = (grads[n],) + tuple(res)
    done["w_ada"] = (grads["w_ada"],) + tuple(_adamw(two_d(given["w_ada"]), two_d(grads["w_ada"]), two_d(given["m_w_ada"]),
                                                     two_d(given["v_w_ada"]), "adamw_w_ada"))
    outs = [[done[n][k].reshape(given[n].shape) for n in names] for k in range(4)]
    return (loss, grad_x2.reshape(nb, seq, d), *outs[0], *outs[1], *outs[2], *outs[3])
```
